```python
import jax, jax.numpy as jnp
from jax import lax
import numpy as np

D_MODEL = 1024
BATCH = 8
SEQ = 2048
DEPTH = 2

HEAD_DIM = 64
N_HEADS_A = D_MODEL // HEAD_DIM
WIDTH_A = N_HEADS_A * HEAD_DIM
N_Q_HEADS_B = D_MODEL // HEAD_DIM
N_KV_HEADS_B = 4
GROUP_B = N_Q_HEADS_B // N_KV_HEADS_B
WIDTH_B = N_Q_HEADS_B * HEAD_DIM
KV_WIDTH_B = N_KV_HEADS_B * HEAD_DIM
WINDOW = 128
Q_BLOCK = 128
ROT_DIM = HEAD_DIM // 4
ROPE_THETA = 500000.0
EPS = 1e-6
N_A_LAYERS = DEPTH // 2
N_B_LAYERS = DEPTH - N_A_LAYERS

kernel_name = "yoco_fox_swa_sink_hybrid"


def rmsnorm(x, g):
    xf = x.astype(jnp.float32)
    y = xf * lax.rsqrt(jnp.mean(xf * xf, axis=-1, keepdims=True) + EPS)
    return (y * g.astype(jnp.float32)).astype(x.dtype)


def partial_rope(x, positions):
    x_rot, x_pass = x[..., :ROT_DIM], x[..., ROT_DIM:]
    half = ROT_DIM // 2
    inv_freq = jnp.power(jnp.float32(ROPE_THETA), -jnp.arange(0, ROT_DIM, 2, dtype=jnp.float32) / ROT_DIM)
    ang = positions.astype(jnp.float32)[:, None] * inv_freq[None, :]
    cos = jnp.cos(ang)[None, :, None, :]
    sin = jnp.sin(ang)[None, :, None, :]
    xr = x_rot.astype(jnp.float32)
    x1, x2 = xr[..., :half], xr[..., half:]
    rot = jnp.concatenate([x1 * cos - x2 * sin, x1 * sin + x2 * cos], axis=-1)
    return jnp.concatenate([rot.astype(x.dtype), x_pass], axis=-1)


def fox_attention(q, k, v, log_f):
    b, s, h, d = q.shape
    nb = s // Q_BLOCK
    scale = HEAD_DIM ** -0.5
    c = jnp.cumsum(log_f, axis=1)
    c_k = jnp.transpose(c, (0, 2, 1))
    q_blocks = jnp.moveaxis(q.reshape(b, nb, Q_BLOCK, h, d), 1, 0)
    c_blocks = jnp.moveaxis(c_k.reshape(b, h, nb, Q_BLOCK), 2, 0)
    k_pos = jnp.arange(s)

    def block(args):
        idx, qi, ci = args
        logits = jnp.einsum('bqhd,bkhd->bhqk', qi, k, preferred_element_type=jnp.float32) * scale
        logits = logits + ci[..., :, None] - c_k[..., None, :]
        q_pos = idx * Q_BLOCK + jnp.arange(Q_BLOCK)
        causal = k_pos[None, :] <= q_pos[:, None]
        logits = jnp.where(causal, logits, -jnp.inf)
        p = jax.nn.softmax(logits, axis=-1)
        return jnp.einsum('bhqk,bkhd->bqhd', p.astype(v.dtype), v)

    out = lax.map(block, (jnp.arange(nb), q_blocks, c_blocks))
    return jnp.moveaxis(out, 0, 1).reshape(b, s, h, d)


def swa_sink_attention(q, k, v, sinks):
    b, s, hq, d = q.shape
    hkv = k.shape[2]
    g = hq // hkv
    nb = s // WINDOW
    scale = HEAD_DIM ** -0.5
    qb = q.reshape(b, nb, WINDOW, hkv, g, d)
    pad = ((0, 0), (WINDOW, 0), (0, 0), (0, 0))
    kb = jnp.pad(k, pad).reshape(b, nb + 1, WINDOW, hkv, d)
    vb = jnp.pad(v, pad).reshape(b, nb + 1, WINDOW, hkv, d)
    k_band = jnp.concatenate([kb[:, :-1], kb[:, 1:]], axis=2)
    v_band = jnp.concatenate([vb[:, :-1], vb[:, 1:]], axis=2)
    logits = jnp.einsum('bnqhgd,bnkhd->bnhgqk', qb, k_band, preferred_element_type=jnp.float32) * scale
    diff = (jnp.arange(WINDOW)[:, None] + WINDOW) - jnp.arange(2 * WINDOW)[None, :]
    in_window = (diff >= 0) & (diff < WINDOW)
    k_abs = jnp.arange(nb)[:, None] * WINDOW + jnp.arange(2 * WINDOW)[None, :] - WINDOW
    valid = in_window[None] & (k_abs >= 0)[:, None, :]
    logits = jnp.where(valid[None, :, None, None], logits, -jnp.inf)
    sink = jnp.broadcast_to(sinks.astype(jnp.float32).reshape(1, 1, hkv, g, 1, 1), logits.shape[:-1] + (1,))
    probs = jax.nn.softmax(jnp.concatenate([logits, sink], axis=-1), axis=-1)[..., :-1]
    out = jnp.einsum('bnhgqk,bnkhd->bnqhgd', probs.astype(v.dtype), v_band)
    return out.reshape(b, s, hq, d)


def _fwd_setup_inputs(seed: int = 0) -> dict:
    key = jax.random.key(seed)
    ks = jax.random.split(key, 20)
    f32 = jnp.float32
    in_a = 3 * WIDTH_A + N_HEADS_A + WIDTH_A
    in_b = WIDTH_B + WIDTH_B
    return {
        "x": jax.random.normal(ks[0], (BATCH, SEQ, D_MODEL), f32),
        "positions": jnp.arange(SEQ, dtype=jnp.int32),
        "norm_a_g": 1.0 + 0.02 * jax.random.normal(ks[1], (N_A_LAYERS, D_MODEL), f32),
        "w_in_a": jax.random.normal(ks[2], (N_A_LAYERS, D_MODEL, in_a), f32) * D_MODEL ** -0.5,
        "b_forget": 3.0 + 0.1 * jax.random.normal(ks[3], (N_A_LAYERS, N_HEADS_A), f32),
        "qnorm_a_g": 1.0 + 0.02 * jax.random.normal(ks[4], (N_A_LAYERS, HEAD_DIM), f32),
        "knorm_a_g": 1.0 + 0.02 * jax.random.normal(ks[5], (N_A_LAYERS, HEAD_DIM), f32),
        "w_out_a": jax.random.normal(ks[6], (N_A_LAYERS, WIDTH_A, D_MODEL), f32) * WIDTH_A ** -0.5,
        "kv_norm_g": 1.0 + 0.02 * jax.random.normal(ks[7], (D_MODEL,), f32),
        "w_kv": jax.random.normal(ks[8], (D_MODEL, 2 * KV_WIDTH_B), f32) * D_MODEL ** -0.5,
        "knorm_b_g": 1.0 + 0.02 * jax.random.normal(ks[9], (HEAD_DIM,), f32),
        "norm_b_g": 1.0 + 0.02 * jax.random.normal(ks[10], (N_B_LAYERS, D_MODEL), f32),
        "w_in_b": jax.random.normal(ks[11], (N_B_LAYERS, D_MODEL, in_b), f32) * D_MODEL ** -0.5,
        "qnorm_b_g": 1.0 + 0.02 * jax.random.normal(ks[12], (N_B_LAYERS, HEAD_DIM), f32),
        "sinks": 0.5 * jax.random.normal(ks[13], (N_B_LAYERS, N_Q_HEADS_B), f32),
        "w_out_b": jax.random.normal(ks[14], (N_B_LAYERS, WIDTH_B, D_MODEL), f32) * WIDTH_B ** -0.5,
    }


def _fwd_reference(x, positions, norm_a_g, w_in_a, b_forget, qnorm_a_g, knorm_a_g, w_out_a,
              kv_norm_g, w_kv, knorm_b_g, norm_b_g, w_in_b, qnorm_b_g, sinks, w_out_b):
    b, s, _ = x.shape
    h = x
    k_shared = None
    v_shared = None
    for layer in range(DEPTH):
        if layer < N_A_LAYERS:
            i = layer
            u = rmsnorm(h, norm_a_g[i])
            proj = u @ w_in_a[i]
            q, k, v, f_logit, gate = jnp.split(
                proj, [WIDTH_A, 2 * WIDTH_A, 3 * WIDTH_A, 3 * WIDTH_A + N_HEADS_A], axis=-1)
            q = rmsnorm(q.reshape(b, s, N_HEADS_A, HEAD_DIM), qnorm_a_g[i])
            k = rmsnorm(k.reshape(b, s, N_HEADS_A, HEAD_DIM), knorm_a_g[i])
            v = v.reshape(b, s, N_HEADS_A, HEAD_DIM)
            log_f = jax.nn.log_sigmoid((f_logit + b_forget[i]).astype(jnp.float32))
            o = fox_attention(q, k, v, log_f).reshape(b, s, WIDTH_A)
            h = h + (o * jax.nn.silu(gate)) @ w_out_a[i]
        else:
            if layer == N_A_LAYERS:
                u_kv = rmsnorm(h, kv_norm_g)
                k_s, v_s = jnp.split(u_kv @ w_kv, [KV_WIDTH_B], axis=-1)
                k_shared = partial_rope(rmsnorm(k_s.reshape(b, s, N_KV_HEADS_B, HEAD_DIM), knorm_b_g), positions)
                v_shared = v_s.reshape(b, s, N_KV_HEADS_B, HEAD_DIM)
            j = layer - N_A_LAYERS
            u = rmsnorm(h, norm_b_g[j])
            q, gate = jnp.split(u @ w_in_b[j], [WIDTH_B], axis=-1)
            q = partial_rope(rmsnorm(q.reshape(b, s, N_Q_HEADS_B, HEAD_DIM), qnorm_b_g[j]), positions)
            o = swa_sink_attention(q, k_shared, v_shared, sinks[j]).reshape(b, s, WIDTH_B)
            h = h + (o * jax.nn.silu(gate)) @ w_out_b[j]
    return h


import jax as _jax
import jax.numpy as _jnp

TWIN_FORMAT = 'train_step'
FWD_PARAMS = ['x', 'positions', 'norm_a_g', 'w_in_a', 'b_forget', 'qnorm_a_g', 'knorm_a_g', 'w_out_a', 'kv_norm_g', 'w_kv', 'knorm_b_g', 'norm_b_g', 'w_in_b', 'qnorm_b_g', 'sinks', 'w_out_b']
TWIN_WEIGHTS = ['norm_a_g', 'w_in_a', 'b_forget', 'qnorm_a_g', 'knorm_a_g', 'w_out_a', 'kv_norm_g', 'w_kv', 'knorm_b_g', 'norm_b_g', 'w_in_b', 'qnorm_b_g', 'sinks', 'w_out_b']
TWIN_DIFF_INPUT = 'x'
TWIN_INPUTS = ['x', 'positions', 'norm_a_g', 'w_in_a', 'b_forget', 'qnorm_a_g', 'knorm_a_g', 'w_out_a', 'kv_norm_g', 'w_kv', 'knorm_b_g', 'norm_b_g', 'w_in_b', 'qnorm_b_g', 'sinks', 'w_out_b', 'loss_target', 'm_norm_a_g', 'm_w_in_a', 'm_b_forget', 'm_qnorm_a_g', 'm_knorm_a_g', 'm_w_out_a', 'm_kv_norm_g', 'm_w_kv', 'm_knorm_b_g', 'm_norm_b_g', 'm_w_in_b', 'm_qnorm_b_g', 'm_sinks', 'm_w_out_b', 'v_norm_a_g', 'v_w_in_a', 'v_b_forget', 'v_qnorm_a_g', 'v_knorm_a_g', 'v_w_out_a', 'v_kv_norm_g', 'v_w_kv', 'v_knorm_b_g', 'v_norm_b_g', 'v_w_in_b', 'v_qnorm_b_g', 'v_sinks', 'v_w_out_b']
TWIN_OUTPUTS = ['loss', 'grad_x', 'grad_norm_a_g', 'grad_w_in_a', 'grad_b_forget', 'grad_qnorm_a_g', 'grad_knorm_a_g', 'grad_w_out_a', 'grad_kv_norm_g', 'grad_w_kv', 'grad_knorm_b_g', 'grad_norm_b_g', 'grad_w_in_b', 'grad_qnorm_b_g', 'grad_sinks', 'grad_w_out_b', 'delta_norm_a_g', 'delta_w_in_a', 'delta_b_forget', 'delta_qnorm_a_g', 'delta_knorm_a_g', 'delta_w_out_a', 'delta_kv_norm_g', 'delta_w_kv', 'delta_knorm_b_g', 'delta_norm_b_g', 'delta_w_in_b', 'delta_qnorm_b_g', 'delta_sinks', 'delta_w_out_b', 'new_m_norm_a_g', 'new_m_w_in_a', 'new_m_b_forget', 'new_m_qnorm_a_g', 'new_m_knorm_a_g', 'new_m_w_out_a', 'new_m_kv_norm_g', 'new_m_w_kv', 'new_m_knorm_b_g', 'new_m_norm_b_g', 'new_m_w_in_b', 'new_m_qnorm_b_g', 'new_m_sinks', 'new_m_w_out_b', 'new_v_norm_a_g', 'new_v_w_in_a', 'new_v_b_forget', 'new_v_qnorm_a_g', 'new_v_knorm_a_g', 'new_v_w_out_a', 'new_v_kv_norm_g', 'new_v_w_kv', 'new_v_knorm_b_g', 'new_v_norm_b_g', 'new_v_w_in_b', 'new_v_qnorm_b_g', 'new_v_sinks', 'new_v_w_out_b']
TWIN_LEAF_KINDS = {'loss': 'loss', 'grad_x': 'grad_x', 'grad_norm_a_g': 'grad_w', 'grad_w_in_a': 'grad_w', 'grad_b_forget': 'grad_w', 'grad_qnorm_a_g': 'grad_w', 'grad_knorm_a_g': 'grad_w', 'grad_w_out_a': 'grad_w', 'grad_kv_norm_g': 'grad_w', 'grad_w_kv': 'grad_w', 'grad_knorm_b_g': 'grad_w', 'grad_norm_b_g': 'grad_w', 'grad_w_in_b': 'grad_w', 'grad_qnorm_b_g': 'grad_w', 'grad_sinks': 'grad_w', 'grad_w_out_b': 'grad_w', 'delta_norm_a_g': 'delta_w', 'delta_w_in_a': 'delta_w', 'delta_b_forget': 'delta_w', 'delta_qnorm_a_g': 'delta_w', 'delta_knorm_a_g': 'delta_w', 'delta_w_out_a': 'delta_w', 'delta_kv_norm_g': 'delta_w', 'delta_w_kv': 'delta_w', 'delta_knorm_b_g': 'delta_w', 'delta_norm_b_g': 'delta_w', 'delta_w_in_b': 'delta_w', 'delta_qnorm_b_g': 'delta_w', 'delta_sinks': 'delta_w', 'delta_w_out_b': 'delta_w', 'new_m_norm_a_g': 'new_m', 'new_m_w_in_a': 'new_m', 'new_m_b_forget': 'new_m', 'new_m_qnorm_a_g': 'new_m', 'new_m_knorm_a_g': 'new_m', 'new_m_w_out_a': 'new_m', 'new_m_kv_norm_g': 'new_m', 'new_m_w_kv': 'new_m', 'new_m_knorm_b_g': 'new_m', 'new_m_norm_b_g': 'new_m', 'new_m_w_in_b': 'new_m', 'new_m_qnorm_b_g': 'new_m', 'new_m_sinks': 'new_m', 'new_m_w_out_b': 'new_m', 'new_v_norm_a_g': 'new_v', 'new_v_w_in_a': 'new_v', 'new_v_b_forget': 'new_v', 'new_v_qnorm_a_g': 'new_v', 'new_v_knorm_a_g': 'new_v', 'new_v_w_out_a': 'new_v', 'new_v_kv_norm_g': 'new_v', 'new_v_w_kv': 'new_v', 'new_v_knorm_b_g': 'new_v', 'new_v_norm_b_g': 'new_v', 'new_v_w_in_b': 'new_v', 'new_v_qnorm_b_g': 'new_v', 'new_v_sinks': 'new_v', 'new_v_w_out_b': 'new_v'}


def _forward(args):
    return _fwd_reference(*[args[k] for k in FWD_PARAMS])


def _output_shape():
    out = _jax.eval_shape(lambda: _forward(_fwd_setup_inputs(0)))
    return out.shape, out.dtype

N_MICROBATCH = 1
ADAM_LR = 0.001
ADAM_B1 = 0.9
ADAM_B2 = 0.999
ADAM_EPS = 1e-08
ADAM_WD = 0.01
ADAM_STEP = 10
PER_EXAMPLE_BATCH_AXIS = {'x': 0, 'loss_target': 0}
SHARED_INPUTS = ['positions']
_WEIGHT_DTYPES = {'norm_a_g': _jnp.float32, 'w_in_a': _jnp.float32, 'b_forget': _jnp.float32, 'qnorm_a_g': _jnp.float32, 'knorm_a_g': _jnp.float32, 'w_out_a': _jnp.float32, 'kv_norm_g': _jnp.float32, 'w_kv': _jnp.float32, 'knorm_b_g': _jnp.float32, 'norm_b_g': _jnp.float32, 'w_in_b': _jnp.float32, 'qnorm_b_g': _jnp.float32, 'sinks': _jnp.float32, 'w_out_b': _jnp.float32}
MOMENT_SCALE = {'norm_a_g': 1.070797e+00, 'w_in_a': 3.892487e-02, 'b_forget': 9.747994e+00, 'qnorm_a_g': 4.036772e+00, 'knorm_a_g': 4.006484e+00, 'w_out_a': 3.673448e-02, 'kv_norm_g': 8.897323e-02, 'w_kv': 5.038521e-02, 'knorm_b_g': 1.249334e+00, 'norm_b_g': 1.038099e-01, 'w_in_b': 2.310879e-02, 'qnorm_b_g': 1.250624e+00, 'sinks': 2.043102e-01, 'w_out_b': 2.219589e-02}


def _to_microbatches(a, axis):
    t = _jnp.moveaxis(a, axis, 0)
    t = t.reshape((N_MICROBATCH, t.shape[0] // N_MICROBATCH) + t.shape[1:])
    return _jnp.moveaxis(t, 1, axis + 1)


def setup_inputs(seed: int = 0) -> dict:
    inp = _fwd_setup_inputs(seed)
    key = _jax.random.fold_in(_jax.random.key(seed), 7919)
    shape, _ = _output_shape()
    out = dict(inp)
    out["loss_target"] = _jax.random.normal(_jax.random.fold_in(key, 0), shape, _jnp.float32)
    for i, name in enumerate(TWIN_WEIGHTS):
        w = inp[name].astype(_jnp.float32)
        if MOMENT_SCALE is None:
            s = _jnp.sqrt(_jnp.mean(_jnp.square(w)) + 1e-30)
        else:
            s = MOMENT_SCALE[name]
        km, kv = _jax.random.split(_jax.random.fold_in(key, i + 1))
        out[name] = w
        out["m_" + name] = s * _jax.random.normal(km, w.shape, _jnp.float32)
        out["v_" + name] = (s * s) * _jax.random.uniform(kv, w.shape, _jnp.float32, 0.5, 1.5)
    if N_MICROBATCH > 1:
        for name, axis in PER_EXAMPLE_BATCH_AXIS.items():
            out[name] = _to_microbatches(out[name], axis)
    return {'x': out['x'], 'positions': out['positions'], 'norm_a_g': out['norm_a_g'], 'w_in_a': out['w_in_a'], 'b_forget': out['b_forget'], 'qnorm_a_g': out['qnorm_a_g'], 'knorm_a_g': out['knorm_a_g'], 'w_out_a': out['w_out_a'], 'kv_norm_g': out['kv_norm_g'], 'w_kv': out['w_kv'], 'knorm_b_g': out['knorm_b_g'], 'norm_b_g': out['norm_b_g'], 'w_in_b': out['w_in_b'], 'qnorm_b_g': out['qnorm_b_g'], 'sinks': out['sinks'], 'w_out_b': out['w_out_b'], 'loss_target': out['loss_target'], 'm_norm_a_g': out['m_norm_a_g'], 'm_w_in_a': out['m_w_in_a'], 'm_b_forget': out['m_b_forget'], 'm_qnorm_a_g': out['m_qnorm_a_g'], 'm_knorm_a_g': out['m_knorm_a_g'], 'm_w_out_a': out['m_w_out_a'], 'm_kv_norm_g': out['m_kv_norm_g'], 'm_w_kv': out['m_w_kv'], 'm_knorm_b_g': out['m_knorm_b_g'], 'm_norm_b_g': out['m_norm_b_g'], 'm_w_in_b': out['m_w_in_b'], 'm_qnorm_b_g': out['m_qnorm_b_g'], 'm_sinks': out['m_sinks'], 'm_w_out_b': out['m_w_out_b'], 'v_norm_a_g': out['v_norm_a_g'], 'v_w_in_a': out['v_w_in_a'], 'v_b_forget': out['v_b_forget'], 'v_qnorm_a_g': out['v_qnorm_a_g'], 'v_knorm_a_g': out['v_knorm_a_g'], 'v_w_out_a': out['v_w_out_a'], 'v_kv_norm_g': out['v_kv_norm_g'], 'v_w_kv': out['v_w_kv'], 'v_knorm_b_g': out['v_knorm_b_g'], 'v_norm_b_g': out['v_norm_b_g'], 'v_w_in_b': out['v_w_in_b'], 'v_qnorm_b_g': out['v_qnorm_b_g'], 'v_sinks': out['v_sinks'], 'v_w_out_b': out['v_w_out_b']}


def _loss(weights, diff, rest, loss_target):
    with _jax.named_scope("forward"):
        args = {**rest, TWIN_DIFF_INPUT: diff, **{k: w.astype(_WEIGHT_DTYPES[k]) for k, w in weights.items()}}
        y = _forward(args)
    with _jax.named_scope("loss_head"):
        err = _jnp.square(y.astype(_jnp.float32) - loss_target)
        return 0.5 * _jnp.sum(_jnp.mean(err, axis=-1)) if err.ndim else 0.5 * err


def _adamw(w, g, m, v):
    m = ADAM_B1 * m + (1.0 - ADAM_B1) * g
    v = ADAM_B2 * v + (1.0 - ADAM_B2) * _jnp.square(g)
    m_hat = m / (1.0 - ADAM_B1 ** ADAM_STEP)
    v_hat = v / (1.0 - ADAM_B2 ** ADAM_STEP)
    delta = -ADAM_LR * (m_hat / (_jnp.sqrt(v_hat) + ADAM_EPS) + ADAM_WD * w)
    return delta, m, v


def reference(x, positions, norm_a_g, w_in_a, b_forget, qnorm_a_g, knorm_a_g, w_out_a, kv_norm_g, w_kv, knorm_b_g, norm_b_g, w_in_b, qnorm_b_g, sinks, w_out_b, loss_target, m_norm_a_g, m_w_in_a, m_b_forget, m_qnorm_a_g, m_knorm_a_g, m_w_out_a, m_kv_norm_g, m_w_kv, m_knorm_b_g, m_norm_b_g, m_w_in_b, m_qnorm_b_g, m_sinks, m_w_out_b, v_norm_a_g, v_w_in_a, v_b_forget, v_qnorm_a_g, v_knorm_a_g, v_w_out_a, v_kv_norm_g, v_w_kv, v_knorm_b_g, v_norm_b_g, v_w_in_b, v_qnorm_b_g, v_sinks, v_w_out_b):
    given = dict(x=x, positions=positions, norm_a_g=norm_a_g, w_in_a=w_in_a, b_forget=b_forget, qnorm_a_g=qnorm_a_g, knorm_a_g=knorm_a_g, w_out_a=w_out_a, kv_norm_g=kv_norm_g, w_kv=w_kv, knorm_b_g=knorm_b_g, norm_b_g=norm_b_g, w_in_b=w_in_b, qnorm_b_g=qnorm_b_g, sinks=sinks, w_out_b=w_out_b, loss_target=loss_target, m_norm_a_g=m_norm_a_g, m_w_in_a=m_w_in_a, m_b_forget=m_b_forget, m_qnorm_a_g=m_qnorm_a_g, m_knorm_a_g=m_knorm_a_g, m_w_out_a=m_w_out_a, m_kv_norm_g=m_kv_norm_g, m_w_kv=m_w_kv, m_knorm_b_g=m_knorm_b_g, m_norm_b_g=m_norm_b_g, m_w_in_b=m_w_in_b, m_qnorm_b_g=m_qnorm_b_g, m_sinks=m_sinks, m_w_out_b=m_w_out_b, v_norm_a_g=v_norm_a_g, v_w_in_a=v_w_in_a, v_b_forget=v_b_forget, v_qnorm_a_g=v_qnorm_a_g, v_knorm_a_g=v_knorm_a_g, v_w_out_a=v_w_out_a, v_kv_norm_g=v_kv_norm_g, v_w_kv=v_w_kv, v_knorm_b_g=v_knorm_b_g, v_norm_b_g=v_norm_b_g, v_w_in_b=v_w_in_b, v_qnorm_b_g=v_qnorm_b_g, v_sinks=v_sinks, v_w_out_b=v_w_out_b)
    weights = {n: given[n] for n in TWIN_WEIGHTS}
    shared = {n: given[n] for n in SHARED_INPUTS}
    per_example = {n: given[n] for n in ['x']}
    grad_fn = _jax.value_and_grad(_loss, argnums=(0, 1))

    def one_microbatch(ex, loss_target):
        ex = dict(ex)
        diff = ex.pop(TWIN_DIFF_INPUT)
        return grad_fn(weights, diff, {**shared, **ex}, loss_target)

    if N_MICROBATCH == 1:
        loss, (grad_w, grad_x) = one_microbatch(per_example, given["loss_target"])
    else:
        def body(carry, xs):
            loss_sum, grad_sum = carry
            l_k, (gw_k, gx_k) = one_microbatch(xs[0], xs[1])
            with _jax.named_scope("update"):
                return (loss_sum + l_k, _jax.tree.map(_jnp.add, grad_sum, gw_k)), gx_k

        init = (_jnp.zeros((), _jnp.float32), _jax.tree.map(_jnp.zeros_like, weights))
        (loss, grad_w), grad_x = _jax.lax.scan(body, init, (per_example, given["loss_target"]))
    with _jax.named_scope("update"):
        delta_w, new_m, new_v = {}, {}, {}
        for n in TWIN_WEIGHTS:
            delta_w[n], new_m[n], new_v[n] = _adamw(weights[n], grad_w[n], given["m_" + n], given["v_" + n])
    return (loss, grad_x, *[grad_w[n] for n in TWIN_WEIGHTS], *[delta_w[n] for n in TWIN_WEIGHTS],
            *[new_m[n] for n in TWIN_WEIGHTS], *[new_v[n] for n in TWIN_WEIGHTS])
```

```python
import numpy as np
import jax
import jax.numpy as jnp
from jax import lax
from jax.experimental import pallas as pl
from jax.experimental.pallas import tpu as pltpu

F32, BF16 = jnp.float32, jnp.bfloat16
S, D, HD, NH, NKV = 2048, 1024, 64, 16, 4
KVW = NKV * HD
WINDOW = 128
ROT = HD // 4
THETA = 500000.0
EPS = 1e-6
SCALE = HD ** -0.5
LANES = 128
NEG = -1e30
VMEM_LIMIT = 48 * 2 ** 20
ROWS = 256
ATT = 256
NCHIP = 4
ADAM_LR, ADAM_B1, ADAM_B2, ADAM_EPS, ADAM_WD, ADAM_STEP = 0.001, 0.9, 0.999, 1e-08, 0.01, 10
NT = (((1,), (1,)), ((), ()))
TN = (((0,), (0,)), ((), ()))
MESH = pl.DeviceIdType.MESH


def _params(n):
    return pltpu.CompilerParams(dimension_semantics=("arbitrary",) * n, vmem_limit_bytes=VMEM_LIMIT)


def _dot(a, b, dims=None):
    if dims is None:
        return jnp.dot(a, b, preferred_element_type=F32)
    return lax.dot_general(a, b, dims, preferred_element_type=F32)


def _dot_split(a, b, n):
    out, rest = None, a
    for _ in range(n):
        hi = rest.astype(BF16)
        term = _dot(hi, b)
        out = term if out is None else out + term
        rest = rest - hi.astype(F32)
    return out


def _seg_mats(w):
    e = (np.arange(w)[:, None] // HD == np.arange(LANES)[None, :]).astype(np.float32)
    return jnp.asarray(e, BF16), jnp.asarray(e.T, BF16)


def _head_rstd(x, e, et):
    ss = _dot_split(x * x, e, 2)
    return _dot_split(lax.rsqrt(ss * (1.0 / HD) + EPS), et, 3)


def _rope(x, c, a, b):
    w = x.shape[1]
    return x * c + pltpu.roll(x, w - ROT // 2, 1) * a + pltpu.roll(x, ROT // 2, 1) * b


def _rope_t(dy, c, a, b):
    w = dy.shape[1]
    return dy * c + pltpu.roll(dy * b, w - ROT // 2, 1) + pltpu.roll(dy * a, ROT // 2, 1)


def _sigmoid(x):
    return 1.0 / (1.0 + jnp.exp(-x))


def _row_spec(shape, ts):
    nd = len(shape)
    if shape[0] == S:
        return pl.BlockSpec((ts,) + tuple(shape[1:]), lambda i: (i,) + (0,) * (nd - 1))
    return pl.BlockSpec(tuple(shape), lambda i: (0,) * nd)


def _rows_call(body, name, ins, outs, ts=ROWS):
    return pl.pallas_call(
        body, name=name, grid=(S // ts,),
        in_specs=[_row_spec(a.shape, ts) for a in ins],
        out_specs=[_row_spec(s, ts) for s, _ in outs],
        out_shape=[jax.ShapeDtypeStruct(s, d) for s, d in outs],
        compiler_params=_params(1))(*ins)


def _col_spec(ts, w, col):
    return pl.BlockSpec((ts, w), lambda i: (i, col))


def _matmul(a, b, mode, out_dtype, name, add=None):
    if mode == "nn":
        (m, k), n = a.shape, b.shape[1]
    elif mode == "nt":
        (m, k), n = a.shape, b.shape[0]
    else:
        (k, m), n = a.shape, b.shape[1]
    tm, tn = min(512, m), min(512, n)
    a_spec = pl.BlockSpec((k, tm), lambda j, i: (0, i)) if mode == "tn" else pl.BlockSpec((tm, k), lambda j, i: (i, 0))
    b_spec = pl.BlockSpec((tn, k), lambda j, i: (j, 0)) if mode == "nt" else pl.BlockSpec((k, tn), lambda j, i: (0, j))
    o_spec = pl.BlockSpec((tm, tn), lambda j, i: (i, j))
    dims = {"nn": None, "nt": NT, "tn": TN}[mode]

    def body(*refs):
        acc = _dot(refs[0][...], refs[1][...], dims)
        if add is not None:
            acc = acc + refs[2][...]
        refs[-1][...] = acc.astype(out_dtype)

    ins = [a, b] + ([add] if add is not None else [])
    return pl.pallas_call(
        body, name=name, grid=(n // tn, m // tm),
        in_specs=[a_spec, b_spec] + ([o_spec] if add is not None else []),
        out_specs=o_spec, out_shape=jax.ShapeDtypeStruct((m, n), out_dtype),
        compiler_params=_params(2))(*ins)


def _rmsnorm_fwd(x, gains, name):
    def body(*refs):
        xv = refs[0][...]
        r = lax.rsqrt(jnp.mean(xv * xv, axis=-1, keepdims=True) + EPS)
        xh = xv * r
        for n in range(len(gains)):
            refs[1 + len(gains) + n][...] = (xh * refs[1 + n][...]).astype(BF16)

    return _rows_call(body, name, [x] + list(gains), [((S, D), BF16)] * len(gains))


def _rmsnorm_bwd(x, gains, dus, dres, name):
    n = len(gains)

    def body(*refs):
        x_ref, g_refs, du_refs, dres_ref = refs[0], refs[1:1 + n], refs[1 + n:1 + 2 * n], refs[1 + 2 * n]
        dx_ref, dxb_ref, dg_refs = refs[2 + 2 * n], refs[3 + 2 * n], refs[4 + 2 * n:]
        xv = x_ref[...]
        r = lax.rsqrt(jnp.mean(xv * xv, axis=-1, keepdims=True) + EPS)
        xh = xv * r
        gy = None
        for m in range(n):
            du = du_refs[m][...]
            part = jnp.sum(du * xh, axis=0, keepdims=True)

            @pl.when(pl.program_id(0) == 0)
            def _(m=m, part=part):
                dg_refs[m][...] = part

            @pl.when(pl.program_id(0) != 0)
            def _(m=m, part=part):
                dg_refs[m][...] += part

            t = du * g_refs[m][...]
            gy = t if gy is None else gy + t
        dx = dres_ref[...] + r * (gy - xh * jnp.mean(gy * xh, axis=-1, keepdims=True))
        dx_ref[...] = dx
        dxb_ref[...] = dx.astype(BF16)

    outs = [((S, D), F32), ((S, D), BF16)] + [((1, D), F32)] * n
    return _rows_call(body, name, [x] + list(gains) + list(dus) + [dres], outs)


def _a_post(qkvg, qg, kg):
    e, et = _seg_mats(D)

    def body(q_ref, k_ref, v_ref, qg_ref, kg_ref, e_ref, et_ref, qo, ko, vo):
        ev, etv = e_ref[...], et_ref[...]
        qv, kv = q_ref[...], k_ref[...]
        qo[...] = (qv * _head_rstd(qv, ev, etv) * qg_ref[...]).astype(BF16)
        ko[...] = (kv * _head_rstd(kv, ev, etv) * kg_ref[...]).astype(BF16)
        vo[...] = v_ref[...].astype(BF16)

    whole = lambda a: pl.BlockSpec(a.shape, lambda i: (0, 0))
    return pl.pallas_call(
        body, name="a_post", grid=(S // ROWS,),
        in_specs=[_col_spec(ROWS, D, 0), _col_spec(ROWS, D, 1), _col_spec(ROWS, D, 2),
                  whole(qg), whole(kg), whole(e), whole(et)],
        out_specs=[_col_spec(ROWS, D, 0)] * 3,
        out_shape=[jax.ShapeDtypeStruct((S, D), BF16)] * 3,
        compiler_params=_params(1))(qkvg, qkvg, qkvg, qg, kg, e, et)


def _tri(upper):
    r, c = np.arange(ROWS)[:, None], np.arange(ROWS)[None, :]
    return jnp.asarray((r <= c) if upper else (r >= c), BF16)


def _forget_cumsum(fpad, bpad):
    def body(f_ref, b_ref, u_ref, c_ref, carry):
        @pl.when(pl.program_id(0) == 0)
        def _():
            carry[...] = jnp.zeros_like(carry)

        lf = jax.nn.log_sigmoid(f_ref[...] + b_ref[...])
        blk = _dot_split(lf.T, u_ref[...], 3) + carry[:, 0:1]
        c_ref[...] = blk
        carry[...] = jnp.broadcast_to(blk[:, ROWS - 1:ROWS], carry.shape)

    return pl.pallas_call(
        body, name="forget_cumsum", grid=(S // ROWS,),
        in_specs=[pl.BlockSpec((ROWS, LANES), lambda i: (i, 0)), pl.BlockSpec((1, LANES), lambda i: (0, 0)),
                  pl.BlockSpec((ROWS, ROWS), lambda i: (0, 0))],
        out_specs=pl.BlockSpec((LANES, ROWS), lambda i: (0, i)),
        out_shape=jax.ShapeDtypeStruct((LANES, S), F32),
        scratch_shapes=[pltpu.VMEM((LANES, LANES), F32)],
        compiler_params=_params(1))(fpad, bpad, _tri(True))


def _forget_bwd(dct, fpad, bpad):
    nb = S // ROWS

    def body(dc_ref, f_ref, b_ref, l_ref, df_ref, db_ref, carry):
        @pl.when(pl.program_id(0) == 0)
        def _():
            carry[...] = jnp.zeros_like(carry)
            db_ref[...] = jnp.zeros_like(db_ref)

        blk = _dot_split(dc_ref[...], l_ref[...], 3) + carry[:, 0:1]
        carry[...] = jnp.broadcast_to(blk[:, 0:1], carry.shape)
        df = blk.T * _sigmoid(-(f_ref[...] + b_ref[...]))
        df_ref[...] = df.astype(BF16)
        db_ref[...] += jnp.sum(df, axis=0, keepdims=True)

    return pl.pallas_call(
        body, name="forget_bwd", grid=(nb,),
        in_specs=[pl.BlockSpec((LANES, ROWS), lambda i: (0, nb - 1 - i)),
                  pl.BlockSpec((ROWS, LANES), lambda i: (nb - 1 - i, 0)),
                  pl.BlockSpec((1, LANES), lambda i: (0, 0)), pl.BlockSpec((ROWS, ROWS), lambda i: (0, 0))],
        out_specs=[pl.BlockSpec((ROWS, LANES), lambda i: (nb - 1 - i, 0)), pl.BlockSpec((1, LANES), lambda i: (0, 0))],
        out_shape=[jax.ShapeDtypeStruct((S, LANES), BF16), jax.ShapeDtypeStruct((1, LANES), F32)],
        scratch_shapes=[pltpu.VMEM((LANES, LANES), F32)],
        compiler_params=_params(1))(dct, fpad, bpad, _tri(False))


def _gate_fwd(o, proj, col, name):
    def body(o_ref, g_ref, y_ref):
        g = g_ref[...]
        y_ref[...] = (o_ref[...] * (g * _sigmoid(g))).astype(BF16)

    return pl.pallas_call(
        body, name=name, grid=(S // ROWS,),
        in_specs=[_col_spec(ROWS, D, 0), _col_spec(ROWS, D, col)],
        out_specs=_col_spec(ROWS, D, 0), out_shape=jax.ShapeDtypeStruct((S, D), BF16),
        compiler_params=_params(1))(o, proj)


def _gate_bwd(dy, o, proj, col, name):
    def body(dy_ref, o_ref, g_ref, do_ref, dg_ref):
        g, dyv = g_ref[...], dy_ref[...]
        sg = _sigmoid(g)
        do_ref[...] = dyv * (g * sg)
        dg_ref[...] = (dyv * o_ref[...] * (sg * (1.0 + g * (1.0 - sg)))).astype(BF16)

    return pl.pallas_call(
        body, name=name, grid=(S // ROWS,),
        in_specs=[_col_spec(ROWS, D, 0), _col_spec(ROWS, D, 0), _col_spec(ROWS, D, col)],
        out_specs=[_col_spec(ROWS, D, 0)] * 2,
        out_shape=[jax.ShapeDtypeStruct((S, D), F32), jax.ShapeDtypeStruct((S, D), BF16)],
        compiler_params=_params(1))(dy, o, proj)


def _headnorm_bwd(x, col, gain, dy, rope, name):
    e, et = _seg_mats(D)
    tabs = list(rope) if rope is not None else []

    def body(*refs):
        x_ref, g_ref, dy_ref, e_ref, et_ref = refs[:5]
        dx_ref, dg_ref = refs[-2:]
        xv, dyv, ev, etv = x_ref[...], dy_ref[...], e_ref[...], et_ref[...]
        if rope is not None:
            c, a, b = (jnp.tile(t[...], (1, D // LANES)) for t in refs[5:8])
            dyv = _rope_t(dyv, c, a, b)
        r = _head_rstd(xv, ev, etv)
        xh = xv * r
        part = jnp.sum(dyv * xh, axis=0, keepdims=True)

        @pl.when(pl.program_id(0) == 0)
        def _():
            dg_ref[...] = part

        @pl.when(pl.program_id(0) != 0)
        def _():
            dg_ref[...] += part

        gy = dyv * g_ref[...]
        seg = _dot_split(_dot_split(gy * xh, ev, 2) * (1.0 / HD), etv, 3)
        dx_ref[...] = (r * (gy - xh * seg)).astype(BF16)

    whole = lambda a: pl.BlockSpec(a.shape, lambda i: (0, 0))
    return pl.pallas_call(
        body, name=name, grid=(S // ROWS,),
        in_specs=[_col_spec(ROWS, D, col), whole(gain), _col_spec(ROWS, D, 0), whole(e), whole(et)]
                 + [pl.BlockSpec((ROWS, LANES), lambda i: (i, 0))] * len(tabs),
        out_specs=[_col_spec(ROWS, D, 0), whole(gain)],
        out_shape=[jax.ShapeDtypeStruct((S, D), BF16), jax.ShapeDtypeStruct((1, D), F32)],
        compiler_params=_params(1))(x, gain, dy, e, et, *tabs)


def _dup_mat():
    r, c = np.arange(KVW)[:, None], np.arange(2 * KVW)[None, :]
    return (r // HD == c // LANES) & (r % HD == c % HD)


def _fold_mat():
    r, c = np.arange(D)[:, None], np.arange(KVW)[None, :]
    return (r // (2 * LANES) == c // HD) & (r % HD == c % HD)


def _b_post(pb, kv, qg, kg, rope):
    e, et = _seg_mats(D)
    ek, etk = _seg_mats(KVW)
    dup = jnp.asarray(_dup_mat(), BF16)

    def body(q_ref, k_ref, v_ref, qg_ref, kg_ref, e_ref, et_ref, ek_ref, etk_ref, dup_ref, c_ref, a_ref, b_ref,
             qo, ko, vo):
        c1, a1, b1 = c_ref[...], a_ref[...], b_ref[...]
        qv = q_ref[...]
        qn = qv * _head_rstd(qv, e_ref[...], et_ref[...]) * qg_ref[...]
        t = lambda z, n: jnp.tile(z, (1, n))
        qo[...] = _rope(qn, t(c1, D // LANES), t(a1, D // LANES), t(b1, D // LANES)).astype(BF16)
        kvv = k_ref[...]
        kn = kvv * _head_rstd(kvv, ek_ref[...], etk_ref[...]) * kg_ref[...]
        kr = _rope(kn, t(c1, KVW // LANES), t(a1, KVW // LANES), t(b1, KVW // LANES)).astype(BF16)
        ko[...] = _dot(kr, dup_ref[...]).astype(BF16)
        vo[...] = _dot(v_ref[...].astype(BF16), dup_ref[...]).astype(BF16)

    whole = lambda a: pl.BlockSpec(a.shape, lambda i: (0, 0))
    tab = pl.BlockSpec((ROWS, LANES), lambda i: (i, 0))
    return pl.pallas_call(
        body, name="b_post", grid=(S // ROWS,),
        in_specs=[_col_spec(ROWS, D, 0), _col_spec(ROWS, KVW, 0), _col_spec(ROWS, KVW, 1),
                  whole(qg), whole(kg), whole(e), whole(et), whole(ek), whole(etk), whole(dup), tab, tab, tab],
        out_specs=[_col_spec(ROWS, D, 0), _col_spec(ROWS, 2 * KVW, 0), _col_spec(ROWS, 2 * KVW, 0)],
        out_shape=[jax.ShapeDtypeStruct((S, D), BF16), jax.ShapeDtypeStruct((S, 2 * KVW), BF16),
                   jax.ShapeDtypeStruct((S, 2 * KVW), BF16)],
        compiler_params=_params(1))(pb, kv, kv, qg, kg, e, et, ek, etk, dup, *rope)


def _kv_bwd(dkdup, dvdup, kv, kg, rope):
    ek, etk = _seg_mats(KVW)
    fold = jnp.asarray(_fold_mat(), BF16)

    def body(dk_ref, dv_ref, k_ref, kg_ref, ek_ref, etk_ref, fold_ref, c_ref, a_ref, b_ref, dkv_ref, dg_ref):
        ev, etv, fv = ek_ref[...], etk_ref[...], fold_ref[...]
        t = lambda z: jnp.tile(z[...], (1, KVW // LANES))
        dk = _rope_t(_dot_split(dk_ref[...], fv, 3), t(c_ref), t(a_ref), t(b_ref))
        dv = _dot_split(dv_ref[...], fv, 3)
        xv = k_ref[...]
        r = _head_rstd(xv, ev, etv)
        xh = xv * r
        part = jnp.sum(dk * xh, axis=0, keepdims=True)

        @pl.when(pl.program_id(0) == 0)
        def _():
            dg_ref[...] = part

        @pl.when(pl.program_id(0) != 0)
        def _():
            dg_ref[...] += part

        gy = dk * kg_ref[...]
        seg = _dot_split(_dot_split(gy * xh, ev, 2) * (1.0 / HD), etv, 3)
        dkv_ref[:, 0:KVW] = (r * (gy - xh * seg)).astype(BF16)
        dkv_ref[:, KVW:2 * KVW] = dv.astype(BF16)

    whole = lambda a: pl.BlockSpec(a.shape, lambda i: (0, 0))
    tab = pl.BlockSpec((ROWS, LANES), lambda i: (i, 0))
    return pl.pallas_call(
        body, name="kv_bwd", grid=(S // ROWS,),
        in_specs=[_col_spec(ROWS, D, 0), _col_spec(ROWS, D, 0), _col_spec(ROWS, KVW, 0),
                  whole(kg), whole(ek), whole(etk), whole(fold), tab, tab, tab],
        out_specs=[_col_spec(ROWS, 2 * KVW, 0), whole(kg)],
        out_shape=[jax.ShapeDtypeStruct((S, 2 * KVW), BF16), jax.ShapeDtypeStruct((1, KVW), F32)],
        compiler_params=_params(1))(dkdup, dvdup, kv, kg, ek, etk, fold, *rope)


def _loss_head(out, target):
    def body(o_ref, t_ref, d_ref, db_ref, l_ref):
        diff = o_ref[...] - t_ref[...]
        d = diff * (1.0 / D)
        d_ref[...] = d
        db_ref[...] = d.astype(BF16)

        @pl.when(pl.program_id(0) == 0)
        def _():
            l_ref[...] = jnp.zeros_like(l_ref)

        l_ref[...] += jnp.sum(diff * diff, axis=0, keepdims=True)

    return _rows_call(body, "loss_head", [out, target], [((S, D), F32), ((S, D), BF16), ((1, D), F32)])


def _lane():
    return lax.broadcasted_iota(jnp.int32, (1, LANES), 1)


def _head_mask(hh):
    return (_lane() < HD) if hh == 0 else (_lane() >= HD)


def _fox_fwd(q, k, v, ct):
    nq = S // ATT

    def body(q_ref, k_ref, v_ref, c_ref, o_ref, lse_ref):
        i = pl.program_id(1)
        row = lax.broadcasted_iota(jnp.int32, (ATT, ATT), 0)
        col = lax.broadcasted_iota(jnp.int32, (ATT, ATT), 1)
        q2 = q_ref[...]
        res = []
        for hh in (0, 1):
            qm = jnp.where(_head_mask(hh), q2, jnp.zeros_like(q2))

            def step(j, carry, diag, hh=hh, qm=qm):
                m, l, acc = carry
                off = pl.multiple_of(j * ATT, ATT)
                kj, vj = k_ref[pl.ds(off, ATT), :], v_ref[pl.ds(off, ATT), :]
                s = _dot(qm, kj, NT) * SCALE - c_ref[hh:hh + 1, pl.ds(off, ATT)]
                if diag:
                    s = jnp.where(col <= row, s, NEG)
                m_new = jnp.maximum(m, jnp.max(s, axis=1, keepdims=True))
                p = jnp.exp(s - m_new)
                alpha = jnp.exp(m - m_new)
                l = alpha * l + jnp.sum(p, axis=1, keepdims=True)
                p_hi = p.astype(BF16)
                p_lo = (p - p_hi.astype(F32)).astype(BF16)
                acc = alpha * acc + (_dot(p_hi, vj) + _dot(p_lo, vj))
                return m_new, l, acc

            init = (jnp.full((ATT, 1), NEG, F32), jnp.zeros((ATT, 1), F32), jnp.zeros((ATT, LANES), F32))
            carry = lax.fori_loop(0, i, lambda j, cr: step(j, cr, False), init)
            m, l, acc = step(i, carry, True)
            res.append((acc / l, m + jnp.log(l)))
        first = _head_mask(0)
        o_ref[...] = jnp.where(first, res[0][0], res[1][0])
        lse_ref[...] = jnp.where(first, res[0][1], res[1][1])

    blk = pl.BlockSpec((ATT, LANES), lambda p, i: (i, p))
    full = pl.BlockSpec((S, LANES), lambda p, i: (0, p))
    return pl.pallas_call(
        body, name="fox_fwd", grid=(NH // 2, nq),
        in_specs=[blk, full, full, pl.BlockSpec((None, 2, S), lambda p, i: (p, 0, 0))],
        out_specs=[blk, blk],
        out_shape=[jax.ShapeDtypeStruct((S, D), F32)] * 2,
        compiler_params=_params(2))(q, k, v, ct)


def _fox_bwd(q, k, v, ct, o, lse, do):
    nq = S // ATT

    def body(q_ref, k_ref, v_ref, c_ref, o_ref, lse_ref, do_ref, dq_ref, dk_ref, dv_ref, dc_ref):
        i = pl.program_id(1)

        @pl.when(i == 0)
        def _():
            dk_ref[...] = jnp.zeros_like(dk_ref)
            dv_ref[...] = jnp.zeros_like(dv_ref)
            dc_ref[...] = jnp.zeros_like(dc_ref)

        row = lax.broadcasted_iota(jnp.int32, (ATT, ATT), 0)
        col = lax.broadcasted_iota(jnp.int32, (ATT, ATT), 1)
        q2, do2, lse2 = q_ref[...], do_ref[...], lse_ref[...]
        do2b = do2.astype(BF16)
        prod = do2b.astype(F32) * o_ref[...]
        dqs = []
        for hh in (0, 1):
            hm = _head_mask(hh)
            qm = jnp.where(hm, q2, jnp.zeros_like(q2))
            dom = jnp.where(hm, do2b, jnp.zeros_like(do2b))
            delta = jnp.sum(jnp.where(hm, prod, 0.0), axis=1, keepdims=True)
            lse_h = jnp.max(jnp.where(hm, lse2, NEG), axis=1, keepdims=True)

            def step(j, dq, diag, hh=hh, qm=qm, dom=dom, delta=delta, lse_h=lse_h):
                off = pl.multiple_of(j * ATT, ATT)
                kj, vj = k_ref[pl.ds(off, ATT), :], v_ref[pl.ds(off, ATT), :]
                s = _dot(qm, kj, NT) * SCALE - c_ref[hh:hh + 1, pl.ds(off, ATT)]
                p = jnp.exp(s - lse_h)
                if diag:
                    p = jnp.where(col <= row, p, 0.0)
                ds = p * (_dot(dom, vj, NT) - delta)
                dc_ref[hh:hh + 1, pl.ds(off, ATT)] += -jnp.sum(ds, axis=0, keepdims=True)
                dsb = (ds * SCALE).astype(BF16)
                dk_ref[pl.ds(off, ATT), :] += _dot(dsb, qm, TN)
                dv_ref[pl.ds(off, ATT), :] += _dot(p.astype(BF16), dom, TN)
                return dq + _dot(dsb, kj)

            dq = lax.fori_loop(0, i, lambda j, acc: step(j, acc, False), jnp.zeros((ATT, LANES), F32))
            dqs.append(step(i, dq, True))
        dq_ref[...] = jnp.where(_head_mask(0), dqs[0], dqs[1])

    blk = pl.BlockSpec((ATT, LANES), lambda p, i: (i, p))
    full = pl.BlockSpec((S, LANES), lambda p, i: (0, p))
    cspec = pl.BlockSpec((None, 2, S), lambda p, i: (p, 0, 0))
    return pl.pallas_call(
        body, name="fox_bwd", grid=(NH // 2, nq),
        in_specs=[blk, full, full, cspec, blk, blk, blk],
        out_specs=[blk, full, full, cspec],
        out_shape=[jax.ShapeDtypeStruct((S, D), F32)] * 3 + [jax.ShapeDtypeStruct((NH // 2, 2, S), F32)],
        compiler_params=_params(2))(q, k, v, ct, o, lse, do)


def _swa_logits(qm, kk, i, start):
    s = _dot(qm, kk, NT) * SCALE
    qabs = i * WINDOW + lax.broadcasted_iota(jnp.int32, (WINDOW, 2 * WINDOW), 0)
    kabs = start + lax.broadcasted_iota(jnp.int32, (WINDOW, 2 * WINDOW), 1)
    valid = (kabs <= qabs) & (qabs - kabs < WINDOW)
    return s, valid


def _swa_fwd(q, kdup, vdup, sinks_t):
    nq = S // WINDOW

    def body(q_ref, k_ref, v_ref, sk_ref, o_ref, lse_ref):
        i = pl.program_id(1)
        start = pl.multiple_of(jnp.maximum(i - 1, 0) * WINDOW, WINDOW)
        kk, vv = k_ref[pl.ds(start, 2 * WINDOW), :], v_ref[pl.ds(start, 2 * WINDOW), :]
        q2, skv = q_ref[...], sk_ref[...]
        res = []
        for hh in (0, 1):
            hm = _head_mask(hh)
            qm = jnp.where(hm, q2, jnp.zeros_like(q2))
            sink = jnp.max(jnp.where(hm, skv, NEG), axis=1, keepdims=True)
            s, valid = _swa_logits(qm, kk, i, start)
            s = jnp.where(valid, s, NEG)
            m = jnp.maximum(jnp.max(s, axis=1, keepdims=True), sink)
            p = jnp.exp(s - m)
            l = jnp.sum(p, axis=1, keepdims=True) + jnp.exp(sink - m)
            res.append((_dot(p.astype(BF16), vv) / l, m + jnp.log(l)))
        first = _head_mask(0)
        o_ref[...] = jnp.where(first, res[0][0], res[1][0])
        lse_ref[...] = jnp.where(first, res[0][1], res[1][1])

    blk = pl.BlockSpec((WINDOW, LANES), lambda p, i: (i, p))
    full = pl.BlockSpec((S, LANES), lambda p, i: (0, p // 2))
    return pl.pallas_call(
        body, name="swa_fwd", grid=(NH // 2, nq),
        in_specs=[blk, full, full, pl.BlockSpec((1, LANES), lambda p, i: (0, p))],
        out_specs=[blk, blk],
        out_shape=[jax.ShapeDtypeStruct((S, D), F32)] * 2,
        compiler_params=_params(2))(q, kdup, vdup, sinks_t)


def _swa_bwd(q, kdup, vdup, sinks_t, o, lse, do):
    nq = S // WINDOW

    def body(q_ref, k_ref, v_ref, sk_ref, o_ref, lse_ref, do_ref, dq_ref, dk_ref, dv_ref, dsk_ref):
        i = pl.program_id(1)

        @pl.when(i == 0)
        def _():
            dk_ref[...] = jnp.zeros_like(dk_ref)
            dv_ref[...] = jnp.zeros_like(dv_ref)
            dsk_ref[...] = jnp.zeros_like(dsk_ref)

        start = pl.multiple_of(jnp.maximum(i - 1, 0) * WINDOW, WINDOW)
        kk, vv = k_ref[pl.ds(start, 2 * WINDOW), :], v_ref[pl.ds(start, 2 * WINDOW), :]
        q2, do2, lse2, skv = q_ref[...], do_ref[...], lse_ref[...], sk_ref[...]
        do2b = do2.astype(BF16)
        prod = do2b.astype(F32) * o_ref[...]
        dqs, dsk = [], []
        for hh in (0, 1):
            hm = _head_mask(hh)
            qm = jnp.where(hm, q2, jnp.zeros_like(q2))
            dom = jnp.where(hm, do2b, jnp.zeros_like(do2b))
            delta = jnp.sum(jnp.where(hm, prod, 0.0), axis=1, keepdims=True)
            lse_h = jnp.max(jnp.where(hm, lse2, NEG), axis=1, keepdims=True)
            sink = jnp.max(jnp.where(hm, skv, NEG), axis=1, keepdims=True)
            s, valid = _swa_logits(qm, kk, i, start)
            p = jnp.where(valid, jnp.exp(s - lse_h), 0.0)
            ds = p * (_dot(dom, vv, NT) - delta)
            dsb = (ds * SCALE).astype(BF16)
            dk_ref[pl.ds(start, 2 * WINDOW), :] += _dot(dsb, qm, TN)
            dv_ref[pl.ds(start, 2 * WINDOW), :] += _dot(p.astype(BF16), dom, TN)
            dqs.append(_dot(dsb, kk))
            dsk.append(-jnp.sum(jnp.exp(sink - lse_h) * delta, axis=0, keepdims=True))
        first = _head_mask(0)
        dq_ref[...] = jnp.where(first, dqs[0], dqs[1])
        dsk_ref[...] += jnp.where(first, dsk[0], dsk[1])

    blk = pl.BlockSpec((WINDOW, LANES), lambda p, i: (i, p))
    full = pl.BlockSpec((S, LANES), lambda p, i: (0, p // 2))
    acc = pl.BlockSpec((S, LANES), lambda p, i: (0, p))
    sk = pl.BlockSpec((1, LANES), lambda p, i: (0, p))
    return pl.pallas_call(
        body, name="swa_bwd", grid=(NH // 2, nq),
        in_specs=[blk, full, full, sk, blk, blk, blk],
        out_specs=[blk, acc, acc, sk],
        out_shape=[jax.ShapeDtypeStruct((S, D), F32)] * 3 + [jax.ShapeDtypeStruct((1, D), F32)],
        compiler_params=_params(2))(q, kdup, vdup, sinks_t, o, lse, do)


def _adamw_math(w, g, m, v):
    m = ADAM_B1 * m + (1.0 - ADAM_B1) * g
    v = ADAM_B2 * v + (1.0 - ADAM_B2) * jnp.square(g)
    m_hat = m / (1.0 - ADAM_B1 ** ADAM_STEP)
    v_hat = v / (1.0 - ADAM_B2 ** ADAM_STEP)
    delta = -ADAM_LR * (m_hat / (jnp.sqrt(v_hat) + ADAM_EPS) + ADAM_WD * w)
    return delta, m, v


def _adamw(w, g, m, v, name):
    r, c = w.shape
    tr = min(r, 128)

    def body(w_ref, g_ref, m_ref, v_ref, d_ref, mo_ref, vo_ref):
        d_ref[...], mo_ref[...], vo_ref[...] = _adamw_math(w_ref[...], g_ref[...], m_ref[...], v_ref[...])

    spec = pl.BlockSpec((tr, c), lambda i: (i, 0))
    return pl.pallas_call(
        body, name=name, grid=(r // tr,), in_specs=[spec] * 4, out_specs=[spec] * 3,
        out_shape=[jax.ShapeDtypeStruct((r, c), F32)] * 3, compiler_params=_params(1))(w, g, m, v)


def _sum_parts(first, parts, name, out_bf16=False):
    shape = first.shape
    rows = int(np.prod(shape[:-1]))
    c = shape[-1]
    n = parts.shape[0]
    tr = 128

    def body(a_ref, p_ref, *outs):
        acc = a_ref[...]
        for k in range(n):
            acc = acc + p_ref[k].astype(F32)
        outs[0][...] = acc
        if out_bf16:
            outs[1][...] = acc.astype(BF16)

    spec = pl.BlockSpec((tr, c), lambda i: (i, 0))
    out_shape = [jax.ShapeDtypeStruct((rows, c), F32)] + ([jax.ShapeDtypeStruct((rows, c), BF16)] if out_bf16 else [])
    res = pl.pallas_call(
        body, name=name, grid=(rows // tr,),
        in_specs=[spec, pl.BlockSpec((n, tr, c), lambda i: (0, i, 0))],
        out_specs=[spec] * len(out_shape), out_shape=out_shape,
        compiler_params=_params(1))(first.reshape(rows, c), parts.reshape(n, rows, c))
    return [t.reshape(shape) for t in res]


def _sum_stack(parts, name):
    n = parts.shape[0]

    def body(p_ref, o_ref):
        acc = p_ref[0]
        for k in range(1, n):
            acc = acc + p_ref[k]
        o_ref[...] = acc

    return pl.pallas_call(body, name=name, out_shape=jax.ShapeDtypeStruct(parts.shape[1:], F32))(parts)


def _coords():
    return lax.axis_index("x"), lax.axis_index("y"), lax.axis_index("c")


def _chip(who):
    return 2 * who[0] + who[1]


def _flip(who, mask):
    return tuple((1 - v) if b else v for v, b in zip(who, mask))


def _exchange(name, ins, outs, transfers, copies=()):
    ni, no = len(ins), len(outs)
    nt = len(transfers)

    def body(*refs):
        I, O = refs[:ni], refs[ni:ni + no]
        ssem, rsem, lsem = refs[ni + no:]
        me = _coords()
        local = [pltpu.make_async_copy(s(I, O, me), d(I, O, me), lsem.at[n]) for n, (s, d) in enumerate(copies)]
        for cp in local:
            cp.start()
        sends, recvs, arrived = [], [], set()
        for t, tr in enumerate(transfers):
            peer = _flip(me, tr["mask"])

            def make(who, t=t, tr=tr, peer=peer):
                return pltpu.make_async_remote_copy(
                    src_ref=tr["src"](I, O, me), dst_ref=tr["dst"](I, O, who),
                    send_sem=ssem.at[t], recv_sem=rsem.at[t], device_id=peer, device_id_type=MESH)

            after = tr.get("after")
            if after is not None and after not in arrived:
                recvs[after].wait_recv()
                arrived.add(after)
            snd = make(me)
            snd.start()
            sends.append(snd)
            recvs.append(make(peer))
        for t in range(nt):
            if t not in arrived:
                recvs[t].wait_recv()
        for snd in sends:
            snd.wait_send()
        for cp in local:
            cp.wait()

    hbm = pl.BlockSpec(memory_space=pltpu.HBM)
    return pl.pallas_call(
        body, name=name, in_specs=[hbm] * ni, out_specs=[hbm] * no,
        out_shape=[jax.ShapeDtypeStruct(s, d) for s, d in outs],
        scratch_shapes=[pltpu.SemaphoreType.DMA((nt,)), pltpu.SemaphoreType.DMA((nt,)),
                        pltpu.SemaphoreType.DMA((max(len(copies), 1),))],
        compiler_params=pltpu.CompilerParams(has_side_effects=True))(*ins)


CHIP_MASKS = [(0, 1, 0), (1, 0, 0), (1, 1, 0)]
SIBLING = (0, 0, 1)


def _gather_shards(shards):
    def half(ref, who):
        r = ref.shape[-2] // 2
        return pl.ds(pl.multiple_of(who[2] * r, r), r)

    over_ici, onward, copies = [], [], []
    for a in range(len(shards)):
        copies.append((lambda I, O, me, a=a: I[a], lambda I, O, me, a=a: O[a].at[_chip(me)]))
        for mask in CHIP_MASKS:
            over_ici.append(dict(
                mask=mask,
                src=lambda I, O, me, a=a: I[a].at[half(I[a], me)],
                dst=lambda I, O, who, a=a: O[a].at[_chip(who), half(I[a], who)]))
            onward.append(dict(
                mask=SIBLING, after=len(over_ici) - 1,
                src=lambda I, O, me, a=a, mask=mask: O[a].at[_chip(_flip(me, mask)), half(I[a], me)],
                dst=lambda I, O, who, a=a, mask=mask: O[a].at[_chip(_flip(who, mask)), half(I[a], who)]))
    transfers = over_ici + onward
    outs = [((NCHIP,) + s.shape, s.dtype) for s in shards]
    return _exchange("gather_weights", shards, outs, transfers, copies)


def _to_sibling(arrs, name):
    transfers = [dict(mask=SIBLING, src=lambda I, O, me, a=a: I[a], dst=lambda I, O, who, a=a: O[a])
                 for a in range(len(arrs))]
    return _exchange(name, arrs, [(t.shape, t.dtype) for t in arrs], transfers)


def _scatter_chip_sums(tb):
    transfers = []
    for a in range(len(tb)):
        for n, mask in enumerate(CHIP_MASKS):
            transfers.append(dict(
                mask=mask,
                src=lambda I, O, me, a=a, mask=mask: I[a].at[_chip(_flip(me, mask))],
                dst=lambda I, O, who, a=a, n=n: O[a].at[n]))
    outs = [((3,) + t.shape[1:], t.dtype) for t in tb]
    return _exchange("scatter_grads", tb, outs, transfers)


def _join_halves(halves):
    def rows(ref, who):
        r = ref.shape[0]
        return pl.ds(pl.multiple_of(who[2] * r, r), r)

    transfers = [dict(mask=SIBLING, src=lambda I, O, me, a=a: I[a],
                      dst=lambda I, O, who, a=a: O[a].at[rows(I[a], who)]) for a in range(len(halves))]
    copies = [(lambda I, O, me, a=a: I[a], lambda I, O, me, a=a: O[a].at[rows(I[a], me)])
              for a in range(len(halves))]
    outs = [((2 * h.shape[0], h.shape[1]), h.dtype) for h in halves]
    return _exchange("join_halves", halves, outs, transfers, copies)


def _gather_small(vec):
    def slot(who):
        return 4 * who[0] + 2 * who[1] + who[2]

    masks = [(m >> 2 & 1, m >> 1 & 1, m & 1) for m in range(1, 8)]
    transfers = [dict(mask=mask, src=lambda I, O, me: I[0], dst=lambda I, O, who: O[0].at[slot(who)])
                 for mask in masks]
    copies = [(lambda I, O, me: I[0], lambda I, O, me: O[0].at[slot(me)])]
    return _exchange("gather_small", [vec], [((8,) + vec.shape, vec.dtype)], transfers, copies)[0]


def _rope_tables(positions):
    half = ROT // 2
    inv_freq = jnp.power(jnp.float32(THETA), -jnp.arange(0, ROT, 2, dtype=F32) / ROT)
    ang = positions.astype(F32)[:, None] * inv_freq[None, :]
    cos, sin = jnp.cos(ang), jnp.sin(ang)
    one, zero, z8 = jnp.ones((S, HD - ROT), F32), jnp.zeros((S, HD - ROT), F32), jnp.zeros((S, half), F32)
    c = jnp.concatenate([cos, cos, one], axis=1)
    a = jnp.concatenate([-sin, z8, zero], axis=1)
    b = jnp.concatenate([z8, sin, zero], axis=1)
    return tuple(jnp.tile(t, (1, 2)) for t in (c, a, b))


def _tile_heads(g, w):
    return jnp.tile(g.reshape(1, HD), (1, w // HD))


def _fold_heads(dg):
    return dg.reshape(-1, HD).sum(axis=0)


def _pad_lanes(a):
    return jnp.pad(a, ((0, 0), (0, LANES - a.shape[1])))


def _local_step(x, target, positions, wt):
    rope = _rope_tables(positions)
    w_in_a = wt["w_in_a"]
    w_qkvg = jnp.concatenate([w_in_a[:, :3 * D], w_in_a[:, 3 * D + NH:]], axis=1)
    w_f = _pad_lanes(w_in_a[:, 3 * D:3 * D + NH])
    b_pad = _pad_lanes(wt["b_forget"].reshape(1, NH))
    qg_a, kg_a = _tile_heads(wt["qnorm_a_g"], D), _tile_heads(wt["knorm_a_g"], D)
    qg_b, kg_b = _tile_heads(wt["qnorm_b_g"], D), _tile_heads(wt["knorm_b_g"], KVW)
    norm_a, kv_g, norm_b = wt["norm_a_g"].reshape(1, D), wt["kv_norm_g"].reshape(1, D), wt["norm_b_g"].reshape(1, D)
    sinks_t = jnp.repeat(wt["sinks"].reshape(1, NH), HD, axis=1)

    (u_a,) = _rmsnorm_fwd(x, [norm_a], "norm_a")
    qkvg = _matmul(u_a, w_qkvg, "nn", F32, "proj_a")
    fpad = _matmul(u_a, w_f, "nn", F32, "proj_f")
    q_a, k_a, v_a = _a_post(qkvg, qg_a, kg_a)
    ct = _forget_cumsum(fpad, b_pad)
    ct2 = ct[:NH].reshape(NH // 2, 2, S)
    o_a, lse_a = _fox_fwd(q_a, k_a, v_a, ct2)
    y_a = _gate_fwd(o_a, qkvg, 3, "gate_a")
    h1 = _matmul(y_a, wt["w_out_a"], "nn", F32, "out_a", add=x)
    u_kv, u_b = _rmsnorm_fwd(h1, [kv_g, norm_b], "norm_b")
    kv = _matmul(u_kv, wt["w_kv"], "nn", F32, "proj_kv")
    pb = _matmul(u_b, wt["w_in_b"], "nn", F32, "proj_b")
    q_b, kdup, vdup = _b_post(pb, kv, qg_b, kg_b, rope)
    o_b, lse_b = _swa_fwd(q_b, kdup, vdup, sinks_t)
    y_b = _gate_fwd(o_b, pb, 1, "gate_b")
    out = _matmul(y_b, wt["w_out_b"], "nn", F32, "out_b", add=h1)
    d_out, d_out_b, sq = _loss_head(out, target)

    g = {}
    g["w_out_b"] = _matmul(y_b, d_out_b, "tn", F32, "dw_out_b")
    d_y_b = _matmul(d_out_b, wt["w_out_b"], "nt", F32, "dy_b")
    d_o_b, d_gate_b = _gate_bwd(d_y_b, o_b, pb, 1, "gate_b_bwd")
    dq_b, dkdup, dvdup, dsk = _swa_bwd(q_b, kdup, vdup, sinks_t, o_b, lse_b, d_o_b)
    g["sinks"] = dsk[0, ::HD]
    d_qb_raw, dg = _headnorm_bwd(pb, 0, qg_b, dq_b, rope, "qnorm_b_bwd")
    g["qnorm_b_g"] = _fold_heads(dg)
    d_pb = jnp.concatenate([d_qb_raw, d_gate_b], axis=1)
    g["w_in_b"] = _matmul(u_b, d_pb, "tn", F32, "dw_in_b")
    d_u_b = _matmul(d_pb, wt["w_in_b"], "nt", F32, "du_b")
    d_kv, dg = _kv_bwd(dkdup, dvdup, kv, kg_b, rope)
    g["knorm_b_g"] = _fold_heads(dg)
    g["w_kv"] = _matmul(u_kv, d_kv, "tn", F32, "dw_kv")
    d_u_kv = _matmul(d_kv, wt["w_kv"], "nt", F32, "du_kv")
    d_h1, d_h1_b, g["kv_norm_g"], g["norm_b_g"] = _rmsnorm_bwd(h1, [kv_g, norm_b], [d_u_kv, d_u_b], d_out, "norm_b_bwd")
    g["w_out_a"] = _matmul(y_a, d_h1_b, "tn", F32, "dw_out_a")
    d_y_a = _matmul(d_h1_b, wt["w_out_a"], "nt", F32, "dy_a")
    d_o_a, d_gate_a = _gate_bwd(d_y_a, o_a, qkvg, 3, "gate_a_bwd")
    dq_a, dk_a, dv_a, dct = _fox_bwd(q_a, k_a, v_a, ct2, o_a, lse_a, d_o_a)
    dct_pad = jnp.pad(dct.reshape(NH, S), ((0, LANES - NH), (0, 0)))
    d_f, db = _forget_bwd(dct_pad, fpad, b_pad)
    g["b_forget"] = db[0, :NH]
    d_q_raw, dg = _headnorm_bwd(qkvg, 0, qg_a, dq_a, None, "qnorm_a_bwd")
    g["qnorm_a_g"] = _fold_heads(dg)
    d_k_raw, dg = _headnorm_bwd(qkvg, 1, kg_a, dk_a, None, "knorm_a_bwd")
    g["knorm_a_g"] = _fold_heads(dg)
    d_proj = jnp.concatenate([d_q_raw, d_k_raw, dv_a.astype(BF16), d_gate_a], axis=1)
    dw_qkvg = _matmul(u_a, d_proj, "tn", F32, "dw_proj_a")
    dw_f = _matmul(u_a, d_f, "tn", F32, "dw_proj_f")
    g["w_in_a"] = jnp.concatenate([dw_qkvg[:, :3 * D], dw_f[:, :NH], dw_qkvg[:, 3 * D:]], axis=1)
    d_u_f = _matmul(d_f, w_f, "nt", F32, "du_f")
    d_u_a = _matmul(d_proj, w_qkvg, "nt", F32, "du_a", add=d_u_f)
    d_x, _, g["norm_a_g"] = _rmsnorm_bwd(x, [norm_a], [d_u_a], d_h1, "norm_a_bwd")
    return sq, d_x, g


BIG = ["w_in_a", "w_out_a", "w_kv", "w_in_b", "w_out_b"]
COL_SHARDED = {"w_in_a", "w_in_b"}
SMALL = ["norm_a_g", "b_forget", "qnorm_a_g", "knorm_a_g", "kv_norm_g", "knorm_b_g", "norm_b_g", "qnorm_b_g", "sinks"]
NAMES = ["norm_a_g", "w_in_a", "b_forget", "qnorm_a_g", "knorm_a_g", "w_out_a", "kv_norm_g", "w_kv", "knorm_b_g",
         "norm_b_g", "w_in_b", "qnorm_b_g", "sinks", "w_out_b"]


def _pack(vals):
    flat = []
    for v in vals:
        v = v.reshape(-1)
        flat.append(jnp.pad(v, (0, -v.shape[0] % LANES)))
    flat = jnp.concatenate(flat)
    flat = jnp.pad(flat, (0, -flat.shape[0] % (8 * LANES)))
    return flat.reshape(-1, LANES)


def _unpack(packed, shapes):
    flat, out, off = packed.reshape(-1), [], 0
    for s in shapes:
        n = int(np.prod(s))
        out.append(flat[off:off + n].reshape(s))
        off += n + (-n % LANES)
    return out


def kernel(x, positions, norm_a_g, w_in_a, b_forget, qnorm_a_g, knorm_a_g, w_out_a, kv_norm_g, w_kv, knorm_b_g, norm_b_g, w_in_b, qnorm_b_g, sinks, w_out_b, loss_target, m_norm_a_g, m_w_in_a, m_b_forget, m_qnorm_a_g, m_knorm_a_g, m_w_out_a, m_kv_norm_g, m_w_kv, m_knorm_b_g, m_norm_b_g, m_w_in_b, m_qnorm_b_g, m_sinks, m_w_out_b, v_norm_a_g, v_w_in_a, v_b_forget, v_qnorm_a_g, v_knorm_a_g, v_w_out_a, v_kv_norm_g, v_w_kv, v_knorm_b_g, v_norm_b_g, v_w_in_b, v_qnorm_b_g, v_sinks, v_w_out_b):
    w = dict(norm_a_g=norm_a_g, w_in_a=w_in_a, b_forget=b_forget, qnorm_a_g=qnorm_a_g, knorm_a_g=knorm_a_g,
             w_out_a=w_out_a, kv_norm_g=kv_norm_g, w_kv=w_kv, knorm_b_g=knorm_b_g, norm_b_g=norm_b_g,
             w_in_b=w_in_b, qnorm_b_g=qnorm_b_g, sinks=sinks, w_out_b=w_out_b)
    m = dict(norm_a_g=m_norm_a_g, w_in_a=m_w_in_a, b_forget=m_b_forget, qnorm_a_g=m_qnorm_a_g, knorm_a_g=m_knorm_a_g,
             w_out_a=m_w_out_a, kv_norm_g=m_kv_norm_g, w_kv=m_w_kv, knorm_b_g=m_knorm_b_g, norm_b_g=m_norm_b_g,
             w_in_b=m_w_in_b, qnorm_b_g=m_qnorm_b_g, sinks=m_sinks, w_out_b=m_w_out_b)
    v = dict(norm_a_g=v_norm_a_g, w_in_a=v_w_in_a, b_forget=v_b_forget, qnorm_a_g=v_qnorm_a_g, knorm_a_g=v_knorm_a_g,
             w_out_a=v_w_out_a, kv_norm_g=v_kv_norm_g, w_kv=v_w_kv, knorm_b_g=v_knorm_b_g, norm_b_g=v_norm_b_g,
             w_in_b=v_w_in_b, qnorm_b_g=v_qnorm_b_g, sinks=v_sinks, w_out_b=v_w_out_b)
    my_chip = 2 * lax.axis_index("x") + lax.axis_index("y")
    my_core = lax.axis_index("c")
    shard2d = {n: w[n].reshape(w[n].shape[-2:]) for n in BIG}

    norm_a_rows = jnp.broadcast_to(norm_a_g.reshape(1, D // NCHIP), (16, D // NCHIP))
    gathered = _gather_shards([shard2d[n].astype(BF16) for n in BIG] + [norm_a_rows])
    wt = {}
    for n, t in zip(BIG, gathered[:-1]):
        wt[n] = jnp.swapaxes(t, 0, 1).reshape(t.shape[1], -1) if n in COL_SHARDED else t.reshape(-1, t.shape[2])
    wt["norm_a_g"] = gathered[-1][:, 0, :].reshape(1, D)
    for n in SMALL[1:]:
        wt[n] = w[n]

    sq, d_x, g = _local_step(x[0], loss_target[0], positions, wt)

    small_shapes = [(D,), (NH,), (HD,), (HD,), (D,), (HD,), (D,), (HD,), (NH,), (D,)]
    packed = _pack([g[n] for n in SMALL] + [sq])
    total = _sum_stack(_gather_small(packed), "sum_small")
    small_g = dict(zip(SMALL, _unpack(total, small_shapes)[:-1]))
    loss = 0.5 * jnp.sum(_unpack(total, small_shapes)[-1]) / D
    small_g["norm_a_g"] = lax.dynamic_slice(small_g["norm_a_g"], (my_chip * (D // NCHIP),), (D // NCHIP,))

    mine, theirs = [], []
    for n in BIG:
        full = g[n]
        if n in COL_SHARDED:
            blocks = jnp.swapaxes(full.reshape(full.shape[0], NCHIP, -1), 0, 1)
        else:
            blocks = full.reshape(NCHIP, -1, full.shape[1])
        r = blocks.shape[1] // 2
        mine.append(lax.dynamic_slice_in_dim(blocks, my_core * r, r, axis=1))
        theirs.append(lax.dynamic_slice_in_dim(blocks, (1 - my_core) * r, r, axis=1).astype(BF16))
    from_sibling = _to_sibling(theirs, "sibling_halves")
    chip_f32, chip_bf16 = [], []
    for n, a, b in zip(BIG, mine, from_sibling):
        t32, t16 = _sum_parts(a, b[None], "chip_sum_" + n, out_bf16=True)
        chip_f32.append(t32)
        chip_bf16.append(t16)
    arrived = _scatter_chip_sums(chip_bf16)
    halves = []
    for n, t32, parts in zip(BIG, chip_f32, arrived):
        own = lax.dynamic_index_in_dim(t32, my_chip, axis=0, keepdims=False)
        halves.append(_sum_parts(own, parts, "mesh_sum_" + n)[0])
    big_g = dict(zip(BIG, _join_halves(halves)))

    res = {}
    for n in BIG:
        shape = w[n].shape
        gr = big_g[n]
        d_, m_, v_ = _adamw(shard2d[n], gr, m[n].reshape(gr.shape), v[n].reshape(gr.shape), "adamw_" + n)
        res[n] = tuple(t.reshape(shape) for t in (gr, d_, m_, v_))
    sm_g = _pack([small_g[n] for n in SMALL])
    sm = [_pack([d[n] for n in SMALL]) for d in (w, m, v)]
    sm_out = _adamw(sm[0], sm_g, sm[1], sm[2], "adamw_small")
    sm_shapes = [w[n].shape for n in SMALL]
    unpacked = [_unpack(t, sm_shapes) for t in (sm_g,) + tuple(sm_out)]
    for i, n in enumerate(SMALL):
        res[n] = tuple(u[i] for u in unpacked)

    outs = [loss, d_x[None]]
    for k in range(4):
        outs += [res[n][k] for n in NAMES]
    return tuple(outs)
```

```python
import numpy as np
import jax
import jax.numpy as jnp
from jax import lax
from jax.experimental import pallas as pl
from jax.experimental.pallas import tpu as pltpu

F32, BF16 = jnp.float32, jnp.bfloat16
S, D, HD, NH, NKV = 2048, 1024, 64, 16, 4
KVW = NKV * HD
WINDOW = 128
ROT = HD // 4
THETA = 500000.0
EPS = 1e-6
SCALE = HD ** -0.5
LANES = 128
NEG = -1e30
VMEM_LIMIT = 48 * 2 ** 20
ROWS = 256
ATT = 256
SWQ = 4
NCHIP = 4
ADAM_LR, ADAM_B1, ADAM_B2, ADAM_EPS, ADAM_WD, ADAM_STEP = 0.001, 0.9, 0.999, 1e-08, 0.01, 10
NT = (((1,), (1,)), ((), ()))
TN = (((0,), (0,)), ((), ()))
MESH = pl.DeviceIdType.MESH


def _params(n):
    return pltpu.CompilerParams(dimension_semantics=("arbitrary",) * n, vmem_limit_bytes=VMEM_LIMIT)


def _dot(a, b, dims=None):
    if dims is None:
        return jnp.dot(a, b, preferred_element_type=F32)
    return lax.dot_general(a, b, dims, preferred_element_type=F32)


def _dot_split(a, b, n):
    out, rest = None, a
    for _ in range(n):
        hi = rest.astype(BF16)
        term = _dot(hi, b)
        out = term if out is None else out + term
        rest = rest - hi.astype(F32)
    return out


def _seg_mats(w):
    e = (np.arange(w)[:, None] // HD == np.arange(LANES)[None, :]).astype(np.float32)
    return jnp.asarray(e, BF16), jnp.asarray(e.T, BF16)


def _head_rstd(x, e, et):
    ss = _dot_split(x * x, e, 2)
    return _dot_split(lax.rsqrt(ss * (1.0 / HD) + EPS), et, 3)


def _rope(x, c, a, b):
    w = x.shape[1]
    return x * c + pltpu.roll(x, w - ROT // 2, 1) * a + pltpu.roll(x, ROT // 2, 1) * b


def _rope_t(dy, c, a, b):
    w = dy.shape[1]
    return dy * c + pltpu.roll(dy * b, w - ROT // 2, 1) + pltpu.roll(dy * a, ROT // 2, 1)


def _sigmoid(x):
    return 1.0 / (1.0 + jnp.exp(-x))


def _row_spec(shape, ts):
    nd = len(shape)
    if shape[0] == S:
        return pl.BlockSpec((ts,) + tuple(shape[1:]), lambda i: (i,) + (0,) * (nd - 1))
    return pl.BlockSpec(tuple(shape), lambda i: (0,) * nd)


def _rows_call(body, name, ins, outs, ts=ROWS):
    return pl.pallas_call(
        body, name=name, grid=(S // ts,),
        in_specs=[_row_spec(a.shape, ts) for a in ins],
        out_specs=[_row_spec(s, ts) for s, _ in outs],
        out_shape=[jax.ShapeDtypeStruct(s, d) for s, d in outs],
        compiler_params=_params(1))(*ins)


def _col_spec(ts, w, col):
    return pl.BlockSpec((ts, w), lambda i: (i, col))


def _matmul(a, b, mode, out_dtype, name, add=None):
    if mode == "nn":
        (m, k), n = a.shape, b.shape[1]
    elif mode == "nt":
        (m, k), n = a.shape, b.shape[0]
    else:
        (k, m), n = a.shape, b.shape[1]
    tm, tn = min(512, m), min(512, n)
    a_spec = pl.BlockSpec((k, tm), lambda j, i: (0, i)) if mode == "tn" else pl.BlockSpec((tm, k), lambda j, i: (i, 0))
    b_spec = pl.BlockSpec((tn, k), lambda j, i: (j, 0)) if mode == "nt" else pl.BlockSpec((k, tn), lambda j, i: (0, j))
    o_spec = pl.BlockSpec((tm, tn), lambda j, i: (i, j))
    dims = {"nn": None, "nt": NT, "tn": TN}[mode]

    def body(*refs):
        acc = _dot(refs[0][...], refs[1][...], dims)
        if add is not None:
            acc = acc + refs[2][...]
        refs[-1][...] = acc.astype(out_dtype)

    ins = [a, b] + ([add] if add is not None else [])
    return pl.pallas_call(
        body, name=name, grid=(n // tn, m // tm),
        in_specs=[a_spec, b_spec] + ([o_spec] if add is not None else []),
        out_specs=o_spec, out_shape=jax.ShapeDtypeStruct((m, n), out_dtype),
        compiler_params=_params(2))(*ins)


def _rmsnorm_fwd(x, gains, name):
    def body(*refs):
        xv = refs[0][...]
        r = lax.rsqrt(jnp.mean(xv * xv, axis=-1, keepdims=True) + EPS)
        xh = xv * r
        for n in range(len(gains)):
            refs[1 + len(gains) + n][...] = (xh * refs[1 + n][...]).astype(BF16)

    return _rows_call(body, name, [x] + list(gains), [((S, D), BF16)] * len(gains))


def _rmsnorm_bwd(x, gains, dus, dres, name):
    n = len(gains)

    def body(*refs):
        x_ref, g_refs, du_refs, dres_ref = refs[0], refs[1:1 + n], refs[1 + n:1 + 2 * n], refs[1 + 2 * n]
        dx_ref, dxb_ref, dg_refs = refs[2 + 2 * n], refs[3 + 2 * n], refs[4 + 2 * n:]
        xv = x_ref[...]
        r = lax.rsqrt(jnp.mean(xv * xv, axis=-1, keepdims=True) + EPS)
        xh = xv * r
        gy = None
        for m in range(n):
            du = du_refs[m][...]
            part = jnp.sum(du * xh, axis=0, keepdims=True)

            @pl.when(pl.program_id(0) == 0)
            def _(m=m, part=part):
                dg_refs[m][...] = part

            @pl.when(pl.program_id(0) != 0)
            def _(m=m, part=part):
                dg_refs[m][...] += part

            t = du * g_refs[m][...]
            gy = t if gy is None else gy + t
        dx = dres_ref[...] + r * (gy - xh * jnp.mean(gy * xh, axis=-1, keepdims=True))
        dx_ref[...] = dx
        dxb_ref[...] = dx.astype(BF16)

    outs = [((S, D), F32), ((S, D), BF16)] + [((1, D), F32)] * n
    return _rows_call(body, name, [x] + list(gains) + list(dus) + [dres], outs)


def _a_post(qkvg, qg, kg):
    e, et = _seg_mats(D)

    def body(q_ref, k_ref, v_ref, qg_ref, kg_ref, e_ref, et_ref, qo, ko, vo):
        ev, etv = e_ref[...], et_ref[...]
        qv, kv = q_ref[...], k_ref[...]
        qo[...] = (qv * _head_rstd(qv, ev, etv) * qg_ref[...] * SCALE).astype(BF16)
        ko[...] = (kv * _head_rstd(kv, ev, etv) * kg_ref[...]).astype(BF16)
        vo[...] = v_ref[...].astype(BF16)

    whole = lambda a: pl.BlockSpec(a.shape, lambda i: (0, 0))
    return pl.pallas_call(
        body, name="a_post", grid=(S // ROWS,),
        in_specs=[_col_spec(ROWS, D, 0), _col_spec(ROWS, D, 1), _col_spec(ROWS, D, 2),
                  whole(qg), whole(kg), whole(e), whole(et)],
        out_specs=[_col_spec(ROWS, D, 0)] * 3,
        out_shape=[jax.ShapeDtypeStruct((S, D), BF16)] * 3,
        compiler_params=_params(1))(qkvg, qkvg, qkvg, qg, kg, e, et)


def _tri(upper):
    r, c = np.arange(ROWS)[:, None], np.arange(ROWS)[None, :]
    return jnp.asarray((r <= c) if upper else (r >= c), BF16)


def _forget_cumsum(fpad, bpad):
    def body(f_ref, b_ref, u_ref, c_ref, carry):
        @pl.when(pl.program_id(0) == 0)
        def _():
            carry[...] = jnp.zeros_like(carry)

        lf = jax.nn.log_sigmoid(f_ref[...] + b_ref[...])
        blk = _dot_split(lf.T, u_ref[...], 3) + carry[:, 0:1]
        c_ref[...] = blk
        carry[...] = jnp.broadcast_to(blk[:, ROWS - 1:ROWS], carry.shape)

    return pl.pallas_call(
        body, name="forget_cumsum", grid=(S // ROWS,),
        in_specs=[pl.BlockSpec((ROWS, LANES), lambda i: (i, 0)), pl.BlockSpec((1, LANES), lambda i: (0, 0)),
                  pl.BlockSpec((ROWS, ROWS), lambda i: (0, 0))],
        out_specs=pl.BlockSpec((LANES, ROWS), lambda i: (0, i)),
        out_shape=jax.ShapeDtypeStruct((LANES, S), F32),
        scratch_shapes=[pltpu.VMEM((LANES, LANES), F32)],
        compiler_params=_params(1))(fpad, bpad, _tri(True))


def _forget_bwd(dct, fpad, bpad):
    nb = S // ROWS

    def body(dc_ref, f_ref, b_ref, l_ref, df_ref, db_ref, carry):
        @pl.when(pl.program_id(0) == 0)
        def _():
            carry[...] = jnp.zeros_like(carry)
            db_ref[...] = jnp.zeros_like(db_ref)

        blk = _dot_split(dc_ref[...], l_ref[...], 3) + carry[:, 0:1]
        carry[...] = jnp.broadcast_to(blk[:, 0:1], carry.shape)
        df = blk.T * _sigmoid(-(f_ref[...] + b_ref[...]))
        df_ref[...] = df.astype(BF16)
        db_ref[...] += jnp.sum(df, axis=0, keepdims=True)

    return pl.pallas_call(
        body, name="forget_bwd", grid=(nb,),
        in_specs=[pl.BlockSpec((LANES, ROWS), lambda i: (0, nb - 1 - i)),
                  pl.BlockSpec((ROWS, LANES), lambda i: (nb - 1 - i, 0)),
                  pl.BlockSpec((1, LANES), lambda i: (0, 0)), pl.BlockSpec((ROWS, ROWS), lambda i: (0, 0))],
        out_specs=[pl.BlockSpec((ROWS, LANES), lambda i: (nb - 1 - i, 0)), pl.BlockSpec((1, LANES), lambda i: (0, 0))],
        out_shape=[jax.ShapeDtypeStruct((S, LANES), BF16), jax.ShapeDtypeStruct((1, LANES), F32)],
        scratch_shapes=[pltpu.VMEM((LANES, LANES), F32)],
        compiler_params=_params(1))(dct, fpad, bpad, _tri(False))


def _gate_fwd(o, proj, col, name):
    def body(o_ref, g_ref, y_ref):
        g = g_ref[...]
        y_ref[...] = (o_ref[...] * (g * _sigmoid(g))).astype(BF16)

    return pl.pallas_call(
        body, name=name, grid=(S // ROWS,),
        in_specs=[_col_spec(ROWS, D, 0), _col_spec(ROWS, D, col)],
        out_specs=_col_spec(ROWS, D, 0), out_shape=jax.ShapeDtypeStruct((S, D), BF16),
        compiler_params=_params(1))(o, proj)


def _gate_bwd(dy, o, proj, col, name):
    def body(dy_ref, o_ref, g_ref, do_ref, dg_ref):
        g, dyv = g_ref[...], dy_ref[...]
        sg = _sigmoid(g)
        do_ref[...] = dyv * (g * sg)
        dg_ref[...] = (dyv * o_ref[...] * (sg * (1.0 + g * (1.0 - sg)))).astype(BF16)

    return pl.pallas_call(
        body, name=name, grid=(S // ROWS,),
        in_specs=[_col_spec(ROWS, D, 0), _col_spec(ROWS, D, 0), _col_spec(ROWS, D, col)],
        out_specs=[_col_spec(ROWS, D, 0)] * 2,
        out_shape=[jax.ShapeDtypeStruct((S, D), F32), jax.ShapeDtypeStruct((S, D), BF16)],
        compiler_params=_params(1))(dy, o, proj)


def _headnorm_bwd(x, col, gain, dy, rope, name):
    e, et = _seg_mats(D)
    tabs = list(rope) if rope is not None else []

    def body(*refs):
        x_ref, g_ref, dy_ref, e_ref, et_ref = refs[:5]
        dx_ref, dg_ref = refs[-2:]
        xv, dyv, ev, etv = x_ref[...], dy_ref[...], e_ref[...], et_ref[...]
        if rope is not None:
            c, a, b = (jnp.tile(t[...], (1, D // LANES)) for t in refs[5:8])
            dyv = _rope_t(dyv, c, a, b)
        r = _head_rstd(xv, ev, etv)
        xh = xv * r
        part = jnp.sum(dyv * xh, axis=0, keepdims=True)

        @pl.when(pl.program_id(0) == 0)
        def _():
            dg_ref[...] = part

        @pl.when(pl.program_id(0) != 0)
        def _():
            dg_ref[...] += part

        gy = dyv * g_ref[...]
        seg = _dot_split(_dot_split(gy * xh, ev, 2) * (1.0 / HD), etv, 3)
        dx_ref[...] = (r * (gy - xh * seg)).astype(BF16)

    whole = lambda a: pl.BlockSpec(a.shape, lambda i: (0, 0))
    return pl.pallas_call(
        body, name=name, grid=(S // ROWS,),
        in_specs=[_col_spec(ROWS, D, col), whole(gain), _col_spec(ROWS, D, 0), whole(e), whole(et)]
                 + [pl.BlockSpec((ROWS, LANES), lambda i: (i, 0))] * len(tabs),
        out_specs=[_col_spec(ROWS, D, 0), whole(gain)],
        out_shape=[jax.ShapeDtypeStruct((S, D), BF16), jax.ShapeDtypeStruct((1, D), F32)],
        compiler_params=_params(1))(x, gain, dy, e, et, *tabs)


def _dup_mat():
    r, c = np.arange(KVW)[:, None], np.arange(2 * KVW)[None, :]
    return (r // HD == c // LANES) & (r % HD == c % HD)


def _fold_mat():
    r, c = np.arange(D)[:, None], np.arange(KVW)[None, :]
    return (r // (2 * LANES) == c // HD) & (r % HD == c % HD)


def _b_post(pb, kv, qg, kg, rope):
    e, et = _seg_mats(D)
    ek, etk = _seg_mats(KVW)
    dup = jnp.asarray(_dup_mat(), BF16)

    def body(q_ref, k_ref, v_ref, qg_ref, kg_ref, e_ref, et_ref, ek_ref, etk_ref, dup_ref, c_ref, a_ref, b_ref,
             qo, ko, vo):
        c1, a1, b1 = c_ref[...], a_ref[...], b_ref[...]
        qv = q_ref[...]
        qn = qv * _head_rstd(qv, e_ref[...], et_ref[...]) * qg_ref[...]
        t = lambda z, n: jnp.tile(z, (1, n))
        qo[...] = (_rope(qn, t(c1, D // LANES), t(a1, D // LANES), t(b1, D // LANES)) * SCALE).astype(BF16)
        kvv = k_ref[...]
        kn = kvv * _head_rstd(kvv, ek_ref[...], etk_ref[...]) * kg_ref[...]
        kr = _rope(kn, t(c1, KVW // LANES), t(a1, KVW // LANES), t(b1, KVW // LANES)).astype(BF16)
        ko[...] = _dot(kr, dup_ref[...]).astype(BF16)
        vo[...] = _dot(v_ref[...].astype(BF16), dup_ref[...]).astype(BF16)

    whole = lambda a: pl.BlockSpec(a.shape, lambda i: (0, 0))
    tab = pl.BlockSpec((ROWS, LANES), lambda i: (i, 0))
    return pl.pallas_call(
        body, name="b_post", grid=(S // ROWS,),
        in_specs=[_col_spec(ROWS, D, 0), _col_spec(ROWS, KVW, 0), _col_spec(ROWS, KVW, 1),
                  whole(qg), whole(kg), whole(e), whole(et), whole(ek), whole(etk), whole(dup), tab, tab, tab],
        out_specs=[_col_spec(ROWS, D, 0), _col_spec(ROWS, 2 * KVW, 0), _col_spec(ROWS, 2 * KVW, 0)],
        out_shape=[jax.ShapeDtypeStruct((S, D), BF16), jax.ShapeDtypeStruct((S, 2 * KVW), BF16),
                   jax.ShapeDtypeStruct((S, 2 * KVW), BF16)],
        compiler_params=_params(1))(pb, kv, kv, qg, kg, e, et, ek, etk, dup, *rope)


def _kv_bwd(dkdup, dvdup, kv, kg, rope):
    ek, etk = _seg_mats(KVW)
    fold = jnp.asarray(_fold_mat(), BF16)

    def body(dk_ref, dv_ref, k_ref, kg_ref, ek_ref, etk_ref, fold_ref, c_ref, a_ref, b_ref, dkv_ref, dg_ref):
        ev, etv, fv = ek_ref[...], etk_ref[...], fold_ref[...]
        t = lambda z: jnp.tile(z[...], (1, KVW // LANES))
        dk = _rope_t(_dot_split(dk_ref[...], fv, 3), t(c_ref), t(a_ref), t(b_ref))
        dv = _dot_split(dv_ref[...], fv, 3)
        xv = k_ref[...]
        r = _head_rstd(xv, ev, etv)
        xh = xv * r
        part = jnp.sum(dk * xh, axis=0, keepdims=True)

        @pl.when(pl.program_id(0) == 0)
        def _():
            dg_ref[...] = part

        @pl.when(pl.program_id(0) != 0)
        def _():
            dg_ref[...] += part

        gy = dk * kg_ref[...]
        seg = _dot_split(_dot_split(gy * xh, ev, 2) * (1.0 / HD), etv, 3)
        dkv_ref[:, 0:KVW] = (r * (gy - xh * seg)).astype(BF16)
        dkv_ref[:, KVW:2 * KVW] = dv.astype(BF16)

    whole = lambda a: pl.BlockSpec(a.shape, lambda i: (0, 0))
    tab = pl.BlockSpec((ROWS, LANES), lambda i: (i, 0))
    return pl.pallas_call(
        body, name="kv_bwd", grid=(S // ROWS,),
        in_specs=[_col_spec(ROWS, D, 0), _col_spec(ROWS, D, 0), _col_spec(ROWS, KVW, 0),
                  whole(kg), whole(ek), whole(etk), whole(fold), tab, tab, tab],
        out_specs=[_col_spec(ROWS, 2 * KVW, 0), whole(kg)],
        out_shape=[jax.ShapeDtypeStruct((S, 2 * KVW), BF16), jax.ShapeDtypeStruct((1, KVW), F32)],
        compiler_params=_params(1))(dkdup, dvdup, kv, kg, ek, etk, fold, *rope)


def _loss_head(out, target):
    def body(o_ref, t_ref, d_ref, db_ref, l_ref):
        diff = o_ref[...] - t_ref[...]
        d = diff * (1.0 / D)
        d_ref[...] = d
        db_ref[...] = d.astype(BF16)

        @pl.when(pl.program_id(0) == 0)
        def _():
            l_ref[...] = jnp.zeros_like(l_ref)

        l_ref[...] += jnp.sum(diff * diff, axis=0, keepdims=True)

    return _rows_call(body, "loss_head", [out, target], [((S, D), F32), ((S, D), BF16), ((1, D), F32)])


def _lane():
    return lax.broadcasted_iota(jnp.int32, (1, LANES), 1)


def _head_mask(hh):
    return (_lane() < HD) if hh == 0 else (_lane() >= HD)


def _fox_fwd(q, k, v, ct):
    nq = S // ATT

    def body(q_ref, k_ref, v_ref, c_ref, o_ref, lse_ref):
        i = pl.program_id(1)
        row = lax.broadcasted_iota(jnp.int32, (ATT, ATT), 0)
        col = lax.broadcasted_iota(jnp.int32, (ATT, ATT), 1)
        q2 = q_ref[...]
        qms = [jnp.where(_head_mask(hh), q2, jnp.zeros_like(q2)) for hh in (0, 1)]

        def step(j, carry, diag):
            off = pl.multiple_of(j * ATT, ATT)
            kj, vj = k_ref[pl.ds(off, ATT), :], v_ref[pl.ds(off, ATT), :]
            out = []
            for hh in (0, 1):
                m, l, acc = carry[hh]
                s = _dot(qms[hh], kj, NT) - c_ref[hh:hh + 1, pl.ds(off, ATT)]
                if diag:
                    s = jnp.where(col <= row, s, NEG)
                m_new = jnp.maximum(m, jnp.max(s, axis=1, keepdims=True))
                p = jnp.exp(s - m_new)
                alpha = jnp.exp(m - m_new)
                l = alpha * l + jnp.sum(p, axis=1, keepdims=True)
                p_hi = p.astype(BF16)
                p_lo = (p - p_hi.astype(F32)).astype(BF16)
                acc = alpha * acc + (_dot(p_hi, vj) + _dot(p_lo, vj))
                out.append((m_new, l, acc))
            return tuple(out)

        one = (jnp.full((ATT, 1), NEG, F32), jnp.zeros((ATT, 1), F32), jnp.zeros((ATT, LANES), F32))
        carry = lax.fori_loop(0, i, lambda j, cr: step(j, cr, False), (one, one))
        res = [(acc / l, m + jnp.log(l)) for m, l, acc in step(i, carry, True)]
        first = _head_mask(0)
        o_ref[...] = jnp.where(first, res[0][0], res[1][0])
        lse_ref[...] = jnp.where(first, res[0][1], res[1][1])

    blk = pl.BlockSpec((ATT, LANES), lambda p, i: (i, p))
    full = pl.BlockSpec((S, LANES), lambda p, i: (0, p))
    return pl.pallas_call(
        body, name="fox_fwd", grid=(NH // 2, nq),
        in_specs=[blk, full, full, pl.BlockSpec((None, 2, S), lambda p, i: (p, 0, 0))],
        out_specs=[blk, blk],
        out_shape=[jax.ShapeDtypeStruct((S, D), F32)] * 2,
        compiler_params=_params(2))(q, k, v, ct)


def _fox_bwd(q, k, v, ct, o, lse, do):
    nq = S // ATT

    def body(q_ref, k_ref, v_ref, c_ref, o_ref, lse_ref, do_ref, dq_ref, dk_ref, dv_ref, dc_ref):
        i = pl.program_id(1)

        @pl.when(i == 0)
        def _():
            dk_ref[...] = jnp.zeros_like(dk_ref)
            dv_ref[...] = jnp.zeros_like(dv_ref)
            dc_ref[...] = jnp.zeros_like(dc_ref)

        row = lax.broadcasted_iota(jnp.int32, (ATT, ATT), 0)
        col = lax.broadcasted_iota(jnp.int32, (ATT, ATT), 1)
        q2, do2, lse2 = q_ref[...], do_ref[...], lse_ref[...]
        do2b = do2.astype(BF16)
        prod = do2b.astype(F32) * o_ref[...]
        heads = []
        for hh in (0, 1):
            hm = _head_mask(hh)
            heads.append((jnp.where(hm, q2, jnp.zeros_like(q2)), jnp.where(hm, do2b, jnp.zeros_like(do2b)),
                          jnp.sum(jnp.where(hm, prod, 0.0), axis=1, keepdims=True),
                          jnp.max(jnp.where(hm, lse2, NEG), axis=1, keepdims=True)))

        def step(j, dqs, diag):
            off = pl.multiple_of(j * ATT, ATT)
            kj, vj = k_ref[pl.ds(off, ATT), :], v_ref[pl.ds(off, ATT), :]
            dk, dv, out = None, None, []
            for hh in (0, 1):
                qm, dom, delta, lse_h = heads[hh]
                s = _dot(qm, kj, NT) - c_ref[hh:hh + 1, pl.ds(off, ATT)]
                p = jnp.exp(s - lse_h)
                if diag:
                    p = jnp.where(col <= row, p, 0.0)
                ds = p * (_dot(dom, vj, NT) - delta)
                dc_ref[hh:hh + 1, pl.ds(off, ATT)] += -jnp.sum(ds, axis=0, keepdims=True)
                dsb = ds.astype(BF16)
                dk_h, dv_h = _dot(dsb, qm, TN), _dot(p.astype(BF16), dom, TN)
                dk, dv = (dk_h, dv_h) if dk is None else (dk + dk_h, dv + dv_h)
                out.append(dqs[hh] + _dot(dsb, kj))
            dk_ref[pl.ds(off, ATT), :] += dk
            dv_ref[pl.ds(off, ATT), :] += dv
            return tuple(out)

        zero = jnp.zeros((ATT, LANES), F32)
        dqs = lax.fori_loop(0, i, lambda j, acc: step(j, acc, False), (zero, zero))
        dqs = step(i, dqs, True)
        dq_ref[...] = jnp.where(_head_mask(0), dqs[0], dqs[1]) * SCALE

    blk = pl.BlockSpec((ATT, LANES), lambda p, i: (i, p))
    full = pl.BlockSpec((S, LANES), lambda p, i: (0, p))
    cspec = pl.BlockSpec((None, 2, S), lambda p, i: (p, 0, 0))
    return pl.pallas_call(
        body, name="fox_bwd", grid=(NH // 2, nq),
        in_specs=[blk, full, full, cspec, blk, blk, blk],
        out_specs=[blk, full, full, cspec],
        out_shape=[jax.ShapeDtypeStruct((S, D), F32)] * 3 + [jax.ShapeDtypeStruct((NH // 2, 2, S), F32)],
        compiler_params=_params(2))(q, k, v, ct, o, lse, do)


def _swa_logits(qm, kk, i, start):
    s = _dot(qm, kk, NT)
    qabs = i * WINDOW + lax.broadcasted_iota(jnp.int32, (WINDOW, 2 * WINDOW), 0)
    kabs = start + lax.broadcasted_iota(jnp.int32, (WINDOW, 2 * WINDOW), 1)
    valid = (kabs <= qabs) & (qabs - kabs < WINDOW)
    return s, valid


def _swa_fwd(q, kdup, vdup, sinks_t):
    def body(q_ref, k_ref, v_ref, sk_ref, o_ref, lse_ref):
        skv = sk_ref[...]
        first = _head_mask(0)
        for sb in range(SWQ):
            i = pl.program_id(1) * SWQ + sb
            rows = slice(sb * WINDOW, (sb + 1) * WINDOW)
            start = pl.multiple_of(jnp.maximum(i - 1, 0) * WINDOW, WINDOW)
            kk, vv = k_ref[pl.ds(start, 2 * WINDOW), :], v_ref[pl.ds(start, 2 * WINDOW), :]
            q2 = q_ref[rows, :]
            res = []
            for hh in (0, 1):
                hm = _head_mask(hh)
                qm = jnp.where(hm, q2, jnp.zeros_like(q2))
                sink = jnp.max(jnp.where(hm, skv, NEG), axis=1, keepdims=True)
                s, valid = _swa_logits(qm, kk, i, start)
                s = jnp.where(valid, s, NEG)
                m = jnp.maximum(jnp.max(s, axis=1, keepdims=True), sink)
                p = jnp.exp(s - m)
                l = jnp.sum(p, axis=1, keepdims=True) + jnp.exp(sink - m)
                res.append((_dot(p.astype(BF16), vv) / l, m + jnp.log(l)))
            o_ref[rows, :] = jnp.where(first, res[0][0], res[1][0])
            lse_ref[rows, :] = jnp.where(first, res[0][1], res[1][1])

    blk = pl.BlockSpec((SWQ * WINDOW, LANES), lambda p, i: (i, p))
    full = pl.BlockSpec((S, LANES), lambda p, i: (0, p // 2))
    return pl.pallas_call(
        body, name="swa_fwd", grid=(NH // 2, S // (SWQ * WINDOW)),
        in_specs=[blk, full, full, pl.BlockSpec((1, LANES), lambda p, i: (0, p))],
        out_specs=[blk, blk],
        out_shape=[jax.ShapeDtypeStruct((S, D), F32)] * 2,
        compiler_params=_params(2))(q, kdup, vdup, sinks_t)


def _swa_bwd(q, kdup, vdup, sinks_t, o, lse, do):
    def body(q_ref, k_ref, v_ref, sk_ref, o_ref, lse_ref, do_ref, dq_ref, dk_ref, dv_ref, dsk_ref):
        @pl.when(pl.program_id(1) == 0)
        def _():
            dk_ref[...] = jnp.zeros_like(dk_ref)
            dv_ref[...] = jnp.zeros_like(dv_ref)
            dsk_ref[...] = jnp.zeros_like(dsk_ref)

        skv = sk_ref[...]
        first = _head_mask(0)
        for sb in range(SWQ):
            i = pl.program_id(1) * SWQ + sb
            rows = slice(sb * WINDOW, (sb + 1) * WINDOW)
            start = pl.multiple_of(jnp.maximum(i - 1, 0) * WINDOW, WINDOW)
            kk, vv = k_ref[pl.ds(start, 2 * WINDOW), :], v_ref[pl.ds(start, 2 * WINDOW), :]
            q2, do2, lse2 = q_ref[rows, :], do_ref[rows, :], lse_ref[rows, :]
            do2b = do2.astype(BF16)
            prod = do2b.astype(F32) * o_ref[rows, :]
            dqs, dsk, dk, dv = [], [], None, None
            for hh in (0, 1):
                hm = _head_mask(hh)
                qm = jnp.where(hm, q2, jnp.zeros_like(q2))
                dom = jnp.where(hm, do2b, jnp.zeros_like(do2b))
                delta = jnp.sum(jnp.where(hm, prod, 0.0), axis=1, keepdims=True)
                lse_h = jnp.max(jnp.where(hm, lse2, NEG), axis=1, keepdims=True)
                sink = jnp.max(jnp.where(hm, skv, NEG), axis=1, keepdims=True)
                s, valid = _swa_logits(qm, kk, i, start)
                p = jnp.where(valid, jnp.exp(s - lse_h), 0.0)
                ds = p * (_dot(dom, vv, NT) - delta)
                dsb = ds.astype(BF16)
                dk_h, dv_h = _dot(dsb, qm, TN), _dot(p.astype(BF16), dom, TN)
                dk, dv = (dk_h, dv_h) if dk is None else (dk + dk_h, dv + dv_h)
                dqs.append(_dot(dsb, kk))
                dsk.append(-jnp.sum(jnp.exp(sink - lse_h) * delta, axis=0, keepdims=True))
            dk_ref[pl.ds(start, 2 * WINDOW), :] += dk
            dv_ref[pl.ds(start, 2 * WINDOW), :] += dv
            dq_ref[rows, :] = jnp.where(first, dqs[0], dqs[1]) * SCALE
            dsk_ref[...] += jnp.where(first, dsk[0], dsk[1])

    blk = pl.BlockSpec((SWQ * WINDOW, LANES), lambda p, i: (i, p))
    full = pl.BlockSpec((S, LANES), lambda p, i: (0, p // 2))
    acc = pl.BlockSpec((S, LANES), lambda p, i: (0, p))
    sk = pl.BlockSpec((1, LANES), lambda p, i: (0, p))
    return pl.pallas_call(
        body, name="swa_bwd", grid=(NH // 2, S // (SWQ * WINDOW)),
        in_specs=[blk, full, full, sk, blk, blk, blk],
        out_specs=[blk, acc, acc, sk],
        out_shape=[jax.ShapeDtypeStruct((S, D), F32)] * 3 + [jax.ShapeDtypeStruct((1, D), F32)],
        compiler_params=_params(2))(q, kdup, vdup, sinks_t, o, lse, do)


def _adamw_math(w, g, m, v):
    m = ADAM_B1 * m + (1.0 - ADAM_B1) * g
    v = ADAM_B2 * v + (1.0 - ADAM_B2) * jnp.square(g)
    m_hat = m / (1.0 - ADAM_B1 ** ADAM_STEP)
    v_hat = v / (1.0 - ADAM_B2 ** ADAM_STEP)
    delta = -ADAM_LR * (m_hat / (jnp.sqrt(v_hat) + ADAM_EPS) + ADAM_WD * w)
    return delta, m, v


def _adamw(w, g, m, v, name):
    r, c = w.shape
    tr = min(r, 128)

    def body(w_ref, g_ref, m_ref, v_ref, d_ref, mo_ref, vo_ref):
        d_ref[...], mo_ref[...], vo_ref[...] = _adamw_math(w_ref[...], g_ref[...], m_ref[...], v_ref[...])

    spec = pl.BlockSpec((tr, c), lambda i: (i, 0))
    return pl.pallas_call(
        body, name=name, grid=(r // tr,), in_specs=[spec] * 4, out_specs=[spec] * 3,
        out_shape=[jax.ShapeDtypeStruct((r, c), F32)] * 3, compiler_params=_params(1))(w, g, m, v)


def _adamw_halves(w, g_mine, g_theirs, m, v, name):
    rows, c = w.shape
    tr = 128
    per_half = rows // 2 // tr

    def body(w_ref, a_ref, b_ref, m_ref, v_ref, g_ref, d_ref, mo_ref, vo_ref):
        is_mine = pl.program_id(0) // per_half == lax.axis_index("c")
        g = jnp.where(is_mine, a_ref[...], b_ref[...])
        g_ref[...] = g
        d_ref[...], mo_ref[...], vo_ref[...] = _adamw_math(w_ref[...], g, m_ref[...], v_ref[...])

    spec = pl.BlockSpec((tr, c), lambda i: (i, 0))
    half = pl.BlockSpec((tr, c), lambda i: (i % per_half, 0))
    return pl.pallas_call(
        body, name=name, grid=(rows // tr,), in_specs=[spec, half, half, spec, spec], out_specs=[spec] * 4,
        out_shape=[jax.ShapeDtypeStruct((rows, c), F32)] * 4, compiler_params=_params(1))(w, g_mine, g_theirs, m, v)


def _sum_parts(first, parts, name, out_bf16=False):
    shape = first.shape
    rows = int(np.prod(shape[:-1]))
    c = shape[-1]
    n = parts.shape[0]
    tr = 128

    def body(a_ref, p_ref, *outs):
        acc = a_ref[...]
        for k in range(n):
            acc = acc + p_ref[k].astype(F32)
        outs[0][...] = acc
        if out_bf16:
            outs[1][...] = acc.astype(BF16)

    spec = pl.BlockSpec((tr, c), lambda i: (i, 0))
    out_shape = [jax.ShapeDtypeStruct((rows, c), F32)] + ([jax.ShapeDtypeStruct((rows, c), BF16)] if out_bf16 else [])
    res = pl.pallas_call(
        body, name=name, grid=(rows // tr,),
        in_specs=[spec, pl.BlockSpec((n, tr, c), lambda i: (0, i, 0))],
        out_specs=[spec] * len(out_shape), out_shape=out_shape,
        compiler_params=_params(1))(first.reshape(rows, c), parts.reshape(n, rows, c))
    return [t.reshape(shape) for t in res]


def _sum_stack(parts, name):
    n = parts.shape[0]

    def body(p_ref, o_ref):
        acc = p_ref[0]
        for k in range(1, n):
            acc = acc + p_ref[k]
        o_ref[...] = acc

    return pl.pallas_call(body, name=name, out_shape=jax.ShapeDtypeStruct(parts.shape[1:], F32))(parts)


def _coords():
    return lax.axis_index("x"), lax.axis_index("y"), lax.axis_index("c")


def _chip(who):
    return 2 * who[0] + who[1]


def _flip(who, mask):
    return tuple((1 - v) if b else v for v, b in zip(who, mask))


def _exchange(name, ins, outs, transfers, copies=()):
    ni, no = len(ins), len(outs)
    nt = len(transfers)

    def body(*refs):
        I, O = refs[:ni], refs[ni:ni + no]
        ssem, rsem, lsem = refs[ni + no:]
        me = _coords()
        local = [pltpu.make_async_copy(s(I, O, me), d(I, O, me), lsem.at[n]) for n, (s, d) in enumerate(copies)]
        for cp in local:
            cp.start()
        sends, recvs, arrived = [], [], set()
        for t, tr in enumerate(transfers):
            peer = _flip(me, tr["mask"])

            def make(who, t=t, tr=tr, peer=peer):
                return pltpu.make_async_remote_copy(
                    src_ref=tr["src"](I, O, me), dst_ref=tr["dst"](I, O, who),
                    send_sem=ssem.at[t], recv_sem=rsem.at[t], device_id=peer, device_id_type=MESH)

            after = tr.get("after")
            if after is not None and after not in arrived:
                recvs[after].wait_recv()
                arrived.add(after)
            snd = make(me)
            snd.start()
            sends.append(snd)
            recvs.append(make(peer))
        for t in range(nt):
            if t not in arrived:
                recvs[t].wait_recv()
        for snd in sends:
            snd.wait_send()
        for cp in local:
            cp.wait()

    hbm = pl.BlockSpec(memory_space=pltpu.HBM)
    return pl.pallas_call(
        body, name=name, in_specs=[hbm] * ni, out_specs=[hbm] * no,
        out_shape=[jax.ShapeDtypeStruct(s, d) for s, d in outs],
        scratch_shapes=[pltpu.SemaphoreType.DMA((nt,)), pltpu.SemaphoreType.DMA((nt,)),
                        pltpu.SemaphoreType.DMA((max(len(copies), 1),))],
        compiler_params=pltpu.CompilerParams(has_side_effects=True))(*ins)


CHIP_MASKS = [(0, 1, 0), (1, 0, 0), (1, 1, 0)]
SIBLING = (0, 0, 1)


def _gather_shards(shards):
    def half(ref, who):
        r = ref.shape[-2] // 2
        return pl.ds(pl.multiple_of(who[2] * r, r), r)

    over_ici, onward = [], []
    for a in range(len(shards)):
        for mask in CHIP_MASKS:
            over_ici.append(dict(
                mask=mask,
                src=lambda I, O, me, a=a: I[a].at[half(I[a], me)],
                dst=lambda I, O, who, a=a: O[a].at[_chip(who), half(I[a], who)]))
            onward.append(dict(
                mask=SIBLING, after=len(over_ici) - 1,
                src=lambda I, O, me, a=a, mask=mask: O[a].at[_chip(_flip(me, mask)), half(I[a], me)],
                dst=lambda I, O, who, a=a, mask=mask: O[a].at[_chip(_flip(who, mask)), half(I[a], who)]))
    transfers = over_ici + onward
    outs = [((NCHIP,) + s.shape, s.dtype) for s in shards]
    gathered = _exchange("gather_weights", shards, outs, transfers)
    mine = lax.broadcasted_iota(jnp.int32, (NCHIP, 1, 1), 0) == _chip(_coords())
    return [jnp.where(mine, s[None], t) for s, t in zip(shards, gathered)]


def _to_sibling(arrs, name):
    transfers = [dict(mask=SIBLING, src=lambda I, O, me, a=a: I[a], dst=lambda I, O, who, a=a: O[a])
                 for a in range(len(arrs))]
    return _exchange(name, arrs, [(t.shape, t.dtype) for t in arrs], transfers)


def _scatter_chip_sums(tb):
    transfers = []
    for a in range(len(tb)):
        for n, mask in enumerate(CHIP_MASKS):
            transfers.append(dict(
                mask=mask,
                src=lambda I, O, me, a=a, mask=mask: I[a].at[_chip(_flip(me, mask))],
                dst=lambda I, O, who, a=a, n=n: O[a].at[n]))
    outs = [((3,) + t.shape[1:], t.dtype) for t in tb]
    return _exchange("scatter_grads", tb, outs, transfers)


def _gather_small(vec):
    def slot(who):
        return 4 * who[0] + 2 * who[1] + who[2]

    masks = [(m >> 2 & 1, m >> 1 & 1, m & 1) for m in range(1, 8)]
    transfers = [dict(mask=mask, src=lambda I, O, me: I[0], dst=lambda I, O, who: O[0].at[slot(who)])
                 for mask in masks]
    copies = [(lambda I, O, me: I[0], lambda I, O, me: O[0].at[slot(me)])]
    return _exchange("gather_small", [vec], [((8,) + vec.shape, vec.dtype)], transfers, copies)[0]


def _rope_tables(positions):
    half = ROT // 2
    inv_freq = jnp.power(jnp.float32(THETA), -jnp.arange(0, ROT, 2, dtype=F32) / ROT)
    ang = positions.astype(F32)[:, None] * inv_freq[None, :]
    cos, sin = jnp.cos(ang), jnp.sin(ang)
    one, zero, z8 = jnp.ones((S, HD - ROT), F32), jnp.zeros((S, HD - ROT), F32), jnp.zeros((S, half), F32)
    c = jnp.concatenate([cos, cos, one], axis=1)
    a = jnp.concatenate([-sin, z8, zero], axis=1)
    b = jnp.concatenate([z8, sin, zero], axis=1)
    return tuple(jnp.tile(t, (1, 2)) for t in (c, a, b))


def _tile_heads(g, w):
    return jnp.tile(g.reshape(1, HD), (1, w // HD))


def _fold_heads(dg):
    return dg.reshape(-1, HD).sum(axis=0)


def _pad_lanes(a):
    return jnp.pad(a, ((0, 0), (0, LANES - a.shape[1])))


def _local_step(x, target, positions, wt):
    rope = _rope_tables(positions)
    w_in_a = wt["w_in_a"]
    w_qkvg = jnp.concatenate([w_in_a[:, :3 * D], w_in_a[:, 3 * D + NH:]], axis=1)
    w_f = _pad_lanes(w_in_a[:, 3 * D:3 * D + NH])
    b_pad = _pad_lanes(wt["b_forget"].reshape(1, NH))
    qg_a, kg_a = _tile_heads(wt["qnorm_a_g"], D), _tile_heads(wt["knorm_a_g"], D)
    qg_b, kg_b = _tile_heads(wt["qnorm_b_g"], D), _tile_heads(wt["knorm_b_g"], KVW)
    norm_a, kv_g, norm_b = wt["norm_a_g"].reshape(1, D), wt["kv_norm_g"].reshape(1, D), wt["norm_b_g"].reshape(1, D)
    sinks_t = jnp.repeat(wt["sinks"].reshape(1, NH), HD, axis=1)

    (u_a,) = _rmsnorm_fwd(x, [norm_a], "norm_a")
    qkvg = _matmul(u_a, w_qkvg, "nn", F32, "proj_a")
    fpad = _matmul(u_a, w_f, "nn", F32, "proj_f")
    q_a, k_a, v_a = _a_post(qkvg, qg_a, kg_a)
    ct = _forget_cumsum(fpad, b_pad)
    ct2 = ct[:NH].reshape(NH // 2, 2, S)
    o_a, lse_a = _fox_fwd(q_a, k_a, v_a, ct2)
    y_a = _gate_fwd(o_a, qkvg, 3, "gate_a")
    h1 = _matmul(y_a, wt["w_out_a"], "nn", F32, "out_a", add=x)
    u_kv, u_b = _rmsnorm_fwd(h1, [kv_g, norm_b], "norm_b")
    kv = _matmul(u_kv, wt["w_kv"], "nn", F32, "proj_kv")
    pb = _matmul(u_b, wt["w_in_b"], "nn", F32, "proj_b")
    q_b, kdup, vdup = _b_post(pb, kv, qg_b, kg_b, rope)
    o_b, lse_b = _swa_fwd(q_b, kdup, vdup, sinks_t)
    y_b = _gate_fwd(o_b, pb, 1, "gate_b")
    out = _matmul(y_b, wt["w_out_b"], "nn", F32, "out_b", add=h1)
    d_out, d_out_b, sq = _loss_head(out, target)

    g = {}
    g["w_out_b"] = _matmul(y_b, d_out_b, "tn", F32, "dw_out_b")
    d_y_b = _matmul(d_out_b, wt["w_out_b"], "nt", F32, "dy_b")
    d_o_b, d_gate_b = _gate_bwd(d_y_b, o_b, pb, 1, "gate_b_bwd")
    dq_b, dkdup, dvdup, dsk = _swa_bwd(q_b, kdup, vdup, sinks_t, o_b, lse_b, d_o_b)
    g["sinks"] = dsk[0, ::HD]
    d_qb_raw, dg = _headnorm_bwd(pb, 0, qg_b, dq_b, rope, "qnorm_b_bwd")
    g["qnorm_b_g"] = _fold_heads(dg)
    d_pb = jnp.concatenate([d_qb_raw, d_gate_b], axis=1)
    g["w_in_b"] = _matmul(u_b, d_pb, "tn", F32, "dw_in_b")
    d_u_b = _matmul(d_pb, wt["w_in_b"], "nt", F32, "du_b")
    d_kv, dg = _kv_bwd(dkdup, dvdup, kv, kg_b, rope)
    g["knorm_b_g"] = _fold_heads(dg)
    g["w_kv"] = _matmul(u_kv, d_kv, "tn", F32, "dw_kv")
    d_u_kv = _matmul(d_kv, wt["w_kv"], "nt", F32, "du_kv")
    d_h1, d_h1_b, g["kv_norm_g"], g["norm_b_g"] = _rmsnorm_bwd(h1, [kv_g, norm_b], [d_u_kv, d_u_b], d_out, "norm_b_bwd")
    g["w_out_a"] = _matmul(y_a, d_h1_b, "tn", F32, "dw_out_a")
    d_y_a = _matmul(d_h1_b, wt["w_out_a"], "nt", F32, "dy_a")
    d_o_a, d_gate_a = _gate_bwd(d_y_a, o_a, qkvg, 3, "gate_a_bwd")
    dq_a, dk_a, dv_a, dct = _fox_bwd(q_a, k_a, v_a, ct2, o_a, lse_a, d_o_a)
    dct_pad = jnp.pad(dct.reshape(NH, S), ((0, LANES - NH), (0, 0)))
    d_f, db = _forget_bwd(dct_pad, fpad, b_pad)
    g["b_forget"] = db[0, :NH]
    d_q_raw, dg = _headnorm_bwd(qkvg, 0, qg_a, dq_a, None, "qnorm_a_bwd")
    g["qnorm_a_g"] = _fold_heads(dg)
    d_k_raw, dg = _headnorm_bwd(qkvg, 1, kg_a, dk_a, None, "knorm_a_bwd")
    g["knorm_a_g"] = _fold_heads(dg)
    d_proj = jnp.concatenate([d_q_raw, d_k_raw, dv_a.astype(BF16), d_gate_a], axis=1)
    dw_qkvg = _matmul(u_a, d_proj, "tn", F32, "dw_proj_a")
    dw_f = _matmul(u_a, d_f, "tn", F32, "dw_proj_f")
    g["w_in_a"] = jnp.concatenate([dw_qkvg[:, :3 * D], dw_f[:, :NH], dw_qkvg[:, 3 * D:]], axis=1)
    d_u_f = _matmul(d_f, w_f, "nt", F32, "du_f")
    d_u_a = _matmul(d_proj, w_qkvg, "nt", F32, "du_a", add=d_u_f)
    d_x, _, g["norm_a_g"] = _rmsnorm_bwd(x, [norm_a], [d_u_a], d_h1, "norm_a_bwd")
    return sq, d_x, g


BIG = ["w_in_a", "w_out_a", "w_kv", "w_in_b", "w_out_b"]
COL_SHARDED = {"w_in_a", "w_in_b"}
SMALL = ["norm_a_g", "b_forget", "qnorm_a_g", "knorm_a_g", "kv_norm_g", "knorm_b_g", "norm_b_g", "qnorm_b_g", "sinks"]
NAMES = ["norm_a_g", "w_in_a", "b_forget", "qnorm_a_g", "knorm_a_g", "w_out_a", "kv_norm_g", "w_kv", "knorm_b_g",
         "norm_b_g", "w_in_b", "qnorm_b_g", "sinks", "w_out_b"]


def _pack(vals):
    flat = []
    for v in vals:
        v = v.reshape(-1)
        flat.append(jnp.pad(v, (0, -v.shape[0] % LANES)))
    flat = jnp.concatenate(flat)
    flat = jnp.pad(flat, (0, -flat.shape[0] % (8 * LANES)))
    return flat.reshape(-1, LANES)


def _unpack(packed, shapes):
    flat, out, off = packed.reshape(-1), [], 0
    for s in shapes:
        n = int(np.prod(s))
        out.append(flat[off:off + n].reshape(s))
        off += n + (-n % LANES)
    return out


def kernel(x, positions, norm_a_g, w_in_a, b_forget, qnorm_a_g, knorm_a_g, w_out_a, kv_norm_g, w_kv, knorm_b_g, norm_b_g, w_in_b, qnorm_b_g, sinks, w_out_b, loss_target, m_norm_a_g, m_w_in_a, m_b_forget, m_qnorm_a_g, m_knorm_a_g, m_w_out_a, m_kv_norm_g, m_w_kv, m_knorm_b_g, m_norm_b_g, m_w_in_b, m_qnorm_b_g, m_sinks, m_w_out_b, v_norm_a_g, v_w_in_a, v_b_forget, v_qnorm_a_g, v_knorm_a_g, v_w_out_a, v_kv_norm_g, v_w_kv, v_knorm_b_g, v_norm_b_g, v_w_in_b, v_qnorm_b_g, v_sinks, v_w_out_b):
    w = dict(norm_a_g=norm_a_g, w_in_a=w_in_a, b_forget=b_forget, qnorm_a_g=qnorm_a_g, knorm_a_g=knorm_a_g,
             w_out_a=w_out_a, kv_norm_g=kv_norm_g, w_kv=w_kv, knorm_b_g=knorm_b_g, norm_b_g=norm_b_g,
             w_in_b=w_in_b, qnorm_b_g=qnorm_b_g, sinks=sinks, w_out_b=w_out_b)
    m = dict(norm_a_g=m_norm_a_g, w_in_a=m_w_in_a, b_forget=m_b_forget, qnorm_a_g=m_qnorm_a_g, knorm_a_g=m_knorm_a_g,
             w_out_a=m_w_out_a, kv_norm_g=m_kv_norm_g, w_kv=m_w_kv, knorm_b_g=m_knorm_b_g, norm_b_g=m_norm_b_g,
             w_in_b=m_w_in_b, qnorm_b_g=m_qnorm_b_g, sinks=m_sinks, w_out_b=m_w_out_b)
    v = dict(norm_a_g=v_norm_a_g, w_in_a=v_w_in_a, b_forget=v_b_forget, qnorm_a_g=v_qnorm_a_g, knorm_a_g=v_knorm_a_g,
             w_out_a=v_w_out_a, kv_norm_g=v_kv_norm_g, w_kv=v_w_kv, knorm_b_g=v_knorm_b_g, norm_b_g=v_norm_b_g,
             w_in_b=v_w_in_b, qnorm_b_g=v_qnorm_b_g, sinks=v_sinks, w_out_b=v_w_out_b)
    my_chip = 2 * lax.axis_index("x") + lax.axis_index("y")
    my_core = lax.axis_index("c")
    shard2d = {n: w[n].reshape(w[n].shape[-2:]) for n in BIG}

    norm_a_rows = jnp.broadcast_to(norm_a_g.reshape(1, D // NCHIP), (16, D // NCHIP))
    gathered = _gather_shards([shard2d[n].astype(BF16) for n in BIG] + [norm_a_rows])
    wt = {}
    for n, t in zip(BIG, gathered[:-1]):
        wt[n] = jnp.swapaxes(t, 0, 1).reshape(t.shape[1], -1) if n in COL_SHARDED else t.reshape(-1, t.shape[2])
    wt["norm_a_g"] = gathered[-1][:, 0, :].reshape(1, D)
    for n in SMALL[1:]:
        wt[n] = w[n]

    sq, d_x, g = _local_step(x[0], loss_target[0], positions, wt)

    small_shapes = [(D,), (NH,), (HD,), (HD,), (D,), (HD,), (D,), (HD,), (NH,), (D,)]
    packed = _pack([g[n] for n in SMALL] + [sq])
    total = _sum_stack(_gather_small(packed), "sum_small")
    small_g = dict(zip(SMALL, _unpack(total, small_shapes)[:-1]))
    loss = 0.5 * jnp.sum(_unpack(total, small_shapes)[-1]) / D
    small_g["norm_a_g"] = lax.dynamic_slice(small_g["norm_a_g"], (my_chip * (D // NCHIP),), (D // NCHIP,))

    mine, theirs = [], []
    for n in BIG:
        full = g[n]
        if n in COL_SHARDED:
            blocks = jnp.swapaxes(full.reshape(full.shape[0], NCHIP, -1), 0, 1)
        else:
            blocks = full.reshape(NCHIP, -1, full.shape[1])
        r = blocks.shape[1] // 2
        mine.append(lax.dynamic_slice_in_dim(blocks, my_core * r, r, axis=1))
        theirs.append(lax.dynamic_slice_in_dim(blocks, (1 - my_core) * r, r, axis=1).astype(BF16))
    from_sibling = _to_sibling(theirs, "sibling_halves")
    chip_f32, chip_bf16 = [], []
    for n, a, b in zip(BIG, mine, from_sibling):
        t32, t16 = _sum_parts(a, b[None], "chip_sum_" + n, out_bf16=True)
        chip_f32.append(t32)
        chip_bf16.append(t16)
    arrived = _scatter_chip_sums(chip_bf16)
    halves = []
    for n, t32, parts in zip(BIG, chip_f32, arrived):
        own = lax.dynamic_index_in_dim(t32, my_chip, axis=0, keepdims=False)
        halves.append(_sum_parts(own, parts, "mesh_sum_" + n)[0])
    sibling_done = _to_sibling(halves, "finished_halves")

    res = {}
    for n, mine_half, their_half in zip(BIG, halves, sibling_done):
        shape = w[n].shape
        s2 = shard2d[n].shape
        out4 = _adamw_halves(shard2d[n], mine_half, their_half, m[n].reshape(s2), v[n].reshape(s2), "adamw_" + n)
        res[n] = tuple(t.reshape(shape) for t in out4)
    sm_g = _pack([small_g[n] for n in SMALL])
    sm = [_pack([d[n] for n in SMALL]) for d in (w, m, v)]
    sm_out = _adamw(sm[0], sm_g, sm[1], sm[2], "adamw_small")
    sm_shapes = [w[n].shape for n in SMALL]
    unpacked = [_unpack(t, sm_shapes) for t in (sm_g,) + tuple(sm_out)]
    for i, n in enumerate(SMALL):
        res[n] = tuple(u[i] for u in unpacked)

    outs = [loss, d_x[None]]
    for k in range(4):
        outs += [res[n][k] for n in NAMES]
    return tuple(outs)
```

```python
import numpy as np
import jax
import jax.numpy as jnp
from jax import lax
from jax.experimental import pallas as pl
from jax.experimental.pallas import tpu as pltpu

F32, BF16 = jnp.float32, jnp.bfloat16
S, D, HD, NH, NKV = 2048, 1024, 64, 16, 4
KVW = NKV * HD
WINDOW = 128
ROT = HD // 4
THETA = 500000.0
EPS = 1e-6
SCALE = HD ** -0.5
LANES = 128
NEG = -1e30
VMEM_LIMIT = 48 * 2 ** 20
ROWS = 256
ATT = 256
SWQ = 4
NCHIP = 4
ADAM_LR, ADAM_B1, ADAM_B2, ADAM_EPS, ADAM_WD, ADAM_STEP = 0.001, 0.9, 0.999, 1e-08, 0.01, 10
NT = (((1,), (1,)), ((), ()))
TN = (((0,), (0,)), ((), ()))
MESH = pl.DeviceIdType.MESH


def _params(n):
    return pltpu.CompilerParams(dimension_semantics=("arbitrary",) * n, vmem_limit_bytes=VMEM_LIMIT)


def _dot(a, b, dims=None):
    if dims is None:
        return jnp.dot(a, b, preferred_element_type=F32)
    return lax.dot_general(a, b, dims, preferred_element_type=F32)


def _dot_split(a, b, n):
    out, rest = None, a
    for _ in range(n):
        hi = rest.astype(BF16)
        term = _dot(hi, b)
        out = term if out is None else out + term
        rest = rest - hi.astype(F32)
    return out


def _seg_mats(w):
    e = (np.arange(w)[:, None] // HD == np.arange(LANES)[None, :]).astype(np.float32)
    return jnp.asarray(e, BF16), jnp.asarray(e.T, BF16)


def _head_rstd(x, e, et):
    ss = _dot_split(x * x, e, 2)
    return _dot_split(lax.rsqrt(ss * (1.0 / HD) + EPS), et, 3)


def _rope(x, c, a, b):
    w = x.shape[1]
    return x * c + pltpu.roll(x, w - ROT // 2, 1) * a + pltpu.roll(x, ROT // 2, 1) * b


def _rope_t(dy, c, a, b):
    w = dy.shape[1]
    return dy * c + pltpu.roll(dy * b, w - ROT // 2, 1) + pltpu.roll(dy * a, ROT // 2, 1)


def _sigmoid(x):
    return 1.0 / (1.0 + jnp.exp(-x))


def _row_spec(shape, ts):
    nd = len(shape)
    if shape[0] == S:
        return pl.BlockSpec((ts,) + tuple(shape[1:]), lambda i: (i,) + (0,) * (nd - 1))
    return pl.BlockSpec(tuple(shape), lambda i: (0,) * nd)


def _rows_call(body, name, ins, outs, ts=ROWS):
    return pl.pallas_call(
        body, name=name, grid=(S // ts,),
        in_specs=[_row_spec(a.shape, ts) for a in ins],
        out_specs=[_row_spec(s, ts) for s, _ in outs],
        out_shape=[jax.ShapeDtypeStruct(s, d) for s, d in outs],
        compiler_params=_params(1))(*ins)


def _col_spec(ts, w, col):
    return pl.BlockSpec((ts, w), lambda i: (i, col))


TM = TN_ = 512


def _mm(name, m, n, terms, out_dtype=F32, add=None, tm=TM, tn=TN_, stacked=False):
    nterm = len(terms)

    def body(*refs):
        acc = None
        for t in range(nterm):
            part = _dot(refs[2 * t][...], refs[2 * t + 1][...], terms[t][4])
            acc = part if acc is None else acc + part
        if add is not None:
            acc = acc + refs[2 * nterm][...]
        refs[-1][...] = acc.astype(out_dtype)

    tile = pl.BlockSpec((tm, tn), lambda j, i: (i, j))
    ins, specs = [], []
    for a, a_spec, b, b_spec, _ in terms:
        ins += [a, b]
        specs += [a_spec, b_spec]
    if add is not None:
        ins.append(add)
        specs.append(tile)
    return pl.pallas_call(
        body, name=name, grid=(n // tn, m // tm), in_specs=specs,
        out_specs=pl.BlockSpec((None, tm, tn), lambda j, i: (j, i, 0)) if stacked else tile,
        out_shape=jax.ShapeDtypeStruct((n // tn, m, tn) if stacked else (m, n), out_dtype),
        compiler_params=_params(2))(*ins)


def _a_rows(k, col=0, tm=TM):
    return pl.BlockSpec((tm, k), lambda j, i: (i, col))


def _a_cols(k, tm=TM):
    return pl.BlockSpec((k, tm), lambda j, i: (0, i))


def _b_cols(k, row=0, col0=0, tn=TN_):
    return pl.BlockSpec((k, tn), lambda j, i: (row, col0 + j))


def _b_rows(k, row0=0, tn=TN_):
    return pl.BlockSpec((tn, k), lambda j, i: (row0 + j, 0))


def _rmsnorm_fwd(x, gains, name):
    def body(*refs):
        xv = refs[0][...]
        r = lax.rsqrt(jnp.mean(xv * xv, axis=-1, keepdims=True) + EPS)
        xh = xv * r
        for n in range(len(gains)):
            refs[1 + len(gains) + n][...] = (xh * refs[1 + n][...]).astype(BF16)

    return _rows_call(body, name, [x] + list(gains), [((S, D), BF16)] * len(gains))


def _rmsnorm_bwd(x, gains, dus, dres, name):
    n = len(gains)

    def body(*refs):
        x_ref, g_refs, du_refs, dres_ref = refs[0], refs[1:1 + n], refs[1 + n:1 + 2 * n], refs[1 + 2 * n]
        dx_ref, dxb_ref, dg_refs = refs[2 + 2 * n], refs[3 + 2 * n], refs[4 + 2 * n:]
        xv = x_ref[...]
        r = lax.rsqrt(jnp.mean(xv * xv, axis=-1, keepdims=True) + EPS)
        xh = xv * r
        gy = None
        for m in range(n):
            du = du_refs[m][...]
            part = jnp.sum(du * xh, axis=0, keepdims=True)

            @pl.when(pl.program_id(0) == 0)
            def _(m=m, part=part):
                dg_refs[m][...] = part

            @pl.when(pl.program_id(0) != 0)
            def _(m=m, part=part):
                dg_refs[m][...] += part

            t = du * g_refs[m][...]
            gy = t if gy is None else gy + t
        dx = dres_ref[...] + r * (gy - xh * jnp.mean(gy * xh, axis=-1, keepdims=True))
        dx_ref[...] = dx
        dxb_ref[...] = dx.astype(BF16)

    outs = [((S, D), F32), ((S, D), BF16)] + [((1, D), F32)] * n
    return _rows_call(body, name, [x] + list(gains) + list(dus) + [dres], outs)


def _a_post(qkvg, qg, kg):
    e, et = _seg_mats(D)

    def body(q_ref, k_ref, v_ref, qg_ref, kg_ref, e_ref, et_ref, qo, ko, vo):
        ev, etv = e_ref[...], et_ref[...]
        qv, kv = q_ref[...], k_ref[...]
        qo[...] = (qv * _head_rstd(qv, ev, etv) * qg_ref[...] * SCALE).astype(BF16)
        ko[...] = (kv * _head_rstd(kv, ev, etv) * kg_ref[...]).astype(BF16)
        vo[...] = v_ref[...].astype(BF16)

    whole = lambda a: pl.BlockSpec(a.shape, lambda i: (0, 0))
    return pl.pallas_call(
        body, name="a_post", grid=(S // ROWS,),
        in_specs=[_col_spec(ROWS, D, 0), _col_spec(ROWS, D, 1), _col_spec(ROWS, D, 2),
                  whole(qg), whole(kg), whole(e), whole(et)],
        out_specs=[_col_spec(ROWS, D, 0)] * 3,
        out_shape=[jax.ShapeDtypeStruct((S, D), BF16)] * 3,
        compiler_params=_params(1))(qkvg, qkvg, qkvg, qg, kg, e, et)


def _tri(upper):
    r, c = np.arange(ROWS)[:, None], np.arange(ROWS)[None, :]
    return jnp.asarray((r <= c) if upper else (r >= c), BF16)


def _forget_cumsum(fpad, bpad):
    def body(f_ref, b_ref, u_ref, c_ref, carry):
        @pl.when(pl.program_id(0) == 0)
        def _():
            carry[...] = jnp.zeros_like(carry)

        lf = jax.nn.log_sigmoid(f_ref[...] + b_ref[...])
        blk = _dot_split(lf.T, u_ref[...], 3) + carry[:, 0:1]
        c_ref[...] = blk
        carry[...] = jnp.broadcast_to(blk[:, ROWS - 1:ROWS], carry.shape)

    return pl.pallas_call(
        body, name="forget_cumsum", grid=(S // ROWS,),
        in_specs=[pl.BlockSpec((ROWS, LANES), lambda i: (i, 0)), pl.BlockSpec((1, LANES), lambda i: (0, 0)),
                  pl.BlockSpec((ROWS, ROWS), lambda i: (0, 0))],
        out_specs=pl.BlockSpec((LANES, ROWS), lambda i: (0, i)),
        out_shape=jax.ShapeDtypeStruct((LANES, S), F32),
        scratch_shapes=[pltpu.VMEM((LANES, LANES), F32)],
        compiler_params=_params(1))(fpad, bpad, _tri(True))


def _forget_bwd(dct, fpad, bpad):
    nb = S // ROWS

    def body(dc_ref, f_ref, b_ref, l_ref, df_ref, db_ref, carry):
        @pl.when(pl.program_id(0) == 0)
        def _():
            carry[...] = jnp.zeros_like(carry)
            db_ref[...] = jnp.zeros_like(db_ref)

        blk = _dot_split(dc_ref[...], l_ref[...], 3) + carry[:, 0:1]
        carry[...] = jnp.broadcast_to(blk[:, 0:1], carry.shape)
        df = blk.T * _sigmoid(-(f_ref[...] + b_ref[...]))
        df_ref[...] = df.astype(BF16)
        db_ref[...] += jnp.sum(df, axis=0, keepdims=True)

    return pl.pallas_call(
        body, name="forget_bwd", grid=(nb,),
        in_specs=[pl.BlockSpec((LANES, ROWS), lambda i: (0, nb - 1 - i)),
                  pl.BlockSpec((ROWS, LANES), lambda i: (nb - 1 - i, 0)),
                  pl.BlockSpec((1, LANES), lambda i: (0, 0)), pl.BlockSpec((ROWS, ROWS), lambda i: (0, 0))],
        out_specs=[pl.BlockSpec((ROWS, LANES), lambda i: (nb - 1 - i, 0)), pl.BlockSpec((1, LANES), lambda i: (0, 0))],
        out_shape=[jax.ShapeDtypeStruct((S, LANES), BF16), jax.ShapeDtypeStruct((1, LANES), F32)],
        scratch_shapes=[pltpu.VMEM((LANES, LANES), F32)],
        compiler_params=_params(1))(dct, fpad, bpad, _tri(False))


def _gate_fwd(o, proj, col, name):
    def body(o_ref, g_ref, y_ref):
        g = g_ref[...]
        y_ref[...] = (o_ref[...] * (g * _sigmoid(g))).astype(BF16)

    return pl.pallas_call(
        body, name=name, grid=(S // ROWS,),
        in_specs=[_col_spec(ROWS, D, 0), _col_spec(ROWS, D, col)],
        out_specs=_col_spec(ROWS, D, 0), out_shape=jax.ShapeDtypeStruct((S, D), BF16),
        compiler_params=_params(1))(o, proj)


def _gate_bwd(dy, o, proj, col, name):
    def body(dy_ref, o_ref, g_ref, do_ref, dg_ref):
        g, dyv = g_ref[...], dy_ref[...]
        sg = _sigmoid(g)
        do_ref[...] = dyv * (g * sg)
        dg_ref[...] = (dyv * o_ref[...] * (sg * (1.0 + g * (1.0 - sg)))).astype(BF16)

    return pl.pallas_call(
        body, name=name, grid=(S // ROWS,),
        in_specs=[_col_spec(ROWS, D, 0), _col_spec(ROWS, D, 0), _col_spec(ROWS, D, col)],
        out_specs=[_col_spec(ROWS, D, 0)] * 2,
        out_shape=[jax.ShapeDtypeStruct((S, D), F32), jax.ShapeDtypeStruct((S, D), BF16)],
        compiler_params=_params(1))(dy, o, proj)


def _headnorm_bwd(x, col, gain, dy, rope, name):
    e, et = _seg_mats(D)
    tabs = list(rope) if rope is not None else []

    def body(*refs):
        x_ref, g_ref, dy_ref, e_ref, et_ref = refs[:5]
        dx_ref, dg_ref = refs[-2:]
        xv, dyv, ev, etv = x_ref[...], dy_ref[...], e_ref[...], et_ref[...]
        if rope is not None:
            c, a, b = (jnp.tile(t[...], (1, D // LANES)) for t in refs[5:8])
            dyv = _rope_t(dyv, c, a, b)
        r = _head_rstd(xv, ev, etv)
        xh = xv * r
        part = jnp.sum(dyv * xh, axis=0, keepdims=True)

        @pl.when(pl.program_id(0) == 0)
        def _():
            dg_ref[...] = part

        @pl.when(pl.program_id(0) != 0)
        def _():
            dg_ref[...] += part

        gy = dyv * g_ref[...]
        seg = _dot_split(_dot_split(gy * xh, ev, 2) * (1.0 / HD), etv, 3)
        dx_ref[...] = (r * (gy - xh * seg)).astype(BF16)

    whole = lambda a: pl.BlockSpec(a.shape, lambda i: (0, 0))
    return pl.pallas_call(
        body, name=name, grid=(S // ROWS,),
        in_specs=[_col_spec(ROWS, D, col), whole(gain), _col_spec(ROWS, D, 0), whole(e), whole(et)]
                 + [pl.BlockSpec((ROWS, LANES), lambda i: (i, 0))] * len(tabs),
        out_specs=[_col_spec(ROWS, D, 0), whole(gain)],
        out_shape=[jax.ShapeDtypeStruct((S, D), BF16), jax.ShapeDtypeStruct((1, D), F32)],
        compiler_params=_params(1))(x, gain, dy, e, et, *tabs)


def _dup_mat():
    r, c = np.arange(KVW)[:, None], np.arange(2 * KVW)[None, :]
    return (r // HD == c // LANES) & (r % HD == c % HD)


def _fold_mat():
    r, c = np.arange(D)[:, None], np.arange(KVW)[None, :]
    return (r // (2 * LANES) == c // HD) & (r % HD == c % HD)


def _b_post(pb, kv, qg, kg, rope):
    e, et = _seg_mats(D)
    ek, etk = _seg_mats(KVW)
    dup = jnp.asarray(_dup_mat(), BF16)

    def body(q_ref, k_ref, v_ref, qg_ref, kg_ref, e_ref, et_ref, ek_ref, etk_ref, dup_ref, c_ref, a_ref, b_ref,
             qo, ko, vo):
        c1, a1, b1 = c_ref[...], a_ref[...], b_ref[...]
        qv = q_ref[...]
        qn = qv * _head_rstd(qv, e_ref[...], et_ref[...]) * qg_ref[...]
        t = lambda z, n: jnp.tile(z, (1, n))
        qo[...] = (_rope(qn, t(c1, D // LANES), t(a1, D // LANES), t(b1, D // LANES)) * SCALE).astype(BF16)
        kvv = k_ref[...]
        kn = kvv * _head_rstd(kvv, ek_ref[...], etk_ref[...]) * kg_ref[...]
        kr = _rope(kn, t(c1, KVW // LANES), t(a1, KVW // LANES), t(b1, KVW // LANES)).astype(BF16)
        ko[...] = _dot(kr, dup_ref[...]).astype(BF16)
        vo[...] = _dot(v_ref[...].astype(BF16), dup_ref[...]).astype(BF16)

    whole = lambda a: pl.BlockSpec(a.shape, lambda i: (0, 0))
    tab = pl.BlockSpec((ROWS, LANES), lambda i: (i, 0))
    return pl.pallas_call(
        body, name="b_post", grid=(S // ROWS,),
        in_specs=[_col_spec(ROWS, D, 0), _col_spec(ROWS, KVW, 0), _col_spec(ROWS, KVW, 1),
                  whole(qg), whole(kg), whole(e), whole(et), whole(ek), whole(etk), whole(dup), tab, tab, tab],
        out_specs=[_col_spec(ROWS, D, 0), _col_spec(ROWS, 2 * KVW, 0), _col_spec(ROWS, 2 * KVW, 0)],
        out_shape=[jax.ShapeDtypeStruct((S, D), BF16), jax.ShapeDtypeStruct((S, 2 * KVW), BF16),
                   jax.ShapeDtypeStruct((S, 2 * KVW), BF16)],
        compiler_params=_params(1))(pb, kv, kv, qg, kg, e, et, ek, etk, dup, *rope)


def _kv_bwd(dkdup, dvdup, kv, kg, rope):
    ek, etk = _seg_mats(KVW)
    fold = jnp.asarray(_fold_mat(), BF16)

    def body(dk_ref, dv_ref, k_ref, kg_ref, ek_ref, etk_ref, fold_ref, c_ref, a_ref, b_ref, dkv_ref, dg_ref):
        ev, etv, fv = ek_ref[...], etk_ref[...], fold_ref[...]
        t = lambda z: jnp.tile(z[...], (1, KVW // LANES))
        dk = _rope_t(_dot_split(dk_ref[...], fv, 3), t(c_ref), t(a_ref), t(b_ref))
        dv = _dot_split(dv_ref[...], fv, 3)
        xv = k_ref[...]
        r = _head_rstd(xv, ev, etv)
        xh = xv * r
        part = jnp.sum(dk * xh, axis=0, keepdims=True)

        @pl.when(pl.program_id(0) == 0)
        def _():
            dg_ref[...] = part

        @pl.when(pl.program_id(0) != 0)
        def _():
            dg_ref[...] += part

        gy = dk * kg_ref[...]
        seg = _dot_split(_dot_split(gy * xh, ev, 2) * (1.0 / HD), etv, 3)
        dkv_ref[:, 0:KVW] = (r * (gy - xh * seg)).astype(BF16)
        dkv_ref[:, KVW:2 * KVW] = dv.astype(BF16)

    whole = lambda a: pl.BlockSpec(a.shape, lambda i: (0, 0))
    tab = pl.BlockSpec((ROWS, LANES), lambda i: (i, 0))
    return pl.pallas_call(
        body, name="kv_bwd", grid=(S // ROWS,),
        in_specs=[_col_spec(ROWS, D, 0), _col_spec(ROWS, D, 0), _col_spec(ROWS, KVW, 0),
                  whole(kg), whole(ek), whole(etk), whole(fold), tab, tab, tab],
        out_specs=[_col_spec(ROWS, 2 * KVW, 0), whole(kg)],
        out_shape=[jax.ShapeDtypeStruct((S, 2 * KVW), BF16), jax.ShapeDtypeStruct((1, KVW), F32)],
        compiler_params=_params(1))(dkdup, dvdup, kv, kg, ek, etk, fold, *rope)


def _loss_head(out, target):
    def body(o_ref, t_ref, d_ref, db_ref, l_ref):
        diff = o_ref[...] - t_ref[...]
        d = diff * (1.0 / D)
        d_ref[...] = d
        db_ref[...] = d.astype(BF16)

        @pl.when(pl.program_id(0) == 0)
        def _():
            l_ref[...] = jnp.zeros_like(l_ref)

        l_ref[...] += jnp.sum(diff * diff, axis=0, keepdims=True)

    return _rows_call(body, "loss_head", [out, target], [((S, D), F32), ((S, D), BF16), ((1, D), F32)])


def _lane():
    return lax.broadcasted_iota(jnp.int32, (1, LANES), 1)


def _head_mask(hh):
    return (_lane() < HD) if hh == 0 else (_lane() >= HD)


def _fox_fwd(q, k, v, ct):
    nq = S // ATT

    def body(q_ref, k_ref, v_ref, c_ref, o_ref, lse_ref):
        i = pl.program_id(1)
        row = lax.broadcasted_iota(jnp.int32, (ATT, ATT), 0)
        col = lax.broadcasted_iota(jnp.int32, (ATT, ATT), 1)
        q2 = q_ref[...]
        qms = [jnp.where(_head_mask(hh), q2, jnp.zeros_like(q2)) for hh in (0, 1)]

        def step(j, carry, diag):
            off = pl.multiple_of(j * ATT, ATT)
            kj, vj = k_ref[pl.ds(off, ATT), :], v_ref[pl.ds(off, ATT), :]
            out = []
            for hh in (0, 1):
                m, l, acc = carry[hh]
                s = _dot(qms[hh], kj, NT) - c_ref[hh:hh + 1, pl.ds(off, ATT)]
                if diag:
                    s = jnp.where(col <= row, s, NEG)
                m_new = jnp.maximum(m, jnp.max(s, axis=1, keepdims=True))
                p = jnp.exp(s - m_new)
                alpha = jnp.exp(m - m_new)
                l = alpha * l + jnp.sum(p, axis=1, keepdims=True)
                p_hi = p.astype(BF16)
                p_lo = (p - p_hi.astype(F32)).astype(BF16)
                acc = alpha * acc + (_dot(p_hi, vj) + _dot(p_lo, vj))
                out.append((m_new, l, acc))
            return tuple(out)

        one = (jnp.full((ATT, 1), NEG, F32), jnp.zeros((ATT, 1), F32), jnp.zeros((ATT, LANES), F32))
        carry = lax.fori_loop(0, i, lambda j, cr: step(j, cr, False), (one, one))
        res = [(acc / l, m + jnp.log(l)) for m, l, acc in step(i, carry, True)]
        first = _head_mask(0)
        o_ref[...] = jnp.where(first, res[0][0], res[1][0])
        lse_ref[...] = jnp.where(first, res[0][1], res[1][1])

    blk = pl.BlockSpec((ATT, LANES), lambda p, i: (i, p))
    full = pl.BlockSpec((S, LANES), lambda p, i: (0, p))
    return pl.pallas_call(
        body, name="fox_fwd", grid=(NH // 2, nq),
        in_specs=[blk, full, full, pl.BlockSpec((None, 2, S), lambda p, i: (p, 0, 0))],
        out_specs=[blk, blk],
        out_shape=[jax.ShapeDtypeStruct((S, D), F32)] * 2,
        compiler_params=_params(2))(q, k, v, ct)


def _fox_bwd(q, k, v, ct, o, lse, do):
    nq = S // ATT

    def body(q_ref, k_ref, v_ref, c_ref, o_ref, lse_ref, do_ref, dq_ref, dk_ref, dvb_ref, dc_ref, dv_ref):
        i = pl.program_id(1)

        @pl.when(i == 0)
        def _():
            dk_ref[...] = jnp.zeros_like(dk_ref)
            dv_ref[...] = jnp.zeros_like(dv_ref)
            dc_ref[...] = jnp.zeros_like(dc_ref)

        row = lax.broadcasted_iota(jnp.int32, (ATT, ATT), 0)
        col = lax.broadcasted_iota(jnp.int32, (ATT, ATT), 1)
        q2, do2, lse2 = q_ref[...], do_ref[...], lse_ref[...]
        do2b = do2.astype(BF16)
        prod = do2b.astype(F32) * o_ref[...]
        heads = []
        for hh in (0, 1):
            hm = _head_mask(hh)
            heads.append((jnp.where(hm, q2, jnp.zeros_like(q2)), jnp.where(hm, do2b, jnp.zeros_like(do2b)),
                          jnp.sum(jnp.where(hm, prod, 0.0), axis=1, keepdims=True),
                          jnp.max(jnp.where(hm, lse2, NEG), axis=1, keepdims=True)))

        def step(j, dqs, diag):
            off = pl.multiple_of(j * ATT, ATT)
            kj, vj = k_ref[pl.ds(off, ATT), :], v_ref[pl.ds(off, ATT), :]
            dk, dv, out = None, None, []
            for hh in (0, 1):
                qm, dom, delta, lse_h = heads[hh]
                s = _dot(qm, kj, NT) - c_ref[hh:hh + 1, pl.ds(off, ATT)]
                p = jnp.exp(s - lse_h)
                if diag:
                    p = jnp.where(col <= row, p, 0.0)
                ds = p * (_dot(dom, vj, NT) - delta)
                dc_ref[hh:hh + 1, pl.ds(off, ATT)] += -jnp.sum(ds, axis=0, keepdims=True)
                dsb = ds.astype(BF16)
                dk_h, dv_h = _dot(dsb, qm, TN), _dot(p.astype(BF16), dom, TN)
                dk, dv = (dk_h, dv_h) if dk is None else (dk + dk_h, dv + dv_h)
                out.append(dqs[hh] + _dot(dsb, kj))
            dk_ref[pl.ds(off, ATT), :] += dk
            dv_ref[pl.ds(off, ATT), :] += dv
            return tuple(out)

        zero = jnp.zeros((ATT, LANES), F32)
        dqs = lax.fori_loop(0, i, lambda j, acc: step(j, acc, False), (zero, zero))
        dqs = step(i, dqs, True)
        dq_ref[...] = jnp.where(_head_mask(0), dqs[0], dqs[1]) * SCALE

        @pl.when(i == nq - 1)
        def _():
            dvb_ref[...] = dv_ref[...].astype(BF16)

    blk = pl.BlockSpec((ATT, LANES), lambda p, i: (i, p))
    full = pl.BlockSpec((S, LANES), lambda p, i: (0, p))
    cspec = pl.BlockSpec((None, 2, S), lambda p, i: (p, 0, 0))
    return pl.pallas_call(
        body, name="fox_bwd", grid=(NH // 2, nq),
        in_specs=[blk, full, full, cspec, blk, blk, blk],
        out_specs=[blk, full, full, cspec],
        out_shape=[jax.ShapeDtypeStruct((S, D), F32)] * 2 + [jax.ShapeDtypeStruct((S, D), BF16),
                                                              jax.ShapeDtypeStruct((NH // 2, 2, S), F32)],
        scratch_shapes=[pltpu.VMEM((S, LANES), F32)],
        compiler_params=_params(2))(q, k, v, ct, o, lse, do)


def _swa_logits(qm, kk, i, start):
    s = _dot(qm, kk, NT)
    qabs = i * WINDOW + lax.broadcasted_iota(jnp.int32, (WINDOW, 2 * WINDOW), 0)
    kabs = start + lax.broadcasted_iota(jnp.int32, (WINDOW, 2 * WINDOW), 1)
    valid = (kabs <= qabs) & (qabs - kabs < WINDOW)
    return s, valid


def _swa_fwd(q, kdup, vdup, sinks_t):
    def body(q_ref, k_ref, v_ref, sk_ref, o_ref, lse_ref):
        skv = sk_ref[...]
        first = _head_mask(0)
        for sb in range(SWQ):
            i = pl.program_id(1) * SWQ + sb
            rows = slice(sb * WINDOW, (sb + 1) * WINDOW)
            start = pl.multiple_of(jnp.maximum(i - 1, 0) * WINDOW, WINDOW)
            kk, vv = k_ref[pl.ds(start, 2 * WINDOW), :], v_ref[pl.ds(start, 2 * WINDOW), :]
            q2 = q_ref[rows, :]
            res = []
            for hh in (0, 1):
                hm = _head_mask(hh)
                qm = jnp.where(hm, q2, jnp.zeros_like(q2))
                sink = jnp.max(jnp.where(hm, skv, NEG), axis=1, keepdims=True)
                s, valid = _swa_logits(qm, kk, i, start)
                s = jnp.where(valid, s, NEG)
                m = jnp.maximum(jnp.max(s, axis=1, keepdims=True), sink)
                p = jnp.exp(s - m)
                l = jnp.sum(p, axis=1, keepdims=True) + jnp.exp(sink - m)
                res.append((_dot(p.astype(BF16), vv) / l, m + jnp.log(l)))
            o_ref[rows, :] = jnp.where(first, res[0][0], res[1][0])
            lse_ref[rows, :] = jnp.where(first, res[0][1], res[1][1])

    blk = pl.BlockSpec((SWQ * WINDOW, LANES), lambda p, i: (i, p))
    full = pl.BlockSpec((S, LANES), lambda p, i: (0, p // 2))
    return pl.pallas_call(
        body, name="swa_fwd", grid=(NH // 2, S // (SWQ * WINDOW)),
        in_specs=[blk, full, full, pl.BlockSpec((1, LANES), lambda p, i: (0, p))],
        out_specs=[blk, blk],
        out_shape=[jax.ShapeDtypeStruct((S, D), F32)] * 2,
        compiler_params=_params(2))(q, kdup, vdup, sinks_t)


def _swa_bwd(q, kdup, vdup, sinks_t, o, lse, do):
    def body(q_ref, k_ref, v_ref, sk_ref, o_ref, lse_ref, do_ref, dq_ref, dk_ref, dv_ref, dsk_ref):
        @pl.when(pl.program_id(1) == 0)
        def _():
            dk_ref[...] = jnp.zeros_like(dk_ref)
            dv_ref[...] = jnp.zeros_like(dv_ref)
            dsk_ref[...] = jnp.zeros_like(dsk_ref)

        skv = sk_ref[...]
        first = _head_mask(0)
        for sb in range(SWQ):
            i = pl.program_id(1) * SWQ + sb
            rows = slice(sb * WINDOW, (sb + 1) * WINDOW)
            start = pl.multiple_of(jnp.maximum(i - 1, 0) * WINDOW, WINDOW)
            kk, vv = k_ref[pl.ds(start, 2 * WINDOW), :], v_ref[pl.ds(start, 2 * WINDOW), :]
            q2, do2, lse2 = q_ref[rows, :], do_ref[rows, :], lse_ref[rows, :]
            do2b = do2.astype(BF16)
            prod = do2b.astype(F32) * o_ref[rows, :]
            dqs, dsk, dk, dv = [], [], None, None
            for hh in (0, 1):
                hm = _head_mask(hh)
                qm = jnp.where(hm, q2, jnp.zeros_like(q2))
                dom = jnp.where(hm, do2b, jnp.zeros_like(do2b))
                delta = jnp.sum(jnp.where(hm, prod, 0.0), axis=1, keepdims=True)
                lse_h = jnp.max(jnp.where(hm, lse2, NEG), axis=1, keepdims=True)
                sink = jnp.max(jnp.where(hm, skv, NEG), axis=1, keepdims=True)
                s, valid = _swa_logits(qm, kk, i, start)
                p = jnp.where(valid, jnp.exp(s - lse_h), 0.0)
                ds = p * (_dot(dom, vv, NT) - delta)
                dsb = ds.astype(BF16)
                dk_h, dv_h = _dot(dsb, qm, TN), _dot(p.astype(BF16), dom, TN)
                dk, dv = (dk_h, dv_h) if dk is None else (dk + dk_h, dv + dv_h)
                dqs.append(_dot(dsb, kk))
                dsk.append(-jnp.sum(jnp.exp(sink - lse_h) * delta, axis=0, keepdims=True))
            dk_ref[pl.ds(start, 2 * WINDOW), :] += dk
            dv_ref[pl.ds(start, 2 * WINDOW), :] += dv
            dq_ref[rows, :] = jnp.where(first, dqs[0], dqs[1]) * SCALE
            dsk_ref[...] += jnp.where(first, dsk[0], dsk[1])

    blk = pl.BlockSpec((SWQ * WINDOW, LANES), lambda p, i: (i, p))
    full = pl.BlockSpec((S, LANES), lambda p, i: (0, p // 2))
    acc = pl.BlockSpec((S, LANES), lambda p, i: (0, p))
    sk = pl.BlockSpec((1, LANES), lambda p, i: (0, p))
    return pl.pallas_call(
        body, name="swa_bwd", grid=(NH // 2, S // (SWQ * WINDOW)),
        in_specs=[blk, full, full, sk, blk, blk, blk],
        out_specs=[blk, acc, acc, sk],
        out_shape=[jax.ShapeDtypeStruct((S, D), F32)] * 3 + [jax.ShapeDtypeStruct((1, D), F32)],
        compiler_params=_params(2))(q, kdup, vdup, sinks_t, o, lse, do)


def _adamw_math(w, g, m, v):
    m = ADAM_B1 * m + (1.0 - ADAM_B1) * g
    v = ADAM_B2 * v + (1.0 - ADAM_B2) * jnp.square(g)
    m_hat = m / (1.0 - ADAM_B1 ** ADAM_STEP)
    v_hat = v / (1.0 - ADAM_B2 ** ADAM_STEP)
    delta = -ADAM_LR * (m_hat / (jnp.sqrt(v_hat) + ADAM_EPS) + ADAM_WD * w)
    return delta, m, v


def _adamw(w, g, m, v, name):
    r, c = w.shape
    tr = min(r, 128)

    def body(w_ref, g_ref, m_ref, v_ref, d_ref, mo_ref, vo_ref):
        d_ref[...], mo_ref[...], vo_ref[...] = _adamw_math(w_ref[...], g_ref[...], m_ref[...], v_ref[...])

    spec = pl.BlockSpec((tr, c), lambda i: (i, 0))
    return pl.pallas_call(
        body, name=name, grid=(r // tr,), in_specs=[spec] * 4, out_specs=[spec] * 3,
        out_shape=[jax.ShapeDtypeStruct((r, c), F32)] * 3, compiler_params=_params(1))(w, g, m, v)


SUM_TILE = 128


def _tiles(shape2d, axis, lead=0):
    r, c = shape2d
    blk = (SUM_TILE, c) if axis == 0 else (r, SUM_TILE)
    count = shape2d[axis] // SUM_TILE

    def index(pos, *lead_idx):
        return tuple(lead_idx) + ((pos, 0) if axis == 0 else (0, pos))

    return (None,) * lead + blk, count, index


def _adamw_halves(w, g_mine, g_theirs, m, v, axis, name):
    blk, count, index = _tiles(w.shape, axis)
    per_half = count // 2

    def body(w_ref, a_ref, b_ref, m_ref, v_ref, g_ref, d_ref, mo_ref, vo_ref):
        is_mine = pl.program_id(0) // per_half == lax.axis_index("c")
        g = jnp.where(is_mine, a_ref[...], b_ref[...])
        g_ref[...] = g
        d_ref[...], mo_ref[...], vo_ref[...] = _adamw_math(w_ref[...], g, m_ref[...], v_ref[...])

    spec = pl.BlockSpec(blk, lambda i: index(i))
    half = pl.BlockSpec(blk, lambda i: index(i % per_half))
    return pl.pallas_call(
        body, name=name, grid=(count,), in_specs=[spec, half, half, spec, spec], out_specs=[spec] * 4,
        out_shape=[jax.ShapeDtypeStruct(w.shape, F32)] * 4, compiler_params=_params(1))(w, g_mine, g_theirs, m, v)


def _chip_sum(blocks, from_sibling, axis, name):
    half2d = from_sibling.shape[1:]
    blk, count, index = _tiles(half2d, axis, lead=1)

    def body(lo_ref, hi_ref, p_ref, o32, o16):
        mine = jnp.where(lax.axis_index("c") == 0, lo_ref[...], hi_ref[...])
        acc = mine + p_ref[...]
        o32[...] = acc
        o16[...] = acc.astype(BF16)

    lo = pl.BlockSpec(blk, lambda k, i: index(i, k))
    hi = pl.BlockSpec(blk, lambda k, i: index(i + count, k))
    return pl.pallas_call(
        body, name=name, grid=(NCHIP, count), in_specs=[lo, hi, lo], out_specs=[lo, lo],
        out_shape=[jax.ShapeDtypeStruct(from_sibling.shape, F32), jax.ShapeDtypeStruct(from_sibling.shape, BF16)],
        compiler_params=_params(2))(blocks, blocks, from_sibling)


def _mesh_sum(own, parts, axis, name):
    blk, count, index = _tiles(own.shape, axis)
    n = parts.shape[0]

    def body(a_ref, p_ref, o_ref):
        acc = a_ref[...]
        for k in range(n):
            acc = acc + p_ref[k].astype(F32)
        o_ref[...] = acc

    spec = pl.BlockSpec(blk, lambda i: index(i))
    return pl.pallas_call(
        body, name=name, grid=(count,),
        in_specs=[spec, pl.BlockSpec((n,) + blk, lambda i: index(i, 0))],
        out_specs=spec, out_shape=jax.ShapeDtypeStruct(own.shape, F32),
        compiler_params=_params(1))(own, parts)


def _sum_stack(parts, name):
    n = parts.shape[0]

    def body(p_ref, o_ref):
        acc = p_ref[0]
        for k in range(1, n):
            acc = acc + p_ref[k]
        o_ref[...] = acc

    return pl.pallas_call(body, name=name, out_shape=jax.ShapeDtypeStruct(parts.shape[1:], F32))(parts)


def _coords():
    return lax.axis_index("x"), lax.axis_index("y"), lax.axis_index("c")


def _chip(who):
    return 2 * who[0] + who[1]


def _flip(who, mask):
    return tuple((1 - v) if b else v for v, b in zip(who, mask))


def _exchange(name, ins, outs, transfers, copies=()):
    ni, no = len(ins), len(outs)
    nt = len(transfers)

    def body(*refs):
        I, O = refs[:ni], refs[ni:ni + no]
        ssem, rsem, lsem = refs[ni + no:]
        me = _coords()
        local = [pltpu.make_async_copy(s(I, O, me), d(I, O, me), lsem.at[n]) for n, (s, d) in enumerate(copies)]
        for cp in local:
            cp.start()
        sends, recvs, arrived = [], [], set()
        for t, tr in enumerate(transfers):
            peer = _flip(me, tr["mask"])

            def make(who, t=t, tr=tr, peer=peer):
                return pltpu.make_async_remote_copy(
                    src_ref=tr["src"](I, O, me), dst_ref=tr["dst"](I, O, who),
                    send_sem=ssem.at[t], recv_sem=rsem.at[t], device_id=peer, device_id_type=MESH)

            after = tr.get("after")
            if after is not None and after not in arrived:
                recvs[after].wait_recv()
                arrived.add(after)
            snd = make(me)
            snd.start()
            sends.append(snd)
            recvs.append(make(peer))
        for t in range(nt):
            if t not in arrived:
                recvs[t].wait_recv()
        for snd in sends:
            snd.wait_send()
        for cp in local:
            cp.wait()

    hbm = pl.BlockSpec(memory_space=pltpu.HBM)
    return pl.pallas_call(
        body, name=name, in_specs=[hbm] * ni, out_specs=[hbm] * no,
        out_shape=[jax.ShapeDtypeStruct(s, d) for s, d in outs],
        scratch_shapes=[pltpu.SemaphoreType.DMA((nt,)), pltpu.SemaphoreType.DMA((nt,)),
                        pltpu.SemaphoreType.DMA((max(len(copies), 1),))],
        compiler_params=pltpu.CompilerParams(has_side_effects=True))(*ins)


CHIP_MASKS = [(0, 1, 0), (1, 0, 0), (1, 1, 0)]
SIBLING = (0, 0, 1)


def _half(shape2d, axis, which):
    n = shape2d[axis] // 2
    cut = pl.ds(pl.multiple_of(which * n, n), n)
    return (cut, slice(None)) if axis == 0 else (slice(None), cut)


def _gather_shards(shards, axes):
    def half(a, who):
        return _half(shards[a].shape, axes[a], who[2])

    over_ici, onward = [], []
    for a in range(len(shards)):
        for mask in CHIP_MASKS:
            over_ici.append(dict(
                mask=mask,
                src=lambda I, O, me, a=a: I[a].at[half(a, me)],
                dst=lambda I, O, who, a=a: O[a].at[(_chip(who),) + half(a, who)]))
            onward.append(dict(
                mask=SIBLING, after=len(over_ici) - 1,
                src=lambda I, O, me, a=a, mask=mask: O[a].at[(_chip(_flip(me, mask)),) + half(a, me)],
                dst=lambda I, O, who, a=a, mask=mask: O[a].at[(_chip(_flip(who, mask)),) + half(a, who)]))
    transfers = over_ici + onward
    outs = [((NCHIP,) + s.shape, s.dtype) for s in shards]
    gathered = _exchange("gather_weights", shards, outs, transfers)
    mine = lax.broadcasted_iota(jnp.int32, (NCHIP, 1, 1), 0) == _chip(_coords())
    return [jnp.where(mine, s[None], t) for s, t in zip(shards, gathered)]


def _to_sibling(arrs, name):
    transfers = [dict(mask=SIBLING, src=lambda I, O, me, a=a: I[a], dst=lambda I, O, who, a=a: O[a])
                 for a in range(len(arrs))]
    return _exchange(name, arrs, [(t.shape, t.dtype) for t in arrs], transfers)


def _halves_to_sibling(blocks, axes):
    def cut(a, which):
        return (slice(None),) + _half(blocks[a].shape[1:], axes[a], which)

    transfers = [dict(mask=SIBLING, src=lambda I, O, me, a=a: I[a].at[cut(a, 1 - me[2])],
                      dst=lambda I, O, who, a=a: O[a]) for a in range(len(blocks))]
    outs = []
    for b, ax in zip(blocks, axes):
        shape = list(b.shape)
        shape[ax + 1] //= 2
        outs.append((tuple(shape), b.dtype))
    return _exchange("sibling_halves", blocks, outs, transfers)


def _scatter_chip_sums(tb):
    transfers = []
    for a in range(len(tb)):
        for n, mask in enumerate(CHIP_MASKS):
            transfers.append(dict(
                mask=mask,
                src=lambda I, O, me, a=a, mask=mask: I[a].at[_chip(_flip(me, mask))],
                dst=lambda I, O, who, a=a, n=n: O[a].at[n]))
    outs = [((3,) + t.shape[1:], t.dtype) for t in tb]
    return _exchange("scatter_grads", tb, outs, transfers)


def _gather_small(vec):
    def slot(who):
        return 4 * who[0] + 2 * who[1] + who[2]

    masks = [(m >> 2 & 1, m >> 1 & 1, m & 1) for m in range(1, 8)]
    transfers = [dict(mask=mask, src=lambda I, O, me: I[0], dst=lambda I, O, who: O[0].at[slot(who)])
                 for mask in masks]
    copies = [(lambda I, O, me: I[0], lambda I, O, me: O[0].at[slot(me)])]
    return _exchange("gather_small", [vec], [((8,) + vec.shape, vec.dtype)], transfers, copies)[0]


def _rope_tables(positions):
    half = ROT // 2
    inv_freq = jnp.power(jnp.float32(THETA), -jnp.arange(0, ROT, 2, dtype=F32) / ROT)
    ang = positions.astype(F32)[:, None] * inv_freq[None, :]
    cos, sin = jnp.cos(ang), jnp.sin(ang)
    one, zero, z8 = jnp.ones((S, HD - ROT), F32), jnp.zeros((S, HD - ROT), F32), jnp.zeros((S, half), F32)
    c = jnp.concatenate([cos, cos, one], axis=1)
    a = jnp.concatenate([-sin, z8, zero], axis=1)
    b = jnp.concatenate([z8, sin, zero], axis=1)
    return tuple(jnp.tile(t, (1, 2)) for t in (c, a, b))


def _tile_heads(g, w):
    return jnp.tile(g.reshape(1, HD), (1, w // HD))


def _fold_heads(dg):
    return dg.reshape(-1, HD).sum(axis=0)


def _pad_lanes(a):
    return jnp.pad(a, ((0, 0), (0, LANES - a.shape[1])))


def _local_step(x, target, positions, wt):
    rope = _rope_tables(positions)
    w1t, w_in_b = wt["w_in_a_t"], wt["w_in_b"]
    f_row = 3 * D // LANES
    wg_t = w1t[3 * D + NH:]
    in_b_block = lambda c: pl.BlockSpec((None, TN_, TN_), lambda j, i: (c, j, 0))
    b_pad = _pad_lanes(wt["b_forget"].reshape(1, NH))
    qg_a, kg_a = _tile_heads(wt["qnorm_a_g"], D), _tile_heads(wt["knorm_a_g"], D)
    qg_b, kg_b = _tile_heads(wt["qnorm_b_g"], D), _tile_heads(wt["knorm_b_g"], KVW)
    norm_a, kv_g, norm_b = wt["norm_a_g"].reshape(1, D), wt["kv_norm_g"].reshape(1, D), wt["norm_b_g"].reshape(1, D)
    sinks_t = jnp.repeat(wt["sinks"].reshape(1, NH), HD, axis=1)

    (u_a,) = _rmsnorm_fwd(x, [norm_a], "norm_a")
    qkv = _mm("proj_a", S, 3 * D, [(u_a, _a_rows(D), w1t, _b_rows(D), NT)])
    fpad = _mm("proj_f", S, LANES, [(u_a, _a_rows(D), w1t, _b_rows(D, row0=f_row, tn=LANES), NT)], tn=LANES)
    gate_a = _mm("proj_gate_a", S, D, [(u_a, _a_rows(D), wg_t, _b_rows(D), NT)])
    q_a, k_a, v_a = _a_post(qkv, qg_a, kg_a)
    ct = _forget_cumsum(fpad, b_pad)
    ct2 = ct[:NH].reshape(NH // 2, 2, S)
    o_a, lse_a = _fox_fwd(q_a, k_a, v_a, ct2)
    y_a = _gate_fwd(o_a, gate_a, 0, "gate_a")
    h1 = _mm("out_a", S, D, [(y_a, _a_rows(D), wt["w_out_a"], _b_cols(D), None)], add=x)
    u_kv, u_b = _rmsnorm_fwd(h1, [kv_g, norm_b], "norm_b")
    kv = _mm("proj_kv", S, 2 * KVW, [(u_kv, _a_rows(D), wt["w_kv"], _b_cols(D), None)])
    pb = _mm("proj_b", S, 2 * D,
             [(u_b, _a_rows(D), w_in_b, pl.BlockSpec((None, D, TN_), lambda j, i: (j, 0, 0)), None)])
    q_b, kdup, vdup = _b_post(pb, kv, qg_b, kg_b, rope)
    o_b, lse_b = _swa_fwd(q_b, kdup, vdup, sinks_t)
    y_b = _gate_fwd(o_b, pb, 1, "gate_b")
    out = _mm("out_b", S, D, [(y_b, _a_rows(D), wt["w_out_b"], _b_cols(D), None)], add=h1)
    d_out, d_out_b, sq = _loss_head(out, target)

    g = {}
    g["w_out_b"] = _mm("dw_out_b", D, D, [(y_b, _a_cols(S), d_out_b, _b_cols(S), TN)])
    d_y_b = _mm("dy_b", S, D, [(d_out_b, _a_rows(D), wt["w_out_b"], _b_rows(D), NT)])
    d_o_b, d_gate_b = _gate_bwd(d_y_b, o_b, pb, 1, "gate_b_bwd")
    dq_b, dkdup, dvdup, dsk = _swa_bwd(q_b, kdup, vdup, sinks_t, o_b, lse_b, d_o_b)
    g["sinks"] = dsk[0, ::HD]
    d_qb_raw, dg = _headnorm_bwd(pb, 0, qg_b, dq_b, rope, "qnorm_b_bwd")
    g["qnorm_b_g"] = _fold_heads(dg)
    d_pb = [d_qb_raw, d_qb_raw, d_gate_b, d_gate_b]
    g["w_in_b"] = jnp.concatenate([
        _mm("dw_in_b_q", D, D, [(u_b, _a_cols(S), d_qb_raw, _b_cols(S), TN)], stacked=True),
        _mm("dw_in_b_gate", D, D, [(u_b, _a_cols(S), d_gate_b, _b_cols(S), TN)], stacked=True)], axis=0)
    d_u_b = _mm("du_b", S, D, [(d_pb[c], _a_rows(TN_, col=c % 2), w_in_b, in_b_block(c), NT) for c in range(NCHIP)])
    d_kv, dg = _kv_bwd(dkdup, dvdup, kv, kg_b, rope)
    g["knorm_b_g"] = _fold_heads(dg)
    g["w_kv"] = _mm("dw_kv", D, 2 * KVW, [(u_kv, _a_cols(S), d_kv, _b_cols(S), TN)])
    d_u_kv = _mm("du_kv", S, D, [(d_kv, _a_rows(2 * KVW), wt["w_kv"], _b_rows(2 * KVW), NT)])
    d_h1, d_h1_b, g["kv_norm_g"], g["norm_b_g"] = _rmsnorm_bwd(h1, [kv_g, norm_b], [d_u_kv, d_u_b], d_out, "norm_b_bwd")
    g["w_out_a"] = _mm("dw_out_a", D, D, [(y_a, _a_cols(S), d_h1_b, _b_cols(S), TN)])
    d_y_a = _mm("dy_a", S, D, [(d_h1_b, _a_rows(D), wt["w_out_a"], _b_rows(D), NT)])
    d_o_a, d_gate_a = _gate_bwd(d_y_a, o_a, gate_a, 0, "gate_a_bwd")
    dq_a, dk_a, dv_a, dct = _fox_bwd(q_a, k_a, v_a, ct2, o_a, lse_a, d_o_a)
    dct_pad = jnp.pad(dct.reshape(NH, S), ((0, LANES - NH), (0, 0)))
    d_f, db = _forget_bwd(dct_pad, fpad, b_pad)
    g["b_forget"] = db[0, :NH]
    d_q_raw, dg = _headnorm_bwd(qkv, 0, qg_a, dq_a, None, "qnorm_a_bwd")
    g["qnorm_a_g"] = _fold_heads(dg)
    d_k_raw, dg = _headnorm_bwd(qkv, 1, kg_a, dk_a, None, "knorm_a_bwd")
    g["knorm_a_g"] = _fold_heads(dg)
    pieces = [("q", d_q_raw), ("k", d_k_raw), ("v", dv_a), ("gate", d_gate_a)]
    dw = {n: _mm("dw_in_a_" + n, D, D, [(t, _a_cols(S), u_a, _b_cols(S), TN)]) for n, t in pieces}
    dw_f = _mm("dw_in_a_f", LANES, D, [(d_f, _a_cols(S, tm=LANES), u_a, _b_cols(S), TN)], tm=LANES)
    g["w_in_a"] = jnp.concatenate([dw["q"], dw["k"], dw["v"], dw_f[:NH], dw["gate"]], axis=0).reshape(
        NCHIP, (4 * D + NH) // NCHIP, D)
    d_u_a = _mm("du_a", S, D, [
        (d_q_raw, _a_rows(D), w1t, _b_cols(D, row=0), None), (d_k_raw, _a_rows(D), w1t, _b_cols(D, row=1), None),
        (dv_a, _a_rows(D), w1t, _b_cols(D, row=2), None), (d_gate_a, _a_rows(D), wg_t, _b_cols(D), None),
        (d_f, _a_rows(LANES), w1t, _b_cols(LANES, row=f_row), None)])
    d_x, _, g["norm_a_g"] = _rmsnorm_bwd(x, [norm_a], [d_u_a], d_h1, "norm_a_bwd")
    return sq, d_x, g


BIG = ["w_in_a", "w_out_a", "w_kv", "w_in_b", "w_out_b"]
SPLIT = {"w_in_a": 1, "w_out_a": 0, "w_kv": 0, "w_in_b": 0, "w_out_b": 0}
SMALL = ["norm_a_g", "b_forget", "qnorm_a_g", "knorm_a_g", "kv_norm_g", "knorm_b_g", "norm_b_g", "qnorm_b_g", "sinks"]
NAMES = ["norm_a_g", "w_in_a", "b_forget", "qnorm_a_g", "knorm_a_g", "w_out_a", "kv_norm_g", "w_kv", "knorm_b_g",
         "norm_b_g", "w_in_b", "qnorm_b_g", "sinks", "w_out_b"]


def _pack(vals):
    flat = []
    for v in vals:
        v = v.reshape(-1)
        flat.append(jnp.pad(v, (0, -v.shape[0] % LANES)))
    flat = jnp.concatenate(flat)
    flat = jnp.pad(flat, (0, -flat.shape[0] % (8 * LANES)))
    return flat.reshape(-1, LANES)


def _unpack(packed, shapes):
    flat, out, off = packed.reshape(-1), [], 0
    for s in shapes:
        n = int(np.prod(s))
        out.append(flat[off:off + n].reshape(s))
        off += n + (-n % LANES)
    return out


def kernel(x, positions, norm_a_g, w_in_a, b_forget, qnorm_a_g, knorm_a_g, w_out_a, kv_norm_g, w_kv, knorm_b_g, norm_b_g, w_in_b, qnorm_b_g, sinks, w_out_b, loss_target, m_norm_a_g, m_w_in_a, m_b_forget, m_qnorm_a_g, m_knorm_a_g, m_w_out_a, m_kv_norm_g, m_w_kv, m_knorm_b_g, m_norm_b_g, m_w_in_b, m_qnorm_b_g, m_sinks, m_w_out_b, v_norm_a_g, v_w_in_a, v_b_forget, v_qnorm_a_g, v_knorm_a_g, v_w_out_a, v_kv_norm_g, v_w_kv, v_knorm_b_g, v_norm_b_g, v_w_in_b, v_qnorm_b_g, v_sinks, v_w_out_b):
    w = dict(norm_a_g=norm_a_g, w_in_a=w_in_a, b_forget=b_forget, qnorm_a_g=qnorm_a_g, knorm_a_g=knorm_a_g,
             w_out_a=w_out_a, kv_norm_g=kv_norm_g, w_kv=w_kv, knorm_b_g=knorm_b_g, norm_b_g=norm_b_g,
             w_in_b=w_in_b, qnorm_b_g=qnorm_b_g, sinks=sinks, w_out_b=w_out_b)
    m = dict(norm_a_g=m_norm_a_g, w_in_a=m_w_in_a, b_forget=m_b_forget, qnorm_a_g=m_qnorm_a_g, knorm_a_g=m_knorm_a_g,
             w_out_a=m_w_out_a, kv_norm_g=m_kv_norm_g, w_kv=m_w_kv, knorm_b_g=m_knorm_b_g, norm_b_g=m_norm_b_g,
             w_in_b=m_w_in_b, qnorm_b_g=m_qnorm_b_g, sinks=m_sinks, w_out_b=m_w_out_b)
    v = dict(norm_a_g=v_norm_a_g, w_in_a=v_w_in_a, b_forget=v_b_forget, qnorm_a_g=v_qnorm_a_g, knorm_a_g=v_knorm_a_g,
             w_out_a=v_w_out_a, kv_norm_g=v_kv_norm_g, w_kv=v_w_kv, knorm_b_g=v_knorm_b_g, norm_b_g=v_norm_b_g,
             w_in_b=v_w_in_b, qnorm_b_g=v_qnorm_b_g, sinks=v_sinks, w_out_b=v_w_out_b)
    my_chip = 2 * lax.axis_index("x") + lax.axis_index("y")

    def shard2d(t, n):
        t = t.reshape(t.shape[-2:])
        return t.T if n == "w_in_a" else t

    w2d = {n: shard2d(w[n], n) for n in BIG}

    norm_a_rows = jnp.broadcast_to(norm_a_g.reshape(1, D // NCHIP), (16, D // NCHIP))
    gathered = _gather_shards([w2d[n].astype(BF16) for n in BIG] + [norm_a_rows], [SPLIT[n] for n in BIG] + [0])
    wt = {n: t.reshape(-1, t.shape[2]) for n, t in zip(BIG, gathered[:-1])}
    wt["w_in_a_t"] = wt.pop("w_in_a")
    wt["w_in_b"] = gathered[BIG.index("w_in_b")]
    wt["norm_a_g"] = gathered[-1][:, 0, :].reshape(1, D)
    for n in SMALL[1:]:
        wt[n] = w[n]

    sq, d_x, g = _local_step(x[0], loss_target[0], positions, wt)

    small_shapes = [(D,), (NH,), (HD,), (HD,), (D,), (HD,), (D,), (HD,), (NH,), (D,)]
    packed = _pack([g[n] for n in SMALL] + [sq])
    total = _sum_stack(_gather_small(packed), "sum_small")
    small_g = dict(zip(SMALL, _unpack(total, small_shapes)[:-1]))
    loss = 0.5 * jnp.sum(_unpack(total, small_shapes)[-1]) / D
    small_g["norm_a_g"] = lax.dynamic_slice(small_g["norm_a_g"], (my_chip * (D // NCHIP),), (D // NCHIP,))

    axes = [SPLIT[n] for n in BIG]
    blocks = [g[n] if g[n].ndim == 3 else g[n].reshape(NCHIP, -1, g[n].shape[1]) for n in BIG]
    from_sibling = _halves_to_sibling(blocks, axes)
    chip_f32, chip_bf16 = [], []
    for n, ax, blk, part in zip(BIG, axes, blocks, from_sibling):
        t32, t16 = _chip_sum(blk, part, ax, "chip_sum_" + n)
        chip_f32.append(t32)
        chip_bf16.append(t16)
    arrived = _scatter_chip_sums(chip_bf16)
    halves = []
    for n, ax, t32, parts in zip(BIG, axes, chip_f32, arrived):
        own = lax.dynamic_index_in_dim(t32, my_chip, axis=0, keepdims=False)
        halves.append(_mesh_sum(own, parts, ax, "mesh_sum_" + n))
    sibling_done = _to_sibling(halves, "finished_halves")

    res = {}
    for n, ax, mine_half, their_half in zip(BIG, axes, halves, sibling_done):
        out4 = _adamw_halves(w2d[n], mine_half, their_half, shard2d(m[n], n), shard2d(v[n], n), ax, "adamw_" + n)
        res[n] = tuple((t.T if n == "w_in_a" else t).reshape(w[n].shape) for t in out4)
    sm_g = _pack([small_g[n] for n in SMALL])
    sm = [_pack([d[n] for n in SMALL]) for d in (w, m, v)]
    sm_out = _adamw(sm[0], sm_g, sm[1], sm[2], "adamw_small")
    sm_shapes = [w[n].shape for n in SMALL]
    unpacked = [_unpack(t, sm_shapes) for t in (sm_g,) + tuple(sm_out)]
    for i, n in enumerate(SMALL):
        res[n] = tuple(u[i] for u in unpacked)

    outs = [loss, d_x[None]]
    for k in range(4):
        outs += [res[n][k] for n in NAMES]
    return tuple(outs)
```

```python
import numpy as np
import jax
import jax.numpy as jnp
from jax import lax
from jax.experimental import pallas as pl
from jax.experimental.pallas import tpu as pltpu

F32, BF16 = jnp.float32, jnp.bfloat16
S, D, HD, NH, NKV = 2048, 1024, 64, 16, 4
KVW = NKV * HD
WINDOW = 128
ROT = HD // 4
THETA = 500000.0
EPS = 1e-6
SCALE = HD ** -0.5
LANES = 128
NEG = -1e30
VMEM_LIMIT = 48 * 2 ** 20
ROWS = 256
ATT = 256
SWQ = 4
NCHIP = 4
ADAM_LR, ADAM_B1, ADAM_B2, ADAM_EPS, ADAM_WD, ADAM_STEP = 0.001, 0.9, 0.999, 1e-08, 0.01, 10
NT = (((1,), (1,)), ((), ()))
TN = (((0,), (0,)), ((), ()))
MESH = pl.DeviceIdType.MESH


def _params(n):
    return pltpu.CompilerParams(dimension_semantics=("arbitrary",) * n, vmem_limit_bytes=VMEM_LIMIT)


def _dot(a, b, dims=None):
    if dims is None:
        return jnp.dot(a, b, preferred_element_type=F32)
    return lax.dot_general(a, b, dims, preferred_element_type=F32)


def _dot_split(a, b, n):
    out, rest = None, a
    for _ in range(n):
        hi = rest.astype(BF16)
        term = _dot(hi, b)
        out = term if out is None else out + term
        rest = rest - hi.astype(F32)
    return out


def _seg_mats(w):
    e = (np.arange(w)[:, None] // HD == np.arange(LANES)[None, :]).astype(np.float32)
    return jnp.asarray(e, BF16), jnp.asarray(e.T, BF16)


def _head_rstd(x, e, et):
    ss = _dot_split(x * x, e, 2)
    return _dot_split(lax.rsqrt(ss * (1.0 / HD) + EPS), et, 3)


def _rope(x, c, a, b):
    w = x.shape[1]
    return x * c + pltpu.roll(x, w - ROT // 2, 1) * a + pltpu.roll(x, ROT // 2, 1) * b


def _rope_t(dy, c, a, b):
    w = dy.shape[1]
    return dy * c + pltpu.roll(dy * b, w - ROT // 2, 1) + pltpu.roll(dy * a, ROT // 2, 1)


def _sigmoid(x):
    return 1.0 / (1.0 + jnp.exp(-x))


def _row_spec(shape, ts):
    nd = len(shape)
    if shape[0] == S:
        return pl.BlockSpec((ts,) + tuple(shape[1:]), lambda i: (i,) + (0,) * (nd - 1))
    return pl.BlockSpec(tuple(shape), lambda i: (0,) * nd)


def _rows_call(body, name, ins, outs, ts=ROWS):
    return pl.pallas_call(
        body, name=name, grid=(S // ts,),
        in_specs=[_row_spec(a.shape, ts) for a in ins],
        out_specs=[_row_spec(s, ts) for s, _ in outs],
        out_shape=[jax.ShapeDtypeStruct(s, d) for s, d in outs],
        compiler_params=_params(1))(*ins)


def _col_spec(ts, w, col):
    return pl.BlockSpec((ts, w), lambda i: (i, col))


TM = TN_ = 512


def _mm(name, m, n, terms, out_dtype=F32, add=None, tm=TM, tn=TN_, stacked=False):
    nterm = len(terms)

    def body(*refs):
        acc = None
        for t in range(nterm):
            part = _dot(refs[2 * t][...], refs[2 * t + 1][...], terms[t][4])
            acc = part if acc is None else acc + part
        if add is not None:
            acc = acc + refs[2 * nterm][...]
        refs[-1][...] = acc.astype(out_dtype)

    tile = pl.BlockSpec((tm, tn), lambda j, i: (i, j))
    ins, specs = [], []
    for a, a_spec, b, b_spec, _ in terms:
        ins += [a, b]
        specs += [a_spec, b_spec]
    if add is not None:
        ins.append(add)
        specs.append(tile)
    return pl.pallas_call(
        body, name=name, grid=(n // tn, m // tm), in_specs=specs,
        out_specs=pl.BlockSpec((None, tm, tn), lambda j, i: (j, i, 0)) if stacked else tile,
        out_shape=jax.ShapeDtypeStruct((n // tn, m, tn) if stacked else (m, n), out_dtype),
        compiler_params=_params(2))(*ins)


def _a_rows(k, col=0, tm=TM):
    return pl.BlockSpec((tm, k), lambda j, i: (i, col))


def _a_cols(k, tm=TM):
    return pl.BlockSpec((k, tm), lambda j, i: (0, i))


def _b_cols(k, row=0, col0=0, tn=TN_):
    return pl.BlockSpec((k, tn), lambda j, i: (row, col0 + j))


def _b_rows(k, row0=0, tn=TN_):
    return pl.BlockSpec((tn, k), lambda j, i: (row0 + j, 0))


def _rmsnorm_fwd(x, gains, name):
    def body(*refs):
        xv = refs[0][...]
        r = lax.rsqrt(jnp.mean(xv * xv, axis=-1, keepdims=True) + EPS)
        xh = xv * r
        for n in range(len(gains)):
            refs[1 + len(gains) + n][...] = (xh * refs[1 + n][...]).astype(BF16)

    return _rows_call(body, name, [x] + list(gains), [((S, D), BF16)] * len(gains))


def _rmsnorm_bwd(x, gains, dus, dres, name):
    n = len(gains)

    def body(*refs):
        x_ref, g_refs, du_refs, dres_ref = refs[0], refs[1:1 + n], refs[1 + n:1 + 2 * n], refs[1 + 2 * n]
        dx_ref, dxb_ref, dg_refs = refs[2 + 2 * n], refs[3 + 2 * n], refs[4 + 2 * n:]
        xv = x_ref[...]
        r = lax.rsqrt(jnp.mean(xv * xv, axis=-1, keepdims=True) + EPS)
        xh = xv * r
        gy = None
        for m in range(n):
            du = du_refs[m][...]
            part = jnp.sum(du * xh, axis=0, keepdims=True)

            @pl.when(pl.program_id(0) == 0)
            def _(m=m, part=part):
                dg_refs[m][...] = part

            @pl.when(pl.program_id(0) != 0)
            def _(m=m, part=part):
                dg_refs[m][...] += part

            t = du * g_refs[m][...]
            gy = t if gy is None else gy + t
        dx = dres_ref[...] + r * (gy - xh * jnp.mean(gy * xh, axis=-1, keepdims=True))
        dx_ref[...] = dx
        dxb_ref[...] = dx.astype(BF16)

    outs = [((S, D), F32), ((S, D), BF16)] + [((1, D), F32)] * n
    return _rows_call(body, name, [x] + list(gains) + list(dus) + [dres], outs)


def _a_post(qkvg, qg, kg):
    e, et = _seg_mats(D)

    def body(q_ref, k_ref, v_ref, qg_ref, kg_ref, e_ref, et_ref, qo, ko, vo):
        ev, etv = e_ref[...], et_ref[...]
        qv, kv = q_ref[...], k_ref[...]
        qo[...] = (qv * _head_rstd(qv, ev, etv) * qg_ref[...] * SCALE).astype(BF16)
        ko[...] = (kv * _head_rstd(kv, ev, etv) * kg_ref[...]).astype(BF16)
        vo[...] = v_ref[...].astype(BF16)

    whole = lambda a: pl.BlockSpec(a.shape, lambda i: (0, 0))
    return pl.pallas_call(
        body, name="a_post", grid=(S // ROWS,),
        in_specs=[_col_spec(ROWS, D, 0), _col_spec(ROWS, D, 1), _col_spec(ROWS, D, 2),
                  whole(qg), whole(kg), whole(e), whole(et)],
        out_specs=[_col_spec(ROWS, D, 0)] * 3,
        out_shape=[jax.ShapeDtypeStruct((S, D), BF16)] * 3,
        compiler_params=_params(1))(qkvg, qkvg, qkvg, qg, kg, e, et)


def _tri(upper):
    r, c = np.arange(ROWS)[:, None], np.arange(ROWS)[None, :]
    return jnp.asarray((r <= c) if upper else (r >= c), BF16)


def _forget_cumsum(fpad, bpad):
    def body(f_ref, b_ref, u_ref, c_ref, carry):
        @pl.when(pl.program_id(0) == 0)
        def _():
            carry[...] = jnp.zeros_like(carry)

        lf = jax.nn.log_sigmoid(f_ref[...] + b_ref[...])
        blk = _dot_split(lf.T, u_ref[...], 3) + carry[:, 0:1]
        c_ref[...] = blk
        carry[...] = jnp.broadcast_to(blk[:, ROWS - 1:ROWS], carry.shape)

    return pl.pallas_call(
        body, name="forget_cumsum", grid=(S // ROWS,),
        in_specs=[pl.BlockSpec((ROWS, LANES), lambda i: (i, 0)), pl.BlockSpec((1, LANES), lambda i: (0, 0)),
                  pl.BlockSpec((ROWS, ROWS), lambda i: (0, 0))],
        out_specs=pl.BlockSpec((LANES, ROWS), lambda i: (0, i)),
        out_shape=jax.ShapeDtypeStruct((LANES, S), F32),
        scratch_shapes=[pltpu.VMEM((LANES, LANES), F32)],
        compiler_params=_params(1))(fpad, bpad, _tri(True))


def _forget_bwd(dct, fpad, bpad):
    nb = S // ROWS

    def body(dc_ref, f_ref, b_ref, l_ref, df_ref, db_ref, carry):
        @pl.when(pl.program_id(0) == 0)
        def _():
            carry[...] = jnp.zeros_like(carry)
            db_ref[...] = jnp.zeros_like(db_ref)

        blk = _dot_split(dc_ref[...], l_ref[...], 3) + carry[:, 0:1]
        carry[...] = jnp.broadcast_to(blk[:, 0:1], carry.shape)
        df = blk.T * _sigmoid(-(f_ref[...] + b_ref[...]))
        df_ref[...] = df.astype(BF16)
        db_ref[...] += jnp.sum(df, axis=0, keepdims=True)

    return pl.pallas_call(
        body, name="forget_bwd", grid=(nb,),
        in_specs=[pl.BlockSpec((LANES, ROWS), lambda i: (0, nb - 1 - i)),
                  pl.BlockSpec((ROWS, LANES), lambda i: (nb - 1 - i, 0)),
                  pl.BlockSpec((1, LANES), lambda i: (0, 0)), pl.BlockSpec((ROWS, ROWS), lambda i: (0, 0))],
        out_specs=[pl.BlockSpec((ROWS, LANES), lambda i: (nb - 1 - i, 0)), pl.BlockSpec((1, LANES), lambda i: (0, 0))],
        out_shape=[jax.ShapeDtypeStruct((S, LANES), BF16), jax.ShapeDtypeStruct((1, LANES), F32)],
        scratch_shapes=[pltpu.VMEM((LANES, LANES), F32)],
        compiler_params=_params(1))(dct, fpad, bpad, _tri(False))


def _gate_fwd(o, proj, col, name):
    def body(o_ref, g_ref, y_ref):
        g = g_ref[...]
        y_ref[...] = (o_ref[...] * (g * _sigmoid(g))).astype(BF16)

    return pl.pallas_call(
        body, name=name, grid=(S // ROWS,),
        in_specs=[_col_spec(ROWS, D, 0), _col_spec(ROWS, D, col)],
        out_specs=_col_spec(ROWS, D, 0), out_shape=jax.ShapeDtypeStruct((S, D), BF16),
        compiler_params=_params(1))(o, proj)


def _gate_bwd(dy, o, proj, col, name):
    def body(dy_ref, o_ref, g_ref, do_ref, dg_ref):
        g, dyv = g_ref[...], dy_ref[...]
        sg = _sigmoid(g)
        do_ref[...] = dyv * (g * sg)
        dg_ref[...] = (dyv * o_ref[...] * (sg * (1.0 + g * (1.0 - sg)))).astype(BF16)

    return pl.pallas_call(
        body, name=name, grid=(S // ROWS,),
        in_specs=[_col_spec(ROWS, D, 0), _col_spec(ROWS, D, 0), _col_spec(ROWS, D, col)],
        out_specs=[_col_spec(ROWS, D, 0)] * 2,
        out_shape=[jax.ShapeDtypeStruct((S, D), F32), jax.ShapeDtypeStruct((S, D), BF16)],
        compiler_params=_params(1))(dy, o, proj)


def _headnorm_bwd(x, col, gain, dy, rope, name):
    e, et = _seg_mats(D)
    tabs = list(rope) if rope is not None else []

    def body(*refs):
        x_ref, g_ref, dy_ref, e_ref, et_ref = refs[:5]
        dx_ref, dg_ref = refs[-2:]
        xv, dyv, ev, etv = x_ref[...], dy_ref[...], e_ref[...], et_ref[...]
        if rope is not None:
            c, a, b = (jnp.tile(t[...], (1, D // LANES)) for t in refs[5:8])
            dyv = _rope_t(dyv, c, a, b)
        r = _head_rstd(xv, ev, etv)
        xh = xv * r
        part = jnp.sum(dyv * xh, axis=0, keepdims=True)

        @pl.when(pl.program_id(0) == 0)
        def _():
            dg_ref[...] = part

        @pl.when(pl.program_id(0) != 0)
        def _():
            dg_ref[...] += part

        gy = dyv * g_ref[...]
        seg = _dot_split(_dot_split(gy * xh, ev, 2) * (1.0 / HD), etv, 3)
        dx_ref[...] = (r * (gy - xh * seg)).astype(BF16)

    whole = lambda a: pl.BlockSpec(a.shape, lambda i: (0, 0))
    return pl.pallas_call(
        body, name=name, grid=(S // ROWS,),
        in_specs=[_col_spec(ROWS, D, col), whole(gain), _col_spec(ROWS, D, 0), whole(e), whole(et)]
                 + [pl.BlockSpec((ROWS, LANES), lambda i: (i, 0))] * len(tabs),
        out_specs=[_col_spec(ROWS, D, 0), whole(gain)],
        out_shape=[jax.ShapeDtypeStruct((S, D), BF16), jax.ShapeDtypeStruct((1, D), F32)],
        compiler_params=_params(1))(x, gain, dy, e, et, *tabs)


def _dup_mat():
    r, c = np.arange(KVW)[:, None], np.arange(2 * KVW)[None, :]
    return (r // HD == c // LANES) & (r % HD == c % HD)


def _fold_mat():
    r, c = np.arange(D)[:, None], np.arange(KVW)[None, :]
    return (r // (2 * LANES) == c // HD) & (r % HD == c % HD)


def _b_post(pb, kv, qg, kg, rope):
    e, et = _seg_mats(D)
    ek, etk = _seg_mats(KVW)
    dup = jnp.asarray(_dup_mat(), BF16)

    def body(q_ref, k_ref, v_ref, qg_ref, kg_ref, e_ref, et_ref, ek_ref, etk_ref, dup_ref, c_ref, a_ref, b_ref,
             qo, ko, vo):
        c1, a1, b1 = c_ref[...], a_ref[...], b_ref[...]
        qv = q_ref[...]
        qn = qv * _head_rstd(qv, e_ref[...], et_ref[...]) * qg_ref[...]
        t = lambda z, n: jnp.tile(z, (1, n))
        qo[...] = (_rope(qn, t(c1, D // LANES), t(a1, D // LANES), t(b1, D // LANES)) * SCALE).astype(BF16)
        kvv = k_ref[...]
        kn = kvv * _head_rstd(kvv, ek_ref[...], etk_ref[...]) * kg_ref[...]
        kr = _rope(kn, t(c1, KVW // LANES), t(a1, KVW // LANES), t(b1, KVW // LANES)).astype(BF16)
        ko[...] = _dot(kr, dup_ref[...]).astype(BF16)
        vo[...] = _dot(v_ref[...].astype(BF16), dup_ref[...]).astype(BF16)

    whole = lambda a: pl.BlockSpec(a.shape, lambda i: (0, 0))
    tab = pl.BlockSpec((ROWS, LANES), lambda i: (i, 0))
    return pl.pallas_call(
        body, name="b_post", grid=(S // ROWS,),
        in_specs=[_col_spec(ROWS, D, 0), _col_spec(ROWS, KVW, 0), _col_spec(ROWS, KVW, 1),
                  whole(qg), whole(kg), whole(e), whole(et), whole(ek), whole(etk), whole(dup), tab, tab, tab],
        out_specs=[_col_spec(ROWS, D, 0), _col_spec(ROWS, 2 * KVW, 0), _col_spec(ROWS, 2 * KVW, 0)],
        out_shape=[jax.ShapeDtypeStruct((S, D), BF16), jax.ShapeDtypeStruct((S, 2 * KVW), BF16),
                   jax.ShapeDtypeStruct((S, 2 * KVW), BF16)],
        compiler_params=_params(1))(pb, kv, kv, qg, kg, e, et, ek, etk, dup, *rope)


def _kv_bwd(dkdup, dvdup, kv, kg, rope):
    ek, etk = _seg_mats(KVW)
    fold = jnp.asarray(_fold_mat(), BF16)

    def body(dk_ref, dv_ref, k_ref, kg_ref, ek_ref, etk_ref, fold_ref, c_ref, a_ref, b_ref, dkv_ref, dg_ref):
        ev, etv, fv = ek_ref[...], etk_ref[...], fold_ref[...]
        t = lambda z: jnp.tile(z[...], (1, KVW // LANES))
        dk = _rope_t(_dot_split(dk_ref[...], fv, 3), t(c_ref), t(a_ref), t(b_ref))
        dv = _dot_split(dv_ref[...], fv, 3)
        xv = k_ref[...]
        r = _head_rstd(xv, ev, etv)
        xh = xv * r
        part = jnp.sum(dk * xh, axis=0, keepdims=True)

        @pl.when(pl.program_id(0) == 0)
        def _():
            dg_ref[...] = part

        @pl.when(pl.program_id(0) != 0)
        def _():
            dg_ref[...] += part

        gy = dk * kg_ref[...]
        seg = _dot_split(_dot_split(gy * xh, ev, 2) * (1.0 / HD), etv, 3)
        dkv_ref[:, 0:KVW] = (r * (gy - xh * seg)).astype(BF16)
        dkv_ref[:, KVW:2 * KVW] = dv.astype(BF16)

    whole = lambda a: pl.BlockSpec(a.shape, lambda i: (0, 0))
    tab = pl.BlockSpec((ROWS, LANES), lambda i: (i, 0))
    return pl.pallas_call(
        body, name="kv_bwd", grid=(S // ROWS,),
        in_specs=[_col_spec(ROWS, D, 0), _col_spec(ROWS, D, 0), _col_spec(ROWS, KVW, 0),
                  whole(kg), whole(ek), whole(etk), whole(fold), tab, tab, tab],
        out_specs=[_col_spec(ROWS, 2 * KVW, 0), whole(kg)],
        out_shape=[jax.ShapeDtypeStruct((S, 2 * KVW), BF16), jax.ShapeDtypeStruct((1, KVW), F32)],
        compiler_params=_params(1))(dkdup, dvdup, kv, kg, ek, etk, fold, *rope)


def _loss_head(out, target):
    def body(o_ref, t_ref, d_ref, db_ref, l_ref):
        diff = o_ref[...] - t_ref[...]
        d = diff * (1.0 / D)
        d_ref[...] = d
        db_ref[...] = d.astype(BF16)

        @pl.when(pl.program_id(0) == 0)
        def _():
            l_ref[...] = jnp.zeros_like(l_ref)

        l_ref[...] += jnp.sum(diff * diff, axis=0, keepdims=True)

    return _rows_call(body, "loss_head", [out, target], [((S, D), F32), ((S, D), BF16), ((1, D), F32)])


def _lane():
    return lax.broadcasted_iota(jnp.int32, (1, LANES), 1)


def _head_mask(hh):
    return (_lane() < HD) if hh == 0 else (_lane() >= HD)


def _fox_fwd(q, k, v, ct):
    nq = S // ATT

    def body(q_ref, k_ref, v_ref, c_ref, o_ref, lse_ref):
        i = pl.program_id(1)
        q2 = q_ref[...]
        qms = [jnp.where(_head_mask(hh), q2, jnp.zeros_like(q2)) for hh in (0, 1)]

        def step(off, width, carry, diag):
            off = pl.multiple_of(off, ATT)
            kj, vj = k_ref[pl.ds(off, width), :], v_ref[pl.ds(off, width), :]
            out = []
            for hh in (0, 1):
                m, acc = carry[hh]
                s = _dot(qms[hh], kj, NT) - c_ref[hh:hh + 1, pl.ds(off, width)]
                if diag:
                    row = i * ATT + lax.broadcasted_iota(jnp.int32, (ATT, width), 0)
                    col = off + lax.broadcasted_iota(jnp.int32, (ATT, width), 1)
                    s = jnp.where(col <= row, s, NEG)
                m_new = jnp.maximum(m, jnp.max(s, axis=1, keepdims=True))
                p = jnp.exp(s - m_new)
                p_hi = p.astype(BF16)
                p_lo = (p - p_hi.astype(F32)).astype(BF16)
                v1 = jnp.where(_head_mask(hh), vj, jnp.ones_like(vj))
                acc = jnp.exp(m - m_new) * acc + (_dot(p_hi, v1) + _dot(p_lo, v1))
                out.append((m_new, acc))
            return tuple(out)

        one = (jnp.full((ATT, 1), NEG, F32), jnp.zeros((ATT, LANES), F32))
        carry = lax.fori_loop(0, i // 2, lambda j, cr: step(j * (2 * ATT), 2 * ATT, cr, False), (one, one))
        carry = lax.cond(i % 2 == 1, lambda cr: step((i - 1) * ATT, 2 * ATT, cr, True),
                         lambda cr: step(i * ATT, ATT, cr, True), carry)
        res = []
        for hh in (0, 1):
            m, acc = carry[hh]
            l = jnp.max(jnp.where(_head_mask(1 - hh), acc, 0.0), axis=1, keepdims=True)
            res.append((acc / l, m + jnp.log(l)))
        first = _head_mask(0)
        o_ref[...] = jnp.where(first, res[0][0], res[1][0])
        lse_ref[...] = jnp.where(first, res[0][1], res[1][1])

    blk = pl.BlockSpec((ATT, LANES), lambda p, i: (i, p))
    full = pl.BlockSpec((S, LANES), lambda p, i: (0, p))
    return pl.pallas_call(
        body, name="fox_fwd", grid=(NH // 2, nq),
        in_specs=[blk, full, full, pl.BlockSpec((None, 2, S), lambda p, i: (p, 0, 0))],
        out_specs=[blk, blk],
        out_shape=[jax.ShapeDtypeStruct((S, D), F32)] * 2,
        compiler_params=_params(2))(q, k, v, ct)


def _fox_bwd(q, k, v, ct, o, lse, do):
    nq = S // ATT

    def body(q_ref, k_ref, v_ref, c_ref, o_ref, lse_ref, do_ref, dq_ref, dk_ref, dvb_ref, dc_ref, dv_ref):
        i = pl.program_id(1)

        @pl.when(i == 0)
        def _():
            dk_ref[...] = jnp.zeros_like(dk_ref)
            dv_ref[...] = jnp.zeros_like(dv_ref)
            dc_ref[...] = jnp.zeros_like(dc_ref)

        q2, do2, lse2 = q_ref[...], do_ref[...], lse_ref[...]
        do2b = do2.astype(BF16)
        prod = do2b.astype(F32) * o_ref[...]
        heads = []
        for hh in (0, 1):
            hm = _head_mask(hh)
            heads.append((jnp.where(hm, q2, jnp.zeros_like(q2)), jnp.where(hm, do2b, jnp.zeros_like(do2b)),
                          jnp.sum(jnp.where(hm, prod, 0.0), axis=1, keepdims=True),
                          jnp.max(jnp.where(hm, lse2, NEG), axis=1, keepdims=True)))

        def step(off, width, dqs, diag):
            off = pl.multiple_of(off, ATT)
            kj, vj = k_ref[pl.ds(off, width), :], v_ref[pl.ds(off, width), :]
            dk, dv, out = None, None, []
            for hh in (0, 1):
                qm, dom, delta, lse_h = heads[hh]
                s = _dot(qm, kj, NT) - c_ref[hh:hh + 1, pl.ds(off, width)]
                p = jnp.exp(s - lse_h)
                if diag:
                    row = i * ATT + lax.broadcasted_iota(jnp.int32, (ATT, width), 0)
                    col = off + lax.broadcasted_iota(jnp.int32, (ATT, width), 1)
                    p = jnp.where(col <= row, p, 0.0)
                ds = p * (_dot(dom, vj, NT) - delta)
                dc_ref[hh:hh + 1, pl.ds(off, width)] += -jnp.sum(ds, axis=0, keepdims=True)
                dsb = ds.astype(BF16)
                dk_h, dv_h = _dot(dsb, qm, TN), _dot(p.astype(BF16), dom, TN)
                dk, dv = (dk_h, dv_h) if dk is None else (dk + dk_h, dv + dv_h)
                out.append(dqs[hh] + _dot(dsb, kj))
            dk_ref[pl.ds(off, width), :] += dk
            dv_ref[pl.ds(off, width), :] += dv
            return tuple(out)

        zero = jnp.zeros((ATT, LANES), F32)
        dqs = lax.fori_loop(0, i // 2, lambda j, acc: step(j * (2 * ATT), 2 * ATT, acc, False), (zero, zero))
        dqs = lax.cond(i % 2 == 1, lambda acc: step((i - 1) * ATT, 2 * ATT, acc, True),
                       lambda acc: step(i * ATT, ATT, acc, True), dqs)
        dq_ref[...] = jnp.where(_head_mask(0), dqs[0], dqs[1]) * SCALE

        @pl.when(i == nq - 1)
        def _():
            dvb_ref[...] = dv_ref[...].astype(BF16)

    blk = pl.BlockSpec((ATT, LANES), lambda p, i: (i, p))
    full = pl.BlockSpec((S, LANES), lambda p, i: (0, p))
    cspec = pl.BlockSpec((None, 2, S), lambda p, i: (p, 0, 0))
    return pl.pallas_call(
        body, name="fox_bwd", grid=(NH // 2, nq),
        in_specs=[blk, full, full, cspec, blk, blk, blk],
        out_specs=[blk, full, full, cspec],
        out_shape=[jax.ShapeDtypeStruct((S, D), F32)] * 2 + [jax.ShapeDtypeStruct((S, D), BF16),
                                                              jax.ShapeDtypeStruct((NH // 2, 2, S), F32)],
        scratch_shapes=[pltpu.VMEM((S, LANES), F32)],
        compiler_params=_params(2))(q, k, v, ct, o, lse, do)


def _swa_logits(qm, kk, i, start):
    s = _dot(qm, kk, NT)
    qabs = i * WINDOW + lax.broadcasted_iota(jnp.int32, (WINDOW, 2 * WINDOW), 0)
    kabs = start + lax.broadcasted_iota(jnp.int32, (WINDOW, 2 * WINDOW), 1)
    valid = (kabs <= qabs) & (qabs - kabs < WINDOW)
    return s, valid


def _swa_fwd(q, kdup, vdup, sinks_t):
    def body(q_ref, k_ref, v_ref, sk_ref, o_ref, lse_ref):
        skv = sk_ref[...]
        first = _head_mask(0)
        for sb in range(SWQ):
            i = pl.program_id(1) * SWQ + sb
            rows = slice(sb * WINDOW, (sb + 1) * WINDOW)
            start = pl.multiple_of(jnp.maximum(i - 1, 0) * WINDOW, WINDOW)
            kk, vv = k_ref[pl.ds(start, 2 * WINDOW), :], v_ref[pl.ds(start, 2 * WINDOW), :]
            q2 = q_ref[rows, :]
            res = []
            for hh in (0, 1):
                hm = _head_mask(hh)
                qm = jnp.where(hm, q2, jnp.zeros_like(q2))
                sink = jnp.max(jnp.where(hm, skv, NEG), axis=1, keepdims=True)
                s, valid = _swa_logits(qm, kk, i, start)
                s = jnp.where(valid, s, NEG)
                m = jnp.maximum(jnp.max(s, axis=1, keepdims=True), sink)
                p = jnp.exp(s - m)
                l = jnp.sum(p, axis=1, keepdims=True) + jnp.exp(sink - m)
                res.append((_dot(p.astype(BF16), vv) / l, m + jnp.log(l)))
            o_ref[rows, :] = jnp.where(first, res[0][0], res[1][0])
            lse_ref[rows, :] = jnp.where(first, res[0][1], res[1][1])

    blk = pl.BlockSpec((SWQ * WINDOW, LANES), lambda p, i: (i, p))
    full = pl.BlockSpec((S, LANES), lambda p, i: (0, p // 2))
    return pl.pallas_call(
        body, name="swa_fwd", grid=(NH // 2, S // (SWQ * WINDOW)),
        in_specs=[blk, full, full, pl.BlockSpec((1, LANES), lambda p, i: (0, p))],
        out_specs=[blk, blk],
        out_shape=[jax.ShapeDtypeStruct((S, D), F32)] * 2,
        compiler_params=_params(2))(q, kdup, vdup, sinks_t)


def _swa_bwd(q, kdup, vdup, sinks_t, o, lse, do):
    def body(q_ref, k_ref, v_ref, sk_ref, o_ref, lse_ref, do_ref, dq_ref, dk_ref, dv_ref, dsk_ref):
        @pl.when(pl.program_id(1) == 0)
        def _():
            dk_ref[...] = jnp.zeros_like(dk_ref)
            dv_ref[...] = jnp.zeros_like(dv_ref)
            dsk_ref[...] = jnp.zeros_like(dsk_ref)

        skv = sk_ref[...]
        first = _head_mask(0)
        for sb in range(SWQ):
            i = pl.program_id(1) * SWQ + sb
            rows = slice(sb * WINDOW, (sb + 1) * WINDOW)
            start = pl.multiple_of(jnp.maximum(i - 1, 0) * WINDOW, WINDOW)
            kk, vv = k_ref[pl.ds(start, 2 * WINDOW), :], v_ref[pl.ds(start, 2 * WINDOW), :]
            q2, do2, lse2 = q_ref[rows, :], do_ref[rows, :], lse_ref[rows, :]
            do2b = do2.astype(BF16)
            prod = do2b.astype(F32) * o_ref[rows, :]
            dqs, dsk, dk, dv = [], [], None, None
            for hh in (0, 1):
                hm = _head_mask(hh)
                qm = jnp.where(hm, q2, jnp.zeros_like(q2))
                dom = jnp.where(hm, do2b, jnp.zeros_like(do2b))
                delta = jnp.sum(jnp.where(hm, prod, 0.0), axis=1, keepdims=True)
                lse_h = jnp.max(jnp.where(hm, lse2, NEG), axis=1, keepdims=True)
                sink = jnp.max(jnp.where(hm, skv, NEG), axis=1, keepdims=True)
                s, valid = _swa_logits(qm, kk, i, start)
                p = jnp.where(valid, jnp.exp(s - lse_h), 0.0)
                ds = p * (_dot(dom, vv, NT) - delta)
                dsb = ds.astype(BF16)
                dk_h, dv_h = _dot(dsb, qm, TN), _dot(p.astype(BF16), dom, TN)
                dk, dv = (dk_h, dv_h) if dk is None else (dk + dk_h, dv + dv_h)
                dqs.append(_dot(dsb, kk))
                dsk.append(-jnp.sum(jnp.exp(sink - lse_h) * delta, axis=0, keepdims=True))
            dk_ref[pl.ds(start, 2 * WINDOW), :] += dk
            dv_ref[pl.ds(start, 2 * WINDOW), :] += dv
            dq_ref[rows, :] = jnp.where(first, dqs[0], dqs[1]) * SCALE
            dsk_ref[...] += jnp.where(first, dsk[0], dsk[1])

    blk = pl.BlockSpec((SWQ * WINDOW, LANES), lambda p, i: (i, p))
    full = pl.BlockSpec((S, LANES), lambda p, i: (0, p // 2))
    acc = pl.BlockSpec((S, LANES), lambda p, i: (0, p))
    sk = pl.BlockSpec((1, LANES), lambda p, i: (0, p))
    return pl.pallas_call(
        body, name="swa_bwd", grid=(NH // 2, S // (SWQ * WINDOW)),
        in_specs=[blk, full, full, sk, blk, blk, blk],
        out_specs=[blk, acc, acc, sk],
        out_shape=[jax.ShapeDtypeStruct((S, D), F32)] * 3 + [jax.ShapeDtypeStruct((1, D), F32)],
        compiler_params=_params(2))(q, kdup, vdup, sinks_t, o, lse, do)


def _adamw_math(w, g, m, v):
    m = ADAM_B1 * m + (1.0 - ADAM_B1) * g
    v = ADAM_B2 * v + (1.0 - ADAM_B2) * jnp.square(g)
    m_hat = m / (1.0 - ADAM_B1 ** ADAM_STEP)
    v_hat = v / (1.0 - ADAM_B2 ** ADAM_STEP)
    delta = -ADAM_LR * (m_hat / (jnp.sqrt(v_hat) + ADAM_EPS) + ADAM_WD * w)
    return delta, m, v


def _adamw(w, g, m, v, name):
    r, c = w.shape
    tr = min(r, 128)

    def body(w_ref, g_ref, m_ref, v_ref, d_ref, mo_ref, vo_ref):
        d_ref[...], mo_ref[...], vo_ref[...] = _adamw_math(w_ref[...], g_ref[...], m_ref[...], v_ref[...])

    spec = pl.BlockSpec((tr, c), lambda i: (i, 0))
    return pl.pallas_call(
        body, name=name, grid=(r // tr,), in_specs=[spec] * 4, out_specs=[spec] * 3,
        out_shape=[jax.ShapeDtypeStruct((r, c), F32)] * 3, compiler_params=_params(1))(w, g, m, v)


SUM_TILE = 128


def _tiles(shape2d, axis, lead=0):
    r, c = shape2d
    blk = (SUM_TILE, c) if axis == 0 else (r, SUM_TILE)
    count = shape2d[axis] // SUM_TILE

    def index(pos, *lead_idx):
        return tuple(lead_idx) + ((pos, 0) if axis == 0 else (0, pos))

    return (None,) * lead + blk, count, index


def _adamw_halves(w, g_mine, g_theirs, m, v, axis, name):
    blk, count, index = _tiles(w.shape, axis)
    per_half = count // 2

    def body(w_ref, a_ref, b_ref, m_ref, v_ref, g_ref, d_ref, mo_ref, vo_ref):
        is_mine = pl.program_id(0) // per_half == lax.axis_index("c")
        g = jnp.where(is_mine, a_ref[...], b_ref[...])
        g_ref[...] = g
        d_ref[...], mo_ref[...], vo_ref[...] = _adamw_math(w_ref[...], g, m_ref[...], v_ref[...])

    spec = pl.BlockSpec(blk, lambda i: index(i))
    half = pl.BlockSpec(blk, lambda i: index(i % per_half))
    return pl.pallas_call(
        body, name=name, grid=(count,), in_specs=[spec, half, half, spec, spec], out_specs=[spec] * 4,
        out_shape=[jax.ShapeDtypeStruct(w.shape, F32)] * 4, compiler_params=_params(1))(w, g_mine, g_theirs, m, v)


def _chip_sum(blocks, from_sibling, axis, name):
    half2d = from_sibling.shape[1:]
    blk, count, index = _tiles(half2d, axis, lead=1)

    def body(lo_ref, hi_ref, p_ref, o32, o16):
        mine = jnp.where(lax.axis_index("c") == 0, lo_ref[...], hi_ref[...])
        acc = mine + p_ref[...]
        o32[...] = acc
        o16[...] = acc.astype(BF16)

    lo = pl.BlockSpec(blk, lambda k, i: index(i, k))
    hi = pl.BlockSpec(blk, lambda k, i: index(i + count, k))
    return pl.pallas_call(
        body, name=name, grid=(NCHIP, count), in_specs=[lo, hi, lo], out_specs=[lo, lo],
        out_shape=[jax.ShapeDtypeStruct(from_sibling.shape, F32), jax.ShapeDtypeStruct(from_sibling.shape, BF16)],
        compiler_params=_params(2))(blocks, blocks, from_sibling)


def _mesh_sum(own, parts, axis, name):
    blk, count, index = _tiles(own.shape, axis)
    n = parts.shape[0]

    def body(a_ref, p_ref, o_ref):
        acc = a_ref[...]
        for k in range(n):
            acc = acc + p_ref[k].astype(F32)
        o_ref[...] = acc

    spec = pl.BlockSpec(blk, lambda i: index(i))
    return pl.pallas_call(
        body, name=name, grid=(count,),
        in_specs=[spec, pl.BlockSpec((n,) + blk, lambda i: index(i, 0))],
        out_specs=spec, out_shape=jax.ShapeDtypeStruct(own.shape, F32),
        compiler_params=_params(1))(own, parts)


def _sum_stack(parts, name):
    n = parts.shape[0]

    def body(p_ref, o_ref):
        acc = p_ref[0]
        for k in range(1, n):
            acc = acc + p_ref[k]
        o_ref[...] = acc

    return pl.pallas_call(body, name=name, out_shape=jax.ShapeDtypeStruct(parts.shape[1:], F32))(parts)


def _coords():
    return lax.axis_index("x"), lax.axis_index("y"), lax.axis_index("c")


def _chip(who):
    return 2 * who[0] + who[1]


def _flip(who, mask):
    return tuple((1 - v) if b else v for v, b in zip(who, mask))


def _exchange(name, ins, outs, transfers, copies=()):
    ni, no = len(ins), len(outs)
    nt = len(transfers)

    def body(*refs):
        I, O = refs[:ni], refs[ni:ni + no]
        ssem, rsem, lsem = refs[ni + no:]
        me = _coords()
        local = [pltpu.make_async_copy(s(I, O, me), d(I, O, me), lsem.at[n]) for n, (s, d) in enumerate(copies)]
        for cp in local:
            cp.start()
        sends, recvs, arrived = [], [], set()
        for t, tr in enumerate(transfers):
            peer = _flip(me, tr["mask"])

            def make(who, t=t, tr=tr, peer=peer):
                return pltpu.make_async_remote_copy(
                    src_ref=tr["src"](I, O, me), dst_ref=tr["dst"](I, O, who),
                    send_sem=ssem.at[t], recv_sem=rsem.at[t], device_id=peer, device_id_type=MESH)

            after = tr.get("after")
            if after is not None and after not in arrived:
                recvs[after].wait_recv()
                arrived.add(after)
            snd = make(me)
            snd.start()
            sends.append(snd)
            recvs.append(make(peer))
        for t in range(nt):
            if t not in arrived:
                recvs[t].wait_recv()
        for snd in sends:
            snd.wait_send()
        for cp in local:
            cp.wait()

    hbm = pl.BlockSpec(memory_space=pltpu.HBM)
    return pl.pallas_call(
        body, name=name, in_specs=[hbm] * ni, out_specs=[hbm] * no,
        out_shape=[jax.ShapeDtypeStruct(s, d) for s, d in outs],
        scratch_shapes=[pltpu.SemaphoreType.DMA((nt,)), pltpu.SemaphoreType.DMA((nt,)),
                        pltpu.SemaphoreType.DMA((max(len(copies), 1),))],
        compiler_params=pltpu.CompilerParams(has_side_effects=True))(*ins)


CHIP_MASKS = [(0, 1, 0), (1, 0, 0), (1, 1, 0)]
SIBLING = (0, 0, 1)


def _half(shape2d, axis, which):
    n = shape2d[axis] // 2
    cut = pl.ds(pl.multiple_of(which * n, n), n)
    return (cut, slice(None)) if axis == 0 else (slice(None), cut)


def _gather_shards(shards, axes):
    def half(a, who):
        return _half(shards[a].shape, axes[a], who[2])

    over_ici, onward = [], []
    for a in range(len(shards)):
        for mask in CHIP_MASKS:
            over_ici.append(dict(
                mask=mask,
                src=lambda I, O, me, a=a: I[a].at[half(a, me)],
                dst=lambda I, O, who, a=a: O[a].at[(_chip(who),) + half(a, who)]))
            onward.append(dict(
                mask=SIBLING, after=len(over_ici) - 1,
                src=lambda I, O, me, a=a, mask=mask: O[a].at[(_chip(_flip(me, mask)),) + half(a, me)],
                dst=lambda I, O, who, a=a, mask=mask: O[a].at[(_chip(_flip(who, mask)),) + half(a, who)]))
    transfers = over_ici + onward
    outs = [((NCHIP,) + s.shape, s.dtype) for s in shards]
    gathered = _exchange("gather_weights", shards, outs, transfers)
    mine = lax.broadcasted_iota(jnp.int32, (NCHIP, 1, 1), 0) == _chip(_coords())
    return [jnp.where(mine, s[None], t) for s, t in zip(shards, gathered)]


def _to_sibling(arrs, name):
    transfers = [dict(mask=SIBLING, src=lambda I, O, me, a=a: I[a], dst=lambda I, O, who, a=a: O[a])
                 for a in range(len(arrs))]
    return _exchange(name, arrs, [(t.shape, t.dtype) for t in arrs], transfers)


def _halves_to_sibling(blocks, axes):
    def cut(a, which):
        return (slice(None),) + _half(blocks[a].shape[1:], axes[a], which)

    transfers = [dict(mask=SIBLING, src=lambda I, O, me, a=a: I[a].at[cut(a, 1 - me[2])],
                      dst=lambda I, O, who, a=a: O[a]) for a in range(len(blocks))]
    outs = []
    for b, ax in zip(blocks, axes):
        shape = list(b.shape)
        shape[ax + 1] //= 2
        outs.append((tuple(shape), b.dtype))
    return _exchange("sibling_halves", blocks, outs, transfers)


def _scatter_chip_sums(tb):
    transfers = []
    for a in range(len(tb)):
        for n, mask in enumerate(CHIP_MASKS):
            transfers.append(dict(
                mask=mask,
                src=lambda I, O, me, a=a, mask=mask: I[a].at[_chip(_flip(me, mask))],
                dst=lambda I, O, who, a=a, n=n: O[a].at[n]))
    outs = [((3,) + t.shape[1:], t.dtype) for t in tb]
    return _exchange("scatter_grads", tb, outs, transfers)


def _gather_small(vec):
    def slot(who):
        return 4 * who[0] + 2 * who[1] + who[2]

    masks = [(m >> 2 & 1, m >> 1 & 1, m & 1) for m in range(1, 8)]
    transfers = [dict(mask=mask, src=lambda I, O, me: I[0], dst=lambda I, O, who: O[0].at[slot(who)])
                 for mask in masks]
    copies = [(lambda I, O, me: I[0], lambda I, O, me: O[0].at[slot(me)])]
    return _exchange("gather_small", [vec], [((8,) + vec.shape, vec.dtype)], transfers, copies)[0]


def _rope_tables(positions):
    half = ROT // 2
    inv_freq = jnp.power(jnp.float32(THETA), -jnp.arange(0, ROT, 2, dtype=F32) / ROT)
    ang = positions.astype(F32)[:, None] * inv_freq[None, :]
    cos, sin = jnp.cos(ang), jnp.sin(ang)
    one, zero, z8 = jnp.ones((S, HD - ROT), F32), jnp.zeros((S, HD - ROT), F32), jnp.zeros((S, half), F32)
    c = jnp.concatenate([cos, cos, one], axis=1)
    a = jnp.concatenate([-sin, z8, zero], axis=1)
    b = jnp.concatenate([z8, sin, zero], axis=1)
    return tuple(jnp.tile(t, (1, 2)) for t in (c, a, b))


def _tile_heads(g, w):
    return jnp.tile(g.reshape(1, HD), (1, w // HD))


def _fold_heads(dg):
    return dg.reshape(-1, HD).sum(axis=0)


def _pad_lanes(a):
    return jnp.pad(a, ((0, 0), (0, LANES - a.shape[1])))


def _local_step(x, target, positions, wt):
    rope = _rope_tables(positions)
    w1t, w_in_b = wt["w_in_a_t"], wt["w_in_b"]
    f_row = 3 * D // LANES
    wg_t = w1t[3 * D + NH:]
    in_b_block = lambda c: pl.BlockSpec((None, TN_, TN_), lambda j, i: (c, j, 0))
    b_pad = _pad_lanes(wt["b_forget"].reshape(1, NH))
    qg_a, kg_a = _tile_heads(wt["qnorm_a_g"], D), _tile_heads(wt["knorm_a_g"], D)
    qg_b, kg_b = _tile_heads(wt["qnorm_b_g"], D), _tile_heads(wt["knorm_b_g"], KVW)
    norm_a, kv_g, norm_b = wt["norm_a_g"].reshape(1, D), wt["kv_norm_g"].reshape(1, D), wt["norm_b_g"].reshape(1, D)
    sinks_t = jnp.repeat(wt["sinks"].reshape(1, NH), HD, axis=1)

    (u_a,) = _rmsnorm_fwd(x, [norm_a], "norm_a")
    qkv = _mm("proj_a", S, 3 * D, [(u_a, _a_rows(D), w1t, _b_rows(D), NT)])
    fpad = _mm("proj_f", S, LANES, [(u_a, _a_rows(D), w1t, _b_rows(D, row0=f_row, tn=LANES), NT)], tn=LANES)
    gate_a = _mm("proj_gate_a", S, D, [(u_a, _a_rows(D), wg_t, _b_rows(D), NT)])
    q_a, k_a, v_a = _a_post(qkv, qg_a, kg_a)
    ct = _forget_cumsum(fpad, b_pad)
    ct2 = ct[:NH].reshape(NH // 2, 2, S)
    o_a, lse_a = _fox_fwd(q_a, k_a, v_a, ct2)
    y_a = _gate_fwd(o_a, gate_a, 0, "gate_a")
    h1 = _mm("out_a", S, D, [(y_a, _a_rows(D), wt["w_out_a"], _b_cols(D), None)], add=x)
    u_kv, u_b = _rmsnorm_fwd(h1, [kv_g, norm_b], "norm_b")
    kv = _mm("proj_kv", S, 2 * KVW, [(u_kv, _a_rows(D), wt["w_kv"], _b_cols(D), None)])
    pb = _mm("proj_b", S, 2 * D,
             [(u_b, _a_rows(D), w_in_b, pl.BlockSpec((None, D, TN_), lambda j, i: (j, 0, 0)), None)])
    q_b, kdup, vdup = _b_post(pb, kv, qg_b, kg_b, rope)
    o_b, lse_b = _swa_fwd(q_b, kdup, vdup, sinks_t)
    y_b = _gate_fwd(o_b, pb, 1, "gate_b")
    out = _mm("out_b", S, D, [(y_b, _a_rows(D), wt["w_out_b"], _b_cols(D), None)], add=h1)
    d_out, d_out_b, sq = _loss_head(out, target)

    g = {}
    g["w_out_b"] = _mm("dw_out_b", D, D, [(y_b, _a_cols(S), d_out_b, _b_cols(S), TN)])
    d_y_b = _mm("dy_b", S, D, [(d_out_b, _a_rows(D), wt["w_out_b"], _b_rows(D), NT)])
    d_o_b, d_gate_b = _gate_bwd(d_y_b, o_b, pb, 1, "gate_b_bwd")
    dq_b, dkdup, dvdup, dsk = _swa_bwd(q_b, kdup, vdup, sinks_t, o_b, lse_b, d_o_b)
    g["sinks"] = dsk[0, ::HD]
    d_qb_raw, dg = _headnorm_bwd(pb, 0, qg_b, dq_b, rope, "qnorm_b_bwd")
    g["qnorm_b_g"] = _fold_heads(dg)
    d_pb = [d_qb_raw, d_qb_raw, d_gate_b, d_gate_b]
    g["w_in_b"] = jnp.concatenate([
        _mm("dw_in_b_q", D, D, [(u_b, _a_cols(S), d_qb_raw, _b_cols(S), TN)], stacked=True),
        _mm("dw_in_b_gate", D, D, [(u_b, _a_cols(S), d_gate_b, _b_cols(S), TN)], stacked=True)], axis=0)
    d_u_b = _mm("du_b", S, D, [(d_pb[c], _a_rows(TN_, col=c % 2), w_in_b, in_b_block(c), NT) for c in range(NCHIP)])
    d_kv, dg = _kv_bwd(dkdup, dvdup, kv, kg_b, rope)
    g["knorm_b_g"] = _fold_heads(dg)
    g["w_kv"] = _mm("dw_kv", D, 2 * KVW, [(u_kv, _a_cols(S), d_kv, _b_cols(S), TN)])
    d_u_kv = _mm("du_kv", S, D, [(d_kv, _a_rows(2 * KVW), wt["w_kv"], _b_rows(2 * KVW), NT)])
    d_h1, d_h1_b, g["kv_norm_g"], g["norm_b_g"] = _rmsnorm_bwd(h1, [kv_g, norm_b], [d_u_kv, d_u_b], d_out, "norm_b_bwd")
    g["w_out_a"] = _mm("dw_out_a", D, D, [(y_a, _a_cols(S), d_h1_b, _b_cols(S), TN)])
    d_y_a = _mm("dy_a", S, D, [(d_h1_b, _a_rows(D), wt["w_out_a"], _b_rows(D), NT)])
    d_o_a, d_gate_a = _gate_bwd(d_y_a, o_a, gate_a, 0, "gate_a_bwd")
    dq_a, dk_a, dv_a, dct = _fox_bwd(q_a, k_a, v_a, ct2, o_a, lse_a, d_o_a)
    dct_pad = jnp.pad(dct.reshape(NH, S), ((0, LANES - NH), (0, 0)))
    d_f, db = _forget_bwd(dct_pad, fpad, b_pad)
    g["b_forget"] = db[0, :NH]
    d_q_raw, dg = _headnorm_bwd(qkv, 0, qg_a, dq_a, None, "qnorm_a_bwd")
    g["qnorm_a_g"] = _fold_heads(dg)
    d_k_raw, dg = _headnorm_bwd(qkv, 1, kg_a, dk_a, None, "knorm_a_bwd")
    g["knorm_a_g"] = _fold_heads(dg)
    pieces = [("q", d_q_raw), ("k", d_k_raw), ("v", dv_a), ("gate", d_gate_a)]
    dw = {n: _mm("dw_in_a_" + n, D, D, [(t, _a_cols(S), u_a, _b_cols(S), TN)]) for n, t in pieces}
    dw_f = _mm("dw_in_a_f", LANES, D, [(d_f, _a_cols(S, tm=LANES), u_a, _b_cols(S), TN)], tm=LANES)
    g["w_in_a"] = jnp.concatenate([dw["q"], dw["k"], dw["v"], dw_f[:NH], dw["gate"]], axis=0).reshape(
        NCHIP, (4 * D + NH) // NCHIP, D)
    d_u_a = _mm("du_a", S, D, [
        (d_q_raw, _a_rows(D), w1t, _b_cols(D, row=0), None), (d_k_raw, _a_rows(D), w1t, _b_cols(D, row=1), None),
        (dv_a, _a_rows(D), w1t, _b_cols(D, row=2), None), (d_gate_a, _a_rows(D), wg_t, _b_cols(D), None),
        (d_f, _a_rows(LANES), w1t, _b_cols(LANES, row=f_row), None)])
    d_x, _, g["norm_a_g"] = _rmsnorm_bwd(x, [norm_a], [d_u_a], d_h1, "norm_a_bwd")
    return sq, d_x, g


BIG = ["w_in_a", "w_out_a", "w_kv", "w_in_b", "w_out_b"]
SPLIT = {"w_in_a": 1, "w_out_a": 0, "w_kv": 0, "w_in_b": 0, "w_out_b": 0}
SMALL = ["norm_a_g", "b_forget", "qnorm_a_g", "knorm_a_g", "kv_norm_g", "knorm_b_g", "norm_b_g", "qnorm_b_g", "sinks"]
NAMES = ["norm_a_g", "w_in_a", "b_forget", "qnorm_a_g", "knorm_a_g", "w_out_a", "kv_norm_g", "w_kv", "knorm_b_g",
         "norm_b_g", "w_in_b", "qnorm_b_g", "sinks", "w_out_b"]


def _pack(vals):
    flat = []
    for v in vals:
        v = v.reshape(-1)
        flat.append(jnp.pad(v, (0, -v.shape[0] % LANES)))
    flat = jnp.concatenate(flat)
    flat = jnp.pad(flat, (0, -flat.shape[0] % (8 * LANES)))
    return flat.reshape(-1, LANES)


def _unpack(packed, shapes):
    flat, out, off = packed.reshape(-1), [], 0
    for s in shapes:
        n = int(np.prod(s))
        out.append(flat[off:off + n].reshape(s))
        off += n + (-n % LANES)
    return out


def kernel(x, positions, norm_a_g, w_in_a, b_forget, qnorm_a_g, knorm_a_g, w_out_a, kv_norm_g, w_kv, knorm_b_g, norm_b_g, w_in_b, qnorm_b_g, sinks, w_out_b, loss_target, m_norm_a_g, m_w_in_a, m_b_forget, m_qnorm_a_g, m_knorm_a_g, m_w_out_a, m_kv_norm_g, m_w_kv, m_knorm_b_g, m_norm_b_g, m_w_in_b, m_qnorm_b_g, m_sinks, m_w_out_b, v_norm_a_g, v_w_in_a, v_b_forget, v_qnorm_a_g, v_knorm_a_g, v_w_out_a, v_kv_norm_g, v_w_kv, v_knorm_b_g, v_norm_b_g, v_w_in_b, v_qnorm_b_g, v_sinks, v_w_out_b):
    w = dict(norm_a_g=norm_a_g, w_in_a=w_in_a, b_forget=b_forget, qnorm_a_g=qnorm_a_g, knorm_a_g=knorm_a_g,
             w_out_a=w_out_a, kv_norm_g=kv_norm_g, w_kv=w_kv, knorm_b_g=knorm_b_g, norm_b_g=norm_b_g,
             w_in_b=w_in_b, qnorm_b_g=qnorm_b_g, sinks=sinks, w_out_b=w_out_b)
    m = dict(norm_a_g=m_norm_a_g, w_in_a=m_w_in_a, b_forget=m_b_forget, qnorm_a_g=m_qnorm_a_g, knorm_a_g=m_knorm_a_g,
             w_out_a=m_w_out_a, kv_norm_g=m_kv_norm_g, w_kv=m_w_kv, knorm_b_g=m_knorm_b_g, norm_b_g=m_norm_b_g,
             w_in_b=m_w_in_b, qnorm_b_g=m_qnorm_b_g, sinks=m_sinks, w_out_b=m_w_out_b)
    v = dict(norm_a_g=v_norm_a_g, w_in_a=v_w_in_a, b_forget=v_b_forget, qnorm_a_g=v_qnorm_a_g, knorm_a_g=v_knorm_a_g,
             w_out_a=v_w_out_a, kv_norm_g=v_kv_norm_g, w_kv=v_w_kv, knorm_b_g=v_knorm_b_g, norm_b_g=v_norm_b_g,
             w_in_b=v_w_in_b, qnorm_b_g=v_qnorm_b_g, sinks=v_sinks, w_out_b=v_w_out_b)
    my_chip = 2 * lax.axis_index("x") + lax.axis_index("y")

    def shard2d(t, n):
        t = t.reshape(t.shape[-2:])
        return t.T if n == "w_in_a" else t

    w2d = {n: shard2d(w[n], n) for n in BIG}

    norm_a_rows = jnp.broadcast_to(norm_a_g.reshape(1, D // NCHIP), (16, D // NCHIP))
    gathered = _gather_shards([w2d[n].astype(BF16) for n in BIG] + [norm_a_rows], [SPLIT[n] for n in BIG] + [0])
    wt = {n: t.reshape(-1, t.shape[2]) for n, t in zip(BIG, gathered[:-1])}
    wt["w_in_a_t"] = wt.pop("w_in_a")
    wt["w_in_b"] = gathered[BIG.index("w_in_b")]
    wt["norm_a_g"] = gathered[-1][:, 0, :].reshape(1, D)
    for n in SMALL[1:]:
        wt[n] = w[n]

    sq, d_x, g = _local_step(x[0], loss_target[0], positions, wt)

    small_shapes = [(D,), (NH,), (HD,), (HD,), (D,), (HD,), (D,), (HD,), (NH,), (D,)]
    packed = _pack([g[n] for n in SMALL] + [sq])
    total = _sum_stack(_gather_small(packed), "sum_small")
    small_g = dict(zip(SMALL, _unpack(total, small_shapes)[:-1]))
    loss = 0.5 * jnp.sum(_unpack(total, small_shapes)[-1]) / D
    small_g["norm_a_g"] = lax.dynamic_slice(small_g["norm_a_g"], (my_chip * (D // NCHIP),), (D // NCHIP,))

    axes = [SPLIT[n] for n in BIG]
    blocks = [g[n] if g[n].ndim == 3 else g[n].reshape(NCHIP, -1, g[n].shape[1]) for n in BIG]
    from_sibling = _halves_to_sibling(blocks, axes)
    chip_f32, chip_bf16 = [], []
    for n, ax, blk, part in zip(BIG, axes, blocks, from_sibling):
        t32, t16 = _chip_sum(blk, part, ax, "chip_sum_" + n)
        chip_f32.append(t32)
        chip_bf16.append(t16)
    arrived = _scatter_chip_sums(chip_bf16)
    halves = []
    for n, ax, t32, parts in zip(BIG, axes, chip_f32, arrived):
        own = lax.dynamic_index_in_dim(t32, my_chip, axis=0, keepdims=False)
        halves.append(_mesh_sum(own, parts, ax, "mesh_sum_" + n))
    sibling_done = _to_sibling(halves, "finished_halves")

    res = {}
    for n, ax, mine_half, their_half in zip(BIG, axes, halves, sibling_done):
        out4 = _adamw_halves(w2d[n], mine_half, their_half, shard2d(m[n], n), shard2d(v[n], n), ax, "adamw_" + n)
        res[n] = tuple((t.T if n == "w_in_a" else t).reshape(w[n].shape) for t in out4)
    sm_g = _pack([small_g[n] for n in SMALL])
    sm = [_pack([d[n] for n in SMALL]) for d in (w, m, v)]
    sm_out = _adamw(sm[0], sm_g, sm[1], sm[2], "adamw_small")
    sm_shapes = [w[n].shape for n in SMALL]
    unpacked = [_unpack(t, sm_shapes) for t in (sm_g,) + tuple(sm_out)]
    for i, n in enumerate(SMALL):
        res[n] = tuple(u[i] for u in unpacked)

    outs = [loss, d_x[None]]
    for k in range(4):
        outs += [res[n][k] for n in NAMES]
    return tuple(outs)
```

```python
import numpy as np
import jax
import jax.numpy as jnp
from jax import lax
from jax.experimental import pallas as pl
from jax.experimental.pallas import tpu as pltpu

F32, BF16 = jnp.float32, jnp.bfloat16
S, D, HD, NH, NKV = 2048, 1024, 64, 16, 4
KVW = NKV * HD
WINDOW = 128
ROT = HD // 4
THETA = 500000.0
EPS = 1e-6
SCALE = HD ** -0.5
LANES = 128
NEG = -1e30
VMEM_LIMIT = 48 * 2 ** 20
ROWS = 256
ATT = 256
SWQ = 4
NCHIP = 4
ADAM_LR, ADAM_B1, ADAM_B2, ADAM_EPS, ADAM_WD, ADAM_STEP = 0.001, 0.9, 0.999, 1e-08, 0.01, 10
NT = (((1,), (1,)), ((), ()))
TN = (((0,), (0,)), ((), ()))
MESH = pl.DeviceIdType.MESH


def _params(n):
    return pltpu.CompilerParams(dimension_semantics=("arbitrary",) * n, vmem_limit_bytes=VMEM_LIMIT)


def _dot(a, b, dims=None):
    if dims is None:
        return jnp.dot(a, b, preferred_element_type=F32)
    return lax.dot_general(a, b, dims, preferred_element_type=F32)


def _dot_split(a, b, n):
    out, rest = None, a
    for _ in range(n):
        hi = rest.astype(BF16)
        term = _dot(hi, b)
        out = term if out is None else out + term
        rest = rest - hi.astype(F32)
    return out


def _seg_mats(w):
    e = (np.arange(w)[:, None] // HD == np.arange(LANES)[None, :]).astype(np.float32)
    return jnp.asarray(e, BF16), jnp.asarray(e.T, BF16)


def _head_rstd(x, e, et):
    ss = _dot_split(x * x, e, 2)
    return _dot_split(lax.rsqrt(ss * (1.0 / HD) + EPS), et, 3)


def _rope(x, c, a, b):
    w = x.shape[1]
    return x * c + pltpu.roll(x, w - ROT // 2, 1) * a + pltpu.roll(x, ROT // 2, 1) * b


def _rope_t(dy, c, a, b):
    w = dy.shape[1]
    return dy * c + pltpu.roll(dy * b, w - ROT // 2, 1) + pltpu.roll(dy * a, ROT // 2, 1)


def _sigmoid(x):
    return 1.0 / (1.0 + jnp.exp(-x))


def _row_spec(shape, ts):
    nd = len(shape)
    if shape[0] == S:
        return pl.BlockSpec((ts,) + tuple(shape[1:]), lambda i: (i,) + (0,) * (nd - 1))
    return pl.BlockSpec(tuple(shape), lambda i: (0,) * nd)


def _rows_call(body, name, ins, outs, ts=ROWS):
    return pl.pallas_call(
        body, name=name, grid=(S // ts,),
        in_specs=[_row_spec(a.shape, ts) for a in ins],
        out_specs=[_row_spec(s, ts) for s, _ in outs],
        out_shape=[jax.ShapeDtypeStruct(s, d) for s, d in outs],
        compiler_params=_params(1))(*ins)


def _col_spec(ts, w, col):
    return pl.BlockSpec((ts, w), lambda i: (i, col))


TM = TN_ = 512
TM_TOKENS = 1024


def _mm(name, m, n, terms, out_dtype=F32, add=None, tm=None, tn=TN_, stacked=False):
    nterm = len(terms)
    if tm is None:
        tm = TM_TOKENS if m == S else TM

    def body(*refs):
        acc = None
        for t in range(nterm):
            part = _dot(refs[2 * t][...], refs[2 * t + 1][...], terms[t][4])
            acc = part if acc is None else acc + part
        if add is not None:
            acc = acc + refs[2 * nterm][...]
        refs[-1][...] = acc.astype(out_dtype)

    tile = pl.BlockSpec((tm, tn), lambda j, i: (i, j))
    ins, specs = [], []
    for a, a_spec, b, b_spec, _ in terms:
        ins += [a, b]
        specs += [a_spec, b_spec]
    if add is not None:
        ins.append(add)
        specs.append(tile)
    return pl.pallas_call(
        body, name=name, grid=(n // tn, m // tm), in_specs=specs,
        out_specs=pl.BlockSpec((None, tm, tn), lambda j, i: (j, i, 0)) if stacked else tile,
        out_shape=jax.ShapeDtypeStruct((n // tn, m, tn) if stacked else (m, n), out_dtype),
        compiler_params=_params(2))(*ins)


def _a_rows(k, col=0, tm=TM_TOKENS):
    return pl.BlockSpec((tm, k), lambda j, i: (i, col))


def _a_cols(k, tm=TM):
    return pl.BlockSpec((k, tm), lambda j, i: (0, i))


def _b_cols(k, row=0, col0=0, tn=TN_):
    return pl.BlockSpec((k, tn), lambda j, i: (row, col0 + j))


def _b_rows(k, row0=0, tn=TN_):
    return pl.BlockSpec((tn, k), lambda j, i: (row0 + j, 0))


def _rmsnorm_fwd(x, gains, name):
    def body(*refs):
        xv = refs[0][...]
        r = lax.rsqrt(jnp.mean(xv * xv, axis=-1, keepdims=True) + EPS)
        xh = xv * r
        for n in range(len(gains)):
            refs[1 + len(gains) + n][...] = (xh * refs[1 + n][...]).astype(BF16)

    return _rows_call(body, name, [x] + list(gains), [((S, D), BF16)] * len(gains))


def _rmsnorm_bwd(x, gains, dus, dres, name):
    n = len(gains)

    def body(*refs):
        x_ref, g_refs, du_refs, dres_ref = refs[0], refs[1:1 + n], refs[1 + n:1 + 2 * n], refs[1 + 2 * n]
        dx_ref, dxb_ref, dg_refs = refs[2 + 2 * n], refs[3 + 2 * n], refs[4 + 2 * n:]
        xv = x_ref[...]
        r = lax.rsqrt(jnp.mean(xv * xv, axis=-1, keepdims=True) + EPS)
        xh = xv * r
        gy = None
        for m in range(n):
            du = du_refs[m][...]
            part = jnp.sum(du * xh, axis=0, keepdims=True)

            @pl.when(pl.program_id(0) == 0)
            def _(m=m, part=part):
                dg_refs[m][...] = part

            @pl.when(pl.program_id(0) != 0)
            def _(m=m, part=part):
                dg_refs[m][...] += part

            t = du * g_refs[m][...]
            gy = t if gy is None else gy + t
        dx = dres_ref[...] + r * (gy - xh * jnp.mean(gy * xh, axis=-1, keepdims=True))
        dx_ref[...] = dx
        dxb_ref[...] = dx.astype(BF16)

    outs = [((S, D), F32), ((S, D), BF16)] + [((1, D), F32)] * n
    return _rows_call(body, name, [x] + list(gains) + list(dus) + [dres], outs)


def _a_post(qkvg, qg, kg):
    e, et = _seg_mats(D)

    def body(q_ref, k_ref, v_ref, qg_ref, kg_ref, e_ref, et_ref, qo, ko, vo):
        ev, etv = e_ref[...], et_ref[...]
        qv, kv = q_ref[...], k_ref[...]
        qo[...] = (qv * _head_rstd(qv, ev, etv) * qg_ref[...] * SCALE).astype(BF16)
        ko[...] = (kv * _head_rstd(kv, ev, etv) * kg_ref[...]).astype(BF16)
        vo[...] = v_ref[...].astype(BF16)

    whole = lambda a: pl.BlockSpec(a.shape, lambda i: (0, 0))
    return pl.pallas_call(
        body, name="a_post", grid=(S // ROWS,),
        in_specs=[_col_spec(ROWS, D, 0), _col_spec(ROWS, D, 1), _col_spec(ROWS, D, 2),
                  whole(qg), whole(kg), whole(e), whole(et)],
        out_specs=[_col_spec(ROWS, D, 0)] * 3,
        out_shape=[jax.ShapeDtypeStruct((S, D), BF16)] * 3,
        compiler_params=_params(1))(qkvg, qkvg, qkvg, qg, kg, e, et)


def _tri(upper):
    r, c = np.arange(ROWS)[:, None], np.arange(ROWS)[None, :]
    return jnp.asarray((r <= c) if upper else (r >= c), BF16)


def _forget_cumsum(fpad, bpad):
    def body(f_ref, b_ref, u_ref, c_ref, carry):
        @pl.when(pl.program_id(0) == 0)
        def _():
            carry[...] = jnp.zeros_like(carry)

        lf = jax.nn.log_sigmoid(f_ref[...] + b_ref[...])
        blk = _dot_split(lf.T, u_ref[...], 3) + carry[:, 0:1]
        c_ref[...] = blk
        carry[...] = jnp.broadcast_to(blk[:, ROWS - 1:ROWS], carry.shape)

    return pl.pallas_call(
        body, name="forget_cumsum", grid=(S // ROWS,),
        in_specs=[pl.BlockSpec((ROWS, LANES), lambda i: (i, 0)), pl.BlockSpec((1, LANES), lambda i: (0, 0)),
                  pl.BlockSpec((ROWS, ROWS), lambda i: (0, 0))],
        out_specs=pl.BlockSpec((LANES, ROWS), lambda i: (0, i)),
        out_shape=jax.ShapeDtypeStruct((LANES, S), F32),
        scratch_shapes=[pltpu.VMEM((LANES, LANES), F32)],
        compiler_params=_params(1))(fpad, bpad, _tri(True))


def _forget_bwd(dct, fpad, bpad):
    nb = S // ROWS

    def body(dc_ref, f_ref, b_ref, l_ref, df_ref, db_ref, carry):
        @pl.when(pl.program_id(0) == 0)
        def _():
            carry[...] = jnp.zeros_like(carry)
            db_ref[...] = jnp.zeros_like(db_ref)

        blk = _dot_split(dc_ref[...], l_ref[...], 3) + carry[:, 0:1]
        carry[...] = jnp.broadcast_to(blk[:, 0:1], carry.shape)
        df = blk.T * _sigmoid(-(f_ref[...] + b_ref[...]))
        df_ref[...] = df.astype(BF16)
        db_ref[...] += jnp.sum(df, axis=0, keepdims=True)

    return pl.pallas_call(
        body, name="forget_bwd", grid=(nb,),
        in_specs=[pl.BlockSpec((LANES, ROWS), lambda i: (0, nb - 1 - i)),
                  pl.BlockSpec((ROWS, LANES), lambda i: (nb - 1 - i, 0)),
                  pl.BlockSpec((1, LANES), lambda i: (0, 0)), pl.BlockSpec((ROWS, ROWS), lambda i: (0, 0))],
        out_specs=[pl.BlockSpec((ROWS, LANES), lambda i: (nb - 1 - i, 0)), pl.BlockSpec((1, LANES), lambda i: (0, 0))],
        out_shape=[jax.ShapeDtypeStruct((S, LANES), BF16), jax.ShapeDtypeStruct((1, LANES), F32)],
        scratch_shapes=[pltpu.VMEM((LANES, LANES), F32)],
        compiler_params=_params(1))(dct, fpad, bpad, _tri(False))


def _gate_fwd(o, proj, col, name):
    def body(o_ref, g_ref, y_ref):
        g = g_ref[...]
        y_ref[...] = (o_ref[...] * (g * _sigmoid(g))).astype(BF16)

    return pl.pallas_call(
        body, name=name, grid=(S // ROWS,),
        in_specs=[_col_spec(ROWS, D, 0), _col_spec(ROWS, D, col)],
        out_specs=_col_spec(ROWS, D, 0), out_shape=jax.ShapeDtypeStruct((S, D), BF16),
        compiler_params=_params(1))(o, proj)


def _gate_bwd(dy, o, proj, col, name):
    def body(dy_ref, o_ref, g_ref, do_ref, dg_ref):
        g, dyv = g_ref[...], dy_ref[...]
        sg = _sigmoid(g)
        do_ref[...] = dyv * (g * sg)
        dg_ref[...] = (dyv * o_ref[...] * (sg * (1.0 + g * (1.0 - sg)))).astype(BF16)

    return pl.pallas_call(
        body, name=name, grid=(S // ROWS,),
        in_specs=[_col_spec(ROWS, D, 0), _col_spec(ROWS, D, 0), _col_spec(ROWS, D, col)],
        out_specs=[_col_spec(ROWS, D, 0)] * 2,
        out_shape=[jax.ShapeDtypeStruct((S, D), F32), jax.ShapeDtypeStruct((S, D), BF16)],
        compiler_params=_params(1))(dy, o, proj)


def _headnorm_bwd(x, col, gain, dy, rope, name):
    e, et = _seg_mats(D)
    tabs = list(rope) if rope is not None else []

    def body(*refs):
        x_ref, g_ref, dy_ref, e_ref, et_ref = refs[:5]
        dx_ref, dg_ref = refs[-2:]
        xv, dyv, ev, etv = x_ref[...], dy_ref[...], e_ref[...], et_ref[...]
        if rope is not None:
            c, a, b = (jnp.tile(t[...], (1, D // LANES)) for t in refs[5:8])
            dyv = _rope_t(dyv, c, a, b)
        r = _head_rstd(xv, ev, etv)
        xh = xv * r
        part = jnp.sum(dyv * xh, axis=0, keepdims=True)

        @pl.when(pl.program_id(0) == 0)
        def _():
            dg_ref[...] = part

        @pl.when(pl.program_id(0) != 0)
        def _():
            dg_ref[...] += part

        gy = dyv * g_ref[...]
        seg = _dot_split(_dot_split(gy * xh, ev, 2) * (1.0 / HD), etv, 3)
        dx_ref[...] = (r * (gy - xh * seg)).astype(BF16)

    whole = lambda a: pl.BlockSpec(a.shape, lambda i: (0, 0))
    return pl.pallas_call(
        body, name=name, grid=(S // ROWS,),
        in_specs=[_col_spec(ROWS, D, col), whole(gain), _col_spec(ROWS, D, 0), whole(e), whole(et)]
                 + [pl.BlockSpec((ROWS, LANES), lambda i: (i, 0))] * len(tabs),
        out_specs=[_col_spec(ROWS, D, 0), whole(gain)],
        out_shape=[jax.ShapeDtypeStruct((S, D), BF16), jax.ShapeDtypeStruct((1, D), F32)],
        compiler_params=_params(1))(x, gain, dy, e, et, *tabs)


def _dup_mat():
    r, c = np.arange(KVW)[:, None], np.arange(2 * KVW)[None, :]
    return (r // HD == c // LANES) & (r % HD == c % HD)


def _fold_mat():
    r, c = np.arange(D)[:, None], np.arange(KVW)[None, :]
    return (r // (2 * LANES) == c // HD) & (r % HD == c % HD)


def _b_post(pb, kv, qg, kg, rope):
    e, et = _seg_mats(D)
    ek, etk = _seg_mats(KVW)
    dup = jnp.asarray(_dup_mat(), BF16)

    def body(q_ref, k_ref, v_ref, qg_ref, kg_ref, e_ref, et_ref, ek_ref, etk_ref, dup_ref, c_ref, a_ref, b_ref,
             qo, ko, vo):
        c1, a1, b1 = c_ref[...], a_ref[...], b_ref[...]
        qv = q_ref[...]
        qn = qv * _head_rstd(qv, e_ref[...], et_ref[...]) * qg_ref[...]
        t = lambda z, n: jnp.tile(z, (1, n))
        qo[...] = (_rope(qn, t(c1, D // LANES), t(a1, D // LANES), t(b1, D // LANES)) * SCALE).astype(BF16)
        kvv = k_ref[...]
        kn = kvv * _head_rstd(kvv, ek_ref[...], etk_ref[...]) * kg_ref[...]
        kr = _rope(kn, t(c1, KVW // LANES), t(a1, KVW // LANES), t(b1, KVW // LANES)).astype(BF16)
        ko[...] = _dot(kr, dup_ref[...]).astype(BF16)
        vo[...] = _dot(v_ref[...].astype(BF16), dup_ref[...]).astype(BF16)

    whole = lambda a: pl.BlockSpec(a.shape, lambda i: (0, 0))
    tab = pl.BlockSpec((ROWS, LANES), lambda i: (i, 0))
    return pl.pallas_call(
        body, name="b_post", grid=(S // ROWS,),
        in_specs=[_col_spec(ROWS, D, 0), _col_spec(ROWS, KVW, 0), _col_spec(ROWS, KVW, 1),
                  whole(qg), whole(kg), whole(e), whole(et), whole(ek), whole(etk), whole(dup), tab, tab, tab],
        out_specs=[_col_spec(ROWS, D, 0), _col_spec(ROWS, 2 * KVW, 0), _col_spec(ROWS, 2 * KVW, 0)],
        out_shape=[jax.ShapeDtypeStruct((S, D), BF16), jax.ShapeDtypeStruct((S, 2 * KVW), BF16),
                   jax.ShapeDtypeStruct((S, 2 * KVW), BF16)],
        compiler_params=_params(1))(pb, kv, kv, qg, kg, e, et, ek, etk, dup, *rope)


def _kv_bwd(dkdup, dvdup, kv, kg, rope):
    ek, etk = _seg_mats(KVW)
    fold = jnp.asarray(_fold_mat(), BF16)

    def body(dk_ref, dv_ref, k_ref, kg_ref, ek_ref, etk_ref, fold_ref, c_ref, a_ref, b_ref, dkv_ref, dg_ref):
        ev, etv, fv = ek_ref[...], etk_ref[...], fold_ref[...]
        t = lambda z: jnp.tile(z[...], (1, KVW // LANES))
        dk = _rope_t(_dot_split(dk_ref[...], fv, 3), t(c_ref), t(a_ref), t(b_ref))
        dv = _dot_split(dv_ref[...], fv, 3)
        xv = k_ref[...]
        r = _head_rstd(xv, ev, etv)
        xh = xv * r
        part = jnp.sum(dk * xh, axis=0, keepdims=True)

        @pl.when(pl.program_id(0) == 0)
        def _():
            dg_ref[...] = part

        @pl.when(pl.program_id(0) != 0)
        def _():
            dg_ref[...] += part

        gy = dk * kg_ref[...]
        seg = _dot_split(_dot_split(gy * xh, ev, 2) * (1.0 / HD), etv, 3)
        dkv_ref[:, 0:KVW] = (r * (gy - xh * seg)).astype(BF16)
        dkv_ref[:, KVW:2 * KVW] = dv.astype(BF16)

    whole = lambda a: pl.BlockSpec(a.shape, lambda i: (0, 0))
    tab = pl.BlockSpec((ROWS, LANES), lambda i: (i, 0))
    return pl.pallas_call(
        body, name="kv_bwd", grid=(S // ROWS,),
        in_specs=[_col_spec(ROWS, D, 0), _col_spec(ROWS, D, 0), _col_spec(ROWS, KVW, 0),
                  whole(kg), whole(ek), whole(etk), whole(fold), tab, tab, tab],
        out_specs=[_col_spec(ROWS, 2 * KVW, 0), whole(kg)],
        out_shape=[jax.ShapeDtypeStruct((S, 2 * KVW), BF16), jax.ShapeDtypeStruct((1, KVW), F32)],
        compiler_params=_params(1))(dkdup, dvdup, kv, kg, ek, etk, fold, *rope)


def _loss_head(out, target):
    def body(o_ref, t_ref, d_ref, db_ref, l_ref):
        diff = o_ref[...] - t_ref[...]
        d = diff * (1.0 / D)
        d_ref[...] = d
        db_ref[...] = d.astype(BF16)

        @pl.when(pl.program_id(0) == 0)
        def _():
            l_ref[...] = jnp.zeros_like(l_ref)

        l_ref[...] += jnp.sum(diff * diff, axis=0, keepdims=True)

    return _rows_call(body, "loss_head", [out, target], [((S, D), F32), ((S, D), BF16), ((1, D), F32)])


def _lane():
    return lax.broadcasted_iota(jnp.int32, (1, LANES), 1)


def _head_mask(hh):
    return (_lane() < HD) if hh == 0 else (_lane() >= HD)


def _fox_fwd(q, k, v, ct, riding):
    nq, npair = S // ATT, NH // 2
    ni, no = len(riding.ins), len(riding.outs)

    def body(q_ref, k_ref, v_ref, c_ref, *rest):
        o_ref, lse_ref = rest[ni:ni + 2]
        pair, i = pl.program_id(0), pl.program_id(1)
        at_end = riding.hooks(rest[:ni], rest[ni + 2:ni + 2 + no], *rest[ni + 2 + no:],
                              first=(pair == 0) & (i == 0), middle=(pair == npair // 2) & (i == 0),
                              last=(pair == npair - 1) & (i == nq - 1))
        q2 = q_ref[...]
        qms = [jnp.where(_head_mask(hh), q2, jnp.zeros_like(q2)) for hh in (0, 1)]

        def probs(off, width, m, hh, diag):
            s = _dot(qms[hh], k_ref[pl.ds(off, width), :], NT) - c_ref[hh:hh + 1, pl.ds(off, width)]
            if diag:
                row = i * ATT + lax.broadcasted_iota(jnp.int32, (ATT, width), 0)
                col = off + lax.broadcasted_iota(jnp.int32, (ATT, width), 1)
                s = jnp.where(col <= row, s, NEG)
            m_new = jnp.maximum(m, jnp.max(s, axis=1, keepdims=True))
            p = jnp.exp(s - m_new)
            p_hi = p.astype(BF16)
            return m_new, jnp.exp(m - m_new), p_hi, (p - p_hi.astype(F32)).astype(BF16)

        def weighted(off, width, p_hi, p_lo, hh):
            vj = v_ref[pl.ds(off, width), :]
            v1 = jnp.where(_head_mask(hh), vj, jnp.ones_like(vj))
            return _dot(p_hi, v1) + _dot(p_lo, v1)

        def step(off, width, carry, diag):
            off = pl.multiple_of(off, ATT)
            out = []
            for hh in (0, 1):
                m, acc = carry[hh]
                m, alpha, p_hi, p_lo = probs(off, width, m, hh, diag)
                out.append((m, alpha * acc + weighted(off, width, p_hi, p_lo, hh)))
            return tuple(out)

        one = (jnp.full((ATT, 1), NEG, F32), jnp.zeros((ATT, LANES), F32))
        carry = lax.fori_loop(0, i // 2, lambda j, cr: step(j * (2 * ATT), 2 * ATT, cr, False), (one, one))
        carry = lax.cond(i % 2 == 1, lambda cr: step((i - 1) * ATT, 2 * ATT, cr, True),
                         lambda cr: step(i * ATT, ATT, cr, True), carry)
        res = []
        for hh in (0, 1):
            m, acc = carry[hh]
            l = jnp.max(jnp.where(_head_mask(1 - hh), acc, 0.0), axis=1, keepdims=True)
            res.append((acc / l, m + jnp.log(l)))
        first = _head_mask(0)
        o_ref[...] = jnp.where(first, res[0][0], res[1][0])
        lse_ref[...] = jnp.where(first, res[0][1], res[1][1])
        at_end()

    blk = pl.BlockSpec((ATT, LANES), lambda p, i: (i, p))
    full = pl.BlockSpec((S, LANES), lambda p, i: (0, p))
    res = pl.pallas_call(
        body, name="fox_fwd", grid=(npair, nq),
        in_specs=[blk, full, full, pl.BlockSpec((None, 2, S), lambda p, i: (p, 0, 0))] + riding.in_specs,
        out_specs=[blk, blk] + riding.out_specs,
        out_shape=[jax.ShapeDtypeStruct((S, D), F32)] * 2 + riding.out_shape,
        scratch_shapes=riding.scratch,
        compiler_params=_params(2))(q, k, v, ct, *riding.ins)
    return res[0], res[1], res[2:]


def _fox_bwd(q, k, v, ct, o, lse, do, riding):
    nq, npair = S // ATT, NH // 2
    ni, no = len(riding.ins), len(riding.outs)

    def body(q_ref, k_ref, v_ref, c_ref, o_ref, lse_ref, do_ref, *rest):
        dq_ref, dk_ref, dvb_ref, dc_ref = rest[ni:ni + 4]
        dv_ref = rest[ni + 4 + no]
        pair, i = pl.program_id(0), pl.program_id(1)
        at_end = riding.hooks(rest[:ni], rest[ni + 4:ni + 4 + no], *rest[ni + 5 + no:],
                              first=(pair == 0) & (i == 0), middle=(pair == npair // 2) & (i == 0),
                              last=(pair == npair - 1) & (i == nq - 1))

        @pl.when(i == 0)
        def _():
            dk_ref[...] = jnp.zeros_like(dk_ref)
            dv_ref[...] = jnp.zeros_like(dv_ref)
            dc_ref[...] = jnp.zeros_like(dc_ref)

        q2, do2, lse2 = q_ref[...], do_ref[...], lse_ref[...]
        do2b = do2.astype(BF16)
        prod = do2b.astype(F32) * o_ref[...]
        heads = []
        for hh in (0, 1):
            hm = _head_mask(hh)
            heads.append((jnp.where(hm, q2, jnp.zeros_like(q2)), jnp.where(hm, do2b, jnp.zeros_like(do2b)),
                          jnp.sum(jnp.where(hm, prod, 0.0), axis=1, keepdims=True),
                          jnp.max(jnp.where(hm, lse2, NEG), axis=1, keepdims=True)))

        def step(off, width, dqs, diag):
            off = pl.multiple_of(off, ATT)
            kj, vj = k_ref[pl.ds(off, width), :], v_ref[pl.ds(off, width), :]
            dk, dv, out = None, None, []
            for hh in (0, 1):
                qm, dom, delta, lse_h = heads[hh]
                s = _dot(qm, kj, NT) - c_ref[hh:hh + 1, pl.ds(off, width)]
                p = jnp.exp(s - lse_h)
                if diag:
                    row = i * ATT + lax.broadcasted_iota(jnp.int32, (ATT, width), 0)
                    col = off + lax.broadcasted_iota(jnp.int32, (ATT, width), 1)
                    p = jnp.where(col <= row, p, 0.0)
                ds = p * (_dot(dom, vj, NT) - delta)
                dc_ref[hh:hh + 1, pl.ds(off, width)] += -jnp.sum(ds, axis=0, keepdims=True)
                dsb = ds.astype(BF16)
                dk_h, dv_h = _dot(dsb, qm, TN), _dot(p.astype(BF16), dom, TN)
                dk, dv = (dk_h, dv_h) if dk is None else (dk + dk_h, dv + dv_h)
                out.append(dqs[hh] + _dot(dsb, kj))
            dk_ref[pl.ds(off, width), :] += dk
            dv_ref[pl.ds(off, width), :] += dv
            return tuple(out)

        zero = jnp.zeros((ATT, LANES), F32)
        dqs = lax.fori_loop(0, i // 2, lambda j, acc: step(j * (2 * ATT), 2 * ATT, acc, False), (zero, zero))
        dqs = lax.cond(i % 2 == 1, lambda acc: step((i - 1) * ATT, 2 * ATT, acc, True),
                       lambda acc: step(i * ATT, ATT, acc, True), dqs)
        dq_ref[...] = jnp.where(_head_mask(0), dqs[0], dqs[1]) * SCALE

        @pl.when(i == nq - 1)
        def _():
            dvb_ref[...] = dv_ref[...].astype(BF16)

        at_end()

    blk = pl.BlockSpec((ATT, LANES), lambda p, i: (i, p))
    full = pl.BlockSpec((S, LANES), lambda p, i: (0, p))
    cspec = pl.BlockSpec((None, 2, S), lambda p, i: (p, 0, 0))
    res = pl.pallas_call(
        body, name="fox_bwd", grid=(npair, nq),
        in_specs=[blk, full, full, cspec, blk, blk, blk] + riding.in_specs,
        out_specs=[blk, full, full, cspec] + riding.out_specs,
        out_shape=[jax.ShapeDtypeStruct((S, D), F32)] * 2 + [jax.ShapeDtypeStruct((S, D), BF16),
                                                              jax.ShapeDtypeStruct((npair, 2, S), F32)]
                  + riding.out_shape,
        scratch_shapes=[pltpu.VMEM((S, LANES), F32)] + riding.scratch,
        compiler_params=_params(2))(q, k, v, ct, o, lse, do, *riding.ins)
    return res[0], res[1], res[2], res[3], res[4:]


def _both_heads(x):
    return jnp.concatenate([jnp.where(_head_mask(hh), x, jnp.zeros_like(x)) for hh in (0, 1)], axis=0)


def _per_head(col0, col1):
    return jnp.concatenate([jnp.broadcast_to(col0, (WINDOW, 1)), jnp.broadcast_to(col1, (WINDOW, 1))], axis=0)


def _unstack(x2):
    return jnp.where(_head_mask(0), x2[:WINDOW], x2[WINDOW:])


def _swa_valid(i, start):
    r = lax.broadcasted_iota(jnp.int32, (2 * WINDOW, 2 * WINDOW), 0)
    qabs = i * WINDOW + jnp.where(r >= WINDOW, r - WINDOW, r)
    kabs = start + lax.broadcasted_iota(jnp.int32, (2 * WINDOW, 2 * WINDOW), 1)
    return (kabs <= qabs) & (qabs - kabs < WINDOW)


def _swa_fwd(q, kdup, vdup, sinks_t):
    def body(q_ref, k_ref, v_ref, sk_ref, o_ref, lse_ref):
        skv = sk_ref[...]
        first = _head_mask(0)
        for sb in range(SWQ):
            i = pl.program_id(1) * SWQ + sb
            rows = slice(sb * WINDOW, (sb + 1) * WINDOW)
            start = pl.multiple_of(jnp.maximum(i - 1, 0) * WINDOW, WINDOW)
            kk, vv = k_ref[pl.ds(start, 2 * WINDOW), :], v_ref[pl.ds(start, 2 * WINDOW), :]
            q2 = q_ref[rows, :]
            valid = _swa_valid(i, start)[:WINDOW]
            res = []
            for hh in (0, 1):
                hm = _head_mask(hh)
                sink = jnp.max(jnp.where(hm, skv, NEG), axis=1, keepdims=True)
                s = jnp.where(valid, _dot(jnp.where(hm, q2, jnp.zeros_like(q2)), kk, NT), NEG)
                m = jnp.maximum(jnp.max(s, axis=1, keepdims=True), sink)
                p = jnp.exp(s - m)
                l = jnp.sum(p, axis=1, keepdims=True) + jnp.exp(sink - m)
                res.append((_dot(p.astype(BF16), vv) / l, m + jnp.log(l)))
            o_ref[rows, :] = jnp.where(first, res[0][0], res[1][0])
            lse_ref[rows, :] = jnp.where(first, res[0][1], res[1][1])

    blk = pl.BlockSpec((SWQ * WINDOW, LANES), lambda p, i: (i, p))
    full = pl.BlockSpec((S, LANES), lambda p, i: (0, p // 2))
    return pl.pallas_call(
        body, name="swa_fwd", grid=(NH // 2, S // (SWQ * WINDOW)),
        in_specs=[blk, full, full, pl.BlockSpec((1, LANES), lambda p, i: (0, p))],
        out_specs=[blk, blk],
        out_shape=[jax.ShapeDtypeStruct((S, D), F32)] * 2,
        compiler_params=_params(2))(q, kdup, vdup, sinks_t)


def _swa_bwd(q, kdup, vdup, sinks_t, o, lse, do):
    def body(q_ref, k_ref, v_ref, sk_ref, o_ref, lse_ref, do_ref, dq_ref, dk_ref, dv_ref, dsk_ref):
        @pl.when(pl.program_id(1) == 0)
        def _():
            dk_ref[...] = jnp.zeros_like(dk_ref)
            dv_ref[...] = jnp.zeros_like(dv_ref)
            dsk_ref[...] = jnp.zeros_like(dsk_ref)

        skv = sk_ref[...]
        first = _head_mask(0)
        sink = _per_head(*[jnp.max(jnp.where(_head_mask(hh), skv, NEG), axis=1, keepdims=True) for hh in (0, 1)])
        for sb in range(SWQ):
            i = pl.program_id(1) * SWQ + sb
            rows = slice(sb * WINDOW, (sb + 1) * WINDOW)
            start = pl.multiple_of(jnp.maximum(i - 1, 0) * WINDOW, WINDOW)
            kk, vv = k_ref[pl.ds(start, 2 * WINDOW), :], v_ref[pl.ds(start, 2 * WINDOW), :]
            do2b = do_ref[rows, :].astype(BF16)
            prod, lse2 = do2b.astype(F32) * o_ref[rows, :], lse_ref[rows, :]
            qs, dos = _both_heads(q_ref[rows, :]), _both_heads(do2b)
            delta = jnp.concatenate([jnp.sum(jnp.where(_head_mask(hh), prod, 0.0), axis=1, keepdims=True)
                                     for hh in (0, 1)], axis=0)
            lse_h = jnp.concatenate([jnp.max(jnp.where(_head_mask(hh), lse2, NEG), axis=1, keepdims=True)
                                     for hh in (0, 1)], axis=0)
            p = jnp.where(_swa_valid(i, start), jnp.exp(_dot(qs, kk, NT) - lse_h), 0.0)
            dsb = (p * (_dot(dos, vv, NT) - delta)).astype(BF16)
            dk_ref[pl.ds(start, 2 * WINDOW), :] += _dot(dsb, qs, TN)
            dv_ref[pl.ds(start, 2 * WINDOW), :] += _dot(p.astype(BF16), dos, TN)
            dq_ref[rows, :] = _unstack(_dot(dsb, kk)) * SCALE
            t = jnp.exp(sink - lse_h) * delta
            dsk_ref[...] += -jnp.where(first, jnp.sum(t[:WINDOW], axis=0, keepdims=True),
                                       jnp.sum(t[WINDOW:], axis=0, keepdims=True))

    blk = pl.BlockSpec((SWQ * WINDOW, LANES), lambda p, i: (i, p))
    full = pl.BlockSpec((S, LANES), lambda p, i: (0, p // 2))
    acc = pl.BlockSpec((S, LANES), lambda p, i: (0, p))
    sk = pl.BlockSpec((1, LANES), lambda p, i: (0, p))
    return pl.pallas_call(
        body, name="swa_bwd", grid=(NH // 2, S // (SWQ * WINDOW)),
        in_specs=[blk, full, full, sk, blk, blk, blk],
        out_specs=[blk, acc, acc, sk],
        out_shape=[jax.ShapeDtypeStruct((S, D), F32)] * 3 + [jax.ShapeDtypeStruct((1, D), F32)],
        compiler_params=_params(2))(q, kdup, vdup, sinks_t, o, lse, do)


def _adamw_math(w, g, m, v):
    m = ADAM_B1 * m + (1.0 - ADAM_B1) * g
    v = ADAM_B2 * v + (1.0 - ADAM_B2) * jnp.square(g)
    m_hat = m / (1.0 - ADAM_B1 ** ADAM_STEP)
    v_hat = v / (1.0 - ADAM_B2 ** ADAM_STEP)
    delta = -ADAM_LR * (m_hat / (jnp.sqrt(v_hat) + ADAM_EPS) + ADAM_WD * w)
    return delta, m, v


def _adamw(w, g, m, v, name):
    r, c = w.shape
    tr = min(r, 128)

    def body(w_ref, g_ref, m_ref, v_ref, d_ref, mo_ref, vo_ref):
        d_ref[...], mo_ref[...], vo_ref[...] = _adamw_math(w_ref[...], g_ref[...], m_ref[...], v_ref[...])

    spec = pl.BlockSpec((tr, c), lambda i: (i, 0))
    return pl.pallas_call(
        body, name=name, grid=(r // tr,), in_specs=[spec] * 4, out_specs=[spec] * 3,
        out_shape=[jax.ShapeDtypeStruct((r, c), F32)] * 3, compiler_params=_params(1))(w, g, m, v)


SUM_TILE = 128


def _tiles(shape2d, axis, lead=0):
    r, c = shape2d
    blk = (SUM_TILE, c) if axis == 0 else (r, SUM_TILE)
    count = shape2d[axis] // SUM_TILE

    def index(pos, *lead_idx):
        return tuple(lead_idx) + ((pos, 0) if axis == 0 else (0, pos))

    return (None,) * lead + blk, count, index


def _adamw_halves(w, g_mine, g_theirs, m, v, axis, name):
    blk, count, index = _tiles(w.shape, axis)
    per_half = count // 2

    def body(w_ref, a_ref, b_ref, m_ref, v_ref, g_ref, d_ref, mo_ref, vo_ref):
        is_mine = pl.program_id(0) // per_half == lax.axis_index("c")
        g = jnp.where(is_mine, a_ref[...], b_ref[...])
        g_ref[...] = g
        d_ref[...], mo_ref[...], vo_ref[...] = _adamw_math(w_ref[...], g, m_ref[...], v_ref[...])

    spec = pl.BlockSpec(blk, lambda i: index(i))
    half = pl.BlockSpec(blk, lambda i: index(i % per_half))
    return pl.pallas_call(
        body, name=name, grid=(count,), in_specs=[spec, half, half, spec, spec], out_specs=[spec] * 4,
        out_shape=[jax.ShapeDtypeStruct(w.shape, F32)] * 4, compiler_params=_params(1))(w, g_mine, g_theirs, m, v)


def _chip_sum(blocks, from_sibling, axis, name):
    half2d = from_sibling.shape[1:]
    blk, count, index = _tiles(half2d, axis, lead=1)

    def body(lo_ref, hi_ref, p_ref, o32, o16):
        mine = jnp.where(lax.axis_index("c") == 0, lo_ref[...], hi_ref[...])
        acc = mine + p_ref[...]
        o32[...] = acc
        o16[...] = acc.astype(BF16)

    lo = pl.BlockSpec(blk, lambda k, i: index(i, k))
    hi = pl.BlockSpec(blk, lambda k, i: index(i + count, k))
    return pl.pallas_call(
        body, name=name, grid=(NCHIP, count), in_specs=[lo, hi, lo], out_specs=[lo, lo],
        out_shape=[jax.ShapeDtypeStruct(from_sibling.shape, F32), jax.ShapeDtypeStruct(from_sibling.shape, BF16)],
        compiler_params=_params(2))(blocks, blocks, from_sibling)


def _mesh_sum(own, parts, axis, name):
    blk, count, index = _tiles(own.shape, axis)
    n = parts.shape[0]

    def body(a_ref, p_ref, o_ref):
        acc = a_ref[...]
        for k in range(n):
            acc = acc + p_ref[k].astype(F32)
        o_ref[...] = acc

    spec = pl.BlockSpec(blk, lambda i: index(i))
    return pl.pallas_call(
        body, name=name, grid=(count,),
        in_specs=[spec, pl.BlockSpec((n,) + blk, lambda i: index(i, 0))],
        out_specs=spec, out_shape=jax.ShapeDtypeStruct(own.shape, F32),
        compiler_params=_params(1))(own, parts)


def _sum_stack(parts, name):
    n = parts.shape[0]

    def body(p_ref, o_ref):
        acc = p_ref[0]
        for k in range(1, n):
            acc = acc + p_ref[k]
        o_ref[...] = acc

    return pl.pallas_call(body, name=name, out_shape=jax.ShapeDtypeStruct(parts.shape[1:], F32))(parts)


def _coords():
    return lax.axis_index("x"), lax.axis_index("y"), lax.axis_index("c")


def _chip(who):
    return 2 * who[0] + who[1]


def _flip(who, mask):
    return tuple((1 - v) if b else v for v, b in zip(who, mask))


def _transfer(transfers, t, I, O, ssem, rsem, receiving):
    tr, me = transfers[t], _coords()
    peer = _flip(me, tr["mask"])
    return pltpu.make_async_remote_copy(
        src_ref=tr["src"](I, O, me), dst_ref=tr["dst"](I, O, peer if receiving else me),
        send_sem=ssem.at[t], recv_sem=rsem.at[t], device_id=peer, device_id_type=MESH)


def _start_transfers(transfers, I, O, ssem, rsem, onward):
    arrived = set()
    for t, tr in enumerate(transfers):
        after = tr.get("after")
        if (after is not None) != onward:
            continue
        if after is not None and after not in arrived:
            _transfer(transfers, after, I, O, ssem, rsem, True).wait_recv()
            arrived.add(after)
        _transfer(transfers, t, I, O, ssem, rsem, False).start()


def _finish_transfers(transfers, I, O, ssem, rsem):
    passed_on = {tr["after"] for tr in transfers if tr.get("after") is not None}
    for t in range(len(transfers)):
        if t not in passed_on:
            _transfer(transfers, t, I, O, ssem, rsem, True).wait_recv()
    for t in range(len(transfers)):
        _transfer(transfers, t, I, O, ssem, rsem, False).wait_send()


def _exchange(name, ins, outs, transfers, copies=()):
    ni, no = len(ins), len(outs)
    nt = len(transfers)

    def body(*refs):
        I, O = refs[:ni], refs[ni:ni + no]
        ssem, rsem, lsem = refs[ni + no:]
        me = _coords()
        local = [pltpu.make_async_copy(s(I, O, me), d(I, O, me), lsem.at[n]) for n, (s, d) in enumerate(copies)]
        for cp in local:
            cp.start()
        _start_transfers(transfers, I, O, ssem, rsem, False)
        _start_transfers(transfers, I, O, ssem, rsem, True)
        _finish_transfers(transfers, I, O, ssem, rsem)
        for cp in local:
            cp.wait()

    hbm = pl.BlockSpec(memory_space=pltpu.HBM)
    return pl.pallas_call(
        body, name=name, in_specs=[hbm] * ni, out_specs=[hbm] * no,
        out_shape=[jax.ShapeDtypeStruct(s, d) for s, d in outs],
        scratch_shapes=[pltpu.SemaphoreType.DMA((nt,)), pltpu.SemaphoreType.DMA((nt,)),
                        pltpu.SemaphoreType.DMA((max(len(copies), 1),))],
        compiler_params=pltpu.CompilerParams(has_side_effects=True))(*ins)


CHIP_MASKS = [(0, 1, 0), (1, 0, 0), (1, 1, 0)]
SIBLING = (0, 0, 1)


def _half(shape2d, axis, which):
    n = shape2d[axis] // 2
    cut = pl.ds(pl.multiple_of(which * n, n), n)
    return (cut, slice(None)) if axis == 0 else (slice(None), cut)


class _Riding:
    def __init__(self, transfers, ins, outs):
        self.transfers, self.ins, self.outs = transfers, list(ins), list(outs)
        hbm = pl.BlockSpec(memory_space=pltpu.HBM)
        self.in_specs, self.out_specs = [hbm] * len(self.ins), [hbm] * len(self.outs)
        self.out_shape = [jax.ShapeDtypeStruct(s, d) for s, d in self.outs]
        self.scratch = [pltpu.SemaphoreType.DMA((max(len(transfers), 1),))] * 2

    def hooks(self, I, O, ssem, rsem, first, middle, last):
        tr = self.transfers

        @pl.when(first)
        def _():
            _start_transfers(tr, I, O, ssem, rsem, False)

        if any(t.get("after") is not None for t in tr):
            @pl.when(middle)
            def _():
                _start_transfers(tr, I, O, ssem, rsem, True)

        def at_end():
            @pl.when(last)
            def _():
                _finish_transfers(tr, I, O, ssem, rsem)

        return at_end


def _select_own(shards, gathered):
    mine = lax.broadcasted_iota(jnp.int32, (NCHIP, 1, 1), 0) == _chip(_coords())
    return [jnp.where(mine, s[None], t) for s, t in zip(shards, gathered)]


def _gather_plan(shards, axes):
    def half(a, who):
        return _half(shards[a].shape, axes[a], who[2])

    over_ici, onward = [], []
    for a in range(len(shards)):
        for mask in CHIP_MASKS:
            over_ici.append(dict(
                mask=mask,
                src=lambda I, O, me, a=a: I[a].at[half(a, me)],
                dst=lambda I, O, who, a=a: O[a].at[(_chip(who),) + half(a, who)]))
            onward.append(dict(
                mask=SIBLING, after=len(over_ici) - 1,
                src=lambda I, O, me, a=a, mask=mask: O[a].at[(_chip(_flip(me, mask)),) + half(a, me)],
                dst=lambda I, O, who, a=a, mask=mask: O[a].at[(_chip(_flip(who, mask)),) + half(a, who)]))
    return over_ici + onward, [((NCHIP,) + s.shape, s.dtype) for s in shards]


def _gather_shards(shards, axes):
    transfers, outs = _gather_plan(shards, axes)
    return _select_own(shards, _exchange("gather_weights", shards, outs, transfers))


def _to_sibling(arrs, name):
    transfers = [dict(mask=SIBLING, src=lambda I, O, me, a=a: I[a], dst=lambda I, O, who, a=a: O[a])
                 for a in range(len(arrs))]
    return _exchange(name, arrs, [(t.shape, t.dtype) for t in arrs], transfers)


def _halves_to_sibling(blocks, axes, name):
    def cut(a, which):
        return (slice(None),) + _half(blocks[a].shape[1:], axes[a], which)

    transfers = [dict(mask=SIBLING, src=lambda I, O, me, a=a: I[a].at[cut(a, 1 - me[2])],
                      dst=lambda I, O, who, a=a: O[a]) for a in range(len(blocks))]
    outs = []
    for b, ax in zip(blocks, axes):
        shape = list(b.shape)
        shape[ax + 1] //= 2
        outs.append((tuple(shape), b.dtype))
    return _exchange(name, blocks, outs, transfers)


def _scatter_plan(tb):
    transfers = []
    for a in range(len(tb)):
        for n, mask in enumerate(CHIP_MASKS):
            transfers.append(dict(
                mask=mask,
                src=lambda I, O, me, a=a, mask=mask: I[a].at[_chip(_flip(me, mask))],
                dst=lambda I, O, who, a=a, n=n: O[a].at[n]))
    return transfers, [((3,) + t.shape[1:], t.dtype) for t in tb]


def _scatter_chip_sums(tb):
    transfers, outs = _scatter_plan(tb)
    return _exchange("scatter_grads", tb, outs, transfers)


def _gather_small(vec):
    def slot(who):
        return 4 * who[0] + 2 * who[1] + who[2]

    masks = [(m >> 2 & 1, m >> 1 & 1, m & 1) for m in range(1, 8)]
    transfers = [dict(mask=mask, src=lambda I, O, me: I[0], dst=lambda I, O, who: O[0].at[slot(who)])
                 for mask in masks]
    copies = [(lambda I, O, me: I[0], lambda I, O, me: O[0].at[slot(me)])]
    return _exchange("gather_small", [vec], [((8,) + vec.shape, vec.dtype)], transfers, copies)[0]


def _rope_tables(positions):
    half = ROT // 2
    inv_freq = jnp.power(jnp.float32(THETA), -jnp.arange(0, ROT, 2, dtype=F32) / ROT)
    ang = positions.astype(F32)[:, None] * inv_freq[None, :]
    cos, sin = jnp.cos(ang), jnp.sin(ang)
    one, zero, z8 = jnp.ones((S, HD - ROT), F32), jnp.zeros((S, HD - ROT), F32), jnp.zeros((S, half), F32)
    c = jnp.concatenate([cos, cos, one], axis=1)
    a = jnp.concatenate([-sin, z8, zero], axis=1)
    b = jnp.concatenate([z8, sin, zero], axis=1)
    return tuple(jnp.tile(t, (1, 2)) for t in (c, a, b))


def _tile_heads(g, w):
    return jnp.tile(g.reshape(1, HD), (1, w // HD))


def _fold_heads(dg):
    return dg.reshape(-1, HD).sum(axis=0)


def _pad_lanes(a):
    return jnp.pad(a, ((0, 0), (0, LANES - a.shape[1])))


def _local_step(x, target, positions, wt, fetch, late_weights, begin_reduce):
    rope = _rope_tables(positions)
    w1t = wt["w_in_a_t"]
    f_row = 3 * D // LANES
    wg_t = w1t[3 * D + NH:]
    in_b_block = lambda c: pl.BlockSpec((None, TN_, TN_), lambda j, i: (c, j, 0))
    b_pad = _pad_lanes(wt["b_forget"].reshape(1, NH))
    qg_a, kg_a = _tile_heads(wt["qnorm_a_g"], D), _tile_heads(wt["knorm_a_g"], D)
    qg_b, kg_b = _tile_heads(wt["qnorm_b_g"], D), _tile_heads(wt["knorm_b_g"], KVW)
    norm_a, kv_g, norm_b = wt["norm_a_g"].reshape(1, D), wt["kv_norm_g"].reshape(1, D), wt["norm_b_g"].reshape(1, D)
    sinks_t = jnp.repeat(wt["sinks"].reshape(1, NH), HD, axis=1)

    (u_a,) = _rmsnorm_fwd(x, [norm_a], "norm_a")
    qkv = _mm("proj_a", S, 3 * D, [(u_a, _a_rows(D), w1t, _b_rows(D), NT)])
    fpad = _mm("proj_f", S, LANES, [(u_a, _a_rows(D), w1t, _b_rows(D, row0=f_row, tn=LANES), NT)], tn=LANES)
    gate_a = _mm("proj_gate_a", S, D, [(u_a, _a_rows(D), wg_t, _b_rows(D), NT)])
    q_a, k_a, v_a = _a_post(qkv, qg_a, kg_a)
    ct = _forget_cumsum(fpad, b_pad)
    ct2 = ct[:NH].reshape(NH // 2, 2, S)
    o_a, lse_a, fetched = _fox_fwd(q_a, k_a, v_a, ct2, fetch)
    wt = {**wt, **late_weights(fetched)}
    w_in_b = wt["w_in_b"]
    y_a = _gate_fwd(o_a, gate_a, 0, "gate_a")
    h1 = _mm("out_a", S, D, [(y_a, _a_rows(D), wt["w_out_a"], _b_cols(D), None)], add=x)
    u_kv, u_b = _rmsnorm_fwd(h1, [kv_g, norm_b], "norm_b")
    kv = _mm("proj_kv", S, 2 * KVW, [(u_kv, _a_rows(D), wt["w_kv"], _b_cols(D), None)])
    pb = _mm("proj_b", S, 2 * D,
             [(u_b, _a_rows(D), w_in_b, pl.BlockSpec((None, D, TN_), lambda j, i: (j, 0, 0)), None)])
    q_b, kdup, vdup = _b_post(pb, kv, qg_b, kg_b, rope)
    o_b, lse_b = _swa_fwd(q_b, kdup, vdup, sinks_t)
    y_b = _gate_fwd(o_b, pb, 1, "gate_b")
    out = _mm("out_b", S, D, [(y_b, _a_rows(D), wt["w_out_b"], _b_cols(D), None)], add=h1)
    d_out, d_out_b, sq = _loss_head(out, target)

    g = {}
    g["w_out_b"] = _mm("dw_out_b", D, D, [(y_b, _a_cols(S), d_out_b, _b_cols(S), TN)])
    d_y_b = _mm("dy_b", S, D, [(d_out_b, _a_rows(D), wt["w_out_b"], _b_rows(D), NT)])
    d_o_b, d_gate_b = _gate_bwd(d_y_b, o_b, pb, 1, "gate_b_bwd")
    dq_b, dkdup, dvdup, dsk = _swa_bwd(q_b, kdup, vdup, sinks_t, o_b, lse_b, d_o_b)
    g["sinks"] = dsk[0, ::HD]
    d_qb_raw, dg = _headnorm_bwd(pb, 0, qg_b, dq_b, rope, "qnorm_b_bwd")
    g["qnorm_b_g"] = _fold_heads(dg)
    d_pb = [d_qb_raw, d_qb_raw, d_gate_b, d_gate_b]
    g["w_in_b"] = jnp.concatenate([
        _mm("dw_in_b_q", D, D, [(u_b, _a_cols(S), d_qb_raw, _b_cols(S), TN)], stacked=True),
        _mm("dw_in_b_gate", D, D, [(u_b, _a_cols(S), d_gate_b, _b_cols(S), TN)], stacked=True)], axis=0)
    d_u_b = _mm("du_b", S, D, [(d_pb[c], _a_rows(TN_, col=c % 2), w_in_b, in_b_block(c), NT) for c in range(NCHIP)])
    d_kv, dg = _kv_bwd(dkdup, dvdup, kv, kg_b, rope)
    g["knorm_b_g"] = _fold_heads(dg)
    g["w_kv"] = _mm("dw_kv", D, 2 * KVW, [(u_kv, _a_cols(S), d_kv, _b_cols(S), TN)])
    d_u_kv = _mm("du_kv", S, D, [(d_kv, _a_rows(2 * KVW), wt["w_kv"], _b_rows(2 * KVW), NT)])
    d_h1, d_h1_b, g["kv_norm_g"], g["norm_b_g"] = _rmsnorm_bwd(h1, [kv_g, norm_b], [d_u_kv, d_u_b], d_out, "norm_b_bwd")
    g["w_out_a"] = _mm("dw_out_a", D, D, [(y_a, _a_cols(S), d_h1_b, _b_cols(S), TN)])
    d_y_a = _mm("dy_a", S, D, [(d_h1_b, _a_rows(D), wt["w_out_a"], _b_rows(D), NT)])
    d_o_a, d_gate_a = _gate_bwd(d_y_a, o_a, gate_a, 0, "gate_a_bwd")
    riding, so_far = begin_reduce({n: g[n] for n in LATE})
    dq_a, dk_a, dv_a, dct, arrived = _fox_bwd(q_a, k_a, v_a, ct2, o_a, lse_a, d_o_a, riding)
    dct_pad = jnp.pad(dct.reshape(NH, S), ((0, LANES - NH), (0, 0)))
    d_f, db = _forget_bwd(dct_pad, fpad, b_pad)
    g["b_forget"] = db[0, :NH]
    d_q_raw, dg = _headnorm_bwd(qkv, 0, qg_a, dq_a, None, "qnorm_a_bwd")
    g["qnorm_a_g"] = _fold_heads(dg)
    d_k_raw, dg = _headnorm_bwd(qkv, 1, kg_a, dk_a, None, "knorm_a_bwd")
    g["knorm_a_g"] = _fold_heads(dg)
    pieces = [("q", d_q_raw), ("k", d_k_raw), ("v", dv_a), ("gate", d_gate_a)]
    dw = {n: _mm("dw_in_a_" + n, D, D, [(t, _a_cols(S), u_a, _b_cols(S), TN)]) for n, t in pieces}
    dw_f = _mm("dw_in_a_f", LANES, D, [(d_f, _a_cols(S, tm=LANES), u_a, _b_cols(S), TN)], tm=LANES)
    g["w_in_a"] = jnp.concatenate([dw["q"], dw["k"], dw["v"], dw_f[:NH], dw["gate"]], axis=0).reshape(
        NCHIP, (4 * D + NH) // NCHIP, D)
    d_u_a = _mm("du_a", S, D, [
        (d_q_raw, _a_rows(D), w1t, _b_cols(D, row=0), None), (d_k_raw, _a_rows(D), w1t, _b_cols(D, row=1), None),
        (dv_a, _a_rows(D), w1t, _b_cols(D, row=2), None), (d_gate_a, _a_rows(D), wg_t, _b_cols(D), None),
        (d_f, _a_rows(LANES), w1t, _b_cols(LANES, row=f_row), None)])
    d_x, _, g["norm_a_g"] = _rmsnorm_bwd(x, [norm_a], [d_u_a], d_h1, "norm_a_bwd")
    return sq, d_x, g, (so_far, arrived)


BIG = ["w_in_a", "w_out_a", "w_kv", "w_in_b", "w_out_b"]
LATE = BIG[1:]
SPLIT = {"w_in_a": 1, "w_out_a": 0, "w_kv": 0, "w_in_b": 0, "w_out_b": 0}
SMALL = ["norm_a_g", "b_forget", "qnorm_a_g", "knorm_a_g", "kv_norm_g", "knorm_b_g", "norm_b_g", "qnorm_b_g", "sinks"]
NAMES = ["norm_a_g", "w_in_a", "b_forget", "qnorm_a_g", "knorm_a_g", "w_out_a", "kv_norm_g", "w_kv", "knorm_b_g",
         "norm_b_g", "w_in_b", "qnorm_b_g", "sinks", "w_out_b"]


def _pack(vals):
    flat = []
    for v in vals:
        v = v.reshape(-1)
        flat.append(jnp.pad(v, (0, -v.shape[0] % LANES)))
    flat = jnp.concatenate(flat)
    flat = jnp.pad(flat, (0, -flat.shape[0] % (8 * LANES)))
    return flat.reshape(-1, LANES)


def _unpack(packed, shapes):
    flat, out, off = packed.reshape(-1), [], 0
    for s in shapes:
        n = int(np.prod(s))
        out.append(flat[off:off + n].reshape(s))
        off += n + (-n % LANES)
    return out


def kernel(x, positions, norm_a_g, w_in_a, b_forget, qnorm_a_g, knorm_a_g, w_out_a, kv_norm_g, w_kv, knorm_b_g, norm_b_g, w_in_b, qnorm_b_g, sinks, w_out_b, loss_target, m_norm_a_g, m_w_in_a, m_b_forget, m_qnorm_a_g, m_knorm_a_g, m_w_out_a, m_kv_norm_g, m_w_kv, m_knorm_b_g, m_norm_b_g, m_w_in_b, m_qnorm_b_g, m_sinks, m_w_out_b, v_norm_a_g, v_w_in_a, v_b_forget, v_qnorm_a_g, v_knorm_a_g, v_w_out_a, v_kv_norm_g, v_w_kv, v_knorm_b_g, v_norm_b_g, v_w_in_b, v_qnorm_b_g, v_sinks, v_w_out_b):
    w = dict(norm_a_g=norm_a_g, w_in_a=w_in_a, b_forget=b_forget, qnorm_a_g=qnorm_a_g, knorm_a_g=knorm_a_g,
             w_out_a=w_out_a, kv_norm_g=kv_norm_g, w_kv=w_kv, knorm_b_g=knorm_b_g, norm_b_g=norm_b_g,
             w_in_b=w_in_b, qnorm_b_g=qnorm_b_g, sinks=sinks, w_out_b=w_out_b)
    m = dict(norm_a_g=m_norm_a_g, w_in_a=m_w_in_a, b_forget=m_b_forget, qnorm_a_g=m_qnorm_a_g, knorm_a_g=m_knorm_a_g,
             w_out_a=m_w_out_a, kv_norm_g=m_kv_norm_g, w_kv=m_w_kv, knorm_b_g=m_knorm_b_g, norm_b_g=m_norm_b_g,
             w_in_b=m_w_in_b, qnorm_b_g=m_qnorm_b_g, sinks=m_sinks, w_out_b=m_w_out_b)
    v = dict(norm_a_g=v_norm_a_g, w_in_a=v_w_in_a, b_forget=v_b_forget, qnorm_a_g=v_qnorm_a_g, knorm_a_g=v_knorm_a_g,
             w_out_a=v_w_out_a, kv_norm_g=v_kv_norm_g, w_kv=v_w_kv, knorm_b_g=v_knorm_b_g, norm_b_g=v_norm_b_g,
             w_in_b=v_w_in_b, qnorm_b_g=v_qnorm_b_g, sinks=v_sinks, w_out_b=v_w_out_b)
    my_chip = 2 * lax.axis_index("x") + lax.axis_index("y")

    def shard2d(t, n):
        t = t.reshape(t.shape[-2:])
        return t.T if n == "w_in_a" else t

    w2d = {n: shard2d(w[n], n) for n in BIG}

    norm_a_rows = jnp.broadcast_to(norm_a_g.reshape(1, D // NCHIP), (16, D // NCHIP))
    w1t, norm_rows = _gather_shards([w2d["w_in_a"].astype(BF16), norm_a_rows], [SPLIT["w_in_a"], 0])
    wt = {"w_in_a_t": w1t.reshape(-1, D), "norm_a_g": norm_rows[:, 0, :].reshape(1, D)}
    for n in SMALL[1:]:
        wt[n] = w[n]
    late_shards = [w2d[n].astype(BF16) for n in LATE]
    late_axes = [SPLIT[n] for n in LATE]
    transfers, outs = _gather_plan(late_shards, late_axes)
    fetch = _Riding(transfers, late_shards, outs)

    def late_weights(fetched):
        whole = dict(zip(LATE, _select_own(late_shards, fetched)))
        return {n: t if n == "w_in_b" else t.reshape(-1, t.shape[2]) for n, t in whole.items()}

    def as_blocks(t):
        return t if t.ndim == 3 else t.reshape(NCHIP, -1, t.shape[1])

    def chip_sums(names, grads, name):
        axes = [SPLIT[n] for n in names]
        blocks = [as_blocks(grads[n]) for n in names]
        sums = [_chip_sum(blk, part, ax, "chip_sum_" + n)
                for n, ax, blk, part in zip(names, axes, blocks, _halves_to_sibling(blocks, axes, name))]
        return [s[0] for s in sums], [s[1] for s in sums]

    def begin_reduce(grads):
        f32, bf16 = chip_sums(LATE, grads, "sibling_halves_late")
        transfers, outs = _scatter_plan(bf16)
        return _Riding(transfers, bf16, outs), f32

    sq, d_x, g, (late_f32, late_arrived) = _local_step(x[0], loss_target[0], positions, wt, fetch, late_weights,
                                                      begin_reduce)

    small_shapes = [(D,), (NH,), (HD,), (HD,), (D,), (HD,), (D,), (HD,), (NH,), (D,)]
    packed = _pack([g[n] for n in SMALL] + [sq])
    total = _sum_stack(_gather_small(packed), "sum_small")
    small_g = dict(zip(SMALL, _unpack(total, small_shapes)[:-1]))
    loss = 0.5 * jnp.sum(_unpack(total, small_shapes)[-1]) / D
    small_g["norm_a_g"] = lax.dynamic_slice(small_g["norm_a_g"], (my_chip * (D // NCHIP),), (D // NCHIP,))

    axes = [SPLIT[n] for n in BIG]
    first_f32, first_bf16 = chip_sums(["w_in_a"], g, "sibling_halves")
    chip_f32 = first_f32 + list(late_f32)
    arrived = list(_scatter_chip_sums(first_bf16)) + list(late_arrived)
    halves = []
    for n, ax, t32, parts in zip(BIG, axes, chip_f32, arrived):
        own = lax.dynamic_index_in_dim(t32, my_chip, axis=0, keepdims=False)
        halves.append(_mesh_sum(own, parts, ax, "mesh_sum_" + n))
    sibling_done = _to_sibling(halves, "finished_halves")

    res = {}
    for n, ax, mine_half, their_half in zip(BIG, axes, halves, sibling_done):
        out4 = _adamw_halves(w2d[n], mine_half, their_half, shard2d(m[n], n), shard2d(v[n], n), ax, "adamw_" + n)
        res[n] = tuple((t.T if n == "w_in_a" else t).reshape(w[n].shape) for t in out4)
    sm_g = _pack([small_g[n] for n in SMALL])
    sm = [_pack([d[n] for n in SMALL]) for d in (w, m, v)]
    sm_out = _adamw(sm[0], sm_g, sm[1], sm[2], "adamw_small")
    sm_shapes = [w[n].shape for n in SMALL]
    unpacked = [_unpack(t, sm_shapes) for t in (sm_g,) + tuple(sm_out)]
    for i, n in enumerate(SMALL):
        res[n] = tuple(u[i] for u in unpacked)

    outs = [loss, d_x[None]]
    for k in range(4):
        outs += [res[n][k] for n in NAMES]
    return tuple(outs)
```

```python
import numpy as np
import jax
import jax.numpy as jnp
from jax import lax
from jax.experimental import pallas as pl
from jax.experimental.pallas import tpu as pltpu

F32, BF16 = jnp.float32, jnp.bfloat16
S, D, HD, NH, NKV = 2048, 1024, 64, 16, 4
KVW = NKV * HD
WINDOW = 128
ROT = HD // 4
THETA = 500000.0
EPS = 1e-6
SCALE = HD ** -0.5
LANES = 128
NEG = -1e30
VMEM_LIMIT = 48 * 2 ** 20
ROWS = 256
ATT = 256
SWQ = 4
NCHIP = 4
ADAM_LR, ADAM_B1, ADAM_B2, ADAM_EPS, ADAM_WD, ADAM_STEP = 0.001, 0.9, 0.999, 1e-08, 0.01, 10
NT = (((1,), (1,)), ((), ()))
TN = (((0,), (0,)), ((), ()))
MESH = pl.DeviceIdType.MESH


def _params(n):
    return pltpu.CompilerParams(dimension_semantics=("arbitrary",) * n, vmem_limit_bytes=VMEM_LIMIT)


def _dot(a, b, dims=None):
    if dims is None:
        return jnp.dot(a, b, preferred_element_type=F32)
    return lax.dot_general(a, b, dims, preferred_element_type=F32)


def _dot_split(a, b, n):
    out, rest = None, a
    for _ in range(n):
        hi = rest.astype(BF16)
        term = _dot(hi, b)
        out = term if out is None else out + term
        rest = rest - hi.astype(F32)
    return out


def _seg_mat(w):
    e = (np.arange(w)[:, None] // HD == np.arange(LANES)[None, :]).astype(np.float32)
    return jnp.asarray(e, BF16)


def _spread(r, w):
    first = lax.broadcasted_iota(jnp.int32, (1, LANES), 1) < HD
    return jnp.concatenate([jnp.where(first, r[:, 2 * c:2 * c + 1], r[:, 2 * c + 1:2 * c + 2])
                            for c in range(w // LANES)], axis=1)


def _head_rstd(x, e):
    ss = _dot_split(x * x, e, 2)
    return _spread(lax.rsqrt(ss * (1.0 / HD) + EPS), x.shape[1])


def _rope(x, c, a, b):
    w = x.shape[1]
    return x * c + pltpu.roll(x, w - ROT // 2, 1) * a + pltpu.roll(x, ROT // 2, 1) * b


def _rope_t(dy, c, a, b):
    w = dy.shape[1]
    return dy * c + pltpu.roll(dy * b, w - ROT // 2, 1) + pltpu.roll(dy * a, ROT // 2, 1)


def _sigmoid(x):
    return 1.0 / (1.0 + jnp.exp(-x))


def _row_spec(shape, ts):
    nd = len(shape)
    if shape[0] == S:
        return pl.BlockSpec((ts,) + tuple(shape[1:]), lambda i: (i,) + (0,) * (nd - 1))
    return pl.BlockSpec(tuple(shape), lambda i: (0,) * nd)


def _rows_call(body, name, ins, outs, ts=ROWS):
    return pl.pallas_call(
        body, name=name, grid=(S // ts,),
        in_specs=[_row_spec(a.shape, ts) for a in ins],
        out_specs=[_row_spec(s, ts) for s, _ in outs],
        out_shape=[jax.ShapeDtypeStruct(s, d) for s, d in outs],
        compiler_params=_params(1))(*ins)


def _col_spec(ts, w, col):
    return pl.BlockSpec((ts, w), lambda i: (i, col))


TM = TN_ = 512
TM_TOKENS = 1024


def _mm(name, m, n, terms, out_dtype=F32, add=None, tm=None, tn=TN_, stacked=False):
    nterm = len(terms)
    if tm is None:
        tm = TM_TOKENS if m == S else TM

    def body(*refs):
        acc = None
        for t in range(nterm):
            part = _dot(refs[2 * t][...], refs[2 * t + 1][...], terms[t][4])
            acc = part if acc is None else acc + part
        if add is not None:
            acc = acc + refs[2 * nterm][...]
        refs[-1][...] = acc.astype(out_dtype)

    tile = pl.BlockSpec((tm, tn), lambda j, i: (i, j))
    ins, specs = [], []
    for a, a_spec, b, b_spec, _ in terms:
        ins += [a, b]
        specs += [a_spec, b_spec]
    if add is not None:
        ins.append(add)
        specs.append(tile)
    return pl.pallas_call(
        body, name=name, grid=(n // tn, m // tm), in_specs=specs,
        out_specs=pl.BlockSpec((None, tm, tn), lambda j, i: (j, i, 0)) if stacked else tile,
        out_shape=jax.ShapeDtypeStruct((n // tn, m, tn) if stacked else (m, n), out_dtype),
        compiler_params=_params(2))(*ins)


def _a_rows(k, col=0, tm=TM_TOKENS):
    return pl.BlockSpec((tm, k), lambda j, i: (i, col))


def _a_cols(k, tm=TM):
    return pl.BlockSpec((k, tm), lambda j, i: (0, i))


def _b_cols(k, row=0, col0=0, tn=TN_):
    return pl.BlockSpec((k, tn), lambda j, i: (row, col0 + j))


def _b_rows(k, row0=0, tn=TN_):
    return pl.BlockSpec((tn, k), lambda j, i: (row0 + j, 0))


def _rmsnorm_fwd(x, gains, name):
    def body(*refs):
        xv = refs[0][...]
        r = lax.rsqrt(jnp.mean(xv * xv, axis=-1, keepdims=True) + EPS)
        xh = xv * r
        for n in range(len(gains)):
            refs[1 + len(gains) + n][...] = (xh * refs[1 + n][...]).astype(BF16)

    return _rows_call(body, name, [x] + list(gains), [((S, D), BF16)] * len(gains))


def _rmsnorm_bwd(x, gains, dus, dres, name):
    n = len(gains)

    def body(*refs):
        x_ref, g_refs, du_refs, dres_ref = refs[0], refs[1:1 + n], refs[1 + n:1 + 2 * n], refs[1 + 2 * n]
        dx_ref, dxb_ref, dg_refs = refs[2 + 2 * n], refs[3 + 2 * n], refs[4 + 2 * n:]
        xv = x_ref[...]
        r = lax.rsqrt(jnp.mean(xv * xv, axis=-1, keepdims=True) + EPS)
        xh = xv * r
        gy = None
        for m in range(n):
            du = du_refs[m][...]
            part = jnp.sum(du * xh, axis=0, keepdims=True)

            @pl.when(pl.program_id(0) == 0)
            def _(m=m, part=part):
                dg_refs[m][...] = part

            @pl.when(pl.program_id(0) != 0)
            def _(m=m, part=part):
                dg_refs[m][...] += part

            t = du * g_refs[m][...]
            gy = t if gy is None else gy + t
        dx = dres_ref[...] + r * (gy - xh * jnp.mean(gy * xh, axis=-1, keepdims=True))
        dx_ref[...] = dx
        dxb_ref[...] = dx.astype(BF16)

    outs = [((S, D), F32), ((S, D), BF16)] + [((1, D), F32)] * n
    return _rows_call(body, name, [x] + list(gains) + list(dus) + [dres], outs)


def _a_post(qkvg, qg, kg):
    e = _seg_mat(D)

    def body(q_ref, k_ref, v_ref, qg_ref, kg_ref, e_ref, qo, ko, vo):
        ev = e_ref[...]
        qv, kv = q_ref[...], k_ref[...]
        qo[...] = (qv * _head_rstd(qv, ev) * qg_ref[...] * SCALE).astype(BF16)
        ko[...] = (kv * _head_rstd(kv, ev) * kg_ref[...]).astype(BF16)
        vo[...] = v_ref[...].astype(BF16)

    whole = lambda a: pl.BlockSpec(a.shape, lambda i: (0, 0))
    return pl.pallas_call(
        body, name="a_post", grid=(S // ROWS,),
        in_specs=[_col_spec(ROWS, D, 0), _col_spec(ROWS, D, 1), _col_spec(ROWS, D, 2),
                  whole(qg), whole(kg), whole(e)],
        out_specs=[_col_spec(ROWS, D, 0)] * 3,
        out_shape=[jax.ShapeDtypeStruct((S, D), BF16)] * 3,
        compiler_params=_params(1))(qkvg, qkvg, qkvg, qg, kg, e)


def _tri(upper):
    r, c = np.arange(ROWS)[:, None], np.arange(ROWS)[None, :]
    return jnp.asarray((r <= c) if upper else (r >= c), BF16)


def _forget_cumsum(fpad, bpad):
    def body(f_ref, b_ref, u_ref, c_ref, carry):
        @pl.when(pl.program_id(0) == 0)
        def _():
            carry[...] = jnp.zeros_like(carry)

        lf = jax.nn.log_sigmoid(f_ref[...] + b_ref[...])
        blk = _dot_split(lf.T, u_ref[...], 3) + carry[:, 0:1]
        c_ref[...] = blk
        carry[...] = jnp.broadcast_to(blk[:, ROWS - 1:ROWS], carry.shape)

    return pl.pallas_call(
        body, name="forget_cumsum", grid=(S // ROWS,),
        in_specs=[pl.BlockSpec((ROWS, LANES), lambda i: (i, 0)), pl.BlockSpec((1, LANES), lambda i: (0, 0)),
                  pl.BlockSpec((ROWS, ROWS), lambda i: (0, 0))],
        out_specs=pl.BlockSpec((LANES, ROWS), lambda i: (0, i)),
        out_shape=jax.ShapeDtypeStruct((LANES, S), F32),
        scratch_shapes=[pltpu.VMEM((LANES, LANES), F32)],
        compiler_params=_params(1))(fpad, bpad, _tri(True))


def _forget_bwd(dct, fpad, bpad):
    nb = S // ROWS

    def body(dc_ref, f_ref, b_ref, l_ref, df_ref, db_ref, carry):
        @pl.when(pl.program_id(0) == 0)
        def _():
            carry[...] = jnp.zeros_like(carry)
            db_ref[...] = jnp.zeros_like(db_ref)

        blk = _dot_split(dc_ref[...], l_ref[...], 3) + carry[:, 0:1]
        carry[...] = jnp.broadcast_to(blk[:, 0:1], carry.shape)
        df = blk.T * _sigmoid(-(f_ref[...] + b_ref[...]))
        df_ref[...] = df.astype(BF16)
        db_ref[...] += jnp.sum(df, axis=0, keepdims=True)

    return pl.pallas_call(
        body, name="forget_bwd", grid=(nb,),
        in_specs=[pl.BlockSpec((LANES, ROWS), lambda i: (0, nb - 1 - i)),
                  pl.BlockSpec((ROWS, LANES), lambda i: (nb - 1 - i, 0)),
                  pl.BlockSpec((1, LANES), lambda i: (0, 0)), pl.BlockSpec((ROWS, ROWS), lambda i: (0, 0))],
        out_specs=[pl.BlockSpec((ROWS, LANES), lambda i: (nb - 1 - i, 0)), pl.BlockSpec((1, LANES), lambda i: (0, 0))],
        out_shape=[jax.ShapeDtypeStruct((S, LANES), BF16), jax.ShapeDtypeStruct((1, LANES), F32)],
        scratch_shapes=[pltpu.VMEM((LANES, LANES), F32)],
        compiler_params=_params(1))(dct, fpad, bpad, _tri(False))


def _gate_fwd(o, proj, col, name):
    def body(o_ref, g_ref, y_ref):
        g = g_ref[...]
        y_ref[...] = (o_ref[...] * (g * _sigmoid(g))).astype(BF16)

    return pl.pallas_call(
        body, name=name, grid=(S // ROWS,),
        in_specs=[_col_spec(ROWS, D, 0), _col_spec(ROWS, D, col)],
        out_specs=_col_spec(ROWS, D, 0), out_shape=jax.ShapeDtypeStruct((S, D), BF16),
        compiler_params=_params(1))(o, proj)


def _gate_bwd(dy, o, proj, col, name):
    def body(dy_ref, o_ref, g_ref, do_ref, dg_ref):
        g, dyv = g_ref[...], dy_ref[...]
        sg = _sigmoid(g)
        do_ref[...] = dyv * (g * sg)
        dg_ref[...] = (dyv * o_ref[...] * (sg * (1.0 + g * (1.0 - sg)))).astype(BF16)

    return pl.pallas_call(
        body, name=name, grid=(S // ROWS,),
        in_specs=[_col_spec(ROWS, D, 0), _col_spec(ROWS, D, 0), _col_spec(ROWS, D, col)],
        out_specs=[_col_spec(ROWS, D, 0)] * 2,
        out_shape=[jax.ShapeDtypeStruct((S, D), F32), jax.ShapeDtypeStruct((S, D), BF16)],
        compiler_params=_params(1))(dy, o, proj)


def _headnorm_bwd(x, col, gain, dy, rope, name):
    e = _seg_mat(D)
    tabs = list(rope) if rope is not None else []

    def body(*refs):
        x_ref, g_ref, dy_ref, e_ref = refs[:4]
        dx_ref, dg_ref = refs[-2:]
        xv, dyv, ev = x_ref[...], dy_ref[...], e_ref[...]
        if rope is not None:
            c, a, b = (jnp.tile(t[...], (1, D // LANES)) for t in refs[4:7])
            dyv = _rope_t(dyv, c, a, b)
        r = _head_rstd(xv, ev)
        xh = xv * r
        part = jnp.sum(dyv * xh, axis=0, keepdims=True)

        @pl.when(pl.program_id(0) == 0)
        def _():
            dg_ref[...] = part

        @pl.when(pl.program_id(0) != 0)
        def _():
            dg_ref[...] += part

        gy = dyv * g_ref[...]
        seg = _spread(_dot_split(gy * xh, ev, 2) * (1.0 / HD), D)
        dx_ref[...] = (r * (gy - xh * seg)).astype(BF16)

    whole = lambda a: pl.BlockSpec(a.shape, lambda i: (0, 0))
    return pl.pallas_call(
        body, name=name, grid=(S // ROWS,),
        in_specs=[_col_spec(ROWS, D, col), whole(gain), _col_spec(ROWS, D, 0), whole(e)]
                 + [pl.BlockSpec((ROWS, LANES), lambda i: (i, 0))] * len(tabs),
        out_specs=[_col_spec(ROWS, D, 0), whole(gain)],
        out_shape=[jax.ShapeDtypeStruct((S, D), BF16), jax.ShapeDtypeStruct((1, D), F32)],
        compiler_params=_params(1))(x, gain, dy, e, *tabs)


def _dup_mat():
    r, c = np.arange(KVW)[:, None], np.arange(2 * KVW)[None, :]
    return (r // HD == c // LANES) & (r % HD == c % HD)


def _fold_mat():
    r, c = np.arange(D)[:, None], np.arange(KVW)[None, :]
    return (r // (2 * LANES) == c // HD) & (r % HD == c % HD)


def _b_post(pb, kv, qg, kg, rope):
    e, ek = _seg_mat(D), _seg_mat(KVW)
    dup = jnp.asarray(_dup_mat(), BF16)

    def body(q_ref, k_ref, v_ref, qg_ref, kg_ref, e_ref, ek_ref, dup_ref, c_ref, a_ref, b_ref, qo, ko, vo):
        c1, a1, b1 = c_ref[...], a_ref[...], b_ref[...]
        qv = q_ref[...]
        qn = qv * _head_rstd(qv, e_ref[...]) * qg_ref[...]
        t = lambda z, n: jnp.tile(z, (1, n))
        qo[...] = (_rope(qn, t(c1, D // LANES), t(a1, D // LANES), t(b1, D // LANES)) * SCALE).astype(BF16)
        kvv = k_ref[...]
        kn = kvv * _head_rstd(kvv, ek_ref[...]) * kg_ref[...]
        kr = _rope(kn, t(c1, KVW // LANES), t(a1, KVW // LANES), t(b1, KVW // LANES)).astype(BF16)
        ko[...] = _dot(kr, dup_ref[...]).astype(BF16)
        vo[...] = _dot(v_ref[...].astype(BF16), dup_ref[...]).astype(BF16)

    whole = lambda a: pl.BlockSpec(a.shape, lambda i: (0, 0))
    tab = pl.BlockSpec((ROWS, LANES), lambda i: (i, 0))
    return pl.pallas_call(
        body, name="b_post", grid=(S // ROWS,),
        in_specs=[_col_spec(ROWS, D, 0), _col_spec(ROWS, KVW, 0), _col_spec(ROWS, KVW, 1),
                  whole(qg), whole(kg), whole(e), whole(ek), whole(dup), tab, tab, tab],
        out_specs=[_col_spec(ROWS, D, 0), _col_spec(ROWS, 2 * KVW, 0), _col_spec(ROWS, 2 * KVW, 0)],
        out_shape=[jax.ShapeDtypeStruct((S, D), BF16), jax.ShapeDtypeStruct((S, 2 * KVW), BF16),
                   jax.ShapeDtypeStruct((S, 2 * KVW), BF16)],
        compiler_params=_params(1))(pb, kv, kv, qg, kg, e, ek, dup, *rope)


def _kv_bwd(dkdup, dvdup, kv, kg, rope):
    ek = _seg_mat(KVW)
    fold = jnp.asarray(_fold_mat(), BF16)

    def body(dk_ref, dv_ref, k_ref, kg_ref, ek_ref, fold_ref, c_ref, a_ref, b_ref, dkv_ref, dg_ref):
        ev, fv = ek_ref[...], fold_ref[...]
        t = lambda z: jnp.tile(z[...], (1, KVW // LANES))
        dk = _rope_t(_dot_split(dk_ref[...], fv, 3), t(c_ref), t(a_ref), t(b_ref))
        dv = _dot_split(dv_ref[...], fv, 3)
        xv = k_ref[...]
        r = _head_rstd(xv, ev)
        xh = xv * r
        part = jnp.sum(dk * xh, axis=0, keepdims=True)

        @pl.when(pl.program_id(0) == 0)
        def _():
            dg_ref[...] = part

        @pl.when(pl.program_id(0) != 0)
        def _():
            dg_ref[...] += part

        gy = dk * kg_ref[...]
        seg = _spread(_dot_split(gy * xh, ev, 2) * (1.0 / HD), KVW)
        dkv_ref[:, 0:KVW] = (r * (gy - xh * seg)).astype(BF16)
        dkv_ref[:, KVW:2 * KVW] = dv.astype(BF16)

    whole = lambda a: pl.BlockSpec(a.shape, lambda i: (0, 0))
    tab = pl.BlockSpec((ROWS, LANES), lambda i: (i, 0))
    return pl.pallas_call(
        body, name="kv_bwd", grid=(S // ROWS,),
        in_specs=[_col_spec(ROWS, D, 0), _col_spec(ROWS, D, 0), _col_spec(ROWS, KVW, 0),
                  whole(kg), whole(ek), whole(fold), tab, tab, tab],
        out_specs=[_col_spec(ROWS, 2 * KVW, 0), whole(kg)],
        out_shape=[jax.ShapeDtypeStruct((S, 2 * KVW), BF16), jax.ShapeDtypeStruct((1, KVW), F32)],
        compiler_params=_params(1))(dkdup, dvdup, kv, kg, ek, fold, *rope)


def _loss_head(out, target):
    def body(o_ref, t_ref, d_ref, db_ref, l_ref):
        diff = o_ref[...] - t_ref[...]
        d = diff * (1.0 / D)
        d_ref[...] = d
        db_ref[...] = d.astype(BF16)

        @pl.when(pl.program_id(0) == 0)
        def _():
            l_ref[...] = jnp.zeros_like(l_ref)

        l_ref[...] += jnp.sum(diff * diff, axis=0, keepdims=True)

    return _rows_call(body, "loss_head", [out, target], [((S, D), F32), ((S, D), BF16), ((1, D), F32)])


def _lane():
    return lax.broadcasted_iota(jnp.int32, (1, LANES), 1)


def _head_mask(hh):
    return (_lane() < HD) if hh == 0 else (_lane() >= HD)


def _fox_fwd(q, k, v, ct, riding):
    nq, npair = S // ATT, NH // 2
    ni, no = len(riding.ins), len(riding.outs)

    def body(q_ref, k_ref, v_ref, c_ref, *rest):
        o_ref, lse_ref = rest[ni:ni + 2]
        pair, i = pl.program_id(0), pl.program_id(1)
        at_end = riding.hooks(rest[:ni], rest[ni + 2:ni + 2 + no], *rest[ni + 2 + no:],
                              first=(pair == 0) & (i == 0), middle=(pair == npair // 2) & (i == 0),
                              last=(pair == npair - 1) & (i == nq - 1))
        q2 = q_ref[...]
        qms = [jnp.where(_head_mask(hh), q2, jnp.zeros_like(q2)) for hh in (0, 1)]

        def probs(off, width, m, hh, diag):
            s = _dot(qms[hh], k_ref[pl.ds(off, width), :], NT) - c_ref[hh:hh + 1, pl.ds(off, width)]
            if diag:
                row = i * ATT + lax.broadcasted_iota(jnp.int32, (ATT, width), 0)
                col = off + lax.broadcasted_iota(jnp.int32, (ATT, width), 1)
                s = jnp.where(col <= row, s, NEG)
            m_new = jnp.maximum(m, jnp.max(s, axis=1, keepdims=True))
            p = jnp.exp(s - m_new)
            p_hi = p.astype(BF16)
            return m_new, jnp.exp(m - m_new), p_hi, (p - p_hi.astype(F32)).astype(BF16)

        def weighted(off, width, p_hi, p_lo, hh):
            vj = v_ref[pl.ds(off, width), :]
            v1 = jnp.where(_head_mask(hh), vj, jnp.ones_like(vj))
            return _dot(p_hi, v1) + _dot(p_lo, v1)

        def step(off, width, carry, diag):
            off = pl.multiple_of(off, ATT)
            out = []
            for hh in (0, 1):
                m, acc = carry[hh]
                m, alpha, p_hi, p_lo = probs(off, width, m, hh, diag)
                out.append((m, alpha * acc + weighted(off, width, p_hi, p_lo, hh)))
            return tuple(out)

        one = (jnp.full((ATT, 1), NEG, F32), jnp.zeros((ATT, LANES), F32))
        carry = lax.fori_loop(0, i // 2, lambda j, cr: step(j * (2 * ATT), 2 * ATT, cr, False), (one, one))
        carry = lax.cond(i % 2 == 1, lambda cr: step((i - 1) * ATT, 2 * ATT, cr, True),
                         lambda cr: step(i * ATT, ATT, cr, True), carry)
        res = []
        for hh in (0, 1):
            m, acc = carry[hh]
            l = jnp.max(jnp.where(_head_mask(1 - hh), acc, 0.0), axis=1, keepdims=True)
            res.append((acc / l, m + jnp.log(l)))
        first = _head_mask(0)
        o_ref[...] = jnp.where(first, res[0][0], res[1][0])
        lse_ref[...] = jnp.where(first, res[0][1], res[1][1])
        at_end()

    blk = pl.BlockSpec((ATT, LANES), lambda p, i: (i, p))
    full = pl.BlockSpec((S, LANES), lambda p, i: (0, p))
    res = pl.pallas_call(
        body, name="fox_fwd", grid=(npair, nq),
        in_specs=[blk, full, full, pl.BlockSpec((None, 2, S), lambda p, i: (p, 0, 0))] + riding.in_specs,
        out_specs=[blk, blk] + riding.out_specs,
        out_shape=[jax.ShapeDtypeStruct((S, D), F32)] * 2 + riding.out_shape,
        scratch_shapes=riding.scratch,
        compiler_params=_params(2))(q, k, v, ct, *riding.ins)
    return res[0], res[1], res[2:]


def _fox_bwd(q, k, v, ct, o, lse, do, riding):
    nq, npair = S // ATT, NH // 2
    ni, no = len(riding.ins), len(riding.outs)

    def body(q_ref, k_ref, v_ref, c_ref, o_ref, lse_ref, do_ref, *rest):
        dq_ref, dk_ref, dvb_ref, dc_ref = rest[ni:ni + 4]
        dv_ref = rest[ni + 4 + no]
        pair, i = pl.program_id(0), pl.program_id(1)
        at_end = riding.hooks(rest[:ni], rest[ni + 4:ni + 4 + no], *rest[ni + 5 + no:],
                              first=(pair == 0) & (i == 0), middle=(pair == npair // 2) & (i == 0),
                              last=(pair == npair - 1) & (i == nq - 1))

        @pl.when(i == 0)
        def _():
            dk_ref[...] = jnp.zeros_like(dk_ref)
            dv_ref[...] = jnp.zeros_like(dv_ref)
            dc_ref[...] = jnp.zeros_like(dc_ref)

        q2, do2, lse2 = q_ref[...], do_ref[...], lse_ref[...]
        do2b = do2.astype(BF16)
        prod = do2b.astype(F32) * o_ref[...]
        heads = []
        for hh in (0, 1):
            hm = _head_mask(hh)
            heads.append((jnp.where(hm, q2, jnp.zeros_like(q2)), jnp.where(hm, do2b, jnp.zeros_like(do2b)),
                          jnp.sum(jnp.where(hm, prod, 0.0), axis=1, keepdims=True),
                          jnp.max(jnp.where(hm, lse2, NEG), axis=1, keepdims=True)))

        def step(off, width, dqs, diag):
            off = pl.multiple_of(off, ATT)
            kj, vj = k_ref[pl.ds(off, width), :], v_ref[pl.ds(off, width), :]
            dk, dv, out = None, None, []
            for hh in (0, 1):
                qm, dom, delta, lse_h = heads[hh]
                s = _dot(qm, kj, NT) - c_ref[hh:hh + 1, pl.ds(off, width)]
                p = jnp.exp(s - lse_h)
                if diag:
                    row = i * ATT + lax.broadcasted_iota(jnp.int32, (ATT, width), 0)
                    col = off + lax.broadcasted_iota(jnp.int32, (ATT, width), 1)
                    p = jnp.where(col <= row, p, 0.0)
                ds = p * (_dot(dom, vj, NT) - delta)
                dc_ref[hh:hh + 1, pl.ds(off, width)] += -jnp.sum(ds, axis=0, keepdims=True)
                dsb = ds.astype(BF16)
                dk_h, dv_h = _dot(dsb, qm, TN), _dot(p.astype(BF16), dom, TN)
                dk, dv = (dk_h, dv_h) if dk is None else (dk + dk_h, dv + dv_h)
                out.append(dqs[hh] + _dot(dsb, kj))
            dk_ref[pl.ds(off, width), :] += dk
            dv_ref[pl.ds(off, width), :] += dv
            return tuple(out)

        zero = jnp.zeros((ATT, LANES), F32)
        dqs = lax.fori_loop(0, i // 2, lambda j, acc: step(j * (2 * ATT), 2 * ATT, acc, False), (zero, zero))
        dqs = lax.cond(i % 2 == 1, lambda acc: step((i - 1) * ATT, 2 * ATT, acc, True),
                       lambda acc: step(i * ATT, ATT, acc, True), dqs)
        dq_ref[...] = jnp.where(_head_mask(0), dqs[0], dqs[1]) * SCALE

        @pl.when(i == nq - 1)
        def _():
            dvb_ref[...] = dv_ref[...].astype(BF16)

        at_end()

    blk = pl.BlockSpec((ATT, LANES), lambda p, i: (i, p))
    full = pl.BlockSpec((S, LANES), lambda p, i: (0, p))
    cspec = pl.BlockSpec((None, 2, S), lambda p, i: (p, 0, 0))
    res = pl.pallas_call(
        body, name="fox_bwd", grid=(npair, nq),
        in_specs=[blk, full, full, cspec, blk, blk, blk] + riding.in_specs,
        out_specs=[blk, full, full, cspec] + riding.out_specs,
        out_shape=[jax.ShapeDtypeStruct((S, D), F32)] * 2 + [jax.ShapeDtypeStruct((S, D), BF16),
                                                              jax.ShapeDtypeStruct((npair, 2, S), F32)]
                  + riding.out_shape,
        scratch_shapes=[pltpu.VMEM((S, LANES), F32)] + riding.scratch,
        compiler_params=_params(2))(q, k, v, ct, o, lse, do, *riding.ins)
    return res[0], res[1], res[2], res[3], res[4:]


def _both_heads(x):
    return jnp.concatenate([jnp.where(_head_mask(hh), x, jnp.zeros_like(x)) for hh in (0, 1)], axis=0)


def _per_head(col0, col1):
    return jnp.concatenate([jnp.broadcast_to(col0, (WINDOW, 1)), jnp.broadcast_to(col1, (WINDOW, 1))], axis=0)


def _unstack(x2):
    return jnp.where(_head_mask(0), x2[:WINDOW], x2[WINDOW:])


def _swa_valid(i, start):
    r = lax.broadcasted_iota(jnp.int32, (2 * WINDOW, 2 * WINDOW), 0)
    qabs = i * WINDOW + jnp.where(r >= WINDOW, r - WINDOW, r)
    kabs = start + lax.broadcasted_iota(jnp.int32, (2 * WINDOW, 2 * WINDOW), 1)
    return (kabs <= qabs) & (qabs - kabs < WINDOW)


def _swa_fwd(q, kdup, vdup, sinks_t):
    def body(q_ref, k_ref, v_ref, sk_ref, o_ref, lse_ref):
        skv = sk_ref[...]
        first = _head_mask(0)
        for sb in range(SWQ):
            i = pl.program_id(1) * SWQ + sb
            rows = slice(sb * WINDOW, (sb + 1) * WINDOW)
            start = pl.multiple_of(jnp.maximum(i - 1, 0) * WINDOW, WINDOW)
            kk, vv = k_ref[pl.ds(start, 2 * WINDOW), :], v_ref[pl.ds(start, 2 * WINDOW), :]
            q2 = q_ref[rows, :]
            valid = _swa_valid(i, start)[:WINDOW]
            res = []
            for hh in (0, 1):
                hm = _head_mask(hh)
                sink = jnp.max(jnp.where(hm, skv, NEG), axis=1, keepdims=True)
                s = jnp.where(valid, _dot(jnp.where(hm, q2, jnp.zeros_like(q2)), kk, NT), NEG)
                m = jnp.maximum(jnp.max(s, axis=1, keepdims=True), sink)
                p = jnp.exp(s - m)
                l = jnp.sum(p, axis=1, keepdims=True) + jnp.exp(sink - m)
                res.append((_dot(p.astype(BF16), vv) / l, m + jnp.log(l)))
            o_ref[rows, :] = jnp.where(first, res[0][0], res[1][0])
            lse_ref[rows, :] = jnp.where(first, res[0][1], res[1][1])

    blk = pl.BlockSpec((SWQ * WINDOW, LANES), lambda p, i: (i, p))
    full = pl.BlockSpec((S, LANES), lambda p, i: (0, p // 2))
    return pl.pallas_call(
        body, name="swa_fwd", grid=(NH // 2, S // (SWQ * WINDOW)),
        in_specs=[blk, full, full, pl.BlockSpec((1, LANES), lambda p, i: (0, p))],
        out_specs=[blk, blk],
        out_shape=[jax.ShapeDtypeStruct((S, D), F32)] * 2,
        compiler_params=_params(2))(q, kdup, vdup, sinks_t)


def _swa_bwd(q, kdup, vdup, sinks_t, o, lse, do):
    def body(q_ref, k_ref, v_ref, sk_ref, o_ref, lse_ref, do_ref, dq_ref, dk_ref, dv_ref, dsk_ref):
        @pl.when(pl.program_id(1) == 0)
        def _():
            dk_ref[...] = jnp.zeros_like(dk_ref)
            dv_ref[...] = jnp.zeros_like(dv_ref)
            dsk_ref[...] = jnp.zeros_like(dsk_ref)

        skv = sk_ref[...]
        first = _head_mask(0)
        sink = _per_head(*[jnp.max(jnp.where(_head_mask(hh), skv, NEG), axis=1, keepdims=True) for hh in (0, 1)])
        for sb in range(SWQ):
            i = pl.program_id(1) * SWQ + sb
            rows = slice(sb * WINDOW, (sb + 1) * WINDOW)
            start = pl.multiple_of(jnp.maximum(i - 1, 0) * WINDOW, WINDOW)
            kk, vv = k_ref[pl.ds(start, 2 * WINDOW), :], v_ref[pl.ds(start, 2 * WINDOW), :]
            do2b = do_ref[rows, :].astype(BF16)
            prod, lse2 = do2b.astype(F32) * o_ref[rows, :], lse_ref[rows, :]
            qs, dos = _both_heads(q_ref[rows, :]), _both_heads(do2b)
            delta = jnp.concatenate([jnp.sum(jnp.where(_head_mask(hh), prod, 0.0), axis=1, keepdims=True)
                                     for hh in (0, 1)], axis=0)
            lse_h = jnp.concatenate([jnp.max(jnp.where(_head_mask(hh), lse2, NEG), axis=1, keepdims=True)
                                     for hh in (0, 1)], axis=0)
            p = jnp.where(_swa_valid(i, start), jnp.exp(_dot(qs, kk, NT) - lse_h), 0.0)
            dsb = (p * (_dot(dos, vv, NT) - delta)).astype(BF16)
            dk_ref[pl.ds(start, 2 * WINDOW), :] += _dot(dsb, qs, TN)
            dv_ref[pl.ds(start, 2 * WINDOW), :] += _dot(p.astype(BF16), dos, TN)
            dq_ref[rows, :] = _unstack(_dot(dsb, kk)) * SCALE
            t = jnp.exp(sink - lse_h) * delta
            dsk_ref[...] += -jnp.where(first, jnp.sum(t[:WINDOW], axis=0, keepdims=True),
                                       jnp.sum(t[WINDOW:], axis=0, keepdims=True))

    blk = pl.BlockSpec((SWQ * WINDOW, LANES), lambda p, i: (i, p))
    full = pl.BlockSpec((S, LANES), lambda p, i: (0, p // 2))
    acc = pl.BlockSpec((S, LANES), lambda p, i: (0, p))
    sk = pl.BlockSpec((1, LANES), lambda p, i: (0, p))
    return pl.pallas_call(
        body, name="swa_bwd", grid=(NH // 2, S // (SWQ * WINDOW)),
        in_specs=[blk, full, full, sk, blk, blk, blk],
        out_specs=[blk, acc, acc, sk],
        out_shape=[jax.ShapeDtypeStruct((S, D), F32)] * 3 + [jax.ShapeDtypeStruct((1, D), F32)],
        compiler_params=_params(2))(q, kdup, vdup, sinks_t, o, lse, do)


def _adamw_math(w, g, m, v):
    m = ADAM_B1 * m + (1.0 - ADAM_B1) * g
    v = ADAM_B2 * v + (1.0 - ADAM_B2) * jnp.square(g)
    m_hat = m / (1.0 - ADAM_B1 ** ADAM_STEP)
    v_hat = v / (1.0 - ADAM_B2 ** ADAM_STEP)
    delta = -ADAM_LR * (m_hat / (jnp.sqrt(v_hat) + ADAM_EPS) + ADAM_WD * w)
    return delta, m, v


def _adamw(w, g, m, v, name):
    r, c = w.shape
    tr = min(r, 128)

    def body(w_ref, g_ref, m_ref, v_ref, d_ref, mo_ref, vo_ref):
        d_ref[...], mo_ref[...], vo_ref[...] = _adamw_math(w_ref[...], g_ref[...], m_ref[...], v_ref[...])

    spec = pl.BlockSpec((tr, c), lambda i: (i, 0))
    return pl.pallas_call(
        body, name=name, grid=(r // tr,), in_specs=[spec] * 4, out_specs=[spec] * 3,
        out_shape=[jax.ShapeDtypeStruct((r, c), F32)] * 3, compiler_params=_params(1))(w, g, m, v)


SUM_TILE = 128


def _tiles(shape2d, axis, lead=0):
    r, c = shape2d
    blk = (SUM_TILE, c) if axis == 0 else (r, SUM_TILE)
    count = shape2d[axis] // SUM_TILE

    def index(pos, *lead_idx):
        return tuple(lead_idx) + ((pos, 0) if axis == 0 else (0, pos))

    return (None,) * lead + blk, count, index


def _adamw_halves(w, g_mine, g_theirs, m, v, axis, name):
    blk, count, index = _tiles(w.shape, axis)
    per_half = count // 2

    def body(w_ref, a_ref, b_ref, m_ref, v_ref, g_ref, d_ref, mo_ref, vo_ref):
        is_mine = pl.program_id(0) // per_half == lax.axis_index("c")
        g = jnp.where(is_mine, a_ref[...], b_ref[...])
        g_ref[...] = g
        d_ref[...], mo_ref[...], vo_ref[...] = _adamw_math(w_ref[...], g, m_ref[...], v_ref[...])

    spec = pl.BlockSpec(blk, lambda i: index(i))
    half = pl.BlockSpec(blk, lambda i: index(i % per_half))
    return pl.pallas_call(
        body, name=name, grid=(count,), in_specs=[spec, half, half, spec, spec], out_specs=[spec] * 4,
        out_shape=[jax.ShapeDtypeStruct(w.shape, F32)] * 4, compiler_params=_params(1))(w, g_mine, g_theirs, m, v)


def _chip_sum(blocks, from_sibling, axis, name):
    half2d = from_sibling.shape[1:]
    blk, count, index = _tiles(half2d, axis, lead=1)

    def body(lo_ref, hi_ref, p_ref, o32, o16):
        mine = jnp.where(lax.axis_index("c") == 0, lo_ref[...], hi_ref[...])
        acc = mine + p_ref[...]
        o32[...] = acc
        o16[...] = acc.astype(BF16)

    lo = pl.BlockSpec(blk, lambda k, i: index(i, k))
    hi = pl.BlockSpec(blk, lambda k, i: index(i + count, k))
    return pl.pallas_call(
        body, name=name, grid=(NCHIP, count), in_specs=[lo, hi, lo], out_specs=[lo, lo],
        out_shape=[jax.ShapeDtypeStruct(from_sibling.shape, F32), jax.ShapeDtypeStruct(from_sibling.shape, BF16)],
        compiler_params=_params(2))(blocks, blocks, from_sibling)


def _mesh_sum(own, parts, axis, name):
    blk, count, index = _tiles(own.shape, axis)
    n = parts.shape[0]

    def body(a_ref, p_ref, o_ref):
        acc = a_ref[...]
        for k in range(n):
            acc = acc + p_ref[k].astype(F32)
        o_ref[...] = acc

    spec = pl.BlockSpec(blk, lambda i: index(i))
    return pl.pallas_call(
        body, name=name, grid=(count,),
        in_specs=[spec, pl.BlockSpec((n,) + blk, lambda i: index(i, 0))],
        out_specs=spec, out_shape=jax.ShapeDtypeStruct(own.shape, F32),
        compiler_params=_params(1))(own, parts)


def _sum_stack(parts, name):
    n = parts.shape[0]

    def body(p_ref, o_ref):
        acc = p_ref[0]
        for k in range(1, n):
            acc = acc + p_ref[k]
        o_ref[...] = acc

    return pl.pallas_call(body, name=name, out_shape=jax.ShapeDtypeStruct(parts.shape[1:], F32))(parts)


def _coords():
    return lax.axis_index("x"), lax.axis_index("y"), lax.axis_index("c")


def _chip(who):
    return 2 * who[0] + who[1]


def _flip(who, mask):
    return tuple((1 - v) if b else v for v, b in zip(who, mask))


def _transfer(transfers, t, I, O, ssem, rsem, receiving):
    tr, me = transfers[t], _coords()
    peer = _flip(me, tr["mask"])
    return pltpu.make_async_remote_copy(
        src_ref=tr["src"](I, O, me), dst_ref=tr["dst"](I, O, peer if receiving else me),
        send_sem=ssem.at[t], recv_sem=rsem.at[t], device_id=peer, device_id_type=MESH)


def _start_transfers(transfers, I, O, ssem, rsem, onward):
    arrived = set()
    for t, tr in enumerate(transfers):
        after = tr.get("after")
        if (after is not None) != onward:
            continue
        if after is not None and after not in arrived:
            _transfer(transfers, after, I, O, ssem, rsem, True).wait_recv()
            arrived.add(after)
        _transfer(transfers, t, I, O, ssem, rsem, False).start()


def _finish_transfers(transfers, I, O, ssem, rsem):
    passed_on = {tr["after"] for tr in transfers if tr.get("after") is not None}
    for t in range(len(transfers)):
        if t not in passed_on:
            _transfer(transfers, t, I, O, ssem, rsem, True).wait_recv()
    for t in range(len(transfers)):
        _transfer(transfers, t, I, O, ssem, rsem, False).wait_send()


def _exchange(name, ins, outs, transfers, copies=()):
    ni, no = len(ins), len(outs)
    nt = len(transfers)

    def body(*refs):
        I, O = refs[:ni], refs[ni:ni + no]
        ssem, rsem, lsem = refs[ni + no:]
        me = _coords()
        local = [pltpu.make_async_copy(s(I, O, me), d(I, O, me), lsem.at[n]) for n, (s, d) in enumerate(copies)]
        for cp in local:
            cp.start()
        _start_transfers(transfers, I, O, ssem, rsem, False)
        _start_transfers(transfers, I, O, ssem, rsem, True)
        _finish_transfers(transfers, I, O, ssem, rsem)
        for cp in local:
            cp.wait()

    hbm = pl.BlockSpec(memory_space=pltpu.HBM)
    return pl.pallas_call(
        body, name=name, in_specs=[hbm] * ni, out_specs=[hbm] * no,
        out_shape=[jax.ShapeDtypeStruct(s, d) for s, d in outs],
        scratch_shapes=[pltpu.SemaphoreType.DMA((nt,)), pltpu.SemaphoreType.DMA((nt,)),
                        pltpu.SemaphoreType.DMA((max(len(copies), 1),))],
        compiler_params=pltpu.CompilerParams(has_side_effects=True))(*ins)


CHIP_MASKS = [(0, 1, 0), (1, 0, 0), (1, 1, 0)]
SIBLING = (0, 0, 1)


def _half(shape2d, axis, which):
    n = shape2d[axis] // 2
    cut = pl.ds(pl.multiple_of(which * n, n), n)
    return (cut, slice(None)) if axis == 0 else (slice(None), cut)


class _Riding:
    def __init__(self, transfers, ins, outs):
        self.transfers, self.ins, self.outs = transfers, list(ins), list(outs)
        hbm = pl.BlockSpec(memory_space=pltpu.HBM)
        self.in_specs, self.out_specs = [hbm] * len(self.ins), [hbm] * len(self.outs)
        self.out_shape = [jax.ShapeDtypeStruct(s, d) for s, d in self.outs]
        self.scratch = [pltpu.SemaphoreType.DMA((max(len(transfers), 1),))] * 2

    def hooks(self, I, O, ssem, rsem, first, middle, last):
        tr = self.transfers

        @pl.when(first)
        def _():
            _start_transfers(tr, I, O, ssem, rsem, False)

        if any(t.get("after") is not None for t in tr):
            @pl.when(middle)
            def _():
                _start_transfers(tr, I, O, ssem, rsem, True)

        def at_end():
            @pl.when(last)
            def _():
                _finish_transfers(tr, I, O, ssem, rsem)

        return at_end


def _select_own(shards, gathered):
    mine = lax.broadcasted_iota(jnp.int32, (NCHIP, 1, 1), 0) == _chip(_coords())
    return [jnp.where(mine, s[None], t) for s, t in zip(shards, gathered)]


def _gather_plan(shards, axes):
    def half(a, who):
        return _half(shards[a].shape, axes[a], who[2])

    over_ici, onward = [], []
    for a in range(len(shards)):
        for mask in CHIP_MASKS:
            over_ici.append(dict(
                mask=mask,
                src=lambda I, O, me, a=a: I[a].at[half(a, me)],
                dst=lambda I, O, who, a=a: O[a].at[(_chip(who),) + half(a, who)]))
            onward.append(dict(
                mask=SIBLING, after=len(over_ici) - 1,
                src=lambda I, O, me, a=a, mask=mask: O[a].at[(_chip(_flip(me, mask)),) + half(a, me)],
                dst=lambda I, O, who, a=a, mask=mask: O[a].at[(_chip(_flip(who, mask)),) + half(a, who)]))
    return over_ici + onward, [((NCHIP,) + s.shape, s.dtype) for s in shards]


def _gather_shards(shards, axes):
    transfers, outs = _gather_plan(shards, axes)
    return _select_own(shards, _exchange("gather_weights", shards, outs, transfers))


def _to_sibling(arrs, name):
    transfers = [dict(mask=SIBLING, src=lambda I, O, me, a=a: I[a], dst=lambda I, O, who, a=a: O[a])
                 for a in range(len(arrs))]
    return _exchange(name, arrs, [(t.shape, t.dtype) for t in arrs], transfers)


def _halves_to_sibling(blocks, axes, name):
    def cut(a, which):
        return (slice(None),) + _half(blocks[a].shape[1:], axes[a], which)

    transfers = [dict(mask=SIBLING, src=lambda I, O, me, a=a: I[a].at[cut(a, 1 - me[2])],
                      dst=lambda I, O, who, a=a: O[a]) for a in range(len(blocks))]
    outs = []
    for b, ax in zip(blocks, axes):
        shape = list(b.shape)
        shape[ax + 1] //= 2
        outs.append((tuple(shape), b.dtype))
    return _exchange(name, blocks, outs, transfers)


def _scatter_plan(tb):
    transfers = []
    for a in range(len(tb)):
        for n, mask in enumerate(CHIP_MASKS):
            transfers.append(dict(
                mask=mask,
                src=lambda I, O, me, a=a, mask=mask: I[a].at[_chip(_flip(me, mask))],
                dst=lambda I, O, who, a=a, n=n: O[a].at[n]))
    return transfers, [((3,) + t.shape[1:], t.dtype) for t in tb]


def _scatter_chip_sums(tb):
    transfers, outs = _scatter_plan(tb)
    return _exchange("scatter_grads", tb, outs, transfers)


def _gather_small(vec):
    def slot(who):
        return 4 * who[0] + 2 * who[1] + who[2]

    masks = [(m >> 2 & 1, m >> 1 & 1, m & 1) for m in range(1, 8)]
    transfers = [dict(mask=mask, src=lambda I, O, me: I[0], dst=lambda I, O, who: O[0].at[slot(who)])
                 for mask in masks]
    copies = [(lambda I, O, me: I[0], lambda I, O, me: O[0].at[slot(me)])]
    return _exchange("gather_small", [vec], [((8,) + vec.shape, vec.dtype)], transfers, copies)[0]


def _rope_tables(positions):
    half = ROT // 2
    inv_freq = jnp.power(jnp.float32(THETA), -jnp.arange(0, ROT, 2, dtype=F32) / ROT)
    ang = positions.astype(F32)[:, None] * inv_freq[None, :]
    cos, sin = jnp.cos(ang), jnp.sin(ang)
    one, zero, z8 = jnp.ones((S, HD - ROT), F32), jnp.zeros((S, HD - ROT), F32), jnp.zeros((S, half), F32)
    c = jnp.concatenate([cos, cos, one], axis=1)
    a = jnp.concatenate([-sin, z8, zero], axis=1)
    b = jnp.concatenate([z8, sin, zero], axis=1)
    return tuple(jnp.tile(t, (1, 2)) for t in (c, a, b))


def _tile_heads(g, w):
    return jnp.tile(g.reshape(1, HD), (1, w // HD))


def _fold_heads(dg):
    return dg.reshape(-1, HD).sum(axis=0)


def _pad_lanes(a):
    return jnp.pad(a, ((0, 0), (0, LANES - a.shape[1])))


def _local_step(x, target, positions, wt, fetch, late_weights, begin_reduce):
    rope = _rope_tables(positions)
    w1t = wt["w_in_a_t"]
    f_row = 3 * D // LANES
    wg_t = w1t[3 * D + NH:]
    in_b_block = lambda c: pl.BlockSpec((None, TN_, TN_), lambda j, i: (c, j, 0))
    b_pad = _pad_lanes(wt["b_forget"].reshape(1, NH))
    qg_a, kg_a = _tile_heads(wt["qnorm_a_g"], D), _tile_heads(wt["knorm_a_g"], D)
    qg_b, kg_b = _tile_heads(wt["qnorm_b_g"], D), _tile_heads(wt["knorm_b_g"], KVW)
    norm_a, kv_g, norm_b = wt["norm_a_g"].reshape(1, D), wt["kv_norm_g"].reshape(1, D), wt["norm_b_g"].reshape(1, D)
    sinks_t = jnp.repeat(wt["sinks"].reshape(1, NH), HD, axis=1)

    (u_a,) = _rmsnorm_fwd(x, [norm_a], "norm_a")
    qkv = _mm("proj_a", S, 3 * D, [(u_a, _a_rows(D), w1t, _b_rows(D), NT)])
    fpad = _mm("proj_f", S, LANES, [(u_a, _a_rows(D), w1t, _b_rows(D, row0=f_row, tn=LANES), NT)], tn=LANES)
    gate_a = _mm("proj_gate_a", S, D, [(u_a, _a_rows(D), wg_t, _b_rows(D), NT)])
    q_a, k_a, v_a = _a_post(qkv, qg_a, kg_a)
    ct = _forget_cumsum(fpad, b_pad)
    ct2 = ct[:NH].reshape(NH // 2, 2, S)
    o_a, lse_a, fetched = _fox_fwd(q_a, k_a, v_a, ct2, fetch)
    wt = {**wt, **late_weights(fetched)}
    w_in_b = wt["w_in_b"]
    y_a = _gate_fwd(o_a, gate_a, 0, "gate_a")
    h1 = _mm("out_a", S, D, [(y_a, _a_rows(D), wt["w_out_a"], _b_cols(D), None)], add=x)
    u_kv, u_b = _rmsnorm_fwd(h1, [kv_g, norm_b], "norm_b")
    kv = _mm("proj_kv", S, 2 * KVW, [(u_kv, _a_rows(D), wt["w_kv"], _b_cols(D), None)])
    pb = _mm("proj_b", S, 2 * D,
             [(u_b, _a_rows(D), w_in_b, pl.BlockSpec((None, D, TN_), lambda j, i: (j, 0, 0)), None)])
    q_b, kdup, vdup = _b_post(pb, kv, qg_b, kg_b, rope)
    o_b, lse_b = _swa_fwd(q_b, kdup, vdup, sinks_t)
    y_b = _gate_fwd(o_b, pb, 1, "gate_b")
    out = _mm("out_b", S, D, [(y_b, _a_rows(D), wt["w_out_b"], _b_cols(D), None)], add=h1)
    d_out, d_out_b, sq = _loss_head(out, target)

    g = {}
    g["w_out_b"] = _mm("dw_out_b", D, D, [(y_b, _a_cols(S), d_out_b, _b_cols(S), TN)])
    d_y_b = _mm("dy_b", S, D, [(d_out_b, _a_rows(D), wt["w_out_b"], _b_rows(D), NT)])
    d_o_b, d_gate_b = _gate_bwd(d_y_b, o_b, pb, 1, "gate_b_bwd")
    dq_b, dkdup, dvdup, dsk = _swa_bwd(q_b, kdup, vdup, sinks_t, o_b, lse_b, d_o_b)
    g["sinks"] = dsk[0, ::HD]
    d_qb_raw, dg = _headnorm_bwd(pb, 0, qg_b, dq_b, rope, "qnorm_b_bwd")
    g["qnorm_b_g"] = _fold_heads(dg)
    d_pb = [d_qb_raw, d_qb_raw, d_gate_b, d_gate_b]
    g["w_in_b"] = jnp.concatenate([
        _mm("dw_in_b_q", D, D, [(u_b, _a_cols(S), d_qb_raw, _b_cols(S), TN)], stacked=True),
        _mm("dw_in_b_gate", D, D, [(u_b, _a_cols(S), d_gate_b, _b_cols(S), TN)], stacked=True)], axis=0)
    d_u_b = _mm("du_b", S, D, [(d_pb[c], _a_rows(TN_, col=c % 2), w_in_b, in_b_block(c), NT) for c in range(NCHIP)])
    d_kv, dg = _kv_bwd(dkdup, dvdup, kv, kg_b, rope)
    g["knorm_b_g"] = _fold_heads(dg)
    g["w_kv"] = _mm("dw_kv", D, 2 * KVW, [(u_kv, _a_cols(S), d_kv, _b_cols(S), TN)])
    d_u_kv = _mm("du_kv", S, D, [(d_kv, _a_rows(2 * KVW), wt["w_kv"], _b_rows(2 * KVW), NT)])
    d_h1, d_h1_b, g["kv_norm_g"], g["norm_b_g"] = _rmsnorm_bwd(h1, [kv_g, norm_b], [d_u_kv, d_u_b], d_out, "norm_b_bwd")
    g["w_out_a"] = _mm("dw_out_a", D, D, [(y_a, _a_cols(S), d_h1_b, _b_cols(S), TN)])
    d_y_a = _mm("dy_a", S, D, [(d_h1_b, _a_rows(D), wt["w_out_a"], _b_rows(D), NT)])
    d_o_a, d_gate_a = _gate_bwd(d_y_a, o_a, gate_a, 0, "gate_a_bwd")
    riding, so_far = begin_reduce({n: g[n] for n in LATE})
    dq_a, dk_a, dv_a, dct, arrived = _fox_bwd(q_a, k_a, v_a, ct2, o_a, lse_a, d_o_a, riding)
    dct_pad = jnp.pad(dct.reshape(NH, S), ((0, LANES - NH), (0, 0)))
    d_f, db = _forget_bwd(dct_pad, fpad, b_pad)
    g["b_forget"] = db[0, :NH]
    d_q_raw, dg = _headnorm_bwd(qkv, 0, qg_a, dq_a, None, "qnorm_a_bwd")
    g["qnorm_a_g"] = _fold_heads(dg)
    d_k_raw, dg = _headnorm_bwd(qkv, 1, kg_a, dk_a, None, "knorm_a_bwd")
    g["knorm_a_g"] = _fold_heads(dg)
    pieces = [("q", d_q_raw), ("k", d_k_raw), ("v", dv_a), ("gate", d_gate_a)]
    dw = {n: _mm("dw_in_a_" + n, D, D, [(t, _a_cols(S), u_a, _b_cols(S), TN)]) for n, t in pieces}
    dw_f = _mm("dw_in_a_f", LANES, D, [(d_f, _a_cols(S, tm=LANES), u_a, _b_cols(S), TN)], tm=LANES)
    g["w_in_a"] = jnp.concatenate([dw["q"], dw["k"], dw["v"], dw_f[:NH], dw["gate"]], axis=0).reshape(
        NCHIP, (4 * D + NH) // NCHIP, D)
    d_u_a = _mm("du_a", S, D, [
        (d_q_raw, _a_rows(D), w1t, _b_cols(D, row=0), None), (d_k_raw, _a_rows(D), w1t, _b_cols(D, row=1), None),
        (dv_a, _a_rows(D), w1t, _b_cols(D, row=2), None), (d_gate_a, _a_rows(D), wg_t, _b_cols(D), None),
        (d_f, _a_rows(LANES), w1t, _b_cols(LANES, row=f_row), None)])
    d_x, _, g["norm_a_g"] = _rmsnorm_bwd(x, [norm_a], [d_u_a], d_h1, "norm_a_bwd")
    return sq, d_x, g, (so_far, arrived)


BIG = ["w_in_a", "w_out_a", "w_kv", "w_in_b", "w_out_b"]
LATE = BIG[1:]
SPLIT = {"w_in_a": 1, "w_out_a": 0, "w_kv": 0, "w_in_b": 0, "w_out_b": 0}
SMALL = ["norm_a_g", "b_forget", "qnorm_a_g", "knorm_a_g", "kv_norm_g", "knorm_b_g", "norm_b_g", "qnorm_b_g", "sinks"]
NAMES = ["norm_a_g", "w_in_a", "b_forget", "qnorm_a_g", "knorm_a_g", "w_out_a", "kv_norm_g", "w_kv", "knorm_b_g",
         "norm_b_g", "w_in_b", "qnorm_b_g", "sinks", "w_out_b"]


def _pack(vals):
    flat = []
    for v in vals:
        v = v.reshape(-1)
        flat.append(jnp.pad(v, (0, -v.shape[0] % LANES)))
    flat = jnp.concatenate(flat)
    flat = jnp.pad(flat, (0, -flat.shape[0] % (8 * LANES)))
    return flat.reshape(-1, LANES)


def _unpack(packed, shapes):
    flat, out, off = packed.reshape(-1), [], 0
    for s in shapes:
        n = int(np.prod(s))
        out.append(flat[off:off + n].reshape(s))
        off += n + (-n % LANES)
    return out


def kernel(x, positions, norm_a_g, w_in_a, b_forget, qnorm_a_g, knorm_a_g, w_out_a, kv_norm_g, w_kv, knorm_b_g, norm_b_g, w_in_b, qnorm_b_g, sinks, w_out_b, loss_target, m_norm_a_g, m_w_in_a, m_b_forget, m_qnorm_a_g, m_knorm_a_g, m_w_out_a, m_kv_norm_g, m_w_kv, m_knorm_b_g, m_norm_b_g, m_w_in_b, m_qnorm_b_g, m_sinks, m_w_out_b, v_norm_a_g, v_w_in_a, v_b_forget, v_qnorm_a_g, v_knorm_a_g, v_w_out_a, v_kv_norm_g, v_w_kv, v_knorm_b_g, v_norm_b_g, v_w_in_b, v_qnorm_b_g, v_sinks, v_w_out_b):
    w = dict(norm_a_g=norm_a_g, w_in_a=w_in_a, b_forget=b_forget, qnorm_a_g=qnorm_a_g, knorm_a_g=knorm_a_g,
             w_out_a=w_out_a, kv_norm_g=kv_norm_g, w_kv=w_kv, knorm_b_g=knorm_b_g, norm_b_g=norm_b_g,
             w_in_b=w_in_b, qnorm_b_g=qnorm_b_g, sinks=sinks, w_out_b=w_out_b)
    m = dict(norm_a_g=m_norm_a_g, w_in_a=m_w_in_a, b_forget=m_b_forget, qnorm_a_g=m_qnorm_a_g, knorm_a_g=m_knorm_a_g,
             w_out_a=m_w_out_a, kv_norm_g=m_kv_norm_g, w_kv=m_w_kv, knorm_b_g=m_knorm_b_g, norm_b_g=m_norm_b_g,
             w_in_b=m_w_in_b, qnorm_b_g=m_qnorm_b_g, sinks=m_sinks, w_out_b=m_w_out_b)
    v = dict(norm_a_g=v_norm_a_g, w_in_a=v_w_in_a, b_forget=v_b_forget, qnorm_a_g=v_qnorm_a_g, knorm_a_g=v_knorm_a_g,
             w_out_a=v_w_out_a, kv_norm_g=v_kv_norm_g, w_kv=v_w_kv, knorm_b_g=v_knorm_b_g, norm_b_g=v_norm_b_g,
             w_in_b=v_w_in_b, qnorm_b_g=v_qnorm_b_g, sinks=v_sinks, w_out_b=v_w_out_b)
    my_chip = 2 * lax.axis_index("x") + lax.axis_index("y")

    def shard2d(t, n):
        t = t.reshape(t.shape[-2:])
        return t.T if n == "w_in_a" else t

    w2d = {n: shard2d(w[n], n) for n in BIG}

    norm_a_rows = jnp.broadcast_to(norm_a_g.reshape(1, D // NCHIP), (16, D // NCHIP))
    w1t, norm_rows = _gather_shards([w2d["w_in_a"].astype(BF16), norm_a_rows], [SPLIT["w_in_a"], 0])
    wt = {"w_in_a_t": w1t.reshape(-1, D), "norm_a_g": norm_rows[:, 0, :].reshape(1, D)}
    for n in SMALL[1:]:
        wt[n] = w[n]
    late_shards = [w2d[n].astype(BF16) for n in LATE]
    late_axes = [SPLIT[n] for n in LATE]
    transfers, outs = _gather_plan(late_shards, late_axes)
    fetch = _Riding(transfers, late_shards, outs)

    def late_weights(fetched):
        whole = dict(zip(LATE, _select_own(late_shards, fetched)))
        return {n: t if n == "w_in_b" else t.reshape(-1, t.shape[2]) for n, t in whole.items()}

    def as_blocks(t):
        return t if t.ndim == 3 else t.reshape(NCHIP, -1, t.shape[1])

    def chip_sums(names, grads, name):
        axes = [SPLIT[n] for n in names]
        blocks = [as_blocks(grads[n]) for n in names]
        sums = [_chip_sum(blk, part, ax, "chip_sum_" + n)
                for n, ax, blk, part in zip(names, axes, blocks, _halves_to_sibling(blocks, axes, name))]
        return [s[0] for s in sums], [s[1] for s in sums]

    def begin_reduce(grads):
        f32, bf16 = chip_sums(LATE, grads, "sibling_halves_late")
        transfers, outs = _scatter_plan(bf16)
        return _Riding(transfers, bf16, outs), f32

    sq, d_x, g, (late_f32, late_arrived) = _local_step(x[0], loss_target[0], positions, wt, fetch, late_weights,
                                                      begin_reduce)

    small_shapes = [(D,), (NH,), (HD,), (HD,), (D,), (HD,), (D,), (HD,), (NH,), (D,)]
    packed = _pack([g[n] for n in SMALL] + [sq])
    total = _sum_stack(_gather_small(packed), "sum_small")
    small_g = dict(zip(SMALL, _unpack(total, small_shapes)[:-1]))
    loss = 0.5 * jnp.sum(_unpack(total, small_shapes)[-1]) / D
    small_g["norm_a_g"] = lax.dynamic_slice(small_g["norm_a_g"], (my_chip * (D // NCHIP),), (D // NCHIP,))

    axes = [SPLIT[n] for n in BIG]
    first_f32, first_bf16 = chip_sums(["w_in_a"], g, "sibling_halves")
    chip_f32 = first_f32 + list(late_f32)
    arrived = list(_scatter_chip_sums(first_bf16)) + list(late_arrived)
    halves = []
    for n, ax, t32, parts in zip(BIG, axes, chip_f32, arrived):
        own = lax.dynamic_index_in_dim(t32, my_chip, axis=0, keepdims=False)
        halves.append(_mesh_sum(own, parts, ax, "mesh_sum_" + n))
    sibling_done = _to_sibling(halves, "finished_halves")

    res = {}
    for n, ax, mine_half, their_half in zip(BIG, axes, halves, sibling_done):
        out4 = _adamw_halves(w2d[n], mine_half, their_half, shard2d(m[n], n), shard2d(v[n], n), ax, "adamw_" + n)
        res[n] = tuple((t.T if n == "w_in_a" else t).reshape(w[n].shape) for t in out4)
    sm_g = _pack([small_g[n] for n in SMALL])
    sm = [_pack([d[n] for n in SMALL]) for d in (w, m, v)]
    sm_out = _adamw(sm[0], sm_g, sm[1], sm[2], "adamw_small")
    sm_shapes = [w[n].shape for n in SMALL]
    unpacked = [_unpack(t, sm_shapes) for t in (sm_g,) + tuple(sm_out)]
    for i, n in enumerate(SMALL):
        res[n] = tuple(u[i] for u in unpacked)

    outs = [loss, d_x[None]]
    for k in range(4):
        outs += [res[n][k] for n in NAMES]
    return tuple(outs)
```

```python
import numpy as np
import jax
import jax.numpy as jnp
from jax import lax
from jax.experimental import pallas as pl
from jax.experimental.pallas import tpu as pltpu

F32, BF16 = jnp.float32, jnp.bfloat16
S, D, HD, NH, NKV = 2048, 1024, 64, 16, 4
KVW = NKV * HD
WINDOW = 128
ROT = HD // 4
THETA = 500000.0
EPS = 1e-6
SCALE = HD ** -0.5
LANES = 128
NEG = -1e30
VMEM_LIMIT = 48 * 2 ** 20
ROWS = 256
ATT = 256
SWQ = 4
NCHIP = 4
ADAM_LR, ADAM_B1, ADAM_B2, ADAM_EPS, ADAM_WD, ADAM_STEP = 0.001, 0.9, 0.999, 1e-08, 0.01, 10
NT = (((1,), (1,)), ((), ()))
TN = (((0,), (0,)), ((), ()))
MESH = pl.DeviceIdType.MESH


def _params(n):
    return pltpu.CompilerParams(dimension_semantics=("arbitrary",) * n, vmem_limit_bytes=VMEM_LIMIT)


def _dot(a, b, dims=None):
    if dims is None:
        return jnp.dot(a, b, preferred_element_type=F32)
    return lax.dot_general(a, b, dims, preferred_element_type=F32)


def _dot_split(a, b, n):
    out, rest = None, a
    for _ in range(n):
        hi = rest.astype(BF16)
        term = _dot(hi, b)
        out = term if out is None else out + term
        rest = rest - hi.astype(F32)
    return out


def _seg_mat(w):
    e = (np.arange(w)[:, None] // HD == np.arange(LANES)[None, :]).astype(np.float32)
    return jnp.asarray(e, BF16)


def _spread(r, w):
    head = lax.broadcasted_iota(jnp.int32, (LANES, w), 1) >> 6
    et = jnp.where(head == lax.broadcasted_iota(jnp.int32, (LANES, w), 0), 1.0, 0.0).astype(BF16)
    return _dot_split(r, et, 3)


def _head_rstd(x, e):
    ss = _dot_split(x * x, e, 2)
    return _spread(lax.rsqrt(ss * (1.0 / HD) + EPS), x.shape[1])


def _rope(x, c, a, b):
    w = x.shape[1]
    return x * c + pltpu.roll(x, w - ROT // 2, 1) * a + pltpu.roll(x, ROT // 2, 1) * b


def _rope_t(dy, c, a, b):
    w = dy.shape[1]
    return dy * c + pltpu.roll(dy * b, w - ROT // 2, 1) + pltpu.roll(dy * a, ROT // 2, 1)


def _sigmoid(x):
    return 1.0 / (1.0 + jnp.exp(-x))


def _row_spec(shape, ts):
    nd = len(shape)
    if shape[0] == S:
        return pl.BlockSpec((ts,) + tuple(shape[1:]), lambda i: (i,) + (0,) * (nd - 1))
    return pl.BlockSpec(tuple(shape), lambda i: (0,) * nd)


def _rows_call(body, name, ins, outs, ts=ROWS):
    return pl.pallas_call(
        body, name=name, grid=(S // ts,),
        in_specs=[_row_spec(a.shape, ts) for a in ins],
        out_specs=[_row_spec(s, ts) for s, _ in outs],
        out_shape=[jax.ShapeDtypeStruct(s, d) for s, d in outs],
        compiler_params=_params(1))(*ins)


def _col_spec(ts, w, col):
    return pl.BlockSpec((ts, w), lambda i: (i, col))


TM = TN_ = 512
TM_TOKENS = 1024


def _mm(name, m, n, terms, out_dtype=F32, add=None, tm=None, tn=TN_, stacked=False):
    nterm = len(terms)
    if tm is None:
        tm = TM_TOKENS if m == S else TM

    def body(*refs):
        acc = None
        for t in range(nterm):
            part = _dot(refs[2 * t][...], refs[2 * t + 1][...], terms[t][4])
            acc = part if acc is None else acc + part
        if add is not None:
            acc = acc + refs[2 * nterm][...]
        refs[-1][...] = acc.astype(out_dtype)

    tile = pl.BlockSpec((tm, tn), lambda j, i: (i, j))
    ins, specs = [], []
    for a, a_spec, b, b_spec, _ in terms:
        ins += [a, b]
        specs += [a_spec, b_spec]
    if add is not None:
        ins.append(add)
        specs.append(tile)
    return pl.pallas_call(
        body, name=name, grid=(n // tn, m // tm), in_specs=specs,
        out_specs=pl.BlockSpec((None, tm, tn), lambda j, i: (j, i, 0)) if stacked else tile,
        out_shape=jax.ShapeDtypeStruct((n // tn, m, tn) if stacked else (m, n), out_dtype),
        compiler_params=_params(2))(*ins)


def _a_rows(k, col=0, tm=TM_TOKENS):
    return pl.BlockSpec((tm, k), lambda j, i: (i, col))


def _a_cols(k, tm=TM):
    return pl.BlockSpec((k, tm), lambda j, i: (0, i))


def _b_cols(k, row=0, col0=0, tn=TN_):
    return pl.BlockSpec((k, tn), lambda j, i: (row, col0 + j))


def _b_rows(k, row0=0, tn=TN_):
    return pl.BlockSpec((tn, k), lambda j, i: (row0 + j, 0))


def _rmsnorm_fwd(x, gains, name):
    def body(*refs):
        xv = refs[0][...]
        r = lax.rsqrt(jnp.mean(xv * xv, axis=-1, keepdims=True) + EPS)
        xh = xv * r
        for n in range(len(gains)):
            refs[1 + len(gains) + n][...] = (xh * refs[1 + n][...]).astype(BF16)

    return _rows_call(body, name, [x] + list(gains), [((S, D), BF16)] * len(gains))


def _rmsnorm_bwd(x, gains, dus, dres, name):
    n = len(gains)

    def body(*refs):
        x_ref, g_refs, du_refs, dres_ref = refs[0], refs[1:1 + n], refs[1 + n:1 + 2 * n], refs[1 + 2 * n]
        dx_ref, dxb_ref, dg_refs = refs[2 + 2 * n], refs[3 + 2 * n], refs[4 + 2 * n:]
        xv = x_ref[...]
        r = lax.rsqrt(jnp.mean(xv * xv, axis=-1, keepdims=True) + EPS)
        xh = xv * r
        gy = None
        for m in range(n):
            du = du_refs[m][...]
            part = jnp.sum(du * xh, axis=0, keepdims=True)

            @pl.when(pl.program_id(0) == 0)
            def _(m=m, part=part):
                dg_refs[m][...] = part

            @pl.when(pl.program_id(0) != 0)
            def _(m=m, part=part):
                dg_refs[m][...] += part

            t = du * g_refs[m][...]
            gy = t if gy is None else gy + t
        dx = dres_ref[...] + r * (gy - xh * jnp.mean(gy * xh, axis=-1, keepdims=True))
        dx_ref[...] = dx
        dxb_ref[...] = dx.astype(BF16)

    outs = [((S, D), F32), ((S, D), BF16)] + [((1, D), F32)] * n
    return _rows_call(body, name, [x] + list(gains) + list(dus) + [dres], outs)


def _a_post(qkvg, qg, kg):
    e = _seg_mat(D)

    def body(q_ref, k_ref, v_ref, qg_ref, kg_ref, e_ref, qo, ko, vo):
        ev = e_ref[...]
        qv, kv = q_ref[...], k_ref[...]
        qo[...] = (qv * _head_rstd(qv, ev) * qg_ref[...] * SCALE).astype(BF16)
        ko[...] = (kv * _head_rstd(kv, ev) * kg_ref[...]).astype(BF16)
        vo[...] = v_ref[...].astype(BF16)

    whole = lambda a: pl.BlockSpec(a.shape, lambda i: (0, 0))
    return pl.pallas_call(
        body, name="a_post", grid=(S // ROWS,),
        in_specs=[_col_spec(ROWS, D, 0), _col_spec(ROWS, D, 1), _col_spec(ROWS, D, 2),
                  whole(qg), whole(kg), whole(e)],
        out_specs=[_col_spec(ROWS, D, 0)] * 3,
        out_shape=[jax.ShapeDtypeStruct((S, D), BF16)] * 3,
        compiler_params=_params(1))(qkvg, qkvg, qkvg, qg, kg, e)


def _tri(upper):
    r, c = np.arange(ROWS)[:, None], np.arange(ROWS)[None, :]
    return jnp.asarray((r <= c) if upper else (r >= c), BF16)


def _forget_cumsum(fpad, bpad):
    def body(f_ref, b_ref, u_ref, c_ref, carry):
        @pl.when(pl.program_id(0) == 0)
        def _():
            carry[...] = jnp.zeros_like(carry)

        lf = jax.nn.log_sigmoid(f_ref[...] + b_ref[...])
        blk = _dot_split(lf.T, u_ref[...], 3) + carry[:, 0:1]
        c_ref[...] = blk
        carry[...] = jnp.broadcast_to(blk[:, ROWS - 1:ROWS], carry.shape)

    return pl.pallas_call(
        body, name="forget_cumsum", grid=(S // ROWS,),
        in_specs=[pl.BlockSpec((ROWS, LANES), lambda i: (i, 0)), pl.BlockSpec((1, LANES), lambda i: (0, 0)),
                  pl.BlockSpec((ROWS, ROWS), lambda i: (0, 0))],
        out_specs=pl.BlockSpec((LANES, ROWS), lambda i: (0, i)),
        out_shape=jax.ShapeDtypeStruct((LANES, S), F32),
        scratch_shapes=[pltpu.VMEM((LANES, LANES), F32)],
        compiler_params=_params(1))(fpad, bpad, _tri(True))


def _forget_bwd(dct, fpad, bpad):
    nb = S // ROWS

    def body(dc_ref, f_ref, b_ref, l_ref, df_ref, db_ref, carry):
        @pl.when(pl.program_id(0) == 0)
        def _():
            carry[...] = jnp.zeros_like(carry)
            db_ref[...] = jnp.zeros_like(db_ref)

        blk = _dot_split(dc_ref[...], l_ref[...], 3) + carry[:, 0:1]
        carry[...] = jnp.broadcast_to(blk[:, 0:1], carry.shape)
        df = blk.T * _sigmoid(-(f_ref[...] + b_ref[...]))
        df_ref[...] = df.astype(BF16)
        db_ref[...] += jnp.sum(df, axis=0, keepdims=True)

    return pl.pallas_call(
        body, name="forget_bwd", grid=(nb,),
        in_specs=[pl.BlockSpec((LANES, ROWS), lambda i: (0, nb - 1 - i)),
                  pl.BlockSpec((ROWS, LANES), lambda i: (nb - 1 - i, 0)),
                  pl.BlockSpec((1, LANES), lambda i: (0, 0)), pl.BlockSpec((ROWS, ROWS), lambda i: (0, 0))],
        out_specs=[pl.BlockSpec((ROWS, LANES), lambda i: (nb - 1 - i, 0)), pl.BlockSpec((1, LANES), lambda i: (0, 0))],
        out_shape=[jax.ShapeDtypeStruct((S, LANES), BF16), jax.ShapeDtypeStruct((1, LANES), F32)],
        scratch_shapes=[pltpu.VMEM((LANES, LANES), F32)],
        compiler_params=_params(1))(dct, fpad, bpad, _tri(False))


def _gate_fwd(o, proj, col, name):
    def body(o_ref, g_ref, y_ref):
        g = g_ref[...]
        y_ref[...] = (o_ref[...] * (g * _sigmoid(g))).astype(BF16)

    return pl.pallas_call(
        body, name=name, grid=(S // ROWS,),
        in_specs=[_col_spec(ROWS, D, 0), _col_spec(ROWS, D, col)],
        out_specs=_col_spec(ROWS, D, 0), out_shape=jax.ShapeDtypeStruct((S, D), BF16),
        compiler_params=_params(1))(o, proj)


def _gate_bwd(dy, o, proj, col, name):
    def body(dy_ref, o_ref, g_ref, do_ref, dg_ref):
        g, dyv = g_ref[...], dy_ref[...]
        sg = _sigmoid(g)
        do_ref[...] = dyv * (g * sg)
        dg_ref[...] = (dyv * o_ref[...] * (sg * (1.0 + g * (1.0 - sg)))).astype(BF16)

    return pl.pallas_call(
        body, name=name, grid=(S // ROWS,),
        in_specs=[_col_spec(ROWS, D, 0), _col_spec(ROWS, D, 0), _col_spec(ROWS, D, col)],
        out_specs=[_col_spec(ROWS, D, 0)] * 2,
        out_shape=[jax.ShapeDtypeStruct((S, D), F32), jax.ShapeDtypeStruct((S, D), BF16)],
        compiler_params=_params(1))(dy, o, proj)


def _headnorm_bwd(x, col, gain, dy, rope, name):
    e = _seg_mat(D)
    tabs = list(rope) if rope is not None else []

    def body(*refs):
        x_ref, g_ref, dy_ref, e_ref = refs[:4]
        dx_ref, dg_ref = refs[-2:]
        xv, dyv, ev = x_ref[...], dy_ref[...], e_ref[...]
        if rope is not None:
            c, a, b = (jnp.tile(t[...], (1, D // LANES)) for t in refs[4:7])
            dyv = _rope_t(dyv, c, a, b)
        r = _head_rstd(xv, ev)
        xh = xv * r
        part = jnp.sum(dyv * xh, axis=0, keepdims=True)

        @pl.when(pl.program_id(0) == 0)
        def _():
            dg_ref[...] = part

        @pl.when(pl.program_id(0) != 0)
        def _():
            dg_ref[...] += part

        gy = dyv * g_ref[...]
        seg = _spread(_dot_split(gy * xh, ev, 2) * (1.0 / HD), D)
        dx_ref[...] = (r * (gy - xh * seg)).astype(BF16)

    whole = lambda a: pl.BlockSpec(a.shape, lambda i: (0, 0))
    return pl.pallas_call(
        body, name=name, grid=(S // ROWS,),
        in_specs=[_col_spec(ROWS, D, col), whole(gain), _col_spec(ROWS, D, 0), whole(e)]
                 + [pl.BlockSpec((ROWS, LANES), lambda i: (i, 0))] * len(tabs),
        out_specs=[_col_spec(ROWS, D, 0), whole(gain)],
        out_shape=[jax.ShapeDtypeStruct((S, D), BF16), jax.ShapeDtypeStruct((1, D), F32)],
        compiler_params=_params(1))(x, gain, dy, e, *tabs)


def _dup_mat():
    r, c = np.arange(KVW)[:, None], np.arange(2 * KVW)[None, :]
    return (r // HD == c // LANES) & (r % HD == c % HD)


def _fold_mat():
    r, c = np.arange(D)[:, None], np.arange(KVW)[None, :]
    return (r // (2 * LANES) == c // HD) & (r % HD == c % HD)


def _b_post(pb, kv, qg, kg, rope):
    e, ek = _seg_mat(D), _seg_mat(KVW)
    dup = jnp.asarray(_dup_mat(), BF16)

    def body(q_ref, k_ref, v_ref, qg_ref, kg_ref, e_ref, ek_ref, dup_ref, c_ref, a_ref, b_ref, qo, ko, vo):
        c1, a1, b1 = c_ref[...], a_ref[...], b_ref[...]
        qv = q_ref[...]
        qn = qv * _head_rstd(qv, e_ref[...]) * qg_ref[...]
        t = lambda z, n: jnp.tile(z, (1, n))
        qo[...] = (_rope(qn, t(c1, D // LANES), t(a1, D // LANES), t(b1, D // LANES)) * SCALE).astype(BF16)
        kvv = k_ref[...]
        kn = kvv * _head_rstd(kvv, ek_ref[...]) * kg_ref[...]
        kr = _rope(kn, t(c1, KVW // LANES), t(a1, KVW // LANES), t(b1, KVW // LANES)).astype(BF16)
        ko[...] = _dot(kr, dup_ref[...]).astype(BF16)
        vo[...] = _dot(v_ref[...].astype(BF16), dup_ref[...]).astype(BF16)

    whole = lambda a: pl.BlockSpec(a.shape, lambda i: (0, 0))
    tab = pl.BlockSpec((ROWS, LANES), lambda i: (i, 0))
    return pl.pallas_call(
        body, name="b_post", grid=(S // ROWS,),
        in_specs=[_col_spec(ROWS, D, 0), _col_spec(ROWS, KVW, 0), _col_spec(ROWS, KVW, 1),
                  whole(qg), whole(kg), whole(e), whole(ek), whole(dup), tab, tab, tab],
        out_specs=[_col_spec(ROWS, D, 0), _col_spec(ROWS, 2 * KVW, 0), _col_spec(ROWS, 2 * KVW, 0)],
        out_shape=[jax.ShapeDtypeStruct((S, D), BF16), jax.ShapeDtypeStruct((S, 2 * KVW), BF16),
                   jax.ShapeDtypeStruct((S, 2 * KVW), BF16)],
        compiler_params=_params(1))(pb, kv, kv, qg, kg, e, ek, dup, *rope)


def _kv_bwd(dkdup, dvdup, kv, kg, rope):
    ek = _seg_mat(KVW)
    fold = jnp.asarray(_fold_mat(), BF16)

    def body(dk_ref, dv_ref, k_ref, kg_ref, ek_ref, fold_ref, c_ref, a_ref, b_ref, dkv_ref, dg_ref):
        ev, fv = ek_ref[...], fold_ref[...]
        t = lambda z: jnp.tile(z[...], (1, KVW // LANES))
        dk = _rope_t(_dot_split(dk_ref[...], fv, 3), t(c_ref), t(a_ref), t(b_ref))
        dv = _dot_split(dv_ref[...], fv, 3)
        xv = k_ref[...]
        r = _head_rstd(xv, ev)
        xh = xv * r
        part = jnp.sum(dk * xh, axis=0, keepdims=True)

        @pl.when(pl.program_id(0) == 0)
        def _():
            dg_ref[...] = part

        @pl.when(pl.program_id(0) != 0)
        def _():
            dg_ref[...] += part

        gy = dk * kg_ref[...]
        seg = _spread(_dot_split(gy * xh, ev, 2) * (1.0 / HD), KVW)
        dkv_ref[:, 0:KVW] = (r * (gy - xh * seg)).astype(BF16)
        dkv_ref[:, KVW:2 * KVW] = dv.astype(BF16)

    whole = lambda a: pl.BlockSpec(a.shape, lambda i: (0, 0))
    tab = pl.BlockSpec((ROWS, LANES), lambda i: (i, 0))
    return pl.pallas_call(
        body, name="kv_bwd", grid=(S // ROWS,),
        in_specs=[_col_spec(ROWS, D, 0), _col_spec(ROWS, D, 0), _col_spec(ROWS, KVW, 0),
                  whole(kg), whole(ek), whole(fold), tab, tab, tab],
        out_specs=[_col_spec(ROWS, 2 * KVW, 0), whole(kg)],
        out_shape=[jax.ShapeDtypeStruct((S, 2 * KVW), BF16), jax.ShapeDtypeStruct((1, KVW), F32)],
        compiler_params=_params(1))(dkdup, dvdup, kv, kg, ek, fold, *rope)


def _loss_head(out, target):
    def body(o_ref, t_ref, d_ref, db_ref, l_ref):
        diff = o_ref[...] - t_ref[...]
        d = diff * (1.0 / D)
        d_ref[...] = d
        db_ref[...] = d.astype(BF16)

        @pl.when(pl.program_id(0) == 0)
        def _():
            l_ref[...] = jnp.zeros_like(l_ref)

        l_ref[...] += jnp.sum(diff * diff, axis=0, keepdims=True)

    return _rows_call(body, "loss_head", [out, target], [((S, D), F32), ((S, D), BF16), ((1, D), F32)])


def _lane():
    return lax.broadcasted_iota(jnp.int32, (1, LANES), 1)


def _head_mask(hh):
    return (_lane() < HD) if hh == 0 else (_lane() >= HD)


def _fox_fwd(q, k, v, ct, riding):
    nq, npair = S // ATT, NH // 2
    ni, no = len(riding.ins), len(riding.outs)

    def body(q_ref, k_ref, v_ref, c_ref, *rest):
        o_ref, lse_ref = rest[ni:ni + 2]
        pair, i = pl.program_id(0), pl.program_id(1)
        at_end = riding.hooks(rest[:ni], rest[ni + 2:ni + 2 + no], *rest[ni + 2 + no:],
                              first=(pair == 0) & (i == 0), middle=(pair == npair // 2) & (i == 0),
                              last=(pair == npair - 1) & (i == nq - 1))
        q2 = q_ref[...]
        qms = [jnp.where(_head_mask(hh), q2, jnp.zeros_like(q2)) for hh in (0, 1)]

        def probs(off, width, m, hh, diag):
            s = _dot(qms[hh], k_ref[pl.ds(off, width), :], NT) - c_ref[hh:hh + 1, pl.ds(off, width)]
            if diag:
                row = i * ATT + lax.broadcasted_iota(jnp.int32, (ATT, width), 0)
                col = off + lax.broadcasted_iota(jnp.int32, (ATT, width), 1)
                s = jnp.where(col <= row, s, NEG)
            m_new = jnp.maximum(m, jnp.max(s, axis=1, keepdims=True))
            p = jnp.exp(s - m_new)
            p_hi = p.astype(BF16)
            return m_new, jnp.exp(m - m_new), p_hi, (p - p_hi.astype(F32)).astype(BF16)

        def weighted(off, width, p_hi, p_lo, hh):
            vj = v_ref[pl.ds(off, width), :]
            v1 = jnp.where(_head_mask(hh), vj, jnp.ones_like(vj))
            return _dot(p_hi, v1) + _dot(p_lo, v1)

        def step(off, width, carry, diag):
            off = pl.multiple_of(off, ATT)
            out = []
            for hh in (0, 1):
                m, acc = carry[hh]
                m, alpha, p_hi, p_lo = probs(off, width, m, hh, diag)
                out.append((m, alpha * acc + weighted(off, width, p_hi, p_lo, hh)))
            return tuple(out)

        one = (jnp.full((ATT, 1), NEG, F32), jnp.zeros((ATT, LANES), F32))
        carry = lax.fori_loop(0, i // 2, lambda j, cr: step(j * (2 * ATT), 2 * ATT, cr, False), (one, one))
        carry = lax.cond(i % 2 == 1, lambda cr: step((i - 1) * ATT, 2 * ATT, cr, True),
                         lambda cr: step(i * ATT, ATT, cr, True), carry)
        res = []
        for hh in (0, 1):
            m, acc = carry[hh]
            l = jnp.max(jnp.where(_head_mask(1 - hh), acc, 0.0), axis=1, keepdims=True)
            res.append((acc / l, m + jnp.log(l)))
        first = _head_mask(0)
        o_ref[...] = jnp.where(first, res[0][0], res[1][0])
        lse_ref[...] = jnp.where(first, res[0][1], res[1][1])
        at_end()

    blk = pl.BlockSpec((ATT, LANES), lambda p, i: (i, p))
    full = pl.BlockSpec((S, LANES), lambda p, i: (0, p))
    res = pl.pallas_call(
        body, name="fox_fwd", grid=(npair, nq),
        in_specs=[blk, full, full, pl.BlockSpec((None, 2, S), lambda p, i: (p, 0, 0))] + riding.in_specs,
        out_specs=[blk, blk] + riding.out_specs,
        out_shape=[jax.ShapeDtypeStruct((S, D), F32)] * 2 + riding.out_shape,
        scratch_shapes=riding.scratch,
        compiler_params=_params(2))(q, k, v, ct, *riding.ins)
    return res[0], res[1], res[2:]


def _fox_bwd(q, k, v, ct, o, lse, do, riding):
    nq, npair = S // ATT, NH // 2
    ni, no = len(riding.ins), len(riding.outs)

    def body(q_ref, k_ref, v_ref, c_ref, o_ref, lse_ref, do_ref, *rest):
        dq_ref, dk_ref, dvb_ref, dc_ref = rest[ni:ni + 4]
        dv_ref = rest[ni + 4 + no]
        pair, i = pl.program_id(0), pl.program_id(1)
        at_end = riding.hooks(rest[:ni], rest[ni + 4:ni + 4 + no], *rest[ni + 5 + no:],
                              first=(pair == 0) & (i == 0), middle=(pair == npair // 2) & (i == 0),
                              last=(pair == npair - 1) & (i == nq - 1))

        @pl.when(i == 0)
        def _():
            dk_ref[...] = jnp.zeros_like(dk_ref)
            dv_ref[...] = jnp.zeros_like(dv_ref)
            dc_ref[...] = jnp.zeros_like(dc_ref)

        q2, do2, lse2 = q_ref[...], do_ref[...], lse_ref[...]
        do2b = do2.astype(BF16)
        prod = do2b.astype(F32) * o_ref[...]
        heads = []
        for hh in (0, 1):
            hm = _head_mask(hh)
            heads.append((jnp.where(hm, q2, jnp.zeros_like(q2)), jnp.where(hm, do2b, jnp.zeros_like(do2b)),
                          jnp.sum(jnp.where(hm, prod, 0.0), axis=1, keepdims=True),
                          jnp.max(jnp.where(hm, lse2, NEG), axis=1, keepdims=True)))

        def step(off, width, dqs, diag):
            off = pl.multiple_of(off, ATT)
            kj, vj = k_ref[pl.ds(off, width), :], v_ref[pl.ds(off, width), :]
            dk, dv, out = None, None, []
            for hh in (0, 1):
                qm, dom, delta, lse_h = heads[hh]
                s = _dot(qm, kj, NT) - c_ref[hh:hh + 1, pl.ds(off, width)]
                p = jnp.exp(s - lse_h)
                if diag:
                    row = i * ATT + lax.broadcasted_iota(jnp.int32, (ATT, width), 0)
                    col = off + lax.broadcasted_iota(jnp.int32, (ATT, width), 1)
                    p = jnp.where(col <= row, p, 0.0)
                ds = p * (_dot(dom, vj, NT) - delta)
                dc_ref[hh:hh + 1, pl.ds(off, width)] += -jnp.sum(ds, axis=0, keepdims=True)
                dsb = ds.astype(BF16)
                dk_h, dv_h = _dot(dsb, qm, TN), _dot(p.astype(BF16), dom, TN)
                dk, dv = (dk_h, dv_h) if dk is None else (dk + dk_h, dv + dv_h)
                out.append(dqs[hh] + _dot(dsb, kj))
            dk_ref[pl.ds(off, width), :] += dk
            dv_ref[pl.ds(off, width), :] += dv
            return tuple(out)

        zero = jnp.zeros((ATT, LANES), F32)
        dqs = lax.fori_loop(0, i // 2, lambda j, acc: step(j * (2 * ATT), 2 * ATT, acc, False), (zero, zero))
        dqs = lax.cond(i % 2 == 1, lambda acc: step((i - 1) * ATT, 2 * ATT, acc, True),
                       lambda acc: step(i * ATT, ATT, acc, True), dqs)
        dq_ref[...] = jnp.where(_head_mask(0), dqs[0], dqs[1]) * SCALE

        @pl.when(i == nq - 1)
        def _():
            dvb_ref[...] = dv_ref[...].astype(BF16)

        at_end()

    blk = pl.BlockSpec((ATT, LANES), lambda p, i: (i, p))
    full = pl.BlockSpec((S, LANES), lambda p, i: (0, p))
    cspec = pl.BlockSpec((None, 2, S), lambda p, i: (p, 0, 0))
    res = pl.pallas_call(
        body, name="fox_bwd", grid=(npair, nq),
        in_specs=[blk, full, full, cspec, blk, blk, blk] + riding.in_specs,
        out_specs=[blk, full, full, cspec] + riding.out_specs,
        out_shape=[jax.ShapeDtypeStruct((S, D), F32)] * 2 + [jax.ShapeDtypeStruct((S, D), BF16),
                                                              jax.ShapeDtypeStruct((npair, 2, S), F32)]
                  + riding.out_shape,
        scratch_shapes=[pltpu.VMEM((S, LANES), F32)] + riding.scratch,
        compiler_params=_params(2))(q, k, v, ct, o, lse, do, *riding.ins)
    return res[0], res[1], res[2], res[3], res[4:]


def _both_heads(x):
    return jnp.concatenate([jnp.where(_head_mask(hh), x, jnp.zeros_like(x)) for hh in (0, 1)], axis=0)


def _per_head(col0, col1):
    return jnp.concatenate([jnp.broadcast_to(col0, (WINDOW, 1)), jnp.broadcast_to(col1, (WINDOW, 1))], axis=0)


def _unstack(x2):
    return jnp.where(_head_mask(0), x2[:WINDOW], x2[WINDOW:])


def _swa_valid(i, start):
    r = lax.broadcasted_iota(jnp.int32, (2 * WINDOW, 2 * WINDOW), 0)
    qabs = i * WINDOW + jnp.where(r >= WINDOW, r - WINDOW, r)
    kabs = start + lax.broadcasted_iota(jnp.int32, (2 * WINDOW, 2 * WINDOW), 1)
    return (kabs <= qabs) & (qabs - kabs < WINDOW)


def _swa_fwd(q, kdup, vdup, sinks_t):
    def body(q_ref, k_ref, v_ref, sk_ref, o_ref, lse_ref):
        skv = sk_ref[...]
        first = _head_mask(0)
        for sb in range(SWQ):
            i = pl.program_id(1) * SWQ + sb
            rows = slice(sb * WINDOW, (sb + 1) * WINDOW)
            start = pl.multiple_of(jnp.maximum(i - 1, 0) * WINDOW, WINDOW)
            kk, vv = k_ref[pl.ds(start, 2 * WINDOW), :], v_ref[pl.ds(start, 2 * WINDOW), :]
            q2 = q_ref[rows, :]
            valid = _swa_valid(i, start)[:WINDOW]
            res = []
            for hh in (0, 1):
                hm = _head_mask(hh)
                sink = jnp.max(jnp.where(hm, skv, NEG), axis=1, keepdims=True)
                s = jnp.where(valid, _dot(jnp.where(hm, q2, jnp.zeros_like(q2)), kk, NT), NEG)
                m = jnp.maximum(jnp.max(s, axis=1, keepdims=True), sink)
                p = jnp.exp(s - m)
                l = jnp.sum(p, axis=1, keepdims=True) + jnp.exp(sink - m)
                res.append((_dot(p.astype(BF16), vv) / l, m + jnp.log(l)))
            o_ref[rows, :] = jnp.where(first, res[0][0], res[1][0])
            lse_ref[rows, :] = jnp.where(first, res[0][1], res[1][1])

    blk = pl.BlockSpec((SWQ * WINDOW, LANES), lambda p, i: (i, p))
    full = pl.BlockSpec((S, LANES), lambda p, i: (0, p // 2))
    return pl.pallas_call(
        body, name="swa_fwd", grid=(NH // 2, S // (SWQ * WINDOW)),
        in_specs=[blk, full, full, pl.BlockSpec((1, LANES), lambda p, i: (0, p))],
        out_specs=[blk, blk],
        out_shape=[jax.ShapeDtypeStruct((S, D), F32)] * 2,
        compiler_params=_params(2))(q, kdup, vdup, sinks_t)


def _swa_bwd(q, kdup, vdup, sinks_t, o, lse, do):
    def body(q_ref, k_ref, v_ref, sk_ref, o_ref, lse_ref, do_ref, dq_ref, dk_ref, dv_ref, dsk_ref):
        @pl.when(pl.program_id(1) == 0)
        def _():
            dk_ref[...] = jnp.zeros_like(dk_ref)
            dv_ref[...] = jnp.zeros_like(dv_ref)
            dsk_ref[...] = jnp.zeros_like(dsk_ref)

        skv = sk_ref[...]
        first = _head_mask(0)
        sink = _per_head(*[jnp.max(jnp.where(_head_mask(hh), skv, NEG), axis=1, keepdims=True) for hh in (0, 1)])
        for sb in range(SWQ):
            i = pl.program_id(1) * SWQ + sb
            rows = slice(sb * WINDOW, (sb + 1) * WINDOW)
            start = pl.multiple_of(jnp.maximum(i - 1, 0) * WINDOW, WINDOW)
            kk, vv = k_ref[pl.ds(start, 2 * WINDOW), :], v_ref[pl.ds(start, 2 * WINDOW), :]
            do2b = do_ref[rows, :].astype(BF16)
            prod, lse2 = do2b.astype(F32) * o_ref[rows, :], lse_ref[rows, :]
            qs, dos = _both_heads(q_ref[rows, :]), _both_heads(do2b)
            delta = jnp.concatenate([jnp.sum(jnp.where(_head_mask(hh), prod, 0.0), axis=1, keepdims=True)
                                     for hh in (0, 1)], axis=0)
            lse_h = jnp.concatenate([jnp.max(jnp.where(_head_mask(hh), lse2, NEG), axis=1, keepdims=True)
                                     for hh in (0, 1)], axis=0)
            p = jnp.where(_swa_valid(i, start), jnp.exp(_dot(qs, kk, NT) - lse_h), 0.0)
            dsb = (p * (_dot(dos, vv, NT) - delta)).astype(BF16)
            dk_ref[pl.ds(start, 2 * WINDOW), :] += _dot(dsb, qs, TN)
            dv_ref[pl.ds(start, 2 * WINDOW), :] += _dot(p.astype(BF16), dos, TN)
            dq_ref[rows, :] = _unstack(_dot(dsb, kk)) * SCALE
            t = jnp.exp(sink - lse_h) * delta
            dsk_ref[...] += -jnp.where(first, jnp.sum(t[:WINDOW], axis=0, keepdims=True),
                                       jnp.sum(t[WINDOW:], axis=0, keepdims=True))

    blk = pl.BlockSpec((SWQ * WINDOW, LANES), lambda p, i: (i, p))
    full = pl.BlockSpec((S, LANES), lambda p, i: (0, p // 2))
    acc = pl.BlockSpec((S, LANES), lambda p, i: (0, p))
    sk = pl.BlockSpec((1, LANES), lambda p, i: (0, p))
    return pl.pallas_call(
        body, name="swa_bwd", grid=(NH // 2, S // (SWQ * WINDOW)),
        in_specs=[blk, full, full, sk, blk, blk, blk],
        out_specs=[blk, acc, acc, sk],
        out_shape=[jax.ShapeDtypeStruct((S, D), F32)] * 3 + [jax.ShapeDtypeStruct((1, D), F32)],
        compiler_params=_params(2))(q, kdup, vdup, sinks_t, o, lse, do)


def _adamw_math(w, g, m, v):
    m = ADAM_B1 * m + (1.0 - ADAM_B1) * g
    v = ADAM_B2 * v + (1.0 - ADAM_B2) * jnp.square(g)
    m_hat = m / (1.0 - ADAM_B1 ** ADAM_STEP)
    v_hat = v / (1.0 - ADAM_B2 ** ADAM_STEP)
    delta = -ADAM_LR * (m_hat / (jnp.sqrt(v_hat) + ADAM_EPS) + ADAM_WD * w)
    return delta, m, v


def _adamw(w, g, m, v, name):
    r, c = w.shape
    tr = min(r, 128)

    def body(w_ref, g_ref, m_ref, v_ref, d_ref, mo_ref, vo_ref):
        d_ref[...], mo_ref[...], vo_ref[...] = _adamw_math(w_ref[...], g_ref[...], m_ref[...], v_ref[...])

    spec = pl.BlockSpec((tr, c), lambda i: (i, 0))
    return pl.pallas_call(
        body, name=name, grid=(r // tr,), in_specs=[spec] * 4, out_specs=[spec] * 3,
        out_shape=[jax.ShapeDtypeStruct((r, c), F32)] * 3, compiler_params=_params(1))(w, g, m, v)


SUM_TILE = 128


FLAT_BLOCK = 257 * 1024


def _tiles(shape, axis, lead=0):
    if len(shape) == 1:
        count = shape[0] // FLAT_BLOCK
        return (FLAT_BLOCK,), count, lambda pos, *lead_idx: (sum(k * count for k in lead_idx) + pos,)
    r, c = shape
    blk = (SUM_TILE, c) if axis == 0 else (r, SUM_TILE)
    count = shape[axis] // SUM_TILE

    def index(pos, *lead_idx):
        return tuple(lead_idx) + ((pos, 0) if axis == 0 else (0, pos))

    return (None,) * lead + blk, count, index


def _adamw_halves(w, g_mine, g_theirs, m, v, axis, name):
    blk, count, index = _tiles(w.shape, axis)
    per_half = count // 2

    def body(w_ref, a_ref, b_ref, m_ref, v_ref, g_ref, d_ref, mo_ref, vo_ref):
        is_mine = pl.program_id(0) // per_half == lax.axis_index("c")
        g = jnp.where(is_mine, a_ref[...], b_ref[...])
        g_ref[...] = g
        d_ref[...], mo_ref[...], vo_ref[...] = _adamw_math(w_ref[...], g, m_ref[...], v_ref[...])

    spec = pl.BlockSpec(blk, lambda i: index(i))
    half = pl.BlockSpec(blk, lambda i: index(i % per_half))
    return pl.pallas_call(
        body, name=name, grid=(count,), in_specs=[spec, half, half, spec, spec], out_specs=[spec] * 4,
        out_shape=[jax.ShapeDtypeStruct(w.shape, F32)] * 4, compiler_params=_params(1))(w, g_mine, g_theirs, m, v)


def _chip_sum(blocks, from_sibling, axis, name):
    flat = blocks.ndim == 1
    blk, count, index = _tiles((from_sibling.shape[0] // NCHIP,) if flat else from_sibling.shape[1:], axis, lead=1)

    def body(lo_ref, hi_ref, p_ref, o32, o16):
        mine = jnp.where(lax.axis_index("c") == 0, lo_ref[...], hi_ref[...])
        acc = mine + p_ref[...]
        o32[...] = acc
        o16[...] = acc.astype(BF16)

    half = pl.BlockSpec(blk, lambda k, i: index(i, k))
    if flat:
        lo = pl.BlockSpec(blk, lambda k, i: (2 * count * k + i,))
        hi = pl.BlockSpec(blk, lambda k, i: (2 * count * k + count + i,))
    else:
        lo, hi = half, pl.BlockSpec(blk, lambda k, i: index(i + count, k))
    return pl.pallas_call(
        body, name=name, grid=(NCHIP, count), in_specs=[lo, hi, half], out_specs=[half, half],
        out_shape=[jax.ShapeDtypeStruct(from_sibling.shape, F32), jax.ShapeDtypeStruct(from_sibling.shape, BF16)],
        compiler_params=_params(2))(blocks, blocks, from_sibling)


def _mesh_sum(own, parts, axis, name):
    blk, count, index = _tiles(own.shape, axis)
    n = NCHIP - 1

    def body(a_ref, *refs):
        acc = a_ref[...]
        for k in range(n):
            acc = acc + refs[k][...].astype(F32)
        refs[n][...] = acc

    spec = pl.BlockSpec(blk, lambda i: index(i))
    if own.ndim == 1:
        part = [pl.BlockSpec(blk, lambda i, k=k: (k * count + i,)) for k in range(n)]
    else:
        part = [pl.BlockSpec((None,) + blk, lambda i, k=k: (k,) + index(i)) for k in range(n)]
    return pl.pallas_call(
        body, name=name, grid=(count,), in_specs=[spec] + part,
        out_specs=spec, out_shape=jax.ShapeDtypeStruct(own.shape, F32),
        compiler_params=_params(1))(own, *([parts] * n))


def _sum_stack(parts, name):
    n = parts.shape[0]

    def body(p_ref, o_ref):
        acc = p_ref[0]
        for k in range(1, n):
            acc = acc + p_ref[k]
        o_ref[...] = acc

    return pl.pallas_call(body, name=name, out_shape=jax.ShapeDtypeStruct(parts.shape[1:], F32))(parts)


def _coords():
    return lax.axis_index("x"), lax.axis_index("y"), lax.axis_index("c")


def _chip(who):
    return 2 * who[0] + who[1]


def _flip(who, mask):
    return tuple((1 - v) if b else v for v, b in zip(who, mask))


def _transfer(transfers, t, I, O, ssem, rsem, receiving):
    tr, me = transfers[t], _coords()
    peer = _flip(me, tr["mask"])
    return pltpu.make_async_remote_copy(
        src_ref=tr["src"](I, O, me), dst_ref=tr["dst"](I, O, peer if receiving else me),
        send_sem=ssem.at[t], recv_sem=rsem.at[t], device_id=peer, device_id_type=MESH)


def _start_transfers(transfers, I, O, ssem, rsem, onward):
    arrived = set()
    for t, tr in enumerate(transfers):
        after = tr.get("after")
        if (after is not None) != onward:
            continue
        if after is not None and after not in arrived:
            _transfer(transfers, after, I, O, ssem, rsem, True).wait_recv()
            arrived.add(after)
        _transfer(transfers, t, I, O, ssem, rsem, False).start()


def _finish_transfers(transfers, I, O, ssem, rsem):
    passed_on = {tr["after"] for tr in transfers if tr.get("after") is not None}
    for t in range(len(transfers)):
        if t not in passed_on:
            _transfer(transfers, t, I, O, ssem, rsem, True).wait_recv()
    for t in range(len(transfers)):
        _transfer(transfers, t, I, O, ssem, rsem, False).wait_send()


def _exchange(name, ins, outs, transfers, copies=()):
    ni, no = len(ins), len(outs)
    nt = len(transfers)

    def body(*refs):
        I, O = refs[:ni], refs[ni:ni + no]
        ssem, rsem, lsem = refs[ni + no:]
        me = _coords()
        local = [pltpu.make_async_copy(s(I, O, me), d(I, O, me), lsem.at[n]) for n, (s, d) in enumerate(copies)]
        for cp in local:
            cp.start()
        _start_transfers(transfers, I, O, ssem, rsem, False)
        _start_transfers(transfers, I, O, ssem, rsem, True)
        _finish_transfers(transfers, I, O, ssem, rsem)
        for cp in local:
            cp.wait()

    hbm = pl.BlockSpec(memory_space=pltpu.HBM)
    return pl.pallas_call(
        body, name=name, in_specs=[hbm] * ni, out_specs=[hbm] * no,
        out_shape=[jax.ShapeDtypeStruct(s, d) for s, d in outs],
        scratch_shapes=[pltpu.SemaphoreType.DMA((nt,)), pltpu.SemaphoreType.DMA((nt,)),
                        pltpu.SemaphoreType.DMA((max(len(copies), 1),))],
        compiler_params=pltpu.CompilerParams(has_side_effects=True))(*ins)


CHIP_MASKS = [(0, 1, 0), (1, 0, 0), (1, 1, 0)]
SIBLING = (0, 0, 1)


def _half(shape2d, axis, which):
    n = shape2d[axis] // 2
    cut = pl.ds(pl.multiple_of(which * n, n), n)
    return (cut, slice(None)) if axis == 0 else (slice(None), cut)


class _Riding:
    def __init__(self, transfers, ins, outs):
        self.transfers, self.ins, self.outs = transfers, list(ins), list(outs)
        hbm = pl.BlockSpec(memory_space=pltpu.HBM)
        self.in_specs, self.out_specs = [hbm] * len(self.ins), [hbm] * len(self.outs)
        self.out_shape = [jax.ShapeDtypeStruct(s, d) for s, d in self.outs]
        self.scratch = [pltpu.SemaphoreType.DMA((max(len(transfers), 1),))] * 2

    def hooks(self, I, O, ssem, rsem, first, middle, last):
        tr = self.transfers

        @pl.when(first)
        def _():
            _start_transfers(tr, I, O, ssem, rsem, False)

        if any(t.get("after") is not None for t in tr):
            @pl.when(middle)
            def _():
                _start_transfers(tr, I, O, ssem, rsem, True)

        def at_end():
            @pl.when(last)
            def _():
                _finish_transfers(tr, I, O, ssem, rsem)

        return at_end


def _stretch(n, pos):
    return (pl.ds(pos * n if isinstance(pos, int) else pl.multiple_of(pos * n, n), n),)


def _select_own(shards, gathered):
    chip = _chip(_coords())
    out = []
    for s, t in zip(shards, gathered):
        if s.ndim == 1:
            at = lax.broadcasted_iota(jnp.int32, t.shape, 0) - chip * s.shape[0]
            out.append(jnp.where((at >= 0) & (at < s.shape[0]), jnp.tile(s, NCHIP), t))
        else:
            out.append(jnp.where(lax.broadcasted_iota(jnp.int32, (NCHIP, 1, 1), 0) == chip, s[None], t))
    return out


def _gather_plan(shards, axes):
    def half(a, who):
        if shards[a].ndim == 1:
            return _stretch(shards[a].shape[0] // 2, who[2])
        return _half(shards[a].shape, axes[a], who[2])

    def landed(a, chip, who):
        if shards[a].ndim == 1:
            return _stretch(shards[a].shape[0] // 2, 2 * chip + who[2])
        return (chip,) + half(a, who)

    over_ici, onward = [], []
    for a in range(len(shards)):
        for mask in CHIP_MASKS:
            over_ici.append(dict(
                mask=mask,
                src=lambda I, O, me, a=a: I[a].at[half(a, me)],
                dst=lambda I, O, who, a=a: O[a].at[landed(a, _chip(who), who)]))
            onward.append(dict(
                mask=SIBLING, after=len(over_ici) - 1,
                src=lambda I, O, me, a=a, mask=mask: O[a].at[landed(a, _chip(_flip(me, mask)), me)],
                dst=lambda I, O, who, a=a, mask=mask: O[a].at[landed(a, _chip(_flip(who, mask)), who)]))
    outs = [((NCHIP * s.shape[0],) if s.ndim == 1 else (NCHIP,) + s.shape, s.dtype) for s in shards]
    return over_ici + onward, outs


def _gather_shards(shards, axes):
    transfers, outs = _gather_plan(shards, axes)
    return _select_own(shards, _exchange("gather_weights", shards, outs, transfers))


def _to_sibling(arrs, name):
    transfers = [dict(mask=SIBLING, src=lambda I, O, me, a=a: I[a], dst=lambda I, O, who, a=a: O[a])
                 for a in range(len(arrs))]
    return _exchange(name, arrs, [(t.shape, t.dtype) for t in arrs], transfers)


def _halves_to_sibling(blocks, axes, name):
    def cut(a, which):
        return (slice(None),) + _half(blocks[a].shape[1:], axes[a], which)

    transfers, outs = [], []
    for a, (b, ax) in enumerate(zip(blocks, axes)):
        if b.ndim == 1:
            h = b.shape[0] // NCHIP // 2
            for k in range(NCHIP):
                transfers.append(dict(mask=SIBLING,
                                      src=lambda I, O, me, a=a, k=k, h=h: I[a].at[_stretch(h, 2 * k + 1 - me[2])],
                                      dst=lambda I, O, who, a=a, k=k, h=h: O[a].at[_stretch(h, k)]))
            outs.append(((NCHIP * h,), b.dtype))
        else:
            transfers.append(dict(mask=SIBLING, src=lambda I, O, me, a=a: I[a].at[cut(a, 1 - me[2])],
                                  dst=lambda I, O, who, a=a: O[a]))
            shape = list(b.shape)
            shape[ax + 1] //= 2
            outs.append((tuple(shape), b.dtype))
    return _exchange(name, blocks, outs, transfers)


def _scatter_plan(tb):
    def slot(a, k):
        return (k,) if tb[a].ndim == 3 else _stretch(tb[a].shape[0] // NCHIP, k)

    transfers = []
    for a in range(len(tb)):
        for n, mask in enumerate(CHIP_MASKS):
            transfers.append(dict(
                mask=mask,
                src=lambda I, O, me, a=a, mask=mask: I[a].at[slot(a, _chip(_flip(me, mask)))],
                dst=lambda I, O, who, a=a, n=n: O[a].at[slot(a, n)]))
    outs = [((3,) + t.shape[1:] if t.ndim == 3 else (3 * (t.shape[0] // NCHIP),), t.dtype) for t in tb]
    return transfers, outs


def _scatter_chip_sums(tb):
    transfers, outs = _scatter_plan(tb)
    return _exchange("scatter_grads", tb, outs, transfers)


def _gather_small(vec):
    def slot(who):
        return 4 * who[0] + 2 * who[1] + who[2]

    masks = [(m >> 2 & 1, m >> 1 & 1, m & 1) for m in range(1, 8)]
    transfers = [dict(mask=mask, src=lambda I, O, me: I[0], dst=lambda I, O, who: O[0].at[slot(who)])
                 for mask in masks]
    copies = [(lambda I, O, me: I[0], lambda I, O, me: O[0].at[slot(me)])]
    return _exchange("gather_small", [vec], [((8,) + vec.shape, vec.dtype)], transfers, copies)[0]


def _rope_tables(positions):
    half = ROT // 2
    inv_freq = jnp.power(jnp.float32(THETA), -jnp.arange(0, ROT, 2, dtype=F32) / ROT)
    ang = positions.astype(F32)[:, None] * inv_freq[None, :]
    cos, sin = jnp.cos(ang), jnp.sin(ang)
    one, zero, z8 = jnp.ones((S, HD - ROT), F32), jnp.zeros((S, HD - ROT), F32), jnp.zeros((S, half), F32)
    c = jnp.concatenate([cos, cos, one], axis=1)
    a = jnp.concatenate([-sin, z8, zero], axis=1)
    b = jnp.concatenate([z8, sin, zero], axis=1)
    return tuple(jnp.tile(t, (1, 2)) for t in (c, a, b))


def _tile_heads(g, w):
    return jnp.tile(g.reshape(1, HD), (1, w // HD))


def _fold_heads(dg):
    return dg.reshape(-1, HD).sum(axis=0)


def _pad_lanes(a):
    return jnp.pad(a, ((0, 0), (0, LANES - a.shape[1])))


def _local_step(x, target, positions, wt, fetch, late_weights, begin_reduce):
    rope = _rope_tables(positions)
    w1t = wt["w_in_a_t"]
    f_row = 3 * D // LANES
    wg_t = w1t[3 * D + NH:]
    in_b_block = lambda c: pl.BlockSpec((None, TN_, TN_), lambda j, i: (c, j, 0))
    b_pad = _pad_lanes(wt["b_forget"].reshape(1, NH))
    qg_a, kg_a = _tile_heads(wt["qnorm_a_g"], D), _tile_heads(wt["knorm_a_g"], D)
    qg_b, kg_b = _tile_heads(wt["qnorm_b_g"], D), _tile_heads(wt["knorm_b_g"], KVW)
    norm_a, kv_g, norm_b = wt["norm_a_g"].reshape(1, D), wt["kv_norm_g"].reshape(1, D), wt["norm_b_g"].reshape(1, D)
    sinks_t = jnp.repeat(wt["sinks"].reshape(1, NH), HD, axis=1)

    (u_a,) = _rmsnorm_fwd(x, [norm_a], "norm_a")
    qkv = _mm("proj_a", S, 3 * D, [(u_a, _a_rows(D), w1t, _b_rows(D), NT)])
    fpad = _mm("proj_f", S, LANES, [(u_a, _a_rows(D), w1t, _b_rows(D, row0=f_row, tn=LANES), NT)], tn=LANES)
    gate_a = _mm("proj_gate_a", S, D, [(u_a, _a_rows(D), wg_t, _b_rows(D), NT)])
    q_a, k_a, v_a = _a_post(qkv, qg_a, kg_a)
    ct = _forget_cumsum(fpad, b_pad)
    ct2 = ct[:NH].reshape(NH // 2, 2, S)
    o_a, lse_a, fetched = _fox_fwd(q_a, k_a, v_a, ct2, fetch)
    wt = {**wt, **late_weights(fetched)}
    w_in_b = wt["w_in_b"]
    y_a = _gate_fwd(o_a, gate_a, 0, "gate_a")
    h1 = _mm("out_a", S, D, [(y_a, _a_rows(D), wt["w_out_a"], _b_cols(D), None)], add=x)
    u_kv, u_b = _rmsnorm_fwd(h1, [kv_g, norm_b], "norm_b")
    kv = _mm("proj_kv", S, 2 * KVW, [(u_kv, _a_rows(D), wt["w_kv"], _b_cols(D), None)])
    pb = _mm("proj_b", S, 2 * D,
             [(u_b, _a_rows(D), w_in_b, pl.BlockSpec((None, D, TN_), lambda j, i: (j, 0, 0)), None)])
    q_b, kdup, vdup = _b_post(pb, kv, qg_b, kg_b, rope)
    o_b, lse_b = _swa_fwd(q_b, kdup, vdup, sinks_t)
    y_b = _gate_fwd(o_b, pb, 1, "gate_b")
    out = _mm("out_b", S, D, [(y_b, _a_rows(D), wt["w_out_b"], _b_cols(D), None)], add=h1)
    d_out, d_out_b, sq = _loss_head(out, target)

    g = {}
    g["w_out_b"] = _mm("dw_out_b", D, D, [(y_b, _a_cols(S), d_out_b, _b_cols(S), TN)])
    d_y_b = _mm("dy_b", S, D, [(d_out_b, _a_rows(D), wt["w_out_b"], _b_rows(D), NT)])
    d_o_b, d_gate_b = _gate_bwd(d_y_b, o_b, pb, 1, "gate_b_bwd")
    dq_b, dkdup, dvdup, dsk = _swa_bwd(q_b, kdup, vdup, sinks_t, o_b, lse_b, d_o_b)
    g["sinks"] = dsk[0, ::HD]
    d_qb_raw, dg = _headnorm_bwd(pb, 0, qg_b, dq_b, rope, "qnorm_b_bwd")
    g["qnorm_b_g"] = _fold_heads(dg)
    d_pb = [d_qb_raw, d_qb_raw, d_gate_b, d_gate_b]
    g["w_in_b"] = jnp.concatenate([
        _mm("dw_in_b_q", D, D, [(u_b, _a_cols(S), d_qb_raw, _b_cols(S), TN)], stacked=True),
        _mm("dw_in_b_gate", D, D, [(u_b, _a_cols(S), d_gate_b, _b_cols(S), TN)], stacked=True)], axis=0)
    d_u_b = _mm("du_b", S, D, [(d_pb[c], _a_rows(TN_, col=c % 2), w_in_b, in_b_block(c), NT) for c in range(NCHIP)])
    d_kv, dg = _kv_bwd(dkdup, dvdup, kv, kg_b, rope)
    g["knorm_b_g"] = _fold_heads(dg)
    g["w_kv"] = _mm("dw_kv", D, 2 * KVW, [(u_kv, _a_cols(S), d_kv, _b_cols(S), TN)])
    d_u_kv = _mm("du_kv", S, D, [(d_kv, _a_rows(2 * KVW), wt["w_kv"], _b_rows(2 * KVW), NT)])
    d_h1, d_h1_b, g["kv_norm_g"], g["norm_b_g"] = _rmsnorm_bwd(h1, [kv_g, norm_b], [d_u_kv, d_u_b], d_out, "norm_b_bwd")
    g["w_out_a"] = _mm("dw_out_a", D, D, [(y_a, _a_cols(S), d_h1_b, _b_cols(S), TN)])
    d_y_a = _mm("dy_a", S, D, [(d_h1_b, _a_rows(D), wt["w_out_a"], _b_rows(D), NT)])
    d_o_a, d_gate_a = _gate_bwd(d_y_a, o_a, gate_a, 0, "gate_a_bwd")
    riding, so_far = begin_reduce({n: g[n] for n in LATE})
    dq_a, dk_a, dv_a, dct, arrived = _fox_bwd(q_a, k_a, v_a, ct2, o_a, lse_a, d_o_a, riding)
    dct_pad = jnp.pad(dct.reshape(NH, S), ((0, LANES - NH), (0, 0)))
    d_f, db = _forget_bwd(dct_pad, fpad, b_pad)
    g["b_forget"] = db[0, :NH]
    d_q_raw, dg = _headnorm_bwd(qkv, 0, qg_a, dq_a, None, "qnorm_a_bwd")
    g["qnorm_a_g"] = _fold_heads(dg)
    d_k_raw, dg = _headnorm_bwd(qkv, 1, kg_a, dk_a, None, "knorm_a_bwd")
    g["knorm_a_g"] = _fold_heads(dg)
    pieces = [("q", d_q_raw), ("k", d_k_raw), ("v", dv_a), ("gate", d_gate_a)]
    dw = {n: _mm("dw_in_a_" + n, D, D, [(t, _a_cols(S), u_a, _b_cols(S), TN)]) for n, t in pieces}
    dw_f = _mm("dw_in_a_f", LANES, D, [(d_f, _a_cols(S, tm=LANES), u_a, _b_cols(S), TN)], tm=LANES)
    g["w_in_a"] = jnp.concatenate([dw["q"], dw["k"], dw["v"], dw_f[:NH], dw["gate"]], axis=0)
    d_u_a = _mm("du_a", S, D, [
        (d_q_raw, _a_rows(D), w1t, _b_cols(D, row=0), None), (d_k_raw, _a_rows(D), w1t, _b_cols(D, row=1), None),
        (dv_a, _a_rows(D), w1t, _b_cols(D, row=2), None), (d_gate_a, _a_rows(D), wg_t, _b_cols(D), None),
        (d_f, _a_rows(LANES), w1t, _b_cols(LANES, row=f_row), None)])
    d_x, _, g["norm_a_g"] = _rmsnorm_bwd(x, [norm_a], [d_u_a], d_h1, "norm_a_bwd")
    return sq, d_x, g, (so_far, arrived)


BIG = ["w_in_a", "w_out_a", "w_kv", "w_in_b", "w_out_b"]
LATE = BIG[1:]
SPLIT = {"w_in_a": None, "w_out_a": 0, "w_kv": 0, "w_in_b": 0, "w_out_b": 0}
SMALL = ["norm_a_g", "b_forget", "qnorm_a_g", "knorm_a_g", "kv_norm_g", "knorm_b_g", "norm_b_g", "qnorm_b_g", "sinks"]
NAMES = ["norm_a_g", "w_in_a", "b_forget", "qnorm_a_g", "knorm_a_g", "w_out_a", "kv_norm_g", "w_kv", "knorm_b_g",
         "norm_b_g", "w_in_b", "qnorm_b_g", "sinks", "w_out_b"]


def _pack(vals):
    flat = []
    for v in vals:
        v = v.reshape(-1)
        flat.append(jnp.pad(v, (0, -v.shape[0] % LANES)))
    flat = jnp.concatenate(flat)
    flat = jnp.pad(flat, (0, -flat.shape[0] % (8 * LANES)))
    return flat.reshape(-1, LANES)


def _unpack(packed, shapes):
    flat, out, off = packed.reshape(-1), [], 0
    for s in shapes:
        n = int(np.prod(s))
        out.append(flat[off:off + n].reshape(s))
        off += n + (-n % LANES)
    return out


def kernel(x, positions, norm_a_g, w_in_a, b_forget, qnorm_a_g, knorm_a_g, w_out_a, kv_norm_g, w_kv, knorm_b_g, norm_b_g, w_in_b, qnorm_b_g, sinks, w_out_b, loss_target, m_norm_a_g, m_w_in_a, m_b_forget, m_qnorm_a_g, m_knorm_a_g, m_w_out_a, m_kv_norm_g, m_w_kv, m_knorm_b_g, m_norm_b_g, m_w_in_b, m_qnorm_b_g, m_sinks, m_w_out_b, v_norm_a_g, v_w_in_a, v_b_forget, v_qnorm_a_g, v_knorm_a_g, v_w_out_a, v_kv_norm_g, v_w_kv, v_knorm_b_g, v_norm_b_g, v_w_in_b, v_qnorm_b_g, v_sinks, v_w_out_b):
    w = dict(norm_a_g=norm_a_g, w_in_a=w_in_a, b_forget=b_forget, qnorm_a_g=qnorm_a_g, knorm_a_g=knorm_a_g,
             w_out_a=w_out_a, kv_norm_g=kv_norm_g, w_kv=w_kv, knorm_b_g=knorm_b_g, norm_b_g=norm_b_g,
             w_in_b=w_in_b, qnorm_b_g=qnorm_b_g, sinks=sinks, w_out_b=w_out_b)
    m = dict(norm_a_g=m_norm_a_g, w_in_a=m_w_in_a, b_forget=m_b_forget, qnorm_a_g=m_qnorm_a_g, knorm_a_g=m_knorm_a_g,
             w_out_a=m_w_out_a, kv_norm_g=m_kv_norm_g, w_kv=m_w_kv, knorm_b_g=m_knorm_b_g, norm_b_g=m_norm_b_g,
             w_in_b=m_w_in_b, qnorm_b_g=m_qnorm_b_g, sinks=m_sinks, w_out_b=m_w_out_b)
    v = dict(norm_a_g=v_norm_a_g, w_in_a=v_w_in_a, b_forget=v_b_forget, qnorm_a_g=v_qnorm_a_g, knorm_a_g=v_knorm_a_g,
             w_out_a=v_w_out_a, kv_norm_g=v_kv_norm_g, w_kv=v_w_kv, knorm_b_g=v_knorm_b_g, norm_b_g=v_norm_b_g,
             w_in_b=v_w_in_b, qnorm_b_g=v_qnorm_b_g, sinks=v_sinks, w_out_b=v_w_out_b)
    my_chip = 2 * lax.axis_index("x") + lax.axis_index("y")

    def shard2d(t, n):
        if n == "w_in_a":
            return jnp.transpose(t, (2, 0, 1)).reshape(-1)
        return t.reshape(t.shape[-2:])

    def unflat(t, n):
        return jnp.transpose(t.reshape(-1, 1, D), (1, 2, 0)) if n == "w_in_a" else t.reshape(w[n].shape)

    w2d = {n: shard2d(w[n], n) for n in BIG}

    norm_a_rows = jnp.broadcast_to(norm_a_g.reshape(1, D // NCHIP), (16, D // NCHIP))
    w1t, norm_rows = _gather_shards([w2d["w_in_a"].astype(BF16), norm_a_rows], [SPLIT["w_in_a"], 0])
    wt = {"w_in_a_t": w1t.reshape(-1, D), "norm_a_g": norm_rows[:, 0, :].reshape(1, D)}
    for n in SMALL[1:]:
        wt[n] = w[n]
    late_shards = [w2d[n].astype(BF16) for n in LATE]
    late_axes = [SPLIT[n] for n in LATE]
    transfers, outs = _gather_plan(late_shards, late_axes)
    fetch = _Riding(transfers, late_shards, outs)

    def late_weights(fetched):
        whole = dict(zip(LATE, _select_own(late_shards, fetched)))
        return {n: t if n == "w_in_b" else t.reshape(-1, t.shape[2]) for n, t in whole.items()}

    def as_blocks(t):
        if t.ndim == 3:
            return t
        return t.reshape(-1) if t.shape[0] % (8 * NCHIP) else t.reshape(NCHIP, -1, t.shape[1])

    def chip_sums(names, grads, name):
        axes = [SPLIT[n] for n in names]
        blocks = [as_blocks(grads[n]) for n in names]
        sums = [_chip_sum(blk, part, ax, "chip_sum_" + n)
                for n, ax, blk, part in zip(names, axes, blocks, _halves_to_sibling(blocks, axes, name))]
        return [s[0] for s in sums], [s[1] for s in sums]

    def begin_reduce(grads):
        f32, bf16 = chip_sums(LATE, grads, "sibling_halves_late")
        transfers, outs = _scatter_plan(bf16)
        return _Riding(transfers, bf16, outs), f32

    sq, d_x, g, (late_f32, late_arrived) = _local_step(x[0], loss_target[0], positions, wt, fetch, late_weights,
                                                      begin_reduce)

    small_shapes = [(D,), (NH,), (HD,), (HD,), (D,), (HD,), (D,), (HD,), (NH,), (D,)]
    packed = _pack([g[n] for n in SMALL] + [sq])
    total = _sum_stack(_gather_small(packed), "sum_small")
    small_g = dict(zip(SMALL, _unpack(total, small_shapes)[:-1]))
    loss = 0.5 * jnp.sum(_unpack(total, small_shapes)[-1]) / D
    small_g["norm_a_g"] = lax.dynamic_slice(small_g["norm_a_g"], (my_chip * (D // NCHIP),), (D // NCHIP,))

    axes = [SPLIT[n] for n in BIG]
    first_f32, first_bf16 = chip_sums(["w_in_a"], g, "sibling_halves")
    chip_f32 = first_f32 + list(late_f32)
    arrived = list(_scatter_chip_sums(first_bf16)) + list(late_arrived)
    halves = []
    for n, ax, t32, parts in zip(BIG, axes, chip_f32, arrived):
        if t32.ndim == 1:
            own = lax.dynamic_slice_in_dim(t32, my_chip * (t32.shape[0] // NCHIP), t32.shape[0] // NCHIP)
        else:
            own = lax.dynamic_index_in_dim(t32, my_chip, axis=0, keepdims=False)
        halves.append(_mesh_sum(own, parts, ax, "mesh_sum_" + n))
    sibling_done = _to_sibling(halves, "finished_halves")

    res = {}
    for n, ax, mine_half, their_half in zip(BIG, axes, halves, sibling_done):
        out4 = _adamw_halves(w2d[n], mine_half, their_half, shard2d(m[n], n), shard2d(v[n], n), ax, "adamw_" + n)
        res[n] = tuple(unflat(t, n) for t in out4)
    sm_g = _pack([small_g[n] for n in SMALL])
    sm = [_pack([d[n] for n in SMALL]) for d in (w, m, v)]
    sm_out = _adamw(sm[0], sm_g, sm[1], sm[2], "adamw_small")
    sm_shapes = [w[n].shape for n in SMALL]
    unpacked = [_unpack(t, sm_shapes) for t in (sm_g,) + tuple(sm_out)]
    for i, n in enumerate(SMALL):
        res[n] = tuple(u[i] for u in unpacked)

    outs = [loss, d_x[None]]
    for k in range(4):
        outs += [res[n][k] for n in NAMES]
    return tuple(outs)
```

```python
import numpy as np
import jax
import jax.numpy as jnp
from jax import lax
from jax.experimental import pallas as pl
from jax.experimental.pallas import tpu as pltpu

F32, BF16 = jnp.float32, jnp.bfloat16
S, D, HD, NH, NKV = 2048, 1024, 64, 16, 4
KVW = NKV * HD
WINDOW = 128
ROT = HD // 4
THETA = 500000.0
EPS = 1e-6
SCALE = HD ** -0.5
LANES = 128
NEG = -1e30
VMEM_LIMIT = 48 * 2 ** 20
ROWS = 256
ATT = 256
SWQ = 4
NCHIP = 4
ADAM_LR, ADAM_B1, ADAM_B2, ADAM_EPS, ADAM_WD, ADAM_STEP = 0.001, 0.9, 0.999, 1e-08, 0.01, 10
NT = (((1,), (1,)), ((), ()))
TN = (((0,), (0,)), ((), ()))
MESH = pl.DeviceIdType.MESH


def _params(n):
    return pltpu.CompilerParams(dimension_semantics=("arbitrary",) * n, vmem_limit_bytes=VMEM_LIMIT)


def _dot(a, b, dims=None):
    if dims is None:
        return jnp.dot(a, b, preferred_element_type=F32)
    return lax.dot_general(a, b, dims, preferred_element_type=F32)


def _dot_split(a, b, n):
    out, rest = None, a
    for _ in range(n):
        hi = rest.astype(BF16)
        term = _dot(hi, b)
        out = term if out is None else out + term
        rest = rest - hi.astype(F32)
    return out


def _seg_mat(w):
    e = (np.arange(w)[:, None] // HD == np.arange(LANES)[None, :]).astype(np.float32)
    return jnp.asarray(e, BF16)


def _spread(r, w):
    head = lax.broadcasted_iota(jnp.int32, (LANES, w), 1) >> 6
    et = jnp.where(head == lax.broadcasted_iota(jnp.int32, (LANES, w), 0), 1.0, 0.0).astype(BF16)
    return _dot_split(r, et, 3)


def _head_rstd(x, e):
    ss = _dot_split(x * x, e, 2)
    return _spread(lax.rsqrt(ss * (1.0 / HD) + EPS), x.shape[1])


def _rope(x, c, a, b):
    w = x.shape[1]
    return x * c + pltpu.roll(x, w - ROT // 2, 1) * a + pltpu.roll(x, ROT // 2, 1) * b


def _rope_t(dy, c, a, b):
    w = dy.shape[1]
    return dy * c + pltpu.roll(dy * b, w - ROT // 2, 1) + pltpu.roll(dy * a, ROT // 2, 1)


def _sigmoid(x):
    return 1.0 / (1.0 + jnp.exp(-x))


def _row_spec(shape, ts):
    nd = len(shape)
    if shape[0] == S:
        return pl.BlockSpec((ts,) + tuple(shape[1:]), lambda i: (i,) + (0,) * (nd - 1))
    return pl.BlockSpec(tuple(shape), lambda i: (0,) * nd)


def _rows_call(body, name, ins, outs, ts=ROWS):
    return pl.pallas_call(
        body, name=name, grid=(S // ts,),
        in_specs=[_row_spec(a.shape, ts) for a in ins],
        out_specs=[_row_spec(s, ts) for s, _ in outs],
        out_shape=[jax.ShapeDtypeStruct(s, d) for s, d in outs],
        compiler_params=_params(1))(*ins)


def _col_spec(ts, w, col):
    return pl.BlockSpec((ts, w), lambda i: (i, col))


TM = TN_ = 512
TM_TOKENS = 1024


def _mm(name, m, n, terms, out_dtype=F32, add=None, tm=None, tn=TN_, stacked=False):
    nterm = len(terms)
    if tm is None:
        tm = TM_TOKENS if m == S else TM

    def body(*refs):
        acc = None
        for t in range(nterm):
            part = _dot(refs[2 * t][...], refs[2 * t + 1][...], terms[t][4])
            acc = part if acc is None else acc + part
        if add is not None:
            acc = acc + refs[2 * nterm][...]
        refs[-1][...] = acc.astype(out_dtype)

    tile = pl.BlockSpec((tm, tn), lambda j, i: (i, j))
    ins, specs = [], []
    for a, a_spec, b, b_spec, _ in terms:
        ins += [a, b]
        specs += [a_spec, b_spec]
    if add is not None:
        ins.append(add)
        specs.append(tile)
    return pl.pallas_call(
        body, name=name, grid=(n // tn, m // tm), in_specs=specs,
        out_specs=pl.BlockSpec((None, tm, tn), lambda j, i: (j, i, 0)) if stacked else tile,
        out_shape=jax.ShapeDtypeStruct((n // tn, m, tn) if stacked else (m, n), out_dtype),
        compiler_params=_params(2))(*ins)


def _a_rows(k, col=0, tm=TM_TOKENS):
    return pl.BlockSpec((tm, k), lambda j, i: (i, col))


def _a_cols(k, tm=TM):
    return pl.BlockSpec((k, tm), lambda j, i: (0, i))


def _b_cols(k, row=0, col0=0, tn=TN_):
    return pl.BlockSpec((k, tn), lambda j, i: (row, col0 + j))


def _b_rows(k, row0=0, tn=TN_):
    return pl.BlockSpec((tn, k), lambda j, i: (row0 + j, 0))


def _rmsnorm_fwd(x, gains, name):
    def body(*refs):
        xv = refs[0][...]
        r = lax.rsqrt(jnp.mean(xv * xv, axis=-1, keepdims=True) + EPS)
        xh = xv * r
        for n in range(len(gains)):
            refs[1 + len(gains) + n][...] = (xh * refs[1 + n][...]).astype(BF16)

    return _rows_call(body, name, [x] + list(gains), [((S, D), BF16)] * len(gains))


def _rmsnorm_bwd(x, gains, dus, dres, name):
    n = len(gains)

    def body(*refs):
        x_ref, g_refs, du_refs, dres_ref = refs[0], refs[1:1 + n], refs[1 + n:1 + 2 * n], refs[1 + 2 * n]
        dx_ref, dxb_ref, dg_refs = refs[2 + 2 * n], refs[3 + 2 * n], refs[4 + 2 * n:]
        xv = x_ref[...]
        r = lax.rsqrt(jnp.mean(xv * xv, axis=-1, keepdims=True) + EPS)
        xh = xv * r
        gy = None
        for m in range(n):
            du = du_refs[m][...]
            part = jnp.sum(du * xh, axis=0, keepdims=True)

            @pl.when(pl.program_id(0) == 0)
            def _(m=m, part=part):
                dg_refs[m][...] = part

            @pl.when(pl.program_id(0) != 0)
            def _(m=m, part=part):
                dg_refs[m][...] += part

            t = du * g_refs[m][...]
            gy = t if gy is None else gy + t
        dx = dres_ref[...] + r * (gy - xh * jnp.mean(gy * xh, axis=-1, keepdims=True))
        dx_ref[...] = dx
        dxb_ref[...] = dx.astype(BF16)

    outs = [((S, D), F32), ((S, D), BF16)] + [((1, D), F32)] * n
    return _rows_call(body, name, [x] + list(gains) + list(dus) + [dres], outs)


def _a_post(qkvg, qg, kg):
    e = _seg_mat(D)

    def body(q_ref, k_ref, v_ref, qg_ref, kg_ref, e_ref, qo, ko, vo):
        ev = e_ref[...]
        qv, kv = q_ref[...], k_ref[...]
        qo[...] = (qv * _head_rstd(qv, ev) * qg_ref[...] * SCALE).astype(BF16)
        ko[...] = (kv * _head_rstd(kv, ev) * kg_ref[...]).astype(BF16)
        vo[...] = v_ref[...].astype(BF16)

    whole = lambda a: pl.BlockSpec(a.shape, lambda i: (0, 0))
    return pl.pallas_call(
        body, name="a_post", grid=(S // ROWS,),
        in_specs=[_col_spec(ROWS, D, 0), _col_spec(ROWS, D, 1), _col_spec(ROWS, D, 2),
                  whole(qg), whole(kg), whole(e)],
        out_specs=[_col_spec(ROWS, D, 0)] * 3,
        out_shape=[jax.ShapeDtypeStruct((S, D), BF16)] * 3,
        compiler_params=_params(1))(qkvg, qkvg, qkvg, qg, kg, e)


def _tri(upper):
    r, c = np.arange(ROWS)[:, None], np.arange(ROWS)[None, :]
    return jnp.asarray((r <= c) if upper else (r >= c), BF16)


def _forget_cumsum(fpad, bpad):
    def body(f_ref, b_ref, u_ref, c_ref, carry):
        @pl.when(pl.program_id(0) == 0)
        def _():
            carry[...] = jnp.zeros_like(carry)

        lf = jax.nn.log_sigmoid(f_ref[...] + b_ref[...])
        blk = _dot_split(lf.T, u_ref[...], 3) + carry[:, 0:1]
        c_ref[...] = blk
        carry[...] = jnp.broadcast_to(blk[:, ROWS - 1:ROWS], carry.shape)

    return pl.pallas_call(
        body, name="forget_cumsum", grid=(S // ROWS,),
        in_specs=[pl.BlockSpec((ROWS, LANES), lambda i: (i, 0)), pl.BlockSpec((1, LANES), lambda i: (0, 0)),
                  pl.BlockSpec((ROWS, ROWS), lambda i: (0, 0))],
        out_specs=pl.BlockSpec((LANES, ROWS), lambda i: (0, i)),
        out_shape=jax.ShapeDtypeStruct((LANES, S), F32),
        scratch_shapes=[pltpu.VMEM((LANES, LANES), F32)],
        compiler_params=_params(1))(fpad, bpad, _tri(True))


def _forget_bwd(dct, fpad, bpad):
    nb = S // ROWS

    def body(dc_ref, f_ref, b_ref, l_ref, df_ref, db_ref, carry):
        @pl.when(pl.program_id(0) == 0)
        def _():
            carry[...] = jnp.zeros_like(carry)
            db_ref[...] = jnp.zeros_like(db_ref)

        blk = _dot_split(dc_ref[...], l_ref[...], 3) + carry[:, 0:1]
        carry[...] = jnp.broadcast_to(blk[:, 0:1], carry.shape)
        df = blk.T * _sigmoid(-(f_ref[...] + b_ref[...]))
        df_ref[...] = df.astype(BF16)
        db_ref[...] += jnp.sum(df, axis=0, keepdims=True)

    return pl.pallas_call(
        body, name="forget_bwd", grid=(nb,),
        in_specs=[pl.BlockSpec((LANES, ROWS), lambda i: (0, nb - 1 - i)),
                  pl.BlockSpec((ROWS, LANES), lambda i: (nb - 1 - i, 0)),
                  pl.BlockSpec((1, LANES), lambda i: (0, 0)), pl.BlockSpec((ROWS, ROWS), lambda i: (0, 0))],
        out_specs=[pl.BlockSpec((ROWS, LANES), lambda i: (nb - 1 - i, 0)), pl.BlockSpec((1, LANES), lambda i: (0, 0))],
        out_shape=[jax.ShapeDtypeStruct((S, LANES), BF16), jax.ShapeDtypeStruct((1, LANES), F32)],
        scratch_shapes=[pltpu.VMEM((LANES, LANES), F32)],
        compiler_params=_params(1))(dct, fpad, bpad, _tri(False))


def _gate_fwd(o, proj, col, name):
    def body(o_ref, g_ref, y_ref):
        g = g_ref[...]
        y_ref[...] = (o_ref[...] * (g * _sigmoid(g))).astype(BF16)

    return pl.pallas_call(
        body, name=name, grid=(S // ROWS,),
        in_specs=[_col_spec(ROWS, D, 0), _col_spec(ROWS, D, col)],
        out_specs=_col_spec(ROWS, D, 0), out_shape=jax.ShapeDtypeStruct((S, D), BF16),
        compiler_params=_params(1))(o, proj)


def _gate_bwd(dy, o, proj, col, name):
    def body(dy_ref, o_ref, g_ref, do_ref, dg_ref):
        g, dyv = g_ref[...], dy_ref[...]
        sg = _sigmoid(g)
        do_ref[...] = dyv * (g * sg)
        dg_ref[...] = (dyv * o_ref[...] * (sg * (1.0 + g * (1.0 - sg)))).astype(BF16)

    return pl.pallas_call(
        body, name=name, grid=(S // ROWS,),
        in_specs=[_col_spec(ROWS, D, 0), _col_spec(ROWS, D, 0), _col_spec(ROWS, D, col)],
        out_specs=[_col_spec(ROWS, D, 0)] * 2,
        out_shape=[jax.ShapeDtypeStruct((S, D), F32), jax.ShapeDtypeStruct((S, D), BF16)],
        compiler_params=_params(1))(dy, o, proj)


def _headnorm_bwd(x, col, gain, dy, rope, name):
    e = _seg_mat(D)
    tabs = list(rope) if rope is not None else []

    def body(*refs):
        x_ref, g_ref, dy_ref, e_ref = refs[:4]
        dx_ref, dg_ref = refs[-2:]
        xv, dyv, ev = x_ref[...], dy_ref[...], e_ref[...]
        if rope is not None:
            c, a, b = (jnp.tile(t[...], (1, D // LANES)) for t in refs[4:7])
            dyv = _rope_t(dyv, c, a, b)
        r = _head_rstd(xv, ev)
        xh = xv * r
        part = jnp.sum(dyv * xh, axis=0, keepdims=True)

        @pl.when(pl.program_id(0) == 0)
        def _():
            dg_ref[...] = part

        @pl.when(pl.program_id(0) != 0)
        def _():
            dg_ref[...] += part

        gy = dyv * g_ref[...]
        seg = _spread(_dot_split(gy * xh, ev, 2) * (1.0 / HD), D)
        dx_ref[...] = (r * (gy - xh * seg)).astype(BF16)

    whole = lambda a: pl.BlockSpec(a.shape, lambda i: (0, 0))
    return pl.pallas_call(
        body, name=name, grid=(S // ROWS,),
        in_specs=[_col_spec(ROWS, D, col), whole(gain), _col_spec(ROWS, D, 0), whole(e)]
                 + [pl.BlockSpec((ROWS, LANES), lambda i: (i, 0))] * len(tabs),
        out_specs=[_col_spec(ROWS, D, 0), whole(gain)],
        out_shape=[jax.ShapeDtypeStruct((S, D), BF16), jax.ShapeDtypeStruct((1, D), F32)],
        compiler_params=_params(1))(x, gain, dy, e, *tabs)


def _dup_mat():
    r, c = np.arange(KVW)[:, None], np.arange(2 * KVW)[None, :]
    return (r // HD == c // LANES) & (r % HD == c % HD)


def _fold_mat():
    r, c = np.arange(D)[:, None], np.arange(KVW)[None, :]
    return (r // (2 * LANES) == c // HD) & (r % HD == c % HD)


def _b_post(pb, kv, qg, kg, rope):
    e, ek = _seg_mat(D), _seg_mat(KVW)
    dup = jnp.asarray(_dup_mat(), BF16)

    def body(q_ref, k_ref, v_ref, qg_ref, kg_ref, e_ref, ek_ref, dup_ref, c_ref, a_ref, b_ref, qo, ko, vo):
        c1, a1, b1 = c_ref[...], a_ref[...], b_ref[...]
        qv = q_ref[...]
        qn = qv * _head_rstd(qv, e_ref[...]) * qg_ref[...]
        t = lambda z, n: jnp.tile(z, (1, n))
        qo[...] = (_rope(qn, t(c1, D // LANES), t(a1, D // LANES), t(b1, D // LANES)) * SCALE).astype(BF16)
        kvv = k_ref[...]
        kn = kvv * _head_rstd(kvv, ek_ref[...]) * kg_ref[...]
        kr = _rope(kn, t(c1, KVW // LANES), t(a1, KVW // LANES), t(b1, KVW // LANES)).astype(BF16)
        ko[...] = _dot(kr, dup_ref[...]).astype(BF16)
        vo[...] = _dot(v_ref[...].astype(BF16), dup_ref[...]).astype(BF16)

    whole = lambda a: pl.BlockSpec(a.shape, lambda i: (0, 0))
    tab = pl.BlockSpec((ROWS, LANES), lambda i: (i, 0))
    return pl.pallas_call(
        body, name="b_post", grid=(S // ROWS,),
        in_specs=[_col_spec(ROWS, D, 0), _col_spec(ROWS, KVW, 0), _col_spec(ROWS, KVW, 1),
                  whole(qg), whole(kg), whole(e), whole(ek), whole(dup), tab, tab, tab],
        out_specs=[_col_spec(ROWS, D, 0), _col_spec(ROWS, 2 * KVW, 0), _col_spec(ROWS, 2 * KVW, 0)],
        out_shape=[jax.ShapeDtypeStruct((S, D), BF16), jax.ShapeDtypeStruct((S, 2 * KVW), BF16),
                   jax.ShapeDtypeStruct((S, 2 * KVW), BF16)],
        compiler_params=_params(1))(pb, kv, kv, qg, kg, e, ek, dup, *rope)


def _kv_bwd(dkdup, dvdup, kv, kg, rope):
    ek = _seg_mat(KVW)
    fold = jnp.asarray(_fold_mat(), BF16)

    def body(dk_ref, dv_ref, k_ref, kg_ref, ek_ref, fold_ref, c_ref, a_ref, b_ref, dkv_ref, dg_ref):
        ev, fv = ek_ref[...], fold_ref[...]
        t = lambda z: jnp.tile(z[...], (1, KVW // LANES))
        dk = _rope_t(_dot_split(dk_ref[...], fv, 3), t(c_ref), t(a_ref), t(b_ref))
        dv = _dot_split(dv_ref[...], fv, 3)
        xv = k_ref[...]
        r = _head_rstd(xv, ev)
        xh = xv * r
        part = jnp.sum(dk * xh, axis=0, keepdims=True)

        @pl.when(pl.program_id(0) == 0)
        def _():
            dg_ref[...] = part

        @pl.when(pl.program_id(0) != 0)
        def _():
            dg_ref[...] += part

        gy = dk * kg_ref[...]
        seg = _spread(_dot_split(gy * xh, ev, 2) * (1.0 / HD), KVW)
        dkv_ref[:, 0:KVW] = (r * (gy - xh * seg)).astype(BF16)
        dkv_ref[:, KVW:2 * KVW] = dv.astype(BF16)

    whole = lambda a: pl.BlockSpec(a.shape, lambda i: (0, 0))
    tab = pl.BlockSpec((ROWS, LANES), lambda i: (i, 0))
    return pl.pallas_call(
        body, name="kv_bwd", grid=(S // ROWS,),
        in_specs=[_col_spec(ROWS, D, 0), _col_spec(ROWS, D, 0), _col_spec(ROWS, KVW, 0),
                  whole(kg), whole(ek), whole(fold), tab, tab, tab],
        out_specs=[_col_spec(ROWS, 2 * KVW, 0), whole(kg)],
        out_shape=[jax.ShapeDtypeStruct((S, 2 * KVW), BF16), jax.ShapeDtypeStruct((1, KVW), F32)],
        compiler_params=_params(1))(dkdup, dvdup, kv, kg, ek, fold, *rope)


def _loss_head(out, target):
    def body(o_ref, t_ref, d_ref, db_ref, l_ref):
        diff = o_ref[...] - t_ref[...]
        d = diff * (1.0 / D)
        d_ref[...] = d
        db_ref[...] = d.astype(BF16)

        @pl.when(pl.program_id(0) == 0)
        def _():
            l_ref[...] = jnp.zeros_like(l_ref)

        l_ref[...] += jnp.sum(diff * diff, axis=0, keepdims=True)

    return _rows_call(body, "loss_head", [out, target], [((S, D), F32), ((S, D), BF16), ((1, D), F32)])


def _lane():
    return lax.broadcasted_iota(jnp.int32, (1, LANES), 1)


def _head_mask(hh):
    return (_lane() < HD) if hh == 0 else (_lane() >= HD)


def _fox_fwd(q, k, v, ct, riding):
    nq, npair = S // ATT, NH // 2
    ni, no = len(riding.ins), len(riding.outs)

    def body(q_ref, k_ref, v_ref, c_ref, *rest):
        o_ref, lse_ref = rest[ni:ni + 2]
        pair, i = pl.program_id(0), pl.program_id(1)
        at_end = riding.hooks(rest[:ni], rest[ni + 2:ni + 2 + no], *rest[ni + 2 + no:],
                              first=(pair == 0) & (i == 0), middle=(pair == npair // 2) & (i == 0),
                              last=(pair == npair - 1) & (i == nq - 1))
        q2 = q_ref[...]
        qms = [jnp.where(_head_mask(hh), q2, jnp.zeros_like(q2)) for hh in (0, 1)]

        def probs(off, width, m, hh, diag):
            s = _dot(qms[hh], k_ref[pl.ds(off, width), :], NT) - c_ref[hh:hh + 1, pl.ds(off, width)]
            if diag:
                row = i * ATT + lax.broadcasted_iota(jnp.int32, (ATT, width), 0)
                col = off + lax.broadcasted_iota(jnp.int32, (ATT, width), 1)
                s = jnp.where(col <= row, s, NEG)
            m_new = jnp.maximum(m, jnp.max(s, axis=1, keepdims=True))
            p = jnp.exp(s - m_new)
            p_hi = p.astype(BF16)
            return m_new, jnp.exp(m - m_new), p_hi, (p - p_hi.astype(F32)).astype(BF16)

        def weighted(off, width, p_hi, p_lo, hh):
            vj = v_ref[pl.ds(off, width), :]
            v1 = jnp.where(_head_mask(hh), vj, jnp.ones_like(vj))
            return _dot(p_hi, v1) + _dot(p_lo, v1)

        def step(off, width, carry, diag):
            off = pl.multiple_of(off, ATT)
            out = []
            for hh in (0, 1):
                m, acc = carry[hh]
                m, alpha, p_hi, p_lo = probs(off, width, m, hh, diag)
                out.append((m, alpha * acc + weighted(off, width, p_hi, p_lo, hh)))
            return tuple(out)

        one = (jnp.full((ATT, 1), NEG, F32), jnp.zeros((ATT, LANES), F32))
        carry = lax.fori_loop(0, i // 2, lambda j, cr: step(j * (2 * ATT), 2 * ATT, cr, False), (one, one))
        carry = lax.cond(i % 2 == 1, lambda cr: step((i - 1) * ATT, 2 * ATT, cr, True),
                         lambda cr: step(i * ATT, ATT, cr, True), carry)
        res = []
        for hh in (0, 1):
            m, acc = carry[hh]
            l = jnp.max(jnp.where(_head_mask(1 - hh), acc, 0.0), axis=1, keepdims=True)
            res.append((acc / l, m + jnp.log(l)))
        first = _head_mask(0)
        o_ref[...] = jnp.where(first, res[0][0], res[1][0])
        lse_ref[...] = jnp.where(first, res[0][1], res[1][1])
        at_end()

    blk = pl.BlockSpec((ATT, LANES), lambda p, i: (i, p))
    full = pl.BlockSpec((S, LANES), lambda p, i: (0, p))
    res = pl.pallas_call(
        body, name="fox_fwd", grid=(npair, nq),
        in_specs=[blk, full, full, pl.BlockSpec((None, 2, S), lambda p, i: (p, 0, 0))] + riding.in_specs,
        out_specs=[blk, blk] + riding.out_specs,
        out_shape=[jax.ShapeDtypeStruct((S, D), F32)] * 2 + riding.out_shape,
        scratch_shapes=riding.scratch,
        compiler_params=_params(2))(q, k, v, ct, *riding.ins)
    return res[0], res[1], res[2:]


def _fox_bwd(q, k, v, ct, o, lse, do, riding):
    nq, npair = S // ATT, NH // 2
    ni, no = len(riding.ins), len(riding.outs)

    def body(q_ref, k_ref, v_ref, c_ref, o_ref, lse_ref, do_ref, *rest):
        dq_ref, dk_ref, dvb_ref, dc_ref = rest[ni:ni + 4]
        dv_ref = rest[ni + 4 + no]
        pair, i = pl.program_id(0), pl.program_id(1)
        at_end = riding.hooks(rest[:ni], rest[ni + 4:ni + 4 + no], *rest[ni + 5 + no:],
                              first=(pair == 0) & (i == 0), middle=(pair == npair // 2) & (i == 0),
                              last=(pair == npair - 1) & (i == nq - 1))

        @pl.when(i == 0)
        def _():
            dk_ref[...] = jnp.zeros_like(dk_ref)
            dv_ref[...] = jnp.zeros_like(dv_ref)
            dc_ref[...] = jnp.zeros_like(dc_ref)

        q2, do2, lse2 = q_ref[...], do_ref[...], lse_ref[...]
        do2b = do2.astype(BF16)
        prod = do2b.astype(F32) * o_ref[...]
        heads = []
        for hh in (0, 1):
            hm = _head_mask(hh)
            heads.append((jnp.where(hm, q2, jnp.zeros_like(q2)), jnp.where(hm, do2b, jnp.zeros_like(do2b)),
                          jnp.sum(jnp.where(hm, prod, 0.0), axis=1, keepdims=True),
                          jnp.max(jnp.where(hm, lse2, NEG), axis=1, keepdims=True)))

        def step(off, width, dqs, diag):
            off = pl.multiple_of(off, ATT)
            kj, vj = k_ref[pl.ds(off, width), :], v_ref[pl.ds(off, width), :]
            dk, dv, out = None, None, []
            for hh in (0, 1):
                qm, dom, delta, lse_h = heads[hh]
                s = _dot(qm, kj, NT) - c_ref[hh:hh + 1, pl.ds(off, width)]
                p = jnp.exp(s - lse_h)
                if diag:
                    row = i * ATT + lax.broadcasted_iota(jnp.int32, (ATT, width), 0)
                    col = off + lax.broadcasted_iota(jnp.int32, (ATT, width), 1)
                    p = jnp.where(col <= row, p, 0.0)
                ds = p * (_dot(dom, vj, NT) - delta)
                dc_ref[hh:hh + 1, pl.ds(off, width)] += -jnp.sum(ds, axis=0, keepdims=True)
                dsb = ds.astype(BF16)
                dk_h, dv_h = _dot(dsb, qm, TN), _dot(p.astype(BF16), dom, TN)
                dk, dv = (dk_h, dv_h) if dk is None else (dk + dk_h, dv + dv_h)
                out.append(dqs[hh] + _dot(dsb, kj))
            dk_ref[pl.ds(off, width), :] += dk
            dv_ref[pl.ds(off, width), :] += dv
            return tuple(out)

        zero = jnp.zeros((ATT, LANES), F32)
        dqs = lax.fori_loop(0, i // 2, lambda j, acc: step(j * (2 * ATT), 2 * ATT, acc, False), (zero, zero))
        dqs = lax.cond(i % 2 == 1, lambda acc: step((i - 1) * ATT, 2 * ATT, acc, True),
                       lambda acc: step(i * ATT, ATT, acc, True), dqs)
        dq_ref[...] = jnp.where(_head_mask(0), dqs[0], dqs[1]) * SCALE

        @pl.when(i == nq - 1)
        def _():
            dvb_ref[...] = dv_ref[...].astype(BF16)

        at_end()

    blk = pl.BlockSpec((ATT, LANES), lambda p, i: (i, p))
    full = pl.BlockSpec((S, LANES), lambda p, i: (0, p))
    cspec = pl.BlockSpec((None, 2, S), lambda p, i: (p, 0, 0))
    res = pl.pallas_call(
        body, name="fox_bwd", grid=(npair, nq),
        in_specs=[blk, full, full, cspec, blk, blk, blk] + riding.in_specs,
        out_specs=[blk, full, full, cspec] + riding.out_specs,
        out_shape=[jax.ShapeDtypeStruct((S, D), F32)] * 2 + [jax.ShapeDtypeStruct((S, D), BF16),
                                                              jax.ShapeDtypeStruct((npair, 2, S), F32)]
                  + riding.out_shape,
        scratch_shapes=[pltpu.VMEM((S, LANES), F32)] + riding.scratch,
        compiler_params=_params(2))(q, k, v, ct, o, lse, do, *riding.ins)
    return res[0], res[1], res[2], res[3], res[4:]


def _both_heads(x):
    return jnp.concatenate([jnp.where(_head_mask(hh), x, jnp.zeros_like(x)) for hh in (0, 1)], axis=0)


def _per_head(col0, col1):
    return jnp.concatenate([jnp.broadcast_to(col0, (WINDOW, 1)), jnp.broadcast_to(col1, (WINDOW, 1))], axis=0)


def _unstack(x2):
    return jnp.where(_head_mask(0), x2[:WINDOW], x2[WINDOW:])


def _swa_valid(i, start):
    r = lax.broadcasted_iota(jnp.int32, (2 * WINDOW, 2 * WINDOW), 0)
    qabs = i * WINDOW + jnp.where(r >= WINDOW, r - WINDOW, r)
    kabs = start + lax.broadcasted_iota(jnp.int32, (2 * WINDOW, 2 * WINDOW), 1)
    return (kabs <= qabs) & (qabs - kabs < WINDOW)


def _swa_fwd(q, kdup, vdup, sinks_t):
    def body(q_ref, k_ref, v_ref, sk_ref, o_ref, lse_ref):
        skv = sk_ref[...]
        first = _head_mask(0)
        for sb in range(SWQ):
            i = pl.program_id(1) * SWQ + sb
            rows = slice(sb * WINDOW, (sb + 1) * WINDOW)
            start = pl.multiple_of(jnp.maximum(i - 1, 0) * WINDOW, WINDOW)
            kk, vv = k_ref[pl.ds(start, 2 * WINDOW), :], v_ref[pl.ds(start, 2 * WINDOW), :]
            q2 = q_ref[rows, :]
            valid = _swa_valid(i, start)[:WINDOW]
            res = []
            for hh in (0, 1):
                hm = _head_mask(hh)
                sink = jnp.max(jnp.where(hm, skv, NEG), axis=1, keepdims=True)
                s = jnp.where(valid, _dot(jnp.where(hm, q2, jnp.zeros_like(q2)), kk, NT), NEG)
                m = jnp.maximum(jnp.max(s, axis=1, keepdims=True), sink)
                p = jnp.exp(s - m)
                l = jnp.sum(p, axis=1, keepdims=True) + jnp.exp(sink - m)
                res.append((_dot(p.astype(BF16), vv) / l, m + jnp.log(l)))
            o_ref[rows, :] = jnp.where(first, res[0][0], res[1][0])
            lse_ref[rows, :] = jnp.where(first, res[0][1], res[1][1])

    blk = pl.BlockSpec((SWQ * WINDOW, LANES), lambda p, i: (i, p))
    full = pl.BlockSpec((S, LANES), lambda p, i: (0, p // 2))
    return pl.pallas_call(
        body, name="swa_fwd", grid=(NH // 2, S // (SWQ * WINDOW)),
        in_specs=[blk, full, full, pl.BlockSpec((1, LANES), lambda p, i: (0, p))],
        out_specs=[blk, blk],
        out_shape=[jax.ShapeDtypeStruct((S, D), F32)] * 2,
        compiler_params=_params(2))(q, kdup, vdup, sinks_t)


def _swa_bwd(q, kdup, vdup, sinks_t, o, lse, do):
    def body(q_ref, k_ref, v_ref, sk_ref, o_ref, lse_ref, do_ref, dq_ref, dk_ref, dv_ref, dsk_ref):
        @pl.when(pl.program_id(1) == 0)
        def _():
            dk_ref[...] = jnp.zeros_like(dk_ref)
            dv_ref[...] = jnp.zeros_like(dv_ref)
            dsk_ref[...] = jnp.zeros_like(dsk_ref)

        skv = sk_ref[...]
        first = _head_mask(0)
        sink = _per_head(*[jnp.max(jnp.where(_head_mask(hh), skv, NEG), axis=1, keepdims=True) for hh in (0, 1)])
        for sb in range(SWQ):
            i = pl.program_id(1) * SWQ + sb
            rows = slice(sb * WINDOW, (sb + 1) * WINDOW)
            start = pl.multiple_of(jnp.maximum(i - 1, 0) * WINDOW, WINDOW)
            kk, vv = k_ref[pl.ds(start, 2 * WINDOW), :], v_ref[pl.ds(start, 2 * WINDOW), :]
            do2b = do_ref[rows, :].astype(BF16)
            prod, lse2 = do2b.astype(F32) * o_ref[rows, :], lse_ref[rows, :]
            qs, dos = _both_heads(q_ref[rows, :]), _both_heads(do2b)
            delta = jnp.concatenate([jnp.sum(jnp.where(_head_mask(hh), prod, 0.0), axis=1, keepdims=True)
                                     for hh in (0, 1)], axis=0)
            lse_h = jnp.concatenate([jnp.max(jnp.where(_head_mask(hh), lse2, NEG), axis=1, keepdims=True)
                                     for hh in (0, 1)], axis=0)
            p = jnp.where(_swa_valid(i, start), jnp.exp(_dot(qs, kk, NT) - lse_h), 0.0)
            dsb = (p * (_dot(dos, vv, NT) - delta)).astype(BF16)
            dk_ref[pl.ds(start, 2 * WINDOW), :] += _dot(dsb, qs, TN)
            dv_ref[pl.ds(start, 2 * WINDOW), :] += _dot(p.astype(BF16), dos, TN)
            dq_ref[rows, :] = _unstack(_dot(dsb, kk)) * SCALE
            t = jnp.exp(sink - lse_h) * delta
            dsk_ref[...] += -jnp.where(first, jnp.sum(t[:WINDOW], axis=0, keepdims=True),
                                       jnp.sum(t[WINDOW:], axis=0, keepdims=True))

    blk = pl.BlockSpec((SWQ * WINDOW, LANES), lambda p, i: (i, p))
    full = pl.BlockSpec((S, LANES), lambda p, i: (0, p // 2))
    acc = pl.BlockSpec((S, LANES), lambda p, i: (0, p))
    sk = pl.BlockSpec((1, LANES), lambda p, i: (0, p))
    return pl.pallas_call(
        body, name="swa_bwd", grid=(NH // 2, S // (SWQ * WINDOW)),
        in_specs=[blk, full, full, sk, blk, blk, blk],
        out_specs=[blk, acc, acc, sk],
        out_shape=[jax.ShapeDtypeStruct((S, D), F32)] * 3 + [jax.ShapeDtypeStruct((1, D), F32)],
        compiler_params=_params(2))(q, kdup, vdup, sinks_t, o, lse, do)


def _adamw_math(w, g, m, v):
    m = ADAM_B1 * m + (1.0 - ADAM_B1) * g
    v = ADAM_B2 * v + (1.0 - ADAM_B2) * jnp.square(g)
    m_hat = m / (1.0 - ADAM_B1 ** ADAM_STEP)
    v_hat = v / (1.0 - ADAM_B2 ** ADAM_STEP)
    delta = -ADAM_LR * (m_hat / (jnp.sqrt(v_hat) + ADAM_EPS) + ADAM_WD * w)
    return delta, m, v


def _adamw(w, g, m, v, name):
    r, c = w.shape
    tr = min(r, 128)

    def body(w_ref, g_ref, m_ref, v_ref, d_ref, mo_ref, vo_ref):
        d_ref[...], mo_ref[...], vo_ref[...] = _adamw_math(w_ref[...], g_ref[...], m_ref[...], v_ref[...])

    spec = pl.BlockSpec((tr, c), lambda i: (i, 0))
    return pl.pallas_call(
        body, name=name, grid=(r // tr,), in_specs=[spec] * 4, out_specs=[spec] * 3,
        out_shape=[jax.ShapeDtypeStruct((r, c), F32)] * 3, compiler_params=_params(1))(w, g, m, v)


SUM_TILE = 128


FLAT_BLOCK = 257 * 1024


def _tiles(shape, axis, lead=0):
    if len(shape) == 1:
        count = shape[0] // FLAT_BLOCK
        return (FLAT_BLOCK,), count, lambda pos, *lead_idx: (sum(k * count for k in lead_idx) + pos,)
    r, c = shape
    blk = (SUM_TILE, c) if axis == 0 else (r, SUM_TILE)
    count = shape[axis] // SUM_TILE

    def index(pos, *lead_idx):
        return tuple(lead_idx) + ((pos, 0) if axis == 0 else (0, pos))

    return (None,) * lead + blk, count, index


def _adamw_halves(w, g_mine, g_theirs, m, v, axis, name):
    blk, count, index = _tiles(w.shape, axis)
    per_half = count // 2

    def body(w_ref, a_ref, b_ref, m_ref, v_ref, g_ref, d_ref, mo_ref, vo_ref):
        is_mine = pl.program_id(0) // per_half == lax.axis_index("c")
        g = jnp.where(is_mine, a_ref[...], b_ref[...])
        g_ref[...] = g
        d_ref[...], mo_ref[...], vo_ref[...] = _adamw_math(w_ref[...], g, m_ref[...], v_ref[...])

    spec = pl.BlockSpec(blk, lambda i: index(i))
    half = pl.BlockSpec(blk, lambda i: index(i % per_half))
    return pl.pallas_call(
        body, name=name, grid=(count,), in_specs=[spec, half, half, spec, spec], out_specs=[spec] * 4,
        out_shape=[jax.ShapeDtypeStruct(w.shape, F32)] * 4, compiler_params=_params(1))(w, g_mine, g_theirs, m, v)


def _chip_sum(blocks, from_sibling, axis, name):
    flat = blocks.ndim == 1
    blk, count, index = _tiles((from_sibling.shape[0] // NCHIP,) if flat else from_sibling.shape[1:], axis, lead=1)

    def body(lo_ref, hi_ref, p_ref, o32, o16):
        mine = jnp.where(lax.axis_index("c") == 0, lo_ref[...], hi_ref[...])
        acc = mine + p_ref[...]
        o32[...] = acc
        o16[...] = acc.astype(BF16)

    half = pl.BlockSpec(blk, lambda k, i: index(i, k))
    if flat:
        lo = pl.BlockSpec(blk, lambda k, i: (2 * count * k + i,))
        hi = pl.BlockSpec(blk, lambda k, i: (2 * count * k + count + i,))
    else:
        lo, hi = half, pl.BlockSpec(blk, lambda k, i: index(i + count, k))
    return pl.pallas_call(
        body, name=name, grid=(NCHIP, count), in_specs=[lo, hi, half], out_specs=[half, half],
        out_shape=[jax.ShapeDtypeStruct(from_sibling.shape, F32), jax.ShapeDtypeStruct(from_sibling.shape, BF16)],
        compiler_params=_params(2))(blocks, blocks, from_sibling)


def _mesh_sum(own, parts, axis, name):
    blk, count, index = _tiles(own.shape, axis)
    n = NCHIP - 1

    def body(a_ref, *refs):
        acc = a_ref[...]
        for k in range(n):
            acc = acc + refs[k][...].astype(F32)
        refs[n][...] = acc

    spec = pl.BlockSpec(blk, lambda i: index(i))
    if own.ndim == 1:
        part = [pl.BlockSpec(blk, lambda i, k=k: (k * count + i,)) for k in range(n)]
    else:
        part = [pl.BlockSpec((None,) + blk, lambda i, k=k: (k,) + index(i)) for k in range(n)]
    return pl.pallas_call(
        body, name=name, grid=(count,), in_specs=[spec] + part,
        out_specs=spec, out_shape=jax.ShapeDtypeStruct(own.shape, F32),
        compiler_params=_params(1))(own, *([parts] * n))


def _sum_stack(parts, name):
    n = parts.shape[0]

    def body(p_ref, o_ref):
        acc = p_ref[0]
        for k in range(1, n):
            acc = acc + p_ref[k]
        o_ref[...] = acc

    return pl.pallas_call(body, name=name, out_shape=jax.ShapeDtypeStruct(parts.shape[1:], F32))(parts)


def _coords():
    return lax.axis_index("x"), lax.axis_index("y"), lax.axis_index("c")


def _chip(who):
    return 2 * who[0] + who[1]


def _flip(who, mask):
    return tuple((1 - v) if b else v for v, b in zip(who, mask))


def _transfer(transfers, t, I, O, ssem, rsem, receiving):
    tr, me = transfers[t], _coords()
    peer = _flip(me, tr["mask"])
    return pltpu.make_async_remote_copy(
        src_ref=tr["src"](I, O, me), dst_ref=tr["dst"](I, O, peer if receiving else me),
        send_sem=ssem.at[t], recv_sem=rsem.at[t], device_id=peer, device_id_type=MESH)


def _start_transfers(transfers, I, O, ssem, rsem, onward):
    arrived = set()
    for t, tr in enumerate(transfers):
        after = tr.get("after")
        if (after is not None) != onward:
            continue
        if after is not None and after not in arrived:
            _transfer(transfers, after, I, O, ssem, rsem, True).wait_recv()
            arrived.add(after)
        _transfer(transfers, t, I, O, ssem, rsem, False).start()


def _finish_transfers(transfers, I, O, ssem, rsem):
    passed_on = {tr["after"] for tr in transfers if tr.get("after") is not None}
    for t in range(len(transfers)):
        if t not in passed_on:
            _transfer(transfers, t, I, O, ssem, rsem, True).wait_recv()
    for t in range(len(transfers)):
        _transfer(transfers, t, I, O, ssem, rsem, False).wait_send()


def _exchange(name, ins, outs, transfers, copies=()):
    ni, no = len(ins), len(outs)
    nt = len(transfers)

    def body(*refs):
        I, O = refs[:ni], refs[ni:ni + no]
        ssem, rsem, lsem = refs[ni + no:]
        me = _coords()
        local = [pltpu.make_async_copy(s(I, O, me), d(I, O, me), lsem.at[n]) for n, (s, d) in enumerate(copies)]
        for cp in local:
            cp.start()
        _start_transfers(transfers, I, O, ssem, rsem, False)
        _start_transfers(transfers, I, O, ssem, rsem, True)
        _finish_transfers(transfers, I, O, ssem, rsem)
        for cp in local:
            cp.wait()

    hbm = pl.BlockSpec(memory_space=pltpu.HBM)
    return pl.pallas_call(
        body, name=name, in_specs=[hbm] * ni, out_specs=[hbm] * no,
        out_shape=[jax.ShapeDtypeStruct(s, d) for s, d in outs],
        scratch_shapes=[pltpu.SemaphoreType.DMA((nt,)), pltpu.SemaphoreType.DMA((nt,)),
                        pltpu.SemaphoreType.DMA((max(len(copies), 1),))],
        compiler_params=pltpu.CompilerParams(has_side_effects=True))(*ins)


CHIP_MASKS = [(0, 1, 0), (1, 0, 0), (1, 1, 0)]
SIBLING = (0, 0, 1)


def _half(shape2d, axis, which):
    n = shape2d[axis] // 2
    cut = pl.ds(pl.multiple_of(which * n, n), n)
    return (cut, slice(None)) if axis == 0 else (slice(None), cut)


class _Riding:
    def __init__(self, transfers, ins, outs):
        self.transfers, self.ins, self.outs = transfers, list(ins), list(outs)
        hbm = pl.BlockSpec(memory_space=pltpu.HBM)
        self.in_specs, self.out_specs = [hbm] * len(self.ins), [hbm] * len(self.outs)
        self.out_shape = [jax.ShapeDtypeStruct(s, d) for s, d in self.outs]
        self.scratch = [pltpu.SemaphoreType.DMA((max(len(transfers), 1),))] * 2

    def hooks(self, I, O, ssem, rsem, first, middle, last):
        tr = self.transfers

        @pl.when(first)
        def _():
            _start_transfers(tr, I, O, ssem, rsem, False)

        if any(t.get("after") is not None for t in tr):
            @pl.when(middle)
            def _():
                _start_transfers(tr, I, O, ssem, rsem, True)

        def at_end():
            @pl.when(last)
            def _():
                _finish_transfers(tr, I, O, ssem, rsem)

        return at_end


def _stretch(n, pos):
    return (pl.ds(pos * n if isinstance(pos, int) else pl.multiple_of(pos * n, n), n),)


def _select_own(shards, gathered):
    chip = _chip(_coords())
    out = []
    for s, t in zip(shards, gathered):
        if s.ndim == 1:
            out.append(lax.dynamic_update_slice(t, s, (chip * s.shape[0],)))
        else:
            out.append(lax.dynamic_update_slice(t, s[None], (chip, 0, 0)))
    return out


def _gather_plan(shards, axes):
    def half(a, who):
        if shards[a].ndim == 1:
            return _stretch(shards[a].shape[0] // 2, who[2])
        return _half(shards[a].shape, axes[a], who[2])

    def landed(a, chip, who):
        if shards[a].ndim == 1:
            return _stretch(shards[a].shape[0] // 2, 2 * chip + who[2])
        return (chip,) + half(a, who)

    over_ici, onward = [], []
    for a in range(len(shards)):
        for mask in CHIP_MASKS:
            over_ici.append(dict(
                mask=mask,
                src=lambda I, O, me, a=a: I[a].at[half(a, me)],
                dst=lambda I, O, who, a=a: O[a].at[landed(a, _chip(who), who)]))
            onward.append(dict(
                mask=SIBLING, after=len(over_ici) - 1,
                src=lambda I, O, me, a=a, mask=mask: O[a].at[landed(a, _chip(_flip(me, mask)), me)],
                dst=lambda I, O, who, a=a, mask=mask: O[a].at[landed(a, _chip(_flip(who, mask)), who)]))
    outs = [((NCHIP * s.shape[0],) if s.ndim == 1 else (NCHIP,) + s.shape, s.dtype) for s in shards]
    return over_ici + onward, outs


def _gather_shards(shards, axes):
    transfers, outs = _gather_plan(shards, axes)
    return _select_own(shards, _exchange("gather_weights", shards, outs, transfers))


def _to_sibling(arrs, name):
    transfers = [dict(mask=SIBLING, src=lambda I, O, me, a=a: I[a], dst=lambda I, O, who, a=a: O[a])
                 for a in range(len(arrs))]
    return _exchange(name, arrs, [(t.shape, t.dtype) for t in arrs], transfers)


def _halves_to_sibling(blocks, axes, name):
    def cut(a, which):
        return (slice(None),) + _half(blocks[a].shape[1:], axes[a], which)

    transfers, outs = [], []
    for a, (b, ax) in enumerate(zip(blocks, axes)):
        if b.ndim == 1:
            h = b.shape[0] // NCHIP // 2
            for k in range(NCHIP):
                transfers.append(dict(mask=SIBLING,
                                      src=lambda I, O, me, a=a, k=k, h=h: I[a].at[_stretch(h, 2 * k + 1 - me[2])],
                                      dst=lambda I, O, who, a=a, k=k, h=h: O[a].at[_stretch(h, k)]))
            outs.append(((NCHIP * h,), b.dtype))
        else:
            transfers.append(dict(mask=SIBLING, src=lambda I, O, me, a=a: I[a].at[cut(a, 1 - me[2])],
                                  dst=lambda I, O, who, a=a: O[a]))
            shape = list(b.shape)
            shape[ax + 1] //= 2
            outs.append((tuple(shape), b.dtype))
    return _exchange(name, blocks, outs, transfers)


def _scatter_plan(tb):
    def slot(a, k):
        return (k,) if tb[a].ndim == 3 else _stretch(tb[a].shape[0] // NCHIP, k)

    transfers = []
    for a in range(len(tb)):
        for n, mask in enumerate(CHIP_MASKS):
            transfers.append(dict(
                mask=mask,
                src=lambda I, O, me, a=a, mask=mask: I[a].at[slot(a, _chip(_flip(me, mask)))],
                dst=lambda I, O, who, a=a, n=n: O[a].at[slot(a, n)]))
    outs = [((3,) + t.shape[1:] if t.ndim == 3 else (3 * (t.shape[0] // NCHIP),), t.dtype) for t in tb]
    return transfers, outs


def _scatter_chip_sums(tb):
    transfers, outs = _scatter_plan(tb)
    return _exchange("scatter_grads", tb, outs, transfers)


def _gather_small(vec):
    def slot(who):
        return 4 * who[0] + 2 * who[1] + who[2]

    masks = [(m >> 2 & 1, m >> 1 & 1, m & 1) for m in range(1, 8)]
    transfers = [dict(mask=mask, src=lambda I, O, me: I[0], dst=lambda I, O, who: O[0].at[slot(who)])
                 for mask in masks]
    copies = [(lambda I, O, me: I[0], lambda I, O, me: O[0].at[slot(me)])]
    return _exchange("gather_small", [vec], [((8,) + vec.shape, vec.dtype)], transfers, copies)[0]


def _rope_tables(positions):
    half = ROT // 2
    inv_freq = jnp.power(jnp.float32(THETA), -jnp.arange(0, ROT, 2, dtype=F32) / ROT)
    ang = positions.astype(F32)[:, None] * inv_freq[None, :]
    cos, sin = jnp.cos(ang), jnp.sin(ang)
    one, zero, z8 = jnp.ones((S, HD - ROT), F32), jnp.zeros((S, HD - ROT), F32), jnp.zeros((S, half), F32)
    c = jnp.concatenate([cos, cos, one], axis=1)
    a = jnp.concatenate([-sin, z8, zero], axis=1)
    b = jnp.concatenate([z8, sin, zero], axis=1)
    return tuple(jnp.tile(t, (1, 2)) for t in (c, a, b))


def _tile_heads(g, w):
    return jnp.tile(g.reshape(1, HD), (1, w // HD))


def _fold_heads(dg):
    return dg.reshape(-1, HD).sum(axis=0)


def _pad_lanes(a):
    return jnp.pad(a, ((0, 0), (0, LANES - a.shape[1])))


def _local_step(x, target, positions, wt, fetch, late_weights, begin_reduce):
    rope = _rope_tables(positions)
    w1t = wt["w_in_a_t"]
    f_row = 3 * D // LANES
    wg_t = w1t[3 * D + NH:]
    in_b_block = lambda c: pl.BlockSpec((None, TN_, TN_), lambda j, i: (c, j, 0))
    b_pad = _pad_lanes(wt["b_forget"].reshape(1, NH))
    qg_a, kg_a = _tile_heads(wt["qnorm_a_g"], D), _tile_heads(wt["knorm_a_g"], D)
    qg_b, kg_b = _tile_heads(wt["qnorm_b_g"], D), _tile_heads(wt["knorm_b_g"], KVW)
    norm_a, kv_g, norm_b = wt["norm_a_g"].reshape(1, D), wt["kv_norm_g"].reshape(1, D), wt["norm_b_g"].reshape(1, D)
    sinks_t = jnp.repeat(wt["sinks"].reshape(1, NH), HD, axis=1)

    (u_a,) = _rmsnorm_fwd(x, [norm_a], "norm_a")
    qkv = _mm("proj_a", S, 3 * D, [(u_a, _a_rows(D), w1t, _b_rows(D), NT)])
    fpad = _mm("proj_f", S, LANES, [(u_a, _a_rows(D), w1t, _b_rows(D, row0=f_row, tn=LANES), NT)], tn=LANES)
    gate_a = _mm("proj_gate_a", S, D, [(u_a, _a_rows(D), wg_t, _b_rows(D), NT)])
    q_a, k_a, v_a = _a_post(qkv, qg_a, kg_a)
    ct = _forget_cumsum(fpad, b_pad)
    ct2 = ct[:NH].reshape(NH // 2, 2, S)
    o_a, lse_a, fetched = _fox_fwd(q_a, k_a, v_a, ct2, fetch)
    wt = {**wt, **late_weights(fetched)}
    w_in_b = wt["w_in_b"]
    y_a = _gate_fwd(o_a, gate_a, 0, "gate_a")
    h1 = _mm("out_a", S, D, [(y_a, _a_rows(D), wt["w_out_a"], _b_cols(D), None)], add=x)
    u_kv, u_b = _rmsnorm_fwd(h1, [kv_g, norm_b], "norm_b")
    kv = _mm("proj_kv", S, 2 * KVW, [(u_kv, _a_rows(D), wt["w_kv"], _b_cols(D), None)])
    pb = _mm("proj_b", S, 2 * D,
             [(u_b, _a_rows(D), w_in_b, pl.BlockSpec((None, D, TN_), lambda j, i: (j, 0, 0)), None)])
    q_b, kdup, vdup = _b_post(pb, kv, qg_b, kg_b, rope)
    o_b, lse_b = _swa_fwd(q_b, kdup, vdup, sinks_t)
    y_b = _gate_fwd(o_b, pb, 1, "gate_b")
    out = _mm("out_b", S, D, [(y_b, _a_rows(D), wt["w_out_b"], _b_cols(D), None)], add=h1)
    d_out, d_out_b, sq = _loss_head(out, target)

    g = {}
    g["w_out_b"] = _mm("dw_out_b", D, D, [(y_b, _a_cols(S), d_out_b, _b_cols(S), TN)])
    d_y_b = _mm("dy_b", S, D, [(d_out_b, _a_rows(D), wt["w_out_b"], _b_rows(D), NT)])
    d_o_b, d_gate_b = _gate_bwd(d_y_b, o_b, pb, 1, "gate_b_bwd")
    dq_b, dkdup, dvdup, dsk = _swa_bwd(q_b, kdup, vdup, sinks_t, o_b, lse_b, d_o_b)
    g["sinks"] = dsk[0, ::HD]
    d_qb_raw, dg = _headnorm_bwd(pb, 0, qg_b, dq_b, rope, "qnorm_b_bwd")
    g["qnorm_b_g"] = _fold_heads(dg)
    d_pb = [d_qb_raw, d_qb_raw, d_gate_b, d_gate_b]
    g["w_in_b"] = jnp.concatenate([
        _mm("dw_in_b_q", D, D, [(u_b, _a_cols(S), d_qb_raw, _b_cols(S), TN)], stacked=True),
        _mm("dw_in_b_gate", D, D, [(u_b, _a_cols(S), d_gate_b, _b_cols(S), TN)], stacked=True)], axis=0)
    d_u_b = _mm("du_b", S, D, [(d_pb[c], _a_rows(TN_, col=c % 2), w_in_b, in_b_block(c), NT) for c in range(NCHIP)])
    d_kv, dg = _kv_bwd(dkdup, dvdup, kv, kg_b, rope)
    g["knorm_b_g"] = _fold_heads(dg)
    g["w_kv"] = _mm("dw_kv", D, 2 * KVW, [(u_kv, _a_cols(S), d_kv, _b_cols(S), TN)])
    d_u_kv = _mm("du_kv", S, D, [(d_kv, _a_rows(2 * KVW), wt["w_kv"], _b_rows(2 * KVW), NT)])
    d_h1, d_h1_b, g["kv_norm_g"], g["norm_b_g"] = _rmsnorm_bwd(h1, [kv_g, norm_b], [d_u_kv, d_u_b], d_out, "norm_b_bwd")
    g["w_out_a"] = _mm("dw_out_a", D, D, [(y_a, _a_cols(S), d_h1_b, _b_cols(S), TN)])
    d_y_a = _mm("dy_a", S, D, [(d_h1_b, _a_rows(D), wt["w_out_a"], _b_rows(D), NT)])
    d_o_a, d_gate_a = _gate_bwd(d_y_a, o_a, gate_a, 0, "gate_a_bwd")
    riding, so_far = begin_reduce({n: g[n] for n in LATE})
    dq_a, dk_a, dv_a, dct, arrived = _fox_bwd(q_a, k_a, v_a, ct2, o_a, lse_a, d_o_a, riding)
    dct_pad = jnp.pad(dct.reshape(NH, S), ((0, LANES - NH), (0, 0)))
    d_f, db = _forget_bwd(dct_pad, fpad, b_pad)
    g["b_forget"] = db[0, :NH]
    d_q_raw, dg = _headnorm_bwd(qkv, 0, qg_a, dq_a, None, "qnorm_a_bwd")
    g["qnorm_a_g"] = _fold_heads(dg)
    d_k_raw, dg = _headnorm_bwd(qkv, 1, kg_a, dk_a, None, "knorm_a_bwd")
    g["knorm_a_g"] = _fold_heads(dg)
    pieces = [("q", d_q_raw), ("k", d_k_raw), ("v", dv_a), ("gate", d_gate_a)]
    dw = {n: _mm("dw_in_a_" + n, D, D, [(t, _a_cols(S), u_a, _b_cols(S), TN)]) for n, t in pieces}
    dw_f = _mm("dw_in_a_f", LANES, D, [(d_f, _a_cols(S, tm=LANES), u_a, _b_cols(S), TN)], tm=LANES)
    g["w_in_a"] = jnp.concatenate([dw["q"], dw["k"], dw["v"], dw_f[:NH], dw["gate"]], axis=0)
    d_u_a = _mm("du_a", S, D, [
        (d_q_raw, _a_rows(D), w1t, _b_cols(D, row=0), None), (d_k_raw, _a_rows(D), w1t, _b_cols(D, row=1), None),
        (dv_a, _a_rows(D), w1t, _b_cols(D, row=2), None), (d_gate_a, _a_rows(D), wg_t, _b_cols(D), None),
        (d_f, _a_rows(LANES), w1t, _b_cols(LANES, row=f_row), None)])
    d_x, _, g["norm_a_g"] = _rmsnorm_bwd(x, [norm_a], [d_u_a], d_h1, "norm_a_bwd")
    return sq, d_x, g, (so_far, arrived)


BIG = ["w_in_a", "w_out_a", "w_kv", "w_in_b", "w_out_b"]
LATE = BIG[1:]
SPLIT = {"w_in_a": None, "w_out_a": 0, "w_kv": 0, "w_in_b": 0, "w_out_b": 0}
SMALL = ["norm_a_g", "b_forget", "qnorm_a_g", "knorm_a_g", "kv_norm_g", "knorm_b_g", "norm_b_g", "qnorm_b_g", "sinks"]
NAMES = ["norm_a_g", "w_in_a", "b_forget", "qnorm_a_g", "knorm_a_g", "w_out_a", "kv_norm_g", "w_kv", "knorm_b_g",
         "norm_b_g", "w_in_b", "qnorm_b_g", "sinks", "w_out_b"]


def _pack(vals):
    flat = []
    for v in vals:
        v = v.reshape(-1)
        flat.append(jnp.pad(v, (0, -v.shape[0] % LANES)))
    flat = jnp.concatenate(flat)
    flat = jnp.pad(flat, (0, -flat.shape[0] % (8 * LANES)))
    return flat.reshape(-1, LANES)


def _unpack(packed, shapes):
    flat, out, off = packed.reshape(-1), [], 0
    for s in shapes:
        n = int(np.prod(s))
        out.append(flat[off:off + n].reshape(s))
        off += n + (-n % LANES)
    return out


def kernel(x, positions, norm_a_g, w_in_a, b_forget, qnorm_a_g, knorm_a_g, w_out_a, kv_norm_g, w_kv, knorm_b_g, norm_b_g, w_in_b, qnorm_b_g, sinks, w_out_b, loss_target, m_norm_a_g, m_w_in_a, m_b_forget, m_qnorm_a_g, m_knorm_a_g, m_w_out_a, m_kv_norm_g, m_w_kv, m_knorm_b_g, m_norm_b_g, m_w_in_b, m_qnorm_b_g, m_sinks, m_w_out_b, v_norm_a_g, v_w_in_a, v_b_forget, v_qnorm_a_g, v_knorm_a_g, v_w_out_a, v_kv_norm_g, v_w_kv, v_knorm_b_g, v_norm_b_g, v_w_in_b, v_qnorm_b_g, v_sinks, v_w_out_b):
    w = dict(norm_a_g=norm_a_g, w_in_a=w_in_a, b_forget=b_forget, qnorm_a_g=qnorm_a_g, knorm_a_g=knorm_a_g,
             w_out_a=w_out_a, kv_norm_g=kv_norm_g, w_kv=w_kv, knorm_b_g=knorm_b_g, norm_b_g=norm_b_g,
             w_in_b=w_in_b, qnorm_b_g=qnorm_b_g, sinks=sinks, w_out_b=w_out_b)
    m = dict(norm_a_g=m_norm_a_g, w_in_a=m_w_in_a, b_forget=m_b_forget, qnorm_a_g=m_qnorm_a_g, knorm_a_g=m_knorm_a_g,
             w_out_a=m_w_out_a, kv_norm_g=m_kv_norm_g, w_kv=m_w_kv, knorm_b_g=m_knorm_b_g, norm_b_g=m_norm_b_g,
             w_in_b=m_w_in_b, qnorm_b_g=m_qnorm_b_g, sinks=m_sinks, w_out_b=m_w_out_b)
    v = dict(norm_a_g=v_norm_a_g, w_in_a=v_w_in_a, b_forget=v_b_forget, qnorm_a_g=v_qnorm_a_g, knorm_a_g=v_knorm_a_g,
             w_out_a=v_w_out_a, kv_norm_g=v_kv_norm_g, w_kv=v_w_kv, knorm_b_g=v_knorm_b_g, norm_b_g=v_norm_b_g,
             w_in_b=v_w_in_b, qnorm_b_g=v_qnorm_b_g, sinks=v_sinks, w_out_b=v_w_out_b)
    my_chip = 2 * lax.axis_index("x") + lax.axis_index("y")

    def shard2d(t, n):
        if n == "w_in_a":
            return jnp.transpose(t, (2, 0, 1)).reshape(-1)
        return t.reshape(t.shape[-2:])

    def unflat(t, n):
        return jnp.transpose(t.reshape(-1, 1, D), (1, 2, 0)) if n == "w_in_a" else t.reshape(w[n].shape)

    w2d = {n: shard2d(w[n], n) for n in BIG}

    norm_a_rows = jnp.broadcast_to(norm_a_g.reshape(1, D // NCHIP), (16, D // NCHIP))
    w1t, norm_rows = _gather_shards([w2d["w_in_a"].astype(BF16), norm_a_rows], [SPLIT["w_in_a"], 0])
    wt = {"w_in_a_t": w1t.reshape(-1, D), "norm_a_g": norm_rows[:, 0, :].reshape(1, D)}
    for n in SMALL[1:]:
        wt[n] = w[n]
    late_shards = [w2d[n].astype(BF16) for n in LATE]
    late_axes = [SPLIT[n] for n in LATE]
    transfers, outs = _gather_plan(late_shards, late_axes)
    fetch = _Riding(transfers, late_shards, outs)

    def late_weights(fetched):
        whole = dict(zip(LATE, _select_own(late_shards, fetched)))
        return {n: t if n == "w_in_b" else t.reshape(-1, t.shape[2]) for n, t in whole.items()}

    def as_blocks(t):
        if t.ndim == 3:
            return t
        return t.reshape(-1) if t.shape[0] % (8 * NCHIP) else t.reshape(NCHIP, -1, t.shape[1])

    def chip_sums(names, grads, name):
        axes = [SPLIT[n] for n in names]
        blocks = [as_blocks(grads[n]) for n in names]
        sums = [_chip_sum(blk, part, ax, "chip_sum_" + n)
                for n, ax, blk, part in zip(names, axes, blocks, _halves_to_sibling(blocks, axes, name))]
        return [s[0] for s in sums], [s[1] for s in sums]

    def begin_reduce(grads):
        f32, bf16 = chip_sums(LATE, grads, "sibling_halves_late")
        transfers, outs = _scatter_plan(bf16)
        return _Riding(transfers, bf16, outs), f32

    sq, d_x, g, (late_f32, late_arrived) = _local_step(x[0], loss_target[0], positions, wt, fetch, late_weights,
                                                      begin_reduce)

    small_shapes = [(D,), (NH,), (HD,), (HD,), (D,), (HD,), (D,), (HD,), (NH,), (D,)]
    packed = _pack([g[n] for n in SMALL] + [sq])
    total = _sum_stack(_gather_small(packed), "sum_small")
    small_g = dict(zip(SMALL, _unpack(total, small_shapes)[:-1]))
    loss = 0.5 * jnp.sum(_unpack(total, small_shapes)[-1]) / D
    small_g["norm_a_g"] = lax.dynamic_slice(small_g["norm_a_g"], (my_chip * (D // NCHIP),), (D // NCHIP,))

    axes = [SPLIT[n] for n in BIG]
    first_f32, first_bf16 = chip_sums(["w_in_a"], g, "sibling_halves")
    chip_f32 = first_f32 + list(late_f32)
    arrived = list(_scatter_chip_sums(first_bf16)) + list(late_arrived)
    halves = []
    for n, ax, t32, parts in zip(BIG, axes, chip_f32, arrived):
        if t32.ndim == 1:
            own = lax.dynamic_slice_in_dim(t32, my_chip * (t32.shape[0] // NCHIP), t32.shape[0] // NCHIP)
        else:
            own = lax.dynamic_index_in_dim(t32, my_chip, axis=0, keepdims=False)
        halves.append(_mesh_sum(own, parts, ax, "mesh_sum_" + n))
    sibling_done = _to_sibling(halves, "finished_halves")

    res = {}
    for n, ax, mine_half, their_half in zip(BIG, axes, halves, sibling_done):
        out4 = _adamw_halves(w2d[n], mine_half, their_half, shard2d(m[n], n), shard2d(v[n], n), ax, "adamw_" + n)
        res[n] = tuple(unflat(t, n) for t in out4)
    sm_g = _pack([small_g[n] for n in SMALL])
    sm = [_pack([d[n] for n in SMALL]) for d in (w, m, v)]
    sm_out = _adamw(sm[0], sm_g, sm[1], sm[2], "adamw_small")
    sm_shapes = [w[n].shape for n in SMALL]
    unpacked = [_unpack(t, sm_shapes) for t in (sm_g,) + tuple(sm_out)]
    for i, n in enumerate(SMALL):
        res[n] = tuple(u[i] for u in unpacked)

    outs = [loss, d_x[None]]
    for k in range(4):
        outs += [res[n][k] for n in NAMES]
    return tuple(outs)
```

```python
import numpy as np
import jax
import jax.numpy as jnp
from jax import lax
from jax.experimental import pallas as pl
from jax.experimental.pallas import tpu as pltpu

F32, BF16 = jnp.float32, jnp.bfloat16
S, D, HD, NH, NKV = 2048, 1024, 64, 16, 4
KVW = NKV * HD
WINDOW = 128
ROT = HD // 4
THETA = 500000.0
EPS = 1e-6
SCALE = HD ** -0.5
LANES = 128
NEG = -1e30
VMEM_LIMIT = 48 * 2 ** 20
ROWS = 256
ATT = 256
SWQ = 4
NCHIP = 4
ADAM_LR, ADAM_B1, ADAM_B2, ADAM_EPS, ADAM_WD, ADAM_STEP = 0.001, 0.9, 0.999, 1e-08, 0.01, 10
NT = (((1,), (1,)), ((), ()))
TN = (((0,), (0,)), ((), ()))
MESH = pl.DeviceIdType.MESH


def _params(n):
    return pltpu.CompilerParams(dimension_semantics=("arbitrary",) * n, vmem_limit_bytes=VMEM_LIMIT)


def _dot(a, b, dims=None):
    if dims is None:
        return jnp.dot(a, b, preferred_element_type=F32)
    return lax.dot_general(a, b, dims, preferred_element_type=F32)


def _dot_split(a, b, n):
    out, rest = None, a
    for _ in range(n):
        hi = rest.astype(BF16)
        term = _dot(hi, b)
        out = term if out is None else out + term
        rest = rest - hi.astype(F32)
    return out


def _seg_mat(w):
    e = (np.arange(w)[:, None] // HD == np.arange(LANES)[None, :]).astype(np.float32)
    return jnp.asarray(e, BF16)


def _spread(r, w):
    head = lax.broadcasted_iota(jnp.int32, (LANES, w), 1) >> 6
    et = jnp.where(head == lax.broadcasted_iota(jnp.int32, (LANES, w), 0), 1.0, 0.0).astype(BF16)
    return _dot_split(r, et, 3)


def _head_rstd(x, e):
    ss = _dot_split(x * x, e, 2)
    return _spread(lax.rsqrt(ss * (1.0 / HD) + EPS), x.shape[1])


def _rope(x, c, a, b):
    w = x.shape[1]
    return x * c + pltpu.roll(x, w - ROT // 2, 1) * a + pltpu.roll(x, ROT // 2, 1) * b


def _rope_t(dy, c, a, b):
    w = dy.shape[1]
    return dy * c + pltpu.roll(dy * b, w - ROT // 2, 1) + pltpu.roll(dy * a, ROT // 2, 1)


def _sigmoid(x):
    return 1.0 / (1.0 + jnp.exp(-x))


def _row_spec(shape, ts):
    nd = len(shape)
    if shape[0] == S:
        return pl.BlockSpec((ts,) + tuple(shape[1:]), lambda i: (i,) + (0,) * (nd - 1))
    return pl.BlockSpec(tuple(shape), lambda i: (0,) * nd)


def _rows_call(body, name, ins, outs, ts=ROWS):
    return pl.pallas_call(
        body, name=name, grid=(S // ts,),
        in_specs=[_row_spec(a.shape, ts) for a in ins],
        out_specs=[_row_spec(s, ts) for s, _ in outs],
        out_shape=[jax.ShapeDtypeStruct(s, d) for s, d in outs],
        compiler_params=_params(1))(*ins)


def _col_spec(ts, w, col):
    return pl.BlockSpec((ts, w), lambda i: (i, col))


TM = TN_ = 512
TM_TOKENS = 1024


def _mm(name, m, n, terms, out_dtype=F32, add=None, tm=None, tn=TN_, stacked=False):
    nterm = len(terms)
    if tm is None:
        tm = TM_TOKENS if m == S else TM

    def body(*refs):
        acc = None
        for t in range(nterm):
            part = _dot(refs[2 * t][...], refs[2 * t + 1][...], terms[t][4])
            acc = part if acc is None else acc + part
        if add is not None:
            acc = acc + refs[2 * nterm][...]
        refs[-1][...] = acc.astype(out_dtype)

    tile = pl.BlockSpec((tm, tn), lambda j, i: (i, j))
    ins, specs = [], []
    for a, a_spec, b, b_spec, _ in terms:
        ins += [a, b]
        specs += [a_spec, b_spec]
    if add is not None:
        ins.append(add)
        specs.append(tile)
    return pl.pallas_call(
        body, name=name, grid=(n // tn, m // tm), in_specs=specs,
        out_specs=pl.BlockSpec((None, tm, tn), lambda j, i: (j, i, 0)) if stacked else tile,
        out_shape=jax.ShapeDtypeStruct((n // tn, m, tn) if stacked else (m, n), out_dtype),
        compiler_params=_params(2))(*ins)


def _a_rows(k, col=0, tm=TM_TOKENS):
    return pl.BlockSpec((tm, k), lambda j, i: (i, col))


def _a_cols(k, tm=TM):
    return pl.BlockSpec((k, tm), lambda j, i: (0, i))


def _b_cols(k, row=0, col0=0, tn=TN_):
    return pl.BlockSpec((k, tn), lambda j, i: (row, col0 + j))


def _b_rows(k, row0=0, tn=TN_):
    return pl.BlockSpec((tn, k), lambda j, i: (row0 + j, 0))


def _rmsnorm_fwd(x, gains, name):
    def body(*refs):
        xv = refs[0][...]
        r = lax.rsqrt(jnp.mean(xv * xv, axis=-1, keepdims=True) + EPS)
        xh = xv * r
        for n in range(len(gains)):
            refs[1 + len(gains) + n][...] = (xh * refs[1 + n][...]).astype(BF16)

    return _rows_call(body, name, [x] + list(gains), [((S, D), BF16)] * len(gains))


def _rmsnorm_bwd(x, gains, dus, dres, name):
    n = len(gains)

    def body(*refs):
        x_ref, g_refs, du_refs, dres_ref = refs[0], refs[1:1 + n], refs[1 + n:1 + 2 * n], refs[1 + 2 * n]
        dx_ref, dxb_ref, dg_refs = refs[2 + 2 * n], refs[3 + 2 * n], refs[4 + 2 * n:]
        xv = x_ref[...]
        r = lax.rsqrt(jnp.mean(xv * xv, axis=-1, keepdims=True) + EPS)
        xh = xv * r
        gy = None
        for m in range(n):
            du = du_refs[m][...]
            part = jnp.sum(du * xh, axis=0, keepdims=True)

            @pl.when(pl.program_id(0) == 0)
            def _(m=m, part=part):
                dg_refs[m][...] = part

            @pl.when(pl.program_id(0) != 0)
            def _(m=m, part=part):
                dg_refs[m][...] += part

            t = du * g_refs[m][...]
            gy = t if gy is None else gy + t
        dx = dres_ref[...] + r * (gy - xh * jnp.mean(gy * xh, axis=-1, keepdims=True))
        dx_ref[...] = dx
        dxb_ref[...] = dx.astype(BF16)

    outs = [((S, D), F32), ((S, D), BF16)] + [((1, D), F32)] * n
    return _rows_call(body, name, [x] + list(gains) + list(dus) + [dres], outs)


def _a_post(qkvg, qg, kg):
    e = _seg_mat(D)

    def body(q_ref, k_ref, v_ref, qg_ref, kg_ref, e_ref, qo, ko, vo):
        ev = e_ref[...]
        qv, kv = q_ref[...], k_ref[...]
        qo[...] = (qv * _head_rstd(qv, ev) * qg_ref[...] * SCALE).astype(BF16)
        ko[...] = (kv * _head_rstd(kv, ev) * kg_ref[...]).astype(BF16)
        vo[...] = v_ref[...].astype(BF16)

    whole = lambda a: pl.BlockSpec(a.shape, lambda i: (0, 0))
    return pl.pallas_call(
        body, name="a_post", grid=(S // ROWS,),
        in_specs=[_col_spec(ROWS, D, 0), _col_spec(ROWS, D, 1), _col_spec(ROWS, D, 2),
                  whole(qg), whole(kg), whole(e)],
        out_specs=[_col_spec(ROWS, D, 0)] * 3,
        out_shape=[jax.ShapeDtypeStruct((S, D), BF16)] * 3,
        compiler_params=_params(1))(qkvg, qkvg, qkvg, qg, kg, e)


def _tri(upper):
    r, c = np.arange(ROWS)[:, None], np.arange(ROWS)[None, :]
    return jnp.asarray((r <= c) if upper else (r >= c), BF16)


def _forget_cumsum(fpad, bpad):
    def body(f_ref, b_ref, u_ref, c_ref, carry):
        @pl.when(pl.program_id(0) == 0)
        def _():
            carry[...] = jnp.zeros_like(carry)

        lf = jax.nn.log_sigmoid(f_ref[...] + b_ref[...])
        blk = _dot_split(lf.T, u_ref[...], 3) + carry[:, 0:1]
        c_ref[...] = blk
        carry[...] = jnp.broadcast_to(blk[:, ROWS - 1:ROWS], carry.shape)

    return pl.pallas_call(
        body, name="forget_cumsum", grid=(S // ROWS,),
        in_specs=[pl.BlockSpec((ROWS, LANES), lambda i: (i, 0)), pl.BlockSpec((1, LANES), lambda i: (0, 0)),
                  pl.BlockSpec((ROWS, ROWS), lambda i: (0, 0))],
        out_specs=pl.BlockSpec((LANES, ROWS), lambda i: (0, i)),
        out_shape=jax.ShapeDtypeStruct((LANES, S), F32),
        scratch_shapes=[pltpu.VMEM((LANES, LANES), F32)],
        compiler_params=_params(1))(fpad, bpad, _tri(True))


def _forget_bwd(dct, fpad, bpad):
    nb = S // ROWS

    def body(dc_ref, f_ref, b_ref, l_ref, df_ref, db_ref, carry):
        @pl.when(pl.program_id(0) == 0)
        def _():
            carry[...] = jnp.zeros_like(carry)
            db_ref[...] = jnp.zeros_like(db_ref)

        blk = _dot_split(dc_ref[...], l_ref[...], 3) + carry[:, 0:1]
        carry[...] = jnp.broadcast_to(blk[:, 0:1], carry.shape)
        df = blk.T * _sigmoid(-(f_ref[...] + b_ref[...]))
        df_ref[...] = df.astype(BF16)
        db_ref[...] += jnp.sum(df, axis=0, keepdims=True)

    return pl.pallas_call(
        body, name="forget_bwd", grid=(nb,),
        in_specs=[pl.BlockSpec((LANES, ROWS), lambda i: (0, nb - 1 - i)),
                  pl.BlockSpec((ROWS, LANES), lambda i: (nb - 1 - i, 0)),
                  pl.BlockSpec((1, LANES), lambda i: (0, 0)), pl.BlockSpec((ROWS, ROWS), lambda i: (0, 0))],
        out_specs=[pl.BlockSpec((ROWS, LANES), lambda i: (nb - 1 - i, 0)), pl.BlockSpec((1, LANES), lambda i: (0, 0))],
        out_shape=[jax.ShapeDtypeStruct((S, LANES), BF16), jax.ShapeDtypeStruct((1, LANES), F32)],
        scratch_shapes=[pltpu.VMEM((LANES, LANES), F32)],
        compiler_params=_params(1))(dct, fpad, bpad, _tri(False))


def _gate_fwd(o, proj, col, name):
    def body(o_ref, g_ref, y_ref):
        g = g_ref[...]
        y_ref[...] = (o_ref[...] * (g * _sigmoid(g))).astype(BF16)

    return pl.pallas_call(
        body, name=name, grid=(S // ROWS,),
        in_specs=[_col_spec(ROWS, D, 0), _col_spec(ROWS, D, col)],
        out_specs=_col_spec(ROWS, D, 0), out_shape=jax.ShapeDtypeStruct((S, D), BF16),
        compiler_params=_params(1))(o, proj)


def _gate_bwd(dy, o, proj, col, name):
    def body(dy_ref, o_ref, g_ref, do_ref, dg_ref):
        g, dyv = g_ref[...], dy_ref[...]
        sg = _sigmoid(g)
        do_ref[...] = dyv * (g * sg)
        dg_ref[...] = (dyv * o_ref[...] * (sg * (1.0 + g * (1.0 - sg)))).astype(BF16)

    return pl.pallas_call(
        body, name=name, grid=(S // ROWS,),
        in_specs=[_col_spec(ROWS, D, 0), _col_spec(ROWS, D, 0), _col_spec(ROWS, D, col)],
        out_specs=[_col_spec(ROWS, D, 0)] * 2,
        out_shape=[jax.ShapeDtypeStruct((S, D), F32), jax.ShapeDtypeStruct((S, D), BF16)],
        compiler_params=_params(1))(dy, o, proj)


def _headnorm_bwd(x, col, gain, dy, rope, name):
    e = _seg_mat(D)
    tabs = list(rope) if rope is not None else []

    def body(*refs):
        x_ref, g_ref, dy_ref, e_ref = refs[:4]
        dx_ref, dg_ref = refs[-2:]
        xv, dyv, ev = x_ref[...], dy_ref[...], e_ref[...]
        if rope is not None:
            c, a, b = (jnp.tile(t[...], (1, D // LANES)) for t in refs[4:7])
            dyv = _rope_t(dyv, c, a, b)
        r = _head_rstd(xv, ev)
        xh = xv * r
        part = jnp.sum(dyv * xh, axis=0, keepdims=True)

        @pl.when(pl.program_id(0) == 0)
        def _():
            dg_ref[...] = part

        @pl.when(pl.program_id(0) != 0)
        def _():
            dg_ref[...] += part

        gy = dyv * g_ref[...]
        seg = _spread(_dot_split(gy * xh, ev, 2) * (1.0 / HD), D)
        dx_ref[...] = (r * (gy - xh * seg)).astype(BF16)

    whole = lambda a: pl.BlockSpec(a.shape, lambda i: (0, 0))
    return pl.pallas_call(
        body, name=name, grid=(S // ROWS,),
        in_specs=[_col_spec(ROWS, D, col), whole(gain), _col_spec(ROWS, D, 0), whole(e)]
                 + [pl.BlockSpec((ROWS, LANES), lambda i: (i, 0))] * len(tabs),
        out_specs=[_col_spec(ROWS, D, 0), whole(gain)],
        out_shape=[jax.ShapeDtypeStruct((S, D), BF16), jax.ShapeDtypeStruct((1, D), F32)],
        compiler_params=_params(1))(x, gain, dy, e, *tabs)


def _dup_mat():
    r, c = np.arange(KVW)[:, None], np.arange(2 * KVW)[None, :]
    return (r // HD == c // LANES) & (r % HD == c % HD)


def _fold_mat():
    r, c = np.arange(D)[:, None], np.arange(KVW)[None, :]
    return (r // (2 * LANES) == c // HD) & (r % HD == c % HD)


def _b_post(pb, kv, qg, kg, rope):
    e, ek = _seg_mat(D), _seg_mat(KVW)
    dup = jnp.asarray(_dup_mat(), BF16)

    def body(q_ref, k_ref, v_ref, qg_ref, kg_ref, e_ref, ek_ref, dup_ref, c_ref, a_ref, b_ref, qo, ko, vo):
        c1, a1, b1 = c_ref[...], a_ref[...], b_ref[...]
        qv = q_ref[...]
        qn = qv * _head_rstd(qv, e_ref[...]) * qg_ref[...]
        t = lambda z, n: jnp.tile(z, (1, n))
        qo[...] = (_rope(qn, t(c1, D // LANES), t(a1, D // LANES), t(b1, D // LANES)) * SCALE).astype(BF16)
        kvv = k_ref[...]
        kn = kvv * _head_rstd(kvv, ek_ref[...]) * kg_ref[...]
        kr = _rope(kn, t(c1, KVW // LANES), t(a1, KVW // LANES), t(b1, KVW // LANES)).astype(BF16)
        ko[...] = _dot(kr, dup_ref[...]).astype(BF16)
        vo[...] = _dot(v_ref[...].astype(BF16), dup_ref[...]).astype(BF16)

    whole = lambda a: pl.BlockSpec(a.shape, lambda i: (0, 0))
    tab = pl.BlockSpec((ROWS, LANES), lambda i: (i, 0))
    return pl.pallas_call(
        body, name="b_post", grid=(S // ROWS,),
        in_specs=[_col_spec(ROWS, D, 0), _col_spec(ROWS, KVW, 0), _col_spec(ROWS, KVW, 1),
                  whole(qg), whole(kg), whole(e), whole(ek), whole(dup), tab, tab, tab],
        out_specs=[_col_spec(ROWS, D, 0), _col_spec(ROWS, 2 * KVW, 0), _col_spec(ROWS, 2 * KVW, 0)],
        out_shape=[jax.ShapeDtypeStruct((S, D), BF16), jax.ShapeDtypeStruct((S, 2 * KVW), BF16),
                   jax.ShapeDtypeStruct((S, 2 * KVW), BF16)],
        compiler_params=_params(1))(pb, kv, kv, qg, kg, e, ek, dup, *rope)


def _kv_bwd(dkdup, dvdup, kv, kg, rope):
    ek = _seg_mat(KVW)
    fold = jnp.asarray(_fold_mat(), BF16)

    def body(dk_ref, dv_ref, k_ref, kg_ref, ek_ref, fold_ref, c_ref, a_ref, b_ref, dkv_ref, dg_ref):
        ev, fv = ek_ref[...], fold_ref[...]
        t = lambda z: jnp.tile(z[...], (1, KVW // LANES))
        dk = _rope_t(_dot_split(dk_ref[...], fv, 3), t(c_ref), t(a_ref), t(b_ref))
        dv = _dot_split(dv_ref[...], fv, 3)
        xv = k_ref[...]
        r = _head_rstd(xv, ev)
        xh = xv * r
        part = jnp.sum(dk * xh, axis=0, keepdims=True)

        @pl.when(pl.program_id(0) == 0)
        def _():
            dg_ref[...] = part

        @pl.when(pl.program_id(0) != 0)
        def _():
            dg_ref[...] += part

        gy = dk * kg_ref[...]
        seg = _spread(_dot_split(gy * xh, ev, 2) * (1.0 / HD), KVW)
        dkv_ref[:, 0:KVW] = (r * (gy - xh * seg)).astype(BF16)
        dkv_ref[:, KVW:2 * KVW] = dv.astype(BF16)

    whole = lambda a: pl.BlockSpec(a.shape, lambda i: (0, 0))
    tab = pl.BlockSpec((ROWS, LANES), lambda i: (i, 0))
    return pl.pallas_call(
        body, name="kv_bwd", grid=(S // ROWS,),
        in_specs=[_col_spec(ROWS, D, 0), _col_spec(ROWS, D, 0), _col_spec(ROWS, KVW, 0),
                  whole(kg), whole(ek), whole(fold), tab, tab, tab],
        out_specs=[_col_spec(ROWS, 2 * KVW, 0), whole(kg)],
        out_shape=[jax.ShapeDtypeStruct((S, 2 * KVW), BF16), jax.ShapeDtypeStruct((1, KVW), F32)],
        compiler_params=_params(1))(dkdup, dvdup, kv, kg, ek, fold, *rope)


def _loss_head(out, target):
    def body(o_ref, t_ref, d_ref, db_ref, l_ref):
        diff = o_ref[...] - t_ref[...]
        d = diff * (1.0 / D)
        d_ref[...] = d
        db_ref[...] = d.astype(BF16)

        @pl.when(pl.program_id(0) == 0)
        def _():
            l_ref[...] = jnp.zeros_like(l_ref)

        l_ref[...] += jnp.sum(diff * diff, axis=0, keepdims=True)

    return _rows_call(body, "loss_head", [out, target], [((S, D), F32), ((S, D), BF16), ((1, D), F32)])


def _lane():
    return lax.broadcasted_iota(jnp.int32, (1, LANES), 1)


def _head_mask(hh):
    return (_lane() < HD) if hh == 0 else (_lane() >= HD)


def _fox_fwd(q, k, v, ct, riding):
    nq, npair = S // ATT, NH // 2
    ni, no = len(riding.ins), len(riding.outs)

    def body(q_ref, k_ref, v_ref, c_ref, *rest):
        o_ref, lse_ref = rest[ni:ni + 2]
        pair, i = pl.program_id(0), pl.program_id(1)
        at_end = riding.hooks(rest[:ni], rest[ni + 2:ni + 2 + no], *rest[ni + 2 + no:],
                              first=(pair == 0) & (i == 0), middle=(pair == npair // 2) & (i == 0),
                              last=(pair == npair - 1) & (i == nq - 1))
        q2 = q_ref[...]
        qms = [jnp.where(_head_mask(hh), q2, jnp.zeros_like(q2)) for hh in (0, 1)]

        def probs(off, width, m, hh, diag):
            s = _dot(qms[hh], k_ref[pl.ds(off, width), :], NT) - c_ref[hh:hh + 1, pl.ds(off, width)]
            if diag:
                row = i * ATT + lax.broadcasted_iota(jnp.int32, (ATT, width), 0)
                col = off + lax.broadcasted_iota(jnp.int32, (ATT, width), 1)
                s = jnp.where(col <= row, s, NEG)
            m_new = jnp.maximum(m, jnp.max(s, axis=1, keepdims=True))
            p = jnp.exp(s - m_new)
            p_hi = p.astype(BF16)
            return m_new, jnp.exp(m - m_new), p_hi, (p - p_hi.astype(F32)).astype(BF16)

        def weighted(off, width, p_hi, p_lo, hh):
            vj = v_ref[pl.ds(off, width), :]
            v1 = jnp.where(_head_mask(hh), vj, jnp.ones_like(vj))
            return _dot(p_hi, v1) + _dot(p_lo, v1)

        def step(off, width, carry, diag):
            off = pl.multiple_of(off, ATT)
            out = []
            for hh in (0, 1):
                m, acc = carry[hh]
                m, alpha, p_hi, p_lo = probs(off, width, m, hh, diag)
                out.append((m, alpha * acc + weighted(off, width, p_hi, p_lo, hh)))
            return tuple(out)

        one = (jnp.full((ATT, 1), NEG, F32), jnp.zeros((ATT, LANES), F32))
        carry = lax.fori_loop(0, i // 2, lambda j, cr: step(j * (2 * ATT), 2 * ATT, cr, False), (one, one))
        carry = lax.cond(i % 2 == 1, lambda cr: step((i - 1) * ATT, 2 * ATT, cr, True),
                         lambda cr: step(i * ATT, ATT, cr, True), carry)
        res = []
        for hh in (0, 1):
            m, acc = carry[hh]
            l = jnp.max(jnp.where(_head_mask(1 - hh), acc, 0.0), axis=1, keepdims=True)
            res.append((acc / l, m + jnp.log(l)))
        first = _head_mask(0)
        o_ref[...] = jnp.where(first, res[0][0], res[1][0])
        lse_ref[...] = jnp.where(first, res[0][1], res[1][1])
        at_end()

    blk = pl.BlockSpec((ATT, LANES), lambda p, i: (i, p))
    full = pl.BlockSpec((S, LANES), lambda p, i: (0, p))
    res = pl.pallas_call(
        body, name="fox_fwd", grid=(npair, nq),
        in_specs=[blk, full, full, pl.BlockSpec((None, 2, S), lambda p, i: (p, 0, 0))] + riding.in_specs,
        out_specs=[blk, blk] + riding.out_specs,
        out_shape=[jax.ShapeDtypeStruct((S, D), F32)] * 2 + riding.out_shape,
        scratch_shapes=riding.scratch,
        compiler_params=_params(2))(q, k, v, ct, *riding.ins)
    return res[0], res[1], res[2:]


def _fox_bwd(q, k, v, ct, o, lse, do, riding):
    nq, npair = S // ATT, NH // 2
    ni, no = len(riding.ins), len(riding.outs)

    def body(q_ref, k_ref, v_ref, c_ref, o_ref, lse_ref, do_ref, *rest):
        dq_ref, dk_ref, dvb_ref, dc_ref = rest[ni:ni + 4]
        dv_ref = rest[ni + 4 + no]
        pair, i = pl.program_id(0), pl.program_id(1)
        at_end = riding.hooks(rest[:ni], rest[ni + 4:ni + 4 + no], *rest[ni + 5 + no:],
                              first=(pair == 0) & (i == 0), middle=(pair == npair // 2) & (i == 0),
                              last=(pair == npair - 1) & (i == nq - 1))

        @pl.when(i == 0)
        def _():
            dk_ref[...] = jnp.zeros_like(dk_ref)
            dv_ref[...] = jnp.zeros_like(dv_ref)
            dc_ref[...] = jnp.zeros_like(dc_ref)

        q2, do2, lse2 = q_ref[...], do_ref[...], lse_ref[...]
        do2b = do2.astype(BF16)
        prod = do2b.astype(F32) * o_ref[...]
        heads = []
        for hh in (0, 1):
            hm = _head_mask(hh)
            heads.append((jnp.where(hm, q2, jnp.zeros_like(q2)), jnp.where(hm, do2b, jnp.zeros_like(do2b)),
                          jnp.sum(jnp.where(hm, prod, 0.0), axis=1, keepdims=True),
                          jnp.max(jnp.where(hm, lse2, NEG), axis=1, keepdims=True)))

        def step(off, width, dqs, diag):
            off = pl.multiple_of(off, ATT)
            kj, vj = k_ref[pl.ds(off, width), :], v_ref[pl.ds(off, width), :]
            dk, dv, out = None, None, []
            for hh in (0, 1):
                qm, dom, delta, lse_h = heads[hh]
                s = _dot(qm, kj, NT) - c_ref[hh:hh + 1, pl.ds(off, width)]
                p = jnp.exp(s - lse_h)
                if diag:
                    row = i * ATT + lax.broadcasted_iota(jnp.int32, (ATT, width), 0)
                    col = off + lax.broadcasted_iota(jnp.int32, (ATT, width), 1)
                    p = jnp.where(col <= row, p, 0.0)
                ds = p * (_dot(dom, vj, NT) - delta)
                dc_ref[hh:hh + 1, pl.ds(off, width)] += -jnp.sum(ds, axis=0, keepdims=True)
                dsb = ds.astype(BF16)
                dk_h, dv_h = _dot(dsb, qm, TN), _dot(p.astype(BF16), dom, TN)
                dk, dv = (dk_h, dv_h) if dk is None else (dk + dk_h, dv + dv_h)
                out.append(dqs[hh] + _dot(dsb, kj))
            dk_ref[pl.ds(off, width), :] += dk
            dv_ref[pl.ds(off, width), :] += dv
            return tuple(out)

        zero = jnp.zeros((ATT, LANES), F32)
        dqs = lax.fori_loop(0, i // 2, lambda j, acc: step(j * (2 * ATT), 2 * ATT, acc, False), (zero, zero))
        dqs = lax.cond(i % 2 == 1, lambda acc: step((i - 1) * ATT, 2 * ATT, acc, True),
                       lambda acc: step(i * ATT, ATT, acc, True), dqs)
        dq_ref[...] = jnp.where(_head_mask(0), dqs[0], dqs[1]) * SCALE

        @pl.when(i == nq - 1)
        def _():
            dvb_ref[...] = dv_ref[...].astype(BF16)

        at_end()

    blk = pl.BlockSpec((ATT, LANES), lambda p, i: (i, p))
    full = pl.BlockSpec((S, LANES), lambda p, i: (0, p))
    cspec = pl.BlockSpec((None, 2, S), lambda p, i: (p, 0, 0))
    res = pl.pallas_call(
        body, name="fox_bwd", grid=(npair, nq),
        in_specs=[blk, full, full, cspec, blk, blk, blk] + riding.in_specs,
        out_specs=[blk, full, full, cspec] + riding.out_specs,
        out_shape=[jax.ShapeDtypeStruct((S, D), F32)] * 2 + [jax.ShapeDtypeStruct((S, D), BF16),
                                                              jax.ShapeDtypeStruct((npair, 2, S), F32)]
                  + riding.out_shape,
        scratch_shapes=[pltpu.VMEM((S, LANES), F32)] + riding.scratch,
        compiler_params=_params(2))(q, k, v, ct, o, lse, do, *riding.ins)
    return res[0], res[1], res[2], res[3], res[4:]


def _both_heads(x):
    return jnp.concatenate([jnp.where(_head_mask(hh), x, jnp.zeros_like(x)) for hh in (0, 1)], axis=0)


def _per_head(col0, col1):
    return jnp.concatenate([jnp.broadcast_to(col0, (WINDOW, 1)), jnp.broadcast_to(col1, (WINDOW, 1))], axis=0)


def _unstack(x2):
    return jnp.where(_head_mask(0), x2[:WINDOW], x2[WINDOW:])


def _swa_valid(i, start):
    r = lax.broadcasted_iota(jnp.int32, (2 * WINDOW, 2 * WINDOW), 0)
    qabs = i * WINDOW + jnp.where(r >= WINDOW, r - WINDOW, r)
    kabs = start + lax.broadcasted_iota(jnp.int32, (2 * WINDOW, 2 * WINDOW), 1)
    return (kabs <= qabs) & (qabs - kabs < WINDOW)


def _swa_fwd(q, kdup, vdup, sinks_t):
    def body(q_ref, k_ref, v_ref, sk_ref, o_ref, lse_ref):
        skv = sk_ref[...]
        first = _head_mask(0)
        for sb in range(SWQ):
            i = pl.program_id(1) * SWQ + sb
            rows = slice(sb * WINDOW, (sb + 1) * WINDOW)
            start = pl.multiple_of(jnp.maximum(i - 1, 0) * WINDOW, WINDOW)
            kk, vv = k_ref[pl.ds(start, 2 * WINDOW), :], v_ref[pl.ds(start, 2 * WINDOW), :]
            q2 = q_ref[rows, :]
            valid = _swa_valid(i, start)[:WINDOW]
            res = []
            for hh in (0, 1):
                hm = _head_mask(hh)
                sink = jnp.max(jnp.where(hm, skv, NEG), axis=1, keepdims=True)
                s = jnp.where(valid, _dot(jnp.where(hm, q2, jnp.zeros_like(q2)), kk, NT), NEG)
                m = jnp.maximum(jnp.max(s, axis=1, keepdims=True), sink)
                p = jnp.exp(s - m)
                l = jnp.sum(p, axis=1, keepdims=True) + jnp.exp(sink - m)
                res.append((_dot(p.astype(BF16), vv) / l, m + jnp.log(l)))
            o_ref[rows, :] = jnp.where(first, res[0][0], res[1][0])
            lse_ref[rows, :] = jnp.where(first, res[0][1], res[1][1])

    blk = pl.BlockSpec((SWQ * WINDOW, LANES), lambda p, i: (i, p))
    full = pl.BlockSpec((S, LANES), lambda p, i: (0, p // 2))
    return pl.pallas_call(
        body, name="swa_fwd", grid=(NH // 2, S // (SWQ * WINDOW)),
        in_specs=[blk, full, full, pl.BlockSpec((1, LANES), lambda p, i: (0, p))],
        out_specs=[blk, blk],
        out_shape=[jax.ShapeDtypeStruct((S, D), F32)] * 2,
        compiler_params=_params(2))(q, kdup, vdup, sinks_t)


def _swa_bwd(q, kdup, vdup, sinks_t, o, lse, do):
    def body(q_ref, k_ref, v_ref, sk_ref, o_ref, lse_ref, do_ref, dq_ref, dk_ref, dv_ref, dsk_ref):
        @pl.when(pl.program_id(1) == 0)
        def _():
            dk_ref[...] = jnp.zeros_like(dk_ref)
            dv_ref[...] = jnp.zeros_like(dv_ref)
            dsk_ref[...] = jnp.zeros_like(dsk_ref)

        skv = sk_ref[...]
        first = _head_mask(0)
        sink = _per_head(*[jnp.max(jnp.where(_head_mask(hh), skv, NEG), axis=1, keepdims=True) for hh in (0, 1)])
        for sb in range(SWQ):
            i = pl.program_id(1) * SWQ + sb
            rows = slice(sb * WINDOW, (sb + 1) * WINDOW)
            start = pl.multiple_of(jnp.maximum(i - 1, 0) * WINDOW, WINDOW)
            kk, vv = k_ref[pl.ds(start, 2 * WINDOW), :], v_ref[pl.ds(start, 2 * WINDOW), :]
            do2b = do_ref[rows, :].astype(BF16)
            prod, lse2 = do2b.astype(F32) * o_ref[rows, :], lse_ref[rows, :]
            qs, dos = _both_heads(q_ref[rows, :]), _both_heads(do2b)
            delta = jnp.concatenate([jnp.sum(jnp.where(_head_mask(hh), prod, 0.0), axis=1, keepdims=True)
                                     for hh in (0, 1)], axis=0)
            lse_h = jnp.concatenate([jnp.max(jnp.where(_head_mask(hh), lse2, NEG), axis=1, keepdims=True)
                                     for hh in (0, 1)], axis=0)
            p = jnp.where(_swa_valid(i, start), jnp.exp(_dot(qs, kk, NT) - lse_h), 0.0)
            dsb = (p * (_dot(dos, vv, NT) - delta)).astype(BF16)
            dk_ref[pl.ds(start, 2 * WINDOW), :] += _dot(dsb, qs, TN)
            dv_ref[pl.ds(start, 2 * WINDOW), :] += _dot(p.astype(BF16), dos, TN)
            dq_ref[rows, :] = _unstack(_dot(dsb, kk)) * SCALE
            t = jnp.exp(sink - lse_h) * delta
            dsk_ref[...] += -jnp.where(first, jnp.sum(t[:WINDOW], axis=0, keepdims=True),
                                       jnp.sum(t[WINDOW:], axis=0, keepdims=True))

    blk = pl.BlockSpec((SWQ * WINDOW, LANES), lambda p, i: (i, p))
    full = pl.BlockSpec((S, LANES), lambda p, i: (0, p // 2))
    acc = pl.BlockSpec((S, LANES), lambda p, i: (0, p))
    sk = pl.BlockSpec((1, LANES), lambda p, i: (0, p))
    return pl.pallas_call(
        body, name="swa_bwd", grid=(NH // 2, S // (SWQ * WINDOW)),
        in_specs=[blk, full, full, sk, blk, blk, blk],
        out_specs=[blk, acc, acc, sk],
        out_shape=[jax.ShapeDtypeStruct((S, D), F32)] * 3 + [jax.ShapeDtypeStruct((1, D), F32)],
        compiler_params=_params(2))(q, kdup, vdup, sinks_t, o, lse, do)


def _adamw_math(w, g, m, v):
    m = ADAM_B1 * m + (1.0 - ADAM_B1) * g
    v = ADAM_B2 * v + (1.0 - ADAM_B2) * jnp.square(g)
    m_hat = m / (1.0 - ADAM_B1 ** ADAM_STEP)
    v_hat = v / (1.0 - ADAM_B2 ** ADAM_STEP)
    delta = -ADAM_LR * (m_hat / (jnp.sqrt(v_hat) + ADAM_EPS) + ADAM_WD * w)
    return delta, m, v


def _adamw(w, g, m, v, name):
    r, c = w.shape
    tr = min(r, 128)

    def body(w_ref, g_ref, m_ref, v_ref, d_ref, mo_ref, vo_ref):
        d_ref[...], mo_ref[...], vo_ref[...] = _adamw_math(w_ref[...], g_ref[...], m_ref[...], v_ref[...])

    spec = pl.BlockSpec((tr, c), lambda i: (i, 0))
    return pl.pallas_call(
        body, name=name, grid=(r // tr,), in_specs=[spec] * 4, out_specs=[spec] * 3,
        out_shape=[jax.ShapeDtypeStruct((r, c), F32)] * 3, compiler_params=_params(1))(w, g, m, v)


SUM_TILE = 128


FLAT_BLOCK = 257 * 1024


def _tiles(shape, axis, lead=0):
    if len(shape) == 1:
        count = shape[0] // FLAT_BLOCK
        return (FLAT_BLOCK,), count, lambda pos, *lead_idx: (sum(k * count for k in lead_idx) + pos,)
    r, c = shape
    blk = (SUM_TILE, c) if axis == 0 else (r, SUM_TILE)
    count = shape[axis] // SUM_TILE

    def index(pos, *lead_idx):
        return tuple(lead_idx) + ((pos, 0) if axis == 0 else (0, pos))

    return (None,) * lead + blk, count, index


def _adamw_halves(w, g_mine, g_theirs, m, v, axis, name):
    blk, count, index = _tiles(w.shape, axis)
    per_half = count // 2

    def body(w_ref, a_ref, b_ref, m_ref, v_ref, g_ref, d_ref, mo_ref, vo_ref):
        is_mine = pl.program_id(0) // per_half == lax.axis_index("c")
        g = jnp.where(is_mine, a_ref[...], b_ref[...])
        g_ref[...] = g
        d_ref[...], mo_ref[...], vo_ref[...] = _adamw_math(w_ref[...], g, m_ref[...], v_ref[...])

    spec = pl.BlockSpec(blk, lambda i: index(i))
    half = pl.BlockSpec(blk, lambda i: index(i % per_half))
    return pl.pallas_call(
        body, name=name, grid=(count,), in_specs=[spec, half, half, spec, spec], out_specs=[spec] * 4,
        out_shape=[jax.ShapeDtypeStruct(w.shape, F32)] * 4, compiler_params=_params(1))(w, g_mine, g_theirs, m, v)


def _chip_sum(blocks, from_sibling, axis, name):
    flat = blocks.ndim == 1
    blk, count, index = _tiles((from_sibling.shape[0] // NCHIP,) if flat else from_sibling.shape[1:], axis, lead=1)

    def body(lo_ref, hi_ref, p_ref, o32, o16):
        mine = jnp.where(lax.axis_index("c") == 0, lo_ref[...], hi_ref[...])
        acc = mine + p_ref[...]
        o32[...] = acc
        o16[...] = acc.astype(BF16)

    half = pl.BlockSpec(blk, lambda k, i: index(i, k))
    if flat:
        lo = pl.BlockSpec(blk, lambda k, i: (2 * count * k + i,))
        hi = pl.BlockSpec(blk, lambda k, i: (2 * count * k + count + i,))
    else:
        lo, hi = half, pl.BlockSpec(blk, lambda k, i: index(i + count, k))
    return pl.pallas_call(
        body, name=name, grid=(NCHIP, count), in_specs=[lo, hi, half], out_specs=[half, half],
        out_shape=[jax.ShapeDtypeStruct(from_sibling.shape, F32), jax.ShapeDtypeStruct(from_sibling.shape, BF16)],
        compiler_params=_params(2))(blocks, blocks, from_sibling)


def _mesh_sum(own, parts, axis, name):
    blk, count, index = _tiles(own.shape, axis)
    n = NCHIP - 1

    def body(a_ref, *refs):
        acc = a_ref[...]
        for k in range(n):
            acc = acc + refs[k][...].astype(F32)
        refs[n][...] = acc

    spec = pl.BlockSpec(blk, lambda i: index(i))
    if own.ndim == 1:
        part = [pl.BlockSpec(blk, lambda i, k=k: (k * count + i,)) for k in range(n)]
    else:
        part = [pl.BlockSpec((None,) + blk, lambda i, k=k: (k,) + index(i)) for k in range(n)]
    return pl.pallas_call(
        body, name=name, grid=(count,), in_specs=[spec] + part,
        out_specs=spec, out_shape=jax.ShapeDtypeStruct(own.shape, F32),
        compiler_params=_params(1))(own, *([parts] * n))


def _sum_stack(parts, name):
    n = parts.shape[0]

    def body(p_ref, o_ref):
        acc = p_ref[0]
        for k in range(1, n):
            acc = acc + p_ref[k]
        o_ref[...] = acc

    return pl.pallas_call(body, name=name, out_shape=jax.ShapeDtypeStruct(parts.shape[1:], F32))(parts)


def _coords():
    return lax.axis_index("x"), lax.axis_index("y"), lax.axis_index("c")


def _chip(who):
    return 2 * who[0] + who[1]


def _flip(who, mask):
    return tuple((1 - v) if b else v for v, b in zip(who, mask))


def _transfer(transfers, t, I, O, ssem, rsem, receiving):
    tr, me = transfers[t], _coords()
    peer = _flip(me, tr["mask"])
    return pltpu.make_async_remote_copy(
        src_ref=tr["src"](I, O, me), dst_ref=tr["dst"](I, O, peer if receiving else me),
        send_sem=ssem.at[t], recv_sem=rsem.at[t], device_id=peer, device_id_type=MESH)


def _start_transfers(transfers, I, O, ssem, rsem, onward):
    arrived = set()
    for t, tr in enumerate(transfers):
        after = tr.get("after")
        if (after is not None) != onward:
            continue
        if after is not None and after not in arrived:
            _transfer(transfers, after, I, O, ssem, rsem, True).wait_recv()
            arrived.add(after)
        _transfer(transfers, t, I, O, ssem, rsem, False).start()


def _finish_transfers(transfers, I, O, ssem, rsem):
    passed_on = {tr["after"] for tr in transfers if tr.get("after") is not None}
    for t in range(len(transfers)):
        if t not in passed_on:
            _transfer(transfers, t, I, O, ssem, rsem, True).wait_recv()
    for t in range(len(transfers)):
        _transfer(transfers, t, I, O, ssem, rsem, False).wait_send()


def _own_copies(own, I, O, stage, lsem, leg):
    for n, (src, dst) in enumerate(own):
        me = _coords()
        bring =pltpu.make_async_copy(src(I, O, me), stage[n], lsem.at[2 * n])
        put = pltpu.make_async_copy(stage[n], dst(I, O, me), lsem.at[2 * n + 1])
        if leg == 0:
            bring.start()
        elif leg == 1:
            bring.wait()
            put.start()
        else:
            put.wait()


def _own_scratch(own, ins):
    return [pltpu.VMEM(ins[n].shape, ins[n].dtype) for n in range(len(own))], pltpu.SemaphoreType.DMA((max(2 * len(own), 1),))


def _exchange(name, ins, outs, transfers, own=()):
    ni, no = len(ins), len(outs)
    nt = len(transfers)
    stages, stage_sems = _own_scratch(own, ins)

    def body(*refs):
        I, O = refs[:ni], refs[ni:ni + no]
        ssem, rsem, lsem = refs[ni + no:ni + no + 3]
        stage = refs[ni + no + 3:]
        _own_copies(own, I, O, stage, lsem, 0)
        _start_transfers(transfers, I, O, ssem, rsem, False)
        _own_copies(own, I, O, stage, lsem, 1)
        _start_transfers(transfers, I, O, ssem, rsem, True)
        _finish_transfers(transfers, I, O, ssem, rsem)
        _own_copies(own, I, O, stage, lsem, 2)

    hbm = pl.BlockSpec(memory_space=pltpu.HBM)
    return pl.pallas_call(
        body, name=name, in_specs=[hbm] * ni, out_specs=[hbm] * no,
        out_shape=[jax.ShapeDtypeStruct(s, d) for s, d in outs],
        scratch_shapes=[pltpu.SemaphoreType.DMA((nt,)), pltpu.SemaphoreType.DMA((nt,)), stage_sems] + stages,
        compiler_params=pltpu.CompilerParams(has_side_effects=True, vmem_limit_bytes=VMEM_LIMIT))(*ins)


CHIP_MASKS = [(0, 1, 0), (1, 0, 0), (1, 1, 0)]
SIBLING = (0, 0, 1)


def _half(shape2d, axis, which):
    n = shape2d[axis] // 2
    cut = pl.ds(pl.multiple_of(which * n, n), n)
    return (cut, slice(None)) if axis == 0 else (slice(None), cut)


class _Riding:
    def __init__(self, transfers, ins, outs, own=()):
        self.transfers, self.ins, self.outs, self.own = transfers, list(ins), list(outs), list(own)
        hbm = pl.BlockSpec(memory_space=pltpu.HBM)
        self.in_specs, self.out_specs = [hbm] * len(self.ins), [hbm] * len(self.outs)
        self.out_shape = [jax.ShapeDtypeStruct(s, d) for s, d in self.outs]
        stages, stage_sems = _own_scratch(self.own, self.ins)
        self.scratch = [pltpu.SemaphoreType.DMA((max(len(transfers), 1),))] * 2 + [stage_sems] + stages

    def hooks(self, I, O, ssem, rsem, lsem, *stage, first, middle, last):
        tr, own = self.transfers, self.own

        @pl.when(first)
        def _():
            _own_copies(own, I, O, stage, lsem, 0)
            _start_transfers(tr, I, O, ssem, rsem, False)

        if own or any(t.get("after") is not None for t in tr):
            @pl.when(middle)
            def _():
                _own_copies(own, I, O, stage, lsem, 1)
                _start_transfers(tr, I, O, ssem, rsem, True)

        def at_end():
            @pl.when(last)
            def _():
                _finish_transfers(tr, I, O, ssem, rsem)
                _own_copies(own, I, O, stage, lsem, 2)

        return at_end


def _stretch(n, pos):
    return (pl.ds(pos * n if isinstance(pos, int) else pl.multiple_of(pos * n, n), n),)


def _gather_plan(shards, axes):
    def half(a, who):
        if shards[a].ndim == 1:
            return _stretch(shards[a].shape[0] // 2, who[2])
        return _half(shards[a].shape, axes[a], who[2])

    def landed(a, chip, who):
        if shards[a].ndim == 1:
            return _stretch(shards[a].shape[0] // 2, 2 * chip + who[2])
        return (chip,) + half(a, who)

    over_ici, onward = [], []
    for a in range(len(shards)):
        for mask in CHIP_MASKS:
            over_ici.append(dict(
                mask=mask,
                src=lambda I, O, me, a=a: I[a].at[half(a, me)],
                dst=lambda I, O, who, a=a: O[a].at[landed(a, _chip(who), who)]))
            onward.append(dict(
                mask=SIBLING, after=len(over_ici) - 1,
                src=lambda I, O, me, a=a, mask=mask: O[a].at[landed(a, _chip(_flip(me, mask)), me)],
                dst=lambda I, O, who, a=a, mask=mask: O[a].at[landed(a, _chip(_flip(who, mask)), who)]))
    outs = [((NCHIP * s.shape[0],) if s.ndim == 1 else (NCHIP,) + s.shape, s.dtype) for s in shards]

    def whole(a, chip):
        return _stretch(shards[a].shape[0], chip) if shards[a].ndim == 1 else (chip,)

    own = [(lambda I, O, me, a=a: I[a], lambda I, O, me, a=a: O[a].at[whole(a, _chip(me))])
           for a in range(len(shards))]
    return over_ici + onward, outs, own


def _gather_shards(shards, axes):
    transfers, outs, own = _gather_plan(shards, axes)
    return _exchange("gather_weights", shards, outs, transfers, own)


def _to_sibling(arrs, name):
    transfers = [dict(mask=SIBLING, src=lambda I, O, me, a=a: I[a], dst=lambda I, O, who, a=a: O[a])
                 for a in range(len(arrs))]
    return _exchange(name, arrs, [(t.shape, t.dtype) for t in arrs], transfers)


def _halves_to_sibling(blocks, axes, name):
    def cut(a, which):
        return (slice(None),) + _half(blocks[a].shape[1:], axes[a], which)

    transfers, outs = [], []
    for a, (b, ax) in enumerate(zip(blocks, axes)):
        if b.ndim == 1:
            h = b.shape[0] // NCHIP // 2
            for k in range(NCHIP):
                transfers.append(dict(mask=SIBLING,
                                      src=lambda I, O, me, a=a, k=k, h=h: I[a].at[_stretch(h, 2 * k + 1 - me[2])],
                                      dst=lambda I, O, who, a=a, k=k, h=h: O[a].at[_stretch(h, k)]))
            outs.append(((NCHIP * h,), b.dtype))
        else:
            transfers.append(dict(mask=SIBLING, src=lambda I, O, me, a=a: I[a].at[cut(a, 1 - me[2])],
                                  dst=lambda I, O, who, a=a: O[a]))
            shape = list(b.shape)
            shape[ax + 1] //= 2
            outs.append((tuple(shape), b.dtype))
    return _exchange(name, blocks, outs, transfers)


def _scatter_plan(tb):
    def slot(a, k):
        return (k,) if tb[a].ndim == 3 else _stretch(tb[a].shape[0] // NCHIP, k)

    transfers = []
    for a in range(len(tb)):
        for n, mask in enumerate(CHIP_MASKS):
            transfers.append(dict(
                mask=mask,
                src=lambda I, O, me, a=a, mask=mask: I[a].at[slot(a, _chip(_flip(me, mask)))],
                dst=lambda I, O, who, a=a, n=n: O[a].at[slot(a, n)]))
    outs = [((3,) + t.shape[1:] if t.ndim == 3 else (3 * (t.shape[0] // NCHIP),), t.dtype) for t in tb]
    return transfers, outs


def _scatter_chip_sums(tb):
    transfers, outs = _scatter_plan(tb)
    return _exchange("scatter_grads", tb, outs, transfers)


def _gather_small(vec):
    def slot(who):
        return 4 * who[0] + 2 * who[1] + who[2]

    masks = [(m >> 2 & 1, m >> 1 & 1, m & 1) for m in range(1, 8)]
    transfers = [dict(mask=mask, src=lambda I, O, me: I[0], dst=lambda I, O, who: O[0].at[slot(who)])
                 for mask in masks]
    own = [(lambda I, O, me: I[0], lambda I, O, me: O[0].at[slot(me)])]
    return _exchange("gather_small", [vec], [((8,) + vec.shape, vec.dtype)], transfers, own)[0]


def _rope_tables(positions):
    half = ROT // 2
    inv_freq = jnp.power(jnp.float32(THETA), -jnp.arange(0, ROT, 2, dtype=F32) / ROT)
    ang = positions.astype(F32)[:, None] * inv_freq[None, :]
    cos, sin = jnp.cos(ang), jnp.sin(ang)
    one, zero, z8 = jnp.ones((S, HD - ROT), F32), jnp.zeros((S, HD - ROT), F32), jnp.zeros((S, half), F32)
    c = jnp.concatenate([cos, cos, one], axis=1)
    a = jnp.concatenate([-sin, z8, zero], axis=1)
    b = jnp.concatenate([z8, sin, zero], axis=1)
    return tuple(jnp.tile(t, (1, 2)) for t in (c, a, b))


def _tile_heads(g, w):
    return jnp.tile(g.reshape(1, HD), (1, w // HD))


def _fold_heads(dg):
    return dg.reshape(-1, HD).sum(axis=0)


def _pad_lanes(a):
    return jnp.pad(a, ((0, 0), (0, LANES - a.shape[1])))


def _local_step(x, target, positions, wt, fetch, late_weights, begin_reduce):
    rope = _rope_tables(positions)
    w1t = wt["w_in_a_t"]
    f_row = 3 * D // LANES
    wg_t = w1t[3 * D + NH:]
    in_b_block = lambda c: pl.BlockSpec((None, TN_, TN_), lambda j, i: (c, j, 0))
    b_pad = _pad_lanes(wt["b_forget"].reshape(1, NH))
    qg_a, kg_a = _tile_heads(wt["qnorm_a_g"], D), _tile_heads(wt["knorm_a_g"], D)
    qg_b, kg_b = _tile_heads(wt["qnorm_b_g"], D), _tile_heads(wt["knorm_b_g"], KVW)
    norm_a, kv_g, norm_b = wt["norm_a_g"].reshape(1, D), wt["kv_norm_g"].reshape(1, D), wt["norm_b_g"].reshape(1, D)
    sinks_t = jnp.repeat(wt["sinks"].reshape(1, NH), HD, axis=1)

    (u_a,) = _rmsnorm_fwd(x, [norm_a], "norm_a")
    qkv = _mm("proj_a", S, 3 * D, [(u_a, _a_rows(D), w1t, _b_rows(D), NT)])
    fpad = _mm("proj_f", S, LANES, [(u_a, _a_rows(D), w1t, _b_rows(D, row0=f_row, tn=LANES), NT)], tn=LANES)
    gate_a = _mm("proj_gate_a", S, D, [(u_a, _a_rows(D), wg_t, _b_rows(D), NT)])
    q_a, k_a, v_a = _a_post(qkv, qg_a, kg_a)
    ct = _forget_cumsum(fpad, b_pad)
    ct2 = ct[:NH].reshape(NH // 2, 2, S)
    o_a, lse_a, fetched = _fox_fwd(q_a, k_a, v_a, ct2, fetch)
    wt = {**wt, **late_weights(fetched)}
    w_in_b = wt["w_in_b"]
    y_a = _gate_fwd(o_a, gate_a, 0, "gate_a")
    h1 = _mm("out_a", S, D, [(y_a, _a_rows(D), wt["w_out_a"], _b_cols(D), None)], add=x)
    u_kv, u_b = _rmsnorm_fwd(h1, [kv_g, norm_b], "norm_b")
    kv = _mm("proj_kv", S, 2 * KVW, [(u_kv, _a_rows(D), wt["w_kv"], _b_cols(D), None)])
    pb = _mm("proj_b", S, 2 * D,
             [(u_b, _a_rows(D), w_in_b, pl.BlockSpec((None, D, TN_), lambda j, i: (j, 0, 0)), None)])
    q_b, kdup, vdup = _b_post(pb, kv, qg_b, kg_b, rope)
    o_b, lse_b = _swa_fwd(q_b, kdup, vdup, sinks_t)
    y_b = _gate_fwd(o_b, pb, 1, "gate_b")
    out = _mm("out_b", S, D, [(y_b, _a_rows(D), wt["w_out_b"], _b_cols(D), None)], add=h1)
    d_out, d_out_b, sq = _loss_head(out, target)

    g = {}
    g["w_out_b"] = _mm("dw_out_b", D, D, [(y_b, _a_cols(S), d_out_b, _b_cols(S), TN)])
    d_y_b = _mm("dy_b", S, D, [(d_out_b, _a_rows(D), wt["w_out_b"], _b_rows(D), NT)])
    d_o_b, d_gate_b = _gate_bwd(d_y_b, o_b, pb, 1, "gate_b_bwd")
    dq_b, dkdup, dvdup, dsk = _swa_bwd(q_b, kdup, vdup, sinks_t, o_b, lse_b, d_o_b)
    g["sinks"] = dsk[0, ::HD]
    d_qb_raw, dg = _headnorm_bwd(pb, 0, qg_b, dq_b, rope, "qnorm_b_bwd")
    g["qnorm_b_g"] = _fold_heads(dg)
    d_pb = [d_qb_raw, d_qb_raw, d_gate_b, d_gate_b]
    g["w_in_b"] = jnp.concatenate([
        _mm("dw_in_b_q", D, D, [(u_b, _a_cols(S), d_qb_raw, _b_cols(S), TN)], stacked=True),
        _mm("dw_in_b_gate", D, D, [(u_b, _a_cols(S), d_gate_b, _b_cols(S), TN)], stacked=True)], axis=0)
    d_u_b = _mm("du_b", S, D, [(d_pb[c], _a_rows(TN_, col=c % 2), w_in_b, in_b_block(c), NT) for c in range(NCHIP)])
    d_kv, dg = _kv_bwd(dkdup, dvdup, kv, kg_b, rope)
    g["knorm_b_g"] = _fold_heads(dg)
    g["w_kv"] = _mm("dw_kv", D, 2 * KVW, [(u_kv, _a_cols(S), d_kv, _b_cols(S), TN)])
    d_u_kv = _mm("du_kv", S, D, [(d_kv, _a_rows(2 * KVW), wt["w_kv"], _b_rows(2 * KVW), NT)])
    d_h1, d_h1_b, g["kv_norm_g"], g["norm_b_g"] = _rmsnorm_bwd(h1, [kv_g, norm_b], [d_u_kv, d_u_b], d_out, "norm_b_bwd")
    g["w_out_a"] = _mm("dw_out_a", D, D, [(y_a, _a_cols(S), d_h1_b, _b_cols(S), TN)])
    d_y_a = _mm("dy_a", S, D, [(d_h1_b, _a_rows(D), wt["w_out_a"], _b_rows(D), NT)])
    d_o_a, d_gate_a = _gate_bwd(d_y_a, o_a, gate_a, 0, "gate_a_bwd")
    riding, so_far = begin_reduce({n: g[n] for n in LATE})
    dq_a, dk_a, dv_a, dct, arrived = _fox_bwd(q_a, k_a, v_a, ct2, o_a, lse_a, d_o_a, riding)
    dct_pad = jnp.pad(dct.reshape(NH, S), ((0, LANES - NH), (0, 0)))
    d_f, db = _forget_bwd(dct_pad, fpad, b_pad)
    g["b_forget"] = db[0, :NH]
    d_q_raw, dg = _headnorm_bwd(qkv, 0, qg_a, dq_a, None, "qnorm_a_bwd")
    g["qnorm_a_g"] = _fold_heads(dg)
    d_k_raw, dg = _headnorm_bwd(qkv, 1, kg_a, dk_a, None, "knorm_a_bwd")
    g["knorm_a_g"] = _fold_heads(dg)
    pieces = [("q", d_q_raw), ("k", d_k_raw), ("v", dv_a), ("gate", d_gate_a)]
    dw = {n: _mm("dw_in_a_" + n, D, D, [(t, _a_cols(S), u_a, _b_cols(S), TN)]) for n, t in pieces}
    dw_f = _mm("dw_in_a_f", LANES, D, [(d_f, _a_cols(S, tm=LANES), u_a, _b_cols(S), TN)], tm=LANES)
    g["w_in_a"] = jnp.concatenate([dw["q"], dw["k"], dw["v"], dw_f[:NH], dw["gate"]], axis=0)
    d_u_a = _mm("du_a", S, D, [
        (d_q_raw, _a_rows(D), w1t, _b_cols(D, row=0), None), (d_k_raw, _a_rows(D), w1t, _b_cols(D, row=1), None),
        (dv_a, _a_rows(D), w1t, _b_cols(D, row=2), None), (d_gate_a, _a_rows(D), wg_t, _b_cols(D), None),
        (d_f, _a_rows(LANES), w1t, _b_cols(LANES, row=f_row), None)])
    d_x, _, g["norm_a_g"] = _rmsnorm_bwd(x, [norm_a], [d_u_a], d_h1, "norm_a_bwd")
    return sq, d_x, g, (so_far, arrived)


BIG = ["w_in_a", "w_out_a", "w_kv", "w_in_b", "w_out_b"]
LATE = BIG[1:]
SPLIT = {"w_in_a": None, "w_out_a": 0, "w_kv": 0, "w_in_b": 0, "w_out_b": 0}
SMALL = ["norm_a_g", "b_forget", "qnorm_a_g", "knorm_a_g", "kv_norm_g", "knorm_b_g", "norm_b_g", "qnorm_b_g", "sinks"]
NAMES = ["norm_a_g", "w_in_a", "b_forget", "qnorm_a_g", "knorm_a_g", "w_out_a", "kv_norm_g", "w_kv", "knorm_b_g",
         "norm_b_g", "w_in_b", "qnorm_b_g", "sinks", "w_out_b"]


def _pack(vals):
    flat = []
    for v in vals:
        v = v.reshape(-1)
        flat.append(jnp.pad(v, (0, -v.shape[0] % LANES)))
    flat = jnp.concatenate(flat)
    flat = jnp.pad(flat, (0, -flat.shape[0] % (8 * LANES)))
    return flat.reshape(-1, LANES)


def _unpack(packed, shapes):
    flat, out, off = packed.reshape(-1), [], 0
    for s in shapes:
        n = int(np.prod(s))
        out.append(flat[off:off + n].reshape(s))
        off += n + (-n % LANES)
    return out


def kernel(x, positions, norm_a_g, w_in_a, b_forget, qnorm_a_g, knorm_a_g, w_out_a, kv_norm_g, w_kv, knorm_b_g, norm_b_g, w_in_b, qnorm_b_g, sinks, w_out_b, loss_target, m_norm_a_g, m_w_in_a, m_b_forget, m_qnorm_a_g, m_knorm_a_g, m_w_out_a, m_kv_norm_g, m_w_kv, m_knorm_b_g, m_norm_b_g, m_w_in_b, m_qnorm_b_g, m_sinks, m_w_out_b, v_norm_a_g, v_w_in_a, v_b_forget, v_qnorm_a_g, v_knorm_a_g, v_w_out_a, v_kv_norm_g, v_w_kv, v_knorm_b_g, v_norm_b_g, v_w_in_b, v_qnorm_b_g, v_sinks, v_w_out_b):
    w = dict(norm_a_g=norm_a_g, w_in_a=w_in_a, b_forget=b_forget, qnorm_a_g=qnorm_a_g, knorm_a_g=knorm_a_g,
             w_out_a=w_out_a, kv_norm_g=kv_norm_g, w_kv=w_kv, knorm_b_g=knorm_b_g, norm_b_g=norm_b_g,
             w_in_b=w_in_b, qnorm_b_g=qnorm_b_g, sinks=sinks, w_out_b=w_out_b)
    m = dict(norm_a_g=m_norm_a_g, w_in_a=m_w_in_a, b_forget=m_b_forget, qnorm_a_g=m_qnorm_a_g, knorm_a_g=m_knorm_a_g,
             w_out_a=m_w_out_a, kv_norm_g=m_kv_norm_g, w_kv=m_w_kv, knorm_b_g=m_knorm_b_g, norm_b_g=m_norm_b_g,
             w_in_b=m_w_in_b, qnorm_b_g=m_qnorm_b_g, sinks=m_sinks, w_out_b=m_w_out_b)
    v = dict(norm_a_g=v_norm_a_g, w_in_a=v_w_in_a, b_forget=v_b_forget, qnorm_a_g=v_qnorm_a_g, knorm_a_g=v_knorm_a_g,
             w_out_a=v_w_out_a, kv_norm_g=v_kv_norm_g, w_kv=v_w_kv, knorm_b_g=v_knorm_b_g, norm_b_g=v_norm_b_g,
             w_in_b=v_w_in_b, qnorm_b_g=v_qnorm_b_g, sinks=v_sinks, w_out_b=v_w_out_b)
    my_chip = 2 * lax.axis_index("x") + lax.axis_index("y")

    def shard2d(t, n):
        if n == "w_in_a":
            return jnp.transpose(t, (2, 0, 1)).reshape(-1)
        return t.reshape(t.shape[-2:])

    def unflat(t, n):
        return jnp.transpose(t.reshape(-1, 1, D), (1, 2, 0)) if n == "w_in_a" else t.reshape(w[n].shape)

    w2d = {n: shard2d(w[n], n) for n in BIG}

    norm_a_rows = jnp.broadcast_to(norm_a_g.reshape(1, D // NCHIP), (16, D // NCHIP))
    w1t, norm_rows = _gather_shards([w2d["w_in_a"].astype(BF16), norm_a_rows], [SPLIT["w_in_a"], 0])
    wt = {"w_in_a_t": w1t.reshape(-1, D), "norm_a_g": norm_rows[:, 0, :].reshape(1, D)}
    for n in SMALL[1:]:
        wt[n] = w[n]
    late_shards = [w2d[n].astype(BF16) for n in LATE]
    late_axes = [SPLIT[n] for n in LATE]
    transfers, outs, own = _gather_plan(late_shards, late_axes)
    fetch = _Riding(transfers, late_shards, outs, own)

    def late_weights(fetched):
        return {n: t if n == "w_in_b" else t.reshape(-1, t.shape[2]) for n, t in zip(LATE, fetched)}

    def as_blocks(t):
        if t.ndim == 3:
            return t
        return t.reshape(-1) if t.shape[0] % (8 * NCHIP) else t.reshape(NCHIP, -1, t.shape[1])

    def chip_sums(names, grads, name):
        axes = [SPLIT[n] for n in names]
        blocks = [as_blocks(grads[n]) for n in names]
        sums = [_chip_sum(blk, part, ax, "chip_sum_" + n)
                for n, ax, blk, part in zip(names, axes, blocks, _halves_to_sibling(blocks, axes, name))]
        return [s[0] for s in sums], [s[1] for s in sums]

    def begin_reduce(grads):
        f32, bf16 = chip_sums(LATE, grads, "sibling_halves_late")
        transfers, outs = _scatter_plan(bf16)
        return _Riding(transfers, bf16, outs), f32

    sq, d_x, g, (late_f32, late_arrived) = _local_step(x[0], loss_target[0], positions, wt, fetch, late_weights,
                                                      begin_reduce)

    small_shapes = [(D,), (NH,), (HD,), (HD,), (D,), (HD,), (D,), (HD,), (NH,), (D,)]
    packed = _pack([g[n] for n in SMALL] + [sq])
    total = _sum_stack(_gather_small(packed), "sum_small")
    small_g = dict(zip(SMALL, _unpack(total, small_shapes)[:-1]))
    loss = 0.5 * jnp.sum(_unpack(total, small_shapes)[-1]) / D
    small_g["norm_a_g"] = lax.dynamic_slice(small_g["norm_a_g"], (my_chip * (D // NCHIP),), (D // NCHIP,))

    axes = [SPLIT[n] for n in BIG]
    first_f32, first_bf16 = chip_sums(["w_in_a"], g, "sibling_halves")
    chip_f32 = first_f32 + list(late_f32)
    arrived = list(_scatter_chip_sums(first_bf16)) + list(late_arrived)
    halves = []
    for n, ax, t32, parts in zip(BIG, axes, chip_f32, arrived):
        if t32.ndim == 1:
            own = lax.dynamic_slice_in_dim(t32, my_chip * (t32.shape[0] // NCHIP), t32.shape[0] // NCHIP)
        else:
            own = lax.dynamic_index_in_dim(t32, my_chip, axis=0, keepdims=False)
        halves.append(_mesh_sum(own, parts, ax, "mesh_sum_" + n))
    sibling_done = _to_sibling(halves, "finished_halves")

    res = {}
    for n, ax, mine_half, their_half in zip(BIG, axes, halves, sibling_done):
        out4 = _adamw_halves(w2d[n], mine_half, their_half, shard2d(m[n], n), shard2d(v[n], n), ax, "adamw_" + n)
        res[n] = tuple(unflat(t, n) for t in out4)
    sm_g = _pack([small_g[n] for n in SMALL])
    sm = [_pack([d[n] for n in SMALL]) for d in (w, m, v)]
    sm_out = _adamw(sm[0], sm_g, sm[1], sm[2], "adamw_small")
    sm_shapes = [w[n].shape for n in SMALL]
    unpacked = [_unpack(t, sm_shapes) for t in (sm_g,) + tuple(sm_out)]
    for i, n in enumerate(SMALL):
        res[n] = tuple(u[i] for u in unpacked)

    outs = [loss, d_x[None]]
    for k in range(4):
        outs += [res[n][k] for n in NAMES]
    return tuple(outs)
```

```python
import numpy as np
import jax
import jax.numpy as jnp
from jax import lax
from jax.experimental import pallas as pl
from jax.experimental.pallas import tpu as pltpu

F32, BF16 = jnp.float32, jnp.bfloat16
S, D, HD, NH, NKV = 2048, 1024, 64, 16, 4
KVW = NKV * HD
WINDOW = 128
ROT = HD // 4
THETA = 500000.0
EPS = 1e-6
SCALE = HD ** -0.5
LANES = 128
NEG = -1e30
VMEM_LIMIT = 48 * 2 ** 20
ROWS = 256
ATT = 256
SWQ = 4
NCHIP = 4
ADAM_LR, ADAM_B1, ADAM_B2, ADAM_EPS, ADAM_WD, ADAM_STEP = 0.001, 0.9, 0.999, 1e-08, 0.01, 10
NT = (((1,), (1,)), ((), ()))
TN = (((0,), (0,)), ((), ()))
MESH = pl.DeviceIdType.MESH


def _params(n):
    return pltpu.CompilerParams(dimension_semantics=("arbitrary",) * n, vmem_limit_bytes=VMEM_LIMIT)


def _dot(a, b, dims=None):
    if dims is None:
        return jnp.dot(a, b, preferred_element_type=F32)
    return lax.dot_general(a, b, dims, preferred_element_type=F32)


def _dot_split(a, b, n):
    out, rest = None, a
    for _ in range(n):
        hi = rest.astype(BF16)
        term = _dot(hi, b)
        out = term if out is None else out + term
        rest = rest - hi.astype(F32)
    return out


def _seg_mat(w):
    e = (np.arange(w)[:, None] // HD == np.arange(LANES)[None, :]).astype(np.float32)
    return jnp.asarray(e, BF16)


def _spread(r, w):
    head = lax.broadcasted_iota(jnp.int32, (LANES, w), 1) >> 6
    et = jnp.where(head == lax.broadcasted_iota(jnp.int32, (LANES, w), 0), 1.0, 0.0).astype(BF16)
    return _dot_split(r, et, 3)


def _head_rstd(x, e):
    ss = _dot_split(x * x, e, 2)
    return _spread(lax.rsqrt(ss * (1.0 / HD) + EPS), x.shape[1])


def _rope(x, c, a, b):
    w = x.shape[1]
    return x * c + pltpu.roll(x, w - ROT // 2, 1) * a + pltpu.roll(x, ROT // 2, 1) * b


def _rope_t(dy, c, a, b):
    w = dy.shape[1]
    return dy * c + pltpu.roll(dy * b, w - ROT // 2, 1) + pltpu.roll(dy * a, ROT // 2, 1)


def _sigmoid(x):
    return 1.0 / (1.0 + jnp.exp(-x))


def _row_spec(shape, ts):
    nd = len(shape)
    if shape[0] == S:
        return pl.BlockSpec((ts,) + tuple(shape[1:]), lambda i: (i,) + (0,) * (nd - 1))
    return pl.BlockSpec(tuple(shape), lambda i: (0,) * nd)


def _rows_call(body, name, ins, outs, ts=ROWS):
    return pl.pallas_call(
        body, name=name, grid=(S // ts,),
        in_specs=[_row_spec(a.shape, ts) for a in ins],
        out_specs=[_row_spec(s, ts) for s, _ in outs],
        out_shape=[jax.ShapeDtypeStruct(s, d) for s, d in outs],
        compiler_params=_params(1))(*ins)


def _col_spec(ts, w, col):
    return pl.BlockSpec((ts, w), lambda i: (i, col))


TM = TN_ = 512
TM_TOKENS = 1024


def _mm(name, m, n, terms, out_dtype=F32, add=None, tm=None, tn=TN_, stacked=False, riding=None):
    nterm = len(terms)
    if tm is None:
        tm = TM_TOKENS if m == S else TM
    nj, ni_ = n // tn, m // tm
    n_in = 2 * nterm + (add is not None)
    r_in, r_out = (len(riding.ins), len(riding.outs)) if riding is not None else (0, 0)

    def body(*refs):
        if riding is not None:
            j, i = pl.program_id(0), pl.program_id(1)
            at_end = riding.hooks(refs[n_in:n_in + r_in], refs[n_in + r_in + 1:n_in + r_in + 1 + r_out],
                                  *refs[n_in + r_in + 1 + r_out:], first=(j == 0) & (i == 0),
                                  middle=(j == nj // 2) & (i == 0), last=(j == nj - 1) & (i == ni_ - 1))
        acc = None
        for t in range(nterm):
            part = _dot(refs[2 * t][...], refs[2 * t + 1][...], terms[t][4])
            acc = part if acc is None else acc + part
        if add is not None:
            acc = acc + refs[2 * nterm][...]
        refs[n_in + r_in][...] = acc.astype(out_dtype)
        if riding is not None:
            at_end()

    tile = pl.BlockSpec((tm, tn), lambda j, i: (i, j))
    ins, specs = [], []
    for a, a_spec, b, b_spec, _ in terms:
        ins += [a, b]
        specs += [a_spec, b_spec]
    if add is not None:
        ins.append(add)
        specs.append(tile)
    out_spec = pl.BlockSpec((None, tm, tn), lambda j, i: (j, i, 0)) if stacked else tile
    out_shape = jax.ShapeDtypeStruct((nj, m, tn) if stacked else (m, n), out_dtype)
    if riding is None:
        return pl.pallas_call(body, name=name, grid=(nj, ni_), in_specs=specs, out_specs=out_spec,
                              out_shape=out_shape, compiler_params=_params(2))(*ins)
    res = pl.pallas_call(
        body, name=name, grid=(nj, ni_), in_specs=specs + riding.in_specs,
        out_specs=[out_spec] + riding.out_specs, out_shape=[out_shape] + riding.out_shape,
        scratch_shapes=riding.scratch, compiler_params=_params(2))(*ins, *riding.ins)
    return res[0], res[1:]


def _a_rows(k, col=0, tm=TM_TOKENS):
    return pl.BlockSpec((tm, k), lambda j, i: (i, col))


def _a_cols(k, tm=TM):
    return pl.BlockSpec((k, tm), lambda j, i: (0, i))


def _b_cols(k, row=0, col0=0, tn=TN_):
    return pl.BlockSpec((k, tn), lambda j, i: (row, col0 + j))


def _b_rows(k, row0=0, tn=TN_):
    return pl.BlockSpec((tn, k), lambda j, i: (row0 + j, 0))


def _rmsnorm_fwd(x, gains, name):
    def body(*refs):
        xv = refs[0][...]
        r = lax.rsqrt(jnp.mean(xv * xv, axis=-1, keepdims=True) + EPS)
        xh = xv * r
        for n in range(len(gains)):
            refs[1 + len(gains) + n][...] = (xh * refs[1 + n][...]).astype(BF16)

    return _rows_call(body, name, [x] + list(gains), [((S, D), BF16)] * len(gains))


def _rmsnorm_bwd(x, gains, dus, dres, name):
    n = len(gains)

    def body(*refs):
        x_ref, g_refs, du_refs, dres_ref = refs[0], refs[1:1 + n], refs[1 + n:1 + 2 * n], refs[1 + 2 * n]
        dx_ref, dxb_ref, dg_refs = refs[2 + 2 * n], refs[3 + 2 * n], refs[4 + 2 * n:]
        xv = x_ref[...]
        r = lax.rsqrt(jnp.mean(xv * xv, axis=-1, keepdims=True) + EPS)
        xh = xv * r
        gy = None
        for m in range(n):
            du = du_refs[m][...]
            part = jnp.sum(du * xh, axis=0, keepdims=True)

            @pl.when(pl.program_id(0) == 0)
            def _(m=m, part=part):
                dg_refs[m][...] = part

            @pl.when(pl.program_id(0) != 0)
            def _(m=m, part=part):
                dg_refs[m][...] += part

            t = du * g_refs[m][...]
            gy = t if gy is None else gy + t
        dx = dres_ref[...] + r * (gy - xh * jnp.mean(gy * xh, axis=-1, keepdims=True))
        dx_ref[...] = dx
        dxb_ref[...] = dx.astype(BF16)

    outs = [((S, D), F32), ((S, D), BF16)] + [((1, D), F32)] * n
    return _rows_call(body, name, [x] + list(gains) + list(dus) + [dres], outs)


def _a_post(qkvg, qg, kg):
    e = _seg_mat(D)

    def body(q_ref, k_ref, v_ref, qg_ref, kg_ref, e_ref, qo, ko, vo):
        ev = e_ref[...]
        qv, kv = q_ref[...], k_ref[...]
        qo[...] = (qv * _head_rstd(qv, ev) * qg_ref[...] * SCALE).astype(BF16)
        ko[...] = (kv * _head_rstd(kv, ev) * kg_ref[...]).astype(BF16)
        vo[...] = v_ref[...].astype(BF16)

    whole = lambda a: pl.BlockSpec(a.shape, lambda i: (0, 0))
    return pl.pallas_call(
        body, name="a_post", grid=(S // ROWS,),
        in_specs=[_col_spec(ROWS, D, 0), _col_spec(ROWS, D, 1), _col_spec(ROWS, D, 2),
                  whole(qg), whole(kg), whole(e)],
        out_specs=[_col_spec(ROWS, D, 0)] * 3,
        out_shape=[jax.ShapeDtypeStruct((S, D), BF16)] * 3,
        compiler_params=_params(1))(qkvg, qkvg, qkvg, qg, kg, e)


def _tri(upper):
    r, c = np.arange(ROWS)[:, None], np.arange(ROWS)[None, :]
    return jnp.asarray((r <= c) if upper else (r >= c), BF16)


def _forget_cumsum(fpad, bpad):
    def body(f_ref, b_ref, u_ref, c_ref, carry):
        @pl.when(pl.program_id(0) == 0)
        def _():
            carry[...] = jnp.zeros_like(carry)

        lf = jax.nn.log_sigmoid(f_ref[...] + b_ref[...])
        blk = _dot_split(lf.T, u_ref[...], 3) + carry[:, 0:1]
        c_ref[...] = blk
        carry[...] = jnp.broadcast_to(blk[:, ROWS - 1:ROWS], carry.shape)

    return pl.pallas_call(
        body, name="forget_cumsum", grid=(S // ROWS,),
        in_specs=[pl.BlockSpec((ROWS, LANES), lambda i: (i, 0)), pl.BlockSpec((1, LANES), lambda i: (0, 0)),
                  pl.BlockSpec((ROWS, ROWS), lambda i: (0, 0))],
        out_specs=pl.BlockSpec((LANES, ROWS), lambda i: (0, i)),
        out_shape=jax.ShapeDtypeStruct((LANES, S), F32),
        scratch_shapes=[pltpu.VMEM((LANES, LANES), F32)],
        compiler_params=_params(1))(fpad, bpad, _tri(True))


def _forget_bwd(dct, fpad, bpad):
    nb = S // ROWS

    def body(dc_ref, f_ref, b_ref, l_ref, df_ref, db_ref, carry):
        @pl.when(pl.program_id(0) == 0)
        def _():
            carry[...] = jnp.zeros_like(carry)
            db_ref[...] = jnp.zeros_like(db_ref)

        blk = _dot_split(dc_ref[...], l_ref[...], 3) + carry[:, 0:1]
        carry[...] = jnp.broadcast_to(blk[:, 0:1], carry.shape)
        df = blk.T * _sigmoid(-(f_ref[...] + b_ref[...]))
        df_ref[...] = df.astype(BF16)
        db_ref[...] += jnp.sum(df, axis=0, keepdims=True)

    return pl.pallas_call(
        body, name="forget_bwd", grid=(nb,),
        in_specs=[pl.BlockSpec((LANES, ROWS), lambda i: (0, nb - 1 - i)),
                  pl.BlockSpec((ROWS, LANES), lambda i: (nb - 1 - i, 0)),
                  pl.BlockSpec((1, LANES), lambda i: (0, 0)), pl.BlockSpec((ROWS, ROWS), lambda i: (0, 0))],
        out_specs=[pl.BlockSpec((ROWS, LANES), lambda i: (nb - 1 - i, 0)), pl.BlockSpec((1, LANES), lambda i: (0, 0))],
        out_shape=[jax.ShapeDtypeStruct((S, LANES), BF16), jax.ShapeDtypeStruct((1, LANES), F32)],
        scratch_shapes=[pltpu.VMEM((LANES, LANES), F32)],
        compiler_params=_params(1))(dct, fpad, bpad, _tri(False))


def _gate_fwd(o, proj, col, name):
    def body(o_ref, g_ref, y_ref):
        g = g_ref[...]
        y_ref[...] = (o_ref[...] * (g * _sigmoid(g))).astype(BF16)

    return pl.pallas_call(
        body, name=name, grid=(S // ROWS,),
        in_specs=[_col_spec(ROWS, D, 0), _col_spec(ROWS, D, col)],
        out_specs=_col_spec(ROWS, D, 0), out_shape=jax.ShapeDtypeStruct((S, D), BF16),
        compiler_params=_params(1))(o, proj)


def _gate_bwd(dy, o, proj, col, name):
    def body(dy_ref, o_ref, g_ref, do_ref, dg_ref):
        g, dyv = g_ref[...], dy_ref[...]
        sg = _sigmoid(g)
        do_ref[...] = dyv * (g * sg)
        dg_ref[...] = (dyv * o_ref[...] * (sg * (1.0 + g * (1.0 - sg)))).astype(BF16)

    return pl.pallas_call(
        body, name=name, grid=(S // ROWS,),
        in_specs=[_col_spec(ROWS, D, 0), _col_spec(ROWS, D, 0), _col_spec(ROWS, D, col)],
        out_specs=[_col_spec(ROWS, D, 0)] * 2,
        out_shape=[jax.ShapeDtypeStruct((S, D), F32), jax.ShapeDtypeStruct((S, D), BF16)],
        compiler_params=_params(1))(dy, o, proj)


def _headnorm_bwd(x, col, gain, dy, rope, name):
    e = _seg_mat(D)
    tabs = list(rope) if rope is not None else []

    def body(*refs):
        x_ref, g_ref, dy_ref, e_ref = refs[:4]
        dx_ref, dg_ref = refs[-2:]
        xv, dyv, ev = x_ref[...], dy_ref[...], e_ref[...]
        if rope is not None:
            c, a, b = (jnp.tile(t[...], (1, D // LANES)) for t in refs[4:7])
            dyv = _rope_t(dyv, c, a, b)
        r = _head_rstd(xv, ev)
        xh = xv * r
        part = jnp.sum(dyv * xh, axis=0, keepdims=True)

        @pl.when(pl.program_id(0) == 0)
        def _():
            dg_ref[...] = part

        @pl.when(pl.program_id(0) != 0)
        def _():
            dg_ref[...] += part

        gy = dyv * g_ref[...]
        seg = _spread(_dot_split(gy * xh, ev, 2) * (1.0 / HD), D)
        dx_ref[...] = (r * (gy - xh * seg)).astype(BF16)

    whole = lambda a: pl.BlockSpec(a.shape, lambda i: (0, 0))
    return pl.pallas_call(
        body, name=name, grid=(S // ROWS,),
        in_specs=[_col_spec(ROWS, D, col), whole(gain), _col_spec(ROWS, D, 0), whole(e)]
                 + [pl.BlockSpec((ROWS, LANES), lambda i: (i, 0))] * len(tabs),
        out_specs=[_col_spec(ROWS, D, 0), whole(gain)],
        out_shape=[jax.ShapeDtypeStruct((S, D), BF16), jax.ShapeDtypeStruct((1, D), F32)],
        compiler_params=_params(1))(x, gain, dy, e, *tabs)


def _dup_mat():
    r, c = np.arange(KVW)[:, None], np.arange(2 * KVW)[None, :]
    return (r // HD == c // LANES) & (r % HD == c % HD)


def _fold_mat():
    r, c = np.arange(D)[:, None], np.arange(KVW)[None, :]
    return (r // (2 * LANES) == c // HD) & (r % HD == c % HD)


def _b_post(pb, kv, qg, kg, rope):
    e, ek = _seg_mat(D), _seg_mat(KVW)
    dup = jnp.asarray(_dup_mat(), BF16)

    def body(q_ref, k_ref, v_ref, qg_ref, kg_ref, e_ref, ek_ref, dup_ref, c_ref, a_ref, b_ref, qo, ko, vo):
        c1, a1, b1 = c_ref[...], a_ref[...], b_ref[...]
        qv = q_ref[...]
        qn = qv * _head_rstd(qv, e_ref[...]) * qg_ref[...]
        t = lambda z, n: jnp.tile(z, (1, n))
        qo[...] = (_rope(qn, t(c1, D // LANES), t(a1, D // LANES), t(b1, D // LANES)) * SCALE).astype(BF16)
        kvv = k_ref[...]
        kn = kvv * _head_rstd(kvv, ek_ref[...]) * kg_ref[...]
        kr = _rope(kn, t(c1, KVW // LANES), t(a1, KVW // LANES), t(b1, KVW // LANES)).astype(BF16)
        ko[...] = _dot(kr, dup_ref[...]).astype(BF16)
        vo[...] = _dot(v_ref[...].astype(BF16), dup_ref[...]).astype(BF16)

    whole = lambda a: pl.BlockSpec(a.shape, lambda i: (0, 0))
    tab = pl.BlockSpec((ROWS, LANES), lambda i: (i, 0))
    return pl.pallas_call(
        body, name="b_post", grid=(S // ROWS,),
        in_specs=[_col_spec(ROWS, D, 0), _col_spec(ROWS, KVW, 0), _col_spec(ROWS, KVW, 1),
                  whole(qg), whole(kg), whole(e), whole(ek), whole(dup), tab, tab, tab],
        out_specs=[_col_spec(ROWS, D, 0), _col_spec(ROWS, 2 * KVW, 0), _col_spec(ROWS, 2 * KVW, 0)],
        out_shape=[jax.ShapeDtypeStruct((S, D), BF16), jax.ShapeDtypeStruct((S, 2 * KVW), BF16),
                   jax.ShapeDtypeStruct((S, 2 * KVW), BF16)],
        compiler_params=_params(1))(pb, kv, kv, qg, kg, e, ek, dup, *rope)


def _kv_bwd(dkdup, dvdup, kv, kg, rope):
    ek = _seg_mat(KVW)
    fold = jnp.asarray(_fold_mat(), BF16)

    def body(dk_ref, dv_ref, k_ref, kg_ref, ek_ref, fold_ref, c_ref, a_ref, b_ref, dkv_ref, dg_ref):
        ev, fv = ek_ref[...], fold_ref[...]
        t = lambda z: jnp.tile(z[...], (1, KVW // LANES))
        dk = _rope_t(_dot_split(dk_ref[...], fv, 3), t(c_ref), t(a_ref), t(b_ref))
        dv = _dot_split(dv_ref[...], fv, 3)
        xv = k_ref[...]
        r = _head_rstd(xv, ev)
        xh = xv * r
        part = jnp.sum(dk * xh, axis=0, keepdims=True)

        @pl.when(pl.program_id(0) == 0)
        def _():
            dg_ref[...] = part

        @pl.when(pl.program_id(0) != 0)
        def _():
            dg_ref[...] += part

        gy = dk * kg_ref[...]
        seg = _spread(_dot_split(gy * xh, ev, 2) * (1.0 / HD), KVW)
        dkv_ref[:, 0:KVW] = (r * (gy - xh * seg)).astype(BF16)
        dkv_ref[:, KVW:2 * KVW] = dv.astype(BF16)

    whole = lambda a: pl.BlockSpec(a.shape, lambda i: (0, 0))
    tab = pl.BlockSpec((ROWS, LANES), lambda i: (i, 0))
    return pl.pallas_call(
        body, name="kv_bwd", grid=(S // ROWS,),
        in_specs=[_col_spec(ROWS, D, 0), _col_spec(ROWS, D, 0), _col_spec(ROWS, KVW, 0),
                  whole(kg), whole(ek), whole(fold), tab, tab, tab],
        out_specs=[_col_spec(ROWS, 2 * KVW, 0), whole(kg)],
        out_shape=[jax.ShapeDtypeStruct((S, 2 * KVW), BF16), jax.ShapeDtypeStruct((1, KVW), F32)],
        compiler_params=_params(1))(dkdup, dvdup, kv, kg, ek, fold, *rope)


def _loss_head(out, target):
    def body(o_ref, t_ref, d_ref, db_ref, l_ref):
        diff = o_ref[...] - t_ref[...]
        d = diff * (1.0 / D)
        d_ref[...] = d
        db_ref[...] = d.astype(BF16)

        @pl.when(pl.program_id(0) == 0)
        def _():
            l_ref[...] = jnp.zeros_like(l_ref)

        l_ref[...] += jnp.sum(diff * diff, axis=0, keepdims=True)

    return _rows_call(body, "loss_head", [out, target], [((S, D), F32), ((S, D), BF16), ((1, D), F32)])


def _lane():
    return lax.broadcasted_iota(jnp.int32, (1, LANES), 1)


def _head_mask(hh):
    return (_lane() < HD) if hh == 0 else (_lane() >= HD)


def _fox_fwd(q, k, v, ct, riding):
    nq, npair = S // ATT, NH // 2
    ni, no = len(riding.ins), len(riding.outs)

    def body(q_ref, k_ref, v_ref, c_ref, *rest):
        o_ref, lse_ref = rest[ni:ni + 2]
        pair, i = pl.program_id(0), pl.program_id(1)
        at_end = riding.hooks(rest[:ni], rest[ni + 2:ni + 2 + no], *rest[ni + 2 + no:],
                              first=(pair == 0) & (i == 0), middle=(pair == npair // 2) & (i == 0),
                              last=(pair == npair - 1) & (i == nq - 1))
        q2 = q_ref[...]
        qms = [jnp.where(_head_mask(hh), q2, jnp.zeros_like(q2)) for hh in (0, 1)]

        def probs(off, width, m, hh, diag):
            s = _dot(qms[hh], k_ref[pl.ds(off, width), :], NT) - c_ref[hh:hh + 1, pl.ds(off, width)]
            if diag:
                row = i * ATT + lax.broadcasted_iota(jnp.int32, (ATT, width), 0)
                col = off + lax.broadcasted_iota(jnp.int32, (ATT, width), 1)
                s = jnp.where(col <= row, s, NEG)
            m_new = jnp.maximum(m, jnp.max(s, axis=1, keepdims=True))
            p = jnp.exp(s - m_new)
            p_hi = p.astype(BF16)
            return m_new, jnp.exp(m - m_new), p_hi, (p - p_hi.astype(F32)).astype(BF16)

        def weighted(off, width, p_hi, p_lo, hh):
            vj = v_ref[pl.ds(off, width), :]
            v1 = jnp.where(_head_mask(hh), vj, jnp.ones_like(vj))
            return _dot(p_hi, v1) + _dot(p_lo, v1)

        def step(off, width, carry, diag):
            off = pl.multiple_of(off, ATT)
            out = []
            for hh in (0, 1):
                m, acc = carry[hh]
                m, alpha, p_hi, p_lo = probs(off, width, m, hh, diag)
                out.append((m, alpha * acc + weighted(off, width, p_hi, p_lo, hh)))
            return tuple(out)

        one = (jnp.full((ATT, 1), NEG, F32), jnp.zeros((ATT, LANES), F32))
        carry = lax.fori_loop(0, i // 2, lambda j, cr: step(j * (2 * ATT), 2 * ATT, cr, False), (one, one))
        carry = lax.cond(i % 2 == 1, lambda cr: step((i - 1) * ATT, 2 * ATT, cr, True),
                         lambda cr: step(i * ATT, ATT, cr, True), carry)
        res = []
        for hh in (0, 1):
            m, acc = carry[hh]
            l = jnp.max(jnp.where(_head_mask(1 - hh), acc, 0.0), axis=1, keepdims=True)
            res.append((acc / l, m + jnp.log(l)))
        first = _head_mask(0)
        o_ref[...] = jnp.where(first, res[0][0], res[1][0])
        lse_ref[...] = jnp.where(first, res[0][1], res[1][1])
        at_end()

    blk = pl.BlockSpec((ATT, LANES), lambda p, i: (i, p))
    full = pl.BlockSpec((S, LANES), lambda p, i: (0, p))
    res = pl.pallas_call(
        body, name="fox_fwd", grid=(npair, nq),
        in_specs=[blk, full, full, pl.BlockSpec((None, 2, S), lambda p, i: (p, 0, 0))] + riding.in_specs,
        out_specs=[blk, blk] + riding.out_specs,
        out_shape=[jax.ShapeDtypeStruct((S, D), F32)] * 2 + riding.out_shape,
        scratch_shapes=riding.scratch,
        compiler_params=_params(2))(q, k, v, ct, *riding.ins)
    return res[0], res[1], res[2:]


def _fox_bwd(q, k, v, ct, o, lse, do, riding):
    nq, npair = S // ATT, NH // 2
    ni, no = len(riding.ins), len(riding.outs)

    def body(q_ref, k_ref, v_ref, c_ref, o_ref, lse_ref, do_ref, *rest):
        dq_ref, dk_ref, dvb_ref, dc_ref = rest[ni:ni + 4]
        dv_ref = rest[ni + 4 + no]
        pair, i = pl.program_id(0), pl.program_id(1)
        at_end = riding.hooks(rest[:ni], rest[ni + 4:ni + 4 + no], *rest[ni + 5 + no:],
                              first=(pair == 0) & (i == 0), middle=(pair == npair // 2) & (i == 0),
                              last=(pair == npair - 1) & (i == nq - 1))

        @pl.when(i == 0)
        def _():
            dk_ref[...] = jnp.zeros_like(dk_ref)
            dv_ref[...] = jnp.zeros_like(dv_ref)
            dc_ref[...] = jnp.zeros_like(dc_ref)

        q2, do2, lse2 = q_ref[...], do_ref[...], lse_ref[...]
        do2b = do2.astype(BF16)
        prod = do2b.astype(F32) * o_ref[...]
        heads = []
        for hh in (0, 1):
            hm = _head_mask(hh)
            heads.append((jnp.where(hm, q2, jnp.zeros_like(q2)), jnp.where(hm, do2b, jnp.zeros_like(do2b)),
                          jnp.sum(jnp.where(hm, prod, 0.0), axis=1, keepdims=True),
                          jnp.max(jnp.where(hm, lse2, NEG), axis=1, keepdims=True)))

        def step(off, width, dqs, diag):
            off = pl.multiple_of(off, ATT)
            kj, vj = k_ref[pl.ds(off, width), :], v_ref[pl.ds(off, width), :]
            dk, dv, out = None, None, []
            for hh in (0, 1):
                qm, dom, delta, lse_h = heads[hh]
                s = _dot(qm, kj, NT) - c_ref[hh:hh + 1, pl.ds(off, width)]
                p = jnp.exp(s - lse_h)
                if diag:
                    row = i * ATT + lax.broadcasted_iota(jnp.int32, (ATT, width), 0)
                    col = off + lax.broadcasted_iota(jnp.int32, (ATT, width), 1)
                    p = jnp.where(col <= row, p, 0.0)
                ds = p * (_dot(dom, vj, NT) - delta)
                dc_ref[hh:hh + 1, pl.ds(off, width)] += -jnp.sum(ds, axis=0, keepdims=True)
                dsb = ds.astype(BF16)
                dk_h, dv_h = _dot(dsb, qm, TN), _dot(p.astype(BF16), dom, TN)
                dk, dv = (dk_h, dv_h) if dk is None else (dk + dk_h, dv + dv_h)
                out.append(dqs[hh] + _dot(dsb, kj))
            dk_ref[pl.ds(off, width), :] += dk
            dv_ref[pl.ds(off, width), :] += dv
            return tuple(out)

        zero = jnp.zeros((ATT, LANES), F32)
        dqs = lax.fori_loop(0, i // 2, lambda j, acc: step(j * (2 * ATT), 2 * ATT, acc, False), (zero, zero))
        dqs = lax.cond(i % 2 == 1, lambda acc: step((i - 1) * ATT, 2 * ATT, acc, True),
                       lambda acc: step(i * ATT, ATT, acc, True), dqs)
        dq_ref[...] = jnp.where(_head_mask(0), dqs[0], dqs[1]) * SCALE

        @pl.when(i == nq - 1)
        def _():
            dvb_ref[...] = dv_ref[...].astype(BF16)

        at_end()

    blk = pl.BlockSpec((ATT, LANES), lambda p, i: (i, p))
    full = pl.BlockSpec((S, LANES), lambda p, i: (0, p))
    cspec = pl.BlockSpec((None, 2, S), lambda p, i: (p, 0, 0))
    res = pl.pallas_call(
        body, name="fox_bwd", grid=(npair, nq),
        in_specs=[blk, full, full, cspec, blk, blk, blk] + riding.in_specs,
        out_specs=[blk, full, full, cspec] + riding.out_specs,
        out_shape=[jax.ShapeDtypeStruct((S, D), F32)] * 2 + [jax.ShapeDtypeStruct((S, D), BF16),
                                                              jax.ShapeDtypeStruct((npair, 2, S), F32)]
                  + riding.out_shape,
        scratch_shapes=[pltpu.VMEM((S, LANES), F32)] + riding.scratch,
        compiler_params=_params(2))(q, k, v, ct, o, lse, do, *riding.ins)
    return res[0], res[1], res[2], res[3], res[4:]


def _both_heads(x):
    return jnp.concatenate([jnp.where(_head_mask(hh), x, jnp.zeros_like(x)) for hh in (0, 1)], axis=0)


def _per_head(col0, col1):
    return jnp.concatenate([jnp.broadcast_to(col0, (WINDOW, 1)), jnp.broadcast_to(col1, (WINDOW, 1))], axis=0)


def _unstack(x2):
    return jnp.where(_head_mask(0), x2[:WINDOW], x2[WINDOW:])


def _swa_valid(i, start):
    r = lax.broadcasted_iota(jnp.int32, (2 * WINDOW, 2 * WINDOW), 0)
    qabs = i * WINDOW + jnp.where(r >= WINDOW, r - WINDOW, r)
    kabs = start + lax.broadcasted_iota(jnp.int32, (2 * WINDOW, 2 * WINDOW), 1)
    return (kabs <= qabs) & (qabs - kabs < WINDOW)


def _swa_fwd(q, kdup, vdup, sinks_t):
    def body(q_ref, k_ref, v_ref, sk_ref, o_ref, lse_ref):
        skv = sk_ref[...]
        first = _head_mask(0)
        for sb in range(SWQ):
            i = pl.program_id(1) * SWQ + sb
            rows = slice(sb * WINDOW, (sb + 1) * WINDOW)
            start = pl.multiple_of(jnp.maximum(i - 1, 0) * WINDOW, WINDOW)
            kk, vv = k_ref[pl.ds(start, 2 * WINDOW), :], v_ref[pl.ds(start, 2 * WINDOW), :]
            q2 = q_ref[rows, :]
            valid = _swa_valid(i, start)[:WINDOW]
            res = []
            for hh in (0, 1):
                hm = _head_mask(hh)
                sink = jnp.max(jnp.where(hm, skv, NEG), axis=1, keepdims=True)
                s = jnp.where(valid, _dot(jnp.where(hm, q2, jnp.zeros_like(q2)), kk, NT), NEG)
                m = jnp.maximum(jnp.max(s, axis=1, keepdims=True), sink)
                p = jnp.exp(s - m)
                l = jnp.sum(p, axis=1, keepdims=True) + jnp.exp(sink - m)
                res.append((_dot(p.astype(BF16), vv) / l, m + jnp.log(l)))
            o_ref[rows, :] = jnp.where(first, res[0][0], res[1][0])
            lse_ref[rows, :] = jnp.where(first, res[0][1], res[1][1])

    blk = pl.BlockSpec((SWQ * WINDOW, LANES), lambda p, i: (i, p))
    full = pl.BlockSpec((S, LANES), lambda p, i: (0, p // 2))
    return pl.pallas_call(
        body, name="swa_fwd", grid=(NH // 2, S // (SWQ * WINDOW)),
        in_specs=[blk, full, full, pl.BlockSpec((1, LANES), lambda p, i: (0, p))],
        out_specs=[blk, blk],
        out_shape=[jax.ShapeDtypeStruct((S, D), F32)] * 2,
        compiler_params=_params(2))(q, kdup, vdup, sinks_t)


def _swa_bwd(q, kdup, vdup, sinks_t, o, lse, do):
    def body(q_ref, k_ref, v_ref, sk_ref, o_ref, lse_ref, do_ref, dq_ref, dk_ref, dv_ref, dsk_ref):
        @pl.when(pl.program_id(1) == 0)
        def _():
            dk_ref[...] = jnp.zeros_like(dk_ref)
            dv_ref[...] = jnp.zeros_like(dv_ref)
            dsk_ref[...] = jnp.zeros_like(dsk_ref)

        skv = sk_ref[...]
        first = _head_mask(0)
        sink = _per_head(*[jnp.max(jnp.where(_head_mask(hh), skv, NEG), axis=1, keepdims=True) for hh in (0, 1)])
        for sb in range(SWQ):
            i = pl.program_id(1) * SWQ + sb
            rows = slice(sb * WINDOW, (sb + 1) * WINDOW)
            start = pl.multiple_of(jnp.maximum(i - 1, 0) * WINDOW, WINDOW)
            kk, vv = k_ref[pl.ds(start, 2 * WINDOW), :], v_ref[pl.ds(start, 2 * WINDOW), :]
            do2b = do_ref[rows, :].astype(BF16)
            prod, lse2 = do2b.astype(F32) * o_ref[rows, :], lse_ref[rows, :]
            qs, dos = _both_heads(q_ref[rows, :]), _both_heads(do2b)
            delta = jnp.concatenate([jnp.sum(jnp.where(_head_mask(hh), prod, 0.0), axis=1, keepdims=True)
                                     for hh in (0, 1)], axis=0)
            lse_h = jnp.concatenate([jnp.max(jnp.where(_head_mask(hh), lse2, NEG), axis=1, keepdims=True)
                                     for hh in (0, 1)], axis=0)
            p = jnp.where(_swa_valid(i, start), jnp.exp(_dot(qs, kk, NT) - lse_h), 0.0)
            dsb = (p * (_dot(dos, vv, NT) - delta)).astype(BF16)
            dk_ref[pl.ds(start, 2 * WINDOW), :] += _dot(dsb, qs, TN)
            dv_ref[pl.ds(start, 2 * WINDOW), :] += _dot(p.astype(BF16), dos, TN)
            dq_ref[rows, :] = _unstack(_dot(dsb, kk)) * SCALE
            t = jnp.exp(sink - lse_h) * delta
            dsk_ref[...] += -jnp.where(first, jnp.sum(t[:WINDOW], axis=0, keepdims=True),
                                       jnp.sum(t[WINDOW:], axis=0, keepdims=True))

    blk = pl.BlockSpec((SWQ * WINDOW, LANES), lambda p, i: (i, p))
    full = pl.BlockSpec((S, LANES), lambda p, i: (0, p // 2))
    acc = pl.BlockSpec((S, LANES), lambda p, i: (0, p))
    sk = pl.BlockSpec((1, LANES), lambda p, i: (0, p))
    return pl.pallas_call(
        body, name="swa_bwd", grid=(NH // 2, S // (SWQ * WINDOW)),
        in_specs=[blk, full, full, sk, blk, blk, blk],
        out_specs=[blk, acc, acc, sk],
        out_shape=[jax.ShapeDtypeStruct((S, D), F32)] * 3 + [jax.ShapeDtypeStruct((1, D), F32)],
        compiler_params=_params(2))(q, kdup, vdup, sinks_t, o, lse, do)


def _adamw_math(w, g, m, v):
    m = ADAM_B1 * m + (1.0 - ADAM_B1) * g
    v = ADAM_B2 * v + (1.0 - ADAM_B2) * jnp.square(g)
    m_hat = m / (1.0 - ADAM_B1 ** ADAM_STEP)
    v_hat = v / (1.0 - ADAM_B2 ** ADAM_STEP)
    delta = -ADAM_LR * (m_hat / (jnp.sqrt(v_hat) + ADAM_EPS) + ADAM_WD * w)
    return delta, m, v


def _adamw(w, g, m, v, name):
    r, c = w.shape
    tr = min(r, 128)

    def body(w_ref, g_ref, m_ref, v_ref, d_ref, mo_ref, vo_ref):
        d_ref[...], mo_ref[...], vo_ref[...] = _adamw_math(w_ref[...], g_ref[...], m_ref[...], v_ref[...])

    spec = pl.BlockSpec((tr, c), lambda i: (i, 0))
    return pl.pallas_call(
        body, name=name, grid=(r // tr,), in_specs=[spec] * 4, out_specs=[spec] * 3,
        out_shape=[jax.ShapeDtypeStruct((r, c), F32)] * 3, compiler_params=_params(1))(w, g, m, v)


SUM_TILE = 128


FLAT_BLOCK = 257 * 1024


def _tiles(shape, axis, lead=0):
    if len(shape) == 1:
        count = shape[0] // FLAT_BLOCK
        return (FLAT_BLOCK,), count, lambda pos, *lead_idx: (sum(k * count for k in lead_idx) + pos,)
    r, c = shape
    blk = (SUM_TILE, c) if axis == 0 else (r, SUM_TILE)
    count = shape[axis] // SUM_TILE

    def index(pos, *lead_idx):
        return tuple(lead_idx) + ((pos, 0) if axis == 0 else (0, pos))

    return (None,) * lead + blk, count, index


def _adamw_halves(w, g_mine, g_theirs, m, v, axis, name):
    blk, count, index = _tiles(w.shape, axis)
    per_half = count // 2

    def body(w_ref, a_ref, b_ref, m_ref, v_ref, g_ref, d_ref, mo_ref, vo_ref):
        is_mine = pl.program_id(0) // per_half == lax.axis_index("c")
        g = jnp.where(is_mine, a_ref[...], b_ref[...])
        g_ref[...] = g
        d_ref[...], mo_ref[...], vo_ref[...] = _adamw_math(w_ref[...], g, m_ref[...], v_ref[...])

    spec = pl.BlockSpec(blk, lambda i: index(i))
    half = pl.BlockSpec(blk, lambda i: index(i % per_half))
    return pl.pallas_call(
        body, name=name, grid=(count,), in_specs=[spec, half, half, spec, spec], out_specs=[spec] * 4,
        out_shape=[jax.ShapeDtypeStruct(w.shape, F32)] * 4, compiler_params=_params(1))(w, g_mine, g_theirs, m, v)


def _chip_sum(blocks, from_sibling, axis, name):
    flat = blocks.ndim == 1
    blk, count, index = _tiles((from_sibling.shape[0] // NCHIP,) if flat else from_sibling.shape[1:], axis, lead=1)

    def body(lo_ref, hi_ref, p_ref, o32, o16):
        mine = jnp.where(lax.axis_index("c") == 0, lo_ref[...], hi_ref[...])
        acc = mine + p_ref[...]
        o32[...] = acc
        o16[...] = acc.astype(BF16)

    half = pl.BlockSpec(blk, lambda k, i: index(i, k))
    if flat:
        lo = pl.BlockSpec(blk, lambda k, i: (2 * count * k + i,))
        hi = pl.BlockSpec(blk, lambda k, i: (2 * count * k + count + i,))
    else:
        lo, hi = half, pl.BlockSpec(blk, lambda k, i: index(i + count, k))
    return pl.pallas_call(
        body, name=name, grid=(NCHIP, count), in_specs=[lo, hi, half], out_specs=[half, half],
        out_shape=[jax.ShapeDtypeStruct(from_sibling.shape, F32), jax.ShapeDtypeStruct(from_sibling.shape, BF16)],
        compiler_params=_params(2))(blocks, blocks, from_sibling)


def _mesh_sum(own, parts, axis, name):
    blk, count, index = _tiles(own.shape, axis)
    n = NCHIP - 1

    def body(a_ref, *refs):
        acc = a_ref[...]
        for k in range(n):
            acc = acc + refs[k][...].astype(F32)
        refs[n][...] = acc

    spec = pl.BlockSpec(blk, lambda i: index(i))
    if own.ndim == 1:
        part = [pl.BlockSpec(blk, lambda i, k=k: (k * count + i,)) for k in range(n)]
    else:
        part = [pl.BlockSpec((None,) + blk, lambda i, k=k: (k,) + index(i)) for k in range(n)]
    return pl.pallas_call(
        body, name=name, grid=(count,), in_specs=[spec] + part,
        out_specs=spec, out_shape=jax.ShapeDtypeStruct(own.shape, F32),
        compiler_params=_params(1))(own, *([parts] * n))


def _sum_stack(parts, name):
    n = parts.shape[0]

    def body(p_ref, o_ref):
        acc = p_ref[0]
        for k in range(1, n):
            acc = acc + p_ref[k]
        o_ref[...] = acc

    return pl.pallas_call(body, name=name, out_shape=jax.ShapeDtypeStruct(parts.shape[1:], F32))(parts)


def _coords():
    return lax.axis_index("x"), lax.axis_index("y"), lax.axis_index("c")


def _chip(who):
    return 2 * who[0] + who[1]


def _flip(who, mask):
    return tuple((1 - v) if b else v for v, b in zip(who, mask))


def _transfer(transfers, t, I, O, ssem, rsem, receiving):
    tr, me = transfers[t], _coords()
    peer = _flip(me, tr["mask"])
    return pltpu.make_async_remote_copy(
        src_ref=tr["src"](I, O, me), dst_ref=tr["dst"](I, O, peer if receiving else me),
        send_sem=ssem.at[t], recv_sem=rsem.at[t], device_id=peer, device_id_type=MESH)


def _start_transfers(transfers, I, O, ssem, rsem, onward):
    arrived = set()
    for t, tr in enumerate(transfers):
        after = tr.get("after")
        if (after is not None) != onward:
            continue
        if after is not None and after not in arrived:
            _transfer(transfers, after, I, O, ssem, rsem, True).wait_recv()
            arrived.add(after)
        _transfer(transfers, t, I, O, ssem, rsem, False).start()


def _finish_transfers(transfers, I, O, ssem, rsem):
    passed_on = {tr["after"] for tr in transfers if tr.get("after") is not None}
    for t in range(len(transfers)):
        if t not in passed_on:
            _transfer(transfers, t, I, O, ssem, rsem, True).wait_recv()
    for t in range(len(transfers)):
        _transfer(transfers, t, I, O, ssem, rsem, False).wait_send()


def _own_copies(own, I, O, stage, lsem, leg):
    for n, (src, dst) in enumerate(own):
        me = _coords()
        bring =pltpu.make_async_copy(src(I, O, me), stage[n], lsem.at[2 * n])
        put = pltpu.make_async_copy(stage[n], dst(I, O, me), lsem.at[2 * n + 1])
        if leg == 0:
            bring.start()
        elif leg == 1:
            bring.wait()
            put.start()
        else:
            put.wait()


def _own_scratch(own, ins):
    return [pltpu.VMEM(ins[n].shape, ins[n].dtype) for n in range(len(own))], pltpu.SemaphoreType.DMA((max(2 * len(own), 1),))


def _exchange(name, ins, outs, transfers, own=()):
    ni, no = len(ins), len(outs)
    nt = len(transfers)
    stages, stage_sems = _own_scratch(own, ins)

    def body(*refs):
        I, O = refs[:ni], refs[ni:ni + no]
        ssem, rsem, lsem = refs[ni + no:ni + no + 3]
        stage = refs[ni + no + 3:]
        _own_copies(own, I, O, stage, lsem, 0)
        _start_transfers(transfers, I, O, ssem, rsem, False)
        _own_copies(own, I, O, stage, lsem, 1)
        _start_transfers(transfers, I, O, ssem, rsem, True)
        _finish_transfers(transfers, I, O, ssem, rsem)
        _own_copies(own, I, O, stage, lsem, 2)

    hbm = pl.BlockSpec(memory_space=pltpu.HBM)
    return pl.pallas_call(
        body, name=name, in_specs=[hbm] * ni, out_specs=[hbm] * no,
        out_shape=[jax.ShapeDtypeStruct(s, d) for s, d in outs],
        scratch_shapes=[pltpu.SemaphoreType.DMA((nt,)), pltpu.SemaphoreType.DMA((nt,)), stage_sems] + stages,
        compiler_params=pltpu.CompilerParams(has_side_effects=True, vmem_limit_bytes=VMEM_LIMIT))(*ins)


CHIP_MASKS = [(0, 1, 0), (1, 0, 0), (1, 1, 0)]
SIBLING = (0, 0, 1)


def _half(shape2d, axis, which):
    n = shape2d[axis] // 2
    cut = pl.ds(pl.multiple_of(which * n, n), n)
    return (cut, slice(None)) if axis == 0 else (slice(None), cut)


class _Riding:
    def __init__(self, transfers, ins, outs, own=()):
        self.transfers, self.ins, self.outs, self.own = transfers, list(ins), list(outs), list(own)
        hbm = pl.BlockSpec(memory_space=pltpu.HBM)
        self.in_specs, self.out_specs = [hbm] * len(self.ins), [hbm] * len(self.outs)
        self.out_shape = [jax.ShapeDtypeStruct(s, d) for s, d in self.outs]
        stages, stage_sems = _own_scratch(self.own, self.ins)
        self.scratch = [pltpu.SemaphoreType.DMA((max(len(transfers), 1),))] * 2 + [stage_sems] + stages

    def hooks(self, I, O, ssem, rsem, lsem, *stage, first, middle, last):
        tr, own = self.transfers, self.own

        @pl.when(first)
        def _():
            _own_copies(own, I, O, stage, lsem, 0)
            _start_transfers(tr, I, O, ssem, rsem, False)

        if own or any(t.get("after") is not None for t in tr):
            @pl.when(middle)
            def _():
                _own_copies(own, I, O, stage, lsem, 1)
                _start_transfers(tr, I, O, ssem, rsem, True)

        def at_end():
            @pl.when(last)
            def _():
                _finish_transfers(tr, I, O, ssem, rsem)
                _own_copies(own, I, O, stage, lsem, 2)

        return at_end


def _stretch(n, pos):
    return (pl.ds(pos * n if isinstance(pos, int) else pl.multiple_of(pos * n, n), n),)


def _gather_plan(shards, axes):
    def half(a, who):
        if shards[a].ndim == 1:
            return _stretch(shards[a].shape[0] // 2, who[2])
        return _half(shards[a].shape, axes[a], who[2])

    def landed(a, chip, who):
        if shards[a].ndim == 1:
            return _stretch(shards[a].shape[0] // 2, 2 * chip + who[2])
        return (chip,) + half(a, who)

    over_ici, onward = [], []
    for a in range(len(shards)):
        for mask in CHIP_MASKS:
            over_ici.append(dict(
                mask=mask,
                src=lambda I, O, me, a=a: I[a].at[half(a, me)],
                dst=lambda I, O, who, a=a: O[a].at[landed(a, _chip(who), who)]))
            onward.append(dict(
                mask=SIBLING, after=len(over_ici) - 1,
                src=lambda I, O, me, a=a, mask=mask: O[a].at[landed(a, _chip(_flip(me, mask)), me)],
                dst=lambda I, O, who, a=a, mask=mask: O[a].at[landed(a, _chip(_flip(who, mask)), who)]))
    outs = [((NCHIP * s.shape[0],) if s.ndim == 1 else (NCHIP,) + s.shape, s.dtype) for s in shards]

    def whole(a, chip):
        return _stretch(shards[a].shape[0], chip) if shards[a].ndim == 1 else (chip,)

    own = [(lambda I, O, me, a=a: I[a], lambda I, O, me, a=a: O[a].at[whole(a, _chip(me))])
           for a in range(len(shards))]
    return over_ici + onward, outs, own


def _gather_shards(shards, axes):
    transfers, outs, own = _gather_plan(shards, axes)
    return _exchange("gather_weights", shards, outs, transfers, own)


def _to_sibling(arrs, name):
    transfers = [dict(mask=SIBLING, src=lambda I, O, me, a=a: I[a], dst=lambda I, O, who, a=a: O[a])
                 for a in range(len(arrs))]
    return _exchange(name, arrs, [(t.shape, t.dtype) for t in arrs], transfers)


def _halves_to_sibling(blocks, axes, name):
    def cut(a, which):
        return (slice(None),) + _half(blocks[a].shape[1:], axes[a], which)

    transfers, outs = [], []
    for a, (b, ax) in enumerate(zip(blocks, axes)):
        if b.ndim == 1:
            h = b.shape[0] // NCHIP // 2
            for k in range(NCHIP):
                transfers.append(dict(mask=SIBLING,
                                      src=lambda I, O, me, a=a, k=k, h=h: I[a].at[_stretch(h, 2 * k + 1 - me[2])],
                                      dst=lambda I, O, who, a=a, k=k, h=h: O[a].at[_stretch(h, k)]))
            outs.append(((NCHIP * h,), b.dtype))
        else:
            transfers.append(dict(mask=SIBLING, src=lambda I, O, me, a=a: I[a].at[cut(a, 1 - me[2])],
                                  dst=lambda I, O, who, a=a: O[a]))
            shape = list(b.shape)
            shape[ax + 1] //= 2
            outs.append((tuple(shape), b.dtype))
    return _exchange(name, blocks, outs, transfers)


def _scatter_plan(tb):
    def slot(a, k):
        return (k,) if tb[a].ndim == 3 else _stretch(tb[a].shape[0] // NCHIP, k)

    transfers = []
    for a in range(len(tb)):
        for n, mask in enumerate(CHIP_MASKS):
            transfers.append(dict(
                mask=mask,
                src=lambda I, O, me, a=a, mask=mask: I[a].at[slot(a, _chip(_flip(me, mask)))],
                dst=lambda I, O, who, a=a, n=n: O[a].at[slot(a, n)]))
    outs = [((3,) + t.shape[1:] if t.ndim == 3 else (3 * (t.shape[0] // NCHIP),), t.dtype) for t in tb]
    return transfers, outs


def _scatter_chip_sums(tb):
    transfers, outs = _scatter_plan(tb)
    return _exchange("scatter_grads", tb, outs, transfers)


def _gather_small(vec):
    def slot(who):
        return 4 * who[0] + 2 * who[1] + who[2]

    masks = [(m >> 2 & 1, m >> 1 & 1, m & 1) for m in range(1, 8)]
    transfers = [dict(mask=mask, src=lambda I, O, me: I[0], dst=lambda I, O, who: O[0].at[slot(who)])
                 for mask in masks]
    own = [(lambda I, O, me: I[0], lambda I, O, me: O[0].at[slot(me)])]
    return _exchange("gather_small", [vec], [((8,) + vec.shape, vec.dtype)], transfers, own)[0]


def _rope_tables(positions):
    half = ROT // 2
    inv_freq = jnp.power(jnp.float32(THETA), -jnp.arange(0, ROT, 2, dtype=F32) / ROT)
    ang = positions.astype(F32)[:, None] * inv_freq[None, :]
    cos, sin = jnp.cos(ang), jnp.sin(ang)
    one, zero, z8 = jnp.ones((S, HD - ROT), F32), jnp.zeros((S, HD - ROT), F32), jnp.zeros((S, half), F32)
    c = jnp.concatenate([cos, cos, one], axis=1)
    a = jnp.concatenate([-sin, z8, zero], axis=1)
    b = jnp.concatenate([z8, sin, zero], axis=1)
    return tuple(jnp.tile(t, (1, 2)) for t in (c, a, b))


def _tile_heads(g, w):
    return jnp.tile(g.reshape(1, HD), (1, w // HD))


def _fold_heads(dg):
    return dg.reshape(-1, HD).sum(axis=0)


def _pad_lanes(a):
    return jnp.pad(a, ((0, 0), (0, LANES - a.shape[1])))


def _local_step(x, target, positions, wt, fetch, late_weights, begin_reduce):
    rope = _rope_tables(positions)
    w1t = wt["w_in_a_t"]
    f_row = 3 * D // LANES
    wg_t = w1t[3 * D + NH:]
    in_b_block = lambda c: pl.BlockSpec((None, TN_, TN_), lambda j, i: (c, j, 0))
    b_pad = _pad_lanes(wt["b_forget"].reshape(1, NH))
    qg_a, kg_a = _tile_heads(wt["qnorm_a_g"], D), _tile_heads(wt["knorm_a_g"], D)
    qg_b, kg_b = _tile_heads(wt["qnorm_b_g"], D), _tile_heads(wt["knorm_b_g"], KVW)
    norm_a, kv_g, norm_b = wt["norm_a_g"].reshape(1, D), wt["kv_norm_g"].reshape(1, D), wt["norm_b_g"].reshape(1, D)
    sinks_t = jnp.repeat(wt["sinks"].reshape(1, NH), HD, axis=1)

    (u_a,) = _rmsnorm_fwd(x, [norm_a], "norm_a")
    qkv = _mm("proj_a", S, 3 * D, [(u_a, _a_rows(D), w1t, _b_rows(D), NT)])
    fpad = _mm("proj_f", S, LANES, [(u_a, _a_rows(D), w1t, _b_rows(D, row0=f_row, tn=LANES), NT)], tn=LANES)
    gate_a = _mm("proj_gate_a", S, D, [(u_a, _a_rows(D), wg_t, _b_rows(D), NT)])
    q_a, k_a, v_a = _a_post(qkv, qg_a, kg_a)
    ct = _forget_cumsum(fpad, b_pad)
    ct2 = ct[:NH].reshape(NH // 2, 2, S)
    o_a, lse_a, fetched = _fox_fwd(q_a, k_a, v_a, ct2, fetch)
    wt = {**wt, **late_weights(fetched)}
    w_in_b = wt["w_in_b"]
    y_a = _gate_fwd(o_a, gate_a, 0, "gate_a")
    h1 = _mm("out_a", S, D, [(y_a, _a_rows(D), wt["w_out_a"], _b_cols(D), None)], add=x)
    u_kv, u_b = _rmsnorm_fwd(h1, [kv_g, norm_b], "norm_b")
    kv = _mm("proj_kv", S, 2 * KVW, [(u_kv, _a_rows(D), wt["w_kv"], _b_cols(D), None)])
    pb = _mm("proj_b", S, 2 * D,
             [(u_b, _a_rows(D), w_in_b, pl.BlockSpec((None, D, TN_), lambda j, i: (j, 0, 0)), None)])
    q_b, kdup, vdup = _b_post(pb, kv, qg_b, kg_b, rope)
    o_b, lse_b = _swa_fwd(q_b, kdup, vdup, sinks_t)
    y_b = _gate_fwd(o_b, pb, 1, "gate_b")
    out = _mm("out_b", S, D, [(y_b, _a_rows(D), wt["w_out_b"], _b_cols(D), None)], add=h1)
    d_out, d_out_b, sq = _loss_head(out, target)

    g = {}
    g["w_out_b"] = _mm("dw_out_b", D, D, [(y_b, _a_cols(S), d_out_b, _b_cols(S), TN)])
    d_y_b = _mm("dy_b", S, D, [(d_out_b, _a_rows(D), wt["w_out_b"], _b_rows(D), NT)])
    d_o_b, d_gate_b = _gate_bwd(d_y_b, o_b, pb, 1, "gate_b_bwd")
    dq_b, dkdup, dvdup, dsk = _swa_bwd(q_b, kdup, vdup, sinks_t, o_b, lse_b, d_o_b)
    g["sinks"] = dsk[0, ::HD]
    d_qb_raw, dg = _headnorm_bwd(pb, 0, qg_b, dq_b, rope, "qnorm_b_bwd")
    g["qnorm_b_g"] = _fold_heads(dg)
    d_pb = [d_qb_raw, d_qb_raw, d_gate_b, d_gate_b]
    g["w_in_b"] = jnp.concatenate([
        _mm("dw_in_b_q", D, D, [(u_b, _a_cols(S), d_qb_raw, _b_cols(S), TN)], stacked=True),
        _mm("dw_in_b_gate", D, D, [(u_b, _a_cols(S), d_gate_b, _b_cols(S), TN)], stacked=True)], axis=0)
    d_u_b = _mm("du_b", S, D, [(d_pb[c], _a_rows(TN_, col=c % 2), w_in_b, in_b_block(c), NT) for c in range(NCHIP)])
    d_kv, dg = _kv_bwd(dkdup, dvdup, kv, kg_b, rope)
    g["knorm_b_g"] = _fold_heads(dg)
    g["w_kv"] = _mm("dw_kv", D, 2 * KVW, [(u_kv, _a_cols(S), d_kv, _b_cols(S), TN)])
    d_u_kv = _mm("du_kv", S, D, [(d_kv, _a_rows(2 * KVW), wt["w_kv"], _b_rows(2 * KVW), NT)])
    d_h1, d_h1_b, g["kv_norm_g"], g["norm_b_g"] = _rmsnorm_bwd(h1, [kv_g, norm_b], [d_u_kv, d_u_b], d_out, "norm_b_bwd")
    g["w_out_a"] = _mm("dw_out_a", D, D, [(y_a, _a_cols(S), d_h1_b, _b_cols(S), TN)])
    d_y_a = _mm("dy_a", S, D, [(d_h1_b, _a_rows(D), wt["w_out_a"], _b_rows(D), NT)])
    d_o_a, d_gate_a = _gate_bwd(d_y_a, o_a, gate_a, 0, "gate_a_bwd")
    riding, so_far = begin_reduce({n: g[n] for n in LATE})
    dq_a, dk_a, dv_a, dct, arrived = _fox_bwd(q_a, k_a, v_a, ct2, o_a, lse_a, d_o_a, riding)
    dct_pad = jnp.pad(dct.reshape(NH, S), ((0, LANES - NH), (0, 0)))
    d_f, db = _forget_bwd(dct_pad, fpad, b_pad)
    g["b_forget"] = db[0, :NH]
    d_q_raw, dg = _headnorm_bwd(qkv, 0, qg_a, dq_a, None, "qnorm_a_bwd")
    g["qnorm_a_g"] = _fold_heads(dg)
    d_k_raw, dg = _headnorm_bwd(qkv, 1, kg_a, dk_a, None, "knorm_a_bwd")
    g["knorm_a_g"] = _fold_heads(dg)
    pieces = [("q", d_q_raw), ("k", d_k_raw), ("v", dv_a), ("gate", d_gate_a)]
    dw = {n: _mm("dw_in_a_" + n, D, D, [(t, _a_cols(S), u_a, _b_cols(S), TN)]) for n, t in pieces}
    dw_f = _mm("dw_in_a_f", LANES, D, [(d_f, _a_cols(S, tm=LANES), u_a, _b_cols(S), TN)], tm=LANES)
    g["w_in_a"] = jnp.concatenate([dw["q"], dw["k"], dw["v"], dw_f[:NH], dw["gate"]], axis=0)
    riding, so_far_first = begin_reduce({"w_in_a": g["w_in_a"]})
    d_u_a, arrived_first = _mm("du_a", S, D, [
        (d_q_raw, _a_rows(D), w1t, _b_cols(D, row=0), None), (d_k_raw, _a_rows(D), w1t, _b_cols(D, row=1), None),
        (dv_a, _a_rows(D), w1t, _b_cols(D, row=2), None), (d_gate_a, _a_rows(D), wg_t, _b_cols(D), None),
        (d_f, _a_rows(LANES), w1t, _b_cols(LANES, row=f_row), None)], riding=riding)
    d_x, _, g["norm_a_g"] = _rmsnorm_bwd(x, [norm_a], [d_u_a], d_h1, "norm_a_bwd")
    return sq, d_x, g, (list(so_far_first) + list(so_far), list(arrived_first) + list(arrived))


BIG = ["w_in_a", "w_out_a", "w_kv", "w_in_b", "w_out_b"]
LATE = BIG[1:]
SPLIT = {"w_in_a": None, "w_out_a": 0, "w_kv": 0, "w_in_b": 0, "w_out_b": 0}
SMALL = ["norm_a_g", "b_forget", "qnorm_a_g", "knorm_a_g", "kv_norm_g", "knorm_b_g", "norm_b_g", "qnorm_b_g", "sinks"]
NAMES = ["norm_a_g", "w_in_a", "b_forget", "qnorm_a_g", "knorm_a_g", "w_out_a", "kv_norm_g", "w_kv", "knorm_b_g",
         "norm_b_g", "w_in_b", "qnorm_b_g", "sinks", "w_out_b"]


def _pack(vals):
    flat = []
    for v in vals:
        v = v.reshape(-1)
        flat.append(jnp.pad(v, (0, -v.shape[0] % LANES)))
    flat = jnp.concatenate(flat)
    flat = jnp.pad(flat, (0, -flat.shape[0] % (8 * LANES)))
    return flat.reshape(-1, LANES)


def _unpack(packed, shapes):
    flat, out, off = packed.reshape(-1), [], 0
    for s in shapes:
        n = int(np.prod(s))
        out.append(flat[off:off + n].reshape(s))
        off += n + (-n % LANES)
    return out


def kernel(x, positions, norm_a_g, w_in_a, b_forget, qnorm_a_g, knorm_a_g, w_out_a, kv_norm_g, w_kv, knorm_b_g, norm_b_g, w_in_b, qnorm_b_g, sinks, w_out_b, loss_target, m_norm_a_g, m_w_in_a, m_b_forget, m_qnorm_a_g, m_knorm_a_g, m_w_out_a, m_kv_norm_g, m_w_kv, m_knorm_b_g, m_norm_b_g, m_w_in_b, m_qnorm_b_g, m_sinks, m_w_out_b, v_norm_a_g, v_w_in_a, v_b_forget, v_qnorm_a_g, v_knorm_a_g, v_w_out_a, v_kv_norm_g, v_w_kv, v_knorm_b_g, v_norm_b_g, v_w_in_b, v_qnorm_b_g, v_sinks, v_w_out_b):
    w = dict(norm_a_g=norm_a_g, w_in_a=w_in_a, b_forget=b_forget, qnorm_a_g=qnorm_a_g, knorm_a_g=knorm_a_g,
             w_out_a=w_out_a, kv_norm_g=kv_norm_g, w_kv=w_kv, knorm_b_g=knorm_b_g, norm_b_g=norm_b_g,
             w_in_b=w_in_b, qnorm_b_g=qnorm_b_g, sinks=sinks, w_out_b=w_out_b)
    m = dict(norm_a_g=m_norm_a_g, w_in_a=m_w_in_a, b_forget=m_b_forget, qnorm_a_g=m_qnorm_a_g, knorm_a_g=m_knorm_a_g,
             w_out_a=m_w_out_a, kv_norm_g=m_kv_norm_g, w_kv=m_w_kv, knorm_b_g=m_knorm_b_g, norm_b_g=m_norm_b_g,
             w_in_b=m_w_in_b, qnorm_b_g=m_qnorm_b_g, sinks=m_sinks, w_out_b=m_w_out_b)
    v = dict(norm_a_g=v_norm_a_g, w_in_a=v_w_in_a, b_forget=v_b_forget, qnorm_a_g=v_qnorm_a_g, knorm_a_g=v_knorm_a_g,
             w_out_a=v_w_out_a, kv_norm_g=v_kv_norm_g, w_kv=v_w_kv, knorm_b_g=v_knorm_b_g, norm_b_g=v_norm_b_g,
             w_in_b=v_w_in_b, qnorm_b_g=v_qnorm_b_g, sinks=v_sinks, w_out_b=v_w_out_b)
    my_chip = 2 * lax.axis_index("x") + lax.axis_index("y")

    def shard2d(t, n):
        if n == "w_in_a":
            return jnp.transpose(t, (2, 0, 1)).reshape(-1)
        return t.reshape(t.shape[-2:])

    def unflat(t, n):
        return jnp.transpose(t.reshape(-1, 1, D), (1, 2, 0)) if n == "w_in_a" else t.reshape(w[n].shape)

    w2d = {n: shard2d(w[n], n) for n in BIG}

    norm_a_rows = jnp.broadcast_to(norm_a_g.reshape(1, D // NCHIP), (16, D // NCHIP))
    w1t, norm_rows = _gather_shards([w2d["w_in_a"].astype(BF16), norm_a_rows], [SPLIT["w_in_a"], 0])
    wt = {"w_in_a_t": w1t.reshape(-1, D), "norm_a_g": norm_rows[:, 0, :].reshape(1, D)}
    for n in SMALL[1:]:
        wt[n] = w[n]
    late_shards = [w2d[n].astype(BF16) for n in LATE]
    late_axes = [SPLIT[n] for n in LATE]
    transfers, outs, own = _gather_plan(late_shards, late_axes)
    fetch = _Riding(transfers, late_shards, outs, own)

    def late_weights(fetched):
        return {n: t if n == "w_in_b" else t.reshape(-1, t.shape[2]) for n, t in zip(LATE, fetched)}

    def as_blocks(t):
        if t.ndim == 3:
            return t
        return t.reshape(-1) if t.shape[0] % (8 * NCHIP) else t.reshape(NCHIP, -1, t.shape[1])

    def begin_reduce(grads):
        names = list(grads)
        axes = [SPLIT[n] for n in names]
        blocks = [as_blocks(grads[n]) for n in names]
        halves = _halves_to_sibling(blocks, axes, "sibling_halves_" + names[0])
        sums = [_chip_sum(blk, part, ax, "chip_sum_" + n) for n, ax, blk, part in zip(names, axes, blocks, halves)]
        bf16 = [s[1] for s in sums]
        transfers, outs = _scatter_plan(bf16)
        return _Riding(transfers, bf16, outs), [s[0] for s in sums]

    sq, d_x, g, (chip_f32, arrived) = _local_step(x[0], loss_target[0], positions, wt, fetch, late_weights,
                                                  begin_reduce)

    small_shapes = [(D,), (NH,), (HD,), (HD,), (D,), (HD,), (D,), (HD,), (NH,), (D,)]
    packed = _pack([g[n] for n in SMALL] + [sq])
    total = _sum_stack(_gather_small(packed), "sum_small")
    small_g = dict(zip(SMALL, _unpack(total, small_shapes)[:-1]))
    loss = 0.5 * jnp.sum(_unpack(total, small_shapes)[-1]) / D
    small_g["norm_a_g"] = lax.dynamic_slice(small_g["norm_a_g"], (my_chip * (D // NCHIP),), (D // NCHIP,))

    axes = [SPLIT[n] for n in BIG]
    halves = []
    for n, ax, t32, parts in zip(BIG, axes, chip_f32, arrived):
        if t32.ndim == 1:
            own = lax.dynamic_slice_in_dim(t32, my_chip * (t32.shape[0] // NCHIP), t32.shape[0] // NCHIP)
        else:
            own = lax.dynamic_index_in_dim(t32, my_chip, axis=0, keepdims=False)
        halves.append(_mesh_sum(own, parts, ax, "mesh_sum_" + n))
    sibling_done = _to_sibling(halves, "finished_halves")

    res = {}
    for n, ax, mine_half, their_half in zip(BIG, axes, halves, sibling_done):
        out4 = _adamw_halves(w2d[n], mine_half, their_half, shard2d(m[n], n), shard2d(v[n], n), ax, "adamw_" + n)
        res[n] = tuple(unflat(t, n) for t in out4)
    sm_g = _pack([small_g[n] for n in SMALL])
    sm = [_pack([d[n] for n in SMALL]) for d in (w, m, v)]
    sm_out = _adamw(sm[0], sm_g, sm[1], sm[2], "adamw_small")
    sm_shapes = [w[n].shape for n in SMALL]
    unpacked = [_unpack(t, sm_shapes) for t in (sm_g,) + tuple(sm_out)]
    for i, n in enumerate(SMALL):
        res[n] = tuple(u[i] for u in unpacked)

    outs = [loss, d_x[None]]
    for k in range(4):
        outs += [res[n][k] for n in NAMES]
    return tuple(outs)
```

```python
import numpy as np
import jax
import jax.numpy as jnp
from jax import lax
from jax.experimental import pallas as pl
from jax.experimental.pallas import tpu as pltpu

F32, BF16 = jnp.float32, jnp.bfloat16
S, D, HD, NH, NKV = 2048, 1024, 64, 16, 4
KVW = NKV * HD
WINDOW = 128
ROT = HD // 4
THETA = 500000.0
EPS = 1e-6
SCALE = HD ** -0.5
LANES = 128
NEG = -1e30
VMEM_LIMIT = 48 * 2 ** 20
ROWS = 256
ATT = 512
SWQ = 4
NCHIP = 4
ADAM_LR, ADAM_B1, ADAM_B2, ADAM_EPS, ADAM_WD, ADAM_STEP = 0.001, 0.9, 0.999, 1e-08, 0.01, 10
NT = (((1,), (1,)), ((), ()))
TN = (((0,), (0,)), ((), ()))
MESH = pl.DeviceIdType.MESH


def _params(n):
    return pltpu.CompilerParams(dimension_semantics=("arbitrary",) * n, vmem_limit_bytes=VMEM_LIMIT)


def _dot(a, b, dims=None):
    if dims is None:
        return jnp.dot(a, b, preferred_element_type=F32)
    return lax.dot_general(a, b, dims, preferred_element_type=F32)


def _dot_split(a, b, n):
    out, rest = None, a
    for _ in range(n):
        hi = rest.astype(BF16)
        term = _dot(hi, b)
        out = term if out is None else out + term
        rest = rest - hi.astype(F32)
    return out


def _seg_mat(w):
    e = (np.arange(w)[:, None] // HD == np.arange(LANES)[None, :]).astype(np.float32)
    return jnp.asarray(e, BF16)


def _spread(r, w):
    head = lax.broadcasted_iota(jnp.int32, (LANES, w), 1) >> 6
    et = jnp.where(head == lax.broadcasted_iota(jnp.int32, (LANES, w), 0), 1.0, 0.0).astype(BF16)
    return _dot_split(r, et, 3)


def _head_rstd(x, e):
    ss = _dot_split(x * x, e, 2)
    return _spread(lax.rsqrt(ss * (1.0 / HD) + EPS), x.shape[1])


def _rope(x, c, a, b):
    w = x.shape[1]
    return x * c + pltpu.roll(x, w - ROT // 2, 1) * a + pltpu.roll(x, ROT // 2, 1) * b


def _rope_t(dy, c, a, b):
    w = dy.shape[1]
    return dy * c + pltpu.roll(dy * b, w - ROT // 2, 1) + pltpu.roll(dy * a, ROT // 2, 1)


def _sigmoid(x):
    return 1.0 / (1.0 + jnp.exp(-x))


def _row_spec(shape, ts):
    nd = len(shape)
    if shape[0] == S:
        return pl.BlockSpec((ts,) + tuple(shape[1:]), lambda i: (i,) + (0,) * (nd - 1))
    return pl.BlockSpec(tuple(shape), lambda i: (0,) * nd)


def _rows_call(body, name, ins, outs, ts=ROWS):
    return pl.pallas_call(
        body, name=name, grid=(S // ts,),
        in_specs=[_row_spec(a.shape, ts) for a in ins],
        out_specs=[_row_spec(s, ts) for s, _ in outs],
        out_shape=[jax.ShapeDtypeStruct(s, d) for s, d in outs],
        compiler_params=_params(1))(*ins)


def _col_spec(ts, w, col):
    return pl.BlockSpec((ts, w), lambda i: (i, col))


TM = TN_ = 512
TM_TOKENS = 1024


def _mm(name, m, n, terms, out_dtype=F32, add=None, tm=None, tn=TN_, stacked=False, riding=None):
    nterm = len(terms)
    if tm is None:
        tm = TM_TOKENS if m == S else TM
    nj, ni_ = n // tn, m // tm
    n_in = 2 * nterm + (add is not None)
    r_in, r_out = (len(riding.ins), len(riding.outs)) if riding is not None else (0, 0)

    def body(*refs):
        if riding is not None:
            j, i = pl.program_id(0), pl.program_id(1)
            at_end = riding.hooks(refs[n_in:n_in + r_in], refs[n_in + r_in + 1:n_in + r_in + 1 + r_out],
                                  *refs[n_in + r_in + 1 + r_out:], first=(j == 0) & (i == 0),
                                  middle=(j == nj // 2) & (i == 0), last=(j == nj - 1) & (i == ni_ - 1))
        acc = None
        for t in range(nterm):
            part = _dot(refs[2 * t][...], refs[2 * t + 1][...], terms[t][4])
            acc = part if acc is None else acc + part
        if add is not None:
            acc = acc + refs[2 * nterm][...]
        refs[n_in + r_in][...] = acc.astype(out_dtype)
        if riding is not None:
            at_end()

    tile = pl.BlockSpec((tm, tn), lambda j, i: (i, j))
    ins, specs = [], []
    for a, a_spec, b, b_spec, _ in terms:
        ins += [a, b]
        specs += [a_spec, b_spec]
    if add is not None:
        ins.append(add)
        specs.append(tile)
    out_spec = pl.BlockSpec((None, tm, tn), lambda j, i: (j, i, 0)) if stacked else tile
    out_shape = jax.ShapeDtypeStruct((nj, m, tn) if stacked else (m, n), out_dtype)
    if riding is None:
        return pl.pallas_call(body, name=name, grid=(nj, ni_), in_specs=specs, out_specs=out_spec,
                              out_shape=out_shape, compiler_params=_params(2))(*ins)
    res = pl.pallas_call(
        body, name=name, grid=(nj, ni_), in_specs=specs + riding.in_specs,
        out_specs=[out_spec] + riding.out_specs, out_shape=[out_shape] + riding.out_shape,
        scratch_shapes=riding.scratch, compiler_params=_params(2))(*ins, *riding.ins)
    return res[0], res[1:]


def _a_rows(k, col=0, tm=TM_TOKENS):
    return pl.BlockSpec((tm, k), lambda j, i: (i, col))


def _a_cols(k, tm=TM):
    return pl.BlockSpec((k, tm), lambda j, i: (0, i))


def _b_cols(k, row=0, col0=0, tn=TN_):
    return pl.BlockSpec((k, tn), lambda j, i: (row, col0 + j))


def _b_rows(k, row0=0, tn=TN_):
    return pl.BlockSpec((tn, k), lambda j, i: (row0 + j, 0))


def _rmsnorm_fwd(x, gains, name):
    def body(*refs):
        xv = refs[0][...]
        r = lax.rsqrt(jnp.mean(xv * xv, axis=-1, keepdims=True) + EPS)
        xh = xv * r
        for n in range(len(gains)):
            refs[1 + len(gains) + n][...] = (xh * refs[1 + n][...]).astype(BF16)

    return _rows_call(body, name, [x] + list(gains), [((S, D), BF16)] * len(gains))


def _rmsnorm_bwd(x, gains, dus, dres, name):
    n = len(gains)

    def body(*refs):
        x_ref, g_refs, du_refs, dres_ref = refs[0], refs[1:1 + n], refs[1 + n:1 + 2 * n], refs[1 + 2 * n]
        dx_ref, dxb_ref, dg_refs = refs[2 + 2 * n], refs[3 + 2 * n], refs[4 + 2 * n:]
        xv = x_ref[...]
        r = lax.rsqrt(jnp.mean(xv * xv, axis=-1, keepdims=True) + EPS)
        xh = xv * r
        gy = None
        for m in range(n):
            du = du_refs[m][...]
            part = jnp.sum(du * xh, axis=0, keepdims=True)

            @pl.when(pl.program_id(0) == 0)
            def _(m=m, part=part):
                dg_refs[m][...] = part

            @pl.when(pl.program_id(0) != 0)
            def _(m=m, part=part):
                dg_refs[m][...] += part

            t = du * g_refs[m][...]
            gy = t if gy is None else gy + t
        dx = dres_ref[...] + r * (gy - xh * jnp.mean(gy * xh, axis=-1, keepdims=True))
        dx_ref[...] = dx
        dxb_ref[...] = dx.astype(BF16)

    outs = [((S, D), F32), ((S, D), BF16)] + [((1, D), F32)] * n
    return _rows_call(body, name, [x] + list(gains) + list(dus) + [dres], outs)


def _a_post(qkvg, qg, kg):
    e = _seg_mat(D)

    def body(q_ref, k_ref, v_ref, qg_ref, kg_ref, e_ref, qo, ko, vo):
        ev = e_ref[...]
        qv, kv = q_ref[...], k_ref[...]
        qo[...] = (qv * _head_rstd(qv, ev) * qg_ref[...] * SCALE).astype(BF16)
        ko[...] = (kv * _head_rstd(kv, ev) * kg_ref[...]).astype(BF16)
        vo[...] = v_ref[...].astype(BF16)

    whole = lambda a: pl.BlockSpec(a.shape, lambda i: (0, 0))
    return pl.pallas_call(
        body, name="a_post", grid=(S // ROWS,),
        in_specs=[_col_spec(ROWS, D, 0), _col_spec(ROWS, D, 1), _col_spec(ROWS, D, 2),
                  whole(qg), whole(kg), whole(e)],
        out_specs=[_col_spec(ROWS, D, 0)] * 3,
        out_shape=[jax.ShapeDtypeStruct((S, D), BF16)] * 3,
        compiler_params=_params(1))(qkvg, qkvg, qkvg, qg, kg, e)


def _tri(upper):
    r, c = np.arange(ROWS)[:, None], np.arange(ROWS)[None, :]
    return jnp.asarray((r <= c) if upper else (r >= c), BF16)


def _forget_cumsum(fpad, bpad):
    def body(f_ref, b_ref, u_ref, c_ref, carry):
        @pl.when(pl.program_id(0) == 0)
        def _():
            carry[...] = jnp.zeros_like(carry)

        lf = jax.nn.log_sigmoid(f_ref[...] + b_ref[...])
        blk = _dot_split(lf.T, u_ref[...], 3) + carry[:, 0:1]
        c_ref[...] = blk
        carry[...] = jnp.broadcast_to(blk[:, ROWS - 1:ROWS], carry.shape)

    return pl.pallas_call(
        body, name="forget_cumsum", grid=(S // ROWS,),
        in_specs=[pl.BlockSpec((ROWS, LANES), lambda i: (i, 0)), pl.BlockSpec((1, LANES), lambda i: (0, 0)),
                  pl.BlockSpec((ROWS, ROWS), lambda i: (0, 0))],
        out_specs=pl.BlockSpec((LANES, ROWS), lambda i: (0, i)),
        out_shape=jax.ShapeDtypeStruct((LANES, S), F32),
        scratch_shapes=[pltpu.VMEM((LANES, LANES), F32)],
        compiler_params=_params(1))(fpad, bpad, _tri(True))


def _forget_bwd(dct, fpad, bpad):
    nb = S // ROWS

    def body(dc_ref, f_ref, b_ref, l_ref, df_ref, db_ref, carry):
        @pl.when(pl.program_id(0) == 0)
        def _():
            carry[...] = jnp.zeros_like(carry)
            db_ref[...] = jnp.zeros_like(db_ref)

        blk = _dot_split(dc_ref[...], l_ref[...], 3) + carry[:, 0:1]
        carry[...] = jnp.broadcast_to(blk[:, 0:1], carry.shape)
        df = blk.T * _sigmoid(-(f_ref[...] + b_ref[...]))
        df_ref[...] = df.astype(BF16)
        db_ref[...] += jnp.sum(df, axis=0, keepdims=True)

    return pl.pallas_call(
        body, name="forget_bwd", grid=(nb,),
        in_specs=[pl.BlockSpec((LANES, ROWS), lambda i: (0, nb - 1 - i)),
                  pl.BlockSpec((ROWS, LANES), lambda i: (nb - 1 - i, 0)),
                  pl.BlockSpec((1, LANES), lambda i: (0, 0)), pl.BlockSpec((ROWS, ROWS), lambda i: (0, 0))],
        out_specs=[pl.BlockSpec((ROWS, LANES), lambda i: (nb - 1 - i, 0)), pl.BlockSpec((1, LANES), lambda i: (0, 0))],
        out_shape=[jax.ShapeDtypeStruct((S, LANES), BF16), jax.ShapeDtypeStruct((1, LANES), F32)],
        scratch_shapes=[pltpu.VMEM((LANES, LANES), F32)],
        compiler_params=_params(1))(dct, fpad, bpad, _tri(False))


def _gate_fwd(o, proj, col, name):
    def body(o_ref, g_ref, y_ref):
        g = g_ref[...]
        y_ref[...] = (o_ref[...] * (g * _sigmoid(g))).astype(BF16)

    return pl.pallas_call(
        body, name=name, grid=(S // ROWS,),
        in_specs=[_col_spec(ROWS, D, 0), _col_spec(ROWS, D, col)],
        out_specs=_col_spec(ROWS, D, 0), out_shape=jax.ShapeDtypeStruct((S, D), BF16),
        compiler_params=_params(1))(o, proj)


def _gate_bwd(dy, o, proj, col, name):
    def body(dy_ref, o_ref, g_ref, do_ref, dg_ref):
        g, dyv = g_ref[...], dy_ref[...]
        sg = _sigmoid(g)
        do_ref[...] = dyv * (g * sg)
        dg_ref[...] = (dyv * o_ref[...] * (sg * (1.0 + g * (1.0 - sg)))).astype(BF16)

    return pl.pallas_call(
        body, name=name, grid=(S // ROWS,),
        in_specs=[_col_spec(ROWS, D, 0), _col_spec(ROWS, D, 0), _col_spec(ROWS, D, col)],
        out_specs=[_col_spec(ROWS, D, 0)] * 2,
        out_shape=[jax.ShapeDtypeStruct((S, D), F32), jax.ShapeDtypeStruct((S, D), BF16)],
        compiler_params=_params(1))(dy, o, proj)


def _headnorm_bwd(x, col, gain, dy, rope, name):
    e = _seg_mat(D)
    tabs = list(rope) if rope is not None else []

    def body(*refs):
        x_ref, g_ref, dy_ref, e_ref = refs[:4]
        dx_ref, dg_ref = refs[-2:]
        xv, dyv, ev = x_ref[...], dy_ref[...], e_ref[...]
        if rope is not None:
            c, a, b = (jnp.tile(t[...], (1, D // LANES)) for t in refs[4:7])
            dyv = _rope_t(dyv, c, a, b)
        r = _head_rstd(xv, ev)
        xh = xv * r
        part = jnp.sum(dyv * xh, axis=0, keepdims=True)

        @pl.when(pl.program_id(0) == 0)
        def _():
            dg_ref[...] = part

        @pl.when(pl.program_id(0) != 0)
        def _():
            dg_ref[...] += part

        gy = dyv * g_ref[...]
        seg = _spread(_dot_split(gy * xh, ev, 2) * (1.0 / HD), D)
        dx_ref[...] = (r * (gy - xh * seg)).astype(BF16)

    whole = lambda a: pl.BlockSpec(a.shape, lambda i: (0, 0))
    return pl.pallas_call(
        body, name=name, grid=(S // ROWS,),
        in_specs=[_col_spec(ROWS, D, col), whole(gain), _col_spec(ROWS, D, 0), whole(e)]
                 + [pl.BlockSpec((ROWS, LANES), lambda i: (i, 0))] * len(tabs),
        out_specs=[_col_spec(ROWS, D, 0), whole(gain)],
        out_shape=[jax.ShapeDtypeStruct((S, D), BF16), jax.ShapeDtypeStruct((1, D), F32)],
        compiler_params=_params(1))(x, gain, dy, e, *tabs)


def _dup_mat():
    r, c = np.arange(KVW)[:, None], np.arange(2 * KVW)[None, :]
    return (r // HD == c // LANES) & (r % HD == c % HD)


def _fold_mat():
    r, c = np.arange(D)[:, None], np.arange(KVW)[None, :]
    return (r // (2 * LANES) == c // HD) & (r % HD == c % HD)


def _b_post(pb, kv, qg, kg, rope):
    e, ek = _seg_mat(D), _seg_mat(KVW)
    dup = jnp.asarray(_dup_mat(), BF16)

    def body(q_ref, k_ref, v_ref, qg_ref, kg_ref, e_ref, ek_ref, dup_ref, c_ref, a_ref, b_ref, qo, ko, vo):
        c1, a1, b1 = c_ref[...], a_ref[...], b_ref[...]
        qv = q_ref[...]
        qn = qv * _head_rstd(qv, e_ref[...]) * qg_ref[...]
        t = lambda z, n: jnp.tile(z, (1, n))
        qo[...] = (_rope(qn, t(c1, D // LANES), t(a1, D // LANES), t(b1, D // LANES)) * SCALE).astype(BF16)
        kvv = k_ref[...]
        kn = kvv * _head_rstd(kvv, ek_ref[...]) * kg_ref[...]
        kr = _rope(kn, t(c1, KVW // LANES), t(a1, KVW // LANES), t(b1, KVW // LANES)).astype(BF16)
        ko[...] = _dot(kr, dup_ref[...]).astype(BF16)
        vo[...] = _dot(v_ref[...].astype(BF16), dup_ref[...]).astype(BF16)

    whole = lambda a: pl.BlockSpec(a.shape, lambda i: (0, 0))
    tab = pl.BlockSpec((ROWS, LANES), lambda i: (i, 0))
    return pl.pallas_call(
        body, name="b_post", grid=(S // ROWS,),
        in_specs=[_col_spec(ROWS, D, 0), _col_spec(ROWS, KVW, 0), _col_spec(ROWS, KVW, 1),
                  whole(qg), whole(kg), whole(e), whole(ek), whole(dup), tab, tab, tab],
        out_specs=[_col_spec(ROWS, D, 0), _col_spec(ROWS, 2 * KVW, 0), _col_spec(ROWS, 2 * KVW, 0)],
        out_shape=[jax.ShapeDtypeStruct((S, D), BF16), jax.ShapeDtypeStruct((S, 2 * KVW), BF16),
                   jax.ShapeDtypeStruct((S, 2 * KVW), BF16)],
        compiler_params=_params(1))(pb, kv, kv, qg, kg, e, ek, dup, *rope)


def _kv_bwd(dkdup, dvdup, kv, kg, rope):
    ek = _seg_mat(KVW)
    fold = jnp.asarray(_fold_mat(), BF16)

    def body(dk_ref, dv_ref, k_ref, kg_ref, ek_ref, fold_ref, c_ref, a_ref, b_ref, dkv_ref, dg_ref):
        ev, fv = ek_ref[...], fold_ref[...]
        t = lambda z: jnp.tile(z[...], (1, KVW // LANES))
        dk = _rope_t(_dot_split(dk_ref[...], fv, 3), t(c_ref), t(a_ref), t(b_ref))
        dv = _dot_split(dv_ref[...], fv, 3)
        xv = k_ref[...]
        r = _head_rstd(xv, ev)
        xh = xv * r
        part = jnp.sum(dk * xh, axis=0, keepdims=True)

        @pl.when(pl.program_id(0) == 0)
        def _():
            dg_ref[...] = part

        @pl.when(pl.program_id(0) != 0)
        def _():
            dg_ref[...] += part

        gy = dk * kg_ref[...]
        seg = _spread(_dot_split(gy * xh, ev, 2) * (1.0 / HD), KVW)
        dkv_ref[:, 0:KVW] = (r * (gy - xh * seg)).astype(BF16)
        dkv_ref[:, KVW:2 * KVW] = dv.astype(BF16)

    whole = lambda a: pl.BlockSpec(a.shape, lambda i: (0, 0))
    tab = pl.BlockSpec((ROWS, LANES), lambda i: (i, 0))
    return pl.pallas_call(
        body, name="kv_bwd", grid=(S // ROWS,),
        in_specs=[_col_spec(ROWS, D, 0), _col_spec(ROWS, D, 0), _col_spec(ROWS, KVW, 0),
                  whole(kg), whole(ek), whole(fold), tab, tab, tab],
        out_specs=[_col_spec(ROWS, 2 * KVW, 0), whole(kg)],
        out_shape=[jax.ShapeDtypeStruct((S, 2 * KVW), BF16), jax.ShapeDtypeStruct((1, KVW), F32)],
        compiler_params=_params(1))(dkdup, dvdup, kv, kg, ek, fold, *rope)


def _loss_head(out, target):
    def body(o_ref, t_ref, d_ref, db_ref, l_ref):
        diff = o_ref[...] - t_ref[...]
        d = diff * (1.0 / D)
        d_ref[...] = d
        db_ref[...] = d.astype(BF16)

        @pl.when(pl.program_id(0) == 0)
        def _():
            l_ref[...] = jnp.zeros_like(l_ref)

        l_ref[...] += jnp.sum(diff * diff, axis=0, keepdims=True)

    return _rows_call(body, "loss_head", [out, target], [((S, D), F32), ((S, D), BF16), ((1, D), F32)])


def _lane():
    return lax.broadcasted_iota(jnp.int32, (1, LANES), 1)


def _head_mask(hh):
    return (_lane() < HD) if hh == 0 else (_lane() >= HD)


def _fox_fwd(q, k, v, ct, riding):
    nq, npair = S // ATT, NH // 2
    ni, no = len(riding.ins), len(riding.outs)

    def body(q_ref, k_ref, v_ref, c_ref, *rest):
        o_ref, lse_ref = rest[ni:ni + 2]
        pair, i = pl.program_id(0), pl.program_id(1)
        at_end = riding.hooks(rest[:ni], rest[ni + 2:ni + 2 + no], *rest[ni + 2 + no:],
                              first=(pair == 0) & (i == 0), middle=(pair == npair // 2) & (i == 0),
                              last=(pair == npair - 1) & (i == nq - 1))
        q2 = q_ref[...]
        qms = [jnp.where(_head_mask(hh), q2, jnp.zeros_like(q2)) for hh in (0, 1)]

        def probs(off, width, m, hh, diag):
            s = _dot(qms[hh], k_ref[pl.ds(off, width), :], NT) - c_ref[hh:hh + 1, pl.ds(off, width)]
            if diag:
                row = i * ATT + lax.broadcasted_iota(jnp.int32, (ATT, width), 0)
                col = off + lax.broadcasted_iota(jnp.int32, (ATT, width), 1)
                s = jnp.where(col <= row, s, NEG)
            m_new = jnp.maximum(m, jnp.max(s, axis=1, keepdims=True))
            p = jnp.exp(s - m_new)
            p_hi = p.astype(BF16)
            return m_new, jnp.exp(m - m_new), p_hi, (p - p_hi.astype(F32)).astype(BF16)

        def weighted(off, width, p_hi, p_lo, hh):
            vj = v_ref[pl.ds(off, width), :]
            v1 = jnp.where(_head_mask(hh), vj, jnp.ones_like(vj))
            return _dot(p_hi, v1) + _dot(p_lo, v1)

        def step(off, width, carry, diag):
            off = pl.multiple_of(off, ATT)
            out = []
            for hh in (0, 1):
                m, acc = carry[hh]
                m, alpha, p_hi, p_lo = probs(off, width, m, hh, diag)
                out.append((m, alpha * acc + weighted(off, width, p_hi, p_lo, hh)))
            return tuple(out)

        one = (jnp.full((ATT, 1), NEG, F32), jnp.zeros((ATT, LANES), F32))
        carry = lax.fori_loop(0, i // 2, lambda j, cr: step(j * (2 * ATT), 2 * ATT, cr, False), (one, one))
        carry = lax.cond(i % 2 == 1, lambda cr: step((i - 1) * ATT, 2 * ATT, cr, True),
                         lambda cr: step(i * ATT, ATT, cr, True), carry)
        res = []
        for hh in (0, 1):
            m, acc = carry[hh]
            l = jnp.max(jnp.where(_head_mask(1 - hh), acc, 0.0), axis=1, keepdims=True)
            res.append((acc / l, m + jnp.log(l)))
        first = _head_mask(0)
        o_ref[...] = jnp.where(first, res[0][0], res[1][0])
        lse_ref[...] = jnp.where(first, res[0][1], res[1][1])
        at_end()

    blk = pl.BlockSpec((ATT, LANES), lambda p, i: (i, p))
    full = pl.BlockSpec((S, LANES), lambda p, i: (0, p))
    res = pl.pallas_call(
        body, name="fox_fwd", grid=(npair, nq),
        in_specs=[blk, full, full, pl.BlockSpec((None, 2, S), lambda p, i: (p, 0, 0))] + riding.in_specs,
        out_specs=[blk, blk] + riding.out_specs,
        out_shape=[jax.ShapeDtypeStruct((S, D), F32)] * 2 + riding.out_shape,
        scratch_shapes=riding.scratch,
        compiler_params=_params(2))(q, k, v, ct, *riding.ins)
    return res[0], res[1], res[2:]


def _fox_bwd(q, k, v, ct, o, lse, do, riding):
    nq, npair = S // ATT, NH // 2
    ni, no = len(riding.ins), len(riding.outs)

    def body(q_ref, k_ref, v_ref, c_ref, o_ref, lse_ref, do_ref, *rest):
        dq_ref, dk_ref, dvb_ref, dc_ref = rest[ni:ni + 4]
        dv_ref = rest[ni + 4 + no]
        pair, i = pl.program_id(0), pl.program_id(1)
        at_end = riding.hooks(rest[:ni], rest[ni + 4:ni + 4 + no], *rest[ni + 5 + no:],
                              first=(pair == 0) & (i == 0), middle=(pair == npair // 2) & (i == 0),
                              last=(pair == npair - 1) & (i == nq - 1))

        @pl.when(i == 0)
        def _():
            dk_ref[...] = jnp.zeros_like(dk_ref)
            dv_ref[...] = jnp.zeros_like(dv_ref)
            dc_ref[...] = jnp.zeros_like(dc_ref)

        q2, do2, lse2 = q_ref[...], do_ref[...], lse_ref[...]
        do2b = do2.astype(BF16)
        prod = do2b.astype(F32) * o_ref[...]
        heads = []
        for hh in (0, 1):
            hm = _head_mask(hh)
            heads.append((jnp.where(hm, q2, jnp.zeros_like(q2)), jnp.where(hm, do2b, jnp.zeros_like(do2b)),
                          jnp.sum(jnp.where(hm, prod, 0.0), axis=1, keepdims=True),
                          jnp.max(jnp.where(hm, lse2, NEG), axis=1, keepdims=True)))

        def step(off, width, dqs, diag):
            off = pl.multiple_of(off, ATT)
            kj, vj = k_ref[pl.ds(off, width), :], v_ref[pl.ds(off, width), :]
            dk, dv, out = None, None, []
            for hh in (0, 1):
                qm, dom, delta, lse_h = heads[hh]
                s = _dot(qm, kj, NT) - c_ref[hh:hh + 1, pl.ds(off, width)]
                p = jnp.exp(s - lse_h)
                if diag:
                    row = i * ATT + lax.broadcasted_iota(jnp.int32, (ATT, width), 0)
                    col = off + lax.broadcasted_iota(jnp.int32, (ATT, width), 1)
                    p = jnp.where(col <= row, p, 0.0)
                ds = p * (_dot(dom, vj, NT) - delta)
                dc_ref[hh:hh + 1, pl.ds(off, width)] += -jnp.sum(ds, axis=0, keepdims=True)
                dsb = ds.astype(BF16)
                dk_h, dv_h = _dot(dsb, qm, TN), _dot(p.astype(BF16), dom, TN)
                dk, dv = (dk_h, dv_h) if dk is None else (dk + dk_h, dv + dv_h)
                out.append(dqs[hh] + _dot(dsb, kj))
            dk_ref[pl.ds(off, width), :] += dk
            dv_ref[pl.ds(off, width), :] += dv
            return tuple(out)

        zero = jnp.zeros((ATT, LANES), F32)
        dqs = lax.fori_loop(0, i // 2, lambda j, acc: step(j * (2 * ATT), 2 * ATT, acc, False), (zero, zero))
        dqs = lax.cond(i % 2 == 1, lambda acc: step((i - 1) * ATT, 2 * ATT, acc, True),
                       lambda acc: step(i * ATT, ATT, acc, True), dqs)
        dq_ref[...] = jnp.where(_head_mask(0), dqs[0], dqs[1]) * SCALE

        @pl.when(i == nq - 1)
        def _():
            dvb_ref[...] = dv_ref[...].astype(BF16)

        at_end()

    blk = pl.BlockSpec((ATT, LANES), lambda p, i: (i, p))
    full = pl.BlockSpec((S, LANES), lambda p, i: (0, p))
    cspec = pl.BlockSpec((None, 2, S), lambda p, i: (p, 0, 0))
    res = pl.pallas_call(
        body, name="fox_bwd", grid=(npair, nq),
        in_specs=[blk, full, full, cspec, blk, blk, blk] + riding.in_specs,
        out_specs=[blk, full, full, cspec] + riding.out_specs,
        out_shape=[jax.ShapeDtypeStruct((S, D), F32)] * 2 + [jax.ShapeDtypeStruct((S, D), BF16),
                                                              jax.ShapeDtypeStruct((npair, 2, S), F32)]
                  + riding.out_shape,
        scratch_shapes=[pltpu.VMEM((S, LANES), F32)] + riding.scratch,
        compiler_params=_params(2))(q, k, v, ct, o, lse, do, *riding.ins)
    return res[0], res[1], res[2], res[3], res[4:]


def _both_heads(x):
    return jnp.concatenate([jnp.where(_head_mask(hh), x, jnp.zeros_like(x)) for hh in (0, 1)], axis=0)


def _per_head(col0, col1):
    return jnp.concatenate([jnp.broadcast_to(col0, (WINDOW, 1)), jnp.broadcast_to(col1, (WINDOW, 1))], axis=0)


def _unstack(x2):
    return jnp.where(_head_mask(0), x2[:WINDOW], x2[WINDOW:])


def _swa_valid(i, start):
    r = lax.broadcasted_iota(jnp.int32, (2 * WINDOW, 2 * WINDOW), 0)
    qabs = i * WINDOW + jnp.where(r >= WINDOW, r - WINDOW, r)
    kabs = start + lax.broadcasted_iota(jnp.int32, (2 * WINDOW, 2 * WINDOW), 1)
    return (kabs <= qabs) & (qabs - kabs < WINDOW)


def _swa_fwd(q, kdup, vdup, sinks_t):
    def body(q_ref, k_ref, v_ref, sk_ref, o_ref, lse_ref):
        skv = sk_ref[...]
        first = _head_mask(0)
        for sb in range(SWQ):
            i = pl.program_id(1) * SWQ + sb
            rows = slice(sb * WINDOW, (sb + 1) * WINDOW)
            start = pl.multiple_of(jnp.maximum(i - 1, 0) * WINDOW, WINDOW)
            kk, vv = k_ref[pl.ds(start, 2 * WINDOW), :], v_ref[pl.ds(start, 2 * WINDOW), :]
            q2 = q_ref[rows, :]
            valid = _swa_valid(i, start)[:WINDOW]
            res = []
            for hh in (0, 1):
                hm = _head_mask(hh)
                sink = jnp.max(jnp.where(hm, skv, NEG), axis=1, keepdims=True)
                s = jnp.where(valid, _dot(jnp.where(hm, q2, jnp.zeros_like(q2)), kk, NT), NEG)
                m = jnp.maximum(jnp.max(s, axis=1, keepdims=True), sink)
                p = jnp.exp(s - m)
                l = jnp.sum(p, axis=1, keepdims=True) + jnp.exp(sink - m)
                res.append((_dot(p.astype(BF16), vv) / l, m + jnp.log(l)))
            o_ref[rows, :] = jnp.where(first, res[0][0], res[1][0])
            lse_ref[rows, :] = jnp.where(first, res[0][1], res[1][1])

    blk = pl.BlockSpec((SWQ * WINDOW, LANES), lambda p, i: (i, p))
    full = pl.BlockSpec((S, LANES), lambda p, i: (0, p // 2))
    return pl.pallas_call(
        body, name="swa_fwd", grid=(NH // 2, S // (SWQ * WINDOW)),
        in_specs=[blk, full, full, pl.BlockSpec((1, LANES), lambda p, i: (0, p))],
        out_specs=[blk, blk],
        out_shape=[jax.ShapeDtypeStruct((S, D), F32)] * 2,
        compiler_params=_params(2))(q, kdup, vdup, sinks_t)


def _swa_bwd(q, kdup, vdup, sinks_t, o, lse, do):
    def body(q_ref, k_ref, v_ref, sk_ref, o_ref, lse_ref, do_ref, dq_ref, dk_ref, dv_ref, dsk_ref):
        @pl.when(pl.program_id(1) == 0)
        def _():
            dk_ref[...] = jnp.zeros_like(dk_ref)
            dv_ref[...] = jnp.zeros_like(dv_ref)
            dsk_ref[...] = jnp.zeros_like(dsk_ref)

        skv = sk_ref[...]
        first = _head_mask(0)
        sink = _per_head(*[jnp.max(jnp.where(_head_mask(hh), skv, NEG), axis=1, keepdims=True) for hh in (0, 1)])
        for sb in range(SWQ):
            i = pl.program_id(1) * SWQ + sb
            rows = slice(sb * WINDOW, (sb + 1) * WINDOW)
            start = pl.multiple_of(jnp.maximum(i - 1, 0) * WINDOW, WINDOW)
            kk, vv = k_ref[pl.ds(start, 2 * WINDOW), :], v_ref[pl.ds(start, 2 * WINDOW), :]
            do2b = do_ref[rows, :].astype(BF16)
            prod, lse2 = do2b.astype(F32) * o_ref[rows, :], lse_ref[rows, :]
            qs, dos = _both_heads(q_ref[rows, :]), _both_heads(do2b)
            delta = jnp.concatenate([jnp.sum(jnp.where(_head_mask(hh), prod, 0.0), axis=1, keepdims=True)
                                     for hh in (0, 1)], axis=0)
            lse_h = jnp.concatenate([jnp.max(jnp.where(_head_mask(hh), lse2, NEG), axis=1, keepdims=True)
                                     for hh in (0, 1)], axis=0)
            p = jnp.where(_swa_valid(i, start), jnp.exp(_dot(qs, kk, NT) - lse_h), 0.0)
            dsb = (p * (_dot(dos, vv, NT) - delta)).astype(BF16)
            dk_ref[pl.ds(start, 2 * WINDOW), :] += _dot(dsb, qs, TN)
            dv_ref[pl.ds(start, 2 * WINDOW), :] += _dot(p.astype(BF16), dos, TN)
            dq_ref[rows, :] = _unstack(_dot(dsb, kk)) * SCALE
            t = jnp.exp(sink - lse_h) * delta
            dsk_ref[...] += -jnp.where(first, jnp.sum(t[:WINDOW], axis=0, keepdims=True),
                                       jnp.sum(t[WINDOW:], axis=0, keepdims=True))

    blk = pl.BlockSpec((SWQ * WINDOW, LANES), lambda p, i: (i, p))
    full = pl.BlockSpec((S, LANES), lambda p, i: (0, p // 2))
    acc = pl.BlockSpec((S, LANES), lambda p, i: (0, p))
    sk = pl.BlockSpec((1, LANES), lambda p, i: (0, p))
    return pl.pallas_call(
        body, name="swa_bwd", grid=(NH // 2, S // (SWQ * WINDOW)),
        in_specs=[blk, full, full, sk, blk, blk, blk],
        out_specs=[blk, acc, acc, sk],
        out_shape=[jax.ShapeDtypeStruct((S, D), F32)] * 3 + [jax.ShapeDtypeStruct((1, D), F32)],
        compiler_params=_params(2))(q, kdup, vdup, sinks_t, o, lse, do)


def _adamw_math(w, g, m, v):
    m = ADAM_B1 * m + (1.0 - ADAM_B1) * g
    v = ADAM_B2 * v + (1.0 - ADAM_B2) * jnp.square(g)
    m_hat = m / (1.0 - ADAM_B1 ** ADAM_STEP)
    v_hat = v / (1.0 - ADAM_B2 ** ADAM_STEP)
    delta = -ADAM_LR * (m_hat / (jnp.sqrt(v_hat) + ADAM_EPS) + ADAM_WD * w)
    return delta, m, v


def _adamw(w, g, m, v, name):
    r, c = w.shape
    tr = min(r, 128)

    def body(w_ref, g_ref, m_ref, v_ref, d_ref, mo_ref, vo_ref):
        d_ref[...], mo_ref[...], vo_ref[...] = _adamw_math(w_ref[...], g_ref[...], m_ref[...], v_ref[...])

    spec = pl.BlockSpec((tr, c), lambda i: (i, 0))
    return pl.pallas_call(
        body, name=name, grid=(r // tr,), in_specs=[spec] * 4, out_specs=[spec] * 3,
        out_shape=[jax.ShapeDtypeStruct((r, c), F32)] * 3, compiler_params=_params(1))(w, g, m, v)


SUM_TILE = 128


FLAT_BLOCK = 257 * 1024


def _tiles(shape, axis, lead=0):
    if len(shape) == 1:
        count = shape[0] // FLAT_BLOCK
        return (FLAT_BLOCK,), count, lambda pos, *lead_idx: (sum(k * count for k in lead_idx) + pos,)
    r, c = shape
    blk = (SUM_TILE, c) if axis == 0 else (r, SUM_TILE)
    count = shape[axis] // SUM_TILE

    def index(pos, *lead_idx):
        return tuple(lead_idx) + ((pos, 0) if axis == 0 else (0, pos))

    return (None,) * lead + blk, count, index


def _adamw_halves(w, g_mine, g_theirs, m, v, axis, name):
    blk, count, index = _tiles(w.shape, axis)
    per_half = count // 2

    def body(w_ref, a_ref, b_ref, m_ref, v_ref, g_ref, d_ref, mo_ref, vo_ref):
        is_mine = pl.program_id(0) // per_half == lax.axis_index("c")
        g = jnp.where(is_mine, a_ref[...], b_ref[...])
        g_ref[...] = g
        d_ref[...], mo_ref[...], vo_ref[...] = _adamw_math(w_ref[...], g, m_ref[...], v_ref[...])

    spec = pl.BlockSpec(blk, lambda i: index(i))
    half = pl.BlockSpec(blk, lambda i: index(i % per_half))
    return pl.pallas_call(
        body, name=name, grid=(count,), in_specs=[spec, half, half, spec, spec], out_specs=[spec] * 4,
        out_shape=[jax.ShapeDtypeStruct(w.shape, F32)] * 4, compiler_params=_params(1))(w, g_mine, g_theirs, m, v)


def _chip_sum(blocks, from_sibling, axis, name):
    flat = blocks.ndim == 1
    blk, count, index = _tiles((from_sibling.shape[0] // NCHIP,) if flat else from_sibling.shape[1:], axis, lead=1)

    def body(lo_ref, hi_ref, p_ref, o32, o16):
        mine = jnp.where(lax.axis_index("c") == 0, lo_ref[...], hi_ref[...])
        acc = mine + p_ref[...]
        o32[...] = acc
        o16[...] = acc.astype(BF16)

    half = pl.BlockSpec(blk, lambda k, i: index(i, k))
    if flat:
        lo = pl.BlockSpec(blk, lambda k, i: (2 * count * k + i,))
        hi = pl.BlockSpec(blk, lambda k, i: (2 * count * k + count + i,))
    else:
        lo, hi = half, pl.BlockSpec(blk, lambda k, i: index(i + count, k))
    return pl.pallas_call(
        body, name=name, grid=(NCHIP, count), in_specs=[lo, hi, half], out_specs=[half, half],
        out_shape=[jax.ShapeDtypeStruct(from_sibling.shape, F32), jax.ShapeDtypeStruct(from_sibling.shape, BF16)],
        compiler_params=_params(2))(blocks, blocks, from_sibling)


def _mesh_sum(own, parts, axis, name):
    blk, count, index = _tiles(own.shape, axis)
    n = NCHIP - 1

    def body(a_ref, *refs):
        acc = a_ref[...]
        for k in range(n):
            acc = acc + refs[k][...].astype(F32)
        refs[n][...] = acc

    spec = pl.BlockSpec(blk, lambda i: index(i))
    if own.ndim == 1:
        part = [pl.BlockSpec(blk, lambda i, k=k: (k * count + i,)) for k in range(n)]
    else:
        part = [pl.BlockSpec((None,) + blk, lambda i, k=k: (k,) + index(i)) for k in range(n)]
    return pl.pallas_call(
        body, name=name, grid=(count,), in_specs=[spec] + part,
        out_specs=spec, out_shape=jax.ShapeDtypeStruct(own.shape, F32),
        compiler_params=_params(1))(own, *([parts] * n))


def _sum_stack(parts, name):
    n = parts.shape[0]

    def body(p_ref, o_ref):
        acc = p_ref[0]
        for k in range(1, n):
            acc = acc + p_ref[k]
        o_ref[...] = acc

    return pl.pallas_call(body, name=name, out_shape=jax.ShapeDtypeStruct(parts.shape[1:], F32))(parts)


def _coords():
    return lax.axis_index("x"), lax.axis_index("y"), lax.axis_index("c")


def _chip(who):
    return 2 * who[0] + who[1]


def _flip(who, mask):
    return tuple((1 - v) if b else v for v, b in zip(who, mask))


def _transfer(transfers, t, I, O, ssem, rsem, receiving):
    tr, me = transfers[t], _coords()
    peer = _flip(me, tr["mask"])
    return pltpu.make_async_remote_copy(
        src_ref=tr["src"](I, O, me), dst_ref=tr["dst"](I, O, peer if receiving else me),
        send_sem=ssem.at[t], recv_sem=rsem.at[t], device_id=peer, device_id_type=MESH)


def _start_transfers(transfers, I, O, ssem, rsem, onward):
    arrived = set()
    for t, tr in enumerate(transfers):
        after = tr.get("after")
        if (after is not None) != onward:
            continue
        if after is not None and after not in arrived:
            _transfer(transfers, after, I, O, ssem, rsem, True).wait_recv()
            arrived.add(after)
        _transfer(transfers, t, I, O, ssem, rsem, False).start()


def _finish_transfers(transfers, I, O, ssem, rsem):
    passed_on = {tr["after"] for tr in transfers if tr.get("after") is not None}
    for t in range(len(transfers)):
        if t not in passed_on:
            _transfer(transfers, t, I, O, ssem, rsem, True).wait_recv()
    for t in range(len(transfers)):
        _transfer(transfers, t, I, O, ssem, rsem, False).wait_send()


def _own_copies(own, I, O, stage, lsem, leg):
    for n, (src, dst) in enumerate(own):
        me = _coords()
        bring =pltpu.make_async_copy(src(I, O, me), stage[n], lsem.at[2 * n])
        put = pltpu.make_async_copy(stage[n], dst(I, O, me), lsem.at[2 * n + 1])
        if leg == 0:
            bring.start()
        elif leg == 1:
            bring.wait()
            put.start()
        else:
            put.wait()


def _own_scratch(own, ins):
    return [pltpu.VMEM(ins[n].shape, ins[n].dtype) for n in range(len(own))], pltpu.SemaphoreType.DMA((max(2 * len(own), 1),))


def _exchange(name, ins, outs, transfers, own=()):
    ni, no = len(ins), len(outs)
    nt = len(transfers)
    stages, stage_sems = _own_scratch(own, ins)

    def body(*refs):
        I, O = refs[:ni], refs[ni:ni + no]
        ssem, rsem, lsem = refs[ni + no:ni + no + 3]
        stage = refs[ni + no + 3:]
        _own_copies(own, I, O, stage, lsem, 0)
        _start_transfers(transfers, I, O, ssem, rsem, False)
        _own_copies(own, I, O, stage, lsem, 1)
        _start_transfers(transfers, I, O, ssem, rsem, True)
        _finish_transfers(transfers, I, O, ssem, rsem)
        _own_copies(own, I, O, stage, lsem, 2)

    hbm = pl.BlockSpec(memory_space=pltpu.HBM)
    return pl.pallas_call(
        body, name=name, in_specs=[hbm] * ni, out_specs=[hbm] * no,
        out_shape=[jax.ShapeDtypeStruct(s, d) for s, d in outs],
        scratch_shapes=[pltpu.SemaphoreType.DMA((nt,)), pltpu.SemaphoreType.DMA((nt,)), stage_sems] + stages,
        compiler_params=pltpu.CompilerParams(has_side_effects=True, vmem_limit_bytes=VMEM_LIMIT))(*ins)


CHIP_MASKS = [(0, 1, 0), (1, 0, 0), (1, 1, 0)]
SIBLING = (0, 0, 1)


def _half(shape2d, axis, which):
    n = shape2d[axis] // 2
    cut = pl.ds(pl.multiple_of(which * n, n), n)
    return (cut, slice(None)) if axis == 0 else (slice(None), cut)


class _Riding:
    def __init__(self, transfers, ins, outs, own=()):
        self.transfers, self.ins, self.outs, self.own = transfers, list(ins), list(outs), list(own)
        hbm = pl.BlockSpec(memory_space=pltpu.HBM)
        self.in_specs, self.out_specs = [hbm] * len(self.ins), [hbm] * len(self.outs)
        self.out_shape = [jax.ShapeDtypeStruct(s, d) for s, d in self.outs]
        stages, stage_sems = _own_scratch(self.own, self.ins)
        self.scratch = [pltpu.SemaphoreType.DMA((max(len(transfers), 1),))] * 2 + [stage_sems] + stages

    def hooks(self, I, O, ssem, rsem, lsem, *stage, first, middle, last):
        tr, own = self.transfers, self.own

        @pl.when(first)
        def _():
            _own_copies(own, I, O, stage, lsem, 0)
            _start_transfers(tr, I, O, ssem, rsem, False)

        if own or any(t.get("after") is not None for t in tr):
            @pl.when(middle)
            def _():
                _own_copies(own, I, O, stage, lsem, 1)
                _start_transfers(tr, I, O, ssem, rsem, True)

        def at_end():
            @pl.when(last)
            def _():
                _finish_transfers(tr, I, O, ssem, rsem)
                _own_copies(own, I, O, stage, lsem, 2)

        return at_end


def _stretch(n, pos):
    return (pl.ds(pos * n if isinstance(pos, int) else pl.multiple_of(pos * n, n), n),)


def _gather_plan(shards, axes):
    def half(a, who):
        if shards[a].ndim == 1:
            return _stretch(shards[a].shape[0] // 2, who[2])
        return _half(shards[a].shape, axes[a], who[2])

    def landed(a, chip, who):
        if shards[a].ndim == 1:
            return _stretch(shards[a].shape[0] // 2, 2 * chip + who[2])
        return (chip,) + half(a, who)

    over_ici, onward = [], []
    for a in range(len(shards)):
        for mask in CHIP_MASKS:
            over_ici.append(dict(
                mask=mask,
                src=lambda I, O, me, a=a: I[a].at[half(a, me)],
                dst=lambda I, O, who, a=a: O[a].at[landed(a, _chip(who), who)]))
            onward.append(dict(
                mask=SIBLING, after=len(over_ici) - 1,
                src=lambda I, O, me, a=a, mask=mask: O[a].at[landed(a, _chip(_flip(me, mask)), me)],
                dst=lambda I, O, who, a=a, mask=mask: O[a].at[landed(a, _chip(_flip(who, mask)), who)]))
    outs = [((NCHIP * s.shape[0],) if s.ndim == 1 else (NCHIP,) + s.shape, s.dtype) for s in shards]

    def whole(a, chip):
        return _stretch(shards[a].shape[0], chip) if shards[a].ndim == 1 else (chip,)

    own = [(lambda I, O, me, a=a: I[a], lambda I, O, me, a=a: O[a].at[whole(a, _chip(me))])
           for a in range(len(shards))]
    return over_ici + onward, outs, own


def _gather_shards(shards, axes):
    transfers, outs, own = _gather_plan(shards, axes)
    return _exchange("gather_weights", shards, outs, transfers, own)


def _to_sibling(arrs, name):
    transfers = [dict(mask=SIBLING, src=lambda I, O, me, a=a: I[a], dst=lambda I, O, who, a=a: O[a])
                 for a in range(len(arrs))]
    return _exchange(name, arrs, [(t.shape, t.dtype) for t in arrs], transfers)


def _halves_to_sibling(blocks, axes, name):
    def cut(a, which):
        return (slice(None),) + _half(blocks[a].shape[1:], axes[a], which)

    transfers, outs = [], []
    for a, (b, ax) in enumerate(zip(blocks, axes)):
        if b.ndim == 1:
            h = b.shape[0] // NCHIP // 2
            for k in range(NCHIP):
                transfers.append(dict(mask=SIBLING,
                                      src=lambda I, O, me, a=a, k=k, h=h: I[a].at[_stretch(h, 2 * k + 1 - me[2])],
                                      dst=lambda I, O, who, a=a, k=k, h=h: O[a].at[_stretch(h, k)]))
            outs.append(((NCHIP * h,), b.dtype))
        else:
            transfers.append(dict(mask=SIBLING, src=lambda I, O, me, a=a: I[a].at[cut(a, 1 - me[2])],
                                  dst=lambda I, O, who, a=a: O[a]))
            shape = list(b.shape)
            shape[ax + 1] //= 2
            outs.append((tuple(shape), b.dtype))
    return _exchange(name, blocks, outs, transfers)


def _scatter_plan(tb):
    def slot(a, k):
        return (k,) if tb[a].ndim == 3 else _stretch(tb[a].shape[0] // NCHIP, k)

    transfers = []
    for a in range(len(tb)):
        for n, mask in enumerate(CHIP_MASKS):
            transfers.append(dict(
                mask=mask,
                src=lambda I, O, me, a=a, mask=mask: I[a].at[slot(a, _chip(_flip(me, mask)))],
                dst=lambda I, O, who, a=a, n=n: O[a].at[slot(a, n)]))
    outs = [((3,) + t.shape[1:] if t.ndim == 3 else (3 * (t.shape[0] // NCHIP),), t.dtype) for t in tb]
    return transfers, outs


def _scatter_chip_sums(tb):
    transfers, outs = _scatter_plan(tb)
    return _exchange("scatter_grads", tb, outs, transfers)


def _gather_small(vec):
    def slot(who):
        return 4 * who[0] + 2 * who[1] + who[2]

    masks = [(m >> 2 & 1, m >> 1 & 1, m & 1) for m in range(1, 8)]
    transfers = [dict(mask=mask, src=lambda I, O, me: I[0], dst=lambda I, O, who: O[0].at[slot(who)])
                 for mask in masks]
    own = [(lambda I, O, me: I[0], lambda I, O, me: O[0].at[slot(me)])]
    return _exchange("gather_small", [vec], [((8,) + vec.shape, vec.dtype)], transfers, own)[0]


def _rope_tables(positions):
    half = ROT // 2
    inv_freq = jnp.power(jnp.float32(THETA), -jnp.arange(0, ROT, 2, dtype=F32) / ROT)
    ang = positions.astype(F32)[:, None] * inv_freq[None, :]
    cos, sin = jnp.cos(ang), jnp.sin(ang)
    one, zero, z8 = jnp.ones((S, HD - ROT), F32), jnp.zeros((S, HD - ROT), F32), jnp.zeros((S, half), F32)
    c = jnp.concatenate([cos, cos, one], axis=1)
    a = jnp.concatenate([-sin, z8, zero], axis=1)
    b = jnp.concatenate([z8, sin, zero], axis=1)
    return tuple(jnp.tile(t, (1, 2)) for t in (c, a, b))


def _tile_heads(g, w):
    return jnp.tile(g.reshape(1, HD), (1, w // HD))


def _fold_heads(dg):
    return dg.reshape(-1, HD).sum(axis=0)


def _pad_lanes(a):
    return jnp.pad(a, ((0, 0), (0, LANES - a.shape[1])))


def _local_step(x, target, positions, wt, fetch, late_weights, begin_reduce):
    rope = _rope_tables(positions)
    w1t = wt["w_in_a_t"]
    f_row = 3 * D // LANES
    wg_t = w1t[3 * D + NH:]
    in_b_block = lambda c: pl.BlockSpec((None, TN_, TN_), lambda j, i: (c, j, 0))
    b_pad = _pad_lanes(wt["b_forget"].reshape(1, NH))
    qg_a, kg_a = _tile_heads(wt["qnorm_a_g"], D), _tile_heads(wt["knorm_a_g"], D)
    qg_b, kg_b = _tile_heads(wt["qnorm_b_g"], D), _tile_heads(wt["knorm_b_g"], KVW)
    norm_a, kv_g, norm_b = wt["norm_a_g"].reshape(1, D), wt["kv_norm_g"].reshape(1, D), wt["norm_b_g"].reshape(1, D)
    sinks_t = jnp.repeat(wt["sinks"].reshape(1, NH), HD, axis=1)

    (u_a,) = _rmsnorm_fwd(x, [norm_a], "norm_a")
    qkv = _mm("proj_a", S, 3 * D, [(u_a, _a_rows(D), w1t, _b_rows(D), NT)])
    fpad = _mm("proj_f", S, LANES, [(u_a, _a_rows(D), w1t, _b_rows(D, row0=f_row, tn=LANES), NT)], tn=LANES)
    gate_a = _mm("proj_gate_a", S, D, [(u_a, _a_rows(D), wg_t, _b_rows(D), NT)])
    q_a, k_a, v_a = _a_post(qkv, qg_a, kg_a)
    ct = _forget_cumsum(fpad, b_pad)
    ct2 = ct[:NH].reshape(NH // 2, 2, S)
    o_a, lse_a, fetched = _fox_fwd(q_a, k_a, v_a, ct2, fetch)
    wt = {**wt, **late_weights(fetched)}
    w_in_b = wt["w_in_b"]
    y_a = _gate_fwd(o_a, gate_a, 0, "gate_a")
    h1 = _mm("out_a", S, D, [(y_a, _a_rows(D), wt["w_out_a"], _b_cols(D), None)], add=x)
    u_kv, u_b = _rmsnorm_fwd(h1, [kv_g, norm_b], "norm_b")
    kv = _mm("proj_kv", S, 2 * KVW, [(u_kv, _a_rows(D), wt["w_kv"], _b_cols(D), None)])
    pb = _mm("proj_b", S, 2 * D,
             [(u_b, _a_rows(D), w_in_b, pl.BlockSpec((None, D, TN_), lambda j, i: (j, 0, 0)), None)])
    q_b, kdup, vdup = _b_post(pb, kv, qg_b, kg_b, rope)
    o_b, lse_b = _swa_fwd(q_b, kdup, vdup, sinks_t)
    y_b = _gate_fwd(o_b, pb, 1, "gate_b")
    out = _mm("out_b", S, D, [(y_b, _a_rows(D), wt["w_out_b"], _b_cols(D), None)], add=h1)
    d_out, d_out_b, sq = _loss_head(out, target)

    g = {}
    g["w_out_b"] = _mm("dw_out_b", D, D, [(y_b, _a_cols(S), d_out_b, _b_cols(S), TN)])
    d_y_b = _mm("dy_b", S, D, [(d_out_b, _a_rows(D), wt["w_out_b"], _b_rows(D), NT)])
    d_o_b, d_gate_b = _gate_bwd(d_y_b, o_b, pb, 1, "gate_b_bwd")
    dq_b, dkdup, dvdup, dsk = _swa_bwd(q_b, kdup, vdup, sinks_t, o_b, lse_b, d_o_b)
    g["sinks"] = dsk[0, ::HD]
    d_qb_raw, dg = _headnorm_bwd(pb, 0, qg_b, dq_b, rope, "qnorm_b_bwd")
    g["qnorm_b_g"] = _fold_heads(dg)
    d_pb = [d_qb_raw, d_qb_raw, d_gate_b, d_gate_b]
    g["w_in_b"] = jnp.concatenate([
        _mm("dw_in_b_q", D, D, [(u_b, _a_cols(S), d_qb_raw, _b_cols(S), TN)], stacked=True),
        _mm("dw_in_b_gate", D, D, [(u_b, _a_cols(S), d_gate_b, _b_cols(S), TN)], stacked=True)], axis=0)
    d_u_b = _mm("du_b", S, D, [(d_pb[c], _a_rows(TN_, col=c % 2), w_in_b, in_b_block(c), NT) for c in range(NCHIP)])
    d_kv, dg = _kv_bwd(dkdup, dvdup, kv, kg_b, rope)
    g["knorm_b_g"] = _fold_heads(dg)
    g["w_kv"] = _mm("dw_kv", D, 2 * KVW, [(u_kv, _a_cols(S), d_kv, _b_cols(S), TN)])
    d_u_kv = _mm("du_kv", S, D, [(d_kv, _a_rows(2 * KVW), wt["w_kv"], _b_rows(2 * KVW), NT)])
    d_h1, d_h1_b, g["kv_norm_g"], g["norm_b_g"] = _rmsnorm_bwd(h1, [kv_g, norm_b], [d_u_kv, d_u_b], d_out, "norm_b_bwd")
    g["w_out_a"] = _mm("dw_out_a", D, D, [(y_a, _a_cols(S), d_h1_b, _b_cols(S), TN)])
    d_y_a = _mm("dy_a", S, D, [(d_h1_b, _a_rows(D), wt["w_out_a"], _b_rows(D), NT)])
    d_o_a, d_gate_a = _gate_bwd(d_y_a, o_a, gate_a, 0, "gate_a_bwd")
    riding, so_far = begin_reduce({n: g[n] for n in LATE})
    dq_a, dk_a, dv_a, dct, arrived = _fox_bwd(q_a, k_a, v_a, ct2, o_a, lse_a, d_o_a, riding)
    dct_pad = jnp.pad(dct.reshape(NH, S), ((0, LANES - NH), (0, 0)))
    d_f, db = _forget_bwd(dct_pad, fpad, b_pad)
    g["b_forget"] = db[0, :NH]
    d_q_raw, dg = _headnorm_bwd(qkv, 0, qg_a, dq_a, None, "qnorm_a_bwd")
    g["qnorm_a_g"] = _fold_heads(dg)
    d_k_raw, dg = _headnorm_bwd(qkv, 1, kg_a, dk_a, None, "knorm_a_bwd")
    g["knorm_a_g"] = _fold_heads(dg)
    pieces = [("q", d_q_raw), ("k", d_k_raw), ("v", dv_a), ("gate", d_gate_a)]
    dw = {n: _mm("dw_in_a_" + n, D, D, [(t, _a_cols(S), u_a, _b_cols(S), TN)]) for n, t in pieces}
    dw_f = _mm("dw_in_a_f", LANES, D, [(d_f, _a_cols(S, tm=LANES), u_a, _b_cols(S), TN)], tm=LANES)
    g["w_in_a"] = jnp.concatenate([dw["q"], dw["k"], dw["v"], dw_f[:NH], dw["gate"]], axis=0)
    riding, so_far_first = begin_reduce({"w_in_a": g["w_in_a"]})
    d_u_a, arrived_first = _mm("du_a", S, D, [
        (d_q_raw, _a_rows(D), w1t, _b_cols(D, row=0), None), (d_k_raw, _a_rows(D), w1t, _b_cols(D, row=1), None),
        (dv_a, _a_rows(D), w1t, _b_cols(D, row=2), None), (d_gate_a, _a_rows(D), wg_t, _b_cols(D), None),
        (d_f, _a_rows(LANES), w1t, _b_cols(LANES, row=f_row), None)], riding=riding)
    d_x, _, g["norm_a_g"] = _rmsnorm_bwd(x, [norm_a], [d_u_a], d_h1, "norm_a_bwd")
    return sq, d_x, g, (list(so_far_first) + list(so_far), list(arrived_first) + list(arrived))


BIG = ["w_in_a", "w_out_a", "w_kv", "w_in_b", "w_out_b"]
LATE = BIG[1:]
SPLIT = {"w_in_a": None, "w_out_a": 0, "w_kv": 0, "w_in_b": 0, "w_out_b": 0}
SMALL = ["norm_a_g", "b_forget", "qnorm_a_g", "knorm_a_g", "kv_norm_g", "knorm_b_g", "norm_b_g", "qnorm_b_g", "sinks"]
NAMES = ["norm_a_g", "w_in_a", "b_forget", "qnorm_a_g", "knorm_a_g", "w_out_a", "kv_norm_g", "w_kv", "knorm_b_g",
         "norm_b_g", "w_in_b", "qnorm_b_g", "sinks", "w_out_b"]


def _pack(vals):
    flat = []
    for v in vals:
        v = v.reshape(-1)
        flat.append(jnp.pad(v, (0, -v.shape[0] % LANES)))
    flat = jnp.concatenate(flat)
    flat = jnp.pad(flat, (0, -flat.shape[0] % (8 * LANES)))
    return flat.reshape(-1, LANES)


def _unpack(packed, shapes):
    flat, out, off = packed.reshape(-1), [], 0
    for s in shapes:
        n = int(np.prod(s))
        out.append(flat[off:off + n].reshape(s))
        off += n + (-n % LANES)
    return out


def kernel(x, positions, norm_a_g, w_in_a, b_forget, qnorm_a_g, knorm_a_g, w_out_a, kv_norm_g, w_kv, knorm_b_g, norm_b_g, w_in_b, qnorm_b_g, sinks, w_out_b, loss_target, m_norm_a_g, m_w_in_a, m_b_forget, m_qnorm_a_g, m_knorm_a_g, m_w_out_a, m_kv_norm_g, m_w_kv, m_knorm_b_g, m_norm_b_g, m_w_in_b, m_qnorm_b_g, m_sinks, m_w_out_b, v_norm_a_g, v_w_in_a, v_b_forget, v_qnorm_a_g, v_knorm_a_g, v_w_out_a, v_kv_norm_g, v_w_kv, v_knorm_b_g, v_norm_b_g, v_w_in_b, v_qnorm_b_g, v_sinks, v_w_out_b):
    w = dict(norm_a_g=norm_a_g, w_in_a=w_in_a, b_forget=b_forget, qnorm_a_g=qnorm_a_g, knorm_a_g=knorm_a_g,
             w_out_a=w_out_a, kv_norm_g=kv_norm_g, w_kv=w_kv, knorm_b_g=knorm_b_g, norm_b_g=norm_b_g,
             w_in_b=w_in_b, qnorm_b_g=qnorm_b_g, sinks=sinks, w_out_b=w_out_b)
    m = dict(norm_a_g=m_norm_a_g, w_in_a=m_w_in_a, b_forget=m_b_forget, qnorm_a_g=m_qnorm_a_g, knorm_a_g=m_knorm_a_g,
             w_out_a=m_w_out_a, kv_norm_g=m_kv_norm_g, w_kv=m_w_kv, knorm_b_g=m_knorm_b_g, norm_b_g=m_norm_b_g,
             w_in_b=m_w_in_b, qnorm_b_g=m_qnorm_b_g, sinks=m_sinks, w_out_b=m_w_out_b)
    v = dict(norm_a_g=v_norm_a_g, w_in_a=v_w_in_a, b_forget=v_b_forget, qnorm_a_g=v_qnorm_a_g, knorm_a_g=v_knorm_a_g,
             w_out_a=v_w_out_a, kv_norm_g=v_kv_norm_g, w_kv=v_w_kv, knorm_b_g=v_knorm_b_g, norm_b_g=v_norm_b_g,
             w_in_b=v_w_in_b, qnorm_b_g=v_qnorm_b_g, sinks=v_sinks, w_out_b=v_w_out_b)
    my_chip = 2 * lax.axis_index("x") + lax.axis_index("y")

    def shard2d(t, n):
        if n == "w_in_a":
            return jnp.transpose(t, (2, 0, 1)).reshape(-1)
        return t.reshape(t.shape[-2:])

    def unflat(t, n):
        return jnp.transpose(t.reshape(-1, 1, D), (1, 2, 0)) if n == "w_in_a" else t.reshape(w[n].shape)

    w2d = {n: shard2d(w[n], n) for n in BIG}

    norm_a_rows = jnp.broadcast_to(norm_a_g.reshape(1, D // NCHIP), (16, D // NCHIP))
    w1t, norm_rows = _gather_shards([w2d["w_in_a"].astype(BF16), norm_a_rows], [SPLIT["w_in_a"], 0])
    wt = {"w_in_a_t": w1t.reshape(-1, D), "norm_a_g": norm_rows[:, 0, :].reshape(1, D)}
    for n in SMALL[1:]:
        wt[n] = w[n]
    late_shards = [w2d[n].astype(BF16) for n in LATE]
    late_axes = [SPLIT[n] for n in LATE]
    transfers, outs, own = _gather_plan(late_shards, late_axes)
    fetch = _Riding(transfers, late_shards, outs, own)

    def late_weights(fetched):
        return {n: t if n == "w_in_b" else t.reshape(-1, t.shape[2]) for n, t in zip(LATE, fetched)}

    def as_blocks(t):
        if t.ndim == 3:
            return t
        return t.reshape(-1) if t.shape[0] % (8 * NCHIP) else t.reshape(NCHIP, -1, t.shape[1])

    def begin_reduce(grads):
        names = list(grads)
        axes = [SPLIT[n] for n in names]
        blocks = [as_blocks(grads[n]) for n in names]
        halves = _halves_to_sibling(blocks, axes, "sibling_halves_" + names[0])
        sums = [_chip_sum(blk, part, ax, "chip_sum_" + n) for n, ax, blk, part in zip(names, axes, blocks, halves)]
        bf16 = [s[1] for s in sums]
        transfers, outs = _scatter_plan(bf16)
        return _Riding(transfers, bf16, outs), [s[0] for s in sums]

    sq, d_x, g, (chip_f32, arrived) = _local_step(x[0], loss_target[0], positions, wt, fetch, late_weights,
                                                  begin_reduce)

    small_shapes = [(D,), (NH,), (HD,), (HD,), (D,), (HD,), (D,), (HD,), (NH,), (D,)]
    packed = _pack([g[n] for n in SMALL] + [sq])
    total = _sum_stack(_gather_small(packed), "sum_small")
    small_g = dict(zip(SMALL, _unpack(total, small_shapes)[:-1]))
    loss = 0.5 * jnp.sum(_unpack(total, small_shapes)[-1]) / D
    small_g["norm_a_g"] = lax.dynamic_slice(small_g["norm_a_g"], (my_chip * (D // NCHIP),), (D // NCHIP,))

    axes = [SPLIT[n] for n in BIG]
    halves = []
    for n, ax, t32, parts in zip(BIG, axes, chip_f32, arrived):
        if t32.ndim == 1:
            own = lax.dynamic_slice_in_dim(t32, my_chip * (t32.shape[0] // NCHIP), t32.shape[0] // NCHIP)
        else:
            own = lax.dynamic_index_in_dim(t32, my_chip, axis=0, keepdims=False)
        halves.append(_mesh_sum(own, parts, ax, "mesh_sum_" + n))
    sibling_done = _to_sibling(halves, "finished_halves")

    res = {}
    for n, ax, mine_half, their_half in zip(BIG, axes, halves, sibling_done):
        out4 = _adamw_halves(w2d[n], mine_half, their_half, shard2d(m[n], n), shard2d(v[n], n), ax, "adamw_" + n)
        res[n] = tuple(unflat(t, n) for t in out4)
    sm_g = _pack([small_g[n] for n in SMALL])
    sm = [_pack([d[n] for n in SMALL]) for d in (w, m, v)]
    sm_out = _adamw(sm[0], sm_g, sm[1], sm[2], "adamw_small")
    sm_shapes = [w[n].shape for n in SMALL]
    unpacked = [_unpack(t, sm_shapes) for t in (sm_g,) + tuple(sm_out)]
    for i, n in enumerate(SMALL):
        res[n] = tuple(u[i] for u in unpacked)

    outs = [loss, d_x[None]]
    for k in range(4):
        outs += [res[n][k] for n in NAMES]
    return tuple(outs)
```

```python
import numpy as np
import jax
import jax.numpy as jnp
from jax import lax
from jax.experimental import pallas as pl
from jax.experimental.pallas import tpu as pltpu

F32, BF16 = jnp.float32, jnp.bfloat16
S, D, HD, NH, NKV = 2048, 1024, 64, 16, 4
KVW = NKV * HD
WINDOW = 128
ROT = HD // 4
THETA = 500000.0
EPS = 1e-6
SCALE = HD ** -0.5
LANES = 128
NEG = -1e30
VMEM_LIMIT = 48 * 2 ** 20
ROWS = 256
ATT = 512
SWQ = 4
NCHIP = 4
ADAM_LR, ADAM_B1, ADAM_B2, ADAM_EPS, ADAM_WD, ADAM_STEP = 0.001, 0.9, 0.999, 1e-08, 0.01, 10
NT = (((1,), (1,)), ((), ()))
TN = (((0,), (0,)), ((), ()))
MESH = pl.DeviceIdType.MESH


def _params(n):
    return pltpu.CompilerParams(dimension_semantics=("arbitrary",) * n, vmem_limit_bytes=VMEM_LIMIT)


def _dot(a, b, dims=None):
    if dims is None:
        return jnp.dot(a, b, preferred_element_type=F32)
    return lax.dot_general(a, b, dims, preferred_element_type=F32)


def _dot_split(a, b, n):
    out, rest = None, a
    for _ in range(n):
        hi = rest.astype(BF16)
        term = _dot(hi, b)
        out = term if out is None else out + term
        rest = rest - hi.astype(F32)
    return out


def _seg_mat(w):
    e = (np.arange(w)[:, None] // HD == np.arange(LANES)[None, :]).astype(np.float32)
    return jnp.asarray(e, BF16)


def _spread(r, w):
    head = lax.broadcasted_iota(jnp.int32, (2 * LANES, w), 1) >> 6
    row = lax.broadcasted_iota(jnp.int32, (2 * LANES, w), 0)
    et2 = jnp.where(head == (row & (LANES - 1)), 1.0, 0.0).astype(BF16)
    hi = r.astype(BF16)
    lo = (r - hi.astype(F32)).astype(BF16)
    return _dot(jnp.concatenate([hi, lo], axis=1), et2)


def _head_rstd(x, e):
    ss = _dot_split(x * x, e, 2)
    return _spread(lax.rsqrt(ss * (1.0 / HD) + EPS), x.shape[1])


def _rope(x, c, a, b):
    w = x.shape[1]
    return x * c + pltpu.roll(x, w - ROT // 2, 1) * a + pltpu.roll(x, ROT // 2, 1) * b


def _rope_t(dy, c, a, b):
    w = dy.shape[1]
    return dy * c + pltpu.roll(dy * b, w - ROT // 2, 1) + pltpu.roll(dy * a, ROT // 2, 1)


def _sigmoid(x):
    return 1.0 / (1.0 + jnp.exp(-x))


def _row_spec(shape, ts):
    nd = len(shape)
    if shape[0] == S:
        return pl.BlockSpec((ts,) + tuple(shape[1:]), lambda i: (i,) + (0,) * (nd - 1))
    return pl.BlockSpec(tuple(shape), lambda i: (0,) * nd)


def _rows_call(body, name, ins, outs, ts=ROWS):
    return pl.pallas_call(
        body, name=name, grid=(S // ts,),
        in_specs=[_row_spec(a.shape, ts) for a in ins],
        out_specs=[_row_spec(s, ts) for s, _ in outs],
        out_shape=[jax.ShapeDtypeStruct(s, d) for s, d in outs],
        compiler_params=_params(1))(*ins)


def _col_spec(ts, w, col):
    return pl.BlockSpec((ts, w), lambda i: (i, col))


TM = TN_ = 512
TM_TOKENS = 1024


def _mm(name, m, n, terms, out_dtype=F32, add=None, tm=None, tn=TN_, stacked=False, riding=None):
    nterm = len(terms)
    if tm is None:
        tm = TM_TOKENS if m == S else TM
    nj, ni_ = n // tn, m // tm
    n_in = 2 * nterm + (add is not None)
    r_in, r_out = (len(riding.ins), len(riding.outs)) if riding is not None else (0, 0)

    def body(*refs):
        if riding is not None:
            j, i = pl.program_id(0), pl.program_id(1)
            at_end = riding.hooks(refs[n_in:n_in + r_in], refs[n_in + r_in + 1:n_in + r_in + 1 + r_out],
                                  *refs[n_in + r_in + 1 + r_out:], first=(j == 0) & (i == 0),
                                  middle=(j == nj // 2) & (i == 0), last=(j == nj - 1) & (i == ni_ - 1))
        acc = None
        for t in range(nterm):
            part = _dot(refs[2 * t][...], refs[2 * t + 1][...], terms[t][4])
            acc = part if acc is None else acc + part
        if add is not None:
            acc = acc + refs[2 * nterm][...]
        refs[n_in + r_in][...] = acc.astype(out_dtype)
        if riding is not None:
            at_end()

    tile = pl.BlockSpec((tm, tn), lambda j, i: (i, j))
    ins, specs = [], []
    for a, a_spec, b, b_spec, _ in terms:
        ins += [a, b]
        specs += [a_spec, b_spec]
    if add is not None:
        ins.append(add)
        specs.append(tile)
    out_spec = pl.BlockSpec((None, tm, tn), lambda j, i: (j, i, 0)) if stacked else tile
    out_shape = jax.ShapeDtypeStruct((nj, m, tn) if stacked else (m, n), out_dtype)
    if riding is None:
        return pl.pallas_call(body, name=name, grid=(nj, ni_), in_specs=specs, out_specs=out_spec,
                              out_shape=out_shape, compiler_params=_params(2))(*ins)
    res = pl.pallas_call(
        body, name=name, grid=(nj, ni_), in_specs=specs + riding.in_specs,
        out_specs=[out_spec] + riding.out_specs, out_shape=[out_shape] + riding.out_shape,
        scratch_shapes=riding.scratch, compiler_params=_params(2))(*ins, *riding.ins)
    return res[0], res[1:]


def _a_rows(k, col=0, tm=TM_TOKENS):
    return pl.BlockSpec((tm, k), lambda j, i: (i, col))


def _a_cols(k, tm=TM):
    return pl.BlockSpec((k, tm), lambda j, i: (0, i))


def _b_cols(k, row=0, col0=0, tn=TN_):
    return pl.BlockSpec((k, tn), lambda j, i: (row, col0 + j))


def _b_rows(k, row0=0, tn=TN_):
    return pl.BlockSpec((tn, k), lambda j, i: (row0 + j, 0))


def _rmsnorm_fwd(x, gains, name):
    def body(*refs):
        xv = refs[0][...]
        r = lax.rsqrt(jnp.mean(xv * xv, axis=-1, keepdims=True) + EPS)
        xh = xv * r
        for n in range(len(gains)):
            refs[1 + len(gains) + n][...] = (xh * refs[1 + n][...]).astype(BF16)

    return _rows_call(body, name, [x] + list(gains), [((S, D), BF16)] * len(gains))


def _rmsnorm_bwd(x, gains, dus, dres, name):
    n = len(gains)

    def body(*refs):
        x_ref, g_refs, du_refs, dres_ref = refs[0], refs[1:1 + n], refs[1 + n:1 + 2 * n], refs[1 + 2 * n]
        dx_ref, dxb_ref, dg_refs = refs[2 + 2 * n], refs[3 + 2 * n], refs[4 + 2 * n:]
        xv = x_ref[...]
        r = lax.rsqrt(jnp.mean(xv * xv, axis=-1, keepdims=True) + EPS)
        xh = xv * r
        gy = None
        for m in range(n):
            du = du_refs[m][...]
            part = jnp.sum(du * xh, axis=0, keepdims=True)

            @pl.when(pl.program_id(0) == 0)
            def _(m=m, part=part):
                dg_refs[m][...] = part

            @pl.when(pl.program_id(0) != 0)
            def _(m=m, part=part):
                dg_refs[m][...] += part

            t = du * g_refs[m][...]
            gy = t if gy is None else gy + t
        dx = dres_ref[...] + r * (gy - xh * jnp.mean(gy * xh, axis=-1, keepdims=True))
        dx_ref[...] = dx
        dxb_ref[...] = dx.astype(BF16)

    outs = [((S, D), F32), ((S, D), BF16)] + [((1, D), F32)] * n
    return _rows_call(body, name, [x] + list(gains) + list(dus) + [dres], outs)


def _a_post(qkvg, qg, kg):
    e = _seg_mat(D)

    def body(q_ref, k_ref, v_ref, qg_ref, kg_ref, e_ref, qo, ko, vo):
        ev = e_ref[...]
        qv, kv = q_ref[...], k_ref[...]
        qo[...] = (qv * _head_rstd(qv, ev) * qg_ref[...] * SCALE).astype(BF16)
        ko[...] = (kv * _head_rstd(kv, ev) * kg_ref[...]).astype(BF16)
        vo[...] = v_ref[...].astype(BF16)

    whole = lambda a: pl.BlockSpec(a.shape, lambda i: (0, 0))
    return pl.pallas_call(
        body, name="a_post", grid=(S // ROWS,),
        in_specs=[_col_spec(ROWS, D, 0), _col_spec(ROWS, D, 1), _col_spec(ROWS, D, 2),
                  whole(qg), whole(kg), whole(e)],
        out_specs=[_col_spec(ROWS, D, 0)] * 3,
        out_shape=[jax.ShapeDtypeStruct((S, D), BF16)] * 3,
        compiler_params=_params(1))(qkvg, qkvg, qkvg, qg, kg, e)


def _tri(upper):
    r, c = np.arange(ROWS)[:, None], np.arange(ROWS)[None, :]
    return jnp.asarray((r <= c) if upper else (r >= c), BF16)


def _forget_cumsum(fpad, bpad):
    def body(f_ref, b_ref, u_ref, c_ref, carry):
        @pl.when(pl.program_id(0) == 0)
        def _():
            carry[...] = jnp.zeros_like(carry)

        lf = jax.nn.log_sigmoid(f_ref[...] + b_ref[...])
        blk = _dot_split(lf.T, u_ref[...], 3) + carry[:, 0:1]
        c_ref[...] = blk
        carry[...] = jnp.broadcast_to(blk[:, ROWS - 1:ROWS], carry.shape)

    return pl.pallas_call(
        body, name="forget_cumsum", grid=(S // ROWS,),
        in_specs=[pl.BlockSpec((ROWS, LANES), lambda i: (i, 0)), pl.BlockSpec((1, LANES), lambda i: (0, 0)),
                  pl.BlockSpec((ROWS, ROWS), lambda i: (0, 0))],
        out_specs=pl.BlockSpec((LANES, ROWS), lambda i: (0, i)),
        out_shape=jax.ShapeDtypeStruct((LANES, S), F32),
        scratch_shapes=[pltpu.VMEM((LANES, LANES), F32)],
        compiler_params=_params(1))(fpad, bpad, _tri(True))


def _forget_bwd(dct, fpad, bpad):
    nb = S // ROWS

    def body(dc_ref, f_ref, b_ref, l_ref, df_ref, db_ref, carry):
        @pl.when(pl.program_id(0) == 0)
        def _():
            carry[...] = jnp.zeros_like(carry)
            db_ref[...] = jnp.zeros_like(db_ref)

        blk = _dot_split(dc_ref[...], l_ref[...], 3) + carry[:, 0:1]
        carry[...] = jnp.broadcast_to(blk[:, 0:1], carry.shape)
        df = blk.T * _sigmoid(-(f_ref[...] + b_ref[...]))
        df_ref[...] = df.astype(BF16)
        db_ref[...] += jnp.sum(df, axis=0, keepdims=True)

    return pl.pallas_call(
        body, name="forget_bwd", grid=(nb,),
        in_specs=[pl.BlockSpec((LANES, ROWS), lambda i: (0, nb - 1 - i)),
                  pl.BlockSpec((ROWS, LANES), lambda i: (nb - 1 - i, 0)),
                  pl.BlockSpec((1, LANES), lambda i: (0, 0)), pl.BlockSpec((ROWS, ROWS), lambda i: (0, 0))],
        out_specs=[pl.BlockSpec((ROWS, LANES), lambda i: (nb - 1 - i, 0)), pl.BlockSpec((1, LANES), lambda i: (0, 0))],
        out_shape=[jax.ShapeDtypeStruct((S, LANES), BF16), jax.ShapeDtypeStruct((1, LANES), F32)],
        scratch_shapes=[pltpu.VMEM((LANES, LANES), F32)],
        compiler_params=_params(1))(dct, fpad, bpad, _tri(False))


def _headnorm_bwd(x, col, gain, dy, rope, name):
    e = _seg_mat(D)
    tabs = list(rope) if rope is not None else []

    def body(*refs):
        x_ref, g_ref, dy_ref, e_ref = refs[:4]
        dx_ref, dg_ref = refs[-2:]
        xv, dyv, ev = x_ref[...], dy_ref[...], e_ref[...]
        if rope is not None:
            c, a, b = (jnp.tile(t[...], (1, D // LANES)) for t in refs[4:7])
            dyv = _rope_t(dyv, c, a, b)
        r = _head_rstd(xv, ev)
        xh = xv * r
        part = jnp.sum(dyv * xh, axis=0, keepdims=True)

        @pl.when(pl.program_id(0) == 0)
        def _():
            dg_ref[...] = part

        @pl.when(pl.program_id(0) != 0)
        def _():
            dg_ref[...] += part

        gy = dyv * g_ref[...]
        seg = _spread(_dot_split(gy * xh, ev, 2) * (1.0 / HD), D)
        dx_ref[...] = (r * (gy - xh * seg)).astype(BF16)

    whole = lambda a: pl.BlockSpec(a.shape, lambda i: (0, 0))
    return pl.pallas_call(
        body, name=name, grid=(S // ROWS,),
        in_specs=[_col_spec(ROWS, D, col), whole(gain), _col_spec(ROWS, D, 0), whole(e)]
                 + [pl.BlockSpec((ROWS, LANES), lambda i: (i, 0))] * len(tabs),
        out_specs=[_col_spec(ROWS, D, 0), whole(gain)],
        out_shape=[jax.ShapeDtypeStruct((S, D), BF16), jax.ShapeDtypeStruct((1, D), F32)],
        compiler_params=_params(1))(x, gain, dy, e, *tabs)


def _dup_mat():
    r, c = np.arange(KVW)[:, None], np.arange(2 * KVW)[None, :]
    return (r // HD == c // LANES) & (r % HD == c % HD)


def _fold_mat():
    r, c = np.arange(D)[:, None], np.arange(KVW)[None, :]
    return (r // (2 * LANES) == c // HD) & (r % HD == c % HD)


def _b_post(pb, kv, qg, kg, rope):
    e, ek = _seg_mat(D), _seg_mat(KVW)
    dup = jnp.asarray(_dup_mat(), BF16)

    def body(q_ref, k_ref, v_ref, qg_ref, kg_ref, e_ref, ek_ref, dup_ref, c_ref, a_ref, b_ref, qo, ko, vo):
        c1, a1, b1 = c_ref[...], a_ref[...], b_ref[...]
        qv = q_ref[...]
        qn = qv * _head_rstd(qv, e_ref[...]) * qg_ref[...]
        t = lambda z, n: jnp.tile(z, (1, n))
        qo[...] = (_rope(qn, t(c1, D // LANES), t(a1, D // LANES), t(b1, D // LANES)) * SCALE).astype(BF16)
        kvv = k_ref[...]
        kn = kvv * _head_rstd(kvv, ek_ref[...]) * kg_ref[...]
        kr = _rope(kn, t(c1, KVW // LANES), t(a1, KVW // LANES), t(b1, KVW // LANES)).astype(BF16)
        ko[...] = _dot(kr, dup_ref[...]).astype(BF16)
        vo[...] = _dot(v_ref[...].astype(BF16), dup_ref[...]).astype(BF16)

    whole = lambda a: pl.BlockSpec(a.shape, lambda i: (0, 0))
    tab = pl.BlockSpec((ROWS, LANES), lambda i: (i, 0))
    return pl.pallas_call(
        body, name="b_post", grid=(S // ROWS,),
        in_specs=[_col_spec(ROWS, D, 0), _col_spec(ROWS, KVW, 0), _col_spec(ROWS, KVW, 1),
                  whole(qg), whole(kg), whole(e), whole(ek), whole(dup), tab, tab, tab],
        out_specs=[_col_spec(ROWS, D, 0), _col_spec(ROWS, 2 * KVW, 0), _col_spec(ROWS, 2 * KVW, 0)],
        out_shape=[jax.ShapeDtypeStruct((S, D), BF16), jax.ShapeDtypeStruct((S, 2 * KVW), BF16),
                   jax.ShapeDtypeStruct((S, 2 * KVW), BF16)],
        compiler_params=_params(1))(pb, kv, kv, qg, kg, e, ek, dup, *rope)


def _kv_bwd(dkdup, dvdup, kv, kg, rope):
    ek = _seg_mat(KVW)
    fold = jnp.asarray(_fold_mat(), BF16)

    def body(dk_ref, dv_ref, k_ref, kg_ref, ek_ref, fold_ref, c_ref, a_ref, b_ref, dkv_ref, dg_ref):
        ev, fv = ek_ref[...], fold_ref[...]
        t = lambda z: jnp.tile(z[...], (1, KVW // LANES))
        dk = _rope_t(_dot_split(dk_ref[...], fv, 2), t(c_ref), t(a_ref), t(b_ref))
        dv = _dot_split(dv_ref[...], fv, 2)
        xv = k_ref[...]
        r = _head_rstd(xv, ev)
        xh = xv * r
        part = jnp.sum(dk * xh, axis=0, keepdims=True)

        @pl.when(pl.program_id(0) == 0)
        def _():
            dg_ref[...] = part

        @pl.when(pl.program_id(0) != 0)
        def _():
            dg_ref[...] += part

        gy = dk * kg_ref[...]
        seg = _spread(_dot_split(gy * xh, ev, 2) * (1.0 / HD), KVW)
        dkv_ref[:, 0:KVW] = (r * (gy - xh * seg)).astype(BF16)
        dkv_ref[:, KVW:2 * KVW] = dv.astype(BF16)

    whole = lambda a: pl.BlockSpec(a.shape, lambda i: (0, 0))
    tab = pl.BlockSpec((ROWS, LANES), lambda i: (i, 0))
    return pl.pallas_call(
        body, name="kv_bwd", grid=(S // ROWS,),
        in_specs=[_col_spec(ROWS, D, 0), _col_spec(ROWS, D, 0), _col_spec(ROWS, KVW, 0),
                  whole(kg), whole(ek), whole(fold), tab, tab, tab],
        out_specs=[_col_spec(ROWS, 2 * KVW, 0), whole(kg)],
        out_shape=[jax.ShapeDtypeStruct((S, 2 * KVW), BF16), jax.ShapeDtypeStruct((1, KVW), F32)],
        compiler_params=_params(1))(dkdup, dvdup, kv, kg, ek, fold, *rope)


def _loss_head(out, target):
    def body(o_ref, t_ref, d_ref, db_ref, l_ref):
        diff = o_ref[...] - t_ref[...]
        d = diff * (1.0 / D)
        d_ref[...] = d
        db_ref[...] = d.astype(BF16)

        @pl.when(pl.program_id(0) == 0)
        def _():
            l_ref[...] = jnp.zeros_like(l_ref)

        l_ref[...] += jnp.sum(diff * diff, axis=0, keepdims=True)

    return _rows_call(body, "loss_head", [out, target], [((S, D), F32), ((S, D), BF16), ((1, D), F32)])


def _lane():
    return lax.broadcasted_iota(jnp.int32, (1, LANES), 1)


def _head_mask(hh):
    return (_lane() < HD) if hh == 0 else (_lane() >= HD)


def _fox_fwd(q, k, v, ct, gate, riding):
    nq, npair = S // ATT, NH // 2
    ni, no = len(riding.ins), len(riding.outs)

    def body(q_ref, k_ref, v_ref, c_ref, gate_ref, *rest):
        o_ref, lse_ref, y_ref = rest[ni:ni + 3]
        pair, i = pl.program_id(0), pl.program_id(1)
        at_end = riding.hooks(rest[:ni], rest[ni + 3:ni + 3 + no], *rest[ni + 3 + no:],
                              first=(pair == 0) & (i == 0), middle=(pair == npair // 2) & (i == 0),
                              last=(pair == npair - 1) & (i == nq - 1))
        q2 = q_ref[...]
        qms = [jnp.where(_head_mask(hh), q2, jnp.zeros_like(q2)) for hh in (0, 1)]

        def probs(off, width, m, hh, diag):
            s = _dot(qms[hh], k_ref[pl.ds(off, width), :], NT) - c_ref[hh:hh + 1, pl.ds(off, width)]
            if diag:
                row = i * ATT + lax.broadcasted_iota(jnp.int32, (ATT, width), 0)
                col = off + lax.broadcasted_iota(jnp.int32, (ATT, width), 1)
                s = jnp.where(col <= row, s, NEG)
            m_new = jnp.maximum(m, jnp.max(s, axis=1, keepdims=True))
            p = jnp.exp(s - m_new)
            p_hi = p.astype(BF16)
            return m_new, jnp.exp(m - m_new), p_hi, (p - p_hi.astype(F32)).astype(BF16)

        def weighted(off, width, p_hi, p_lo, hh):
            vj = v_ref[pl.ds(off, width), :]
            v1 = jnp.where(_head_mask(hh), vj, jnp.ones_like(vj))
            return _dot(p_hi, v1) + _dot(p_lo, v1)

        def step(off, width, carry, diag):
            off = pl.multiple_of(off, ATT)
            out = []
            for hh in (0, 1):
                m, acc = carry[hh]
                m, alpha, p_hi, p_lo = probs(off, width, m, hh, diag)
                out.append((m, alpha * acc + weighted(off, width, p_hi, p_lo, hh)))
            return tuple(out)

        one = (jnp.full((ATT, 1), NEG, F32), jnp.zeros((ATT, LANES), F32))
        carry = lax.fori_loop(0, i // 2, lambda j, cr: step(j * (2 * ATT), 2 * ATT, cr, False), (one, one))
        carry = lax.cond(i % 2 == 1, lambda cr: step((i - 1) * ATT, 2 * ATT, cr, True),
                         lambda cr: step(i * ATT, ATT, cr, True), carry)
        res = []
        for hh in (0, 1):
            m, acc = carry[hh]
            l = jnp.max(jnp.where(_head_mask(1 - hh), acc, 0.0), axis=1, keepdims=True)
            res.append((acc / l, m + jnp.log(l)))
        first = _head_mask(0)
        o = jnp.where(first, res[0][0], res[1][0])
        o_ref[...] = o
        lse_ref[...] = jnp.where(first, res[0][1], res[1][1])
        g = gate_ref[...]
        y_ref[...] = (o * (g * _sigmoid(g))).astype(BF16)
        at_end()

    blk = pl.BlockSpec((ATT, LANES), lambda p, i: (i, p))
    full = pl.BlockSpec((S, LANES), lambda p, i: (0, p))
    res = pl.pallas_call(
        body, name="fox_fwd", grid=(npair, nq),
        in_specs=[blk, full, full, pl.BlockSpec((None, 2, S), lambda p, i: (p, 0, 0)), blk] + riding.in_specs,
        out_specs=[blk, blk, blk] + riding.out_specs,
        out_shape=[jax.ShapeDtypeStruct((S, D), F32)] * 2 + [jax.ShapeDtypeStruct((S, D), BF16)] + riding.out_shape,
        scratch_shapes=riding.scratch,
        compiler_params=_params(2))(q, k, v, ct, gate, *riding.ins)
    return res[0], res[1], res[2], res[3:]


def _gate_grads(dy, o, g):
    sg = _sigmoid(g)
    return dy * (g * sg), dy * o * (sg * (1.0 + g * (1.0 - sg)))


def _fox_bwd(q, k, v, ct, o, lse, dy, gate, riding):
    nq, npair = S // ATT, NH // 2
    ni, no = len(riding.ins), len(riding.outs)

    def body(q_ref, k_ref, v_ref, c_ref, o_ref, lse_ref, dy_ref, gate_ref, *rest):
        dq_ref, dk_ref, dvb_ref, dc_ref, dgate_ref = rest[ni:ni + 5]
        dv_ref = rest[ni + 5 + no]
        pair, i = pl.program_id(0), pl.program_id(1)
        at_end = riding.hooks(rest[:ni], rest[ni + 5:ni + 5 + no], *rest[ni + 6 + no:],
                              first=(pair == 0) & (i == 0), middle=(pair == npair // 2) & (i == 0),
                              last=(pair == npair - 1) & (i == nq - 1))

        @pl.when(i == 0)
        def _():
            dk_ref[...] = jnp.zeros_like(dk_ref)
            dv_ref[...] = jnp.zeros_like(dv_ref)
            dc_ref[...] = jnp.zeros_like(dc_ref)

        q2, lse2 = q_ref[...], lse_ref[...]
        do2, dgate = _gate_grads(dy_ref[...], o_ref[...], gate_ref[...])
        dgate_ref[...] = dgate.astype(BF16)
        do2b = do2.astype(BF16)
        prod = do2b.astype(F32) * o_ref[...]
        heads = []
        for hh in (0, 1):
            hm = _head_mask(hh)
            heads.append((jnp.where(hm, q2, jnp.zeros_like(q2)), jnp.where(hm, do2b, jnp.zeros_like(do2b)),
                          jnp.sum(jnp.where(hm, prod, 0.0), axis=1, keepdims=True),
                          jnp.max(jnp.where(hm, lse2, NEG), axis=1, keepdims=True)))

        def step(off, width, dqs, diag):
            off = pl.multiple_of(off, ATT)
            kj, vj = k_ref[pl.ds(off, width), :], v_ref[pl.ds(off, width), :]
            dk, dv, out = None, None, []
            for hh in (0, 1):
                qm, dom, delta, lse_h = heads[hh]
                s = _dot(qm, kj, NT) - c_ref[hh:hh + 1, pl.ds(off, width)]
                p = jnp.exp(s - lse_h)
                if diag:
                    row = i * ATT + lax.broadcasted_iota(jnp.int32, (ATT, width), 0)
                    col = off + lax.broadcasted_iota(jnp.int32, (ATT, width), 1)
                    p = jnp.where(col <= row, p, 0.0)
                ds = p * (_dot(dom, vj, NT) - delta)
                dc_ref[hh:hh + 1, pl.ds(off, width)] += -jnp.sum(ds, axis=0, keepdims=True)
                dsb = ds.astype(BF16)
                dk_h, dv_h = _dot(dsb, qm, TN), _dot(p.astype(BF16), dom, TN)
                dk, dv = (dk_h, dv_h) if dk is None else (dk + dk_h, dv + dv_h)
                out.append(dqs[hh] + _dot(dsb, kj))
            dk_ref[pl.ds(off, width), :] += dk
            dv_ref[pl.ds(off, width), :] += dv
            return tuple(out)

        zero = jnp.zeros((ATT, LANES), F32)
        dqs = lax.fori_loop(0, i // 2, lambda j, acc: step(j * (2 * ATT), 2 * ATT, acc, False), (zero, zero))
        dqs = lax.cond(i % 2 == 1, lambda acc: step((i - 1) * ATT, 2 * ATT, acc, True),
                       lambda acc: step(i * ATT, ATT, acc, True), dqs)
        dq_ref[...] = jnp.where(_head_mask(0), dqs[0], dqs[1]) * SCALE

        @pl.when(i == nq - 1)
        def _():
            dvb_ref[...] = dv_ref[...].astype(BF16)

        at_end()

    blk = pl.BlockSpec((ATT, LANES), lambda p, i: (i, p))
    full = pl.BlockSpec((S, LANES), lambda p, i: (0, p))
    cspec = pl.BlockSpec((None, 2, S), lambda p, i: (p, 0, 0))
    res = pl.pallas_call(
        body, name="fox_bwd", grid=(npair, nq),
        in_specs=[blk, full, full, cspec, blk, blk, blk, blk] + riding.in_specs,
        out_specs=[blk, full, full, cspec, blk] + riding.out_specs,
        out_shape=[jax.ShapeDtypeStruct((S, D), F32)] * 2 + [jax.ShapeDtypeStruct((S, D), BF16),
                                                              jax.ShapeDtypeStruct((npair, 2, S), F32),
                                                              jax.ShapeDtypeStruct((S, D), BF16)]
                  + riding.out_shape,
        scratch_shapes=[pltpu.VMEM((S, LANES), F32)] + riding.scratch,
        compiler_params=_params(2))(q, k, v, ct, o, lse, dy, gate, *riding.ins)
    return res[0], res[1], res[2], res[3], res[4], res[5:]


def _both_heads(x):
    return jnp.concatenate([jnp.where(_head_mask(hh), x, jnp.zeros_like(x)) for hh in (0, 1)], axis=0)


def _per_head(col0, col1):
    return jnp.concatenate([jnp.broadcast_to(col0, (WINDOW, 1)), jnp.broadcast_to(col1, (WINDOW, 1))], axis=0)


def _unstack(x2):
    return jnp.where(_head_mask(0), x2[:WINDOW], x2[WINDOW:])


def _swa_valid(i, start):
    r = lax.broadcasted_iota(jnp.int32, (2 * WINDOW, 2 * WINDOW), 0)
    qabs = i * WINDOW + jnp.where(r >= WINDOW, r - WINDOW, r)
    kabs = start + lax.broadcasted_iota(jnp.int32, (2 * WINDOW, 2 * WINDOW), 1)
    return (kabs <= qabs) & (qabs - kabs < WINDOW)


def _swa_fwd(q, kdup, vdup, sinks_t, proj, gate_col):
    def body(q_ref, k_ref, v_ref, sk_ref, gate_ref, o_ref, lse_ref, y_ref):
        skv = sk_ref[...]
        first = _head_mask(0)
        for sb in range(SWQ):
            i = pl.program_id(1) * SWQ + sb
            rows = slice(sb * WINDOW, (sb + 1) * WINDOW)
            start = pl.multiple_of(jnp.maximum(i - 1, 0) * WINDOW, WINDOW)
            kk, vv = k_ref[pl.ds(start, 2 * WINDOW), :], v_ref[pl.ds(start, 2 * WINDOW), :]
            q2 = q_ref[rows, :]
            valid = _swa_valid(i, start)[:WINDOW]
            res = []
            for hh in (0, 1):
                hm = _head_mask(hh)
                sink = jnp.max(jnp.where(hm, skv, NEG), axis=1, keepdims=True)
                s = jnp.where(valid, _dot(jnp.where(hm, q2, jnp.zeros_like(q2)), kk, NT), NEG)
                m = jnp.maximum(jnp.max(s, axis=1, keepdims=True), sink)
                p = jnp.exp(s - m)
                l = jnp.sum(p, axis=1, keepdims=True) + jnp.exp(sink - m)
                res.append((_dot(p.astype(BF16), vv) / l, m + jnp.log(l)))
            o = jnp.where(first, res[0][0], res[1][0])
            o_ref[rows, :] = o
            lse_ref[rows, :] = jnp.where(first, res[0][1], res[1][1])
            g = gate_ref[rows, :]
            y_ref[rows, :] = (o * (g * _sigmoid(g))).astype(BF16)

    blk = pl.BlockSpec((SWQ * WINDOW, LANES), lambda p, i: (i, p))
    gate = pl.BlockSpec((SWQ * WINDOW, LANES), lambda p, i: (i, gate_col + p))
    full = pl.BlockSpec((S, LANES), lambda p, i: (0, p // 2))
    return pl.pallas_call(
        body, name="swa_fwd", grid=(NH // 2, S // (SWQ * WINDOW)),
        in_specs=[blk, full, full, pl.BlockSpec((1, LANES), lambda p, i: (0, p)), gate],
        out_specs=[blk, blk, blk],
        out_shape=[jax.ShapeDtypeStruct((S, D), F32)] * 2 + [jax.ShapeDtypeStruct((S, D), BF16)],
        compiler_params=_params(2))(q, kdup, vdup, sinks_t, proj)


def _swa_bwd(q, kdup, vdup, sinks_t, o, lse, dy, proj, gate_col):
    def body(q_ref, k_ref, v_ref, sk_ref, o_ref, lse_ref, dy_ref, gate_ref, dq_ref, dk_ref, dv_ref, dsk_ref,
             dgate_ref):
        @pl.when(pl.program_id(1) == 0)
        def _():
            dk_ref[...] = jnp.zeros_like(dk_ref)
            dv_ref[...] = jnp.zeros_like(dv_ref)
            dsk_ref[...] = jnp.zeros_like(dsk_ref)

        skv = sk_ref[...]
        first = _head_mask(0)
        sink = _per_head(*[jnp.max(jnp.where(_head_mask(hh), skv, NEG), axis=1, keepdims=True) for hh in (0, 1)])
        for sb in range(SWQ):
            i = pl.program_id(1) * SWQ + sb
            rows = slice(sb * WINDOW, (sb + 1) * WINDOW)
            start = pl.multiple_of(jnp.maximum(i - 1, 0) * WINDOW, WINDOW)
            kk, vv = k_ref[pl.ds(start, 2 * WINDOW), :], v_ref[pl.ds(start, 2 * WINDOW), :]
            do2, dgate = _gate_grads(dy_ref[rows, :], o_ref[rows, :], gate_ref[rows, :])
            dgate_ref[rows, :] = dgate.astype(BF16)
            do2b = do2.astype(BF16)
            prod, lse2 = do2b.astype(F32) * o_ref[rows, :], lse_ref[rows, :]
            qs, dos = _both_heads(q_ref[rows, :]), _both_heads(do2b)
            delta = jnp.concatenate([jnp.sum(jnp.where(_head_mask(hh), prod, 0.0), axis=1, keepdims=True)
                                     for hh in (0, 1)], axis=0)
            lse_h = jnp.concatenate([jnp.max(jnp.where(_head_mask(hh), lse2, NEG), axis=1, keepdims=True)
                                     for hh in (0, 1)], axis=0)
            p = jnp.where(_swa_valid(i, start), jnp.exp(_dot(qs, kk, NT) - lse_h), 0.0)
            dsb = (p * (_dot(dos, vv, NT) - delta)).astype(BF16)
            dk_ref[pl.ds(start, 2 * WINDOW), :] += _dot(dsb, qs, TN)
            dv_ref[pl.ds(start, 2 * WINDOW), :] += _dot(p.astype(BF16), dos, TN)
            dq_ref[rows, :] = _unstack(_dot(dsb, kk)) * SCALE
            t = jnp.exp(sink - lse_h) * delta
            dsk_ref[...] += -jnp.where(first, jnp.sum(t[:WINDOW], axis=0, keepdims=True),
                                       jnp.sum(t[WINDOW:], axis=0, keepdims=True))

    blk = pl.BlockSpec((SWQ * WINDOW, LANES), lambda p, i: (i, p))
    full = pl.BlockSpec((S, LANES), lambda p, i: (0, p // 2))
    acc = pl.BlockSpec((S, LANES), lambda p, i: (0, p))
    sk = pl.BlockSpec((1, LANES), lambda p, i: (0, p))
    gate = pl.BlockSpec((SWQ * WINDOW, LANES), lambda p, i: (i, gate_col + p))
    return pl.pallas_call(
        body, name="swa_bwd", grid=(NH // 2, S // (SWQ * WINDOW)),
        in_specs=[blk, full, full, sk, blk, blk, blk, gate],
        out_specs=[blk, acc, acc, sk, blk],
        out_shape=[jax.ShapeDtypeStruct((S, D), F32)] * 3 + [jax.ShapeDtypeStruct((1, D), F32),
                                                              jax.ShapeDtypeStruct((S, D), BF16)],
        compiler_params=_params(2))(q, kdup, vdup, sinks_t, o, lse, dy, proj)


def _adamw_math(w, g, m, v):
    m = ADAM_B1 * m + (1.0 - ADAM_B1) * g
    v = ADAM_B2 * v + (1.0 - ADAM_B2) * jnp.square(g)
    m_hat = m / (1.0 - ADAM_B1 ** ADAM_STEP)
    v_hat = v / (1.0 - ADAM_B2 ** ADAM_STEP)
    delta = -ADAM_LR * (m_hat / (jnp.sqrt(v_hat) + ADAM_EPS) + ADAM_WD * w)
    return delta, m, v


def _adamw(w, g, m, v, name):
    r, c = w.shape
    tr = min(r, 128)

    def body(w_ref, g_ref, m_ref, v_ref, d_ref, mo_ref, vo_ref):
        d_ref[...], mo_ref[...], vo_ref[...] = _adamw_math(w_ref[...], g_ref[...], m_ref[...], v_ref[...])

    spec = pl.BlockSpec((tr, c), lambda i: (i, 0))
    return pl.pallas_call(
        body, name=name, grid=(r // tr,), in_specs=[spec] * 4, out_specs=[spec] * 3,
        out_shape=[jax.ShapeDtypeStruct((r, c), F32)] * 3, compiler_params=_params(1))(w, g, m, v)


SUM_TILE = 128


FLAT_BLOCK = 257 * 1024


def _tiles(shape, axis, lead=0):
    if len(shape) == 1:
        count = shape[0] // FLAT_BLOCK
        return (FLAT_BLOCK,), count, lambda pos, *lead_idx: (sum(k * count for k in lead_idx) + pos,)
    r, c = shape
    blk = (SUM_TILE, c) if axis == 0 else (r, SUM_TILE)
    count = shape[axis] // SUM_TILE

    def index(pos, *lead_idx):
        return tuple(lead_idx) + ((pos, 0) if axis == 0 else (0, pos))

    return (None,) * lead + blk, count, index


def _adamw_halves(w, g_mine, g_theirs, m, v, axis, name):
    blk, count, index = _tiles(w.shape, axis)
    per_half = count // 2

    def body(w_ref, a_ref, b_ref, m_ref, v_ref, g_ref, d_ref, mo_ref, vo_ref):
        is_mine = pl.program_id(0) // per_half == lax.axis_index("c")
        g = jnp.where(is_mine, a_ref[...], b_ref[...])
        g_ref[...] = g
        d_ref[...], mo_ref[...], vo_ref[...] = _adamw_math(w_ref[...], g, m_ref[...], v_ref[...])

    spec = pl.BlockSpec(blk, lambda i: index(i))
    half = pl.BlockSpec(blk, lambda i: index(i % per_half))
    return pl.pallas_call(
        body, name=name, grid=(count,), in_specs=[spec, half, half, spec, spec], out_specs=[spec] * 4,
        out_shape=[jax.ShapeDtypeStruct(w.shape, F32)] * 4, compiler_params=_params(1))(w, g_mine, g_theirs, m, v)


def _chip_sum(blocks, from_sibling, axis, name):
    flat = blocks.ndim == 1
    blk, count, index = _tiles((from_sibling.shape[0] // NCHIP,) if flat else from_sibling.shape[1:], axis, lead=1)

    def body(lo_ref, hi_ref, p_ref, o32, o16):
        mine = jnp.where(lax.axis_index("c") == 0, lo_ref[...], hi_ref[...])
        acc = mine + p_ref[...]
        o32[...] = acc
        o16[...] = acc.astype(BF16)

    half = pl.BlockSpec(blk, lambda k, i: index(i, k))
    if flat:
        lo = pl.BlockSpec(blk, lambda k, i: (2 * count * k + i,))
        hi = pl.BlockSpec(blk, lambda k, i: (2 * count * k + count + i,))
    else:
        lo, hi = half, pl.BlockSpec(blk, lambda k, i: index(i + count, k))
    return pl.pallas_call(
        body, name=name, grid=(NCHIP, count), in_specs=[lo, hi, half], out_specs=[half, half],
        out_shape=[jax.ShapeDtypeStruct(from_sibling.shape, F32), jax.ShapeDtypeStruct(from_sibling.shape, BF16)],
        compiler_params=_params(2))(blocks, blocks, from_sibling)


def _mesh_sum(own, parts, axis, name):
    blk, count, index = _tiles(own.shape, axis)
    n = NCHIP - 1

    def body(a_ref, *refs):
        acc = a_ref[...]
        for k in range(n):
            acc = acc + refs[k][...].astype(F32)
        refs[n][...] = acc

    spec = pl.BlockSpec(blk, lambda i: index(i))
    if own.ndim == 1:
        part = [pl.BlockSpec(blk, lambda i, k=k: (k * count + i,)) for k in range(n)]
    else:
        part = [pl.BlockSpec((None,) + blk, lambda i, k=k: (k,) + index(i)) for k in range(n)]
    return pl.pallas_call(
        body, name=name, grid=(count,), in_specs=[spec] + part,
        out_specs=spec, out_shape=jax.ShapeDtypeStruct(own.shape, F32),
        compiler_params=_params(1))(own, *([parts] * n))


def _sum_stack(parts, name):
    n = parts.shape[0]

    def body(p_ref, o_ref):
        acc = p_ref[0]
        for k in range(1, n):
            acc = acc + p_ref[k]
        o_ref[...] = acc

    return pl.pallas_call(body, name=name, out_shape=jax.ShapeDtypeStruct(parts.shape[1:], F32))(parts)


def _coords():
    return lax.axis_index("x"), lax.axis_index("y"), lax.axis_index("c")


def _chip(who):
    return 2 * who[0] + who[1]


def _flip(who, mask):
    return tuple((1 - v) if b else v for v, b in zip(who, mask))


def _transfer(transfers, t, I, O, ssem, rsem, receiving):
    tr, me = transfers[t], _coords()
    peer = _flip(me, tr["mask"])
    return pltpu.make_async_remote_copy(
        src_ref=tr["src"](I, O, me), dst_ref=tr["dst"](I, O, peer if receiving else me),
        send_sem=ssem.at[t], recv_sem=rsem.at[t], device_id=peer, device_id_type=MESH)


def _start_transfers(transfers, I, O, ssem, rsem, onward):
    arrived = set()
    for t, tr in enumerate(transfers):
        after = tr.get("after")
        if (after is not None) != onward:
            continue
        if after is not None and after not in arrived:
            _transfer(transfers, after, I, O, ssem, rsem, True).wait_recv()
            arrived.add(after)
        _transfer(transfers, t, I, O, ssem, rsem, False).start()


def _finish_transfers(transfers, I, O, ssem, rsem):
    passed_on = {tr["after"] for tr in transfers if tr.get("after") is not None}
    for t in range(len(transfers)):
        if t not in passed_on:
            _transfer(transfers, t, I, O, ssem, rsem, True).wait_recv()
    for t in range(len(transfers)):
        _transfer(transfers, t, I, O, ssem, rsem, False).wait_send()


def _own_copies(own, I, O, stage, lsem, leg):
    for n, (src, dst) in enumerate(own):
        me = _coords()
        bring =pltpu.make_async_copy(src(I, O, me), stage[n], lsem.at[2 * n])
        put = pltpu.make_async_copy(stage[n], dst(I, O, me), lsem.at[2 * n + 1])
        if leg == 0:
            bring.start()
        elif leg == 1:
            bring.wait()
            put.start()
        else:
            put.wait()


def _own_scratch(own, ins):
    return [pltpu.VMEM(ins[n].shape, ins[n].dtype) for n in range(len(own))], pltpu.SemaphoreType.DMA((max(2 * len(own), 1),))


def _exchange(name, ins, outs, transfers, own=()):
    ni, no = len(ins), len(outs)
    nt = len(transfers)
    stages, stage_sems = _own_scratch(own, ins)

    def body(*refs):
        I, O = refs[:ni], refs[ni:ni + no]
        ssem, rsem, lsem = refs[ni + no:ni + no + 3]
        stage = refs[ni + no + 3:]
        _own_copies(own, I, O, stage, lsem, 0)
        _start_transfers(transfers, I, O, ssem, rsem, False)
        _own_copies(own, I, O, stage, lsem, 1)
        _start_transfers(transfers, I, O, ssem, rsem, True)
        _finish_transfers(transfers, I, O, ssem, rsem)
        _own_copies(own, I, O, stage, lsem, 2)

    hbm = pl.BlockSpec(memory_space=pltpu.HBM)
    return pl.pallas_call(
        body, name=name, in_specs=[hbm] * ni, out_specs=[hbm] * no,
        out_shape=[jax.ShapeDtypeStruct(s, d) for s, d in outs],
        scratch_shapes=[pltpu.SemaphoreType.DMA((nt,)), pltpu.SemaphoreType.DMA((nt,)), stage_sems] + stages,
        compiler_params=pltpu.CompilerParams(has_side_effects=True, vmem_limit_bytes=VMEM_LIMIT))(*ins)


CHIP_MASKS = [(0, 1, 0), (1, 0, 0), (1, 1, 0)]
SIBLING = (0, 0, 1)


def _half(shape2d, axis, which):
    n = shape2d[axis] // 2
    cut = pl.ds(pl.multiple_of(which * n, n), n)
    return (cut, slice(None)) if axis == 0 else (slice(None), cut)


class _Riding:
    def __init__(self, transfers, ins, outs, own=()):
        self.transfers, self.ins, self.outs, self.own = transfers, list(ins), list(outs), list(own)
        hbm = pl.BlockSpec(memory_space=pltpu.HBM)
        self.in_specs, self.out_specs = [hbm] * len(self.ins), [hbm] * len(self.outs)
        self.out_shape = [jax.ShapeDtypeStruct(s, d) for s, d in self.outs]
        stages, stage_sems = _own_scratch(self.own, self.ins)
        self.scratch = [pltpu.SemaphoreType.DMA((max(len(transfers), 1),))] * 2 + [stage_sems] + stages

    def hooks(self, I, O, ssem, rsem, lsem, *stage, first, middle, last):
        tr, own = self.transfers, self.own

        @pl.when(first)
        def _():
            _own_copies(own, I, O, stage, lsem, 0)
            _start_transfers(tr, I, O, ssem, rsem, False)

        if own or any(t.get("after") is not None for t in tr):
            @pl.when(middle)
            def _():
                _own_copies(own, I, O, stage, lsem, 1)
                _start_transfers(tr, I, O, ssem, rsem, True)

        def at_end():
            @pl.when(last)
            def _():
                _finish_transfers(tr, I, O, ssem, rsem)
                _own_copies(own, I, O, stage, lsem, 2)

        return at_end


def _stretch(n, pos):
    return (pl.ds(pos * n if isinstance(pos, int) else pl.multiple_of(pos * n, n), n),)


def _gather_plan(shards, axes):
    def half(a, who):
        if shards[a].ndim == 1:
            return _stretch(shards[a].shape[0] // 2, who[2])
        return _half(shards[a].shape, axes[a], who[2])

    def landed(a, chip, who):
        if shards[a].ndim == 1:
            return _stretch(shards[a].shape[0] // 2, 2 * chip + who[2])
        return (chip,) + half(a, who)

    over_ici, onward = [], []
    for a in range(len(shards)):
        for mask in CHIP_MASKS:
            over_ici.append(dict(
                mask=mask,
                src=lambda I, O, me, a=a: I[a].at[half(a, me)],
                dst=lambda I, O, who, a=a: O[a].at[landed(a, _chip(who), who)]))
            onward.append(dict(
                mask=SIBLING, after=len(over_ici) - 1,
                src=lambda I, O, me, a=a, mask=mask: O[a].at[landed(a, _chip(_flip(me, mask)), me)],
                dst=lambda I, O, who, a=a, mask=mask: O[a].at[landed(a, _chip(_flip(who, mask)), who)]))
    outs = [((NCHIP * s.shape[0],) if s.ndim == 1 else (NCHIP,) + s.shape, s.dtype) for s in shards]

    def whole(a, chip):
        return _stretch(shards[a].shape[0], chip) if shards[a].ndim == 1 else (chip,)

    own = [(lambda I, O, me, a=a: I[a], lambda I, O, me, a=a: O[a].at[whole(a, _chip(me))])
           for a in range(len(shards))]
    return over_ici + onward, outs, own


def _gather_shards(shards, axes):
    transfers, outs, own = _gather_plan(shards, axes)
    return _exchange("gather_weights", shards, outs, transfers, own)


def _to_sibling(arrs, name):
    transfers = [dict(mask=SIBLING, src=lambda I, O, me, a=a: I[a], dst=lambda I, O, who, a=a: O[a])
                 for a in range(len(arrs))]
    return _exchange(name, arrs, [(t.shape, t.dtype) for t in arrs], transfers)


def _halves_to_sibling(blocks, axes, name):
    def cut(a, which):
        return (slice(None),) + _half(blocks[a].shape[1:], axes[a], which)

    transfers, outs = [], []
    for a, (b, ax) in enumerate(zip(blocks, axes)):
        if b.ndim == 1:
            h = b.shape[0] // NCHIP // 2
            for k in range(NCHIP):
                transfers.append(dict(mask=SIBLING,
                                      src=lambda I, O, me, a=a, k=k, h=h: I[a].at[_stretch(h, 2 * k + 1 - me[2])],
                                      dst=lambda I, O, who, a=a, k=k, h=h: O[a].at[_stretch(h, k)]))
            outs.append(((NCHIP * h,), b.dtype))
        else:
            transfers.append(dict(mask=SIBLING, src=lambda I, O, me, a=a: I[a].at[cut(a, 1 - me[2])],
                                  dst=lambda I, O, who, a=a: O[a]))
            shape = list(b.shape)
            shape[ax + 1] //= 2
            outs.append((tuple(shape), b.dtype))
    return _exchange(name, blocks, outs, transfers)


def _scatter_plan(tb):
    def slot(a, k):
        return (k,) if tb[a].ndim == 3 else _stretch(tb[a].shape[0] // NCHIP, k)

    transfers = []
    for a in range(len(tb)):
        for n, mask in enumerate(CHIP_MASKS):
            transfers.append(dict(
                mask=mask,
                src=lambda I, O, me, a=a, mask=mask: I[a].at[slot(a, _chip(_flip(me, mask)))],
                dst=lambda I, O, who, a=a, n=n: O[a].at[slot(a, n)]))
    outs = [((3,) + t.shape[1:] if t.ndim == 3 else (3 * (t.shape[0] // NCHIP),), t.dtype) for t in tb]
    return transfers, outs


def _scatter_chip_sums(tb):
    transfers, outs = _scatter_plan(tb)
    return _exchange("scatter_grads", tb, outs, transfers)


def _gather_small(vec):
    def slot(who):
        return 4 * who[0] + 2 * who[1] + who[2]

    masks = [(m >> 2 & 1, m >> 1 & 1, m & 1) for m in range(1, 8)]
    transfers = [dict(mask=mask, src=lambda I, O, me: I[0], dst=lambda I, O, who: O[0].at[slot(who)])
                 for mask in masks]
    own = [(lambda I, O, me: I[0], lambda I, O, me: O[0].at[slot(me)])]
    return _exchange("gather_small", [vec], [((8,) + vec.shape, vec.dtype)], transfers, own)[0]


def _rope_tables(positions):
    half = ROT // 2
    inv_freq = jnp.power(jnp.float32(THETA), -jnp.arange(0, ROT, 2, dtype=F32) / ROT)
    ang = positions.astype(F32)[:, None] * inv_freq[None, :]
    cos, sin = jnp.cos(ang), jnp.sin(ang)
    one, zero, z8 = jnp.ones((S, HD - ROT), F32), jnp.zeros((S, HD - ROT), F32), jnp.zeros((S, half), F32)
    c = jnp.concatenate([cos, cos, one], axis=1)
    a = jnp.concatenate([-sin, z8, zero], axis=1)
    b = jnp.concatenate([z8, sin, zero], axis=1)
    return tuple(jnp.tile(t, (1, 2)) for t in (c, a, b))


def _tile_heads(g, w):
    return jnp.tile(g.reshape(1, HD), (1, w // HD))


def _fold_heads(dg):
    return dg.reshape(-1, HD).sum(axis=0)


def _pad_lanes(a):
    return jnp.pad(a, ((0, 0), (0, LANES - a.shape[1])))


def _local_step(x, target, positions, wt, fetch, late_weights, begin_reduce):
    rope = _rope_tables(positions)
    w1t = wt["w_in_a_t"]
    f_row = 3 * D // LANES
    wg_t = w1t[3 * D + NH:]
    in_b_block = lambda c: pl.BlockSpec((None, TN_, TN_), lambda j, i: (c, j, 0))
    b_pad = _pad_lanes(wt["b_forget"].reshape(1, NH))
    qg_a, kg_a = _tile_heads(wt["qnorm_a_g"], D), _tile_heads(wt["knorm_a_g"], D)
    qg_b, kg_b = _tile_heads(wt["qnorm_b_g"], D), _tile_heads(wt["knorm_b_g"], KVW)
    norm_a, kv_g, norm_b = wt["norm_a_g"].reshape(1, D), wt["kv_norm_g"].reshape(1, D), wt["norm_b_g"].reshape(1, D)
    sinks_t = jnp.repeat(wt["sinks"].reshape(1, NH), HD, axis=1)

    (u_a,) = _rmsnorm_fwd(x, [norm_a], "norm_a")
    qkv = _mm("proj_a", S, 3 * D, [(u_a, _a_rows(D), w1t, _b_rows(D), NT)])
    fpad = _mm("proj_f", S, LANES, [(u_a, _a_rows(D), w1t, _b_rows(D, row0=f_row, tn=LANES), NT)], tn=LANES)
    gate_a = _mm("proj_gate_a", S, D, [(u_a, _a_rows(D), wg_t, _b_rows(D), NT)])
    q_a, k_a, v_a = _a_post(qkv, qg_a, kg_a)
    ct = _forget_cumsum(fpad, b_pad)
    ct2 = ct[:NH].reshape(NH // 2, 2, S)
    o_a, lse_a, y_a, fetched = _fox_fwd(q_a, k_a, v_a, ct2, gate_a, fetch)
    wt = {**wt, **late_weights(fetched)}
    w_in_b = wt["w_in_b"]
    h1 = _mm("out_a", S, D, [(y_a, _a_rows(D), wt["w_out_a"], _b_cols(D), None)], add=x)
    u_kv, u_b = _rmsnorm_fwd(h1, [kv_g, norm_b], "norm_b")
    kv = _mm("proj_kv", S, 2 * KVW, [(u_kv, _a_rows(D), wt["w_kv"], _b_cols(D), None)])
    pb = _mm("proj_b", S, 2 * D,
             [(u_b, _a_rows(D), w_in_b, pl.BlockSpec((None, D, TN_), lambda j, i: (j, 0, 0)), None)])
    q_b, kdup, vdup = _b_post(pb, kv, qg_b, kg_b, rope)
    gate_b_col = D // LANES
    o_b, lse_b, y_b = _swa_fwd(q_b, kdup, vdup, sinks_t, pb, gate_b_col)
    out = _mm("out_b", S, D, [(y_b, _a_rows(D), wt["w_out_b"], _b_cols(D), None)], add=h1)
    d_out, d_out_b, sq = _loss_head(out, target)

    g = {}
    g["w_out_b"] = _mm("dw_out_b", D, D, [(y_b, _a_cols(S), d_out_b, _b_cols(S), TN)])
    d_y_b = _mm("dy_b", S, D, [(d_out_b, _a_rows(D), wt["w_out_b"], _b_rows(D), NT)])
    dq_b, dkdup, dvdup, dsk, d_gate_b = _swa_bwd(q_b, kdup, vdup, sinks_t, o_b, lse_b, d_y_b, pb, gate_b_col)
    g["sinks"] = dsk[0, ::HD]
    d_qb_raw, dg = _headnorm_bwd(pb, 0, qg_b, dq_b, rope, "qnorm_b_bwd")
    g["qnorm_b_g"] = _fold_heads(dg)
    d_pb = [d_qb_raw, d_qb_raw, d_gate_b, d_gate_b]
    g["w_in_b"] = jnp.concatenate([
        _mm("dw_in_b_q", D, D, [(u_b, _a_cols(S), d_qb_raw, _b_cols(S), TN)], stacked=True),
        _mm("dw_in_b_gate", D, D, [(u_b, _a_cols(S), d_gate_b, _b_cols(S), TN)], stacked=True)], axis=0)
    d_u_b = _mm("du_b", S, D, [(d_pb[c], _a_rows(TN_, col=c % 2), w_in_b, in_b_block(c), NT) for c in range(NCHIP)])
    d_kv, dg = _kv_bwd(dkdup, dvdup, kv, kg_b, rope)
    g["knorm_b_g"] = _fold_heads(dg)
    g["w_kv"] = _mm("dw_kv", D, 2 * KVW, [(u_kv, _a_cols(S), d_kv, _b_cols(S), TN)])
    d_u_kv = _mm("du_kv", S, D, [(d_kv, _a_rows(2 * KVW), wt["w_kv"], _b_rows(2 * KVW), NT)])
    d_h1, d_h1_b, g["kv_norm_g"], g["norm_b_g"] = _rmsnorm_bwd(h1, [kv_g, norm_b], [d_u_kv, d_u_b], d_out, "norm_b_bwd")
    g["w_out_a"] = _mm("dw_out_a", D, D, [(y_a, _a_cols(S), d_h1_b, _b_cols(S), TN)])
    d_y_a = _mm("dy_a", S, D, [(d_h1_b, _a_rows(D), wt["w_out_a"], _b_rows(D), NT)])
    riding, so_far = begin_reduce({n: g[n] for n in LATE})
    dq_a, dk_a, dv_a, dct, d_gate_a, arrived = _fox_bwd(q_a, k_a, v_a, ct2, o_a, lse_a, d_y_a, gate_a, riding)
    dct_pad = jnp.pad(dct.reshape(NH, S), ((0, LANES - NH), (0, 0)))
    d_f, db = _forget_bwd(dct_pad, fpad, b_pad)
    g["b_forget"] = db[0, :NH]
    d_q_raw, dg = _headnorm_bwd(qkv, 0, qg_a, dq_a, None, "qnorm_a_bwd")
    g["qnorm_a_g"] = _fold_heads(dg)
    d_k_raw, dg = _headnorm_bwd(qkv, 1, kg_a, dk_a, None, "knorm_a_bwd")
    g["knorm_a_g"] = _fold_heads(dg)
    pieces = [("q", d_q_raw), ("k", d_k_raw), ("v", dv_a), ("gate", d_gate_a)]
    dw = {n: _mm("dw_in_a_" + n, D, D, [(t, _a_cols(S), u_a, _b_cols(S), TN)]) for n, t in pieces}
    dw_f = _mm("dw_in_a_f", LANES, D, [(d_f, _a_cols(S, tm=LANES), u_a, _b_cols(S), TN)], tm=LANES)
    g["w_in_a"] = jnp.concatenate([dw["q"], dw["k"], dw["v"], dw_f[:NH], dw["gate"]], axis=0)
    riding, so_far_first = begin_reduce({"w_in_a": g["w_in_a"]})
    d_u_a, arrived_first = _mm("du_a", S, D, [
        (d_q_raw, _a_rows(D), w1t, _b_cols(D, row=0), None), (d_k_raw, _a_rows(D), w1t, _b_cols(D, row=1), None),
        (dv_a, _a_rows(D), w1t, _b_cols(D, row=2), None), (d_gate_a, _a_rows(D), wg_t, _b_cols(D), None),
        (d_f, _a_rows(LANES), w1t, _b_cols(LANES, row=f_row), None)], riding=riding)
    d_x, _, g["norm_a_g"] = _rmsnorm_bwd(x, [norm_a], [d_u_a], d_h1, "norm_a_bwd")
    return sq, d_x, g, (list(so_far_first) + list(so_far), list(arrived_first) + list(arrived))


BIG = ["w_in_a", "w_out_a", "w_kv", "w_in_b", "w_out_b"]
LATE = BIG[1:]
SPLIT = {"w_in_a": None, "w_out_a": 0, "w_kv": 0, "w_in_b": 0, "w_out_b": 0}
SMALL = ["norm_a_g", "b_forget", "qnorm_a_g", "knorm_a_g", "kv_norm_g", "knorm_b_g", "norm_b_g", "qnorm_b_g", "sinks"]
NAMES = ["norm_a_g", "w_in_a", "b_forget", "qnorm_a_g", "knorm_a_g", "w_out_a", "kv_norm_g", "w_kv", "knorm_b_g",
         "norm_b_g", "w_in_b", "qnorm_b_g", "sinks", "w_out_b"]


def _pack(vals):
    flat = []
    for v in vals:
        v = v.reshape(-1)
        flat.append(jnp.pad(v, (0, -v.shape[0] % LANES)))
    flat = jnp.concatenate(flat)
    flat = jnp.pad(flat, (0, -flat.shape[0] % (8 * LANES)))
    return flat.reshape(-1, LANES)


def _unpack(packed, shapes):
    flat, out, off = packed.reshape(-1), [], 0
    for s in shapes:
        n = int(np.prod(s))
        out.append(flat[off:off + n].reshape(s))
        off += n + (-n % LANES)
    return out


def kernel(x, positions, norm_a_g, w_in_a, b_forget, qnorm_a_g, knorm_a_g, w_out_a, kv_norm_g, w_kv, knorm_b_g, norm_b_g, w_in_b, qnorm_b_g, sinks, w_out_b, loss_target, m_norm_a_g, m_w_in_a, m_b_forget, m_qnorm_a_g, m_knorm_a_g, m_w_out_a, m_kv_norm_g, m_w_kv, m_knorm_b_g, m_norm_b_g, m_w_in_b, m_qnorm_b_g, m_sinks, m_w_out_b, v_norm_a_g, v_w_in_a, v_b_forget, v_qnorm_a_g, v_knorm_a_g, v_w_out_a, v_kv_norm_g, v_w_kv, v_knorm_b_g, v_norm_b_g, v_w_in_b, v_qnorm_b_g, v_sinks, v_w_out_b):
    w = dict(norm_a_g=norm_a_g, w_in_a=w_in_a, b_forget=b_forget, qnorm_a_g=qnorm_a_g, knorm_a_g=knorm_a_g,
             w_out_a=w_out_a, kv_norm_g=kv_norm_g, w_kv=w_kv, knorm_b_g=knorm_b_g, norm_b_g=norm_b_g,
             w_in_b=w_in_b, qnorm_b_g=qnorm_b_g, sinks=sinks, w_out_b=w_out_b)
    m = dict(norm_a_g=m_norm_a_g, w_in_a=m_w_in_a, b_forget=m_b_forget, qnorm_a_g=m_qnorm_a_g, knorm_a_g=m_knorm_a_g,
             w_out_a=m_w_out_a, kv_norm_g=m_kv_norm_g, w_kv=m_w_kv, knorm_b_g=m_knorm_b_g, norm_b_g=m_norm_b_g,
             w_in_b=m_w_in_b, qnorm_b_g=m_qnorm_b_g, sinks=m_sinks, w_out_b=m_w_out_b)
    v = dict(norm_a_g=v_norm_a_g, w_in_a=v_w_in_a, b_forget=v_b_forget, qnorm_a_g=v_qnorm_a_g, knorm_a_g=v_knorm_a_g,
             w_out_a=v_w_out_a, kv_norm_g=v_kv_norm_g, w_kv=v_w_kv, knorm_b_g=v_knorm_b_g, norm_b_g=v_norm_b_g,
             w_in_b=v_w_in_b, qnorm_b_g=v_qnorm_b_g, sinks=v_sinks, w_out_b=v_w_out_b)
    my_chip = 2 * lax.axis_index("x") + lax.axis_index("y")

    def shard2d(t, n):
        if n == "w_in_a":
            return jnp.transpose(t, (2, 0, 1)).reshape(-1)
        return t.reshape(t.shape[-2:])

    def unflat(t, n):
        return jnp.transpose(t.reshape(-1, 1, D), (1, 2, 0)) if n == "w_in_a" else t.reshape(w[n].shape)

    w2d = {n: shard2d(w[n], n) for n in BIG}

    norm_a_rows = jnp.broadcast_to(norm_a_g.reshape(1, D // NCHIP), (16, D // NCHIP))
    w1t, norm_rows = _gather_shards([w2d["w_in_a"].astype(BF16), norm_a_rows], [SPLIT["w_in_a"], 0])
    wt = {"w_in_a_t": w1t.reshape(-1, D), "norm_a_g": norm_rows[:, 0, :].reshape(1, D)}
    for n in SMALL[1:]:
        wt[n] = w[n]
    late_shards = [w2d[n].astype(BF16) for n in LATE]
    late_axes = [SPLIT[n] for n in LATE]
    transfers, outs, own = _gather_plan(late_shards, late_axes)
    fetch = _Riding(transfers, late_shards, outs, own)

    def late_weights(fetched):
        return {n: t if n == "w_in_b" else t.reshape(-1, t.shape[2]) for n, t in zip(LATE, fetched)}

    def as_blocks(t):
        if t.ndim == 3:
            return t
        return t.reshape(-1) if t.shape[0] % (8 * NCHIP) else t.reshape(NCHIP, -1, t.shape[1])

    def begin_reduce(grads):
        names = list(grads)
        axes = [SPLIT[n] for n in names]
        blocks = [as_blocks(grads[n]) for n in names]
        halves = _halves_to_sibling(blocks, axes, "sibling_halves_" + names[0])
        sums = [_chip_sum(blk, part, ax, "chip_sum_" + n) for n, ax, blk, part in zip(names, axes, blocks, halves)]
        bf16 = [s[1] for s in sums]
        transfers, outs = _scatter_plan(bf16)
        return _Riding(transfers, bf16, outs), [s[0] for s in sums]

    sq, d_x, g, (chip_f32, arrived) = _local_step(x[0], loss_target[0], positions, wt, fetch, late_weights,
                                                  begin_reduce)

    small_shapes = [(D,), (NH,), (HD,), (HD,), (D,), (HD,), (D,), (HD,), (NH,), (D,)]
    packed = _pack([g[n] for n in SMALL] + [sq])
    total = _sum_stack(_gather_small(packed), "sum_small")
    small_g = dict(zip(SMALL, _unpack(total, small_shapes)[:-1]))
    loss = 0.5 * jnp.sum(_unpack(total, small_shapes)[-1]) / D
    small_g["norm_a_g"] = lax.dynamic_slice(small_g["norm_a_g"], (my_chip * (D // NCHIP),), (D // NCHIP,))

    axes = [SPLIT[n] for n in BIG]
    halves = []
    for n, ax, t32, parts in zip(BIG, axes, chip_f32, arrived):
        if t32.ndim == 1:
            own = lax.dynamic_slice_in_dim(t32, my_chip * (t32.shape[0] // NCHIP), t32.shape[0] // NCHIP)
        else:
            own = lax.dynamic_index_in_dim(t32, my_chip, axis=0, keepdims=False)
        halves.append(_mesh_sum(own, parts, ax, "mesh_sum_" + n))
    sibling_done = _to_sibling(halves, "finished_halves")

    res = {}
    for n, ax, mine_half, their_half in zip(BIG, axes, halves, sibling_done):
        out4 = _adamw_halves(w2d[n], mine_half, their_half, shard2d(m[n], n), shard2d(v[n], n), ax, "adamw_" + n)
        res[n] = tuple(unflat(t, n) for t in out4)
    sm_g = _pack([small_g[n] for n in SMALL])
    sm = [_pack([d[n] for n in SMALL]) for d in (w, m, v)]
    sm_out = _adamw(sm[0], sm_g, sm[1], sm[2], "adamw_small")
    sm_shapes = [w[n].shape for n in SMALL]
    unpacked = [_unpack(t, sm_shapes) for t in (sm_g,) + tuple(sm_out)]
    for i, n in enumerate(SMALL):
        res[n] = tuple(u[i] for u in unpacked)

    outs = [loss, d_x[None]]
    for k in range(4):
        outs += [res[n][k] for n in NAMES]
    return tuple(outs)
```

```python
import numpy as np
import jax
import jax.numpy as jnp
from jax import lax
from jax.experimental import pallas as pl
from jax.experimental.pallas import tpu as pltpu

F32, BF16 = jnp.float32, jnp.bfloat16
S, D, HD, NH, NKV = 2048, 1024, 64, 16, 4
KVW = NKV * HD
WINDOW = 128
ROT = HD // 4
THETA = 500000.0
EPS = 1e-6
SCALE = HD ** -0.5
LANES = 128
NEG = -1e30
VMEM_LIMIT = 48 * 2 ** 20
ROWS = 256
ATT = 512
SWQ = 4
NCHIP = 4
ADAM_LR, ADAM_B1, ADAM_B2, ADAM_EPS, ADAM_WD, ADAM_STEP = 0.001, 0.9, 0.999, 1e-08, 0.01, 10
NT = (((1,), (1,)), ((), ()))
TN = (((0,), (0,)), ((), ()))
MESH = pl.DeviceIdType.MESH


def _params(n):
    return pltpu.CompilerParams(dimension_semantics=("arbitrary",) * n, vmem_limit_bytes=VMEM_LIMIT)


def _dot(a, b, dims=None):
    if dims is None:
        return jnp.dot(a, b, preferred_element_type=F32)
    return lax.dot_general(a, b, dims, preferred_element_type=F32)


def _dot_split(a, b, n):
    out, rest = None, a
    for _ in range(n):
        hi = rest.astype(BF16)
        term = _dot(hi, b)
        out = term if out is None else out + term
        rest = rest - hi.astype(F32)
    return out


def _seg_mat(w):
    e = (np.arange(w)[:, None] // HD == np.arange(LANES)[None, :]).astype(np.float32)
    return jnp.asarray(e, BF16)


def _spread(r, w):
    head = lax.broadcasted_iota(jnp.int32, (2 * LANES, w), 1) >> 6
    row = lax.broadcasted_iota(jnp.int32, (2 * LANES, w), 0)
    et2 = jnp.where(head == (row & (LANES - 1)), 1.0, 0.0).astype(BF16)
    hi = r.astype(BF16)
    lo = (r - hi.astype(F32)).astype(BF16)
    return _dot(jnp.concatenate([hi, lo], axis=1), et2)


def _head_rstd(x, e):
    ss = _dot_split(x * x, e, 2)
    return _spread(lax.rsqrt(ss * (1.0 / HD) + EPS), x.shape[1])


def _rope(x, c, a, b):
    w = x.shape[1]
    return x * c + pltpu.roll(x, w - ROT // 2, 1) * a + pltpu.roll(x, ROT // 2, 1) * b


def _rope_t(dy, c, a, b):
    w = dy.shape[1]
    return dy * c + pltpu.roll(dy * b, w - ROT // 2, 1) + pltpu.roll(dy * a, ROT // 2, 1)


def _sigmoid(x):
    return 1.0 / (1.0 + jnp.exp(-x))


def _row_spec(shape, ts):
    nd = len(shape)
    if shape[0] == S:
        return pl.BlockSpec((ts,) + tuple(shape[1:]), lambda i: (i,) + (0,) * (nd - 1))
    return pl.BlockSpec(tuple(shape), lambda i: (0,) * nd)


def _rows_call(body, name, ins, outs, ts=ROWS):
    return pl.pallas_call(
        body, name=name, grid=(S // ts,),
        in_specs=[_row_spec(a.shape, ts) for a in ins],
        out_specs=[_row_spec(s, ts) for s, _ in outs],
        out_shape=[jax.ShapeDtypeStruct(s, d) for s, d in outs],
        compiler_params=_params(1))(*ins)


def _col_spec(ts, w, col):
    return pl.BlockSpec((ts, w), lambda i: (i, col))


TM = TN_ = 512
TM_TOKENS = 1024


def _mm(name, m, n, terms, out_dtype=F32, add=None, tm=None, tn=TN_, stacked=False, riding=None, rows_of=None):
    nterm = len(terms)
    if tm is None:
        tm = TM_TOKENS if m == S else TM
    nj, ni_ = n // tn, m // tm
    n_in = 2 * nterm + (add is not None) + (rows_of is not None and rows_of[0] is not None)
    r_in, r_out = (len(riding.ins), len(riding.outs)) if riding is not None else (0, 0)

    def body(*refs):
        if riding is not None:
            j, i = pl.program_id(0), pl.program_id(1)
            at_end = riding.hooks(refs[n_in:n_in + r_in], refs[n_in + r_in + 1:n_in + r_in + 1 + r_out],
                                  *refs[n_in + r_in + 1 + r_out:], first=(j == 0) & (i == 0),
                                  middle=(j == nj // 2) & (i == 0), last=(j == nj - 1) & (i == ni_ - 1))
        acc = None
        for t in range(nterm):
            part = _dot(refs[2 * t][...], refs[2 * t + 1][...], terms[t][4])
            acc = part if acc is None else acc + part
        if add is not None:
            acc = acc + refs[2 * nterm][...]
        refs[n_in + r_in][...] = acc.astype(out_dtype)
        if riding is not None:
            at_end()

    tile = pl.BlockSpec((tm, tn), lambda j, i: (i, j))
    ins, specs = [], []
    for a, a_spec, b, b_spec, _ in terms:
        ins += [a, b]
        specs += [a_spec, b_spec]
    if add is not None:
        ins.append(add)
        specs.append(tile)
    out_spec = pl.BlockSpec((None, tm, tn), lambda j, i: (j, i, 0)) if stacked else tile
    out_shape = jax.ShapeDtypeStruct((nj, m, tn) if stacked else (m, n), out_dtype)
    if rows_of is not None:
        taller, rows, row0 = rows_of
        out_spec = pl.BlockSpec((pl.Element(tm), pl.Element(tn)), lambda j, i: (
            pl.multiple_of(row0 + i * tm, 8), pl.multiple_of(j * tn, LANES)))
        out_shape = jax.ShapeDtypeStruct((rows, n), out_dtype)
        alias = {}
        if taller is not None:
            ins.append(taller)
            specs.append(pl.BlockSpec(memory_space=pltpu.HBM))
            alias = {len(ins) - 1: 0}
        return pl.pallas_call(body, name=name, grid=(nj, ni_), in_specs=specs, out_specs=out_spec,
                              out_shape=out_shape, input_output_aliases=alias, compiler_params=_params(2))(*ins)
    if riding is None:
        return pl.pallas_call(body, name=name, grid=(nj, ni_), in_specs=specs, out_specs=out_spec,
                              out_shape=out_shape, compiler_params=_params(2))(*ins)
    res = pl.pallas_call(
        body, name=name, grid=(nj, ni_), in_specs=specs + riding.in_specs,
        out_specs=[out_spec] + riding.out_specs, out_shape=[out_shape] + riding.out_shape,
        scratch_shapes=riding.scratch, compiler_params=_params(2))(*ins, *riding.ins)
    return res[0], res[1:]


def _a_rows(k, col=0, tm=TM_TOKENS):
    return pl.BlockSpec((tm, k), lambda j, i: (i, col))


def _a_cols(k, tm=TM):
    return pl.BlockSpec((k, tm), lambda j, i: (0, i))


def _b_cols(k, row=0, col0=0, tn=TN_):
    return pl.BlockSpec((k, tn), lambda j, i: (row, col0 + j))


def _b_rows(k, row0=0, tn=TN_):
    return pl.BlockSpec((tn, k), lambda j, i: (row0 + j, 0))


def _rmsnorm_fwd(x, gains, name):
    def body(*refs):
        xv = refs[0][...]
        r = lax.rsqrt(jnp.mean(xv * xv, axis=-1, keepdims=True) + EPS)
        xh = xv * r
        for n in range(len(gains)):
            refs[1 + len(gains) + n][...] = (xh * refs[1 + n][...]).astype(BF16)

    return _rows_call(body, name, [x] + list(gains), [((S, D), BF16)] * len(gains))


def _rmsnorm_bwd(x, gains, dus, dres, name):
    n = len(gains)

    def body(*refs):
        x_ref, g_refs, du_refs, dres_ref = refs[0], refs[1:1 + n], refs[1 + n:1 + 2 * n], refs[1 + 2 * n]
        dx_ref, dxb_ref, dg_refs = refs[2 + 2 * n], refs[3 + 2 * n], refs[4 + 2 * n:]
        xv = x_ref[...]
        r = lax.rsqrt(jnp.mean(xv * xv, axis=-1, keepdims=True) + EPS)
        xh = xv * r
        gy = None
        for m in range(n):
            du = du_refs[m][...]
            part = jnp.sum(du * xh, axis=0, keepdims=True)

            @pl.when(pl.program_id(0) == 0)
            def _(m=m, part=part):
                dg_refs[m][...] = part

            @pl.when(pl.program_id(0) != 0)
            def _(m=m, part=part):
                dg_refs[m][...] += part

            t = du * g_refs[m][...]
            gy = t if gy is None else gy + t
        dx = dres_ref[...] + r * (gy - xh * jnp.mean(gy * xh, axis=-1, keepdims=True))
        dx_ref[...] = dx
        dxb_ref[...] = dx.astype(BF16)

    outs = [((S, D), F32), ((S, D), BF16)] + [((1, D), F32)] * n
    return _rows_call(body, name, [x] + list(gains) + list(dus) + [dres], outs)


def _a_post(qkvg, qg, kg):
    e = _seg_mat(D)

    def body(q_ref, k_ref, v_ref, qg_ref, kg_ref, e_ref, qo, ko, vo):
        ev = e_ref[...]
        qv, kv = q_ref[...], k_ref[...]
        qo[...] = (qv * _head_rstd(qv, ev) * qg_ref[...] * SCALE).astype(BF16)
        ko[...] = (kv * _head_rstd(kv, ev) * kg_ref[...]).astype(BF16)
        vo[...] = v_ref[...].astype(BF16)

    whole = lambda a: pl.BlockSpec(a.shape, lambda i: (0, 0))
    return pl.pallas_call(
        body, name="a_post", grid=(S // ROWS,),
        in_specs=[_col_spec(ROWS, D, 0), _col_spec(ROWS, D, 1), _col_spec(ROWS, D, 2),
                  whole(qg), whole(kg), whole(e)],
        out_specs=[_col_spec(ROWS, D, 0)] * 3,
        out_shape=[jax.ShapeDtypeStruct((S, D), BF16)] * 3,
        compiler_params=_params(1))(qkvg, qkvg, qkvg, qg, kg, e)


def _tri(upper):
    r, c = np.arange(ROWS)[:, None], np.arange(ROWS)[None, :]
    return jnp.asarray((r <= c) if upper else (r >= c), BF16)


def _forget_cumsum(fpad, bpad):
    def body(f_ref, b_ref, u_ref, c_ref, carry):
        @pl.when(pl.program_id(0) == 0)
        def _():
            carry[...] = jnp.zeros_like(carry)

        lf = jax.nn.log_sigmoid(f_ref[...] + b_ref[...])
        blk = _dot_split(lf.T, u_ref[...], 3) + carry[:, 0:1]
        c_ref[...] = blk
        carry[...] = jnp.broadcast_to(blk[:, ROWS - 1:ROWS], carry.shape)

    return pl.pallas_call(
        body, name="forget_cumsum", grid=(S // ROWS,),
        in_specs=[pl.BlockSpec((ROWS, LANES), lambda i: (i, 0)), pl.BlockSpec((1, LANES), lambda i: (0, 0)),
                  pl.BlockSpec((ROWS, ROWS), lambda i: (0, 0))],
        out_specs=pl.BlockSpec((LANES, ROWS), lambda i: (0, i)),
        out_shape=jax.ShapeDtypeStruct((LANES, S), F32),
        scratch_shapes=[pltpu.VMEM((LANES, LANES), F32)],
        compiler_params=_params(1))(fpad, bpad, _tri(True))


def _forget_bwd(dct, fpad, bpad):
    nb = S // ROWS

    def body(dc_ref, f_ref, b_ref, l_ref, df_ref, db_ref, carry):
        @pl.when(pl.program_id(0) == 0)
        def _():
            carry[...] = jnp.zeros_like(carry)
            db_ref[...] = jnp.zeros_like(db_ref)

        blk = _dot_split(dc_ref[...], l_ref[...], 3) + carry[:, 0:1]
        carry[...] = jnp.broadcast_to(blk[:, 0:1], carry.shape)
        df = blk.T * _sigmoid(-(f_ref[...] + b_ref[...]))
        df_ref[...] = df.astype(BF16)
        db_ref[...] += jnp.sum(df, axis=0, keepdims=True)

    return pl.pallas_call(
        body, name="forget_bwd", grid=(nb,),
        in_specs=[pl.BlockSpec((LANES, ROWS), lambda i: (0, nb - 1 - i)),
                  pl.BlockSpec((ROWS, LANES), lambda i: (nb - 1 - i, 0)),
                  pl.BlockSpec((1, LANES), lambda i: (0, 0)), pl.BlockSpec((ROWS, ROWS), lambda i: (0, 0))],
        out_specs=[pl.BlockSpec((ROWS, LANES), lambda i: (nb - 1 - i, 0)), pl.BlockSpec((1, LANES), lambda i: (0, 0))],
        out_shape=[jax.ShapeDtypeStruct((S, LANES), BF16), jax.ShapeDtypeStruct((1, LANES), F32)],
        scratch_shapes=[pltpu.VMEM((LANES, LANES), F32)],
        compiler_params=_params(1))(dct, fpad, bpad, _tri(False))


def _headnorm_bwd(x, col, gain, dy, rope, name):
    e = _seg_mat(D)
    tabs = list(rope) if rope is not None else []

    def body(*refs):
        x_ref, g_ref, dy_ref, e_ref = refs[:4]
        dx_ref, dg_ref = refs[-2:]
        xv, dyv, ev = x_ref[...], dy_ref[...], e_ref[...]
        if rope is not None:
            c, a, b = (jnp.tile(t[...], (1, D // LANES)) for t in refs[4:7])
            dyv = _rope_t(dyv, c, a, b)
        r = _head_rstd(xv, ev)
        xh = xv * r
        part = jnp.sum(dyv * xh, axis=0, keepdims=True)

        @pl.when(pl.program_id(0) == 0)
        def _():
            dg_ref[...] = part

        @pl.when(pl.program_id(0) != 0)
        def _():
            dg_ref[...] += part

        gy = dyv * g_ref[...]
        seg = _spread(_dot_split(gy * xh, ev, 2) * (1.0 / HD), D)
        dx_ref[...] = (r * (gy - xh * seg)).astype(BF16)

    whole = lambda a: pl.BlockSpec(a.shape, lambda i: (0, 0))
    return pl.pallas_call(
        body, name=name, grid=(S // ROWS,),
        in_specs=[_col_spec(ROWS, D, col), whole(gain), _col_spec(ROWS, D, 0), whole(e)]
                 + [pl.BlockSpec((ROWS, LANES), lambda i: (i, 0))] * len(tabs),
        out_specs=[_col_spec(ROWS, D, 0), whole(gain)],
        out_shape=[jax.ShapeDtypeStruct((S, D), BF16), jax.ShapeDtypeStruct((1, D), F32)],
        compiler_params=_params(1))(x, gain, dy, e, *tabs)


def _dup_mat():
    r, c = np.arange(KVW)[:, None], np.arange(2 * KVW)[None, :]
    return (r // HD == c // LANES) & (r % HD == c % HD)


def _fold_mat():
    r, c = np.arange(D)[:, None], np.arange(KVW)[None, :]
    return (r // (2 * LANES) == c // HD) & (r % HD == c % HD)


def _b_post(pb, kv, qg, kg, rope):
    e, ek = _seg_mat(D), _seg_mat(KVW)
    dup = jnp.asarray(_dup_mat(), BF16)

    def body(q_ref, k_ref, v_ref, qg_ref, kg_ref, e_ref, ek_ref, dup_ref, c_ref, a_ref, b_ref, qo, ko, vo):
        c1, a1, b1 = c_ref[...], a_ref[...], b_ref[...]
        qv = q_ref[...]
        qn = qv * _head_rstd(qv, e_ref[...]) * qg_ref[...]
        t = lambda z, n: jnp.tile(z, (1, n))
        qo[...] = (_rope(qn, t(c1, D // LANES), t(a1, D // LANES), t(b1, D // LANES)) * SCALE).astype(BF16)
        kvv = k_ref[...]
        kn = kvv * _head_rstd(kvv, ek_ref[...]) * kg_ref[...]
        kr = _rope(kn, t(c1, KVW // LANES), t(a1, KVW // LANES), t(b1, KVW // LANES)).astype(BF16)
        ko[...] = _dot(kr, dup_ref[...]).astype(BF16)
        vo[...] = _dot(v_ref[...].astype(BF16), dup_ref[...]).astype(BF16)

    whole = lambda a: pl.BlockSpec(a.shape, lambda i: (0, 0))
    tab = pl.BlockSpec((ROWS, LANES), lambda i: (i, 0))
    return pl.pallas_call(
        body, name="b_post", grid=(S // ROWS,),
        in_specs=[_col_spec(ROWS, D, 0), _col_spec(ROWS, KVW, 0), _col_spec(ROWS, KVW, 1),
                  whole(qg), whole(kg), whole(e), whole(ek), whole(dup), tab, tab, tab],
        out_specs=[_col_spec(ROWS, D, 0), _col_spec(ROWS, 2 * KVW, 0), _col_spec(ROWS, 2 * KVW, 0)],
        out_shape=[jax.ShapeDtypeStruct((S, D), BF16), jax.ShapeDtypeStruct((S, 2 * KVW), BF16),
                   jax.ShapeDtypeStruct((S, 2 * KVW), BF16)],
        compiler_params=_params(1))(pb, kv, kv, qg, kg, e, ek, dup, *rope)


def _kv_bwd(dkdup, dvdup, kv, kg, rope):
    ek = _seg_mat(KVW)
    fold = jnp.asarray(_fold_mat(), BF16)

    def body(dk_ref, dv_ref, k_ref, kg_ref, ek_ref, fold_ref, c_ref, a_ref, b_ref, dkv_ref, dg_ref):
        ev, fv = ek_ref[...], fold_ref[...]
        t = lambda z: jnp.tile(z[...], (1, KVW // LANES))
        dk = _rope_t(_dot_split(dk_ref[...], fv, 2), t(c_ref), t(a_ref), t(b_ref))
        dv = _dot_split(dv_ref[...], fv, 2)
        xv = k_ref[...]
        r = _head_rstd(xv, ev)
        xh = xv * r
        part = jnp.sum(dk * xh, axis=0, keepdims=True)

        @pl.when(pl.program_id(0) == 0)
        def _():
            dg_ref[...] = part

        @pl.when(pl.program_id(0) != 0)
        def _():
            dg_ref[...] += part

        gy = dk * kg_ref[...]
        seg = _spread(_dot_split(gy * xh, ev, 2) * (1.0 / HD), KVW)
        dkv_ref[:, 0:KVW] = (r * (gy - xh * seg)).astype(BF16)
        dkv_ref[:, KVW:2 * KVW] = dv.astype(BF16)

    whole = lambda a: pl.BlockSpec(a.shape, lambda i: (0, 0))
    tab = pl.BlockSpec((ROWS, LANES), lambda i: (i, 0))
    return pl.pallas_call(
        body, name="kv_bwd", grid=(S // ROWS,),
        in_specs=[_col_spec(ROWS, D, 0), _col_spec(ROWS, D, 0), _col_spec(ROWS, KVW, 0),
                  whole(kg), whole(ek), whole(fold), tab, tab, tab],
        out_specs=[_col_spec(ROWS, 2 * KVW, 0), whole(kg)],
        out_shape=[jax.ShapeDtypeStruct((S, 2 * KVW), BF16), jax.ShapeDtypeStruct((1, KVW), F32)],
        compiler_params=_params(1))(dkdup, dvdup, kv, kg, ek, fold, *rope)


def _loss_head(out, target):
    def body(o_ref, t_ref, d_ref, db_ref, l_ref):
        diff = o_ref[...] - t_ref[...]
        d = diff * (1.0 / D)
        d_ref[...] = d
        db_ref[...] = d.astype(BF16)

        @pl.when(pl.program_id(0) == 0)
        def _():
            l_ref[...] = jnp.zeros_like(l_ref)

        l_ref[...] += jnp.sum(diff * diff, axis=0, keepdims=True)

    return _rows_call(body, "loss_head", [out, target], [((S, D), F32), ((S, D), BF16), ((1, D), F32)])


def _lane():
    return lax.broadcasted_iota(jnp.int32, (1, LANES), 1)


def _head_mask(hh):
    return (_lane() < HD) if hh == 0 else (_lane() >= HD)


def _fox_fwd(q, k, v, ct, gate, riding):
    nq, npair = S // ATT, NH // 2
    ni, no = len(riding.ins), len(riding.outs)

    def body(q_ref, k_ref, v_ref, c_ref, gate_ref, *rest):
        o_ref, lse_ref, y_ref = rest[ni:ni + 3]
        pair, i = pl.program_id(0), pl.program_id(1)
        at_end = riding.hooks(rest[:ni], rest[ni + 3:ni + 3 + no], *rest[ni + 3 + no:],
                              first=(pair == 0) & (i == 0), middle=(pair == npair // 2) & (i == 0),
                              last=(pair == npair - 1) & (i == nq - 1))
        q2 = q_ref[...]
        qms = [jnp.where(_head_mask(hh), q2, jnp.zeros_like(q2)) for hh in (0, 1)]

        def probs(off, width, m, hh, diag):
            s = _dot(qms[hh], k_ref[pl.ds(off, width), :], NT) - c_ref[hh:hh + 1, pl.ds(off, width)]
            if diag:
                row = i * ATT + lax.broadcasted_iota(jnp.int32, (ATT, width), 0)
                col = off + lax.broadcasted_iota(jnp.int32, (ATT, width), 1)
                s = jnp.where(col <= row, s, NEG)
            m_new = jnp.maximum(m, jnp.max(s, axis=1, keepdims=True))
            p = jnp.exp(s - m_new)
            p_hi = p.astype(BF16)
            return m_new, jnp.exp(m - m_new), p_hi, (p - p_hi.astype(F32)).astype(BF16)

        def weighted(off, width, p_hi, p_lo, hh):
            vj = v_ref[pl.ds(off, width), :]
            v1 = jnp.where(_head_mask(hh), vj, jnp.ones_like(vj))
            return _dot(p_hi, v1) + _dot(p_lo, v1)

        def step(off, width, carry, diag):
            off = pl.multiple_of(off, ATT)
            out = []
            for hh in (0, 1):
                m, acc = carry[hh]
                m, alpha, p_hi, p_lo = probs(off, width, m, hh, diag)
                out.append((m, alpha * acc + weighted(off, width, p_hi, p_lo, hh)))
            return tuple(out)

        one = (jnp.full((ATT, 1), NEG, F32), jnp.zeros((ATT, LANES), F32))
        carry = lax.fori_loop(0, i // 2, lambda j, cr: step(j * (2 * ATT), 2 * ATT, cr, False), (one, one))
        carry = lax.cond(i % 2 == 1, lambda cr: step((i - 1) * ATT, 2 * ATT, cr, True),
                         lambda cr: step(i * ATT, ATT, cr, True), carry)
        res = []
        for hh in (0, 1):
            m, acc = carry[hh]
            l = jnp.max(jnp.where(_head_mask(1 - hh), acc, 0.0), axis=1, keepdims=True)
            res.append((acc / l, m + jnp.log(l)))
        first = _head_mask(0)
        o = jnp.where(first, res[0][0], res[1][0])
        o_ref[...] = o
        lse_ref[...] = jnp.where(first, res[0][1], res[1][1])
        g = gate_ref[...]
        y_ref[...] = (o * (g * _sigmoid(g))).astype(BF16)
        at_end()

    blk = pl.BlockSpec((ATT, LANES), lambda p, i: (i, p))
    full = pl.BlockSpec((S, LANES), lambda p, i: (0, p))
    res = pl.pallas_call(
        body, name="fox_fwd", grid=(npair, nq),
        in_specs=[blk, full, full, pl.BlockSpec((None, 2, S), lambda p, i: (p, 0, 0)), blk] + riding.in_specs,
        out_specs=[blk, blk, blk] + riding.out_specs,
        out_shape=[jax.ShapeDtypeStruct((S, D), F32)] * 2 + [jax.ShapeDtypeStruct((S, D), BF16)] + riding.out_shape,
        scratch_shapes=riding.scratch,
        compiler_params=_params(2))(q, k, v, ct, gate, *riding.ins)
    return res[0], res[1], res[2], res[3:]


def _gate_grads(dy, o, g):
    sg = _sigmoid(g)
    return dy * (g * sg), dy * o * (sg * (1.0 + g * (1.0 - sg)))


def _fox_bwd(q, k, v, ct, o, lse, dy, gate, riding):
    nq, npair = S // ATT, NH // 2
    ni, no = len(riding.ins), len(riding.outs)

    def body(q_ref, k_ref, v_ref, c_ref, o_ref, lse_ref, dy_ref, gate_ref, *rest):
        dq_ref, dk_ref, dvb_ref, dc_ref, dgate_ref = rest[ni:ni + 5]
        dv_ref = rest[ni + 5 + no]
        pair, i = pl.program_id(0), pl.program_id(1)
        at_end = riding.hooks(rest[:ni], rest[ni + 5:ni + 5 + no], *rest[ni + 6 + no:],
                              first=(pair == 0) & (i == 0), middle=(pair == npair // 2) & (i == 0),
                              last=(pair == npair - 1) & (i == nq - 1))

        @pl.when(i == 0)
        def _():
            dk_ref[...] = jnp.zeros_like(dk_ref)
            dv_ref[...] = jnp.zeros_like(dv_ref)
            dc_ref[...] = jnp.zeros_like(dc_ref)

        q2, lse2 = q_ref[...], lse_ref[...]
        do2, dgate = _gate_grads(dy_ref[...], o_ref[...], gate_ref[...])
        dgate_ref[...] = dgate.astype(BF16)
        do2b = do2.astype(BF16)
        prod = do2b.astype(F32) * o_ref[...]
        heads = []
        for hh in (0, 1):
            hm = _head_mask(hh)
            heads.append((jnp.where(hm, q2, jnp.zeros_like(q2)), jnp.where(hm, do2b, jnp.zeros_like(do2b)),
                          jnp.sum(jnp.where(hm, prod, 0.0), axis=1, keepdims=True),
                          jnp.max(jnp.where(hm, lse2, NEG), axis=1, keepdims=True)))

        def step(off, width, dqs, diag):
            off = pl.multiple_of(off, ATT)
            kj, vj = k_ref[pl.ds(off, width), :], v_ref[pl.ds(off, width), :]
            dk, dv, out = None, None, []
            for hh in (0, 1):
                qm, dom, delta, lse_h = heads[hh]
                s = _dot(qm, kj, NT) - c_ref[hh:hh + 1, pl.ds(off, width)]
                p = jnp.exp(s - lse_h)
                if diag:
                    row = i * ATT + lax.broadcasted_iota(jnp.int32, (ATT, width), 0)
                    col = off + lax.broadcasted_iota(jnp.int32, (ATT, width), 1)
                    p = jnp.where(col <= row, p, 0.0)
                ds = p * (_dot(dom, vj, NT) - delta)
                dc_ref[hh:hh + 1, pl.ds(off, width)] += -jnp.sum(ds, axis=0, keepdims=True)
                dsb = ds.astype(BF16)
                dk_h, dv_h = _dot(dsb, qm, TN), _dot(p.astype(BF16), dom, TN)
                dk, dv = (dk_h, dv_h) if dk is None else (dk + dk_h, dv + dv_h)
                out.append(dqs[hh] + _dot(dsb, kj))
            dk_ref[pl.ds(off, width), :] += dk
            dv_ref[pl.ds(off, width), :] += dv
            return tuple(out)

        zero = jnp.zeros((ATT, LANES), F32)
        dqs = lax.fori_loop(0, i // 2, lambda j, acc: step(j * (2 * ATT), 2 * ATT, acc, False), (zero, zero))
        dqs = lax.cond(i % 2 == 1, lambda acc: step((i - 1) * ATT, 2 * ATT, acc, True),
                       lambda acc: step(i * ATT, ATT, acc, True), dqs)
        dq_ref[...] = jnp.where(_head_mask(0), dqs[0], dqs[1]) * SCALE

        @pl.when(i == nq - 1)
        def _():
            dvb_ref[...] = dv_ref[...].astype(BF16)

        at_end()

    blk = pl.BlockSpec((ATT, LANES), lambda p, i: (i, p))
    full = pl.BlockSpec((S, LANES), lambda p, i: (0, p))
    cspec = pl.BlockSpec((None, 2, S), lambda p, i: (p, 0, 0))
    res = pl.pallas_call(
        body, name="fox_bwd", grid=(npair, nq),
        in_specs=[blk, full, full, cspec, blk, blk, blk, blk] + riding.in_specs,
        out_specs=[blk, full, full, cspec, blk] + riding.out_specs,
        out_shape=[jax.ShapeDtypeStruct((S, D), F32)] * 2 + [jax.ShapeDtypeStruct((S, D), BF16),
                                                              jax.ShapeDtypeStruct((npair, 2, S), F32),
                                                              jax.ShapeDtypeStruct((S, D), BF16)]
                  + riding.out_shape,
        scratch_shapes=[pltpu.VMEM((S, LANES), F32)] + riding.scratch,
        compiler_params=_params(2))(q, k, v, ct, o, lse, dy, gate, *riding.ins)
    return res[0], res[1], res[2], res[3], res[4], res[5:]


def _both_heads(x):
    return jnp.concatenate([jnp.where(_head_mask(hh), x, jnp.zeros_like(x)) for hh in (0, 1)], axis=0)


def _per_head(col0, col1):
    return jnp.concatenate([jnp.broadcast_to(col0, (WINDOW, 1)), jnp.broadcast_to(col1, (WINDOW, 1))], axis=0)


def _unstack(x2):
    return jnp.where(_head_mask(0), x2[:WINDOW], x2[WINDOW:])


def _swa_valid(i, start):
    r = lax.broadcasted_iota(jnp.int32, (2 * WINDOW, 2 * WINDOW), 0)
    qabs = i * WINDOW + jnp.where(r >= WINDOW, r - WINDOW, r)
    kabs = start + lax.broadcasted_iota(jnp.int32, (2 * WINDOW, 2 * WINDOW), 1)
    return (kabs <= qabs) & (qabs - kabs < WINDOW)


def _swa_fwd(q, kdup, vdup, sinks_t, proj, gate_col):
    def body(q_ref, k_ref, v_ref, sk_ref, gate_ref, o_ref, lse_ref, y_ref):
        skv = sk_ref[...]
        first = _head_mask(0)
        for sb in range(SWQ):
            i = pl.program_id(1) * SWQ + sb
            rows = slice(sb * WINDOW, (sb + 1) * WINDOW)
            start = pl.multiple_of(jnp.maximum(i - 1, 0) * WINDOW, WINDOW)
            kk, vv = k_ref[pl.ds(start, 2 * WINDOW), :], v_ref[pl.ds(start, 2 * WINDOW), :]
            q2 = q_ref[rows, :]
            valid = _swa_valid(i, start)[:WINDOW]
            res = []
            for hh in (0, 1):
                hm = _head_mask(hh)
                sink = jnp.max(jnp.where(hm, skv, NEG), axis=1, keepdims=True)
                s = jnp.where(valid, _dot(jnp.where(hm, q2, jnp.zeros_like(q2)), kk, NT), NEG)
                m = jnp.maximum(jnp.max(s, axis=1, keepdims=True), sink)
                p = jnp.exp(s - m)
                l = jnp.sum(p, axis=1, keepdims=True) + jnp.exp(sink - m)
                res.append((_dot(p.astype(BF16), vv) / l, m + jnp.log(l)))
            o = jnp.where(first, res[0][0], res[1][0])
            o_ref[rows, :] = o
            lse_ref[rows, :] = jnp.where(first, res[0][1], res[1][1])
            g = gate_ref[rows, :]
            y_ref[rows, :] = (o * (g * _sigmoid(g))).astype(BF16)

    blk = pl.BlockSpec((SWQ * WINDOW, LANES), lambda p, i: (i, p))
    gate = pl.BlockSpec((SWQ * WINDOW, LANES), lambda p, i: (i, gate_col + p))
    full = pl.BlockSpec((S, LANES), lambda p, i: (0, p // 2))
    return pl.pallas_call(
        body, name="swa_fwd", grid=(NH // 2, S // (SWQ * WINDOW)),
        in_specs=[blk, full, full, pl.BlockSpec((1, LANES), lambda p, i: (0, p)), gate],
        out_specs=[blk, blk, blk],
        out_shape=[jax.ShapeDtypeStruct((S, D), F32)] * 2 + [jax.ShapeDtypeStruct((S, D), BF16)],
        compiler_params=_params(2))(q, kdup, vdup, sinks_t, proj)


def _swa_bwd(q, kdup, vdup, sinks_t, o, lse, dy, proj, gate_col):
    def body(q_ref, k_ref, v_ref, sk_ref, o_ref, lse_ref, dy_ref, gate_ref, dq_ref, dk_ref, dv_ref, dsk_ref,
             dgate_ref):
        @pl.when(pl.program_id(1) == 0)
        def _():
            dk_ref[...] = jnp.zeros_like(dk_ref)
            dv_ref[...] = jnp.zeros_like(dv_ref)
            dsk_ref[...] = jnp.zeros_like(dsk_ref)

        skv = sk_ref[...]
        first = _head_mask(0)
        sink = _per_head(*[jnp.max(jnp.where(_head_mask(hh), skv, NEG), axis=1, keepdims=True) for hh in (0, 1)])
        for sb in range(SWQ):
            i = pl.program_id(1) * SWQ + sb
            rows = slice(sb * WINDOW, (sb + 1) * WINDOW)
            start = pl.multiple_of(jnp.maximum(i - 1, 0) * WINDOW, WINDOW)
            kk, vv = k_ref[pl.ds(start, 2 * WINDOW), :], v_ref[pl.ds(start, 2 * WINDOW), :]
            do2, dgate = _gate_grads(dy_ref[rows, :], o_ref[rows, :], gate_ref[rows, :])
            dgate_ref[rows, :] = dgate.astype(BF16)
            do2b = do2.astype(BF16)
            prod, lse2 = do2b.astype(F32) * o_ref[rows, :], lse_ref[rows, :]
            qs, dos = _both_heads(q_ref[rows, :]), _both_heads(do2b)
            delta = jnp.concatenate([jnp.sum(jnp.where(_head_mask(hh), prod, 0.0), axis=1, keepdims=True)
                                     for hh in (0, 1)], axis=0)
            lse_h = jnp.concatenate([jnp.max(jnp.where(_head_mask(hh), lse2, NEG), axis=1, keepdims=True)
                                     for hh in (0, 1)], axis=0)
            p = jnp.where(_swa_valid(i, start), jnp.exp(_dot(qs, kk, NT) - lse_h), 0.0)
            dsb = (p * (_dot(dos, vv, NT) - delta)).astype(BF16)
            dk_ref[pl.ds(start, 2 * WINDOW), :] += _dot(dsb, qs, TN)
            dv_ref[pl.ds(start, 2 * WINDOW), :] += _dot(p.astype(BF16), dos, TN)
            dq_ref[rows, :] = _unstack(_dot(dsb, kk)) * SCALE
            t = jnp.exp(sink - lse_h) * delta
            dsk_ref[...] += -jnp.where(first, jnp.sum(t[:WINDOW], axis=0, keepdims=True),
                                       jnp.sum(t[WINDOW:], axis=0, keepdims=True))

    blk = pl.BlockSpec((SWQ * WINDOW, LANES), lambda p, i: (i, p))
    full = pl.BlockSpec((S, LANES), lambda p, i: (0, p // 2))
    acc = pl.BlockSpec((S, LANES), lambda p, i: (0, p))
    sk = pl.BlockSpec((1, LANES), lambda p, i: (0, p))
    gate = pl.BlockSpec((SWQ * WINDOW, LANES), lambda p, i: (i, gate_col + p))
    return pl.pallas_call(
        body, name="swa_bwd", grid=(NH // 2, S // (SWQ * WINDOW)),
        in_specs=[blk, full, full, sk, blk, blk, blk, gate],
        out_specs=[blk, acc, acc, sk, blk],
        out_shape=[jax.ShapeDtypeStruct((S, D), F32)] * 3 + [jax.ShapeDtypeStruct((1, D), F32),
                                                              jax.ShapeDtypeStruct((S, D), BF16)],
        compiler_params=_params(2))(q, kdup, vdup, sinks_t, o, lse, dy, proj)


def _adamw_math(w, g, m, v):
    m = ADAM_B1 * m + (1.0 - ADAM_B1) * g
    v = ADAM_B2 * v + (1.0 - ADAM_B2) * jnp.square(g)
    m_hat = m / (1.0 - ADAM_B1 ** ADAM_STEP)
    v_hat = v / (1.0 - ADAM_B2 ** ADAM_STEP)
    delta = -ADAM_LR * (m_hat / (jnp.sqrt(v_hat) + ADAM_EPS) + ADAM_WD * w)
    return delta, m, v


def _adamw(w, g, m, v, name):
    r, c = w.shape
    tr = min(r, 128)

    def body(w_ref, g_ref, m_ref, v_ref, d_ref, mo_ref, vo_ref):
        d_ref[...], mo_ref[...], vo_ref[...] = _adamw_math(w_ref[...], g_ref[...], m_ref[...], v_ref[...])

    spec = pl.BlockSpec((tr, c), lambda i: (i, 0))
    return pl.pallas_call(
        body, name=name, grid=(r // tr,), in_specs=[spec] * 4, out_specs=[spec] * 3,
        out_shape=[jax.ShapeDtypeStruct((r, c), F32)] * 3, compiler_params=_params(1))(w, g, m, v)


SUM_TILE = 128


FLAT_BLOCK = 257 * 1024


def _tiles(shape, axis, lead=0):
    if len(shape) == 1:
        count = shape[0] // FLAT_BLOCK
        return (FLAT_BLOCK,), count, lambda pos, *lead_idx: (sum(k * count for k in lead_idx) + pos,)
    r, c = shape
    blk = (SUM_TILE, c) if axis == 0 else (r, SUM_TILE)
    count = shape[axis] // SUM_TILE

    def index(pos, *lead_idx):
        return tuple(lead_idx) + ((pos, 0) if axis == 0 else (0, pos))

    return (None,) * lead + blk, count, index


def _adamw_halves(w, g_mine, g_theirs, m, v, axis, name):
    blk, count, index = _tiles(w.shape, axis)
    per_half = count // 2

    def body(w_ref, a_ref, b_ref, m_ref, v_ref, g_ref, d_ref, mo_ref, vo_ref):
        is_mine = pl.program_id(0) // per_half == lax.axis_index("c")
        g = jnp.where(is_mine, a_ref[...], b_ref[...])
        g_ref[...] = g
        d_ref[...], mo_ref[...], vo_ref[...] = _adamw_math(w_ref[...], g, m_ref[...], v_ref[...])

    spec = pl.BlockSpec(blk, lambda i: index(i))
    half = pl.BlockSpec(blk, lambda i: index(i % per_half))
    return pl.pallas_call(
        body, name=name, grid=(count,), in_specs=[spec, half, half, spec, spec], out_specs=[spec] * 4,
        out_shape=[jax.ShapeDtypeStruct(w.shape, F32)] * 4, compiler_params=_params(1))(w, g_mine, g_theirs, m, v)


def _chip_sum(blocks, from_sibling, axis, name):
    flat = blocks.ndim == 1
    blk, count, index = _tiles((from_sibling.shape[0] // NCHIP,) if flat else from_sibling.shape[1:], axis, lead=1)

    def body(lo_ref, hi_ref, p_ref, o32, o16):
        mine = jnp.where(lax.axis_index("c") == 0, lo_ref[...], hi_ref[...])
        acc = mine + p_ref[...]
        o32[...] = acc
        o16[...] = acc.astype(BF16)

    half = pl.BlockSpec(blk, lambda k, i: index(i, k))
    if flat:
        lo = pl.BlockSpec(blk, lambda k, i: (2 * count * k + i,))
        hi = pl.BlockSpec(blk, lambda k, i: (2 * count * k + count + i,))
    else:
        lo, hi = half, pl.BlockSpec(blk, lambda k, i: index(i + count, k))
    return pl.pallas_call(
        body, name=name, grid=(NCHIP, count), in_specs=[lo, hi, half], out_specs=[half, half],
        out_shape=[jax.ShapeDtypeStruct(from_sibling.shape, F32), jax.ShapeDtypeStruct(from_sibling.shape, BF16)],
        compiler_params=_params(2))(blocks, blocks, from_sibling)


def _mesh_sum(own, parts, axis, name):
    blk, count, index = _tiles(own.shape, axis)
    n = NCHIP - 1

    def body(a_ref, *refs):
        acc = a_ref[...]
        for k in range(n):
            acc = acc + refs[k][...].astype(F32)
        refs[n][...] = acc

    spec = pl.BlockSpec(blk, lambda i: index(i))
    if own.ndim == 1:
        part = [pl.BlockSpec(blk, lambda i, k=k: (k * count + i,)) for k in range(n)]
    else:
        part = [pl.BlockSpec((None,) + blk, lambda i, k=k: (k,) + index(i)) for k in range(n)]
    return pl.pallas_call(
        body, name=name, grid=(count,), in_specs=[spec] + part,
        out_specs=spec, out_shape=jax.ShapeDtypeStruct(own.shape, F32),
        compiler_params=_params(1))(own, *([parts] * n))


def _sum_stack(parts, name):
    n = parts.shape[0]

    def body(p_ref, o_ref):
        acc = p_ref[0]
        for k in range(1, n):
            acc = acc + p_ref[k]
        o_ref[...] = acc

    return pl.pallas_call(body, name=name, out_shape=jax.ShapeDtypeStruct(parts.shape[1:], F32))(parts)


def _coords():
    return lax.axis_index("x"), lax.axis_index("y"), lax.axis_index("c")


def _chip(who):
    return 2 * who[0] + who[1]


def _flip(who, mask):
    return tuple((1 - v) if b else v for v, b in zip(who, mask))


def _transfer(transfers, t, I, O, ssem, rsem, receiving):
    tr, me = transfers[t], _coords()
    peer = _flip(me, tr["mask"])
    return pltpu.make_async_remote_copy(
        src_ref=tr["src"](I, O, me), dst_ref=tr["dst"](I, O, peer if receiving else me),
        send_sem=ssem.at[t], recv_sem=rsem.at[t], device_id=peer, device_id_type=MESH)


def _start_transfers(transfers, I, O, ssem, rsem, onward):
    arrived = set()
    for t, tr in enumerate(transfers):
        after = tr.get("after")
        if (after is not None) != onward:
            continue
        if after is not None and after not in arrived:
            _transfer(transfers, after, I, O, ssem, rsem, True).wait_recv()
            arrived.add(after)
        _transfer(transfers, t, I, O, ssem, rsem, False).start()


def _finish_transfers(transfers, I, O, ssem, rsem):
    passed_on = {tr["after"] for tr in transfers if tr.get("after") is not None}
    for t in range(len(transfers)):
        if t not in passed_on:
            _transfer(transfers, t, I, O, ssem, rsem, True).wait_recv()
    for t in range(len(transfers)):
        _transfer(transfers, t, I, O, ssem, rsem, False).wait_send()


def _own_copies(own, I, O, stage, lsem, leg):
    for n, (src, dst) in enumerate(own):
        me = _coords()
        bring =pltpu.make_async_copy(src(I, O, me), stage[n], lsem.at[2 * n])
        put = pltpu.make_async_copy(stage[n], dst(I, O, me), lsem.at[2 * n + 1])
        if leg == 0:
            bring.start()
        elif leg == 1:
            bring.wait()
            put.start()
        else:
            put.wait()


def _own_scratch(own, ins):
    return [pltpu.VMEM(ins[n].shape, ins[n].dtype) for n in range(len(own))], pltpu.SemaphoreType.DMA((max(2 * len(own), 1),))


def _exchange(name, ins, outs, transfers, own=()):
    ni, no = len(ins), len(outs)
    nt = len(transfers)
    stages, stage_sems = _own_scratch(own, ins)

    def body(*refs):
        I, O = refs[:ni], refs[ni:ni + no]
        ssem, rsem, lsem = refs[ni + no:ni + no + 3]
        stage = refs[ni + no + 3:]
        _own_copies(own, I, O, stage, lsem, 0)
        _start_transfers(transfers, I, O, ssem, rsem, False)
        _own_copies(own, I, O, stage, lsem, 1)
        _start_transfers(transfers, I, O, ssem, rsem, True)
        _finish_transfers(transfers, I, O, ssem, rsem)
        _own_copies(own, I, O, stage, lsem, 2)

    hbm = pl.BlockSpec(memory_space=pltpu.HBM)
    return pl.pallas_call(
        body, name=name, in_specs=[hbm] * ni, out_specs=[hbm] * no,
        out_shape=[jax.ShapeDtypeStruct(s, d) for s, d in outs],
        scratch_shapes=[pltpu.SemaphoreType.DMA((nt,)), pltpu.SemaphoreType.DMA((nt,)), stage_sems] + stages,
        compiler_params=pltpu.CompilerParams(has_side_effects=True, vmem_limit_bytes=VMEM_LIMIT))(*ins)


CHIP_MASKS = [(0, 1, 0), (1, 0, 0), (1, 1, 0)]
SIBLING = (0, 0, 1)


def _half(shape2d, axis, which):
    n = shape2d[axis] // 2
    cut = pl.ds(pl.multiple_of(which * n, n), n)
    return (cut, slice(None)) if axis == 0 else (slice(None), cut)


class _Riding:
    def __init__(self, transfers, ins, outs, own=()):
        self.transfers, self.ins, self.outs, self.own = transfers, list(ins), list(outs), list(own)
        hbm = pl.BlockSpec(memory_space=pltpu.HBM)
        self.in_specs, self.out_specs = [hbm] * len(self.ins), [hbm] * len(self.outs)
        self.out_shape = [jax.ShapeDtypeStruct(s, d) for s, d in self.outs]
        stages, stage_sems = _own_scratch(self.own, self.ins)
        self.scratch = [pltpu.SemaphoreType.DMA((max(len(transfers), 1),))] * 2 + [stage_sems] + stages

    def hooks(self, I, O, ssem, rsem, lsem, *stage, first, middle, last):
        tr, own = self.transfers, self.own

        @pl.when(first)
        def _():
            _own_copies(own, I, O, stage, lsem, 0)
            _start_transfers(tr, I, O, ssem, rsem, False)

        if own or any(t.get("after") is not None for t in tr):
            @pl.when(middle)
            def _():
                _own_copies(own, I, O, stage, lsem, 1)
                _start_transfers(tr, I, O, ssem, rsem, True)

        def at_end():
            @pl.when(last)
            def _():
                _finish_transfers(tr, I, O, ssem, rsem)
                _own_copies(own, I, O, stage, lsem, 2)

        return at_end


def _stretch(n, pos):
    return (pl.ds(pos * n if isinstance(pos, int) else pl.multiple_of(pos * n, n), n),)


def _gather_plan(shards, axes):
    def half(a, who):
        if shards[a].ndim == 1:
            return _stretch(shards[a].shape[0] // 2, who[2])
        return _half(shards[a].shape, axes[a], who[2])

    def landed(a, chip, who):
        if shards[a].ndim == 1:
            return _stretch(shards[a].shape[0] // 2, 2 * chip + who[2])
        return (chip,) + half(a, who)

    over_ici, onward = [], []
    for a in range(len(shards)):
        for mask in CHIP_MASKS:
            over_ici.append(dict(
                mask=mask,
                src=lambda I, O, me, a=a: I[a].at[half(a, me)],
                dst=lambda I, O, who, a=a: O[a].at[landed(a, _chip(who), who)]))
            onward.append(dict(
                mask=SIBLING, after=len(over_ici) - 1,
                src=lambda I, O, me, a=a, mask=mask: O[a].at[landed(a, _chip(_flip(me, mask)), me)],
                dst=lambda I, O, who, a=a, mask=mask: O[a].at[landed(a, _chip(_flip(who, mask)), who)]))
    outs = [((NCHIP * s.shape[0],) if s.ndim == 1 else (NCHIP,) + s.shape, s.dtype) for s in shards]

    def whole(a, chip):
        return _stretch(shards[a].shape[0], chip) if shards[a].ndim == 1 else (chip,)

    own = [(lambda I, O, me, a=a: I[a], lambda I, O, me, a=a: O[a].at[whole(a, _chip(me))])
           for a in range(len(shards))]
    return over_ici + onward, outs, own


def _gather_shards(shards, axes):
    transfers, outs, own = _gather_plan(shards, axes)
    return _exchange("gather_weights", shards, outs, transfers, own)


def _to_sibling(arrs, name):
    transfers = [dict(mask=SIBLING, src=lambda I, O, me, a=a: I[a], dst=lambda I, O, who, a=a: O[a])
                 for a in range(len(arrs))]
    return _exchange(name, arrs, [(t.shape, t.dtype) for t in arrs], transfers)


def _halves_to_sibling(blocks, axes, name):
    def cut(a, which):
        return (slice(None),) + _half(blocks[a].shape[1:], axes[a], which)

    transfers, outs = [], []
    for a, (b, ax) in enumerate(zip(blocks, axes)):
        if b.ndim == 1:
            h = b.shape[0] // NCHIP // 2
            for k in range(NCHIP):
                transfers.append(dict(mask=SIBLING,
                                      src=lambda I, O, me, a=a, k=k, h=h: I[a].at[_stretch(h, 2 * k + 1 - me[2])],
                                      dst=lambda I, O, who, a=a, k=k, h=h: O[a].at[_stretch(h, k)]))
            outs.append(((NCHIP * h,), b.dtype))
        else:
            transfers.append(dict(mask=SIBLING, src=lambda I, O, me, a=a: I[a].at[cut(a, 1 - me[2])],
                                  dst=lambda I, O, who, a=a: O[a]))
            shape = list(b.shape)
            shape[ax + 1] //= 2
            outs.append((tuple(shape), b.dtype))
    return _exchange(name, blocks, outs, transfers)


def _scatter_plan(tb):
    def slot(a, k):
        return (k,) if tb[a].ndim == 3 else _stretch(tb[a].shape[0] // NCHIP, k)

    transfers = []
    for a in range(len(tb)):
        for n, mask in enumerate(CHIP_MASKS):
            transfers.append(dict(
                mask=mask,
                src=lambda I, O, me, a=a, mask=mask: I[a].at[slot(a, _chip(_flip(me, mask)))],
                dst=lambda I, O, who, a=a, n=n: O[a].at[slot(a, n)]))
    outs = [((3,) + t.shape[1:] if t.ndim == 3 else (3 * (t.shape[0] // NCHIP),), t.dtype) for t in tb]
    return transfers, outs


def _scatter_chip_sums(tb):
    transfers, outs = _scatter_plan(tb)
    return _exchange("scatter_grads", tb, outs, transfers)


def _gather_small(vec):
    def slot(who):
        return 4 * who[0] + 2 * who[1] + who[2]

    masks = [(m >> 2 & 1, m >> 1 & 1, m & 1) for m in range(1, 8)]
    transfers = [dict(mask=mask, src=lambda I, O, me: I[0], dst=lambda I, O, who: O[0].at[slot(who)])
                 for mask in masks]
    own = [(lambda I, O, me: I[0], lambda I, O, me: O[0].at[slot(me)])]
    return _exchange("gather_small", [vec], [((8,) + vec.shape, vec.dtype)], transfers, own)[0]


def _rope_tables(positions):
    half = ROT // 2
    inv_freq = jnp.power(jnp.float32(THETA), -jnp.arange(0, ROT, 2, dtype=F32) / ROT)
    ang = positions.astype(F32)[:, None] * inv_freq[None, :]
    cos, sin = jnp.cos(ang), jnp.sin(ang)
    one, zero, z8 = jnp.ones((S, HD - ROT), F32), jnp.zeros((S, HD - ROT), F32), jnp.zeros((S, half), F32)
    c = jnp.concatenate([cos, cos, one], axis=1)
    a = jnp.concatenate([-sin, z8, zero], axis=1)
    b = jnp.concatenate([z8, sin, zero], axis=1)
    return tuple(jnp.tile(t, (1, 2)) for t in (c, a, b))


def _tile_heads(g, w):
    return jnp.tile(g.reshape(1, HD), (1, w // HD))


def _fold_heads(dg):
    return dg.reshape(-1, HD).sum(axis=0)


def _pad_lanes(a):
    return jnp.pad(a, ((0, 0), (0, LANES - a.shape[1])))


def _local_step(x, target, positions, wt, fetch, late_weights, begin_reduce):
    rope = _rope_tables(positions)
    w1t = wt["w_in_a_t"]
    f_row = 3 * D // LANES
    wg_t = w1t[3 * D + NH:]
    in_b_block = lambda c: pl.BlockSpec((None, TN_, TN_), lambda j, i: (c, j, 0))
    b_pad = _pad_lanes(wt["b_forget"].reshape(1, NH))
    qg_a, kg_a = _tile_heads(wt["qnorm_a_g"], D), _tile_heads(wt["knorm_a_g"], D)
    qg_b, kg_b = _tile_heads(wt["qnorm_b_g"], D), _tile_heads(wt["knorm_b_g"], KVW)
    norm_a, kv_g, norm_b = wt["norm_a_g"].reshape(1, D), wt["kv_norm_g"].reshape(1, D), wt["norm_b_g"].reshape(1, D)
    sinks_t = jnp.repeat(wt["sinks"].reshape(1, NH), HD, axis=1)

    (u_a,) = _rmsnorm_fwd(x, [norm_a], "norm_a")
    qkv = _mm("proj_a", S, 3 * D, [(u_a, _a_rows(D), w1t, _b_rows(D), NT)])
    fpad = _mm("proj_f", S, LANES, [(u_a, _a_rows(D), w1t, _b_rows(D, row0=f_row, tn=LANES), NT)], tn=LANES)
    gate_a = _mm("proj_gate_a", S, D, [(u_a, _a_rows(D), wg_t, _b_rows(D), NT)])
    q_a, k_a, v_a = _a_post(qkv, qg_a, kg_a)
    ct = _forget_cumsum(fpad, b_pad)
    ct2 = ct[:NH].reshape(NH // 2, 2, S)
    o_a, lse_a, y_a, fetched = _fox_fwd(q_a, k_a, v_a, ct2, gate_a, fetch)
    wt = {**wt, **late_weights(fetched)}
    w_in_b = wt["w_in_b"]
    h1 = _mm("out_a", S, D, [(y_a, _a_rows(D), wt["w_out_a"], _b_cols(D), None)], add=x)
    u_kv, u_b = _rmsnorm_fwd(h1, [kv_g, norm_b], "norm_b")
    kv = _mm("proj_kv", S, 2 * KVW, [(u_kv, _a_rows(D), wt["w_kv"], _b_cols(D), None)])
    pb = _mm("proj_b", S, 2 * D,
             [(u_b, _a_rows(D), w_in_b, pl.BlockSpec((None, D, TN_), lambda j, i: (j, 0, 0)), None)])
    q_b, kdup, vdup = _b_post(pb, kv, qg_b, kg_b, rope)
    gate_b_col = D // LANES
    o_b, lse_b, y_b = _swa_fwd(q_b, kdup, vdup, sinks_t, pb, gate_b_col)
    out = _mm("out_b", S, D, [(y_b, _a_rows(D), wt["w_out_b"], _b_cols(D), None)], add=h1)
    d_out, d_out_b, sq = _loss_head(out, target)

    g = {}
    g["w_out_b"] = _mm("dw_out_b", D, D, [(y_b, _a_cols(S), d_out_b, _b_cols(S), TN)])
    d_y_b = _mm("dy_b", S, D, [(d_out_b, _a_rows(D), wt["w_out_b"], _b_rows(D), NT)])
    dq_b, dkdup, dvdup, dsk, d_gate_b = _swa_bwd(q_b, kdup, vdup, sinks_t, o_b, lse_b, d_y_b, pb, gate_b_col)
    g["sinks"] = dsk[0, ::HD]
    d_qb_raw, dg = _headnorm_bwd(pb, 0, qg_b, dq_b, rope, "qnorm_b_bwd")
    g["qnorm_b_g"] = _fold_heads(dg)
    d_pb = [d_qb_raw, d_qb_raw, d_gate_b, d_gate_b]
    g["w_in_b"] = jnp.concatenate([
        _mm("dw_in_b_q", D, D, [(u_b, _a_cols(S), d_qb_raw, _b_cols(S), TN)], stacked=True),
        _mm("dw_in_b_gate", D, D, [(u_b, _a_cols(S), d_gate_b, _b_cols(S), TN)], stacked=True)], axis=0)
    d_u_b = _mm("du_b", S, D, [(d_pb[c], _a_rows(TN_, col=c % 2), w_in_b, in_b_block(c), NT) for c in range(NCHIP)])
    d_kv, dg = _kv_bwd(dkdup, dvdup, kv, kg_b, rope)
    g["knorm_b_g"] = _fold_heads(dg)
    g["w_kv"] = _mm("dw_kv", D, 2 * KVW, [(u_kv, _a_cols(S), d_kv, _b_cols(S), TN)])
    d_u_kv = _mm("du_kv", S, D, [(d_kv, _a_rows(2 * KVW), wt["w_kv"], _b_rows(2 * KVW), NT)])
    d_h1, d_h1_b, g["kv_norm_g"], g["norm_b_g"] = _rmsnorm_bwd(h1, [kv_g, norm_b], [d_u_kv, d_u_b], d_out, "norm_b_bwd")
    g["w_out_a"] = _mm("dw_out_a", D, D, [(y_a, _a_cols(S), d_h1_b, _b_cols(S), TN)])
    d_y_a = _mm("dy_a", S, D, [(d_h1_b, _a_rows(D), wt["w_out_a"], _b_rows(D), NT)])
    riding, so_far = begin_reduce({n: g[n] for n in LATE})
    dq_a, dk_a, dv_a, dct, d_gate_a, arrived = _fox_bwd(q_a, k_a, v_a, ct2, o_a, lse_a, d_y_a, gate_a, riding)
    dct_pad = jnp.pad(dct.reshape(NH, S), ((0, LANES - NH), (0, 0)))
    d_f, db = _forget_bwd(dct_pad, fpad, b_pad)
    g["b_forget"] = db[0, :NH]
    d_q_raw, dg = _headnorm_bwd(qkv, 0, qg_a, dq_a, None, "qnorm_a_bwd")
    g["qnorm_a_g"] = _fold_heads(dg)
    d_k_raw, dg = _headnorm_bwd(qkv, 1, kg_a, dk_a, None, "knorm_a_bwd")
    g["knorm_a_g"] = _fold_heads(dg)
    rows, gw = 4 * D + NH, None
    for n, t, row0 in (("q", d_q_raw, 0), ("k", d_k_raw, D), ("v", dv_a, 2 * D)):
        gw = _mm("dw_in_a_" + n, D, D, [(t, _a_cols(S), u_a, _b_cols(S), TN)], rows_of=(gw, rows, row0))
    gw = _mm("dw_in_a_f", LANES, D, [(d_f, _a_cols(S, tm=LANES), u_a, _b_cols(S), TN)], tm=LANES,
             rows_of=(gw, rows, 3 * D))
    g["w_in_a"] = _mm("dw_in_a_gate", D, D, [(d_gate_a, _a_cols(S), u_a, _b_cols(S), TN)],
                      rows_of=(gw, rows, 3 * D + NH))
    riding, so_far_first = begin_reduce({"w_in_a": g["w_in_a"]})
    d_u_a, arrived_first = _mm("du_a", S, D, [
        (d_q_raw, _a_rows(D), w1t, _b_cols(D, row=0), None), (d_k_raw, _a_rows(D), w1t, _b_cols(D, row=1), None),
        (dv_a, _a_rows(D), w1t, _b_cols(D, row=2), None), (d_gate_a, _a_rows(D), wg_t, _b_cols(D), None),
        (d_f, _a_rows(LANES), w1t, _b_cols(LANES, row=f_row), None)], riding=riding)
    d_x, _, g["norm_a_g"] = _rmsnorm_bwd(x, [norm_a], [d_u_a], d_h1, "norm_a_bwd")
    return sq, d_x, g, (list(so_far_first) + list(so_far), list(arrived_first) + list(arrived))


BIG = ["w_in_a", "w_out_a", "w_kv", "w_in_b", "w_out_b"]
LATE = BIG[1:]
SPLIT = {"w_in_a": None, "w_out_a": 0, "w_kv": 0, "w_in_b": 0, "w_out_b": 0}
SMALL = ["norm_a_g", "b_forget", "qnorm_a_g", "knorm_a_g", "kv_norm_g", "knorm_b_g", "norm_b_g", "qnorm_b_g", "sinks"]
NAMES = ["norm_a_g", "w_in_a", "b_forget", "qnorm_a_g", "knorm_a_g", "w_out_a", "kv_norm_g", "w_kv", "knorm_b_g",
         "norm_b_g", "w_in_b", "qnorm_b_g", "sinks", "w_out_b"]


def _pack(vals):
    flat = []
    for v in vals:
        v = v.reshape(-1)
        flat.append(jnp.pad(v, (0, -v.shape[0] % LANES)))
    flat = jnp.concatenate(flat)
    flat = jnp.pad(flat, (0, -flat.shape[0] % (8 * LANES)))
    return flat.reshape(-1, LANES)


def _unpack(packed, shapes):
    flat, out, off = packed.reshape(-1), [], 0
    for s in shapes:
        n = int(np.prod(s))
        out.append(flat[off:off + n].reshape(s))
        off += n + (-n % LANES)
    return out


def kernel(x, positions, norm_a_g, w_in_a, b_forget, qnorm_a_g, knorm_a_g, w_out_a, kv_norm_g, w_kv, knorm_b_g, norm_b_g, w_in_b, qnorm_b_g, sinks, w_out_b, loss_target, m_norm_a_g, m_w_in_a, m_b_forget, m_qnorm_a_g, m_knorm_a_g, m_w_out_a, m_kv_norm_g, m_w_kv, m_knorm_b_g, m_norm_b_g, m_w_in_b, m_qnorm_b_g, m_sinks, m_w_out_b, v_norm_a_g, v_w_in_a, v_b_forget, v_qnorm_a_g, v_knorm_a_g, v_w_out_a, v_kv_norm_g, v_w_kv, v_knorm_b_g, v_norm_b_g, v_w_in_b, v_qnorm_b_g, v_sinks, v_w_out_b):
    w = dict(norm_a_g=norm_a_g, w_in_a=w_in_a, b_forget=b_forget, qnorm_a_g=qnorm_a_g, knorm_a_g=knorm_a_g,
             w_out_a=w_out_a, kv_norm_g=kv_norm_g, w_kv=w_kv, knorm_b_g=knorm_b_g, norm_b_g=norm_b_g,
             w_in_b=w_in_b, qnorm_b_g=qnorm_b_g, sinks=sinks, w_out_b=w_out_b)
    m = dict(norm_a_g=m_norm_a_g, w_in_a=m_w_in_a, b_forget=m_b_forget, qnorm_a_g=m_qnorm_a_g, knorm_a_g=m_knorm_a_g,
             w_out_a=m_w_out_a, kv_norm_g=m_kv_norm_g, w_kv=m_w_kv, knorm_b_g=m_knorm_b_g, norm_b_g=m_norm_b_g,
             w_in_b=m_w_in_b, qnorm_b_g=m_qnorm_b_g, sinks=m_sinks, w_out_b=m_w_out_b)
    v = dict(norm_a_g=v_norm_a_g, w_in_a=v_w_in_a, b_forget=v_b_forget, qnorm_a_g=v_qnorm_a_g, knorm_a_g=v_knorm_a_g,
             w_out_a=v_w_out_a, kv_norm_g=v_kv_norm_g, w_kv=v_w_kv, knorm_b_g=v_knorm_b_g, norm_b_g=v_norm_b_g,
             w_in_b=v_w_in_b, qnorm_b_g=v_qnorm_b_g, sinks=v_sinks, w_out_b=v_w_out_b)
    my_chip = 2 * lax.axis_index("x") + lax.axis_index("y")

    def shard2d(t, n):
        if n == "w_in_a":
            return jnp.transpose(t, (2, 0, 1)).reshape(-1)
        return t.reshape(t.shape[-2:])

    def unflat(t, n):
        return jnp.transpose(t.reshape(-1, 1, D), (1, 2, 0)) if n == "w_in_a" else t.reshape(w[n].shape)

    w2d = {n: shard2d(w[n], n) for n in BIG}

    norm_a_rows = jnp.broadcast_to(norm_a_g.reshape(1, D // NCHIP), (16, D // NCHIP))
    w1t, norm_rows = _gather_shards([w2d["w_in_a"].astype(BF16), norm_a_rows], [SPLIT["w_in_a"], 0])
    wt = {"w_in_a_t": w1t.reshape(-1, D), "norm_a_g": norm_rows[:, 0, :].reshape(1, D)}
    for n in SMALL[1:]:
        wt[n] = w[n]
    late_shards = [w2d[n].astype(BF16) for n in LATE]
    late_axes = [SPLIT[n] for n in LATE]
    transfers, outs, own = _gather_plan(late_shards, late_axes)
    fetch = _Riding(transfers, late_shards, outs, own)

    def late_weights(fetched):
        return {n: t if n == "w_in_b" else t.reshape(-1, t.shape[2]) for n, t in zip(LATE, fetched)}

    def as_blocks(t):
        if t.ndim == 3:
            return t
        return t.reshape(-1) if t.shape[0] % (8 * NCHIP) else t.reshape(NCHIP, -1, t.shape[1])

    def begin_reduce(grads):
        names = list(grads)
        axes = [SPLIT[n] for n in names]
        blocks = [as_blocks(grads[n]) for n in names]
        halves = _halves_to_sibling(blocks, axes, "sibling_halves_" + names[0])
        sums = [_chip_sum(blk, part, ax, "chip_sum_" + n) for n, ax, blk, part in zip(names, axes, blocks, halves)]
        bf16 = [s[1] for s in sums]
        transfers, outs = _scatter_plan(bf16)
        return _Riding(transfers, bf16, outs), [s[0] for s in sums]

    sq, d_x, g, (chip_f32, arrived) = _local_step(x[0], loss_target[0], positions, wt, fetch, late_weights,
                                                  begin_reduce)

    small_shapes = [(D,), (NH,), (HD,), (HD,), (D,), (HD,), (D,), (HD,), (NH,), (D,)]
    packed = _pack([g[n] for n in SMALL] + [sq])
    total = _sum_stack(_gather_small(packed), "sum_small")
    small_g = dict(zip(SMALL, _unpack(total, small_shapes)[:-1]))
    loss = 0.5 * jnp.sum(_unpack(total, small_shapes)[-1]) / D
    small_g["norm_a_g"] = lax.dynamic_slice(small_g["norm_a_g"], (my_chip * (D // NCHIP),), (D // NCHIP,))

    axes = [SPLIT[n] for n in BIG]
    halves = []
    for n, ax, t32, parts in zip(BIG, axes, chip_f32, arrived):
        if t32.ndim == 1:
            own = lax.dynamic_slice_in_dim(t32, my_chip * (t32.shape[0] // NCHIP), t32.shape[0] // NCHIP)
        else:
            own = lax.dynamic_index_in_dim(t32, my_chip, axis=0, keepdims=False)
        halves.append(_mesh_sum(own, parts, ax, "mesh_sum_" + n))
    sibling_done = _to_sibling(halves, "finished_halves")

    res = {}
    for n, ax, mine_half, their_half in zip(BIG, axes, halves, sibling_done):
        out4 = _adamw_halves(w2d[n], mine_half, their_half, shard2d(m[n], n), shard2d(v[n], n), ax, "adamw_" + n)
        res[n] = tuple(unflat(t, n) for t in out4)
    sm_g = _pack([small_g[n] for n in SMALL])
    sm = [_pack([d[n] for n in SMALL]) for d in (w, m, v)]
    sm_out = _adamw(sm[0], sm_g, sm[1], sm[2], "adamw_small")
    sm_shapes = [w[n].shape for n in SMALL]
    unpacked = [_unpack(t, sm_shapes) for t in (sm_g,) + tuple(sm_out)]
    for i, n in enumerate(SMALL):
        res[n] = tuple(u[i] for u in unpacked)

    outs = [loss, d_x[None]]
    for k in range(4):
        outs += [res[n][k] for n in NAMES]
    return tuple(outs)
```

```python
import numpy as np
import jax
import jax.numpy as jnp
from jax import lax
from jax.experimental import pallas as pl
from jax.experimental.pallas import tpu as pltpu

F32, BF16 = jnp.float32, jnp.bfloat16
S, D, HD, NH, NKV = 2048, 1024, 64, 16, 4
KVW = NKV * HD
WINDOW = 128
ROT = HD // 4
THETA = 500000.0
EPS = 1e-6
SCALE = HD ** -0.5
LANES = 128
NEG = -1e30
VMEM_LIMIT = 48 * 2 ** 20
ROWS = 256
ATT = 512
SWQ = 4
NCHIP = 4
ADAM_LR, ADAM_B1, ADAM_B2, ADAM_EPS, ADAM_WD, ADAM_STEP = 0.001, 0.9, 0.999, 1e-08, 0.01, 10
NT = (((1,), (1,)), ((), ()))
TN = (((0,), (0,)), ((), ()))
MESH = pl.DeviceIdType.MESH


def _params(n):
    return pltpu.CompilerParams(dimension_semantics=("arbitrary",) * n, vmem_limit_bytes=VMEM_LIMIT)


def _dot(a, b, dims=None):
    if dims is None:
        return jnp.dot(a, b, preferred_element_type=F32)
    return lax.dot_general(a, b, dims, preferred_element_type=F32)


def _dot_split(a, b, n):
    out, rest = None, a
    for _ in range(n):
        hi = rest.astype(BF16)
        term = _dot(hi, b)
        out = term if out is None else out + term
        rest = rest - hi.astype(F32)
    return out


def _seg_mat(w):
    e = (np.arange(w)[:, None] // HD == np.arange(LANES)[None, :]).astype(np.float32)
    return jnp.asarray(e, BF16)


def _spread(r, w):
    head = lax.broadcasted_iota(jnp.int32, (2 * LANES, w), 1) >> 6
    row = lax.broadcasted_iota(jnp.int32, (2 * LANES, w), 0)
    et2 = jnp.where(head == (row & (LANES - 1)), 1.0, 0.0).astype(BF16)
    hi = r.astype(BF16)
    lo = (r - hi.astype(F32)).astype(BF16)
    return _dot(jnp.concatenate([hi, lo], axis=1), et2)


def _head_rstd(x, e):
    ss = _dot_split(x * x, e, 2)
    return _spread(lax.rsqrt(ss * (1.0 / HD) + EPS), x.shape[1])


def _rope(x, c, a, b):
    w = x.shape[1]
    return x * c + pltpu.roll(x, w - ROT // 2, 1) * a + pltpu.roll(x, ROT // 2, 1) * b


def _rope_t(dy, c, a, b):
    w = dy.shape[1]
    return dy * c + pltpu.roll(dy * b, w - ROT // 2, 1) + pltpu.roll(dy * a, ROT // 2, 1)


def _sigmoid(x):
    return 1.0 / (1.0 + jnp.exp(-x))


def _row_spec(shape, ts):
    nd = len(shape)
    if shape[0] == S:
        return pl.BlockSpec((ts,) + tuple(shape[1:]), lambda i: (i,) + (0,) * (nd - 1))
    return pl.BlockSpec(tuple(shape), lambda i: (0,) * nd)


def _in_hbm(arrays):
    return [pltpu.with_memory_space_constraint(a, pltpu.HBM) for a in arrays]


def _rows_call(body, name, ins, outs, ts=ROWS):
    return pl.pallas_call(
        body, name=name, grid=(S // ts,),
        in_specs=[_row_spec(a.shape, ts) for a in ins],
        out_specs=[_row_spec(s, ts) for s, _ in outs],
        out_shape=[jax.ShapeDtypeStruct(s, d) for s, d in outs],
        compiler_params=_params(1))(*_in_hbm(ins))


def _col_spec(ts, w, col):
    return pl.BlockSpec((ts, w), lambda i: (i, col))


TM = TN_ = 512
TM_TOKENS = 1024


def _mm(name, m, n, terms, out_dtype=F32, add=None, tm=None, tn=TN_, stacked=False, riding=None, rows_of=None):
    nterm = len(terms)
    if tm is None:
        tm = TM_TOKENS if m == S else TM
    nj, ni_ = n // tn, m // tm
    n_in = 2 * nterm + (add is not None) + (rows_of is not None and rows_of[0] is not None)
    r_in, r_out = (len(riding.ins), len(riding.outs)) if riding is not None else (0, 0)

    def body(*refs):
        if riding is not None:
            j, i = pl.program_id(0), pl.program_id(1)
            at_end = riding.hooks(refs[n_in:n_in + r_in], refs[n_in + r_in + 1:n_in + r_in + 1 + r_out],
                                  *refs[n_in + r_in + 1 + r_out:], first=(j == 0) & (i == 0),
                                  middle=(j == nj // 2) & (i == 0), last=(j == nj - 1) & (i == ni_ - 1))
        acc = None
        for t in range(nterm):
            part = _dot(refs[2 * t][...], refs[2 * t + 1][...], terms[t][4])
            acc = part if acc is None else acc + part
        if add is not None:
            acc = acc + refs[2 * nterm][...]
        refs[n_in + r_in][...] = acc.astype(out_dtype)
        if riding is not None:
            at_end()

    tile = pl.BlockSpec((tm, tn), lambda j, i: (i, j))
    ins, specs = [], []
    for a, a_spec, b, b_spec, _ in terms:
        ins += [a, b]
        specs += [a_spec, b_spec]
    if add is not None:
        ins.append(add)
        specs.append(tile)
    out_spec = pl.BlockSpec((None, tm, tn), lambda j, i: (j, i, 0)) if stacked else tile
    out_shape = jax.ShapeDtypeStruct((nj, m, tn) if stacked else (m, n), out_dtype)
    if rows_of is not None:
        taller, rows, row0 = rows_of
        out_spec = pl.BlockSpec((pl.Element(tm), pl.Element(tn)), lambda j, i: (
            pl.multiple_of(row0 + i * tm, 8), pl.multiple_of(j * tn, LANES)))
        out_shape = jax.ShapeDtypeStruct((rows, n), out_dtype)
        alias = {}
        if taller is not None:
            ins.append(taller)
            specs.append(pl.BlockSpec(memory_space=pltpu.HBM))
            alias = {len(ins) - 1: 0}
        return pl.pallas_call(body, name=name, grid=(nj, ni_), in_specs=specs, out_specs=out_spec,
                              out_shape=out_shape, input_output_aliases=alias,
                              compiler_params=_params(2))(*_in_hbm(ins))
    if riding is None:
        return pl.pallas_call(body, name=name, grid=(nj, ni_), in_specs=specs, out_specs=out_spec,
                              out_shape=out_shape, compiler_params=_params(2))(*_in_hbm(ins))
    res = pl.pallas_call(
        body, name=name, grid=(nj, ni_), in_specs=specs + riding.in_specs,
        out_specs=[out_spec] + riding.out_specs, out_shape=[out_shape] + riding.out_shape,
        scratch_shapes=riding.scratch, compiler_params=_params(2))(*_in_hbm(ins), *riding.ins)
    return res[0], res[1:]


def _a_rows(k, col=0, tm=TM_TOKENS):
    return pl.BlockSpec((tm, k), lambda j, i: (i, col))


def _a_cols(k, tm=TM):
    return pl.BlockSpec((k, tm), lambda j, i: (0, i))


def _b_cols(k, row=0, col0=0, tn=TN_):
    return pl.BlockSpec((k, tn), lambda j, i: (row, col0 + j))


def _b_rows(k, row0=0, tn=TN_):
    return pl.BlockSpec((tn, k), lambda j, i: (row0 + j, 0))


def _rmsnorm_fwd(x, gains, name):
    def body(*refs):
        xv = refs[0][...]
        r = lax.rsqrt(jnp.mean(xv * xv, axis=-1, keepdims=True) + EPS)
        xh = xv * r
        for n in range(len(gains)):
            refs[1 + len(gains) + n][...] = (xh * refs[1 + n][...]).astype(BF16)

    return _rows_call(body, name, [x] + list(gains), [((S, D), BF16)] * len(gains))


def _rmsnorm_bwd(x, gains, dus, dres, name):
    n = len(gains)

    def body(*refs):
        x_ref, g_refs, du_refs, dres_ref = refs[0], refs[1:1 + n], refs[1 + n:1 + 2 * n], refs[1 + 2 * n]
        dx_ref, dxb_ref, dg_refs = refs[2 + 2 * n], refs[3 + 2 * n], refs[4 + 2 * n:]
        xv = x_ref[...]
        r = lax.rsqrt(jnp.mean(xv * xv, axis=-1, keepdims=True) + EPS)
        xh = xv * r
        gy = None
        for m in range(n):
            du = du_refs[m][...]
            part = jnp.sum(du * xh, axis=0, keepdims=True)

            @pl.when(pl.program_id(0) == 0)
            def _(m=m, part=part):
                dg_refs[m][...] = part

            @pl.when(pl.program_id(0) != 0)
            def _(m=m, part=part):
                dg_refs[m][...] += part

            t = du * g_refs[m][...]
            gy = t if gy is None else gy + t
        dx = dres_ref[...] + r * (gy - xh * jnp.mean(gy * xh, axis=-1, keepdims=True))
        dx_ref[...] = dx
        dxb_ref[...] = dx.astype(BF16)

    outs = [((S, D), F32), ((S, D), BF16)] + [((1, D), F32)] * n
    return _rows_call(body, name, [x] + list(gains) + list(dus) + [dres], outs)


def _a_post(qkvg, qg, kg):
    e = _seg_mat(D)

    def body(q_ref, k_ref, v_ref, qg_ref, kg_ref, e_ref, qo, ko, vo):
        ev = e_ref[...]
        qv, kv = q_ref[...], k_ref[...]
        qo[...] = (qv * _head_rstd(qv, ev) * qg_ref[...] * SCALE).astype(BF16)
        ko[...] = (kv * _head_rstd(kv, ev) * kg_ref[...]).astype(BF16)
        vo[...] = v_ref[...].astype(BF16)

    whole = lambda a: pl.BlockSpec(a.shape, lambda i: (0, 0))
    return pl.pallas_call(
        body, name="a_post", grid=(S // ROWS,),
        in_specs=[_col_spec(ROWS, D, 0), _col_spec(ROWS, D, 1), _col_spec(ROWS, D, 2),
                  whole(qg), whole(kg), whole(e)],
        out_specs=[_col_spec(ROWS, D, 0)] * 3,
        out_shape=[jax.ShapeDtypeStruct((S, D), BF16)] * 3,
        compiler_params=_params(1))(qkvg, qkvg, qkvg, qg, kg, e)


def _tri(upper):
    r, c = np.arange(ROWS)[:, None], np.arange(ROWS)[None, :]
    return jnp.asarray((r <= c) if upper else (r >= c), BF16)


def _forget_cumsum(fpad, bpad):
    def body(f_ref, b_ref, u_ref, c_ref, carry):
        @pl.when(pl.program_id(0) == 0)
        def _():
            carry[...] = jnp.zeros_like(carry)

        lf = jax.nn.log_sigmoid(f_ref[...] + b_ref[...])
        blk = _dot_split(lf.T, u_ref[...], 3) + carry[:, 0:1]
        c_ref[...] = blk
        carry[...] = jnp.broadcast_to(blk[:, ROWS - 1:ROWS], carry.shape)

    return pl.pallas_call(
        body, name="forget_cumsum", grid=(S // ROWS,),
        in_specs=[pl.BlockSpec((ROWS, LANES), lambda i: (i, 0)), pl.BlockSpec((1, LANES), lambda i: (0, 0)),
                  pl.BlockSpec((ROWS, ROWS), lambda i: (0, 0))],
        out_specs=pl.BlockSpec((LANES, ROWS), lambda i: (0, i)),
        out_shape=jax.ShapeDtypeStruct((LANES, S), F32),
        scratch_shapes=[pltpu.VMEM((LANES, LANES), F32)],
        compiler_params=_params(1))(fpad, bpad, _tri(True))


def _forget_bwd(dct, fpad, bpad):
    nb = S // ROWS

    def body(dc_ref, f_ref, b_ref, l_ref, df_ref, db_ref, carry):
        @pl.when(pl.program_id(0) == 0)
        def _():
            carry[...] = jnp.zeros_like(carry)
            db_ref[...] = jnp.zeros_like(db_ref)

        blk = _dot_split(dc_ref[...], l_ref[...], 3) + carry[:, 0:1]
        carry[...] = jnp.broadcast_to(blk[:, 0:1], carry.shape)
        df = blk.T * _sigmoid(-(f_ref[...] + b_ref[...]))
        df_ref[...] = df.astype(BF16)
        db_ref[...] += jnp.sum(df, axis=0, keepdims=True)

    return pl.pallas_call(
        body, name="forget_bwd", grid=(nb,),
        in_specs=[pl.BlockSpec((LANES, ROWS), lambda i: (0, nb - 1 - i)),
                  pl.BlockSpec((ROWS, LANES), lambda i: (nb - 1 - i, 0)),
                  pl.BlockSpec((1, LANES), lambda i: (0, 0)), pl.BlockSpec((ROWS, ROWS), lambda i: (0, 0))],
        out_specs=[pl.BlockSpec((ROWS, LANES), lambda i: (nb - 1 - i, 0)), pl.BlockSpec((1, LANES), lambda i: (0, 0))],
        out_shape=[jax.ShapeDtypeStruct((S, LANES), BF16), jax.ShapeDtypeStruct((1, LANES), F32)],
        scratch_shapes=[pltpu.VMEM((LANES, LANES), F32)],
        compiler_params=_params(1))(dct, fpad, bpad, _tri(False))


def _headnorm_bwd(x, col, gain, dy, rope, name):
    e = _seg_mat(D)
    tabs = list(rope) if rope is not None else []

    def body(*refs):
        x_ref, g_ref, dy_ref, e_ref = refs[:4]
        dx_ref, dg_ref = refs[-2:]
        xv, dyv, ev = x_ref[...], dy_ref[...], e_ref[...]
        if rope is not None:
            c, a, b = (jnp.tile(t[...], (1, D // LANES)) for t in refs[4:7])
            dyv = _rope_t(dyv, c, a, b)
        r = _head_rstd(xv, ev)
        xh = xv * r
        part = jnp.sum(dyv * xh, axis=0, keepdims=True)

        @pl.when(pl.program_id(0) == 0)
        def _():
            dg_ref[...] = part

        @pl.when(pl.program_id(0) != 0)
        def _():
            dg_ref[...] += part

        gy = dyv * g_ref[...]
        seg = _spread(_dot_split(gy * xh, ev, 2) * (1.0 / HD), D)
        dx_ref[...] = (r * (gy - xh * seg)).astype(BF16)

    whole = lambda a: pl.BlockSpec(a.shape, lambda i: (0, 0))
    return pl.pallas_call(
        body, name=name, grid=(S // ROWS,),
        in_specs=[_col_spec(ROWS, D, col), whole(gain), _col_spec(ROWS, D, 0), whole(e)]
                 + [pl.BlockSpec((ROWS, LANES), lambda i: (i, 0))] * len(tabs),
        out_specs=[_col_spec(ROWS, D, 0), whole(gain)],
        out_shape=[jax.ShapeDtypeStruct((S, D), BF16), jax.ShapeDtypeStruct((1, D), F32)],
        compiler_params=_params(1))(x, gain, dy, e, *tabs)


def _dup_mat():
    r, c = np.arange(KVW)[:, None], np.arange(2 * KVW)[None, :]
    return (r // HD == c // LANES) & (r % HD == c % HD)


def _fold_mat():
    r, c = np.arange(D)[:, None], np.arange(KVW)[None, :]
    return (r // (2 * LANES) == c // HD) & (r % HD == c % HD)


def _b_post(pb, kv, qg, kg, rope):
    e, ek = _seg_mat(D), _seg_mat(KVW)
    dup = jnp.asarray(_dup_mat(), BF16)

    def body(q_ref, k_ref, v_ref, qg_ref, kg_ref, e_ref, ek_ref, dup_ref, c_ref, a_ref, b_ref, qo, ko, vo):
        c1, a1, b1 = c_ref[...], a_ref[...], b_ref[...]
        qv = q_ref[...]
        qn = qv * _head_rstd(qv, e_ref[...]) * qg_ref[...]
        t = lambda z, n: jnp.tile(z, (1, n))
        qo[...] = (_rope(qn, t(c1, D // LANES), t(a1, D // LANES), t(b1, D // LANES)) * SCALE).astype(BF16)
        kvv = k_ref[...]
        kn = kvv * _head_rstd(kvv, ek_ref[...]) * kg_ref[...]
        kr = _rope(kn, t(c1, KVW // LANES), t(a1, KVW // LANES), t(b1, KVW // LANES)).astype(BF16)
        ko[...] = _dot(kr, dup_ref[...]).astype(BF16)
        vo[...] = _dot(v_ref[...].astype(BF16), dup_ref[...]).astype(BF16)

    whole = lambda a: pl.BlockSpec(a.shape, lambda i: (0, 0))
    tab = pl.BlockSpec((ROWS, LANES), lambda i: (i, 0))
    return pl.pallas_call(
        body, name="b_post", grid=(S // ROWS,),
        in_specs=[_col_spec(ROWS, D, 0), _col_spec(ROWS, KVW, 0), _col_spec(ROWS, KVW, 1),
                  whole(qg), whole(kg), whole(e), whole(ek), whole(dup), tab, tab, tab],
        out_specs=[_col_spec(ROWS, D, 0), _col_spec(ROWS, 2 * KVW, 0), _col_spec(ROWS, 2 * KVW, 0)],
        out_shape=[jax.ShapeDtypeStruct((S, D), BF16), jax.ShapeDtypeStruct((S, 2 * KVW), BF16),
                   jax.ShapeDtypeStruct((S, 2 * KVW), BF16)],
        compiler_params=_params(1))(pb, kv, kv, qg, kg, e, ek, dup, *rope)


def _kv_bwd(dkdup, dvdup, kv, kg, rope):
    ek = _seg_mat(KVW)
    fold = jnp.asarray(_fold_mat(), BF16)

    def body(dk_ref, dv_ref, k_ref, kg_ref, ek_ref, fold_ref, c_ref, a_ref, b_ref, dkv_ref, dg_ref):
        ev, fv = ek_ref[...], fold_ref[...]
        t = lambda z: jnp.tile(z[...], (1, KVW // LANES))
        dk = _rope_t(_dot_split(dk_ref[...], fv, 2), t(c_ref), t(a_ref), t(b_ref))
        dv = _dot_split(dv_ref[...], fv, 2)
        xv = k_ref[...]
        r = _head_rstd(xv, ev)
        xh = xv * r
        part = jnp.sum(dk * xh, axis=0, keepdims=True)

        @pl.when(pl.program_id(0) == 0)
        def _():
            dg_ref[...] = part

        @pl.when(pl.program_id(0) != 0)
        def _():
            dg_ref[...] += part

        gy = dk * kg_ref[...]
        seg = _spread(_dot_split(gy * xh, ev, 2) * (1.0 / HD), KVW)
        dkv_ref[:, 0:KVW] = (r * (gy - xh * seg)).astype(BF16)
        dkv_ref[:, KVW:2 * KVW] = dv.astype(BF16)

    whole = lambda a: pl.BlockSpec(a.shape, lambda i: (0, 0))
    tab = pl.BlockSpec((ROWS, LANES), lambda i: (i, 0))
    return pl.pallas_call(
        body, name="kv_bwd", grid=(S // ROWS,),
        in_specs=[_col_spec(ROWS, D, 0), _col_spec(ROWS, D, 0), _col_spec(ROWS, KVW, 0),
                  whole(kg), whole(ek), whole(fold), tab, tab, tab],
        out_specs=[_col_spec(ROWS, 2 * KVW, 0), whole(kg)],
        out_shape=[jax.ShapeDtypeStruct((S, 2 * KVW), BF16), jax.ShapeDtypeStruct((1, KVW), F32)],
        compiler_params=_params(1))(dkdup, dvdup, kv, kg, ek, fold, *rope)


def _loss_head(out, target):
    def body(o_ref, t_ref, d_ref, db_ref, l_ref):
        diff = o_ref[...] - t_ref[...]
        d = diff * (1.0 / D)
        d_ref[...] = d
        db_ref[...] = d.astype(BF16)

        @pl.when(pl.program_id(0) == 0)
        def _():
            l_ref[...] = jnp.zeros_like(l_ref)

        l_ref[...] += jnp.sum(diff * diff, axis=0, keepdims=True)

    return _rows_call(body, "loss_head", [out, target], [((S, D), F32), ((S, D), BF16), ((1, D), F32)])


def _lane():
    return lax.broadcasted_iota(jnp.int32, (1, LANES), 1)


def _head_mask(hh):
    return (_lane() < HD) if hh == 0 else (_lane() >= HD)


def _fox_fwd(q, k, v, ct, gate, riding):
    nq, npair = S // ATT, NH // 2
    ni, no = len(riding.ins), len(riding.outs)

    def body(q_ref, k_ref, v_ref, c_ref, gate_ref, *rest):
        o_ref, lse_ref, y_ref = rest[ni:ni + 3]
        pair, i = pl.program_id(0), pl.program_id(1)
        at_end = riding.hooks(rest[:ni], rest[ni + 3:ni + 3 + no], *rest[ni + 3 + no:],
                              first=(pair == 0) & (i == 0), middle=(pair == npair // 2) & (i == 0),
                              last=(pair == npair - 1) & (i == nq - 1))
        q2 = q_ref[...]
        qms = [jnp.where(_head_mask(hh), q2, jnp.zeros_like(q2)) for hh in (0, 1)]

        def probs(off, width, m, hh, diag):
            s = _dot(qms[hh], k_ref[pl.ds(off, width), :], NT) - c_ref[hh:hh + 1, pl.ds(off, width)]
            if diag:
                row = i * ATT + lax.broadcasted_iota(jnp.int32, (ATT, width), 0)
                col = off + lax.broadcasted_iota(jnp.int32, (ATT, width), 1)
                s = jnp.where(col <= row, s, NEG)
            m_new = jnp.maximum(m, jnp.max(s, axis=1, keepdims=True))
            p = jnp.exp(s - m_new)
            p_hi = p.astype(BF16)
            return m_new, jnp.exp(m - m_new), p_hi, (p - p_hi.astype(F32)).astype(BF16)

        def weighted(off, width, p_hi, p_lo, hh):
            vj = v_ref[pl.ds(off, width), :]
            v1 = jnp.where(_head_mask(hh), vj, jnp.ones_like(vj))
            return _dot(p_hi, v1) + _dot(p_lo, v1)

        def step(off, width, carry, diag):
            off = pl.multiple_of(off, ATT)
            out = []
            for hh in (0, 1):
                m, acc = carry[hh]
                m, alpha, p_hi, p_lo = probs(off, width, m, hh, diag)
                out.append((m, alpha * acc + weighted(off, width, p_hi, p_lo, hh)))
            return tuple(out)

        one = (jnp.full((ATT, 1), NEG, F32), jnp.zeros((ATT, LANES), F32))
        carry = lax.fori_loop(0, i // 2, lambda j, cr: step(j * (2 * ATT), 2 * ATT, cr, False), (one, one))
        carry = lax.cond(i % 2 == 1, lambda cr: step((i - 1) * ATT, 2 * ATT, cr, True),
                         lambda cr: step(i * ATT, ATT, cr, True), carry)
        res = []
        for hh in (0, 1):
            m, acc = carry[hh]
            l = jnp.max(jnp.where(_head_mask(1 - hh), acc, 0.0), axis=1, keepdims=True)
            res.append((acc / l, m + jnp.log(l)))
        first = _head_mask(0)
        o = jnp.where(first, res[0][0], res[1][0])
        o_ref[...] = o
        lse_ref[...] = jnp.where(first, res[0][1], res[1][1])
        g = gate_ref[...]
        y_ref[...] = (o * (g * _sigmoid(g))).astype(BF16)
        at_end()

    blk = pl.BlockSpec((ATT, LANES), lambda p, i: (i, p))
    full = pl.BlockSpec((S, LANES), lambda p, i: (0, p))
    res = pl.pallas_call(
        body, name="fox_fwd", grid=(npair, nq),
        in_specs=[blk, full, full, pl.BlockSpec((None, 2, S), lambda p, i: (p, 0, 0)), blk] + riding.in_specs,
        out_specs=[blk, blk, blk] + riding.out_specs,
        out_shape=[jax.ShapeDtypeStruct((S, D), F32)] * 2 + [jax.ShapeDtypeStruct((S, D), BF16)] + riding.out_shape,
        scratch_shapes=riding.scratch,
        compiler_params=_params(2))(q, k, v, ct, gate, *riding.ins)
    return res[0], res[1], res[2], res[3:]


def _gate_grads(dy, o, g):
    sg = _sigmoid(g)
    return dy * (g * sg), dy * o * (sg * (1.0 + g * (1.0 - sg)))


def _fox_bwd(q, k, v, ct, o, lse, dy, gate, riding):
    nq, npair = S // ATT, NH // 2
    ni, no = len(riding.ins), len(riding.outs)

    def body(q_ref, k_ref, v_ref, c_ref, o_ref, lse_ref, dy_ref, gate_ref, *rest):
        dq_ref, dk_ref, dvb_ref, dc_ref, dgate_ref = rest[ni:ni + 5]
        dv_ref = rest[ni + 5 + no]
        pair, i = pl.program_id(0), pl.program_id(1)
        at_end = riding.hooks(rest[:ni], rest[ni + 5:ni + 5 + no], *rest[ni + 6 + no:],
                              first=(pair == 0) & (i == 0), middle=(pair == npair // 2) & (i == 0),
                              last=(pair == npair - 1) & (i == nq - 1))

        @pl.when(i == 0)
        def _():
            dk_ref[...] = jnp.zeros_like(dk_ref)
            dv_ref[...] = jnp.zeros_like(dv_ref)
            dc_ref[...] = jnp.zeros_like(dc_ref)

        q2, lse2 = q_ref[...], lse_ref[...]
        do2, dgate = _gate_grads(dy_ref[...], o_ref[...], gate_ref[...])
        dgate_ref[...] = dgate.astype(BF16)
        do2b = do2.astype(BF16)
        prod = do2b.astype(F32) * o_ref[...]
        heads = []
        for hh in (0, 1):
            hm = _head_mask(hh)
            heads.append((jnp.where(hm, q2, jnp.zeros_like(q2)), jnp.where(hm, do2b, jnp.zeros_like(do2b)),
                          jnp.sum(jnp.where(hm, prod, 0.0), axis=1, keepdims=True),
                          jnp.max(jnp.where(hm, lse2, NEG), axis=1, keepdims=True)))

        def step(off, width, dqs, diag):
            off = pl.multiple_of(off, ATT)
            kj, vj = k_ref[pl.ds(off, width), :], v_ref[pl.ds(off, width), :]
            dk, dv, out = None, None, []
            for hh in (0, 1):
                qm, dom, delta, lse_h = heads[hh]
                s = _dot(qm, kj, NT) - c_ref[hh:hh + 1, pl.ds(off, width)]
                p = jnp.exp(s - lse_h)
                if diag:
                    row = i * ATT + lax.broadcasted_iota(jnp.int32, (ATT, width), 0)
                    col = off + lax.broadcasted_iota(jnp.int32, (ATT, width), 1)
                    p = jnp.where(col <= row, p, 0.0)
                ds = p * (_dot(dom, vj, NT) - delta)
                dc_ref[hh:hh + 1, pl.ds(off, width)] += -jnp.sum(ds, axis=0, keepdims=True)
                dsb = ds.astype(BF16)
                dk_h, dv_h = _dot(dsb, qm, TN), _dot(p.astype(BF16), dom, TN)
                dk, dv = (dk_h, dv_h) if dk is None else (dk + dk_h, dv + dv_h)
                out.append(dqs[hh] + _dot(dsb, kj))
            dk_ref[pl.ds(off, width), :] += dk
            dv_ref[pl.ds(off, width), :] += dv
            return tuple(out)

        zero = jnp.zeros((ATT, LANES), F32)
        dqs = lax.fori_loop(0, i // 2, lambda j, acc: step(j * (2 * ATT), 2 * ATT, acc, False), (zero, zero))
        dqs = lax.cond(i % 2 == 1, lambda acc: step((i - 1) * ATT, 2 * ATT, acc, True),
                       lambda acc: step(i * ATT, ATT, acc, True), dqs)
        dq_ref[...] = jnp.where(_head_mask(0), dqs[0], dqs[1]) * SCALE

        @pl.when(i == nq - 1)
        def _():
            dvb_ref[...] = dv_ref[...].astype(BF16)

        at_end()

    blk = pl.BlockSpec((ATT, LANES), lambda p, i: (i, p))
    full = pl.BlockSpec((S, LANES), lambda p, i: (0, p))
    cspec = pl.BlockSpec((None, 2, S), lambda p, i: (p, 0, 0))
    res = pl.pallas_call(
        body, name="fox_bwd", grid=(npair, nq),
        in_specs=[blk, full, full, cspec, blk, blk, blk, blk] + riding.in_specs,
        out_specs=[blk, full, full, cspec, blk] + riding.out_specs,
        out_shape=[jax.ShapeDtypeStruct((S, D), F32)] * 2 + [jax.ShapeDtypeStruct((S, D), BF16),
                                                              jax.ShapeDtypeStruct((npair, 2, S), F32),
                                                              jax.ShapeDtypeStruct((S, D), BF16)]
                  + riding.out_shape,
        scratch_shapes=[pltpu.VMEM((S, LANES), F32)] + riding.scratch,
        compiler_params=_params(2))(q, k, v, ct, o, lse, dy, gate, *riding.ins)
    return res[0], res[1], res[2], res[3], res[4], res[5:]


def _both_heads(x):
    return jnp.concatenate([jnp.where(_head_mask(hh), x, jnp.zeros_like(x)) for hh in (0, 1)], axis=0)


def _per_head(col0, col1):
    return jnp.concatenate([jnp.broadcast_to(col0, (WINDOW, 1)), jnp.broadcast_to(col1, (WINDOW, 1))], axis=0)


def _unstack(x2):
    return jnp.where(_head_mask(0), x2[:WINDOW], x2[WINDOW:])


def _swa_valid(i, start):
    r = lax.broadcasted_iota(jnp.int32, (2 * WINDOW, 2 * WINDOW), 0)
    qabs = i * WINDOW + jnp.where(r >= WINDOW, r - WINDOW, r)
    kabs = start + lax.broadcasted_iota(jnp.int32, (2 * WINDOW, 2 * WINDOW), 1)
    return (kabs <= qabs) & (qabs - kabs < WINDOW)


def _swa_fwd(q, kdup, vdup, sinks_t, proj, gate_col):
    def body(q_ref, k_ref, v_ref, sk_ref, gate_ref, o_ref, lse_ref, y_ref):
        skv = sk_ref[...]
        first = _head_mask(0)
        for sb in range(SWQ):
            i = pl.program_id(1) * SWQ + sb
            rows = slice(sb * WINDOW, (sb + 1) * WINDOW)
            start = pl.multiple_of(jnp.maximum(i - 1, 0) * WINDOW, WINDOW)
            kk, vv = k_ref[pl.ds(start, 2 * WINDOW), :], v_ref[pl.ds(start, 2 * WINDOW), :]
            q2 = q_ref[rows, :]
            valid = _swa_valid(i, start)[:WINDOW]
            res = []
            for hh in (0, 1):
                hm = _head_mask(hh)
                sink = jnp.max(jnp.where(hm, skv, NEG), axis=1, keepdims=True)
                s = jnp.where(valid, _dot(jnp.where(hm, q2, jnp.zeros_like(q2)), kk, NT), NEG)
                m = jnp.maximum(jnp.max(s, axis=1, keepdims=True), sink)
                p = jnp.exp(s - m)
                l = jnp.sum(p, axis=1, keepdims=True) + jnp.exp(sink - m)
                res.append((_dot(p.astype(BF16), vv) / l, m + jnp.log(l)))
            o = jnp.where(first, res[0][0], res[1][0])
            o_ref[rows, :] = o
            lse_ref[rows, :] = jnp.where(first, res[0][1], res[1][1])
            g = gate_ref[rows, :]
            y_ref[rows, :] = (o * (g * _sigmoid(g))).astype(BF16)

    blk = pl.BlockSpec((SWQ * WINDOW, LANES), lambda p, i: (i, p))
    gate = pl.BlockSpec((SWQ * WINDOW, LANES), lambda p, i: (i, gate_col + p))
    full = pl.BlockSpec((S, LANES), lambda p, i: (0, p // 2))
    return pl.pallas_call(
        body, name="swa_fwd", grid=(NH // 2, S // (SWQ * WINDOW)),
        in_specs=[blk, full, full, pl.BlockSpec((1, LANES), lambda p, i: (0, p)), gate],
        out_specs=[blk, blk, blk],
        out_shape=[jax.ShapeDtypeStruct((S, D), F32)] * 2 + [jax.ShapeDtypeStruct((S, D), BF16)],
        compiler_params=_params(2))(q, kdup, vdup, sinks_t, proj)


def _swa_bwd(q, kdup, vdup, sinks_t, o, lse, dy, proj, gate_col):
    def body(q_ref, k_ref, v_ref, sk_ref, o_ref, lse_ref, dy_ref, gate_ref, dq_ref, dk_ref, dv_ref, dsk_ref,
             dgate_ref):
        @pl.when(pl.program_id(1) == 0)
        def _():
            dk_ref[...] = jnp.zeros_like(dk_ref)
            dv_ref[...] = jnp.zeros_like(dv_ref)
            dsk_ref[...] = jnp.zeros_like(dsk_ref)

        skv = sk_ref[...]
        first = _head_mask(0)
        sink = _per_head(*[jnp.max(jnp.where(_head_mask(hh), skv, NEG), axis=1, keepdims=True) for hh in (0, 1)])
        for sb in range(SWQ):
            i = pl.program_id(1) * SWQ + sb
            rows = slice(sb * WINDOW, (sb + 1) * WINDOW)
            start = pl.multiple_of(jnp.maximum(i - 1, 0) * WINDOW, WINDOW)
            kk, vv = k_ref[pl.ds(start, 2 * WINDOW), :], v_ref[pl.ds(start, 2 * WINDOW), :]
            do2, dgate = _gate_grads(dy_ref[rows, :], o_ref[rows, :], gate_ref[rows, :])
            dgate_ref[rows, :] = dgate.astype(BF16)
            do2b = do2.astype(BF16)
            prod, lse2 = do2b.astype(F32) * o_ref[rows, :], lse_ref[rows, :]
            qs, dos = _both_heads(q_ref[rows, :]), _both_heads(do2b)
            delta = jnp.concatenate([jnp.sum(jnp.where(_head_mask(hh), prod, 0.0), axis=1, keepdims=True)
                                     for hh in (0, 1)], axis=0)
            lse_h = jnp.concatenate([jnp.max(jnp.where(_head_mask(hh), lse2, NEG), axis=1, keepdims=True)
                                     for hh in (0, 1)], axis=0)
            p = jnp.where(_swa_valid(i, start), jnp.exp(_dot(qs, kk, NT) - lse_h), 0.0)
            dsb = (p * (_dot(dos, vv, NT) - delta)).astype(BF16)
            dk_ref[pl.ds(start, 2 * WINDOW), :] += _dot(dsb, qs, TN)
            dv_ref[pl.ds(start, 2 * WINDOW), :] += _dot(p.astype(BF16), dos, TN)
            dq_ref[rows, :] = _unstack(_dot(dsb, kk)) * SCALE
            t = jnp.exp(sink - lse_h) * delta
            dsk_ref[...] += -jnp.where(first, jnp.sum(t[:WINDOW], axis=0, keepdims=True),
                                       jnp.sum(t[WINDOW:], axis=0, keepdims=True))

    blk = pl.BlockSpec((SWQ * WINDOW, LANES), lambda p, i: (i, p))
    full = pl.BlockSpec((S, LANES), lambda p, i: (0, p // 2))
    acc = pl.BlockSpec((S, LANES), lambda p, i: (0, p))
    sk = pl.BlockSpec((1, LANES), lambda p, i: (0, p))
    gate = pl.BlockSpec((SWQ * WINDOW, LANES), lambda p, i: (i, gate_col + p))
    return pl.pallas_call(
        body, name="swa_bwd", grid=(NH // 2, S // (SWQ * WINDOW)),
        in_specs=[blk, full, full, sk, blk, blk, blk, gate],
        out_specs=[blk, acc, acc, sk, blk],
        out_shape=[jax.ShapeDtypeStruct((S, D), F32)] * 3 + [jax.ShapeDtypeStruct((1, D), F32),
                                                              jax.ShapeDtypeStruct((S, D), BF16)],
        compiler_params=_params(2))(q, kdup, vdup, sinks_t, o, lse, dy, proj)


def _adamw_math(w, g, m, v):
    m = ADAM_B1 * m + (1.0 - ADAM_B1) * g
    v = ADAM_B2 * v + (1.0 - ADAM_B2) * jnp.square(g)
    m_hat = m / (1.0 - ADAM_B1 ** ADAM_STEP)
    v_hat = v / (1.0 - ADAM_B2 ** ADAM_STEP)
    delta = -ADAM_LR * (m_hat / (jnp.sqrt(v_hat) + ADAM_EPS) + ADAM_WD * w)
    return delta, m, v


def _adamw(w, g, m, v, name):
    r, c = w.shape
    tr = min(r, 128)

    def body(w_ref, g_ref, m_ref, v_ref, d_ref, mo_ref, vo_ref):
        d_ref[...], mo_ref[...], vo_ref[...] = _adamw_math(w_ref[...], g_ref[...], m_ref[...], v_ref[...])

    spec = pl.BlockSpec((tr, c), lambda i: (i, 0))
    return pl.pallas_call(
        body, name=name, grid=(r // tr,), in_specs=[spec] * 4, out_specs=[spec] * 3,
        out_shape=[jax.ShapeDtypeStruct((r, c), F32)] * 3, compiler_params=_params(1))(w, g, m, v)


SUM_TILE = 128


FLAT_BLOCK = 257 * 1024


def _tiles(shape, axis, lead=0):
    if len(shape) == 1:
        count = shape[0] // FLAT_BLOCK
        return (FLAT_BLOCK,), count, lambda pos, *lead_idx: (sum(k * count for k in lead_idx) + pos,)
    r, c = shape
    blk = (SUM_TILE, c) if axis == 0 else (r, SUM_TILE)
    count = shape[axis] // SUM_TILE

    def index(pos, *lead_idx):
        return tuple(lead_idx) + ((pos, 0) if axis == 0 else (0, pos))

    return (None,) * lead + blk, count, index


def _adamw_halves(w, g_mine, g_theirs, m, v, axis, name):
    blk, count, index = _tiles(w.shape, axis)
    per_half = count // 2

    def body(w_ref, a_ref, b_ref, m_ref, v_ref, g_ref, d_ref, mo_ref, vo_ref):
        is_mine = pl.program_id(0) // per_half == lax.axis_index("c")
        g = jnp.where(is_mine, a_ref[...], b_ref[...])
        g_ref[...] = g
        d_ref[...], mo_ref[...], vo_ref[...] = _adamw_math(w_ref[...], g, m_ref[...], v_ref[...])

    spec = pl.BlockSpec(blk, lambda i: index(i))
    half = pl.BlockSpec(blk, lambda i: index(i % per_half))
    return pl.pallas_call(
        body, name=name, grid=(count,), in_specs=[spec, half, half, spec, spec], out_specs=[spec] * 4,
        out_shape=[jax.ShapeDtypeStruct(w.shape, F32)] * 4, compiler_params=_params(1))(w, g_mine, g_theirs, m, v)


def _chip_sum(blocks, from_sibling, axis, name):
    flat = blocks.ndim == 1
    blk, count, index = _tiles((from_sibling.shape[0] // NCHIP,) if flat else from_sibling.shape[1:], axis, lead=1)

    def body(lo_ref, hi_ref, p_ref, o32, o16):
        mine = jnp.where(lax.axis_index("c") == 0, lo_ref[...], hi_ref[...])
        acc = mine + p_ref[...]
        o32[...] = acc
        o16[...] = acc.astype(BF16)

    half = pl.BlockSpec(blk, lambda k, i: index(i, k))
    if flat:
        lo = pl.BlockSpec(blk, lambda k, i: (2 * count * k + i,))
        hi = pl.BlockSpec(blk, lambda k, i: (2 * count * k + count + i,))
    else:
        lo, hi = half, pl.BlockSpec(blk, lambda k, i: index(i + count, k))
    return pl.pallas_call(
        body, name=name, grid=(NCHIP, count), in_specs=[lo, hi, half], out_specs=[half, half],
        out_shape=[jax.ShapeDtypeStruct(from_sibling.shape, F32), jax.ShapeDtypeStruct(from_sibling.shape, BF16)],
        compiler_params=_params(2))(blocks, blocks, from_sibling)


def _mesh_sum(own, parts, axis, name):
    blk, count, index = _tiles(own.shape, axis)
    n = NCHIP - 1

    def body(a_ref, *refs):
        acc = a_ref[...]
        for k in range(n):
            acc = acc + refs[k][...].astype(F32)
        refs[n][...] = acc

    spec = pl.BlockSpec(blk, lambda i: index(i))
    if own.ndim == 1:
        part = [pl.BlockSpec(blk, lambda i, k=k: (k * count + i,)) for k in range(n)]
    else:
        part = [pl.BlockSpec((None,) + blk, lambda i, k=k: (k,) + index(i)) for k in range(n)]
    return pl.pallas_call(
        body, name=name, grid=(count,), in_specs=[spec] + part,
        out_specs=spec, out_shape=jax.ShapeDtypeStruct(own.shape, F32),
        compiler_params=_params(1))(own, *([parts] * n))


def _sum_stack(parts, name):
    n = parts.shape[0]

    def body(p_ref, o_ref):
        acc = p_ref[0]
        for k in range(1, n):
            acc = acc + p_ref[k]
        o_ref[...] = acc

    return pl.pallas_call(body, name=name, out_shape=jax.ShapeDtypeStruct(parts.shape[1:], F32))(parts)


def _coords():
    return lax.axis_index("x"), lax.axis_index("y"), lax.axis_index("c")


def _chip(who):
    return 2 * who[0] + who[1]


def _flip(who, mask):
    return tuple((1 - v) if b else v for v, b in zip(who, mask))


def _transfer(transfers, t, I, O, ssem, rsem, receiving):
    tr, me = transfers[t], _coords()
    peer = _flip(me, tr["mask"])
    return pltpu.make_async_remote_copy(
        src_ref=tr["src"](I, O, me), dst_ref=tr["dst"](I, O, peer if receiving else me),
        send_sem=ssem.at[t], recv_sem=rsem.at[t], device_id=peer, device_id_type=MESH)


def _start_transfers(transfers, I, O, ssem, rsem, onward):
    arrived = set()
    for t, tr in enumerate(transfers):
        after = tr.get("after")
        if (after is not None) != onward:
            continue
        if after is not None and after not in arrived:
            _transfer(transfers, after, I, O, ssem, rsem, True).wait_recv()
            arrived.add(after)
        _transfer(transfers, t, I, O, ssem, rsem, False).start()


def _finish_transfers(transfers, I, O, ssem, rsem):
    passed_on = {tr["after"] for tr in transfers if tr.get("after") is not None}
    for t in range(len(transfers)):
        if t not in passed_on:
            _transfer(transfers, t, I, O, ssem, rsem, True).wait_recv()
    for t in range(len(transfers)):
        _transfer(transfers, t, I, O, ssem, rsem, False).wait_send()


def _own_copies(own, I, O, stage, lsem, leg):
    for n, (src, dst) in enumerate(own):
        me = _coords()
        bring =pltpu.make_async_copy(src(I, O, me), stage[n], lsem.at[2 * n])
        put = pltpu.make_async_copy(stage[n], dst(I, O, me), lsem.at[2 * n + 1])
        if leg == 0:
            bring.start()
        elif leg == 1:
            bring.wait()
            put.start()
        else:
            put.wait()


def _own_scratch(own, ins):
    return [pltpu.VMEM(ins[n].shape, ins[n].dtype) for n in range(len(own))], pltpu.SemaphoreType.DMA((max(2 * len(own), 1),))


def _exchange(name, ins, outs, transfers, own=()):
    ni, no = len(ins), len(outs)
    nt = len(transfers)
    stages, stage_sems = _own_scratch(own, ins)

    def body(*refs):
        I, O = refs[:ni], refs[ni:ni + no]
        ssem, rsem, lsem = refs[ni + no:ni + no + 3]
        stage = refs[ni + no + 3:]
        _own_copies(own, I, O, stage, lsem, 0)
        _start_transfers(transfers, I, O, ssem, rsem, False)
        _own_copies(own, I, O, stage, lsem, 1)
        _start_transfers(transfers, I, O, ssem, rsem, True)
        _finish_transfers(transfers, I, O, ssem, rsem)
        _own_copies(own, I, O, stage, lsem, 2)

    hbm = pl.BlockSpec(memory_space=pltpu.HBM)
    return pl.pallas_call(
        body, name=name, in_specs=[hbm] * ni, out_specs=[hbm] * no,
        out_shape=[jax.ShapeDtypeStruct(s, d) for s, d in outs],
        scratch_shapes=[pltpu.SemaphoreType.DMA((nt,)), pltpu.SemaphoreType.DMA((nt,)), stage_sems] + stages,
        compiler_params=pltpu.CompilerParams(has_side_effects=True, vmem_limit_bytes=VMEM_LIMIT))(*ins)


CHIP_MASKS = [(0, 1, 0), (1, 0, 0), (1, 1, 0)]
SIBLING = (0, 0, 1)


def _half(shape2d, axis, which):
    n = shape2d[axis] // 2
    cut = pl.ds(pl.multiple_of(which * n, n), n)
    return (cut, slice(None)) if axis == 0 else (slice(None), cut)


class _Riding:
    def __init__(self, transfers, ins, outs, own=()):
        self.transfers, self.ins, self.outs, self.own = transfers, list(ins), list(outs), list(own)
        hbm = pl.BlockSpec(memory_space=pltpu.HBM)
        self.in_specs, self.out_specs = [hbm] * len(self.ins), [hbm] * len(self.outs)
        self.out_shape = [jax.ShapeDtypeStruct(s, d) for s, d in self.outs]
        stages, stage_sems = _own_scratch(self.own, self.ins)
        self.scratch = [pltpu.SemaphoreType.DMA((max(len(transfers), 1),))] * 2 + [stage_sems] + stages

    def hooks(self, I, O, ssem, rsem, lsem, *stage, first, middle, last):
        tr, own = self.transfers, self.own

        @pl.when(first)
        def _():
            _own_copies(own, I, O, stage, lsem, 0)
            _start_transfers(tr, I, O, ssem, rsem, False)

        if own or any(t.get("after") is not None for t in tr):
            @pl.when(middle)
            def _():
                _own_copies(own, I, O, stage, lsem, 1)
                _start_transfers(tr, I, O, ssem, rsem, True)

        def at_end():
            @pl.when(last)
            def _():
                _finish_transfers(tr, I, O, ssem, rsem)
                _own_copies(own, I, O, stage, lsem, 2)

        return at_end


def _stretch(n, pos):
    return (pl.ds(pos * n if isinstance(pos, int) else pl.multiple_of(pos * n, n), n),)


def _gather_plan(shards, axes):
    def half(a, who):
        if shards[a].ndim == 1:
            return _stretch(shards[a].shape[0] // 2, who[2])
        return _half(shards[a].shape, axes[a], who[2])

    def landed(a, chip, who):
        if shards[a].ndim == 1:
            return _stretch(shards[a].shape[0] // 2, 2 * chip + who[2])
        return (chip,) + half(a, who)

    over_ici, onward = [], []
    for a in range(len(shards)):
        for mask in CHIP_MASKS:
            over_ici.append(dict(
                mask=mask,
                src=lambda I, O, me, a=a: I[a].at[half(a, me)],
                dst=lambda I, O, who, a=a: O[a].at[landed(a, _chip(who), who)]))
            onward.append(dict(
                mask=SIBLING, after=len(over_ici) - 1,
                src=lambda I, O, me, a=a, mask=mask: O[a].at[landed(a, _chip(_flip(me, mask)), me)],
                dst=lambda I, O, who, a=a, mask=mask: O[a].at[landed(a, _chip(_flip(who, mask)), who)]))
    outs = [((NCHIP * s.shape[0],) if s.ndim == 1 else (NCHIP,) + s.shape, s.dtype) for s in shards]

    def whole(a, chip):
        return _stretch(shards[a].shape[0], chip) if shards[a].ndim == 1 else (chip,)

    own = [(lambda I, O, me, a=a: I[a], lambda I, O, me, a=a: O[a].at[whole(a, _chip(me))])
           for a in range(len(shards))]
    return over_ici + onward, outs, own


def _gather_shards(shards, axes):
    transfers, outs, own = _gather_plan(shards, axes)
    return _exchange("gather_weights", shards, outs, transfers, own)


def _to_sibling(arrs, name):
    transfers = [dict(mask=SIBLING, src=lambda I, O, me, a=a: I[a], dst=lambda I, O, who, a=a: O[a])
                 for a in range(len(arrs))]
    return _exchange(name, arrs, [(t.shape, t.dtype) for t in arrs], transfers)


def _halves_to_sibling(blocks, axes, name):
    def cut(a, which):
        return (slice(None),) + _half(blocks[a].shape[1:], axes[a], which)

    transfers, outs = [], []
    for a, (b, ax) in enumerate(zip(blocks, axes)):
        if b.ndim == 1:
            h = b.shape[0] // NCHIP // 2
            for k in range(NCHIP):
                transfers.append(dict(mask=SIBLING,
                                      src=lambda I, O, me, a=a, k=k, h=h: I[a].at[_stretch(h, 2 * k + 1 - me[2])],
                                      dst=lambda I, O, who, a=a, k=k, h=h: O[a].at[_stretch(h, k)]))
            outs.append(((NCHIP * h,), b.dtype))
        else:
            transfers.append(dict(mask=SIBLING, src=lambda I, O, me, a=a: I[a].at[cut(a, 1 - me[2])],
                                  dst=lambda I, O, who, a=a: O[a]))
            shape = list(b.shape)
            shape[ax + 1] //= 2
            outs.append((tuple(shape), b.dtype))
    return _exchange(name, blocks, outs, transfers)


def _scatter_plan(tb):
    def slot(a, k):
        return (k,) if tb[a].ndim == 3 else _stretch(tb[a].shape[0] // NCHIP, k)

    transfers = []
    for a in range(len(tb)):
        for n, mask in enumerate(CHIP_MASKS):
            transfers.append(dict(
                mask=mask,
                src=lambda I, O, me, a=a, mask=mask: I[a].at[slot(a, _chip(_flip(me, mask)))],
                dst=lambda I, O, who, a=a, n=n: O[a].at[slot(a, n)]))
    outs = [((3,) + t.shape[1:] if t.ndim == 3 else (3 * (t.shape[0] // NCHIP),), t.dtype) for t in tb]
    return transfers, outs


def _scatter_chip_sums(tb):
    transfers, outs = _scatter_plan(tb)
    return _exchange("scatter_grads", tb, outs, transfers)


def _gather_small(vec):
    def slot(who):
        return 4 * who[0] + 2 * who[1] + who[2]

    masks = [(m >> 2 & 1, m >> 1 & 1, m & 1) for m in range(1, 8)]
    transfers = [dict(mask=mask, src=lambda I, O, me: I[0], dst=lambda I, O, who: O[0].at[slot(who)])
                 for mask in masks]
    own = [(lambda I, O, me: I[0], lambda I, O, me: O[0].at[slot(me)])]
    return _exchange("gather_small", [vec], [((8,) + vec.shape, vec.dtype)], transfers, own)[0]


def _rope_tables(positions):
    half = ROT // 2
    inv_freq = jnp.power(jnp.float32(THETA), -jnp.arange(0, ROT, 2, dtype=F32) / ROT)
    ang = positions.astype(F32)[:, None] * inv_freq[None, :]
    cos, sin = jnp.cos(ang), jnp.sin(ang)
    one, zero, z8 = jnp.ones((S, HD - ROT), F32), jnp.zeros((S, HD - ROT), F32), jnp.zeros((S, half), F32)
    c = jnp.concatenate([cos, cos, one], axis=1)
    a = jnp.concatenate([-sin, z8, zero], axis=1)
    b = jnp.concatenate([z8, sin, zero], axis=1)
    return tuple(jnp.tile(t, (1, 2)) for t in (c, a, b))


def _tile_heads(g, w):
    return jnp.tile(g.reshape(1, HD), (1, w // HD))


def _fold_heads(dg):
    return dg.reshape(-1, HD).sum(axis=0)


def _pad_lanes(a):
    return jnp.pad(a, ((0, 0), (0, LANES - a.shape[1])))


def _local_step(x, target, positions, wt, fetch, late_weights, begin_reduce):
    rope = _rope_tables(positions)
    w1t = wt["w_in_a_t"]
    f_row = 3 * D // LANES
    wg_t = w1t[3 * D + NH:]
    in_b_block = lambda c: pl.BlockSpec((None, TN_, TN_), lambda j, i: (c, j, 0))
    b_pad = _pad_lanes(wt["b_forget"].reshape(1, NH))
    qg_a, kg_a = _tile_heads(wt["qnorm_a_g"], D), _tile_heads(wt["knorm_a_g"], D)
    qg_b, kg_b = _tile_heads(wt["qnorm_b_g"], D), _tile_heads(wt["knorm_b_g"], KVW)
    norm_a, kv_g, norm_b = wt["norm_a_g"].reshape(1, D), wt["kv_norm_g"].reshape(1, D), wt["norm_b_g"].reshape(1, D)
    sinks_t = jnp.repeat(wt["sinks"].reshape(1, NH), HD, axis=1)

    (u_a,) = _rmsnorm_fwd(x, [norm_a], "norm_a")
    qkv = _mm("proj_a", S, 3 * D, [(u_a, _a_rows(D), w1t, _b_rows(D), NT)])
    fpad = _mm("proj_f", S, LANES, [(u_a, _a_rows(D), w1t, _b_rows(D, row0=f_row, tn=LANES), NT)], tn=LANES)
    gate_a = _mm("proj_gate_a", S, D, [(u_a, _a_rows(D), wg_t, _b_rows(D), NT)])
    q_a, k_a, v_a = _a_post(qkv, qg_a, kg_a)
    ct = _forget_cumsum(fpad, b_pad)
    ct2 = ct[:NH].reshape(NH // 2, 2, S)
    o_a, lse_a, y_a, fetched = _fox_fwd(q_a, k_a, v_a, ct2, gate_a, fetch)
    wt = {**wt, **late_weights(fetched)}
    w_in_b = wt["w_in_b"]
    h1 = _mm("out_a", S, D, [(y_a, _a_rows(D), wt["w_out_a"], _b_cols(D), None)], add=x)
    u_kv, u_b = _rmsnorm_fwd(h1, [kv_g, norm_b], "norm_b")
    kv = _mm("proj_kv", S, 2 * KVW, [(u_kv, _a_rows(D), wt["w_kv"], _b_cols(D), None)])
    pb = _mm("proj_b", S, 2 * D,
             [(u_b, _a_rows(D), w_in_b, pl.BlockSpec((None, D, TN_), lambda j, i: (j, 0, 0)), None)])
    q_b, kdup, vdup = _b_post(pb, kv, qg_b, kg_b, rope)
    gate_b_col = D // LANES
    o_b, lse_b, y_b = _swa_fwd(q_b, kdup, vdup, sinks_t, pb, gate_b_col)
    out = _mm("out_b", S, D, [(y_b, _a_rows(D), wt["w_out_b"], _b_cols(D), None)], add=h1)
    d_out, d_out_b, sq = _loss_head(out, target)

    g = {}
    g["w_out_b"] = _mm("dw_out_b", D, D, [(y_b, _a_cols(S), d_out_b, _b_cols(S), TN)])
    d_y_b = _mm("dy_b", S, D, [(d_out_b, _a_rows(D), wt["w_out_b"], _b_rows(D), NT)])
    dq_b, dkdup, dvdup, dsk, d_gate_b = _swa_bwd(q_b, kdup, vdup, sinks_t, o_b, lse_b, d_y_b, pb, gate_b_col)
    g["sinks"] = dsk[0, ::HD]
    d_qb_raw, dg = _headnorm_bwd(pb, 0, qg_b, dq_b, rope, "qnorm_b_bwd")
    g["qnorm_b_g"] = _fold_heads(dg)
    d_pb = [d_qb_raw, d_qb_raw, d_gate_b, d_gate_b]
    g["w_in_b"] = jnp.concatenate([
        _mm("dw_in_b_q", D, D, [(u_b, _a_cols(S), d_qb_raw, _b_cols(S), TN)], stacked=True),
        _mm("dw_in_b_gate", D, D, [(u_b, _a_cols(S), d_gate_b, _b_cols(S), TN)], stacked=True)], axis=0)
    d_u_b = _mm("du_b", S, D, [(d_pb[c], _a_rows(TN_, col=c % 2), w_in_b, in_b_block(c), NT) for c in range(NCHIP)])
    d_kv, dg = _kv_bwd(dkdup, dvdup, kv, kg_b, rope)
    g["knorm_b_g"] = _fold_heads(dg)
    g["w_kv"] = _mm("dw_kv", D, 2 * KVW, [(u_kv, _a_cols(S), d_kv, _b_cols(S), TN)])
    d_u_kv = _mm("du_kv", S, D, [(d_kv, _a_rows(2 * KVW), wt["w_kv"], _b_rows(2 * KVW), NT)])
    d_h1, d_h1_b, g["kv_norm_g"], g["norm_b_g"] = _rmsnorm_bwd(h1, [kv_g, norm_b], [d_u_kv, d_u_b], d_out, "norm_b_bwd")
    g["w_out_a"] = _mm("dw_out_a", D, D, [(y_a, _a_cols(S), d_h1_b, _b_cols(S), TN)])
    d_y_a = _mm("dy_a", S, D, [(d_h1_b, _a_rows(D), wt["w_out_a"], _b_rows(D), NT)])
    riding, so_far = begin_reduce({n: g[n] for n in LATE})
    dq_a, dk_a, dv_a, dct, d_gate_a, arrived = _fox_bwd(q_a, k_a, v_a, ct2, o_a, lse_a, d_y_a, gate_a, riding)
    dct_pad = jnp.pad(dct.reshape(NH, S), ((0, LANES - NH), (0, 0)))
    d_f, db = _forget_bwd(dct_pad, fpad, b_pad)
    g["b_forget"] = db[0, :NH]
    d_q_raw, dg = _headnorm_bwd(qkv, 0, qg_a, dq_a, None, "qnorm_a_bwd")
    g["qnorm_a_g"] = _fold_heads(dg)
    d_k_raw, dg = _headnorm_bwd(qkv, 1, kg_a, dk_a, None, "knorm_a_bwd")
    g["knorm_a_g"] = _fold_heads(dg)
    rows, gw = 4 * D + NH, None
    for n, t, row0 in (("q", d_q_raw, 0), ("k", d_k_raw, D), ("v", dv_a, 2 * D)):
        gw = _mm("dw_in_a_" + n, D, D, [(t, _a_cols(S), u_a, _b_cols(S), TN)], rows_of=(gw, rows, row0))
    gw = _mm("dw_in_a_f", LANES, D, [(d_f, _a_cols(S, tm=LANES), u_a, _b_cols(S), TN)], tm=LANES,
             rows_of=(gw, rows, 3 * D))
    g["w_in_a"] = _mm("dw_in_a_gate", D, D, [(d_gate_a, _a_cols(S), u_a, _b_cols(S), TN)],
                      rows_of=(gw, rows, 3 * D + NH))
    riding, so_far_first = begin_reduce({"w_in_a": g["w_in_a"]})
    d_u_a, arrived_first = _mm("du_a", S, D, [
        (d_q_raw, _a_rows(D), w1t, _b_cols(D, row=0), None), (d_k_raw, _a_rows(D), w1t, _b_cols(D, row=1), None),
        (dv_a, _a_rows(D), w1t, _b_cols(D, row=2), None), (d_gate_a, _a_rows(D), wg_t, _b_cols(D), None),
        (d_f, _a_rows(LANES), w1t, _b_cols(LANES, row=f_row), None)], riding=riding)
    d_x, _, g["norm_a_g"] = _rmsnorm_bwd(x, [norm_a], [d_u_a], d_h1, "norm_a_bwd")
    return sq, d_x, g, (list(so_far_first) + list(so_far), list(arrived_first) + list(arrived))


BIG = ["w_in_a", "w_out_a", "w_kv", "w_in_b", "w_out_b"]
LATE = BIG[1:]
SPLIT = {"w_in_a": None, "w_out_a": 0, "w_kv": 0, "w_in_b": 0, "w_out_b": 0}
SMALL = ["norm_a_g", "b_forget", "qnorm_a_g", "knorm_a_g", "kv_norm_g", "knorm_b_g", "norm_b_g", "qnorm_b_g", "sinks"]
NAMES = ["norm_a_g", "w_in_a", "b_forget", "qnorm_a_g", "knorm_a_g", "w_out_a", "kv_norm_g", "w_kv", "knorm_b_g",
         "norm_b_g", "w_in_b", "qnorm_b_g", "sinks", "w_out_b"]


def _pack(vals):
    flat = []
    for v in vals:
        v = v.reshape(-1)
        flat.append(jnp.pad(v, (0, -v.shape[0] % LANES)))
    flat = jnp.concatenate(flat)
    flat = jnp.pad(flat, (0, -flat.shape[0] % (8 * LANES)))
    return flat.reshape(-1, LANES)


def _unpack(packed, shapes):
    flat, out, off = packed.reshape(-1), [], 0
    for s in shapes:
        n = int(np.prod(s))
        out.append(flat[off:off + n].reshape(s))
        off += n + (-n % LANES)
    return out


def kernel(x, positions, norm_a_g, w_in_a, b_forget, qnorm_a_g, knorm_a_g, w_out_a, kv_norm_g, w_kv, knorm_b_g, norm_b_g, w_in_b, qnorm_b_g, sinks, w_out_b, loss_target, m_norm_a_g, m_w_in_a, m_b_forget, m_qnorm_a_g, m_knorm_a_g, m_w_out_a, m_kv_norm_g, m_w_kv, m_knorm_b_g, m_norm_b_g, m_w_in_b, m_qnorm_b_g, m_sinks, m_w_out_b, v_norm_a_g, v_w_in_a, v_b_forget, v_qnorm_a_g, v_knorm_a_g, v_w_out_a, v_kv_norm_g, v_w_kv, v_knorm_b_g, v_norm_b_g, v_w_in_b, v_qnorm_b_g, v_sinks, v_w_out_b):
    w = dict(norm_a_g=norm_a_g, w_in_a=w_in_a, b_forget=b_forget, qnorm_a_g=qnorm_a_g, knorm_a_g=knorm_a_g,
             w_out_a=w_out_a, kv_norm_g=kv_norm_g, w_kv=w_kv, knorm_b_g=knorm_b_g, norm_b_g=norm_b_g,
             w_in_b=w_in_b, qnorm_b_g=qnorm_b_g, sinks=sinks, w_out_b=w_out_b)
    m = dict(norm_a_g=m_norm_a_g, w_in_a=m_w_in_a, b_forget=m_b_forget, qnorm_a_g=m_qnorm_a_g, knorm_a_g=m_knorm_a_g,
             w_out_a=m_w_out_a, kv_norm_g=m_kv_norm_g, w_kv=m_w_kv, knorm_b_g=m_knorm_b_g, norm_b_g=m_norm_b_g,
             w_in_b=m_w_in_b, qnorm_b_g=m_qnorm_b_g, sinks=m_sinks, w_out_b=m_w_out_b)
    v = dict(norm_a_g=v_norm_a_g, w_in_a=v_w_in_a, b_forget=v_b_forget, qnorm_a_g=v_qnorm_a_g, knorm_a_g=v_knorm_a_g,
             w_out_a=v_w_out_a, kv_norm_g=v_kv_norm_g, w_kv=v_w_kv, knorm_b_g=v_knorm_b_g, norm_b_g=v_norm_b_g,
             w_in_b=v_w_in_b, qnorm_b_g=v_qnorm_b_g, sinks=v_sinks, w_out_b=v_w_out_b)
    my_chip = 2 * lax.axis_index("x") + lax.axis_index("y")

    def shard2d(t, n):
        if n == "w_in_a":
            return jnp.transpose(t, (2, 0, 1)).reshape(-1)
        return t.reshape(t.shape[-2:])

    def unflat(t, n):
        return jnp.transpose(t.reshape(-1, 1, D), (1, 2, 0)) if n == "w_in_a" else t.reshape(w[n].shape)

    w2d = {n: shard2d(w[n], n) for n in BIG}

    norm_a_rows = jnp.broadcast_to(norm_a_g.reshape(1, D // NCHIP), (16, D // NCHIP))
    w1t, norm_rows = _gather_shards([w2d["w_in_a"].astype(BF16), norm_a_rows], [SPLIT["w_in_a"], 0])
    wt = {"w_in_a_t": w1t.reshape(-1, D), "norm_a_g": norm_rows[:, 0, :].reshape(1, D)}
    for n in SMALL[1:]:
        wt[n] = w[n]
    late_shards = [w2d[n].astype(BF16) for n in LATE]
    late_axes = [SPLIT[n] for n in LATE]
    transfers, outs, own = _gather_plan(late_shards, late_axes)
    fetch = _Riding(transfers, late_shards, outs, own)

    def late_weights(fetched):
        return {n: t if n == "w_in_b" else t.reshape(-1, t.shape[2]) for n, t in zip(LATE, fetched)}

    def as_blocks(t):
        if t.ndim == 3:
            return t
        return t.reshape(-1) if t.shape[0] % (8 * NCHIP) else t.reshape(NCHIP, -1, t.shape[1])

    def begin_reduce(grads):
        names = list(grads)
        axes = [SPLIT[n] for n in names]
        blocks = [as_blocks(grads[n]) for n in names]
        halves = _halves_to_sibling(blocks, axes, "sibling_halves_" + names[0])
        sums = [_chip_sum(blk, part, ax, "chip_sum_" + n) for n, ax, blk, part in zip(names, axes, blocks, halves)]
        bf16 = [s[1] for s in sums]
        transfers, outs = _scatter_plan(bf16)
        return _Riding(transfers, bf16, outs), [s[0] for s in sums]

    sq, d_x, g, (chip_f32, arrived) = _local_step(x[0], loss_target[0], positions, wt, fetch, late_weights,
                                                  begin_reduce)

    small_shapes = [(D,), (NH,), (HD,), (HD,), (D,), (HD,), (D,), (HD,), (NH,), (D,)]
    packed = _pack([g[n] for n in SMALL] + [sq])
    total = _sum_stack(_gather_small(packed), "sum_small")
    small_g = dict(zip(SMALL, _unpack(total, small_shapes)[:-1]))
    loss = 0.5 * jnp.sum(_unpack(total, small_shapes)[-1]) / D
    small_g["norm_a_g"] = lax.dynamic_slice(small_g["norm_a_g"], (my_chip * (D // NCHIP),), (D // NCHIP,))

    axes = [SPLIT[n] for n in BIG]
    halves = []
    for n, ax, t32, parts in zip(BIG, axes, chip_f32, arrived):
        if t32.ndim == 1:
            own = lax.dynamic_slice_in_dim(t32, my_chip * (t32.shape[0] // NCHIP), t32.shape[0] // NCHIP)
        else:
            own = lax.dynamic_index_in_dim(t32, my_chip, axis=0, keepdims=False)
        halves.append(_mesh_sum(own, parts, ax, "mesh_sum_" + n))
    sibling_done = _to_sibling(halves, "finished_halves")

    res = {}
    for n, ax, mine_half, their_half in zip(BIG, axes, halves, sibling_done):
        out4 = _adamw_halves(w2d[n], mine_half, their_half, shard2d(m[n], n), shard2d(v[n], n), ax, "adamw_" + n)
        res[n] = tuple(unflat(t, n) for t in out4)
    sm_g = _pack([small_g[n] for n in SMALL])
    sm = [_pack([d[n] for n in SMALL]) for d in (w, m, v)]
    sm_out = _adamw(sm[0], sm_g, sm[1], sm[2], "adamw_small")
    sm_shapes = [w[n].shape for n in SMALL]
    unpacked = [_unpack(t, sm_shapes) for t in (sm_g,) + tuple(sm_out)]
    for i, n in enumerate(SMALL):
        res[n] = tuple(u[i] for u in unpacked)

    outs = [loss, d_x[None]]
    for k in range(4):
        outs += [res[n][k] for n in NAMES]
    return tuple(outs)
```

```python
import numpy as np
import jax
import jax.numpy as jnp
from jax import lax
from jax.experimental import pallas as pl
from jax.experimental.pallas import tpu as pltpu

F32, BF16 = jnp.float32, jnp.bfloat16
S, D, HD, NH, NKV = 2048, 1024, 64, 16, 4
KVW = NKV * HD
WINDOW = 128
ROT = HD // 4
THETA = 500000.0
EPS = 1e-6
SCALE = HD ** -0.5
LANES = 128
NEG = -1e30
VMEM_LIMIT = 48 * 2 ** 20
ROWS = 256
ATT = 512
SWQ = 4
NCHIP = 4
ADAM_LR, ADAM_B1, ADAM_B2, ADAM_EPS, ADAM_WD, ADAM_STEP = 0.001, 0.9, 0.999, 1e-08, 0.01, 10
NT = (((1,), (1,)), ((), ()))
TN = (((0,), (0,)), ((), ()))
MESH = pl.DeviceIdType.MESH


def _params(n):
    return pltpu.CompilerParams(dimension_semantics=("arbitrary",) * n, vmem_limit_bytes=VMEM_LIMIT)


def _dot(a, b, dims=None):
    if dims is None:
        return jnp.dot(a, b, preferred_element_type=F32)
    return lax.dot_general(a, b, dims, preferred_element_type=F32)


def _dot_split(a, b, n):
    out, rest = None, a
    for _ in range(n):
        hi = rest.astype(BF16)
        term = _dot(hi, b)
        out = term if out is None else out + term
        rest = rest - hi.astype(F32)
    return out


def _seg_mat(w):
    e = (np.arange(w)[:, None] // HD == np.arange(LANES)[None, :]).astype(np.float32)
    return jnp.asarray(e, BF16)


def _spread(r, w):
    head = lax.broadcasted_iota(jnp.int32, (2 * LANES, w), 1) >> 6
    row = lax.broadcasted_iota(jnp.int32, (2 * LANES, w), 0)
    et2 = jnp.where(head == (row & (LANES - 1)), 1.0, 0.0).astype(BF16)
    hi = r.astype(BF16)
    lo = (r - hi.astype(F32)).astype(BF16)
    return _dot(jnp.concatenate([hi, lo], axis=1), et2)


def _head_rstd(x, e):
    ss = _dot_split(x * x, e, 2)
    return _spread(lax.rsqrt(ss * (1.0 / HD) + EPS), x.shape[1])


def _rope(x, c, a, b):
    w = x.shape[1]
    return x * c + pltpu.roll(x, w - ROT // 2, 1) * a + pltpu.roll(x, ROT // 2, 1) * b


def _rope_t(dy, c, a, b):
    w = dy.shape[1]
    return dy * c + pltpu.roll(dy * b, w - ROT // 2, 1) + pltpu.roll(dy * a, ROT // 2, 1)


def _sigmoid(x):
    return 1.0 / (1.0 + jnp.exp(-x))


def _row_spec(shape, ts):
    nd = len(shape)
    if shape[0] == S:
        return pl.BlockSpec((ts,) + tuple(shape[1:]), lambda i: (i,) + (0,) * (nd - 1))
    return pl.BlockSpec(tuple(shape), lambda i: (0,) * nd)


def _rows_call(body, name, ins, outs, ts=ROWS):
    return pl.pallas_call(
        body, name=name, grid=(S // ts,),
        in_specs=[_row_spec(a.shape, ts) for a in ins],
        out_specs=[_row_spec(s, ts) for s, _ in outs],
        out_shape=[jax.ShapeDtypeStruct(s, d) for s, d in outs],
        compiler_params=_params(1))(*ins)


def _col_spec(ts, w, col):
    return pl.BlockSpec((ts, w), lambda i: (i, col))


TM = TN_ = 512
TM_TOKENS = 1024


def _mm(name, m, n, terms, out_dtype=F32, add=None, tm=None, tn=TN_, stacked=False, riding=None, rows_of=None):
    nterm = len(terms)
    if tm is None:
        tm = TM_TOKENS if m == S else TM
    nj, ni_ = n // tn, m // tm
    n_in = 2 * nterm + (add is not None) + (rows_of is not None and rows_of[0] is not None)
    r_in, r_out = (len(riding.ins), len(riding.outs)) if riding is not None else (0, 0)

    def body(*refs):
        if riding is not None:
            j, i = pl.program_id(0), pl.program_id(1)
            at_end = riding.hooks(refs[n_in:n_in + r_in], refs[n_in + r_in + 1:n_in + r_in + 1 + r_out],
                                  *refs[n_in + r_in + 1 + r_out:], first=(j == 0) & (i == 0),
                                  middle=(j == nj // 2) & (i == 0), last=(j == nj - 1) & (i == ni_ - 1))
        acc = None
        for t in range(nterm):
            part = _dot(refs[2 * t][...], refs[2 * t + 1][...], terms[t][4])
            acc = part if acc is None else acc + part
        if add is not None:
            acc = acc + refs[2 * nterm][...]
        refs[n_in + r_in][...] = acc.astype(out_dtype)
        if riding is not None:
            at_end()

    tile = pl.BlockSpec((tm, tn), lambda j, i: (i, j))
    ins, specs = [], []
    for a, a_spec, b, b_spec, _ in terms:
        ins += [a, b]
        specs += [a_spec, b_spec]
    if add is not None:
        ins.append(add)
        specs.append(tile)
    out_spec = pl.BlockSpec((None, tm, tn), lambda j, i: (j, i, 0)) if stacked else tile
    out_shape = jax.ShapeDtypeStruct((nj, m, tn) if stacked else (m, n), out_dtype)
    if rows_of is not None:
        taller, rows, row0 = rows_of
        out_spec = pl.BlockSpec((pl.Element(tm), pl.Element(tn)), lambda j, i: (
            pl.multiple_of(row0 + i * tm, 8), pl.multiple_of(j * tn, LANES)))
        out_shape = jax.ShapeDtypeStruct((rows, n), out_dtype)
        alias = {}
        if taller is not None:
            ins.append(taller)
            specs.append(pl.BlockSpec(memory_space=pltpu.HBM))
            alias = {len(ins) - 1: 0}
        return pl.pallas_call(body, name=name, grid=(nj, ni_), in_specs=specs, out_specs=out_spec,
                              out_shape=out_shape, input_output_aliases=alias, compiler_params=_params(2))(*ins)
    if riding is None:
        return pl.pallas_call(body, name=name, grid=(nj, ni_), in_specs=specs, out_specs=out_spec,
                              out_shape=out_shape, compiler_params=_params(2))(*ins)
    res = pl.pallas_call(
        body, name=name, grid=(nj, ni_), in_specs=specs + riding.in_specs,
        out_specs=[out_spec] + riding.out_specs, out_shape=[out_shape] + riding.out_shape,
        scratch_shapes=riding.scratch, compiler_params=_params(2))(*ins, *riding.ins)
    return res[0], res[1:]


def _a_rows(k, col=0, tm=TM_TOKENS):
    return pl.BlockSpec((tm, k), lambda j, i: (i, col))


def _a_cols(k, tm=TM):
    return pl.BlockSpec((k, tm), lambda j, i: (0, i))


def _b_cols(k, row=0, col0=0, tn=TN_):
    return pl.BlockSpec((k, tn), lambda j, i: (row, col0 + j))


def _b_rows(k, row0=0, tn=TN_):
    return pl.BlockSpec((tn, k), lambda j, i: (row0 + j, 0))


def _rmsnorm_fwd(x, gains, name):
    def body(*refs):
        xv = refs[0][...]
        r = lax.rsqrt(jnp.mean(xv * xv, axis=-1, keepdims=True) + EPS)
        xh = xv * r
        for n in range(len(gains)):
            refs[1 + len(gains) + n][...] = (xh * refs[1 + n][...]).astype(BF16)

    return _rows_call(body, name, [x] + list(gains), [((S, D), BF16)] * len(gains))


def _rmsnorm_bwd(x, gains, dus, dres, name):
    n = len(gains)

    def body(*refs):
        x_ref, g_refs, du_refs, dres_ref = refs[0], refs[1:1 + n], refs[1 + n:1 + 2 * n], refs[1 + 2 * n]
        dx_ref, dxb_ref, dg_refs = refs[2 + 2 * n], refs[3 + 2 * n], refs[4 + 2 * n:]
        xv = x_ref[...]
        r = lax.rsqrt(jnp.mean(xv * xv, axis=-1, keepdims=True) + EPS)
        xh = xv * r
        gy = None
        for m in range(n):
            du = du_refs[m][...]
            part = jnp.sum(du * xh, axis=0, keepdims=True)

            @pl.when(pl.program_id(0) == 0)
            def _(m=m, part=part):
                dg_refs[m][...] = part

            @pl.when(pl.program_id(0) != 0)
            def _(m=m, part=part):
                dg_refs[m][...] += part

            t = du * g_refs[m][...]
            gy = t if gy is None else gy + t
        dx = dres_ref[...] + r * (gy - xh * jnp.mean(gy * xh, axis=-1, keepdims=True))
        dx_ref[...] = dx
        dxb_ref[...] = dx.astype(BF16)

    outs = [((S, D), F32), ((S, D), BF16)] + [((1, D), F32)] * n
    return _rows_call(body, name, [x] + list(gains) + list(dus) + [dres], outs)


def _a_post(qkvg, qg, kg):
    e = _seg_mat(D)

    def body(q_ref, k_ref, v_ref, qg_ref, kg_ref, e_ref, qo, ko, vo):
        ev = e_ref[...]
        qv, kv = q_ref[...], k_ref[...]
        qo[...] = (qv * _head_rstd(qv, ev) * qg_ref[...] * SCALE).astype(BF16)
        ko[...] = (kv * _head_rstd(kv, ev) * kg_ref[...]).astype(BF16)
        vo[...] = v_ref[...].astype(BF16)

    whole = lambda a: pl.BlockSpec(a.shape, lambda i: (0, 0))
    return pl.pallas_call(
        body, name="a_post", grid=(S // ROWS,),
        in_specs=[_col_spec(ROWS, D, 0), _col_spec(ROWS, D, 1), _col_spec(ROWS, D, 2),
                  whole(qg), whole(kg), whole(e)],
        out_specs=[_col_spec(ROWS, D, 0)] * 3,
        out_shape=[jax.ShapeDtypeStruct((S, D), BF16)] * 3,
        compiler_params=_params(1))(qkvg, qkvg, qkvg, qg, kg, e)


def _tri(upper):
    r, c = np.arange(ROWS)[:, None], np.arange(ROWS)[None, :]
    return jnp.asarray((r <= c) if upper else (r >= c), BF16)


def _forget_cumsum(fpad, bpad):
    def body(f_ref, b_ref, u_ref, c_ref, carry):
        @pl.when(pl.program_id(0) == 0)
        def _():
            carry[...] = jnp.zeros_like(carry)

        lf = jax.nn.log_sigmoid(f_ref[...] + b_ref[...])
        blk = _dot_split(lf.T, u_ref[...], 3) + carry[:, 0:1]
        c_ref[...] = blk
        carry[...] = jnp.broadcast_to(blk[:, ROWS - 1:ROWS], carry.shape)

    return pl.pallas_call(
        body, name="forget_cumsum", grid=(S // ROWS,),
        in_specs=[pl.BlockSpec((ROWS, LANES), lambda i: (i, 0)), pl.BlockSpec((1, LANES), lambda i: (0, 0)),
                  pl.BlockSpec((ROWS, ROWS), lambda i: (0, 0))],
        out_specs=pl.BlockSpec((LANES, ROWS), lambda i: (0, i)),
        out_shape=jax.ShapeDtypeStruct((LANES, S), F32),
        scratch_shapes=[pltpu.VMEM((LANES, LANES), F32)],
        compiler_params=_params(1))(fpad, bpad, _tri(True))


def _forget_bwd(dct, fpad, bpad):
    nb = S // ROWS

    def body(dc_ref, f_ref, b_ref, l_ref, df_ref, db_ref, carry):
        @pl.when(pl.program_id(0) == 0)
        def _():
            carry[...] = jnp.zeros_like(carry)
            db_ref[...] = jnp.zeros_like(db_ref)

        blk = _dot_split(dc_ref[...], l_ref[...], 3) + carry[:, 0:1]
        carry[...] = jnp.broadcast_to(blk[:, 0:1], carry.shape)
        df = blk.T * _sigmoid(-(f_ref[...] + b_ref[...]))
        df_ref[...] = df.astype(BF16)
        db_ref[...] += jnp.sum(df, axis=0, keepdims=True)

    return pl.pallas_call(
        body, name="forget_bwd", grid=(nb,),
        in_specs=[pl.BlockSpec((LANES, ROWS), lambda i: (0, nb - 1 - i)),
                  pl.BlockSpec((ROWS, LANES), lambda i: (nb - 1 - i, 0)),
                  pl.BlockSpec((1, LANES), lambda i: (0, 0)), pl.BlockSpec((ROWS, ROWS), lambda i: (0, 0))],
        out_specs=[pl.BlockSpec((ROWS, LANES), lambda i: (nb - 1 - i, 0)), pl.BlockSpec((1, LANES), lambda i: (0, 0))],
        out_shape=[jax.ShapeDtypeStruct((S, LANES), BF16), jax.ShapeDtypeStruct((1, LANES), F32)],
        scratch_shapes=[pltpu.VMEM((LANES, LANES), F32)],
        compiler_params=_params(1))(dct, fpad, bpad, _tri(False))


def _headnorm_bwd(x, col, gain, dy, rope, name):
    e = _seg_mat(D)
    tabs = list(rope) if rope is not None else []

    def body(*refs):
        x_ref, g_ref, dy_ref, e_ref = refs[:4]
        dx_ref, dg_ref = refs[-2:]
        xv, dyv, ev = x_ref[...], dy_ref[...], e_ref[...]
        if rope is not None:
            c, a, b = (jnp.tile(t[...], (1, D // LANES)) for t in refs[4:7])
            dyv = _rope_t(dyv, c, a, b)
        r = _head_rstd(xv, ev)
        xh = xv * r
        part = jnp.sum(dyv * xh, axis=0, keepdims=True)

        @pl.when(pl.program_id(0) == 0)
        def _():
            dg_ref[...] = part

        @pl.when(pl.program_id(0) != 0)
        def _():
            dg_ref[...] += part

        gy = dyv * g_ref[...]
        seg = _spread(_dot_split(gy * xh, ev, 2) * (1.0 / HD), D)
        dx_ref[...] = (r * (gy - xh * seg)).astype(BF16)

    whole = lambda a: pl.BlockSpec(a.shape, lambda i: (0, 0))
    return pl.pallas_call(
        body, name=name, grid=(S // ROWS,),
        in_specs=[_col_spec(ROWS, D, col), whole(gain), _col_spec(ROWS, D, 0), whole(e)]
                 + [pl.BlockSpec((ROWS, LANES), lambda i: (i, 0))] * len(tabs),
        out_specs=[_col_spec(ROWS, D, 0), whole(gain)],
        out_shape=[jax.ShapeDtypeStruct((S, D), BF16), jax.ShapeDtypeStruct((1, D), F32)],
        compiler_params=_params(1))(x, gain, dy, e, *tabs)


def _dup_mat():
    r, c = np.arange(KVW)[:, None], np.arange(2 * KVW)[None, :]
    return (r // HD == c // LANES) & (r % HD == c % HD)


def _fold_mat():
    r, c = np.arange(D)[:, None], np.arange(KVW)[None, :]
    return (r // (2 * LANES) == c // HD) & (r % HD == c % HD)


def _b_post(pb, kv, qg, kg, rope):
    e, ek = _seg_mat(D), _seg_mat(KVW)
    dup = jnp.asarray(_dup_mat(), BF16)

    def body(q_ref, k_ref, v_ref, qg_ref, kg_ref, e_ref, ek_ref, dup_ref, c_ref, a_ref, b_ref, qo, ko, vo):
        c1, a1, b1 = c_ref[...], a_ref[...], b_ref[...]
        qv = q_ref[...]
        qn = qv * _head_rstd(qv, e_ref[...]) * qg_ref[...]
        t = lambda z, n: jnp.tile(z, (1, n))
        qo[...] = (_rope(qn, t(c1, D // LANES), t(a1, D // LANES), t(b1, D // LANES)) * SCALE).astype(BF16)
        kvv = k_ref[...]
        kn = kvv * _head_rstd(kvv, ek_ref[...]) * kg_ref[...]
        kr = _rope(kn, t(c1, KVW // LANES), t(a1, KVW // LANES), t(b1, KVW // LANES)).astype(BF16)
        ko[...] = _dot(kr, dup_ref[...]).astype(BF16)
        vo[...] = _dot(v_ref[...].astype(BF16), dup_ref[...]).astype(BF16)

    whole = lambda a: pl.BlockSpec(a.shape, lambda i: (0, 0))
    tab = pl.BlockSpec((ROWS, LANES), lambda i: (i, 0))
    return pl.pallas_call(
        body, name="b_post", grid=(S // ROWS,),
        in_specs=[_col_spec(ROWS, D, 0), _col_spec(ROWS, KVW, 0), _col_spec(ROWS, KVW, 1),
                  whole(qg), whole(kg), whole(e), whole(ek), whole(dup), tab, tab, tab],
        out_specs=[_col_spec(ROWS, D, 0), _col_spec(ROWS, 2 * KVW, 0), _col_spec(ROWS, 2 * KVW, 0)],
        out_shape=[jax.ShapeDtypeStruct((S, D), BF16), jax.ShapeDtypeStruct((S, 2 * KVW), BF16),
                   jax.ShapeDtypeStruct((S, 2 * KVW), BF16)],
        compiler_params=_params(1))(pb, kv, kv, qg, kg, e, ek, dup, *rope)


def _kv_bwd(dkdup, dvdup, kv, kg, rope):
    ek = _seg_mat(KVW)
    fold = jnp.asarray(_fold_mat(), BF16)

    def body(dk_ref, dv_ref, k_ref, kg_ref, ek_ref, fold_ref, c_ref, a_ref, b_ref, dkv_ref, dg_ref):
        ev, fv = ek_ref[...], fold_ref[...]
        t = lambda z: jnp.tile(z[...], (1, KVW // LANES))
        dk = _rope_t(_dot_split(dk_ref[...], fv, 2), t(c_ref), t(a_ref), t(b_ref))
        dv = _dot_split(dv_ref[...], fv, 2)
        xv = k_ref[...]
        r = _head_rstd(xv, ev)
        xh = xv * r
        part = jnp.sum(dk * xh, axis=0, keepdims=True)

        @pl.when(pl.program_id(0) == 0)
        def _():
            dg_ref[...] = part

        @pl.when(pl.program_id(0) != 0)
        def _():
            dg_ref[...] += part

        gy = dk * kg_ref[...]
        seg = _spread(_dot_split(gy * xh, ev, 2) * (1.0 / HD), KVW)
        dkv_ref[:, 0:KVW] = (r * (gy - xh * seg)).astype(BF16)
        dkv_ref[:, KVW:2 * KVW] = dv.astype(BF16)

    whole = lambda a: pl.BlockSpec(a.shape, lambda i: (0, 0))
    tab = pl.BlockSpec((ROWS, LANES), lambda i: (i, 0))
    return pl.pallas_call(
        body, name="kv_bwd", grid=(S // ROWS,),
        in_specs=[_col_spec(ROWS, D, 0), _col_spec(ROWS, D, 0), _col_spec(ROWS, KVW, 0),
                  whole(kg), whole(ek), whole(fold), tab, tab, tab],
        out_specs=[_col_spec(ROWS, 2 * KVW, 0), whole(kg)],
        out_shape=[jax.ShapeDtypeStruct((S, 2 * KVW), BF16), jax.ShapeDtypeStruct((1, KVW), F32)],
        compiler_params=_params(1))(dkdup, dvdup, kv, kg, ek, fold, *rope)


def _loss_head(out, target):
    def body(o_ref, t_ref, d_ref, db_ref, l_ref):
        diff = o_ref[...] - t_ref[...]
        d = diff * (1.0 / D)
        d_ref[...] = d
        db_ref[...] = d.astype(BF16)

        @pl.when(pl.program_id(0) == 0)
        def _():
            l_ref[...] = jnp.zeros_like(l_ref)

        l_ref[...] += jnp.sum(diff * diff, axis=0, keepdims=True)

    return _rows_call(body, "loss_head", [out, target], [((S, D), F32), ((S, D), BF16), ((1, D), F32)])


def _lane():
    return lax.broadcasted_iota(jnp.int32, (1, LANES), 1)


def _head_mask(hh):
    return (_lane() < HD) if hh == 0 else (_lane() >= HD)


def _fox_fwd(q, k, v, ct, gate, riding):
    nq, npair = S // ATT, NH // 2
    ni, no = len(riding.ins), len(riding.outs)

    def body(q_ref, k_ref, v_ref, c_ref, gate_ref, *rest):
        o_ref, lse_ref, y_ref = rest[ni:ni + 3]
        pair, i = pl.program_id(0), pl.program_id(1)
        at_end = riding.hooks(rest[:ni], rest[ni + 3:ni + 3 + no], *rest[ni + 3 + no:],
                              first=(pair == 0) & (i == 0), middle=(pair == npair // 2) & (i == 0),
                              last=(pair == npair - 1) & (i == nq - 1))
        q2 = q_ref[...]
        qms = [jnp.where(_head_mask(hh), q2, jnp.zeros_like(q2)) for hh in (0, 1)]

        def probs(off, width, m, hh, diag):
            s = _dot(qms[hh], k_ref[pl.ds(off, width), :], NT) - c_ref[hh:hh + 1, pl.ds(off, width)]
            if diag:
                row = i * ATT + lax.broadcasted_iota(jnp.int32, (ATT, width), 0)
                col = off + lax.broadcasted_iota(jnp.int32, (ATT, width), 1)
                s = jnp.where(col <= row, s, NEG)
            m_new = jnp.maximum(m, jnp.max(s, axis=1, keepdims=True))
            p = jnp.exp(s - m_new)
            p_hi = p.astype(BF16)
            return m_new, jnp.exp(m - m_new), p_hi, (p - p_hi.astype(F32)).astype(BF16)

        def weighted(off, width, p_hi, p_lo, hh):
            vj = v_ref[pl.ds(off, width), :]
            v1 = jnp.where(_head_mask(hh), vj, jnp.ones_like(vj))
            return _dot(p_hi, v1) + _dot(p_lo, v1)

        def step(off, width, carry, diag):
            off = pl.multiple_of(off, ATT)
            out = []
            for hh in (0, 1):
                m, acc = carry[hh]
                m, alpha, p_hi, p_lo = probs(off, width, m, hh, diag)
                out.append((m, alpha * acc + weighted(off, width, p_hi, p_lo, hh)))
            return tuple(out)

        one = (jnp.full((ATT, 1), NEG, F32), jnp.zeros((ATT, LANES), F32))
        carry = lax.fori_loop(0, i // 2, lambda j, cr: step(j * (2 * ATT), 2 * ATT, cr, False), (one, one))
        carry = lax.cond(i % 2 == 1, lambda cr: step((i - 1) * ATT, 2 * ATT, cr, True),
                         lambda cr: step(i * ATT, ATT, cr, True), carry)
        res = []
        for hh in (0, 1):
            m, acc = carry[hh]
            l = jnp.max(jnp.where(_head_mask(1 - hh), acc, 0.0), axis=1, keepdims=True)
            res.append((acc / l, m + jnp.log(l)))
        first = _head_mask(0)
        o = jnp.where(first, res[0][0], res[1][0])
        o_ref[...] = o
        lse_ref[...] = jnp.where(first, res[0][1], res[1][1])
        g = gate_ref[...]
        y_ref[...] = (o * (g * _sigmoid(g))).astype(BF16)
        at_end()

    blk = pl.BlockSpec((ATT, LANES), lambda p, i: (i, p))
    full = pl.BlockSpec((S, LANES), lambda p, i: (0, p))
    res = pl.pallas_call(
        body, name="fox_fwd", grid=(npair, nq),
        in_specs=[blk, full, full, pl.BlockSpec((None, 2, S), lambda p, i: (p, 0, 0)), blk] + riding.in_specs,
        out_specs=[blk, blk, blk] + riding.out_specs,
        out_shape=[jax.ShapeDtypeStruct((S, D), F32)] * 2 + [jax.ShapeDtypeStruct((S, D), BF16)] + riding.out_shape,
        scratch_shapes=riding.scratch,
        compiler_params=_params(2))(q, k, v, ct, gate, *riding.ins)
    return res[0], res[1], res[2], res[3:]


def _gate_grads(dy, o, g):
    sg = _sigmoid(g)
    return dy * (g * sg), dy * o * (sg * (1.0 + g * (1.0 - sg)))


def _fox_bwd(q, k, v, ct, o, lse, dy, gate, riding):
    nq, npair = S // ATT, NH // 2
    ni, no = len(riding.ins), len(riding.outs)

    def body(q_ref, k_ref, v_ref, c_ref, o_ref, lse_ref, dy_ref, gate_ref, *rest):
        dq_ref, dk_ref, dvb_ref, dc_ref, dgate_ref = rest[ni:ni + 5]
        dv_ref = rest[ni + 5 + no]
        pair, i = pl.program_id(0), pl.program_id(1)
        at_end = riding.hooks(rest[:ni], rest[ni + 5:ni + 5 + no], *rest[ni + 6 + no:],
                              first=(pair == 0) & (i == 0), middle=(pair == npair // 2) & (i == 0),
                              last=(pair == npair - 1) & (i == nq - 1))

        @pl.when(i == 0)
        def _():
            dk_ref[...] = jnp.zeros_like(dk_ref)
            dv_ref[...] = jnp.zeros_like(dv_ref)
            dc_ref[...] = jnp.zeros_like(dc_ref)

        q2, lse2 = q_ref[...], lse_ref[...]
        do2, dgate = _gate_grads(dy_ref[...], o_ref[...], gate_ref[...])
        dgate_ref[...] = dgate.astype(BF16)
        do2b = do2.astype(BF16)
        prod = do2b.astype(F32) * o_ref[...]
        heads = []
        for hh in (0, 1):
            hm = _head_mask(hh)
            heads.append((jnp.where(hm, q2, jnp.zeros_like(q2)), jnp.where(hm, do2b, jnp.zeros_like(do2b)),
                          jnp.sum(jnp.where(hm, prod, 0.0), axis=1, keepdims=True),
                          jnp.max(jnp.where(hm, lse2, NEG), axis=1, keepdims=True)))

        def step(off, width, dqs, diag):
            off = pl.multiple_of(off, ATT)
            kj, vj = k_ref[pl.ds(off, width), :], v_ref[pl.ds(off, width), :]
            dk, dv, out = None, None, []
            for hh in (0, 1):
                qm, dom, delta, lse_h = heads[hh]
                s = _dot(qm, kj, NT) - c_ref[hh:hh + 1, pl.ds(off, width)]
                p = jnp.exp(s - lse_h)
                if diag:
                    row = i * ATT + lax.broadcasted_iota(jnp.int32, (ATT, width), 0)
                    col = off + lax.broadcasted_iota(jnp.int32, (ATT, width), 1)
                    p = jnp.where(col <= row, p, 0.0)
                ds = p * (_dot(dom, vj, NT) - delta)
                dc_ref[hh:hh + 1, pl.ds(off, width)] += -jnp.sum(ds, axis=0, keepdims=True)
                dsb = ds.astype(BF16)
                dk_h, dv_h = _dot(dsb, qm, TN), _dot(p.astype(BF16), dom, TN)
                dk, dv = (dk_h, dv_h) if dk is None else (dk + dk_h, dv + dv_h)
                out.append(dqs[hh] + _dot(dsb, kj))
            dk_ref[pl.ds(off, width), :] += dk
            dv_ref[pl.ds(off, width), :] += dv
            return tuple(out)

        zero = jnp.zeros((ATT, LANES), F32)
        dqs = lax.fori_loop(0, i // 2, lambda j, acc: step(j * (2 * ATT), 2 * ATT, acc, False), (zero, zero))
        dqs = lax.cond(i % 2 == 1, lambda acc: step((i - 1) * ATT, 2 * ATT, acc, True),
                       lambda acc: step(i * ATT, ATT, acc, True), dqs)
        dq_ref[...] = jnp.where(_head_mask(0), dqs[0], dqs[1]) * SCALE

        @pl.when(i == nq - 1)
        def _():
            dvb_ref[...] = dv_ref[...].astype(BF16)

        at_end()

    blk = pl.BlockSpec((ATT, LANES), lambda p, i: (i, p))
    full = pl.BlockSpec((S, LANES), lambda p, i: (0, p))
    cspec = pl.BlockSpec((None, 2, S), lambda p, i: (p, 0, 0))
    res = pl.pallas_call(
        body, name="fox_bwd", grid=(npair, nq),
        in_specs=[blk, full, full, cspec, blk, blk, blk, blk] + riding.in_specs,
        out_specs=[blk, full, full, cspec, blk] + riding.out_specs,
        out_shape=[jax.ShapeDtypeStruct((S, D), F32)] * 2 + [jax.ShapeDtypeStruct((S, D), BF16),
                                                              jax.ShapeDtypeStruct((npair, 2, S), F32),
                                                              jax.ShapeDtypeStruct((S, D), BF16)]
                  + riding.out_shape,
        scratch_shapes=[pltpu.VMEM((S, LANES), F32)] + riding.scratch,
        compiler_params=_params(2))(q, k, v, ct, o, lse, dy, gate, *riding.ins)
    return res[0], res[1], res[2], res[3], res[4], res[5:]


def _both_heads(x):
    return jnp.concatenate([jnp.where(_head_mask(hh), x, jnp.zeros_like(x)) for hh in (0, 1)], axis=0)


def _per_head(col0, col1):
    return jnp.concatenate([jnp.broadcast_to(col0, (WINDOW, 1)), jnp.broadcast_to(col1, (WINDOW, 1))], axis=0)


def _unstack(x2):
    return jnp.where(_head_mask(0), x2[:WINDOW], x2[WINDOW:])


def _swa_valid(i, start):
    r = lax.broadcasted_iota(jnp.int32, (2 * WINDOW, 2 * WINDOW), 0)
    qabs = i * WINDOW + jnp.where(r >= WINDOW, r - WINDOW, r)
    kabs = start + lax.broadcasted_iota(jnp.int32, (2 * WINDOW, 2 * WINDOW), 1)
    return (kabs <= qabs) & (qabs - kabs < WINDOW)


def _swa_fwd(q, kdup, vdup, sinks_t, proj, gate_col):
    def body(q_ref, k_ref, v_ref, sk_ref, gate_ref, o_ref, lse_ref, y_ref):
        skv = sk_ref[...]
        first = _head_mask(0)
        for sb in range(SWQ):
            i = pl.program_id(1) * SWQ + sb
            rows = slice(sb * WINDOW, (sb + 1) * WINDOW)
            start = pl.multiple_of(jnp.maximum(i - 1, 0) * WINDOW, WINDOW)
            kk, vv = k_ref[pl.ds(start, 2 * WINDOW), :], v_ref[pl.ds(start, 2 * WINDOW), :]
            q2 = q_ref[rows, :]
            valid = _swa_valid(i, start)[:WINDOW]
            res = []
            for hh in (0, 1):
                hm = _head_mask(hh)
                sink = jnp.max(jnp.where(hm, skv, NEG), axis=1, keepdims=True)
                s = jnp.where(valid, _dot(jnp.where(hm, q2, jnp.zeros_like(q2)), kk, NT), NEG)
                m = jnp.maximum(jnp.max(s, axis=1, keepdims=True), sink)
                p = jnp.exp(s - m)
                l = jnp.sum(p, axis=1, keepdims=True) + jnp.exp(sink - m)
                res.append((_dot(p.astype(BF16), vv) / l, m + jnp.log(l)))
            o = jnp.where(first, res[0][0], res[1][0])
            o_ref[rows, :] = o
            lse_ref[rows, :] = jnp.where(first, res[0][1], res[1][1])
            g = gate_ref[rows, :]
            y_ref[rows, :] = (o * (g * _sigmoid(g))).astype(BF16)

    blk = pl.BlockSpec((SWQ * WINDOW, LANES), lambda p, i: (i, p))
    gate = pl.BlockSpec((SWQ * WINDOW, LANES), lambda p, i: (i, gate_col + p))
    full = pl.BlockSpec((S, LANES), lambda p, i: (0, p // 2))
    return pl.pallas_call(
        body, name="swa_fwd", grid=(NH // 2, S // (SWQ * WINDOW)),
        in_specs=[blk, full, full, pl.BlockSpec((1, LANES), lambda p, i: (0, p)), gate],
        out_specs=[blk, blk, blk],
        out_shape=[jax.ShapeDtypeStruct((S, D), F32)] * 2 + [jax.ShapeDtypeStruct((S, D), BF16)],
        compiler_params=_params(2))(q, kdup, vdup, sinks_t, proj)


def _swa_bwd(q, kdup, vdup, sinks_t, o, lse, dy, proj, gate_col):
    def body(q_ref, k_ref, v_ref, sk_ref, o_ref, lse_ref, dy_ref, gate_ref, dq_ref, dk_ref, dv_ref, dsk_ref,
             dgate_ref):
        @pl.when(pl.program_id(1) == 0)
        def _():
            dk_ref[...] = jnp.zeros_like(dk_ref)
            dv_ref[...] = jnp.zeros_like(dv_ref)
            dsk_ref[...] = jnp.zeros_like(dsk_ref)

        skv = sk_ref[...]
        first = _head_mask(0)
        sink = _per_head(*[jnp.max(jnp.where(_head_mask(hh), skv, NEG), axis=1, keepdims=True) for hh in (0, 1)])
        for sb in range(SWQ):
            i = pl.program_id(1) * SWQ + sb
            rows = slice(sb * WINDOW, (sb + 1) * WINDOW)
            start = pl.multiple_of(jnp.maximum(i - 1, 0) * WINDOW, WINDOW)
            kk, vv = k_ref[pl.ds(start, 2 * WINDOW), :], v_ref[pl.ds(start, 2 * WINDOW), :]
            do2, dgate = _gate_grads(dy_ref[rows, :], o_ref[rows, :], gate_ref[rows, :])
            dgate_ref[rows, :] = dgate.astype(BF16)
            do2b = do2.astype(BF16)
            prod, lse2 = do2b.astype(F32) * o_ref[rows, :], lse_ref[rows, :]
            qs, dos = _both_heads(q_ref[rows, :]), _both_heads(do2b)
            delta = jnp.concatenate([jnp.sum(jnp.where(_head_mask(hh), prod, 0.0), axis=1, keepdims=True)
                                     for hh in (0, 1)], axis=0)
            lse_h = jnp.concatenate([jnp.max(jnp.where(_head_mask(hh), lse2, NEG), axis=1, keepdims=True)
                                     for hh in (0, 1)], axis=0)
            p = jnp.where(_swa_valid(i, start), jnp.exp(_dot(qs, kk, NT) - lse_h), 0.0)
            dsb = (p * (_dot(dos, vv, NT) - delta)).astype(BF16)
            dk_ref[pl.ds(start, 2 * WINDOW), :] += _dot(dsb, qs, TN)
            dv_ref[pl.ds(start, 2 * WINDOW), :] += _dot(p.astype(BF16), dos, TN)
            dq_ref[rows, :] = _unstack(_dot(dsb, kk)) * SCALE
            t = jnp.exp(sink - lse_h) * delta
            dsk_ref[...] += -jnp.where(first, jnp.sum(t[:WINDOW], axis=0, keepdims=True),
                                       jnp.sum(t[WINDOW:], axis=0, keepdims=True))

    blk = pl.BlockSpec((SWQ * WINDOW, LANES), lambda p, i: (i, p))
    full = pl.BlockSpec((S, LANES), lambda p, i: (0, p // 2))
    acc = pl.BlockSpec((S, LANES), lambda p, i: (0, p))
    sk = pl.BlockSpec((1, LANES), lambda p, i: (0, p))
    gate = pl.BlockSpec((SWQ * WINDOW, LANES), lambda p, i: (i, gate_col + p))
    return pl.pallas_call(
        body, name="swa_bwd", grid=(NH // 2, S // (SWQ * WINDOW)),
        in_specs=[blk, full, full, sk, blk, blk, blk, gate],
        out_specs=[blk, acc, acc, sk, blk],
        out_shape=[jax.ShapeDtypeStruct((S, D), F32)] * 3 + [jax.ShapeDtypeStruct((1, D), F32),
                                                              jax.ShapeDtypeStruct((S, D), BF16)],
        compiler_params=_params(2))(q, kdup, vdup, sinks_t, o, lse, dy, proj)


def _adamw_math(w, g, m, v):
    m = ADAM_B1 * m + (1.0 - ADAM_B1) * g
    v = ADAM_B2 * v + (1.0 - ADAM_B2) * jnp.square(g)
    m_hat = m / (1.0 - ADAM_B1 ** ADAM_STEP)
    v_hat = v / (1.0 - ADAM_B2 ** ADAM_STEP)
    delta = -ADAM_LR * (m_hat / (jnp.sqrt(v_hat) + ADAM_EPS) + ADAM_WD * w)
    return delta, m, v


def _adamw(w, g, m, v, name):
    r, c = w.shape
    tr = min(r, 128)

    def body(w_ref, g_ref, m_ref, v_ref, d_ref, mo_ref, vo_ref):
        d_ref[...], mo_ref[...], vo_ref[...] = _adamw_math(w_ref[...], g_ref[...], m_ref[...], v_ref[...])

    spec = pl.BlockSpec((tr, c), lambda i: (i, 0))
    return pl.pallas_call(
        body, name=name, grid=(r // tr,), in_specs=[spec] * 4, out_specs=[spec] * 3,
        out_shape=[jax.ShapeDtypeStruct((r, c), F32)] * 3, compiler_params=_params(1))(w, g, m, v)


SUM_TILE = 128


FLAT_BLOCK = 257 * 1024


def _tiles(shape, axis, lead=0):
    if len(shape) == 1:
        count = shape[0] // FLAT_BLOCK
        return (FLAT_BLOCK,), count, lambda pos, *lead_idx: (sum(k * count for k in lead_idx) + pos,)
    r, c = shape
    blk = (SUM_TILE, c) if axis == 0 else (r, SUM_TILE)
    count = shape[axis] // SUM_TILE

    def index(pos, *lead_idx):
        return tuple(lead_idx) + ((pos, 0) if axis == 0 else (0, pos))

    return (None,) * lead + blk, count, index


def _adamw_halves(w, g_mine, g_theirs, m, v, axis, name):
    blk, count, index = _tiles(w.shape, axis)
    per_half = count // 2

    def body(w_ref, a_ref, b_ref, m_ref, v_ref, g_ref, d_ref, mo_ref, vo_ref):
        is_mine = pl.program_id(0) // per_half == lax.axis_index("c")
        g = jnp.where(is_mine, a_ref[...], b_ref[...])
        g_ref[...] = g
        d_ref[...], mo_ref[...], vo_ref[...] = _adamw_math(w_ref[...], g, m_ref[...], v_ref[...])

    spec = pl.BlockSpec(blk, lambda i: index(i))
    half = pl.BlockSpec(blk, lambda i: index(i % per_half))
    return pl.pallas_call(
        body, name=name, grid=(count,), in_specs=[spec, half, half, spec, spec], out_specs=[spec] * 4,
        out_shape=[jax.ShapeDtypeStruct(w.shape, F32)] * 4, compiler_params=_params(1))(w, g_mine, g_theirs, m, v)


def _chip_sum(blocks, from_sibling, axis, name):
    flat = blocks.ndim == 1
    blk, count, index = _tiles((from_sibling.shape[0] // NCHIP,) if flat else from_sibling.shape[1:], axis, lead=1)

    def body(core_ref, mine_ref, p_ref, o32, o16):
        acc = mine_ref[...] + p_ref[...]
        o32[...] = acc
        o16[...] = acc.astype(BF16)

    half = pl.BlockSpec(blk, lambda k, i, core: index(i, k))
    if flat:
        mine = pl.BlockSpec(blk, lambda k, i, core: (2 * count * k + core[0] * count + i,))
    else:
        mine = pl.BlockSpec(blk, lambda k, i, core: index(i + core[0] * count, k))
    return pl.pallas_call(
        body, name=name,
        grid_spec=pltpu.PrefetchScalarGridSpec(num_scalar_prefetch=1, grid=(NCHIP, count),
                                               in_specs=[mine, half], out_specs=[half, half]),
        out_shape=[jax.ShapeDtypeStruct(from_sibling.shape, F32), jax.ShapeDtypeStruct(from_sibling.shape, BF16)],
        compiler_params=_params(2))(lax.axis_index("c").astype(jnp.int32).reshape(1), blocks, from_sibling)


def _mesh_sum(own, parts, axis, name):
    blk, count, index = _tiles(own.shape, axis)
    n = NCHIP - 1

    def body(a_ref, *refs):
        acc = a_ref[...]
        for k in range(n):
            acc = acc + refs[k][...].astype(F32)
        refs[n][...] = acc

    spec = pl.BlockSpec(blk, lambda i: index(i))
    if own.ndim == 1:
        part = [pl.BlockSpec(blk, lambda i, k=k: (k * count + i,)) for k in range(n)]
    else:
        part = [pl.BlockSpec((None,) + blk, lambda i, k=k: (k,) + index(i)) for k in range(n)]
    return pl.pallas_call(
        body, name=name, grid=(count,), in_specs=[spec] + part,
        out_specs=spec, out_shape=jax.ShapeDtypeStruct(own.shape, F32),
        compiler_params=_params(1))(own, *([parts] * n))


def _sum_stack(parts, name):
    n = parts.shape[0]

    def body(p_ref, o_ref):
        acc = p_ref[0]
        for k in range(1, n):
            acc = acc + p_ref[k]
        o_ref[...] = acc

    return pl.pallas_call(body, name=name, out_shape=jax.ShapeDtypeStruct(parts.shape[1:], F32))(parts)


def _coords():
    return lax.axis_index("x"), lax.axis_index("y"), lax.axis_index("c")


def _chip(who):
    return 2 * who[0] + who[1]


def _flip(who, mask):
    return tuple((1 - v) if b else v for v, b in zip(who, mask))


def _transfer(transfers, t, I, O, ssem, rsem, receiving):
    tr, me = transfers[t], _coords()
    peer = _flip(me, tr["mask"])
    return pltpu.make_async_remote_copy(
        src_ref=tr["src"](I, O, me), dst_ref=tr["dst"](I, O, peer if receiving else me),
        send_sem=ssem.at[t], recv_sem=rsem.at[t], device_id=peer, device_id_type=MESH)


def _start_transfers(transfers, I, O, ssem, rsem, onward):
    arrived = set()
    for t, tr in enumerate(transfers):
        after = tr.get("after")
        if (after is not None) != onward:
            continue
        if after is not None and after not in arrived:
            _transfer(transfers, after, I, O, ssem, rsem, True).wait_recv()
            arrived.add(after)
        _transfer(transfers, t, I, O, ssem, rsem, False).start()


def _finish_transfers(transfers, I, O, ssem, rsem):
    passed_on = {tr["after"] for tr in transfers if tr.get("after") is not None}
    for t in range(len(transfers)):
        if t not in passed_on:
            _transfer(transfers, t, I, O, ssem, rsem, True).wait_recv()
    for t in range(len(transfers)):
        _transfer(transfers, t, I, O, ssem, rsem, False).wait_send()


def _own_copies(own, I, O, stage, lsem, leg):
    for n, (src, dst) in enumerate(own):
        me = _coords()
        bring =pltpu.make_async_copy(src(I, O, me), stage[n], lsem.at[2 * n])
        put = pltpu.make_async_copy(stage[n], dst(I, O, me), lsem.at[2 * n + 1])
        if leg == 0:
            bring.start()
        elif leg == 1:
            bring.wait()
            put.start()
        else:
            put.wait()


def _own_scratch(own, ins):
    return [pltpu.VMEM(ins[n].shape, ins[n].dtype) for n in range(len(own))], pltpu.SemaphoreType.DMA((max(2 * len(own), 1),))


def _exchange(name, ins, outs, transfers, own=()):
    ni, no = len(ins), len(outs)
    nt = len(transfers)
    stages, stage_sems = _own_scratch(own, ins)

    def body(*refs):
        I, O = refs[:ni], refs[ni:ni + no]
        ssem, rsem, lsem = refs[ni + no:ni + no + 3]
        stage = refs[ni + no + 3:]
        _own_copies(own, I, O, stage, lsem, 0)
        _start_transfers(transfers, I, O, ssem, rsem, False)
        _own_copies(own, I, O, stage, lsem, 1)
        _start_transfers(transfers, I, O, ssem, rsem, True)
        _finish_transfers(transfers, I, O, ssem, rsem)
        _own_copies(own, I, O, stage, lsem, 2)

    hbm = pl.BlockSpec(memory_space=pltpu.HBM)
    return pl.pallas_call(
        body, name=name, in_specs=[hbm] * ni, out_specs=[hbm] * no,
        out_shape=[jax.ShapeDtypeStruct(s, d) for s, d in outs],
        scratch_shapes=[pltpu.SemaphoreType.DMA((nt,)), pltpu.SemaphoreType.DMA((nt,)), stage_sems] + stages,
        compiler_params=pltpu.CompilerParams(has_side_effects=True, vmem_limit_bytes=VMEM_LIMIT))(*ins)


CHIP_MASKS = [(0, 1, 0), (1, 0, 0), (1, 1, 0)]
SIBLING = (0, 0, 1)


def _half(shape2d, axis, which):
    n = shape2d[axis] // 2
    cut = pl.ds(pl.multiple_of(which * n, n), n)
    return (cut, slice(None)) if axis == 0 else (slice(None), cut)


class _Riding:
    def __init__(self, transfers, ins, outs, own=()):
        self.transfers, self.ins, self.outs, self.own = transfers, list(ins), list(outs), list(own)
        hbm = pl.BlockSpec(memory_space=pltpu.HBM)
        self.in_specs, self.out_specs = [hbm] * len(self.ins), [hbm] * len(self.outs)
        self.out_shape = [jax.ShapeDtypeStruct(s, d) for s, d in self.outs]
        stages, stage_sems = _own_scratch(self.own, self.ins)
        self.scratch = [pltpu.SemaphoreType.DMA((max(len(transfers), 1),))] * 2 + [stage_sems] + stages

    def hooks(self, I, O, ssem, rsem, lsem, *stage, first, middle, last):
        tr, own = self.transfers, self.own

        @pl.when(first)
        def _():
            _own_copies(own, I, O, stage, lsem, 0)
            _start_transfers(tr, I, O, ssem, rsem, False)

        if own or any(t.get("after") is not None for t in tr):
            @pl.when(middle)
            def _():
                _own_copies(own, I, O, stage, lsem, 1)
                _start_transfers(tr, I, O, ssem, rsem, True)

        def at_end():
            @pl.when(last)
            def _():
                _finish_transfers(tr, I, O, ssem, rsem)
                _own_copies(own, I, O, stage, lsem, 2)

        return at_end


def _stretch(n, pos):
    return (pl.ds(pos * n if isinstance(pos, int) else pl.multiple_of(pos * n, n), n),)


def _gather_plan(shards, axes):
    def half(a, who):
        if shards[a].ndim == 1:
            return _stretch(shards[a].shape[0] // 2, who[2])
        return _half(shards[a].shape, axes[a], who[2])

    def landed(a, chip, who):
        if shards[a].ndim == 1:
            return _stretch(shards[a].shape[0] // 2, 2 * chip + who[2])
        return (chip,) + half(a, who)

    over_ici, onward = [], []
    for a in range(len(shards)):
        for mask in CHIP_MASKS:
            over_ici.append(dict(
                mask=mask,
                src=lambda I, O, me, a=a: I[a].at[half(a, me)],
                dst=lambda I, O, who, a=a: O[a].at[landed(a, _chip(who), who)]))
            onward.append(dict(
                mask=SIBLING, after=len(over_ici) - 1,
                src=lambda I, O, me, a=a, mask=mask: O[a].at[landed(a, _chip(_flip(me, mask)), me)],
                dst=lambda I, O, who, a=a, mask=mask: O[a].at[landed(a, _chip(_flip(who, mask)), who)]))
    outs = [((NCHIP * s.shape[0],) if s.ndim == 1 else (NCHIP,) + s.shape, s.dtype) for s in shards]

    def whole(a, chip):
        return _stretch(shards[a].shape[0], chip) if shards[a].ndim == 1 else (chip,)

    own = [(lambda I, O, me, a=a: I[a], lambda I, O, me, a=a: O[a].at[whole(a, _chip(me))])
           for a in range(len(shards))]
    return over_ici + onward, outs, own


def _gather_shards(shards, axes):
    transfers, outs, own = _gather_plan(shards, axes)
    return _exchange("gather_weights", shards, outs, transfers, own)


def _to_sibling(arrs, name):
    transfers = [dict(mask=SIBLING, src=lambda I, O, me, a=a: I[a], dst=lambda I, O, who, a=a: O[a])
                 for a in range(len(arrs))]
    return _exchange(name, arrs, [(t.shape, t.dtype) for t in arrs], transfers)


def _halves_to_sibling(blocks, axes, name):
    def cut(a, which):
        return (slice(None),) + _half(blocks[a].shape[1:], axes[a], which)

    transfers, outs = [], []
    for a, (b, ax) in enumerate(zip(blocks, axes)):
        if b.ndim == 1:
            h = b.shape[0] // NCHIP // 2
            for k in range(NCHIP):
                transfers.append(dict(mask=SIBLING,
                                      src=lambda I, O, me, a=a, k=k, h=h: I[a].at[_stretch(h, 2 * k + 1 - me[2])],
                                      dst=lambda I, O, who, a=a, k=k, h=h: O[a].at[_stretch(h, k)]))
            outs.append(((NCHIP * h,), b.dtype))
        else:
            transfers.append(dict(mask=SIBLING, src=lambda I, O, me, a=a: I[a].at[cut(a, 1 - me[2])],
                                  dst=lambda I, O, who, a=a: O[a]))
            shape = list(b.shape)
            shape[ax + 1] //= 2
            outs.append((tuple(shape), b.dtype))
    return _exchange(name, blocks, outs, transfers)


def _scatter_plan(tb):
    def slot(a, k):
        return (k,) if tb[a].ndim == 3 else _stretch(tb[a].shape[0] // NCHIP, k)

    transfers = []
    for a in range(len(tb)):
        for n, mask in enumerate(CHIP_MASKS):
            transfers.append(dict(
                mask=mask,
                src=lambda I, O, me, a=a, mask=mask: I[a].at[slot(a, _chip(_flip(me, mask)))],
                dst=lambda I, O, who, a=a, n=n: O[a].at[slot(a, n)]))
    outs = [((3,) + t.shape[1:] if t.ndim == 3 else (3 * (t.shape[0] // NCHIP),), t.dtype) for t in tb]
    return transfers, outs


def _scatter_chip_sums(tb):
    transfers, outs = _scatter_plan(tb)
    return _exchange("scatter_grads", tb, outs, transfers)


def _gather_small(vec):
    def slot(who):
        return 4 * who[0] + 2 * who[1] + who[2]

    masks = [(m >> 2 & 1, m >> 1 & 1, m & 1) for m in range(1, 8)]
    transfers = [dict(mask=mask, src=lambda I, O, me: I[0], dst=lambda I, O, who: O[0].at[slot(who)])
                 for mask in masks]
    own = [(lambda I, O, me: I[0], lambda I, O, me: O[0].at[slot(me)])]
    return _exchange("gather_small", [vec], [((8,) + vec.shape, vec.dtype)], transfers, own)[0]


def _rope_tables(positions):
    half = ROT // 2
    inv_freq = jnp.power(jnp.float32(THETA), -jnp.arange(0, ROT, 2, dtype=F32) / ROT)
    ang = positions.astype(F32)[:, None] * inv_freq[None, :]
    cos, sin = jnp.cos(ang), jnp.sin(ang)
    one, zero, z8 = jnp.ones((S, HD - ROT), F32), jnp.zeros((S, HD - ROT), F32), jnp.zeros((S, half), F32)
    c = jnp.concatenate([cos, cos, one], axis=1)
    a = jnp.concatenate([-sin, z8, zero], axis=1)
    b = jnp.concatenate([z8, sin, zero], axis=1)
    return tuple(jnp.tile(t, (1, 2)) for t in (c, a, b))


def _tile_heads(g, w):
    return jnp.tile(g.reshape(1, HD), (1, w // HD))


def _fold_heads(dg):
    return dg.reshape(-1, HD).sum(axis=0)


def _pad_lanes(a):
    return jnp.pad(a, ((0, 0), (0, LANES - a.shape[1])))


def _local_step(x, target, positions, wt, fetch, late_weights, begin_reduce):
    rope = _rope_tables(positions)
    w1t = wt["w_in_a_t"]
    f_row = 3 * D // LANES
    wg_t = w1t[3 * D + NH:]
    in_b_block = lambda c: pl.BlockSpec((None, TN_, TN_), lambda j, i: (c, j, 0))
    b_pad = _pad_lanes(wt["b_forget"].reshape(1, NH))
    qg_a, kg_a = _tile_heads(wt["qnorm_a_g"], D), _tile_heads(wt["knorm_a_g"], D)
    qg_b, kg_b = _tile_heads(wt["qnorm_b_g"], D), _tile_heads(wt["knorm_b_g"], KVW)
    norm_a, kv_g, norm_b = wt["norm_a_g"].reshape(1, D), wt["kv_norm_g"].reshape(1, D), wt["norm_b_g"].reshape(1, D)
    sinks_t = jnp.repeat(wt["sinks"].reshape(1, NH), HD, axis=1)

    (u_a,) = _rmsnorm_fwd(x, [norm_a], "norm_a")
    qkv = _mm("proj_a", S, 3 * D, [(u_a, _a_rows(D), w1t, _b_rows(D), NT)])
    fpad = _mm("proj_f", S, LANES, [(u_a, _a_rows(D), w1t, _b_rows(D, row0=f_row, tn=LANES), NT)], tn=LANES)
    gate_a = _mm("proj_gate_a", S, D, [(u_a, _a_rows(D), wg_t, _b_rows(D), NT)])
    q_a, k_a, v_a = _a_post(qkv, qg_a, kg_a)
    ct = _forget_cumsum(fpad, b_pad)
    ct2 = ct[:NH].reshape(NH // 2, 2, S)
    o_a, lse_a, y_a, fetched = _fox_fwd(q_a, k_a, v_a, ct2, gate_a, fetch)
    wt = {**wt, **late_weights(fetched)}
    w_in_b = wt["w_in_b"]
    h1 = _mm("out_a", S, D, [(y_a, _a_rows(D), wt["w_out_a"], _b_cols(D), None)], add=x)
    u_kv, u_b = _rmsnorm_fwd(h1, [kv_g, norm_b], "norm_b")
    kv = _mm("proj_kv", S, 2 * KVW, [(u_kv, _a_rows(D), wt["w_kv"], _b_cols(D), None)])
    pb = _mm("proj_b", S, 2 * D,
             [(u_b, _a_rows(D), w_in_b, pl.BlockSpec((None, D, TN_), lambda j, i: (j, 0, 0)), None)])
    q_b, kdup, vdup = _b_post(pb, kv, qg_b, kg_b, rope)
    gate_b_col = D // LANES
    o_b, lse_b, y_b = _swa_fwd(q_b, kdup, vdup, sinks_t, pb, gate_b_col)
    out = _mm("out_b", S, D, [(y_b, _a_rows(D), wt["w_out_b"], _b_cols(D), None)], add=h1)
    d_out, d_out_b, sq = _loss_head(out, target)

    g = {}
    g["w_out_b"] = _mm("dw_out_b", D, D, [(y_b, _a_cols(S), d_out_b, _b_cols(S), TN)])
    d_y_b = _mm("dy_b", S, D, [(d_out_b, _a_rows(D), wt["w_out_b"], _b_rows(D), NT)])
    dq_b, dkdup, dvdup, dsk, d_gate_b = _swa_bwd(q_b, kdup, vdup, sinks_t, o_b, lse_b, d_y_b, pb, gate_b_col)
    g["sinks"] = dsk[0, ::HD]
    d_qb_raw, dg = _headnorm_bwd(pb, 0, qg_b, dq_b, rope, "qnorm_b_bwd")
    g["qnorm_b_g"] = _fold_heads(dg)
    d_pb = [d_qb_raw, d_qb_raw, d_gate_b, d_gate_b]
    g["w_in_b"] = jnp.concatenate([
        _mm("dw_in_b_q", D, D, [(u_b, _a_cols(S), d_qb_raw, _b_cols(S), TN)], stacked=True),
        _mm("dw_in_b_gate", D, D, [(u_b, _a_cols(S), d_gate_b, _b_cols(S), TN)], stacked=True)], axis=0)
    d_u_b = _mm("du_b", S, D, [(d_pb[c], _a_rows(TN_, col=c % 2), w_in_b, in_b_block(c), NT) for c in range(NCHIP)])
    d_kv, dg = _kv_bwd(dkdup, dvdup, kv, kg_b, rope)
    g["knorm_b_g"] = _fold_heads(dg)
    g["w_kv"] = _mm("dw_kv", D, 2 * KVW, [(u_kv, _a_cols(S), d_kv, _b_cols(S), TN)])
    d_u_kv = _mm("du_kv", S, D, [(d_kv, _a_rows(2 * KVW), wt["w_kv"], _b_rows(2 * KVW), NT)])
    d_h1, d_h1_b, g["kv_norm_g"], g["norm_b_g"] = _rmsnorm_bwd(h1, [kv_g, norm_b], [d_u_kv, d_u_b], d_out, "norm_b_bwd")
    g["w_out_a"] = _mm("dw_out_a", D, D, [(y_a, _a_cols(S), d_h1_b, _b_cols(S), TN)])
    d_y_a = _mm("dy_a", S, D, [(d_h1_b, _a_rows(D), wt["w_out_a"], _b_rows(D), NT)])
    riding, so_far = begin_reduce({n: g[n] for n in LATE})
    dq_a, dk_a, dv_a, dct, d_gate_a, arrived = _fox_bwd(q_a, k_a, v_a, ct2, o_a, lse_a, d_y_a, gate_a, riding)
    dct_pad = jnp.pad(dct.reshape(NH, S), ((0, LANES - NH), (0, 0)))
    d_f, db = _forget_bwd(dct_pad, fpad, b_pad)
    g["b_forget"] = db[0, :NH]
    d_q_raw, dg = _headnorm_bwd(qkv, 0, qg_a, dq_a, None, "qnorm_a_bwd")
    g["qnorm_a_g"] = _fold_heads(dg)
    d_k_raw, dg = _headnorm_bwd(qkv, 1, kg_a, dk_a, None, "knorm_a_bwd")
    g["knorm_a_g"] = _fold_heads(dg)
    rows, gw = 4 * D + NH, None
    for n, t, row0 in (("q", d_q_raw, 0), ("k", d_k_raw, D), ("v", dv_a, 2 * D)):
        gw = _mm("dw_in_a_" + n, D, D, [(t, _a_cols(S), u_a, _b_cols(S), TN)], rows_of=(gw, rows, row0))
    gw = _mm("dw_in_a_f", LANES, D, [(d_f, _a_cols(S, tm=LANES), u_a, _b_cols(S), TN)], tm=LANES,
             rows_of=(gw, rows, 3 * D))
    g["w_in_a"] = _mm("dw_in_a_gate", D, D, [(d_gate_a, _a_cols(S), u_a, _b_cols(S), TN)],
                      rows_of=(gw, rows, 3 * D + NH))
    riding, so_far_first = begin_reduce({"w_in_a": g["w_in_a"]})
    d_u_a, arrived_first = _mm("du_a", S, D, [
        (d_q_raw, _a_rows(D), w1t, _b_cols(D, row=0), None), (d_k_raw, _a_rows(D), w1t, _b_cols(D, row=1), None),
        (dv_a, _a_rows(D), w1t, _b_cols(D, row=2), None), (d_gate_a, _a_rows(D), wg_t, _b_cols(D), None),
        (d_f, _a_rows(LANES), w1t, _b_cols(LANES, row=f_row), None)], riding=riding)
    d_x, _, g["norm_a_g"] = _rmsnorm_bwd(x, [norm_a], [d_u_a], d_h1, "norm_a_bwd")
    return sq, d_x, g, (list(so_far_first) + list(so_far), list(arrived_first) + list(arrived))


BIG = ["w_in_a", "w_out_a", "w_kv", "w_in_b", "w_out_b"]
LATE = BIG[1:]
SPLIT = {"w_in_a": None, "w_out_a": 0, "w_kv": 0, "w_in_b": 0, "w_out_b": 0}
SMALL = ["norm_a_g", "b_forget", "qnorm_a_g", "knorm_a_g", "kv_norm_g", "knorm_b_g", "norm_b_g", "qnorm_b_g", "sinks"]
NAMES = ["norm_a_g", "w_in_a", "b_forget", "qnorm_a_g", "knorm_a_g", "w_out_a", "kv_norm_g", "w_kv", "knorm_b_g",
         "norm_b_g", "w_in_b", "qnorm_b_g", "sinks", "w_out_b"]


def _pack(vals):
    flat = []
    for v in vals:
        v = v.reshape(-1)
        flat.append(jnp.pad(v, (0, -v.shape[0] % LANES)))
    flat = jnp.concatenate(flat)
    flat = jnp.pad(flat, (0, -flat.shape[0] % (8 * LANES)))
    return flat.reshape(-1, LANES)


def _unpack(packed, shapes):
    flat, out, off = packed.reshape(-1), [], 0
    for s in shapes:
        n = int(np.prod(s))
        out.append(flat[off:off + n].reshape(s))
        off += n + (-n % LANES)
    return out


def kernel(x, positions, norm_a_g, w_in_a, b_forget, qnorm_a_g, knorm_a_g, w_out_a, kv_norm_g, w_kv, knorm_b_g, norm_b_g, w_in_b, qnorm_b_g, sinks, w_out_b, loss_target, m_norm_a_g, m_w_in_a, m_b_forget, m_qnorm_a_g, m_knorm_a_g, m_w_out_a, m_kv_norm_g, m_w_kv, m_knorm_b_g, m_norm_b_g, m_w_in_b, m_qnorm_b_g, m_sinks, m_w_out_b, v_norm_a_g, v_w_in_a, v_b_forget, v_qnorm_a_g, v_knorm_a_g, v_w_out_a, v_kv_norm_g, v_w_kv, v_knorm_b_g, v_norm_b_g, v_w_in_b, v_qnorm_b_g, v_sinks, v_w_out_b):
    w = dict(norm_a_g=norm_a_g, w_in_a=w_in_a, b_forget=b_forget, qnorm_a_g=qnorm_a_g, knorm_a_g=knorm_a_g,
             w_out_a=w_out_a, kv_norm_g=kv_norm_g, w_kv=w_kv, knorm_b_g=knorm_b_g, norm_b_g=norm_b_g,
             w_in_b=w_in_b, qnorm_b_g=qnorm_b_g, sinks=sinks, w_out_b=w_out_b)
    m = dict(norm_a_g=m_norm_a_g, w_in_a=m_w_in_a, b_forget=m_b_forget, qnorm_a_g=m_qnorm_a_g, knorm_a_g=m_knorm_a_g,
             w_out_a=m_w_out_a, kv_norm_g=m_kv_norm_g, w_kv=m_w_kv, knorm_b_g=m_knorm_b_g, norm_b_g=m_norm_b_g,
             w_in_b=m_w_in_b, qnorm_b_g=m_qnorm_b_g, sinks=m_sinks, w_out_b=m_w_out_b)
    v = dict(norm_a_g=v_norm_a_g, w_in_a=v_w_in_a, b_forget=v_b_forget, qnorm_a_g=v_qnorm_a_g, knorm_a_g=v_knorm_a_g,
             w_out_a=v_w_out_a, kv_norm_g=v_kv_norm_g, w_kv=v_w_kv, knorm_b_g=v_knorm_b_g, norm_b_g=v_norm_b_g,
             w_in_b=v_w_in_b, qnorm_b_g=v_qnorm_b_g, sinks=v_sinks, w_out_b=v_w_out_b)
    my_chip = 2 * lax.axis_index("x") + lax.axis_index("y")

    def shard2d(t, n):
        if n == "w_in_a":
            return jnp.transpose(t, (2, 0, 1)).reshape(-1)
        return t.reshape(t.shape[-2:])

    def unflat(t, n):
        return jnp.transpose(t.reshape(-1, 1, D), (1, 2, 0)) if n == "w_in_a" else t.reshape(w[n].shape)

    w2d = {n: shard2d(w[n], n) for n in BIG}

    norm_a_rows = jnp.broadcast_to(norm_a_g.reshape(1, D // NCHIP), (16, D // NCHIP))
    w1t, norm_rows = _gather_shards([w2d["w_in_a"].astype(BF16), norm_a_rows], [SPLIT["w_in_a"], 0])
    wt = {"w_in_a_t": w1t.reshape(-1, D), "norm_a_g": norm_rows[:, 0, :].reshape(1, D)}
    for n in SMALL[1:]:
        wt[n] = w[n]
    late_shards = [w2d[n].astype(BF16) for n in LATE]
    late_axes = [SPLIT[n] for n in LATE]
    transfers, outs, own = _gather_plan(late_shards, late_axes)
    fetch = _Riding(transfers, late_shards, outs, own)

    def late_weights(fetched):
        return {n: t if n == "w_in_b" else t.reshape(-1, t.shape[2]) for n, t in zip(LATE, fetched)}

    def as_blocks(t):
        if t.ndim == 3:
            return t
        return t.reshape(-1) if t.shape[0] % (8 * NCHIP) else t.reshape(NCHIP, -1, t.shape[1])

    def begin_reduce(grads):
        names = list(grads)
        axes = [SPLIT[n] for n in names]
        blocks = [as_blocks(grads[n]) for n in names]
        halves = _halves_to_sibling(blocks, axes, "sibling_halves_" + names[0])
        sums = [_chip_sum(blk, part, ax, "chip_sum_" + n) for n, ax, blk, part in zip(names, axes, blocks, halves)]
        bf16 = [s[1] for s in sums]
        transfers, outs = _scatter_plan(bf16)
        return _Riding(transfers, bf16, outs), [s[0] for s in sums]

    sq, d_x, g, (chip_f32, arrived) = _local_step(x[0], loss_target[0], positions, wt, fetch, late_weights,
                                                  begin_reduce)

    small_shapes = [(D,), (NH,), (HD,), (HD,), (D,), (HD,), (D,), (HD,), (NH,), (D,)]
    packed = _pack([g[n] for n in SMALL] + [sq])
    total = _sum_stack(_gather_small(packed), "sum_small")
    small_g = dict(zip(SMALL, _unpack(total, small_shapes)[:-1]))
    loss = 0.5 * jnp.sum(_unpack(total, small_shapes)[-1]) / D
    small_g["norm_a_g"] = lax.dynamic_slice(small_g["norm_a_g"], (my_chip * (D // NCHIP),), (D // NCHIP,))

    axes = [SPLIT[n] for n in BIG]
    halves = []
    for n, ax, t32, parts in zip(BIG, axes, chip_f32, arrived):
        if t32.ndim == 1:
            own = lax.dynamic_slice_in_dim(t32, my_chip * (t32.shape[0] // NCHIP), t32.shape[0] // NCHIP)
        else:
            own = lax.dynamic_index_in_dim(t32, my_chip, axis=0, keepdims=False)
        halves.append(_mesh_sum(own, parts, ax, "mesh_sum_" + n))
    sibling_done = _to_sibling(halves, "finished_halves")

    res = {}
    for n, ax, mine_half, their_half in zip(BIG, axes, halves, sibling_done):
        out4 = _adamw_halves(w2d[n], mine_half, their_half, shard2d(m[n], n), shard2d(v[n], n), ax, "adamw_" + n)
        res[n] = tuple(unflat(t, n) for t in out4)
    sm_g = _pack([small_g[n] for n in SMALL])
    sm = [_pack([d[n] for n in SMALL]) for d in (w, m, v)]
    sm_out = _adamw(sm[0], sm_g, sm[1], sm[2], "adamw_small")
    sm_shapes = [w[n].shape for n in SMALL]
    unpacked = [_unpack(t, sm_shapes) for t in (sm_g,) + tuple(sm_out)]
    for i, n in enumerate(SMALL):
        res[n] = tuple(u[i] for u in unpacked)

    outs = [loss, d_x[None]]
    for k in range(4):
        outs += [res[n][k] for n in NAMES]
    return tuple(outs)
```

```python
import numpy as np
import jax
import jax.numpy as jnp
from jax import lax
from jax.experimental import pallas as pl
from jax.experimental.pallas import tpu as pltpu

F32, BF16 = jnp.float32, jnp.bfloat16
S, D, HD, NH, NKV = 2048, 1024, 64, 16, 4
KVW = NKV * HD
WINDOW = 128
ROT = HD // 4
THETA = 500000.0
EPS = 1e-6
SCALE = HD ** -0.5
LANES = 128
NEG = -1e30
VMEM_LIMIT = 48 * 2 ** 20
ROWS = 256
ATT = 512
SWQ = 4
NCHIP = 4
ADAM_LR, ADAM_B1, ADAM_B2, ADAM_EPS, ADAM_WD, ADAM_STEP = 0.001, 0.9, 0.999, 1e-08, 0.01, 10
NT = (((1,), (1,)), ((), ()))
TN = (((0,), (0,)), ((), ()))
MESH = pl.DeviceIdType.MESH


def _params(n):
    return pltpu.CompilerParams(dimension_semantics=("arbitrary",) * n, vmem_limit_bytes=VMEM_LIMIT)


def _dot(a, b, dims=None):
    if dims is None:
        return jnp.dot(a, b, preferred_element_type=F32)
    return lax.dot_general(a, b, dims, preferred_element_type=F32)


def _dot_split(a, b, n):
    out, rest = None, a
    for _ in range(n):
        hi = rest.astype(BF16)
        term = _dot(hi, b)
        out = term if out is None else out + term
        rest = rest - hi.astype(F32)
    return out


def _seg_mat(w):
    e = (np.arange(w)[:, None] // HD == np.arange(LANES)[None, :]).astype(np.float32)
    return jnp.asarray(e, BF16)


def _spread(r, w):
    head = lax.broadcasted_iota(jnp.int32, (2 * LANES, w), 1) >> 6
    row = lax.broadcasted_iota(jnp.int32, (2 * LANES, w), 0)
    et2 = jnp.where(head == (row & (LANES - 1)), 1.0, 0.0).astype(BF16)
    hi = r.astype(BF16)
    lo = (r - hi.astype(F32)).astype(BF16)
    return _dot(jnp.concatenate([hi, lo], axis=1), et2)


def _head_rstd(x, e):
    ss = _dot_split(x * x, e, 2)
    return _spread(lax.rsqrt(ss * (1.0 / HD) + EPS), x.shape[1])


def _rope(x, c, a, b):
    w = x.shape[1]
    return x * c + pltpu.roll(x, w - ROT // 2, 1) * a + pltpu.roll(x, ROT // 2, 1) * b


def _rope_t(dy, c, a, b):
    w = dy.shape[1]
    return dy * c + pltpu.roll(dy * b, w - ROT // 2, 1) + pltpu.roll(dy * a, ROT // 2, 1)


def _sigmoid(x):
    return 1.0 / (1.0 + jnp.exp(-x))


def _row_spec(shape, ts):
    nd = len(shape)
    if shape[0] == S:
        return pl.BlockSpec((ts,) + tuple(shape[1:]), lambda i: (i,) + (0,) * (nd - 1))
    return pl.BlockSpec(tuple(shape), lambda i: (0,) * nd)


def _rows_call(body, name, ins, outs, ts=ROWS):
    return pl.pallas_call(
        body, name=name, grid=(S // ts,),
        in_specs=[_row_spec(a.shape, ts) for a in ins],
        out_specs=[_row_spec(s, ts) for s, _ in outs],
        out_shape=[jax.ShapeDtypeStruct(s, d) for s, d in outs],
        compiler_params=_params(1))(*ins)


def _col_spec(ts, w, col):
    return pl.BlockSpec((ts, w), lambda i: (i, col))


TM = TN_ = 512
TM_TOKENS = 1024


def _mm(name, m, n, terms, out_dtype=F32, add=None, tm=None, tn=TN_, stacked=False, riding=None, rows_of=None):
    nterm = len(terms)
    if tm is None:
        tm = TM_TOKENS if m == S else TM
    nj, ni_ = n // tn, m // tm
    n_in = 2 * nterm + (add is not None) + (rows_of is not None and rows_of[0] is not None)
    r_in, r_out = (len(riding.ins), len(riding.outs)) if riding is not None else (0, 0)

    def body(*refs):
        if riding is not None:
            j, i = pl.program_id(0), pl.program_id(1)
            at_end = riding.hooks(refs[n_in:n_in + r_in], refs[n_in + r_in + 1:n_in + r_in + 1 + r_out],
                                  *refs[n_in + r_in + 1 + r_out:], first=(j == 0) & (i == 0),
                                  middle=(j == nj // 2) & (i == 0), last=(j == nj - 1) & (i == ni_ - 1))
        acc = None
        for t in range(nterm):
            part = _dot(refs[2 * t][...], refs[2 * t + 1][...], terms[t][4])
            acc = part if acc is None else acc + part
        if add is not None:
            acc = acc + refs[2 * nterm][...]
        refs[n_in + r_in][...] = acc.astype(out_dtype)
        if riding is not None:
            at_end()

    tile = pl.BlockSpec((tm, tn), lambda j, i: (i, j))
    ins, specs = [], []
    for a, a_spec, b, b_spec, _ in terms:
        ins += [a, b]
        specs += [a_spec, b_spec]
    if add is not None:
        ins.append(add)
        specs.append(tile)
    out_spec = pl.BlockSpec((None, tm, tn), lambda j, i: (j, i, 0)) if stacked else tile
    out_shape = jax.ShapeDtypeStruct((nj, m, tn) if stacked else (m, n), out_dtype)
    if rows_of is not None:
        taller, rows, row0 = rows_of
        out_spec = pl.BlockSpec((pl.Element(tm), pl.Element(tn)), lambda j, i: (
            pl.multiple_of(row0 + i * tm, 8), pl.multiple_of(j * tn, LANES)))
        out_shape = jax.ShapeDtypeStruct((rows, n), out_dtype)
        alias = {}
        if taller is not None:
            ins.append(taller)
            specs.append(pl.BlockSpec(memory_space=pltpu.HBM))
            alias = {len(ins) - 1: 0}
        return pl.pallas_call(body, name=name, grid=(nj, ni_), in_specs=specs, out_specs=out_spec,
                              out_shape=out_shape, input_output_aliases=alias, compiler_params=_params(2))(*ins)
    if riding is None:
        return pl.pallas_call(body, name=name, grid=(nj, ni_), in_specs=specs, out_specs=out_spec,
                              out_shape=out_shape, compiler_params=_params(2))(*ins)
    res = pl.pallas_call(
        body, name=name, grid=(nj, ni_), in_specs=specs + riding.in_specs,
        out_specs=[out_spec] + riding.out_specs, out_shape=[out_shape] + riding.out_shape,
        scratch_shapes=riding.scratch, compiler_params=_params(2))(*ins, *riding.ins)
    return res[0], res[1:]


def _a_rows(k, col=0, tm=TM_TOKENS):
    return pl.BlockSpec((tm, k), lambda j, i: (i, col))


def _a_cols(k, tm=TM):
    return pl.BlockSpec((k, tm), lambda j, i: (0, i))


def _b_cols(k, row=0, col0=0, tn=TN_):
    return pl.BlockSpec((k, tn), lambda j, i: (row, col0 + j))


def _b_rows(k, row0=0, tn=TN_):
    return pl.BlockSpec((tn, k), lambda j, i: (row0 + j, 0))


def _rmsnorm_fwd(x, gains, name):
    def body(*refs):
        xv = refs[0][...]
        r = lax.rsqrt(jnp.mean(xv * xv, axis=-1, keepdims=True) + EPS)
        xh = xv * r
        for n in range(len(gains)):
            refs[1 + len(gains) + n][...] = (xh * refs[1 + n][...]).astype(BF16)

    return _rows_call(body, name, [x] + list(gains), [((S, D), BF16)] * len(gains))


def _rmsnorm_bwd(x, gains, dus, dres, name):
    n = len(gains)

    def body(*refs):
        x_ref, g_refs, du_refs, dres_ref = refs[0], refs[1:1 + n], refs[1 + n:1 + 2 * n], refs[1 + 2 * n]
        dx_ref, dxb_ref, dg_refs = refs[2 + 2 * n], refs[3 + 2 * n], refs[4 + 2 * n:]
        xv = x_ref[...]
        r = lax.rsqrt(jnp.mean(xv * xv, axis=-1, keepdims=True) + EPS)
        xh = xv * r
        gy = None
        for m in range(n):
            du = du_refs[m][...]
            part = jnp.sum(du * xh, axis=0, keepdims=True)

            @pl.when(pl.program_id(0) == 0)
            def _(m=m, part=part):
                dg_refs[m][...] = part

            @pl.when(pl.program_id(0) != 0)
            def _(m=m, part=part):
                dg_refs[m][...] += part

            t = du * g_refs[m][...]
            gy = t if gy is None else gy + t
        dx = dres_ref[...] + r * (gy - xh * jnp.mean(gy * xh, axis=-1, keepdims=True))
        dx_ref[...] = dx
        dxb_ref[...] = dx.astype(BF16)

    outs = [((S, D), F32), ((S, D), BF16)] + [((1, D), F32)] * n
    return _rows_call(body, name, [x] + list(gains) + list(dus) + [dres], outs)


def _a_post(qkvg, qg, kg):
    e = _seg_mat(D)

    def body(q_ref, k_ref, v_ref, qg_ref, kg_ref, e_ref, qo, ko, vo):
        ev = e_ref[...]
        qv, kv = q_ref[...], k_ref[...]
        qo[...] = (qv * _head_rstd(qv, ev) * qg_ref[...] * SCALE).astype(BF16)
        ko[...] = (kv * _head_rstd(kv, ev) * kg_ref[...]).astype(BF16)
        vo[...] = v_ref[...].astype(BF16)

    whole = lambda a: pl.BlockSpec(a.shape, lambda i: (0, 0))
    return pl.pallas_call(
        body, name="a_post", grid=(S // ROWS,),
        in_specs=[_col_spec(ROWS, D, 0), _col_spec(ROWS, D, 1), _col_spec(ROWS, D, 2),
                  whole(qg), whole(kg), whole(e)],
        out_specs=[_col_spec(ROWS, D, 0)] * 3,
        out_shape=[jax.ShapeDtypeStruct((S, D), BF16)] * 3,
        compiler_params=_params(1))(qkvg, qkvg, qkvg, qg, kg, e)


def _tri(upper):
    r, c = np.arange(ROWS)[:, None], np.arange(ROWS)[None, :]
    return jnp.asarray((r <= c) if upper else (r >= c), BF16)


def _forget_cumsum(fpad, bpad):
    def body(f_ref, b_ref, u_ref, c_ref, carry):
        @pl.when(pl.program_id(0) == 0)
        def _():
            carry[...] = jnp.zeros_like(carry)

        lf = jax.nn.log_sigmoid(f_ref[...] + b_ref[...])
        blk = _dot_split(lf.T, u_ref[...], 3) + carry[:, 0:1]
        c_ref[...] = blk
        carry[...] = jnp.broadcast_to(blk[:, ROWS - 1:ROWS], carry.shape)

    return pl.pallas_call(
        body, name="forget_cumsum", grid=(S // ROWS,),
        in_specs=[pl.BlockSpec((ROWS, LANES), lambda i: (i, 0)), pl.BlockSpec((1, LANES), lambda i: (0, 0)),
                  pl.BlockSpec((ROWS, ROWS), lambda i: (0, 0))],
        out_specs=pl.BlockSpec((LANES, ROWS), lambda i: (0, i)),
        out_shape=jax.ShapeDtypeStruct((LANES, S), F32),
        scratch_shapes=[pltpu.VMEM((LANES, LANES), F32)],
        compiler_params=_params(1))(fpad, bpad, _tri(True))


def _forget_bwd(dct, fpad, bpad):
    nb = S // ROWS

    def body(dc_ref, f_ref, b_ref, l_ref, df_ref, db_ref, carry):
        @pl.when(pl.program_id(0) == 0)
        def _():
            carry[...] = jnp.zeros_like(carry)
            db_ref[...] = jnp.zeros_like(db_ref)

        blk = _dot_split(dc_ref[...], l_ref[...], 3) + carry[:, 0:1]
        carry[...] = jnp.broadcast_to(blk[:, 0:1], carry.shape)
        df = blk.T * _sigmoid(-(f_ref[...] + b_ref[...]))
        df_ref[...] = df.astype(BF16)
        db_ref[...] += jnp.sum(df, axis=0, keepdims=True)

    return pl.pallas_call(
        body, name="forget_bwd", grid=(nb,),
        in_specs=[pl.BlockSpec((LANES, ROWS), lambda i: (0, nb - 1 - i)),
                  pl.BlockSpec((ROWS, LANES), lambda i: (nb - 1 - i, 0)),
                  pl.BlockSpec((1, LANES), lambda i: (0, 0)), pl.BlockSpec((ROWS, ROWS), lambda i: (0, 0))],
        out_specs=[pl.BlockSpec((ROWS, LANES), lambda i: (nb - 1 - i, 0)), pl.BlockSpec((1, LANES), lambda i: (0, 0))],
        out_shape=[jax.ShapeDtypeStruct((S, LANES), BF16), jax.ShapeDtypeStruct((1, LANES), F32)],
        scratch_shapes=[pltpu.VMEM((LANES, LANES), F32)],
        compiler_params=_params(1))(dct, fpad, bpad, _tri(False))


def _headnorm_bwd(x, col, gain, dy, rope, name):
    e = _seg_mat(D)
    tabs = list(rope) if rope is not None else []

    def body(*refs):
        x_ref, g_ref, dy_ref, e_ref = refs[:4]
        dx_ref, dg_ref = refs[-2:]
        xv, dyv, ev = x_ref[...], dy_ref[...], e_ref[...]
        if rope is not None:
            c, a, b = (jnp.tile(t[...], (1, D // LANES)) for t in refs[4:7])
            dyv = _rope_t(dyv, c, a, b)
        r = _head_rstd(xv, ev)
        xh = xv * r
        part = jnp.sum(dyv * xh, axis=0, keepdims=True)

        @pl.when(pl.program_id(0) == 0)
        def _():
            dg_ref[...] = part

        @pl.when(pl.program_id(0) != 0)
        def _():
            dg_ref[...] += part

        gy = dyv * g_ref[...]
        seg = _spread(_dot_split(gy * xh, ev, 2) * (1.0 / HD), D)
        dx_ref[...] = (r * (gy - xh * seg)).astype(BF16)

    whole = lambda a: pl.BlockSpec(a.shape, lambda i: (0, 0))
    return pl.pallas_call(
        body, name=name, grid=(S // ROWS,),
        in_specs=[_col_spec(ROWS, D, col), whole(gain), _col_spec(ROWS, D, 0), whole(e)]
                 + [pl.BlockSpec((ROWS, LANES), lambda i: (i, 0))] * len(tabs),
        out_specs=[_col_spec(ROWS, D, 0), whole(gain)],
        out_shape=[jax.ShapeDtypeStruct((S, D), BF16), jax.ShapeDtypeStruct((1, D), F32)],
        compiler_params=_params(1))(x, gain, dy, e, *tabs)


def _dup_mat():
    r, c = np.arange(KVW)[:, None], np.arange(2 * KVW)[None, :]
    return (r // HD == c // LANES) & (r % HD == c % HD)


def _fold_mat():
    r, c = np.arange(D)[:, None], np.arange(KVW)[None, :]
    return (r // (2 * LANES) == c // HD) & (r % HD == c % HD)


def _b_post(pb, kv, qg, kg, rope):
    e, ek = _seg_mat(D), _seg_mat(KVW)
    dup = jnp.asarray(_dup_mat(), BF16)

    def body(q_ref, k_ref, v_ref, qg_ref, kg_ref, e_ref, ek_ref, dup_ref, c_ref, a_ref, b_ref, qo, ko, vo):
        c1, a1, b1 = c_ref[...], a_ref[...], b_ref[...]
        qv = q_ref[...]
        qn = qv * _head_rstd(qv, e_ref[...]) * qg_ref[...]
        t = lambda z, n: jnp.tile(z, (1, n))
        qo[...] = (_rope(qn, t(c1, D // LANES), t(a1, D // LANES), t(b1, D // LANES)) * SCALE).astype(BF16)
        kvv = k_ref[...]
        kn = kvv * _head_rstd(kvv, ek_ref[...]) * kg_ref[...]
        kr = _rope(kn, t(c1, KVW // LANES), t(a1, KVW // LANES), t(b1, KVW // LANES)).astype(BF16)
        ko[...] = _dot(kr, dup_ref[...]).astype(BF16)
        vo[...] = _dot(v_ref[...].astype(BF16), dup_ref[...]).astype(BF16)

    whole = lambda a: pl.BlockSpec(a.shape, lambda i: (0, 0))
    tab = pl.BlockSpec((ROWS, LANES), lambda i: (i, 0))
    return pl.pallas_call(
        body, name="b_post", grid=(S // ROWS,),
        in_specs=[_col_spec(ROWS, D, 0), _col_spec(ROWS, KVW, 0), _col_spec(ROWS, KVW, 1),
                  whole(qg), whole(kg), whole(e), whole(ek), whole(dup), tab, tab, tab],
        out_specs=[_col_spec(ROWS, D, 0), _col_spec(ROWS, 2 * KVW, 0), _col_spec(ROWS, 2 * KVW, 0)],
        out_shape=[jax.ShapeDtypeStruct((S, D), BF16), jax.ShapeDtypeStruct((S, 2 * KVW), BF16),
                   jax.ShapeDtypeStruct((S, 2 * KVW), BF16)],
        compiler_params=_params(1))(pb, kv, kv, qg, kg, e, ek, dup, *rope)


def _kv_bwd(dkdup, dvdup, kv, kg, rope):
    ek = _seg_mat(KVW)
    fold = jnp.asarray(_fold_mat(), BF16)

    def body(dk_ref, dv_ref, k_ref, kg_ref, ek_ref, fold_ref, c_ref, a_ref, b_ref, dkv_ref, dg_ref):
        ev, fv = ek_ref[...], fold_ref[...]
        t = lambda z: jnp.tile(z[...], (1, KVW // LANES))
        dk = _rope_t(_dot_split(dk_ref[...], fv, 2), t(c_ref), t(a_ref), t(b_ref))
        dv = _dot_split(dv_ref[...], fv, 2)
        xv = k_ref[...]
        r = _head_rstd(xv, ev)
        xh = xv * r
        part = jnp.sum(dk * xh, axis=0, keepdims=True)

        @pl.when(pl.program_id(0) == 0)
        def _():
            dg_ref[...] = part

        @pl.when(pl.program_id(0) != 0)
        def _():
            dg_ref[...] += part

        gy = dk * kg_ref[...]
        seg = _spread(_dot_split(gy * xh, ev, 2) * (1.0 / HD), KVW)
        dkv_ref[:, 0:KVW] = (r * (gy - xh * seg)).astype(BF16)
        dkv_ref[:, KVW:2 * KVW] = dv.astype(BF16)

    whole = lambda a: pl.BlockSpec(a.shape, lambda i: (0, 0))
    tab = pl.BlockSpec((ROWS, LANES), lambda i: (i, 0))
    return pl.pallas_call(
        body, name="kv_bwd", grid=(S // ROWS,),
        in_specs=[_col_spec(ROWS, D, 0), _col_spec(ROWS, D, 0), _col_spec(ROWS, KVW, 0),
                  whole(kg), whole(ek), whole(fold), tab, tab, tab],
        out_specs=[_col_spec(ROWS, 2 * KVW, 0), whole(kg)],
        out_shape=[jax.ShapeDtypeStruct((S, 2 * KVW), BF16), jax.ShapeDtypeStruct((1, KVW), F32)],
        compiler_params=_params(1))(dkdup, dvdup, kv, kg, ek, fold, *rope)


def _loss_head(out, target):
    def body(o_ref, t_ref, d_ref, db_ref, l_ref):
        diff = o_ref[...] - t_ref[...]
        d = diff * (1.0 / D)
        d_ref[...] = d
        db_ref[...] = d.astype(BF16)

        @pl.when(pl.program_id(0) == 0)
        def _():
            l_ref[...] = jnp.zeros_like(l_ref)

        l_ref[...] += jnp.sum(diff * diff, axis=0, keepdims=True)

    return _rows_call(body, "loss_head", [out, target], [((S, D), F32), ((S, D), BF16), ((1, D), F32)])


def _lane():
    return lax.broadcasted_iota(jnp.int32, (1, LANES), 1)


def _head_mask(hh):
    return (_lane() < HD) if hh == 0 else (_lane() >= HD)


def _fox_fwd(q, k, v, ct, gate, riding):
    nq, npair = S // ATT, NH // 2
    ni, no = len(riding.ins), len(riding.outs)

    def body(q_ref, k_ref, v_ref, c_ref, gate_ref, *rest):
        o_ref, lse_ref, y_ref = rest[ni:ni + 3]
        pair, i = pl.program_id(0), pl.program_id(1)
        at_end = riding.hooks(rest[:ni], rest[ni + 3:ni + 3 + no], *rest[ni + 3 + no:],
                              first=(pair == 0) & (i == 0), middle=(pair == npair // 2) & (i == 0),
                              last=(pair == npair - 1) & (i == nq - 1))
        q2 = q_ref[...]
        qms = [jnp.where(_head_mask(hh), q2, jnp.zeros_like(q2)) for hh in (0, 1)]

        def probs(off, width, m, hh, diag):
            s = _dot(qms[hh], k_ref[pl.ds(off, width), :], NT) - c_ref[hh:hh + 1, pl.ds(off, width)]
            if diag:
                row = i * ATT + lax.broadcasted_iota(jnp.int32, (ATT, width), 0)
                col = off + lax.broadcasted_iota(jnp.int32, (ATT, width), 1)
                s = jnp.where(col <= row, s, NEG)
            m_new = jnp.maximum(m, jnp.max(s, axis=1, keepdims=True))
            p = jnp.exp(s - m_new)
            p_hi = p.astype(BF16)
            return m_new, jnp.exp(m - m_new), p_hi, (p - p_hi.astype(F32)).astype(BF16)

        def weighted(off, width, p_hi, p_lo, hh):
            vj = v_ref[pl.ds(off, width), :]
            v1 = jnp.where(_head_mask(hh), vj, jnp.ones_like(vj))
            return _dot(p_hi, v1) + _dot(p_lo, v1)

        def step(off, width, carry, diag):
            off = pl.multiple_of(off, ATT)
            out = []
            for hh in (0, 1):
                m, acc = carry[hh]
                m, alpha, p_hi, p_lo = probs(off, width, m, hh, diag)
                out.append((m, alpha * acc + weighted(off, width, p_hi, p_lo, hh)))
            return tuple(out)

        one = (jnp.full((ATT, 1), NEG, F32), jnp.zeros((ATT, LANES), F32))
        carry = lax.fori_loop(0, i // 2, lambda j, cr: step(j * (2 * ATT), 2 * ATT, cr, False), (one, one))
        carry = lax.cond(i % 2 == 1, lambda cr: step((i - 1) * ATT, 2 * ATT, cr, True),
                         lambda cr: step(i * ATT, ATT, cr, True), carry)
        res = []
        for hh in (0, 1):
            m, acc = carry[hh]
            l = jnp.max(jnp.where(_head_mask(1 - hh), acc, 0.0), axis=1, keepdims=True)
            res.append((acc / l, m + jnp.log(l)))
        first = _head_mask(0)
        o = jnp.where(first, res[0][0], res[1][0])
        o_ref[...] = o
        lse_ref[...] = jnp.where(first, res[0][1], res[1][1])
        g = gate_ref[...]
        y_ref[...] = (o * (g * _sigmoid(g))).astype(BF16)
        at_end()

    blk = pl.BlockSpec((ATT, LANES), lambda p, i: (i, p))
    full = pl.BlockSpec((S, LANES), lambda p, i: (0, p))
    res = pl.pallas_call(
        body, name="fox_fwd", grid=(npair, nq),
        in_specs=[blk, full, full, pl.BlockSpec((None, 2, S), lambda p, i: (p, 0, 0)), blk] + riding.in_specs,
        out_specs=[blk, blk, blk] + riding.out_specs,
        out_shape=[jax.ShapeDtypeStruct((S, D), F32)] * 2 + [jax.ShapeDtypeStruct((S, D), BF16)] + riding.out_shape,
        scratch_shapes=riding.scratch,
        compiler_params=_params(2))(q, k, v, ct, gate, *riding.ins)
    return res[0], res[1], res[2], res[3:]


def _gate_grads(dy, o, g):
    sg = _sigmoid(g)
    return dy * (g * sg), dy * o * (sg * (1.0 + g * (1.0 - sg)))


def _fox_bwd(q, k, v, ct, o, lse, dy, gate, riding):
    nq, npair = S // ATT, NH // 2
    ni, no = len(riding.ins), len(riding.outs)

    def body(q_ref, k_ref, v_ref, c_ref, o_ref, lse_ref, dy_ref, gate_ref, *rest):
        dq_ref, dk_ref, dvb_ref, dc_ref, dgate_ref = rest[ni:ni + 5]
        dv_ref = rest[ni + 5 + no]
        pair, i = pl.program_id(0), pl.program_id(1)
        at_end = riding.hooks(rest[:ni], rest[ni + 5:ni + 5 + no], *rest[ni + 6 + no:],
                              first=(pair == 0) & (i == 0), middle=(pair == npair // 2) & (i == 0),
                              last=(pair == npair - 1) & (i == nq - 1))

        @pl.when(i == 0)
        def _():
            dk_ref[...] = jnp.zeros_like(dk_ref)
            dv_ref[...] = jnp.zeros_like(dv_ref)
            dc_ref[...] = jnp.zeros_like(dc_ref)

        q2, lse2 = q_ref[...], lse_ref[...]
        do2, dgate = _gate_grads(dy_ref[...], o_ref[...], gate_ref[...])
        dgate_ref[...] = dgate.astype(BF16)
        do2b = do2.astype(BF16)
        prod = do2b.astype(F32) * o_ref[...]
        heads = []
        for hh in (0, 1):
            hm = _head_mask(hh)
            heads.append((jnp.where(hm, q2, jnp.zeros_like(q2)), jnp.where(hm, do2b, jnp.zeros_like(do2b)),
                          jnp.sum(jnp.where(hm, prod, 0.0), axis=1, keepdims=True),
                          jnp.max(jnp.where(hm, lse2, NEG), axis=1, keepdims=True)))

        def step(off, width, dqs, diag):
            off = pl.multiple_of(off, ATT)
            kj, vj = k_ref[pl.ds(off, width), :], v_ref[pl.ds(off, width), :]
            dk, dv, out = None, None, []
            for hh in (0, 1):
                qm, dom, delta, lse_h = heads[hh]
                s = _dot(qm, kj, NT) - c_ref[hh:hh + 1, pl.ds(off, width)]
                p = jnp.exp(s - lse_h)
                if diag:
                    row = i * ATT + lax.broadcasted_iota(jnp.int32, (ATT, width), 0)
                    col = off + lax.broadcasted_iota(jnp.int32, (ATT, width), 1)
                    p = jnp.where(col <= row, p, 0.0)
                ds = p * (_dot(dom, vj, NT) - delta)
                dc_ref[hh:hh + 1, pl.ds(off, width)] += -jnp.sum(ds, axis=0, keepdims=True)
                dsb = ds.astype(BF16)
                dk_h, dv_h = _dot(dsb, qm, TN), _dot(p.astype(BF16), dom, TN)
                dk, dv = (dk_h, dv_h) if dk is None else (dk + dk_h, dv + dv_h)
                out.append(dqs[hh] + _dot(dsb, kj))
            dk_ref[pl.ds(off, width), :] += dk
            dv_ref[pl.ds(off, width), :] += dv
            return tuple(out)

        zero = jnp.zeros((ATT, LANES), F32)
        dqs = lax.fori_loop(0, i // 2, lambda j, acc: step(j * (2 * ATT), 2 * ATT, acc, False), (zero, zero))
        dqs = lax.cond(i % 2 == 1, lambda acc: step((i - 1) * ATT, 2 * ATT, acc, True),
                       lambda acc: step(i * ATT, ATT, acc, True), dqs)
        dq_ref[...] = jnp.where(_head_mask(0), dqs[0], dqs[1]) * SCALE

        @pl.when(i == nq - 1)
        def _():
            dvb_ref[...] = dv_ref[...].astype(BF16)

        at_end()

    blk = pl.BlockSpec((ATT, LANES), lambda p, i: (i, p))
    full = pl.BlockSpec((S, LANES), lambda p, i: (0, p))
    cspec = pl.BlockSpec((None, 2, S), lambda p, i: (p, 0, 0))
    res = pl.pallas_call(
        body, name="fox_bwd", grid=(npair, nq),
        in_specs=[blk, full, full, cspec, blk, blk, blk, blk] + riding.in_specs,
        out_specs=[blk, full, full, cspec, blk] + riding.out_specs,
        out_shape=[jax.ShapeDtypeStruct((S, D), F32)] * 2 + [jax.ShapeDtypeStruct((S, D), BF16),
                                                              jax.ShapeDtypeStruct((npair, 2, S), F32),
                                                              jax.ShapeDtypeStruct((S, D), BF16)]
                  + riding.out_shape,
        scratch_shapes=[pltpu.VMEM((S, LANES), F32)] + riding.scratch,
        compiler_params=_params(2))(q, k, v, ct, o, lse, dy, gate, *riding.ins)
    return res[0], res[1], res[2], res[3], res[4], res[5:]


def _both_heads(x):
    return jnp.concatenate([jnp.where(_head_mask(hh), x, jnp.zeros_like(x)) for hh in (0, 1)], axis=0)


def _per_head(col0, col1):
    return jnp.concatenate([jnp.broadcast_to(col0, (WINDOW, 1)), jnp.broadcast_to(col1, (WINDOW, 1))], axis=0)


def _unstack(x2):
    return jnp.where(_head_mask(0), x2[:WINDOW], x2[WINDOW:])


def _swa_valid(i, start):
    r = lax.broadcasted_iota(jnp.int32, (2 * WINDOW, 2 * WINDOW), 0)
    qabs = i * WINDOW + jnp.where(r >= WINDOW, r - WINDOW, r)
    kabs = start + lax.broadcasted_iota(jnp.int32, (2 * WINDOW, 2 * WINDOW), 1)
    return (kabs <= qabs) & (qabs - kabs < WINDOW)


def _swa_fwd(q, kdup, vdup, sinks_t, proj, gate_col):
    def body(q_ref, k_ref, v_ref, sk_ref, gate_ref, o_ref, lse_ref, y_ref):
        skv = sk_ref[...]
        first = _head_mask(0)
        for sb in range(SWQ):
            i = pl.program_id(1) * SWQ + sb
            rows = slice(sb * WINDOW, (sb + 1) * WINDOW)
            start = pl.multiple_of(jnp.maximum(i - 1, 0) * WINDOW, WINDOW)
            kk, vv = k_ref[pl.ds(start, 2 * WINDOW), :], v_ref[pl.ds(start, 2 * WINDOW), :]
            q2 = q_ref[rows, :]
            valid = _swa_valid(i, start)[:WINDOW]
            res = []
            for hh in (0, 1):
                hm = _head_mask(hh)
                sink = jnp.max(jnp.where(hm, skv, NEG), axis=1, keepdims=True)
                s = jnp.where(valid, _dot(jnp.where(hm, q2, jnp.zeros_like(q2)), kk, NT), NEG)
                m = jnp.maximum(jnp.max(s, axis=1, keepdims=True), sink)
                p = jnp.exp(s - m)
                l = jnp.sum(p, axis=1, keepdims=True) + jnp.exp(sink - m)
                res.append((_dot(p.astype(BF16), vv) / l, m + jnp.log(l)))
            o = jnp.where(first, res[0][0], res[1][0])
            o_ref[rows, :] = o
            lse_ref[rows, :] = jnp.where(first, res[0][1], res[1][1])
            g = gate_ref[rows, :]
            y_ref[rows, :] = (o * (g * _sigmoid(g))).astype(BF16)

    blk = pl.BlockSpec((SWQ * WINDOW, LANES), lambda p, i: (i, p))
    gate = pl.BlockSpec((SWQ * WINDOW, LANES), lambda p, i: (i, gate_col + p))
    full = pl.BlockSpec((S, LANES), lambda p, i: (0, p // 2))
    return pl.pallas_call(
        body, name="swa_fwd", grid=(NH // 2, S // (SWQ * WINDOW)),
        in_specs=[blk, full, full, pl.BlockSpec((1, LANES), lambda p, i: (0, p)), gate],
        out_specs=[blk, blk, blk],
        out_shape=[jax.ShapeDtypeStruct((S, D), F32)] * 2 + [jax.ShapeDtypeStruct((S, D), BF16)],
        compiler_params=_params(2))(q, kdup, vdup, sinks_t, proj)


def _swa_bwd(q, kdup, vdup, sinks_t, o, lse, dy, proj, gate_col):
    def body(q_ref, k_ref, v_ref, sk_ref, o_ref, lse_ref, dy_ref, gate_ref, dq_ref, dk_ref, dv_ref, dsk_ref,
             dgate_ref):
        @pl.when(pl.program_id(1) == 0)
        def _():
            dk_ref[...] = jnp.zeros_like(dk_ref)
            dv_ref[...] = jnp.zeros_like(dv_ref)
            dsk_ref[...] = jnp.zeros_like(dsk_ref)

        skv = sk_ref[...]
        first = _head_mask(0)
        sink = _per_head(*[jnp.max(jnp.where(_head_mask(hh), skv, NEG), axis=1, keepdims=True) for hh in (0, 1)])
        for sb in range(SWQ):
            i = pl.program_id(1) * SWQ + sb
            rows = slice(sb * WINDOW, (sb + 1) * WINDOW)
            start = pl.multiple_of(jnp.maximum(i - 1, 0) * WINDOW, WINDOW)
            kk, vv = k_ref[pl.ds(start, 2 * WINDOW), :], v_ref[pl.ds(start, 2 * WINDOW), :]
            do2, dgate = _gate_grads(dy_ref[rows, :], o_ref[rows, :], gate_ref[rows, :])
            dgate_ref[rows, :] = dgate.astype(BF16)
            do2b = do2.astype(BF16)
            prod, lse2 = do2b.astype(F32) * o_ref[rows, :], lse_ref[rows, :]
            qs, dos = _both_heads(q_ref[rows, :]), _both_heads(do2b)
            delta = jnp.concatenate([jnp.sum(jnp.where(_head_mask(hh), prod, 0.0), axis=1, keepdims=True)
                                     for hh in (0, 1)], axis=0)
            lse_h = jnp.concatenate([jnp.max(jnp.where(_head_mask(hh), lse2, NEG), axis=1, keepdims=True)
                                     for hh in (0, 1)], axis=0)
            p = jnp.where(_swa_valid(i, start), jnp.exp(_dot(qs, kk, NT) - lse_h), 0.0)
            dsb = (p * (_dot(dos, vv, NT) - delta)).astype(BF16)
            dk_ref[pl.ds(start, 2 * WINDOW), :] += _dot(dsb, qs, TN)
            dv_ref[pl.ds(start, 2 * WINDOW), :] += _dot(p.astype(BF16), dos, TN)
            dq_ref[rows, :] = _unstack(_dot(dsb, kk)) * SCALE
            t = jnp.exp(sink - lse_h) * delta
            dsk_ref[...] += -jnp.where(first, jnp.sum(t[:WINDOW], axis=0, keepdims=True),
                                       jnp.sum(t[WINDOW:], axis=0, keepdims=True))

    blk = pl.BlockSpec((SWQ * WINDOW, LANES), lambda p, i: (i, p))
    full = pl.BlockSpec((S, LANES), lambda p, i: (0, p // 2))
    acc = pl.BlockSpec((S, LANES), lambda p, i: (0, p))
    sk = pl.BlockSpec((1, LANES), lambda p, i: (0, p))
    gate = pl.BlockSpec((SWQ * WINDOW, LANES), lambda p, i: (i, gate_col + p))
    return pl.pallas_call(
        body, name="swa_bwd", grid=(NH // 2, S // (SWQ * WINDOW)),
        in_specs=[blk, full, full, sk, blk, blk, blk, gate],
        out_specs=[blk, acc, acc, sk, blk],
        out_shape=[jax.ShapeDtypeStruct((S, D), F32)] * 3 + [jax.ShapeDtypeStruct((1, D), F32),
                                                              jax.ShapeDtypeStruct((S, D), BF16)],
        compiler_params=_params(2))(q, kdup, vdup, sinks_t, o, lse, dy, proj)


def _adamw_math(w, g, m, v):
    m = ADAM_B1 * m + (1.0 - ADAM_B1) * g
    v = ADAM_B2 * v + (1.0 - ADAM_B2) * jnp.square(g)
    m_hat = m / (1.0 - ADAM_B1 ** ADAM_STEP)
    v_hat = v / (1.0 - ADAM_B2 ** ADAM_STEP)
    delta = -ADAM_LR * (m_hat / (jnp.sqrt(v_hat) + ADAM_EPS) + ADAM_WD * w)
    return delta, m, v


def _adamw(w, g, m, v, name):
    r, c = w.shape
    tr = min(r, 128)

    def body(w_ref, g_ref, m_ref, v_ref, d_ref, mo_ref, vo_ref):
        d_ref[...], mo_ref[...], vo_ref[...] = _adamw_math(w_ref[...], g_ref[...], m_ref[...], v_ref[...])

    spec = pl.BlockSpec((tr, c), lambda i: (i, 0))
    return pl.pallas_call(
        body, name=name, grid=(r // tr,), in_specs=[spec] * 4, out_specs=[spec] * 3,
        out_shape=[jax.ShapeDtypeStruct((r, c), F32)] * 3, compiler_params=_params(1))(w, g, m, v)


SUM_TILE = 128


FLAT_BLOCK = 257 * 1024


def _tiles(shape, axis, lead=0):
    if len(shape) == 1:
        count = shape[0] // FLAT_BLOCK
        return (FLAT_BLOCK,), count, lambda pos, *lead_idx: (sum(k * count for k in lead_idx) + pos,)
    r, c = shape
    blk = (SUM_TILE, c) if axis == 0 else (r, SUM_TILE)
    count = shape[axis] // SUM_TILE

    def index(pos, *lead_idx):
        return tuple(lead_idx) + ((pos, 0) if axis == 0 else (0, pos))

    return (None,) * lead + blk, count, index


def _adamw_halves(w, g_mine, g_theirs, m, v, axis, name):
    blk, count, index = _tiles(w.shape, axis)
    per_half = count // 2

    def body(w_ref, a_ref, b_ref, m_ref, v_ref, g_ref, d_ref, mo_ref, vo_ref):
        is_mine = pl.program_id(0) // per_half == lax.axis_index("c")
        g = jnp.where(is_mine, a_ref[...], b_ref[...])
        g_ref[...] = g
        d_ref[...], mo_ref[...], vo_ref[...] = _adamw_math(w_ref[...], g, m_ref[...], v_ref[...])

    spec = pl.BlockSpec(blk, lambda i: index(i))
    half = pl.BlockSpec(blk, lambda i: index(i % per_half))
    return pl.pallas_call(
        body, name=name, grid=(count,), in_specs=[spec, half, half, spec, spec], out_specs=[spec] * 4,
        out_shape=[jax.ShapeDtypeStruct(w.shape, F32)] * 4, compiler_params=_params(1))(w, g_mine, g_theirs, m, v)


def _chip_sum(blocks, from_sibling, axis, name):
    flat = blocks.ndim == 1
    blk, count, index = _tiles((from_sibling.shape[0] // NCHIP,) if flat else from_sibling.shape[1:], axis, lead=1)

    def body(lo_ref, hi_ref, p_ref, o32, o16):
        mine = jnp.where(lax.axis_index("c") == 0, lo_ref[...], hi_ref[...])
        acc = mine + p_ref[...]
        o32[...] = acc
        o16[...] = acc.astype(BF16)

    half = pl.BlockSpec(blk, lambda k, i: index(i, k))
    if flat:
        lo = pl.BlockSpec(blk, lambda k, i: (2 * count * k + i,))
        hi = pl.BlockSpec(blk, lambda k, i: (2 * count * k + count + i,))
    else:
        lo, hi = half, pl.BlockSpec(blk, lambda k, i: index(i + count, k))
    return pl.pallas_call(
        body, name=name, grid=(NCHIP, count), in_specs=[lo, hi, half], out_specs=[half, half],
        out_shape=[jax.ShapeDtypeStruct(from_sibling.shape, F32), jax.ShapeDtypeStruct(from_sibling.shape, BF16)],
        compiler_params=_params(2))(blocks, blocks, from_sibling)


def _mesh_sum(own, parts, axis, name):
    blk, count, index = _tiles(own.shape, axis)
    n = NCHIP - 1

    def body(a_ref, *refs):
        acc = a_ref[...]
        for k in range(n):
            acc = acc + refs[k][...].astype(F32)
        refs[n][...] = acc

    spec = pl.BlockSpec(blk, lambda i: index(i))
    if own.ndim == 1:
        part = [pl.BlockSpec(blk, lambda i, k=k: (k * count + i,)) for k in range(n)]
    else:
        part = [pl.BlockSpec((None,) + blk, lambda i, k=k: (k,) + index(i)) for k in range(n)]
    return pl.pallas_call(
        body, name=name, grid=(count,), in_specs=[spec] + part,
        out_specs=spec, out_shape=jax.ShapeDtypeStruct(own.shape, F32),
        compiler_params=_params(1))(own, *([parts] * n))


def _sum_stack(parts, name):
    n = parts.shape[0]

    def body(p_ref, o_ref):
        acc = p_ref[0]
        for k in range(1, n):
            acc = acc + p_ref[k]
        o_ref[...] = acc

    return pl.pallas_call(body, name=name, out_shape=jax.ShapeDtypeStruct(parts.shape[1:], F32))(parts)


def _coords():
    return lax.axis_index("x"), lax.axis_index("y"), lax.axis_index("c")


def _chip(who):
    return 2 * who[0] + who[1]


def _flip(who, mask):
    return tuple((1 - v) if b else v for v, b in zip(who, mask))


def _transfer(transfers, t, I, O, ssem, rsem, receiving):
    tr, me = transfers[t], _coords()
    peer = _flip(me, tr["mask"])
    return pltpu.make_async_remote_copy(
        src_ref=tr["src"](I, O, me), dst_ref=tr["dst"](I, O, peer if receiving else me),
        send_sem=ssem.at[t], recv_sem=rsem.at[t], device_id=peer, device_id_type=MESH)


def _start_transfers(transfers, I, O, ssem, rsem, onward):
    arrived = set()
    for t, tr in enumerate(transfers):
        after = tr.get("after")
        if (after is not None) != onward:
            continue
        if after is not None and after not in arrived:
            _transfer(transfers, after, I, O, ssem, rsem, True).wait_recv()
            arrived.add(after)
        _transfer(transfers, t, I, O, ssem, rsem, False).start()


def _finish_transfers(transfers, I, O, ssem, rsem):
    passed_on = {tr["after"] for tr in transfers if tr.get("after") is not None}
    for t in range(len(transfers)):
        if t not in passed_on:
            _transfer(transfers, t, I, O, ssem, rsem, True).wait_recv()
    for t in range(len(transfers)):
        _transfer(transfers, t, I, O, ssem, rsem, False).wait_send()


def _own_copies(own, I, O, stage, lsem, leg):
    for n, (src, dst) in enumerate(own):
        me = _coords()
        bring =pltpu.make_async_copy(src(I, O, me), stage[n], lsem.at[2 * n])
        put = pltpu.make_async_copy(stage[n], dst(I, O, me), lsem.at[2 * n + 1])
        if leg == 0:
            bring.start()
        elif leg == 1:
            bring.wait()
            put.start()
        else:
            put.wait()


def _own_scratch(own, ins):
    return [pltpu.VMEM(ins[n].shape, ins[n].dtype) for n in range(len(own))], pltpu.SemaphoreType.DMA((max(2 * len(own), 1),))


def _exchange(name, ins, outs, transfers, own=()):
    ni, no = len(ins), len(outs)
    nt = len(transfers)
    stages, stage_sems = _own_scratch(own, ins)

    def body(*refs):
        I, O = refs[:ni], refs[ni:ni + no]
        ssem, rsem, lsem = refs[ni + no:ni + no + 3]
        stage = refs[ni + no + 3:]
        _own_copies(own, I, O, stage, lsem, 0)
        _start_transfers(transfers, I, O, ssem, rsem, False)
        _own_copies(own, I, O, stage, lsem, 1)
        _start_transfers(transfers, I, O, ssem, rsem, True)
        _finish_transfers(transfers, I, O, ssem, rsem)
        _own_copies(own, I, O, stage, lsem, 2)

    hbm = pl.BlockSpec(memory_space=pltpu.HBM)
    return pl.pallas_call(
        body, name=name, in_specs=[hbm] * ni, out_specs=[hbm] * no,
        out_shape=[jax.ShapeDtypeStruct(s, d) for s, d in outs],
        scratch_shapes=[pltpu.SemaphoreType.DMA((nt,)), pltpu.SemaphoreType.DMA((nt,)), stage_sems] + stages,
        compiler_params=pltpu.CompilerParams(has_side_effects=True, vmem_limit_bytes=VMEM_LIMIT))(*ins)


CHIP_MASKS = [(0, 1, 0), (1, 0, 0), (1, 1, 0)]
SIBLING = (0, 0, 1)


def _half(shape2d, axis, which):
    n = shape2d[axis] // 2
    cut = pl.ds(pl.multiple_of(which * n, n), n)
    return (cut, slice(None)) if axis == 0 else (slice(None), cut)


class _Riding:
    def __init__(self, transfers, ins, outs, own=()):
        self.transfers, self.ins, self.outs, self.own = transfers, list(ins), list(outs), list(own)
        hbm = pl.BlockSpec(memory_space=pltpu.HBM)
        self.in_specs, self.out_specs = [hbm] * len(self.ins), [hbm] * len(self.outs)
        self.out_shape = [jax.ShapeDtypeStruct(s, d) for s, d in self.outs]
        stages, stage_sems = _own_scratch(self.own, self.ins)
        self.scratch = [pltpu.SemaphoreType.DMA((max(len(transfers), 1),))] * 2 + [stage_sems] + stages

    def alone(self, name):
        return _exchange(name, self.ins, self.outs, self.transfers, self.own)

    def hooks(self, I, O, ssem, rsem, lsem, *stage, first, middle, last):
        tr, own = self.transfers, self.own

        @pl.when(first)
        def _():
            _own_copies(own, I, O, stage, lsem, 0)
            _start_transfers(tr, I, O, ssem, rsem, False)

        if own or any(t.get("after") is not None for t in tr):
            @pl.when(middle)
            def _():
                _own_copies(own, I, O, stage, lsem, 1)
                _start_transfers(tr, I, O, ssem, rsem, True)

        def at_end():
            @pl.when(last)
            def _():
                _finish_transfers(tr, I, O, ssem, rsem)
                _own_copies(own, I, O, stage, lsem, 2)

        return at_end


def _stretch(n, pos):
    return (pl.ds(pos * n if isinstance(pos, int) else pl.multiple_of(pos * n, n), n),)


def _gather_plan(shards, axes):
    def half(a, who):
        if shards[a].ndim == 1:
            return _stretch(shards[a].shape[0] // 2, who[2])
        return _half(shards[a].shape, axes[a], who[2])

    def landed(a, chip, who):
        if shards[a].ndim == 1:
            return _stretch(shards[a].shape[0] // 2, 2 * chip + who[2])
        return (chip,) + half(a, who)

    over_ici, onward = [], []
    for a in range(len(shards)):
        for mask in CHIP_MASKS:
            over_ici.append(dict(
                mask=mask,
                src=lambda I, O, me, a=a: I[a].at[half(a, me)],
                dst=lambda I, O, who, a=a: O[a].at[landed(a, _chip(who), who)]))
            onward.append(dict(
                mask=SIBLING, after=len(over_ici) - 1,
                src=lambda I, O, me, a=a, mask=mask: O[a].at[landed(a, _chip(_flip(me, mask)), me)],
                dst=lambda I, O, who, a=a, mask=mask: O[a].at[landed(a, _chip(_flip(who, mask)), who)]))
    outs = [((NCHIP * s.shape[0],) if s.ndim == 1 else (NCHIP,) + s.shape, s.dtype) for s in shards]

    def whole(a, chip):
        return _stretch(shards[a].shape[0], chip) if shards[a].ndim == 1 else (chip,)

    own = [(lambda I, O, me, a=a: I[a], lambda I, O, me, a=a: O[a].at[whole(a, _chip(me))])
           for a in range(len(shards))]
    return over_ici + onward, outs, own


def _gather_shards(shards, axes):
    transfers, outs, own = _gather_plan(shards, axes)
    return _exchange("gather_weights", shards, outs, transfers, own)


def _to_sibling(arrs, name):
    transfers = [dict(mask=SIBLING, src=lambda I, O, me, a=a: I[a], dst=lambda I, O, who, a=a: O[a])
                 for a in range(len(arrs))]
    return _exchange(name, arrs, [(t.shape, t.dtype) for t in arrs], transfers)


def _halves_plan(blocks, axes):
    def cut(a, which):
        return (slice(None),) + _half(blocks[a].shape[1:], axes[a], which)

    transfers, outs = [], []
    for a, (b, ax) in enumerate(zip(blocks, axes)):
        if b.ndim == 1:
            h = b.shape[0] // NCHIP // 2
            for k in range(NCHIP):
                transfers.append(dict(mask=SIBLING,
                                      src=lambda I, O, me, a=a, k=k, h=h: I[a].at[_stretch(h, 2 * k + 1 - me[2])],
                                      dst=lambda I, O, who, a=a, k=k, h=h: O[a].at[_stretch(h, k)]))
            outs.append(((NCHIP * h,), b.dtype))
        else:
            transfers.append(dict(mask=SIBLING, src=lambda I, O, me, a=a: I[a].at[cut(a, 1 - me[2])],
                                  dst=lambda I, O, who, a=a: O[a]))
            shape = list(b.shape)
            shape[ax + 1] //= 2
            outs.append((tuple(shape), b.dtype))
    return transfers, outs


def _scatter_plan(tb):
    def slot(a, k):
        return (k,) if tb[a].ndim == 3 else _stretch(tb[a].shape[0] // NCHIP, k)

    transfers = []
    for a in range(len(tb)):
        for n, mask in enumerate(CHIP_MASKS):
            transfers.append(dict(
                mask=mask,
                src=lambda I, O, me, a=a, mask=mask: I[a].at[slot(a, _chip(_flip(me, mask)))],
                dst=lambda I, O, who, a=a, n=n: O[a].at[slot(a, n)]))
    outs = [((3,) + t.shape[1:] if t.ndim == 3 else (3 * (t.shape[0] // NCHIP),), t.dtype) for t in tb]
    return transfers, outs


def _scatter_chip_sums(tb):
    transfers, outs = _scatter_plan(tb)
    return _exchange("scatter_grads", tb, outs, transfers)


def _gather_small(vec):
    def slot(who):
        return 4 * who[0] + 2 * who[1] + who[2]

    masks = [(m >> 2 & 1, m >> 1 & 1, m & 1) for m in range(1, 8)]
    transfers = [dict(mask=mask, src=lambda I, O, me: I[0], dst=lambda I, O, who: O[0].at[slot(who)])
                 for mask in masks]
    own = [(lambda I, O, me: I[0], lambda I, O, me: O[0].at[slot(me)])]
    return _exchange("gather_small", [vec], [((8,) + vec.shape, vec.dtype)], transfers, own)[0]


def _rope_tables(positions):
    half = ROT // 2
    inv_freq = jnp.power(jnp.float32(THETA), -jnp.arange(0, ROT, 2, dtype=F32) / ROT)
    ang = positions.astype(F32)[:, None] * inv_freq[None, :]
    cos, sin = jnp.cos(ang), jnp.sin(ang)
    one, zero, z8 = jnp.ones((S, HD - ROT), F32), jnp.zeros((S, HD - ROT), F32), jnp.zeros((S, half), F32)
    c = jnp.concatenate([cos, cos, one], axis=1)
    a = jnp.concatenate([-sin, z8, zero], axis=1)
    b = jnp.concatenate([z8, sin, zero], axis=1)
    return tuple(jnp.tile(t, (1, 2)) for t in (c, a, b))


def _tile_heads(g, w):
    return jnp.tile(g.reshape(1, HD), (1, w // HD))


def _fold_heads(dg):
    return dg.reshape(-1, HD).sum(axis=0)


def _pad_lanes(a):
    return jnp.pad(a, ((0, 0), (0, LANES - a.shape[1])))


def _local_step(x, target, positions, wt, fetch, late_weights, begin_reduce):
    rope = _rope_tables(positions)
    w1t = wt["w_in_a_t"]
    f_row = 3 * D // LANES
    wg_t = w1t[3 * D + NH:]
    in_b_block = lambda c: pl.BlockSpec((None, TN_, TN_), lambda j, i: (c, j, 0))
    b_pad = _pad_lanes(wt["b_forget"].reshape(1, NH))
    qg_a, kg_a = _tile_heads(wt["qnorm_a_g"], D), _tile_heads(wt["knorm_a_g"], D)
    qg_b, kg_b = _tile_heads(wt["qnorm_b_g"], D), _tile_heads(wt["knorm_b_g"], KVW)
    norm_a, kv_g, norm_b = wt["norm_a_g"].reshape(1, D), wt["kv_norm_g"].reshape(1, D), wt["norm_b_g"].reshape(1, D)
    sinks_t = jnp.repeat(wt["sinks"].reshape(1, NH), HD, axis=1)

    (u_a,) = _rmsnorm_fwd(x, [norm_a], "norm_a")
    qkv = _mm("proj_a", S, 3 * D, [(u_a, _a_rows(D), w1t, _b_rows(D), NT)])
    fpad = _mm("proj_f", S, LANES, [(u_a, _a_rows(D), w1t, _b_rows(D, row0=f_row, tn=LANES), NT)], tn=LANES)
    gate_a = _mm("proj_gate_a", S, D, [(u_a, _a_rows(D), wg_t, _b_rows(D), NT)])
    q_a, k_a, v_a = _a_post(qkv, qg_a, kg_a)
    ct = _forget_cumsum(fpad, b_pad)
    ct2 = ct[:NH].reshape(NH // 2, 2, S)
    o_a, lse_a, y_a, fetched = _fox_fwd(q_a, k_a, v_a, ct2, gate_a, fetch)
    wt = {**wt, **late_weights(fetched)}
    w_in_b = wt["w_in_b"]
    h1 = _mm("out_a", S, D, [(y_a, _a_rows(D), wt["w_out_a"], _b_cols(D), None)], add=x)
    u_kv, u_b = _rmsnorm_fwd(h1, [kv_g, norm_b], "norm_b")
    kv = _mm("proj_kv", S, 2 * KVW, [(u_kv, _a_rows(D), wt["w_kv"], _b_cols(D), None)])
    pb = _mm("proj_b", S, 2 * D,
             [(u_b, _a_rows(D), w_in_b, pl.BlockSpec((None, D, TN_), lambda j, i: (j, 0, 0)), None)])
    q_b, kdup, vdup = _b_post(pb, kv, qg_b, kg_b, rope)
    gate_b_col = D // LANES
    o_b, lse_b, y_b = _swa_fwd(q_b, kdup, vdup, sinks_t, pb, gate_b_col)
    out = _mm("out_b", S, D, [(y_b, _a_rows(D), wt["w_out_b"], _b_cols(D), None)], add=h1)
    d_out, d_out_b, sq = _loss_head(out, target)

    g = {}
    g["w_out_b"] = _mm("dw_out_b", D, D, [(y_b, _a_cols(S), d_out_b, _b_cols(S), TN)])
    d_y_b = _mm("dy_b", S, D, [(d_out_b, _a_rows(D), wt["w_out_b"], _b_rows(D), NT)])
    dq_b, dkdup, dvdup, dsk, d_gate_b = _swa_bwd(q_b, kdup, vdup, sinks_t, o_b, lse_b, d_y_b, pb, gate_b_col)
    g["sinks"] = dsk[0, ::HD]
    d_qb_raw, dg = _headnorm_bwd(pb, 0, qg_b, dq_b, rope, "qnorm_b_bwd")
    g["qnorm_b_g"] = _fold_heads(dg)
    d_pb = [d_qb_raw, d_qb_raw, d_gate_b, d_gate_b]
    g["w_in_b"] = jnp.concatenate([
        _mm("dw_in_b_q", D, D, [(u_b, _a_cols(S), d_qb_raw, _b_cols(S), TN)], stacked=True),
        _mm("dw_in_b_gate", D, D, [(u_b, _a_cols(S), d_gate_b, _b_cols(S), TN)], stacked=True)], axis=0)
    d_u_b = _mm("du_b", S, D, [(d_pb[c], _a_rows(TN_, col=c % 2), w_in_b, in_b_block(c), NT) for c in range(NCHIP)])
    d_kv, dg = _kv_bwd(dkdup, dvdup, kv, kg_b, rope)
    g["knorm_b_g"] = _fold_heads(dg)
    g["w_kv"] = _mm("dw_kv", D, 2 * KVW, [(u_kv, _a_cols(S), d_kv, _b_cols(S), TN)])
    d_u_kv = _mm("du_kv", S, D, [(d_kv, _a_rows(2 * KVW), wt["w_kv"], _b_rows(2 * KVW), NT)])
    d_h1, d_h1_b, g["kv_norm_g"], g["norm_b_g"] = _rmsnorm_bwd(h1, [kv_g, norm_b], [d_u_kv, d_u_b], d_out, "norm_b_bwd")
    g["w_out_a"] = _mm("dw_out_a", D, D, [(y_a, _a_cols(S), d_h1_b, _b_cols(S), TN)])
    late = {n: g[n] for n in LATE}
    d_y_a, halves = _mm("dy_a", S, D, [(d_h1_b, _a_rows(D), wt["w_out_a"], _b_rows(D), NT)],
                        riding=begin_reduce(late))
    riding, so_far = begin_reduce(late, halves)
    dq_a, dk_a, dv_a, dct, d_gate_a, arrived = _fox_bwd(q_a, k_a, v_a, ct2, o_a, lse_a, d_y_a, gate_a, riding)
    dct_pad = jnp.pad(dct.reshape(NH, S), ((0, LANES - NH), (0, 0)))
    d_f, db = _forget_bwd(dct_pad, fpad, b_pad)
    g["b_forget"] = db[0, :NH]
    d_q_raw, dg = _headnorm_bwd(qkv, 0, qg_a, dq_a, None, "qnorm_a_bwd")
    g["qnorm_a_g"] = _fold_heads(dg)
    d_k_raw, dg = _headnorm_bwd(qkv, 1, kg_a, dk_a, None, "knorm_a_bwd")
    g["knorm_a_g"] = _fold_heads(dg)
    rows, gw = 4 * D + NH, None
    for n, t, row0 in (("q", d_q_raw, 0), ("k", d_k_raw, D), ("v", dv_a, 2 * D)):
        gw = _mm("dw_in_a_" + n, D, D, [(t, _a_cols(S), u_a, _b_cols(S), TN)], rows_of=(gw, rows, row0))
    gw = _mm("dw_in_a_f", LANES, D, [(d_f, _a_cols(S, tm=LANES), u_a, _b_cols(S), TN)], tm=LANES,
             rows_of=(gw, rows, 3 * D))
    g["w_in_a"] = _mm("dw_in_a_gate", D, D, [(d_gate_a, _a_cols(S), u_a, _b_cols(S), TN)],
                      rows_of=(gw, rows, 3 * D + NH))
    first = {"w_in_a": g["w_in_a"]}
    riding, so_far_first = begin_reduce(first, begin_reduce(first).alone("sibling_halves_w_in_a"))
    d_u_a, arrived_first = _mm("du_a", S, D, [
        (d_q_raw, _a_rows(D), w1t, _b_cols(D, row=0), None), (d_k_raw, _a_rows(D), w1t, _b_cols(D, row=1), None),
        (dv_a, _a_rows(D), w1t, _b_cols(D, row=2), None), (d_gate_a, _a_rows(D), wg_t, _b_cols(D), None),
        (d_f, _a_rows(LANES), w1t, _b_cols(LANES, row=f_row), None)], riding=riding)
    d_x, _, g["norm_a_g"] = _rmsnorm_bwd(x, [norm_a], [d_u_a], d_h1, "norm_a_bwd")
    return sq, d_x, g, (list(so_far_first) + list(so_far), list(arrived_first) + list(arrived))


BIG = ["w_in_a", "w_out_a", "w_kv", "w_in_b", "w_out_b"]
LATE = BIG[1:]
SPLIT = {"w_in_a": None, "w_out_a": 0, "w_kv": 0, "w_in_b": 0, "w_out_b": 0}
SMALL = ["norm_a_g", "b_forget", "qnorm_a_g", "knorm_a_g", "kv_norm_g", "knorm_b_g", "norm_b_g", "qnorm_b_g", "sinks"]
NAMES = ["norm_a_g", "w_in_a", "b_forget", "qnorm_a_g", "knorm_a_g", "w_out_a", "kv_norm_g", "w_kv", "knorm_b_g",
         "norm_b_g", "w_in_b", "qnorm_b_g", "sinks", "w_out_b"]


def _pack(vals):
    flat = []
    for v in vals:
        v = v.reshape(-1)
        flat.append(jnp.pad(v, (0, -v.shape[0] % LANES)))
    flat = jnp.concatenate(flat)
    flat = jnp.pad(flat, (0, -flat.shape[0] % (8 * LANES)))
    return flat.reshape(-1, LANES)


def _unpack(packed, shapes):
    flat, out, off = packed.reshape(-1), [], 0
    for s in shapes:
        n = int(np.prod(s))
        out.append(flat[off:off + n].reshape(s))
        off += n + (-n % LANES)
    return out


def kernel(x, positions, norm_a_g, w_in_a, b_forget, qnorm_a_g, knorm_a_g, w_out_a, kv_norm_g, w_kv, knorm_b_g, norm_b_g, w_in_b, qnorm_b_g, sinks, w_out_b, loss_target, m_norm_a_g, m_w_in_a, m_b_forget, m_qnorm_a_g, m_knorm_a_g, m_w_out_a, m_kv_norm_g, m_w_kv, m_knorm_b_g, m_norm_b_g, m_w_in_b, m_qnorm_b_g, m_sinks, m_w_out_b, v_norm_a_g, v_w_in_a, v_b_forget, v_qnorm_a_g, v_knorm_a_g, v_w_out_a, v_kv_norm_g, v_w_kv, v_knorm_b_g, v_norm_b_g, v_w_in_b, v_qnorm_b_g, v_sinks, v_w_out_b):
    w = dict(norm_a_g=norm_a_g, w_in_a=w_in_a, b_forget=b_forget, qnorm_a_g=qnorm_a_g, knorm_a_g=knorm_a_g,
             w_out_a=w_out_a, kv_norm_g=kv_norm_g, w_kv=w_kv, knorm_b_g=knorm_b_g, norm_b_g=norm_b_g,
             w_in_b=w_in_b, qnorm_b_g=qnorm_b_g, sinks=sinks, w_out_b=w_out_b)
    m = dict(norm_a_g=m_norm_a_g, w_in_a=m_w_in_a, b_forget=m_b_forget, qnorm_a_g=m_qnorm_a_g, knorm_a_g=m_knorm_a_g,
             w_out_a=m_w_out_a, kv_norm_g=m_kv_norm_g, w_kv=m_w_kv, knorm_b_g=m_knorm_b_g, norm_b_g=m_norm_b_g,
             w_in_b=m_w_in_b, qnorm_b_g=m_qnorm_b_g, sinks=m_sinks, w_out_b=m_w_out_b)
    v = dict(norm_a_g=v_norm_a_g, w_in_a=v_w_in_a, b_forget=v_b_forget, qnorm_a_g=v_qnorm_a_g, knorm_a_g=v_knorm_a_g,
             w_out_a=v_w_out_a, kv_norm_g=v_kv_norm_g, w_kv=v_w_kv, knorm_b_g=v_knorm_b_g, norm_b_g=v_norm_b_g,
             w_in_b=v_w_in_b, qnorm_b_g=v_qnorm_b_g, sinks=v_sinks, w_out_b=v_w_out_b)
    my_chip = 2 * lax.axis_index("x") + lax.axis_index("y")

    def shard2d(t, n):
        if n == "w_in_a":
            return jnp.transpose(t, (2, 0, 1)).reshape(-1)
        return t.reshape(t.shape[-2:])

    def unflat(t, n):
        return jnp.transpose(t.reshape(-1, 1, D), (1, 2, 0)) if n == "w_in_a" else t.reshape(w[n].shape)

    w2d = {n: shard2d(w[n], n) for n in BIG}

    norm_a_rows = jnp.broadcast_to(norm_a_g.reshape(1, D // NCHIP), (16, D // NCHIP))
    w1t, norm_rows = _gather_shards([w2d["w_in_a"].astype(BF16), norm_a_rows], [SPLIT["w_in_a"], 0])
    wt = {"w_in_a_t": w1t.reshape(-1, D), "norm_a_g": norm_rows[:, 0, :].reshape(1, D)}
    for n in SMALL[1:]:
        wt[n] = w[n]
    late_shards = [w2d[n].astype(BF16) for n in LATE]
    late_axes = [SPLIT[n] for n in LATE]
    transfers, outs, own = _gather_plan(late_shards, late_axes)
    fetch = _Riding(transfers, late_shards, outs, own)

    def late_weights(fetched):
        return {n: t if n == "w_in_b" else t.reshape(-1, t.shape[2]) for n, t in zip(LATE, fetched)}

    def as_blocks(t):
        if t.ndim == 3:
            return t
        return t.reshape(-1) if t.shape[0] % (8 * NCHIP) else t.reshape(NCHIP, -1, t.shape[1])

    def begin_reduce(grads, halves=None):
        names = list(grads)
        axes = [SPLIT[n] for n in names]
        blocks = [as_blocks(grads[n]) for n in names]
        if halves is None:
            transfers, outs = _halves_plan(blocks, axes)
            return _Riding(transfers, blocks, outs)
        sums = [_chip_sum(blk, part, ax, "chip_sum_" + n) for n, ax, blk, part in zip(names, axes, blocks, halves)]
        bf16 = [s[1] for s in sums]
        transfers, outs = _scatter_plan(bf16)
        return _Riding(transfers, bf16, outs), [s[0] for s in sums]

    sq, d_x, g, (chip_f32, arrived) = _local_step(x[0], loss_target[0], positions, wt, fetch, late_weights,
                                                  begin_reduce)

    small_shapes = [(D,), (NH,), (HD,), (HD,), (D,), (HD,), (D,), (HD,), (NH,), (D,)]
    packed = _pack([g[n] for n in SMALL] + [sq])
    total = _sum_stack(_gather_small(packed), "sum_small")
    small_g = dict(zip(SMALL, _unpack(total, small_shapes)[:-1]))
    loss = 0.5 * jnp.sum(_unpack(total, small_shapes)[-1]) / D
    small_g["norm_a_g"] = lax.dynamic_slice(small_g["norm_a_g"], (my_chip * (D // NCHIP),), (D // NCHIP,))

    axes = [SPLIT[n] for n in BIG]
    halves = []
    for n, ax, t32, parts in zip(BIG, axes, chip_f32, arrived):
        if t32.ndim == 1:
            own = lax.dynamic_slice_in_dim(t32, my_chip * (t32.shape[0] // NCHIP), t32.shape[0] // NCHIP)
        else:
            own = lax.dynamic_index_in_dim(t32, my_chip, axis=0, keepdims=False)
        halves.append(_mesh_sum(own, parts, ax, "mesh_sum_" + n))
    sibling_done = _to_sibling(halves, "finished_halves")

    res = {}
    for n, ax, mine_half, their_half in zip(BIG, axes, halves, sibling_done):
        out4 = _adamw_halves(w2d[n], mine_half, their_half, shard2d(m[n], n), shard2d(v[n], n), ax, "adamw_" + n)
        res[n] = tuple(unflat(t, n) for t in out4)
    sm_g = _pack([small_g[n] for n in SMALL])
    sm = [_pack([d[n] for n in SMALL]) for d in (w, m, v)]
    sm_out = _adamw(sm[0], sm_g, sm[1], sm[2], "adamw_small")
    sm_shapes = [w[n].shape for n in SMALL]
    unpacked = [_unpack(t, sm_shapes) for t in (sm_g,) + tuple(sm_out)]
    for i, n in enumerate(SMALL):
        res[n] = tuple(u[i] for u in unpacked)

    outs = [loss, d_x[None]]
    for k in range(4):
        outs += [res[n][k] for n in NAMES]
    return tuple(outs)
```

```python
import numpy as np
import jax
import jax.numpy as jnp
from jax import lax
from jax.experimental import pallas as pl
from jax.experimental.pallas import tpu as pltpu

F32, BF16 = jnp.float32, jnp.bfloat16
S, D, HD, NH, NKV = 2048, 1024, 64, 16, 4
KVW = NKV * HD
WINDOW = 128
ROT = HD // 4
THETA = 500000.0
EPS = 1e-6
SCALE = HD ** -0.5
LANES = 128
NEG = -1e30
VMEM_LIMIT = 48 * 2 ** 20
ROWS = 256
ATT = 512
SWQ = 4
NCHIP = 4
ADAM_LR, ADAM_B1, ADAM_B2, ADAM_EPS, ADAM_WD, ADAM_STEP = 0.001, 0.9, 0.999, 1e-08, 0.01, 10
NT = (((1,), (1,)), ((), ()))
TN = (((0,), (0,)), ((), ()))
MESH = pl.DeviceIdType.MESH


def _params(n):
    return pltpu.CompilerParams(dimension_semantics=("arbitrary",) * n, vmem_limit_bytes=VMEM_LIMIT)


def _dot(a, b, dims=None):
    if dims is None:
        return jnp.dot(a, b, preferred_element_type=F32)
    return lax.dot_general(a, b, dims, preferred_element_type=F32)


def _dot_split(a, b, n):
    out, rest = None, a
    for _ in range(n):
        hi = rest.astype(BF16)
        term = _dot(hi, b)
        out = term if out is None else out + term
        rest = rest - hi.astype(F32)
    return out


def _seg_mat(w):
    e = (np.arange(w)[:, None] // HD == np.arange(LANES)[None, :]).astype(np.float32)
    return jnp.asarray(e, BF16)


def _spread(r, w):
    head = lax.broadcasted_iota(jnp.int32, (2 * LANES, w), 1) >> 6
    row = lax.broadcasted_iota(jnp.int32, (2 * LANES, w), 0)
    et2 = jnp.where(head == (row & (LANES - 1)), 1.0, 0.0).astype(BF16)
    hi = r.astype(BF16)
    lo = (r - hi.astype(F32)).astype(BF16)
    return _dot(jnp.concatenate([hi, lo], axis=1), et2)


def _head_rstd(x, e):
    ss = _dot_split(x * x, e, 2)
    return _spread(lax.rsqrt(ss * (1.0 / HD) + EPS), x.shape[1])


def _rope(x, c, a, b):
    w = x.shape[1]
    return x * c + pltpu.roll(x, w - ROT // 2, 1) * a + pltpu.roll(x, ROT // 2, 1) * b


def _rope_t(dy, c, a, b):
    w = dy.shape[1]
    return dy * c + pltpu.roll(dy * b, w - ROT // 2, 1) + pltpu.roll(dy * a, ROT // 2, 1)


def _sigmoid(x):
    return 1.0 / (1.0 + jnp.exp(-x))


def _row_spec(shape, ts):
    nd = len(shape)
    if shape[0] == S:
        return pl.BlockSpec((ts,) + tuple(shape[1:]), lambda i: (i,) + (0,) * (nd - 1))
    return pl.BlockSpec(tuple(shape), lambda i: (0,) * nd)


def _rows_call(body, name, ins, outs, ts=ROWS):
    return pl.pallas_call(
        body, name=name, grid=(S // ts,),
        in_specs=[_row_spec(a.shape, ts) for a in ins],
        out_specs=[_row_spec(s, ts) for s, _ in outs],
        out_shape=[jax.ShapeDtypeStruct(s, d) for s, d in outs],
        compiler_params=_params(1))(*ins)


def _col_spec(ts, w, col):
    return pl.BlockSpec((ts, w), lambda i: (i, col))


TM = TN_ = 512
TM_TOKENS = 1024


def _mm(name, m, n, terms, out_dtype=F32, add=None, tm=None, tn=TN_, stacked=False, riding=None, rows_of=None):
    nterm = len(terms)
    if tm is None:
        tm = TM_TOKENS if m == S else TM
    nj, ni_ = n // tn, m // tm
    n_in = 2 * nterm + (add is not None) + (rows_of is not None and rows_of[0] is not None)
    r_in, r_out = (len(riding.ins), len(riding.outs)) if riding is not None else (0, 0)

    def body(*refs):
        if riding is not None:
            j, i = pl.program_id(0), pl.program_id(1)
            at_end = riding.hooks(refs[n_in:n_in + r_in], refs[n_in + r_in + 1:n_in + r_in + 1 + r_out],
                                  *refs[n_in + r_in + 1 + r_out:], first=(j == 0) & (i == 0),
                                  middle=(j == nj // 2) & (i == 0), last=(j == nj - 1) & (i == ni_ - 1))
        acc = None
        for t in range(nterm):
            part = _dot(refs[2 * t][...], refs[2 * t + 1][...], terms[t][4])
            acc = part if acc is None else acc + part
        if add is not None:
            acc = acc + refs[2 * nterm][...]
        refs[n_in + r_in][...] = acc.astype(out_dtype)
        if riding is not None:
            at_end()

    tile = pl.BlockSpec((tm, tn), lambda j, i: (i, j))
    ins, specs = [], []
    for a, a_spec, b, b_spec, _ in terms:
        ins += [a, b]
        specs += [a_spec, b_spec]
    if add is not None:
        ins.append(add)
        specs.append(tile)
    out_spec = pl.BlockSpec((None, tm, tn), lambda j, i: (j, i, 0)) if stacked else tile
    out_shape = jax.ShapeDtypeStruct((nj, m, tn) if stacked else (m, n), out_dtype)
    if rows_of is not None:
        taller, rows, row0 = rows_of
        out_spec = pl.BlockSpec((pl.Element(tm), pl.Element(tn)), lambda j, i: (
            pl.multiple_of(row0 + i * tm, 8), pl.multiple_of(j * tn, LANES)))
        out_shape = jax.ShapeDtypeStruct((rows, n), out_dtype)
        alias = {}
        if taller is not None:
            ins.append(taller)
            specs.append(pl.BlockSpec(memory_space=pltpu.HBM))
            alias = {len(ins) - 1: 0}
        return pl.pallas_call(body, name=name, grid=(nj, ni_), in_specs=specs, out_specs=out_spec,
                              out_shape=out_shape, input_output_aliases=alias, compiler_params=_params(2))(*ins)
    if riding is None:
        return pl.pallas_call(body, name=name, grid=(nj, ni_), in_specs=specs, out_specs=out_spec,
                              out_shape=out_shape, compiler_params=_params(2))(*ins)
    res = pl.pallas_call(
        body, name=name, grid=(nj, ni_), in_specs=specs + riding.in_specs,
        out_specs=[out_spec] + riding.out_specs, out_shape=[out_shape] + riding.out_shape,
        scratch_shapes=riding.scratch, compiler_params=_params(2))(*ins, *riding.ins)
    return res[0], res[1:]


def _a_rows(k, col=0, tm=TM_TOKENS):
    return pl.BlockSpec((tm, k), lambda j, i: (i, col))


def _a_cols(k, tm=TM):
    return pl.BlockSpec((k, tm), lambda j, i: (0, i))


def _b_cols(k, row=0, col0=0, tn=TN_):
    return pl.BlockSpec((k, tn), lambda j, i: (row, col0 + j))


def _b_rows(k, row0=0, tn=TN_):
    return pl.BlockSpec((tn, k), lambda j, i: (row0 + j, 0))


def _rmsnorm_fwd(x, gains, name):
    def body(*refs):
        xv = refs[0][...]
        r = lax.rsqrt(jnp.mean(xv * xv, axis=-1, keepdims=True) + EPS)
        xh = xv * r
        for n in range(len(gains)):
            refs[1 + len(gains) + n][...] = (xh * refs[1 + n][...]).astype(BF16)

    return _rows_call(body, name, [x] + list(gains), [((S, D), BF16)] * len(gains))


def _rmsnorm_bwd(x, gains, dus, dres, name):
    n = len(gains)

    def body(*refs):
        x_ref, g_refs, du_refs, dres_ref = refs[0], refs[1:1 + n], refs[1 + n:1 + 2 * n], refs[1 + 2 * n]
        dx_ref, dxb_ref, dg_refs = refs[2 + 2 * n], refs[3 + 2 * n], refs[4 + 2 * n:]
        xv = x_ref[...]
        r = lax.rsqrt(jnp.mean(xv * xv, axis=-1, keepdims=True) + EPS)
        xh = xv * r
        gy = None
        for m in range(n):
            du = du_refs[m][...]
            part = jnp.sum(du * xh, axis=0, keepdims=True)

            @pl.when(pl.program_id(0) == 0)
            def _(m=m, part=part):
                dg_refs[m][...] = part

            @pl.when(pl.program_id(0) != 0)
            def _(m=m, part=part):
                dg_refs[m][...] += part

            t = du * g_refs[m][...]
            gy = t if gy is None else gy + t
        dx = dres_ref[...] + r * (gy - xh * jnp.mean(gy * xh, axis=-1, keepdims=True))
        dx_ref[...] = dx
        dxb_ref[...] = dx.astype(BF16)

    outs = [((S, D), F32), ((S, D), BF16)] + [((1, D), F32)] * n
    return _rows_call(body, name, [x] + list(gains) + list(dus) + [dres], outs)


def _a_post(qkvg, qg, kg):
    e = _seg_mat(D)

    def body(q_ref, k_ref, v_ref, qg_ref, kg_ref, e_ref, qo, ko, vo):
        ev = e_ref[...]
        qv, kv = q_ref[...], k_ref[...]
        qo[...] = (qv * _head_rstd(qv, ev) * qg_ref[...] * SCALE).astype(BF16)
        ko[...] = (kv * _head_rstd(kv, ev) * kg_ref[...]).astype(BF16)
        vo[...] = v_ref[...].astype(BF16)

    whole = lambda a: pl.BlockSpec(a.shape, lambda i: (0, 0))
    return pl.pallas_call(
        body, name="a_post", grid=(S // ROWS,),
        in_specs=[_col_spec(ROWS, D, 0), _col_spec(ROWS, D, 1), _col_spec(ROWS, D, 2),
                  whole(qg), whole(kg), whole(e)],
        out_specs=[_col_spec(ROWS, D, 0)] * 3,
        out_shape=[jax.ShapeDtypeStruct((S, D), BF16)] * 3,
        compiler_params=_params(1))(qkvg, qkvg, qkvg, qg, kg, e)


def _tri(upper):
    r, c = np.arange(ROWS)[:, None], np.arange(ROWS)[None, :]
    return jnp.asarray((r <= c) if upper else (r >= c), BF16)


def _forget_cumsum(fpad, bpad):
    def body(f_ref, b_ref, u_ref, c_ref, carry):
        @pl.when(pl.program_id(0) == 0)
        def _():
            carry[...] = jnp.zeros_like(carry)

        lf = jax.nn.log_sigmoid(f_ref[...] + b_ref[...])
        blk = _dot_split(lf.T, u_ref[...], 3) + carry[:, 0:1]
        c_ref[...] = blk
        carry[...] = jnp.broadcast_to(blk[:, ROWS - 1:ROWS], carry.shape)

    return pl.pallas_call(
        body, name="forget_cumsum", grid=(S // ROWS,),
        in_specs=[pl.BlockSpec((ROWS, LANES), lambda i: (i, 0)), pl.BlockSpec((1, LANES), lambda i: (0, 0)),
                  pl.BlockSpec((ROWS, ROWS), lambda i: (0, 0))],
        out_specs=pl.BlockSpec((LANES, ROWS), lambda i: (0, i)),
        out_shape=jax.ShapeDtypeStruct((LANES, S), F32),
        scratch_shapes=[pltpu.VMEM((LANES, LANES), F32)],
        compiler_params=_params(1))(fpad, bpad, _tri(True))


def _forget_bwd(dct, fpad, bpad):
    nb = S // ROWS

    def body(dc_ref, f_ref, b_ref, l_ref, df_ref, db_ref, carry):
        @pl.when(pl.program_id(0) == 0)
        def _():
            carry[...] = jnp.zeros_like(carry)
            db_ref[...] = jnp.zeros_like(db_ref)

        blk = _dot_split(dc_ref[...], l_ref[...], 3) + carry[:, 0:1]
        carry[...] = jnp.broadcast_to(blk[:, 0:1], carry.shape)
        df = blk.T * _sigmoid(-(f_ref[...] + b_ref[...]))
        df_ref[...] = df.astype(BF16)
        db_ref[...] += jnp.sum(df, axis=0, keepdims=True)

    return pl.pallas_call(
        body, name="forget_bwd", grid=(nb,),
        in_specs=[pl.BlockSpec((LANES, ROWS), lambda i: (0, nb - 1 - i)),
                  pl.BlockSpec((ROWS, LANES), lambda i: (nb - 1 - i, 0)),
                  pl.BlockSpec((1, LANES), lambda i: (0, 0)), pl.BlockSpec((ROWS, ROWS), lambda i: (0, 0))],
        out_specs=[pl.BlockSpec((ROWS, LANES), lambda i: (nb - 1 - i, 0)), pl.BlockSpec((1, LANES), lambda i: (0, 0))],
        out_shape=[jax.ShapeDtypeStruct((S, LANES), BF16), jax.ShapeDtypeStruct((1, LANES), F32)],
        scratch_shapes=[pltpu.VMEM((LANES, LANES), F32)],
        compiler_params=_params(1))(dct, fpad, bpad, _tri(False))


def _headnorm_bwd(x, col, gain, dy, rope, name):
    e = _seg_mat(D)
    tabs = list(rope) if rope is not None else []

    def body(*refs):
        x_ref, g_ref, dy_ref, e_ref = refs[:4]
        dx_ref, dg_ref = refs[-2:]
        xv, dyv, ev = x_ref[...], dy_ref[...], e_ref[...]
        if rope is not None:
            c, a, b = (jnp.tile(t[...], (1, D // LANES)) for t in refs[4:7])
            dyv = _rope_t(dyv, c, a, b)
        r = _head_rstd(xv, ev)
        xh = xv * r
        part = jnp.sum(dyv * xh, axis=0, keepdims=True)

        @pl.when(pl.program_id(0) == 0)
        def _():
            dg_ref[...] = part

        @pl.when(pl.program_id(0) != 0)
        def _():
            dg_ref[...] += part

        gy = dyv * g_ref[...]
        seg = _spread(_dot_split(gy * xh, ev, 2) * (1.0 / HD), D)
        dx_ref[...] = (r * (gy - xh * seg)).astype(BF16)

    whole = lambda a: pl.BlockSpec(a.shape, lambda i: (0, 0))
    return pl.pallas_call(
        body, name=name, grid=(S // ROWS,),
        in_specs=[_col_spec(ROWS, D, col), whole(gain), _col_spec(ROWS, D, 0), whole(e)]
                 + [pl.BlockSpec((ROWS, LANES), lambda i: (i, 0))] * len(tabs),
        out_specs=[_col_spec(ROWS, D, 0), whole(gain)],
        out_shape=[jax.ShapeDtypeStruct((S, D), BF16), jax.ShapeDtypeStruct((1, D), F32)],
        compiler_params=_params(1))(x, gain, dy, e, *tabs)


def _dup_mat():
    r, c = np.arange(KVW)[:, None], np.arange(2 * KVW)[None, :]
    return (r // HD == c // LANES) & (r % HD == c % HD)


def _fold_mat():
    r, c = np.arange(D)[:, None], np.arange(KVW)[None, :]
    return (r // (2 * LANES) == c // HD) & (r % HD == c % HD)


def _b_post(pb, kv, qg, kg, rope):
    e, ek = _seg_mat(D), _seg_mat(KVW)
    dup = jnp.asarray(_dup_mat(), BF16)

    def body(q_ref, k_ref, v_ref, qg_ref, kg_ref, e_ref, ek_ref, dup_ref, c_ref, a_ref, b_ref, qo, ko, vo):
        c1, a1, b1 = c_ref[...], a_ref[...], b_ref[...]
        qv = q_ref[...]
        qn = qv * _head_rstd(qv, e_ref[...]) * qg_ref[...]
        t = lambda z, n: jnp.tile(z, (1, n))
        qo[...] = (_rope(qn, t(c1, D // LANES), t(a1, D // LANES), t(b1, D // LANES)) * SCALE).astype(BF16)
        kvv = k_ref[...]
        kn = kvv * _head_rstd(kvv, ek_ref[...]) * kg_ref[...]
        kr = _rope(kn, t(c1, KVW // LANES), t(a1, KVW // LANES), t(b1, KVW // LANES)).astype(BF16)
        ko[...] = _dot(kr, dup_ref[...]).astype(BF16)
        vo[...] = _dot(v_ref[...].astype(BF16), dup_ref[...]).astype(BF16)

    whole = lambda a: pl.BlockSpec(a.shape, lambda i: (0, 0))
    tab = pl.BlockSpec((ROWS, LANES), lambda i: (i, 0))
    return pl.pallas_call(
        body, name="b_post", grid=(S // ROWS,),
        in_specs=[_col_spec(ROWS, D, 0), _col_spec(ROWS, KVW, 0), _col_spec(ROWS, KVW, 1),
                  whole(qg), whole(kg), whole(e), whole(ek), whole(dup), tab, tab, tab],
        out_specs=[_col_spec(ROWS, D, 0), _col_spec(ROWS, 2 * KVW, 0), _col_spec(ROWS, 2 * KVW, 0)],
        out_shape=[jax.ShapeDtypeStruct((S, D), BF16), jax.ShapeDtypeStruct((S, 2 * KVW), BF16),
                   jax.ShapeDtypeStruct((S, 2 * KVW), BF16)],
        compiler_params=_params(1))(pb, kv, kv, qg, kg, e, ek, dup, *rope)


def _kv_bwd(dkdup, dvdup, kv, kg, rope):
    ek = _seg_mat(KVW)
    fold = jnp.asarray(_fold_mat(), BF16)

    def body(dk_ref, dv_ref, k_ref, kg_ref, ek_ref, fold_ref, c_ref, a_ref, b_ref, dkv_ref, dg_ref):
        ev, fv = ek_ref[...], fold_ref[...]
        t = lambda z: jnp.tile(z[...], (1, KVW // LANES))
        dk = _rope_t(_dot_split(dk_ref[...], fv, 2), t(c_ref), t(a_ref), t(b_ref))
        dv = _dot_split(dv_ref[...], fv, 2)
        xv = k_ref[...]
        r = _head_rstd(xv, ev)
        xh = xv * r
        part = jnp.sum(dk * xh, axis=0, keepdims=True)

        @pl.when(pl.program_id(0) == 0)
        def _():
            dg_ref[...] = part

        @pl.when(pl.program_id(0) != 0)
        def _():
            dg_ref[...] += part

        gy = dk * kg_ref[...]
        seg = _spread(_dot_split(gy * xh, ev, 2) * (1.0 / HD), KVW)
        dkv_ref[:, 0:KVW] = (r * (gy - xh * seg)).astype(BF16)
        dkv_ref[:, KVW:2 * KVW] = dv.astype(BF16)

    whole = lambda a: pl.BlockSpec(a.shape, lambda i: (0, 0))
    tab = pl.BlockSpec((ROWS, LANES), lambda i: (i, 0))
    return pl.pallas_call(
        body, name="kv_bwd", grid=(S // ROWS,),
        in_specs=[_col_spec(ROWS, D, 0), _col_spec(ROWS, D, 0), _col_spec(ROWS, KVW, 0),
                  whole(kg), whole(ek), whole(fold), tab, tab, tab],
        out_specs=[_col_spec(ROWS, 2 * KVW, 0), whole(kg)],
        out_shape=[jax.ShapeDtypeStruct((S, 2 * KVW), BF16), jax.ShapeDtypeStruct((1, KVW), F32)],
        compiler_params=_params(1))(dkdup, dvdup, kv, kg, ek, fold, *rope)


def _loss_head(out, target):
    def body(o_ref, t_ref, d_ref, db_ref, l_ref):
        diff = o_ref[...] - t_ref[...]
        d = diff * (1.0 / D)
        d_ref[...] = d
        db_ref[...] = d.astype(BF16)

        @pl.when(pl.program_id(0) == 0)
        def _():
            l_ref[...] = jnp.zeros_like(l_ref)

        l_ref[...] += jnp.sum(diff * diff, axis=0, keepdims=True)

    return _rows_call(body, "loss_head", [out, target], [((S, D), F32), ((S, D), BF16), ((1, D), F32)])


def _lane():
    return lax.broadcasted_iota(jnp.int32, (1, LANES), 1)


def _head_mask(hh):
    return (_lane() < HD) if hh == 0 else (_lane() >= HD)


def _fox_fwd(q, k, v, ct, gate, riding):
    nq, npair = S // ATT, NH // 2
    ni, no = len(riding.ins), len(riding.outs)

    def body(q_ref, k_ref, v_ref, c_ref, gate_ref, *rest):
        o_ref, lse_ref, y_ref = rest[ni:ni + 3]
        pair, i = pl.program_id(0), pl.program_id(1)
        at_end = riding.hooks(rest[:ni], rest[ni + 3:ni + 3 + no], *rest[ni + 3 + no:],
                              first=(pair == 0) & (i == 0), middle=(pair == npair // 2) & (i == 0),
                              last=(pair == npair - 1) & (i == nq - 1))
        q2 = q_ref[...]
        qms = [jnp.where(_head_mask(hh), q2, jnp.zeros_like(q2)) for hh in (0, 1)]

        def probs(off, width, m, hh, diag):
            s = _dot(qms[hh], k_ref[pl.ds(off, width), :], NT) - c_ref[hh:hh + 1, pl.ds(off, width)]
            if diag:
                row = i * ATT + lax.broadcasted_iota(jnp.int32, (ATT, width), 0)
                col = off + lax.broadcasted_iota(jnp.int32, (ATT, width), 1)
                s = jnp.where(col <= row, s, NEG)
            m_new = jnp.maximum(m, jnp.max(s, axis=1, keepdims=True))
            p = jnp.exp(s - m_new)
            p_hi = p.astype(BF16)
            return m_new, jnp.exp(m - m_new), p_hi, (p - p_hi.astype(F32)).astype(BF16)

        def weighted(off, width, p_hi, p_lo, hh):
            vj = v_ref[pl.ds(off, width), :]
            v1 = jnp.where(_head_mask(hh), vj, jnp.ones_like(vj))
            return _dot(p_hi, v1) + _dot(p_lo, v1)

        def step(off, width, carry, diag):
            off = pl.multiple_of(off, ATT)
            out = []
            for hh in (0, 1):
                m, acc = carry[hh]
                m, alpha, p_hi, p_lo = probs(off, width, m, hh, diag)
                out.append((m, alpha * acc + weighted(off, width, p_hi, p_lo, hh)))
            return tuple(out)

        one = (jnp.full((ATT, 1), NEG, F32), jnp.zeros((ATT, LANES), F32))
        carry = lax.fori_loop(0, i // 2, lambda j, cr: step(j * (2 * ATT), 2 * ATT, cr, False), (one, one))
        carry = lax.cond(i % 2 == 1, lambda cr: step((i - 1) * ATT, 2 * ATT, cr, True),
                         lambda cr: step(i * ATT, ATT, cr, True), carry)
        res = []
        for hh in (0, 1):
            m, acc = carry[hh]
            l = jnp.max(jnp.where(_head_mask(1 - hh), acc, 0.0), axis=1, keepdims=True)
            res.append((acc / l, m + jnp.log(l)))
        first = _head_mask(0)
        o = jnp.where(first, res[0][0], res[1][0])
        o_ref[...] = o
        lse_ref[...] = jnp.where(first, res[0][1], res[1][1])
        g = gate_ref[...]
        y_ref[...] = (o * (g * _sigmoid(g))).astype(BF16)
        at_end()

    blk = pl.BlockSpec((ATT, LANES), lambda p, i: (i, p))
    full = pl.BlockSpec((S, LANES), lambda p, i: (0, p))
    res = pl.pallas_call(
        body, name="fox_fwd", grid=(npair, nq),
        in_specs=[blk, full, full, pl.BlockSpec((None, 2, S), lambda p, i: (p, 0, 0)), blk] + riding.in_specs,
        out_specs=[blk, blk, blk] + riding.out_specs,
        out_shape=[jax.ShapeDtypeStruct((S, D), F32)] * 2 + [jax.ShapeDtypeStruct((S, D), BF16)] + riding.out_shape,
        scratch_shapes=riding.scratch,
        compiler_params=_params(2))(q, k, v, ct, gate, *riding.ins)
    return res[0], res[1], res[2], res[3:]


def _gate_grads(dy, o, g):
    sg = _sigmoid(g)
    return dy * (g * sg), dy * o * (sg * (1.0 + g * (1.0 - sg)))


def _fox_bwd(q, k, v, ct, o, lse, dy, gate, riding):
    nq, npair = S // ATT, NH // 2
    ni, no = len(riding.ins), len(riding.outs)

    def body(q_ref, k_ref, v_ref, c_ref, o_ref, lse_ref, dy_ref, gate_ref, *rest):
        dq_ref, dk_ref, dvb_ref, dc_ref, dgate_ref = rest[ni:ni + 5]
        dv_ref = rest[ni + 5 + no]
        pair, i = pl.program_id(0), pl.program_id(1)
        at_end = riding.hooks(rest[:ni], rest[ni + 5:ni + 5 + no], *rest[ni + 6 + no:],
                              first=(pair == 0) & (i == 0), middle=(pair == npair // 2) & (i == 0),
                              last=(pair == npair - 1) & (i == nq - 1))

        @pl.when(i == 0)
        def _():
            dk_ref[...] = jnp.zeros_like(dk_ref)
            dv_ref[...] = jnp.zeros_like(dv_ref)
            dc_ref[...] = jnp.zeros_like(dc_ref)

        q2, lse2 = q_ref[...], lse_ref[...]
        do2, dgate = _gate_grads(dy_ref[...], o_ref[...], gate_ref[...])
        dgate_ref[...] = dgate.astype(BF16)
        do2b = do2.astype(BF16)
        prod = do2b.astype(F32) * o_ref[...]
        heads = []
        for hh in (0, 1):
            hm = _head_mask(hh)
            heads.append((jnp.where(hm, q2, jnp.zeros_like(q2)), jnp.where(hm, do2b, jnp.zeros_like(do2b)),
                          jnp.sum(jnp.where(hm, prod, 0.0), axis=1, keepdims=True),
                          jnp.max(jnp.where(hm, lse2, NEG), axis=1, keepdims=True)))

        def step(off, width, dqs, diag):
            off = pl.multiple_of(off, ATT)
            kj, vj = k_ref[pl.ds(off, width), :], v_ref[pl.ds(off, width), :]
            dk, dv, out = None, None, []
            for hh in (0, 1):
                qm, dom, delta, lse_h = heads[hh]
                s = _dot(qm, kj, NT) - c_ref[hh:hh + 1, pl.ds(off, width)]
                p = jnp.exp(s - lse_h)
                if diag:
                    row = i * ATT + lax.broadcasted_iota(jnp.int32, (ATT, width), 0)
                    col = off + lax.broadcasted_iota(jnp.int32, (ATT, width), 1)
                    p = jnp.where(col <= row, p, 0.0)
                ds = p * (_dot(dom, vj, NT) - delta)
                dc_ref[hh:hh + 1, pl.ds(off, width)] += -jnp.sum(ds, axis=0, keepdims=True)
                dsb = ds.astype(BF16)
                dk_h, dv_h = _dot(dsb, qm, TN), _dot(p.astype(BF16), dom, TN)
                dk, dv = (dk_h, dv_h) if dk is None else (dk + dk_h, dv + dv_h)
                out.append(dqs[hh] + _dot(dsb, kj))
            dk_ref[pl.ds(off, width), :] += dk
            dv_ref[pl.ds(off, width), :] += dv
            return tuple(out)

        zero = jnp.zeros((ATT, LANES), F32)
        dqs = lax.fori_loop(0, i // 2, lambda j, acc: step(j * (2 * ATT), 2 * ATT, acc, False), (zero, zero))
        dqs = lax.cond(i % 2 == 1, lambda acc: step((i - 1) * ATT, 2 * ATT, acc, True),
                       lambda acc: step(i * ATT, ATT, acc, True), dqs)
        dq_ref[...] = jnp.where(_head_mask(0), dqs[0], dqs[1]) * SCALE

        @pl.when(i == nq - 1)
        def _():
            dvb_ref[...] = dv_ref[...].astype(BF16)

        at_end()

    blk = pl.BlockSpec((ATT, LANES), lambda p, i: (i, p))
    full = pl.BlockSpec((S, LANES), lambda p, i: (0, p))
    cspec = pl.BlockSpec((None, 2, S), lambda p, i: (p, 0, 0))
    res = pl.pallas_call(
        body, name="fox_bwd", grid=(npair, nq),
        in_specs=[blk, full, full, cspec, blk, blk, blk, blk] + riding.in_specs,
        out_specs=[blk, full, full, cspec, blk] + riding.out_specs,
        out_shape=[jax.ShapeDtypeStruct((S, D), F32)] * 2 + [jax.ShapeDtypeStruct((S, D), BF16),
                                                              jax.ShapeDtypeStruct((npair, 2, S), F32),
                                                              jax.ShapeDtypeStruct((S, D), BF16)]
                  + riding.out_shape,
        scratch_shapes=[pltpu.VMEM((S, LANES), F32)] + riding.scratch,
        compiler_params=_params(2))(q, k, v, ct, o, lse, dy, gate, *riding.ins)
    return res[0], res[1], res[2], res[3], res[4], res[5:]


def _both_heads(x):
    return jnp.concatenate([jnp.where(_head_mask(hh), x, jnp.zeros_like(x)) for hh in (0, 1)], axis=0)


def _per_head(col0, col1):
    return jnp.concatenate([jnp.broadcast_to(col0, (WINDOW, 1)), jnp.broadcast_to(col1, (WINDOW, 1))], axis=0)


def _unstack(x2):
    return jnp.where(_head_mask(0), x2[:WINDOW], x2[WINDOW:])


def _swa_valid(i, start):
    r = lax.broadcasted_iota(jnp.int32, (2 * WINDOW, 2 * WINDOW), 0)
    qabs = i * WINDOW + jnp.where(r >= WINDOW, r - WINDOW, r)
    kabs = start + lax.broadcasted_iota(jnp.int32, (2 * WINDOW, 2 * WINDOW), 1)
    return (kabs <= qabs) & (qabs - kabs < WINDOW)


def _swa_fwd(q, kdup, vdup, sinks_t, proj, gate_col):
    def body(q_ref, k_ref, v_ref, sk_ref, gate_ref, o_ref, lse_ref, y_ref):
        skv = sk_ref[...]
        first = _head_mask(0)
        for sb in range(SWQ):
            i = pl.program_id(1) * SWQ + sb
            rows = slice(sb * WINDOW, (sb + 1) * WINDOW)
            start = pl.multiple_of(jnp.maximum(i - 1, 0) * WINDOW, WINDOW)
            kk, vv = k_ref[pl.ds(start, 2 * WINDOW), :], v_ref[pl.ds(start, 2 * WINDOW), :]
            q2 = q_ref[rows, :]
            valid = _swa_valid(i, start)[:WINDOW]
            res = []
            for hh in (0, 1):
                hm = _head_mask(hh)
                sink = jnp.max(jnp.where(hm, skv, NEG), axis=1, keepdims=True)
                s = jnp.where(valid, _dot(jnp.where(hm, q2, jnp.zeros_like(q2)), kk, NT), NEG)
                m = jnp.maximum(jnp.max(s, axis=1, keepdims=True), sink)
                p = jnp.exp(s - m)
                l = jnp.sum(p, axis=1, keepdims=True) + jnp.exp(sink - m)
                res.append((_dot(p.astype(BF16), vv) / l, m + jnp.log(l)))
            o = jnp.where(first, res[0][0], res[1][0])
            o_ref[rows, :] = o
            lse_ref[rows, :] = jnp.where(first, res[0][1], res[1][1])
            g = gate_ref[rows, :]
            y_ref[rows, :] = (o * (g * _sigmoid(g))).astype(BF16)

    blk = pl.BlockSpec((SWQ * WINDOW, LANES), lambda p, i: (i, p))
    gate = pl.BlockSpec((SWQ * WINDOW, LANES), lambda p, i: (i, gate_col + p))
    full = pl.BlockSpec((S, LANES), lambda p, i: (0, p // 2))
    return pl.pallas_call(
        body, name="swa_fwd", grid=(NH // 2, S // (SWQ * WINDOW)),
        in_specs=[blk, full, full, pl.BlockSpec((1, LANES), lambda p, i: (0, p)), gate],
        out_specs=[blk, blk, blk],
        out_shape=[jax.ShapeDtypeStruct((S, D), F32)] * 2 + [jax.ShapeDtypeStruct((S, D), BF16)],
        compiler_params=_params(2))(q, kdup, vdup, sinks_t, proj)


def _swa_bwd(q, kdup, vdup, sinks_t, o, lse, dy, proj, gate_col):
    def body(q_ref, k_ref, v_ref, sk_ref, o_ref, lse_ref, dy_ref, gate_ref, dq_ref, dk_ref, dv_ref, dsk_ref,
             dgate_ref):
        @pl.when(pl.program_id(1) == 0)
        def _():
            dk_ref[...] = jnp.zeros_like(dk_ref)
            dv_ref[...] = jnp.zeros_like(dv_ref)
            dsk_ref[...] = jnp.zeros_like(dsk_ref)

        skv = sk_ref[...]
        first = _head_mask(0)
        sink = _per_head(*[jnp.max(jnp.where(_head_mask(hh), skv, NEG), axis=1, keepdims=True) for hh in (0, 1)])
        for sb in range(SWQ):
            i = pl.program_id(1) * SWQ + sb
            rows = slice(sb * WINDOW, (sb + 1) * WINDOW)
            start = pl.multiple_of(jnp.maximum(i - 1, 0) * WINDOW, WINDOW)
            kk, vv = k_ref[pl.ds(start, 2 * WINDOW), :], v_ref[pl.ds(start, 2 * WINDOW), :]
            do2, dgate = _gate_grads(dy_ref[rows, :], o_ref[rows, :], gate_ref[rows, :])
            dgate_ref[rows, :] = dgate.astype(BF16)
            do2b = do2.astype(BF16)
            prod, lse2 = do2b.astype(F32) * o_ref[rows, :], lse_ref[rows, :]
            qs, dos = _both_heads(q_ref[rows, :]), _both_heads(do2b)
            delta = jnp.concatenate([jnp.sum(jnp.where(_head_mask(hh), prod, 0.0), axis=1, keepdims=True)
                                     for hh in (0, 1)], axis=0)
            lse_h = jnp.concatenate([jnp.max(jnp.where(_head_mask(hh), lse2, NEG), axis=1, keepdims=True)
                                     for hh in (0, 1)], axis=0)
            p = jnp.where(_swa_valid(i, start), jnp.exp(_dot(qs, kk, NT) - lse_h), 0.0)
            dsb = (p * (_dot(dos, vv, NT) - delta)).astype(BF16)
            dk_ref[pl.ds(start, 2 * WINDOW), :] += _dot(dsb, qs, TN)
            dv_ref[pl.ds(start, 2 * WINDOW), :] += _dot(p.astype(BF16), dos, TN)
            dq_ref[rows, :] = _unstack(_dot(dsb, kk)) * SCALE
            t = jnp.exp(sink - lse_h) * delta
            dsk_ref[...] += -jnp.where(first, jnp.sum(t[:WINDOW], axis=0, keepdims=True),
                                       jnp.sum(t[WINDOW:], axis=0, keepdims=True))

    blk = pl.BlockSpec((SWQ * WINDOW, LANES), lambda p, i: (i, p))
    full = pl.BlockSpec((S, LANES), lambda p, i: (0, p // 2))
    acc = pl.BlockSpec((S, LANES), lambda p, i: (0, p))
    sk = pl.BlockSpec((1, LANES), lambda p, i: (0, p))
    gate = pl.BlockSpec((SWQ * WINDOW, LANES), lambda p, i: (i, gate_col + p))
    return pl.pallas_call(
        body, name="swa_bwd", grid=(NH // 2, S // (SWQ * WINDOW)),
        in_specs=[blk, full, full, sk, blk, blk, blk, gate],
        out_specs=[blk, acc, acc, sk, blk],
        out_shape=[jax.ShapeDtypeStruct((S, D), F32)] * 3 + [jax.ShapeDtypeStruct((1, D), F32),
                                                              jax.ShapeDtypeStruct((S, D), BF16)],
        compiler_params=_params(2))(q, kdup, vdup, sinks_t, o, lse, dy, proj)


def _adamw_math(w, g, m, v):
    m = ADAM_B1 * m + (1.0 - ADAM_B1) * g
    v = ADAM_B2 * v + (1.0 - ADAM_B2) * jnp.square(g)
    m_hat = m / (1.0 - ADAM_B1 ** ADAM_STEP)
    v_hat = v / (1.0 - ADAM_B2 ** ADAM_STEP)
    delta = -ADAM_LR * (m_hat / (jnp.sqrt(v_hat) + ADAM_EPS) + ADAM_WD * w)
    return delta, m, v


def _adamw_small(ws, gs, ms, vs):
    k = len(ws)

    def body(*refs):
        for p in range(k):
            w_ref, g_ref, m_ref, v_ref = (refs[q * k + p] for q in range(4))
            d, mo, vo = _adamw_math(w_ref[...], g_ref[...], m_ref[...], v_ref[...])
            refs[4 * k + p][...], refs[5 * k + p][...], refs[6 * k + p][...] = d, mo, vo

    res = pl.pallas_call(
        body, name="adamw_small",
        out_shape=[jax.ShapeDtypeStruct(t.shape, F32) for t in ws] * 3)(*ws, *gs, *ms, *vs)
    return res[:k], res[k:2 * k], res[2 * k:]


SUM_TILE = 128


FLAT_BLOCK = 257 * 1024


def _tiles(shape, axis, lead=0):
    if len(shape) == 1:
        count = shape[0] // FLAT_BLOCK
        return (FLAT_BLOCK,), count, lambda pos, *lead_idx: (sum(k * count for k in lead_idx) + pos,)
    r, c = shape
    blk = (SUM_TILE, c) if axis == 0 else (r, SUM_TILE)
    count = shape[axis] // SUM_TILE

    def index(pos, *lead_idx):
        return tuple(lead_idx) + ((pos, 0) if axis == 0 else (0, pos))

    return (None,) * lead + blk, count, index


def _adamw_halves(w, g_mine, g_theirs, m, v, axis, name):
    blk, count, index = _tiles(w.shape, axis)
    per_half = count // 2

    def body(w_ref, a_ref, b_ref, m_ref, v_ref, g_ref, d_ref, mo_ref, vo_ref):
        is_mine = pl.program_id(0) // per_half == lax.axis_index("c")
        g = jnp.where(is_mine, a_ref[...], b_ref[...])
        g_ref[...] = g
        d_ref[...], mo_ref[...], vo_ref[...] = _adamw_math(w_ref[...], g, m_ref[...], v_ref[...])

    spec = pl.BlockSpec(blk, lambda i: index(i))
    half = pl.BlockSpec(blk, lambda i: index(i % per_half))
    return pl.pallas_call(
        body, name=name, grid=(count,), in_specs=[spec, half, half, spec, spec], out_specs=[spec] * 4,
        out_shape=[jax.ShapeDtypeStruct(w.shape, F32)] * 4, compiler_params=_params(1))(w, g_mine, g_theirs, m, v)


def _chip_sum(blocks, from_sibling, axis, name):
    flat = blocks.ndim == 1
    blk, count, index = _tiles((from_sibling.shape[0] // NCHIP,) if flat else from_sibling.shape[1:], axis, lead=1)

    def body(lo_ref, hi_ref, p_ref, o32, o16):
        mine = jnp.where(lax.axis_index("c") == 0, lo_ref[...], hi_ref[...])
        acc = mine + p_ref[...]
        o32[...] = acc
        o16[...] = acc.astype(BF16)

    half = pl.BlockSpec(blk, lambda k, i: index(i, k))
    if flat:
        lo = pl.BlockSpec(blk, lambda k, i: (2 * count * k + i,))
        hi = pl.BlockSpec(blk, lambda k, i: (2 * count * k + count + i,))
    else:
        lo, hi = half, pl.BlockSpec(blk, lambda k, i: index(i + count, k))
    return pl.pallas_call(
        body, name=name, grid=(NCHIP, count), in_specs=[lo, hi, half], out_specs=[half, half],
        out_shape=[jax.ShapeDtypeStruct(from_sibling.shape, F32), jax.ShapeDtypeStruct(from_sibling.shape, BF16)],
        compiler_params=_params(2))(blocks, blocks, from_sibling)


def _mesh_sum(own, parts, axis, name):
    blk, count, index = _tiles(own.shape, axis)
    n = NCHIP - 1

    def body(a_ref, *refs):
        acc = a_ref[...]
        for k in range(n):
            acc = acc + refs[k][...].astype(F32)
        refs[n][...] = acc

    spec = pl.BlockSpec(blk, lambda i: index(i))
    if own.ndim == 1:
        part = [pl.BlockSpec(blk, lambda i, k=k: (k * count + i,)) for k in range(n)]
    else:
        part = [pl.BlockSpec((None,) + blk, lambda i, k=k: (k,) + index(i)) for k in range(n)]
    return pl.pallas_call(
        body, name=name, grid=(count,), in_specs=[spec] + part,
        out_specs=spec, out_shape=jax.ShapeDtypeStruct(own.shape, F32),
        compiler_params=_params(1))(own, *([parts] * n))


def _sum_stack(parts, name):
    n = parts.shape[0]

    def body(p_ref, o_ref):
        acc = p_ref[0]
        for k in range(1, n):
            acc = acc + p_ref[k]
        o_ref[...] = acc

    return pl.pallas_call(body, name=name, out_shape=jax.ShapeDtypeStruct(parts.shape[1:], F32))(parts)


def _coords():
    return lax.axis_index("x"), lax.axis_index("y"), lax.axis_index("c")


def _chip(who):
    return 2 * who[0] + who[1]


def _flip(who, mask):
    return tuple((1 - v) if b else v for v, b in zip(who, mask))


def _transfer(transfers, t, I, O, ssem, rsem, receiving):
    tr, me = transfers[t], _coords()
    peer = _flip(me, tr["mask"])
    return pltpu.make_async_remote_copy(
        src_ref=tr["src"](I, O, me), dst_ref=tr["dst"](I, O, peer if receiving else me),
        send_sem=ssem.at[t], recv_sem=rsem.at[t], device_id=peer, device_id_type=MESH)


def _start_transfers(transfers, I, O, ssem, rsem, onward):
    arrived = set()
    for t, tr in enumerate(transfers):
        after = tr.get("after")
        if (after is not None) != onward:
            continue
        if after is not None and after not in arrived:
            _transfer(transfers, after, I, O, ssem, rsem, True).wait_recv()
            arrived.add(after)
        _transfer(transfers, t, I, O, ssem, rsem, False).start()


def _finish_transfers(transfers, I, O, ssem, rsem):
    passed_on = {tr["after"] for tr in transfers if tr.get("after") is not None}
    for t in range(len(transfers)):
        if t not in passed_on:
            _transfer(transfers, t, I, O, ssem, rsem, True).wait_recv()
    for t in range(len(transfers)):
        _transfer(transfers, t, I, O, ssem, rsem, False).wait_send()


def _own_copies(own, I, O, stage, lsem, leg):
    for n, (src, dst) in enumerate(own):
        me = _coords()
        bring =pltpu.make_async_copy(src(I, O, me), stage[n], lsem.at[2 * n])
        put = pltpu.make_async_copy(stage[n], dst(I, O, me), lsem.at[2 * n + 1])
        if leg == 0:
            bring.start()
        elif leg == 1:
            bring.wait()
            put.start()
        else:
            put.wait()


def _own_scratch(own, ins):
    return [pltpu.VMEM(ins[n].shape, ins[n].dtype) for n in range(len(own))], pltpu.SemaphoreType.DMA((max(2 * len(own), 1),))


def _exchange(name, ins, outs, transfers, own=()):
    ni, no = len(ins), len(outs)
    nt = len(transfers)
    stages, stage_sems = _own_scratch(own, ins)

    def body(*refs):
        I, O = refs[:ni], refs[ni:ni + no]
        ssem, rsem, lsem = refs[ni + no:ni + no + 3]
        stage = refs[ni + no + 3:]
        _own_copies(own, I, O, stage, lsem, 0)
        _start_transfers(transfers, I, O, ssem, rsem, False)
        _own_copies(own, I, O, stage, lsem, 1)
        _start_transfers(transfers, I, O, ssem, rsem, True)
        _finish_transfers(transfers, I, O, ssem, rsem)
        _own_copies(own, I, O, stage, lsem, 2)

    hbm = pl.BlockSpec(memory_space=pltpu.HBM)
    return pl.pallas_call(
        body, name=name, in_specs=[hbm] * ni, out_specs=[hbm] * no,
        out_shape=[jax.ShapeDtypeStruct(s, d) for s, d in outs],
        scratch_shapes=[pltpu.SemaphoreType.DMA((nt,)), pltpu.SemaphoreType.DMA((nt,)), stage_sems] + stages,
        compiler_params=pltpu.CompilerParams(has_side_effects=True, vmem_limit_bytes=VMEM_LIMIT))(*ins)


CHIP_MASKS = [(0, 1, 0), (1, 0, 0), (1, 1, 0)]
SIBLING = (0, 0, 1)


def _half(shape2d, axis, which):
    n = shape2d[axis] // 2
    cut = pl.ds(pl.multiple_of(which * n, n), n)
    return (cut, slice(None)) if axis == 0 else (slice(None), cut)


class _Riding:
    def __init__(self, transfers, ins, outs, own=()):
        self.transfers, self.ins, self.outs, self.own = transfers, list(ins), list(outs), list(own)
        hbm = pl.BlockSpec(memory_space=pltpu.HBM)
        self.in_specs, self.out_specs = [hbm] * len(self.ins), [hbm] * len(self.outs)
        self.out_shape = [jax.ShapeDtypeStruct(s, d) for s, d in self.outs]
        stages, stage_sems = _own_scratch(self.own, self.ins)
        self.scratch = [pltpu.SemaphoreType.DMA((max(len(transfers), 1),))] * 2 + [stage_sems] + stages

    def alone(self, name):
        return _exchange(name, self.ins, self.outs, self.transfers, self.own)

    def hooks(self, I, O, ssem, rsem, lsem, *stage, first, middle, last):
        tr, own = self.transfers, self.own

        @pl.when(first)
        def _():
            _own_copies(own, I, O, stage, lsem, 0)
            _start_transfers(tr, I, O, ssem, rsem, False)

        if own or any(t.get("after") is not None for t in tr):
            @pl.when(middle)
            def _():
                _own_copies(own, I, O, stage, lsem, 1)
                _start_transfers(tr, I, O, ssem, rsem, True)

        def at_end():
            @pl.when(last)
            def _():
                _finish_transfers(tr, I, O, ssem, rsem)
                _own_copies(own, I, O, stage, lsem, 2)

        return at_end


def _stretch(n, pos):
    return (pl.ds(pos * n if isinstance(pos, int) else pl.multiple_of(pos * n, n), n),)


def _gather_plan(shards, axes):
    def half(a, who):
        if shards[a].ndim == 1:
            return _stretch(shards[a].shape[0] // 2, who[2])
        return _half(shards[a].shape, axes[a], who[2])

    def landed(a, chip, who):
        if shards[a].ndim == 1:
            return _stretch(shards[a].shape[0] // 2, 2 * chip + who[2])
        return (chip,) + half(a, who)

    over_ici, onward = [], []
    for a in range(len(shards)):
        for mask in CHIP_MASKS:
            over_ici.append(dict(
                mask=mask,
                src=lambda I, O, me, a=a: I[a].at[half(a, me)],
                dst=lambda I, O, who, a=a: O[a].at[landed(a, _chip(who), who)]))
            onward.append(dict(
                mask=SIBLING, after=len(over_ici) - 1,
                src=lambda I, O, me, a=a, mask=mask: O[a].at[landed(a, _chip(_flip(me, mask)), me)],
                dst=lambda I, O, who, a=a, mask=mask: O[a].at[landed(a, _chip(_flip(who, mask)), who)]))
    outs = [((NCHIP * s.shape[0],) if s.ndim == 1 else (NCHIP,) + s.shape, s.dtype) for s in shards]

    def whole(a, chip):
        return _stretch(shards[a].shape[0], chip) if shards[a].ndim == 1 else (chip,)

    own = [(lambda I, O, me, a=a: I[a], lambda I, O, me, a=a: O[a].at[whole(a, _chip(me))])
           for a in range(len(shards))]
    return over_ici + onward, outs, own


def _gather_shards(shards, axes):
    transfers, outs, own = _gather_plan(shards, axes)
    return _exchange("gather_weights", shards, outs, transfers, own)


def _to_sibling(arrs, name):
    transfers = [dict(mask=SIBLING, src=lambda I, O, me, a=a: I[a], dst=lambda I, O, who, a=a: O[a])
                 for a in range(len(arrs))]
    return _exchange(name, arrs, [(t.shape, t.dtype) for t in arrs], transfers)


def _halves_plan(blocks, axes):
    def cut(a, which):
        return (slice(None),) + _half(blocks[a].shape[1:], axes[a], which)

    transfers, outs = [], []
    for a, (b, ax) in enumerate(zip(blocks, axes)):
        if b.ndim == 1:
            h = b.shape[0] // NCHIP // 2
            for k in range(NCHIP):
                transfers.append(dict(mask=SIBLING,
                                      src=lambda I, O, me, a=a, k=k, h=h: I[a].at[_stretch(h, 2 * k + 1 - me[2])],
                                      dst=lambda I, O, who, a=a, k=k, h=h: O[a].at[_stretch(h, k)]))
            outs.append(((NCHIP * h,), b.dtype))
        else:
            transfers.append(dict(mask=SIBLING, src=lambda I, O, me, a=a: I[a].at[cut(a, 1 - me[2])],
                                  dst=lambda I, O, who, a=a: O[a]))
            shape = list(b.shape)
            shape[ax + 1] //= 2
            outs.append((tuple(shape), b.dtype))
    return transfers, outs


def _scatter_plan(tb):
    def slot(a, k):
        return (k,) if tb[a].ndim == 3 else _stretch(tb[a].shape[0] // NCHIP, k)

    transfers = []
    for a in range(len(tb)):
        for n, mask in enumerate(CHIP_MASKS):
            transfers.append(dict(
                mask=mask,
                src=lambda I, O, me, a=a, mask=mask: I[a].at[slot(a, _chip(_flip(me, mask)))],
                dst=lambda I, O, who, a=a, n=n: O[a].at[slot(a, n)]))
    outs = [((3,) + t.shape[1:] if t.ndim == 3 else (3 * (t.shape[0] // NCHIP),), t.dtype) for t in tb]
    return transfers, outs


def _scatter_chip_sums(tb):
    transfers, outs = _scatter_plan(tb)
    return _exchange("scatter_grads", tb, outs, transfers)


def _gather_small(vec):
    def slot(who):
        return 4 * who[0] + 2 * who[1] + who[2]

    masks = [(m >> 2 & 1, m >> 1 & 1, m & 1) for m in range(1, 8)]
    transfers = [dict(mask=mask, src=lambda I, O, me: I[0], dst=lambda I, O, who: O[0].at[slot(who)])
                 for mask in masks]
    own = [(lambda I, O, me: I[0], lambda I, O, me: O[0].at[slot(me)])]
    return _exchange("gather_small", [vec], [((8,) + vec.shape, vec.dtype)], transfers, own)[0]


def _rope_tables(positions):
    half = ROT // 2
    inv_freq = jnp.power(jnp.float32(THETA), -jnp.arange(0, ROT, 2, dtype=F32) / ROT)
    ang = positions.astype(F32)[:, None] * inv_freq[None, :]
    cos, sin = jnp.cos(ang), jnp.sin(ang)
    one, zero, z8 = jnp.ones((S, HD - ROT), F32), jnp.zeros((S, HD - ROT), F32), jnp.zeros((S, half), F32)
    c = jnp.concatenate([cos, cos, one], axis=1)
    a = jnp.concatenate([-sin, z8, zero], axis=1)
    b = jnp.concatenate([z8, sin, zero], axis=1)
    return tuple(jnp.tile(t, (1, 2)) for t in (c, a, b))


def _tile_heads(g, w):
    return jnp.tile(g.reshape(1, HD), (1, w // HD))


def _fold_heads(dg):
    return dg.reshape(-1, HD).sum(axis=0)


def _pad_lanes(a):
    return jnp.pad(a, ((0, 0), (0, LANES - a.shape[1])))


def _local_step(x, target, positions, wt, fetch, late_weights, begin_reduce):
    rope = _rope_tables(positions)
    w1t = wt["w_in_a_t"]
    f_row = 3 * D // LANES
    wg_t = w1t[3 * D + NH:]
    in_b_block = lambda c: pl.BlockSpec((None, TN_, TN_), lambda j, i: (c, j, 0))
    b_pad = _pad_lanes(wt["b_forget"].reshape(1, NH))
    qg_a, kg_a = _tile_heads(wt["qnorm_a_g"], D), _tile_heads(wt["knorm_a_g"], D)
    qg_b, kg_b = _tile_heads(wt["qnorm_b_g"], D), _tile_heads(wt["knorm_b_g"], KVW)
    norm_a, kv_g, norm_b = wt["norm_a_g"].reshape(1, D), wt["kv_norm_g"].reshape(1, D), wt["norm_b_g"].reshape(1, D)
    sinks_t = jnp.repeat(wt["sinks"].reshape(1, NH), HD, axis=1)

    (u_a,) = _rmsnorm_fwd(x, [norm_a], "norm_a")
    qkv = _mm("proj_a", S, 3 * D, [(u_a, _a_rows(D), w1t, _b_rows(D), NT)])
    fpad = _mm("proj_f", S, LANES, [(u_a, _a_rows(D), w1t, _b_rows(D, row0=f_row, tn=LANES), NT)], tn=LANES)
    gate_a = _mm("proj_gate_a", S, D, [(u_a, _a_rows(D), wg_t, _b_rows(D), NT)])
    q_a, k_a, v_a = _a_post(qkv, qg_a, kg_a)
    ct = _forget_cumsum(fpad, b_pad)
    ct2 = ct[:NH].reshape(NH // 2, 2, S)
    o_a, lse_a, y_a, fetched = _fox_fwd(q_a, k_a, v_a, ct2, gate_a, fetch)
    wt = {**wt, **late_weights(fetched)}
    w_in_b = wt["w_in_b"]
    h1 = _mm("out_a", S, D, [(y_a, _a_rows(D), wt["w_out_a"], _b_cols(D), None)], add=x)
    u_kv, u_b = _rmsnorm_fwd(h1, [kv_g, norm_b], "norm_b")
    kv = _mm("proj_kv", S, 2 * KVW, [(u_kv, _a_rows(D), wt["w_kv"], _b_cols(D), None)])
    pb = _mm("proj_b", S, 2 * D,
             [(u_b, _a_rows(D), w_in_b, pl.BlockSpec((None, D, TN_), lambda j, i: (j, 0, 0)), None)])
    q_b, kdup, vdup = _b_post(pb, kv, qg_b, kg_b, rope)
    gate_b_col = D // LANES
    o_b, lse_b, y_b = _swa_fwd(q_b, kdup, vdup, sinks_t, pb, gate_b_col)
    out = _mm("out_b", S, D, [(y_b, _a_rows(D), wt["w_out_b"], _b_cols(D), None)], add=h1)
    d_out, d_out_b, sq = _loss_head(out, target)

    g = {}
    g["w_out_b"] = _mm("dw_out_b", D, D, [(y_b, _a_cols(S), d_out_b, _b_cols(S), TN)])
    d_y_b = _mm("dy_b", S, D, [(d_out_b, _a_rows(D), wt["w_out_b"], _b_rows(D), NT)])
    dq_b, dkdup, dvdup, dsk, d_gate_b = _swa_bwd(q_b, kdup, vdup, sinks_t, o_b, lse_b, d_y_b, pb, gate_b_col)
    g["sinks"] = dsk[0, ::HD]
    d_qb_raw, dg = _headnorm_bwd(pb, 0, qg_b, dq_b, rope, "qnorm_b_bwd")
    g["qnorm_b_g"] = _fold_heads(dg)
    d_pb = [d_qb_raw, d_qb_raw, d_gate_b, d_gate_b]
    g["w_in_b"] = jnp.concatenate([
        _mm("dw_in_b_q", D, D, [(u_b, _a_cols(S), d_qb_raw, _b_cols(S), TN)], stacked=True),
        _mm("dw_in_b_gate", D, D, [(u_b, _a_cols(S), d_gate_b, _b_cols(S), TN)], stacked=True)], axis=0)
    d_u_b = _mm("du_b", S, D, [(d_pb[c], _a_rows(TN_, col=c % 2), w_in_b, in_b_block(c), NT) for c in range(NCHIP)])
    d_kv, dg = _kv_bwd(dkdup, dvdup, kv, kg_b, rope)
    g["knorm_b_g"] = _fold_heads(dg)
    g["w_kv"] = _mm("dw_kv", D, 2 * KVW, [(u_kv, _a_cols(S), d_kv, _b_cols(S), TN)])
    d_u_kv = _mm("du_kv", S, D, [(d_kv, _a_rows(2 * KVW), wt["w_kv"], _b_rows(2 * KVW), NT)])
    d_h1, d_h1_b, g["kv_norm_g"], g["norm_b_g"] = _rmsnorm_bwd(h1, [kv_g, norm_b], [d_u_kv, d_u_b], d_out, "norm_b_bwd")
    g["w_out_a"] = _mm("dw_out_a", D, D, [(y_a, _a_cols(S), d_h1_b, _b_cols(S), TN)])
    late = {n: g[n] for n in LATE}
    d_y_a, halves = _mm("dy_a", S, D, [(d_h1_b, _a_rows(D), wt["w_out_a"], _b_rows(D), NT)],
                        riding=begin_reduce(late))
    riding, so_far = begin_reduce(late, halves)
    dq_a, dk_a, dv_a, dct, d_gate_a, arrived = _fox_bwd(q_a, k_a, v_a, ct2, o_a, lse_a, d_y_a, gate_a, riding)
    dct_pad = jnp.pad(dct.reshape(NH, S), ((0, LANES - NH), (0, 0)))
    d_f, db = _forget_bwd(dct_pad, fpad, b_pad)
    g["b_forget"] = db[0, :NH]
    d_q_raw, dg = _headnorm_bwd(qkv, 0, qg_a, dq_a, None, "qnorm_a_bwd")
    g["qnorm_a_g"] = _fold_heads(dg)
    d_k_raw, dg = _headnorm_bwd(qkv, 1, kg_a, dk_a, None, "knorm_a_bwd")
    g["knorm_a_g"] = _fold_heads(dg)
    rows, gw = 4 * D + NH, None
    for n, t, row0 in (("q", d_q_raw, 0), ("k", d_k_raw, D), ("v", dv_a, 2 * D)):
        gw = _mm("dw_in_a_" + n, D, D, [(t, _a_cols(S), u_a, _b_cols(S), TN)], rows_of=(gw, rows, row0))
    gw = _mm("dw_in_a_f", LANES, D, [(d_f, _a_cols(S, tm=LANES), u_a, _b_cols(S), TN)], tm=LANES,
             rows_of=(gw, rows, 3 * D))
    g["w_in_a"] = _mm("dw_in_a_gate", D, D, [(d_gate_a, _a_cols(S), u_a, _b_cols(S), TN)],
                      rows_of=(gw, rows, 3 * D + NH))
    first = {"w_in_a": g["w_in_a"]}
    riding, so_far_first = begin_reduce(first, begin_reduce(first).alone("sibling_halves_w_in_a"))
    d_u_a, arrived_first = _mm("du_a", S, D, [
        (d_q_raw, _a_rows(D), w1t, _b_cols(D, row=0), None), (d_k_raw, _a_rows(D), w1t, _b_cols(D, row=1), None),
        (dv_a, _a_rows(D), w1t, _b_cols(D, row=2), None), (d_gate_a, _a_rows(D), wg_t, _b_cols(D), None),
        (d_f, _a_rows(LANES), w1t, _b_cols(LANES, row=f_row), None)], riding=riding)
    d_x, _, g["norm_a_g"] = _rmsnorm_bwd(x, [norm_a], [d_u_a], d_h1, "norm_a_bwd")
    return sq, d_x, g, (list(so_far_first) + list(so_far), list(arrived_first) + list(arrived))


BIG = ["w_in_a", "w_out_a", "w_kv", "w_in_b", "w_out_b"]
LATE = BIG[1:]
SPLIT = {"w_in_a": None, "w_out_a": 0, "w_kv": 0, "w_in_b": 0, "w_out_b": 0}
SMALL = ["norm_a_g", "b_forget", "qnorm_a_g", "knorm_a_g", "kv_norm_g", "knorm_b_g", "norm_b_g", "qnorm_b_g", "sinks"]
NAMES = ["norm_a_g", "w_in_a", "b_forget", "qnorm_a_g", "knorm_a_g", "w_out_a", "kv_norm_g", "w_kv", "knorm_b_g",
         "norm_b_g", "w_in_b", "qnorm_b_g", "sinks", "w_out_b"]


def _pack(vals):
    flat = []
    for v in vals:
        v = v.reshape(-1)
        flat.append(jnp.pad(v, (0, -v.shape[0] % LANES)))
    flat = jnp.concatenate(flat)
    flat = jnp.pad(flat, (0, -flat.shape[0] % (8 * LANES)))
    return flat.reshape(-1, LANES)


def _unpack(packed, shapes):
    flat, out, off = packed.reshape(-1), [], 0
    for s in shapes:
        n = int(np.prod(s))
        out.append(flat[off:off + n].reshape(s))
        off += n + (-n % LANES)
    return out


def kernel(x, positions, norm_a_g, w_in_a, b_forget, qnorm_a_g, knorm_a_g, w_out_a, kv_norm_g, w_kv, knorm_b_g, norm_b_g, w_in_b, qnorm_b_g, sinks, w_out_b, loss_target, m_norm_a_g, m_w_in_a, m_b_forget, m_qnorm_a_g, m_knorm_a_g, m_w_out_a, m_kv_norm_g, m_w_kv, m_knorm_b_g, m_norm_b_g, m_w_in_b, m_qnorm_b_g, m_sinks, m_w_out_b, v_norm_a_g, v_w_in_a, v_b_forget, v_qnorm_a_g, v_knorm_a_g, v_w_out_a, v_kv_norm_g, v_w_kv, v_knorm_b_g, v_norm_b_g, v_w_in_b, v_qnorm_b_g, v_sinks, v_w_out_b):
    w = dict(norm_a_g=norm_a_g, w_in_a=w_in_a, b_forget=b_forget, qnorm_a_g=qnorm_a_g, knorm_a_g=knorm_a_g,
             w_out_a=w_out_a, kv_norm_g=kv_norm_g, w_kv=w_kv, knorm_b_g=knorm_b_g, norm_b_g=norm_b_g,
             w_in_b=w_in_b, qnorm_b_g=qnorm_b_g, sinks=sinks, w_out_b=w_out_b)
    m = dict(norm_a_g=m_norm_a_g, w_in_a=m_w_in_a, b_forget=m_b_forget, qnorm_a_g=m_qnorm_a_g, knorm_a_g=m_knorm_a_g,
             w_out_a=m_w_out_a, kv_norm_g=m_kv_norm_g, w_kv=m_w_kv, knorm_b_g=m_knorm_b_g, norm_b_g=m_norm_b_g,
             w_in_b=m_w_in_b, qnorm_b_g=m_qnorm_b_g, sinks=m_sinks, w_out_b=m_w_out_b)
    v = dict(norm_a_g=v_norm_a_g, w_in_a=v_w_in_a, b_forget=v_b_forget, qnorm_a_g=v_qnorm_a_g, knorm_a_g=v_knorm_a_g,
             w_out_a=v_w_out_a, kv_norm_g=v_kv_norm_g, w_kv=v_w_kv, knorm_b_g=v_knorm_b_g, norm_b_g=v_norm_b_g,
             w_in_b=v_w_in_b, qnorm_b_g=v_qnorm_b_g, sinks=v_sinks, w_out_b=v_w_out_b)
    my_chip = 2 * lax.axis_index("x") + lax.axis_index("y")

    def shard2d(t, n):
        if n == "w_in_a":
            return jnp.transpose(t, (2, 0, 1)).reshape(-1)
        return t.reshape(t.shape[-2:])

    def unflat(t, n):
        return jnp.transpose(t.reshape(-1, 1, D), (1, 2, 0)) if n == "w_in_a" else t.reshape(w[n].shape)

    w2d = {n: shard2d(w[n], n) for n in BIG}

    norm_a_rows = jnp.broadcast_to(norm_a_g.reshape(1, D // NCHIP), (16, D // NCHIP))
    w1t, norm_rows = _gather_shards([w2d["w_in_a"].astype(BF16), norm_a_rows], [SPLIT["w_in_a"], 0])
    wt = {"w_in_a_t": w1t.reshape(-1, D), "norm_a_g": norm_rows[:, 0, :].reshape(1, D)}
    for n in SMALL[1:]:
        wt[n] = w[n]
    late_shards = [w2d[n].astype(BF16) for n in LATE]
    late_axes = [SPLIT[n] for n in LATE]
    transfers, outs, own = _gather_plan(late_shards, late_axes)
    fetch = _Riding(transfers, late_shards, outs, own)

    def late_weights(fetched):
        return {n: t if n == "w_in_b" else t.reshape(-1, t.shape[2]) for n, t in zip(LATE, fetched)}

    def as_blocks(t):
        if t.ndim == 3:
            return t
        return t.reshape(-1) if t.shape[0] % (8 * NCHIP) else t.reshape(NCHIP, -1, t.shape[1])

    def begin_reduce(grads, halves=None):
        names = list(grads)
        axes = [SPLIT[n] for n in names]
        blocks = [as_blocks(grads[n]) for n in names]
        if halves is None:
            transfers, outs = _halves_plan(blocks, axes)
            return _Riding(transfers, blocks, outs)
        sums = [_chip_sum(blk, part, ax, "chip_sum_" + n) for n, ax, blk, part in zip(names, axes, blocks, halves)]
        bf16 = [s[1] for s in sums]
        transfers, outs = _scatter_plan(bf16)
        return _Riding(transfers, bf16, outs), [s[0] for s in sums]

    sq, d_x, g, (chip_f32, arrived) = _local_step(x[0], loss_target[0], positions, wt, fetch, late_weights,
                                                  begin_reduce)

    small_shapes = [(D,), (NH,), (HD,), (HD,), (D,), (HD,), (D,), (HD,), (NH,), (D,)]
    packed = _pack([g[n] for n in SMALL] + [sq])
    total = _sum_stack(_gather_small(packed), "sum_small")
    small_g = dict(zip(SMALL, _unpack(total, small_shapes)[:-1]))
    loss = 0.5 * jnp.sum(_unpack(total, small_shapes)[-1]) / D
    small_g["norm_a_g"] = lax.dynamic_slice(small_g["norm_a_g"], (my_chip * (D // NCHIP),), (D // NCHIP,))

    axes = [SPLIT[n] for n in BIG]
    halves = []
    for n, ax, t32, parts in zip(BIG, axes, chip_f32, arrived):
        if t32.ndim == 1:
            own = lax.dynamic_slice_in_dim(t32, my_chip * (t32.shape[0] // NCHIP), t32.shape[0] // NCHIP)
        else:
            own = lax.dynamic_index_in_dim(t32, my_chip, axis=0, keepdims=False)
        halves.append(_mesh_sum(own, parts, ax, "mesh_sum_" + n))
    sibling_done = _to_sibling(halves, "finished_halves")

    res = {}
    for n, ax, mine_half, their_half in zip(BIG, axes, halves, sibling_done):
        out4 = _adamw_halves(w2d[n], mine_half, their_half, shard2d(m[n], n), shard2d(v[n], n), ax, "adamw_" + n)
        res[n] = tuple(unflat(t, n) for t in out4)
    row = lambda t: t.reshape(1, -1)
    small_out = _adamw_small(*[[row(d[n]) for n in SMALL] for d in (w, small_g, m, v)])
    for i, n in enumerate(SMALL):
        res[n] = tuple(t.reshape(w[n].shape) for t in (small_g[n],) + tuple(out[i] for out in small_out))

    outs = [loss, d_x[None]]
    for k in range(4):
        outs += [res[n][k] for n in NAMES]
    return tuple(outs)
```

```python
import numpy as np
import jax
import jax.numpy as jnp
from jax import lax
from jax.experimental import pallas as pl
from jax.experimental.pallas import tpu as pltpu

F32, BF16 = jnp.float32, jnp.bfloat16
S, D, HD, NH, NKV = 2048, 1024, 64, 16, 4
KVW = NKV * HD
WINDOW = 128
ROT = HD // 4
THETA = 500000.0
EPS = 1e-6
SCALE = HD ** -0.5
LANES = 128
SUBLANES = 8
NEG = -1e30
VMEM_LIMIT = 48 * 2 ** 20
ROWS = 512
ATT = 512
SWQ = 4
NCHIP = 4
ADAM_LR, ADAM_B1, ADAM_B2, ADAM_EPS, ADAM_WD, ADAM_STEP = 0.001, 0.9, 0.999, 1e-08, 0.01, 10
NT = (((1,), (1,)), ((), ()))
TN = (((0,), (0,)), ((), ()))
MESH = pl.DeviceIdType.MESH


def _params(n):
    return pltpu.CompilerParams(dimension_semantics=("arbitrary",) * n, vmem_limit_bytes=VMEM_LIMIT)


def _dot(a, b, dims=None):
    if dims is None:
        return jnp.dot(a, b, preferred_element_type=F32)
    return lax.dot_general(a, b, dims, preferred_element_type=F32)


def _dot_split(a, b, n):
    out, rest = None, a
    for _ in range(n):
        hi = rest.astype(BF16)
        term = _dot(hi, b)
        out = term if out is None else out + term
        rest = rest - hi.astype(F32)
    return out


def _seg_mat(w):
    e = (np.arange(w)[:, None] // HD == np.arange(LANES)[None, :]).astype(np.float32)
    return jnp.asarray(e, BF16)


def _spread(r, w):
    head = lax.broadcasted_iota(jnp.int32, (2 * LANES, w), 1) >> (HD.bit_length() - 1)
    row = lax.broadcasted_iota(jnp.int32, (2 * LANES, w), 0)
    et2 = jnp.where(head == (row & (LANES - 1)), 1.0, 0.0).astype(BF16)
    hi = r.astype(BF16)
    lo = (r - hi.astype(F32)).astype(BF16)
    return _dot(jnp.concatenate([hi, lo], axis=1), et2)


def _head_rstd(x, e):
    ss = _dot_split(x * x, e, 2)
    return _spread(lax.rsqrt(ss * (1.0 / HD) + EPS), x.shape[1])


def _rope(x, c, a, b):
    w = x.shape[1]
    return x * c + pltpu.roll(x, w - ROT // 2, 1) * a + pltpu.roll(x, ROT // 2, 1) * b


def _rope_t(dy, c, a, b):
    w = dy.shape[1]
    return dy * c + pltpu.roll(dy * b, w - ROT // 2, 1) + pltpu.roll(dy * a, ROT // 2, 1)


def _sigmoid(x):
    return 1.0 / (1.0 + jnp.exp(-x))


def _row_spec(shape, ts):
    nd = len(shape)
    if shape[0] == S:
        return pl.BlockSpec((ts,) + tuple(shape[1:]), lambda i: (i,) + (0,) * (nd - 1))
    return pl.BlockSpec(tuple(shape), lambda i: (0,) * nd)


def _rows_call(body, name, ins, outs, ts=ROWS):
    return pl.pallas_call(
        body, name=name, grid=(S // ts,),
        in_specs=[_row_spec(a.shape, ts) for a in ins],
        out_specs=[_row_spec(s, ts) for s, _ in outs],
        out_shape=[jax.ShapeDtypeStruct(s, d) for s, d in outs],
        compiler_params=_params(1))(*ins)


def _col_spec(ts, w, col):
    return pl.BlockSpec((ts, w), lambda i: (i, col))


TM = TN_ = 512
TM_TOKENS = 1024


def _mm(name, m, n, terms, out_dtype=F32, add=None, tm=None, tn=TN_, stacked=False, riding=None, rows_of=None):
    nterm = len(terms)
    if tm is None:
        tm = TM_TOKENS if m == S else TM
    nj, ni_ = n // tn, m // tm
    n_in = 2 * nterm + (add is not None) + (rows_of is not None and rows_of[0] is not None)
    r_in, r_out = (len(riding.ins), len(riding.outs)) if riding is not None else (0, 0)

    def body(*refs):
        if riding is not None:
            j, i = pl.program_id(0), pl.program_id(1)
            at_end = riding.hooks(refs[n_in:n_in + r_in], refs[n_in + r_in + 1:n_in + r_in + 1 + r_out],
                                  *refs[n_in + r_in + 1 + r_out:], first=(j == 0) & (i == 0),
                                  middle=(j == nj // 2) & (i == 0), last=(j == nj - 1) & (i == ni_ - 1))
        acc = None
        for t in range(nterm):
            part = _dot(refs[2 * t][...], refs[2 * t + 1][...], terms[t][4])
            acc = part if acc is None else acc + part
        if add is not None:
            acc = acc + refs[2 * nterm][...]
        refs[n_in + r_in][...] = acc.astype(out_dtype)
        if riding is not None:
            at_end()

    tile = pl.BlockSpec((tm, tn), lambda j, i: (i, j))
    ins, specs = [], []
    for a, a_spec, b, b_spec, _ in terms:
        ins += [a, b]
        specs += [a_spec, b_spec]
    if add is not None:
        ins.append(add)
        specs.append(tile)
    out_spec = pl.BlockSpec((None, tm, tn), lambda j, i: (j, i, 0)) if stacked else tile
    out_shape = jax.ShapeDtypeStruct((nj, m, tn) if stacked else (m, n), out_dtype)
    if rows_of is not None:
        taller, rows, row0 = rows_of
        out_spec = pl.BlockSpec((pl.Element(tm), pl.Element(tn)), lambda j, i: (
            pl.multiple_of(row0 + i * tm, SUBLANES), pl.multiple_of(j * tn, LANES)))
        out_shape = jax.ShapeDtypeStruct((rows, n), out_dtype)
        alias = {}
        if taller is not None:
            ins.append(taller)
            specs.append(pl.BlockSpec(memory_space=pltpu.HBM))
            alias = {len(ins) - 1: 0}
        return pl.pallas_call(body, name=name, grid=(nj, ni_), in_specs=specs, out_specs=out_spec,
                              out_shape=out_shape, input_output_aliases=alias, compiler_params=_params(2))(*ins)
    if riding is None:
        return pl.pallas_call(body, name=name, grid=(nj, ni_), in_specs=specs, out_specs=out_spec,
                              out_shape=out_shape, compiler_params=_params(2))(*ins)
    res = pl.pallas_call(
        body, name=name, grid=(nj, ni_), in_specs=specs + riding.in_specs,
        out_specs=[out_spec] + riding.out_specs, out_shape=[out_shape] + riding.out_shape,
        scratch_shapes=riding.scratch, compiler_params=_params(2))(*ins, *riding.ins)
    return res[0], res[1:]


def _a_rows(k, col=0, tm=TM_TOKENS):
    return pl.BlockSpec((tm, k), lambda j, i: (i, col))


def _a_cols(k, tm=TM):
    return pl.BlockSpec((k, tm), lambda j, i: (0, i))


def _b_cols(k, row=0, col0=0, tn=TN_):
    return pl.BlockSpec((k, tn), lambda j, i: (row, col0 + j))


def _b_rows(k, row0=0, tn=TN_):
    return pl.BlockSpec((tn, k), lambda j, i: (row0 + j, 0))


def _rmsnorm_fwd(x, gains, name):
    def body(*refs):
        xv = refs[0][...]
        r = lax.rsqrt(jnp.mean(xv * xv, axis=-1, keepdims=True) + EPS)
        xh = xv * r
        for n in range(len(gains)):
            refs[1 + len(gains) + n][...] = (xh * refs[1 + n][...]).astype(BF16)

    return _rows_call(body, name, [x] + list(gains), [((S, D), BF16)] * len(gains))


def _rmsnorm_bwd(x, gains, dus, dres, name):
    n = len(gains)

    def body(*refs):
        x_ref, g_refs, du_refs, dres_ref = refs[0], refs[1:1 + n], refs[1 + n:1 + 2 * n], refs[1 + 2 * n]
        dx_ref, dxb_ref, dg_refs = refs[2 + 2 * n], refs[3 + 2 * n], refs[4 + 2 * n:]
        xv = x_ref[...]
        r = lax.rsqrt(jnp.mean(xv * xv, axis=-1, keepdims=True) + EPS)
        xh = xv * r
        gy = None
        for m in range(n):
            du = du_refs[m][...]
            part = jnp.sum(du * xh, axis=0, keepdims=True)

            @pl.when(pl.program_id(0) == 0)
            def _(m=m, part=part):
                dg_refs[m][...] = part

            @pl.when(pl.program_id(0) != 0)
            def _(m=m, part=part):
                dg_refs[m][...] += part

            t = du * g_refs[m][...]
            gy = t if gy is None else gy + t
        dx = dres_ref[...] + r * (gy - xh * jnp.mean(gy * xh, axis=-1, keepdims=True))
        dx_ref[...] = dx
        dxb_ref[...] = dx.astype(BF16)

    outs = [((S, D), F32), ((S, D), BF16)] + [((1, D), F32)] * n
    return _rows_call(body, name, [x] + list(gains) + list(dus) + [dres], outs)


def _a_post(qkvg, qg, kg):
    e = _seg_mat(D)

    def body(q_ref, k_ref, v_ref, qg_ref, kg_ref, e_ref, qo, ko, vo):
        ev = e_ref[...]
        qv, kv = q_ref[...], k_ref[...]
        qo[...] = (qv * _head_rstd(qv, ev) * qg_ref[...] * SCALE).astype(BF16)
        ko[...] = (kv * _head_rstd(kv, ev) * kg_ref[...]).astype(BF16)
        vo[...] = v_ref[...].astype(BF16)

    whole = lambda a: pl.BlockSpec(a.shape, lambda i: (0, 0))
    return pl.pallas_call(
        body, name="a_post", grid=(S // ROWS,),
        in_specs=[_col_spec(ROWS, D, 0), _col_spec(ROWS, D, 1), _col_spec(ROWS, D, 2),
                  whole(qg), whole(kg), whole(e)],
        out_specs=[_col_spec(ROWS, D, 0)] * 3,
        out_shape=[jax.ShapeDtypeStruct((S, D), BF16)] * 3,
        compiler_params=_params(1))(qkvg, qkvg, qkvg, qg, kg, e)


def _tri(upper):
    r, c = np.arange(ROWS)[:, None], np.arange(ROWS)[None, :]
    return jnp.asarray((r <= c) if upper else (r >= c), BF16)


def _forget_cumsum(fpad, bpad):
    def body(f_ref, b_ref, u_ref, c_ref, carry):
        @pl.when(pl.program_id(0) == 0)
        def _():
            carry[...] = jnp.zeros_like(carry)

        lf = jax.nn.log_sigmoid(f_ref[...] + b_ref[...])
        blk = _dot_split(lf.T, u_ref[...], 3) + carry[:, 0:1]
        c_ref[...] = blk
        carry[...] = jnp.broadcast_to(blk[:, ROWS - 1:ROWS], carry.shape)

    return pl.pallas_call(
        body, name="forget_cumsum", grid=(S // ROWS,),
        in_specs=[pl.BlockSpec((ROWS, LANES), lambda i: (i, 0)), pl.BlockSpec((1, LANES), lambda i: (0, 0)),
                  pl.BlockSpec((ROWS, ROWS), lambda i: (0, 0))],
        out_specs=pl.BlockSpec((LANES, ROWS), lambda i: (0, i)),
        out_shape=jax.ShapeDtypeStruct((LANES, S), F32),
        scratch_shapes=[pltpu.VMEM((LANES, LANES), F32)],
        compiler_params=_params(1))(fpad, bpad, _tri(True))


def _forget_bwd(dct, fpad, bpad):
    nb = S // ROWS

    def body(dc_ref, f_ref, b_ref, l_ref, df_ref, db_ref, carry):
        @pl.when(pl.program_id(0) == 0)
        def _():
            carry[...] = jnp.zeros_like(carry)
            db_ref[...] = jnp.zeros_like(db_ref)

        blk = _dot_split(dc_ref[...], l_ref[...], 3) + carry[:, 0:1]
        carry[...] = jnp.broadcast_to(blk[:, 0:1], carry.shape)
        df = blk.T * _sigmoid(-(f_ref[...] + b_ref[...]))
        df_ref[...] = df.astype(BF16)
        db_ref[...] += jnp.sum(df, axis=0, keepdims=True)

    return pl.pallas_call(
        body, name="forget_bwd", grid=(nb,),
        in_specs=[pl.BlockSpec((LANES, ROWS), lambda i: (0, nb - 1 - i)),
                  pl.BlockSpec((ROWS, LANES), lambda i: (nb - 1 - i, 0)),
                  pl.BlockSpec((1, LANES), lambda i: (0, 0)), pl.BlockSpec((ROWS, ROWS), lambda i: (0, 0))],
        out_specs=[pl.BlockSpec((ROWS, LANES), lambda i: (nb - 1 - i, 0)), pl.BlockSpec((1, LANES), lambda i: (0, 0))],
        out_shape=[jax.ShapeDtypeStruct((S, LANES), BF16), jax.ShapeDtypeStruct((1, LANES), F32)],
        scratch_shapes=[pltpu.VMEM((LANES, LANES), F32)],
        compiler_params=_params(1))(dct, fpad, bpad, _tri(False))


def _headnorm_bwd(x, col, gain, dy, rope, name):
    e = _seg_mat(D)
    tabs = list(rope) if rope is not None else []

    def body(*refs):
        x_ref, g_ref, dy_ref, e_ref = refs[:4]
        dx_ref, dg_ref = refs[-2:]
        xv, dyv, ev = x_ref[...], dy_ref[...], e_ref[...]
        if rope is not None:
            c, a, b = (jnp.tile(t[...], (1, D // LANES)) for t in refs[4:7])
            dyv = _rope_t(dyv, c, a, b)
        r = _head_rstd(xv, ev)
        xh = xv * r
        part = jnp.sum(dyv * xh, axis=0, keepdims=True)

        @pl.when(pl.program_id(0) == 0)
        def _():
            dg_ref[...] = part

        @pl.when(pl.program_id(0) != 0)
        def _():
            dg_ref[...] += part

        gy = dyv * g_ref[...]
        seg = _spread(_dot_split(gy * xh, ev, 2) * (1.0 / HD), D)
        dx_ref[...] = (r * (gy - xh * seg)).astype(BF16)

    whole = lambda a: pl.BlockSpec(a.shape, lambda i: (0, 0))
    return pl.pallas_call(
        body, name=name, grid=(S // ROWS,),
        in_specs=[_col_spec(ROWS, D, col), whole(gain), _col_spec(ROWS, D, 0), whole(e)]
                 + [pl.BlockSpec((ROWS, LANES), lambda i: (i, 0))] * len(tabs),
        out_specs=[_col_spec(ROWS, D, 0), whole(gain)],
        out_shape=[jax.ShapeDtypeStruct((S, D), BF16), jax.ShapeDtypeStruct((1, D), F32)],
        compiler_params=_params(1))(x, gain, dy, e, *tabs)


def _dup_mat():
    r, c = np.arange(KVW)[:, None], np.arange(2 * KVW)[None, :]
    return (r // HD == c // LANES) & (r % HD == c % HD)


def _fold_mat():
    r, c = np.arange(D)[:, None], np.arange(KVW)[None, :]
    return (r // (2 * LANES) == c // HD) & (r % HD == c % HD)


def _b_post(pb, kv, qg, kg, rope):
    e, ek = _seg_mat(D), _seg_mat(KVW)
    dup = jnp.asarray(_dup_mat(), BF16)

    def body(q_ref, k_ref, v_ref, qg_ref, kg_ref, e_ref, ek_ref, dup_ref, c_ref, a_ref, b_ref, qo, ko, vo):
        c1, a1, b1 = c_ref[...], a_ref[...], b_ref[...]
        qv = q_ref[...]
        qn = qv * _head_rstd(qv, e_ref[...]) * qg_ref[...]
        t = lambda z, n: jnp.tile(z, (1, n))
        qo[...] = (_rope(qn, t(c1, D // LANES), t(a1, D // LANES), t(b1, D // LANES)) * SCALE).astype(BF16)
        kvv = k_ref[...]
        kn = kvv * _head_rstd(kvv, ek_ref[...]) * kg_ref[...]
        kr = _rope(kn, t(c1, KVW // LANES), t(a1, KVW // LANES), t(b1, KVW // LANES)).astype(BF16)
        ko[...] = _dot(kr, dup_ref[...]).astype(BF16)
        vo[...] = _dot(v_ref[...].astype(BF16), dup_ref[...]).astype(BF16)

    whole = lambda a: pl.BlockSpec(a.shape, lambda i: (0, 0))
    tab = pl.BlockSpec((ROWS, LANES), lambda i: (i, 0))
    return pl.pallas_call(
        body, name="b_post", grid=(S // ROWS,),
        in_specs=[_col_spec(ROWS, D, 0), _col_spec(ROWS, KVW, 0), _col_spec(ROWS, KVW, 1),
                  whole(qg), whole(kg), whole(e), whole(ek), whole(dup), tab, tab, tab],
        out_specs=[_col_spec(ROWS, D, 0), _col_spec(ROWS, 2 * KVW, 0), _col_spec(ROWS, 2 * KVW, 0)],
        out_shape=[jax.ShapeDtypeStruct((S, D), BF16), jax.ShapeDtypeStruct((S, 2 * KVW), BF16),
                   jax.ShapeDtypeStruct((S, 2 * KVW), BF16)],
        compiler_params=_params(1))(pb, kv, kv, qg, kg, e, ek, dup, *rope)


def _kv_bwd(dkdup, dvdup, kv, kg, rope):
    ek = _seg_mat(KVW)
    fold = jnp.asarray(_fold_mat(), BF16)

    def body(dk_ref, dv_ref, k_ref, kg_ref, ek_ref, fold_ref, c_ref, a_ref, b_ref, dkv_ref, dg_ref):
        ev, fv = ek_ref[...], fold_ref[...]
        t = lambda z: jnp.tile(z[...], (1, KVW // LANES))
        dk = _rope_t(_dot_split(dk_ref[...], fv, 2), t(c_ref), t(a_ref), t(b_ref))
        dv = _dot_split(dv_ref[...], fv, 2)
        xv = k_ref[...]
        r = _head_rstd(xv, ev)
        xh = xv * r
        part = jnp.sum(dk * xh, axis=0, keepdims=True)

        @pl.when(pl.program_id(0) == 0)
        def _():
            dg_ref[...] = part

        @pl.when(pl.program_id(0) != 0)
        def _():
            dg_ref[...] += part

        gy = dk * kg_ref[...]
        seg = _spread(_dot_split(gy * xh, ev, 2) * (1.0 / HD), KVW)
        dkv_ref[:, 0:KVW] = (r * (gy - xh * seg)).astype(BF16)
        dkv_ref[:, KVW:2 * KVW] = dv.astype(BF16)

    whole = lambda a: pl.BlockSpec(a.shape, lambda i: (0, 0))
    tab = pl.BlockSpec((ROWS, LANES), lambda i: (i, 0))
    return pl.pallas_call(
        body, name="kv_bwd", grid=(S // ROWS,),
        in_specs=[_col_spec(ROWS, D, 0), _col_spec(ROWS, D, 0), _col_spec(ROWS, KVW, 0),
                  whole(kg), whole(ek), whole(fold), tab, tab, tab],
        out_specs=[_col_spec(ROWS, 2 * KVW, 0), whole(kg)],
        out_shape=[jax.ShapeDtypeStruct((S, 2 * KVW), BF16), jax.ShapeDtypeStruct((1, KVW), F32)],
        compiler_params=_params(1))(dkdup, dvdup, kv, kg, ek, fold, *rope)


def _loss_head(out, target):
    def body(o_ref, t_ref, d_ref, db_ref, l_ref):
        diff = o_ref[...] - t_ref[...]
        d = diff * (1.0 / D)
        d_ref[...] = d
        db_ref[...] = d.astype(BF16)

        @pl.when(pl.program_id(0) == 0)
        def _():
            l_ref[...] = jnp.zeros_like(l_ref)

        l_ref[...] += jnp.sum(diff * diff, axis=0, keepdims=True)

    return _rows_call(body, "loss_head", [out, target], [((S, D), F32), ((S, D), BF16), ((1, D), F32)])


def _lane():
    return lax.broadcasted_iota(jnp.int32, (1, LANES), 1)


def _head_mask(hh):
    return (_lane() < HD) if hh == 0 else (_lane() >= HD)


def _fox_fwd(q, k, v, ct, gate, riding):
    nq, npair = S // ATT, NH // 2
    ni, no = len(riding.ins), len(riding.outs)

    def body(q_ref, k_ref, v_ref, c_ref, gate_ref, *rest):
        o_ref, lse_ref, y_ref = rest[ni:ni + 3]
        pair, i = pl.program_id(0), pl.program_id(1)
        at_end = riding.hooks(rest[:ni], rest[ni + 3:ni + 3 + no], *rest[ni + 3 + no:],
                              first=(pair == 0) & (i == 0), middle=(pair == npair // 2) & (i == 0),
                              last=(pair == npair - 1) & (i == nq - 1))
        q2 = q_ref[...]
        qms = [jnp.where(_head_mask(hh), q2, jnp.zeros_like(q2)) for hh in (0, 1)]

        def probs(off, width, m, hh, diag):
            s = _dot(qms[hh], k_ref[pl.ds(off, width), :], NT) - c_ref[hh:hh + 1, pl.ds(off, width)]
            if diag:
                row = i * ATT + lax.broadcasted_iota(jnp.int32, (ATT, width), 0)
                col = off + lax.broadcasted_iota(jnp.int32, (ATT, width), 1)
                s = jnp.where(col <= row, s, NEG)
            m_new = jnp.maximum(m, jnp.max(s, axis=1, keepdims=True))
            p = jnp.exp(s - m_new)
            p_hi = p.astype(BF16)
            return m_new, jnp.exp(m - m_new), p_hi, (p - p_hi.astype(F32)).astype(BF16)

        def weighted(off, width, p_hi, p_lo, hh):
            vj = v_ref[pl.ds(off, width), :]
            v1 = jnp.where(_head_mask(hh), vj, jnp.ones_like(vj))
            return _dot(p_hi, v1) + _dot(p_lo, v1)

        def step(off, width, carry, diag):
            off = pl.multiple_of(off, ATT)
            out = []
            for hh in (0, 1):
                m, acc = carry[hh]
                m, alpha, p_hi, p_lo = probs(off, width, m, hh, diag)
                out.append((m, alpha * acc + weighted(off, width, p_hi, p_lo, hh)))
            return tuple(out)

        one = (jnp.full((ATT, 1), NEG, F32), jnp.zeros((ATT, LANES), F32))
        carry = lax.fori_loop(0, i // 2, lambda j, cr: step(j * (2 * ATT), 2 * ATT, cr, False), (one, one))
        carry = lax.cond(i % 2 == 1, lambda cr: step((i - 1) * ATT, 2 * ATT, cr, True),
                         lambda cr: step(i * ATT, ATT, cr, True), carry)
        res = []
        for hh in (0, 1):
            m, acc = carry[hh]
            l = jnp.max(jnp.where(_head_mask(1 - hh), acc, 0.0), axis=1, keepdims=True)
            res.append((acc / l, m + jnp.log(l)))
        first = _head_mask(0)
        o = jnp.where(first, res[0][0], res[1][0])
        o_ref[...] = o
        lse_ref[...] = jnp.where(first, res[0][1], res[1][1])
        g = gate_ref[...]
        y_ref[...] = (o * (g * _sigmoid(g))).astype(BF16)
        at_end()

    blk = pl.BlockSpec((ATT, LANES), lambda p, i: (i, p))
    full = pl.BlockSpec((S, LANES), lambda p, i: (0, p))
    res = pl.pallas_call(
        body, name="fox_fwd", grid=(npair, nq),
        in_specs=[blk, full, full, pl.BlockSpec((None, 2, S), lambda p, i: (p, 0, 0)), blk] + riding.in_specs,
        out_specs=[blk, blk, blk] + riding.out_specs,
        out_shape=[jax.ShapeDtypeStruct((S, D), F32)] * 2 + [jax.ShapeDtypeStruct((S, D), BF16)] + riding.out_shape,
        scratch_shapes=riding.scratch,
        compiler_params=_params(2))(q, k, v, ct, gate, *riding.ins)
    return res[0], res[1], res[2], res[3:]


def _gate_grads(dy, o, g):
    sg = _sigmoid(g)
    return dy * (g * sg), dy * o * (sg * (1.0 + g * (1.0 - sg)))


def _fox_bwd(q, k, v, ct, o, lse, dy, gate, riding):
    nq, npair = S // ATT, NH // 2
    ni, no = len(riding.ins), len(riding.outs)

    def body(q_ref, k_ref, v_ref, c_ref, o_ref, lse_ref, dy_ref, gate_ref, *rest):
        dq_ref, dk_ref, dvb_ref, dc_ref, dgate_ref = rest[ni:ni + 5]
        dv_ref = rest[ni + 5 + no]
        pair, i = pl.program_id(0), pl.program_id(1)
        at_end = riding.hooks(rest[:ni], rest[ni + 5:ni + 5 + no], *rest[ni + 6 + no:],
                              first=(pair == 0) & (i == 0), middle=(pair == npair // 2) & (i == 0),
                              last=(pair == npair - 1) & (i == nq - 1))

        @pl.when(i == 0)
        def _():
            dk_ref[...] = jnp.zeros_like(dk_ref)
            dv_ref[...] = jnp.zeros_like(dv_ref)
            dc_ref[...] = jnp.zeros_like(dc_ref)

        q2, lse2 = q_ref[...], lse_ref[...]
        do2, dgate = _gate_grads(dy_ref[...], o_ref[...], gate_ref[...])
        dgate_ref[...] = dgate.astype(BF16)
        do2b = do2.astype(BF16)
        prod = do2b.astype(F32) * o_ref[...]
        heads = []
        for hh in (0, 1):
            hm = _head_mask(hh)
            heads.append((jnp.where(hm, q2, jnp.zeros_like(q2)), jnp.where(hm, do2b, jnp.zeros_like(do2b)),
                          jnp.sum(jnp.where(hm, prod, 0.0), axis=1, keepdims=True),
                          jnp.max(jnp.where(hm, lse2, NEG), axis=1, keepdims=True)))

        def step(off, width, dqs, diag):
            off = pl.multiple_of(off, ATT)
            kj, vj = k_ref[pl.ds(off, width), :], v_ref[pl.ds(off, width), :]
            dk, dv, out = None, None, []
            for hh in (0, 1):
                qm, dom, delta, lse_h = heads[hh]
                s = _dot(qm, kj, NT) - c_ref[hh:hh + 1, pl.ds(off, width)]
                p = jnp.exp(s - lse_h)
                if diag:
                    row = i * ATT + lax.broadcasted_iota(jnp.int32, (ATT, width), 0)
                    col = off + lax.broadcasted_iota(jnp.int32, (ATT, width), 1)
                    p = jnp.where(col <= row, p, 0.0)
                ds = p * (_dot(dom, vj, NT) - delta)
                dc_ref[hh:hh + 1, pl.ds(off, width)] += -jnp.sum(ds, axis=0, keepdims=True)
                dsb = ds.astype(BF16)
                dk_h, dv_h = _dot(dsb, qm, TN), _dot(p.astype(BF16), dom, TN)
                dk, dv = (dk_h, dv_h) if dk is None else (dk + dk_h, dv + dv_h)
                out.append(dqs[hh] + _dot(dsb, kj))
            dk_ref[pl.ds(off, width), :] += dk
            dv_ref[pl.ds(off, width), :] += dv
            return tuple(out)

        zero = jnp.zeros((ATT, LANES), F32)
        dqs = lax.fori_loop(0, i // 2, lambda j, acc: step(j * (2 * ATT), 2 * ATT, acc, False), (zero, zero))
        dqs = lax.cond(i % 2 == 1, lambda acc: step((i - 1) * ATT, 2 * ATT, acc, True),
                       lambda acc: step(i * ATT, ATT, acc, True), dqs)
        dq_ref[...] = jnp.where(_head_mask(0), dqs[0], dqs[1]) * SCALE

        @pl.when(i == nq - 1)
        def _():
            dvb_ref[...] = dv_ref[...].astype(BF16)

        at_end()

    blk = pl.BlockSpec((ATT, LANES), lambda p, i: (i, p))
    full = pl.BlockSpec((S, LANES), lambda p, i: (0, p))
    cspec = pl.BlockSpec((None, 2, S), lambda p, i: (p, 0, 0))
    res = pl.pallas_call(
        body, name="fox_bwd", grid=(npair, nq),
        in_specs=[blk, full, full, cspec, blk, blk, blk, blk] + riding.in_specs,
        out_specs=[blk, full, full, cspec, blk] + riding.out_specs,
        out_shape=[jax.ShapeDtypeStruct((S, D), F32)] * 2 + [jax.ShapeDtypeStruct((S, D), BF16),
                                                              jax.ShapeDtypeStruct((npair, 2, S), F32),
                                                              jax.ShapeDtypeStruct((S, D), BF16)]
                  + riding.out_shape,
        scratch_shapes=[pltpu.VMEM((S, LANES), F32)] + riding.scratch,
        compiler_params=_params(2))(q, k, v, ct, o, lse, dy, gate, *riding.ins)
    return res[0], res[1], res[2], res[3], res[4], res[5:]


def _both_heads(x):
    return jnp.concatenate([jnp.where(_head_mask(hh), x, jnp.zeros_like(x)) for hh in (0, 1)], axis=0)


def _per_head(col0, col1):
    return jnp.concatenate([jnp.broadcast_to(col0, (WINDOW, 1)), jnp.broadcast_to(col1, (WINDOW, 1))], axis=0)


def _unstack(x2):
    return jnp.where(_head_mask(0), x2[:WINDOW], x2[WINDOW:])


def _swa_valid(i, start):
    r = lax.broadcasted_iota(jnp.int32, (2 * WINDOW, 2 * WINDOW), 0)
    qabs = i * WINDOW + jnp.where(r >= WINDOW, r - WINDOW, r)
    kabs = start + lax.broadcasted_iota(jnp.int32, (2 * WINDOW, 2 * WINDOW), 1)
    return (kabs <= qabs) & (qabs - kabs < WINDOW)


def _swa_fwd(q, kdup, vdup, sinks_t, proj, gate_col):
    def body(q_ref, k_ref, v_ref, sk_ref, gate_ref, o_ref, lse_ref, y_ref):
        skv = sk_ref[...]
        first = _head_mask(0)
        for sb in range(SWQ):
            i = pl.program_id(1) * SWQ + sb
            rows = slice(sb * WINDOW, (sb + 1) * WINDOW)
            start = pl.multiple_of(jnp.maximum(i - 1, 0) * WINDOW, WINDOW)
            kk, vv = k_ref[pl.ds(start, 2 * WINDOW), :], v_ref[pl.ds(start, 2 * WINDOW), :]
            q2 = q_ref[rows, :]
            valid = _swa_valid(i, start)[:WINDOW]
            res = []
            for hh in (0, 1):
                hm = _head_mask(hh)
                sink = jnp.max(jnp.where(hm, skv, NEG), axis=1, keepdims=True)
                s = jnp.where(valid, _dot(jnp.where(hm, q2, jnp.zeros_like(q2)), kk, NT), NEG)
                m = jnp.maximum(jnp.max(s, axis=1, keepdims=True), sink)
                p = jnp.exp(s - m)
                l = jnp.sum(p, axis=1, keepdims=True) + jnp.exp(sink - m)
                res.append((_dot(p.astype(BF16), vv) / l, m + jnp.log(l)))
            o = jnp.where(first, res[0][0], res[1][0])
            o_ref[rows, :] = o
            lse_ref[rows, :] = jnp.where(first, res[0][1], res[1][1])
            g = gate_ref[rows, :]
            y_ref[rows, :] = (o * (g * _sigmoid(g))).astype(BF16)

    blk = pl.BlockSpec((SWQ * WINDOW, LANES), lambda p, i: (i, p))
    gate = pl.BlockSpec((SWQ * WINDOW, LANES), lambda p, i: (i, gate_col + p))
    full = pl.BlockSpec((S, LANES), lambda p, i: (0, p // 2))
    return pl.pallas_call(
        body, name="swa_fwd", grid=(NH // 2, S // (SWQ * WINDOW)),
        in_specs=[blk, full, full, pl.BlockSpec((1, LANES), lambda p, i: (0, p)), gate],
        out_specs=[blk, blk, blk],
        out_shape=[jax.ShapeDtypeStruct((S, D), F32)] * 2 + [jax.ShapeDtypeStruct((S, D), BF16)],
        compiler_params=_params(2))(q, kdup, vdup, sinks_t, proj)


def _swa_bwd(q, kdup, vdup, sinks_t, o, lse, dy, proj, gate_col):
    def body(q_ref, k_ref, v_ref, sk_ref, o_ref, lse_ref, dy_ref, gate_ref, dq_ref, dk_ref, dv_ref, dsk_ref,
             dgate_ref):
        @pl.when(pl.program_id(1) == 0)
        def _():
            dk_ref[...] = jnp.zeros_like(dk_ref)
            dv_ref[...] = jnp.zeros_like(dv_ref)
            dsk_ref[...] = jnp.zeros_like(dsk_ref)

        skv = sk_ref[...]
        first = _head_mask(0)
        sink = _per_head(*[jnp.max(jnp.where(_head_mask(hh), skv, NEG), axis=1, keepdims=True) for hh in (0, 1)])
        for sb in range(SWQ):
            i = pl.program_id(1) * SWQ + sb
            rows = slice(sb * WINDOW, (sb + 1) * WINDOW)
            start = pl.multiple_of(jnp.maximum(i - 1, 0) * WINDOW, WINDOW)
            kk, vv = k_ref[pl.ds(start, 2 * WINDOW), :], v_ref[pl.ds(start, 2 * WINDOW), :]
            do2, dgate = _gate_grads(dy_ref[rows, :], o_ref[rows, :], gate_ref[rows, :])
            dgate_ref[rows, :] = dgate.astype(BF16)
            do2b = do2.astype(BF16)
            prod, lse2 = do2b.astype(F32) * o_ref[rows, :], lse_ref[rows, :]
            qs, dos = _both_heads(q_ref[rows, :]), _both_heads(do2b)
            delta = jnp.concatenate([jnp.sum(jnp.where(_head_mask(hh), prod, 0.0), axis=1, keepdims=True)
                                     for hh in (0, 1)], axis=0)
            lse_h = jnp.concatenate([jnp.max(jnp.where(_head_mask(hh), lse2, NEG), axis=1, keepdims=True)
                                     for hh in (0, 1)], axis=0)
            p = jnp.where(_swa_valid(i, start), jnp.exp(_dot(qs, kk, NT) - lse_h), 0.0)
            dsb = (p * (_dot(dos, vv, NT) - delta)).astype(BF16)
            dk_ref[pl.ds(start, 2 * WINDOW), :] += _dot(dsb, qs, TN)
            dv_ref[pl.ds(start, 2 * WINDOW), :] += _dot(p.astype(BF16), dos, TN)
            dq_ref[rows, :] = _unstack(_dot(dsb, kk)) * SCALE
            t = jnp.exp(sink - lse_h) * delta
            dsk_ref[...] += -jnp.where(first, jnp.sum(t[:WINDOW], axis=0, keepdims=True),
                                       jnp.sum(t[WINDOW:], axis=0, keepdims=True))

    blk = pl.BlockSpec((SWQ * WINDOW, LANES), lambda p, i: (i, p))
    full = pl.BlockSpec((S, LANES), lambda p, i: (0, p // 2))
    acc = pl.BlockSpec((S, LANES), lambda p, i: (0, p))
    sk = pl.BlockSpec((1, LANES), lambda p, i: (0, p))
    gate = pl.BlockSpec((SWQ * WINDOW, LANES), lambda p, i: (i, gate_col + p))
    return pl.pallas_call(
        body, name="swa_bwd", grid=(NH // 2, S // (SWQ * WINDOW)),
        in_specs=[blk, full, full, sk, blk, blk, blk, gate],
        out_specs=[blk, acc, acc, sk, blk],
        out_shape=[jax.ShapeDtypeStruct((S, D), F32)] * 3 + [jax.ShapeDtypeStruct((1, D), F32),
                                                              jax.ShapeDtypeStruct((S, D), BF16)],
        compiler_params=_params(2))(q, kdup, vdup, sinks_t, o, lse, dy, proj)


def _adamw_math(w, g, m, v):
    m = ADAM_B1 * m + (1.0 - ADAM_B1) * g
    v = ADAM_B2 * v + (1.0 - ADAM_B2) * jnp.square(g)
    m_hat = m / (1.0 - ADAM_B1 ** ADAM_STEP)
    v_hat = v / (1.0 - ADAM_B2 ** ADAM_STEP)
    delta = -ADAM_LR * (m_hat / (jnp.sqrt(v_hat) + ADAM_EPS) + ADAM_WD * w)
    return delta, m, v


def _adamw_small(ws, gs, ms, vs):
    k = len(ws)

    def body(*refs):
        for p in range(k):
            w_ref, g_ref, m_ref, v_ref = (refs[q * k + p] for q in range(4))
            d, mo, vo = _adamw_math(w_ref[...], g_ref[...], m_ref[...], v_ref[...])
            refs[4 * k + p][...], refs[5 * k + p][...], refs[6 * k + p][...] = d, mo, vo

    res = pl.pallas_call(
        body, name="adamw_small",
        out_shape=[jax.ShapeDtypeStruct(t.shape, F32) for t in ws] * 3)(*ws, *gs, *ms, *vs)
    return res[:k], res[k:2 * k], res[2 * k:]


SUM_TILE = 128


FLAT_BLOCK = 257 * 1024


def _tiles(shape, axis, lead=0):
    if len(shape) == 1:
        count = shape[0] // FLAT_BLOCK
        return (FLAT_BLOCK,), count, lambda pos, *lead_idx: (sum(k * count for k in lead_idx) + pos,)
    r, c = shape
    blk = (SUM_TILE, c) if axis == 0 else (r, SUM_TILE)
    count = shape[axis] // SUM_TILE

    def index(pos, *lead_idx):
        return tuple(lead_idx) + ((pos, 0) if axis == 0 else (0, pos))

    return (None,) * lead + blk, count, index


def _adamw_halves(w, g_mine, g_theirs, m, v, axis, name):
    blk, count, index = _tiles(w.shape, axis)
    per_half = count // 2

    def body(w_ref, a_ref, b_ref, m_ref, v_ref, g_ref, d_ref, mo_ref, vo_ref):
        is_mine = pl.program_id(0) // per_half == lax.axis_index("c")
        g = jnp.where(is_mine, a_ref[...], b_ref[...])
        g_ref[...] = g
        d_ref[...], mo_ref[...], vo_ref[...] = _adamw_math(w_ref[...], g, m_ref[...], v_ref[...])

    spec = pl.BlockSpec(blk, lambda i: index(i))
    half = pl.BlockSpec(blk, lambda i: index(i % per_half))
    return pl.pallas_call(
        body, name=name, grid=(count,), in_specs=[spec, half, half, spec, spec], out_specs=[spec] * 4,
        out_shape=[jax.ShapeDtypeStruct(w.shape, F32)] * 4, compiler_params=_params(1))(w, g_mine, g_theirs, m, v)


def _chip_sum(blocks, from_sibling, axis, name):
    flat = blocks.ndim == 1
    blk, count, index = _tiles((from_sibling.shape[0] // NCHIP,) if flat else from_sibling.shape[1:], axis, lead=1)

    def body(lo_ref, hi_ref, p_ref, o32, o16):
        mine = jnp.where(lax.axis_index("c") == 0, lo_ref[...], hi_ref[...])
        acc = mine + p_ref[...]
        o32[...] = acc
        o16[...] = acc.astype(BF16)

    half = pl.BlockSpec(blk, lambda k, i: index(i, k))
    if flat:
        lo = pl.BlockSpec(blk, lambda k, i: (2 * count * k + i,))
        hi = pl.BlockSpec(blk, lambda k, i: (2 * count * k + count + i,))
    else:
        lo, hi = half, pl.BlockSpec(blk, lambda k, i: index(i + count, k))
    return pl.pallas_call(
        body, name=name, grid=(NCHIP, count), in_specs=[lo, hi, half], out_specs=[half, half],
        out_shape=[jax.ShapeDtypeStruct(from_sibling.shape, F32), jax.ShapeDtypeStruct(from_sibling.shape, BF16)],
        compiler_params=_params(2))(blocks, blocks, from_sibling)


def _mesh_sum(own, parts, axis, name):
    blk, count, index = _tiles(own.shape, axis)
    n = NCHIP - 1

    def body(a_ref, *refs):
        acc = a_ref[...]
        for k in range(n):
            acc = acc + refs[k][...].astype(F32)
        refs[n][...] = acc

    spec = pl.BlockSpec(blk, lambda i: index(i))
    if own.ndim == 1:
        part = [pl.BlockSpec(blk, lambda i, k=k: (k * count + i,)) for k in range(n)]
    else:
        part = [pl.BlockSpec((None,) + blk, lambda i, k=k: (k,) + index(i)) for k in range(n)]
    return pl.pallas_call(
        body, name=name, grid=(count,), in_specs=[spec] + part,
        out_specs=spec, out_shape=jax.ShapeDtypeStruct(own.shape, F32),
        compiler_params=_params(1))(own, *([parts] * n))


def _sum_stack(parts, name):
    n = parts.shape[0]

    def body(p_ref, o_ref):
        acc = p_ref[0]
        for k in range(1, n):
            acc = acc + p_ref[k]
        o_ref[...] = acc

    return pl.pallas_call(body, name=name, out_shape=jax.ShapeDtypeStruct(parts.shape[1:], F32))(parts)


def _coords():
    return lax.axis_index("x"), lax.axis_index("y"), lax.axis_index("c")


def _chip(who):
    return 2 * who[0] + who[1]


def _flip(who, mask):
    return tuple((1 - v) if b else v for v, b in zip(who, mask))


def _transfer(transfers, t, I, O, ssem, rsem, receiving):
    tr, me = transfers[t], _coords()
    peer = _flip(me, tr["mask"])
    return pltpu.make_async_remote_copy(
        src_ref=tr["src"](I, O, me), dst_ref=tr["dst"](I, O, peer if receiving else me),
        send_sem=ssem.at[t], recv_sem=rsem.at[t], device_id=peer, device_id_type=MESH)


def _start_transfers(transfers, I, O, ssem, rsem, onward):
    arrived = set()
    for t, tr in enumerate(transfers):
        after = tr.get("after")
        if (after is not None) != onward:
            continue
        if after is not None and after not in arrived:
            _transfer(transfers, after, I, O, ssem, rsem, True).wait_recv()
            arrived.add(after)
        _transfer(transfers, t, I, O, ssem, rsem, False).start()


def _finish_transfers(transfers, I, O, ssem, rsem):
    passed_on = {tr["after"] for tr in transfers if tr.get("after") is not None}
    for t in range(len(transfers)):
        if t not in passed_on:
            _transfer(transfers, t, I, O, ssem, rsem, True).wait_recv()
    for t in range(len(transfers)):
        _transfer(transfers, t, I, O, ssem, rsem, False).wait_send()


def _own_copies(own, I, O, stage, lsem, leg):
    for n, (src, dst) in enumerate(own):
        me = _coords()
        bring =pltpu.make_async_copy(src(I, O, me), stage[n], lsem.at[2 * n])
        put = pltpu.make_async_copy(stage[n], dst(I, O, me), lsem.at[2 * n + 1])
        if leg == 0:
            bring.start()
        elif leg == 1:
            bring.wait()
            put.start()
        else:
            put.wait()


def _own_scratch(own, ins):
    return [pltpu.VMEM(ins[n].shape, ins[n].dtype) for n in range(len(own))], pltpu.SemaphoreType.DMA((max(2 * len(own), 1),))


def _exchange(name, ins, outs, transfers, own=()):
    ni, no = len(ins), len(outs)
    nt = len(transfers)
    stages, stage_sems = _own_scratch(own, ins)

    def body(*refs):
        I, O = refs[:ni], refs[ni:ni + no]
        ssem, rsem, lsem = refs[ni + no:ni + no + 3]
        stage = refs[ni + no + 3:]
        _own_copies(own, I, O, stage, lsem, 0)
        _start_transfers(transfers, I, O, ssem, rsem, False)
        _own_copies(own, I, O, stage, lsem, 1)
        _start_transfers(transfers, I, O, ssem, rsem, True)
        _finish_transfers(transfers, I, O, ssem, rsem)
        _own_copies(own, I, O, stage, lsem, 2)

    hbm = pl.BlockSpec(memory_space=pltpu.HBM)
    return pl.pallas_call(
        body, name=name, in_specs=[hbm] * ni, out_specs=[hbm] * no,
        out_shape=[jax.ShapeDtypeStruct(s, d) for s, d in outs],
        scratch_shapes=[pltpu.SemaphoreType.DMA((nt,)), pltpu.SemaphoreType.DMA((nt,)), stage_sems] + stages,
        compiler_params=pltpu.CompilerParams(has_side_effects=True, vmem_limit_bytes=VMEM_LIMIT))(*ins)


CHIP_MASKS = [(0, 1, 0), (1, 0, 0), (1, 1, 0)]
SIBLING = (0, 0, 1)


def _half(shape2d, axis, which):
    n = shape2d[axis] // 2
    cut = pl.ds(pl.multiple_of(which * n, n), n)
    return (cut, slice(None)) if axis == 0 else (slice(None), cut)


class _Riding:
    def __init__(self, transfers, ins, outs, own=()):
        self.transfers, self.ins, self.outs, self.own = transfers, list(ins), list(outs), list(own)
        hbm = pl.BlockSpec(memory_space=pltpu.HBM)
        self.in_specs, self.out_specs = [hbm] * len(self.ins), [hbm] * len(self.outs)
        self.out_shape = [jax.ShapeDtypeStruct(s, d) for s, d in self.outs]
        stages, stage_sems = _own_scratch(self.own, self.ins)
        self.scratch = [pltpu.SemaphoreType.DMA((max(len(transfers), 1),))] * 2 + [stage_sems] + stages

    def alone(self, name):
        return _exchange(name, self.ins, self.outs, self.transfers, self.own)

    def hooks(self, I, O, ssem, rsem, lsem, *stage, first, middle, last):
        tr, own = self.transfers, self.own

        @pl.when(first)
        def _():
            _own_copies(own, I, O, stage, lsem, 0)
            _start_transfers(tr, I, O, ssem, rsem, False)

        if own or any(t.get("after") is not None for t in tr):
            @pl.when(middle)
            def _():
                _own_copies(own, I, O, stage, lsem, 1)
                _start_transfers(tr, I, O, ssem, rsem, True)

        def at_end():
            @pl.when(last)
            def _():
                _finish_transfers(tr, I, O, ssem, rsem)
                _own_copies(own, I, O, stage, lsem, 2)

        return at_end


def _stretch(n, pos):
    return (pl.ds(pos * n if isinstance(pos, int) else pl.multiple_of(pos * n, n), n),)


def _gather_plan(shards, axes):
    def half(a, who):
        if shards[a].ndim == 1:
            return _stretch(shards[a].shape[0] // 2, who[2])
        return _half(shards[a].shape, axes[a], who[2])

    def landed(a, chip, who):
        if shards[a].ndim == 1:
            return _stretch(shards[a].shape[0] // 2, 2 * chip + who[2])
        return (chip,) + half(a, who)

    over_ici, onward = [], []
    for a in range(len(shards)):
        for mask in CHIP_MASKS:
            over_ici.append(dict(
                mask=mask,
                src=lambda I, O, me, a=a: I[a].at[half(a, me)],
                dst=lambda I, O, who, a=a: O[a].at[landed(a, _chip(who), who)]))
            onward.append(dict(
                mask=SIBLING, after=len(over_ici) - 1,
                src=lambda I, O, me, a=a, mask=mask: O[a].at[landed(a, _chip(_flip(me, mask)), me)],
                dst=lambda I, O, who, a=a, mask=mask: O[a].at[landed(a, _chip(_flip(who, mask)), who)]))
    outs = [((NCHIP * s.shape[0],) if s.ndim == 1 else (NCHIP,) + s.shape, s.dtype) for s in shards]

    def whole(a, chip):
        return _stretch(shards[a].shape[0], chip) if shards[a].ndim == 1 else (chip,)

    own = [(lambda I, O, me, a=a: I[a], lambda I, O, me, a=a: O[a].at[whole(a, _chip(me))])
           for a in range(len(shards))]
    return over_ici + onward, outs, own


def _gather_shards(shards, axes):
    transfers, outs, own = _gather_plan(shards, axes)
    return _exchange("gather_weights", shards, outs, transfers, own)


def _to_sibling(arrs, name):
    transfers = [dict(mask=SIBLING, src=lambda I, O, me, a=a: I[a], dst=lambda I, O, who, a=a: O[a])
                 for a in range(len(arrs))]
    return _exchange(name, arrs, [(t.shape, t.dtype) for t in arrs], transfers)


def _halves_plan(blocks, axes):
    def cut(a, which):
        return (slice(None),) + _half(blocks[a].shape[1:], axes[a], which)

    transfers, outs = [], []
    for a, (b, ax) in enumerate(zip(blocks, axes)):
        if b.ndim == 1:
            h = b.shape[0] // NCHIP // 2
            for k in range(NCHIP):
                transfers.append(dict(mask=SIBLING,
                                      src=lambda I, O, me, a=a, k=k, h=h: I[a].at[_stretch(h, 2 * k + 1 - me[2])],
                                      dst=lambda I, O, who, a=a, k=k, h=h: O[a].at[_stretch(h, k)]))
            outs.append(((NCHIP * h,), b.dtype))
        else:
            transfers.append(dict(mask=SIBLING, src=lambda I, O, me, a=a: I[a].at[cut(a, 1 - me[2])],
                                  dst=lambda I, O, who, a=a: O[a]))
            shape = list(b.shape)
            shape[ax + 1] //= 2
            outs.append((tuple(shape), b.dtype))
    return transfers, outs


def _scatter_plan(tb):
    def slot(a, k):
        return (k,) if tb[a].ndim == 3 else _stretch(tb[a].shape[0] // NCHIP, k)

    transfers = []
    for a in range(len(tb)):
        for n, mask in enumerate(CHIP_MASKS):
            transfers.append(dict(
                mask=mask,
                src=lambda I, O, me, a=a, mask=mask: I[a].at[slot(a, _chip(_flip(me, mask)))],
                dst=lambda I, O, who, a=a, n=n: O[a].at[slot(a, n)]))
    outs = [((3,) + t.shape[1:] if t.ndim == 3 else (3 * (t.shape[0] // NCHIP),), t.dtype) for t in tb]
    return transfers, outs


def _gather_small(vec):
    def slot(who):
        return 4 * who[0] + 2 * who[1] + who[2]

    masks = [(m >> 2 & 1, m >> 1 & 1, m & 1) for m in range(1, 8)]
    transfers = [dict(mask=mask, src=lambda I, O, me: I[0], dst=lambda I, O, who: O[0].at[slot(who)])
                 for mask in masks]
    own = [(lambda I, O, me: I[0], lambda I, O, me: O[0].at[slot(me)])]
    return _exchange("gather_small", [vec], [((8,) + vec.shape, vec.dtype)], transfers, own)[0]


def _rope_tables(positions):
    half = ROT // 2
    inv_freq = jnp.power(jnp.float32(THETA), -jnp.arange(0, ROT, 2, dtype=F32) / ROT)
    ang = positions.astype(F32)[:, None] * inv_freq[None, :]
    cos, sin = jnp.cos(ang), jnp.sin(ang)
    one, zero, z8 = jnp.ones((S, HD - ROT), F32), jnp.zeros((S, HD - ROT), F32), jnp.zeros((S, half), F32)
    c = jnp.concatenate([cos, cos, one], axis=1)
    a = jnp.concatenate([-sin, z8, zero], axis=1)
    b = jnp.concatenate([z8, sin, zero], axis=1)
    return tuple(jnp.tile(t, (1, 2)) for t in (c, a, b))


def _tile_heads(g, w):
    return jnp.tile(g.reshape(1, HD), (1, w // HD))


def _fold_heads(dg):
    return dg.reshape(-1, HD).sum(axis=0)


def _pad_lanes(a):
    return jnp.pad(a, ((0, 0), (0, LANES - a.shape[1])))


def _local_step(x, target, positions, wt, fetch, late_weights, begin_reduce):
    rope = _rope_tables(positions)
    w1t = wt["w_in_a_t"]
    f_row = 3 * D // LANES
    wg_t = w1t[3 * D + NH:]
    in_b_block = lambda c: pl.BlockSpec((None, TN_, TN_), lambda j, i: (c, j, 0))
    b_pad = _pad_lanes(wt["b_forget"].reshape(1, NH))
    qg_a, kg_a = _tile_heads(wt["qnorm_a_g"], D), _tile_heads(wt["knorm_a_g"], D)
    qg_b, kg_b = _tile_heads(wt["qnorm_b_g"], D), _tile_heads(wt["knorm_b_g"], KVW)
    norm_a, kv_g, norm_b = wt["norm_a_g"].reshape(1, D), wt["kv_norm_g"].reshape(1, D), wt["norm_b_g"].reshape(1, D)
    sinks_t = jnp.repeat(wt["sinks"].reshape(1, NH), HD, axis=1)

    (u_a,) = _rmsnorm_fwd(x, [norm_a], "norm_a")
    qkv = _mm("proj_a", S, 3 * D, [(u_a, _a_rows(D), w1t, _b_rows(D), NT)])
    fpad = _mm("proj_f", S, LANES, [(u_a, _a_rows(D), w1t, _b_rows(D, row0=f_row, tn=LANES), NT)], tn=LANES)
    gate_a = _mm("proj_gate_a", S, D, [(u_a, _a_rows(D), wg_t, _b_rows(D), NT)])
    q_a, k_a, v_a = _a_post(qkv, qg_a, kg_a)
    ct = _forget_cumsum(fpad, b_pad)
    ct2 = ct[:NH].reshape(NH // 2, 2, S)
    o_a, lse_a, y_a, fetched = _fox_fwd(q_a, k_a, v_a, ct2, gate_a, fetch)
    wt = {**wt, **late_weights(fetched)}
    w_in_b = wt["w_in_b"]
    h1 = _mm("out_a", S, D, [(y_a, _a_rows(D), wt["w_out_a"], _b_cols(D), None)], add=x)
    u_kv, u_b = _rmsnorm_fwd(h1, [kv_g, norm_b], "norm_b")
    kv = _mm("proj_kv", S, 2 * KVW, [(u_kv, _a_rows(D), wt["w_kv"], _b_cols(D), None)])
    pb = _mm("proj_b", S, 2 * D,
             [(u_b, _a_rows(D), w_in_b, pl.BlockSpec((None, D, TN_), lambda j, i: (j, 0, 0)), None)])
    q_b, kdup, vdup = _b_post(pb, kv, qg_b, kg_b, rope)
    gate_b_col = D // LANES
    o_b, lse_b, y_b = _swa_fwd(q_b, kdup, vdup, sinks_t, pb, gate_b_col)
    out = _mm("out_b", S, D, [(y_b, _a_rows(D), wt["w_out_b"], _b_cols(D), None)], add=h1)
    d_out, d_out_b, sq = _loss_head(out, target)

    g = {}
    g["w_out_b"] = _mm("dw_out_b", D, D, [(y_b, _a_cols(S), d_out_b, _b_cols(S), TN)])
    d_y_b = _mm("dy_b", S, D, [(d_out_b, _a_rows(D), wt["w_out_b"], _b_rows(D), NT)])
    dq_b, dkdup, dvdup, dsk, d_gate_b = _swa_bwd(q_b, kdup, vdup, sinks_t, o_b, lse_b, d_y_b, pb, gate_b_col)
    g["sinks"] = dsk[0, ::HD]
    d_qb_raw, dg = _headnorm_bwd(pb, 0, qg_b, dq_b, rope, "qnorm_b_bwd")
    g["qnorm_b_g"] = _fold_heads(dg)
    d_pb = [d_qb_raw, d_qb_raw, d_gate_b, d_gate_b]
    g["w_in_b"] = jnp.concatenate([
        _mm("dw_in_b_q", D, D, [(u_b, _a_cols(S), d_qb_raw, _b_cols(S), TN)], stacked=True),
        _mm("dw_in_b_gate", D, D, [(u_b, _a_cols(S), d_gate_b, _b_cols(S), TN)], stacked=True)], axis=0)
    d_u_b = _mm("du_b", S, D, [(d_pb[c], _a_rows(TN_, col=c % 2), w_in_b, in_b_block(c), NT) for c in range(NCHIP)])
    d_kv, dg = _kv_bwd(dkdup, dvdup, kv, kg_b, rope)
    g["knorm_b_g"] = _fold_heads(dg)
    g["w_kv"] = _mm("dw_kv", D, 2 * KVW, [(u_kv, _a_cols(S), d_kv, _b_cols(S), TN)])
    d_u_kv = _mm("du_kv", S, D, [(d_kv, _a_rows(2 * KVW), wt["w_kv"], _b_rows(2 * KVW), NT)])
    d_h1, d_h1_b, g["kv_norm_g"], g["norm_b_g"] = _rmsnorm_bwd(h1, [kv_g, norm_b], [d_u_kv, d_u_b], d_out, "norm_b_bwd")
    g["w_out_a"] = _mm("dw_out_a", D, D, [(y_a, _a_cols(S), d_h1_b, _b_cols(S), TN)])
    late = {n: g[n] for n in LATE}
    d_y_a, halves = _mm("dy_a", S, D, [(d_h1_b, _a_rows(D), wt["w_out_a"], _b_rows(D), NT)],
                        riding=begin_reduce(late))
    riding, so_far = begin_reduce(late, halves)
    dq_a, dk_a, dv_a, dct, d_gate_a, arrived = _fox_bwd(q_a, k_a, v_a, ct2, o_a, lse_a, d_y_a, gate_a, riding)
    dct_pad = jnp.pad(dct.reshape(NH, S), ((0, LANES - NH), (0, 0)))
    d_f, db = _forget_bwd(dct_pad, fpad, b_pad)
    g["b_forget"] = db[0, :NH]
    d_q_raw, dg = _headnorm_bwd(qkv, 0, qg_a, dq_a, None, "qnorm_a_bwd")
    g["qnorm_a_g"] = _fold_heads(dg)
    d_k_raw, dg = _headnorm_bwd(qkv, 1, kg_a, dk_a, None, "knorm_a_bwd")
    g["knorm_a_g"] = _fold_heads(dg)
    rows, gw = 4 * D + NH, None
    for n, t, row0 in (("q", d_q_raw, 0), ("k", d_k_raw, D), ("v", dv_a, 2 * D)):
        gw = _mm("dw_in_a_" + n, D, D, [(t, _a_cols(S), u_a, _b_cols(S), TN)], rows_of=(gw, rows, row0))
    gw = _mm("dw_in_a_f", LANES, D, [(d_f, _a_cols(S, tm=LANES), u_a, _b_cols(S), TN)], tm=LANES,
             rows_of=(gw, rows, 3 * D))
    g["w_in_a"] = _mm("dw_in_a_gate", D, D, [(d_gate_a, _a_cols(S), u_a, _b_cols(S), TN)],
                      rows_of=(gw, rows, 3 * D + NH))
    first = {"w_in_a": g["w_in_a"]}
    riding, so_far_first = begin_reduce(first, begin_reduce(first).alone("sibling_halves_w_in_a"))
    d_u_a, arrived_first = _mm("du_a", S, D, [
        (d_q_raw, _a_rows(D), w1t, _b_cols(D, row=0), None), (d_k_raw, _a_rows(D), w1t, _b_cols(D, row=1), None),
        (dv_a, _a_rows(D), w1t, _b_cols(D, row=2), None), (d_gate_a, _a_rows(D), wg_t, _b_cols(D), None),
        (d_f, _a_rows(LANES), w1t, _b_cols(LANES, row=f_row), None)], riding=riding)
    d_x, _, g["norm_a_g"] = _rmsnorm_bwd(x, [norm_a], [d_u_a], d_h1, "norm_a_bwd")
    return sq, d_x, g, (list(so_far_first) + list(so_far), list(arrived_first) + list(arrived))


BIG = ["w_in_a", "w_out_a", "w_kv", "w_in_b", "w_out_b"]
LATE = BIG[1:]
SPLIT = {"w_in_a": None, "w_out_a": 0, "w_kv": 0, "w_in_b": 0, "w_out_b": 0}
SMALL = ["norm_a_g", "b_forget", "qnorm_a_g", "knorm_a_g", "kv_norm_g", "knorm_b_g", "norm_b_g", "qnorm_b_g", "sinks"]
NAMES = ["norm_a_g", "w_in_a", "b_forget", "qnorm_a_g", "knorm_a_g", "w_out_a", "kv_norm_g", "w_kv", "knorm_b_g",
         "norm_b_g", "w_in_b", "qnorm_b_g", "sinks", "w_out_b"]


def _pack(vals):
    flat = []
    for v in vals:
        v = v.reshape(-1)
        flat.append(jnp.pad(v, (0, -v.shape[0] % LANES)))
    flat = jnp.concatenate(flat)
    flat = jnp.pad(flat, (0, -flat.shape[0] % (8 * LANES)))
    return flat.reshape(-1, LANES)


def _unpack(packed, shapes):
    flat, out, off = packed.reshape(-1), [], 0
    for s in shapes:
        n = int(np.prod(s))
        out.append(flat[off:off + n].reshape(s))
        off += n + (-n % LANES)
    return out


def kernel(x, positions, norm_a_g, w_in_a, b_forget, qnorm_a_g, knorm_a_g, w_out_a, kv_norm_g, w_kv, knorm_b_g, norm_b_g, w_in_b, qnorm_b_g, sinks, w_out_b, loss_target, m_norm_a_g, m_w_in_a, m_b_forget, m_qnorm_a_g, m_knorm_a_g, m_w_out_a, m_kv_norm_g, m_w_kv, m_knorm_b_g, m_norm_b_g, m_w_in_b, m_qnorm_b_g, m_sinks, m_w_out_b, v_norm_a_g, v_w_in_a, v_b_forget, v_qnorm_a_g, v_knorm_a_g, v_w_out_a, v_kv_norm_g, v_w_kv, v_knorm_b_g, v_norm_b_g, v_w_in_b, v_qnorm_b_g, v_sinks, v_w_out_b):
    w = dict(norm_a_g=norm_a_g, w_in_a=w_in_a, b_forget=b_forget, qnorm_a_g=qnorm_a_g, knorm_a_g=knorm_a_g,
             w_out_a=w_out_a, kv_norm_g=kv_norm_g, w_kv=w_kv, knorm_b_g=knorm_b_g, norm_b_g=norm_b_g,
             w_in_b=w_in_b, qnorm_b_g=qnorm_b_g, sinks=sinks, w_out_b=w_out_b)
    m = dict(norm_a_g=m_norm_a_g, w_in_a=m_w_in_a, b_forget=m_b_forget, qnorm_a_g=m_qnorm_a_g, knorm_a_g=m_knorm_a_g,
             w_out_a=m_w_out_a, kv_norm_g=m_kv_norm_g, w_kv=m_w_kv, knorm_b_g=m_knorm_b_g, norm_b_g=m_norm_b_g,
             w_in_b=m_w_in_b, qnorm_b_g=m_qnorm_b_g, sinks=m_sinks, w_out_b=m_w_out_b)
    v = dict(norm_a_g=v_norm_a_g, w_in_a=v_w_in_a, b_forget=v_b_forget, qnorm_a_g=v_qnorm_a_g, knorm_a_g=v_knorm_a_g,
             w_out_a=v_w_out_a, kv_norm_g=v_kv_norm_g, w_kv=v_w_kv, knorm_b_g=v_knorm_b_g, norm_b_g=v_norm_b_g,
             w_in_b=v_w_in_b, qnorm_b_g=v_qnorm_b_g, sinks=v_sinks, w_out_b=v_w_out_b)
    my_chip = 2 * lax.axis_index("x") + lax.axis_index("y")

    def shard2d(t, n):
        if n == "w_in_a":
            return jnp.transpose(t, (2, 0, 1)).reshape(-1)
        return t.reshape(t.shape[-2:])

    def unflat(t, n):
        return jnp.transpose(t.reshape(-1, 1, D), (1, 2, 0)) if n == "w_in_a" else t.reshape(w[n].shape)

    w2d = {n: shard2d(w[n], n) for n in BIG}

    norm_a_rows = jnp.broadcast_to(norm_a_g.reshape(1, D // NCHIP), (2 * SUBLANES, D // NCHIP))
    w1t, norm_rows = _gather_shards([w2d["w_in_a"].astype(BF16), norm_a_rows], [SPLIT["w_in_a"], 0])
    wt = {"w_in_a_t": w1t.reshape(-1, D), "norm_a_g": norm_rows[:, 0, :].reshape(1, D)}
    for n in SMALL[1:]:
        wt[n] = w[n]
    late_shards = [w2d[n].astype(BF16) for n in LATE]
    late_axes = [SPLIT[n] for n in LATE]
    transfers, outs, own = _gather_plan(late_shards, late_axes)
    fetch = _Riding(transfers, late_shards, outs, own)

    def late_weights(fetched):
        return {n: t if n == "w_in_b" else t.reshape(-1, t.shape[2]) for n, t in zip(LATE, fetched)}

    def as_blocks(t):
        if t.ndim == 3:
            return t
        return t.reshape(-1) if t.shape[0] % (SUBLANES * NCHIP) else t.reshape(NCHIP, -1, t.shape[1])

    def begin_reduce(grads, halves=None):
        names = list(grads)
        axes = [SPLIT[n] for n in names]
        blocks = [as_blocks(grads[n]) for n in names]
        if halves is None:
            transfers, outs = _halves_plan(blocks, axes)
            return _Riding(transfers, blocks, outs)
        sums = [_chip_sum(blk, part, ax, "chip_sum_" + n) for n, ax, blk, part in zip(names, axes, blocks, halves)]
        bf16 = [s[1] for s in sums]
        transfers, outs = _scatter_plan(bf16)
        return _Riding(transfers, bf16, outs), [s[0] for s in sums]

    sq, d_x, g, (chip_f32, arrived) = _local_step(x[0], loss_target[0], positions, wt, fetch, late_weights,
                                                  begin_reduce)

    small_shapes = [(D,), (NH,), (HD,), (HD,), (D,), (HD,), (D,), (HD,), (NH,), (D,)]
    packed = _pack([g[n] for n in SMALL] + [sq])
    total = _sum_stack(_gather_small(packed), "sum_small")
    small_g = dict(zip(SMALL, _unpack(total, small_shapes)[:-1]))
    loss = 0.5 * jnp.sum(_unpack(total, small_shapes)[-1]) / D
    small_g["norm_a_g"] = lax.dynamic_slice(small_g["norm_a_g"], (my_chip * (D // NCHIP),), (D // NCHIP,))

    axes = [SPLIT[n] for n in BIG]
    halves = []
    for n, ax, t32, parts in zip(BIG, axes, chip_f32, arrived):
        if t32.ndim == 1:
            own = lax.dynamic_slice_in_dim(t32, my_chip * (t32.shape[0] // NCHIP), t32.shape[0] // NCHIP)
        else:
            own = lax.dynamic_index_in_dim(t32, my_chip, axis=0, keepdims=False)
        halves.append(_mesh_sum(own, parts, ax, "mesh_sum_" + n))
    sibling_done = _to_sibling(halves, "finished_halves")

    res = {}
    for n, ax, mine_half, their_half in zip(BIG, axes, halves, sibling_done):
        out4 = _adamw_halves(w2d[n], mine_half, their_half, shard2d(m[n], n), shard2d(v[n], n), ax, "adamw_" + n)
        res[n] = tuple(unflat(t, n) for t in out4)
    row = lambda t: t.reshape(1, -1)
    small_out = _adamw_small(*[[row(d[n]) for n in SMALL] for d in (w, small_g, m, v)])
    for i, n in enumerate(SMALL):
        res[n] = tuple(t.reshape(w[n].shape) for t in (small_g[n],) + tuple(out[i] for out in small_out))

    outs = [loss, d_x[None]]
    for k in range(4):
        outs += [res[n][k] for n in NAMES]
    return tuple(outs)
```

```python
import numpy as np
import jax
import jax.numpy as jnp
from jax import lax
from jax.experimental import pallas as pl
from jax.experimental.pallas import tpu as pltpu

F32, BF16 = jnp.float32, jnp.bfloat16
S, D, HD, NH, NKV = 2048, 1024, 64, 16, 4
KVW = NKV * HD
WINDOW = 128
ROT = HD // 4
THETA = 500000.0
EPS = 1e-6
SCALE = HD ** -0.5
LANES = 128
SUBLANES = 8
NEG = -1e30
VMEM_LIMIT = 48 * 2 ** 20
ROWS = 512
ATT = 512
SWQ = 8
NCHIP = 4
ADAM_LR, ADAM_B1, ADAM_B2, ADAM_EPS, ADAM_WD, ADAM_STEP = 0.001, 0.9, 0.999, 1e-08, 0.01, 10
NT = (((1,), (1,)), ((), ()))
TN = (((0,), (0,)), ((), ()))
MESH = pl.DeviceIdType.MESH


def _params(n):
    return pltpu.CompilerParams(dimension_semantics=("arbitrary",) * n, vmem_limit_bytes=VMEM_LIMIT)


def _dot(a, b, dims=None):
    if dims is None:
        return jnp.dot(a, b, preferred_element_type=F32)
    return lax.dot_general(a, b, dims, preferred_element_type=F32)


def _dot_split(a, b, n):
    out, rest = None, a
    for _ in range(n):
        hi = rest.astype(BF16)
        term = _dot(hi, b)
        out = term if out is None else out + term
        rest = rest - hi.astype(F32)
    return out


def _seg_mat(w):
    e = (np.arange(w)[:, None] // HD == np.arange(LANES)[None, :]).astype(np.float32)
    return jnp.asarray(e, BF16)


def _spread(r, w):
    head = lax.broadcasted_iota(jnp.int32, (2 * LANES, w), 1) >> (HD.bit_length() - 1)
    row = lax.broadcasted_iota(jnp.int32, (2 * LANES, w), 0)
    et2 = jnp.where(head == (row & (LANES - 1)), 1.0, 0.0).astype(BF16)
    hi = r.astype(BF16)
    lo = (r - hi.astype(F32)).astype(BF16)
    return _dot(jnp.concatenate([hi, lo], axis=1), et2)


def _head_rstd(x, e):
    ss = _dot_split(x * x, e, 2)
    return _spread(lax.rsqrt(ss * (1.0 / HD) + EPS), x.shape[1])


def _rope(x, c, a, b):
    w = x.shape[1]
    return x * c + pltpu.roll(x, w - ROT // 2, 1) * a + pltpu.roll(x, ROT // 2, 1) * b


def _rope_t(dy, c, a, b):
    w = dy.shape[1]
    return dy * c + pltpu.roll(dy * b, w - ROT // 2, 1) + pltpu.roll(dy * a, ROT // 2, 1)


def _sigmoid(x):
    return 1.0 / (1.0 + jnp.exp(-x))


def _row_spec(shape, ts):
    nd = len(shape)
    if shape[0] == S:
        return pl.BlockSpec((ts,) + tuple(shape[1:]), lambda i: (i,) + (0,) * (nd - 1))
    return pl.BlockSpec(tuple(shape), lambda i: (0,) * nd)


def _rows_call(body, name, ins, outs, ts=ROWS):
    return pl.pallas_call(
        body, name=name, grid=(S // ts,),
        in_specs=[_row_spec(a.shape, ts) for a in ins],
        out_specs=[_row_spec(s, ts) for s, _ in outs],
        out_shape=[jax.ShapeDtypeStruct(s, d) for s, d in outs],
        compiler_params=_params(1))(*ins)


def _col_spec(ts, w, col):
    return pl.BlockSpec((ts, w), lambda i: (i, col))


TM = TN_ = 512
TM_TOKENS = 1024


def _mm(name, m, n, terms, out_dtype=F32, add=None, tm=None, tn=TN_, stacked=False, riding=None, rows_of=None):
    nterm = len(terms)
    if tm is None:
        tm = TM_TOKENS if m == S else TM
    nj, ni_ = n // tn, m // tm
    n_in = 2 * nterm + (add is not None) + (rows_of is not None and rows_of[0] is not None)
    r_in, r_out = (len(riding.ins), len(riding.outs)) if riding is not None else (0, 0)

    def body(*refs):
        if riding is not None:
            j, i = pl.program_id(0), pl.program_id(1)
            at_end = riding.hooks(refs[n_in:n_in + r_in], refs[n_in + r_in + 1:n_in + r_in + 1 + r_out],
                                  *refs[n_in + r_in + 1 + r_out:], first=(j == 0) & (i == 0),
                                  middle=(j == nj // 2) & (i == 0), last=(j == nj - 1) & (i == ni_ - 1))
        acc = None
        for t in range(nterm):
            part = _dot(refs[2 * t][...], refs[2 * t + 1][...], terms[t][4])
            acc = part if acc is None else acc + part
        if add is not None:
            acc = acc + refs[2 * nterm][...]
        refs[n_in + r_in][...] = acc.astype(out_dtype)
        if riding is not None:
            at_end()

    tile = pl.BlockSpec((tm, tn), lambda j, i: (i, j))
    ins, specs = [], []
    for a, a_spec, b, b_spec, _ in terms:
        ins += [a, b]
        specs += [a_spec, b_spec]
    if add is not None:
        ins.append(add)
        specs.append(tile)
    out_spec = pl.BlockSpec((None, tm, tn), lambda j, i: (j, i, 0)) if stacked else tile
    out_shape = jax.ShapeDtypeStruct((nj, m, tn) if stacked else (m, n), out_dtype)
    if rows_of is not None:
        taller, rows, row0 = rows_of
        out_spec = pl.BlockSpec((pl.Element(tm), pl.Element(tn)), lambda j, i: (
            pl.multiple_of(row0 + i * tm, SUBLANES), pl.multiple_of(j * tn, LANES)))
        out_shape = jax.ShapeDtypeStruct((rows, n), out_dtype)
        alias = {}
        if taller is not None:
            ins.append(taller)
            specs.append(pl.BlockSpec(memory_space=pltpu.HBM))
            alias = {len(ins) - 1: 0}
        return pl.pallas_call(body, name=name, grid=(nj, ni_), in_specs=specs, out_specs=out_spec,
                              out_shape=out_shape, input_output_aliases=alias, compiler_params=_params(2))(*ins)
    if riding is None:
        return pl.pallas_call(body, name=name, grid=(nj, ni_), in_specs=specs, out_specs=out_spec,
                              out_shape=out_shape, compiler_params=_params(2))(*ins)
    res = pl.pallas_call(
        body, name=name, grid=(nj, ni_), in_specs=specs + riding.in_specs,
        out_specs=[out_spec] + riding.out_specs, out_shape=[out_shape] + riding.out_shape,
        scratch_shapes=riding.scratch, compiler_params=_params(2))(*ins, *riding.ins)
    return res[0], res[1:]


def _a_rows(k, col=0, tm=TM_TOKENS):
    return pl.BlockSpec((tm, k), lambda j, i: (i, col))


def _a_cols(k, tm=TM):
    return pl.BlockSpec((k, tm), lambda j, i: (0, i))


def _b_cols(k, row=0, col0=0, tn=TN_):
    return pl.BlockSpec((k, tn), lambda j, i: (row, col0 + j))


def _b_rows(k, row0=0, tn=TN_):
    return pl.BlockSpec((tn, k), lambda j, i: (row0 + j, 0))


def _rmsnorm_fwd(x, gains, name):
    def body(*refs):
        xv = refs[0][...]
        r = lax.rsqrt(jnp.mean(xv * xv, axis=-1, keepdims=True) + EPS)
        xh = xv * r
        for n in range(len(gains)):
            refs[1 + len(gains) + n][...] = (xh * refs[1 + n][...]).astype(BF16)

    return _rows_call(body, name, [x] + list(gains), [((S, D), BF16)] * len(gains))


def _rmsnorm_bwd(x, gains, dus, dres, name):
    n = len(gains)

    def body(*refs):
        x_ref, g_refs, du_refs, dres_ref = refs[0], refs[1:1 + n], refs[1 + n:1 + 2 * n], refs[1 + 2 * n]
        dx_ref, dxb_ref, dg_refs = refs[2 + 2 * n], refs[3 + 2 * n], refs[4 + 2 * n:]
        xv = x_ref[...]
        r = lax.rsqrt(jnp.mean(xv * xv, axis=-1, keepdims=True) + EPS)
        xh = xv * r
        gy = None
        for m in range(n):
            du = du_refs[m][...]
            part = jnp.sum(du * xh, axis=0, keepdims=True)

            @pl.when(pl.program_id(0) == 0)
            def _(m=m, part=part):
                dg_refs[m][...] = part

            @pl.when(pl.program_id(0) != 0)
            def _(m=m, part=part):
                dg_refs[m][...] += part

            t = du * g_refs[m][...]
            gy = t if gy is None else gy + t
        dx = dres_ref[...] + r * (gy - xh * jnp.mean(gy * xh, axis=-1, keepdims=True))
        dx_ref[...] = dx
        dxb_ref[...] = dx.astype(BF16)

    outs = [((S, D), F32), ((S, D), BF16)] + [((1, D), F32)] * n
    return _rows_call(body, name, [x] + list(gains) + list(dus) + [dres], outs)


def _a_post(qkvg, qg, kg):
    e = _seg_mat(D)

    def body(q_ref, k_ref, v_ref, qg_ref, kg_ref, e_ref, qo, ko, vo):
        ev = e_ref[...]
        qv, kv = q_ref[...], k_ref[...]
        qo[...] = (qv * _head_rstd(qv, ev) * qg_ref[...] * SCALE).astype(BF16)
        ko[...] = (kv * _head_rstd(kv, ev) * kg_ref[...]).astype(BF16)
        vo[...] = v_ref[...].astype(BF16)

    whole = lambda a: pl.BlockSpec(a.shape, lambda i: (0, 0))
    return pl.pallas_call(
        body, name="a_post", grid=(S // ROWS,),
        in_specs=[_col_spec(ROWS, D, 0), _col_spec(ROWS, D, 1), _col_spec(ROWS, D, 2),
                  whole(qg), whole(kg), whole(e)],
        out_specs=[_col_spec(ROWS, D, 0)] * 3,
        out_shape=[jax.ShapeDtypeStruct((S, D), BF16)] * 3,
        compiler_params=_params(1))(qkvg, qkvg, qkvg, qg, kg, e)


def _tri(upper):
    r, c = np.arange(ROWS)[:, None], np.arange(ROWS)[None, :]
    return jnp.asarray((r <= c) if upper else (r >= c), BF16)


def _forget_cumsum(fpad, bpad):
    def body(f_ref, b_ref, u_ref, c_ref, carry):
        @pl.when(pl.program_id(0) == 0)
        def _():
            carry[...] = jnp.zeros_like(carry)

        lf = jax.nn.log_sigmoid(f_ref[...] + b_ref[...])
        blk = _dot_split(lf.T, u_ref[...], 3) + carry[:, 0:1]
        c_ref[...] = blk
        carry[...] = jnp.broadcast_to(blk[:, ROWS - 1:ROWS], carry.shape)

    return pl.pallas_call(
        body, name="forget_cumsum", grid=(S // ROWS,),
        in_specs=[pl.BlockSpec((ROWS, LANES), lambda i: (i, 0)), pl.BlockSpec((1, LANES), lambda i: (0, 0)),
                  pl.BlockSpec((ROWS, ROWS), lambda i: (0, 0))],
        out_specs=pl.BlockSpec((LANES, ROWS), lambda i: (0, i)),
        out_shape=jax.ShapeDtypeStruct((LANES, S), F32),
        scratch_shapes=[pltpu.VMEM((LANES, LANES), F32)],
        compiler_params=_params(1))(fpad, bpad, _tri(True))


def _forget_bwd(dct, fpad, bpad):
    nb = S // ROWS

    def body(dc_ref, f_ref, b_ref, l_ref, df_ref, db_ref, carry):
        @pl.when(pl.program_id(0) == 0)
        def _():
            carry[...] = jnp.zeros_like(carry)
            db_ref[...] = jnp.zeros_like(db_ref)

        blk = _dot_split(dc_ref[...], l_ref[...], 3) + carry[:, 0:1]
        carry[...] = jnp.broadcast_to(blk[:, 0:1], carry.shape)
        df = blk.T * _sigmoid(-(f_ref[...] + b_ref[...]))
        df_ref[...] = df.astype(BF16)
        db_ref[...] += jnp.sum(df, axis=0, keepdims=True)

    return pl.pallas_call(
        body, name="forget_bwd", grid=(nb,),
        in_specs=[pl.BlockSpec((LANES, ROWS), lambda i: (0, nb - 1 - i)),
                  pl.BlockSpec((ROWS, LANES), lambda i: (nb - 1 - i, 0)),
                  pl.BlockSpec((1, LANES), lambda i: (0, 0)), pl.BlockSpec((ROWS, ROWS), lambda i: (0, 0))],
        out_specs=[pl.BlockSpec((ROWS, LANES), lambda i: (nb - 1 - i, 0)), pl.BlockSpec((1, LANES), lambda i: (0, 0))],
        out_shape=[jax.ShapeDtypeStruct((S, LANES), BF16), jax.ShapeDtypeStruct((1, LANES), F32)],
        scratch_shapes=[pltpu.VMEM((LANES, LANES), F32)],
        compiler_params=_params(1))(dct, fpad, bpad, _tri(False))


def _headnorm_bwd(x, col, gain, dy, rope, name):
    e = _seg_mat(D)
    tabs = list(rope) if rope is not None else []

    def body(*refs):
        x_ref, g_ref, dy_ref, e_ref = refs[:4]
        dx_ref, dg_ref = refs[-2:]
        xv, dyv, ev = x_ref[...], dy_ref[...], e_ref[...]
        if rope is not None:
            c, a, b = (jnp.tile(t[...], (1, D // LANES)) for t in refs[4:7])
            dyv = _rope_t(dyv, c, a, b)
        r = _head_rstd(xv, ev)
        xh = xv * r
        part = jnp.sum(dyv * xh, axis=0, keepdims=True)

        @pl.when(pl.program_id(0) == 0)
        def _():
            dg_ref[...] = part

        @pl.when(pl.program_id(0) != 0)
        def _():
            dg_ref[...] += part

        gy = dyv * g_ref[...]
        seg = _spread(_dot_split(gy * xh, ev, 2) * (1.0 / HD), D)
        dx_ref[...] = (r * (gy - xh * seg)).astype(BF16)

    whole = lambda a: pl.BlockSpec(a.shape, lambda i: (0, 0))
    return pl.pallas_call(
        body, name=name, grid=(S // ROWS,),
        in_specs=[_col_spec(ROWS, D, col), whole(gain), _col_spec(ROWS, D, 0), whole(e)]
                 + [pl.BlockSpec((ROWS, LANES), lambda i: (i, 0))] * len(tabs),
        out_specs=[_col_spec(ROWS, D, 0), whole(gain)],
        out_shape=[jax.ShapeDtypeStruct((S, D), BF16), jax.ShapeDtypeStruct((1, D), F32)],
        compiler_params=_params(1))(x, gain, dy, e, *tabs)


def _dup_mat():
    r, c = np.arange(KVW)[:, None], np.arange(2 * KVW)[None, :]
    return (r // HD == c // LANES) & (r % HD == c % HD)


def _fold_mat():
    r, c = np.arange(D)[:, None], np.arange(KVW)[None, :]
    return (r // (2 * LANES) == c // HD) & (r % HD == c % HD)


def _b_post(pb, kv, qg, kg, rope):
    e, ek = _seg_mat(D), _seg_mat(KVW)
    dup = jnp.asarray(_dup_mat(), BF16)

    def body(q_ref, k_ref, v_ref, qg_ref, kg_ref, e_ref, ek_ref, dup_ref, c_ref, a_ref, b_ref, qo, ko, vo):
        c1, a1, b1 = c_ref[...], a_ref[...], b_ref[...]
        qv = q_ref[...]
        qn = qv * _head_rstd(qv, e_ref[...]) * qg_ref[...]
        t = lambda z, n: jnp.tile(z, (1, n))
        qo[...] = (_rope(qn, t(c1, D // LANES), t(a1, D // LANES), t(b1, D // LANES)) * SCALE).astype(BF16)
        kvv = k_ref[...]
        kn = kvv * _head_rstd(kvv, ek_ref[...]) * kg_ref[...]
        kr = _rope(kn, t(c1, KVW // LANES), t(a1, KVW // LANES), t(b1, KVW // LANES)).astype(BF16)
        ko[...] = _dot(kr, dup_ref[...]).astype(BF16)
        vo[...] = _dot(v_ref[...].astype(BF16), dup_ref[...]).astype(BF16)

    whole = lambda a: pl.BlockSpec(a.shape, lambda i: (0, 0))
    tab = pl.BlockSpec((ROWS, LANES), lambda i: (i, 0))
    return pl.pallas_call(
        body, name="b_post", grid=(S // ROWS,),
        in_specs=[_col_spec(ROWS, D, 0), _col_spec(ROWS, KVW, 0), _col_spec(ROWS, KVW, 1),
                  whole(qg), whole(kg), whole(e), whole(ek), whole(dup), tab, tab, tab],
        out_specs=[_col_spec(ROWS, D, 0), _col_spec(ROWS, 2 * KVW, 0), _col_spec(ROWS, 2 * KVW, 0)],
        out_shape=[jax.ShapeDtypeStruct((S, D), BF16), jax.ShapeDtypeStruct((S, 2 * KVW), BF16),
                   jax.ShapeDtypeStruct((S, 2 * KVW), BF16)],
        compiler_params=_params(1))(pb, kv, kv, qg, kg, e, ek, dup, *rope)


def _kv_bwd(dkdup, dvdup, kv, kg, rope):
    ek = _seg_mat(KVW)
    fold = jnp.asarray(_fold_mat(), BF16)

    def body(dk_ref, dv_ref, k_ref, kg_ref, ek_ref, fold_ref, c_ref, a_ref, b_ref, dkv_ref, dg_ref):
        ev, fv = ek_ref[...], fold_ref[...]
        t = lambda z: jnp.tile(z[...], (1, KVW // LANES))
        dk = _rope_t(_dot_split(dk_ref[...], fv, 2), t(c_ref), t(a_ref), t(b_ref))
        dv = _dot_split(dv_ref[...], fv, 2)
        xv = k_ref[...]
        r = _head_rstd(xv, ev)
        xh = xv * r
        part = jnp.sum(dk * xh, axis=0, keepdims=True)

        @pl.when(pl.program_id(0) == 0)
        def _():
            dg_ref[...] = part

        @pl.when(pl.program_id(0) != 0)
        def _():
            dg_ref[...] += part

        gy = dk * kg_ref[...]
        seg = _spread(_dot_split(gy * xh, ev, 2) * (1.0 / HD), KVW)
        dkv_ref[:, 0:KVW] = (r * (gy - xh * seg)).astype(BF16)
        dkv_ref[:, KVW:2 * KVW] = dv.astype(BF16)

    whole = lambda a: pl.BlockSpec(a.shape, lambda i: (0, 0))
    tab = pl.BlockSpec((ROWS, LANES), lambda i: (i, 0))
    return pl.pallas_call(
        body, name="kv_bwd", grid=(S // ROWS,),
        in_specs=[_col_spec(ROWS, D, 0), _col_spec(ROWS, D, 0), _col_spec(ROWS, KVW, 0),
                  whole(kg), whole(ek), whole(fold), tab, tab, tab],
        out_specs=[_col_spec(ROWS, 2 * KVW, 0), whole(kg)],
        out_shape=[jax.ShapeDtypeStruct((S, 2 * KVW), BF16), jax.ShapeDtypeStruct((1, KVW), F32)],
        compiler_params=_params(1))(dkdup, dvdup, kv, kg, ek, fold, *rope)


def _loss_head(out, target):
    def body(o_ref, t_ref, d_ref, db_ref, l_ref):
        diff = o_ref[...] - t_ref[...]
        d = diff * (1.0 / D)
        d_ref[...] = d
        db_ref[...] = d.astype(BF16)

        @pl.when(pl.program_id(0) == 0)
        def _():
            l_ref[...] = jnp.zeros_like(l_ref)

        l_ref[...] += jnp.sum(diff * diff, axis=0, keepdims=True)

    return _rows_call(body, "loss_head", [out, target], [((S, D), F32), ((S, D), BF16), ((1, D), F32)])


def _lane():
    return lax.broadcasted_iota(jnp.int32, (1, LANES), 1)


def _head_mask(hh):
    return (_lane() < HD) if hh == 0 else (_lane() >= HD)


def _fox_fwd(q, k, v, ct, gate, riding):
    nq, npair = S // ATT, NH // 2
    ni, no = len(riding.ins), len(riding.outs)

    def body(q_ref, k_ref, v_ref, c_ref, gate_ref, *rest):
        o_ref, lse_ref, y_ref = rest[ni:ni + 3]
        pair, i = pl.program_id(0), pl.program_id(1)
        at_end = riding.hooks(rest[:ni], rest[ni + 3:ni + 3 + no], *rest[ni + 3 + no:],
                              first=(pair == 0) & (i == 0), middle=(pair == npair // 2) & (i == 0),
                              last=(pair == npair - 1) & (i == nq - 1))
        q2 = q_ref[...]
        qms = [jnp.where(_head_mask(hh), q2, jnp.zeros_like(q2)) for hh in (0, 1)]

        def probs(off, width, m, hh, diag):
            s = _dot(qms[hh], k_ref[pl.ds(off, width), :], NT) - c_ref[hh:hh + 1, pl.ds(off, width)]
            if diag:
                row = i * ATT + lax.broadcasted_iota(jnp.int32, (ATT, width), 0)
                col = off + lax.broadcasted_iota(jnp.int32, (ATT, width), 1)
                s = jnp.where(col <= row, s, NEG)
            m_new = jnp.maximum(m, jnp.max(s, axis=1, keepdims=True))
            p = jnp.exp(s - m_new)
            p_hi = p.astype(BF16)
            return m_new, jnp.exp(m - m_new), p_hi, (p - p_hi.astype(F32)).astype(BF16)

        def weighted(off, width, p_hi, p_lo, hh):
            vj = v_ref[pl.ds(off, width), :]
            v1 = jnp.where(_head_mask(hh), vj, jnp.ones_like(vj))
            return _dot(p_hi, v1) + _dot(p_lo, v1)

        def step(off, width, carry, diag):
            off = pl.multiple_of(off, ATT)
            out = []
            for hh in (0, 1):
                m, acc = carry[hh]
                m, alpha, p_hi, p_lo = probs(off, width, m, hh, diag)
                out.append((m, alpha * acc + weighted(off, width, p_hi, p_lo, hh)))
            return tuple(out)

        one = (jnp.full((ATT, 1), NEG, F32), jnp.zeros((ATT, LANES), F32))
        carry = lax.fori_loop(0, i // 2, lambda j, cr: step(j * (2 * ATT), 2 * ATT, cr, False), (one, one))
        carry = lax.cond(i % 2 == 1, lambda cr: step((i - 1) * ATT, 2 * ATT, cr, True),
                         lambda cr: step(i * ATT, ATT, cr, True), carry)
        res = []
        for hh in (0, 1):
            m, acc = carry[hh]
            l = jnp.max(jnp.where(_head_mask(1 - hh), acc, 0.0), axis=1, keepdims=True)
            res.append((acc / l, m + jnp.log(l)))
        first = _head_mask(0)
        o = jnp.where(first, res[0][0], res[1][0])
        o_ref[...] = o
        lse_ref[...] = jnp.where(first, res[0][1], res[1][1])
        g = gate_ref[...]
        y_ref[...] = (o * (g * _sigmoid(g))).astype(BF16)
        at_end()

    blk = pl.BlockSpec((ATT, LANES), lambda p, i: (i, p))
    full = pl.BlockSpec((S, LANES), lambda p, i: (0, p))
    res = pl.pallas_call(
        body, name="fox_fwd", grid=(npair, nq),
        in_specs=[blk, full, full, pl.BlockSpec((None, 2, S), lambda p, i: (p, 0, 0)), blk] + riding.in_specs,
        out_specs=[blk, blk, blk] + riding.out_specs,
        out_shape=[jax.ShapeDtypeStruct((S, D), F32)] * 2 + [jax.ShapeDtypeStruct((S, D), BF16)] + riding.out_shape,
        scratch_shapes=riding.scratch,
        compiler_params=_params(2))(q, k, v, ct, gate, *riding.ins)
    return res[0], res[1], res[2], res[3:]


def _gate_grads(dy, o, g):
    sg = _sigmoid(g)
    return dy * (g * sg), dy * o * (sg * (1.0 + g * (1.0 - sg)))


def _fox_bwd(q, k, v, ct, o, lse, dy, gate, riding):
    nq, npair = S // ATT, NH // 2
    ni, no = len(riding.ins), len(riding.outs)

    def body(q_ref, k_ref, v_ref, c_ref, o_ref, lse_ref, dy_ref, gate_ref, *rest):
        dq_ref, dk_ref, dvb_ref, dc_ref, dgate_ref = rest[ni:ni + 5]
        dv_ref = rest[ni + 5 + no]
        pair, i = pl.program_id(0), pl.program_id(1)
        at_end = riding.hooks(rest[:ni], rest[ni + 5:ni + 5 + no], *rest[ni + 6 + no:],
                              first=(pair == 0) & (i == 0), middle=(pair == npair // 2) & (i == 0),
                              last=(pair == npair - 1) & (i == nq - 1))

        @pl.when(i == 0)
        def _():
            dk_ref[...] = jnp.zeros_like(dk_ref)
            dv_ref[...] = jnp.zeros_like(dv_ref)
            dc_ref[...] = jnp.zeros_like(dc_ref)

        q2, lse2 = q_ref[...], lse_ref[...]
        do2, dgate = _gate_grads(dy_ref[...], o_ref[...], gate_ref[...])
        dgate_ref[...] = dgate.astype(BF16)
        do2b = do2.astype(BF16)
        prod = do2b.astype(F32) * o_ref[...]
        heads = []
        for hh in (0, 1):
            hm = _head_mask(hh)
            heads.append((jnp.where(hm, q2, jnp.zeros_like(q2)), jnp.where(hm, do2b, jnp.zeros_like(do2b)),
                          jnp.sum(jnp.where(hm, prod, 0.0), axis=1, keepdims=True),
                          jnp.max(jnp.where(hm, lse2, NEG), axis=1, keepdims=True)))

        def step(off, width, dqs, diag):
            off = pl.multiple_of(off, ATT)
            kj, vj = k_ref[pl.ds(off, width), :], v_ref[pl.ds(off, width), :]
            dk, dv, out = None, None, []
            for hh in (0, 1):
                qm, dom, delta, lse_h = heads[hh]
                s = _dot(qm, kj, NT) - c_ref[hh:hh + 1, pl.ds(off, width)]
                p = jnp.exp(s - lse_h)
                if diag:
                    row = i * ATT + lax.broadcasted_iota(jnp.int32, (ATT, width), 0)
                    col = off + lax.broadcasted_iota(jnp.int32, (ATT, width), 1)
                    p = jnp.where(col <= row, p, 0.0)
                ds = p * (_dot(dom, vj, NT) - delta)
                dc_ref[hh:hh + 1, pl.ds(off, width)] += -jnp.sum(ds, axis=0, keepdims=True)
                dsb = ds.astype(BF16)
                dk_h, dv_h = _dot(dsb, qm, TN), _dot(p.astype(BF16), dom, TN)
                dk, dv = (dk_h, dv_h) if dk is None else (dk + dk_h, dv + dv_h)
                out.append(dqs[hh] + _dot(dsb, kj))
            dk_ref[pl.ds(off, width), :] += dk
            dv_ref[pl.ds(off, width), :] += dv
            return tuple(out)

        zero = jnp.zeros((ATT, LANES), F32)
        dqs = lax.fori_loop(0, i // 2, lambda j, acc: step(j * (2 * ATT), 2 * ATT, acc, False), (zero, zero))
        dqs = lax.cond(i % 2 == 1, lambda acc: step((i - 1) * ATT, 2 * ATT, acc, True),
                       lambda acc: step(i * ATT, ATT, acc, True), dqs)
        dq_ref[...] = jnp.where(_head_mask(0), dqs[0], dqs[1]) * SCALE

        @pl.when(i == nq - 1)
        def _():
            dvb_ref[...] = dv_ref[...].astype(BF16)

        at_end()

    blk = pl.BlockSpec((ATT, LANES), lambda p, i: (i, p))
    full = pl.BlockSpec((S, LANES), lambda p, i: (0, p))
    cspec = pl.BlockSpec((None, 2, S), lambda p, i: (p, 0, 0))
    res = pl.pallas_call(
        body, name="fox_bwd", grid=(npair, nq),
        in_specs=[blk, full, full, cspec, blk, blk, blk, blk] + riding.in_specs,
        out_specs=[blk, full, full, cspec, blk] + riding.out_specs,
        out_shape=[jax.ShapeDtypeStruct((S, D), F32)] * 2 + [jax.ShapeDtypeStruct((S, D), BF16),
                                                              jax.ShapeDtypeStruct((npair, 2, S), F32),
                                                              jax.ShapeDtypeStruct((S, D), BF16)]
                  + riding.out_shape,
        scratch_shapes=[pltpu.VMEM((S, LANES), F32)] + riding.scratch,
        compiler_params=_params(2))(q, k, v, ct, o, lse, dy, gate, *riding.ins)
    return res[0], res[1], res[2], res[3], res[4], res[5:]


def _both_heads(x):
    return jnp.concatenate([jnp.where(_head_mask(hh), x, jnp.zeros_like(x)) for hh in (0, 1)], axis=0)


def _per_head(col0, col1):
    return jnp.concatenate([jnp.broadcast_to(col0, (WINDOW, 1)), jnp.broadcast_to(col1, (WINDOW, 1))], axis=0)


def _unstack(x2):
    return jnp.where(_head_mask(0), x2[:WINDOW], x2[WINDOW:])


def _swa_valid(i, start):
    r = lax.broadcasted_iota(jnp.int32, (2 * WINDOW, 2 * WINDOW), 0)
    qabs = i * WINDOW + jnp.where(r >= WINDOW, r - WINDOW, r)
    kabs = start + lax.broadcasted_iota(jnp.int32, (2 * WINDOW, 2 * WINDOW), 1)
    return (kabs <= qabs) & (qabs - kabs < WINDOW)


def _swa_fwd(q, kdup, vdup, sinks_t, proj, gate_col):
    def body(q_ref, k_ref, v_ref, sk_ref, gate_ref, o_ref, lse_ref, y_ref):
        skv = sk_ref[...]
        first = _head_mask(0)
        for sb in range(SWQ):
            i = pl.program_id(1) * SWQ + sb
            rows = slice(sb * WINDOW, (sb + 1) * WINDOW)
            start = pl.multiple_of(jnp.maximum(i - 1, 0) * WINDOW, WINDOW)
            kk, vv = k_ref[pl.ds(start, 2 * WINDOW), :], v_ref[pl.ds(start, 2 * WINDOW), :]
            q2 = q_ref[rows, :]
            valid = _swa_valid(i, start)[:WINDOW]
            res = []
            for hh in (0, 1):
                hm = _head_mask(hh)
                sink = jnp.max(jnp.where(hm, skv, NEG), axis=1, keepdims=True)
                s = jnp.where(valid, _dot(jnp.where(hm, q2, jnp.zeros_like(q2)), kk, NT), NEG)
                m = jnp.maximum(jnp.max(s, axis=1, keepdims=True), sink)
                p = jnp.exp(s - m)
                l = jnp.sum(p, axis=1, keepdims=True) + jnp.exp(sink - m)
                res.append((_dot(p.astype(BF16), vv) / l, m + jnp.log(l)))
            o = jnp.where(first, res[0][0], res[1][0])
            o_ref[rows, :] = o
            lse_ref[rows, :] = jnp.where(first, res[0][1], res[1][1])
            g = gate_ref[rows, :]
            y_ref[rows, :] = (o * (g * _sigmoid(g))).astype(BF16)

    blk = pl.BlockSpec((SWQ * WINDOW, LANES), lambda p, i: (i, p))
    gate = pl.BlockSpec((SWQ * WINDOW, LANES), lambda p, i: (i, gate_col + p))
    full = pl.BlockSpec((S, LANES), lambda p, i: (0, p // 2))
    return pl.pallas_call(
        body, name="swa_fwd", grid=(NH // 2, S // (SWQ * WINDOW)),
        in_specs=[blk, full, full, pl.BlockSpec((1, LANES), lambda p, i: (0, p)), gate],
        out_specs=[blk, blk, blk],
        out_shape=[jax.ShapeDtypeStruct((S, D), F32)] * 2 + [jax.ShapeDtypeStruct((S, D), BF16)],
        compiler_params=_params(2))(q, kdup, vdup, sinks_t, proj)


def _swa_bwd(q, kdup, vdup, sinks_t, o, lse, dy, proj, gate_col):
    def body(q_ref, k_ref, v_ref, sk_ref, o_ref, lse_ref, dy_ref, gate_ref, dq_ref, dk_ref, dv_ref, dsk_ref,
             dgate_ref):
        @pl.when(pl.program_id(1) == 0)
        def _():
            dk_ref[...] = jnp.zeros_like(dk_ref)
            dv_ref[...] = jnp.zeros_like(dv_ref)
            dsk_ref[...] = jnp.zeros_like(dsk_ref)

        skv = sk_ref[...]
        first = _head_mask(0)
        sink = _per_head(*[jnp.max(jnp.where(_head_mask(hh), skv, NEG), axis=1, keepdims=True) for hh in (0, 1)])
        for sb in range(SWQ):
            i = pl.program_id(1) * SWQ + sb
            rows = slice(sb * WINDOW, (sb + 1) * WINDOW)
            start = pl.multiple_of(jnp.maximum(i - 1, 0) * WINDOW, WINDOW)
            kk, vv = k_ref[pl.ds(start, 2 * WINDOW), :], v_ref[pl.ds(start, 2 * WINDOW), :]
            do2, dgate = _gate_grads(dy_ref[rows, :], o_ref[rows, :], gate_ref[rows, :])
            dgate_ref[rows, :] = dgate.astype(BF16)
            do2b = do2.astype(BF16)
            prod, lse2 = do2b.astype(F32) * o_ref[rows, :], lse_ref[rows, :]
            qs, dos = _both_heads(q_ref[rows, :]), _both_heads(do2b)
            delta = jnp.concatenate([jnp.sum(jnp.where(_head_mask(hh), prod, 0.0), axis=1, keepdims=True)
                                     for hh in (0, 1)], axis=0)
            lse_h = jnp.concatenate([jnp.max(jnp.where(_head_mask(hh), lse2, NEG), axis=1, keepdims=True)
                                     for hh in (0, 1)], axis=0)
            p = jnp.where(_swa_valid(i, start), jnp.exp(_dot(qs, kk, NT) - lse_h), 0.0)
            dsb = (p * (_dot(dos, vv, NT) - delta)).astype(BF16)
            dk_ref[pl.ds(start, 2 * WINDOW), :] += _dot(dsb, qs, TN)
            dv_ref[pl.ds(start, 2 * WINDOW), :] += _dot(p.astype(BF16), dos, TN)
            dq_ref[rows, :] = _unstack(_dot(dsb, kk)) * SCALE
            t = jnp.exp(sink - lse_h) * delta
            dsk_ref[...] += -jnp.where(first, jnp.sum(t[:WINDOW], axis=0, keepdims=True),
                                       jnp.sum(t[WINDOW:], axis=0, keepdims=True))

    blk = pl.BlockSpec((SWQ * WINDOW, LANES), lambda p, i: (i, p))
    full = pl.BlockSpec((S, LANES), lambda p, i: (0, p // 2))
    acc = pl.BlockSpec((S, LANES), lambda p, i: (0, p))
    sk = pl.BlockSpec((1, LANES), lambda p, i: (0, p))
    gate = pl.BlockSpec((SWQ * WINDOW, LANES), lambda p, i: (i, gate_col + p))
    return pl.pallas_call(
        body, name="swa_bwd", grid=(NH // 2, S // (SWQ * WINDOW)),
        in_specs=[blk, full, full, sk, blk, blk, blk, gate],
        out_specs=[blk, acc, acc, sk, blk],
        out_shape=[jax.ShapeDtypeStruct((S, D), F32)] * 3 + [jax.ShapeDtypeStruct((1, D), F32),
                                                              jax.ShapeDtypeStruct((S, D), BF16)],
        compiler_params=_params(2))(q, kdup, vdup, sinks_t, o, lse, dy, proj)


def _adamw_math(w, g, m, v):
    m = ADAM_B1 * m + (1.0 - ADAM_B1) * g
    v = ADAM_B2 * v + (1.0 - ADAM_B2) * jnp.square(g)
    m_hat = m / (1.0 - ADAM_B1 ** ADAM_STEP)
    v_hat = v / (1.0 - ADAM_B2 ** ADAM_STEP)
    delta = -ADAM_LR * (m_hat / (jnp.sqrt(v_hat) + ADAM_EPS) + ADAM_WD * w)
    return delta, m, v


def _adamw_small(ws, gs, ms, vs):
    k = len(ws)

    def body(*refs):
        for p in range(k):
            w_ref, g_ref, m_ref, v_ref = (refs[q * k + p] for q in range(4))
            d, mo, vo = _adamw_math(w_ref[...], g_ref[...], m_ref[...], v_ref[...])
            refs[4 * k + p][...], refs[5 * k + p][...], refs[6 * k + p][...] = d, mo, vo

    res = pl.pallas_call(
        body, name="adamw_small",
        out_shape=[jax.ShapeDtypeStruct(t.shape, F32) for t in ws] * 3)(*ws, *gs, *ms, *vs)
    return res[:k], res[k:2 * k], res[2 * k:]


SUM_TILE = 128


FLAT_BLOCK = 257 * 1024


def _tiles(shape, axis, lead=0):
    if len(shape) == 1:
        count = shape[0] // FLAT_BLOCK
        return (FLAT_BLOCK,), count, lambda pos, *lead_idx: (sum(k * count for k in lead_idx) + pos,)
    r, c = shape
    blk = (SUM_TILE, c) if axis == 0 else (r, SUM_TILE)
    count = shape[axis] // SUM_TILE

    def index(pos, *lead_idx):
        return tuple(lead_idx) + ((pos, 0) if axis == 0 else (0, pos))

    return (None,) * lead + blk, count, index


def _adamw_halves(w, g_mine, g_theirs, m, v, axis, name):
    blk, count, index = _tiles(w.shape, axis)
    per_half = count // 2

    def body(w_ref, a_ref, b_ref, m_ref, v_ref, g_ref, d_ref, mo_ref, vo_ref):
        is_mine = pl.program_id(0) // per_half == lax.axis_index("c")
        g = jnp.where(is_mine, a_ref[...], b_ref[...])
        g_ref[...] = g
        d_ref[...], mo_ref[...], vo_ref[...] = _adamw_math(w_ref[...], g, m_ref[...], v_ref[...])

    spec = pl.BlockSpec(blk, lambda i: index(i))
    half = pl.BlockSpec(blk, lambda i: index(i % per_half))
    return pl.pallas_call(
        body, name=name, grid=(count,), in_specs=[spec, half, half, spec, spec], out_specs=[spec] * 4,
        out_shape=[jax.ShapeDtypeStruct(w.shape, F32)] * 4, compiler_params=_params(1))(w, g_mine, g_theirs, m, v)


def _chip_sum(blocks, from_sibling, axis, name):
    flat = blocks.ndim == 1
    blk, count, index = _tiles((from_sibling.shape[0] // NCHIP,) if flat else from_sibling.shape[1:], axis, lead=1)

    def body(lo_ref, hi_ref, p_ref, o32, o16):
        mine = jnp.where(lax.axis_index("c") == 0, lo_ref[...], hi_ref[...])
        acc = mine + p_ref[...]
        o32[...] = acc
        o16[...] = acc.astype(BF16)

    half = pl.BlockSpec(blk, lambda k, i: index(i, k))
    if flat:
        lo = pl.BlockSpec(blk, lambda k, i: (2 * count * k + i,))
        hi = pl.BlockSpec(blk, lambda k, i: (2 * count * k + count + i,))
    else:
        lo, hi = half, pl.BlockSpec(blk, lambda k, i: index(i + count, k))
    return pl.pallas_call(
        body, name=name, grid=(NCHIP, count), in_specs=[lo, hi, half], out_specs=[half, half],
        out_shape=[jax.ShapeDtypeStruct(from_sibling.shape, F32), jax.ShapeDtypeStruct(from_sibling.shape, BF16)],
        compiler_params=_params(2))(blocks, blocks, from_sibling)


def _mesh_sum(own, parts, axis, name):
    blk, count, index = _tiles(own.shape, axis)
    n = NCHIP - 1

    def body(a_ref, *refs):
        acc = a_ref[...]
        for k in range(n):
            acc = acc + refs[k][...].astype(F32)
        refs[n][...] = acc

    spec = pl.BlockSpec(blk, lambda i: index(i))
    if own.ndim == 1:
        part = [pl.BlockSpec(blk, lambda i, k=k: (k * count + i,)) for k in range(n)]
    else:
        part = [pl.BlockSpec((None,) + blk, lambda i, k=k: (k,) + index(i)) for k in range(n)]
    return pl.pallas_call(
        body, name=name, grid=(count,), in_specs=[spec] + part,
        out_specs=spec, out_shape=jax.ShapeDtypeStruct(own.shape, F32),
        compiler_params=_params(1))(own, *([parts] * n))


def _sum_stack(parts, name):
    n = parts.shape[0]

    def body(p_ref, o_ref):
        acc = p_ref[0]
        for k in range(1, n):
            acc = acc + p_ref[k]
        o_ref[...] = acc

    return pl.pallas_call(body, name=name, out_shape=jax.ShapeDtypeStruct(parts.shape[1:], F32))(parts)


def _coords():
    return lax.axis_index("x"), lax.axis_index("y"), lax.axis_index("c")


def _chip(who):
    return 2 * who[0] + who[1]


def _flip(who, mask):
    return tuple((1 - v) if b else v for v, b in zip(who, mask))


def _transfer(transfers, t, I, O, ssem, rsem, receiving):
    tr, me = transfers[t], _coords()
    peer = _flip(me, tr["mask"])
    return pltpu.make_async_remote_copy(
        src_ref=tr["src"](I, O, me), dst_ref=tr["dst"](I, O, peer if receiving else me),
        send_sem=ssem.at[t], recv_sem=rsem.at[t], device_id=peer, device_id_type=MESH)


def _start_transfers(transfers, I, O, ssem, rsem, onward):
    arrived = set()
    for t, tr in enumerate(transfers):
        after = tr.get("after")
        if (after is not None) != onward:
            continue
        if after is not None and after not in arrived:
            _transfer(transfers, after, I, O, ssem, rsem, True).wait_recv()
            arrived.add(after)
        _transfer(transfers, t, I, O, ssem, rsem, False).start()


def _finish_transfers(transfers, I, O, ssem, rsem):
    passed_on = {tr["after"] for tr in transfers if tr.get("after") is not None}
    for t in range(len(transfers)):
        if t not in passed_on:
            _transfer(transfers, t, I, O, ssem, rsem, True).wait_recv()
    for t in range(len(transfers)):
        _transfer(transfers, t, I, O, ssem, rsem, False).wait_send()


def _own_copies(own, I, O, stage, lsem, leg):
    for n, (src, dst) in enumerate(own):
        me = _coords()
        bring =pltpu.make_async_copy(src(I, O, me), stage[n], lsem.at[2 * n])
        put = pltpu.make_async_copy(stage[n], dst(I, O, me), lsem.at[2 * n + 1])
        if leg == 0:
            bring.start()
        elif leg == 1:
            bring.wait()
            put.start()
        else:
            put.wait()


def _own_scratch(own, ins):
    return [pltpu.VMEM(ins[n].shape, ins[n].dtype) for n in range(len(own))], pltpu.SemaphoreType.DMA((max(2 * len(own), 1),))


def _exchange(name, ins, outs, transfers, own=()):
    ni, no = len(ins), len(outs)
    nt = len(transfers)
    stages, stage_sems = _own_scratch(own, ins)

    def body(*refs):
        I, O = refs[:ni], refs[ni:ni + no]
        ssem, rsem, lsem = refs[ni + no:ni + no + 3]
        stage = refs[ni + no + 3:]
        _own_copies(own, I, O, stage, lsem, 0)
        _start_transfers(transfers, I, O, ssem, rsem, False)
        _own_copies(own, I, O, stage, lsem, 1)
        _start_transfers(transfers, I, O, ssem, rsem, True)
        _finish_transfers(transfers, I, O, ssem, rsem)
        _own_copies(own, I, O, stage, lsem, 2)

    hbm = pl.BlockSpec(memory_space=pltpu.HBM)
    return pl.pallas_call(
        body, name=name, in_specs=[hbm] * ni, out_specs=[hbm] * no,
        out_shape=[jax.ShapeDtypeStruct(s, d) for s, d in outs],
        scratch_shapes=[pltpu.SemaphoreType.DMA((nt,)), pltpu.SemaphoreType.DMA((nt,)), stage_sems] + stages,
        compiler_params=pltpu.CompilerParams(has_side_effects=True, vmem_limit_bytes=VMEM_LIMIT))(*ins)


CHIP_MASKS = [(0, 1, 0), (1, 0, 0), (1, 1, 0)]
SIBLING = (0, 0, 1)


def _half(shape2d, axis, which):
    n = shape2d[axis] // 2
    cut = pl.ds(pl.multiple_of(which * n, n), n)
    return (cut, slice(None)) if axis == 0 else (slice(None), cut)


class _Riding:
    def __init__(self, transfers, ins, outs, own=()):
        self.transfers, self.ins, self.outs, self.own = transfers, list(ins), list(outs), list(own)
        hbm = pl.BlockSpec(memory_space=pltpu.HBM)
        self.in_specs, self.out_specs = [hbm] * len(self.ins), [hbm] * len(self.outs)
        self.out_shape = [jax.ShapeDtypeStruct(s, d) for s, d in self.outs]
        stages, stage_sems = _own_scratch(self.own, self.ins)
        self.scratch = [pltpu.SemaphoreType.DMA((max(len(transfers), 1),))] * 2 + [stage_sems] + stages

    def alone(self, name):
        return _exchange(name, self.ins, self.outs, self.transfers, self.own)

    def hooks(self, I, O, ssem, rsem, lsem, *stage, first, middle, last):
        tr, own = self.transfers, self.own

        @pl.when(first)
        def _():
            _own_copies(own, I, O, stage, lsem, 0)
            _start_transfers(tr, I, O, ssem, rsem, False)

        if own or any(t.get("after") is not None for t in tr):
            @pl.when(middle)
            def _():
                _own_copies(own, I, O, stage, lsem, 1)
                _start_transfers(tr, I, O, ssem, rsem, True)

        def at_end():
            @pl.when(last)
            def _():
                _finish_transfers(tr, I, O, ssem, rsem)
                _own_copies(own, I, O, stage, lsem, 2)

        return at_end


def _stretch(n, pos):
    return (pl.ds(pos * n if isinstance(pos, int) else pl.multiple_of(pos * n, n), n),)


def _gather_plan(shards, axes):
    def half(a, who):
        if shards[a].ndim == 1:
            return _stretch(shards[a].shape[0] // 2, who[2])
        return _half(shards[a].shape, axes[a], who[2])

    def landed(a, chip, who):
        if shards[a].ndim == 1:
            return _stretch(shards[a].shape[0] // 2, 2 * chip + who[2])
        return (chip,) + half(a, who)

    over_ici, onward = [], []
    for a in range(len(shards)):
        for mask in CHIP_MASKS:
            over_ici.append(dict(
                mask=mask,
                src=lambda I, O, me, a=a: I[a].at[half(a, me)],
                dst=lambda I, O, who, a=a: O[a].at[landed(a, _chip(who), who)]))
            onward.append(dict(
                mask=SIBLING, after=len(over_ici) - 1,
                src=lambda I, O, me, a=a, mask=mask: O[a].at[landed(a, _chip(_flip(me, mask)), me)],
                dst=lambda I, O, who, a=a, mask=mask: O[a].at[landed(a, _chip(_flip(who, mask)), who)]))
    outs = [((NCHIP * s.shape[0],) if s.ndim == 1 else (NCHIP,) + s.shape, s.dtype) for s in shards]

    def whole(a, chip):
        return _stretch(shards[a].shape[0], chip) if shards[a].ndim == 1 else (chip,)

    own = [(lambda I, O, me, a=a: I[a], lambda I, O, me, a=a: O[a].at[whole(a, _chip(me))])
           for a in range(len(shards))]
    return over_ici + onward, outs, own


def _gather_shards(shards, axes):
    transfers, outs, own = _gather_plan(shards, axes)
    return _exchange("gather_weights", shards, outs, transfers, own)


def _to_sibling(arrs, name):
    transfers = [dict(mask=SIBLING, src=lambda I, O, me, a=a: I[a], dst=lambda I, O, who, a=a: O[a])
                 for a in range(len(arrs))]
    return _exchange(name, arrs, [(t.shape, t.dtype) for t in arrs], transfers)


def _halves_plan(blocks, axes):
    def cut(a, which):
        return (slice(None),) + _half(blocks[a].shape[1:], axes[a], which)

    transfers, outs = [], []
    for a, (b, ax) in enumerate(zip(blocks, axes)):
        if b.ndim == 1:
            h = b.shape[0] // NCHIP // 2
            for k in range(NCHIP):
                transfers.append(dict(mask=SIBLING,
                                      src=lambda I, O, me, a=a, k=k, h=h: I[a].at[_stretch(h, 2 * k + 1 - me[2])],
                                      dst=lambda I, O, who, a=a, k=k, h=h: O[a].at[_stretch(h, k)]))
            outs.append(((NCHIP * h,), b.dtype))
        else:
            transfers.append(dict(mask=SIBLING, src=lambda I, O, me, a=a: I[a].at[cut(a, 1 - me[2])],
                                  dst=lambda I, O, who, a=a: O[a]))
            shape = list(b.shape)
            shape[ax + 1] //= 2
            outs.append((tuple(shape), b.dtype))
    return transfers, outs


def _scatter_plan(tb):
    def slot(a, k):
        return (k,) if tb[a].ndim == 3 else _stretch(tb[a].shape[0] // NCHIP, k)

    transfers = []
    for a in range(len(tb)):
        for n, mask in enumerate(CHIP_MASKS):
            transfers.append(dict(
                mask=mask,
                src=lambda I, O, me, a=a, mask=mask: I[a].at[slot(a, _chip(_flip(me, mask)))],
                dst=lambda I, O, who, a=a, n=n: O[a].at[slot(a, n)]))
    outs = [((3,) + t.shape[1:] if t.ndim == 3 else (3 * (t.shape[0] // NCHIP),), t.dtype) for t in tb]
    return transfers, outs


def _gather_small(vec):
    def slot(who):
        return 4 * who[0] + 2 * who[1] + who[2]

    masks = [(m >> 2 & 1, m >> 1 & 1, m & 1) for m in range(1, 8)]
    transfers = [dict(mask=mask, src=lambda I, O, me: I[0], dst=lambda I, O, who: O[0].at[slot(who)])
                 for mask in masks]
    own = [(lambda I, O, me: I[0], lambda I, O, me: O[0].at[slot(me)])]
    return _exchange("gather_small", [vec], [((8,) + vec.shape, vec.dtype)], transfers, own)[0]


def _rope_tables(positions):
    half = ROT // 2
    inv_freq = jnp.power(jnp.float32(THETA), -jnp.arange(0, ROT, 2, dtype=F32) / ROT)
    ang = positions.astype(F32)[:, None] * inv_freq[None, :]
    cos, sin = jnp.cos(ang), jnp.sin(ang)
    one, zero, z8 = jnp.ones((S, HD - ROT), F32), jnp.zeros((S, HD - ROT), F32), jnp.zeros((S, half), F32)
    c = jnp.concatenate([cos, cos, one], axis=1)
    a = jnp.concatenate([-sin, z8, zero], axis=1)
    b = jnp.concatenate([z8, sin, zero], axis=1)
    return tuple(jnp.tile(t, (1, 2)) for t in (c, a, b))


def _tile_heads(g, w):
    return jnp.tile(g.reshape(1, HD), (1, w // HD))


def _fold_heads(dg):
    return dg.reshape(-1, HD).sum(axis=0)


def _pad_lanes(a):
    return jnp.pad(a, ((0, 0), (0, LANES - a.shape[1])))


def _local_step(x, target, positions, wt, fetch, late_weights, begin_reduce):
    rope = _rope_tables(positions)
    w1t = wt["w_in_a_t"]
    f_row = 3 * D // LANES
    wg_t = w1t[3 * D + NH:]
    in_b_block = lambda c: pl.BlockSpec((None, TN_, TN_), lambda j, i: (c, j, 0))
    b_pad = _pad_lanes(wt["b_forget"].reshape(1, NH))
    qg_a, kg_a = _tile_heads(wt["qnorm_a_g"], D), _tile_heads(wt["knorm_a_g"], D)
    qg_b, kg_b = _tile_heads(wt["qnorm_b_g"], D), _tile_heads(wt["knorm_b_g"], KVW)
    norm_a, kv_g, norm_b = wt["norm_a_g"].reshape(1, D), wt["kv_norm_g"].reshape(1, D), wt["norm_b_g"].reshape(1, D)
    sinks_t = jnp.repeat(wt["sinks"].reshape(1, NH), HD, axis=1)

    (u_a,) = _rmsnorm_fwd(x, [norm_a], "norm_a")
    qkv = _mm("proj_a", S, 3 * D, [(u_a, _a_rows(D), w1t, _b_rows(D), NT)])
    fpad = _mm("proj_f", S, LANES, [(u_a, _a_rows(D), w1t, _b_rows(D, row0=f_row, tn=LANES), NT)], tn=LANES)
    gate_a = _mm("proj_gate_a", S, D, [(u_a, _a_rows(D), wg_t, _b_rows(D), NT)])
    q_a, k_a, v_a = _a_post(qkv, qg_a, kg_a)
    ct = _forget_cumsum(fpad, b_pad)
    ct2 = ct[:NH].reshape(NH // 2, 2, S)
    o_a, lse_a, y_a, fetched = _fox_fwd(q_a, k_a, v_a, ct2, gate_a, fetch)
    wt = {**wt, **late_weights(fetched)}
    w_in_b = wt["w_in_b"]
    h1 = _mm("out_a", S, D, [(y_a, _a_rows(D), wt["w_out_a"], _b_cols(D), None)], add=x)
    u_kv, u_b = _rmsnorm_fwd(h1, [kv_g, norm_b], "norm_b")
    kv = _mm("proj_kv", S, 2 * KVW, [(u_kv, _a_rows(D), wt["w_kv"], _b_cols(D), None)])
    pb = _mm("proj_b", S, 2 * D,
             [(u_b, _a_rows(D), w_in_b, pl.BlockSpec((None, D, TN_), lambda j, i: (j, 0, 0)), None)])
    q_b, kdup, vdup = _b_post(pb, kv, qg_b, kg_b, rope)
    gate_b_col = D // LANES
    o_b, lse_b, y_b = _swa_fwd(q_b, kdup, vdup, sinks_t, pb, gate_b_col)
    out = _mm("out_b", S, D, [(y_b, _a_rows(D), wt["w_out_b"], _b_cols(D), None)], add=h1)
    d_out, d_out_b, sq = _loss_head(out, target)

    g = {}
    g["w_out_b"] = _mm("dw_out_b", D, D, [(y_b, _a_cols(S), d_out_b, _b_cols(S), TN)])
    d_y_b = _mm("dy_b", S, D, [(d_out_b, _a_rows(D), wt["w_out_b"], _b_rows(D), NT)])
    dq_b, dkdup, dvdup, dsk, d_gate_b = _swa_bwd(q_b, kdup, vdup, sinks_t, o_b, lse_b, d_y_b, pb, gate_b_col)
    g["sinks"] = dsk[0, ::HD]
    d_qb_raw, dg = _headnorm_bwd(pb, 0, qg_b, dq_b, rope, "qnorm_b_bwd")
    g["qnorm_b_g"] = _fold_heads(dg)
    d_pb = [d_qb_raw, d_qb_raw, d_gate_b, d_gate_b]
    g["w_in_b"] = jnp.concatenate([
        _mm("dw_in_b_q", D, D, [(u_b, _a_cols(S), d_qb_raw, _b_cols(S), TN)], stacked=True),
        _mm("dw_in_b_gate", D, D, [(u_b, _a_cols(S), d_gate_b, _b_cols(S), TN)], stacked=True)], axis=0)
    d_u_b = _mm("du_b", S, D, [(d_pb[c], _a_rows(TN_, col=c % 2), w_in_b, in_b_block(c), NT) for c in range(NCHIP)])
    d_kv, dg = _kv_bwd(dkdup, dvdup, kv, kg_b, rope)
    g["knorm_b_g"] = _fold_heads(dg)
    g["w_kv"] = _mm("dw_kv", D, 2 * KVW, [(u_kv, _a_cols(S), d_kv, _b_cols(S), TN)])
    d_u_kv = _mm("du_kv", S, D, [(d_kv, _a_rows(2 * KVW), wt["w_kv"], _b_rows(2 * KVW), NT)])
    d_h1, d_h1_b, g["kv_norm_g"], g["norm_b_g"] = _rmsnorm_bwd(h1, [kv_g, norm_b], [d_u_kv, d_u_b], d_out, "norm_b_bwd")
    g["w_out_a"] = _mm("dw_out_a", D, D, [(y_a, _a_cols(S), d_h1_b, _b_cols(S), TN)])
    late = {n: g[n] for n in LATE}
    d_y_a, halves = _mm("dy_a", S, D, [(d_h1_b, _a_rows(D), wt["w_out_a"], _b_rows(D), NT)],
                        riding=begin_reduce(late))
    riding, so_far = begin_reduce(late, halves)
    dq_a, dk_a, dv_a, dct, d_gate_a, arrived = _fox_bwd(q_a, k_a, v_a, ct2, o_a, lse_a, d_y_a, gate_a, riding)
    dct_pad = jnp.pad(dct.reshape(NH, S), ((0, LANES - NH), (0, 0)))
    d_f, db = _forget_bwd(dct_pad, fpad, b_pad)
    g["b_forget"] = db[0, :NH]
    d_q_raw, dg = _headnorm_bwd(qkv, 0, qg_a, dq_a, None, "qnorm_a_bwd")
    g["qnorm_a_g"] = _fold_heads(dg)
    d_k_raw, dg = _headnorm_bwd(qkv, 1, kg_a, dk_a, None, "knorm_a_bwd")
    g["knorm_a_g"] = _fold_heads(dg)
    rows, gw = 4 * D + NH, None
    for n, t, row0 in (("q", d_q_raw, 0), ("k", d_k_raw, D), ("v", dv_a, 2 * D)):
        gw = _mm("dw_in_a_" + n, D, D, [(t, _a_cols(S), u_a, _b_cols(S), TN)], rows_of=(gw, rows, row0))
    gw = _mm("dw_in_a_f", LANES, D, [(d_f, _a_cols(S, tm=LANES), u_a, _b_cols(S), TN)], tm=LANES,
             rows_of=(gw, rows, 3 * D))
    g["w_in_a"] = _mm("dw_in_a_gate", D, D, [(d_gate_a, _a_cols(S), u_a, _b_cols(S), TN)],
                      rows_of=(gw, rows, 3 * D + NH))
    first = {"w_in_a": g["w_in_a"]}
    riding, so_far_first = begin_reduce(first, begin_reduce(first).alone("sibling_halves_w_in_a"))
    d_u_a, arrived_first = _mm("du_a", S, D, [
        (d_q_raw, _a_rows(D), w1t, _b_cols(D, row=0), None), (d_k_raw, _a_rows(D), w1t, _b_cols(D, row=1), None),
        (dv_a, _a_rows(D), w1t, _b_cols(D, row=2), None), (d_gate_a, _a_rows(D), wg_t, _b_cols(D), None),
        (d_f, _a_rows(LANES), w1t, _b_cols(LANES, row=f_row), None)], riding=riding)
    d_x, _, g["norm_a_g"] = _rmsnorm_bwd(x, [norm_a], [d_u_a], d_h1, "norm_a_bwd")
    return sq, d_x, g, (list(so_far_first) + list(so_far), list(arrived_first) + list(arrived))


BIG = ["w_in_a", "w_out_a", "w_kv", "w_in_b", "w_out_b"]
LATE = BIG[1:]
SPLIT = {"w_in_a": None, "w_out_a": 0, "w_kv": 0, "w_in_b": 0, "w_out_b": 0}
SMALL = ["norm_a_g", "b_forget", "qnorm_a_g", "knorm_a_g", "kv_norm_g", "knorm_b_g", "norm_b_g", "qnorm_b_g", "sinks"]
NAMES = ["norm_a_g", "w_in_a", "b_forget", "qnorm_a_g", "knorm_a_g", "w_out_a", "kv_norm_g", "w_kv", "knorm_b_g",
         "norm_b_g", "w_in_b", "qnorm_b_g", "sinks", "w_out_b"]


def _pack(vals):
    flat = []
    for v in vals:
        v = v.reshape(-1)
        flat.append(jnp.pad(v, (0, -v.shape[0] % LANES)))
    flat = jnp.concatenate(flat)
    flat = jnp.pad(flat, (0, -flat.shape[0] % (8 * LANES)))
    return flat.reshape(-1, LANES)


def _unpack(packed, shapes):
    flat, out, off = packed.reshape(-1), [], 0
    for s in shapes:
        n = int(np.prod(s))
        out.append(flat[off:off + n].reshape(s))
        off += n + (-n % LANES)
    return out


def kernel(x, positions, norm_a_g, w_in_a, b_forget, qnorm_a_g, knorm_a_g, w_out_a, kv_norm_g, w_kv, knorm_b_g, norm_b_g, w_in_b, qnorm_b_g, sinks, w_out_b, loss_target, m_norm_a_g, m_w_in_a, m_b_forget, m_qnorm_a_g, m_knorm_a_g, m_w_out_a, m_kv_norm_g, m_w_kv, m_knorm_b_g, m_norm_b_g, m_w_in_b, m_qnorm_b_g, m_sinks, m_w_out_b, v_norm_a_g, v_w_in_a, v_b_forget, v_qnorm_a_g, v_knorm_a_g, v_w_out_a, v_kv_norm_g, v_w_kv, v_knorm_b_g, v_norm_b_g, v_w_in_b, v_qnorm_b_g, v_sinks, v_w_out_b):
    w = dict(norm_a_g=norm_a_g, w_in_a=w_in_a, b_forget=b_forget, qnorm_a_g=qnorm_a_g, knorm_a_g=knorm_a_g,
             w_out_a=w_out_a, kv_norm_g=kv_norm_g, w_kv=w_kv, knorm_b_g=knorm_b_g, norm_b_g=norm_b_g,
             w_in_b=w_in_b, qnorm_b_g=qnorm_b_g, sinks=sinks, w_out_b=w_out_b)
    m = dict(norm_a_g=m_norm_a_g, w_in_a=m_w_in_a, b_forget=m_b_forget, qnorm_a_g=m_qnorm_a_g, knorm_a_g=m_knorm_a_g,
             w_out_a=m_w_out_a, kv_norm_g=m_kv_norm_g, w_kv=m_w_kv, knorm_b_g=m_knorm_b_g, norm_b_g=m_norm_b_g,
             w_in_b=m_w_in_b, qnorm_b_g=m_qnorm_b_g, sinks=m_sinks, w_out_b=m_w_out_b)
    v = dict(norm_a_g=v_norm_a_g, w_in_a=v_w_in_a, b_forget=v_b_forget, qnorm_a_g=v_qnorm_a_g, knorm_a_g=v_knorm_a_g,
             w_out_a=v_w_out_a, kv_norm_g=v_kv_norm_g, w_kv=v_w_kv, knorm_b_g=v_knorm_b_g, norm_b_g=v_norm_b_g,
             w_in_b=v_w_in_b, qnorm_b_g=v_qnorm_b_g, sinks=v_sinks, w_out_b=v_w_out_b)
    my_chip = 2 * lax.axis_index("x") + lax.axis_index("y")

    def shard2d(t, n):
        if n == "w_in_a":
            return jnp.transpose(t, (2, 0, 1)).reshape(-1)
        return t.reshape(t.shape[-2:])

    def unflat(t, n):
        return jnp.transpose(t.reshape(-1, 1, D), (1, 2, 0)) if n == "w_in_a" else t.reshape(w[n].shape)

    w2d = {n: shard2d(w[n], n) for n in BIG}

    norm_a_rows = jnp.broadcast_to(norm_a_g.reshape(1, D // NCHIP), (2 * SUBLANES, D // NCHIP))
    w1t, norm_rows = _gather_shards([w2d["w_in_a"].astype(BF16), norm_a_rows], [SPLIT["w_in_a"], 0])
    wt = {"w_in_a_t": w1t.reshape(-1, D), "norm_a_g": norm_rows[:, 0, :].reshape(1, D)}
    for n in SMALL[1:]:
        wt[n] = w[n]
    late_shards = [w2d[n].astype(BF16) for n in LATE]
    late_axes = [SPLIT[n] for n in LATE]
    transfers, outs, own = _gather_plan(late_shards, late_axes)
    fetch = _Riding(transfers, late_shards, outs, own)

    def late_weights(fetched):
        return {n: t if n == "w_in_b" else t.reshape(-1, t.shape[2]) for n, t in zip(LATE, fetched)}

    def as_blocks(t):
        if t.ndim == 3:
            return t
        return t.reshape(-1) if t.shape[0] % (SUBLANES * NCHIP) else t.reshape(NCHIP, -1, t.shape[1])

    def begin_reduce(grads, halves=None):
        names = list(grads)
        axes = [SPLIT[n] for n in names]
        blocks = [as_blocks(grads[n]) for n in names]
        if halves is None:
            transfers, outs = _halves_plan(blocks, axes)
            return _Riding(transfers, blocks, outs)
        sums = [_chip_sum(blk, part, ax, "chip_sum_" + n) for n, ax, blk, part in zip(names, axes, blocks, halves)]
        bf16 = [s[1] for s in sums]
        transfers, outs = _scatter_plan(bf16)
        return _Riding(transfers, bf16, outs), [s[0] for s in sums]

    sq, d_x, g, (chip_f32, arrived) = _local_step(x[0], loss_target[0], positions, wt, fetch, late_weights,
                                                  begin_reduce)

    small_shapes = [(D,), (NH,), (HD,), (HD,), (D,), (HD,), (D,), (HD,), (NH,), (D,)]
    packed = _pack([g[n] for n in SMALL] + [sq])
    total = _sum_stack(_gather_small(packed), "sum_small")
    small_g = dict(zip(SMALL, _unpack(total, small_shapes)[:-1]))
    loss = 0.5 * jnp.sum(_unpack(total, small_shapes)[-1]) / D
    small_g["norm_a_g"] = lax.dynamic_slice(small_g["norm_a_g"], (my_chip * (D // NCHIP),), (D // NCHIP,))

    axes = [SPLIT[n] for n in BIG]
    halves = []
    for n, ax, t32, parts in zip(BIG, axes, chip_f32, arrived):
        if t32.ndim == 1:
            own = lax.dynamic_slice_in_dim(t32, my_chip * (t32.shape[0] // NCHIP), t32.shape[0] // NCHIP)
        else:
            own = lax.dynamic_index_in_dim(t32, my_chip, axis=0, keepdims=False)
        halves.append(_mesh_sum(own, parts, ax, "mesh_sum_" + n))
    sibling_done = _to_sibling(halves, "finished_halves")

    res = {}
    for n, ax, mine_half, their_half in zip(BIG, axes, halves, sibling_done):
        out4 = _adamw_halves(w2d[n], mine_half, their_half, shard2d(m[n], n), shard2d(v[n], n), ax, "adamw_" + n)
        res[n] = tuple(unflat(t, n) for t in out4)
    row = lambda t: t.reshape(1, -1)
    small_out = _adamw_small(*[[row(d[n]) for n in SMALL] for d in (w, small_g, m, v)])
    for i, n in enumerate(SMALL):
        res[n] = tuple(t.reshape(w[n].shape) for t in (small_g[n],) + tuple(out[i] for out in small_out))

    outs = [loss, d_x[None]]
    for k in range(4):
        outs += [res[n][k] for n in NAMES]
    return tuple(outs)
```

```python
import numpy as np
import jax
import jax.numpy as jnp
from jax import lax
from jax.experimental import pallas as pl
from jax.experimental.pallas import tpu as pltpu

F32, BF16 = jnp.float32, jnp.bfloat16
S, D, HD, NH, NKV = 2048, 1024, 64, 16, 4
KVW = NKV * HD
WINDOW = 128
ROT = HD // 4
THETA = 500000.0
EPS = 1e-6
SCALE = HD ** -0.5
LANES = 128
SUBLANES = 8
NEG = -1e30
VMEM_LIMIT = 48 * 2 ** 20
ROWS = 512
ATT = 512
SWQ = 16
NCHIP = 4
ADAM_LR, ADAM_B1, ADAM_B2, ADAM_EPS, ADAM_WD, ADAM_STEP = 0.001, 0.9, 0.999, 1e-08, 0.01, 10
NT = (((1,), (1,)), ((), ()))
TN = (((0,), (0,)), ((), ()))
MESH = pl.DeviceIdType.MESH


def _params(n):
    return pltpu.CompilerParams(dimension_semantics=("arbitrary",) * n, vmem_limit_bytes=VMEM_LIMIT)


def _dot(a, b, dims=None):
    if dims is None:
        return jnp.dot(a, b, preferred_element_type=F32)
    return lax.dot_general(a, b, dims, preferred_element_type=F32)


def _dot_split(a, b, n):
    out, rest = None, a
    for _ in range(n):
        hi = rest.astype(BF16)
        term = _dot(hi, b)
        out = term if out is None else out + term
        rest = rest - hi.astype(F32)
    return out


def _seg_mat(w):
    e = (np.arange(w)[:, None] // HD == np.arange(LANES)[None, :]).astype(np.float32)
    return jnp.asarray(e, BF16)


def _spread(r, w):
    head = lax.broadcasted_iota(jnp.int32, (2 * LANES, w), 1) >> (HD.bit_length() - 1)
    row = lax.broadcasted_iota(jnp.int32, (2 * LANES, w), 0)
    et2 = jnp.where(head == (row & (LANES - 1)), 1.0, 0.0).astype(BF16)
    hi = r.astype(BF16)
    lo = (r - hi.astype(F32)).astype(BF16)
    return _dot(jnp.concatenate([hi, lo], axis=1), et2)


def _head_rstd(x, e):
    ss = _dot_split(x * x, e, 2)
    return _spread(lax.rsqrt(ss * (1.0 / HD) + EPS), x.shape[1])


def _rope(x, c, a, b):
    w = x.shape[1]
    return x * c + pltpu.roll(x, w - ROT // 2, 1) * a + pltpu.roll(x, ROT // 2, 1) * b


def _rope_t(dy, c, a, b):
    w = dy.shape[1]
    return dy * c + pltpu.roll(dy * b, w - ROT // 2, 1) + pltpu.roll(dy * a, ROT // 2, 1)


def _sigmoid(x):
    return 1.0 / (1.0 + jnp.exp(-x))


def _row_spec(shape, ts):
    nd = len(shape)
    if shape[0] == S:
        return pl.BlockSpec((ts,) + tuple(shape[1:]), lambda i: (i,) + (0,) * (nd - 1))
    return pl.BlockSpec(tuple(shape), lambda i: (0,) * nd)


def _rows_call(body, name, ins, outs, ts=ROWS):
    return pl.pallas_call(
        body, name=name, grid=(S // ts,),
        in_specs=[_row_spec(a.shape, ts) for a in ins],
        out_specs=[_row_spec(s, ts) for s, _ in outs],
        out_shape=[jax.ShapeDtypeStruct(s, d) for s, d in outs],
        compiler_params=_params(1))(*ins)


def _col_spec(ts, w, col):
    return pl.BlockSpec((ts, w), lambda i: (i, col))


TM = TN_ = 512
TM_TOKENS = 1024


def _mm(name, m, n, terms, out_dtype=F32, add=None, tm=None, tn=TN_, stacked=False, riding=None, rows_of=None):
    nterm = len(terms)
    if tm is None:
        tm = TM_TOKENS if m == S else TM
    nj, ni_ = n // tn, m // tm
    n_in = 2 * nterm + (add is not None) + (rows_of is not None and rows_of[0] is not None)
    r_in, r_out = (len(riding.ins), len(riding.outs)) if riding is not None else (0, 0)

    def body(*refs):
        if riding is not None:
            j, i = pl.program_id(0), pl.program_id(1)
            at_end = riding.hooks(refs[n_in:n_in + r_in], refs[n_in + r_in + 1:n_in + r_in + 1 + r_out],
                                  *refs[n_in + r_in + 1 + r_out:], first=(j == 0) & (i == 0),
                                  middle=(j == nj // 2) & (i == 0), last=(j == nj - 1) & (i == ni_ - 1))
        acc = None
        for t in range(nterm):
            part = _dot(refs[2 * t][...], refs[2 * t + 1][...], terms[t][4])
            acc = part if acc is None else acc + part
        if add is not None:
            acc = acc + refs[2 * nterm][...]
        refs[n_in + r_in][...] = acc.astype(out_dtype)
        if riding is not None:
            at_end()

    tile = pl.BlockSpec((tm, tn), lambda j, i: (i, j))
    ins, specs = [], []
    for a, a_spec, b, b_spec, _ in terms:
        ins += [a, b]
        specs += [a_spec, b_spec]
    if add is not None:
        ins.append(add)
        specs.append(tile)
    out_spec = pl.BlockSpec((None, tm, tn), lambda j, i: (j, i, 0)) if stacked else tile
    out_shape = jax.ShapeDtypeStruct((nj, m, tn) if stacked else (m, n), out_dtype)
    if rows_of is not None:
        taller, rows, row0 = rows_of
        out_spec = pl.BlockSpec((pl.Element(tm), pl.Element(tn)), lambda j, i: (
            pl.multiple_of(row0 + i * tm, SUBLANES), pl.multiple_of(j * tn, LANES)))
        out_shape = jax.ShapeDtypeStruct((rows, n), out_dtype)
        alias = {}
        if taller is not None:
            ins.append(taller)
            specs.append(pl.BlockSpec(memory_space=pltpu.HBM))
            alias = {len(ins) - 1: 0}
        return pl.pallas_call(body, name=name, grid=(nj, ni_), in_specs=specs, out_specs=out_spec,
                              out_shape=out_shape, input_output_aliases=alias, compiler_params=_params(2))(*ins)
    if riding is None:
        return pl.pallas_call(body, name=name, grid=(nj, ni_), in_specs=specs, out_specs=out_spec,
                              out_shape=out_shape, compiler_params=_params(2))(*ins)
    res = pl.pallas_call(
        body, name=name, grid=(nj, ni_), in_specs=specs + riding.in_specs,
        out_specs=[out_spec] + riding.out_specs, out_shape=[out_shape] + riding.out_shape,
        scratch_shapes=riding.scratch, compiler_params=_params(2))(*ins, *riding.ins)
    return res[0], res[1:]


def _a_rows(k, col=0, tm=TM_TOKENS):
    return pl.BlockSpec((tm, k), lambda j, i: (i, col))


def _a_cols(k, tm=TM):
    return pl.BlockSpec((k, tm), lambda j, i: (0, i))


def _b_cols(k, row=0, col0=0, tn=TN_):
    return pl.BlockSpec((k, tn), lambda j, i: (row, col0 + j))


def _b_rows(k, row0=0, tn=TN_):
    return pl.BlockSpec((tn, k), lambda j, i: (row0 + j, 0))


def _rmsnorm_fwd(x, gains, name):
    def body(*refs):
        xv = refs[0][...]
        r = lax.rsqrt(jnp.mean(xv * xv, axis=-1, keepdims=True) + EPS)
        xh = xv * r
        for n in range(len(gains)):
            refs[1 + len(gains) + n][...] = (xh * refs[1 + n][...]).astype(BF16)

    return _rows_call(body, name, [x] + list(gains), [((S, D), BF16)] * len(gains))


def _rmsnorm_bwd(x, gains, dus, dres, name):
    n = len(gains)

    def body(*refs):
        x_ref, g_refs, du_refs, dres_ref = refs[0], refs[1:1 + n], refs[1 + n:1 + 2 * n], refs[1 + 2 * n]
        dx_ref, dxb_ref, dg_refs = refs[2 + 2 * n], refs[3 + 2 * n], refs[4 + 2 * n:]
        xv = x_ref[...]
        r = lax.rsqrt(jnp.mean(xv * xv, axis=-1, keepdims=True) + EPS)
        xh = xv * r
        gy = None
        for m in range(n):
            du = du_refs[m][...]
            part = jnp.sum(du * xh, axis=0, keepdims=True)

            @pl.when(pl.program_id(0) == 0)
            def _(m=m, part=part):
                dg_refs[m][...] = part

            @pl.when(pl.program_id(0) != 0)
            def _(m=m, part=part):
                dg_refs[m][...] += part

            t = du * g_refs[m][...]
            gy = t if gy is None else gy + t
        dx = dres_ref[...] + r * (gy - xh * jnp.mean(gy * xh, axis=-1, keepdims=True))
        dx_ref[...] = dx
        dxb_ref[...] = dx.astype(BF16)

    outs = [((S, D), F32), ((S, D), BF16)] + [((1, D), F32)] * n
    return _rows_call(body, name, [x] + list(gains) + list(dus) + [dres], outs)


def _a_post(qkvg, qg, kg):
    e = _seg_mat(D)

    def body(q_ref, k_ref, v_ref, qg_ref, kg_ref, e_ref, qo, ko, vo):
        ev = e_ref[...]
        qv, kv = q_ref[...], k_ref[...]
        qo[...] = (qv * _head_rstd(qv, ev) * qg_ref[...] * SCALE).astype(BF16)
        ko[...] = (kv * _head_rstd(kv, ev) * kg_ref[...]).astype(BF16)
        vo[...] = v_ref[...].astype(BF16)

    whole = lambda a: pl.BlockSpec(a.shape, lambda i: (0, 0))
    return pl.pallas_call(
        body, name="a_post", grid=(S // ROWS,),
        in_specs=[_col_spec(ROWS, D, 0), _col_spec(ROWS, D, 1), _col_spec(ROWS, D, 2),
                  whole(qg), whole(kg), whole(e)],
        out_specs=[_col_spec(ROWS, D, 0)] * 3,
        out_shape=[jax.ShapeDtypeStruct((S, D), BF16)] * 3,
        compiler_params=_params(1))(qkvg, qkvg, qkvg, qg, kg, e)


def _tri(upper):
    r, c = np.arange(ROWS)[:, None], np.arange(ROWS)[None, :]
    return jnp.asarray((r <= c) if upper else (r >= c), BF16)


def _forget_cumsum(fpad, bpad):
    def body(f_ref, b_ref, u_ref, c_ref, carry):
        @pl.when(pl.program_id(0) == 0)
        def _():
            carry[...] = jnp.zeros_like(carry)

        lf = jax.nn.log_sigmoid(f_ref[...] + b_ref[...])
        blk = _dot_split(lf.T, u_ref[...], 3) + carry[:, 0:1]
        c_ref[...] = blk
        carry[...] = jnp.broadcast_to(blk[:, ROWS - 1:ROWS], carry.shape)

    return pl.pallas_call(
        body, name="forget_cumsum", grid=(S // ROWS,),
        in_specs=[pl.BlockSpec((ROWS, LANES), lambda i: (i, 0)), pl.BlockSpec((1, LANES), lambda i: (0, 0)),
                  pl.BlockSpec((ROWS, ROWS), lambda i: (0, 0))],
        out_specs=pl.BlockSpec((LANES, ROWS), lambda i: (0, i)),
        out_shape=jax.ShapeDtypeStruct((LANES, S), F32),
        scratch_shapes=[pltpu.VMEM((LANES, LANES), F32)],
        compiler_params=_params(1))(fpad, bpad, _tri(True))


def _forget_bwd(dct, fpad, bpad):
    nb = S // ROWS

    def body(dc_ref, f_ref, b_ref, l_ref, df_ref, db_ref, carry):
        @pl.when(pl.program_id(0) == 0)
        def _():
            carry[...] = jnp.zeros_like(carry)
            db_ref[...] = jnp.zeros_like(db_ref)

        blk = _dot_split(dc_ref[...], l_ref[...], 3) + carry[:, 0:1]
        carry[...] = jnp.broadcast_to(blk[:, 0:1], carry.shape)
        df = blk.T * _sigmoid(-(f_ref[...] + b_ref[...]))
        df_ref[...] = df.astype(BF16)
        db_ref[...] += jnp.sum(df, axis=0, keepdims=True)

    return pl.pallas_call(
        body, name="forget_bwd", grid=(nb,),
        in_specs=[pl.BlockSpec((LANES, ROWS), lambda i: (0, nb - 1 - i)),
                  pl.BlockSpec((ROWS, LANES), lambda i: (nb - 1 - i, 0)),
                  pl.BlockSpec((1, LANES), lambda i: (0, 0)), pl.BlockSpec((ROWS, ROWS), lambda i: (0, 0))],
        out_specs=[pl.BlockSpec((ROWS, LANES), lambda i: (nb - 1 - i, 0)), pl.BlockSpec((1, LANES), lambda i: (0, 0))],
        out_shape=[jax.ShapeDtypeStruct((S, LANES), BF16), jax.ShapeDtypeStruct((1, LANES), F32)],
        scratch_shapes=[pltpu.VMEM((LANES, LANES), F32)],
        compiler_params=_params(1))(dct, fpad, bpad, _tri(False))


def _headnorm_bwd(x, col, gain, dy, rope, name):
    e = _seg_mat(D)
    tabs = list(rope) if rope is not None else []

    def body(*refs):
        x_ref, g_ref, dy_ref, e_ref = refs[:4]
        dx_ref, dg_ref = refs[-2:]
        xv, dyv, ev = x_ref[...], dy_ref[...], e_ref[...]
        if rope is not None:
            c, a, b = (jnp.tile(t[...], (1, D // LANES)) for t in refs[4:7])
            dyv = _rope_t(dyv, c, a, b)
        r = _head_rstd(xv, ev)
        xh = xv * r
        part = jnp.sum(dyv * xh, axis=0, keepdims=True)

        @pl.when(pl.program_id(0) == 0)
        def _():
            dg_ref[...] = part

        @pl.when(pl.program_id(0) != 0)
        def _():
            dg_ref[...] += part

        gy = dyv * g_ref[...]
        seg = _spread(_dot_split(gy * xh, ev, 2) * (1.0 / HD), D)
        dx_ref[...] = (r * (gy - xh * seg)).astype(BF16)

    whole = lambda a: pl.BlockSpec(a.shape, lambda i: (0, 0))
    return pl.pallas_call(
        body, name=name, grid=(S // ROWS,),
        in_specs=[_col_spec(ROWS, D, col), whole(gain), _col_spec(ROWS, D, 0), whole(e)]
                 + [pl.BlockSpec((ROWS, LANES), lambda i: (i, 0))] * len(tabs),
        out_specs=[_col_spec(ROWS, D, 0), whole(gain)],
        out_shape=[jax.ShapeDtypeStruct((S, D), BF16), jax.ShapeDtypeStruct((1, D), F32)],
        compiler_params=_params(1))(x, gain, dy, e, *tabs)


def _dup_mat():
    r, c = np.arange(KVW)[:, None], np.arange(2 * KVW)[None, :]
    return (r // HD == c // LANES) & (r % HD == c % HD)


def _fold_mat():
    r, c = np.arange(D)[:, None], np.arange(KVW)[None, :]
    return (r // (2 * LANES) == c // HD) & (r % HD == c % HD)


def _b_post(pb, kv, qg, kg, rope):
    e, ek = _seg_mat(D), _seg_mat(KVW)
    dup = jnp.asarray(_dup_mat(), BF16)

    def body(q_ref, k_ref, v_ref, qg_ref, kg_ref, e_ref, ek_ref, dup_ref, c_ref, a_ref, b_ref, qo, ko, vo):
        c1, a1, b1 = c_ref[...], a_ref[...], b_ref[...]
        qv = q_ref[...]
        qn = qv * _head_rstd(qv, e_ref[...]) * qg_ref[...]
        t = lambda z, n: jnp.tile(z, (1, n))
        qo[...] = (_rope(qn, t(c1, D // LANES), t(a1, D // LANES), t(b1, D // LANES)) * SCALE).astype(BF16)
        kvv = k_ref[...]
        kn = kvv * _head_rstd(kvv, ek_ref[...]) * kg_ref[...]
        kr = _rope(kn, t(c1, KVW // LANES), t(a1, KVW // LANES), t(b1, KVW // LANES)).astype(BF16)
        ko[...] = _dot(kr, dup_ref[...]).astype(BF16)
        vo[...] = _dot(v_ref[...].astype(BF16), dup_ref[...]).astype(BF16)

    whole = lambda a: pl.BlockSpec(a.shape, lambda i: (0, 0))
    tab = pl.BlockSpec((ROWS, LANES), lambda i: (i, 0))
    return pl.pallas_call(
        body, name="b_post", grid=(S // ROWS,),
        in_specs=[_col_spec(ROWS, D, 0), _col_spec(ROWS, KVW, 0), _col_spec(ROWS, KVW, 1),
                  whole(qg), whole(kg), whole(e), whole(ek), whole(dup), tab, tab, tab],
        out_specs=[_col_spec(ROWS, D, 0), _col_spec(ROWS, 2 * KVW, 0), _col_spec(ROWS, 2 * KVW, 0)],
        out_shape=[jax.ShapeDtypeStruct((S, D), BF16), jax.ShapeDtypeStruct((S, 2 * KVW), BF16),
                   jax.ShapeDtypeStruct((S, 2 * KVW), BF16)],
        compiler_params=_params(1))(pb, kv, kv, qg, kg, e, ek, dup, *rope)


def _kv_bwd(dkdup, dvdup, kv, kg, rope):
    ek = _seg_mat(KVW)
    fold = jnp.asarray(_fold_mat(), BF16)

    def body(dk_ref, dv_ref, k_ref, kg_ref, ek_ref, fold_ref, c_ref, a_ref, b_ref, dkv_ref, dg_ref):
        ev, fv = ek_ref[...], fold_ref[...]
        t = lambda z: jnp.tile(z[...], (1, KVW // LANES))
        dk = _rope_t(_dot_split(dk_ref[...], fv, 2), t(c_ref), t(a_ref), t(b_ref))
        dv = _dot_split(dv_ref[...], fv, 2)
        xv = k_ref[...]
        r = _head_rstd(xv, ev)
        xh = xv * r
        part = jnp.sum(dk * xh, axis=0, keepdims=True)

        @pl.when(pl.program_id(0) == 0)
        def _():
            dg_ref[...] = part

        @pl.when(pl.program_id(0) != 0)
        def _():
            dg_ref[...] += part

        gy = dk * kg_ref[...]
        seg = _spread(_dot_split(gy * xh, ev, 2) * (1.0 / HD), KVW)
        dkv_ref[:, 0:KVW] = (r * (gy - xh * seg)).astype(BF16)
        dkv_ref[:, KVW:2 * KVW] = dv.astype(BF16)

    whole = lambda a: pl.BlockSpec(a.shape, lambda i: (0, 0))
    tab = pl.BlockSpec((ROWS, LANES), lambda i: (i, 0))
    return pl.pallas_call(
        body, name="kv_bwd", grid=(S // ROWS,),
        in_specs=[_col_spec(ROWS, D, 0), _col_spec(ROWS, D, 0), _col_spec(ROWS, KVW, 0),
                  whole(kg), whole(ek), whole(fold), tab, tab, tab],
        out_specs=[_col_spec(ROWS, 2 * KVW, 0), whole(kg)],
        out_shape=[jax.ShapeDtypeStruct((S, 2 * KVW), BF16), jax.ShapeDtypeStruct((1, KVW), F32)],
        compiler_params=_params(1))(dkdup, dvdup, kv, kg, ek, fold, *rope)


def _loss_head(out, target):
    def body(o_ref, t_ref, d_ref, db_ref, l_ref):
        diff = o_ref[...] - t_ref[...]
        d = diff * (1.0 / D)
        d_ref[...] = d
        db_ref[...] = d.astype(BF16)

        @pl.when(pl.program_id(0) == 0)
        def _():
            l_ref[...] = jnp.zeros_like(l_ref)

        l_ref[...] += jnp.sum(diff * diff, axis=0, keepdims=True)

    return _rows_call(body, "loss_head", [out, target], [((S, D), F32), ((S, D), BF16), ((1, D), F32)])


def _lane():
    return lax.broadcasted_iota(jnp.int32, (1, LANES), 1)


def _head_mask(hh):
    return (_lane() < HD) if hh == 0 else (_lane() >= HD)


def _fox_fwd(q, k, v, ct, gate, riding):
    nq, npair = S // ATT, NH // 2
    ni, no = len(riding.ins), len(riding.outs)

    def body(q_ref, k_ref, v_ref, c_ref, gate_ref, *rest):
        o_ref, lse_ref, y_ref = rest[ni:ni + 3]
        pair, i = pl.program_id(0), pl.program_id(1)
        at_end = riding.hooks(rest[:ni], rest[ni + 3:ni + 3 + no], *rest[ni + 3 + no:],
                              first=(pair == 0) & (i == 0), middle=(pair == npair // 2) & (i == 0),
                              last=(pair == npair - 1) & (i == nq - 1))
        q2 = q_ref[...]
        qms = [jnp.where(_head_mask(hh), q2, jnp.zeros_like(q2)) for hh in (0, 1)]

        def probs(off, width, m, hh, diag):
            s = _dot(qms[hh], k_ref[pl.ds(off, width), :], NT) - c_ref[hh:hh + 1, pl.ds(off, width)]
            if diag:
                row = i * ATT + lax.broadcasted_iota(jnp.int32, (ATT, width), 0)
                col = off + lax.broadcasted_iota(jnp.int32, (ATT, width), 1)
                s = jnp.where(col <= row, s, NEG)
            m_new = jnp.maximum(m, jnp.max(s, axis=1, keepdims=True))
            p = jnp.exp(s - m_new)
            p_hi = p.astype(BF16)
            return m_new, jnp.exp(m - m_new), p_hi, (p - p_hi.astype(F32)).astype(BF16)

        def weighted(off, width, p_hi, p_lo, hh):
            vj = v_ref[pl.ds(off, width), :]
            v1 = jnp.where(_head_mask(hh), vj, jnp.ones_like(vj))
            return _dot(p_hi, v1) + _dot(p_lo, v1)

        def step(off, width, carry, diag):
            off = pl.multiple_of(off, ATT)
            out = []
            for hh in (0, 1):
                m, acc = carry[hh]
                m, alpha, p_hi, p_lo = probs(off, width, m, hh, diag)
                out.append((m, alpha * acc + weighted(off, width, p_hi, p_lo, hh)))
            return tuple(out)

        one = (jnp.full((ATT, 1), NEG, F32), jnp.zeros((ATT, LANES), F32))
        carry = lax.fori_loop(0, i // 2, lambda j, cr: step(j * (2 * ATT), 2 * ATT, cr, False), (one, one))
        carry = lax.cond(i % 2 == 1, lambda cr: step((i - 1) * ATT, 2 * ATT, cr, True),
                         lambda cr: step(i * ATT, ATT, cr, True), carry)
        res = []
        for hh in (0, 1):
            m, acc = carry[hh]
            l = jnp.max(jnp.where(_head_mask(1 - hh), acc, 0.0), axis=1, keepdims=True)
            res.append((acc / l, m + jnp.log(l)))
        first = _head_mask(0)
        o = jnp.where(first, res[0][0], res[1][0])
        o_ref[...] = o
        lse_ref[...] = jnp.where(first, res[0][1], res[1][1])
        g = gate_ref[...]
        y_ref[...] = (o * (g * _sigmoid(g))).astype(BF16)
        at_end()

    blk = pl.BlockSpec((ATT, LANES), lambda p, i: (i, p))
    full = pl.BlockSpec((S, LANES), lambda p, i: (0, p))
    res = pl.pallas_call(
        body, name="fox_fwd", grid=(npair, nq),
        in_specs=[blk, full, full, pl.BlockSpec((None, 2, S), lambda p, i: (p, 0, 0)), blk] + riding.in_specs,
        out_specs=[blk, blk, blk] + riding.out_specs,
        out_shape=[jax.ShapeDtypeStruct((S, D), F32)] * 2 + [jax.ShapeDtypeStruct((S, D), BF16)] + riding.out_shape,
        scratch_shapes=riding.scratch,
        compiler_params=_params(2))(q, k, v, ct, gate, *riding.ins)
    return res[0], res[1], res[2], res[3:]


def _gate_grads(dy, o, g):
    sg = _sigmoid(g)
    return dy * (g * sg), dy * o * (sg * (1.0 + g * (1.0 - sg)))


def _fox_bwd(q, k, v, ct, o, lse, dy, gate, riding):
    nq, npair = S // ATT, NH // 2
    ni, no = len(riding.ins), len(riding.outs)

    def body(q_ref, k_ref, v_ref, c_ref, o_ref, lse_ref, dy_ref, gate_ref, *rest):
        dq_ref, dk_ref, dvb_ref, dc_ref, dgate_ref = rest[ni:ni + 5]
        dv_ref = rest[ni + 5 + no]
        pair, i = pl.program_id(0), pl.program_id(1)
        at_end = riding.hooks(rest[:ni], rest[ni + 5:ni + 5 + no], *rest[ni + 6 + no:],
                              first=(pair == 0) & (i == 0), middle=(pair == npair // 2) & (i == 0),
                              last=(pair == npair - 1) & (i == nq - 1))

        @pl.when(i == 0)
        def _():
            dk_ref[...] = jnp.zeros_like(dk_ref)
            dv_ref[...] = jnp.zeros_like(dv_ref)
            dc_ref[...] = jnp.zeros_like(dc_ref)

        q2, lse2 = q_ref[...], lse_ref[...]
        do2, dgate = _gate_grads(dy_ref[...], o_ref[...], gate_ref[...])
        dgate_ref[...] = dgate.astype(BF16)
        do2b = do2.astype(BF16)
        prod = do2b.astype(F32) * o_ref[...]
        heads = []
        for hh in (0, 1):
            hm = _head_mask(hh)
            heads.append((jnp.where(hm, q2, jnp.zeros_like(q2)), jnp.where(hm, do2b, jnp.zeros_like(do2b)),
                          jnp.sum(jnp.where(hm, prod, 0.0), axis=1, keepdims=True),
                          jnp.max(jnp.where(hm, lse2, NEG), axis=1, keepdims=True)))

        def step(off, width, dqs, diag):
            off = pl.multiple_of(off, ATT)
            kj, vj = k_ref[pl.ds(off, width), :], v_ref[pl.ds(off, width), :]
            dk, dv, out = None, None, []
            for hh in (0, 1):
                qm, dom, delta, lse_h = heads[hh]
                s = _dot(qm, kj, NT) - c_ref[hh:hh + 1, pl.ds(off, width)]
                p = jnp.exp(s - lse_h)
                if diag:
                    row = i * ATT + lax.broadcasted_iota(jnp.int32, (ATT, width), 0)
                    col = off + lax.broadcasted_iota(jnp.int32, (ATT, width), 1)
                    p = jnp.where(col <= row, p, 0.0)
                ds = p * (_dot(dom, vj, NT) - delta)
                dc_ref[hh:hh + 1, pl.ds(off, width)] += -jnp.sum(ds, axis=0, keepdims=True)
                dsb = ds.astype(BF16)
                dk_h, dv_h = _dot(dsb, qm, TN), _dot(p.astype(BF16), dom, TN)
                dk, dv = (dk_h, dv_h) if dk is None else (dk + dk_h, dv + dv_h)
                out.append(dqs[hh] + _dot(dsb, kj))
            dk_ref[pl.ds(off, width), :] += dk
            dv_ref[pl.ds(off, width), :] += dv
            return tuple(out)

        zero = jnp.zeros((ATT, LANES), F32)
        dqs = lax.fori_loop(0, i // 2, lambda j, acc: step(j * (2 * ATT), 2 * ATT, acc, False), (zero, zero))
        dqs = lax.cond(i % 2 == 1, lambda acc: step((i - 1) * ATT, 2 * ATT, acc, True),
                       lambda acc: step(i * ATT, ATT, acc, True), dqs)
        dq_ref[...] = jnp.where(_head_mask(0), dqs[0], dqs[1]) * SCALE

        @pl.when(i == nq - 1)
        def _():
            dvb_ref[...] = dv_ref[...].astype(BF16)

        at_end()

    blk = pl.BlockSpec((ATT, LANES), lambda p, i: (i, p))
    full = pl.BlockSpec((S, LANES), lambda p, i: (0, p))
    cspec = pl.BlockSpec((None, 2, S), lambda p, i: (p, 0, 0))
    res = pl.pallas_call(
        body, name="fox_bwd", grid=(npair, nq),
        in_specs=[blk, full, full, cspec, blk, blk, blk, blk] + riding.in_specs,
        out_specs=[blk, full, full, cspec, blk] + riding.out_specs,
        out_shape=[jax.ShapeDtypeStruct((S, D), F32)] * 2 + [jax.ShapeDtypeStruct((S, D), BF16),
                                                              jax.ShapeDtypeStruct((npair, 2, S), F32),
                                                              jax.ShapeDtypeStruct((S, D), BF16)]
                  + riding.out_shape,
        scratch_shapes=[pltpu.VMEM((S, LANES), F32)] + riding.scratch,
        compiler_params=_params(2))(q, k, v, ct, o, lse, dy, gate, *riding.ins)
    return res[0], res[1], res[2], res[3], res[4], res[5:]


def _both_heads(x):
    return jnp.concatenate([jnp.where(_head_mask(hh), x, jnp.zeros_like(x)) for hh in (0, 1)], axis=0)


def _per_head(col0, col1):
    return jnp.concatenate([jnp.broadcast_to(col0, (WINDOW, 1)), jnp.broadcast_to(col1, (WINDOW, 1))], axis=0)


def _unstack(x2):
    return jnp.where(_head_mask(0), x2[:WINDOW], x2[WINDOW:])


def _swa_valid(i, start):
    r = lax.broadcasted_iota(jnp.int32, (2 * WINDOW, 2 * WINDOW), 0)
    qabs = i * WINDOW + jnp.where(r >= WINDOW, r - WINDOW, r)
    kabs = start + lax.broadcasted_iota(jnp.int32, (2 * WINDOW, 2 * WINDOW), 1)
    return (kabs <= qabs) & (qabs - kabs < WINDOW)


def _swa_fwd(q, kdup, vdup, sinks_t, proj, gate_col):
    def body(q_ref, k_ref, v_ref, sk_ref, gate_ref, o_ref, lse_ref, y_ref):
        skv = sk_ref[...]
        first = _head_mask(0)
        for sb in range(SWQ):
            i = pl.program_id(1) * SWQ + sb
            rows = slice(sb * WINDOW, (sb + 1) * WINDOW)
            start = pl.multiple_of(jnp.maximum(i - 1, 0) * WINDOW, WINDOW)
            kk, vv = k_ref[pl.ds(start, 2 * WINDOW), :], v_ref[pl.ds(start, 2 * WINDOW), :]
            q2 = q_ref[rows, :]
            valid = _swa_valid(i, start)[:WINDOW]
            res = []
            for hh in (0, 1):
                hm = _head_mask(hh)
                sink = jnp.max(jnp.where(hm, skv, NEG), axis=1, keepdims=True)
                s = jnp.where(valid, _dot(jnp.where(hm, q2, jnp.zeros_like(q2)), kk, NT), NEG)
                m = jnp.maximum(jnp.max(s, axis=1, keepdims=True), sink)
                p = jnp.exp(s - m)
                l = jnp.sum(p, axis=1, keepdims=True) + jnp.exp(sink - m)
                res.append((_dot(p.astype(BF16), vv) / l, m + jnp.log(l)))
            o = jnp.where(first, res[0][0], res[1][0])
            o_ref[rows, :] = o
            lse_ref[rows, :] = jnp.where(first, res[0][1], res[1][1])
            g = gate_ref[rows, :]
            y_ref[rows, :] = (o * (g * _sigmoid(g))).astype(BF16)

    blk = pl.BlockSpec((SWQ * WINDOW, LANES), lambda p, i: (i, p))
    gate = pl.BlockSpec((SWQ * WINDOW, LANES), lambda p, i: (i, gate_col + p))
    full = pl.BlockSpec((S, LANES), lambda p, i: (0, p // 2))
    return pl.pallas_call(
        body, name="swa_fwd", grid=(NH // 2, S // (SWQ * WINDOW)),
        in_specs=[blk, full, full, pl.BlockSpec((1, LANES), lambda p, i: (0, p)), gate],
        out_specs=[blk, blk, blk],
        out_shape=[jax.ShapeDtypeStruct((S, D), F32)] * 2 + [jax.ShapeDtypeStruct((S, D), BF16)],
        compiler_params=_params(2))(q, kdup, vdup, sinks_t, proj)


def _swa_bwd(q, kdup, vdup, sinks_t, o, lse, dy, proj, gate_col):
    def body(q_ref, k_ref, v_ref, sk_ref, o_ref, lse_ref, dy_ref, gate_ref, dq_ref, dk_ref, dv_ref, dsk_ref,
             dgate_ref):
        @pl.when(pl.program_id(1) == 0)
        def _():
            dk_ref[...] = jnp.zeros_like(dk_ref)
            dv_ref[...] = jnp.zeros_like(dv_ref)
            dsk_ref[...] = jnp.zeros_like(dsk_ref)

        skv = sk_ref[...]
        first = _head_mask(0)
        sink = _per_head(*[jnp.max(jnp.where(_head_mask(hh), skv, NEG), axis=1, keepdims=True) for hh in (0, 1)])
        for sb in range(SWQ):
            i = pl.program_id(1) * SWQ + sb
            rows = slice(sb * WINDOW, (sb + 1) * WINDOW)
            start = pl.multiple_of(jnp.maximum(i - 1, 0) * WINDOW, WINDOW)
            kk, vv = k_ref[pl.ds(start, 2 * WINDOW), :], v_ref[pl.ds(start, 2 * WINDOW), :]
            do2, dgate = _gate_grads(dy_ref[rows, :], o_ref[rows, :], gate_ref[rows, :])
            dgate_ref[rows, :] = dgate.astype(BF16)
            do2b = do2.astype(BF16)
            prod, lse2 = do2b.astype(F32) * o_ref[rows, :], lse_ref[rows, :]
            qs, dos = _both_heads(q_ref[rows, :]), _both_heads(do2b)
            delta = jnp.concatenate([jnp.sum(jnp.where(_head_mask(hh), prod, 0.0), axis=1, keepdims=True)
                                     for hh in (0, 1)], axis=0)
            lse_h = jnp.concatenate([jnp.max(jnp.where(_head_mask(hh), lse2, NEG), axis=1, keepdims=True)
                                     for hh in (0, 1)], axis=0)
            p = jnp.where(_swa_valid(i, start), jnp.exp(_dot(qs, kk, NT) - lse_h), 0.0)
            dsb = (p * (_dot(dos, vv, NT) - delta)).astype(BF16)
            dk_ref[pl.ds(start, 2 * WINDOW), :] += _dot(dsb, qs, TN)
            dv_ref[pl.ds(start, 2 * WINDOW), :] += _dot(p.astype(BF16), dos, TN)
            dq_ref[rows, :] = _unstack(_dot(dsb, kk)) * SCALE
            t = jnp.exp(sink - lse_h) * delta
            dsk_ref[...] += -jnp.where(first, jnp.sum(t[:WINDOW], axis=0, keepdims=True),
                                       jnp.sum(t[WINDOW:], axis=0, keepdims=True))

    blk = pl.BlockSpec((SWQ * WINDOW, LANES), lambda p, i: (i, p))
    full = pl.BlockSpec((S, LANES), lambda p, i: (0, p // 2))
    acc = pl.BlockSpec((S, LANES), lambda p, i: (0, p))
    sk = pl.BlockSpec((1, LANES), lambda p, i: (0, p))
    gate = pl.BlockSpec((SWQ * WINDOW, LANES), lambda p, i: (i, gate_col + p))
    return pl.pallas_call(
        body, name="swa_bwd", grid=(NH // 2, S // (SWQ * WINDOW)),
        in_specs=[blk, full, full, sk, blk, blk, blk, gate],
        out_specs=[blk, acc, acc, sk, blk],
        out_shape=[jax.ShapeDtypeStruct((S, D), F32)] * 3 + [jax.ShapeDtypeStruct((1, D), F32),
                                                              jax.ShapeDtypeStruct((S, D), BF16)],
        compiler_params=_params(2))(q, kdup, vdup, sinks_t, o, lse, dy, proj)


def _adamw_math(w, g, m, v):
    m = ADAM_B1 * m + (1.0 - ADAM_B1) * g
    v = ADAM_B2 * v + (1.0 - ADAM_B2) * jnp.square(g)
    m_hat = m / (1.0 - ADAM_B1 ** ADAM_STEP)
    v_hat = v / (1.0 - ADAM_B2 ** ADAM_STEP)
    delta = -ADAM_LR * (m_hat / (jnp.sqrt(v_hat) + ADAM_EPS) + ADAM_WD * w)
    return delta, m, v


def _adamw_small(ws, gs, ms, vs):
    k = len(ws)

    def body(*refs):
        for p in range(k):
            w_ref, g_ref, m_ref, v_ref = (refs[q * k + p] for q in range(4))
            d, mo, vo = _adamw_math(w_ref[...], g_ref[...], m_ref[...], v_ref[...])
            refs[4 * k + p][...], refs[5 * k + p][...], refs[6 * k + p][...] = d, mo, vo

    res = pl.pallas_call(
        body, name="adamw_small",
        out_shape=[jax.ShapeDtypeStruct(t.shape, F32) for t in ws] * 3)(*ws, *gs, *ms, *vs)
    return res[:k], res[k:2 * k], res[2 * k:]


SUM_TILE = 128


FLAT_BLOCK = 257 * 1024


def _tiles(shape, axis, lead=0):
    if len(shape) == 1:
        count = shape[0] // FLAT_BLOCK
        return (FLAT_BLOCK,), count, lambda pos, *lead_idx: (sum(k * count for k in lead_idx) + pos,)
    r, c = shape
    blk = (SUM_TILE, c) if axis == 0 else (r, SUM_TILE)
    count = shape[axis] // SUM_TILE

    def index(pos, *lead_idx):
        return tuple(lead_idx) + ((pos, 0) if axis == 0 else (0, pos))

    return (None,) * lead + blk, count, index


def _adamw_halves(w, g_mine, g_theirs, m, v, axis, name):
    blk, count, index = _tiles(w.shape, axis)
    per_half = count // 2

    def body(w_ref, a_ref, b_ref, m_ref, v_ref, g_ref, d_ref, mo_ref, vo_ref):
        is_mine = pl.program_id(0) // per_half == lax.axis_index("c")
        g = jnp.where(is_mine, a_ref[...], b_ref[...])
        g_ref[...] = g
        d_ref[...], mo_ref[...], vo_ref[...] = _adamw_math(w_ref[...], g, m_ref[...], v_ref[...])

    spec = pl.BlockSpec(blk, lambda i: index(i))
    half = pl.BlockSpec(blk, lambda i: index(i % per_half))
    return pl.pallas_call(
        body, name=name, grid=(count,), in_specs=[spec, half, half, spec, spec], out_specs=[spec] * 4,
        out_shape=[jax.ShapeDtypeStruct(w.shape, F32)] * 4, compiler_params=_params(1))(w, g_mine, g_theirs, m, v)


def _chip_sum(blocks, from_sibling, axis, name):
    flat = blocks.ndim == 1
    blk, count, index = _tiles((from_sibling.shape[0] // NCHIP,) if flat else from_sibling.shape[1:], axis, lead=1)

    def body(lo_ref, hi_ref, p_ref, o32, o16):
        mine = jnp.where(lax.axis_index("c") == 0, lo_ref[...], hi_ref[...])
        acc = mine + p_ref[...]
        o32[...] = acc
        o16[...] = acc.astype(BF16)

    half = pl.BlockSpec(blk, lambda k, i: index(i, k))
    if flat:
        lo = pl.BlockSpec(blk, lambda k, i: (2 * count * k + i,))
        hi = pl.BlockSpec(blk, lambda k, i: (2 * count * k + count + i,))
    else:
        lo, hi = half, pl.BlockSpec(blk, lambda k, i: index(i + count, k))
    return pl.pallas_call(
        body, name=name, grid=(NCHIP, count), in_specs=[lo, hi, half], out_specs=[half, half],
        out_shape=[jax.ShapeDtypeStruct(from_sibling.shape, F32), jax.ShapeDtypeStruct(from_sibling.shape, BF16)],
        compiler_params=_params(2))(blocks, blocks, from_sibling)


def _mesh_sum(own, parts, axis, name):
    blk, count, index = _tiles(own.shape, axis)
    n = NCHIP - 1

    def body(a_ref, *refs):
        acc = a_ref[...]
        for k in range(n):
            acc = acc + refs[k][...].astype(F32)
        refs[n][...] = acc

    spec = pl.BlockSpec(blk, lambda i: index(i))
    if own.ndim == 1:
        part = [pl.BlockSpec(blk, lambda i, k=k: (k * count + i,)) for k in range(n)]
    else:
        part = [pl.BlockSpec((None,) + blk, lambda i, k=k: (k,) + index(i)) for k in range(n)]
    return pl.pallas_call(
        body, name=name, grid=(count,), in_specs=[spec] + part,
        out_specs=spec, out_shape=jax.ShapeDtypeStruct(own.shape, F32),
        compiler_params=_params(1))(own, *([parts] * n))


def _sum_stack(parts, name):
    n = parts.shape[0]

    def body(p_ref, o_ref):
        acc = p_ref[0]
        for k in range(1, n):
            acc = acc + p_ref[k]
        o_ref[...] = acc

    return pl.pallas_call(body, name=name, out_shape=jax.ShapeDtypeStruct(parts.shape[1:], F32))(parts)


def _coords():
    return lax.axis_index("x"), lax.axis_index("y"), lax.axis_index("c")


def _chip(who):
    return 2 * who[0] + who[1]


def _flip(who, mask):
    return tuple((1 - v) if b else v for v, b in zip(who, mask))


def _transfer(transfers, t, I, O, ssem, rsem, receiving):
    tr, me = transfers[t], _coords()
    peer = _flip(me, tr["mask"])
    return pltpu.make_async_remote_copy(
        src_ref=tr["src"](I, O, me), dst_ref=tr["dst"](I, O, peer if receiving else me),
        send_sem=ssem.at[t], recv_sem=rsem.at[t], device_id=peer, device_id_type=MESH)


def _start_transfers(transfers, I, O, ssem, rsem, onward):
    arrived = set()
    for t, tr in enumerate(transfers):
        after = tr.get("after")
        if (after is not None) != onward:
            continue
        if after is not None and after not in arrived:
            _transfer(transfers, after, I, O, ssem, rsem, True).wait_recv()
            arrived.add(after)
        _transfer(transfers, t, I, O, ssem, rsem, False).start()


def _finish_transfers(transfers, I, O, ssem, rsem):
    passed_on = {tr["after"] for tr in transfers if tr.get("after") is not None}
    for t in range(len(transfers)):
        if t not in passed_on:
            _transfer(transfers, t, I, O, ssem, rsem, True).wait_recv()
    for t in range(len(transfers)):
        _transfer(transfers, t, I, O, ssem, rsem, False).wait_send()


def _own_copies(own, I, O, stage, lsem, leg):
    for n, (src, dst) in enumerate(own):
        me = _coords()
        bring =pltpu.make_async_copy(src(I, O, me), stage[n], lsem.at[2 * n])
        put = pltpu.make_async_copy(stage[n], dst(I, O, me), lsem.at[2 * n + 1])
        if leg == 0:
            bring.start()
        elif leg == 1:
            bring.wait()
            put.start()
        else:
            put.wait()


def _own_scratch(own, ins):
    return [pltpu.VMEM(ins[n].shape, ins[n].dtype) for n in range(len(own))], pltpu.SemaphoreType.DMA((max(2 * len(own), 1),))


def _exchange(name, ins, outs, transfers, own=()):
    ni, no = len(ins), len(outs)
    nt = len(transfers)
    stages, stage_sems = _own_scratch(own, ins)

    def body(*refs):
        I, O = refs[:ni], refs[ni:ni + no]
        ssem, rsem, lsem = refs[ni + no:ni + no + 3]
        stage = refs[ni + no + 3:]
        _own_copies(own, I, O, stage, lsem, 0)
        _start_transfers(transfers, I, O, ssem, rsem, False)
        _own_copies(own, I, O, stage, lsem, 1)
        _start_transfers(transfers, I, O, ssem, rsem, True)
        _finish_transfers(transfers, I, O, ssem, rsem)
        _own_copies(own, I, O, stage, lsem, 2)

    hbm = pl.BlockSpec(memory_space=pltpu.HBM)
    return pl.pallas_call(
        body, name=name, in_specs=[hbm] * ni, out_specs=[hbm] * no,
        out_shape=[jax.ShapeDtypeStruct(s, d) for s, d in outs],
        scratch_shapes=[pltpu.SemaphoreType.DMA((nt,)), pltpu.SemaphoreType.DMA((nt,)), stage_sems] + stages,
        compiler_params=pltpu.CompilerParams(has_side_effects=True, vmem_limit_bytes=VMEM_LIMIT))(*ins)


CHIP_MASKS = [(0, 1, 0), (1, 0, 0), (1, 1, 0)]
SIBLING = (0, 0, 1)


def _half(shape2d, axis, which):
    n = shape2d[axis] // 2
    cut = pl.ds(pl.multiple_of(which * n, n), n)
    return (cut, slice(None)) if axis == 0 else (slice(None), cut)


class _Riding:
    def __init__(self, transfers, ins, outs, own=()):
        self.transfers, self.ins, self.outs, self.own = transfers, list(ins), list(outs), list(own)
        hbm = pl.BlockSpec(memory_space=pltpu.HBM)
        self.in_specs, self.out_specs = [hbm] * len(self.ins), [hbm] * len(self.outs)
        self.out_shape = [jax.ShapeDtypeStruct(s, d) for s, d in self.outs]
        stages, stage_sems = _own_scratch(self.own, self.ins)
        self.scratch = [pltpu.SemaphoreType.DMA((max(len(transfers), 1),))] * 2 + [stage_sems] + stages

    def alone(self, name):
        return _exchange(name, self.ins, self.outs, self.transfers, self.own)

    def hooks(self, I, O, ssem, rsem, lsem, *stage, first, middle, last):
        tr, own = self.transfers, self.own

        @pl.when(first)
        def _():
            _own_copies(own, I, O, stage, lsem, 0)
            _start_transfers(tr, I, O, ssem, rsem, False)

        if own or any(t.get("after") is not None for t in tr):
            @pl.when(middle)
            def _():
                _own_copies(own, I, O, stage, lsem, 1)
                _start_transfers(tr, I, O, ssem, rsem, True)

        def at_end():
            @pl.when(last)
            def _():
                _finish_transfers(tr, I, O, ssem, rsem)
                _own_copies(own, I, O, stage, lsem, 2)

        return at_end


def _stretch(n, pos):
    return (pl.ds(pos * n if isinstance(pos, int) else pl.multiple_of(pos * n, n), n),)


def _gather_plan(shards, axes):
    def half(a, who):
        if shards[a].ndim == 1:
            return _stretch(shards[a].shape[0] // 2, who[2])
        return _half(shards[a].shape, axes[a], who[2])

    def landed(a, chip, who):
        if shards[a].ndim == 1:
            return _stretch(shards[a].shape[0] // 2, 2 * chip + who[2])
        return (chip,) + half(a, who)

    over_ici, onward = [], []
    for a in range(len(shards)):
        for mask in CHIP_MASKS:
            over_ici.append(dict(
                mask=mask,
                src=lambda I, O, me, a=a: I[a].at[half(a, me)],
                dst=lambda I, O, who, a=a: O[a].at[landed(a, _chip(who), who)]))
            onward.append(dict(
                mask=SIBLING, after=len(over_ici) - 1,
                src=lambda I, O, me, a=a, mask=mask: O[a].at[landed(a, _chip(_flip(me, mask)), me)],
                dst=lambda I, O, who, a=a, mask=mask: O[a].at[landed(a, _chip(_flip(who, mask)), who)]))
    outs = [((NCHIP * s.shape[0],) if s.ndim == 1 else (NCHIP,) + s.shape, s.dtype) for s in shards]

    def whole(a, chip):
        return _stretch(shards[a].shape[0], chip) if shards[a].ndim == 1 else (chip,)

    own = [(lambda I, O, me, a=a: I[a], lambda I, O, me, a=a: O[a].at[whole(a, _chip(me))])
           for a in range(len(shards))]
    return over_ici + onward, outs, own


def _gather_shards(shards, axes):
    transfers, outs, own = _gather_plan(shards, axes)
    return _exchange("gather_weights", shards, outs, transfers, own)


def _to_sibling(arrs, name):
    transfers = [dict(mask=SIBLING, src=lambda I, O, me, a=a: I[a], dst=lambda I, O, who, a=a: O[a])
                 for a in range(len(arrs))]
    return _exchange(name, arrs, [(t.shape, t.dtype) for t in arrs], transfers)


def _halves_plan(blocks, axes):
    def cut(a, which):
        return (slice(None),) + _half(blocks[a].shape[1:], axes[a], which)

    transfers, outs = [], []
    for a, (b, ax) in enumerate(zip(blocks, axes)):
        if b.ndim == 1:
            h = b.shape[0] // NCHIP // 2
            for k in range(NCHIP):
                transfers.append(dict(mask=SIBLING,
                                      src=lambda I, O, me, a=a, k=k, h=h: I[a].at[_stretch(h, 2 * k + 1 - me[2])],
                                      dst=lambda I, O, who, a=a, k=k, h=h: O[a].at[_stretch(h, k)]))
            outs.append(((NCHIP * h,), b.dtype))
        else:
            transfers.append(dict(mask=SIBLING, src=lambda I, O, me, a=a: I[a].at[cut(a, 1 - me[2])],
                                  dst=lambda I, O, who, a=a: O[a]))
            shape = list(b.shape)
            shape[ax + 1] //= 2
            outs.append((tuple(shape), b.dtype))
    return transfers, outs


def _scatter_plan(tb):
    def slot(a, k):
        return (k,) if tb[a].ndim == 3 else _stretch(tb[a].shape[0] // NCHIP, k)

    transfers = []
    for a in range(len(tb)):
        for n, mask in enumerate(CHIP_MASKS):
            transfers.append(dict(
                mask=mask,
                src=lambda I, O, me, a=a, mask=mask: I[a].at[slot(a, _chip(_flip(me, mask)))],
                dst=lambda I, O, who, a=a, n=n: O[a].at[slot(a, n)]))
    outs = [((3,) + t.shape[1:] if t.ndim == 3 else (3 * (t.shape[0] // NCHIP),), t.dtype) for t in tb]
    return transfers, outs


def _gather_small(vec):
    def slot(who):
        return 4 * who[0] + 2 * who[1] + who[2]

    masks = [(m >> 2 & 1, m >> 1 & 1, m & 1) for m in range(1, 8)]
    transfers = [dict(mask=mask, src=lambda I, O, me: I[0], dst=lambda I, O, who: O[0].at[slot(who)])
                 for mask in masks]
    own = [(lambda I, O, me: I[0], lambda I, O, me: O[0].at[slot(me)])]
    return _exchange("gather_small", [vec], [((8,) + vec.shape, vec.dtype)], transfers, own)[0]


def _rope_tables(positions):
    half = ROT // 2
    inv_freq = jnp.power(jnp.float32(THETA), -jnp.arange(0, ROT, 2, dtype=F32) / ROT)
    ang = positions.astype(F32)[:, None] * inv_freq[None, :]
    cos, sin = jnp.cos(ang), jnp.sin(ang)
    one, zero, z8 = jnp.ones((S, HD - ROT), F32), jnp.zeros((S, HD - ROT), F32), jnp.zeros((S, half), F32)
    c = jnp.concatenate([cos, cos, one], axis=1)
    a = jnp.concatenate([-sin, z8, zero], axis=1)
    b = jnp.concatenate([z8, sin, zero], axis=1)
    return tuple(jnp.tile(t, (1, 2)) for t in (c, a, b))


def _tile_heads(g, w):
    return jnp.tile(g.reshape(1, HD), (1, w // HD))


def _fold_heads(dg):
    return dg.reshape(-1, HD).sum(axis=0)


def _pad_lanes(a):
    return jnp.pad(a, ((0, 0), (0, LANES - a.shape[1])))


def _local_step(x, target, positions, wt, fetch, late_weights, begin_reduce):
    rope = _rope_tables(positions)
    w1t = wt["w_in_a_t"]
    f_row = 3 * D // LANES
    wg_t = w1t[3 * D + NH:]
    in_b_block = lambda c: pl.BlockSpec((None, TN_, TN_), lambda j, i: (c, j, 0))
    b_pad = _pad_lanes(wt["b_forget"].reshape(1, NH))
    qg_a, kg_a = _tile_heads(wt["qnorm_a_g"], D), _tile_heads(wt["knorm_a_g"], D)
    qg_b, kg_b = _tile_heads(wt["qnorm_b_g"], D), _tile_heads(wt["knorm_b_g"], KVW)
    norm_a, kv_g, norm_b = wt["norm_a_g"].reshape(1, D), wt["kv_norm_g"].reshape(1, D), wt["norm_b_g"].reshape(1, D)
    sinks_t = jnp.repeat(wt["sinks"].reshape(1, NH), HD, axis=1)

    (u_a,) = _rmsnorm_fwd(x, [norm_a], "norm_a")
    qkv = _mm("proj_a", S, 3 * D, [(u_a, _a_rows(D), w1t, _b_rows(D), NT)])
    fpad = _mm("proj_f", S, LANES, [(u_a, _a_rows(D), w1t, _b_rows(D, row0=f_row, tn=LANES), NT)], tn=LANES)
    gate_a = _mm("proj_gate_a", S, D, [(u_a, _a_rows(D), wg_t, _b_rows(D), NT)])
    q_a, k_a, v_a = _a_post(qkv, qg_a, kg_a)
    ct = _forget_cumsum(fpad, b_pad)
    ct2 = ct[:NH].reshape(NH // 2, 2, S)
    o_a, lse_a, y_a, fetched = _fox_fwd(q_a, k_a, v_a, ct2, gate_a, fetch)
    wt = {**wt, **late_weights(fetched)}
    w_in_b = wt["w_in_b"]
    h1 = _mm("out_a", S, D, [(y_a, _a_rows(D), wt["w_out_a"], _b_cols(D), None)], add=x)
    u_kv, u_b = _rmsnorm_fwd(h1, [kv_g, norm_b], "norm_b")
    kv = _mm("proj_kv", S, 2 * KVW, [(u_kv, _a_rows(D), wt["w_kv"], _b_cols(D), None)])
    pb = _mm("proj_b", S, 2 * D,
             [(u_b, _a_rows(D), w_in_b, pl.BlockSpec((None, D, TN_), lambda j, i: (j, 0, 0)), None)])
    q_b, kdup, vdup = _b_post(pb, kv, qg_b, kg_b, rope)
    gate_b_col = D // LANES
    o_b, lse_b, y_b = _swa_fwd(q_b, kdup, vdup, sinks_t, pb, gate_b_col)
    out = _mm("out_b", S, D, [(y_b, _a_rows(D), wt["w_out_b"], _b_cols(D), None)], add=h1)
    d_out, d_out_b, sq = _loss_head(out, target)

    g = {}
    g["w_out_b"] = _mm("dw_out_b", D, D, [(y_b, _a_cols(S), d_out_b, _b_cols(S), TN)])
    d_y_b = _mm("dy_b", S, D, [(d_out_b, _a_rows(D), wt["w_out_b"], _b_rows(D), NT)])
    dq_b, dkdup, dvdup, dsk, d_gate_b = _swa_bwd(q_b, kdup, vdup, sinks_t, o_b, lse_b, d_y_b, pb, gate_b_col)
    g["sinks"] = dsk[0, ::HD]
    d_qb_raw, dg = _headnorm_bwd(pb, 0, qg_b, dq_b, rope, "qnorm_b_bwd")
    g["qnorm_b_g"] = _fold_heads(dg)
    d_pb = [d_qb_raw, d_qb_raw, d_gate_b, d_gate_b]
    g["w_in_b"] = jnp.concatenate([
        _mm("dw_in_b_q", D, D, [(u_b, _a_cols(S), d_qb_raw, _b_cols(S), TN)], stacked=True),
        _mm("dw_in_b_gate", D, D, [(u_b, _a_cols(S), d_gate_b, _b_cols(S), TN)], stacked=True)], axis=0)
    d_u_b = _mm("du_b", S, D, [(d_pb[c], _a_rows(TN_, col=c % 2), w_in_b, in_b_block(c), NT) for c in range(NCHIP)])
    d_kv, dg = _kv_bwd(dkdup, dvdup, kv, kg_b, rope)
    g["knorm_b_g"] = _fold_heads(dg)
    g["w_kv"] = _mm("dw_kv", D, 2 * KVW, [(u_kv, _a_cols(S), d_kv, _b_cols(S), TN)])
    d_u_kv = _mm("du_kv", S, D, [(d_kv, _a_rows(2 * KVW), wt["w_kv"], _b_rows(2 * KVW), NT)])
    d_h1, d_h1_b, g["kv_norm_g"], g["norm_b_g"] = _rmsnorm_bwd(h1, [kv_g, norm_b], [d_u_kv, d_u_b], d_out, "norm_b_bwd")
    g["w_out_a"] = _mm("dw_out_a", D, D, [(y_a, _a_cols(S), d_h1_b, _b_cols(S), TN)])
    late = {n: g[n] for n in LATE}
    d_y_a, halves = _mm("dy_a", S, D, [(d_h1_b, _a_rows(D), wt["w_out_a"], _b_rows(D), NT)],
                        riding=begin_reduce(late))
    riding, so_far = begin_reduce(late, halves)
    dq_a, dk_a, dv_a, dct, d_gate_a, arrived = _fox_bwd(q_a, k_a, v_a, ct2, o_a, lse_a, d_y_a, gate_a, riding)
    dct_pad = jnp.pad(dct.reshape(NH, S), ((0, LANES - NH), (0, 0)))
    d_f, db = _forget_bwd(dct_pad, fpad, b_pad)
    g["b_forget"] = db[0, :NH]
    d_q_raw, dg = _headnorm_bwd(qkv, 0, qg_a, dq_a, None, "qnorm_a_bwd")
    g["qnorm_a_g"] = _fold_heads(dg)
    d_k_raw, dg = _headnorm_bwd(qkv, 1, kg_a, dk_a, None, "knorm_a_bwd")
    g["knorm_a_g"] = _fold_heads(dg)
    rows, gw = 4 * D + NH, None
    for n, t, row0 in (("q", d_q_raw, 0), ("k", d_k_raw, D), ("v", dv_a, 2 * D)):
        gw = _mm("dw_in_a_" + n, D, D, [(t, _a_cols(S), u_a, _b_cols(S), TN)], rows_of=(gw, rows, row0))
    gw = _mm("dw_in_a_f", LANES, D, [(d_f, _a_cols(S, tm=LANES), u_a, _b_cols(S), TN)], tm=LANES,
             rows_of=(gw, rows, 3 * D))
    g["w_in_a"] = _mm("dw_in_a_gate", D, D, [(d_gate_a, _a_cols(S), u_a, _b_cols(S), TN)],
                      rows_of=(gw, rows, 3 * D + NH))
    first = {"w_in_a": g["w_in_a"]}
    riding, so_far_first = begin_reduce(first, begin_reduce(first).alone("sibling_halves_w_in_a"))
    d_u_a, arrived_first = _mm("du_a", S, D, [
        (d_q_raw, _a_rows(D), w1t, _b_cols(D, row=0), None), (d_k_raw, _a_rows(D), w1t, _b_cols(D, row=1), None),
        (dv_a, _a_rows(D), w1t, _b_cols(D, row=2), None), (d_gate_a, _a_rows(D), wg_t, _b_cols(D), None),
        (d_f, _a_rows(LANES), w1t, _b_cols(LANES, row=f_row), None)], riding=riding)
    d_x, _, g["norm_a_g"] = _rmsnorm_bwd(x, [norm_a], [d_u_a], d_h1, "norm_a_bwd")
    return sq, d_x, g, (list(so_far_first) + list(so_far), list(arrived_first) + list(arrived))


BIG = ["w_in_a", "w_out_a", "w_kv", "w_in_b", "w_out_b"]
LATE = BIG[1:]
SPLIT = {"w_in_a": None, "w_out_a": 0, "w_kv": 0, "w_in_b": 0, "w_out_b": 0}
SMALL = ["norm_a_g", "b_forget", "qnorm_a_g", "knorm_a_g", "kv_norm_g", "knorm_b_g", "norm_b_g", "qnorm_b_g", "sinks"]
NAMES = ["norm_a_g", "w_in_a", "b_forget", "qnorm_a_g", "knorm_a_g", "w_out_a", "kv_norm_g", "w_kv", "knorm_b_g",
         "norm_b_g", "w_in_b", "qnorm_b_g", "sinks", "w_out_b"]


def _pack(vals):
    flat = []
    for v in vals:
        v = v.reshape(-1)
        flat.append(jnp.pad(v, (0, -v.shape[0] % LANES)))
    flat = jnp.concatenate(flat)
    flat = jnp.pad(flat, (0, -flat.shape[0] % (8 * LANES)))
    return flat.reshape(-1, LANES)


def _unpack(packed, shapes):
    flat, out, off = packed.reshape(-1), [], 0
    for s in shapes:
        n = int(np.prod(s))
        out.append(flat[off:off + n].reshape(s))
        off += n + (-n % LANES)
    return out


def kernel(x, positions, norm_a_g, w_in_a, b_forget, qnorm_a_g, knorm_a_g, w_out_a, kv_norm_g, w_kv, knorm_b_g, norm_b_g, w_in_b, qnorm_b_g, sinks, w_out_b, loss_target, m_norm_a_g, m_w_in_a, m_b_forget, m_qnorm_a_g, m_knorm_a_g, m_w_out_a, m_kv_norm_g, m_w_kv, m_knorm_b_g, m_norm_b_g, m_w_in_b, m_qnorm_b_g, m_sinks, m_w_out_b, v_norm_a_g, v_w_in_a, v_b_forget, v_qnorm_a_g, v_knorm_a_g, v_w_out_a, v_kv_norm_g, v_w_kv, v_knorm_b_g, v_norm_b_g, v_w_in_b, v_qnorm_b_g, v_sinks, v_w_out_b):
    w = dict(norm_a_g=norm_a_g, w_in_a=w_in_a, b_forget=b_forget, qnorm_a_g=qnorm_a_g, knorm_a_g=knorm_a_g,
             w_out_a=w_out_a, kv_norm_g=kv_norm_g, w_kv=w_kv, knorm_b_g=knorm_b_g, norm_b_g=norm_b_g,
             w_in_b=w_in_b, qnorm_b_g=qnorm_b_g, sinks=sinks, w_out_b=w_out_b)
    m = dict(norm_a_g=m_norm_a_g, w_in_a=m_w_in_a, b_forget=m_b_forget, qnorm_a_g=m_qnorm_a_g, knorm_a_g=m_knorm_a_g,
             w_out_a=m_w_out_a, kv_norm_g=m_kv_norm_g, w_kv=m_w_kv, knorm_b_g=m_knorm_b_g, norm_b_g=m_norm_b_g,
             w_in_b=m_w_in_b, qnorm_b_g=m_qnorm_b_g, sinks=m_sinks, w_out_b=m_w_out_b)
    v = dict(norm_a_g=v_norm_a_g, w_in_a=v_w_in_a, b_forget=v_b_forget, qnorm_a_g=v_qnorm_a_g, knorm_a_g=v_knorm_a_g,
             w_out_a=v_w_out_a, kv_norm_g=v_kv_norm_g, w_kv=v_w_kv, knorm_b_g=v_knorm_b_g, norm_b_g=v_norm_b_g,
             w_in_b=v_w_in_b, qnorm_b_g=v_qnorm_b_g, sinks=v_sinks, w_out_b=v_w_out_b)
    my_chip = 2 * lax.axis_index("x") + lax.axis_index("y")

    def shard2d(t, n):
        if n == "w_in_a":
            return jnp.transpose(t, (2, 0, 1)).reshape(-1)
        return t.reshape(t.shape[-2:])

    def unflat(t, n):
        return jnp.transpose(t.reshape(-1, 1, D), (1, 2, 0)) if n == "w_in_a" else t.reshape(w[n].shape)

    w2d = {n: shard2d(w[n], n) for n in BIG}

    norm_a_rows = jnp.broadcast_to(norm_a_g.reshape(1, D // NCHIP), (2 * SUBLANES, D // NCHIP))
    w1t, norm_rows = _gather_shards([w2d["w_in_a"].astype(BF16), norm_a_rows], [SPLIT["w_in_a"], 0])
    wt = {"w_in_a_t": w1t.reshape(-1, D), "norm_a_g": norm_rows[:, 0, :].reshape(1, D)}
    for n in SMALL[1:]:
        wt[n] = w[n]
    late_shards = [w2d[n].astype(BF16) for n in LATE]
    late_axes = [SPLIT[n] for n in LATE]
    transfers, outs, own = _gather_plan(late_shards, late_axes)
    fetch = _Riding(transfers, late_shards, outs, own)

    def late_weights(fetched):
        return {n: t if n == "w_in_b" else t.reshape(-1, t.shape[2]) for n, t in zip(LATE, fetched)}

    def as_blocks(t):
        if t.ndim == 3:
            return t
        return t.reshape(-1) if t.shape[0] % (SUBLANES * NCHIP) else t.reshape(NCHIP, -1, t.shape[1])

    def begin_reduce(grads, halves=None):
        names = list(grads)
        axes = [SPLIT[n] for n in names]
        blocks = [as_blocks(grads[n]) for n in names]
        if halves is None:
            transfers, outs = _halves_plan(blocks, axes)
            return _Riding(transfers, blocks, outs)
        sums = [_chip_sum(blk, part, ax, "chip_sum_" + n) for n, ax, blk, part in zip(names, axes, blocks, halves)]
        bf16 = [s[1] for s in sums]
        transfers, outs = _scatter_plan(bf16)
        return _Riding(transfers, bf16, outs), [s[0] for s in sums]

    sq, d_x, g, (chip_f32, arrived) = _local_step(x[0], loss_target[0], positions, wt, fetch, late_weights,
                                                  begin_reduce)

    small_shapes = [(D,), (NH,), (HD,), (HD,), (D,), (HD,), (D,), (HD,), (NH,), (D,)]
    packed = _pack([g[n] for n in SMALL] + [sq])
    total = _sum_stack(_gather_small(packed), "sum_small")
    small_g = dict(zip(SMALL, _unpack(total, small_shapes)[:-1]))
    loss = 0.5 * jnp.sum(_unpack(total, small_shapes)[-1]) / D
    small_g["norm_a_g"] = lax.dynamic_slice(small_g["norm_a_g"], (my_chip * (D // NCHIP),), (D // NCHIP,))

    axes = [SPLIT[n] for n in BIG]
    halves = []
    for n, ax, t32, parts in zip(BIG, axes, chip_f32, arrived):
        if t32.ndim == 1:
            own = lax.dynamic_slice_in_dim(t32, my_chip * (t32.shape[0] // NCHIP), t32.shape[0] // NCHIP)
        else:
            own = lax.dynamic_index_in_dim(t32, my_chip, axis=0, keepdims=False)
        halves.append(_mesh_sum(own, parts, ax, "mesh_sum_" + n))
    sibling_done = _to_sibling(halves, "finished_halves")

    res = {}
    for n, ax, mine_half, their_half in zip(BIG, axes, halves, sibling_done):
        out4 = _adamw_halves(w2d[n], mine_half, their_half, shard2d(m[n], n), shard2d(v[n], n), ax, "adamw_" + n)
        res[n] = tuple(unflat(t, n) for t in out4)
    row = lambda t: t.reshape(1, -1)
    small_out = _adamw_small(*[[row(d[n]) for n in SMALL] for d in (w, small_g, m, v)])
    for i, n in enumerate(SMALL):
        res[n] = tuple(t.reshape(w[n].shape) for t in (small_g[n],) + tuple(out[i] for out in small_out))

    outs = [loss, d_x[None]]
    for k in range(4):
        outs += [res[n][k] for n in NAMES]
    return tuple(outs)
```

```python
import numpy as np
import jax
import jax.numpy as jnp
from jax import lax
from jax.experimental import pallas as pl
from jax.experimental.pallas import tpu as pltpu

F32, BF16 = jnp.float32, jnp.bfloat16
S, D, HD, NH, NKV = 2048, 1024, 64, 16, 4
KVW = NKV * HD
WINDOW = 128
ROT = HD // 4
THETA = 500000.0
EPS = 1e-6
SCALE = HD ** -0.5
LANES = 128
SUBLANES = 8
NEG = -1e30
VMEM_LIMIT = 48 * 2 ** 20
ROWS = 512
ATT = 512
SWQ = 16
NCHIP = 4
ADAM_LR, ADAM_B1, ADAM_B2, ADAM_EPS, ADAM_WD, ADAM_STEP = 0.001, 0.9, 0.999, 1e-08, 0.01, 10
NT = (((1,), (1,)), ((), ()))
TN = (((0,), (0,)), ((), ()))
MESH = pl.DeviceIdType.MESH


def _params(n):
    return pltpu.CompilerParams(dimension_semantics=("arbitrary",) * n, vmem_limit_bytes=VMEM_LIMIT)


def _dot(a, b, dims=None):
    if dims is None:
        return jnp.dot(a, b, preferred_element_type=F32)
    return lax.dot_general(a, b, dims, preferred_element_type=F32)


def _dot_split(a, b, n):
    out, rest = None, a
    for _ in range(n):
        hi = rest.astype(BF16)
        term = _dot(hi, b)
        out = term if out is None else out + term
        rest = rest - hi.astype(F32)
    return out


def _seg_mat(w):
    e = (np.arange(w)[:, None] // HD == np.arange(LANES)[None, :]).astype(np.float32)
    return jnp.asarray(e, BF16)


def _spread(r, w):
    head = lax.broadcasted_iota(jnp.int32, (2 * LANES, w), 1) >> (HD.bit_length() - 1)
    row = lax.broadcasted_iota(jnp.int32, (2 * LANES, w), 0)
    et2 = jnp.where(head == (row & (LANES - 1)), 1.0, 0.0).astype(BF16)
    hi = r.astype(BF16)
    lo = (r - hi.astype(F32)).astype(BF16)
    return _dot(jnp.concatenate([hi, lo], axis=1), et2)


def _head_rstd(x, e):
    ss = _dot_split(x * x, e, 2)
    return _spread(lax.rsqrt(ss * (1.0 / HD) + EPS), x.shape[1])


def _rope(x, c, a, b):
    w = x.shape[1]
    return x * c + pltpu.roll(x, w - ROT // 2, 1) * a + pltpu.roll(x, ROT // 2, 1) * b


def _rope_t(dy, c, a, b):
    w = dy.shape[1]
    return dy * c + pltpu.roll(dy * b, w - ROT // 2, 1) + pltpu.roll(dy * a, ROT // 2, 1)


def _sigmoid(x):
    return 1.0 / (1.0 + jnp.exp(-x))


def _row_spec(shape, ts):
    nd = len(shape)
    if shape[0] == S:
        return pl.BlockSpec((ts,) + tuple(shape[1:]), lambda i: (i,) + (0,) * (nd - 1))
    return pl.BlockSpec(tuple(shape), lambda i: (0,) * nd)


def _rows_call(body, name, ins, outs, ts=ROWS):
    return pl.pallas_call(
        body, name=name, grid=(S // ts,),
        in_specs=[_row_spec(a.shape, ts) for a in ins],
        out_specs=[_row_spec(s, ts) for s, _ in outs],
        out_shape=[jax.ShapeDtypeStruct(s, d) for s, d in outs],
        compiler_params=_params(1))(*ins)


def _col_spec(ts, w, col):
    return pl.BlockSpec((ts, w), lambda i: (i, col))


TM = TN_ = 512
TM_TOKENS = 1024
TN_WIDE = 1024


def _mm(name, m, n, terms, out_dtype=F32, add=None, tm=None, tn=TN_, stacked=False, riding=None, rows_of=None):
    nterm = len(terms)
    if tm is None:
        tm = TM_TOKENS if m == S else TM
    nj, ni_ = n // tn, m // tm
    n_in = 2 * nterm + (add is not None) + (rows_of is not None and rows_of[0] is not None)
    r_in, r_out = (len(riding.ins), len(riding.outs)) if riding is not None else (0, 0)

    def body(*refs):
        if riding is not None:
            j, i = pl.program_id(0), pl.program_id(1)
            at_end = riding.hooks(refs[n_in:n_in + r_in], refs[n_in + r_in + 1:n_in + r_in + 1 + r_out],
                                  *refs[n_in + r_in + 1 + r_out:], first=(j == 0) & (i == 0),
                                  middle=(j == nj // 2) & (i == 0), last=(j == nj - 1) & (i == ni_ - 1))
        acc = None
        for t in range(nterm):
            part = _dot(refs[2 * t][...], refs[2 * t + 1][...], terms[t][4])
            acc = part if acc is None else acc + part
        if add is not None:
            acc = acc + refs[2 * nterm][...]
        refs[n_in + r_in][...] = acc.astype(out_dtype)
        if riding is not None:
            at_end()

    tile = pl.BlockSpec((tm, tn), lambda j, i: (i, j))
    ins, specs = [], []
    for a, a_spec, b, b_spec, _ in terms:
        ins += [a, b]
        specs += [a_spec, b_spec]
    if add is not None:
        ins.append(add)
        specs.append(tile)
    out_spec = pl.BlockSpec((None, tm, tn), lambda j, i: (j, i, 0)) if stacked else tile
    out_shape = jax.ShapeDtypeStruct((nj, m, tn) if stacked else (m, n), out_dtype)
    if rows_of is not None:
        taller, rows, row0 = rows_of
        out_spec = pl.BlockSpec((pl.Element(tm), pl.Element(tn)), lambda j, i: (
            pl.multiple_of(row0 + i * tm, SUBLANES), pl.multiple_of(j * tn, LANES)))
        out_shape = jax.ShapeDtypeStruct((rows, n), out_dtype)
        alias = {}
        if taller is not None:
            ins.append(taller)
            specs.append(pl.BlockSpec(memory_space=pltpu.HBM))
            alias = {len(ins) - 1: 0}
        return pl.pallas_call(body, name=name, grid=(nj, ni_), in_specs=specs, out_specs=out_spec,
                              out_shape=out_shape, input_output_aliases=alias, compiler_params=_params(2))(*ins)
    if riding is None:
        return pl.pallas_call(body, name=name, grid=(nj, ni_), in_specs=specs, out_specs=out_spec,
                              out_shape=out_shape, compiler_params=_params(2))(*ins)
    res = pl.pallas_call(
        body, name=name, grid=(nj, ni_), in_specs=specs + riding.in_specs,
        out_specs=[out_spec] + riding.out_specs, out_shape=[out_shape] + riding.out_shape,
        scratch_shapes=riding.scratch, compiler_params=_params(2))(*ins, *riding.ins)
    return res[0], res[1:]


def _a_rows(k, col=0, tm=TM_TOKENS):
    return pl.BlockSpec((tm, k), lambda j, i: (i, col))


def _a_cols(k, tm=TM):
    return pl.BlockSpec((k, tm), lambda j, i: (0, i))


def _b_cols(k, row=0, col0=0, tn=TN_):
    return pl.BlockSpec((k, tn), lambda j, i: (row, col0 + j))


def _b_rows(k, row0=0, tn=TN_):
    return pl.BlockSpec((tn, k), lambda j, i: (row0 + j, 0))


def _rmsnorm_fwd(x, gains, name):
    def body(*refs):
        xv = refs[0][...]
        r = lax.rsqrt(jnp.mean(xv * xv, axis=-1, keepdims=True) + EPS)
        xh = xv * r
        for n in range(len(gains)):
            refs[1 + len(gains) + n][...] = (xh * refs[1 + n][...]).astype(BF16)

    return _rows_call(body, name, [x] + list(gains), [((S, D), BF16)] * len(gains))


def _rmsnorm_bwd(x, gains, dus, dres, name):
    n = len(gains)

    def body(*refs):
        x_ref, g_refs, du_refs, dres_ref = refs[0], refs[1:1 + n], refs[1 + n:1 + 2 * n], refs[1 + 2 * n]
        dx_ref, dxb_ref, dg_refs = refs[2 + 2 * n], refs[3 + 2 * n], refs[4 + 2 * n:]
        xv = x_ref[...]
        r = lax.rsqrt(jnp.mean(xv * xv, axis=-1, keepdims=True) + EPS)
        xh = xv * r
        gy = None
        for m in range(n):
            du = du_refs[m][...]
            part = jnp.sum(du * xh, axis=0, keepdims=True)

            @pl.when(pl.program_id(0) == 0)
            def _(m=m, part=part):
                dg_refs[m][...] = part

            @pl.when(pl.program_id(0) != 0)
            def _(m=m, part=part):
                dg_refs[m][...] += part

            t = du * g_refs[m][...]
            gy = t if gy is None else gy + t
        dx = dres_ref[...] + r * (gy - xh * jnp.mean(gy * xh, axis=-1, keepdims=True))
        dx_ref[...] = dx
        dxb_ref[...] = dx.astype(BF16)

    outs = [((S, D), F32), ((S, D), BF16)] + [((1, D), F32)] * n
    return _rows_call(body, name, [x] + list(gains) + list(dus) + [dres], outs)


def _a_post(qkvg, qg, kg):
    e = _seg_mat(D)

    def body(q_ref, k_ref, v_ref, qg_ref, kg_ref, e_ref, qo, ko, vo):
        ev = e_ref[...]
        qv, kv = q_ref[...], k_ref[...]
        qo[...] = (qv * _head_rstd(qv, ev) * qg_ref[...] * SCALE).astype(BF16)
        ko[...] = (kv * _head_rstd(kv, ev) * kg_ref[...]).astype(BF16)
        vo[...] = v_ref[...].astype(BF16)

    whole = lambda a: pl.BlockSpec(a.shape, lambda i: (0, 0))
    return pl.pallas_call(
        body, name="a_post", grid=(S // ROWS,),
        in_specs=[_col_spec(ROWS, D, 0), _col_spec(ROWS, D, 1), _col_spec(ROWS, D, 2),
                  whole(qg), whole(kg), whole(e)],
        out_specs=[_col_spec(ROWS, D, 0)] * 3,
        out_shape=[jax.ShapeDtypeStruct((S, D), BF16)] * 3,
        compiler_params=_params(1))(qkvg, qkvg, qkvg, qg, kg, e)


def _tri(upper):
    r, c = np.arange(ROWS)[:, None], np.arange(ROWS)[None, :]
    return jnp.asarray((r <= c) if upper else (r >= c), BF16)


def _forget_cumsum(fpad, bpad):
    def body(f_ref, b_ref, u_ref, c_ref, carry):
        @pl.when(pl.program_id(0) == 0)
        def _():
            carry[...] = jnp.zeros_like(carry)

        lf = jax.nn.log_sigmoid(f_ref[...] + b_ref[...])
        blk = _dot_split(lf.T, u_ref[...], 3) + carry[:, 0:1]
        c_ref[...] = blk
        carry[...] = jnp.broadcast_to(blk[:, ROWS - 1:ROWS], carry.shape)

    return pl.pallas_call(
        body, name="forget_cumsum", grid=(S // ROWS,),
        in_specs=[pl.BlockSpec((ROWS, LANES), lambda i: (i, 0)), pl.BlockSpec((1, LANES), lambda i: (0, 0)),
                  pl.BlockSpec((ROWS, ROWS), lambda i: (0, 0))],
        out_specs=pl.BlockSpec((LANES, ROWS), lambda i: (0, i)),
        out_shape=jax.ShapeDtypeStruct((LANES, S), F32),
        scratch_shapes=[pltpu.VMEM((LANES, LANES), F32)],
        compiler_params=_params(1))(fpad, bpad, _tri(True))


def _forget_bwd(dct, fpad, bpad):
    nb = S // ROWS

    def body(dc_ref, f_ref, b_ref, l_ref, df_ref, db_ref, carry):
        @pl.when(pl.program_id(0) == 0)
        def _():
            carry[...] = jnp.zeros_like(carry)
            db_ref[...] = jnp.zeros_like(db_ref)

        blk = _dot_split(dc_ref[...], l_ref[...], 3) + carry[:, 0:1]
        carry[...] = jnp.broadcast_to(blk[:, 0:1], carry.shape)
        df = blk.T * _sigmoid(-(f_ref[...] + b_ref[...]))
        df_ref[...] = df.astype(BF16)
        db_ref[...] += jnp.sum(df, axis=0, keepdims=True)

    return pl.pallas_call(
        body, name="forget_bwd", grid=(nb,),
        in_specs=[pl.BlockSpec((LANES, ROWS), lambda i: (0, nb - 1 - i)),
                  pl.BlockSpec((ROWS, LANES), lambda i: (nb - 1 - i, 0)),
                  pl.BlockSpec((1, LANES), lambda i: (0, 0)), pl.BlockSpec((ROWS, ROWS), lambda i: (0, 0))],
        out_specs=[pl.BlockSpec((ROWS, LANES), lambda i: (nb - 1 - i, 0)), pl.BlockSpec((1, LANES), lambda i: (0, 0))],
        out_shape=[jax.ShapeDtypeStruct((S, LANES), BF16), jax.ShapeDtypeStruct((1, LANES), F32)],
        scratch_shapes=[pltpu.VMEM((LANES, LANES), F32)],
        compiler_params=_params(1))(dct, fpad, bpad, _tri(False))


def _headnorm_bwd(x, col, gain, dy, rope, name):
    e = _seg_mat(D)
    tabs = list(rope) if rope is not None else []

    def body(*refs):
        x_ref, g_ref, dy_ref, e_ref = refs[:4]
        dx_ref, dg_ref = refs[-2:]
        xv, dyv, ev = x_ref[...], dy_ref[...], e_ref[...]
        if rope is not None:
            c, a, b = (jnp.tile(t[...], (1, D // LANES)) for t in refs[4:7])
            dyv = _rope_t(dyv, c, a, b)
        r = _head_rstd(xv, ev)
        xh = xv * r
        part = jnp.sum(dyv * xh, axis=0, keepdims=True)

        @pl.when(pl.program_id(0) == 0)
        def _():
            dg_ref[...] = part

        @pl.when(pl.program_id(0) != 0)
        def _():
            dg_ref[...] += part

        gy = dyv * g_ref[...]
        seg = _spread(_dot_split(gy * xh, ev, 2) * (1.0 / HD), D)
        dx_ref[...] = (r * (gy - xh * seg)).astype(BF16)

    whole = lambda a: pl.BlockSpec(a.shape, lambda i: (0, 0))
    return pl.pallas_call(
        body, name=name, grid=(S // ROWS,),
        in_specs=[_col_spec(ROWS, D, col), whole(gain), _col_spec(ROWS, D, 0), whole(e)]
                 + [pl.BlockSpec((ROWS, LANES), lambda i: (i, 0))] * len(tabs),
        out_specs=[_col_spec(ROWS, D, 0), whole(gain)],
        out_shape=[jax.ShapeDtypeStruct((S, D), BF16), jax.ShapeDtypeStruct((1, D), F32)],
        compiler_params=_params(1))(x, gain, dy, e, *tabs)


def _dup_mat():
    r, c = np.arange(KVW)[:, None], np.arange(2 * KVW)[None, :]
    return (r // HD == c // LANES) & (r % HD == c % HD)


def _fold_mat():
    r, c = np.arange(D)[:, None], np.arange(KVW)[None, :]
    return (r // (2 * LANES) == c // HD) & (r % HD == c % HD)


def _b_post(pb, kv, qg, kg, rope):
    e, ek = _seg_mat(D), _seg_mat(KVW)
    dup = jnp.asarray(_dup_mat(), BF16)

    def body(q_ref, k_ref, v_ref, qg_ref, kg_ref, e_ref, ek_ref, dup_ref, c_ref, a_ref, b_ref, qo, ko, vo):
        c1, a1, b1 = c_ref[...], a_ref[...], b_ref[...]
        qv = q_ref[...]
        qn = qv * _head_rstd(qv, e_ref[...]) * qg_ref[...]
        t = lambda z, n: jnp.tile(z, (1, n))
        qo[...] = (_rope(qn, t(c1, D // LANES), t(a1, D // LANES), t(b1, D // LANES)) * SCALE).astype(BF16)
        kvv = k_ref[...]
        kn = kvv * _head_rstd(kvv, ek_ref[...]) * kg_ref[...]
        kr = _rope(kn, t(c1, KVW // LANES), t(a1, KVW // LANES), t(b1, KVW // LANES)).astype(BF16)
        ko[...] = _dot(kr, dup_ref[...]).astype(BF16)
        vo[...] = _dot(v_ref[...].astype(BF16), dup_ref[...]).astype(BF16)

    whole = lambda a: pl.BlockSpec(a.shape, lambda i: (0, 0))
    tab = pl.BlockSpec((ROWS, LANES), lambda i: (i, 0))
    return pl.pallas_call(
        body, name="b_post", grid=(S // ROWS,),
        in_specs=[_col_spec(ROWS, D, 0), _col_spec(ROWS, KVW, 0), _col_spec(ROWS, KVW, 1),
                  whole(qg), whole(kg), whole(e), whole(ek), whole(dup), tab, tab, tab],
        out_specs=[_col_spec(ROWS, D, 0), _col_spec(ROWS, 2 * KVW, 0), _col_spec(ROWS, 2 * KVW, 0)],
        out_shape=[jax.ShapeDtypeStruct((S, D), BF16), jax.ShapeDtypeStruct((S, 2 * KVW), BF16),
                   jax.ShapeDtypeStruct((S, 2 * KVW), BF16)],
        compiler_params=_params(1))(pb, kv, kv, qg, kg, e, ek, dup, *rope)


def _kv_bwd(dkdup, dvdup, kv, kg, rope):
    ek = _seg_mat(KVW)
    fold = jnp.asarray(_fold_mat(), BF16)

    def body(dk_ref, dv_ref, k_ref, kg_ref, ek_ref, fold_ref, c_ref, a_ref, b_ref, dkv_ref, dg_ref):
        ev, fv = ek_ref[...], fold_ref[...]
        t = lambda z: jnp.tile(z[...], (1, KVW // LANES))
        dk = _rope_t(_dot_split(dk_ref[...], fv, 2), t(c_ref), t(a_ref), t(b_ref))
        dv = _dot_split(dv_ref[...], fv, 2)
        xv = k_ref[...]
        r = _head_rstd(xv, ev)
        xh = xv * r
        part = jnp.sum(dk * xh, axis=0, keepdims=True)

        @pl.when(pl.program_id(0) == 0)
        def _():
            dg_ref[...] = part

        @pl.when(pl.program_id(0) != 0)
        def _():
            dg_ref[...] += part

        gy = dk * kg_ref[...]
        seg = _spread(_dot_split(gy * xh, ev, 2) * (1.0 / HD), KVW)
        dkv_ref[:, 0:KVW] = (r * (gy - xh * seg)).astype(BF16)
        dkv_ref[:, KVW:2 * KVW] = dv.astype(BF16)

    whole = lambda a: pl.BlockSpec(a.shape, lambda i: (0, 0))
    tab = pl.BlockSpec((ROWS, LANES), lambda i: (i, 0))
    return pl.pallas_call(
        body, name="kv_bwd", grid=(S // ROWS,),
        in_specs=[_col_spec(ROWS, D, 0), _col_spec(ROWS, D, 0), _col_spec(ROWS, KVW, 0),
                  whole(kg), whole(ek), whole(fold), tab, tab, tab],
        out_specs=[_col_spec(ROWS, 2 * KVW, 0), whole(kg)],
        out_shape=[jax.ShapeDtypeStruct((S, 2 * KVW), BF16), jax.ShapeDtypeStruct((1, KVW), F32)],
        compiler_params=_params(1))(dkdup, dvdup, kv, kg, ek, fold, *rope)


def _loss_head(out, target):
    def body(o_ref, t_ref, d_ref, db_ref, l_ref):
        diff = o_ref[...] - t_ref[...]
        d = diff * (1.0 / D)
        d_ref[...] = d
        db_ref[...] = d.astype(BF16)

        @pl.when(pl.program_id(0) == 0)
        def _():
            l_ref[...] = jnp.zeros_like(l_ref)

        l_ref[...] += jnp.sum(diff * diff, axis=0, keepdims=True)

    return _rows_call(body, "loss_head", [out, target], [((S, D), F32), ((S, D), BF16), ((1, D), F32)])


def _lane():
    return lax.broadcasted_iota(jnp.int32, (1, LANES), 1)


def _head_mask(hh):
    return (_lane() < HD) if hh == 0 else (_lane() >= HD)


def _fox_fwd(q, k, v, ct, gate, riding):
    nq, npair = S // ATT, NH // 2
    ni, no = len(riding.ins), len(riding.outs)

    def body(q_ref, k_ref, v_ref, c_ref, gate_ref, *rest):
        o_ref, lse_ref, y_ref = rest[ni:ni + 3]
        pair, i = pl.program_id(0), pl.program_id(1)
        at_end = riding.hooks(rest[:ni], rest[ni + 3:ni + 3 + no], *rest[ni + 3 + no:],
                              first=(pair == 0) & (i == 0), middle=(pair == npair // 2) & (i == 0),
                              last=(pair == npair - 1) & (i == nq - 1))
        q2 = q_ref[...]
        qms = [jnp.where(_head_mask(hh), q2, jnp.zeros_like(q2)) for hh in (0, 1)]

        def probs(off, width, m, hh, diag):
            s = _dot(qms[hh], k_ref[pl.ds(off, width), :], NT) - c_ref[hh:hh + 1, pl.ds(off, width)]
            if diag:
                row = i * ATT + lax.broadcasted_iota(jnp.int32, (ATT, width), 0)
                col = off + lax.broadcasted_iota(jnp.int32, (ATT, width), 1)
                s = jnp.where(col <= row, s, NEG)
            m_new = jnp.maximum(m, jnp.max(s, axis=1, keepdims=True))
            p = jnp.exp(s - m_new)
            p_hi = p.astype(BF16)
            return m_new, jnp.exp(m - m_new), p_hi, (p - p_hi.astype(F32)).astype(BF16)

        def weighted(off, width, p_hi, p_lo, hh):
            vj = v_ref[pl.ds(off, width), :]
            v1 = jnp.where(_head_mask(hh), vj, jnp.ones_like(vj))
            return _dot(p_hi, v1) + _dot(p_lo, v1)

        def step(off, width, carry, diag):
            off = pl.multiple_of(off, ATT)
            out = []
            for hh in (0, 1):
                m, acc = carry[hh]
                m, alpha, p_hi, p_lo = probs(off, width, m, hh, diag)
                out.append((m, alpha * acc + weighted(off, width, p_hi, p_lo, hh)))
            return tuple(out)

        one = (jnp.full((ATT, 1), NEG, F32), jnp.zeros((ATT, LANES), F32))
        carry = lax.fori_loop(0, i // 2, lambda j, cr: step(j * (2 * ATT), 2 * ATT, cr, False), (one, one))
        carry = lax.cond(i % 2 == 1, lambda cr: step((i - 1) * ATT, 2 * ATT, cr, True),
                         lambda cr: step(i * ATT, ATT, cr, True), carry)
        res = []
        for hh in (0, 1):
            m, acc = carry[hh]
            l = jnp.max(jnp.where(_head_mask(1 - hh), acc, 0.0), axis=1, keepdims=True)
            res.append((acc / l, m + jnp.log(l)))
        first = _head_mask(0)
        o = jnp.where(first, res[0][0], res[1][0])
        o_ref[...] = o
        lse_ref[...] = jnp.where(first, res[0][1], res[1][1])
        g = gate_ref[...]
        y_ref[...] = (o * (g * _sigmoid(g))).astype(BF16)
        at_end()

    blk = pl.BlockSpec((ATT, LANES), lambda p, i: (i, p))
    full = pl.BlockSpec((S, LANES), lambda p, i: (0, p))
    res = pl.pallas_call(
        body, name="fox_fwd", grid=(npair, nq),
        in_specs=[blk, full, full, pl.BlockSpec((None, 2, S), lambda p, i: (p, 0, 0)), blk] + riding.in_specs,
        out_specs=[blk, blk, blk] + riding.out_specs,
        out_shape=[jax.ShapeDtypeStruct((S, D), F32)] * 2 + [jax.ShapeDtypeStruct((S, D), BF16)] + riding.out_shape,
        scratch_shapes=riding.scratch,
        compiler_params=_params(2))(q, k, v, ct, gate, *riding.ins)
    return res[0], res[1], res[2], res[3:]


def _gate_grads(dy, o, g):
    sg = _sigmoid(g)
    return dy * (g * sg), dy * o * (sg * (1.0 + g * (1.0 - sg)))


def _fox_bwd(q, k, v, ct, o, lse, dy, gate, riding):
    nq, npair = S // ATT, NH // 2
    ni, no = len(riding.ins), len(riding.outs)

    def body(q_ref, k_ref, v_ref, c_ref, o_ref, lse_ref, dy_ref, gate_ref, *rest):
        dq_ref, dk_ref, dvb_ref, dc_ref, dgate_ref = rest[ni:ni + 5]
        dv_ref = rest[ni + 5 + no]
        pair, i = pl.program_id(0), pl.program_id(1)
        at_end = riding.hooks(rest[:ni], rest[ni + 5:ni + 5 + no], *rest[ni + 6 + no:],
                              first=(pair == 0) & (i == 0), middle=(pair == npair // 2) & (i == 0),
                              last=(pair == npair - 1) & (i == nq - 1))

        @pl.when(i == 0)
        def _():
            dk_ref[...] = jnp.zeros_like(dk_ref)
            dv_ref[...] = jnp.zeros_like(dv_ref)
            dc_ref[...] = jnp.zeros_like(dc_ref)

        q2, lse2 = q_ref[...], lse_ref[...]
        do2, dgate = _gate_grads(dy_ref[...], o_ref[...], gate_ref[...])
        dgate_ref[...] = dgate.astype(BF16)
        do2b = do2.astype(BF16)
        prod = do2b.astype(F32) * o_ref[...]
        heads = []
        for hh in (0, 1):
            hm = _head_mask(hh)
            heads.append((jnp.where(hm, q2, jnp.zeros_like(q2)), jnp.where(hm, do2b, jnp.zeros_like(do2b)),
                          jnp.sum(jnp.where(hm, prod, 0.0), axis=1, keepdims=True),
                          jnp.max(jnp.where(hm, lse2, NEG), axis=1, keepdims=True)))

        def step(off, width, dqs, diag):
            off = pl.multiple_of(off, ATT)
            kj, vj = k_ref[pl.ds(off, width), :], v_ref[pl.ds(off, width), :]
            dk, dv, out = None, None, []
            for hh in (0, 1):
                qm, dom, delta, lse_h = heads[hh]
                s = _dot(qm, kj, NT) - c_ref[hh:hh + 1, pl.ds(off, width)]
                p = jnp.exp(s - lse_h)
                if diag:
                    row = i * ATT + lax.broadcasted_iota(jnp.int32, (ATT, width), 0)
                    col = off + lax.broadcasted_iota(jnp.int32, (ATT, width), 1)
                    p = jnp.where(col <= row, p, 0.0)
                ds = p * (_dot(dom, vj, NT) - delta)
                dc_ref[hh:hh + 1, pl.ds(off, width)] += -jnp.sum(ds, axis=0, keepdims=True)
                dsb = ds.astype(BF16)
                dk_h, dv_h = _dot(dsb, qm, TN), _dot(p.astype(BF16), dom, TN)
                dk, dv = (dk_h, dv_h) if dk is None else (dk + dk_h, dv + dv_h)
                out.append(dqs[hh] + _dot(dsb, kj))
            dk_ref[pl.ds(off, width), :] += dk
            dv_ref[pl.ds(off, width), :] += dv
            return tuple(out)

        zero = jnp.zeros((ATT, LANES), F32)
        dqs = lax.fori_loop(0, i // 2, lambda j, acc: step(j * (2 * ATT), 2 * ATT, acc, False), (zero, zero))
        dqs = lax.cond(i % 2 == 1, lambda acc: step((i - 1) * ATT, 2 * ATT, acc, True),
                       lambda acc: step(i * ATT, ATT, acc, True), dqs)
        dq_ref[...] = jnp.where(_head_mask(0), dqs[0], dqs[1]) * SCALE

        @pl.when(i == nq - 1)
        def _():
            dvb_ref[...] = dv_ref[...].astype(BF16)

        at_end()

    blk = pl.BlockSpec((ATT, LANES), lambda p, i: (i, p))
    full = pl.BlockSpec((S, LANES), lambda p, i: (0, p))
    cspec = pl.BlockSpec((None, 2, S), lambda p, i: (p, 0, 0))
    res = pl.pallas_call(
        body, name="fox_bwd", grid=(npair, nq),
        in_specs=[blk, full, full, cspec, blk, blk, blk, blk] + riding.in_specs,
        out_specs=[blk, full, full, cspec, blk] + riding.out_specs,
        out_shape=[jax.ShapeDtypeStruct((S, D), F32)] * 2 + [jax.ShapeDtypeStruct((S, D), BF16),
                                                              jax.ShapeDtypeStruct((npair, 2, S), F32),
                                                              jax.ShapeDtypeStruct((S, D), BF16)]
                  + riding.out_shape,
        scratch_shapes=[pltpu.VMEM((S, LANES), F32)] + riding.scratch,
        compiler_params=_params(2))(q, k, v, ct, o, lse, dy, gate, *riding.ins)
    return res[0], res[1], res[2], res[3], res[4], res[5:]


def _both_heads(x):
    return jnp.concatenate([jnp.where(_head_mask(hh), x, jnp.zeros_like(x)) for hh in (0, 1)], axis=0)


def _per_head(col0, col1):
    return jnp.concatenate([jnp.broadcast_to(col0, (WINDOW, 1)), jnp.broadcast_to(col1, (WINDOW, 1))], axis=0)


def _unstack(x2):
    return jnp.where(_head_mask(0), x2[:WINDOW], x2[WINDOW:])


def _swa_valid(i, start):
    r = lax.broadcasted_iota(jnp.int32, (2 * WINDOW, 2 * WINDOW), 0)
    qabs = i * WINDOW + jnp.where(r >= WINDOW, r - WINDOW, r)
    kabs = start + lax.broadcasted_iota(jnp.int32, (2 * WINDOW, 2 * WINDOW), 1)
    return (kabs <= qabs) & (qabs - kabs < WINDOW)


def _swa_fwd(q, kdup, vdup, sinks_t, proj, gate_col):
    def body(q_ref, k_ref, v_ref, sk_ref, gate_ref, o_ref, lse_ref, y_ref):
        skv = sk_ref[...]
        first = _head_mask(0)
        for sb in range(SWQ):
            i = pl.program_id(1) * SWQ + sb
            rows = slice(sb * WINDOW, (sb + 1) * WINDOW)
            start = pl.multiple_of(jnp.maximum(i - 1, 0) * WINDOW, WINDOW)
            kk, vv = k_ref[pl.ds(start, 2 * WINDOW), :], v_ref[pl.ds(start, 2 * WINDOW), :]
            q2 = q_ref[rows, :]
            valid = _swa_valid(i, start)[:WINDOW]
            res = []
            for hh in (0, 1):
                hm = _head_mask(hh)
                sink = jnp.max(jnp.where(hm, skv, NEG), axis=1, keepdims=True)
                s = jnp.where(valid, _dot(jnp.where(hm, q2, jnp.zeros_like(q2)), kk, NT), NEG)
                m = jnp.maximum(jnp.max(s, axis=1, keepdims=True), sink)
                p = jnp.exp(s - m)
                l = jnp.sum(p, axis=1, keepdims=True) + jnp.exp(sink - m)
                res.append((_dot(p.astype(BF16), vv) / l, m + jnp.log(l)))
            o = jnp.where(first, res[0][0], res[1][0])
            o_ref[rows, :] = o
            lse_ref[rows, :] = jnp.where(first, res[0][1], res[1][1])
            g = gate_ref[rows, :]
            y_ref[rows, :] = (o * (g * _sigmoid(g))).astype(BF16)

    blk = pl.BlockSpec((SWQ * WINDOW, LANES), lambda p, i: (i, p))
    gate = pl.BlockSpec((SWQ * WINDOW, LANES), lambda p, i: (i, gate_col + p))
    full = pl.BlockSpec((S, LANES), lambda p, i: (0, p // 2))
    return pl.pallas_call(
        body, name="swa_fwd", grid=(NH // 2, S // (SWQ * WINDOW)),
        in_specs=[blk, full, full, pl.BlockSpec((1, LANES), lambda p, i: (0, p)), gate],
        out_specs=[blk, blk, blk],
        out_shape=[jax.ShapeDtypeStruct((S, D), F32)] * 2 + [jax.ShapeDtypeStruct((S, D), BF16)],
        compiler_params=_params(2))(q, kdup, vdup, sinks_t, proj)


def _swa_bwd(q, kdup, vdup, sinks_t, o, lse, dy, proj, gate_col):
    def body(q_ref, k_ref, v_ref, sk_ref, o_ref, lse_ref, dy_ref, gate_ref, dq_ref, dk_ref, dv_ref, dsk_ref,
             dgate_ref):
        @pl.when(pl.program_id(1) == 0)
        def _():
            dk_ref[...] = jnp.zeros_like(dk_ref)
            dv_ref[...] = jnp.zeros_like(dv_ref)
            dsk_ref[...] = jnp.zeros_like(dsk_ref)

        skv = sk_ref[...]
        first = _head_mask(0)
        sink = _per_head(*[jnp.max(jnp.where(_head_mask(hh), skv, NEG), axis=1, keepdims=True) for hh in (0, 1)])
        for sb in range(SWQ):
            i = pl.program_id(1) * SWQ + sb
            rows = slice(sb * WINDOW, (sb + 1) * WINDOW)
            start = pl.multiple_of(jnp.maximum(i - 1, 0) * WINDOW, WINDOW)
            kk, vv = k_ref[pl.ds(start, 2 * WINDOW), :], v_ref[pl.ds(start, 2 * WINDOW), :]
            do2, dgate = _gate_grads(dy_ref[rows, :], o_ref[rows, :], gate_ref[rows, :])
            dgate_ref[rows, :] = dgate.astype(BF16)
            do2b = do2.astype(BF16)
            prod, lse2 = do2b.astype(F32) * o_ref[rows, :], lse_ref[rows, :]
            qs, dos = _both_heads(q_ref[rows, :]), _both_heads(do2b)
            delta = jnp.concatenate([jnp.sum(jnp.where(_head_mask(hh), prod, 0.0), axis=1, keepdims=True)
                                     for hh in (0, 1)], axis=0)
            lse_h = jnp.concatenate([jnp.max(jnp.where(_head_mask(hh), lse2, NEG), axis=1, keepdims=True)
                                     for hh in (0, 1)], axis=0)
            p = jnp.where(_swa_valid(i, start), jnp.exp(_dot(qs, kk, NT) - lse_h), 0.0)
            dsb = (p * (_dot(dos, vv, NT) - delta)).astype(BF16)
            dk_ref[pl.ds(start, 2 * WINDOW), :] += _dot(dsb, qs, TN)
            dv_ref[pl.ds(start, 2 * WINDOW), :] += _dot(p.astype(BF16), dos, TN)
            dq_ref[rows, :] = _unstack(_dot(dsb, kk)) * SCALE
            t = jnp.exp(sink - lse_h) * delta
            dsk_ref[...] += -jnp.where(first, jnp.sum(t[:WINDOW], axis=0, keepdims=True),
                                       jnp.sum(t[WINDOW:], axis=0, keepdims=True))

    blk = pl.BlockSpec((SWQ * WINDOW, LANES), lambda p, i: (i, p))
    full = pl.BlockSpec((S, LANES), lambda p, i: (0, p // 2))
    acc = pl.BlockSpec((S, LANES), lambda p, i: (0, p))
    sk = pl.BlockSpec((1, LANES), lambda p, i: (0, p))
    gate = pl.BlockSpec((SWQ * WINDOW, LANES), lambda p, i: (i, gate_col + p))
    return pl.pallas_call(
        body, name="swa_bwd", grid=(NH // 2, S // (SWQ * WINDOW)),
        in_specs=[blk, full, full, sk, blk, blk, blk, gate],
        out_specs=[blk, acc, acc, sk, blk],
        out_shape=[jax.ShapeDtypeStruct((S, D), F32)] * 3 + [jax.ShapeDtypeStruct((1, D), F32),
                                                              jax.ShapeDtypeStruct((S, D), BF16)],
        compiler_params=_params(2))(q, kdup, vdup, sinks_t, o, lse, dy, proj)


def _adamw_math(w, g, m, v):
    m = ADAM_B1 * m + (1.0 - ADAM_B1) * g
    v = ADAM_B2 * v + (1.0 - ADAM_B2) * jnp.square(g)
    m_hat = m / (1.0 - ADAM_B1 ** ADAM_STEP)
    v_hat = v / (1.0 - ADAM_B2 ** ADAM_STEP)
    delta = -ADAM_LR * (m_hat / (jnp.sqrt(v_hat) + ADAM_EPS) + ADAM_WD * w)
    return delta, m, v


def _adamw_small(ws, gs, ms, vs):
    k = len(ws)

    def body(*refs):
        for p in range(k):
            w_ref, g_ref, m_ref, v_ref = (refs[q * k + p] for q in range(4))
            d, mo, vo = _adamw_math(w_ref[...], g_ref[...], m_ref[...], v_ref[...])
            refs[4 * k + p][...], refs[5 * k + p][...], refs[6 * k + p][...] = d, mo, vo

    res = pl.pallas_call(
        body, name="adamw_small",
        out_shape=[jax.ShapeDtypeStruct(t.shape, F32) for t in ws] * 3)(*ws, *gs, *ms, *vs)
    return res[:k], res[k:2 * k], res[2 * k:]


SUM_TILES = (512, 256, 128)


FLAT_BLOCK = 257 * 1024


def _tiles(shape, axis, lead=0, halves=False):
    if len(shape) == 1:
        count = shape[0] // FLAT_BLOCK
        return (FLAT_BLOCK,), count, lambda pos, *lead_idx: (sum(k * count for k in lead_idx) + pos,)
    r, c = shape
    tile = next(t for t in SUM_TILES if (shape[axis] // (2 if halves else 1)) % t == 0)
    blk = (tile, c) if axis == 0 else (r, tile)
    count = shape[axis] // tile

    def index(pos, *lead_idx):
        return tuple(lead_idx) + ((pos, 0) if axis == 0 else (0, pos))

    return (None,) * lead + blk, count, index


def _adamw_halves(w, g_mine, g_theirs, m, v, axis, name):
    blk, count, index = _tiles(w.shape, axis, halves=True)
    per_half = count // 2

    def body(w_ref, a_ref, b_ref, m_ref, v_ref, g_ref, d_ref, mo_ref, vo_ref):
        is_mine = pl.program_id(0) // per_half == lax.axis_index("c")
        g = jnp.where(is_mine, a_ref[...], b_ref[...])
        g_ref[...] = g
        d_ref[...], mo_ref[...], vo_ref[...] = _adamw_math(w_ref[...], g, m_ref[...], v_ref[...])

    spec = pl.BlockSpec(blk, lambda i: index(i))
    half = pl.BlockSpec(blk, lambda i: index(i % per_half))
    return pl.pallas_call(
        body, name=name, grid=(count,), in_specs=[spec, half, half, spec, spec], out_specs=[spec] * 4,
        out_shape=[jax.ShapeDtypeStruct(w.shape, F32)] * 4, compiler_params=_params(1))(w, g_mine, g_theirs, m, v)


def _chip_sum(blocks, from_sibling, axis, name):
    flat = blocks.ndim == 1
    blk, count, index = _tiles((from_sibling.shape[0] // NCHIP,) if flat else from_sibling.shape[1:], axis, lead=1)

    def body(lo_ref, hi_ref, p_ref, o32, o16):
        mine = jnp.where(lax.axis_index("c") == 0, lo_ref[...], hi_ref[...])
        acc = mine + p_ref[...]
        o32[...] = acc
        o16[...] = acc.astype(BF16)

    half = pl.BlockSpec(blk, lambda k, i: index(i, k))
    if flat:
        lo = pl.BlockSpec(blk, lambda k, i: (2 * count * k + i,))
        hi = pl.BlockSpec(blk, lambda k, i: (2 * count * k + count + i,))
    else:
        lo, hi = half, pl.BlockSpec(blk, lambda k, i: index(i + count, k))
    return pl.pallas_call(
        body, name=name, grid=(NCHIP, count), in_specs=[lo, hi, half], out_specs=[half, half],
        out_shape=[jax.ShapeDtypeStruct(from_sibling.shape, F32), jax.ShapeDtypeStruct(from_sibling.shape, BF16)],
        compiler_params=_params(2))(blocks, blocks, from_sibling)


def _mesh_sum(own, parts, axis, name):
    blk, count, index = _tiles(own.shape, axis)
    n = NCHIP - 1

    def body(a_ref, *refs):
        acc = a_ref[...]
        for k in range(n):
            acc = acc + refs[k][...].astype(F32)
        refs[n][...] = acc

    spec = pl.BlockSpec(blk, lambda i: index(i))
    if own.ndim == 1:
        part = [pl.BlockSpec(blk, lambda i, k=k: (k * count + i,)) for k in range(n)]
    else:
        part = [pl.BlockSpec((None,) + blk, lambda i, k=k: (k,) + index(i)) for k in range(n)]
    return pl.pallas_call(
        body, name=name, grid=(count,), in_specs=[spec] + part,
        out_specs=spec, out_shape=jax.ShapeDtypeStruct(own.shape, F32),
        compiler_params=_params(1))(own, *([parts] * n))


def _sum_stack(parts, name):
    n = parts.shape[0]

    def body(p_ref, o_ref):
        acc = p_ref[0]
        for k in range(1, n):
            acc = acc + p_ref[k]
        o_ref[...] = acc

    return pl.pallas_call(body, name=name, out_shape=jax.ShapeDtypeStruct(parts.shape[1:], F32))(parts)


def _coords():
    return lax.axis_index("x"), lax.axis_index("y"), lax.axis_index("c")


def _chip(who):
    return 2 * who[0] + who[1]


def _flip(who, mask):
    return tuple((1 - v) if b else v for v, b in zip(who, mask))


def _transfer(transfers, t, I, O, ssem, rsem, receiving):
    tr, me = transfers[t], _coords()
    peer = _flip(me, tr["mask"])
    return pltpu.make_async_remote_copy(
        src_ref=tr["src"](I, O, me), dst_ref=tr["dst"](I, O, peer if receiving else me),
        send_sem=ssem.at[t], recv_sem=rsem.at[t], device_id=peer, device_id_type=MESH)


def _start_transfers(transfers, I, O, ssem, rsem, onward):
    arrived = set()
    for t, tr in enumerate(transfers):
        after = tr.get("after")
        if (after is not None) != onward:
            continue
        if after is not None and after not in arrived:
            _transfer(transfers, after, I, O, ssem, rsem, True).wait_recv()
            arrived.add(after)
        _transfer(transfers, t, I, O, ssem, rsem, False).start()


def _finish_transfers(transfers, I, O, ssem, rsem):
    passed_on = {tr["after"] for tr in transfers if tr.get("after") is not None}
    for t in range(len(transfers)):
        if t not in passed_on:
            _transfer(transfers, t, I, O, ssem, rsem, True).wait_recv()
    for t in range(len(transfers)):
        _transfer(transfers, t, I, O, ssem, rsem, False).wait_send()


def _own_copies(own, I, O, stage, lsem, leg):
    for n, (src, dst) in enumerate(own):
        me = _coords()
        bring =pltpu.make_async_copy(src(I, O, me), stage[n], lsem.at[2 * n])
        put = pltpu.make_async_copy(stage[n], dst(I, O, me), lsem.at[2 * n + 1])
        if leg == 0:
            bring.start()
        elif leg == 1:
            bring.wait()
            put.start()
        else:
            put.wait()


def _own_scratch(own, ins):
    return [pltpu.VMEM(ins[n].shape, ins[n].dtype) for n in range(len(own))], pltpu.SemaphoreType.DMA((max(2 * len(own), 1),))


def _exchange(name, ins, outs, transfers, own=()):
    ni, no = len(ins), len(outs)
    nt = len(transfers)
    stages, stage_sems = _own_scratch(own, ins)

    def body(*refs):
        I, O = refs[:ni], refs[ni:ni + no]
        ssem, rsem, lsem = refs[ni + no:ni + no + 3]
        stage = refs[ni + no + 3:]
        _own_copies(own, I, O, stage, lsem, 0)
        _start_transfers(transfers, I, O, ssem, rsem, False)
        _own_copies(own, I, O, stage, lsem, 1)
        _start_transfers(transfers, I, O, ssem, rsem, True)
        _finish_transfers(transfers, I, O, ssem, rsem)
        _own_copies(own, I, O, stage, lsem, 2)

    hbm = pl.BlockSpec(memory_space=pltpu.HBM)
    return pl.pallas_call(
        body, name=name, in_specs=[hbm] * ni, out_specs=[hbm] * no,
        out_shape=[jax.ShapeDtypeStruct(s, d) for s, d in outs],
        scratch_shapes=[pltpu.SemaphoreType.DMA((nt,)), pltpu.SemaphoreType.DMA((nt,)), stage_sems] + stages,
        compiler_params=pltpu.CompilerParams(has_side_effects=True, vmem_limit_bytes=VMEM_LIMIT))(*ins)


CHIP_MASKS = [(0, 1, 0), (1, 0, 0), (1, 1, 0)]
SIBLING = (0, 0, 1)


def _half(shape2d, axis, which):
    n = shape2d[axis] // 2
    cut = pl.ds(pl.multiple_of(which * n, n), n)
    return (cut, slice(None)) if axis == 0 else (slice(None), cut)


class _Riding:
    def __init__(self, transfers, ins, outs, own=()):
        self.transfers, self.ins, self.outs, self.own = transfers, list(ins), list(outs), list(own)
        hbm = pl.BlockSpec(memory_space=pltpu.HBM)
        self.in_specs, self.out_specs = [hbm] * len(self.ins), [hbm] * len(self.outs)
        self.out_shape = [jax.ShapeDtypeStruct(s, d) for s, d in self.outs]
        stages, stage_sems = _own_scratch(self.own, self.ins)
        self.scratch = [pltpu.SemaphoreType.DMA((max(len(transfers), 1),))] * 2 + [stage_sems] + stages

    def alone(self, name):
        return _exchange(name, self.ins, self.outs, self.transfers, self.own)

    def hooks(self, I, O, ssem, rsem, lsem, *stage, first, middle, last):
        tr, own = self.transfers, self.own

        @pl.when(first)
        def _():
            _own_copies(own, I, O, stage, lsem, 0)
            _start_transfers(tr, I, O, ssem, rsem, False)

        if own or any(t.get("after") is not None for t in tr):
            @pl.when(middle)
            def _():
                _own_copies(own, I, O, stage, lsem, 1)
                _start_transfers(tr, I, O, ssem, rsem, True)

        def at_end():
            @pl.when(last)
            def _():
                _finish_transfers(tr, I, O, ssem, rsem)
                _own_copies(own, I, O, stage, lsem, 2)

        return at_end


def _stretch(n, pos):
    return (pl.ds(pos * n if isinstance(pos, int) else pl.multiple_of(pos * n, n), n),)


def _gather_plan(shards, axes):
    def half(a, who):
        if shards[a].ndim == 1:
            return _stretch(shards[a].shape[0] // 2, who[2])
        return _half(shards[a].shape, axes[a], who[2])

    def landed(a, chip, who):
        if shards[a].ndim == 1:
            return _stretch(shards[a].shape[0] // 2, 2 * chip + who[2])
        return (chip,) + half(a, who)

    over_ici, onward = [], []
    for a in range(len(shards)):
        for mask in CHIP_MASKS:
            over_ici.append(dict(
                mask=mask,
                src=lambda I, O, me, a=a: I[a].at[half(a, me)],
                dst=lambda I, O, who, a=a: O[a].at[landed(a, _chip(who), who)]))
            onward.append(dict(
                mask=SIBLING, after=len(over_ici) - 1,
                src=lambda I, O, me, a=a, mask=mask: O[a].at[landed(a, _chip(_flip(me, mask)), me)],
                dst=lambda I, O, who, a=a, mask=mask: O[a].at[landed(a, _chip(_flip(who, mask)), who)]))
    outs = [((NCHIP * s.shape[0],) if s.ndim == 1 else (NCHIP,) + s.shape, s.dtype) for s in shards]

    def whole(a, chip):
        return _stretch(shards[a].shape[0], chip) if shards[a].ndim == 1 else (chip,)

    own = [(lambda I, O, me, a=a: I[a], lambda I, O, me, a=a: O[a].at[whole(a, _chip(me))])
           for a in range(len(shards))]
    return over_ici + onward, outs, own


def _gather_shards(shards, axes):
    transfers, outs, own = _gather_plan(shards, axes)
    return _exchange("gather_weights", shards, outs, transfers, own)


def _to_sibling(arrs, name):
    transfers = [dict(mask=SIBLING, src=lambda I, O, me, a=a: I[a], dst=lambda I, O, who, a=a: O[a])
                 for a in range(len(arrs))]
    return _exchange(name, arrs, [(t.shape, t.dtype) for t in arrs], transfers)


def _halves_plan(blocks, axes):
    def cut(a, which):
        return (slice(None),) + _half(blocks[a].shape[1:], axes[a], which)

    transfers, outs = [], []
    for a, (b, ax) in enumerate(zip(blocks, axes)):
        if b.ndim == 1:
            h = b.shape[0] // NCHIP // 2
            for k in range(NCHIP):
                transfers.append(dict(mask=SIBLING,
                                      src=lambda I, O, me, a=a, k=k, h=h: I[a].at[_stretch(h, 2 * k + 1 - me[2])],
                                      dst=lambda I, O, who, a=a, k=k, h=h: O[a].at[_stretch(h, k)]))
            outs.append(((NCHIP * h,), b.dtype))
        else:
            transfers.append(dict(mask=SIBLING, src=lambda I, O, me, a=a: I[a].at[cut(a, 1 - me[2])],
                                  dst=lambda I, O, who, a=a: O[a]))
            shape = list(b.shape)
            shape[ax + 1] //= 2
            outs.append((tuple(shape), b.dtype))
    return transfers, outs


def _scatter_plan(tb):
    def slot(a, k):
        return (k,) if tb[a].ndim == 3 else _stretch(tb[a].shape[0] // NCHIP, k)

    transfers = []
    for a in range(len(tb)):
        for n, mask in enumerate(CHIP_MASKS):
            transfers.append(dict(
                mask=mask,
                src=lambda I, O, me, a=a, mask=mask: I[a].at[slot(a, _chip(_flip(me, mask)))],
                dst=lambda I, O, who, a=a, n=n: O[a].at[slot(a, n)]))
    outs = [((3,) + t.shape[1:] if t.ndim == 3 else (3 * (t.shape[0] // NCHIP),), t.dtype) for t in tb]
    return transfers, outs


def _gather_small(vec):
    def slot(who):
        return 4 * who[0] + 2 * who[1] + who[2]

    masks = [(m >> 2 & 1, m >> 1 & 1, m & 1) for m in range(1, 8)]
    transfers = [dict(mask=mask, src=lambda I, O, me: I[0], dst=lambda I, O, who: O[0].at[slot(who)])
                 for mask in masks]
    own = [(lambda I, O, me: I[0], lambda I, O, me: O[0].at[slot(me)])]
    return _exchange("gather_small", [vec], [((8,) + vec.shape, vec.dtype)], transfers, own)[0]


def _rope_tables(positions):
    half = ROT // 2
    inv_freq = jnp.power(jnp.float32(THETA), -jnp.arange(0, ROT, 2, dtype=F32) / ROT)
    ang = positions.astype(F32)[:, None] * inv_freq[None, :]
    cos, sin = jnp.cos(ang), jnp.sin(ang)
    one, zero, z8 = jnp.ones((S, HD - ROT), F32), jnp.zeros((S, HD - ROT), F32), jnp.zeros((S, half), F32)
    c = jnp.concatenate([cos, cos, one], axis=1)
    a = jnp.concatenate([-sin, z8, zero], axis=1)
    b = jnp.concatenate([z8, sin, zero], axis=1)
    return tuple(jnp.tile(t, (1, 2)) for t in (c, a, b))


def _tile_heads(g, w):
    return jnp.tile(g.reshape(1, HD), (1, w // HD))


def _fold_heads(dg):
    return dg.reshape(-1, HD).sum(axis=0)


def _pad_lanes(a):
    return jnp.pad(a, ((0, 0), (0, LANES - a.shape[1])))


def _local_step(x, target, positions, wt, fetch, late_weights, begin_reduce):
    rope = _rope_tables(positions)
    w1t = wt["w_in_a_t"]
    f_row = 3 * D // LANES
    wg_t = w1t[3 * D + NH:]
    in_b_block = lambda c: pl.BlockSpec((None, TN_, TN_), lambda j, i: (c, j, 0))
    b_pad = _pad_lanes(wt["b_forget"].reshape(1, NH))
    qg_a, kg_a = _tile_heads(wt["qnorm_a_g"], D), _tile_heads(wt["knorm_a_g"], D)
    qg_b, kg_b = _tile_heads(wt["qnorm_b_g"], D), _tile_heads(wt["knorm_b_g"], KVW)
    norm_a, kv_g, norm_b = wt["norm_a_g"].reshape(1, D), wt["kv_norm_g"].reshape(1, D), wt["norm_b_g"].reshape(1, D)
    sinks_t = jnp.repeat(wt["sinks"].reshape(1, NH), HD, axis=1)

    (u_a,) = _rmsnorm_fwd(x, [norm_a], "norm_a")
    qkv = _mm("proj_a", S, 3 * D, [(u_a, _a_rows(D), w1t, _b_rows(D, tn=TN_WIDE), NT)], tn=TN_WIDE)
    fpad = _mm("proj_f", S, LANES, [(u_a, _a_rows(D), w1t, _b_rows(D, row0=f_row, tn=LANES), NT)], tn=LANES)
    gate_a = _mm("proj_gate_a", S, D, [(u_a, _a_rows(D), wg_t, _b_rows(D, tn=TN_WIDE), NT)], tn=TN_WIDE)
    q_a, k_a, v_a = _a_post(qkv, qg_a, kg_a)
    ct = _forget_cumsum(fpad, b_pad)
    ct2 = ct[:NH].reshape(NH // 2, 2, S)
    o_a, lse_a, y_a, fetched = _fox_fwd(q_a, k_a, v_a, ct2, gate_a, fetch)
    wt = {**wt, **late_weights(fetched)}
    w_in_b = wt["w_in_b"]
    h1 = _mm("out_a", S, D, [(y_a, _a_rows(D), wt["w_out_a"], _b_cols(D, tn=TN_WIDE), None)], add=x, tn=TN_WIDE)
    u_kv, u_b = _rmsnorm_fwd(h1, [kv_g, norm_b], "norm_b")
    kv = _mm("proj_kv", S, 2 * KVW, [(u_kv, _a_rows(D), wt["w_kv"], _b_cols(D), None)])
    pb = _mm("proj_b", S, 2 * D,
             [(u_b, _a_rows(D), w_in_b, pl.BlockSpec((None, D, TN_), lambda j, i: (j, 0, 0)), None)])
    q_b, kdup, vdup = _b_post(pb, kv, qg_b, kg_b, rope)
    gate_b_col = D // LANES
    o_b, lse_b, y_b = _swa_fwd(q_b, kdup, vdup, sinks_t, pb, gate_b_col)
    out = _mm("out_b", S, D, [(y_b, _a_rows(D), wt["w_out_b"], _b_cols(D, tn=TN_WIDE), None)], add=h1, tn=TN_WIDE)
    d_out, d_out_b, sq = _loss_head(out, target)

    g = {}
    g["w_out_b"] = _mm("dw_out_b", D, D, [(y_b, _a_cols(S), d_out_b, _b_cols(S), TN)])
    d_y_b = _mm("dy_b", S, D, [(d_out_b, _a_rows(D), wt["w_out_b"], _b_rows(D, tn=TN_WIDE), NT)], tn=TN_WIDE)
    dq_b, dkdup, dvdup, dsk, d_gate_b = _swa_bwd(q_b, kdup, vdup, sinks_t, o_b, lse_b, d_y_b, pb, gate_b_col)
    g["sinks"] = dsk[0, ::HD]
    d_qb_raw, dg = _headnorm_bwd(pb, 0, qg_b, dq_b, rope, "qnorm_b_bwd")
    g["qnorm_b_g"] = _fold_heads(dg)
    d_pb = [d_qb_raw, d_qb_raw, d_gate_b, d_gate_b]
    g["w_in_b"] = jnp.concatenate([
        _mm("dw_in_b_q", D, D, [(u_b, _a_cols(S), d_qb_raw, _b_cols(S), TN)], stacked=True),
        _mm("dw_in_b_gate", D, D, [(u_b, _a_cols(S), d_gate_b, _b_cols(S), TN)], stacked=True)], axis=0)
    d_u_b = _mm("du_b", S, D, [(d_pb[c], _a_rows(TN_, col=c % 2), w_in_b, in_b_block(c), NT) for c in range(NCHIP)])
    d_kv, dg = _kv_bwd(dkdup, dvdup, kv, kg_b, rope)
    g["knorm_b_g"] = _fold_heads(dg)
    g["w_kv"] = _mm("dw_kv", D, 2 * KVW, [(u_kv, _a_cols(S), d_kv, _b_cols(S), TN)])
    d_u_kv = _mm("du_kv", S, D, [(d_kv, _a_rows(2 * KVW), wt["w_kv"], _b_rows(2 * KVW), NT)])
    d_h1, d_h1_b, g["kv_norm_g"], g["norm_b_g"] = _rmsnorm_bwd(h1, [kv_g, norm_b], [d_u_kv, d_u_b], d_out, "norm_b_bwd")
    g["w_out_a"] = _mm("dw_out_a", D, D, [(y_a, _a_cols(S), d_h1_b, _b_cols(S), TN)])
    late = {n: g[n] for n in LATE}
    d_y_a, halves = _mm("dy_a", S, D, [(d_h1_b, _a_rows(D), wt["w_out_a"], _b_rows(D, tn=TN_WIDE), NT)],
                        tn=TN_WIDE, riding=begin_reduce(late))
    riding, so_far = begin_reduce(late, halves)
    dq_a, dk_a, dv_a, dct, d_gate_a, arrived = _fox_bwd(q_a, k_a, v_a, ct2, o_a, lse_a, d_y_a, gate_a, riding)
    dct_pad = jnp.pad(dct.reshape(NH, S), ((0, LANES - NH), (0, 0)))
    d_f, db = _forget_bwd(dct_pad, fpad, b_pad)
    g["b_forget"] = db[0, :NH]
    d_q_raw, dg = _headnorm_bwd(qkv, 0, qg_a, dq_a, None, "qnorm_a_bwd")
    g["qnorm_a_g"] = _fold_heads(dg)
    d_k_raw, dg = _headnorm_bwd(qkv, 1, kg_a, dk_a, None, "knorm_a_bwd")
    g["knorm_a_g"] = _fold_heads(dg)
    rows, gw = 4 * D + NH, None
    for n, t, row0 in (("q", d_q_raw, 0), ("k", d_k_raw, D), ("v", dv_a, 2 * D)):
        gw = _mm("dw_in_a_" + n, D, D, [(t, _a_cols(S), u_a, _b_cols(S), TN)], rows_of=(gw, rows, row0))
    gw = _mm("dw_in_a_f", LANES, D, [(d_f, _a_cols(S, tm=LANES), u_a, _b_cols(S), TN)], tm=LANES,
             rows_of=(gw, rows, 3 * D))
    g["w_in_a"] = _mm("dw_in_a_gate", D, D, [(d_gate_a, _a_cols(S), u_a, _b_cols(S), TN)],
                      rows_of=(gw, rows, 3 * D + NH))
    first = {"w_in_a": g["w_in_a"]}
    riding, so_far_first = begin_reduce(first, begin_reduce(first).alone("sibling_halves_w_in_a"))
    d_u_a, arrived_first = _mm("du_a", S, D, [
        (d_q_raw, _a_rows(D), w1t, _b_cols(D, row=0), None), (d_k_raw, _a_rows(D), w1t, _b_cols(D, row=1), None),
        (dv_a, _a_rows(D), w1t, _b_cols(D, row=2), None), (d_gate_a, _a_rows(D), wg_t, _b_cols(D), None),
        (d_f, _a_rows(LANES), w1t, _b_cols(LANES, row=f_row), None)], riding=riding)
    d_x, _, g["norm_a_g"] = _rmsnorm_bwd(x, [norm_a], [d_u_a], d_h1, "norm_a_bwd")
    return sq, d_x, g, (list(so_far_first) + list(so_far), list(arrived_first) + list(arrived))


BIG = ["w_in_a", "w_out_a", "w_kv", "w_in_b", "w_out_b"]
LATE = BIG[1:]
SPLIT = {"w_in_a": None, "w_out_a": 0, "w_kv": 0, "w_in_b": 0, "w_out_b": 0}
SMALL = ["norm_a_g", "b_forget", "qnorm_a_g", "knorm_a_g", "kv_norm_g", "knorm_b_g", "norm_b_g", "qnorm_b_g", "sinks"]
NAMES = ["norm_a_g", "w_in_a", "b_forget", "qnorm_a_g", "knorm_a_g", "w_out_a", "kv_norm_g", "w_kv", "knorm_b_g",
         "norm_b_g", "w_in_b", "qnorm_b_g", "sinks", "w_out_b"]


def _pack(vals):
    flat = []
    for v in vals:
        v = v.reshape(-1)
        flat.append(jnp.pad(v, (0, -v.shape[0] % LANES)))
    flat = jnp.concatenate(flat)
    flat = jnp.pad(flat, (0, -flat.shape[0] % (8 * LANES)))
    return flat.reshape(-1, LANES)


def _unpack(packed, shapes):
    flat, out, off = packed.reshape(-1), [], 0
    for s in shapes:
        n = int(np.prod(s))
        out.append(flat[off:off + n].reshape(s))
        off += n + (-n % LANES)
    return out


def kernel(x, positions, norm_a_g, w_in_a, b_forget, qnorm_a_g, knorm_a_g, w_out_a, kv_norm_g, w_kv, knorm_b_g, norm_b_g, w_in_b, qnorm_b_g, sinks, w_out_b, loss_target, m_norm_a_g, m_w_in_a, m_b_forget, m_qnorm_a_g, m_knorm_a_g, m_w_out_a, m_kv_norm_g, m_w_kv, m_knorm_b_g, m_norm_b_g, m_w_in_b, m_qnorm_b_g, m_sinks, m_w_out_b, v_norm_a_g, v_w_in_a, v_b_forget, v_qnorm_a_g, v_knorm_a_g, v_w_out_a, v_kv_norm_g, v_w_kv, v_knorm_b_g, v_norm_b_g, v_w_in_b, v_qnorm_b_g, v_sinks, v_w_out_b):
    w = dict(norm_a_g=norm_a_g, w_in_a=w_in_a, b_forget=b_forget, qnorm_a_g=qnorm_a_g, knorm_a_g=knorm_a_g,
             w_out_a=w_out_a, kv_norm_g=kv_norm_g, w_kv=w_kv, knorm_b_g=knorm_b_g, norm_b_g=norm_b_g,
             w_in_b=w_in_b, qnorm_b_g=qnorm_b_g, sinks=sinks, w_out_b=w_out_b)
    m = dict(norm_a_g=m_norm_a_g, w_in_a=m_w_in_a, b_forget=m_b_forget, qnorm_a_g=m_qnorm_a_g, knorm_a_g=m_knorm_a_g,
             w_out_a=m_w_out_a, kv_norm_g=m_kv_norm_g, w_kv=m_w_kv, knorm_b_g=m_knorm_b_g, norm_b_g=m_norm_b_g,
             w_in_b=m_w_in_b, qnorm_b_g=m_qnorm_b_g, sinks=m_sinks, w_out_b=m_w_out_b)
    v = dict(norm_a_g=v_norm_a_g, w_in_a=v_w_in_a, b_forget=v_b_forget, qnorm_a_g=v_qnorm_a_g, knorm_a_g=v_knorm_a_g,
             w_out_a=v_w_out_a, kv_norm_g=v_kv_norm_g, w_kv=v_w_kv, knorm_b_g=v_knorm_b_g, norm_b_g=v_norm_b_g,
             w_in_b=v_w_in_b, qnorm_b_g=v_qnorm_b_g, sinks=v_sinks, w_out_b=v_w_out_b)
    my_chip = 2 * lax.axis_index("x") + lax.axis_index("y")

    def shard2d(t, n):
        if n == "w_in_a":
            return jnp.transpose(t, (2, 0, 1)).reshape(-1)
        return t.reshape(t.shape[-2:])

    def unflat(t, n):
        return jnp.transpose(t.reshape(-1, 1, D), (1, 2, 0)) if n == "w_in_a" else t.reshape(w[n].shape)

    w2d = {n: shard2d(w[n], n) for n in BIG}

    norm_a_rows = jnp.broadcast_to(norm_a_g.reshape(1, D // NCHIP), (2 * SUBLANES, D // NCHIP))
    w1t, norm_rows = _gather_shards([w2d["w_in_a"].astype(BF16), norm_a_rows], [SPLIT["w_in_a"], 0])
    wt = {"w_in_a_t": w1t.reshape(-1, D), "norm_a_g": norm_rows[:, 0, :].reshape(1, D)}
    for n in SMALL[1:]:
        wt[n] = w[n]
    late_shards = [w2d[n].astype(BF16) for n in LATE]
    late_axes = [SPLIT[n] for n in LATE]
    transfers, outs, own = _gather_plan(late_shards, late_axes)
    fetch = _Riding(transfers, late_shards, outs, own)

    def late_weights(fetched):
        return {n: t if n == "w_in_b" else t.reshape(-1, t.shape[2]) for n, t in zip(LATE, fetched)}

    def as_blocks(t):
        if t.ndim == 3:
            return t
        return t.reshape(-1) if t.shape[0] % (SUBLANES * NCHIP) else t.reshape(NCHIP, -1, t.shape[1])

    def begin_reduce(grads, halves=None):
        names = list(grads)
        axes = [SPLIT[n] for n in names]
        blocks = [as_blocks(grads[n]) for n in names]
        if halves is None:
            transfers, outs = _halves_plan(blocks, axes)
            return _Riding(transfers, blocks, outs)
        sums = [_chip_sum(blk, part, ax, "chip_sum_" + n) for n, ax, blk, part in zip(names, axes, blocks, halves)]
        bf16 = [s[1] for s in sums]
        transfers, outs = _scatter_plan(bf16)
        return _Riding(transfers, bf16, outs), [s[0] for s in sums]

    sq, d_x, g, (chip_f32, arrived) = _local_step(x[0], loss_target[0], positions, wt, fetch, late_weights,
                                                  begin_reduce)

    small_shapes = [(D,), (NH,), (HD,), (HD,), (D,), (HD,), (D,), (HD,), (NH,), (D,)]
    packed = _pack([g[n] for n in SMALL] + [sq])
    total = _sum_stack(_gather_small(packed), "sum_small")
    small_g = dict(zip(SMALL, _unpack(total, small_shapes)[:-1]))
    loss = 0.5 * jnp.sum(_unpack(total, small_shapes)[-1]) / D
    small_g["norm_a_g"] = lax.dynamic_slice(small_g["norm_a_g"], (my_chip * (D // NCHIP),), (D // NCHIP,))

    axes = [SPLIT[n] for n in BIG]
    halves = []
    for n, ax, t32, parts in zip(BIG, axes, chip_f32, arrived):
        if t32.ndim == 1:
            own = lax.dynamic_slice_in_dim(t32, my_chip * (t32.shape[0] // NCHIP), t32.shape[0] // NCHIP)
        else:
            own = lax.dynamic_index_in_dim(t32, my_chip, axis=0, keepdims=False)
        halves.append(_mesh_sum(own, parts, ax, "mesh_sum_" + n))
    sibling_done = _to_sibling(halves, "finished_halves")

    res = {}
    for n, ax, mine_half, their_half in zip(BIG, axes, halves, sibling_done):
        out4 = _adamw_halves(w2d[n], mine_half, their_half, shard2d(m[n], n), shard2d(v[n], n), ax, "adamw_" + n)
        res[n] = tuple(unflat(t, n) for t in out4)
    row = lambda t: t.reshape(1, -1)
    small_out = _adamw_small(*[[row(d[n]) for n in SMALL] for d in (w, small_g, m, v)])
    for i, n in enumerate(SMALL):
        res[n] = tuple(t.reshape(w[n].shape) for t in (small_g[n],) + tuple(out[i] for out in small_out))

    outs = [loss, d_x[None]]
    for k in range(4):
        outs += [res[n][k] for n in NAMES]
    return tuple(outs)
```

```python
import numpy as np
import jax
import jax.numpy as jnp
from jax import lax
from jax.experimental import pallas as pl
from jax.experimental.pallas import tpu as pltpu

F32, BF16 = jnp.float32, jnp.bfloat16
S, D, HD, NH, NKV = 2048, 1024, 64, 16, 4
KVW = NKV * HD
WINDOW = 128
ROT = HD // 4
THETA = 500000.0
EPS = 1e-6
SCALE = HD ** -0.5
LANES = 128
SUBLANES = 8
NEG = -1e30
VMEM_LIMIT = 48 * 2 ** 20
ROWS = 512
ATT = 512
SWQ = 16
NCHIP = 4
ADAM_LR, ADAM_B1, ADAM_B2, ADAM_EPS, ADAM_WD, ADAM_STEP = 0.001, 0.9, 0.999, 1e-08, 0.01, 10
NT = (((1,), (1,)), ((), ()))
TN = (((0,), (0,)), ((), ()))
MESH = pl.DeviceIdType.MESH


def _params(n):
    return pltpu.CompilerParams(dimension_semantics=("arbitrary",) * n, vmem_limit_bytes=VMEM_LIMIT)


def _dot(a, b, dims=None):
    if dims is None:
        return jnp.dot(a, b, preferred_element_type=F32)
    return lax.dot_general(a, b, dims, preferred_element_type=F32)


def _dot_split(a, b, n):
    out, rest = None, a
    for _ in range(n):
        hi = rest.astype(BF16)
        term = _dot(hi, b)
        out = term if out is None else out + term
        rest = rest - hi.astype(F32)
    return out


def _seg_mat(w):
    e = (np.arange(w)[:, None] // HD == np.arange(LANES)[None, :]).astype(np.float32)
    return jnp.asarray(e, BF16)


def _spread(r, w):
    head = lax.broadcasted_iota(jnp.int32, (2 * LANES, w), 1) >> (HD.bit_length() - 1)
    row = lax.broadcasted_iota(jnp.int32, (2 * LANES, w), 0)
    et2 = jnp.where(head == (row & (LANES - 1)), 1.0, 0.0).astype(BF16)
    hi = r.astype(BF16)
    lo = (r - hi.astype(F32)).astype(BF16)
    return _dot(jnp.concatenate([hi, lo], axis=1), et2)


def _head_rstd(x, e):
    ss = _dot_split(x * x, e, 2)
    return _spread(lax.rsqrt(ss * (1.0 / HD) + EPS), x.shape[1])


def _rope(x, c, a, b):
    w = x.shape[1]
    return x * c + pltpu.roll(x, w - ROT // 2, 1) * a + pltpu.roll(x, ROT // 2, 1) * b


def _rope_t(dy, c, a, b):
    w = dy.shape[1]
    return dy * c + pltpu.roll(dy * b, w - ROT // 2, 1) + pltpu.roll(dy * a, ROT // 2, 1)


def _sigmoid(x):
    return 1.0 / (1.0 + jnp.exp(-x))


def _row_spec(shape, ts):
    nd = len(shape)
    if shape[0] == S:
        return pl.BlockSpec((ts,) + tuple(shape[1:]), lambda i: (i,) + (0,) * (nd - 1))
    return pl.BlockSpec(tuple(shape), lambda i: (0,) * nd)


def _rows_call(body, name, ins, outs, ts=ROWS):
    return pl.pallas_call(
        body, name=name, grid=(S // ts,),
        in_specs=[_row_spec(a.shape, ts) for a in ins],
        out_specs=[_row_spec(s, ts) for s, _ in outs],
        out_shape=[jax.ShapeDtypeStruct(s, d) for s, d in outs],
        compiler_params=_params(1))(*ins)


def _col_spec(ts, w, col):
    return pl.BlockSpec((ts, w), lambda i: (i, col))


TM = TN_ = 512
TM_TOKENS = 1024
TN_WIDE = 1024


def _mm(name, m, n, terms, out_dtype=F32, add=None, tm=None, tn=TN_, stacked=False, riding=None, rows_of=None):
    nterm = len(terms)
    if tm is None:
        tm = TM_TOKENS if m == S else TM
    nj, ni_ = n // tn, m // tm
    n_in = 2 * nterm + (add is not None) + (rows_of is not None and rows_of[0] is not None)
    r_in, r_out = (len(riding.ins), len(riding.outs)) if riding is not None else (0, 0)

    def body(*refs):
        if riding is not None:
            j, i = pl.program_id(0), pl.program_id(1)
            at_end = riding.hooks(refs[n_in:n_in + r_in], refs[n_in + r_in + 1:n_in + r_in + 1 + r_out],
                                  *refs[n_in + r_in + 1 + r_out:], first=(j == 0) & (i == 0),
                                  middle=(j == nj // 2) & (i == 0), last=(j == nj - 1) & (i == ni_ - 1))
        acc = None
        for t in range(nterm):
            part = _dot(refs[2 * t][...], refs[2 * t + 1][...], terms[t][4])
            acc = part if acc is None else acc + part
        if add is not None:
            acc = acc + refs[2 * nterm][...]
        refs[n_in + r_in][...] = acc.astype(out_dtype)
        if riding is not None:
            at_end()

    tile = pl.BlockSpec((tm, tn), lambda j, i: (i, j))
    ins, specs = [], []
    for a, a_spec, b, b_spec, _ in terms:
        ins += [a, b]
        specs += [a_spec, b_spec]
    if add is not None:
        ins.append(add)
        specs.append(tile)
    out_spec = pl.BlockSpec((None, tm, tn), lambda j, i: (j, i, 0)) if stacked else tile
    out_shape = jax.ShapeDtypeStruct((nj, m, tn) if stacked else (m, n), out_dtype)
    if rows_of is not None:
        taller, rows, row0 = rows_of
        out_spec = pl.BlockSpec((pl.Element(tm), pl.Element(tn)), lambda j, i: (
            pl.multiple_of(row0 + i * tm, SUBLANES), pl.multiple_of(j * tn, LANES)))
        out_shape = jax.ShapeDtypeStruct((rows, n), out_dtype)
        alias = {}
        if taller is not None:
            ins.append(taller)
            specs.append(pl.BlockSpec(memory_space=pltpu.HBM))
            alias = {len(ins) - 1: 0}
        return pl.pallas_call(body, name=name, grid=(nj, ni_), in_specs=specs, out_specs=out_spec,
                              out_shape=out_shape, input_output_aliases=alias, compiler_params=_params(2))(*ins)
    if riding is None:
        return pl.pallas_call(body, name=name, grid=(nj, ni_), in_specs=specs, out_specs=out_spec,
                              out_shape=out_shape, compiler_params=_params(2))(*ins)
    res = pl.pallas_call(
        body, name=name, grid=(nj, ni_), in_specs=specs + riding.in_specs,
        out_specs=[out_spec] + riding.out_specs, out_shape=[out_shape] + riding.out_shape,
        scratch_shapes=riding.scratch, compiler_params=_params(2))(*ins, *riding.ins)
    return res[0], res[1:]


def _a_rows(k, col=0, tm=TM_TOKENS):
    return pl.BlockSpec((tm, k), lambda j, i: (i, col))


def _a_cols(k, tm=TM):
    return pl.BlockSpec((k, tm), lambda j, i: (0, i))


def _b_cols(k, row=0, col0=0, tn=TN_):
    return pl.BlockSpec((k, tn), lambda j, i: (row, col0 + j))


def _b_rows(k, row0=0, tn=TN_):
    return pl.BlockSpec((tn, k), lambda j, i: (row0 + j, 0))


def _rmsnorm_fwd(x, gains, name):
    def body(*refs):
        xv = refs[0][...]
        r = lax.rsqrt(jnp.mean(xv * xv, axis=-1, keepdims=True) + EPS)
        xh = xv * r
        for n in range(len(gains)):
            refs[1 + len(gains) + n][...] = (xh * refs[1 + n][...]).astype(BF16)

    return _rows_call(body, name, [x] + list(gains), [((S, D), BF16)] * len(gains))


def _rmsnorm_bwd(x, gains, dus, dres, name):
    n = len(gains)

    def body(*refs):
        x_ref, g_refs, du_refs, dres_ref = refs[0], refs[1:1 + n], refs[1 + n:1 + 2 * n], refs[1 + 2 * n]
        dx_ref, dxb_ref, dg_refs = refs[2 + 2 * n], refs[3 + 2 * n], refs[4 + 2 * n:]
        xv = x_ref[...]
        r = lax.rsqrt(jnp.mean(xv * xv, axis=-1, keepdims=True) + EPS)
        xh = xv * r
        gy = None
        for m in range(n):
            du = du_refs[m][...]
            part = jnp.sum(du * xh, axis=0, keepdims=True)

            @pl.when(pl.program_id(0) == 0)
            def _(m=m, part=part):
                dg_refs[m][...] = part

            @pl.when(pl.program_id(0) != 0)
            def _(m=m, part=part):
                dg_refs[m][...] += part

            t = du * g_refs[m][...]
            gy = t if gy is None else gy + t
        dx = dres_ref[...] + r * (gy - xh * jnp.mean(gy * xh, axis=-1, keepdims=True))
        dx_ref[...] = dx
        dxb_ref[...] = dx.astype(BF16)

    outs = [((S, D), F32), ((S, D), BF16)] + [((1, D), F32)] * n
    return _rows_call(body, name, [x] + list(gains) + list(dus) + [dres], outs)


def _a_post(qkvg, qg, kg):
    e = _seg_mat(D)

    def body(q_ref, k_ref, v_ref, qg_ref, kg_ref, e_ref, qo, ko, vo):
        ev = e_ref[...]
        qv, kv = q_ref[...], k_ref[...]
        qo[...] = (qv * _head_rstd(qv, ev) * qg_ref[...] * SCALE).astype(BF16)
        ko[...] = (kv * _head_rstd(kv, ev) * kg_ref[...]).astype(BF16)
        vo[...] = v_ref[...].astype(BF16)

    whole = lambda a: pl.BlockSpec(a.shape, lambda i: (0, 0))
    return pl.pallas_call(
        body, name="a_post", grid=(S // ROWS,),
        in_specs=[_col_spec(ROWS, D, 0), _col_spec(ROWS, D, 1), _col_spec(ROWS, D, 2),
                  whole(qg), whole(kg), whole(e)],
        out_specs=[_col_spec(ROWS, D, 0)] * 3,
        out_shape=[jax.ShapeDtypeStruct((S, D), BF16)] * 3,
        compiler_params=_params(1))(qkvg, qkvg, qkvg, qg, kg, e)


def _tri(upper):
    r, c = np.arange(ROWS)[:, None], np.arange(ROWS)[None, :]
    return jnp.asarray((r <= c) if upper else (r >= c), BF16)


def _forget_cumsum(fpad, bpad):
    def body(f_ref, b_ref, u_ref, c_ref, carry):
        @pl.when(pl.program_id(0) == 0)
        def _():
            carry[...] = jnp.zeros_like(carry)

        lf = jax.nn.log_sigmoid(f_ref[...] + b_ref[...])
        blk = _dot_split(lf.T, u_ref[...], 3) + carry[:, 0:1]
        c_ref[...] = blk
        carry[...] = jnp.broadcast_to(blk[:, ROWS - 1:ROWS], carry.shape)

    return pl.pallas_call(
        body, name="forget_cumsum", grid=(S // ROWS,),
        in_specs=[pl.BlockSpec((ROWS, LANES), lambda i: (i, 0)), pl.BlockSpec((1, LANES), lambda i: (0, 0)),
                  pl.BlockSpec((ROWS, ROWS), lambda i: (0, 0))],
        out_specs=pl.BlockSpec((LANES, ROWS), lambda i: (0, i)),
        out_shape=jax.ShapeDtypeStruct((LANES, S), F32),
        scratch_shapes=[pltpu.VMEM((LANES, LANES), F32)],
        compiler_params=_params(1))(fpad, bpad, _tri(True))


def _forget_bwd(dct, fpad, bpad):
    nb = S // ROWS

    def body(dc_ref, f_ref, b_ref, l_ref, df_ref, db_ref, carry):
        @pl.when(pl.program_id(0) == 0)
        def _():
            carry[...] = jnp.zeros_like(carry)
            db_ref[...] = jnp.zeros_like(db_ref)

        blk = _dot_split(dc_ref[...], l_ref[...], 3) + carry[:, 0:1]
        carry[...] = jnp.broadcast_to(blk[:, 0:1], carry.shape)
        df = blk.T * _sigmoid(-(f_ref[...] + b_ref[...]))
        df_ref[...] = df.astype(BF16)
        db_ref[...] += jnp.sum(df, axis=0, keepdims=True)

    return pl.pallas_call(
        body, name="forget_bwd", grid=(nb,),
        in_specs=[pl.BlockSpec((LANES, ROWS), lambda i: (0, nb - 1 - i)),
                  pl.BlockSpec((ROWS, LANES), lambda i: (nb - 1 - i, 0)),
                  pl.BlockSpec((1, LANES), lambda i: (0, 0)), pl.BlockSpec((ROWS, ROWS), lambda i: (0, 0))],
        out_specs=[pl.BlockSpec((ROWS, LANES), lambda i: (nb - 1 - i, 0)), pl.BlockSpec((1, LANES), lambda i: (0, 0))],
        out_shape=[jax.ShapeDtypeStruct((S, LANES), BF16), jax.ShapeDtypeStruct((1, LANES), F32)],
        scratch_shapes=[pltpu.VMEM((LANES, LANES), F32)],
        compiler_params=_params(1))(dct, fpad, bpad, _tri(False))


def _headnorm_bwd(x, col, gain, dy, rope, name):
    e = _seg_mat(D)
    tabs = list(rope) if rope is not None else []

    def body(*refs):
        x_ref, g_ref, dy_ref, e_ref = refs[:4]
        dx_ref, dg_ref = refs[-2:]
        xv, dyv, ev = x_ref[...], dy_ref[...], e_ref[...]
        if rope is not None:
            c, a, b = (jnp.tile(t[...], (1, D // LANES)) for t in refs[4:7])
            dyv = _rope_t(dyv, c, a, b)
        r = _head_rstd(xv, ev)
        xh = xv * r
        part = jnp.sum(dyv * xh, axis=0, keepdims=True)

        @pl.when(pl.program_id(0) == 0)
        def _():
            dg_ref[...] = part

        @pl.when(pl.program_id(0) != 0)
        def _():
            dg_ref[...] += part

        gy = dyv * g_ref[...]
        seg = _spread(_dot_split(gy * xh, ev, 2) * (1.0 / HD), D)
        dx_ref[...] = (r * (gy - xh * seg)).astype(BF16)

    whole = lambda a: pl.BlockSpec(a.shape, lambda i: (0, 0))
    return pl.pallas_call(
        body, name=name, grid=(S // ROWS,),
        in_specs=[_col_spec(ROWS, D, col), whole(gain), _col_spec(ROWS, D, 0), whole(e)]
                 + [pl.BlockSpec((ROWS, LANES), lambda i: (i, 0))] * len(tabs),
        out_specs=[_col_spec(ROWS, D, 0), whole(gain)],
        out_shape=[jax.ShapeDtypeStruct((S, D), BF16), jax.ShapeDtypeStruct((1, D), F32)],
        compiler_params=_params(1))(x, gain, dy, e, *tabs)


def _dup_mat():
    r, c = np.arange(KVW)[:, None], np.arange(2 * KVW)[None, :]
    return (r // HD == c // LANES) & (r % HD == c % HD)


def _fold_mat():
    r, c = np.arange(D)[:, None], np.arange(KVW)[None, :]
    return (r // (2 * LANES) == c // HD) & (r % HD == c % HD)


def _b_post(pb, kv, qg, kg, rope):
    e, ek = _seg_mat(D), _seg_mat(KVW)
    dup = jnp.asarray(_dup_mat(), BF16)

    def body(q_ref, k_ref, v_ref, qg_ref, kg_ref, e_ref, ek_ref, dup_ref, c_ref, a_ref, b_ref, qo, ko, vo):
        c1, a1, b1 = c_ref[...], a_ref[...], b_ref[...]
        qv = q_ref[...]
        qn = qv * _head_rstd(qv, e_ref[...]) * qg_ref[...]
        t = lambda z, n: jnp.tile(z, (1, n))
        qo[...] = (_rope(qn, t(c1, D // LANES), t(a1, D // LANES), t(b1, D // LANES)) * SCALE).astype(BF16)
        kvv = k_ref[...]
        kn = kvv * _head_rstd(kvv, ek_ref[...]) * kg_ref[...]
        kr = _rope(kn, t(c1, KVW // LANES), t(a1, KVW // LANES), t(b1, KVW // LANES)).astype(BF16)
        ko[...] = _dot(kr, dup_ref[...]).astype(BF16)
        vo[...] = _dot(v_ref[...].astype(BF16), dup_ref[...]).astype(BF16)

    whole = lambda a: pl.BlockSpec(a.shape, lambda i: (0, 0))
    tab = pl.BlockSpec((ROWS, LANES), lambda i: (i, 0))
    return pl.pallas_call(
        body, name="b_post", grid=(S // ROWS,),
        in_specs=[_col_spec(ROWS, D, 0), _col_spec(ROWS, KVW, 0), _col_spec(ROWS, KVW, 1),
                  whole(qg), whole(kg), whole(e), whole(ek), whole(dup), tab, tab, tab],
        out_specs=[_col_spec(ROWS, D, 0), _col_spec(ROWS, 2 * KVW, 0), _col_spec(ROWS, 2 * KVW, 0)],
        out_shape=[jax.ShapeDtypeStruct((S, D), BF16), jax.ShapeDtypeStruct((S, 2 * KVW), BF16),
                   jax.ShapeDtypeStruct((S, 2 * KVW), BF16)],
        compiler_params=_params(1))(pb, kv, kv, qg, kg, e, ek, dup, *rope)


def _kv_bwd(dkdup, dvdup, kv, kg, rope):
    ek = _seg_mat(KVW)
    fold = jnp.asarray(_fold_mat(), BF16)

    def body(dk_ref, dv_ref, k_ref, kg_ref, ek_ref, fold_ref, c_ref, a_ref, b_ref, dkv_ref, dg_ref):
        ev, fv = ek_ref[...], fold_ref[...]
        t = lambda z: jnp.tile(z[...], (1, KVW // LANES))
        dk = _rope_t(_dot_split(dk_ref[...], fv, 2), t(c_ref), t(a_ref), t(b_ref))
        dv = _dot_split(dv_ref[...], fv, 2)
        xv = k_ref[...]
        r = _head_rstd(xv, ev)
        xh = xv * r
        part = jnp.sum(dk * xh, axis=0, keepdims=True)

        @pl.when(pl.program_id(0) == 0)
        def _():
            dg_ref[...] = part

        @pl.when(pl.program_id(0) != 0)
        def _():
            dg_ref[...] += part

        gy = dk * kg_ref[...]
        seg = _spread(_dot_split(gy * xh, ev, 2) * (1.0 / HD), KVW)
        dkv_ref[:, 0:KVW] = (r * (gy - xh * seg)).astype(BF16)
        dkv_ref[:, KVW:2 * KVW] = dv.astype(BF16)

    whole = lambda a: pl.BlockSpec(a.shape, lambda i: (0, 0))
    tab = pl.BlockSpec((ROWS, LANES), lambda i: (i, 0))
    return pl.pallas_call(
        body, name="kv_bwd", grid=(S // ROWS,),
        in_specs=[_col_spec(ROWS, D, 0), _col_spec(ROWS, D, 0), _col_spec(ROWS, KVW, 0),
                  whole(kg), whole(ek), whole(fold), tab, tab, tab],
        out_specs=[_col_spec(ROWS, 2 * KVW, 0), whole(kg)],
        out_shape=[jax.ShapeDtypeStruct((S, 2 * KVW), BF16), jax.ShapeDtypeStruct((1, KVW), F32)],
        compiler_params=_params(1))(dkdup, dvdup, kv, kg, ek, fold, *rope)


def _loss_head(out, target):
    def body(o_ref, t_ref, d_ref, db_ref, l_ref):
        diff = o_ref[...] - t_ref[...]
        d = diff * (1.0 / D)
        d_ref[...] = d
        db_ref[...] = d.astype(BF16)

        @pl.when(pl.program_id(0) == 0)
        def _():
            l_ref[...] = jnp.zeros_like(l_ref)

        l_ref[...] += jnp.sum(diff * diff, axis=0, keepdims=True)

    return _rows_call(body, "loss_head", [out, target], [((S, D), F32), ((S, D), BF16), ((1, D), F32)])


def _lane():
    return lax.broadcasted_iota(jnp.int32, (1, LANES), 1)


def _head_mask(hh):
    return (_lane() < HD) if hh == 0 else (_lane() >= HD)


def _fox_fwd(q, k, v, ct, gate, riding):
    nq, npair = S // ATT, NH // 2
    ni, no = len(riding.ins), len(riding.outs)

    def body(q_ref, k_ref, v_ref, c_ref, gate_ref, *rest):
        o_ref, lse_ref, y_ref = rest[ni:ni + 3]
        pair, i = pl.program_id(0), pl.program_id(1)
        at_end = riding.hooks(rest[:ni], rest[ni + 3:ni + 3 + no], *rest[ni + 3 + no:],
                              first=(pair == 0) & (i == 0), middle=(pair == npair // 2) & (i == 0),
                              last=(pair == npair - 1) & (i == nq - 1))
        q2 = q_ref[...]
        qms = [jnp.where(_head_mask(hh), q2, jnp.zeros_like(q2)) for hh in (0, 1)]

        def probs(off, width, m, hh, diag):
            s = _dot(qms[hh], k_ref[pl.ds(off, width), :], NT) - c_ref[hh:hh + 1, pl.ds(off, width)]
            if diag:
                row = i * ATT + lax.broadcasted_iota(jnp.int32, (ATT, width), 0)
                col = off + lax.broadcasted_iota(jnp.int32, (ATT, width), 1)
                s = jnp.where(col <= row, s, NEG)
            m_new = jnp.maximum(m, jnp.max(s, axis=1, keepdims=True))
            p = jnp.exp(s - m_new)
            p_hi = p.astype(BF16)
            return m_new, jnp.exp(m - m_new), p_hi, (p - p_hi.astype(F32)).astype(BF16)

        def weighted(off, width, p_hi, p_lo, hh):
            vj = v_ref[pl.ds(off, width), :]
            v1 = jnp.where(_head_mask(hh), vj, jnp.ones_like(vj))
            return _dot(p_hi, v1) + _dot(p_lo, v1)

        def step(off, width, carry, diag):
            off = pl.multiple_of(off, ATT)
            out = []
            for hh in (0, 1):
                m, acc = carry[hh]
                m, alpha, p_hi, p_lo = probs(off, width, m, hh, diag)
                out.append((m, alpha * acc + weighted(off, width, p_hi, p_lo, hh)))
            return tuple(out)

        one = (jnp.full((ATT, 1), NEG, F32), jnp.zeros((ATT, LANES), F32))
        carry = lax.fori_loop(0, i // 2, lambda j, cr: step(j * (2 * ATT), 2 * ATT, cr, False), (one, one))
        carry = lax.cond(i % 2 == 1, lambda cr: step((i - 1) * ATT, 2 * ATT, cr, True),
                         lambda cr: step(i * ATT, ATT, cr, True), carry)
        res = []
        for hh in (0, 1):
            m, acc = carry[hh]
            l = jnp.max(jnp.where(_head_mask(1 - hh), acc, 0.0), axis=1, keepdims=True)
            res.append((acc / l, m + jnp.log(l)))
        first = _head_mask(0)
        o = jnp.where(first, res[0][0], res[1][0])
        o_ref[...] = o
        lse_ref[...] = jnp.where(first, res[0][1], res[1][1])
        g = gate_ref[...]
        y_ref[...] = (o * (g * _sigmoid(g))).astype(BF16)
        at_end()

    blk = pl.BlockSpec((ATT, LANES), lambda p, i: (i, p))
    full = pl.BlockSpec((S, LANES), lambda p, i: (0, p))
    res = pl.pallas_call(
        body, name="fox_fwd", grid=(npair, nq),
        in_specs=[blk, full, full, pl.BlockSpec((None, 2, S), lambda p, i: (p, 0, 0)), blk] + riding.in_specs,
        out_specs=[blk, blk, blk] + riding.out_specs,
        out_shape=[jax.ShapeDtypeStruct((S, D), F32)] * 2 + [jax.ShapeDtypeStruct((S, D), BF16)] + riding.out_shape,
        scratch_shapes=riding.scratch,
        compiler_params=_params(2))(q, k, v, ct, gate, *riding.ins)
    return res[0], res[1], res[2], res[3:]


def _gate_grads(dy, o, g):
    sg = _sigmoid(g)
    return dy * (g * sg), dy * o * (sg * (1.0 + g * (1.0 - sg)))


def _fox_bwd(q, k, v, ct, o, lse, dy, gate, riding):
    nq, npair = S // ATT, NH // 2
    ni, no = len(riding.ins), len(riding.outs)

    def body(q_ref, k_ref, v_ref, c_ref, o_ref, lse_ref, dy_ref, gate_ref, *rest):
        dq_ref, dk_ref, dvb_ref, dc_ref, dgate_ref = rest[ni:ni + 5]
        dv_ref = rest[ni + 5 + no]
        pair, i = pl.program_id(0), pl.program_id(1)
        at_end = riding.hooks(rest[:ni], rest[ni + 5:ni + 5 + no], *rest[ni + 6 + no:],
                              first=(pair == 0) & (i == 0), middle=(pair == npair // 2) & (i == 0),
                              last=(pair == npair - 1) & (i == nq - 1))

        @pl.when(i == 0)
        def _():
            dk_ref[...] = jnp.zeros_like(dk_ref)
            dv_ref[...] = jnp.zeros_like(dv_ref)
            dc_ref[...] = jnp.zeros_like(dc_ref)

        q2, lse2 = q_ref[...], lse_ref[...]
        do2, dgate = _gate_grads(dy_ref[...], o_ref[...], gate_ref[...])
        dgate_ref[...] = dgate.astype(BF16)
        do2b = do2.astype(BF16)
        prod = do2b.astype(F32) * o_ref[...]
        heads = []
        for hh in (0, 1):
            hm = _head_mask(hh)
            heads.append((jnp.where(hm, q2, jnp.zeros_like(q2)), jnp.where(hm, do2b, jnp.zeros_like(do2b)),
                          jnp.sum(jnp.where(hm, prod, 0.0), axis=1, keepdims=True),
                          jnp.max(jnp.where(hm, lse2, NEG), axis=1, keepdims=True)))

        def step(off, width, dqs, diag):
            off = pl.multiple_of(off, ATT)
            kj, vj = k_ref[pl.ds(off, width), :], v_ref[pl.ds(off, width), :]
            dk, dv, out = None, None, []
            for hh in (0, 1):
                qm, dom, delta, lse_h = heads[hh]
                s = _dot(qm, kj, NT) - c_ref[hh:hh + 1, pl.ds(off, width)]
                p = jnp.exp(s - lse_h)
                if diag:
                    row = i * ATT + lax.broadcasted_iota(jnp.int32, (ATT, width), 0)
                    col = off + lax.broadcasted_iota(jnp.int32, (ATT, width), 1)
                    p = jnp.where(col <= row, p, 0.0)
                ds = p * (_dot(dom, vj, NT) - delta)
                dc_ref[hh:hh + 1, pl.ds(off, width)] += -jnp.sum(ds, axis=0, keepdims=True)
                dsb = ds.astype(BF16)
                dk_h, dv_h = _dot(dsb, qm, TN), _dot(p.astype(BF16), dom, TN)
                dk, dv = (dk_h, dv_h) if dk is None else (dk + dk_h, dv + dv_h)
                out.append(dqs[hh] + _dot(dsb, kj))
            dk_ref[pl.ds(off, width), :] += dk
            dv_ref[pl.ds(off, width), :] += dv
            return tuple(out)

        zero = jnp.zeros((ATT, LANES), F32)
        dqs = lax.fori_loop(0, i // 2, lambda j, acc: step(j * (2 * ATT), 2 * ATT, acc, False), (zero, zero))
        dqs = lax.cond(i % 2 == 1, lambda acc: step((i - 1) * ATT, 2 * ATT, acc, True),
                       lambda acc: step(i * ATT, ATT, acc, True), dqs)
        dq_ref[...] = jnp.where(_head_mask(0), dqs[0], dqs[1]) * SCALE

        @pl.when(i == nq - 1)
        def _():
            dvb_ref[...] = dv_ref[...].astype(BF16)

        at_end()

    blk = pl.BlockSpec((ATT, LANES), lambda p, i: (i, p))
    full = pl.BlockSpec((S, LANES), lambda p, i: (0, p))
    cspec = pl.BlockSpec((None, 2, S), lambda p, i: (p, 0, 0))
    res = pl.pallas_call(
        body, name="fox_bwd", grid=(npair, nq),
        in_specs=[blk, full, full, cspec, blk, blk, blk, blk] + riding.in_specs,
        out_specs=[blk, full, full, cspec, blk] + riding.out_specs,
        out_shape=[jax.ShapeDtypeStruct((S, D), F32)] * 2 + [jax.ShapeDtypeStruct((S, D), BF16),
                                                              jax.ShapeDtypeStruct((npair, 2, S), F32),
                                                              jax.ShapeDtypeStruct((S, D), BF16)]
                  + riding.out_shape,
        scratch_shapes=[pltpu.VMEM((S, LANES), F32)] + riding.scratch,
        compiler_params=_params(2))(q, k, v, ct, o, lse, dy, gate, *riding.ins)
    return res[0], res[1], res[2], res[3], res[4], res[5:]


def _both_heads(x):
    return jnp.concatenate([jnp.where(_head_mask(hh), x, jnp.zeros_like(x)) for hh in (0, 1)], axis=0)


def _per_head(col0, col1):
    return jnp.concatenate([jnp.broadcast_to(col0, (WINDOW, 1)), jnp.broadcast_to(col1, (WINDOW, 1))], axis=0)


def _unstack(x2):
    return jnp.where(_head_mask(0), x2[:WINDOW], x2[WINDOW:])


def _swa_valid(i, start):
    r = lax.broadcasted_iota(jnp.int32, (2 * WINDOW, 2 * WINDOW), 0)
    qabs = i * WINDOW + jnp.where(r >= WINDOW, r - WINDOW, r)
    kabs = start + lax.broadcasted_iota(jnp.int32, (2 * WINDOW, 2 * WINDOW), 1)
    return (kabs <= qabs) & (qabs - kabs < WINDOW)


def _swa_fwd(q, kdup, vdup, sinks_t, proj, gate_col):
    def body(q_ref, k_ref, v_ref, sk_ref, gate_ref, o_ref, lse_ref, y_ref):
        skv = sk_ref[...]
        first = _head_mask(0)
        for sb in range(SWQ):
            i = pl.program_id(1) * SWQ + sb
            rows = slice(sb * WINDOW, (sb + 1) * WINDOW)
            start = pl.multiple_of(jnp.maximum(i - 1, 0) * WINDOW, WINDOW)
            kk, vv = k_ref[pl.ds(start, 2 * WINDOW), :], v_ref[pl.ds(start, 2 * WINDOW), :]
            q2 = q_ref[rows, :]
            valid = _swa_valid(i, start)[:WINDOW]
            res = []
            for hh in (0, 1):
                hm = _head_mask(hh)
                sink = jnp.max(jnp.where(hm, skv, NEG), axis=1, keepdims=True)
                s = jnp.where(valid, _dot(jnp.where(hm, q2, jnp.zeros_like(q2)), kk, NT), NEG)
                m = jnp.maximum(jnp.max(s, axis=1, keepdims=True), sink)
                p = jnp.exp(s - m)
                l = jnp.sum(p, axis=1, keepdims=True) + jnp.exp(sink - m)
                res.append((_dot(p.astype(BF16), vv) / l, m + jnp.log(l)))
            o = jnp.where(first, res[0][0], res[1][0])
            o_ref[rows, :] = o
            lse_ref[rows, :] = jnp.where(first, res[0][1], res[1][1])
            g = gate_ref[rows, :]
            y_ref[rows, :] = (o * (g * _sigmoid(g))).astype(BF16)

    blk = pl.BlockSpec((SWQ * WINDOW, LANES), lambda p, i: (i, p))
    gate = pl.BlockSpec((SWQ * WINDOW, LANES), lambda p, i: (i, gate_col + p))
    full = pl.BlockSpec((S, LANES), lambda p, i: (0, p // 2))
    return pl.pallas_call(
        body, name="swa_fwd", grid=(NH // 2, S // (SWQ * WINDOW)),
        in_specs=[blk, full, full, pl.BlockSpec((1, LANES), lambda p, i: (0, p)), gate],
        out_specs=[blk, blk, blk],
        out_shape=[jax.ShapeDtypeStruct((S, D), F32)] * 2 + [jax.ShapeDtypeStruct((S, D), BF16)],
        compiler_params=_params(2))(q, kdup, vdup, sinks_t, proj)


def _swa_bwd(q, kdup, vdup, sinks_t, o, lse, dy, proj, gate_col):
    def body(q_ref, k_ref, v_ref, sk_ref, o_ref, lse_ref, dy_ref, gate_ref, dq_ref, dk_ref, dv_ref, dsk_ref,
             dgate_ref):
        @pl.when(pl.program_id(1) == 0)
        def _():
            dk_ref[...] = jnp.zeros_like(dk_ref)
            dv_ref[...] = jnp.zeros_like(dv_ref)
            dsk_ref[...] = jnp.zeros_like(dsk_ref)

        skv = sk_ref[...]
        first = _head_mask(0)
        sink = _per_head(*[jnp.max(jnp.where(_head_mask(hh), skv, NEG), axis=1, keepdims=True) for hh in (0, 1)])
        for sb in range(SWQ):
            i = pl.program_id(1) * SWQ + sb
            rows = slice(sb * WINDOW, (sb + 1) * WINDOW)
            start = pl.multiple_of(jnp.maximum(i - 1, 0) * WINDOW, WINDOW)
            kk, vv = k_ref[pl.ds(start, 2 * WINDOW), :], v_ref[pl.ds(start, 2 * WINDOW), :]
            do2, dgate = _gate_grads(dy_ref[rows, :], o_ref[rows, :], gate_ref[rows, :])
            dgate_ref[rows, :] = dgate.astype(BF16)
            do2b = do2.astype(BF16)
            prod, lse2 = do2b.astype(F32) * o_ref[rows, :], lse_ref[rows, :]
            qs, dos = _both_heads(q_ref[rows, :]), _both_heads(do2b)
            delta = jnp.concatenate([jnp.sum(jnp.where(_head_mask(hh), prod, 0.0), axis=1, keepdims=True)
                                     for hh in (0, 1)], axis=0)
            lse_h = jnp.concatenate([jnp.max(jnp.where(_head_mask(hh), lse2, NEG), axis=1, keepdims=True)
                                     for hh in (0, 1)], axis=0)
            p = jnp.where(_swa_valid(i, start), jnp.exp(_dot(qs, kk, NT) - lse_h), 0.0)
            dsb = (p * (_dot(dos, vv, NT) - delta)).astype(BF16)
            dk_ref[pl.ds(start, 2 * WINDOW), :] += _dot(dsb, qs, TN)
            dv_ref[pl.ds(start, 2 * WINDOW), :] += _dot(p.astype(BF16), dos, TN)
            dq_ref[rows, :] = _unstack(_dot(dsb, kk)) * SCALE
            t = jnp.exp(sink - lse_h) * delta
            dsk_ref[...] += -jnp.where(first, jnp.sum(t[:WINDOW], axis=0, keepdims=True),
                                       jnp.sum(t[WINDOW:], axis=0, keepdims=True))

    blk = pl.BlockSpec((SWQ * WINDOW, LANES), lambda p, i: (i, p))
    full = pl.BlockSpec((S, LANES), lambda p, i: (0, p // 2))
    acc = pl.BlockSpec((S, LANES), lambda p, i: (0, p))
    sk = pl.BlockSpec((1, LANES), lambda p, i: (0, p))
    gate = pl.BlockSpec((SWQ * WINDOW, LANES), lambda p, i: (i, gate_col + p))
    return pl.pallas_call(
        body, name="swa_bwd", grid=(NH // 2, S // (SWQ * WINDOW)),
        in_specs=[blk, full, full, sk, blk, blk, blk, gate],
        out_specs=[blk, acc, acc, sk, blk],
        out_shape=[jax.ShapeDtypeStruct((S, D), F32)] * 3 + [jax.ShapeDtypeStruct((1, D), F32),
                                                              jax.ShapeDtypeStruct((S, D), BF16)],
        compiler_params=_params(2))(q, kdup, vdup, sinks_t, o, lse, dy, proj)


def _adamw_math(w, g, m, v):
    m = ADAM_B1 * m + (1.0 - ADAM_B1) * g
    v = ADAM_B2 * v + (1.0 - ADAM_B2) * jnp.square(g)
    m_hat = m / (1.0 - ADAM_B1 ** ADAM_STEP)
    v_hat = v / (1.0 - ADAM_B2 ** ADAM_STEP)
    delta = -ADAM_LR * (m_hat / (jnp.sqrt(v_hat) + ADAM_EPS) + ADAM_WD * w)
    return delta, m, v


def _adamw_small(ws, gs, ms, vs):
    k = len(ws)

    def body(*refs):
        for p in range(k):
            w_ref, g_ref, m_ref, v_ref = (refs[q * k + p] for q in range(4))
            d, mo, vo = _adamw_math(w_ref[...], g_ref[...], m_ref[...], v_ref[...])
            refs[4 * k + p][...], refs[5 * k + p][...], refs[6 * k + p][...] = d, mo, vo

    res = pl.pallas_call(
        body, name="adamw_small",
        out_shape=[jax.ShapeDtypeStruct(t.shape, F32) for t in ws] * 3)(*ws, *gs, *ms, *vs)
    return res[:k], res[k:2 * k], res[2 * k:]


SUM_TILES = (512, 256, 128)


FLAT_BLOCK = 257 * 1024


def _tiles(shape, axis, lead=0, halves=False):
    if len(shape) == 1:
        count = shape[0] // FLAT_BLOCK
        return (FLAT_BLOCK,), count, lambda pos, *lead_idx: (sum(k * count for k in lead_idx) + pos,)
    r, c = shape
    tile = next(t for t in SUM_TILES if (shape[axis] // (2 if halves else 1)) % t == 0)
    blk = (tile, c) if axis == 0 else (r, tile)
    count = shape[axis] // tile

    def index(pos, *lead_idx):
        return tuple(lead_idx) + ((pos, 0) if axis == 0 else (0, pos))

    return (None,) * lead + blk, count, index


def _adamw_halves(w, g_mine, g_theirs, m, v, axis, name):
    blk, count, index = _tiles(w.shape, axis, halves=True)
    per_half = count // 2

    def body(w_ref, a_ref, b_ref, m_ref, v_ref, g_ref, d_ref, mo_ref, vo_ref):
        is_mine = pl.program_id(0) // per_half == lax.axis_index("c")
        g = jnp.where(is_mine, a_ref[...], b_ref[...])
        g_ref[...] = g
        d_ref[...], mo_ref[...], vo_ref[...] = _adamw_math(w_ref[...], g, m_ref[...], v_ref[...])

    spec = pl.BlockSpec(blk, lambda i: index(i))
    half = pl.BlockSpec(blk, lambda i: index(i % per_half))
    return pl.pallas_call(
        body, name=name, grid=(count,), in_specs=[spec, half, half, spec, spec], out_specs=[spec] * 4,
        out_shape=[jax.ShapeDtypeStruct(w.shape, F32)] * 4, compiler_params=_params(1))(w, g_mine, g_theirs, m, v)


def _chip_sum(blocks, from_sibling, axis, name):
    flat = blocks.ndim == 1
    blk, count, index = _tiles((from_sibling.shape[0] // NCHIP,) if flat else from_sibling.shape[1:], axis, lead=1)

    def body(lo_ref, hi_ref, p_ref, o32, o16):
        mine = jnp.where(lax.axis_index("c") == 0, lo_ref[...], hi_ref[...])
        acc = mine + p_ref[...]
        o32[...] = acc
        o16[...] = acc.astype(BF16)

    half = pl.BlockSpec(blk, lambda k, i: index(i, k))
    if flat:
        lo = pl.BlockSpec(blk, lambda k, i: (2 * count * k + i,))
        hi = pl.BlockSpec(blk, lambda k, i: (2 * count * k + count + i,))
    else:
        lo, hi = half, pl.BlockSpec(blk, lambda k, i: index(i + count, k))
    return pl.pallas_call(
        body, name=name, grid=(NCHIP, count), in_specs=[lo, hi, half], out_specs=[half, half],
        out_shape=[jax.ShapeDtypeStruct(from_sibling.shape, F32), jax.ShapeDtypeStruct(from_sibling.shape, BF16)],
        compiler_params=_params(2))(blocks, blocks, from_sibling)


def _mesh_sum(own, parts, axis, name):
    blk, count, index = _tiles(own.shape, axis)
    n = NCHIP - 1

    def body(a_ref, *refs):
        acc = a_ref[...]
        for k in range(n):
            acc = acc + refs[k][...].astype(F32)
        refs[n][...] = acc

    spec = pl.BlockSpec(blk, lambda i: index(i))
    if own.ndim == 1:
        part = [pl.BlockSpec(blk, lambda i, k=k: (k * count + i,)) for k in range(n)]
    else:
        part = [pl.BlockSpec((None,) + blk, lambda i, k=k: (k,) + index(i)) for k in range(n)]
    return pl.pallas_call(
        body, name=name, grid=(count,), in_specs=[spec] + part,
        out_specs=spec, out_shape=jax.ShapeDtypeStruct(own.shape, F32),
        compiler_params=_params(1))(own, *([parts] * n))


def _sum_stack(parts, name):
    n = parts.shape[0]

    def body(p_ref, o_ref):
        acc = p_ref[0]
        for k in range(1, n):
            acc = acc + p_ref[k]
        o_ref[...] = acc

    return pl.pallas_call(body, name=name, out_shape=jax.ShapeDtypeStruct(parts.shape[1:], F32))(parts)


def _coords():
    return lax.axis_index("x"), lax.axis_index("y"), lax.axis_index("c")


def _chip(who):
    return 2 * who[0] + who[1]


def _flip(who, mask):
    return tuple((1 - v) if b else v for v, b in zip(who, mask))


def _transfer(transfers, t, I, O, ssem, rsem, receiving):
    tr, me = transfers[t], _coords()
    peer = _flip(me, tr["mask"])
    return pltpu.make_async_remote_copy(
        src_ref=tr["src"](I, O, me), dst_ref=tr["dst"](I, O, peer if receiving else me),
        send_sem=ssem.at[t], recv_sem=rsem.at[t], device_id=peer, device_id_type=MESH)


def _start_transfers(transfers, I, O, ssem, rsem, onward):
    arrived = set()
    for t, tr in enumerate(transfers):
        after = tr.get("after")
        if (after is not None) != onward:
            continue
        if after is not None and after not in arrived:
            _transfer(transfers, after, I, O, ssem, rsem, True).wait_recv()
            arrived.add(after)
        _transfer(transfers, t, I, O, ssem, rsem, False).start()


def _finish_transfers(transfers, I, O, ssem, rsem):
    passed_on = {tr["after"] for tr in transfers if tr.get("after") is not None}
    for t in range(len(transfers)):
        if t not in passed_on:
            _transfer(transfers, t, I, O, ssem, rsem, True).wait_recv()
    for t in range(len(transfers)):
        _transfer(transfers, t, I, O, ssem, rsem, False).wait_send()


def _own_copies(own, I, O, stage, lsem, leg):
    for n, (src, dst) in enumerate(own):
        me = _coords()
        bring =pltpu.make_async_copy(src(I, O, me), stage[n], lsem.at[2 * n])
        put = pltpu.make_async_copy(stage[n], dst(I, O, me), lsem.at[2 * n + 1])
        if leg == 0:
            bring.start()
        elif leg == 1:
            bring.wait()
            put.start()
        else:
            put.wait()


def _own_scratch(own, ins):
    return [pltpu.VMEM(ins[n].shape, ins[n].dtype) for n in range(len(own))], pltpu.SemaphoreType.DMA((max(2 * len(own), 1),))


def _exchange(name, ins, outs, transfers, own=()):
    ni, no = len(ins), len(outs)
    nt = len(transfers)
    stages, stage_sems = _own_scratch(own, ins)

    def body(*refs):
        I, O = refs[:ni], refs[ni:ni + no]
        ssem, rsem, lsem = refs[ni + no:ni + no + 3]
        stage = refs[ni + no + 3:]
        _own_copies(own, I, O, stage, lsem, 0)
        _start_transfers(transfers, I, O, ssem, rsem, False)
        _own_copies(own, I, O, stage, lsem, 1)
        _start_transfers(transfers, I, O, ssem, rsem, True)
        _finish_transfers(transfers, I, O, ssem, rsem)
        _own_copies(own, I, O, stage, lsem, 2)

    hbm = pl.BlockSpec(memory_space=pltpu.HBM)
    return pl.pallas_call(
        body, name=name, in_specs=[hbm] * ni, out_specs=[hbm] * no,
        out_shape=[jax.ShapeDtypeStruct(s, d) for s, d in outs],
        scratch_shapes=[pltpu.SemaphoreType.DMA((nt,)), pltpu.SemaphoreType.DMA((nt,)), stage_sems] + stages,
        compiler_params=pltpu.CompilerParams(has_side_effects=True, vmem_limit_bytes=VMEM_LIMIT))(*ins)


CHIP_MASKS = [(0, 1, 0), (1, 0, 0), (1, 1, 0)]
SIBLING = (0, 0, 1)


def _half(shape2d, axis, which):
    n = shape2d[axis] // 2
    cut = pl.ds(pl.multiple_of(which * n, n), n)
    return (cut, slice(None)) if axis == 0 else (slice(None), cut)


class _Riding:
    def __init__(self, transfers, ins, outs, own=()):
        self.transfers, self.ins, self.outs, self.own = transfers, list(ins), list(outs), list(own)
        hbm = pl.BlockSpec(memory_space=pltpu.HBM)
        self.in_specs, self.out_specs = [hbm] * len(self.ins), [hbm] * len(self.outs)
        self.out_shape = [jax.ShapeDtypeStruct(s, d) for s, d in self.outs]
        stages, stage_sems = _own_scratch(self.own, self.ins)
        self.scratch = [pltpu.SemaphoreType.DMA((max(len(transfers), 1),))] * 2 + [stage_sems] + stages

    def alone(self, name):
        return _exchange(name, self.ins, self.outs, self.transfers, self.own)

    def hooks(self, I, O, ssem, rsem, lsem, *stage, first, middle, last):
        tr, own = self.transfers, self.own

        @pl.when(first)
        def _():
            _own_copies(own, I, O, stage, lsem, 0)
            _start_transfers(tr, I, O, ssem, rsem, False)

        if own or any(t.get("after") is not None for t in tr):
            @pl.when(middle)
            def _():
                _own_copies(own, I, O, stage, lsem, 1)
                _start_transfers(tr, I, O, ssem, rsem, True)

        def at_end():
            @pl.when(last)
            def _():
                _finish_transfers(tr, I, O, ssem, rsem)
                _own_copies(own, I, O, stage, lsem, 2)

        return at_end


def _stretch(n, pos):
    return (pl.ds(pos * n if isinstance(pos, int) else pl.multiple_of(pos * n, n), n),)


def _gather_plan(shards, axes):
    def half(a, who):
        if shards[a].ndim == 1:
            return _stretch(shards[a].shape[0] // 2, who[2])
        return _half(shards[a].shape, axes[a], who[2])

    def landed(a, chip, who):
        if shards[a].ndim == 1:
            return _stretch(shards[a].shape[0] // 2, 2 * chip + who[2])
        return (chip,) + half(a, who)

    over_ici, onward = [], []
    for a in range(len(shards)):
        for mask in CHIP_MASKS:
            over_ici.append(dict(
                mask=mask,
                src=lambda I, O, me, a=a: I[a].at[half(a, me)],
                dst=lambda I, O, who, a=a: O[a].at[landed(a, _chip(who), who)]))
            onward.append(dict(
                mask=SIBLING, after=len(over_ici) - 1,
                src=lambda I, O, me, a=a, mask=mask: O[a].at[landed(a, _chip(_flip(me, mask)), me)],
                dst=lambda I, O, who, a=a, mask=mask: O[a].at[landed(a, _chip(_flip(who, mask)), who)]))
    outs = [((NCHIP * s.shape[0],) if s.ndim == 1 else (NCHIP,) + s.shape, s.dtype) for s in shards]

    def whole(a, chip):
        return _stretch(shards[a].shape[0], chip) if shards[a].ndim == 1 else (chip,)

    own = [(lambda I, O, me, a=a: I[a], lambda I, O, me, a=a: O[a].at[whole(a, _chip(me))])
           for a in range(len(shards))]
    return over_ici + onward, outs, own


def _gather_shards(shards, axes):
    transfers, outs, own = _gather_plan(shards, axes)
    return _exchange("gather_weights", shards, outs, transfers, own)


def _to_sibling(arrs, name):
    transfers = [dict(mask=SIBLING, src=lambda I, O, me, a=a: I[a], dst=lambda I, O, who, a=a: O[a])
                 for a in range(len(arrs))]
    return _exchange(name, arrs, [(t.shape, t.dtype) for t in arrs], transfers)


def _halves_plan(blocks, axes):
    def cut(a, which):
        return (slice(None),) + _half(blocks[a].shape[1:], axes[a], which)

    transfers, outs = [], []
    for a, (b, ax) in enumerate(zip(blocks, axes)):
        if b.ndim == 1:
            h = b.shape[0] // NCHIP // 2
            for k in range(NCHIP):
                transfers.append(dict(mask=SIBLING,
                                      src=lambda I, O, me, a=a, k=k, h=h: I[a].at[_stretch(h, 2 * k + 1 - me[2])],
                                      dst=lambda I, O, who, a=a, k=k, h=h: O[a].at[_stretch(h, k)]))
            outs.append(((NCHIP * h,), b.dtype))
        else:
            transfers.append(dict(mask=SIBLING, src=lambda I, O, me, a=a: I[a].at[cut(a, 1 - me[2])],
                                  dst=lambda I, O, who, a=a: O[a]))
            shape = list(b.shape)
            shape[ax + 1] //= 2
            outs.append((tuple(shape), b.dtype))
    return transfers, outs


def _scatter_plan(tb):
    def slot(a, k):
        return (k,) if tb[a].ndim == 3 else _stretch(tb[a].shape[0] // NCHIP, k)

    transfers = []
    for a in range(len(tb)):
        for n, mask in enumerate(CHIP_MASKS):
            transfers.append(dict(
                mask=mask,
                src=lambda I, O, me, a=a, mask=mask: I[a].at[slot(a, _chip(_flip(me, mask)))],
                dst=lambda I, O, who, a=a, n=n: O[a].at[slot(a, n)]))
    outs = [((3,) + t.shape[1:] if t.ndim == 3 else (3 * (t.shape[0] // NCHIP),), t.dtype) for t in tb]
    return transfers, outs


def _gather_small(vec):
    def slot(who):
        return 4 * who[0] + 2 * who[1] + who[2]

    masks = [(m >> 2 & 1, m >> 1 & 1, m & 1) for m in range(1, 8)]
    transfers = [dict(mask=mask, src=lambda I, O, me: I[0], dst=lambda I, O, who: O[0].at[slot(who)])
                 for mask in masks]
    own = [(lambda I, O, me: I[0], lambda I, O, me: O[0].at[slot(me)])]
    return _exchange("gather_small", [vec], [((8,) + vec.shape, vec.dtype)], transfers, own)[0]


def _rope_tables(positions):
    half = ROT // 2
    inv_freq = jnp.power(jnp.float32(THETA), -jnp.arange(0, ROT, 2, dtype=F32) / ROT)
    ang = positions.astype(F32)[:, None] * inv_freq[None, :]
    cos, sin = jnp.cos(ang), jnp.sin(ang)
    one, zero, z8 = jnp.ones((S, HD - ROT), F32), jnp.zeros((S, HD - ROT), F32), jnp.zeros((S, half), F32)
    c = jnp.concatenate([cos, cos, one], axis=1)
    a = jnp.concatenate([-sin, z8, zero], axis=1)
    b = jnp.concatenate([z8, sin, zero], axis=1)
    return tuple(jnp.tile(t, (1, 2)) for t in (c, a, b))


def _tile_heads(g, w):
    return jnp.tile(g.reshape(1, HD), (1, w // HD))


def _fold_heads(dg):
    return dg.reshape(-1, HD).sum(axis=0)


def _pad_lanes(a):
    return jnp.pad(a, ((0, 0), (0, LANES - a.shape[1])))


def _local_step(x, target, positions, wt, fetch, late_weights, begin_reduce):
    rope = _rope_tables(positions)
    w1t = wt["w_in_a_t"]
    f_row = 3 * D // LANES
    wg_t = w1t[3 * D + NH:]
    in_b_block = lambda c: pl.BlockSpec((None, TN_WIDE, TN_), lambda j, i: (c, j, 0))
    b_pad = _pad_lanes(wt["b_forget"].reshape(1, NH))
    qg_a, kg_a = _tile_heads(wt["qnorm_a_g"], D), _tile_heads(wt["knorm_a_g"], D)
    qg_b, kg_b = _tile_heads(wt["qnorm_b_g"], D), _tile_heads(wt["knorm_b_g"], KVW)
    norm_a, kv_g, norm_b = wt["norm_a_g"].reshape(1, D), wt["kv_norm_g"].reshape(1, D), wt["norm_b_g"].reshape(1, D)
    sinks_t = jnp.repeat(wt["sinks"].reshape(1, NH), HD, axis=1)

    (u_a,) = _rmsnorm_fwd(x, [norm_a], "norm_a")
    qkv = _mm("proj_a", S, 3 * D, [(u_a, _a_rows(D), w1t, _b_rows(D, tn=TN_WIDE), NT)], tn=TN_WIDE)
    fpad = _mm("proj_f", S, LANES, [(u_a, _a_rows(D), w1t, _b_rows(D, row0=f_row, tn=LANES), NT)], tn=LANES)
    gate_a = _mm("proj_gate_a", S, D, [(u_a, _a_rows(D), wg_t, _b_rows(D, tn=TN_WIDE), NT)], tn=TN_WIDE)
    q_a, k_a, v_a = _a_post(qkv, qg_a, kg_a)
    ct = _forget_cumsum(fpad, b_pad)
    ct2 = ct[:NH].reshape(NH // 2, 2, S)
    o_a, lse_a, y_a, fetched = _fox_fwd(q_a, k_a, v_a, ct2, gate_a, fetch)
    wt = {**wt, **late_weights(fetched)}
    w_in_b = wt["w_in_b"]
    h1 = _mm("out_a", S, D, [(y_a, _a_rows(D), wt["w_out_a"], _b_cols(D, tn=TN_WIDE), None)], add=x, tn=TN_WIDE)
    u_kv, u_b = _rmsnorm_fwd(h1, [kv_g, norm_b], "norm_b")
    kv = _mm("proj_kv", S, 2 * KVW, [(u_kv, _a_rows(D), wt["w_kv"], _b_cols(D), None)])
    pb = _mm("proj_b", S, 2 * D,
             [(u_b, _a_rows(D), w_in_b, pl.BlockSpec((None, D, TN_), lambda j, i: (j, 0, 0)), None)])
    q_b, kdup, vdup = _b_post(pb, kv, qg_b, kg_b, rope)
    gate_b_col = D // LANES
    o_b, lse_b, y_b = _swa_fwd(q_b, kdup, vdup, sinks_t, pb, gate_b_col)
    out = _mm("out_b", S, D, [(y_b, _a_rows(D), wt["w_out_b"], _b_cols(D, tn=TN_WIDE), None)], add=h1, tn=TN_WIDE)
    d_out, d_out_b, sq = _loss_head(out, target)

    g = {}
    g["w_out_b"] = _mm("dw_out_b", D, D, [(y_b, _a_cols(S), d_out_b, _b_cols(S, tn=TN_WIDE), TN)], tn=TN_WIDE)
    d_y_b = _mm("dy_b", S, D, [(d_out_b, _a_rows(D), wt["w_out_b"], _b_rows(D, tn=TN_WIDE), NT)], tn=TN_WIDE)
    dq_b, dkdup, dvdup, dsk, d_gate_b = _swa_bwd(q_b, kdup, vdup, sinks_t, o_b, lse_b, d_y_b, pb, gate_b_col)
    g["sinks"] = dsk[0, ::HD]
    d_qb_raw, dg = _headnorm_bwd(pb, 0, qg_b, dq_b, rope, "qnorm_b_bwd")
    g["qnorm_b_g"] = _fold_heads(dg)
    d_pb = [d_qb_raw, d_qb_raw, d_gate_b, d_gate_b]
    g["w_in_b"] = jnp.concatenate([
        _mm("dw_in_b_q", D, D, [(u_b, _a_cols(S), d_qb_raw, _b_cols(S), TN)], stacked=True),
        _mm("dw_in_b_gate", D, D, [(u_b, _a_cols(S), d_gate_b, _b_cols(S), TN)], stacked=True)], axis=0)
    d_u_b = _mm("du_b", S, D, [(d_pb[c], _a_rows(TN_, col=c % 2), w_in_b, in_b_block(c), NT) for c in range(NCHIP)],
                tn=TN_WIDE)
    d_kv, dg = _kv_bwd(dkdup, dvdup, kv, kg_b, rope)
    g["knorm_b_g"] = _fold_heads(dg)
    g["w_kv"] = _mm("dw_kv", D, 2 * KVW, [(u_kv, _a_cols(S), d_kv, _b_cols(S), TN)])
    d_u_kv = _mm("du_kv", S, D, [(d_kv, _a_rows(2 * KVW), wt["w_kv"], _b_rows(2 * KVW, tn=TN_WIDE), NT)], tn=TN_WIDE)
    d_h1, d_h1_b, g["kv_norm_g"], g["norm_b_g"] = _rmsnorm_bwd(h1, [kv_g, norm_b], [d_u_kv, d_u_b], d_out, "norm_b_bwd")
    g["w_out_a"] = _mm("dw_out_a", D, D, [(y_a, _a_cols(S), d_h1_b, _b_cols(S, tn=TN_WIDE), TN)], tn=TN_WIDE)
    late = {n: g[n] for n in LATE}
    d_y_a, halves = _mm("dy_a", S, D, [(d_h1_b, _a_rows(D), wt["w_out_a"], _b_rows(D, tn=TN_WIDE), NT)],
                        tn=TN_WIDE, riding=begin_reduce(late))
    riding, so_far = begin_reduce(late, halves)
    dq_a, dk_a, dv_a, dct, d_gate_a, arrived = _fox_bwd(q_a, k_a, v_a, ct2, o_a, lse_a, d_y_a, gate_a, riding)
    dct_pad = jnp.pad(dct.reshape(NH, S), ((0, LANES - NH), (0, 0)))
    d_f, db = _forget_bwd(dct_pad, fpad, b_pad)
    g["b_forget"] = db[0, :NH]
    d_q_raw, dg = _headnorm_bwd(qkv, 0, qg_a, dq_a, None, "qnorm_a_bwd")
    g["qnorm_a_g"] = _fold_heads(dg)
    d_k_raw, dg = _headnorm_bwd(qkv, 1, kg_a, dk_a, None, "knorm_a_bwd")
    g["knorm_a_g"] = _fold_heads(dg)
    rows, gw = 4 * D + NH, None
    for n, t, row0 in (("q", d_q_raw, 0), ("k", d_k_raw, D), ("v", dv_a, 2 * D)):
        gw = _mm("dw_in_a_" + n, D, D, [(t, _a_cols(S), u_a, _b_cols(S, tn=TN_WIDE), TN)], tn=TN_WIDE,
                 rows_of=(gw, rows, row0))
    gw = _mm("dw_in_a_f", LANES, D, [(d_f, _a_cols(S, tm=LANES), u_a, _b_cols(S, tn=TN_WIDE), TN)], tm=LANES,
             tn=TN_WIDE, rows_of=(gw, rows, 3 * D))
    g["w_in_a"] = _mm("dw_in_a_gate", D, D, [(d_gate_a, _a_cols(S), u_a, _b_cols(S, tn=TN_WIDE), TN)], tn=TN_WIDE,
                      rows_of=(gw, rows, 3 * D + NH))
    first = {"w_in_a": g["w_in_a"]}
    riding, so_far_first = begin_reduce(first, begin_reduce(first).alone("sibling_halves_w_in_a"))
    d_u_a, arrived_first = _mm("du_a", S, D, [
        (d_q_raw, _a_rows(D), w1t, _b_cols(D, row=0, tn=TN_WIDE), None),
        (d_k_raw, _a_rows(D), w1t, _b_cols(D, row=1, tn=TN_WIDE), None),
        (dv_a, _a_rows(D), w1t, _b_cols(D, row=2, tn=TN_WIDE), None),
        (d_gate_a, _a_rows(D), wg_t, _b_cols(D, tn=TN_WIDE), None),
        (d_f, _a_rows(LANES), w1t, _b_cols(LANES, row=f_row, tn=TN_WIDE), None)], tn=TN_WIDE, riding=riding)
    d_x, _, g["norm_a_g"] = _rmsnorm_bwd(x, [norm_a], [d_u_a], d_h1, "norm_a_bwd")
    return sq, d_x, g, (list(so_far_first) + list(so_far), list(arrived_first) + list(arrived))


BIG = ["w_in_a", "w_out_a", "w_kv", "w_in_b", "w_out_b"]
LATE = BIG[1:]
SPLIT = {"w_in_a": None, "w_out_a": 0, "w_kv": 0, "w_in_b": 0, "w_out_b": 0}
SMALL = ["norm_a_g", "b_forget", "qnorm_a_g", "knorm_a_g", "kv_norm_g", "knorm_b_g", "norm_b_g", "qnorm_b_g", "sinks"]
NAMES = ["norm_a_g", "w_in_a", "b_forget", "qnorm_a_g", "knorm_a_g", "w_out_a", "kv_norm_g", "w_kv", "knorm_b_g",
         "norm_b_g", "w_in_b", "qnorm_b_g", "sinks", "w_out_b"]


def _pack(vals):
    flat = []
    for v in vals:
        v = v.reshape(-1)
        flat.append(jnp.pad(v, (0, -v.shape[0] % LANES)))
    flat = jnp.concatenate(flat)
    flat = jnp.pad(flat, (0, -flat.shape[0] % (8 * LANES)))
    return flat.reshape(-1, LANES)


def _unpack(packed, shapes):
    flat, out, off = packed.reshape(-1), [], 0
    for s in shapes:
        n = int(np.prod(s))
        out.append(flat[off:off + n].reshape(s))
        off += n + (-n % LANES)
    return out


def kernel(x, positions, norm_a_g, w_in_a, b_forget, qnorm_a_g, knorm_a_g, w_out_a, kv_norm_g, w_kv, knorm_b_g, norm_b_g, w_in_b, qnorm_b_g, sinks, w_out_b, loss_target, m_norm_a_g, m_w_in_a, m_b_forget, m_qnorm_a_g, m_knorm_a_g, m_w_out_a, m_kv_norm_g, m_w_kv, m_knorm_b_g, m_norm_b_g, m_w_in_b, m_qnorm_b_g, m_sinks, m_w_out_b, v_norm_a_g, v_w_in_a, v_b_forget, v_qnorm_a_g, v_knorm_a_g, v_w_out_a, v_kv_norm_g, v_w_kv, v_knorm_b_g, v_norm_b_g, v_w_in_b, v_qnorm_b_g, v_sinks, v_w_out_b):
    w = dict(norm_a_g=norm_a_g, w_in_a=w_in_a, b_forget=b_forget, qnorm_a_g=qnorm_a_g, knorm_a_g=knorm_a_g,
             w_out_a=w_out_a, kv_norm_g=kv_norm_g, w_kv=w_kv, knorm_b_g=knorm_b_g, norm_b_g=norm_b_g,
             w_in_b=w_in_b, qnorm_b_g=qnorm_b_g, sinks=sinks, w_out_b=w_out_b)
    m = dict(norm_a_g=m_norm_a_g, w_in_a=m_w_in_a, b_forget=m_b_forget, qnorm_a_g=m_qnorm_a_g, knorm_a_g=m_knorm_a_g,
             w_out_a=m_w_out_a, kv_norm_g=m_kv_norm_g, w_kv=m_w_kv, knorm_b_g=m_knorm_b_g, norm_b_g=m_norm_b_g,
             w_in_b=m_w_in_b, qnorm_b_g=m_qnorm_b_g, sinks=m_sinks, w_out_b=m_w_out_b)
    v = dict(norm_a_g=v_norm_a_g, w_in_a=v_w_in_a, b_forget=v_b_forget, qnorm_a_g=v_qnorm_a_g, knorm_a_g=v_knorm_a_g,
             w_out_a=v_w_out_a, kv_norm_g=v_kv_norm_g, w_kv=v_w_kv, knorm_b_g=v_knorm_b_g, norm_b_g=v_norm_b_g,
             w_in_b=v_w_in_b, qnorm_b_g=v_qnorm_b_g, sinks=v_sinks, w_out_b=v_w_out_b)
    my_chip = 2 * lax.axis_index("x") + lax.axis_index("y")

    def shard2d(t, n):
        if n == "w_in_a":
            return jnp.transpose(t, (2, 0, 1)).reshape(-1)
        return t.reshape(t.shape[-2:])

    def unflat(t, n):
        return jnp.transpose(t.reshape(-1, 1, D), (1, 2, 0)) if n == "w_in_a" else t.reshape(w[n].shape)

    w2d = {n: shard2d(w[n], n) for n in BIG}

    norm_a_rows = jnp.broadcast_to(norm_a_g.reshape(1, D // NCHIP), (2 * SUBLANES, D // NCHIP))
    w1t, norm_rows = _gather_shards([w2d["w_in_a"].astype(BF16), norm_a_rows], [SPLIT["w_in_a"], 0])
    wt = {"w_in_a_t": w1t.reshape(-1, D), "norm_a_g": norm_rows[:, 0, :].reshape(1, D)}
    for n in SMALL[1:]:
        wt[n] = w[n]
    late_shards = [w2d[n].astype(BF16) for n in LATE]
    late_axes = [SPLIT[n] for n in LATE]
    transfers, outs, own = _gather_plan(late_shards, late_axes)
    fetch = _Riding(transfers, late_shards, outs, own)

    def late_weights(fetched):
        return {n: t if n == "w_in_b" else t.reshape(-1, t.shape[2]) for n, t in zip(LATE, fetched)}

    def as_blocks(t):
        if t.ndim == 3:
            return t
        return t.reshape(-1) if t.shape[0] % (SUBLANES * NCHIP) else t.reshape(NCHIP, -1, t.shape[1])

    def begin_reduce(grads, halves=None):
        names = list(grads)
        axes = [SPLIT[n] for n in names]
        blocks = [as_blocks(grads[n]) for n in names]
        if halves is None:
            transfers, outs = _halves_plan(blocks, axes)
            return _Riding(transfers, blocks, outs)
        sums = [_chip_sum(blk, part, ax, "chip_sum_" + n) for n, ax, blk, part in zip(names, axes, blocks, halves)]
        bf16 = [s[1] for s in sums]
        transfers, outs = _scatter_plan(bf16)
        return _Riding(transfers, bf16, outs), [s[0] for s in sums]

    sq, d_x, g, (chip_f32, arrived) = _local_step(x[0], loss_target[0], positions, wt, fetch, late_weights,
                                                  begin_reduce)

    small_shapes = [(D,), (NH,), (HD,), (HD,), (D,), (HD,), (D,), (HD,), (NH,), (D,)]
    packed = _pack([g[n] for n in SMALL] + [sq])
    total = _sum_stack(_gather_small(packed), "sum_small")
    small_g = dict(zip(SMALL, _unpack(total, small_shapes)[:-1]))
    loss = 0.5 * jnp.sum(_unpack(total, small_shapes)[-1]) / D
    small_g["norm_a_g"] = lax.dynamic_slice(small_g["norm_a_g"], (my_chip * (D // NCHIP),), (D // NCHIP,))

    axes = [SPLIT[n] for n in BIG]
    halves = []
    for n, ax, t32, parts in zip(BIG, axes, chip_f32, arrived):
        if t32.ndim == 1:
            own = lax.dynamic_slice_in_dim(t32, my_chip * (t32.shape[0] // NCHIP), t32.shape[0] // NCHIP)
        else:
            own = lax.dynamic_index_in_dim(t32, my_chip, axis=0, keepdims=False)
        halves.append(_mesh_sum(own, parts, ax, "mesh_sum_" + n))
    sibling_done = _to_sibling(halves, "finished_halves")

    res = {}
    for n, ax, mine_half, their_half in zip(BIG, axes, halves, sibling_done):
        out4 = _adamw_halves(w2d[n], mine_half, their_half, shard2d(m[n], n), shard2d(v[n], n), ax, "adamw_" + n)
        res[n] = tuple(unflat(t, n) for t in out4)
    row = lambda t: t.reshape(1, -1)
    small_out = _adamw_small(*[[row(d[n]) for n in SMALL] for d in (w, small_g, m, v)])
    for i, n in enumerate(SMALL):
        res[n] = tuple(t.reshape(w[n].shape) for t in (small_g[n],) + tuple(out[i] for out in small_out))

    outs = [loss, d_x[None]]
    for k in range(4):
        outs += [res[n][k] for n in NAMES]
    return tuple(outs)
```

```python
import numpy as np
import jax
import jax.numpy as jnp
from jax import lax
from jax.experimental import pallas as pl
from jax.experimental.pallas import tpu as pltpu

F32, BF16 = jnp.float32, jnp.bfloat16
S, D, HD, NH, NKV = 2048, 1024, 64, 16, 4
KVW = NKV * HD
WINDOW = 128
ROT = HD // 4
THETA = 500000.0
EPS = 1e-6
SCALE = HD ** -0.5
LANES = 128
SUBLANES = 8
NEG = -1e30
VMEM_LIMIT = 48 * 2 ** 20
ROWS = 512
ATT = 512
SWQ = 16
NCHIP = 4
ADAM_LR, ADAM_B1, ADAM_B2, ADAM_EPS, ADAM_WD, ADAM_STEP = 0.001, 0.9, 0.999, 1e-08, 0.01, 10
NT = (((1,), (1,)), ((), ()))
TN = (((0,), (0,)), ((), ()))
MESH = pl.DeviceIdType.MESH


def _params(n):
    return pltpu.CompilerParams(dimension_semantics=("arbitrary",) * n, vmem_limit_bytes=VMEM_LIMIT)


def _dot(a, b, dims=None):
    if dims is None:
        return jnp.dot(a, b, preferred_element_type=F32)
    return lax.dot_general(a, b, dims, preferred_element_type=F32)


def _dot_split(a, b, n):
    out, rest = None, a
    for _ in range(n):
        hi = rest.astype(BF16)
        term = _dot(hi, b)
        out = term if out is None else out + term
        rest = rest - hi.astype(F32)
    return out


def _seg_mat(w):
    e = (np.arange(w)[:, None] // HD == np.arange(LANES)[None, :]).astype(np.float32)
    return jnp.asarray(e, BF16)


def _spread(r, w):
    head = lax.broadcasted_iota(jnp.int32, (2 * LANES, w), 1) >> (HD.bit_length() - 1)
    row = lax.broadcasted_iota(jnp.int32, (2 * LANES, w), 0)
    et2 = jnp.where(head == (row & (LANES - 1)), 1.0, 0.0).astype(BF16)
    hi = r.astype(BF16)
    lo = (r - hi.astype(F32)).astype(BF16)
    return _dot(jnp.concatenate([hi, lo], axis=1), et2)


def _head_rstd(x, e):
    ss = _dot_split(x * x, e, 2)
    return _spread(lax.rsqrt(ss * (1.0 / HD) + EPS), x.shape[1])


def _rope(x, c, a, b):
    w = x.shape[1]
    return x * c + pltpu.roll(x, w - ROT // 2, 1) * a + pltpu.roll(x, ROT // 2, 1) * b


def _rope_t(dy, c, a, b):
    w = dy.shape[1]
    return dy * c + pltpu.roll(dy * b, w - ROT // 2, 1) + pltpu.roll(dy * a, ROT // 2, 1)


def _sigmoid(x):
    return 1.0 / (1.0 + jnp.exp(-x))


def _row_spec(shape, ts):
    nd = len(shape)
    if shape[0] == S:
        return pl.BlockSpec((ts,) + tuple(shape[1:]), lambda i: (i,) + (0,) * (nd - 1))
    return pl.BlockSpec(tuple(shape), lambda i: (0,) * nd)


def _rows_call(body, name, ins, outs, ts=ROWS):
    return pl.pallas_call(
        body, name=name, grid=(S // ts,),
        in_specs=[_row_spec(a.shape, ts) for a in ins],
        out_specs=[_row_spec(s, ts) for s, _ in outs],
        out_shape=[jax.ShapeDtypeStruct(s, d) for s, d in outs],
        compiler_params=_params(1))(*ins)


def _col_spec(ts, w, col):
    return pl.BlockSpec((ts, w), lambda i: (i, col))


TM = TN_ = 512
TM_TOKENS = 1024
TN_WIDE = 1024


def _mm(name, m, n, terms, out_dtype=F32, add=None, tm=None, tn=TN_, stacked=False, riding=None, rows_of=None):
    nterm = len(terms)
    if tm is None:
        tm = TM_TOKENS if m == S else TM
    nj, ni_ = n // tn, m // tm
    n_in = 2 * nterm + (add is not None) + (rows_of is not None and rows_of[0] is not None)
    r_in, r_out = (len(riding.ins), len(riding.outs)) if riding is not None else (0, 0)

    def body(*refs):
        if riding is not None:
            j, i = pl.program_id(0), pl.program_id(1)
            at_end = riding.hooks(refs[n_in:n_in + r_in], refs[n_in + r_in + 1:n_in + r_in + 1 + r_out],
                                  *refs[n_in + r_in + 1 + r_out:], first=(j == 0) & (i == 0),
                                  middle=(j == nj // 2) & (i == 0), last=(j == nj - 1) & (i == ni_ - 1))
        acc = None
        for t in range(nterm):
            part = _dot(refs[2 * t][...], refs[2 * t + 1][...], terms[t][4])
            acc = part if acc is None else acc + part
        if add is not None:
            acc = acc + refs[2 * nterm][...]
        refs[n_in + r_in][...] = acc.astype(out_dtype)
        if riding is not None:
            at_end()

    tile = pl.BlockSpec((tm, tn), lambda j, i: (i, j))
    ins, specs = [], []
    for a, a_spec, b, b_spec, _ in terms:
        ins += [a, b]
        specs += [a_spec, b_spec]
    if add is not None:
        ins.append(add)
        specs.append(tile)
    out_spec = pl.BlockSpec((None, tm, tn), lambda j, i: (j, i, 0)) if stacked else tile
    out_shape = jax.ShapeDtypeStruct((nj, m, tn) if stacked else (m, n), out_dtype)
    if rows_of is not None:
        taller, rows, row0 = rows_of
        out_spec = pl.BlockSpec((pl.Element(tm), pl.Element(tn)), lambda j, i: (
            pl.multiple_of(row0 + i * tm, SUBLANES), pl.multiple_of(j * tn, LANES)))
        out_shape = jax.ShapeDtypeStruct((rows, n), out_dtype)
        alias = {}
        if taller is not None:
            ins.append(taller)
            specs.append(pl.BlockSpec(memory_space=pltpu.HBM))
            alias = {len(ins) - 1: 0}
        return pl.pallas_call(body, name=name, grid=(nj, ni_), in_specs=specs, out_specs=out_spec,
                              out_shape=out_shape, input_output_aliases=alias, compiler_params=_params(2))(*ins)
    if riding is None:
        return pl.pallas_call(body, name=name, grid=(nj, ni_), in_specs=specs, out_specs=out_spec,
                              out_shape=out_shape, compiler_params=_params(2))(*ins)
    res = pl.pallas_call(
        body, name=name, grid=(nj, ni_), in_specs=specs + riding.in_specs,
        out_specs=[out_spec] + riding.out_specs, out_shape=[out_shape] + riding.out_shape,
        scratch_shapes=riding.scratch, compiler_params=_params(2))(*ins, *riding.ins)
    return res[0], res[1:]


def _a_rows(k, col=0, tm=TM_TOKENS):
    return pl.BlockSpec((tm, k), lambda j, i: (i, col))


def _a_cols(k, tm=TM):
    return pl.BlockSpec((k, tm), lambda j, i: (0, i))


def _b_cols(k, row=0, col0=0, tn=TN_):
    return pl.BlockSpec((k, tn), lambda j, i: (row, col0 + j))


def _b_rows(k, row0=0, tn=TN_):
    return pl.BlockSpec((tn, k), lambda j, i: (row0 + j, 0))


def _rmsnorm_fwd(x, gains, name):
    def body(*refs):
        xv = refs[0][...]
        r = lax.rsqrt(jnp.mean(xv * xv, axis=-1, keepdims=True) + EPS)
        xh = xv * r
        for n in range(len(gains)):
            refs[1 + len(gains) + n][...] = (xh * refs[1 + n][...]).astype(BF16)

    return _rows_call(body, name, [x] + list(gains), [((S, D), BF16)] * len(gains))


def _rmsnorm_bwd(x, gains, dus, dres, name):
    n = len(gains)

    def body(*refs):
        x_ref, g_refs, du_refs, dres_ref = refs[0], refs[1:1 + n], refs[1 + n:1 + 2 * n], refs[1 + 2 * n]
        dx_ref, dxb_ref, dg_refs = refs[2 + 2 * n], refs[3 + 2 * n], refs[4 + 2 * n:]
        xv = x_ref[...]
        r = lax.rsqrt(jnp.mean(xv * xv, axis=-1, keepdims=True) + EPS)
        xh = xv * r
        gy = None
        for m in range(n):
            du = du_refs[m][...]
            part = jnp.sum(du * xh, axis=0, keepdims=True)

            @pl.when(pl.program_id(0) == 0)
            def _(m=m, part=part):
                dg_refs[m][...] = part

            @pl.when(pl.program_id(0) != 0)
            def _(m=m, part=part):
                dg_refs[m][...] += part

            t = du * g_refs[m][...]
            gy = t if gy is None else gy + t
        dx = dres_ref[...] + r * (gy - xh * jnp.mean(gy * xh, axis=-1, keepdims=True))
        dx_ref[...] = dx
        dxb_ref[...] = dx.astype(BF16)

    outs = [((S, D), F32), ((S, D), BF16)] + [((1, D), F32)] * n
    return _rows_call(body, name, [x] + list(gains) + list(dus) + [dres], outs)


def _a_post(qkvg, qg, kg):
    e = _seg_mat(D)

    def body(q_ref, k_ref, v_ref, qg_ref, kg_ref, e_ref, qo, ko, vo):
        ev = e_ref[...]
        qv, kv = q_ref[...], k_ref[...]
        qo[...] = (qv * _head_rstd(qv, ev) * qg_ref[...] * SCALE).astype(BF16)
        ko[...] = (kv * _head_rstd(kv, ev) * kg_ref[...]).astype(BF16)
        vo[...] = v_ref[...].astype(BF16)

    whole = lambda a: pl.BlockSpec(a.shape, lambda i: (0, 0))
    return pl.pallas_call(
        body, name="a_post", grid=(S // ROWS,),
        in_specs=[_col_spec(ROWS, D, 0), _col_spec(ROWS, D, 1), _col_spec(ROWS, D, 2),
                  whole(qg), whole(kg), whole(e)],
        out_specs=[_col_spec(ROWS, D, 0)] * 3,
        out_shape=[jax.ShapeDtypeStruct((S, D), BF16)] * 3,
        compiler_params=_params(1))(qkvg, qkvg, qkvg, qg, kg, e)


def _tri(upper):
    r, c = np.arange(ROWS)[:, None], np.arange(ROWS)[None, :]
    return jnp.asarray((r <= c) if upper else (r >= c), BF16)


def _forget_cumsum(fpad, bpad):
    def body(f_ref, b_ref, u_ref, c_ref, carry):
        @pl.when(pl.program_id(0) == 0)
        def _():
            carry[...] = jnp.zeros_like(carry)

        lf = jax.nn.log_sigmoid(f_ref[...] + b_ref[...])
        blk = _dot_split(lf.T, u_ref[...], 3) + carry[:, 0:1]
        c_ref[...] = blk
        carry[...] = jnp.broadcast_to(blk[:, ROWS - 1:ROWS], carry.shape)

    return pl.pallas_call(
        body, name="forget_cumsum", grid=(S // ROWS,),
        in_specs=[pl.BlockSpec((ROWS, LANES), lambda i: (i, 0)), pl.BlockSpec((1, LANES), lambda i: (0, 0)),
                  pl.BlockSpec((ROWS, ROWS), lambda i: (0, 0))],
        out_specs=pl.BlockSpec((LANES, ROWS), lambda i: (0, i)),
        out_shape=jax.ShapeDtypeStruct((LANES, S), F32),
        scratch_shapes=[pltpu.VMEM((LANES, LANES), F32)],
        compiler_params=_params(1))(fpad, bpad, _tri(True))


def _forget_bwd(dct, fpad, bpad):
    nb = S // ROWS

    def body(dc_ref, f_ref, b_ref, l_ref, df_ref, db_ref, carry):
        @pl.when(pl.program_id(0) == 0)
        def _():
            carry[...] = jnp.zeros_like(carry)
            db_ref[...] = jnp.zeros_like(db_ref)

        blk = _dot_split(dc_ref[...], l_ref[...], 3) + carry[:, 0:1]
        carry[...] = jnp.broadcast_to(blk[:, 0:1], carry.shape)
        df = blk.T * _sigmoid(-(f_ref[...] + b_ref[...]))
        df_ref[...] = df.astype(BF16)
        db_ref[...] += jnp.sum(df, axis=0, keepdims=True)

    return pl.pallas_call(
        body, name="forget_bwd", grid=(nb,),
        in_specs=[pl.BlockSpec((LANES, ROWS), lambda i: (0, nb - 1 - i)),
                  pl.BlockSpec((ROWS, LANES), lambda i: (nb - 1 - i, 0)),
                  pl.BlockSpec((1, LANES), lambda i: (0, 0)), pl.BlockSpec((ROWS, ROWS), lambda i: (0, 0))],
        out_specs=[pl.BlockSpec((ROWS, LANES), lambda i: (nb - 1 - i, 0)), pl.BlockSpec((1, LANES), lambda i: (0, 0))],
        out_shape=[jax.ShapeDtypeStruct((S, LANES), BF16), jax.ShapeDtypeStruct((1, LANES), F32)],
        scratch_shapes=[pltpu.VMEM((LANES, LANES), F32)],
        compiler_params=_params(1))(dct, fpad, bpad, _tri(False))


def _headnorm_bwd(x, col, gain, dy, rope, name):
    e = _seg_mat(D)
    tabs = list(rope) if rope is not None else []

    def body(*refs):
        x_ref, g_ref, dy_ref, e_ref = refs[:4]
        dx_ref, dg_ref = refs[-2:]
        xv, dyv, ev = x_ref[...], dy_ref[...], e_ref[...]
        if rope is not None:
            c, a, b = (jnp.tile(t[...], (1, D // LANES)) for t in refs[4:7])
            dyv = _rope_t(dyv, c, a, b)
        r = _head_rstd(xv, ev)
        xh = xv * r
        part = jnp.sum(dyv * xh, axis=0, keepdims=True)

        @pl.when(pl.program_id(0) == 0)
        def _():
            dg_ref[...] = part

        @pl.when(pl.program_id(0) != 0)
        def _():
            dg_ref[...] += part

        gy = dyv * g_ref[...]
        seg = _spread(_dot_split(gy * xh, ev, 2) * (1.0 / HD), D)
        dx_ref[...] = (r * (gy - xh * seg)).astype(BF16)

    whole = lambda a: pl.BlockSpec(a.shape, lambda i: (0, 0))
    return pl.pallas_call(
        body, name=name, grid=(S // ROWS,),
        in_specs=[_col_spec(ROWS, D, col), whole(gain), _col_spec(ROWS, D, 0), whole(e)]
                 + [pl.BlockSpec((ROWS, LANES), lambda i: (i, 0))] * len(tabs),
        out_specs=[_col_spec(ROWS, D, 0), whole(gain)],
        out_shape=[jax.ShapeDtypeStruct((S, D), BF16), jax.ShapeDtypeStruct((1, D), F32)],
        compiler_params=_params(1))(x, gain, dy, e, *tabs)


def _dup_mat():
    r, c = np.arange(KVW)[:, None], np.arange(2 * KVW)[None, :]
    return (r // HD == c // LANES) & (r % HD == c % HD)


def _fold_mat():
    r, c = np.arange(D)[:, None], np.arange(KVW)[None, :]
    return (r // (2 * LANES) == c // HD) & (r % HD == c % HD)


def _b_post(pb, kv, qg, kg, rope):
    e, ek = _seg_mat(D), _seg_mat(KVW)
    dup = jnp.asarray(_dup_mat(), BF16)

    def body(q_ref, k_ref, v_ref, qg_ref, kg_ref, e_ref, ek_ref, dup_ref, c_ref, a_ref, b_ref, qo, ko, vo):
        c1, a1, b1 = c_ref[...], a_ref[...], b_ref[...]
        qv = q_ref[...]
        qn = qv * _head_rstd(qv, e_ref[...]) * qg_ref[...]
        t = lambda z, n: jnp.tile(z, (1, n))
        qo[...] = (_rope(qn, t(c1, D // LANES), t(a1, D // LANES), t(b1, D // LANES)) * SCALE).astype(BF16)
        kvv = k_ref[...]
        kn = kvv * _head_rstd(kvv, ek_ref[...]) * kg_ref[...]
        kr = _rope(kn, t(c1, KVW // LANES), t(a1, KVW // LANES), t(b1, KVW // LANES)).astype(BF16)
        ko[...] = _dot(kr, dup_ref[...]).astype(BF16)
        vo[...] = _dot(v_ref[...].astype(BF16), dup_ref[...]).astype(BF16)

    whole = lambda a: pl.BlockSpec(a.shape, lambda i: (0, 0))
    tab = pl.BlockSpec((ROWS, LANES), lambda i: (i, 0))
    return pl.pallas_call(
        body, name="b_post", grid=(S // ROWS,),
        in_specs=[_col_spec(ROWS, D, 0), _col_spec(ROWS, KVW, 0), _col_spec(ROWS, KVW, 1),
                  whole(qg), whole(kg), whole(e), whole(ek), whole(dup), tab, tab, tab],
        out_specs=[_col_spec(ROWS, D, 0), _col_spec(ROWS, 2 * KVW, 0), _col_spec(ROWS, 2 * KVW, 0)],
        out_shape=[jax.ShapeDtypeStruct((S, D), BF16), jax.ShapeDtypeStruct((S, 2 * KVW), BF16),
                   jax.ShapeDtypeStruct((S, 2 * KVW), BF16)],
        compiler_params=_params(1))(pb, kv, kv, qg, kg, e, ek, dup, *rope)


def _kv_bwd(dkdup, dvdup, kv, kg, rope):
    ek = _seg_mat(KVW)
    fold = jnp.asarray(_fold_mat(), BF16)

    def body(dk_ref, dv_ref, k_ref, kg_ref, ek_ref, fold_ref, c_ref, a_ref, b_ref, dkv_ref, dg_ref):
        ev, fv = ek_ref[...], fold_ref[...]
        t = lambda z: jnp.tile(z[...], (1, KVW // LANES))
        dk = _rope_t(_dot_split(dk_ref[...], fv, 2), t(c_ref), t(a_ref), t(b_ref))
        dv = _dot_split(dv_ref[...], fv, 2)
        xv = k_ref[...]
        r = _head_rstd(xv, ev)
        xh = xv * r
        part = jnp.sum(dk * xh, axis=0, keepdims=True)

        @pl.when(pl.program_id(0) == 0)
        def _():
            dg_ref[...] = part

        @pl.when(pl.program_id(0) != 0)
        def _():
            dg_ref[...] += part

        gy = dk * kg_ref[...]
        seg = _spread(_dot_split(gy * xh, ev, 2) * (1.0 / HD), KVW)
        dkv_ref[:, 0:KVW] = (r * (gy - xh * seg)).astype(BF16)
        dkv_ref[:, KVW:2 * KVW] = dv.astype(BF16)

    whole = lambda a: pl.BlockSpec(a.shape, lambda i: (0, 0))
    tab = pl.BlockSpec((ROWS, LANES), lambda i: (i, 0))
    return pl.pallas_call(
        body, name="kv_bwd", grid=(S // ROWS,),
        in_specs=[_col_spec(ROWS, D, 0), _col_spec(ROWS, D, 0), _col_spec(ROWS, KVW, 0),
                  whole(kg), whole(ek), whole(fold), tab, tab, tab],
        out_specs=[_col_spec(ROWS, 2 * KVW, 0), whole(kg)],
        out_shape=[jax.ShapeDtypeStruct((S, 2 * KVW), BF16), jax.ShapeDtypeStruct((1, KVW), F32)],
        compiler_params=_params(1))(dkdup, dvdup, kv, kg, ek, fold, *rope)


def _loss_head(out, target):
    def body(o_ref, t_ref, d_ref, db_ref, l_ref):
        diff = o_ref[...] - t_ref[...]
        d = diff * (1.0 / D)
        d_ref[...] = d
        db_ref[...] = d.astype(BF16)

        @pl.when(pl.program_id(0) == 0)
        def _():
            l_ref[...] = jnp.zeros_like(l_ref)

        l_ref[...] += jnp.sum(diff * diff, axis=0, keepdims=True)

    return _rows_call(body, "loss_head", [out, target], [((S, D), F32), ((S, D), BF16), ((1, D), F32)])


def _lane():
    return lax.broadcasted_iota(jnp.int32, (1, LANES), 1)


def _head_mask(hh):
    return (_lane() < HD) if hh == 0 else (_lane() >= HD)


def _fox_fwd(q, k, v, ct, gate, riding):
    nq, npair = S // ATT, NH // 2
    ni, no = len(riding.ins), len(riding.outs)

    def body(q_ref, k_ref, v_ref, c_ref, gate_ref, *rest):
        o_ref, lse_ref, y_ref = rest[ni:ni + 3]
        pair, i = pl.program_id(0), pl.program_id(1)
        at_end = riding.hooks(rest[:ni], rest[ni + 3:ni + 3 + no], *rest[ni + 3 + no:],
                              first=(pair == 0) & (i == 0), middle=(pair == npair // 2) & (i == 0),
                              last=(pair == npair - 1) & (i == nq - 1))
        q2 = q_ref[...]
        qms = [jnp.where(_head_mask(hh), q2, jnp.zeros_like(q2)) for hh in (0, 1)]

        def probs(off, width, m, hh, diag):
            s = _dot(qms[hh], k_ref[pl.ds(off, width), :], NT) - c_ref[hh:hh + 1, pl.ds(off, width)]
            if diag:
                row = i * ATT + lax.broadcasted_iota(jnp.int32, (ATT, width), 0)
                col = off + lax.broadcasted_iota(jnp.int32, (ATT, width), 1)
                s = jnp.where(col <= row, s, NEG)
            m_new = jnp.maximum(m, jnp.max(s, axis=1, keepdims=True))
            p = jnp.exp(s - m_new)
            p_hi = p.astype(BF16)
            return m_new, jnp.exp(m - m_new), p_hi, (p - p_hi.astype(F32)).astype(BF16)

        def weighted(off, width, p_hi, p_lo, hh):
            vj = v_ref[pl.ds(off, width), :]
            v1 = jnp.where(_head_mask(hh), vj, jnp.ones_like(vj))
            return _dot(p_hi, v1) + _dot(p_lo, v1)

        def step(off, width, carry, diag):
            off = pl.multiple_of(off, ATT)
            out = []
            for hh in (0, 1):
                m, acc = carry[hh]
                m, alpha, p_hi, p_lo = probs(off, width, m, hh, diag)
                out.append((m, alpha * acc + weighted(off, width, p_hi, p_lo, hh)))
            return tuple(out)

        one = (jnp.full((ATT, 1), NEG, F32), jnp.zeros((ATT, LANES), F32))
        carry = lax.fori_loop(0, i // 2, lambda j, cr: step(j * (2 * ATT), 2 * ATT, cr, False), (one, one))
        carry = lax.cond(i % 2 == 1, lambda cr: step((i - 1) * ATT, 2 * ATT, cr, True),
                         lambda cr: step(i * ATT, ATT, cr, True), carry)
        res = []
        for hh in (0, 1):
            m, acc = carry[hh]
            l = jnp.max(jnp.where(_head_mask(1 - hh), acc, 0.0), axis=1, keepdims=True)
            res.append((acc / l, m + jnp.log(l)))
        first = _head_mask(0)
        o = jnp.where(first, res[0][0], res[1][0])
        o_ref[...] = o
        lse_ref[...] = jnp.where(first, res[0][1], res[1][1])
        g = gate_ref[...]
        y_ref[...] = (o * (g * _sigmoid(g))).astype(BF16)
        at_end()

    blk = pl.BlockSpec((ATT, LANES), lambda p, i: (i, p))
    full = pl.BlockSpec((S, LANES), lambda p, i: (0, p))
    res = pl.pallas_call(
        body, name="fox_fwd", grid=(npair, nq),
        in_specs=[blk, full, full, pl.BlockSpec((None, 2, S), lambda p, i: (p, 0, 0)), blk] + riding.in_specs,
        out_specs=[blk, blk, blk] + riding.out_specs,
        out_shape=[jax.ShapeDtypeStruct((S, D), F32)] * 2 + [jax.ShapeDtypeStruct((S, D), BF16)] + riding.out_shape,
        scratch_shapes=riding.scratch,
        compiler_params=_params(2))(q, k, v, ct, gate, *riding.ins)
    return res[0], res[1], res[2], res[3:]


def _gate_grads(dy, o, g):
    sg = _sigmoid(g)
    return dy * (g * sg), dy * o * (sg * (1.0 + g * (1.0 - sg)))


def _fox_bwd(q, k, v, ct, o, lse, dy, gate, riding):
    nq, npair = S // ATT, NH // 2
    ni, no = len(riding.ins), len(riding.outs)

    def body(q_ref, k_ref, v_ref, c_ref, o_ref, lse_ref, dy_ref, gate_ref, *rest):
        dq_ref, dk_ref, dvb_ref, dc_ref, dgate_ref = rest[ni:ni + 5]
        dv_ref = rest[ni + 5 + no]
        pair, i = pl.program_id(0), pl.program_id(1)
        at_end = riding.hooks(rest[:ni], rest[ni + 5:ni + 5 + no], *rest[ni + 6 + no:],
                              first=(pair == 0) & (i == 0), middle=(pair == npair // 2) & (i == 0),
                              last=(pair == npair - 1) & (i == nq - 1))

        @pl.when(i == 0)
        def _():
            dk_ref[...] = jnp.zeros_like(dk_ref)
            dv_ref[...] = jnp.zeros_like(dv_ref)
            dc_ref[...] = jnp.zeros_like(dc_ref)

        q2, lse2 = q_ref[...], lse_ref[...]
        do2, dgate = _gate_grads(dy_ref[...], o_ref[...], gate_ref[...])
        dgate_ref[...] = dgate.astype(BF16)
        do2b = do2.astype(BF16)
        prod = do2b.astype(F32) * o_ref[...]
        heads = []
        for hh in (0, 1):
            hm = _head_mask(hh)
            heads.append((jnp.where(hm, q2, jnp.zeros_like(q2)), jnp.where(hm, do2b, jnp.zeros_like(do2b)),
                          jnp.sum(jnp.where(hm, prod, 0.0), axis=1, keepdims=True),
                          jnp.max(jnp.where(hm, lse2, NEG), axis=1, keepdims=True)))

        def step(off, width, dqs, diag):
            off = pl.multiple_of(off, ATT)
            kj, vj = k_ref[pl.ds(off, width), :], v_ref[pl.ds(off, width), :]
            dk, dv, out = None, None, []
            for hh in (0, 1):
                qm, dom, delta, lse_h = heads[hh]
                s = _dot(qm, kj, NT) - c_ref[hh:hh + 1, pl.ds(off, width)]
                p = jnp.exp(s - lse_h)
                if diag:
                    row = i * ATT + lax.broadcasted_iota(jnp.int32, (ATT, width), 0)
                    col = off + lax.broadcasted_iota(jnp.int32, (ATT, width), 1)
                    p = jnp.where(col <= row, p, 0.0)
                ds = p * (_dot(dom, vj, NT) - delta)
                dc_ref[hh:hh + 1, pl.ds(off, width)] += -jnp.sum(ds, axis=0, keepdims=True)
                dsb = ds.astype(BF16)
                dk_h, dv_h = _dot(dsb, qm, TN), _dot(p.astype(BF16), dom, TN)
                dk, dv = (dk_h, dv_h) if dk is None else (dk + dk_h, dv + dv_h)
                out.append(dqs[hh] + _dot(dsb, kj))
            dk_ref[pl.ds(off, width), :] += dk
            dv_ref[pl.ds(off, width), :] += dv
            return tuple(out)

        zero = jnp.zeros((ATT, LANES), F32)
        dqs = lax.fori_loop(0, i // 2, lambda j, acc: step(j * (2 * ATT), 2 * ATT, acc, False), (zero, zero))
        dqs = lax.cond(i % 2 == 1, lambda acc: step((i - 1) * ATT, 2 * ATT, acc, True),
                       lambda acc: step(i * ATT, ATT, acc, True), dqs)
        dq_ref[...] = jnp.where(_head_mask(0), dqs[0], dqs[1]) * SCALE

        @pl.when(i == nq - 1)
        def _():
            dvb_ref[...] = dv_ref[...].astype(BF16)

        at_end()

    blk = pl.BlockSpec((ATT, LANES), lambda p, i: (i, p))
    full = pl.BlockSpec((S, LANES), lambda p, i: (0, p))
    cspec = pl.BlockSpec((None, 2, S), lambda p, i: (p, 0, 0))
    res = pl.pallas_call(
        body, name="fox_bwd", grid=(npair, nq),
        in_specs=[blk, full, full, cspec, blk, blk, blk, blk] + riding.in_specs,
        out_specs=[blk, full, full, cspec, blk] + riding.out_specs,
        out_shape=[jax.ShapeDtypeStruct((S, D), F32)] * 2 + [jax.ShapeDtypeStruct((S, D), BF16),
                                                              jax.ShapeDtypeStruct((npair, 2, S), F32),
                                                              jax.ShapeDtypeStruct((S, D), BF16)]
                  + riding.out_shape,
        scratch_shapes=[pltpu.VMEM((S, LANES), F32)] + riding.scratch,
        compiler_params=_params(2))(q, k, v, ct, o, lse, dy, gate, *riding.ins)
    return res[0], res[1], res[2], res[3], res[4], res[5:]


def _both_heads(x):
    return jnp.concatenate([jnp.where(_head_mask(hh), x, jnp.zeros_like(x)) for hh in (0, 1)], axis=0)


def _per_head(col0, col1):
    return jnp.concatenate([jnp.broadcast_to(col0, (WINDOW, 1)), jnp.broadcast_to(col1, (WINDOW, 1))], axis=0)


def _unstack(x2):
    return jnp.where(_head_mask(0), x2[:WINDOW], x2[WINDOW:])


def _swa_valid(i, start):
    r = lax.broadcasted_iota(jnp.int32, (2 * WINDOW, 2 * WINDOW), 0)
    qabs = i * WINDOW + jnp.where(r >= WINDOW, r - WINDOW, r)
    kabs = start + lax.broadcasted_iota(jnp.int32, (2 * WINDOW, 2 * WINDOW), 1)
    return (kabs <= qabs) & (qabs - kabs < WINDOW)


def _swa_fwd(q, kdup, vdup, sinks_t, proj, gate_col):
    def body(q_ref, k_ref, v_ref, sk_ref, gate_ref, o_ref, lse_ref, y_ref):
        skv = sk_ref[...]
        first = _head_mask(0)
        for sb in range(SWQ):
            i = pl.program_id(1) * SWQ + sb
            rows = slice(sb * WINDOW, (sb + 1) * WINDOW)
            start = pl.multiple_of(jnp.maximum(i - 1, 0) * WINDOW, WINDOW)
            kk, vv = k_ref[pl.ds(start, 2 * WINDOW), :], v_ref[pl.ds(start, 2 * WINDOW), :]
            q2 = q_ref[rows, :]
            valid = _swa_valid(i, start)[:WINDOW]
            res = []
            for hh in (0, 1):
                hm = _head_mask(hh)
                sink = jnp.max(jnp.where(hm, skv, NEG), axis=1, keepdims=True)
                s = jnp.where(valid, _dot(jnp.where(hm, q2, jnp.zeros_like(q2)), kk, NT), NEG)
                m = jnp.maximum(jnp.max(s, axis=1, keepdims=True), sink)
                p = jnp.exp(s - m)
                l = jnp.sum(p, axis=1, keepdims=True) + jnp.exp(sink - m)
                res.append((_dot(p.astype(BF16), vv) / l, m + jnp.log(l)))
            o = jnp.where(first, res[0][0], res[1][0])
            o_ref[rows, :] = o
            lse_ref[rows, :] = jnp.where(first, res[0][1], res[1][1])
            g = gate_ref[rows, :]
            y_ref[rows, :] = (o * (g * _sigmoid(g))).astype(BF16)

    blk = pl.BlockSpec((SWQ * WINDOW, LANES), lambda p, i: (i, p))
    gate = pl.BlockSpec((SWQ * WINDOW, LANES), lambda p, i: (i, gate_col + p))
    full = pl.BlockSpec((S, LANES), lambda p, i: (0, p // 2))
    return pl.pallas_call(
        body, name="swa_fwd", grid=(NH // 2, S // (SWQ * WINDOW)),
        in_specs=[blk, full, full, pl.BlockSpec((1, LANES), lambda p, i: (0, p)), gate],
        out_specs=[blk, blk, blk],
        out_shape=[jax.ShapeDtypeStruct((S, D), F32)] * 2 + [jax.ShapeDtypeStruct((S, D), BF16)],
        compiler_params=_params(2))(q, kdup, vdup, sinks_t, proj)


def _swa_bwd(q, kdup, vdup, sinks_t, o, lse, dy, proj, gate_col):
    def body(q_ref, k_ref, v_ref, sk_ref, o_ref, lse_ref, dy_ref, gate_ref, dq_ref, dk_ref, dv_ref, dsk_ref,
             dgate_ref):
        @pl.when(pl.program_id(1) == 0)
        def _():
            dk_ref[...] = jnp.zeros_like(dk_ref)
            dv_ref[...] = jnp.zeros_like(dv_ref)
            dsk_ref[...] = jnp.zeros_like(dsk_ref)

        skv = sk_ref[...]
        first = _head_mask(0)
        sink = _per_head(*[jnp.max(jnp.where(_head_mask(hh), skv, NEG), axis=1, keepdims=True) for hh in (0, 1)])
        for sb in range(SWQ):
            i = pl.program_id(1) * SWQ + sb
            rows = slice(sb * WINDOW, (sb + 1) * WINDOW)
            start = pl.multiple_of(jnp.maximum(i - 1, 0) * WINDOW, WINDOW)
            kk, vv = k_ref[pl.ds(start, 2 * WINDOW), :], v_ref[pl.ds(start, 2 * WINDOW), :]
            do2, dgate = _gate_grads(dy_ref[rows, :], o_ref[rows, :], gate_ref[rows, :])
            dgate_ref[rows, :] = dgate.astype(BF16)
            do2b = do2.astype(BF16)
            prod, lse2 = do2b.astype(F32) * o_ref[rows, :], lse_ref[rows, :]
            qs, dos = _both_heads(q_ref[rows, :]), _both_heads(do2b)
            delta = jnp.concatenate([jnp.sum(jnp.where(_head_mask(hh), prod, 0.0), axis=1, keepdims=True)
                                     for hh in (0, 1)], axis=0)
            lse_h = jnp.concatenate([jnp.max(jnp.where(_head_mask(hh), lse2, NEG), axis=1, keepdims=True)
                                     for hh in (0, 1)], axis=0)
            p = jnp.where(_swa_valid(i, start), jnp.exp(_dot(qs, kk, NT) - lse_h), 0.0)
            dsb = (p * (_dot(dos, vv, NT) - delta)).astype(BF16)
            dk_ref[pl.ds(start, 2 * WINDOW), :] += _dot(dsb, qs, TN)
            dv_ref[pl.ds(start, 2 * WINDOW), :] += _dot(p.astype(BF16), dos, TN)
            dq_ref[rows, :] = _unstack(_dot(dsb, kk)) * SCALE
            t = jnp.exp(sink - lse_h) * delta
            dsk_ref[...] += -jnp.where(first, jnp.sum(t[:WINDOW], axis=0, keepdims=True),
                                       jnp.sum(t[WINDOW:], axis=0, keepdims=True))

    blk = pl.BlockSpec((SWQ * WINDOW, LANES), lambda p, i: (i, p))
    full = pl.BlockSpec((S, LANES), lambda p, i: (0, p // 2))
    acc = pl.BlockSpec((S, LANES), lambda p, i: (0, p))
    sk = pl.BlockSpec((1, LANES), lambda p, i: (0, p))
    gate = pl.BlockSpec((SWQ * WINDOW, LANES), lambda p, i: (i, gate_col + p))
    return pl.pallas_call(
        body, name="swa_bwd", grid=(NH // 2, S // (SWQ * WINDOW)),
        in_specs=[blk, full, full, sk, blk, blk, blk, gate],
        out_specs=[blk, acc, acc, sk, blk],
        out_shape=[jax.ShapeDtypeStruct((S, D), F32)] * 3 + [jax.ShapeDtypeStruct((1, D), F32),
                                                              jax.ShapeDtypeStruct((S, D), BF16)],
        compiler_params=_params(2))(q, kdup, vdup, sinks_t, o, lse, dy, proj)


def _adamw_math(w, g, m, v):
    m = ADAM_B1 * m + (1.0 - ADAM_B1) * g
    v = ADAM_B2 * v + (1.0 - ADAM_B2) * jnp.square(g)
    m_hat = m / (1.0 - ADAM_B1 ** ADAM_STEP)
    v_hat = v / (1.0 - ADAM_B2 ** ADAM_STEP)
    delta = -ADAM_LR * (m_hat / (jnp.sqrt(v_hat) + ADAM_EPS) + ADAM_WD * w)
    return delta, m, v


def _adamw_small(ws, gs, ms, vs):
    k = len(ws)

    def body(*refs):
        for p in range(k):
            w_ref, g_ref, m_ref, v_ref = (refs[q * k + p] for q in range(4))
            d, mo, vo = _adamw_math(w_ref[...], g_ref[...], m_ref[...], v_ref[...])
            refs[4 * k + p][...], refs[5 * k + p][...], refs[6 * k + p][...] = d, mo, vo

    res = pl.pallas_call(
        body, name="adamw_small",
        out_shape=[jax.ShapeDtypeStruct(t.shape, F32) for t in ws] * 3)(*ws, *gs, *ms, *vs)
    return res[:k], res[k:2 * k], res[2 * k:]


SUM_TILES = (512, 256, 128)


FLAT_BLOCK = 257 * 1024


def _tiles(shape, axis, lead=0, halves=False):
    if len(shape) == 1:
        count = shape[0] // FLAT_BLOCK
        return (FLAT_BLOCK,), count, lambda pos, *lead_idx: (sum(k * count for k in lead_idx) + pos,)
    r, c = shape
    tile = next(t for t in SUM_TILES if (shape[axis] // (2 if halves else 1)) % t == 0)
    blk = (tile, c) if axis == 0 else (r, tile)
    count = shape[axis] // tile

    def index(pos, *lead_idx):
        return tuple(lead_idx) + ((pos, 0) if axis == 0 else (0, pos))

    return (None,) * lead + blk, count, index


def _adamw_halves(w, g_mine, g_theirs, m, v, axis, name):
    blk, count, index = _tiles(w.shape, axis, halves=True)
    per_half = count // 2

    def body(w_ref, a_ref, b_ref, m_ref, v_ref, g_ref, d_ref, mo_ref, vo_ref):
        is_mine = pl.program_id(0) // per_half == lax.axis_index("c")
        g = jnp.where(is_mine, a_ref[...], b_ref[...])
        g_ref[...] = g
        d_ref[...], mo_ref[...], vo_ref[...] = _adamw_math(w_ref[...], g, m_ref[...], v_ref[...])

    spec = pl.BlockSpec(blk, lambda i: index(i))
    half = pl.BlockSpec(blk, lambda i: index(i % per_half))
    return pl.pallas_call(
        body, name=name, grid=(count,), in_specs=[spec, half, half, spec, spec], out_specs=[spec] * 4,
        out_shape=[jax.ShapeDtypeStruct(w.shape, F32)] * 4, compiler_params=_params(1))(w, g_mine, g_theirs, m, v)


def _chip_sum(blocks, from_sibling, axis, name):
    flat = blocks.ndim == 1
    blk, count, index = _tiles((from_sibling.shape[0] // NCHIP,) if flat else from_sibling.shape[1:], axis, lead=1)

    def body(lo_ref, hi_ref, p_ref, o32, o16):
        mine = jnp.where(lax.axis_index("c") == 0, lo_ref[...], hi_ref[...])
        acc = mine + p_ref[...]
        o32[...] = acc
        o16[...] = acc.astype(BF16)

    half = pl.BlockSpec(blk, lambda k, i: index(i, k))
    if flat:
        lo = pl.BlockSpec(blk, lambda k, i: (2 * count * k + i,))
        hi = pl.BlockSpec(blk, lambda k, i: (2 * count * k + count + i,))
    else:
        lo, hi = half, pl.BlockSpec(blk, lambda k, i: index(i + count, k))
    return pl.pallas_call(
        body, name=name, grid=(NCHIP, count), in_specs=[lo, hi, half], out_specs=[half, half],
        out_shape=[jax.ShapeDtypeStruct(from_sibling.shape, F32), jax.ShapeDtypeStruct(from_sibling.shape, BF16)],
        compiler_params=_params(2))(blocks, blocks, from_sibling)


def _mesh_sum(own, parts, axis, name):
    blk, count, index = _tiles(own.shape, axis)
    n = NCHIP - 1

    def body(a_ref, *refs):
        acc = a_ref[...]
        for k in range(n):
            acc = acc + refs[k][...].astype(F32)
        refs[n][...] = acc

    spec = pl.BlockSpec(blk, lambda i: index(i))
    if own.ndim == 1:
        part = [pl.BlockSpec(blk, lambda i, k=k: (k * count + i,)) for k in range(n)]
    else:
        part = [pl.BlockSpec((None,) + blk, lambda i, k=k: (k,) + index(i)) for k in range(n)]
    return pl.pallas_call(
        body, name=name, grid=(count,), in_specs=[spec] + part,
        out_specs=spec, out_shape=jax.ShapeDtypeStruct(own.shape, F32),
        compiler_params=_params(1))(own, *([parts] * n))


def _sum_stack(parts, name):
    n = parts.shape[0]

    def body(p_ref, o_ref):
        acc = p_ref[0]
        for k in range(1, n):
            acc = acc + p_ref[k]
        o_ref[...] = acc

    return pl.pallas_call(body, name=name, out_shape=jax.ShapeDtypeStruct(parts.shape[1:], F32))(parts)


def _coords():
    return lax.axis_index("x"), lax.axis_index("y"), lax.axis_index("c")


def _chip(who):
    return 2 * who[0] + who[1]


def _flip(who, mask):
    return tuple((1 - v) if b else v for v, b in zip(who, mask))


def _transfer(transfers, t, I, O, ssem, rsem, receiving):
    tr, me = transfers[t], _coords()
    peer = _flip(me, tr["mask"])
    return pltpu.make_async_remote_copy(
        src_ref=tr["src"](I, O, me), dst_ref=tr["dst"](I, O, peer if receiving else me),
        send_sem=ssem.at[t], recv_sem=rsem.at[t], device_id=peer, device_id_type=MESH)


def _start_transfers(transfers, I, O, ssem, rsem, onward):
    arrived = set()
    for t, tr in enumerate(transfers):
        after = tr.get("after")
        if (after is not None) != onward:
            continue
        if after is not None and after not in arrived:
            _transfer(transfers, after, I, O, ssem, rsem, True).wait_recv()
            arrived.add(after)
        _transfer(transfers, t, I, O, ssem, rsem, False).start()


def _finish_transfers(transfers, I, O, ssem, rsem):
    passed_on = {tr["after"] for tr in transfers if tr.get("after") is not None}
    for t in range(len(transfers)):
        if t not in passed_on:
            _transfer(transfers, t, I, O, ssem, rsem, True).wait_recv()
    for t in range(len(transfers)):
        _transfer(transfers, t, I, O, ssem, rsem, False).wait_send()


def _own_copies(own, I, O, stage, lsem, leg):
    for n, (src, dst) in enumerate(own):
        me = _coords()
        bring =pltpu.make_async_copy(src(I, O, me), stage[n], lsem.at[2 * n])
        put = pltpu.make_async_copy(stage[n], dst(I, O, me), lsem.at[2 * n + 1])
        if leg == 0:
            bring.start()
        elif leg == 1:
            bring.wait()
            put.start()
        else:
            put.wait()


def _own_scratch(own, ins):
    return [pltpu.VMEM(ins[n].shape, ins[n].dtype) for n in range(len(own))], pltpu.SemaphoreType.DMA((max(2 * len(own), 1),))


def _exchange(name, ins, outs, transfers, own=()):
    ni, no = len(ins), len(outs)
    nt = len(transfers)
    stages, stage_sems = _own_scratch(own, ins)

    def body(*refs):
        I, O = refs[:ni], refs[ni:ni + no]
        ssem, rsem, lsem = refs[ni + no:ni + no + 3]
        stage = refs[ni + no + 3:]
        _own_copies(own, I, O, stage, lsem, 0)
        _start_transfers(transfers, I, O, ssem, rsem, False)
        _own_copies(own, I, O, stage, lsem, 1)
        _start_transfers(transfers, I, O, ssem, rsem, True)
        _finish_transfers(transfers, I, O, ssem, rsem)
        _own_copies(own, I, O, stage, lsem, 2)

    hbm = pl.BlockSpec(memory_space=pltpu.HBM)
    return pl.pallas_call(
        body, name=name, in_specs=[hbm] * ni, out_specs=[hbm] * no,
        out_shape=[jax.ShapeDtypeStruct(s, d) for s, d in outs],
        scratch_shapes=[pltpu.SemaphoreType.DMA((nt,)), pltpu.SemaphoreType.DMA((nt,)), stage_sems] + stages,
        compiler_params=pltpu.CompilerParams(has_side_effects=True, vmem_limit_bytes=VMEM_LIMIT))(*ins)


CHIP_MASKS = [(0, 1, 0), (1, 0, 0), (1, 1, 0)]
SIBLING = (0, 0, 1)


def _half(shape2d, axis, which):
    n = shape2d[axis] // 2
    cut = pl.ds(pl.multiple_of(which * n, n), n)
    return (cut, slice(None)) if axis == 0 else (slice(None), cut)


class _Riding:
    def __init__(self, transfers, ins, outs, own=()):
        self.transfers, self.ins, self.outs, self.own = transfers, list(ins), list(outs), list(own)
        hbm = pl.BlockSpec(memory_space=pltpu.HBM)
        self.in_specs, self.out_specs = [hbm] * len(self.ins), [hbm] * len(self.outs)
        self.out_shape = [jax.ShapeDtypeStruct(s, d) for s, d in self.outs]
        stages, stage_sems = _own_scratch(self.own, self.ins)
        self.scratch = [pltpu.SemaphoreType.DMA((max(len(transfers), 1),))] * 2 + [stage_sems] + stages

    def alone(self, name):
        return _exchange(name, self.ins, self.outs, self.transfers, self.own)

    def hooks(self, I, O, ssem, rsem, lsem, *stage, first, middle, last):
        tr, own = self.transfers, self.own

        @pl.when(first)
        def _():
            _own_copies(own, I, O, stage, lsem, 0)
            _start_transfers(tr, I, O, ssem, rsem, False)

        if own or any(t.get("after") is not None for t in tr):
            @pl.when(middle)
            def _():
                _own_copies(own, I, O, stage, lsem, 1)
                _start_transfers(tr, I, O, ssem, rsem, True)

        def at_end():
            @pl.when(last)
            def _():
                _finish_transfers(tr, I, O, ssem, rsem)
                _own_copies(own, I, O, stage, lsem, 2)

        return at_end


def _stretch(n, pos):
    return (pl.ds(pos * n if isinstance(pos, int) else pl.multiple_of(pos * n, n), n),)


def _gather_plan(shards, axes):
    def half(a, who):
        if shards[a].ndim == 1:
            return _stretch(shards[a].shape[0] // 2, who[2])
        return _half(shards[a].shape, axes[a], who[2])

    def landed(a, chip, who):
        if shards[a].ndim == 1:
            return _stretch(shards[a].shape[0] // 2, 2 * chip + who[2])
        return (chip,) + half(a, who)

    over_ici, onward = [], []
    for a in range(len(shards)):
        for mask in CHIP_MASKS:
            over_ici.append(dict(
                mask=mask,
                src=lambda I, O, me, a=a: I[a].at[half(a, me)],
                dst=lambda I, O, who, a=a: O[a].at[landed(a, _chip(who), who)]))
            onward.append(dict(
                mask=SIBLING, after=len(over_ici) - 1,
                src=lambda I, O, me, a=a, mask=mask: O[a].at[landed(a, _chip(_flip(me, mask)), me)],
                dst=lambda I, O, who, a=a, mask=mask: O[a].at[landed(a, _chip(_flip(who, mask)), who)]))
    outs = [((NCHIP * s.shape[0],) if s.ndim == 1 else (NCHIP,) + s.shape, s.dtype) for s in shards]

    def whole(a, chip):
        return _stretch(shards[a].shape[0], chip) if shards[a].ndim == 1 else (chip,)

    own = [(lambda I, O, me, a=a: I[a], lambda I, O, me, a=a: O[a].at[whole(a, _chip(me))])
           for a in range(len(shards))]
    return over_ici + onward, outs, own


def _gather_shards(shards, axes):
    transfers, outs, own = _gather_plan(shards, axes)
    return _exchange("gather_weights", shards, outs, transfers, own)


def _halves_plan(blocks, axes):
    def cut(a, which):
        return (slice(None),) + _half(blocks[a].shape[1:], axes[a], which)

    transfers, outs = [], []
    for a, (b, ax) in enumerate(zip(blocks, axes)):
        if b.ndim == 1:
            h = b.shape[0] // NCHIP // 2
            for k in range(NCHIP):
                transfers.append(dict(mask=SIBLING,
                                      src=lambda I, O, me, a=a, k=k, h=h: I[a].at[_stretch(h, 2 * k + 1 - me[2])],
                                      dst=lambda I, O, who, a=a, k=k, h=h: O[a].at[_stretch(h, k)]))
            outs.append(((NCHIP * h,), b.dtype))
        else:
            transfers.append(dict(mask=SIBLING, src=lambda I, O, me, a=a: I[a].at[cut(a, 1 - me[2])],
                                  dst=lambda I, O, who, a=a: O[a]))
            shape = list(b.shape)
            shape[ax + 1] //= 2
            outs.append((tuple(shape), b.dtype))
    return transfers, outs


def _scatter_plan(tb):
    def slot(a, k):
        return (k,) if tb[a].ndim == 3 else _stretch(tb[a].shape[0] // NCHIP, k)

    transfers = []
    for a in range(len(tb)):
        for n, mask in enumerate(CHIP_MASKS):
            transfers.append(dict(
                mask=mask,
                src=lambda I, O, me, a=a, mask=mask: I[a].at[slot(a, _chip(_flip(me, mask)))],
                dst=lambda I, O, who, a=a, n=n: O[a].at[slot(a, n)]))
    outs = [((3,) + t.shape[1:] if t.ndim == 3 else (3 * (t.shape[0] // NCHIP),), t.dtype) for t in tb]
    return transfers, outs


def _last_exchange(vec, halves):
    def slot(who):
        return 4 * who[0] + 2 * who[1] + who[2]

    masks = [(m >> 2 & 1, m >> 1 & 1, m & 1) for m in range(1, 8)]
    transfers = [dict(mask=mask, src=lambda I, O, me: I[0], dst=lambda I, O, who: O[0].at[slot(who)])
                 for mask in masks]
    transfers += [dict(mask=SIBLING, src=lambda I, O, me, a=a: I[a], dst=lambda I, O, who, a=a: O[a])
                  for a in range(1, 1 + len(halves))]
    own = [(lambda I, O, me: I[0], lambda I, O, me: O[0].at[slot(me)])]
    outs = [((8,) + vec.shape, vec.dtype)] + [(t.shape, t.dtype) for t in halves]
    res = _exchange("last_exchange", [vec] + list(halves), outs, transfers, own)
    return res[0], res[1:]


def _rope_tables(positions):
    half = ROT // 2
    inv_freq = jnp.power(jnp.float32(THETA), -jnp.arange(0, ROT, 2, dtype=F32) / ROT)
    ang = positions.astype(F32)[:, None] * inv_freq[None, :]
    cos, sin = jnp.cos(ang), jnp.sin(ang)
    one, zero, z8 = jnp.ones((S, HD - ROT), F32), jnp.zeros((S, HD - ROT), F32), jnp.zeros((S, half), F32)
    c = jnp.concatenate([cos, cos, one], axis=1)
    a = jnp.concatenate([-sin, z8, zero], axis=1)
    b = jnp.concatenate([z8, sin, zero], axis=1)
    return tuple(jnp.tile(t, (1, 2)) for t in (c, a, b))


def _tile_heads(g, w):
    return jnp.tile(g.reshape(1, HD), (1, w // HD))


def _fold_heads(dg):
    return dg.reshape(-1, HD).sum(axis=0)


def _pad_lanes(a):
    return jnp.pad(a, ((0, 0), (0, LANES - a.shape[1])))


def _local_step(x, target, positions, wt, fetch, late_weights, begin_reduce):
    rope = _rope_tables(positions)
    w1t = wt["w_in_a_t"]
    f_row = 3 * D // LANES
    wg_t = w1t[3 * D + NH:]
    in_b_block = lambda c: pl.BlockSpec((None, TN_WIDE, TN_), lambda j, i: (c, j, 0))
    b_pad = _pad_lanes(wt["b_forget"].reshape(1, NH))
    qg_a, kg_a = _tile_heads(wt["qnorm_a_g"], D), _tile_heads(wt["knorm_a_g"], D)
    qg_b, kg_b = _tile_heads(wt["qnorm_b_g"], D), _tile_heads(wt["knorm_b_g"], KVW)
    norm_a, kv_g, norm_b = wt["norm_a_g"].reshape(1, D), wt["kv_norm_g"].reshape(1, D), wt["norm_b_g"].reshape(1, D)
    sinks_t = jnp.repeat(wt["sinks"].reshape(1, NH), HD, axis=1)

    (u_a,) = _rmsnorm_fwd(x, [norm_a], "norm_a")
    qkv = _mm("proj_a", S, 3 * D, [(u_a, _a_rows(D), w1t, _b_rows(D, tn=TN_WIDE), NT)], tn=TN_WIDE)
    fpad = _mm("proj_f", S, LANES, [(u_a, _a_rows(D), w1t, _b_rows(D, row0=f_row, tn=LANES), NT)], tn=LANES)
    gate_a = _mm("proj_gate_a", S, D, [(u_a, _a_rows(D), wg_t, _b_rows(D, tn=TN_WIDE), NT)], tn=TN_WIDE)
    q_a, k_a, v_a = _a_post(qkv, qg_a, kg_a)
    ct = _forget_cumsum(fpad, b_pad)
    ct2 = ct[:NH].reshape(NH // 2, 2, S)
    o_a, lse_a, y_a, fetched = _fox_fwd(q_a, k_a, v_a, ct2, gate_a, fetch)
    wt = {**wt, **late_weights(fetched)}
    w_in_b = wt["w_in_b"]
    h1 = _mm("out_a", S, D, [(y_a, _a_rows(D), wt["w_out_a"], _b_cols(D, tn=TN_WIDE), None)], add=x, tn=TN_WIDE)
    u_kv, u_b = _rmsnorm_fwd(h1, [kv_g, norm_b], "norm_b")
    kv = _mm("proj_kv", S, 2 * KVW, [(u_kv, _a_rows(D), wt["w_kv"], _b_cols(D), None)])
    pb = _mm("proj_b", S, 2 * D,
             [(u_b, _a_rows(D), w_in_b, pl.BlockSpec((None, D, TN_), lambda j, i: (j, 0, 0)), None)])
    q_b, kdup, vdup = _b_post(pb, kv, qg_b, kg_b, rope)
    gate_b_col = D // LANES
    o_b, lse_b, y_b = _swa_fwd(q_b, kdup, vdup, sinks_t, pb, gate_b_col)
    out = _mm("out_b", S, D, [(y_b, _a_rows(D), wt["w_out_b"], _b_cols(D, tn=TN_WIDE), None)], add=h1, tn=TN_WIDE)
    d_out, d_out_b, sq = _loss_head(out, target)

    g = {}
    g["w_out_b"] = _mm("dw_out_b", D, D, [(y_b, _a_cols(S), d_out_b, _b_cols(S, tn=TN_WIDE), TN)], tn=TN_WIDE)
    d_y_b = _mm("dy_b", S, D, [(d_out_b, _a_rows(D), wt["w_out_b"], _b_rows(D, tn=TN_WIDE), NT)], tn=TN_WIDE)
    dq_b, dkdup, dvdup, dsk, d_gate_b = _swa_bwd(q_b, kdup, vdup, sinks_t, o_b, lse_b, d_y_b, pb, gate_b_col)
    g["sinks"] = dsk[0, ::HD]
    d_qb_raw, dg = _headnorm_bwd(pb, 0, qg_b, dq_b, rope, "qnorm_b_bwd")
    g["qnorm_b_g"] = _fold_heads(dg)
    d_pb = [d_qb_raw, d_qb_raw, d_gate_b, d_gate_b]
    g["w_in_b"] = jnp.concatenate([
        _mm("dw_in_b_q", D, D, [(u_b, _a_cols(S), d_qb_raw, _b_cols(S), TN)], stacked=True),
        _mm("dw_in_b_gate", D, D, [(u_b, _a_cols(S), d_gate_b, _b_cols(S), TN)], stacked=True)], axis=0)
    d_u_b = _mm("du_b", S, D, [(d_pb[c], _a_rows(TN_, col=c % 2), w_in_b, in_b_block(c), NT) for c in range(NCHIP)],
                tn=TN_WIDE)
    d_kv, dg = _kv_bwd(dkdup, dvdup, kv, kg_b, rope)
    g["knorm_b_g"] = _fold_heads(dg)
    g["w_kv"] = _mm("dw_kv", D, 2 * KVW, [(u_kv, _a_cols(S), d_kv, _b_cols(S), TN)])
    d_u_kv = _mm("du_kv", S, D, [(d_kv, _a_rows(2 * KVW), wt["w_kv"], _b_rows(2 * KVW, tn=TN_WIDE), NT)], tn=TN_WIDE)
    d_h1, d_h1_b, g["kv_norm_g"], g["norm_b_g"] = _rmsnorm_bwd(h1, [kv_g, norm_b], [d_u_kv, d_u_b], d_out, "norm_b_bwd")
    g["w_out_a"] = _mm("dw_out_a", D, D, [(y_a, _a_cols(S), d_h1_b, _b_cols(S, tn=TN_WIDE), TN)], tn=TN_WIDE)
    late = {n: g[n] for n in LATE}
    d_y_a, halves = _mm("dy_a", S, D, [(d_h1_b, _a_rows(D), wt["w_out_a"], _b_rows(D, tn=TN_WIDE), NT)],
                        tn=TN_WIDE, riding=begin_reduce(late))
    riding, so_far = begin_reduce(late, halves)
    dq_a, dk_a, dv_a, dct, d_gate_a, arrived = _fox_bwd(q_a, k_a, v_a, ct2, o_a, lse_a, d_y_a, gate_a, riding)
    dct_pad = jnp.pad(dct.reshape(NH, S), ((0, LANES - NH), (0, 0)))
    d_f, db = _forget_bwd(dct_pad, fpad, b_pad)
    g["b_forget"] = db[0, :NH]
    d_q_raw, dg = _headnorm_bwd(qkv, 0, qg_a, dq_a, None, "qnorm_a_bwd")
    g["qnorm_a_g"] = _fold_heads(dg)
    d_k_raw, dg = _headnorm_bwd(qkv, 1, kg_a, dk_a, None, "knorm_a_bwd")
    g["knorm_a_g"] = _fold_heads(dg)
    rows, gw = 4 * D + NH, None
    for n, t, row0 in (("q", d_q_raw, 0), ("k", d_k_raw, D), ("v", dv_a, 2 * D)):
        gw = _mm("dw_in_a_" + n, D, D, [(t, _a_cols(S), u_a, _b_cols(S, tn=TN_WIDE), TN)], tn=TN_WIDE,
                 rows_of=(gw, rows, row0))
    gw = _mm("dw_in_a_f", LANES, D, [(d_f, _a_cols(S, tm=LANES), u_a, _b_cols(S, tn=TN_WIDE), TN)], tm=LANES,
             tn=TN_WIDE, rows_of=(gw, rows, 3 * D))
    g["w_in_a"] = _mm("dw_in_a_gate", D, D, [(d_gate_a, _a_cols(S), u_a, _b_cols(S, tn=TN_WIDE), TN)], tn=TN_WIDE,
                      rows_of=(gw, rows, 3 * D + NH))
    first = {"w_in_a": g["w_in_a"]}
    riding, so_far_first = begin_reduce(first, begin_reduce(first).alone("sibling_halves_w_in_a"))
    d_u_a, arrived_first = _mm("du_a", S, D, [
        (d_q_raw, _a_rows(D), w1t, _b_cols(D, row=0, tn=TN_WIDE), None),
        (d_k_raw, _a_rows(D), w1t, _b_cols(D, row=1, tn=TN_WIDE), None),
        (dv_a, _a_rows(D), w1t, _b_cols(D, row=2, tn=TN_WIDE), None),
        (d_gate_a, _a_rows(D), wg_t, _b_cols(D, tn=TN_WIDE), None),
        (d_f, _a_rows(LANES), w1t, _b_cols(LANES, row=f_row, tn=TN_WIDE), None)], tn=TN_WIDE, riding=riding)
    d_x, _, g["norm_a_g"] = _rmsnorm_bwd(x, [norm_a], [d_u_a], d_h1, "norm_a_bwd")
    return sq, d_x, g, (list(so_far_first) + list(so_far), list(arrived_first) + list(arrived))


BIG = ["w_in_a", "w_out_a", "w_kv", "w_in_b", "w_out_b"]
LATE = BIG[1:]
SPLIT = {"w_in_a": None, "w_out_a": 0, "w_kv": 0, "w_in_b": 0, "w_out_b": 0}
SMALL = ["norm_a_g", "b_forget", "qnorm_a_g", "knorm_a_g", "kv_norm_g", "knorm_b_g", "norm_b_g", "qnorm_b_g", "sinks"]
NAMES = ["norm_a_g", "w_in_a", "b_forget", "qnorm_a_g", "knorm_a_g", "w_out_a", "kv_norm_g", "w_kv", "knorm_b_g",
         "norm_b_g", "w_in_b", "qnorm_b_g", "sinks", "w_out_b"]


def _pack(vals):
    flat = []
    for v in vals:
        v = v.reshape(-1)
        flat.append(jnp.pad(v, (0, -v.shape[0] % LANES)))
    flat = jnp.concatenate(flat)
    flat = jnp.pad(flat, (0, -flat.shape[0] % (8 * LANES)))
    return flat.reshape(-1, LANES)


def _unpack(packed, shapes):
    flat, out, off = packed.reshape(-1), [], 0
    for s in shapes:
        n = int(np.prod(s))
        out.append(flat[off:off + n].reshape(s))
        off += n + (-n % LANES)
    return out


def kernel(x, positions, norm_a_g, w_in_a, b_forget, qnorm_a_g, knorm_a_g, w_out_a, kv_norm_g, w_kv, knorm_b_g, norm_b_g, w_in_b, qnorm_b_g, sinks, w_out_b, loss_target, m_norm_a_g, m_w_in_a, m_b_forget, m_qnorm_a_g, m_knorm_a_g, m_w_out_a, m_kv_norm_g, m_w_kv, m_knorm_b_g, m_norm_b_g, m_w_in_b, m_qnorm_b_g, m_sinks, m_w_out_b, v_norm_a_g, v_w_in_a, v_b_forget, v_qnorm_a_g, v_knorm_a_g, v_w_out_a, v_kv_norm_g, v_w_kv, v_knorm_b_g, v_norm_b_g, v_w_in_b, v_qnorm_b_g, v_sinks, v_w_out_b):
    w = dict(norm_a_g=norm_a_g, w_in_a=w_in_a, b_forget=b_forget, qnorm_a_g=qnorm_a_g, knorm_a_g=knorm_a_g,
             w_out_a=w_out_a, kv_norm_g=kv_norm_g, w_kv=w_kv, knorm_b_g=knorm_b_g, norm_b_g=norm_b_g,
             w_in_b=w_in_b, qnorm_b_g=qnorm_b_g, sinks=sinks, w_out_b=w_out_b)
    m = dict(norm_a_g=m_norm_a_g, w_in_a=m_w_in_a, b_forget=m_b_forget, qnorm_a_g=m_qnorm_a_g, knorm_a_g=m_knorm_a_g,
             w_out_a=m_w_out_a, kv_norm_g=m_kv_norm_g, w_kv=m_w_kv, knorm_b_g=m_knorm_b_g, norm_b_g=m_norm_b_g,
             w_in_b=m_w_in_b, qnorm_b_g=m_qnorm_b_g, sinks=m_sinks, w_out_b=m_w_out_b)
    v = dict(norm_a_g=v_norm_a_g, w_in_a=v_w_in_a, b_forget=v_b_forget, qnorm_a_g=v_qnorm_a_g, knorm_a_g=v_knorm_a_g,
             w_out_a=v_w_out_a, kv_norm_g=v_kv_norm_g, w_kv=v_w_kv, knorm_b_g=v_knorm_b_g, norm_b_g=v_norm_b_g,
             w_in_b=v_w_in_b, qnorm_b_g=v_qnorm_b_g, sinks=v_sinks, w_out_b=v_w_out_b)
    my_chip = 2 * lax.axis_index("x") + lax.axis_index("y")

    def shard2d(t, n):
        if n == "w_in_a":
            return jnp.transpose(t, (2, 0, 1)).reshape(-1)
        return t.reshape(t.shape[-2:])

    def unflat(t, n):
        return jnp.transpose(t.reshape(-1, 1, D), (1, 2, 0)) if n == "w_in_a" else t.reshape(w[n].shape)

    w2d = {n: shard2d(w[n], n) for n in BIG}

    norm_a_rows = jnp.broadcast_to(norm_a_g.reshape(1, D // NCHIP), (2 * SUBLANES, D // NCHIP))
    w1t, norm_rows = _gather_shards([w2d["w_in_a"].astype(BF16), norm_a_rows], [SPLIT["w_in_a"], 0])
    wt = {"w_in_a_t": w1t.reshape(-1, D), "norm_a_g": norm_rows[:, 0, :].reshape(1, D)}
    for n in SMALL[1:]:
        wt[n] = w[n]
    late_shards = [w2d[n].astype(BF16) for n in LATE]
    late_axes = [SPLIT[n] for n in LATE]
    transfers, outs, own = _gather_plan(late_shards, late_axes)
    fetch = _Riding(transfers, late_shards, outs, own)

    def late_weights(fetched):
        return {n: t if n == "w_in_b" else t.reshape(-1, t.shape[2]) for n, t in zip(LATE, fetched)}

    def as_blocks(t):
        if t.ndim == 3:
            return t
        return t.reshape(-1) if t.shape[0] % (SUBLANES * NCHIP) else t.reshape(NCHIP, -1, t.shape[1])

    def begin_reduce(grads, halves=None):
        names = list(grads)
        axes = [SPLIT[n] for n in names]
        blocks = [as_blocks(grads[n]) for n in names]
        if halves is None:
            transfers, outs = _halves_plan(blocks, axes)
            return _Riding(transfers, blocks, outs)
        sums = [_chip_sum(blk, part, ax, "chip_sum_" + n) for n, ax, blk, part in zip(names, axes, blocks, halves)]
        bf16 = [s[1] for s in sums]
        transfers, outs = _scatter_plan(bf16)
        return _Riding(transfers, bf16, outs), [s[0] for s in sums]

    sq, d_x, g, (chip_f32, arrived) = _local_step(x[0], loss_target[0], positions, wt, fetch, late_weights,
                                                  begin_reduce)

    axes = [SPLIT[n] for n in BIG]
    halves = []
    for n, ax, t32, parts in zip(BIG, axes, chip_f32, arrived):
        if t32.ndim == 1:
            own = lax.dynamic_slice_in_dim(t32, my_chip * (t32.shape[0] // NCHIP), t32.shape[0] // NCHIP)
        else:
            own = lax.dynamic_index_in_dim(t32, my_chip, axis=0, keepdims=False)
        halves.append(_mesh_sum(own, parts, ax, "mesh_sum_" + n))

    small_shapes = [(D,), (NH,), (HD,), (HD,), (D,), (HD,), (D,), (HD,), (NH,), (D,)]
    gathered_small, sibling_done = _last_exchange(_pack([g[n] for n in SMALL] + [sq]), halves)
    total = _sum_stack(gathered_small, "sum_small")
    small_g = dict(zip(SMALL, _unpack(total, small_shapes)[:-1]))
    loss = 0.5 * jnp.sum(_unpack(total, small_shapes)[-1]) / D
    small_g["norm_a_g"] = lax.dynamic_slice(small_g["norm_a_g"], (my_chip * (D // NCHIP),), (D // NCHIP,))

    res = {}
    for n, ax, mine_half, their_half in zip(BIG, axes, halves, sibling_done):
        out4 = _adamw_halves(w2d[n], mine_half, their_half, shard2d(m[n], n), shard2d(v[n], n), ax, "adamw_" + n)
        res[n] = tuple(unflat(t, n) for t in out4)
    row = lambda t: t.reshape(1, -1)
    small_out = _adamw_small(*[[row(d[n]) for n in SMALL] for d in (w, small_g, m, v)])
    for i, n in enumerate(SMALL):
        res[n] = tuple(t.reshape(w[n].shape) for t in (small_g[n],) + tuple(out[i] for out in small_out))

    outs = [loss, d_x[None]]
    for k in range(4):
        outs += [res[n][k] for n in NAMES]
    return tuple(outs)
```

```python
import numpy as np
import jax
import jax.numpy as jnp
from jax import lax
from jax.experimental import pallas as pl
from jax.experimental.pallas import tpu as pltpu

F32, BF16 = jnp.float32, jnp.bfloat16
S, D, HD, NH, NKV = 2048, 1024, 64, 16, 4
KVW = NKV * HD
WINDOW = 128
ROT = HD // 4
THETA = 500000.0
EPS = 1e-6
SCALE = HD ** -0.5
LANES = 128
SUBLANES = 8
NEG = -1e30
VMEM_LIMIT = 48 * 2 ** 20
ROWS = 512
ATT = 512
SWQ = 16
NCHIP = 4
ADAM_LR, ADAM_B1, ADAM_B2, ADAM_EPS, ADAM_WD, ADAM_STEP = 0.001, 0.9, 0.999, 1e-08, 0.01, 10
NT = (((1,), (1,)), ((), ()))
TN = (((0,), (0,)), ((), ()))
MESH = pl.DeviceIdType.MESH


def _params(n):
    return pltpu.CompilerParams(dimension_semantics=("arbitrary",) * n, vmem_limit_bytes=VMEM_LIMIT)


def _dot(a, b, dims=None):
    if dims is None:
        return jnp.dot(a, b, preferred_element_type=F32)
    return lax.dot_general(a, b, dims, preferred_element_type=F32)


def _dot_split(a, b, n):
    out, rest = None, a
    for _ in range(n):
        hi = rest.astype(BF16)
        term = _dot(hi, b)
        out = term if out is None else out + term
        rest = rest - hi.astype(F32)
    return out


def _seg_mat(w):
    e = (np.arange(w)[:, None] // HD == np.arange(LANES)[None, :]).astype(np.float32)
    return jnp.asarray(e, BF16)


def _spread(r, w):
    head = lax.broadcasted_iota(jnp.int32, (2 * LANES, w), 1) >> (HD.bit_length() - 1)
    row = lax.broadcasted_iota(jnp.int32, (2 * LANES, w), 0)
    et2 = jnp.where(head == (row & (LANES - 1)), 1.0, 0.0).astype(BF16)
    hi = r.astype(BF16)
    lo = (r - hi.astype(F32)).astype(BF16)
    return _dot(jnp.concatenate([hi, lo], axis=1), et2)


def _head_rstd(x, e):
    ss = _dot_split(x * x, e, 2)
    return _spread(lax.rsqrt(ss * (1.0 / HD) + EPS), x.shape[1])


def _rope(x, c, a, b):
    w = x.shape[1]
    return x * c + pltpu.roll(x, w - ROT // 2, 1) * a + pltpu.roll(x, ROT // 2, 1) * b


def _rope_t(dy, c, a, b):
    w = dy.shape[1]
    return dy * c + pltpu.roll(dy * b, w - ROT // 2, 1) + pltpu.roll(dy * a, ROT // 2, 1)


def _sigmoid(x):
    return 1.0 / (1.0 + jnp.exp(-x))


def _row_spec(shape, ts):
    nd = len(shape)
    if shape[0] == S:
        return pl.BlockSpec((ts,) + tuple(shape[1:]), lambda i: (i,) + (0,) * (nd - 1))
    return pl.BlockSpec(tuple(shape), lambda i: (0,) * nd)


def _rows_call(body, name, ins, outs, ts=ROWS):
    return pl.pallas_call(
        body, name=name, grid=(S // ts,),
        in_specs=[_row_spec(a.shape, ts) for a in ins],
        out_specs=[_row_spec(s, ts) for s, _ in outs],
        out_shape=[jax.ShapeDtypeStruct(s, d) for s, d in outs],
        compiler_params=_params(1))(*ins)


def _col_spec(ts, w, col):
    return pl.BlockSpec((ts, w), lambda i: (i, col))


TM = TN_ = 512
TM_TOKENS = 1024
TN_WIDE = 1024


def _mm(name, m, n, terms, out_dtype=F32, add=None, tm=None, tn=TN_, stacked=False, riding=None, rows_of=None):
    nterm = len(terms)
    if tm is None:
        tm = TM_TOKENS if m == S else TM
    nj, ni_ = n // tn, m // tm
    n_in = 2 * nterm + (add is not None) + (rows_of is not None and rows_of[0] is not None)
    r_in, r_out = (len(riding.ins), len(riding.outs)) if riding is not None else (0, 0)

    def body(*refs):
        if riding is not None:
            j, i = pl.program_id(0), pl.program_id(1)
            at_end = riding.hooks(refs[n_in:n_in + r_in], refs[n_in + r_in + 1:n_in + r_in + 1 + r_out],
                                  *refs[n_in + r_in + 1 + r_out:], first=(j == 0) & (i == 0),
                                  middle=(j == nj // 2) & (i == 0), last=(j == nj - 1) & (i == ni_ - 1))
        acc = None
        for t in range(nterm):
            part = _dot(refs[2 * t][...], refs[2 * t + 1][...], terms[t][4])
            acc = part if acc is None else acc + part
        if add is not None:
            acc = acc + refs[2 * nterm][...]
        refs[n_in + r_in][...] = acc.astype(out_dtype)
        if riding is not None:
            at_end()

    tile = pl.BlockSpec((tm, tn), lambda j, i: (i, j))
    ins, specs = [], []
    for a, a_spec, b, b_spec, _ in terms:
        ins += [a, b]
        specs += [a_spec, b_spec]
    if add is not None:
        ins.append(add)
        specs.append(tile)
    out_spec = pl.BlockSpec((None, tm, tn), lambda j, i: (j, i, 0)) if stacked else tile
    out_shape = jax.ShapeDtypeStruct((nj, m, tn) if stacked else (m, n), out_dtype)
    if rows_of is not None:
        taller, rows, row0 = rows_of
        out_spec = pl.BlockSpec((pl.Element(tm), pl.Element(tn)), lambda j, i: (
            pl.multiple_of(row0 + i * tm, SUBLANES), pl.multiple_of(j * tn, LANES)))
        out_shape = jax.ShapeDtypeStruct((rows, n), out_dtype)
        alias = {}
        if taller is not None:
            ins.append(taller)
            specs.append(pl.BlockSpec(memory_space=pltpu.HBM))
            alias = {len(ins) - 1: 0}
        return pl.pallas_call(body, name=name, grid=(nj, ni_), in_specs=specs, out_specs=out_spec,
                              out_shape=out_shape, input_output_aliases=alias, compiler_params=_params(2))(*ins)
    if riding is None:
        return pl.pallas_call(body, name=name, grid=(nj, ni_), in_specs=specs, out_specs=out_spec,
                              out_shape=out_shape, compiler_params=_params(2))(*ins)
    res = pl.pallas_call(
        body, name=name, grid=(nj, ni_), in_specs=specs + riding.in_specs,
        out_specs=[out_spec] + riding.out_specs, out_shape=[out_shape] + riding.out_shape,
        scratch_shapes=riding.scratch, compiler_params=_params(2))(*ins, *riding.ins)
    return res[0], res[1:]


def _a_rows(k, col=0, tm=TM_TOKENS):
    return pl.BlockSpec((tm, k), lambda j, i: (i, col))


def _a_cols(k, tm=TM):
    return pl.BlockSpec((k, tm), lambda j, i: (0, i))


def _b_cols(k, row=0, col0=0, tn=TN_):
    return pl.BlockSpec((k, tn), lambda j, i: (row, col0 + j))


def _b_rows(k, row0=0, tn=TN_):
    return pl.BlockSpec((tn, k), lambda j, i: (row0 + j, 0))


def _rmsnorm_fwd(x, gains, name):
    def body(*refs):
        xv = refs[0][...]
        r = lax.rsqrt(jnp.mean(xv * xv, axis=-1, keepdims=True) + EPS)
        xh = xv * r
        for n in range(len(gains)):
            refs[1 + len(gains) + n][...] = (xh * refs[1 + n][...]).astype(BF16)

    return _rows_call(body, name, [x] + list(gains), [((S, D), BF16)] * len(gains))


def _rmsnorm_bwd(x, gains, dus, dres, name):
    n = len(gains)

    def body(*refs):
        x_ref, g_refs, du_refs, dres_ref = refs[0], refs[1:1 + n], refs[1 + n:1 + 2 * n], refs[1 + 2 * n]
        dx_ref, dxb_ref, dg_refs = refs[2 + 2 * n], refs[3 + 2 * n], refs[4 + 2 * n:]
        xv = x_ref[...]
        r = lax.rsqrt(jnp.mean(xv * xv, axis=-1, keepdims=True) + EPS)
        xh = xv * r
        gy = None
        for m in range(n):
            du = du_refs[m][...]
            part = jnp.sum(du * xh, axis=0, keepdims=True)

            @pl.when(pl.program_id(0) == 0)
            def _(m=m, part=part):
                dg_refs[m][...] = part

            @pl.when(pl.program_id(0) != 0)
            def _(m=m, part=part):
                dg_refs[m][...] += part

            t = du * g_refs[m][...]
            gy = t if gy is None else gy + t
        dx = dres_ref[...] + r * (gy - xh * jnp.mean(gy * xh, axis=-1, keepdims=True))
        dx_ref[...] = dx
        dxb_ref[...] = dx.astype(BF16)

    outs = [((S, D), F32), ((S, D), BF16)] + [((1, D), F32)] * n
    return _rows_call(body, name, [x] + list(gains) + list(dus) + [dres], outs)


def _a_post(qkvg, qg, kg):
    e = _seg_mat(D)

    def body(q_ref, k_ref, v_ref, qg_ref, kg_ref, e_ref, qo, ko, vo):
        ev = e_ref[...]
        qv, kv = q_ref[...], k_ref[...]
        qo[...] = (qv * _head_rstd(qv, ev) * qg_ref[...] * SCALE).astype(BF16)
        ko[...] = (kv * _head_rstd(kv, ev) * kg_ref[...]).astype(BF16)
        vo[...] = v_ref[...].astype(BF16)

    whole = lambda a: pl.BlockSpec(a.shape, lambda i: (0, 0))
    return pl.pallas_call(
        body, name="a_post", grid=(S // ROWS,),
        in_specs=[_col_spec(ROWS, D, 0), _col_spec(ROWS, D, 1), _col_spec(ROWS, D, 2),
                  whole(qg), whole(kg), whole(e)],
        out_specs=[_col_spec(ROWS, D, 0)] * 3,
        out_shape=[jax.ShapeDtypeStruct((S, D), BF16)] * 3,
        compiler_params=_params(1))(qkvg, qkvg, qkvg, qg, kg, e)


def _tri(upper):
    r, c = np.arange(ROWS)[:, None], np.arange(ROWS)[None, :]
    return jnp.asarray((r <= c) if upper else (r >= c), BF16)


def _forget_cumsum(fpad, bpad):
    def body(f_ref, b_ref, u_ref, c_ref, carry):
        @pl.when(pl.program_id(0) == 0)
        def _():
            carry[...] = jnp.zeros_like(carry)

        lf = jax.nn.log_sigmoid(f_ref[...] + b_ref[...])
        blk = _dot_split(lf.T, u_ref[...], 3) + carry[:, 0:1]
        c_ref[...] = blk
        carry[...] = jnp.broadcast_to(blk[:, ROWS - 1:ROWS], carry.shape)

    return pl.pallas_call(
        body, name="forget_cumsum", grid=(S // ROWS,),
        in_specs=[pl.BlockSpec((ROWS, LANES), lambda i: (i, 0)), pl.BlockSpec((1, LANES), lambda i: (0, 0)),
                  pl.BlockSpec((ROWS, ROWS), lambda i: (0, 0))],
        out_specs=pl.BlockSpec((LANES, ROWS), lambda i: (0, i)),
        out_shape=jax.ShapeDtypeStruct((LANES, S), F32),
        scratch_shapes=[pltpu.VMEM((LANES, LANES), F32)],
        compiler_params=_params(1))(fpad, bpad, _tri(True))


def _forget_bwd(dct, fpad, bpad):
    nb = S // ROWS

    def body(dc_ref, f_ref, b_ref, l_ref, df_ref, db_ref, carry):
        @pl.when(pl.program_id(0) == 0)
        def _():
            carry[...] = jnp.zeros_like(carry)
            db_ref[...] = jnp.zeros_like(db_ref)

        blk = _dot_split(dc_ref[...], l_ref[...], 3) + carry[:, 0:1]
        carry[...] = jnp.broadcast_to(blk[:, 0:1], carry.shape)
        df = blk.T * _sigmoid(-(f_ref[...] + b_ref[...]))
        df_ref[...] = df.astype(BF16)
        db_ref[...] += jnp.sum(df, axis=0, keepdims=True)

    return pl.pallas_call(
        body, name="forget_bwd", grid=(nb,),
        in_specs=[pl.BlockSpec((LANES, ROWS), lambda i: (0, nb - 1 - i)),
                  pl.BlockSpec((ROWS, LANES), lambda i: (nb - 1 - i, 0)),
                  pl.BlockSpec((1, LANES), lambda i: (0, 0)), pl.BlockSpec((ROWS, ROWS), lambda i: (0, 0))],
        out_specs=[pl.BlockSpec((ROWS, LANES), lambda i: (nb - 1 - i, 0)), pl.BlockSpec((1, LANES), lambda i: (0, 0))],
        out_shape=[jax.ShapeDtypeStruct((S, LANES), BF16), jax.ShapeDtypeStruct((1, LANES), F32)],
        scratch_shapes=[pltpu.VMEM((LANES, LANES), F32)],
        compiler_params=_params(1))(dct, fpad, bpad, _tri(False))


def _headnorm_bwd(x, col, gain, dy, rope, name):
    e = _seg_mat(D)
    tabs = list(rope) if rope is not None else []

    def body(*refs):
        x_ref, g_ref, dy_ref, e_ref = refs[:4]
        dx_ref, dg_ref = refs[-2:]
        xv, dyv, ev = x_ref[...], dy_ref[...], e_ref[...]
        if rope is not None:
            c, a, b = (jnp.tile(t[...], (1, D // LANES)) for t in refs[4:7])
            dyv = _rope_t(dyv, c, a, b)
        r = _head_rstd(xv, ev)
        xh = xv * r
        part = jnp.sum(dyv * xh, axis=0, keepdims=True)

        @pl.when(pl.program_id(0) == 0)
        def _():
            dg_ref[...] = part

        @pl.when(pl.program_id(0) != 0)
        def _():
            dg_ref[...] += part

        gy = dyv * g_ref[...]
        seg = _spread(_dot_split(gy * xh, ev, 2) * (1.0 / HD), D)
        dx_ref[...] = (r * (gy - xh * seg)).astype(BF16)

    whole = lambda a: pl.BlockSpec(a.shape, lambda i: (0, 0))
    return pl.pallas_call(
        body, name=name, grid=(S // ROWS,),
        in_specs=[_col_spec(ROWS, D, col), whole(gain), _col_spec(ROWS, D, 0), whole(e)]
                 + [pl.BlockSpec((ROWS, LANES), lambda i: (i, 0))] * len(tabs),
        out_specs=[_col_spec(ROWS, D, 0), whole(gain)],
        out_shape=[jax.ShapeDtypeStruct((S, D), BF16), jax.ShapeDtypeStruct((1, D), F32)],
        compiler_params=_params(1))(x, gain, dy, e, *tabs)


def _dup_mat():
    r, c = np.arange(KVW)[:, None], np.arange(2 * KVW)[None, :]
    return (r // HD == c // LANES) & (r % HD == c % HD)


def _fold_mat():
    r, c = np.arange(D)[:, None], np.arange(KVW)[None, :]
    return (r // (2 * LANES) == c // HD) & (r % HD == c % HD)


def _b_post(pb, kv, qg, kg, rope):
    e, ek = _seg_mat(D), _seg_mat(KVW)
    dup = jnp.asarray(_dup_mat(), BF16)

    def body(q_ref, k_ref, v_ref, qg_ref, kg_ref, e_ref, ek_ref, dup_ref, c_ref, a_ref, b_ref, qo, ko, vo):
        c1, a1, b1 = c_ref[...], a_ref[...], b_ref[...]
        qv = q_ref[...]
        qn = qv * _head_rstd(qv, e_ref[...]) * qg_ref[...]
        t = lambda z, n: jnp.tile(z, (1, n))
        qo[...] = (_rope(qn, t(c1, D // LANES), t(a1, D // LANES), t(b1, D // LANES)) * SCALE).astype(BF16)
        kvv = k_ref[...]
        kn = kvv * _head_rstd(kvv, ek_ref[...]) * kg_ref[...]
        kr = _rope(kn, t(c1, KVW // LANES), t(a1, KVW // LANES), t(b1, KVW // LANES)).astype(BF16)
        ko[...] = _dot(kr, dup_ref[...]).astype(BF16)
        vo[...] = _dot(v_ref[...].astype(BF16), dup_ref[...]).astype(BF16)

    whole = lambda a: pl.BlockSpec(a.shape, lambda i: (0, 0))
    tab = pl.BlockSpec((ROWS, LANES), lambda i: (i, 0))
    return pl.pallas_call(
        body, name="b_post", grid=(S // ROWS,),
        in_specs=[_col_spec(ROWS, D, 0), _col_spec(ROWS, KVW, 0), _col_spec(ROWS, KVW, 1),
                  whole(qg), whole(kg), whole(e), whole(ek), whole(dup), tab, tab, tab],
        out_specs=[_col_spec(ROWS, D, 0), _col_spec(ROWS, 2 * KVW, 0), _col_spec(ROWS, 2 * KVW, 0)],
        out_shape=[jax.ShapeDtypeStruct((S, D), BF16), jax.ShapeDtypeStruct((S, 2 * KVW), BF16),
                   jax.ShapeDtypeStruct((S, 2 * KVW), BF16)],
        compiler_params=_params(1))(pb, kv, kv, qg, kg, e, ek, dup, *rope)


def _kv_bwd(dkdup, dvdup, kv, kg, rope):
    ek = _seg_mat(KVW)
    fold = jnp.asarray(_fold_mat(), BF16)

    def body(dk_ref, dv_ref, k_ref, kg_ref, ek_ref, fold_ref, c_ref, a_ref, b_ref, dkv_ref, dg_ref):
        ev, fv = ek_ref[...], fold_ref[...]
        t = lambda z: jnp.tile(z[...], (1, KVW // LANES))
        dk = _rope_t(_dot_split(dk_ref[...], fv, 2), t(c_ref), t(a_ref), t(b_ref))
        dv = _dot_split(dv_ref[...], fv, 2)
        xv = k_ref[...]
        r = _head_rstd(xv, ev)
        xh = xv * r
        part = jnp.sum(dk * xh, axis=0, keepdims=True)

        @pl.when(pl.program_id(0) == 0)
        def _():
            dg_ref[...] = part

        @pl.when(pl.program_id(0) != 0)
        def _():
            dg_ref[...] += part

        gy = dk * kg_ref[...]
        seg = _spread(_dot_split(gy * xh, ev, 2) * (1.0 / HD), KVW)
        dkv_ref[:, 0:KVW] = (r * (gy - xh * seg)).astype(BF16)
        dkv_ref[:, KVW:2 * KVW] = dv.astype(BF16)

    whole = lambda a: pl.BlockSpec(a.shape, lambda i: (0, 0))
    tab = pl.BlockSpec((ROWS, LANES), lambda i: (i, 0))
    return pl.pallas_call(
        body, name="kv_bwd", grid=(S // ROWS,),
        in_specs=[_col_spec(ROWS, D, 0), _col_spec(ROWS, D, 0), _col_spec(ROWS, KVW, 0),
                  whole(kg), whole(ek), whole(fold), tab, tab, tab],
        out_specs=[_col_spec(ROWS, 2 * KVW, 0), whole(kg)],
        out_shape=[jax.ShapeDtypeStruct((S, 2 * KVW), BF16), jax.ShapeDtypeStruct((1, KVW), F32)],
        compiler_params=_params(1))(dkdup, dvdup, kv, kg, ek, fold, *rope)


def _loss_head(out, target):
    def body(o_ref, t_ref, d_ref, db_ref, l_ref):
        diff = o_ref[...] - t_ref[...]
        d = diff * (1.0 / D)
        d_ref[...] = d
        db_ref[...] = d.astype(BF16)

        @pl.when(pl.program_id(0) == 0)
        def _():
            l_ref[...] = jnp.zeros_like(l_ref)

        l_ref[...] += jnp.sum(diff * diff, axis=0, keepdims=True)

    return _rows_call(body, "loss_head", [out, target], [((S, D), F32), ((S, D), BF16), ((1, D), F32)])


def _lane():
    return lax.broadcasted_iota(jnp.int32, (1, LANES), 1)


def _head_mask(hh):
    return (_lane() < HD) if hh == 0 else (_lane() >= HD)


def _fox_fwd(q, k, v, ct, gate, riding):
    nq, npair = S // ATT, NH // 2
    ni, no = len(riding.ins), len(riding.outs)

    def body(q_ref, k_ref, v_ref, c_ref, gate_ref, *rest):
        o_ref, lse_ref, y_ref = rest[ni:ni + 3]
        pair, i = pl.program_id(0), pl.program_id(1)
        at_end = riding.hooks(rest[:ni], rest[ni + 3:ni + 3 + no], *rest[ni + 3 + no:],
                              first=(pair == 0) & (i == 0), middle=(pair == npair // 2) & (i == 0),
                              last=(pair == npair - 1) & (i == nq - 1))
        q2 = q_ref[...]
        qms = [jnp.where(_head_mask(hh), q2, jnp.zeros_like(q2)) for hh in (0, 1)]

        def probs(off, width, m, hh, diag):
            s = _dot(qms[hh], k_ref[pl.ds(off, width), :], NT) - c_ref[hh:hh + 1, pl.ds(off, width)]
            if diag:
                row = i * ATT + lax.broadcasted_iota(jnp.int32, (ATT, width), 0)
                col = off + lax.broadcasted_iota(jnp.int32, (ATT, width), 1)
                s = jnp.where(col <= row, s, NEG)
            m_new = jnp.maximum(m, jnp.max(s, axis=1, keepdims=True))
            p = jnp.exp(s - m_new)
            p_hi = p.astype(BF16)
            return m_new, jnp.exp(m - m_new), p_hi, (p - p_hi.astype(F32)).astype(BF16)

        def weighted(off, width, p_hi, p_lo, hh):
            vj = v_ref[pl.ds(off, width), :]
            v1 = jnp.where(_head_mask(hh), vj, jnp.ones_like(vj))
            return _dot(p_hi, v1) + _dot(p_lo, v1)

        def step(off, width, carry, diag):
            off = pl.multiple_of(off, ATT)
            out = []
            for hh in (0, 1):
                m, acc = carry[hh]
                m, alpha, p_hi, p_lo = probs(off, width, m, hh, diag)
                out.append((m, alpha * acc + weighted(off, width, p_hi, p_lo, hh)))
            return tuple(out)

        one = (jnp.full((ATT, 1), NEG, F32), jnp.zeros((ATT, LANES), F32))
        carry = lax.fori_loop(0, i // 2, lambda j, cr: step(j * (2 * ATT), 2 * ATT, cr, False), (one, one))
        carry = lax.cond(i % 2 == 1, lambda cr: step((i - 1) * ATT, 2 * ATT, cr, True),
                         lambda cr: step(i * ATT, ATT, cr, True), carry)
        res = []
        for hh in (0, 1):
            m, acc = carry[hh]
            l = jnp.max(jnp.where(_head_mask(1 - hh), acc, 0.0), axis=1, keepdims=True)
            res.append((acc / l, m + jnp.log(l)))
        first = _head_mask(0)
        o = jnp.where(first, res[0][0], res[1][0])
        o_ref[...] = o
        lse_ref[...] = jnp.where(first, res[0][1], res[1][1])
        g = gate_ref[...]
        y_ref[...] = (o * (g * _sigmoid(g))).astype(BF16)
        at_end()

    blk = pl.BlockSpec((ATT, LANES), lambda p, i: (i, p))
    full = pl.BlockSpec((S, LANES), lambda p, i: (0, p))
    res = pl.pallas_call(
        body, name="fox_fwd", grid=(npair, nq),
        in_specs=[blk, full, full, pl.BlockSpec((None, 2, S), lambda p, i: (p, 0, 0)), blk] + riding.in_specs,
        out_specs=[blk, blk, blk] + riding.out_specs,
        out_shape=[jax.ShapeDtypeStruct((S, D), F32)] * 2 + [jax.ShapeDtypeStruct((S, D), BF16)] + riding.out_shape,
        scratch_shapes=riding.scratch,
        compiler_params=_params(2))(q, k, v, ct, gate, *riding.ins)
    return res[0], res[1], res[2], res[3:]


def _gate_grads(dy, o, g):
    sg = _sigmoid(g)
    return dy * (g * sg), dy * o * (sg * (1.0 + g * (1.0 - sg)))


def _fox_bwd(q, k, v, ct, o, lse, dy, gate, riding):
    nq, npair = S // ATT, NH // 2
    ni, no = len(riding.ins), len(riding.outs)

    def body(q_ref, k_ref, v_ref, c_ref, o_ref, lse_ref, dy_ref, gate_ref, *rest):
        dq_ref, dk_ref, dvb_ref, dc_ref, dgate_ref = rest[ni:ni + 5]
        dv_ref = rest[ni + 5 + no]
        pair, i = pl.program_id(0), pl.program_id(1)
        at_end = riding.hooks(rest[:ni], rest[ni + 5:ni + 5 + no], *rest[ni + 6 + no:],
                              first=(pair == 0) & (i == 0), middle=(pair == npair // 2) & (i == 0),
                              last=(pair == npair - 1) & (i == nq - 1))

        @pl.when(i == 0)
        def _():
            dk_ref[...] = jnp.zeros_like(dk_ref)
            dv_ref[...] = jnp.zeros_like(dv_ref)
            dc_ref[...] = jnp.zeros_like(dc_ref)

        q2, lse2 = q_ref[...], lse_ref[...]
        do2, dgate = _gate_grads(dy_ref[...], o_ref[...], gate_ref[...])
        dgate_ref[...] = dgate.astype(BF16)
        do2b = do2.astype(BF16)
        prod = do2b.astype(F32) * o_ref[...]
        heads = []
        for hh in (0, 1):
            hm = _head_mask(hh)
            heads.append((jnp.where(hm, q2, jnp.zeros_like(q2)), jnp.where(hm, do2b, jnp.zeros_like(do2b)),
                          jnp.sum(jnp.where(hm, prod, 0.0), axis=1, keepdims=True),
                          jnp.max(jnp.where(hm, lse2, NEG), axis=1, keepdims=True)))

        def step(off, width, dqs, diag):
            off = pl.multiple_of(off, ATT)
            kj, vj = k_ref[pl.ds(off, width), :], v_ref[pl.ds(off, width), :]
            dk, dv, out = None, None, []
            for hh in (0, 1):
                qm, dom, delta, lse_h = heads[hh]
                s = _dot(qm, kj, NT) - c_ref[hh:hh + 1, pl.ds(off, width)]
                p = jnp.exp(s - lse_h)
                if diag:
                    row = i * ATT + lax.broadcasted_iota(jnp.int32, (ATT, width), 0)
                    col = off + lax.broadcasted_iota(jnp.int32, (ATT, width), 1)
                    p = jnp.where(col <= row, p, 0.0)
                ds = p * (_dot(dom, vj, NT) - delta)
                dc_ref[hh:hh + 1, pl.ds(off, width)] += -jnp.sum(ds, axis=0, keepdims=True)
                dsb = ds.astype(BF16)
                dk_h, dv_h = _dot(dsb, qm, TN), _dot(p.astype(BF16), dom, TN)
                dk, dv = (dk_h, dv_h) if dk is None else (dk + dk_h, dv + dv_h)
                out.append(dqs[hh] + _dot(dsb, kj))
            dk_ref[pl.ds(off, width), :] += dk
            dv_ref[pl.ds(off, width), :] += dv
            return tuple(out)

        zero = jnp.zeros((ATT, LANES), F32)
        dqs = lax.fori_loop(0, i // 2, lambda j, acc: step(j * (2 * ATT), 2 * ATT, acc, False), (zero, zero))
        dqs = lax.cond(i % 2 == 1, lambda acc: step((i - 1) * ATT, 2 * ATT, acc, True),
                       lambda acc: step(i * ATT, ATT, acc, True), dqs)
        dq_ref[...] = jnp.where(_head_mask(0), dqs[0], dqs[1]) * SCALE

        @pl.when(i == nq - 1)
        def _():
            dvb_ref[...] = dv_ref[...].astype(BF16)

        at_end()

    blk = pl.BlockSpec((ATT, LANES), lambda p, i: (i, p))
    full = pl.BlockSpec((S, LANES), lambda p, i: (0, p))
    cspec = pl.BlockSpec((None, 2, S), lambda p, i: (p, 0, 0))
    res = pl.pallas_call(
        body, name="fox_bwd", grid=(npair, nq),
        in_specs=[blk, full, full, cspec, blk, blk, blk, blk] + riding.in_specs,
        out_specs=[blk, full, full, cspec, blk] + riding.out_specs,
        out_shape=[jax.ShapeDtypeStruct((S, D), F32)] * 2 + [jax.ShapeDtypeStruct((S, D), BF16),
                                                              jax.ShapeDtypeStruct((npair, 2, S), F32),
                                                              jax.ShapeDtypeStruct((S, D), BF16)]
                  + riding.out_shape,
        scratch_shapes=[pltpu.VMEM((S, LANES), F32)] + riding.scratch,
        compiler_params=_params(2))(q, k, v, ct, o, lse, dy, gate, *riding.ins)
    return res[0], res[1], res[2], res[3], res[4], res[5:]


def _both_heads(x):
    return jnp.concatenate([jnp.where(_head_mask(hh), x, jnp.zeros_like(x)) for hh in (0, 1)], axis=0)


def _per_head(col0, col1):
    return jnp.concatenate([jnp.broadcast_to(col0, (WINDOW, 1)), jnp.broadcast_to(col1, (WINDOW, 1))], axis=0)


def _unstack(x2):
    return jnp.where(_head_mask(0), x2[:WINDOW], x2[WINDOW:])


def _swa_valid(i, start):
    r = lax.broadcasted_iota(jnp.int32, (2 * WINDOW, 2 * WINDOW), 0)
    qabs = i * WINDOW + jnp.where(r >= WINDOW, r - WINDOW, r)
    kabs = start + lax.broadcasted_iota(jnp.int32, (2 * WINDOW, 2 * WINDOW), 1)
    return (kabs <= qabs) & (qabs - kabs < WINDOW)


def _swa_fwd(q, kdup, vdup, sinks_t, proj, gate_col):
    def body(q_ref, k_ref, v_ref, sk_ref, gate_ref, o_ref, lse_ref, y_ref):
        skv = sk_ref[...]
        first = _head_mask(0)
        for sb in range(SWQ):
            i = pl.program_id(1) * SWQ + sb
            rows = slice(sb * WINDOW, (sb + 1) * WINDOW)
            start = pl.multiple_of(jnp.maximum(i - 1, 0) * WINDOW, WINDOW)
            kk, vv = k_ref[pl.ds(start, 2 * WINDOW), :], v_ref[pl.ds(start, 2 * WINDOW), :]
            q2 = q_ref[rows, :]
            valid = _swa_valid(i, start)[:WINDOW]
            res = []
            for hh in (0, 1):
                hm = _head_mask(hh)
                sink = jnp.max(jnp.where(hm, skv, NEG), axis=1, keepdims=True)
                s = jnp.where(valid, _dot(jnp.where(hm, q2, jnp.zeros_like(q2)), kk, NT), NEG)
                m = jnp.maximum(jnp.max(s, axis=1, keepdims=True), sink)
                p = jnp.exp(s - m)
                l = jnp.sum(p, axis=1, keepdims=True) + jnp.exp(sink - m)
                res.append((_dot(p.astype(BF16), vv) / l, m + jnp.log(l)))
            o = jnp.where(first, res[0][0], res[1][0])
            o_ref[rows, :] = o
            lse_ref[rows, :] = jnp.where(first, res[0][1], res[1][1])
            g = gate_ref[rows, :]
            y_ref[rows, :] = (o * (g * _sigmoid(g))).astype(BF16)

    blk = pl.BlockSpec((SWQ * WINDOW, LANES), lambda p, i: (i, p))
    gate = pl.BlockSpec((SWQ * WINDOW, LANES), lambda p, i: (i, gate_col + p))
    full = pl.BlockSpec((S, LANES), lambda p, i: (0, p // 2))
    return pl.pallas_call(
        body, name="swa_fwd", grid=(NH // 2, S // (SWQ * WINDOW)),
        in_specs=[blk, full, full, pl.BlockSpec((1, LANES), lambda p, i: (0, p)), gate],
        out_specs=[blk, blk, blk],
        out_shape=[jax.ShapeDtypeStruct((S, D), F32)] * 2 + [jax.ShapeDtypeStruct((S, D), BF16)],
        compiler_params=_params(2))(q, kdup, vdup, sinks_t, proj)


def _swa_bwd(q, kdup, vdup, sinks_t, o, lse, dy, proj, gate_col):
    def body(q_ref, k_ref, v_ref, sk_ref, o_ref, lse_ref, dy_ref, gate_ref, dq_ref, dk_ref, dv_ref, dsk_ref,
             dgate_ref):
        @pl.when(pl.program_id(1) == 0)
        def _():
            dk_ref[...] = jnp.zeros_like(dk_ref)
            dv_ref[...] = jnp.zeros_like(dv_ref)
            dsk_ref[...] = jnp.zeros_like(dsk_ref)

        skv = sk_ref[...]
        first = _head_mask(0)
        sink = _per_head(*[jnp.max(jnp.where(_head_mask(hh), skv, NEG), axis=1, keepdims=True) for hh in (0, 1)])
        for sb in range(SWQ):
            i = pl.program_id(1) * SWQ + sb
            rows = slice(sb * WINDOW, (sb + 1) * WINDOW)
            start = pl.multiple_of(jnp.maximum(i - 1, 0) * WINDOW, WINDOW)
            kk, vv = k_ref[pl.ds(start, 2 * WINDOW), :], v_ref[pl.ds(start, 2 * WINDOW), :]
            do2, dgate = _gate_grads(dy_ref[rows, :], o_ref[rows, :], gate_ref[rows, :])
            dgate_ref[rows, :] = dgate.astype(BF16)
            do2b = do2.astype(BF16)
            prod, lse2 = do2b.astype(F32) * o_ref[rows, :], lse_ref[rows, :]
            qs, dos = _both_heads(q_ref[rows, :]), _both_heads(do2b)
            delta = jnp.concatenate([jnp.sum(jnp.where(_head_mask(hh), prod, 0.0), axis=1, keepdims=True)
                                     for hh in (0, 1)], axis=0)
            lse_h = jnp.concatenate([jnp.max(jnp.where(_head_mask(hh), lse2, NEG), axis=1, keepdims=True)
                                     for hh in (0, 1)], axis=0)
            p = jnp.where(_swa_valid(i, start), jnp.exp(_dot(qs, kk, NT) - lse_h), 0.0)
            dsb = (p * (_dot(dos, vv, NT) - delta)).astype(BF16)
            dk_ref[pl.ds(start, 2 * WINDOW), :] += _dot(dsb, qs, TN)
            dv_ref[pl.ds(start, 2 * WINDOW), :] += _dot(p.astype(BF16), dos, TN)
            dq_ref[rows, :] = _unstack(_dot(dsb, kk)) * SCALE
            t = jnp.exp(sink - lse_h) * delta
            dsk_ref[...] += -jnp.where(first, jnp.sum(t[:WINDOW], axis=0, keepdims=True),
                                       jnp.sum(t[WINDOW:], axis=0, keepdims=True))

    blk = pl.BlockSpec((SWQ * WINDOW, LANES), lambda p, i: (i, p))
    full = pl.BlockSpec((S, LANES), lambda p, i: (0, p // 2))
    acc = pl.BlockSpec((S, LANES), lambda p, i: (0, p))
    sk = pl.BlockSpec((1, LANES), lambda p, i: (0, p))
    gate = pl.BlockSpec((SWQ * WINDOW, LANES), lambda p, i: (i, gate_col + p))
    return pl.pallas_call(
        body, name="swa_bwd", grid=(NH // 2, S // (SWQ * WINDOW)),
        in_specs=[blk, full, full, sk, blk, blk, blk, gate],
        out_specs=[blk, acc, acc, sk, blk],
        out_shape=[jax.ShapeDtypeStruct((S, D), F32)] * 3 + [jax.ShapeDtypeStruct((1, D), F32),
                                                              jax.ShapeDtypeStruct((S, D), BF16)],
        compiler_params=_params(2))(q, kdup, vdup, sinks_t, o, lse, dy, proj)


def _adamw_math(w, g, m, v):
    m = ADAM_B1 * m + (1.0 - ADAM_B1) * g
    v = ADAM_B2 * v + (1.0 - ADAM_B2) * jnp.square(g)
    m_hat = m / (1.0 - ADAM_B1 ** ADAM_STEP)
    v_hat = v / (1.0 - ADAM_B2 ** ADAM_STEP)
    delta = -ADAM_LR * (m_hat / (jnp.sqrt(v_hat) + ADAM_EPS) + ADAM_WD * w)
    return delta, m, v


def _adamw_small(ws, gs, ms, vs):
    k = len(ws)

    def body(*refs):
        for p in range(k):
            w_ref, g_ref, m_ref, v_ref = (refs[q * k + p] for q in range(4))
            d, mo, vo = _adamw_math(w_ref[...], g_ref[...], m_ref[...], v_ref[...])
            refs[4 * k + p][...], refs[5 * k + p][...], refs[6 * k + p][...] = d, mo, vo

    res = pl.pallas_call(
        body, name="adamw_small",
        out_shape=[jax.ShapeDtypeStruct(t.shape, F32) for t in ws] * 3)(*ws, *gs, *ms, *vs)
    return res[:k], res[k:2 * k], res[2 * k:]


SUM_TILES = (512, 256, 128)


FLAT_BLOCK = 257 * 1024


def _tiles(shape, axis, lead=0, halves=False):
    if len(shape) == 1:
        count = shape[0] // FLAT_BLOCK
        return (FLAT_BLOCK,), count, lambda pos, *lead_idx: (sum(k * count for k in lead_idx) + pos,)
    r, c = shape
    tile = next(t for t in SUM_TILES if (shape[axis] // (2 if halves else 1)) % t == 0)
    blk = (tile, c) if axis == 0 else (r, tile)
    count = shape[axis] // tile

    def index(pos, *lead_idx):
        return tuple(lead_idx) + ((pos, 0) if axis == 0 else (0, pos))

    return (None,) * lead + blk, count, index


def _adamw_halves(w, g_mine, g_theirs, m, v, axis, name):
    blk, count, index = _tiles(w.shape, axis, halves=True)
    per_half = count // 2

    def body(w_ref, a_ref, b_ref, m_ref, v_ref, g_ref, d_ref, mo_ref, vo_ref):
        is_mine = pl.program_id(0) // per_half == lax.axis_index("c")
        g = jnp.where(is_mine, a_ref[...], b_ref[...])
        g_ref[...] = g
        d_ref[...], mo_ref[...], vo_ref[...] = _adamw_math(w_ref[...], g, m_ref[...], v_ref[...])

    spec = pl.BlockSpec(blk, lambda i: index(i))
    half = pl.BlockSpec(blk, lambda i: index(i % per_half))
    return pl.pallas_call(
        body, name=name, grid=(count,), in_specs=[spec, half, half, spec, spec], out_specs=[spec] * 4,
        out_shape=[jax.ShapeDtypeStruct(w.shape, F32)] * 4, compiler_params=_params(1))(w, g_mine, g_theirs, m, v)


def _chip_sum(blocks, from_sibling, axis, name):
    flat = blocks.ndim == 1
    blk, count, index = _tiles((from_sibling.shape[0] // NCHIP,) if flat else from_sibling.shape[1:], axis, lead=1)

    def body(lo_ref, hi_ref, p_ref, o32, o16):
        mine = jnp.where(lax.axis_index("c") == 0, lo_ref[...], hi_ref[...])
        acc = mine + p_ref[...]
        o32[...] = acc
        o16[...] = acc.astype(BF16)

    half = pl.BlockSpec(blk, lambda k, i: index(i, k))
    if flat:
        lo = pl.BlockSpec(blk, lambda k, i: (2 * count * k + i,))
        hi = pl.BlockSpec(blk, lambda k, i: (2 * count * k + count + i,))
    else:
        lo, hi = half, pl.BlockSpec(blk, lambda k, i: index(i + count, k))
    return pl.pallas_call(
        body, name=name, grid=(NCHIP, count), in_specs=[lo, hi, half], out_specs=[half, half],
        out_shape=[jax.ShapeDtypeStruct(from_sibling.shape, F32), jax.ShapeDtypeStruct(from_sibling.shape, BF16)],
        compiler_params=_params(2))(blocks, blocks, from_sibling)


def _mesh_sum(own, parts, axis, name):
    blk, count, index = _tiles(own.shape, axis)
    n = NCHIP - 1

    def body(a_ref, *refs):
        acc = a_ref[...]
        for k in range(n):
            acc = acc + refs[k][...].astype(F32)
        refs[n][...] = acc

    spec = pl.BlockSpec(blk, lambda i: index(i))
    if own.ndim == 1:
        part = [pl.BlockSpec(blk, lambda i, k=k: (k * count + i,)) for k in range(n)]
    else:
        part = [pl.BlockSpec((None,) + blk, lambda i, k=k: (k,) + index(i)) for k in range(n)]
    return pl.pallas_call(
        body, name=name, grid=(count,), in_specs=[spec] + part,
        out_specs=spec, out_shape=jax.ShapeDtypeStruct(own.shape, F32),
        compiler_params=_params(1))(own, *([parts] * n))


def _sum_stack(parts, name):
    n = parts.shape[0]

    def body(p_ref, o_ref):
        acc = p_ref[0]
        for k in range(1, n):
            acc = acc + p_ref[k]
        o_ref[...] = acc

    return pl.pallas_call(body, name=name, out_shape=jax.ShapeDtypeStruct(parts.shape[1:], F32))(parts)


def _coords():
    return lax.axis_index("x"), lax.axis_index("y"), lax.axis_index("c")


def _chip(who):
    return 2 * who[0] + who[1]


def _flip(who, mask):
    return tuple((1 - v) if b else v for v, b in zip(who, mask))


def _transfer(transfers, t, I, O, ssem, rsem, receiving):
    tr, me = transfers[t], _coords()
    peer = _flip(me, tr["mask"])
    return pltpu.make_async_remote_copy(
        src_ref=tr["src"](I, O, me), dst_ref=tr["dst"](I, O, peer if receiving else me),
        send_sem=ssem.at[t], recv_sem=rsem.at[t], device_id=peer, device_id_type=MESH)


def _start_transfers(transfers, I, O, ssem, rsem, onward):
    arrived = set()
    for t, tr in enumerate(transfers):
        after = tr.get("after")
        if (after is not None) != onward:
            continue
        if after is not None and after not in arrived:
            _transfer(transfers, after, I, O, ssem, rsem, True).wait_recv()
            arrived.add(after)
        _transfer(transfers, t, I, O, ssem, rsem, False).start()


def _finish_transfers(transfers, I, O, ssem, rsem):
    passed_on = {tr["after"] for tr in transfers if tr.get("after") is not None}
    for t in range(len(transfers)):
        if t not in passed_on:
            _transfer(transfers, t, I, O, ssem, rsem, True).wait_recv()
    for t in range(len(transfers)):
        _transfer(transfers, t, I, O, ssem, rsem, False).wait_send()


def _own_copies(own, I, O, stage, lsem, leg):
    for n, (src, dst) in enumerate(own):
        me = _coords()
        bring =pltpu.make_async_copy(src(I, O, me), stage[n], lsem.at[2 * n])
        put = pltpu.make_async_copy(stage[n], dst(I, O, me), lsem.at[2 * n + 1])
        if leg == 0:
            bring.start()
        elif leg == 1:
            bring.wait()
            put.start()
        else:
            put.wait()


def _own_scratch(own, ins):
    return [pltpu.VMEM(ins[n].shape, ins[n].dtype) for n in range(len(own))], pltpu.SemaphoreType.DMA((max(2 * len(own), 1),))


def _exchange(name, ins, outs, transfers, own=()):
    ni, no = len(ins), len(outs)
    nt = len(transfers)
    stages, stage_sems = _own_scratch(own, ins)

    def body(*refs):
        I, O = refs[:ni], refs[ni:ni + no]
        ssem, rsem, lsem = refs[ni + no:ni + no + 3]
        stage = refs[ni + no + 3:]
        _own_copies(own, I, O, stage, lsem, 0)
        _start_transfers(transfers, I, O, ssem, rsem, False)
        _own_copies(own, I, O, stage, lsem, 1)
        _start_transfers(transfers, I, O, ssem, rsem, True)
        _finish_transfers(transfers, I, O, ssem, rsem)
        _own_copies(own, I, O, stage, lsem, 2)

    hbm = pl.BlockSpec(memory_space=pltpu.HBM)
    return pl.pallas_call(
        body, name=name, in_specs=[hbm] * ni, out_specs=[hbm] * no,
        out_shape=[jax.ShapeDtypeStruct(s, d) for s, d in outs],
        scratch_shapes=[pltpu.SemaphoreType.DMA((nt,)), pltpu.SemaphoreType.DMA((nt,)), stage_sems] + stages,
        compiler_params=pltpu.CompilerParams(has_side_effects=True, vmem_limit_bytes=VMEM_LIMIT))(*ins)


CHIP_MASKS = [(0, 1, 0), (1, 0, 0), (1, 1, 0)]
SIBLING = (0, 0, 1)


def _half(shape2d, axis, which):
    n = shape2d[axis] // 2
    cut = pl.ds(pl.multiple_of(which * n, n), n)
    return (cut, slice(None)) if axis == 0 else (slice(None), cut)


class _Riding:
    def __init__(self, transfers, ins, outs, own=()):
        self.transfers, self.ins, self.outs, self.own = transfers, list(ins), list(outs), list(own)
        hbm = pl.BlockSpec(memory_space=pltpu.HBM)
        self.in_specs, self.out_specs = [hbm] * len(self.ins), [hbm] * len(self.outs)
        self.out_shape = [jax.ShapeDtypeStruct(s, d) for s, d in self.outs]
        stages, stage_sems = _own_scratch(self.own, self.ins)
        self.scratch = [pltpu.SemaphoreType.DMA((max(len(transfers), 1),))] * 2 + [stage_sems] + stages

    def alone(self, name):
        return _exchange(name, self.ins, self.outs, self.transfers, self.own)

    def hooks(self, I, O, ssem, rsem, lsem, *stage, first, middle, last):
        tr, own = self.transfers, self.own

        @pl.when(first)
        def _():
            _own_copies(own, I, O, stage, lsem, 0)
            _start_transfers(tr, I, O, ssem, rsem, False)

        if own or any(t.get("after") is not None for t in tr):
            @pl.when(middle)
            def _():
                _own_copies(own, I, O, stage, lsem, 1)
                _start_transfers(tr, I, O, ssem, rsem, True)

        def at_end():
            @pl.when(last)
            def _():
                _finish_transfers(tr, I, O, ssem, rsem)
                _own_copies(own, I, O, stage, lsem, 2)

        return at_end


def _stretch(n, pos):
    return (pl.ds(pos * n if isinstance(pos, int) else pl.multiple_of(pos * n, n), n),)


def _gather_plan(shards, axes):
    def half(a, who):
        if shards[a].ndim == 1:
            return _stretch(shards[a].shape[0] // 2, who[2])
        return _half(shards[a].shape, axes[a], who[2])

    def landed(a, chip, who):
        if shards[a].ndim == 1:
            return _stretch(shards[a].shape[0] // 2, 2 * chip + who[2])
        return (chip,) + half(a, who)

    over_ici, onward = [], []
    for a in range(len(shards)):
        for mask in CHIP_MASKS:
            over_ici.append(dict(
                mask=mask,
                src=lambda I, O, me, a=a: I[a].at[half(a, me)],
                dst=lambda I, O, who, a=a: O[a].at[landed(a, _chip(who), who)]))
            onward.append(dict(
                mask=SIBLING, after=len(over_ici) - 1,
                src=lambda I, O, me, a=a, mask=mask: O[a].at[landed(a, _chip(_flip(me, mask)), me)],
                dst=lambda I, O, who, a=a, mask=mask: O[a].at[landed(a, _chip(_flip(who, mask)), who)]))
    outs = [((NCHIP * s.shape[0],) if s.ndim == 1 else (NCHIP,) + s.shape, s.dtype) for s in shards]

    def whole(a, chip):
        return _stretch(shards[a].shape[0], chip) if shards[a].ndim == 1 else (chip,)

    own = [(lambda I, O, me, a=a: I[a], lambda I, O, me, a=a: O[a].at[whole(a, _chip(me))])
           for a in range(len(shards))]
    return over_ici + onward, outs, own


def _gather_shards(shards, axes):
    transfers, outs, own = _gather_plan(shards, axes)
    return _exchange("gather_weights", shards, outs, transfers, own)


def _halves_plan(blocks, axes):
    def cut(a, which):
        return (slice(None),) + _half(blocks[a].shape[1:], axes[a], which)

    transfers, outs = [], []
    for a, (b, ax) in enumerate(zip(blocks, axes)):
        if b.ndim == 1:
            h = b.shape[0] // NCHIP // 2
            for k in range(NCHIP):
                transfers.append(dict(mask=SIBLING,
                                      src=lambda I, O, me, a=a, k=k, h=h: I[a].at[_stretch(h, 2 * k + 1 - me[2])],
                                      dst=lambda I, O, who, a=a, k=k, h=h: O[a].at[_stretch(h, k)]))
            outs.append(((NCHIP * h,), b.dtype))
        else:
            transfers.append(dict(mask=SIBLING, src=lambda I, O, me, a=a: I[a].at[cut(a, 1 - me[2])],
                                  dst=lambda I, O, who, a=a: O[a]))
            shape = list(b.shape)
            shape[ax + 1] //= 2
            outs.append((tuple(shape), b.dtype))
    return transfers, outs


def _scatter_plan(tb):
    def slot(a, k):
        return (k,) if tb[a].ndim == 3 else _stretch(tb[a].shape[0] // NCHIP, k)

    transfers = []
    for a in range(len(tb)):
        for n, mask in enumerate(CHIP_MASKS):
            transfers.append(dict(
                mask=mask,
                src=lambda I, O, me, a=a, mask=mask: I[a].at[slot(a, _chip(_flip(me, mask)))],
                dst=lambda I, O, who, a=a, n=n: O[a].at[slot(a, n)]))
    outs = [((3,) + t.shape[1:] if t.ndim == 3 else (3 * (t.shape[0] // NCHIP),), t.dtype) for t in tb]
    return transfers, outs


def _last_exchange(vec, halves):
    def slot(who):
        return 4 * who[0] + 2 * who[1] + who[2]

    masks = [(m >> 2 & 1, m >> 1 & 1, m & 1) for m in range(1, 8)]
    transfers = [dict(mask=mask, src=lambda I, O, me: I[0], dst=lambda I, O, who: O[0].at[slot(who)])
                 for mask in masks]
    transfers += [dict(mask=SIBLING, src=lambda I, O, me, a=a: I[a], dst=lambda I, O, who, a=a: O[a])
                  for a in range(1, 1 + len(halves))]
    own = [(lambda I, O, me: I[0], lambda I, O, me: O[0].at[slot(me)])]
    outs = [((8,) + vec.shape, vec.dtype)] + [(t.shape, t.dtype) for t in halves]
    res = _exchange("last_exchange", [vec] + list(halves), outs, transfers, own)
    return res[0], res[1:]


def _rope_tables(positions):
    half = ROT // 2
    inv_freq = jnp.power(jnp.float32(THETA), -jnp.arange(0, ROT, 2, dtype=F32) / ROT)
    ang = positions.astype(F32)[:, None] * inv_freq[None, :]
    cos, sin = jnp.cos(ang), jnp.sin(ang)
    one, zero, z8 = jnp.ones((S, HD - ROT), F32), jnp.zeros((S, HD - ROT), F32), jnp.zeros((S, half), F32)
    c = jnp.concatenate([cos, cos, one], axis=1)
    a = jnp.concatenate([-sin, z8, zero], axis=1)
    b = jnp.concatenate([z8, sin, zero], axis=1)
    return tuple(jnp.tile(t, (1, 2)) for t in (c, a, b))


def _tile_heads(g, w):
    return jnp.tile(g.reshape(1, HD), (1, w // HD))


def _fold_heads(dg):
    return dg.reshape(-1, HD).sum(axis=0)


def _pad_lanes(a):
    return jnp.pad(a, ((0, 0), (0, LANES - a.shape[1])))


def _local_step(x, target, positions, wt, fetch, late_weights, begin_reduce):
    rope = _rope_tables(positions)
    w1t = wt["w_in_a_t"]
    f_row = 3 * D // LANES
    wg_t = w1t[3 * D + NH:]
    in_b_block = lambda c: pl.BlockSpec((None, TN_WIDE, TN_), lambda j, i: (c, j, 0))
    b_pad = _pad_lanes(wt["b_forget"].reshape(1, NH))
    qg_a, kg_a = _tile_heads(wt["qnorm_a_g"], D), _tile_heads(wt["knorm_a_g"], D)
    qg_b, kg_b = _tile_heads(wt["qnorm_b_g"], D), _tile_heads(wt["knorm_b_g"], KVW)
    norm_a, kv_g, norm_b = wt["norm_a_g"].reshape(1, D), wt["kv_norm_g"].reshape(1, D), wt["norm_b_g"].reshape(1, D)
    sinks_t = jnp.repeat(wt["sinks"].reshape(1, NH), HD, axis=1)

    (u_a,) = _rmsnorm_fwd(x, [norm_a], "norm_a")
    qkv = _mm("proj_a", S, 3 * D, [(u_a, _a_rows(D), w1t, _b_rows(D, tn=TN_WIDE), NT)], tn=TN_WIDE)
    fpad = _mm("proj_f", S, LANES, [(u_a, _a_rows(D), w1t, _b_rows(D, row0=f_row, tn=LANES), NT)], tn=LANES)
    gate_a = _mm("proj_gate_a", S, D, [(u_a, _a_rows(D), wg_t, _b_rows(D, tn=TN_WIDE), NT)], tn=TN_WIDE)
    q_a, k_a, v_a = _a_post(qkv, qg_a, kg_a)
    ct = _forget_cumsum(fpad, b_pad)
    ct2 = ct[:NH].reshape(NH // 2, 2, S)
    o_a, lse_a, y_a, fetched = _fox_fwd(q_a, k_a, v_a, ct2, gate_a, fetch)
    wt = {**wt, **late_weights(fetched)}
    w_in_b = wt["w_in_b"]
    h1 = _mm("out_a", S, D, [(y_a, _a_rows(D), wt["w_out_a"], _b_cols(D, tn=TN_WIDE), None)], add=x, tn=TN_WIDE)
    u_kv, u_b = _rmsnorm_fwd(h1, [kv_g, norm_b], "norm_b")
    kv = _mm("proj_kv", S, 2 * KVW, [(u_kv, _a_rows(D), wt["w_kv"], _b_cols(D), None)])
    pb = _mm("proj_b", S, 2 * D,
             [(u_b, _a_rows(D), w_in_b, pl.BlockSpec((None, D, TN_), lambda j, i: (j, 0, 0)), None)])
    q_b, kdup, vdup = _b_post(pb, kv, qg_b, kg_b, rope)
    gate_b_col = D // LANES
    o_b, lse_b, y_b = _swa_fwd(q_b, kdup, vdup, sinks_t, pb, gate_b_col)
    out = _mm("out_b", S, D, [(y_b, _a_rows(D), wt["w_out_b"], _b_cols(D, tn=TN_WIDE), None)], add=h1, tn=TN_WIDE)
    d_out, d_out_b, sq = _loss_head(out, target)

    g = {}
    g["w_out_b"] = _mm("dw_out_b", D, D, [(y_b, _a_cols(S), d_out_b, _b_cols(S, tn=TN_WIDE), TN)], tn=TN_WIDE)
    d_y_b = _mm("dy_b", S, D, [(d_out_b, _a_rows(D), wt["w_out_b"], _b_rows(D, tn=TN_WIDE), NT)], tn=TN_WIDE)
    dq_b, dkdup, dvdup, dsk, d_gate_b = _swa_bwd(q_b, kdup, vdup, sinks_t, o_b, lse_b, d_y_b, pb, gate_b_col)
    g["sinks"] = dsk[0, ::HD]
    d_qb_raw, dg = _headnorm_bwd(pb, 0, qg_b, dq_b, rope, "qnorm_b_bwd")
    g["qnorm_b_g"] = _fold_heads(dg)
    d_pb = [d_qb_raw, d_qb_raw, d_gate_b, d_gate_b]
    g["w_in_b"] = jnp.concatenate([
        _mm("dw_in_b_q", D, D, [(u_b, _a_cols(S), d_qb_raw, _b_cols(S), TN)], stacked=True),
        _mm("dw_in_b_gate", D, D, [(u_b, _a_cols(S), d_gate_b, _b_cols(S), TN)], stacked=True)], axis=0)
    d_u_b = _mm("du_b", S, D, [(d_pb[c], _a_rows(TN_, col=c % 2), w_in_b, in_b_block(c), NT) for c in range(NCHIP)],
                tn=TN_WIDE)
    d_kv, dg = _kv_bwd(dkdup, dvdup, kv, kg_b, rope)
    g["knorm_b_g"] = _fold_heads(dg)
    g["w_kv"] = _mm("dw_kv", D, 2 * KVW, [(u_kv, _a_cols(S), d_kv, _b_cols(S), TN)])
    d_u_kv = _mm("du_kv", S, D, [(d_kv, _a_rows(2 * KVW), wt["w_kv"], _b_rows(2 * KVW, tn=TN_WIDE), NT)], tn=TN_WIDE)
    d_h1, d_h1_b, g["kv_norm_g"], g["norm_b_g"] = _rmsnorm_bwd(h1, [kv_g, norm_b], [d_u_kv, d_u_b], d_out, "norm_b_bwd")
    layer_b = {n: g[n] for n in LATE[1:]}
    g["w_out_a"], halves_b = _mm("dw_out_a", D, D, [(y_a, _a_cols(S), d_h1_b, _b_cols(S, tn=TN_WIDE), TN)],
                                 tn=TN_WIDE, riding=begin_reduce(layer_b))
    d_y_a, halves_a = _mm("dy_a", S, D, [(d_h1_b, _a_rows(D), wt["w_out_a"], _b_rows(D, tn=TN_WIDE), NT)],
                          tn=TN_WIDE, riding=begin_reduce({LATE[0]: g["w_out_a"]}))
    riding, so_far = begin_reduce({n: g[n] for n in LATE}, list(halves_a) + list(halves_b))
    dq_a, dk_a, dv_a, dct, d_gate_a, arrived = _fox_bwd(q_a, k_a, v_a, ct2, o_a, lse_a, d_y_a, gate_a, riding)
    dct_pad = jnp.pad(dct.reshape(NH, S), ((0, LANES - NH), (0, 0)))
    d_f, db = _forget_bwd(dct_pad, fpad, b_pad)
    g["b_forget"] = db[0, :NH]
    d_q_raw, dg = _headnorm_bwd(qkv, 0, qg_a, dq_a, None, "qnorm_a_bwd")
    g["qnorm_a_g"] = _fold_heads(dg)
    d_k_raw, dg = _headnorm_bwd(qkv, 1, kg_a, dk_a, None, "knorm_a_bwd")
    g["knorm_a_g"] = _fold_heads(dg)
    rows, gw = 4 * D + NH, None
    for n, t, row0 in (("q", d_q_raw, 0), ("k", d_k_raw, D), ("v", dv_a, 2 * D)):
        gw = _mm("dw_in_a_" + n, D, D, [(t, _a_cols(S), u_a, _b_cols(S, tn=TN_WIDE), TN)], tn=TN_WIDE,
                 rows_of=(gw, rows, row0))
    gw = _mm("dw_in_a_f", LANES, D, [(d_f, _a_cols(S, tm=LANES), u_a, _b_cols(S, tn=TN_WIDE), TN)], tm=LANES,
             tn=TN_WIDE, rows_of=(gw, rows, 3 * D))
    g["w_in_a"] = _mm("dw_in_a_gate", D, D, [(d_gate_a, _a_cols(S), u_a, _b_cols(S, tn=TN_WIDE), TN)], tn=TN_WIDE,
                      rows_of=(gw, rows, 3 * D + NH))
    first = {"w_in_a": g["w_in_a"]}
    riding, so_far_first = begin_reduce(first, begin_reduce(first).alone("sibling_halves_w_in_a"))
    d_u_a, arrived_first = _mm("du_a", S, D, [
        (d_q_raw, _a_rows(D), w1t, _b_cols(D, row=0, tn=TN_WIDE), None),
        (d_k_raw, _a_rows(D), w1t, _b_cols(D, row=1, tn=TN_WIDE), None),
        (dv_a, _a_rows(D), w1t, _b_cols(D, row=2, tn=TN_WIDE), None),
        (d_gate_a, _a_rows(D), wg_t, _b_cols(D, tn=TN_WIDE), None),
        (d_f, _a_rows(LANES), w1t, _b_cols(LANES, row=f_row, tn=TN_WIDE), None)], tn=TN_WIDE, riding=riding)
    d_x, _, g["norm_a_g"] = _rmsnorm_bwd(x, [norm_a], [d_u_a], d_h1, "norm_a_bwd")
    return sq, d_x, g, (list(so_far_first) + list(so_far), list(arrived_first) + list(arrived))


BIG = ["w_in_a", "w_out_a", "w_kv", "w_in_b", "w_out_b"]
LATE = BIG[1:]
SPLIT = {"w_in_a": None, "w_out_a": 0, "w_kv": 0, "w_in_b": 0, "w_out_b": 0}
SMALL = ["norm_a_g", "b_forget", "qnorm_a_g", "knorm_a_g", "kv_norm_g", "knorm_b_g", "norm_b_g", "qnorm_b_g", "sinks"]
NAMES = ["norm_a_g", "w_in_a", "b_forget", "qnorm_a_g", "knorm_a_g", "w_out_a", "kv_norm_g", "w_kv", "knorm_b_g",
         "norm_b_g", "w_in_b", "qnorm_b_g", "sinks", "w_out_b"]


def _pack(vals):
    flat = []
    for v in vals:
        v = v.reshape(-1)
        flat.append(jnp.pad(v, (0, -v.shape[0] % LANES)))
    flat = jnp.concatenate(flat)
    flat = jnp.pad(flat, (0, -flat.shape[0] % (8 * LANES)))
    return flat.reshape(-1, LANES)


def _unpack(packed, shapes):
    flat, out, off = packed.reshape(-1), [], 0
    for s in shapes:
        n = int(np.prod(s))
        out.append(flat[off:off + n].reshape(s))
        off += n + (-n % LANES)
    return out


def kernel(x, positions, norm_a_g, w_in_a, b_forget, qnorm_a_g, knorm_a_g, w_out_a, kv_norm_g, w_kv, knorm_b_g, norm_b_g, w_in_b, qnorm_b_g, sinks, w_out_b, loss_target, m_norm_a_g, m_w_in_a, m_b_forget, m_qnorm_a_g, m_knorm_a_g, m_w_out_a, m_kv_norm_g, m_w_kv, m_knorm_b_g, m_norm_b_g, m_w_in_b, m_qnorm_b_g, m_sinks, m_w_out_b, v_norm_a_g, v_w_in_a, v_b_forget, v_qnorm_a_g, v_knorm_a_g, v_w_out_a, v_kv_norm_g, v_w_kv, v_knorm_b_g, v_norm_b_g, v_w_in_b, v_qnorm_b_g, v_sinks, v_w_out_b):
    w = dict(norm_a_g=norm_a_g, w_in_a=w_in_a, b_forget=b_forget, qnorm_a_g=qnorm_a_g, knorm_a_g=knorm_a_g,
             w_out_a=w_out_a, kv_norm_g=kv_norm_g, w_kv=w_kv, knorm_b_g=knorm_b_g, norm_b_g=norm_b_g,
             w_in_b=w_in_b, qnorm_b_g=qnorm_b_g, sinks=sinks, w_out_b=w_out_b)
    m = dict(norm_a_g=m_norm_a_g, w_in_a=m_w_in_a, b_forget=m_b_forget, qnorm_a_g=m_qnorm_a_g, knorm_a_g=m_knorm_a_g,
             w_out_a=m_w_out_a, kv_norm_g=m_kv_norm_g, w_kv=m_w_kv, knorm_b_g=m_knorm_b_g, norm_b_g=m_norm_b_g,
             w_in_b=m_w_in_b, qnorm_b_g=m_qnorm_b_g, sinks=m_sinks, w_out_b=m_w_out_b)
    v = dict(norm_a_g=v_norm_a_g, w_in_a=v_w_in_a, b_forget=v_b_forget, qnorm_a_g=v_qnorm_a_g, knorm_a_g=v_knorm_a_g,
             w_out_a=v_w_out_a, kv_norm_g=v_kv_norm_g, w_kv=v_w_kv, knorm_b_g=v_knorm_b_g, norm_b_g=v_norm_b_g,
             w_in_b=v_w_in_b, qnorm_b_g=v_qnorm_b_g, sinks=v_sinks, w_out_b=v_w_out_b)
    my_chip = 2 * lax.axis_index("x") + lax.axis_index("y")

    def shard2d(t, n):
        if n == "w_in_a":
            return jnp.transpose(t, (2, 0, 1)).reshape(-1)
        return t.reshape(t.shape[-2:])

    def unflat(t, n):
        return jnp.transpose(t.reshape(-1, 1, D), (1, 2, 0)) if n == "w_in_a" else t.reshape(w[n].shape)

    w2d = {n: shard2d(w[n], n) for n in BIG}

    norm_a_rows = jnp.broadcast_to(norm_a_g.reshape(1, D // NCHIP), (2 * SUBLANES, D // NCHIP))
    w1t, norm_rows = _gather_shards([w2d["w_in_a"].astype(BF16), norm_a_rows], [SPLIT["w_in_a"], 0])
    wt = {"w_in_a_t": w1t.reshape(-1, D), "norm_a_g": norm_rows[:, 0, :].reshape(1, D)}
    for n in SMALL[1:]:
        wt[n] = w[n]
    late_shards = [w2d[n].astype(BF16) for n in LATE]
    late_axes = [SPLIT[n] for n in LATE]
    transfers, outs, own = _gather_plan(late_shards, late_axes)
    fetch = _Riding(transfers, late_shards, outs, own)

    def late_weights(fetched):
        return {n: t if n == "w_in_b" else t.reshape(-1, t.shape[2]) for n, t in zip(LATE, fetched)}

    def as_blocks(t):
        if t.ndim == 3:
            return t
        return t.reshape(-1) if t.shape[0] % (SUBLANES * NCHIP) else t.reshape(NCHIP, -1, t.shape[1])

    def begin_reduce(grads, halves=None):
        names = list(grads)
        axes = [SPLIT[n] for n in names]
        blocks = [as_blocks(grads[n]) for n in names]
        if halves is None:
            transfers, outs = _halves_plan(blocks, axes)
            return _Riding(transfers, blocks, outs)
        sums = [_chip_sum(blk, part, ax, "chip_sum_" + n) for n, ax, blk, part in zip(names, axes, blocks, halves)]
        bf16 = [s[1] for s in sums]
        transfers, outs = _scatter_plan(bf16)
        return _Riding(transfers, bf16, outs), [s[0] for s in sums]

    sq, d_x, g, (chip_f32, arrived) = _local_step(x[0], loss_target[0], positions, wt, fetch, late_weights,
                                                  begin_reduce)

    axes = [SPLIT[n] for n in BIG]
    halves = []
    for n, ax, t32, parts in zip(BIG, axes, chip_f32, arrived):
        if t32.ndim == 1:
            own = lax.dynamic_slice_in_dim(t32, my_chip * (t32.shape[0] // NCHIP), t32.shape[0] // NCHIP)
        else:
            own = lax.dynamic_index_in_dim(t32, my_chip, axis=0, keepdims=False)
        halves.append(_mesh_sum(own, parts, ax, "mesh_sum_" + n))

    small_shapes = [(D,), (NH,), (HD,), (HD,), (D,), (HD,), (D,), (HD,), (NH,), (D,)]
    gathered_small, sibling_done = _last_exchange(_pack([g[n] for n in SMALL] + [sq]), halves)
    total = _sum_stack(gathered_small, "sum_small")
    small_g = dict(zip(SMALL, _unpack(total, small_shapes)[:-1]))
    loss = 0.5 * jnp.sum(_unpack(total, small_shapes)[-1]) / D
    small_g["norm_a_g"] = lax.dynamic_slice(small_g["norm_a_g"], (my_chip * (D // NCHIP),), (D // NCHIP,))

    res = {}
    for n, ax, mine_half, their_half in zip(BIG, axes, halves, sibling_done):
        out4 = _adamw_halves(w2d[n], mine_half, their_half, shard2d(m[n], n), shard2d(v[n], n), ax, "adamw_" + n)
        res[n] = tuple(unflat(t, n) for t in out4)
    row = lambda t: t.reshape(1, -1)
    small_out = _adamw_small(*[[row(d[n]) for n in SMALL] for d in (w, small_g, m, v)])
    for i, n in enumerate(SMALL):
        res[n] = tuple(t.reshape(w[n].shape) for t in (small_g[n],) + tuple(out[i] for out in small_out))

    outs = [loss, d_x[None]]
    for k in range(4):
        outs += [res[n][k] for n in NAMES]
    return tuple(outs)
```

```python
import numpy as np
import jax
import jax.numpy as jnp
from jax import lax
from jax.experimental import pallas as pl
from jax.experimental.pallas import tpu as pltpu

F32, BF16 = jnp.float32, jnp.bfloat16
S, D, HD, NH, NKV = 2048, 1024, 64, 16, 4
KVW = NKV * HD
WINDOW = 128
ROT = HD // 4
THETA = 500000.0
EPS = 1e-6
SCALE = HD ** -0.5
LANES = 128
SUBLANES = 8
NEG = -1e30
VMEM_LIMIT = 48 * 2 ** 20
ROWS = 512
ATT = 512
SWQ = 16
NCHIP = 4
ADAM_LR, ADAM_B1, ADAM_B2, ADAM_EPS, ADAM_WD, ADAM_STEP = 0.001, 0.9, 0.999, 1e-08, 0.01, 10
NT = (((1,), (1,)), ((), ()))
TN = (((0,), (0,)), ((), ()))
MESH = pl.DeviceIdType.MESH


def _params(n):
    return pltpu.CompilerParams(dimension_semantics=("arbitrary",) * n, vmem_limit_bytes=VMEM_LIMIT)


def _dot(a, b, dims=None):
    if dims is None:
        return jnp.dot(a, b, preferred_element_type=F32)
    return lax.dot_general(a, b, dims, preferred_element_type=F32)


def _dot_split(a, b, n):
    out, rest = None, a
    for _ in range(n):
        hi = rest.astype(BF16)
        term = _dot(hi, b)
        out = term if out is None else out + term
        rest = rest - hi.astype(F32)
    return out


def _seg_mat(w):
    e = (np.arange(w)[:, None] // HD == np.arange(LANES)[None, :]).astype(np.float32)
    return jnp.asarray(e, BF16)


def _spread(r, w):
    head = lax.broadcasted_iota(jnp.int32, (2 * LANES, w), 1) >> (HD.bit_length() - 1)
    row = lax.broadcasted_iota(jnp.int32, (2 * LANES, w), 0)
    et2 = jnp.where(head == (row & (LANES - 1)), 1.0, 0.0).astype(BF16)
    hi = r.astype(BF16)
    lo = (r - hi.astype(F32)).astype(BF16)
    return _dot(jnp.concatenate([hi, lo], axis=1), et2)


def _head_rstd(x, e):
    ss = _dot_split(x * x, e, 2)
    return _spread(lax.rsqrt(ss * (1.0 / HD) + EPS), x.shape[1])


def _rope(x, c, a, b):
    w = x.shape[1]
    return x * c + pltpu.roll(x, w - ROT // 2, 1) * a + pltpu.roll(x, ROT // 2, 1) * b


def _rope_t(dy, c, a, b):
    w = dy.shape[1]
    return dy * c + pltpu.roll(dy * b, w - ROT // 2, 1) + pltpu.roll(dy * a, ROT // 2, 1)


def _sigmoid(x):
    return 1.0 / (1.0 + jnp.exp(-x))


def _row_spec(shape, ts):
    nd = len(shape)
    if shape[0] == S:
        return pl.BlockSpec((ts,) + tuple(shape[1:]), lambda i: (i,) + (0,) * (nd - 1))
    return pl.BlockSpec(tuple(shape), lambda i: (0,) * nd)


def _rows_call(body, name, ins, outs, ts=ROWS):
    return pl.pallas_call(
        body, name=name, grid=(S // ts,),
        in_specs=[_row_spec(a.shape, ts) for a in ins],
        out_specs=[_row_spec(s, ts) for s, _ in outs],
        out_shape=[jax.ShapeDtypeStruct(s, d) for s, d in outs],
        compiler_params=_params(1))(*ins)


def _col_spec(ts, w, col):
    return pl.BlockSpec((ts, w), lambda i: (i, col))


TM = TN_ = 512
TM_TOKENS = 1024
TN_WIDE = 1024


def _mm(name, m, n, terms, out_dtype=F32, add=None, tm=None, tn=TN_, stacked=False, riding=None, rows_of=None):
    nterm = len(terms)
    if tm is None:
        tm = TM_TOKENS if m == S else TM
    nj, ni_ = n // tn, m // tm
    n_in = 2 * nterm + (add is not None) + (rows_of is not None and rows_of[0] is not None)
    r_in, r_out = (len(riding.ins), len(riding.outs)) if riding is not None else (0, 0)

    def body(*refs):
        if riding is not None:
            j, i = pl.program_id(0), pl.program_id(1)
            at_end = riding.hooks(refs[n_in:n_in + r_in], refs[n_in + r_in + 1:n_in + r_in + 1 + r_out],
                                  *refs[n_in + r_in + 1 + r_out:], first=(j == 0) & (i == 0),
                                  middle=(j == nj // 2) & (i == 0), last=(j == nj - 1) & (i == ni_ - 1))
        acc = None
        for t in range(nterm):
            part = _dot(refs[2 * t][...], refs[2 * t + 1][...], terms[t][4])
            acc = part if acc is None else acc + part
        if add is not None:
            acc = acc + refs[2 * nterm][...]
        refs[n_in + r_in][...] = acc.astype(out_dtype)
        if riding is not None:
            at_end()

    tile = pl.BlockSpec((tm, tn), lambda j, i: (i, j))
    ins, specs = [], []
    for a, a_spec, b, b_spec, _ in terms:
        ins += [a, b]
        specs += [a_spec, b_spec]
    if add is not None:
        ins.append(add)
        specs.append(tile)
    out_spec = pl.BlockSpec((None, tm, tn), lambda j, i: (j, i, 0)) if stacked else tile
    out_shape = jax.ShapeDtypeStruct((nj, m, tn) if stacked else (m, n), out_dtype)
    if rows_of is not None:
        taller, rows, row0 = rows_of
        out_spec = pl.BlockSpec((pl.Element(tm), pl.Element(tn)), lambda j, i: (
            pl.multiple_of(row0 + i * tm, SUBLANES), pl.multiple_of(j * tn, LANES)))
        out_shape = jax.ShapeDtypeStruct((rows, n), out_dtype)
        alias = {}
        if taller is not None:
            ins.append(taller)
            specs.append(pl.BlockSpec(memory_space=pltpu.HBM))
            alias = {len(ins) - 1: 0}
        return pl.pallas_call(body, name=name, grid=(nj, ni_), in_specs=specs, out_specs=out_spec,
                              out_shape=out_shape, input_output_aliases=alias, compiler_params=_params(2))(*ins)
    if riding is None:
        return pl.pallas_call(body, name=name, grid=(nj, ni_), in_specs=specs, out_specs=out_spec,
                              out_shape=out_shape, compiler_params=_params(2))(*ins)
    res = pl.pallas_call(
        body, name=name, grid=(nj, ni_), in_specs=specs + riding.in_specs,
        out_specs=[out_spec] + riding.out_specs, out_shape=[out_shape] + riding.out_shape,
        scratch_shapes=riding.scratch, compiler_params=_params(2))(*ins, *riding.ins)
    return res[0], res[1:]


def _a_rows(k, col=0, tm=TM_TOKENS):
    return pl.BlockSpec((tm, k), lambda j, i: (i, col))


def _a_cols(k, tm=TM):
    return pl.BlockSpec((k, tm), lambda j, i: (0, i))


def _b_cols(k, row=0, col0=0, tn=TN_):
    return pl.BlockSpec((k, tn), lambda j, i: (row, col0 + j))


def _b_rows(k, row0=0, tn=TN_):
    return pl.BlockSpec((tn, k), lambda j, i: (row0 + j, 0))


def _rmsnorm_fwd(x, gains, name):
    def body(*refs):
        xv = refs[0][...]
        r = lax.rsqrt(jnp.mean(xv * xv, axis=-1, keepdims=True) + EPS)
        xh = xv * r
        for n in range(len(gains)):
            refs[1 + len(gains) + n][...] = (xh * refs[1 + n][...]).astype(BF16)

    return _rows_call(body, name, [x] + list(gains), [((S, D), BF16)] * len(gains))


def _rmsnorm_bwd(x, gains, dus, dres, name):
    n = len(gains)

    def body(*refs):
        x_ref, g_refs, du_refs, dres_ref = refs[0], refs[1:1 + n], refs[1 + n:1 + 2 * n], refs[1 + 2 * n]
        dx_ref, dxb_ref, dg_refs = refs[2 + 2 * n], refs[3 + 2 * n], refs[4 + 2 * n:]
        xv = x_ref[...]
        r = lax.rsqrt(jnp.mean(xv * xv, axis=-1, keepdims=True) + EPS)
        xh = xv * r
        gy = None
        for m in range(n):
            du = du_refs[m][...]
            part = jnp.sum(du * xh, axis=0, keepdims=True)

            @pl.when(pl.program_id(0) == 0)
            def _(m=m, part=part):
                dg_refs[m][...] = part

            @pl.when(pl.program_id(0) != 0)
            def _(m=m, part=part):
                dg_refs[m][...] += part

            t = du * g_refs[m][...]
            gy = t if gy is None else gy + t
        dx = dres_ref[...] + r * (gy - xh * jnp.mean(gy * xh, axis=-1, keepdims=True))
        dx_ref[...] = dx
        dxb_ref[...] = dx.astype(BF16)

    outs = [((S, D), F32), ((S, D), BF16)] + [((1, D), F32)] * n
    return _rows_call(body, name, [x] + list(gains) + list(dus) + [dres], outs)


def _a_post(qkvg, qg, kg):
    e = _seg_mat(D)

    def body(q_ref, k_ref, v_ref, qg_ref, kg_ref, e_ref, qo, ko, vo):
        ev = e_ref[...]
        qv, kv = q_ref[...], k_ref[...]
        qo[...] = (qv * _head_rstd(qv, ev) * qg_ref[...] * SCALE).astype(BF16)
        ko[...] = (kv * _head_rstd(kv, ev) * kg_ref[...]).astype(BF16)
        vo[...] = v_ref[...].astype(BF16)

    whole = lambda a: pl.BlockSpec(a.shape, lambda i: (0, 0))
    return pl.pallas_call(
        body, name="a_post", grid=(S // ROWS,),
        in_specs=[_col_spec(ROWS, D, 0), _col_spec(ROWS, D, 1), _col_spec(ROWS, D, 2),
                  whole(qg), whole(kg), whole(e)],
        out_specs=[_col_spec(ROWS, D, 0)] * 3,
        out_shape=[jax.ShapeDtypeStruct((S, D), BF16)] * 3,
        compiler_params=_params(1))(qkvg, qkvg, qkvg, qg, kg, e)


def _tri(upper):
    r, c = np.arange(ROWS)[:, None], np.arange(ROWS)[None, :]
    return jnp.asarray((r <= c) if upper else (r >= c), BF16)


def _forget_cumsum(fpad, bpad):
    def body(f_ref, b_ref, u_ref, c_ref, carry):
        @pl.when(pl.program_id(0) == 0)
        def _():
            carry[...] = jnp.zeros_like(carry)

        lf = jax.nn.log_sigmoid(f_ref[...] + b_ref[...])
        blk = _dot_split(lf.T, u_ref[...], 3) + carry[:, 0:1]
        c_ref[...] = blk
        carry[...] = jnp.broadcast_to(blk[:, ROWS - 1:ROWS], carry.shape)

    return pl.pallas_call(
        body, name="forget_cumsum", grid=(S // ROWS,),
        in_specs=[pl.BlockSpec((ROWS, LANES), lambda i: (i, 0)), pl.BlockSpec((1, LANES), lambda i: (0, 0)),
                  pl.BlockSpec((ROWS, ROWS), lambda i: (0, 0))],
        out_specs=pl.BlockSpec((LANES, ROWS), lambda i: (0, i)),
        out_shape=jax.ShapeDtypeStruct((LANES, S), F32),
        scratch_shapes=[pltpu.VMEM((LANES, LANES), F32)],
        compiler_params=_params(1))(fpad, bpad, _tri(True))


def _forget_bwd(dct, fpad, bpad):
    nb = S // ROWS

    def body(dc_ref, f_ref, b_ref, l_ref, df_ref, db_ref, carry):
        @pl.when(pl.program_id(0) == 0)
        def _():
            carry[...] = jnp.zeros_like(carry)
            db_ref[...] = jnp.zeros_like(db_ref)

        blk = _dot_split(dc_ref[...], l_ref[...], 3) + carry[:, 0:1]
        carry[...] = jnp.broadcast_to(blk[:, 0:1], carry.shape)
        df = blk.T * _sigmoid(-(f_ref[...] + b_ref[...]))
        df_ref[...] = df.astype(BF16)
        db_ref[...] += jnp.sum(df, axis=0, keepdims=True)

    return pl.pallas_call(
        body, name="forget_bwd", grid=(nb,),
        in_specs=[pl.BlockSpec((LANES, ROWS), lambda i: (0, nb - 1 - i)),
                  pl.BlockSpec((ROWS, LANES), lambda i: (nb - 1 - i, 0)),
                  pl.BlockSpec((1, LANES), lambda i: (0, 0)), pl.BlockSpec((ROWS, ROWS), lambda i: (0, 0))],
        out_specs=[pl.BlockSpec((ROWS, LANES), lambda i: (nb - 1 - i, 0)), pl.BlockSpec((1, LANES), lambda i: (0, 0))],
        out_shape=[jax.ShapeDtypeStruct((S, LANES), BF16), jax.ShapeDtypeStruct((1, LANES), F32)],
        scratch_shapes=[pltpu.VMEM((LANES, LANES), F32)],
        compiler_params=_params(1))(dct, fpad, bpad, _tri(False))


def _headnorm_bwd(x, col, gain, dy, rope, name):
    e = _seg_mat(D)
    tabs = list(rope) if rope is not None else []

    def body(*refs):
        x_ref, g_ref, dy_ref, e_ref = refs[:4]
        dx_ref, dg_ref = refs[-2:]
        xv, dyv, ev = x_ref[...], dy_ref[...], e_ref[...]
        if rope is not None:
            c, a, b = (jnp.tile(t[...], (1, D // LANES)) for t in refs[4:7])
            dyv = _rope_t(dyv, c, a, b)
        r = _head_rstd(xv, ev)
        xh = xv * r
        part = jnp.sum(dyv * xh, axis=0, keepdims=True)

        @pl.when(pl.program_id(0) == 0)
        def _():
            dg_ref[...] = part

        @pl.when(pl.program_id(0) != 0)
        def _():
            dg_ref[...] += part

        gy = dyv * g_ref[...]
        seg = _spread(_dot_split(gy * xh, ev, 2) * (1.0 / HD), D)
        dx_ref[...] = (r * (gy - xh * seg)).astype(BF16)

    whole = lambda a: pl.BlockSpec(a.shape, lambda i: (0, 0))
    return pl.pallas_call(
        body, name=name, grid=(S // ROWS,),
        in_specs=[_col_spec(ROWS, D, col), whole(gain), _col_spec(ROWS, D, 0), whole(e)]
                 + [pl.BlockSpec((ROWS, LANES), lambda i: (i, 0))] * len(tabs),
        out_specs=[_col_spec(ROWS, D, 0), whole(gain)],
        out_shape=[jax.ShapeDtypeStruct((S, D), BF16), jax.ShapeDtypeStruct((1, D), F32)],
        compiler_params=_params(1))(x, gain, dy, e, *tabs)


def _dup_mat():
    r, c = np.arange(KVW)[:, None], np.arange(2 * KVW)[None, :]
    return (r // HD == c // LANES) & (r % HD == c % HD)


def _fold_mat():
    r, c = np.arange(D)[:, None], np.arange(KVW)[None, :]
    return (r // (2 * LANES) == c // HD) & (r % HD == c % HD)


def _b_post(pb, kv, qg, kg, rope):
    e, ek = _seg_mat(D), _seg_mat(KVW)
    dup = jnp.asarray(_dup_mat(), BF16)

    def body(q_ref, k_ref, v_ref, qg_ref, kg_ref, e_ref, ek_ref, dup_ref, c_ref, a_ref, b_ref, qo, ko, vo):
        c1, a1, b1 = c_ref[...], a_ref[...], b_ref[...]
        qv = q_ref[...]
        qn = qv * _head_rstd(qv, e_ref[...]) * qg_ref[...]
        t = lambda z, n: jnp.tile(z, (1, n))
        qo[...] = (_rope(qn, t(c1, D // LANES), t(a1, D // LANES), t(b1, D // LANES)) * SCALE).astype(BF16)
        kvv = k_ref[...]
        kn = kvv * _head_rstd(kvv, ek_ref[...]) * kg_ref[...]
        kr = _rope(kn, t(c1, KVW // LANES), t(a1, KVW // LANES), t(b1, KVW // LANES)).astype(BF16)
        ko[...] = _dot(kr, dup_ref[...]).astype(BF16)
        vo[...] = _dot(v_ref[...].astype(BF16), dup_ref[...]).astype(BF16)

    whole = lambda a: pl.BlockSpec(a.shape, lambda i: (0, 0))
    tab = pl.BlockSpec((ROWS, LANES), lambda i: (i, 0))
    return pl.pallas_call(
        body, name="b_post", grid=(S // ROWS,),
        in_specs=[_col_spec(ROWS, D, 0), _col_spec(ROWS, KVW, 0), _col_spec(ROWS, KVW, 1),
                  whole(qg), whole(kg), whole(e), whole(ek), whole(dup), tab, tab, tab],
        out_specs=[_col_spec(ROWS, D, 0), _col_spec(ROWS, 2 * KVW, 0), _col_spec(ROWS, 2 * KVW, 0)],
        out_shape=[jax.ShapeDtypeStruct((S, D), BF16), jax.ShapeDtypeStruct((S, 2 * KVW), BF16),
                   jax.ShapeDtypeStruct((S, 2 * KVW), BF16)],
        compiler_params=_params(1))(pb, kv, kv, qg, kg, e, ek, dup, *rope)


def _kv_bwd(dkdup, dvdup, kv, kg, rope):
    ek = _seg_mat(KVW)
    fold = jnp.asarray(_fold_mat(), BF16)

    def body(dk_ref, dv_ref, k_ref, kg_ref, ek_ref, fold_ref, c_ref, a_ref, b_ref, dkv_ref, dg_ref):
        ev, fv = ek_ref[...], fold_ref[...]
        t = lambda z: jnp.tile(z[...], (1, KVW // LANES))
        dk = _rope_t(_dot_split(dk_ref[...], fv, 2), t(c_ref), t(a_ref), t(b_ref))
        dv = _dot_split(dv_ref[...], fv, 2)
        xv = k_ref[...]
        r = _head_rstd(xv, ev)
        xh = xv * r
        part = jnp.sum(dk * xh, axis=0, keepdims=True)

        @pl.when(pl.program_id(0) == 0)
        def _():
            dg_ref[...] = part

        @pl.when(pl.program_id(0) != 0)
        def _():
            dg_ref[...] += part

        gy = dk * kg_ref[...]
        seg = _spread(_dot_split(gy * xh, ev, 2) * (1.0 / HD), KVW)
        dkv_ref[:, 0:KVW] = (r * (gy - xh * seg)).astype(BF16)
        dkv_ref[:, KVW:2 * KVW] = dv.astype(BF16)

    whole = lambda a: pl.BlockSpec(a.shape, lambda i: (0, 0))
    tab = pl.BlockSpec((ROWS, LANES), lambda i: (i, 0))
    return pl.pallas_call(
        body, name="kv_bwd", grid=(S // ROWS,),
        in_specs=[_col_spec(ROWS, D, 0), _col_spec(ROWS, D, 0), _col_spec(ROWS, KVW, 0),
                  whole(kg), whole(ek), whole(fold), tab, tab, tab],
        out_specs=[_col_spec(ROWS, 2 * KVW, 0), whole(kg)],
        out_shape=[jax.ShapeDtypeStruct((S, 2 * KVW), BF16), jax.ShapeDtypeStruct((1, KVW), F32)],
        compiler_params=_params(1))(dkdup, dvdup, kv, kg, ek, fold, *rope)


def _out_loss(y, w_out, residual, target):
    def body(y_ref, w_ref, r_ref, t_ref, d_ref, db_ref, l_ref):
        diff = _dot(y_ref[...], w_ref[...]) + r_ref[...] - t_ref[...]
        d = diff * (1.0 / D)
        d_ref[...] = d
        db_ref[...] = d.astype(BF16)

        @pl.when(pl.program_id(0) == 0)
        def _():
            l_ref[...] = jnp.zeros_like(l_ref)

        l_ref[...] += jnp.sum(diff * diff, axis=0, keepdims=True)

    rows = pl.BlockSpec((TM_TOKENS, D), lambda i: (i, 0))
    whole = pl.BlockSpec((D, D), lambda i: (0, 0))
    return pl.pallas_call(
        body, name="out_b_loss", grid=(S // TM_TOKENS,), in_specs=[rows, whole, rows, rows],
        out_specs=[rows, rows, pl.BlockSpec((1, D), lambda i: (0, 0))],
        out_shape=[jax.ShapeDtypeStruct((S, D), F32), jax.ShapeDtypeStruct((S, D), BF16),
                   jax.ShapeDtypeStruct((1, D), F32)],
        compiler_params=_params(1))(y, w_out, residual, target)


def _lane():
    return lax.broadcasted_iota(jnp.int32, (1, LANES), 1)


def _head_mask(hh):
    return (_lane() < HD) if hh == 0 else (_lane() >= HD)


def _fox_fwd(q, k, v, ct, gate, riding):
    nq, npair = S // ATT, NH // 2
    ni, no = len(riding.ins), len(riding.outs)

    def body(q_ref, k_ref, v_ref, c_ref, gate_ref, *rest):
        o_ref, lse_ref, y_ref = rest[ni:ni + 3]
        pair, i = pl.program_id(0), pl.program_id(1)
        at_end = riding.hooks(rest[:ni], rest[ni + 3:ni + 3 + no], *rest[ni + 3 + no:],
                              first=(pair == 0) & (i == 0), middle=(pair == npair // 2) & (i == 0),
                              last=(pair == npair - 1) & (i == nq - 1))
        q2 = q_ref[...]
        qms = [jnp.where(_head_mask(hh), q2, jnp.zeros_like(q2)) for hh in (0, 1)]

        def probs(off, width, m, hh, diag):
            s = _dot(qms[hh], k_ref[pl.ds(off, width), :], NT) - c_ref[hh:hh + 1, pl.ds(off, width)]
            if diag:
                row = i * ATT + lax.broadcasted_iota(jnp.int32, (ATT, width), 0)
                col = off + lax.broadcasted_iota(jnp.int32, (ATT, width), 1)
                s = jnp.where(col <= row, s, NEG)
            m_new = jnp.maximum(m, jnp.max(s, axis=1, keepdims=True))
            p = jnp.exp(s - m_new)
            p_hi = p.astype(BF16)
            return m_new, jnp.exp(m - m_new), p_hi, (p - p_hi.astype(F32)).astype(BF16)

        def weighted(off, width, p_hi, p_lo, hh):
            vj = v_ref[pl.ds(off, width), :]
            v1 = jnp.where(_head_mask(hh), vj, jnp.ones_like(vj))
            return _dot(p_hi, v1) + _dot(p_lo, v1)

        def step(off, width, carry, diag):
            off = pl.multiple_of(off, ATT)
            out = []
            for hh in (0, 1):
                m, acc = carry[hh]
                m, alpha, p_hi, p_lo = probs(off, width, m, hh, diag)
                out.append((m, alpha * acc + weighted(off, width, p_hi, p_lo, hh)))
            return tuple(out)

        one = (jnp.full((ATT, 1), NEG, F32), jnp.zeros((ATT, LANES), F32))
        carry = lax.fori_loop(0, i // 2, lambda j, cr: step(j * (2 * ATT), 2 * ATT, cr, False), (one, one))
        carry = lax.cond(i % 2 == 1, lambda cr: step((i - 1) * ATT, 2 * ATT, cr, True),
                         lambda cr: step(i * ATT, ATT, cr, True), carry)
        res = []
        for hh in (0, 1):
            m, acc = carry[hh]
            l = jnp.max(jnp.where(_head_mask(1 - hh), acc, 0.0), axis=1, keepdims=True)
            res.append((acc / l, m + jnp.log(l)))
        first = _head_mask(0)
        o = jnp.where(first, res[0][0], res[1][0])
        o_ref[...] = o
        lse_ref[...] = jnp.where(first, res[0][1], res[1][1])
        g = gate_ref[...]
        y_ref[...] = (o * (g * _sigmoid(g))).astype(BF16)
        at_end()

    blk = pl.BlockSpec((ATT, LANES), lambda p, i: (i, p))
    full = pl.BlockSpec((S, LANES), lambda p, i: (0, p))
    res = pl.pallas_call(
        body, name="fox_fwd", grid=(npair, nq),
        in_specs=[blk, full, full, pl.BlockSpec((None, 2, S), lambda p, i: (p, 0, 0)), blk] + riding.in_specs,
        out_specs=[blk, blk, blk] + riding.out_specs,
        out_shape=[jax.ShapeDtypeStruct((S, D), F32)] * 2 + [jax.ShapeDtypeStruct((S, D), BF16)] + riding.out_shape,
        scratch_shapes=riding.scratch,
        compiler_params=_params(2))(q, k, v, ct, gate, *riding.ins)
    return res[0], res[1], res[2], res[3:]


def _gate_grads(dy, o, g):
    sg = _sigmoid(g)
    return dy * (g * sg), dy * o * (sg * (1.0 + g * (1.0 - sg)))


def _fox_bwd(q, k, v, ct, o, lse, dy, gate, riding):
    nq, npair = S // ATT, NH // 2
    ni, no = len(riding.ins), len(riding.outs)

    def body(q_ref, k_ref, v_ref, c_ref, o_ref, lse_ref, dy_ref, gate_ref, *rest):
        dq_ref, dk_ref, dvb_ref, dc_ref, dgate_ref = rest[ni:ni + 5]
        dv_ref = rest[ni + 5 + no]
        pair, i = pl.program_id(0), pl.program_id(1)
        at_end = riding.hooks(rest[:ni], rest[ni + 5:ni + 5 + no], *rest[ni + 6 + no:],
                              first=(pair == 0) & (i == 0), middle=(pair == npair // 2) & (i == 0),
                              last=(pair == npair - 1) & (i == nq - 1))

        @pl.when(i == 0)
        def _():
            dk_ref[...] = jnp.zeros_like(dk_ref)
            dv_ref[...] = jnp.zeros_like(dv_ref)
            dc_ref[...] = jnp.zeros_like(dc_ref)

        q2, lse2 = q_ref[...], lse_ref[...]
        do2, dgate = _gate_grads(dy_ref[...], o_ref[...], gate_ref[...])
        dgate_ref[...] = dgate.astype(BF16)
        do2b = do2.astype(BF16)
        prod = do2b.astype(F32) * o_ref[...]
        heads = []
        for hh in (0, 1):
            hm = _head_mask(hh)
            heads.append((jnp.where(hm, q2, jnp.zeros_like(q2)), jnp.where(hm, do2b, jnp.zeros_like(do2b)),
                          jnp.sum(jnp.where(hm, prod, 0.0), axis=1, keepdims=True),
                          jnp.max(jnp.where(hm, lse2, NEG), axis=1, keepdims=True)))

        def step(off, width, dqs, diag):
            off = pl.multiple_of(off, ATT)
            kj, vj = k_ref[pl.ds(off, width), :], v_ref[pl.ds(off, width), :]
            dk, dv, out = None, None, []
            for hh in (0, 1):
                qm, dom, delta, lse_h = heads[hh]
                s = _dot(qm, kj, NT) - c_ref[hh:hh + 1, pl.ds(off, width)]
                p = jnp.exp(s - lse_h)
                if diag:
                    row = i * ATT + lax.broadcasted_iota(jnp.int32, (ATT, width), 0)
                    col = off + lax.broadcasted_iota(jnp.int32, (ATT, width), 1)
                    p = jnp.where(col <= row, p, 0.0)
                ds = p * (_dot(dom, vj, NT) - delta)
                dc_ref[hh:hh + 1, pl.ds(off, width)] += -jnp.sum(ds, axis=0, keepdims=True)
                dsb = ds.astype(BF16)
                dk_h, dv_h = _dot(dsb, qm, TN), _dot(p.astype(BF16), dom, TN)
                dk, dv = (dk_h, dv_h) if dk is None else (dk + dk_h, dv + dv_h)
                out.append(dqs[hh] + _dot(dsb, kj))
            dk_ref[pl.ds(off, width), :] += dk
            dv_ref[pl.ds(off, width), :] += dv
            return tuple(out)

        zero = jnp.zeros((ATT, LANES), F32)
        dqs = lax.fori_loop(0, i // 2, lambda j, acc: step(j * (2 * ATT), 2 * ATT, acc, False), (zero, zero))
        dqs = lax.cond(i % 2 == 1, lambda acc: step((i - 1) * ATT, 2 * ATT, acc, True),
                       lambda acc: step(i * ATT, ATT, acc, True), dqs)
        dq_ref[...] = jnp.where(_head_mask(0), dqs[0], dqs[1]) * SCALE

        @pl.when(i == nq - 1)
        def _():
            dvb_ref[...] = dv_ref[...].astype(BF16)

        at_end()

    blk = pl.BlockSpec((ATT, LANES), lambda p, i: (i, p))
    full = pl.BlockSpec((S, LANES), lambda p, i: (0, p))
    cspec = pl.BlockSpec((None, 2, S), lambda p, i: (p, 0, 0))
    res = pl.pallas_call(
        body, name="fox_bwd", grid=(npair, nq),
        in_specs=[blk, full, full, cspec, blk, blk, blk, blk] + riding.in_specs,
        out_specs=[blk, full, full, cspec, blk] + riding.out_specs,
        out_shape=[jax.ShapeDtypeStruct((S, D), F32)] * 2 + [jax.ShapeDtypeStruct((S, D), BF16),
                                                              jax.ShapeDtypeStruct((npair, 2, S), F32),
                                                              jax.ShapeDtypeStruct((S, D), BF16)]
                  + riding.out_shape,
        scratch_shapes=[pltpu.VMEM((S, LANES), F32)] + riding.scratch,
        compiler_params=_params(2))(q, k, v, ct, o, lse, dy, gate, *riding.ins)
    return res[0], res[1], res[2], res[3], res[4], res[5:]


def _both_heads(x):
    return jnp.concatenate([jnp.where(_head_mask(hh), x, jnp.zeros_like(x)) for hh in (0, 1)], axis=0)


def _per_head(col0, col1):
    return jnp.concatenate([jnp.broadcast_to(col0, (WINDOW, 1)), jnp.broadcast_to(col1, (WINDOW, 1))], axis=0)


def _unstack(x2):
    return jnp.where(_head_mask(0), x2[:WINDOW], x2[WINDOW:])


def _swa_valid(i, start):
    r = lax.broadcasted_iota(jnp.int32, (2 * WINDOW, 2 * WINDOW), 0)
    qabs = i * WINDOW + jnp.where(r >= WINDOW, r - WINDOW, r)
    kabs = start + lax.broadcasted_iota(jnp.int32, (2 * WINDOW, 2 * WINDOW), 1)
    return (kabs <= qabs) & (qabs - kabs < WINDOW)


def _swa_fwd(q, kdup, vdup, sinks_t, proj, gate_col):
    def body(q_ref, k_ref, v_ref, sk_ref, gate_ref, o_ref, lse_ref, y_ref):
        skv = sk_ref[...]
        first = _head_mask(0)
        for sb in range(SWQ):
            i = pl.program_id(1) * SWQ + sb
            rows = slice(sb * WINDOW, (sb + 1) * WINDOW)
            start = pl.multiple_of(jnp.maximum(i - 1, 0) * WINDOW, WINDOW)
            kk, vv = k_ref[pl.ds(start, 2 * WINDOW), :], v_ref[pl.ds(start, 2 * WINDOW), :]
            q2 = q_ref[rows, :]
            valid = _swa_valid(i, start)[:WINDOW]
            res = []
            for hh in (0, 1):
                hm = _head_mask(hh)
                sink = jnp.max(jnp.where(hm, skv, NEG), axis=1, keepdims=True)
                s = jnp.where(valid, _dot(jnp.where(hm, q2, jnp.zeros_like(q2)), kk, NT), NEG)
                m = jnp.maximum(jnp.max(s, axis=1, keepdims=True), sink)
                p = jnp.exp(s - m)
                l = jnp.sum(p, axis=1, keepdims=True) + jnp.exp(sink - m)
                res.append((_dot(p.astype(BF16), vv) / l, m + jnp.log(l)))
            o = jnp.where(first, res[0][0], res[1][0])
            o_ref[rows, :] = o
            lse_ref[rows, :] = jnp.where(first, res[0][1], res[1][1])
            g = gate_ref[rows, :]
            y_ref[rows, :] = (o * (g * _sigmoid(g))).astype(BF16)

    blk = pl.BlockSpec((SWQ * WINDOW, LANES), lambda p, i: (i, p))
    gate = pl.BlockSpec((SWQ * WINDOW, LANES), lambda p, i: (i, gate_col + p))
    full = pl.BlockSpec((S, LANES), lambda p, i: (0, p // 2))
    return pl.pallas_call(
        body, name="swa_fwd", grid=(NH // 2, S // (SWQ * WINDOW)),
        in_specs=[blk, full, full, pl.BlockSpec((1, LANES), lambda p, i: (0, p)), gate],
        out_specs=[blk, blk, blk],
        out_shape=[jax.ShapeDtypeStruct((S, D), F32)] * 2 + [jax.ShapeDtypeStruct((S, D), BF16)],
        compiler_params=_params(2))(q, kdup, vdup, sinks_t, proj)


def _swa_bwd(q, kdup, vdup, sinks_t, o, lse, dy, proj, gate_col):
    def body(q_ref, k_ref, v_ref, sk_ref, o_ref, lse_ref, dy_ref, gate_ref, dq_ref, dk_ref, dv_ref, dsk_ref,
             dgate_ref):
        @pl.when(pl.program_id(1) == 0)
        def _():
            dk_ref[...] = jnp.zeros_like(dk_ref)
            dv_ref[...] = jnp.zeros_like(dv_ref)
            dsk_ref[...] = jnp.zeros_like(dsk_ref)

        skv = sk_ref[...]
        first = _head_mask(0)
        sink = _per_head(*[jnp.max(jnp.where(_head_mask(hh), skv, NEG), axis=1, keepdims=True) for hh in (0, 1)])
        for sb in range(SWQ):
            i = pl.program_id(1) * SWQ + sb
            rows = slice(sb * WINDOW, (sb + 1) * WINDOW)
            start = pl.multiple_of(jnp.maximum(i - 1, 0) * WINDOW, WINDOW)
            kk, vv = k_ref[pl.ds(start, 2 * WINDOW), :], v_ref[pl.ds(start, 2 * WINDOW), :]
            do2, dgate = _gate_grads(dy_ref[rows, :], o_ref[rows, :], gate_ref[rows, :])
            dgate_ref[rows, :] = dgate.astype(BF16)
            do2b = do2.astype(BF16)
            prod, lse2 = do2b.astype(F32) * o_ref[rows, :], lse_ref[rows, :]
            qs, dos = _both_heads(q_ref[rows, :]), _both_heads(do2b)
            delta = jnp.concatenate([jnp.sum(jnp.where(_head_mask(hh), prod, 0.0), axis=1, keepdims=True)
                                     for hh in (0, 1)], axis=0)
            lse_h = jnp.concatenate([jnp.max(jnp.where(_head_mask(hh), lse2, NEG), axis=1, keepdims=True)
                                     for hh in (0, 1)], axis=0)
            p = jnp.where(_swa_valid(i, start), jnp.exp(_dot(qs, kk, NT) - lse_h), 0.0)
            dsb = (p * (_dot(dos, vv, NT) - delta)).astype(BF16)
            dk_ref[pl.ds(start, 2 * WINDOW), :] += _dot(dsb, qs, TN)
            dv_ref[pl.ds(start, 2 * WINDOW), :] += _dot(p.astype(BF16), dos, TN)
            dq_ref[rows, :] = _unstack(_dot(dsb, kk)) * SCALE
            t = jnp.exp(sink - lse_h) * delta
            dsk_ref[...] += -jnp.where(first, jnp.sum(t[:WINDOW], axis=0, keepdims=True),
                                       jnp.sum(t[WINDOW:], axis=0, keepdims=True))

    blk = pl.BlockSpec((SWQ * WINDOW, LANES), lambda p, i: (i, p))
    full = pl.BlockSpec((S, LANES), lambda p, i: (0, p // 2))
    acc = pl.BlockSpec((S, LANES), lambda p, i: (0, p))
    sk = pl.BlockSpec((1, LANES), lambda p, i: (0, p))
    gate = pl.BlockSpec((SWQ * WINDOW, LANES), lambda p, i: (i, gate_col + p))
    return pl.pallas_call(
        body, name="swa_bwd", grid=(NH // 2, S // (SWQ * WINDOW)),
        in_specs=[blk, full, full, sk, blk, blk, blk, gate],
        out_specs=[blk, acc, acc, sk, blk],
        out_shape=[jax.ShapeDtypeStruct((S, D), F32)] * 3 + [jax.ShapeDtypeStruct((1, D), F32),
                                                              jax.ShapeDtypeStruct((S, D), BF16)],
        compiler_params=_params(2))(q, kdup, vdup, sinks_t, o, lse, dy, proj)


def _adamw_math(w, g, m, v):
    m = ADAM_B1 * m + (1.0 - ADAM_B1) * g
    v = ADAM_B2 * v + (1.0 - ADAM_B2) * jnp.square(g)
    m_hat = m / (1.0 - ADAM_B1 ** ADAM_STEP)
    v_hat = v / (1.0 - ADAM_B2 ** ADAM_STEP)
    delta = -ADAM_LR * (m_hat / (jnp.sqrt(v_hat) + ADAM_EPS) + ADAM_WD * w)
    return delta, m, v


def _adamw_small(ws, gs, ms, vs):
    k = len(ws)

    def body(*refs):
        for p in range(k):
            w_ref, g_ref, m_ref, v_ref = (refs[q * k + p] for q in range(4))
            d, mo, vo = _adamw_math(w_ref[...], g_ref[...], m_ref[...], v_ref[...])
            refs[4 * k + p][...], refs[5 * k + p][...], refs[6 * k + p][...] = d, mo, vo

    res = pl.pallas_call(
        body, name="adamw_small",
        out_shape=[jax.ShapeDtypeStruct(t.shape, F32) for t in ws] * 3)(*ws, *gs, *ms, *vs)
    return res[:k], res[k:2 * k], res[2 * k:]


SUM_TILES = (512, 256, 128)


FLAT_BLOCK = 257 * 1024


def _tiles(shape, axis, lead=0, halves=False):
    if len(shape) == 1:
        count = shape[0] // FLAT_BLOCK
        return (FLAT_BLOCK,), count, lambda pos, *lead_idx: (sum(k * count for k in lead_idx) + pos,)
    r, c = shape
    tile = next(t for t in SUM_TILES if (shape[axis] // (2 if halves else 1)) % t == 0)
    blk = (tile, c) if axis == 0 else (r, tile)
    count = shape[axis] // tile

    def index(pos, *lead_idx):
        return tuple(lead_idx) + ((pos, 0) if axis == 0 else (0, pos))

    return (None,) * lead + blk, count, index


def _adamw_halves(w, g_mine, g_theirs, m, v, axis, name):
    blk, count, index = _tiles(w.shape, axis, halves=True)
    per_half = count // 2

    def body(w_ref, a_ref, b_ref, m_ref, v_ref, g_ref, d_ref, mo_ref, vo_ref):
        is_mine = pl.program_id(0) // per_half == lax.axis_index("c")
        g = jnp.where(is_mine, a_ref[...], b_ref[...])
        g_ref[...] = g
        d_ref[...], mo_ref[...], vo_ref[...] = _adamw_math(w_ref[...], g, m_ref[...], v_ref[...])

    spec = pl.BlockSpec(blk, lambda i: index(i))
    half = pl.BlockSpec(blk, lambda i: index(i % per_half))
    return pl.pallas_call(
        body, name=name, grid=(count,), in_specs=[spec, half, half, spec, spec], out_specs=[spec] * 4,
        out_shape=[jax.ShapeDtypeStruct(w.shape, F32)] * 4, compiler_params=_params(1))(w, g_mine, g_theirs, m, v)


def _chip_sum(blocks, from_sibling, axis, name):
    flat = blocks.ndim == 1
    blk, count, index = _tiles((from_sibling.shape[0] // NCHIP,) if flat else from_sibling.shape[1:], axis, lead=1)

    def body(lo_ref, hi_ref, p_ref, o32, o16):
        mine = jnp.where(lax.axis_index("c") == 0, lo_ref[...], hi_ref[...])
        acc = mine + p_ref[...]
        o32[...] = acc
        o16[...] = acc.astype(BF16)

    half = pl.BlockSpec(blk, lambda k, i: index(i, k))
    if flat:
        lo = pl.BlockSpec(blk, lambda k, i: (2 * count * k + i,))
        hi = pl.BlockSpec(blk, lambda k, i: (2 * count * k + count + i,))
    else:
        lo, hi = half, pl.BlockSpec(blk, lambda k, i: index(i + count, k))
    return pl.pallas_call(
        body, name=name, grid=(NCHIP, count), in_specs=[lo, hi, half], out_specs=[half, half],
        out_shape=[jax.ShapeDtypeStruct(from_sibling.shape, F32), jax.ShapeDtypeStruct(from_sibling.shape, BF16)],
        compiler_params=_params(2))(blocks, blocks, from_sibling)


def _mesh_sum(own, parts, axis, name):
    blk, count, index = _tiles(own.shape, axis)
    n = NCHIP - 1

    def body(a_ref, *refs):
        acc = a_ref[...]
        for k in range(n):
            acc = acc + refs[k][...].astype(F32)
        refs[n][...] = acc

    spec = pl.BlockSpec(blk, lambda i: index(i))
    if own.ndim == 1:
        part = [pl.BlockSpec(blk, lambda i, k=k: (k * count + i,)) for k in range(n)]
    else:
        part = [pl.BlockSpec((None,) + blk, lambda i, k=k: (k,) + index(i)) for k in range(n)]
    return pl.pallas_call(
        body, name=name, grid=(count,), in_specs=[spec] + part,
        out_specs=spec, out_shape=jax.ShapeDtypeStruct(own.shape, F32),
        compiler_params=_params(1))(own, *([parts] * n))


def _sum_stack(parts, name):
    n = parts.shape[0]

    def body(p_ref, o_ref):
        acc = p_ref[0]
        for k in range(1, n):
            acc = acc + p_ref[k]
        o_ref[...] = acc

    return pl.pallas_call(body, name=name, out_shape=jax.ShapeDtypeStruct(parts.shape[1:], F32))(parts)


def _coords():
    return lax.axis_index("x"), lax.axis_index("y"), lax.axis_index("c")


def _chip(who):
    return 2 * who[0] + who[1]


def _flip(who, mask):
    return tuple((1 - v) if b else v for v, b in zip(who, mask))


def _transfer(transfers, t, I, O, ssem, rsem, receiving):
    tr, me = transfers[t], _coords()
    peer = _flip(me, tr["mask"])
    return pltpu.make_async_remote_copy(
        src_ref=tr["src"](I, O, me), dst_ref=tr["dst"](I, O, peer if receiving else me),
        send_sem=ssem.at[t], recv_sem=rsem.at[t], device_id=peer, device_id_type=MESH)


def _start_transfers(transfers, I, O, ssem, rsem, onward):
    arrived = set()
    for t, tr in enumerate(transfers):
        after = tr.get("after")
        if (after is not None) != onward:
            continue
        if after is not None and after not in arrived:
            _transfer(transfers, after, I, O, ssem, rsem, True).wait_recv()
            arrived.add(after)
        _transfer(transfers, t, I, O, ssem, rsem, False).start()


def _finish_transfers(transfers, I, O, ssem, rsem):
    passed_on = {tr["after"] for tr in transfers if tr.get("after") is not None}
    for t in range(len(transfers)):
        if t not in passed_on:
            _transfer(transfers, t, I, O, ssem, rsem, True).wait_recv()
    for t in range(len(transfers)):
        _transfer(transfers, t, I, O, ssem, rsem, False).wait_send()


def _own_copies(own, I, O, stage, lsem, leg):
    for n, (src, dst) in enumerate(own):
        me = _coords()
        bring =pltpu.make_async_copy(src(I, O, me), stage[n], lsem.at[2 * n])
        put = pltpu.make_async_copy(stage[n], dst(I, O, me), lsem.at[2 * n + 1])
        if leg == 0:
            bring.start()
        elif leg == 1:
            bring.wait()
            put.start()
        else:
            put.wait()


def _own_scratch(own, ins):
    return [pltpu.VMEM(ins[n].shape, ins[n].dtype) for n in range(len(own))], pltpu.SemaphoreType.DMA((max(2 * len(own), 1),))


def _exchange(name, ins, outs, transfers, own=()):
    ni, no = len(ins), len(outs)
    nt = len(transfers)
    stages, stage_sems = _own_scratch(own, ins)

    def body(*refs):
        I, O = refs[:ni], refs[ni:ni + no]
        ssem, rsem, lsem = refs[ni + no:ni + no + 3]
        stage = refs[ni + no + 3:]
        _own_copies(own, I, O, stage, lsem, 0)
        _start_transfers(transfers, I, O, ssem, rsem, False)
        _own_copies(own, I, O, stage, lsem, 1)
        _start_transfers(transfers, I, O, ssem, rsem, True)
        _finish_transfers(transfers, I, O, ssem, rsem)
        _own_copies(own, I, O, stage, lsem, 2)

    hbm = pl.BlockSpec(memory_space=pltpu.HBM)
    return pl.pallas_call(
        body, name=name, in_specs=[hbm] * ni, out_specs=[hbm] * no,
        out_shape=[jax.ShapeDtypeStruct(s, d) for s, d in outs],
        scratch_shapes=[pltpu.SemaphoreType.DMA((nt,)), pltpu.SemaphoreType.DMA((nt,)), stage_sems] + stages,
        compiler_params=pltpu.CompilerParams(has_side_effects=True, vmem_limit_bytes=VMEM_LIMIT))(*ins)


CHIP_MASKS = [(0, 1, 0), (1, 0, 0), (1, 1, 0)]
SIBLING = (0, 0, 1)


def _half(shape2d, axis, which):
    n = shape2d[axis] // 2
    cut = pl.ds(pl.multiple_of(which * n, n), n)
    return (cut, slice(None)) if axis == 0 else (slice(None), cut)


class _Riding:
    def __init__(self, transfers, ins, outs, own=()):
        self.transfers, self.ins, self.outs, self.own = transfers, list(ins), list(outs), list(own)
        hbm = pl.BlockSpec(memory_space=pltpu.HBM)
        self.in_specs, self.out_specs = [hbm] * len(self.ins), [hbm] * len(self.outs)
        self.out_shape = [jax.ShapeDtypeStruct(s, d) for s, d in self.outs]
        stages, stage_sems = _own_scratch(self.own, self.ins)
        self.scratch = [pltpu.SemaphoreType.DMA((max(len(transfers), 1),))] * 2 + [stage_sems] + stages

    def alone(self, name):
        return _exchange(name, self.ins, self.outs, self.transfers, self.own)

    def hooks(self, I, O, ssem, rsem, lsem, *stage, first, middle, last):
        tr, own = self.transfers, self.own

        @pl.when(first)
        def _():
            _own_copies(own, I, O, stage, lsem, 0)
            _start_transfers(tr, I, O, ssem, rsem, False)

        if own or any(t.get("after") is not None for t in tr):
            @pl.when(middle)
            def _():
                _own_copies(own, I, O, stage, lsem, 1)
                _start_transfers(tr, I, O, ssem, rsem, True)

        def at_end():
            @pl.when(last)
            def _():
                _finish_transfers(tr, I, O, ssem, rsem)
                _own_copies(own, I, O, stage, lsem, 2)

        return at_end


def _stretch(n, pos):
    return (pl.ds(pos * n if isinstance(pos, int) else pl.multiple_of(pos * n, n), n),)


def _gather_plan(shards, axes):
    def half(a, who):
        if shards[a].ndim == 1:
            return _stretch(shards[a].shape[0] // 2, who[2])
        return _half(shards[a].shape, axes[a], who[2])

    def landed(a, chip, who):
        if shards[a].ndim == 1:
            return _stretch(shards[a].shape[0] // 2, 2 * chip + who[2])
        return (chip,) + half(a, who)

    over_ici, onward = [], []
    for a in range(len(shards)):
        for mask in CHIP_MASKS:
            over_ici.append(dict(
                mask=mask,
                src=lambda I, O, me, a=a: I[a].at[half(a, me)],
                dst=lambda I, O, who, a=a: O[a].at[landed(a, _chip(who), who)]))
            onward.append(dict(
                mask=SIBLING, after=len(over_ici) - 1,
                src=lambda I, O, me, a=a, mask=mask: O[a].at[landed(a, _chip(_flip(me, mask)), me)],
                dst=lambda I, O, who, a=a, mask=mask: O[a].at[landed(a, _chip(_flip(who, mask)), who)]))
    outs = [((NCHIP * s.shape[0],) if s.ndim == 1 else (NCHIP,) + s.shape, s.dtype) for s in shards]

    def whole(a, chip):
        return _stretch(shards[a].shape[0], chip) if shards[a].ndim == 1 else (chip,)

    own = [(lambda I, O, me, a=a: I[a], lambda I, O, me, a=a: O[a].at[whole(a, _chip(me))])
           for a in range(len(shards))]
    return over_ici + onward, outs, own


def _gather_shards(shards, axes):
    transfers, outs, own = _gather_plan(shards, axes)
    return _exchange("gather_weights", shards, outs, transfers, own)


def _halves_plan(blocks, axes):
    def cut(a, which):
        return (slice(None),) + _half(blocks[a].shape[1:], axes[a], which)

    transfers, outs = [], []
    for a, (b, ax) in enumerate(zip(blocks, axes)):
        if b.ndim == 1:
            h = b.shape[0] // NCHIP // 2
            for k in range(NCHIP):
                transfers.append(dict(mask=SIBLING,
                                      src=lambda I, O, me, a=a, k=k, h=h: I[a].at[_stretch(h, 2 * k + 1 - me[2])],
                                      dst=lambda I, O, who, a=a, k=k, h=h: O[a].at[_stretch(h, k)]))
            outs.append(((NCHIP * h,), b.dtype))
        else:
            transfers.append(dict(mask=SIBLING, src=lambda I, O, me, a=a: I[a].at[cut(a, 1 - me[2])],
                                  dst=lambda I, O, who, a=a: O[a]))
            shape = list(b.shape)
            shape[ax + 1] //= 2
            outs.append((tuple(shape), b.dtype))
    return transfers, outs


def _scatter_plan(tb):
    def slot(a, k):
        return (k,) if tb[a].ndim == 3 else _stretch(tb[a].shape[0] // NCHIP, k)

    transfers = []
    for a in range(len(tb)):
        for n, mask in enumerate(CHIP_MASKS):
            transfers.append(dict(
                mask=mask,
                src=lambda I, O, me, a=a, mask=mask: I[a].at[slot(a, _chip(_flip(me, mask)))],
                dst=lambda I, O, who, a=a, n=n: O[a].at[slot(a, n)]))
    outs = [((3,) + t.shape[1:] if t.ndim == 3 else (3 * (t.shape[0] // NCHIP),), t.dtype) for t in tb]
    return transfers, outs


def _last_exchange(vec, halves):
    def slot(who):
        return 4 * who[0] + 2 * who[1] + who[2]

    masks = [(m >> 2 & 1, m >> 1 & 1, m & 1) for m in range(1, 8)]
    transfers = [dict(mask=mask, src=lambda I, O, me: I[0], dst=lambda I, O, who: O[0].at[slot(who)])
                 for mask in masks]
    transfers += [dict(mask=SIBLING, src=lambda I, O, me, a=a: I[a], dst=lambda I, O, who, a=a: O[a])
                  for a in range(1, 1 + len(halves))]
    own = [(lambda I, O, me: I[0], lambda I, O, me: O[0].at[slot(me)])]
    outs = [((8,) + vec.shape, vec.dtype)] + [(t.shape, t.dtype) for t in halves]
    res = _exchange("last_exchange", [vec] + list(halves), outs, transfers, own)
    return res[0], res[1:]


def _rope_tables(positions):
    half = ROT // 2
    inv_freq = jnp.power(jnp.float32(THETA), -jnp.arange(0, ROT, 2, dtype=F32) / ROT)
    ang = positions.astype(F32)[:, None] * inv_freq[None, :]
    cos, sin = jnp.cos(ang), jnp.sin(ang)
    one, zero, z8 = jnp.ones((S, HD - ROT), F32), jnp.zeros((S, HD - ROT), F32), jnp.zeros((S, half), F32)
    c = jnp.concatenate([cos, cos, one], axis=1)
    a = jnp.concatenate([-sin, z8, zero], axis=1)
    b = jnp.concatenate([z8, sin, zero], axis=1)
    return tuple(jnp.tile(t, (1, 2)) for t in (c, a, b))


def _tile_heads(g, w):
    return jnp.tile(g.reshape(1, HD), (1, w // HD))


def _fold_heads(dg):
    return dg.reshape(-1, HD).sum(axis=0)


def _pad_lanes(a):
    return jnp.pad(a, ((0, 0), (0, LANES - a.shape[1])))


def _local_step(x, target, positions, wt, fetch, late_weights, begin_reduce):
    rope = _rope_tables(positions)
    w1t = wt["w_in_a_t"]
    f_row = 3 * D // LANES
    wg_t = w1t[3 * D + NH:]
    in_b_block = lambda c: pl.BlockSpec((None, TN_WIDE, TN_), lambda j, i: (c, j, 0))
    b_pad = _pad_lanes(wt["b_forget"].reshape(1, NH))
    qg_a, kg_a = _tile_heads(wt["qnorm_a_g"], D), _tile_heads(wt["knorm_a_g"], D)
    qg_b, kg_b = _tile_heads(wt["qnorm_b_g"], D), _tile_heads(wt["knorm_b_g"], KVW)
    norm_a, kv_g, norm_b = wt["norm_a_g"].reshape(1, D), wt["kv_norm_g"].reshape(1, D), wt["norm_b_g"].reshape(1, D)
    sinks_t = jnp.repeat(wt["sinks"].reshape(1, NH), HD, axis=1)

    (u_a,) = _rmsnorm_fwd(x, [norm_a], "norm_a")
    qkv = _mm("proj_a", S, 3 * D, [(u_a, _a_rows(D), w1t, _b_rows(D, tn=TN_WIDE), NT)], tn=TN_WIDE)
    fpad = _mm("proj_f", S, LANES, [(u_a, _a_rows(D), w1t, _b_rows(D, row0=f_row, tn=LANES), NT)], tn=LANES)
    gate_a = _mm("proj_gate_a", S, D, [(u_a, _a_rows(D), wg_t, _b_rows(D, tn=TN_WIDE), NT)], tn=TN_WIDE)
    q_a, k_a, v_a = _a_post(qkv, qg_a, kg_a)
    ct = _forget_cumsum(fpad, b_pad)
    ct2 = ct[:NH].reshape(NH // 2, 2, S)
    o_a, lse_a, y_a, fetched = _fox_fwd(q_a, k_a, v_a, ct2, gate_a, fetch)
    wt = {**wt, **late_weights(fetched)}
    w_in_b = wt["w_in_b"]
    h1 = _mm("out_a", S, D, [(y_a, _a_rows(D), wt["w_out_a"], _b_cols(D, tn=TN_WIDE), None)], add=x, tn=TN_WIDE)
    u_kv, u_b = _rmsnorm_fwd(h1, [kv_g, norm_b], "norm_b")
    kv = _mm("proj_kv", S, 2 * KVW, [(u_kv, _a_rows(D), wt["w_kv"], _b_cols(D), None)])
    pb = _mm("proj_b", S, 2 * D,
             [(u_b, _a_rows(D), w_in_b, pl.BlockSpec((None, D, TN_), lambda j, i: (j, 0, 0)), None)])
    q_b, kdup, vdup = _b_post(pb, kv, qg_b, kg_b, rope)
    gate_b_col = D // LANES
    o_b, lse_b, y_b = _swa_fwd(q_b, kdup, vdup, sinks_t, pb, gate_b_col)
    d_out, d_out_b, sq = _out_loss(y_b, wt["w_out_b"], h1, target)

    g = {}
    g["w_out_b"] = _mm("dw_out_b", D, D, [(y_b, _a_cols(S), d_out_b, _b_cols(S, tn=TN_WIDE), TN)], tn=TN_WIDE)
    d_y_b = _mm("dy_b", S, D, [(d_out_b, _a_rows(D), wt["w_out_b"], _b_rows(D, tn=TN_WIDE), NT)], tn=TN_WIDE)
    dq_b, dkdup, dvdup, dsk, d_gate_b = _swa_bwd(q_b, kdup, vdup, sinks_t, o_b, lse_b, d_y_b, pb, gate_b_col)
    g["sinks"] = dsk[0, ::HD]
    d_qb_raw, dg = _headnorm_bwd(pb, 0, qg_b, dq_b, rope, "qnorm_b_bwd")
    g["qnorm_b_g"] = _fold_heads(dg)
    d_pb = [d_qb_raw, d_qb_raw, d_gate_b, d_gate_b]
    g["w_in_b"] = jnp.concatenate([
        _mm("dw_in_b_q", D, D, [(u_b, _a_cols(S), d_qb_raw, _b_cols(S), TN)], stacked=True),
        _mm("dw_in_b_gate", D, D, [(u_b, _a_cols(S), d_gate_b, _b_cols(S), TN)], stacked=True)], axis=0)
    d_u_b = _mm("du_b", S, D, [(d_pb[c], _a_rows(TN_, col=c % 2), w_in_b, in_b_block(c), NT) for c in range(NCHIP)],
                tn=TN_WIDE)
    d_kv, dg = _kv_bwd(dkdup, dvdup, kv, kg_b, rope)
    g["knorm_b_g"] = _fold_heads(dg)
    g["w_kv"] = _mm("dw_kv", D, 2 * KVW, [(u_kv, _a_cols(S), d_kv, _b_cols(S), TN)])
    d_u_kv = _mm("du_kv", S, D, [(d_kv, _a_rows(2 * KVW), wt["w_kv"], _b_rows(2 * KVW, tn=TN_WIDE), NT)], tn=TN_WIDE)
    d_h1, d_h1_b, g["kv_norm_g"], g["norm_b_g"] = _rmsnorm_bwd(h1, [kv_g, norm_b], [d_u_kv, d_u_b], d_out, "norm_b_bwd")
    g["w_out_a"] = _mm("dw_out_a", D, D, [(y_a, _a_cols(S), d_h1_b, _b_cols(S, tn=TN_WIDE), TN)], tn=TN_WIDE)
    late = {n: g[n] for n in LATE}
    d_y_a, halves = _mm("dy_a", S, D, [(d_h1_b, _a_rows(D), wt["w_out_a"], _b_rows(D, tn=TN_WIDE), NT)],
                        tn=TN_WIDE, riding=begin_reduce(late))
    riding, so_far = begin_reduce(late, halves)
    dq_a, dk_a, dv_a, dct, d_gate_a, arrived = _fox_bwd(q_a, k_a, v_a, ct2, o_a, lse_a, d_y_a, gate_a, riding)
    dct_pad = jnp.pad(dct.reshape(NH, S), ((0, LANES - NH), (0, 0)))
    d_f, db = _forget_bwd(dct_pad, fpad, b_pad)
    g["b_forget"] = db[0, :NH]
    d_q_raw, dg = _headnorm_bwd(qkv, 0, qg_a, dq_a, None, "qnorm_a_bwd")
    g["qnorm_a_g"] = _fold_heads(dg)
    d_k_raw, dg = _headnorm_bwd(qkv, 1, kg_a, dk_a, None, "knorm_a_bwd")
    g["knorm_a_g"] = _fold_heads(dg)
    rows, gw = 4 * D + NH, None
    for n, t, row0 in (("q", d_q_raw, 0), ("k", d_k_raw, D), ("v", dv_a, 2 * D)):
        gw = _mm("dw_in_a_" + n, D, D, [(t, _a_cols(S), u_a, _b_cols(S, tn=TN_WIDE), TN)], tn=TN_WIDE,
                 rows_of=(gw, rows, row0))
    gw = _mm("dw_in_a_f", LANES, D, [(d_f, _a_cols(S, tm=LANES), u_a, _b_cols(S, tn=TN_WIDE), TN)], tm=LANES,
             tn=TN_WIDE, rows_of=(gw, rows, 3 * D))
    g["w_in_a"] = _mm("dw_in_a_gate", D, D, [(d_gate_a, _a_cols(S), u_a, _b_cols(S, tn=TN_WIDE), TN)], tn=TN_WIDE,
                      rows_of=(gw, rows, 3 * D + NH))
    first = {"w_in_a": g["w_in_a"]}
    riding, so_far_first = begin_reduce(first, begin_reduce(first).alone("sibling_halves_w_in_a"))
    d_u_a, arrived_first = _mm("du_a", S, D, [
        (d_q_raw, _a_rows(D), w1t, _b_cols(D, row=0, tn=TN_WIDE), None),
        (d_k_raw, _a_rows(D), w1t, _b_cols(D, row=1, tn=TN_WIDE), None),
        (dv_a, _a_rows(D), w1t, _b_cols(D, row=2, tn=TN_WIDE), None),
        (d_gate_a, _a_rows(D), wg_t, _b_cols(D, tn=TN_WIDE), None),
        (d_f, _a_rows(LANES), w1t, _b_cols(LANES, row=f_row, tn=TN_WIDE), None)], tn=TN_WIDE, riding=riding)
    d_x, _, g["norm_a_g"] = _rmsnorm_bwd(x, [norm_a], [d_u_a], d_h1, "norm_a_bwd")
    return sq, d_x, g, (list(so_far_first) + list(so_far), list(arrived_first) + list(arrived))


BIG = ["w_in_a", "w_out_a", "w_kv", "w_in_b", "w_out_b"]
LATE = BIG[1:]
SPLIT = {"w_in_a": None, "w_out_a": 0, "w_kv": 0, "w_in_b": 0, "w_out_b": 0}
SMALL = ["norm_a_g", "b_forget", "qnorm_a_g", "knorm_a_g", "kv_norm_g", "knorm_b_g", "norm_b_g", "qnorm_b_g", "sinks"]
NAMES = ["norm_a_g", "w_in_a", "b_forget", "qnorm_a_g", "knorm_a_g", "w_out_a", "kv_norm_g", "w_kv", "knorm_b_g",
         "norm_b_g", "w_in_b", "qnorm_b_g", "sinks", "w_out_b"]


def _pack(vals):
    flat = []
    for v in vals:
        v = v.reshape(-1)
        flat.append(jnp.pad(v, (0, -v.shape[0] % LANES)))
    flat = jnp.concatenate(flat)
    flat = jnp.pad(flat, (0, -flat.shape[0] % (8 * LANES)))
    return flat.reshape(-1, LANES)


def _unpack(packed, shapes):
    flat, out, off = packed.reshape(-1), [], 0
    for s in shapes:
        n = int(np.prod(s))
        out.append(flat[off:off + n].reshape(s))
        off += n + (-n % LANES)
    return out


def kernel(x, positions, norm_a_g, w_in_a, b_forget, qnorm_a_g, knorm_a_g, w_out_a, kv_norm_g, w_kv, knorm_b_g, norm_b_g, w_in_b, qnorm_b_g, sinks, w_out_b, loss_target, m_norm_a_g, m_w_in_a, m_b_forget, m_qnorm_a_g, m_knorm_a_g, m_w_out_a, m_kv_norm_g, m_w_kv, m_knorm_b_g, m_norm_b_g, m_w_in_b, m_qnorm_b_g, m_sinks, m_w_out_b, v_norm_a_g, v_w_in_a, v_b_forget, v_qnorm_a_g, v_knorm_a_g, v_w_out_a, v_kv_norm_g, v_w_kv, v_knorm_b_g, v_norm_b_g, v_w_in_b, v_qnorm_b_g, v_sinks, v_w_out_b):
    w = dict(norm_a_g=norm_a_g, w_in_a=w_in_a, b_forget=b_forget, qnorm_a_g=qnorm_a_g, knorm_a_g=knorm_a_g,
             w_out_a=w_out_a, kv_norm_g=kv_norm_g, w_kv=w_kv, knorm_b_g=knorm_b_g, norm_b_g=norm_b_g,
             w_in_b=w_in_b, qnorm_b_g=qnorm_b_g, sinks=sinks, w_out_b=w_out_b)
    m = dict(norm_a_g=m_norm_a_g, w_in_a=m_w_in_a, b_forget=m_b_forget, qnorm_a_g=m_qnorm_a_g, knorm_a_g=m_knorm_a_g,
             w_out_a=m_w_out_a, kv_norm_g=m_kv_norm_g, w_kv=m_w_kv, knorm_b_g=m_knorm_b_g, norm_b_g=m_norm_b_g,
             w_in_b=m_w_in_b, qnorm_b_g=m_qnorm_b_g, sinks=m_sinks, w_out_b=m_w_out_b)
    v = dict(norm_a_g=v_norm_a_g, w_in_a=v_w_in_a, b_forget=v_b_forget, qnorm_a_g=v_qnorm_a_g, knorm_a_g=v_knorm_a_g,
             w_out_a=v_w_out_a, kv_norm_g=v_kv_norm_g, w_kv=v_w_kv, knorm_b_g=v_knorm_b_g, norm_b_g=v_norm_b_g,
             w_in_b=v_w_in_b, qnorm_b_g=v_qnorm_b_g, sinks=v_sinks, w_out_b=v_w_out_b)
    my_chip = 2 * lax.axis_index("x") + lax.axis_index("y")

    def shard2d(t, n):
        if n == "w_in_a":
            return jnp.transpose(t, (2, 0, 1)).reshape(-1)
        return t.reshape(t.shape[-2:])

    def unflat(t, n):
        return jnp.transpose(t.reshape(-1, 1, D), (1, 2, 0)) if n == "w_in_a" else t.reshape(w[n].shape)

    w2d = {n: shard2d(w[n], n) for n in BIG}

    norm_a_rows = jnp.broadcast_to(norm_a_g.reshape(1, D // NCHIP), (2 * SUBLANES, D // NCHIP))
    w1t, norm_rows = _gather_shards([w2d["w_in_a"].astype(BF16), norm_a_rows], [SPLIT["w_in_a"], 0])
    wt = {"w_in_a_t": w1t.reshape(-1, D), "norm_a_g": norm_rows[:, 0, :].reshape(1, D)}
    for n in SMALL[1:]:
        wt[n] = w[n]
    late_shards = [w2d[n].astype(BF16) for n in LATE]
    late_axes = [SPLIT[n] for n in LATE]
    transfers, outs, own = _gather_plan(late_shards, late_axes)
    fetch = _Riding(transfers, late_shards, outs, own)

    def late_weights(fetched):
        return {n: t if n == "w_in_b" else t.reshape(-1, t.shape[2]) for n, t in zip(LATE, fetched)}

    def as_blocks(t):
        if t.ndim == 3:
            return t
        return t.reshape(-1) if t.shape[0] % (SUBLANES * NCHIP) else t.reshape(NCHIP, -1, t.shape[1])

    def begin_reduce(grads, halves=None):
        names = list(grads)
        axes = [SPLIT[n] for n in names]
        blocks = [as_blocks(grads[n]) for n in names]
        if halves is None:
            transfers, outs = _halves_plan(blocks, axes)
            return _Riding(transfers, blocks, outs)
        sums = [_chip_sum(blk, part, ax, "chip_sum_" + n) for n, ax, blk, part in zip(names, axes, blocks, halves)]
        bf16 = [s[1] for s in sums]
        transfers, outs = _scatter_plan(bf16)
        return _Riding(transfers, bf16, outs), [s[0] for s in sums]

    sq, d_x, g, (chip_f32, arrived) = _local_step(x[0], loss_target[0], positions, wt, fetch, late_weights,
                                                  begin_reduce)

    axes = [SPLIT[n] for n in BIG]
    halves = []
    for n, ax, t32, parts in zip(BIG, axes, chip_f32, arrived):
        if t32.ndim == 1:
            own = lax.dynamic_slice_in_dim(t32, my_chip * (t32.shape[0] // NCHIP), t32.shape[0] // NCHIP)
        else:
            own = lax.dynamic_index_in_dim(t32, my_chip, axis=0, keepdims=False)
        halves.append(_mesh_sum(own, parts, ax, "mesh_sum_" + n))

    small_shapes = [(D,), (NH,), (HD,), (HD,), (D,), (HD,), (D,), (HD,), (NH,), (D,)]
    gathered_small, sibling_done = _last_exchange(_pack([g[n] for n in SMALL] + [sq]), halves)
    total = _sum_stack(gathered_small, "sum_small")
    small_g = dict(zip(SMALL, _unpack(total, small_shapes)[:-1]))
    loss = 0.5 * jnp.sum(_unpack(total, small_shapes)[-1]) / D
    small_g["norm_a_g"] = lax.dynamic_slice(small_g["norm_a_g"], (my_chip * (D // NCHIP),), (D // NCHIP,))

    res = {}
    for n, ax, mine_half, their_half in zip(BIG, axes, halves, sibling_done):
        out4 = _adamw_halves(w2d[n], mine_half, their_half, shard2d(m[n], n), shard2d(v[n], n), ax, "adamw_" + n)
        res[n] = tuple(unflat(t, n) for t in out4)
    row = lambda t: t.reshape(1, -1)
    small_out = _adamw_small(*[[row(d[n]) for n in SMALL] for d in (w, small_g, m, v)])
    for i, n in enumerate(SMALL):
        res[n] = tuple(t.reshape(w[n].shape) for t in (small_g[n],) + tuple(out[i] for out in small_out))

    outs = [loss, d_x[None]]
    for k in range(4):
        outs += [res[n][k] for n in NAMES]
    return tuple(outs)
```

```python
import numpy as np
import jax
import jax.numpy as jnp
from jax import lax
from jax.experimental import pallas as pl
from jax.experimental.pallas import tpu as pltpu

F32, BF16 = jnp.float32, jnp.bfloat16
S, D, HD, NH, NKV = 2048, 1024, 64, 16, 4
KVW = NKV * HD
WINDOW = 128
ROT = HD // 4
THETA = 500000.0
EPS = 1e-6
SCALE = HD ** -0.5
LANES = 128
SUBLANES = 8
NEG = -1e30
VMEM_LIMIT = 48 * 2 ** 20
ROWS = 512
ATT = 512
SWQ = 16
NCHIP = 4
ADAM_LR, ADAM_B1, ADAM_B2, ADAM_EPS, ADAM_WD, ADAM_STEP = 0.001, 0.9, 0.999, 1e-08, 0.01, 10
NT = (((1,), (1,)), ((), ()))
TN = (((0,), (0,)), ((), ()))
MESH = pl.DeviceIdType.MESH


def _params(n):
    return pltpu.CompilerParams(dimension_semantics=("arbitrary",) * n, vmem_limit_bytes=VMEM_LIMIT)


def _dot(a, b, dims=None):
    if dims is None:
        return jnp.dot(a, b, preferred_element_type=F32)
    return lax.dot_general(a, b, dims, preferred_element_type=F32)


def _dot_split(a, b, n):
    out, rest = None, a
    for _ in range(n):
        hi = rest.astype(BF16)
        term = _dot(hi, b)
        out = term if out is None else out + term
        rest = rest - hi.astype(F32)
    return out


def _seg_mat(w):
    e = (np.arange(w)[:, None] // HD == np.arange(LANES)[None, :]).astype(np.float32)
    return jnp.asarray(e, BF16)


def _spread(r, w):
    head = lax.broadcasted_iota(jnp.int32, (2 * LANES, w), 1) >> (HD.bit_length() - 1)
    row = lax.broadcasted_iota(jnp.int32, (2 * LANES, w), 0)
    et2 = jnp.where(head == (row & (LANES - 1)), 1.0, 0.0).astype(BF16)
    hi = r.astype(BF16)
    lo = (r - hi.astype(F32)).astype(BF16)
    return _dot(jnp.concatenate([hi, lo], axis=1), et2)


def _head_rstd(x, e):
    ss = _dot_split(x * x, e, 2)
    return _spread(lax.rsqrt(ss * (1.0 / HD) + EPS), x.shape[1])


def _rope(x, c, a, b):
    w = x.shape[1]
    return x * c + pltpu.roll(x, w - ROT // 2, 1) * a + pltpu.roll(x, ROT // 2, 1) * b


def _rope_t(dy, c, a, b):
    w = dy.shape[1]
    return dy * c + pltpu.roll(dy * b, w - ROT // 2, 1) + pltpu.roll(dy * a, ROT // 2, 1)


def _sigmoid(x):
    return 1.0 / (1.0 + jnp.exp(-x))


def _row_spec(shape, ts):
    nd = len(shape)
    if shape[0] == S:
        return pl.BlockSpec((ts,) + tuple(shape[1:]), lambda i: (i,) + (0,) * (nd - 1))
    return pl.BlockSpec(tuple(shape), lambda i: (0,) * nd)


def _rows_call(body, name, ins, outs, ts=ROWS):
    return pl.pallas_call(
        body, name=name, grid=(S // ts,),
        in_specs=[_row_spec(a.shape, ts) for a in ins],
        out_specs=[_row_spec(s, ts) for s, _ in outs],
        out_shape=[jax.ShapeDtypeStruct(s, d) for s, d in outs],
        compiler_params=_params(1))(*ins)


def _col_spec(ts, w, col):
    return pl.BlockSpec((ts, w), lambda i: (i, col))


TM = TN_ = 512
TM_TOKENS = 1024
TN_WIDE = 1024


def _mm(name, m, n, terms, out_dtype=F32, add=None, tm=None, tn=TN_, stacked=False, riding=None, rows_of=None):
    nterm = len(terms)
    if tm is None:
        tm = TM_TOKENS if m == S else TM
    nj, ni_ = n // tn, m // tm
    n_in = 2 * nterm + (add is not None) + (rows_of is not None and rows_of[0] is not None)
    r_in, r_out = (len(riding.ins), len(riding.outs)) if riding is not None else (0, 0)

    def body(*refs):
        if riding is not None:
            j, i = pl.program_id(0), pl.program_id(1)
            at_end = riding.hooks(refs[n_in:n_in + r_in], refs[n_in + r_in + 1:n_in + r_in + 1 + r_out],
                                  *refs[n_in + r_in + 1 + r_out:], first=(j == 0) & (i == 0),
                                  middle=(j == nj // 2) & (i == 0), last=(j == nj - 1) & (i == ni_ - 1))
        acc = None
        for t in range(nterm):
            part = _dot(refs[2 * t][...], refs[2 * t + 1][...], terms[t][4])
            acc = part if acc is None else acc + part
        if add is not None:
            acc = acc + refs[2 * nterm][...]
        refs[n_in + r_in][...] = acc.astype(out_dtype)
        if riding is not None:
            at_end()

    tile = pl.BlockSpec((tm, tn), lambda j, i: (i, j))
    ins, specs = [], []
    for a, a_spec, b, b_spec, _ in terms:
        ins += [a, b]
        specs += [a_spec, b_spec]
    if add is not None:
        ins.append(add)
        specs.append(tile)
    out_spec = pl.BlockSpec((None, tm, tn), lambda j, i: (j, i, 0)) if stacked else tile
    out_shape = jax.ShapeDtypeStruct((nj, m, tn) if stacked else (m, n), out_dtype)
    if rows_of is not None:
        taller, rows, row0 = rows_of
        out_spec = pl.BlockSpec((pl.Element(tm), pl.Element(tn)), lambda j, i: (
            pl.multiple_of(row0 + i * tm, SUBLANES), pl.multiple_of(j * tn, LANES)))
        out_shape = jax.ShapeDtypeStruct((rows, n), out_dtype)
        alias = {}
        if taller is not None:
            ins.append(taller)
            specs.append(pl.BlockSpec(memory_space=pltpu.HBM))
            alias = {len(ins) - 1: 0}
        return pl.pallas_call(body, name=name, grid=(nj, ni_), in_specs=specs, out_specs=out_spec,
                              out_shape=out_shape, input_output_aliases=alias, compiler_params=_params(2))(*ins)
    if riding is None:
        return pl.pallas_call(body, name=name, grid=(nj, ni_), in_specs=specs, out_specs=out_spec,
                              out_shape=out_shape, compiler_params=_params(2))(*ins)
    res = pl.pallas_call(
        body, name=name, grid=(nj, ni_), in_specs=specs + riding.in_specs,
        out_specs=[out_spec] + riding.out_specs, out_shape=[out_shape] + riding.out_shape,
        scratch_shapes=riding.scratch, compiler_params=_params(2))(*ins, *riding.ins)
    return res[0], res[1:]


def _a_rows(k, col=0, tm=TM_TOKENS):
    return pl.BlockSpec((tm, k), lambda j, i: (i, col))


def _a_cols(k, tm=TM):
    return pl.BlockSpec((k, tm), lambda j, i: (0, i))


def _b_cols(k, row=0, col0=0, tn=TN_):
    return pl.BlockSpec((k, tn), lambda j, i: (row, col0 + j))


def _b_rows(k, row0=0, tn=TN_):
    return pl.BlockSpec((tn, k), lambda j, i: (row0 + j, 0))


def _rmsnorm_fwd(x, gains, name):
    def body(*refs):
        xv = refs[0][...]
        r = lax.rsqrt(jnp.mean(xv * xv, axis=-1, keepdims=True) + EPS)
        xh = xv * r
        for n in range(len(gains)):
            refs[1 + len(gains) + n][...] = (xh * refs[1 + n][...]).astype(BF16)

    return _rows_call(body, name, [x] + list(gains), [((S, D), BF16)] * len(gains))


def _rmsnorm_bwd(x, gains, dus, dres, name):
    n = len(gains)

    def body(*refs):
        x_ref, g_refs, du_refs, dres_ref = refs[0], refs[1:1 + n], refs[1 + n:1 + 2 * n], refs[1 + 2 * n]
        dx_ref, dxb_ref, dg_refs = refs[2 + 2 * n], refs[3 + 2 * n], refs[4 + 2 * n:]
        xv = x_ref[...]
        r = lax.rsqrt(jnp.mean(xv * xv, axis=-1, keepdims=True) + EPS)
        xh = xv * r
        gy = None
        for m in range(n):
            du = du_refs[m][...]
            part = jnp.sum(du * xh, axis=0, keepdims=True)

            @pl.when(pl.program_id(0) == 0)
            def _(m=m, part=part):
                dg_refs[m][...] = part

            @pl.when(pl.program_id(0) != 0)
            def _(m=m, part=part):
                dg_refs[m][...] += part

            t = du * g_refs[m][...]
            gy = t if gy is None else gy + t
        dx = dres_ref[...] + r * (gy - xh * jnp.mean(gy * xh, axis=-1, keepdims=True))
        dx_ref[...] = dx
        dxb_ref[...] = dx.astype(BF16)

    outs = [((S, D), F32), ((S, D), BF16)] + [((1, D), F32)] * n
    return _rows_call(body, name, [x] + list(gains) + list(dus) + [dres], outs)


def _a_post(qkvg, qg, kg):
    e = _seg_mat(D)

    def body(q_ref, k_ref, v_ref, qg_ref, kg_ref, e_ref, qo, ko, vo):
        ev = e_ref[...]
        qv, kv = q_ref[...], k_ref[...]
        qo[...] = (qv * _head_rstd(qv, ev) * qg_ref[...] * SCALE).astype(BF16)
        ko[...] = (kv * _head_rstd(kv, ev) * kg_ref[...]).astype(BF16)
        vo[...] = v_ref[...].astype(BF16)

    whole = lambda a: pl.BlockSpec(a.shape, lambda i: (0, 0))
    return pl.pallas_call(
        body, name="a_post", grid=(S // ROWS,),
        in_specs=[_col_spec(ROWS, D, 0), _col_spec(ROWS, D, 1), _col_spec(ROWS, D, 2),
                  whole(qg), whole(kg), whole(e)],
        out_specs=[_col_spec(ROWS, D, 0)] * 3,
        out_shape=[jax.ShapeDtypeStruct((S, D), BF16)] * 3,
        compiler_params=_params(1))(qkvg, qkvg, qkvg, qg, kg, e)


def _tri(upper):
    r, c = np.arange(ROWS)[:, None], np.arange(ROWS)[None, :]
    return jnp.asarray((r <= c) if upper else (r >= c), BF16)


def _forget_cumsum(fpad, bpad):
    def body(f_ref, b_ref, u_ref, c_ref, carry):
        @pl.when(pl.program_id(0) == 0)
        def _():
            carry[...] = jnp.zeros_like(carry)

        lf = jax.nn.log_sigmoid(f_ref[...] + b_ref[...])
        blk = _dot_split(lf.T, u_ref[...], 3) + carry[:, 0:1]
        c_ref[...] = blk
        carry[...] = jnp.broadcast_to(blk[:, ROWS - 1:ROWS], carry.shape)

    return pl.pallas_call(
        body, name="forget_cumsum", grid=(S // ROWS,),
        in_specs=[pl.BlockSpec((ROWS, LANES), lambda i: (i, 0)), pl.BlockSpec((1, LANES), lambda i: (0, 0)),
                  pl.BlockSpec((ROWS, ROWS), lambda i: (0, 0))],
        out_specs=pl.BlockSpec((LANES, ROWS), lambda i: (0, i)),
        out_shape=jax.ShapeDtypeStruct((LANES, S), F32),
        scratch_shapes=[pltpu.VMEM((LANES, LANES), F32)],
        compiler_params=_params(1))(fpad, bpad, _tri(True))


def _forget_bwd(dct, fpad, bpad):
    nb = S // ROWS

    def body(dc_ref, f_ref, b_ref, l_ref, df_ref, db_ref, carry):
        @pl.when(pl.program_id(0) == 0)
        def _():
            carry[...] = jnp.zeros_like(carry)
            db_ref[...] = jnp.zeros_like(db_ref)

        blk = _dot_split(dc_ref[...], l_ref[...], 3) + carry[:, 0:1]
        carry[...] = jnp.broadcast_to(blk[:, 0:1], carry.shape)
        df = blk.T * _sigmoid(-(f_ref[...] + b_ref[...]))
        df_ref[...] = df.astype(BF16)
        db_ref[...] += jnp.sum(df, axis=0, keepdims=True)

    return pl.pallas_call(
        body, name="forget_bwd", grid=(nb,),
        in_specs=[pl.BlockSpec((LANES, ROWS), lambda i: (0, nb - 1 - i)),
                  pl.BlockSpec((ROWS, LANES), lambda i: (nb - 1 - i, 0)),
                  pl.BlockSpec((1, LANES), lambda i: (0, 0)), pl.BlockSpec((ROWS, ROWS), lambda i: (0, 0))],
        out_specs=[pl.BlockSpec((ROWS, LANES), lambda i: (nb - 1 - i, 0)), pl.BlockSpec((1, LANES), lambda i: (0, 0))],
        out_shape=[jax.ShapeDtypeStruct((S, LANES), BF16), jax.ShapeDtypeStruct((1, LANES), F32)],
        scratch_shapes=[pltpu.VMEM((LANES, LANES), F32)],
        compiler_params=_params(1))(dct, fpad, bpad, _tri(False))


def _headnorm_bwd(x, col, gain, dy, rope, name):
    e = _seg_mat(D)
    tabs = list(rope) if rope is not None else []

    def body(*refs):
        x_ref, g_ref, dy_ref, e_ref = refs[:4]
        dx_ref, dg_ref = refs[-2:]
        xv, dyv, ev = x_ref[...], dy_ref[...], e_ref[...]
        if rope is not None:
            c, a, b = (jnp.tile(t[...], (1, D // LANES)) for t in refs[4:7])
            dyv = _rope_t(dyv, c, a, b)
        r = _head_rstd(xv, ev)
        xh = xv * r
        part = jnp.sum(dyv * xh, axis=0, keepdims=True)

        @pl.when(pl.program_id(0) == 0)
        def _():
            dg_ref[...] = part

        @pl.when(pl.program_id(0) != 0)
        def _():
            dg_ref[...] += part

        gy = dyv * g_ref[...]
        seg = _spread(_dot_split(gy * xh, ev, 2) * (1.0 / HD), D)
        dx_ref[...] = (r * (gy - xh * seg)).astype(BF16)

    whole = lambda a: pl.BlockSpec(a.shape, lambda i: (0, 0))
    return pl.pallas_call(
        body, name=name, grid=(S // ROWS,),
        in_specs=[_col_spec(ROWS, D, col), whole(gain), _col_spec(ROWS, D, 0), whole(e)]
                 + [pl.BlockSpec((ROWS, LANES), lambda i: (i, 0))] * len(tabs),
        out_specs=[_col_spec(ROWS, D, 0), whole(gain)],
        out_shape=[jax.ShapeDtypeStruct((S, D), BF16), jax.ShapeDtypeStruct((1, D), F32)],
        compiler_params=_params(1))(x, gain, dy, e, *tabs)


def _dup_mat():
    r, c = np.arange(KVW)[:, None], np.arange(2 * KVW)[None, :]
    return (r // HD == c // LANES) & (r % HD == c % HD)


def _fold_mat():
    r, c = np.arange(D)[:, None], np.arange(KVW)[None, :]
    return (r // (2 * LANES) == c // HD) & (r % HD == c % HD)


def _b_post(pb, kv, qg, kg, rope):
    e, ek = _seg_mat(D), _seg_mat(KVW)
    dup = jnp.asarray(_dup_mat(), BF16)

    def body(q_ref, k_ref, v_ref, qg_ref, kg_ref, e_ref, ek_ref, dup_ref, c_ref, a_ref, b_ref, qo, ko, vo):
        c1, a1, b1 = c_ref[...], a_ref[...], b_ref[...]
        qv = q_ref[...]
        qn = qv * _head_rstd(qv, e_ref[...]) * qg_ref[...]
        t = lambda z, n: jnp.tile(z, (1, n))
        qo[...] = (_rope(qn, t(c1, D // LANES), t(a1, D // LANES), t(b1, D // LANES)) * SCALE).astype(BF16)
        kvv = k_ref[...]
        kn = kvv * _head_rstd(kvv, ek_ref[...]) * kg_ref[...]
        kr = _rope(kn, t(c1, KVW // LANES), t(a1, KVW // LANES), t(b1, KVW // LANES)).astype(BF16)
        ko[...] = _dot(kr, dup_ref[...]).astype(BF16)
        vo[...] = _dot(v_ref[...].astype(BF16), dup_ref[...]).astype(BF16)

    whole = lambda a: pl.BlockSpec(a.shape, lambda i: (0, 0))
    tab = pl.BlockSpec((ROWS, LANES), lambda i: (i, 0))
    return pl.pallas_call(
        body, name="b_post", grid=(S // ROWS,),
        in_specs=[_col_spec(ROWS, D, 0), _col_spec(ROWS, KVW, 0), _col_spec(ROWS, KVW, 1),
                  whole(qg), whole(kg), whole(e), whole(ek), whole(dup), tab, tab, tab],
        out_specs=[_col_spec(ROWS, D, 0), _col_spec(ROWS, 2 * KVW, 0), _col_spec(ROWS, 2 * KVW, 0)],
        out_shape=[jax.ShapeDtypeStruct((S, D), BF16), jax.ShapeDtypeStruct((S, 2 * KVW), BF16),
                   jax.ShapeDtypeStruct((S, 2 * KVW), BF16)],
        compiler_params=_params(1))(pb, kv, kv, qg, kg, e, ek, dup, *rope)


def _kv_bwd(dkdup, dvdup, kv, kg, rope):
    ek = _seg_mat(KVW)
    fold = jnp.asarray(_fold_mat(), BF16)

    def body(dk_ref, dv_ref, k_ref, kg_ref, ek_ref, fold_ref, c_ref, a_ref, b_ref, dkv_ref, dg_ref):
        ev, fv = ek_ref[...], fold_ref[...]
        t = lambda z: jnp.tile(z[...], (1, KVW // LANES))
        dk = _rope_t(_dot_split(dk_ref[...], fv, 2), t(c_ref), t(a_ref), t(b_ref))
        dv = _dot_split(dv_ref[...], fv, 2)
        xv = k_ref[...]
        r = _head_rstd(xv, ev)
        xh = xv * r
        part = jnp.sum(dk * xh, axis=0, keepdims=True)

        @pl.when(pl.program_id(0) == 0)
        def _():
            dg_ref[...] = part

        @pl.when(pl.program_id(0) != 0)
        def _():
            dg_ref[...] += part

        gy = dk * kg_ref[...]
        seg = _spread(_dot_split(gy * xh, ev, 2) * (1.0 / HD), KVW)
        dkv_ref[:, 0:KVW] = (r * (gy - xh * seg)).astype(BF16)
        dkv_ref[:, KVW:2 * KVW] = dv.astype(BF16)

    whole = lambda a: pl.BlockSpec(a.shape, lambda i: (0, 0))
    tab = pl.BlockSpec((ROWS, LANES), lambda i: (i, 0))
    return pl.pallas_call(
        body, name="kv_bwd", grid=(S // ROWS,),
        in_specs=[_col_spec(ROWS, D, 0), _col_spec(ROWS, D, 0), _col_spec(ROWS, KVW, 0),
                  whole(kg), whole(ek), whole(fold), tab, tab, tab],
        out_specs=[_col_spec(ROWS, 2 * KVW, 0), whole(kg)],
        out_shape=[jax.ShapeDtypeStruct((S, 2 * KVW), BF16), jax.ShapeDtypeStruct((1, KVW), F32)],
        compiler_params=_params(1))(dkdup, dvdup, kv, kg, ek, fold, *rope)


def _out_loss(y, w_out, residual, target):
    def body(y_ref, w_ref, r_ref, t_ref, d_ref, db_ref, l_ref):
        diff = _dot(y_ref[...], w_ref[...]) + r_ref[...] - t_ref[...]
        d = diff * (1.0 / D)
        d_ref[...] = d
        db_ref[...] = d.astype(BF16)

        @pl.when(pl.program_id(0) == 0)
        def _():
            l_ref[...] = jnp.zeros_like(l_ref)

        l_ref[...] += jnp.sum(diff * diff, axis=0, keepdims=True)

    rows = pl.BlockSpec((TM_TOKENS, D), lambda i: (i, 0))
    whole = pl.BlockSpec((D, D), lambda i: (0, 0))
    return pl.pallas_call(
        body, name="out_b_loss", grid=(S // TM_TOKENS,), in_specs=[rows, whole, rows, rows],
        out_specs=[rows, rows, pl.BlockSpec((1, D), lambda i: (0, 0))],
        out_shape=[jax.ShapeDtypeStruct((S, D), F32), jax.ShapeDtypeStruct((S, D), BF16),
                   jax.ShapeDtypeStruct((1, D), F32)],
        compiler_params=_params(1))(y, w_out, residual, target)


def _lane():
    return lax.broadcasted_iota(jnp.int32, (1, LANES), 1)


def _head_mask(hh):
    return (_lane() < HD) if hh == 0 else (_lane() >= HD)


def _fox_fwd(q, k, v, ct, gate, riding):
    nq, npair = S // ATT, NH // 2
    ni, no = len(riding.ins), len(riding.outs)

    def body(q_ref, k_ref, v_ref, c_ref, gate_ref, *rest):
        o_ref, lse_ref, y_ref = rest[ni:ni + 3]
        pair, i = pl.program_id(0), pl.program_id(1)
        at_end = riding.hooks(rest[:ni], rest[ni + 3:ni + 3 + no], *rest[ni + 3 + no:],
                              first=(pair == 0) & (i == 0), middle=(pair == npair // 2) & (i == 0),
                              last=(pair == npair - 1) & (i == nq - 1))
        q2 = q_ref[...]
        qms = [jnp.where(_head_mask(hh), q2, jnp.zeros_like(q2)) for hh in (0, 1)]

        def probs(off, width, m, hh, diag):
            s = _dot(qms[hh], k_ref[pl.ds(off, width), :], NT) - c_ref[hh:hh + 1, pl.ds(off, width)]
            if diag:
                row = i * ATT + lax.broadcasted_iota(jnp.int32, (ATT, width), 0)
                col = off + lax.broadcasted_iota(jnp.int32, (ATT, width), 1)
                s = jnp.where(col <= row, s, NEG)
            m_new = jnp.maximum(m, jnp.max(s, axis=1, keepdims=True))
            p = jnp.exp(s - m_new)
            p_hi = p.astype(BF16)
            return m_new, jnp.exp(m - m_new), p_hi, (p - p_hi.astype(F32)).astype(BF16)

        def weighted(off, width, p_hi, p_lo, hh):
            vj = v_ref[pl.ds(off, width), :]
            v1 = jnp.where(_head_mask(hh), vj, jnp.ones_like(vj))
            return _dot(p_hi, v1) + _dot(p_lo, v1)

        def step(off, width, carry, diag):
            off = pl.multiple_of(off, ATT)
            out = []
            for hh in (0, 1):
                m, acc = carry[hh]
                m, alpha, p_hi, p_lo = probs(off, width, m, hh, diag)
                out.append((m, alpha * acc + weighted(off, width, p_hi, p_lo, hh)))
            return tuple(out)

        one = (jnp.full((ATT, 1), NEG, F32), jnp.zeros((ATT, LANES), F32))
        carry = lax.fori_loop(0, i // 2, lambda j, cr: step(j * (2 * ATT), 2 * ATT, cr, False), (one, one))
        carry = lax.cond(i % 2 == 1, lambda cr: step((i - 1) * ATT, 2 * ATT, cr, True),
                         lambda cr: step(i * ATT, ATT, cr, True), carry)
        res = []
        for hh in (0, 1):
            m, acc = carry[hh]
            l = jnp.max(jnp.where(_head_mask(1 - hh), acc, 0.0), axis=1, keepdims=True)
            res.append((acc / l, m + jnp.log(l)))
        first = _head_mask(0)
        o = jnp.where(first, res[0][0], res[1][0])
        o_ref[...] = o
        lse_ref[...] = jnp.where(first, res[0][1], res[1][1])
        g = gate_ref[...]
        y_ref[...] = (o * (g * _sigmoid(g))).astype(BF16)
        at_end()

    blk = pl.BlockSpec((ATT, LANES), lambda p, i: (i, p))
    full = pl.BlockSpec((S, LANES), lambda p, i: (0, p))
    res = pl.pallas_call(
        body, name="fox_fwd", grid=(npair, nq),
        in_specs=[blk, full, full, pl.BlockSpec((None, 2, S), lambda p, i: (p, 0, 0)), blk] + riding.in_specs,
        out_specs=[blk, blk, blk] + riding.out_specs,
        out_shape=[jax.ShapeDtypeStruct((S, D), F32)] * 2 + [jax.ShapeDtypeStruct((S, D), BF16)] + riding.out_shape,
        scratch_shapes=riding.scratch,
        compiler_params=_params(2))(q, k, v, ct, gate, *riding.ins)
    return res[0], res[1], res[2], res[3:]


def _gate_grads(dy, o, g):
    sg = _sigmoid(g)
    return dy * (g * sg), dy * o * (sg * (1.0 + g * (1.0 - sg)))


def _fox_bwd(q, k, v, ct, o, lse, dy, gate, riding):
    nq, npair = S // ATT, NH // 2
    ni, no = len(riding.ins), len(riding.outs)

    def body(q_ref, k_ref, v_ref, c_ref, o_ref, lse_ref, dy_ref, gate_ref, *rest):
        dq_ref, dk_ref, dvb_ref, dc_ref, dgate_ref = rest[ni:ni + 5]
        dv_ref = rest[ni + 5 + no]
        pair, i = pl.program_id(0), pl.program_id(1)
        at_end = riding.hooks(rest[:ni], rest[ni + 5:ni + 5 + no], *rest[ni + 6 + no:],
                              first=(pair == 0) & (i == 0), middle=(pair == npair // 2) & (i == 0),
                              last=(pair == npair - 1) & (i == nq - 1))

        @pl.when(i == 0)
        def _():
            dk_ref[...] = jnp.zeros_like(dk_ref)
            dv_ref[...] = jnp.zeros_like(dv_ref)
            dc_ref[...] = jnp.zeros_like(dc_ref)

        q2, lse2 = q_ref[...], lse_ref[...]
        do2, dgate = _gate_grads(dy_ref[...], o_ref[...], gate_ref[...])
        dgate_ref[...] = dgate.astype(BF16)
        do2b = do2.astype(BF16)
        prod = do2b.astype(F32) * o_ref[...]
        heads = []
        for hh in (0, 1):
            hm = _head_mask(hh)
            heads.append((jnp.where(hm, q2, jnp.zeros_like(q2)), jnp.where(hm, do2b, jnp.zeros_like(do2b)),
                          jnp.sum(jnp.where(hm, prod, 0.0), axis=1, keepdims=True),
                          jnp.max(jnp.where(hm, lse2, NEG), axis=1, keepdims=True)))

        def step(off, width, dqs, diag):
            off = pl.multiple_of(off, ATT)
            kj, vj = k_ref[pl.ds(off, width), :], v_ref[pl.ds(off, width), :]
            dk, dv, out = None, None, []
            for hh in (0, 1):
                qm, dom, delta, lse_h = heads[hh]
                s = _dot(qm, kj, NT) - c_ref[hh:hh + 1, pl.ds(off, width)]
                p = jnp.exp(s - lse_h)
                if diag:
                    row = i * ATT + lax.broadcasted_iota(jnp.int32, (ATT, width), 0)
                    col = off + lax.broadcasted_iota(jnp.int32, (ATT, width), 1)
                    p = jnp.where(col <= row, p, 0.0)
                ds = p * (_dot(dom, vj, NT) - delta)
                dc_ref[hh:hh + 1, pl.ds(off, width)] += -jnp.sum(ds, axis=0, keepdims=True)
                dsb = ds.astype(BF16)
                dk_h, dv_h = _dot(dsb, qm, TN), _dot(p.astype(BF16), dom, TN)
                dk, dv = (dk_h, dv_h) if dk is None else (dk + dk_h, dv + dv_h)
                out.append(dqs[hh] + _dot(dsb, kj))
            dk_ref[pl.ds(off, width), :] += dk
            dv_ref[pl.ds(off, width), :] += dv
            return tuple(out)

        zero = jnp.zeros((ATT, LANES), F32)
        dqs = lax.fori_loop(0, i // 2, lambda j, acc: step(j * (2 * ATT), 2 * ATT, acc, False), (zero, zero))
        dqs = lax.cond(i % 2 == 1, lambda acc: step((i - 1) * ATT, 2 * ATT, acc, True),
                       lambda acc: step(i * ATT, ATT, acc, True), dqs)
        dq_ref[...] = jnp.where(_head_mask(0), dqs[0], dqs[1]) * SCALE

        @pl.when(i == nq - 1)
        def _():
            dvb_ref[...] = dv_ref[...].astype(BF16)

        at_end()

    blk = pl.BlockSpec((ATT, LANES), lambda p, i: (i, p))
    full = pl.BlockSpec((S, LANES), lambda p, i: (0, p))
    cspec = pl.BlockSpec((None, 2, S), lambda p, i: (p, 0, 0))
    res = pl.pallas_call(
        body, name="fox_bwd", grid=(npair, nq),
        in_specs=[blk, full, full, cspec, blk, blk, blk, blk] + riding.in_specs,
        out_specs=[blk, full, full, cspec, blk] + riding.out_specs,
        out_shape=[jax.ShapeDtypeStruct((S, D), F32)] * 2 + [jax.ShapeDtypeStruct((S, D), BF16),
                                                              jax.ShapeDtypeStruct((npair, 2, S), F32),
                                                              jax.ShapeDtypeStruct((S, D), BF16)]
                  + riding.out_shape,
        scratch_shapes=[pltpu.VMEM((S, LANES), F32)] + riding.scratch,
        compiler_params=_params(2))(q, k, v, ct, o, lse, dy, gate, *riding.ins)
    return res[0], res[1], res[2], res[3], res[4], res[5:]


def _both_heads(x):
    return jnp.concatenate([jnp.where(_head_mask(hh), x, jnp.zeros_like(x)) for hh in (0, 1)], axis=0)


def _per_head(col0, col1):
    return jnp.concatenate([jnp.broadcast_to(col0, (WINDOW, 1)), jnp.broadcast_to(col1, (WINDOW, 1))], axis=0)


def _unstack(x2):
    return jnp.where(_head_mask(0), x2[:WINDOW], x2[WINDOW:])


def _swa_valid(i, start):
    r = lax.broadcasted_iota(jnp.int32, (2 * WINDOW, 2 * WINDOW), 0)
    qabs = i * WINDOW + jnp.where(r >= WINDOW, r - WINDOW, r)
    kabs = start + lax.broadcasted_iota(jnp.int32, (2 * WINDOW, 2 * WINDOW), 1)
    return (kabs <= qabs) & (qabs - kabs < WINDOW)


def _swa_fwd(q, kdup, vdup, sinks_t, proj, gate_col):
    def body(q_ref, k_ref, v_ref, sk_ref, gate_ref, o_ref, lse_ref, y_ref):
        skv = sk_ref[...]
        first = _head_mask(0)
        for sb in range(SWQ):
            i = pl.program_id(1) * SWQ + sb
            rows = slice(sb * WINDOW, (sb + 1) * WINDOW)
            start = pl.multiple_of(jnp.maximum(i - 1, 0) * WINDOW, WINDOW)
            kk, vv = k_ref[pl.ds(start, 2 * WINDOW), :], v_ref[pl.ds(start, 2 * WINDOW), :]
            q2 = q_ref[rows, :]
            valid = _swa_valid(i, start)[:WINDOW]
            res = []
            for hh in (0, 1):
                hm = _head_mask(hh)
                sink = jnp.max(jnp.where(hm, skv, NEG), axis=1, keepdims=True)
                s = jnp.where(valid, _dot(jnp.where(hm, q2, jnp.zeros_like(q2)), kk, NT), NEG)
                m = jnp.maximum(jnp.max(s, axis=1, keepdims=True), sink)
                p = jnp.exp(s - m)
                l = jnp.sum(p, axis=1, keepdims=True) + jnp.exp(sink - m)
                res.append((_dot(p.astype(BF16), vv) / l, m + jnp.log(l)))
            o = jnp.where(first, res[0][0], res[1][0])
            o_ref[rows, :] = o
            lse_ref[rows, :] = jnp.where(first, res[0][1], res[1][1])
            g = gate_ref[rows, :]
            y_ref[rows, :] = (o * (g * _sigmoid(g))).astype(BF16)

    blk = pl.BlockSpec((SWQ * WINDOW, LANES), lambda p, i: (i, p))
    gate = pl.BlockSpec((SWQ * WINDOW, LANES), lambda p, i: (i, gate_col + p))
    full = pl.BlockSpec((S, LANES), lambda p, i: (0, p // 2))
    return pl.pallas_call(
        body, name="swa_fwd", grid=(NH // 2, S // (SWQ * WINDOW)),
        in_specs=[blk, full, full, pl.BlockSpec((1, LANES), lambda p, i: (0, p)), gate],
        out_specs=[blk, blk, blk],
        out_shape=[jax.ShapeDtypeStruct((S, D), F32)] * 2 + [jax.ShapeDtypeStruct((S, D), BF16)],
        compiler_params=_params(2))(q, kdup, vdup, sinks_t, proj)


def _swa_bwd(q, kdup, vdup, sinks_t, o, lse, dy, proj, gate_col):
    def body(q_ref, k_ref, v_ref, sk_ref, o_ref, lse_ref, dy_ref, gate_ref, dq_ref, dk_ref, dv_ref, dsk_ref,
             dgate_ref):
        @pl.when(pl.program_id(1) == 0)
        def _():
            dk_ref[...] = jnp.zeros_like(dk_ref)
            dv_ref[...] = jnp.zeros_like(dv_ref)
            dsk_ref[...] = jnp.zeros_like(dsk_ref)

        skv = sk_ref[...]
        first = _head_mask(0)
        sink = _per_head(*[jnp.max(jnp.where(_head_mask(hh), skv, NEG), axis=1, keepdims=True) for hh in (0, 1)])
        for sb in range(SWQ):
            i = pl.program_id(1) * SWQ + sb
            rows = slice(sb * WINDOW, (sb + 1) * WINDOW)
            start = pl.multiple_of(jnp.maximum(i - 1, 0) * WINDOW, WINDOW)
            kk, vv = k_ref[pl.ds(start, 2 * WINDOW), :], v_ref[pl.ds(start, 2 * WINDOW), :]
            do2, dgate = _gate_grads(dy_ref[rows, :], o_ref[rows, :], gate_ref[rows, :])
            dgate_ref[rows, :] = dgate.astype(BF16)
            do2b = do2.astype(BF16)
            prod, lse2 = do2b.astype(F32) * o_ref[rows, :], lse_ref[rows, :]
            qs, dos = _both_heads(q_ref[rows, :]), _both_heads(do2b)
            delta = jnp.concatenate([jnp.sum(jnp.where(_head_mask(hh), prod, 0.0), axis=1, keepdims=True)
                                     for hh in (0, 1)], axis=0)
            lse_h = jnp.concatenate([jnp.max(jnp.where(_head_mask(hh), lse2, NEG), axis=1, keepdims=True)
                                     for hh in (0, 1)], axis=0)
            p = jnp.where(_swa_valid(i, start), jnp.exp(_dot(qs, kk, NT) - lse_h), 0.0)
            dsb = (p * (_dot(dos, vv, NT) - delta)).astype(BF16)
            dk_ref[pl.ds(start, 2 * WINDOW), :] += _dot(dsb, qs, TN)
            dv_ref[pl.ds(start, 2 * WINDOW), :] += _dot(p.astype(BF16), dos, TN)
            dq_ref[rows, :] = _unstack(_dot(dsb, kk)) * SCALE
            t = jnp.exp(sink - lse_h) * delta
            dsk_ref[...] += -jnp.where(first, jnp.sum(t[:WINDOW], axis=0, keepdims=True),
                                       jnp.sum(t[WINDOW:], axis=0, keepdims=True))

    blk = pl.BlockSpec((SWQ * WINDOW, LANES), lambda p, i: (i, p))
    full = pl.BlockSpec((S, LANES), lambda p, i: (0, p // 2))
    acc = pl.BlockSpec((S, LANES), lambda p, i: (0, p))
    sk = pl.BlockSpec((1, LANES), lambda p, i: (0, p))
    gate = pl.BlockSpec((SWQ * WINDOW, LANES), lambda p, i: (i, gate_col + p))
    return pl.pallas_call(
        body, name="swa_bwd", grid=(NH // 2, S // (SWQ * WINDOW)),
        in_specs=[blk, full, full, sk, blk, blk, blk, gate],
        out_specs=[blk, acc, acc, sk, blk],
        out_shape=[jax.ShapeDtypeStruct((S, D), F32)] * 3 + [jax.ShapeDtypeStruct((1, D), F32),
                                                              jax.ShapeDtypeStruct((S, D), BF16)],
        compiler_params=_params(2))(q, kdup, vdup, sinks_t, o, lse, dy, proj)


def _adamw_math(w, g, m, v):
    m = ADAM_B1 * m + (1.0 - ADAM_B1) * g
    v = ADAM_B2 * v + (1.0 - ADAM_B2) * jnp.square(g)
    m_hat = m / (1.0 - ADAM_B1 ** ADAM_STEP)
    v_hat = v / (1.0 - ADAM_B2 ** ADAM_STEP)
    delta = -ADAM_LR * (m_hat / (jnp.sqrt(v_hat) + ADAM_EPS) + ADAM_WD * w)
    return delta, m, v


def _adamw_small(ws, gs, ms, vs):
    k = len(ws)

    def body(*refs):
        for p in range(k):
            w_ref, g_ref, m_ref, v_ref = (refs[q * k + p] for q in range(4))
            d, mo, vo = _adamw_math(w_ref[...], g_ref[...], m_ref[...], v_ref[...])
            refs[4 * k + p][...], refs[5 * k + p][...], refs[6 * k + p][...] = d, mo, vo

    res = pl.pallas_call(
        body, name="adamw_small",
        out_shape=[jax.ShapeDtypeStruct(t.shape, F32) for t in ws] * 3)(*ws, *gs, *ms, *vs)
    return res[:k], res[k:2 * k], res[2 * k:]


SUM_TILES = (512, 256, 128)


FLAT_BLOCK = 257 * 1024


def _tiles(shape, axis, lead=0, halves=False):
    if len(shape) == 1:
        count = shape[0] // FLAT_BLOCK
        return (FLAT_BLOCK,), count, lambda pos, *lead_idx: (sum(k * count for k in lead_idx) + pos,)
    r, c = shape
    tile = next(t for t in SUM_TILES if (shape[axis] // (2 if halves else 1)) % t == 0)
    blk = (tile, c) if axis == 0 else (r, tile)
    count = shape[axis] // tile

    def index(pos, *lead_idx):
        return tuple(lead_idx) + ((pos, 0) if axis == 0 else (0, pos))

    return (None,) * lead + blk, count, index


def _adamw_halves(w, g_mine, g_theirs, m, v, axis, name):
    blk, count, index = _tiles(w.shape, axis, halves=True)
    per_half = count // 2

    def body(w_ref, a_ref, b_ref, m_ref, v_ref, g_ref, d_ref, mo_ref, vo_ref):
        is_mine = pl.program_id(0) // per_half == lax.axis_index("c")
        g = jnp.where(is_mine, a_ref[...], b_ref[...])
        g_ref[...] = g
        d_ref[...], mo_ref[...], vo_ref[...] = _adamw_math(w_ref[...], g, m_ref[...], v_ref[...])

    spec = pl.BlockSpec(blk, lambda i: index(i))
    half = pl.BlockSpec(blk, lambda i: index(i % per_half))
    return pl.pallas_call(
        body, name=name, grid=(count,), in_specs=[spec, half, half, spec, spec], out_specs=[spec] * 4,
        out_shape=[jax.ShapeDtypeStruct(w.shape, F32)] * 4, compiler_params=_params(1))(w, g_mine, g_theirs, m, v)


def _chip_sum(blocks, from_sibling, axis, name):
    flat = blocks.ndim == 1
    blk, count, index = _tiles((from_sibling.shape[0] // NCHIP,) if flat else from_sibling.shape[1:], axis, lead=1)

    def body(lo_ref, hi_ref, p_ref, o32, o16):
        mine = jnp.where(lax.axis_index("c") == 0, lo_ref[...], hi_ref[...])
        acc = mine + p_ref[...]
        o32[...] = acc
        o16[...] = acc.astype(BF16)

    half = pl.BlockSpec(blk, lambda k, i: index(i, k))
    if flat:
        lo = pl.BlockSpec(blk, lambda k, i: (2 * count * k + i,))
        hi = pl.BlockSpec(blk, lambda k, i: (2 * count * k + count + i,))
    else:
        lo, hi = half, pl.BlockSpec(blk, lambda k, i: index(i + count, k))
    return pl.pallas_call(
        body, name=name, grid=(NCHIP, count), in_specs=[lo, hi, half], out_specs=[half, half],
        out_shape=[jax.ShapeDtypeStruct(from_sibling.shape, F32), jax.ShapeDtypeStruct(from_sibling.shape, BF16)],
        compiler_params=_params(2))(blocks, blocks, from_sibling)


def _mesh_sum(chip_sums, parts, axis, name):
    flat = chip_sums.ndim == 1
    one = (chip_sums.shape[0] // NCHIP,) if flat else chip_sums.shape[1:]
    blk, count, index = _tiles(one, axis)
    n = NCHIP - 1

    def body(chip_ref, a_ref, *refs):
        acc = a_ref[...]
        for k in range(n):
            acc = acc + refs[k][...].astype(F32)
        refs[n][...] = acc

    spec = pl.BlockSpec(blk, lambda i, chip: index(i))
    if flat:
        mine = pl.BlockSpec(blk, lambda i, chip: (chip[0] * count + i,))
        part = [pl.BlockSpec(blk, lambda i, chip, k=k: (k * count + i,)) for k in range(n)]
    else:
        mine = pl.BlockSpec((None,) + blk, lambda i, chip: (chip[0],) + index(i))
        part = [pl.BlockSpec((None,) + blk, lambda i, chip, k=k: (k,) + index(i)) for k in range(n)]
    return pl.pallas_call(
        body, name=name,
        grid_spec=pltpu.PrefetchScalarGridSpec(num_scalar_prefetch=1, grid=(count,), in_specs=[mine] + part,
                                               out_specs=spec),
        out_shape=jax.ShapeDtypeStruct(one, F32),
        compiler_params=_params(1))(_chip(_coords()).astype(jnp.int32).reshape(1), chip_sums, *([parts] * n))


def _sum_stack(parts, name):
    n = parts.shape[0]

    def body(p_ref, o_ref):
        acc = p_ref[0]
        for k in range(1, n):
            acc = acc + p_ref[k]
        o_ref[...] = acc

    return pl.pallas_call(body, name=name, out_shape=jax.ShapeDtypeStruct(parts.shape[1:], F32))(parts)


def _coords():
    return lax.axis_index("x"), lax.axis_index("y"), lax.axis_index("c")


def _chip(who):
    return 2 * who[0] + who[1]


def _flip(who, mask):
    return tuple((1 - v) if b else v for v, b in zip(who, mask))


def _transfer(transfers, t, I, O, ssem, rsem, receiving):
    tr, me = transfers[t], _coords()
    peer = _flip(me, tr["mask"])
    return pltpu.make_async_remote_copy(
        src_ref=tr["src"](I, O, me), dst_ref=tr["dst"](I, O, peer if receiving else me),
        send_sem=ssem.at[t], recv_sem=rsem.at[t], device_id=peer, device_id_type=MESH)


def _start_transfers(transfers, I, O, ssem, rsem, onward):
    arrived = set()
    for t, tr in enumerate(transfers):
        after = tr.get("after")
        if (after is not None) != onward:
            continue
        if after is not None and after not in arrived:
            _transfer(transfers, after, I, O, ssem, rsem, True).wait_recv()
            arrived.add(after)
        _transfer(transfers, t, I, O, ssem, rsem, False).start()


def _finish_transfers(transfers, I, O, ssem, rsem):
    passed_on = {tr["after"] for tr in transfers if tr.get("after") is not None}
    for t in range(len(transfers)):
        if t not in passed_on:
            _transfer(transfers, t, I, O, ssem, rsem, True).wait_recv()
    for t in range(len(transfers)):
        _transfer(transfers, t, I, O, ssem, rsem, False).wait_send()


def _own_copies(own, I, O, stage, lsem, leg):
    for n, (src, dst) in enumerate(own):
        me = _coords()
        bring =pltpu.make_async_copy(src(I, O, me), stage[n], lsem.at[2 * n])
        put = pltpu.make_async_copy(stage[n], dst(I, O, me), lsem.at[2 * n + 1])
        if leg == 0:
            bring.start()
        elif leg == 1:
            bring.wait()
            put.start()
        else:
            put.wait()


def _own_scratch(own, ins):
    return [pltpu.VMEM(ins[n].shape, ins[n].dtype) for n in range(len(own))], pltpu.SemaphoreType.DMA((max(2 * len(own), 1),))


def _exchange(name, ins, outs, transfers, own=()):
    ni, no = len(ins), len(outs)
    nt = len(transfers)
    stages, stage_sems = _own_scratch(own, ins)

    def body(*refs):
        I, O = refs[:ni], refs[ni:ni + no]
        ssem, rsem, lsem = refs[ni + no:ni + no + 3]
        stage = refs[ni + no + 3:]
        _own_copies(own, I, O, stage, lsem, 0)
        _start_transfers(transfers, I, O, ssem, rsem, False)
        _own_copies(own, I, O, stage, lsem, 1)
        _start_transfers(transfers, I, O, ssem, rsem, True)
        _finish_transfers(transfers, I, O, ssem, rsem)
        _own_copies(own, I, O, stage, lsem, 2)

    hbm = pl.BlockSpec(memory_space=pltpu.HBM)
    return pl.pallas_call(
        body, name=name, in_specs=[hbm] * ni, out_specs=[hbm] * no,
        out_shape=[jax.ShapeDtypeStruct(s, d) for s, d in outs],
        scratch_shapes=[pltpu.SemaphoreType.DMA((nt,)), pltpu.SemaphoreType.DMA((nt,)), stage_sems] + stages,
        compiler_params=pltpu.CompilerParams(has_side_effects=True, vmem_limit_bytes=VMEM_LIMIT))(*ins)


CHIP_MASKS = [(0, 1, 0), (1, 0, 0), (1, 1, 0)]
SIBLING = (0, 0, 1)


def _half(shape2d, axis, which):
    n = shape2d[axis] // 2
    cut = pl.ds(pl.multiple_of(which * n, n), n)
    return (cut, slice(None)) if axis == 0 else (slice(None), cut)


class _Riding:
    def __init__(self, transfers, ins, outs, own=()):
        self.transfers, self.ins, self.outs, self.own = transfers, list(ins), list(outs), list(own)
        hbm = pl.BlockSpec(memory_space=pltpu.HBM)
        self.in_specs, self.out_specs = [hbm] * len(self.ins), [hbm] * len(self.outs)
        self.out_shape = [jax.ShapeDtypeStruct(s, d) for s, d in self.outs]
        stages, stage_sems = _own_scratch(self.own, self.ins)
        self.scratch = [pltpu.SemaphoreType.DMA((max(len(transfers), 1),))] * 2 + [stage_sems] + stages

    def alone(self, name):
        return _exchange(name, self.ins, self.outs, self.transfers, self.own)

    def hooks(self, I, O, ssem, rsem, lsem, *stage, first, middle, last):
        tr, own = self.transfers, self.own

        @pl.when(first)
        def _():
            _own_copies(own, I, O, stage, lsem, 0)
            _start_transfers(tr, I, O, ssem, rsem, False)

        if own or any(t.get("after") is not None for t in tr):
            @pl.when(middle)
            def _():
                _own_copies(own, I, O, stage, lsem, 1)
                _start_transfers(tr, I, O, ssem, rsem, True)

        def at_end():
            @pl.when(last)
            def _():
                _finish_transfers(tr, I, O, ssem, rsem)
                _own_copies(own, I, O, stage, lsem, 2)

        return at_end


def _stretch(n, pos):
    return (pl.ds(pos * n if isinstance(pos, int) else pl.multiple_of(pos * n, n), n),)


def _gather_plan(shards, axes):
    def half(a, who):
        if shards[a].ndim == 1:
            return _stretch(shards[a].shape[0] // 2, who[2])
        return _half(shards[a].shape, axes[a], who[2])

    def landed(a, chip, who):
        if shards[a].ndim == 1:
            return _stretch(shards[a].shape[0] // 2, 2 * chip + who[2])
        return (chip,) + half(a, who)

    over_ici, onward = [], []
    for a in range(len(shards)):
        for mask in CHIP_MASKS:
            over_ici.append(dict(
                mask=mask,
                src=lambda I, O, me, a=a: I[a].at[half(a, me)],
                dst=lambda I, O, who, a=a: O[a].at[landed(a, _chip(who), who)]))
            onward.append(dict(
                mask=SIBLING, after=len(over_ici) - 1,
                src=lambda I, O, me, a=a, mask=mask: O[a].at[landed(a, _chip(_flip(me, mask)), me)],
                dst=lambda I, O, who, a=a, mask=mask: O[a].at[landed(a, _chip(_flip(who, mask)), who)]))
    outs = [((NCHIP * s.shape[0],) if s.ndim == 1 else (NCHIP,) + s.shape, s.dtype) for s in shards]

    def whole(a, chip):
        return _stretch(shards[a].shape[0], chip) if shards[a].ndim == 1 else (chip,)

    own = [(lambda I, O, me, a=a: I[a], lambda I, O, me, a=a: O[a].at[whole(a, _chip(me))])
           for a in range(len(shards))]
    return over_ici + onward, outs, own


def _gather_shards(shards, axes):
    transfers, outs, own = _gather_plan(shards, axes)
    return _exchange("gather_weights", shards, outs, transfers, own)


def _halves_plan(blocks, axes):
    def cut(a, which):
        return (slice(None),) + _half(blocks[a].shape[1:], axes[a], which)

    transfers, outs = [], []
    for a, (b, ax) in enumerate(zip(blocks, axes)):
        if b.ndim == 1:
            h = b.shape[0] // NCHIP // 2
            for k in range(NCHIP):
                transfers.append(dict(mask=SIBLING,
                                      src=lambda I, O, me, a=a, k=k, h=h: I[a].at[_stretch(h, 2 * k + 1 - me[2])],
                                      dst=lambda I, O, who, a=a, k=k, h=h: O[a].at[_stretch(h, k)]))
            outs.append(((NCHIP * h,), b.dtype))
        else:
            transfers.append(dict(mask=SIBLING, src=lambda I, O, me, a=a: I[a].at[cut(a, 1 - me[2])],
                                  dst=lambda I, O, who, a=a: O[a]))
            shape = list(b.shape)
            shape[ax + 1] //= 2
            outs.append((tuple(shape), b.dtype))
    return transfers, outs


def _scatter_plan(tb):
    def slot(a, k):
        return (k,) if tb[a].ndim == 3 else _stretch(tb[a].shape[0] // NCHIP, k)

    transfers = []
    for a in range(len(tb)):
        for n, mask in enumerate(CHIP_MASKS):
            transfers.append(dict(
                mask=mask,
                src=lambda I, O, me, a=a, mask=mask: I[a].at[slot(a, _chip(_flip(me, mask)))],
                dst=lambda I, O, who, a=a, n=n: O[a].at[slot(a, n)]))
    outs = [((3,) + t.shape[1:] if t.ndim == 3 else (3 * (t.shape[0] // NCHIP),), t.dtype) for t in tb]
    return transfers, outs


def _last_exchange(vec, halves):
    def slot(who):
        return 4 * who[0] + 2 * who[1] + who[2]

    masks = [(m >> 2 & 1, m >> 1 & 1, m & 1) for m in range(1, 8)]
    transfers = [dict(mask=mask, src=lambda I, O, me: I[0], dst=lambda I, O, who: O[0].at[slot(who)])
                 for mask in masks]
    transfers += [dict(mask=SIBLING, src=lambda I, O, me, a=a: I[a], dst=lambda I, O, who, a=a: O[a])
                  for a in range(1, 1 + len(halves))]
    own = [(lambda I, O, me: I[0], lambda I, O, me: O[0].at[slot(me)])]
    outs = [((8,) + vec.shape, vec.dtype)] + [(t.shape, t.dtype) for t in halves]
    res = _exchange("last_exchange", [vec] + list(halves), outs, transfers, own)
    return res[0], res[1:]


def _rope_tables(positions):
    half = ROT // 2
    inv_freq = jnp.power(jnp.float32(THETA), -jnp.arange(0, ROT, 2, dtype=F32) / ROT)
    ang = positions.astype(F32)[:, None] * inv_freq[None, :]
    cos, sin = jnp.cos(ang), jnp.sin(ang)
    one, zero, z8 = jnp.ones((S, HD - ROT), F32), jnp.zeros((S, HD - ROT), F32), jnp.zeros((S, half), F32)
    c = jnp.concatenate([cos, cos, one], axis=1)
    a = jnp.concatenate([-sin, z8, zero], axis=1)
    b = jnp.concatenate([z8, sin, zero], axis=1)
    return tuple(jnp.tile(t, (1, 2)) for t in (c, a, b))


def _tile_heads(g, w):
    return jnp.tile(g.reshape(1, HD), (1, w // HD))


def _fold_heads(dg):
    return dg.reshape(-1, HD).sum(axis=0)


def _pad_lanes(a):
    return jnp.pad(a, ((0, 0), (0, LANES - a.shape[1])))


def _local_step(x, target, positions, wt, fetch, late_weights, begin_reduce):
    rope = _rope_tables(positions)
    w1t = wt["w_in_a_t"]
    f_row = 3 * D // LANES
    wg_t = w1t[3 * D + NH:]
    in_b_block = lambda c: pl.BlockSpec((None, TN_WIDE, TN_), lambda j, i: (c, j, 0))
    b_pad = _pad_lanes(wt["b_forget"].reshape(1, NH))
    qg_a, kg_a = _tile_heads(wt["qnorm_a_g"], D), _tile_heads(wt["knorm_a_g"], D)
    qg_b, kg_b = _tile_heads(wt["qnorm_b_g"], D), _tile_heads(wt["knorm_b_g"], KVW)
    norm_a, kv_g, norm_b = wt["norm_a_g"].reshape(1, D), wt["kv_norm_g"].reshape(1, D), wt["norm_b_g"].reshape(1, D)
    sinks_t = jnp.repeat(wt["sinks"].reshape(1, NH), HD, axis=1)

    (u_a,) = _rmsnorm_fwd(x, [norm_a], "norm_a")
    qkv = _mm("proj_a", S, 3 * D, [(u_a, _a_rows(D), w1t, _b_rows(D, tn=TN_WIDE), NT)], tn=TN_WIDE)
    fpad = _mm("proj_f", S, LANES, [(u_a, _a_rows(D), w1t, _b_rows(D, row0=f_row, tn=LANES), NT)], tn=LANES)
    gate_a = _mm("proj_gate_a", S, D, [(u_a, _a_rows(D), wg_t, _b_rows(D, tn=TN_WIDE), NT)], tn=TN_WIDE)
    q_a, k_a, v_a = _a_post(qkv, qg_a, kg_a)
    ct = _forget_cumsum(fpad, b_pad)
    ct2 = ct[:NH].reshape(NH // 2, 2, S)
    o_a, lse_a, y_a, fetched = _fox_fwd(q_a, k_a, v_a, ct2, gate_a, fetch)
    wt = {**wt, **late_weights(fetched)}
    w_in_b = wt["w_in_b"]
    h1 = _mm("out_a", S, D, [(y_a, _a_rows(D), wt["w_out_a"], _b_cols(D, tn=TN_WIDE), None)], add=x, tn=TN_WIDE)
    u_kv, u_b = _rmsnorm_fwd(h1, [kv_g, norm_b], "norm_b")
    kv = _mm("proj_kv", S, 2 * KVW, [(u_kv, _a_rows(D), wt["w_kv"], _b_cols(D), None)])
    pb = _mm("proj_b", S, 2 * D,
             [(u_b, _a_rows(D), w_in_b, pl.BlockSpec((None, D, TN_), lambda j, i: (j, 0, 0)), None)])
    q_b, kdup, vdup = _b_post(pb, kv, qg_b, kg_b, rope)
    gate_b_col = D // LANES
    o_b, lse_b, y_b = _swa_fwd(q_b, kdup, vdup, sinks_t, pb, gate_b_col)
    d_out, d_out_b, sq = _out_loss(y_b, wt["w_out_b"], h1, target)

    g = {}
    g["w_out_b"] = _mm("dw_out_b", D, D, [(y_b, _a_cols(S), d_out_b, _b_cols(S, tn=TN_WIDE), TN)], tn=TN_WIDE)
    d_y_b = _mm("dy_b", S, D, [(d_out_b, _a_rows(D), wt["w_out_b"], _b_rows(D, tn=TN_WIDE), NT)], tn=TN_WIDE)
    dq_b, dkdup, dvdup, dsk, d_gate_b = _swa_bwd(q_b, kdup, vdup, sinks_t, o_b, lse_b, d_y_b, pb, gate_b_col)
    g["sinks"] = dsk[0, ::HD]
    d_qb_raw, dg = _headnorm_bwd(pb, 0, qg_b, dq_b, rope, "qnorm_b_bwd")
    g["qnorm_b_g"] = _fold_heads(dg)
    d_pb = [d_qb_raw, d_qb_raw, d_gate_b, d_gate_b]
    g["w_in_b"] = jnp.concatenate([
        _mm("dw_in_b_q", D, D, [(u_b, _a_cols(S), d_qb_raw, _b_cols(S), TN)], stacked=True),
        _mm("dw_in_b_gate", D, D, [(u_b, _a_cols(S), d_gate_b, _b_cols(S), TN)], stacked=True)], axis=0)
    d_u_b = _mm("du_b", S, D, [(d_pb[c], _a_rows(TN_, col=c % 2), w_in_b, in_b_block(c), NT) for c in range(NCHIP)],
                tn=TN_WIDE)
    d_kv, dg = _kv_bwd(dkdup, dvdup, kv, kg_b, rope)
    g["knorm_b_g"] = _fold_heads(dg)
    g["w_kv"] = _mm("dw_kv", D, 2 * KVW, [(u_kv, _a_cols(S), d_kv, _b_cols(S), TN)])
    d_u_kv = _mm("du_kv", S, D, [(d_kv, _a_rows(2 * KVW), wt["w_kv"], _b_rows(2 * KVW, tn=TN_WIDE), NT)], tn=TN_WIDE)
    d_h1, d_h1_b, g["kv_norm_g"], g["norm_b_g"] = _rmsnorm_bwd(h1, [kv_g, norm_b], [d_u_kv, d_u_b], d_out, "norm_b_bwd")
    g["w_out_a"] = _mm("dw_out_a", D, D, [(y_a, _a_cols(S), d_h1_b, _b_cols(S, tn=TN_WIDE), TN)], tn=TN_WIDE)
    late = {n: g[n] for n in LATE}
    d_y_a, halves = _mm("dy_a", S, D, [(d_h1_b, _a_rows(D), wt["w_out_a"], _b_rows(D, tn=TN_WIDE), NT)],
                        tn=TN_WIDE, riding=begin_reduce(late))
    riding, so_far = begin_reduce(late, halves)
    dq_a, dk_a, dv_a, dct, d_gate_a, arrived = _fox_bwd(q_a, k_a, v_a, ct2, o_a, lse_a, d_y_a, gate_a, riding)
    dct_pad = jnp.pad(dct.reshape(NH, S), ((0, LANES - NH), (0, 0)))
    d_f, db = _forget_bwd(dct_pad, fpad, b_pad)
    g["b_forget"] = db[0, :NH]
    d_q_raw, dg = _headnorm_bwd(qkv, 0, qg_a, dq_a, None, "qnorm_a_bwd")
    g["qnorm_a_g"] = _fold_heads(dg)
    d_k_raw, dg = _headnorm_bwd(qkv, 1, kg_a, dk_a, None, "knorm_a_bwd")
    g["knorm_a_g"] = _fold_heads(dg)
    rows, gw = 4 * D + NH, None
    for n, t, row0 in (("q", d_q_raw, 0), ("k", d_k_raw, D), ("v", dv_a, 2 * D)):
        gw = _mm("dw_in_a_" + n, D, D, [(t, _a_cols(S), u_a, _b_cols(S, tn=TN_WIDE), TN)], tn=TN_WIDE,
                 rows_of=(gw, rows, row0))
    gw = _mm("dw_in_a_f", LANES, D, [(d_f, _a_cols(S, tm=LANES), u_a, _b_cols(S, tn=TN_WIDE), TN)], tm=LANES,
             tn=TN_WIDE, rows_of=(gw, rows, 3 * D))
    g["w_in_a"] = _mm("dw_in_a_gate", D, D, [(d_gate_a, _a_cols(S), u_a, _b_cols(S, tn=TN_WIDE), TN)], tn=TN_WIDE,
                      rows_of=(gw, rows, 3 * D + NH))
    first = {"w_in_a": g["w_in_a"]}
    riding, so_far_first = begin_reduce(first, begin_reduce(first).alone("sibling_halves_w_in_a"))
    d_u_a, arrived_first = _mm("du_a", S, D, [
        (d_q_raw, _a_rows(D), w1t, _b_cols(D, row=0, tn=TN_WIDE), None),
        (d_k_raw, _a_rows(D), w1t, _b_cols(D, row=1, tn=TN_WIDE), None),
        (dv_a, _a_rows(D), w1t, _b_cols(D, row=2, tn=TN_WIDE), None),
        (d_gate_a, _a_rows(D), wg_t, _b_cols(D, tn=TN_WIDE), None),
        (d_f, _a_rows(LANES), w1t, _b_cols(LANES, row=f_row, tn=TN_WIDE), None)], tn=TN_WIDE, riding=riding)
    d_x, _, g["norm_a_g"] = _rmsnorm_bwd(x, [norm_a], [d_u_a], d_h1, "norm_a_bwd")
    return sq, d_x, g, (list(so_far_first) + list(so_far), list(arrived_first) + list(arrived))


BIG = ["w_in_a", "w_out_a", "w_kv", "w_in_b", "w_out_b"]
LATE = BIG[1:]
SPLIT = {"w_in_a": None, "w_out_a": 0, "w_kv": 0, "w_in_b": 0, "w_out_b": 0}
SMALL = ["norm_a_g", "b_forget", "qnorm_a_g", "knorm_a_g", "kv_norm_g", "knorm_b_g", "norm_b_g", "qnorm_b_g", "sinks"]
NAMES = ["norm_a_g", "w_in_a", "b_forget", "qnorm_a_g", "knorm_a_g", "w_out_a", "kv_norm_g", "w_kv", "knorm_b_g",
         "norm_b_g", "w_in_b", "qnorm_b_g", "sinks", "w_out_b"]


def _pack(vals):
    flat = []
    for v in vals:
        v = v.reshape(-1)
        flat.append(jnp.pad(v, (0, -v.shape[0] % LANES)))
    flat = jnp.concatenate(flat)
    flat = jnp.pad(flat, (0, -flat.shape[0] % (8 * LANES)))
    return flat.reshape(-1, LANES)


def _unpack(packed, shapes):
    flat, out, off = packed.reshape(-1), [], 0
    for s in shapes:
        n = int(np.prod(s))
        out.append(flat[off:off + n].reshape(s))
        off += n + (-n % LANES)
    return out


def kernel(x, positions, norm_a_g, w_in_a, b_forget, qnorm_a_g, knorm_a_g, w_out_a, kv_norm_g, w_kv, knorm_b_g, norm_b_g, w_in_b, qnorm_b_g, sinks, w_out_b, loss_target, m_norm_a_g, m_w_in_a, m_b_forget, m_qnorm_a_g, m_knorm_a_g, m_w_out_a, m_kv_norm_g, m_w_kv, m_knorm_b_g, m_norm_b_g, m_w_in_b, m_qnorm_b_g, m_sinks, m_w_out_b, v_norm_a_g, v_w_in_a, v_b_forget, v_qnorm_a_g, v_knorm_a_g, v_w_out_a, v_kv_norm_g, v_w_kv, v_knorm_b_g, v_norm_b_g, v_w_in_b, v_qnorm_b_g, v_sinks, v_w_out_b):
    w = dict(norm_a_g=norm_a_g, w_in_a=w_in_a, b_forget=b_forget, qnorm_a_g=qnorm_a_g, knorm_a_g=knorm_a_g,
             w_out_a=w_out_a, kv_norm_g=kv_norm_g, w_kv=w_kv, knorm_b_g=knorm_b_g, norm_b_g=norm_b_g,
             w_in_b=w_in_b, qnorm_b_g=qnorm_b_g, sinks=sinks, w_out_b=w_out_b)
    m = dict(norm_a_g=m_norm_a_g, w_in_a=m_w_in_a, b_forget=m_b_forget, qnorm_a_g=m_qnorm_a_g, knorm_a_g=m_knorm_a_g,
             w_out_a=m_w_out_a, kv_norm_g=m_kv_norm_g, w_kv=m_w_kv, knorm_b_g=m_knorm_b_g, norm_b_g=m_norm_b_g,
             w_in_b=m_w_in_b, qnorm_b_g=m_qnorm_b_g, sinks=m_sinks, w_out_b=m_w_out_b)
    v = dict(norm_a_g=v_norm_a_g, w_in_a=v_w_in_a, b_forget=v_b_forget, qnorm_a_g=v_qnorm_a_g, knorm_a_g=v_knorm_a_g,
             w_out_a=v_w_out_a, kv_norm_g=v_kv_norm_g, w_kv=v_w_kv, knorm_b_g=v_knorm_b_g, norm_b_g=v_norm_b_g,
             w_in_b=v_w_in_b, qnorm_b_g=v_qnorm_b_g, sinks=v_sinks, w_out_b=v_w_out_b)
    my_chip = 2 * lax.axis_index("x") + lax.axis_index("y")

    def shard2d(t, n):
        if n == "w_in_a":
            return jnp.transpose(t, (2, 0, 1)).reshape(-1)
        return t.reshape(t.shape[-2:])

    def unflat(t, n):
        return jnp.transpose(t.reshape(-1, 1, D), (1, 2, 0)) if n == "w_in_a" else t.reshape(w[n].shape)

    w2d = {n: shard2d(w[n], n) for n in BIG}

    norm_a_rows = jnp.broadcast_to(norm_a_g.reshape(1, D // NCHIP), (2 * SUBLANES, D // NCHIP))
    w1t, norm_rows = _gather_shards([w2d["w_in_a"].astype(BF16), norm_a_rows], [SPLIT["w_in_a"], 0])
    wt = {"w_in_a_t": w1t.reshape(-1, D), "norm_a_g": norm_rows[:, 0, :].reshape(1, D)}
    for n in SMALL[1:]:
        wt[n] = w[n]
    late_shards = [w2d[n].astype(BF16) for n in LATE]
    late_axes = [SPLIT[n] for n in LATE]
    transfers, outs, own = _gather_plan(late_shards, late_axes)
    fetch = _Riding(transfers, late_shards, outs, own)

    def late_weights(fetched):
        return {n: t if n == "w_in_b" else t.reshape(-1, t.shape[2]) for n, t in zip(LATE, fetched)}

    def as_blocks(t):
        if t.ndim == 3:
            return t
        return t.reshape(-1) if t.shape[0] % (SUBLANES * NCHIP) else t.reshape(NCHIP, -1, t.shape[1])

    def begin_reduce(grads, halves=None):
        names = list(grads)
        axes = [SPLIT[n] for n in names]
        blocks = [as_blocks(grads[n]) for n in names]
        if halves is None:
            transfers, outs = _halves_plan(blocks, axes)
            return _Riding(transfers, blocks, outs)
        sums = [_chip_sum(blk, part, ax, "chip_sum_" + n) for n, ax, blk, part in zip(names, axes, blocks, halves)]
        bf16 = [s[1] for s in sums]
        transfers, outs = _scatter_plan(bf16)
        return _Riding(transfers, bf16, outs), [s[0] for s in sums]

    sq, d_x, g, (chip_f32, arrived) = _local_step(x[0], loss_target[0], positions, wt, fetch, late_weights,
                                                  begin_reduce)

    axes = [SPLIT[n] for n in BIG]
    halves = [_mesh_sum(t32, parts, ax, "mesh_sum_" + n) for n, ax, t32, parts in zip(BIG, axes, chip_f32, arrived)]

    small_shapes = [(D,), (NH,), (HD,), (HD,), (D,), (HD,), (D,), (HD,), (NH,), (D,)]
    gathered_small, sibling_done = _last_exchange(_pack([g[n] for n in SMALL] + [sq]), halves)
    total = _sum_stack(gathered_small, "sum_small")
    small_g = dict(zip(SMALL, _unpack(total, small_shapes)[:-1]))
    loss = 0.5 * jnp.sum(_unpack(total, small_shapes)[-1]) / D
    small_g["norm_a_g"] = lax.dynamic_slice(small_g["norm_a_g"], (my_chip * (D // NCHIP),), (D // NCHIP,))

    res = {}
    for n, ax, mine_half, their_half in zip(BIG, axes, halves, sibling_done):
        out4 = _adamw_halves(w2d[n], mine_half, their_half, shard2d(m[n], n), shard2d(v[n], n), ax, "adamw_" + n)
        res[n] = tuple(unflat(t, n) for t in out4)
    row = lambda t: t.reshape(1, -1)
    small_out = _adamw_small(*[[row(d[n]) for n in SMALL] for d in (w, small_g, m, v)])
    for i, n in enumerate(SMALL):
        res[n] = tuple(t.reshape(w[n].shape) for t in (small_g[n],) + tuple(out[i] for out in small_out))

    outs = [loss, d_x[None]]
    for k in range(4):
        outs += [res[n][k] for n in NAMES]
    return tuple(outs)
```

```python
import numpy as np
import jax
import jax.numpy as jnp
from jax import lax
from jax.experimental import pallas as pl
from jax.experimental.pallas import tpu as pltpu

F32, BF16 = jnp.float32, jnp.bfloat16
S, D, HD, NH, NKV = 2048, 1024, 64, 16, 4
KVW = NKV * HD
WINDOW = 128
ROT = HD // 4
THETA = 500000.0
EPS = 1e-6
SCALE = HD ** -0.5
LANES = 128
SUBLANES = 8
NEG = -1e30
VMEM_LIMIT = 48 * 2 ** 20
ROWS = 512
ATT = 512
SWQ = 16
NCHIP = 4
ADAM_LR, ADAM_B1, ADAM_B2, ADAM_EPS, ADAM_WD, ADAM_STEP = 0.001, 0.9, 0.999, 1e-08, 0.01, 10
NT = (((1,), (1,)), ((), ()))
TN = (((0,), (0,)), ((), ()))
MESH = pl.DeviceIdType.MESH


def _params(n):
    return pltpu.CompilerParams(dimension_semantics=("arbitrary",) * n, vmem_limit_bytes=VMEM_LIMIT)


def _dot(a, b, dims=None):
    if dims is None:
        return jnp.dot(a, b, preferred_element_type=F32)
    return lax.dot_general(a, b, dims, preferred_element_type=F32)


def _dot_split(a, b, n):
    out, rest = None, a
    for _ in range(n):
        hi = rest.astype(BF16)
        term = _dot(hi, b)
        out = term if out is None else out + term
        rest = rest - hi.astype(F32)
    return out


def _seg_mat(w):
    e = (np.arange(w)[:, None] // HD == np.arange(LANES)[None, :]).astype(np.float32)
    return jnp.asarray(e, BF16)


def _spread(r, w):
    head = lax.broadcasted_iota(jnp.int32, (2 * LANES, w), 1) >> (HD.bit_length() - 1)
    row = lax.broadcasted_iota(jnp.int32, (2 * LANES, w), 0)
    et2 = jnp.where(head == (row & (LANES - 1)), 1.0, 0.0).astype(BF16)
    hi = r.astype(BF16)
    lo = (r - hi.astype(F32)).astype(BF16)
    return _dot(jnp.concatenate([hi, lo], axis=1), et2)


def _head_rstd(x, e):
    ss = _dot_split(x * x, e, 2)
    return _spread(lax.rsqrt(ss * (1.0 / HD) + EPS), x.shape[1])


def _rope(x, c, a, b):
    w = x.shape[1]
    return x * c + pltpu.roll(x, w - ROT // 2, 1) * a + pltpu.roll(x, ROT // 2, 1) * b


def _rope_t(dy, c, a, b):
    w = dy.shape[1]
    return dy * c + pltpu.roll(dy * b, w - ROT // 2, 1) + pltpu.roll(dy * a, ROT // 2, 1)


def _sigmoid(x):
    return 1.0 / (1.0 + jnp.exp(-x))


def _row_spec(shape, ts):
    nd = len(shape)
    if shape[0] == S:
        return pl.BlockSpec((ts,) + tuple(shape[1:]), lambda i: (i,) + (0,) * (nd - 1))
    return pl.BlockSpec(tuple(shape), lambda i: (0,) * nd)


def _rows_call(body, name, ins, outs, ts=ROWS):
    return pl.pallas_call(
        body, name=name, grid=(S // ts,),
        in_specs=[_row_spec(a.shape, ts) for a in ins],
        out_specs=[_row_spec(s, ts) for s, _ in outs],
        out_shape=[jax.ShapeDtypeStruct(s, d) for s, d in outs],
        compiler_params=_params(1))(*ins)


def _col_spec(ts, w, col):
    return pl.BlockSpec((ts, w), lambda i: (i, col))


TM = TN_ = 512
TM_TOKENS = 1024
TN_WIDE = 1024


def _mm(name, m, n, terms, out_dtype=F32, add=None, tm=None, tn=TN_, stacked=False, riding=None, rows_of=None):
    nterm = len(terms)
    if tm is None:
        tm = TM_TOKENS if m == S else TM
    nj, ni_ = n // tn, m // tm
    n_in = 2 * nterm + (add is not None) + (rows_of is not None and rows_of[0] is not None)
    r_in, r_out = (len(riding.ins), len(riding.outs)) if riding is not None else (0, 0)

    def body(*refs):
        if riding is not None:
            j, i = pl.program_id(0), pl.program_id(1)
            at_end = riding.hooks(refs[n_in:n_in + r_in], refs[n_in + r_in + 1:n_in + r_in + 1 + r_out],
                                  *refs[n_in + r_in + 1 + r_out:], first=(j == 0) & (i == 0),
                                  middle=(j == nj // 2) & (i == 0), last=(j == nj - 1) & (i == ni_ - 1))
        acc = None
        for t in range(nterm):
            part = _dot(refs[2 * t][...], refs[2 * t + 1][...], terms[t][4])
            acc = part if acc is None else acc + part
        if add is not None:
            acc = acc + refs[2 * nterm][...]
        refs[n_in + r_in][...] = acc.astype(out_dtype)
        if riding is not None:
            at_end()

    tile = pl.BlockSpec((tm, tn), lambda j, i: (i, j))
    ins, specs = [], []
    for a, a_spec, b, b_spec, _ in terms:
        ins += [a, b]
        specs += [a_spec, b_spec]
    if add is not None:
        ins.append(add)
        specs.append(tile)
    out_spec = pl.BlockSpec((None, tm, tn), lambda j, i: (j, i, 0)) if stacked else tile
    out_shape = jax.ShapeDtypeStruct((nj, m, tn) if stacked else (m, n), out_dtype)
    if rows_of is not None:
        taller, rows, row0 = rows_of
        out_spec = pl.BlockSpec((pl.Element(tm), pl.Element(tn)), lambda j, i: (
            pl.multiple_of(row0 + i * tm, SUBLANES), pl.multiple_of(j * tn, LANES)))
        out_shape = jax.ShapeDtypeStruct((rows, n), out_dtype)
        alias = {}
        if taller is not None:
            ins.append(taller)
            specs.append(pl.BlockSpec(memory_space=pltpu.HBM))
            alias = {len(ins) - 1: 0}
        return pl.pallas_call(body, name=name, grid=(nj, ni_), in_specs=specs, out_specs=out_spec,
                              out_shape=out_shape, input_output_aliases=alias, compiler_params=_params(2))(*ins)
    if riding is None:
        return pl.pallas_call(body, name=name, grid=(nj, ni_), in_specs=specs, out_specs=out_spec,
                              out_shape=out_shape, compiler_params=_params(2))(*ins)
    res = pl.pallas_call(
        body, name=name, grid=(nj, ni_), in_specs=specs + riding.in_specs,
        out_specs=[out_spec] + riding.out_specs, out_shape=[out_shape] + riding.out_shape,
        scratch_shapes=riding.scratch, compiler_params=_params(2))(*ins, *riding.ins)
    return res[0], res[1:]


def _a_rows(k, col=0, tm=TM_TOKENS):
    return pl.BlockSpec((tm, k), lambda j, i: (i, col))


def _a_cols(k, tm=TM):
    return pl.BlockSpec((k, tm), lambda j, i: (0, i))


def _b_cols(k, row=0, col0=0, tn=TN_):
    return pl.BlockSpec((k, tn), lambda j, i: (row, col0 + j))


def _b_rows(k, row0=0, tn=TN_):
    return pl.BlockSpec((tn, k), lambda j, i: (row0 + j, 0))


def _rmsnorm_fwd(x, gains, name):
    def body(*refs):
        xv = refs[0][...]
        r = lax.rsqrt(jnp.mean(xv * xv, axis=-1, keepdims=True) + EPS)
        xh = xv * r
        for n in range(len(gains)):
            refs[1 + len(gains) + n][...] = (xh * refs[1 + n][...]).astype(BF16)

    return _rows_call(body, name, [x] + list(gains), [((S, D), BF16)] * len(gains))


def _rmsnorm_bwd(x, gains, dus, dres, name):
    n = len(gains)

    def body(*refs):
        x_ref, g_refs, du_refs, dres_ref = refs[0], refs[1:1 + n], refs[1 + n:1 + 2 * n], refs[1 + 2 * n]
        dx_ref, dxb_ref, dg_refs = refs[2 + 2 * n], refs[3 + 2 * n], refs[4 + 2 * n:]
        xv = x_ref[...]
        r = lax.rsqrt(jnp.mean(xv * xv, axis=-1, keepdims=True) + EPS)
        xh = xv * r
        gy = None
        for m in range(n):
            du = du_refs[m][...]
            part = jnp.sum(du * xh, axis=0, keepdims=True)

            @pl.when(pl.program_id(0) == 0)
            def _(m=m, part=part):
                dg_refs[m][...] = part

            @pl.when(pl.program_id(0) != 0)
            def _(m=m, part=part):
                dg_refs[m][...] += part

            t = du * g_refs[m][...]
            gy = t if gy is None else gy + t
        dx = dres_ref[...] + r * (gy - xh * jnp.mean(gy * xh, axis=-1, keepdims=True))
        dx_ref[...] = dx
        dxb_ref[...] = dx.astype(BF16)

    outs = [((S, D), F32), ((S, D), BF16)] + [((1, D), F32)] * n
    return _rows_call(body, name, [x] + list(gains) + list(dus) + [dres], outs)


def _a_post(qkvg, qg, kg):
    e = _seg_mat(D)

    def body(q_ref, k_ref, v_ref, qg_ref, kg_ref, e_ref, qo, ko, vo):
        ev = e_ref[...]
        qv, kv = q_ref[...], k_ref[...]
        qo[...] = (qv * _head_rstd(qv, ev) * qg_ref[...] * SCALE).astype(BF16)
        ko[...] = (kv * _head_rstd(kv, ev) * kg_ref[...]).astype(BF16)
        vo[...] = v_ref[...].astype(BF16)

    whole = lambda a: pl.BlockSpec(a.shape, lambda i: (0, 0))
    return pl.pallas_call(
        body, name="a_post", grid=(S // ROWS,),
        in_specs=[_col_spec(ROWS, D, 0), _col_spec(ROWS, D, 1), _col_spec(ROWS, D, 2),
                  whole(qg), whole(kg), whole(e)],
        out_specs=[_col_spec(ROWS, D, 0)] * 3,
        out_shape=[jax.ShapeDtypeStruct((S, D), BF16)] * 3,
        compiler_params=_params(1))(qkvg, qkvg, qkvg, qg, kg, e)


def _tri(upper):
    r, c = np.arange(ROWS)[:, None], np.arange(ROWS)[None, :]
    return jnp.asarray((r <= c) if upper else (r >= c), BF16)


def _forget_cumsum(fpad, bpad):
    def body(f_ref, b_ref, u_ref, c_ref, carry):
        @pl.when(pl.program_id(0) == 0)
        def _():
            carry[...] = jnp.zeros_like(carry)

        lf = jax.nn.log_sigmoid(f_ref[...] + b_ref[...])
        blk = _dot_split(lf.T, u_ref[...], 3) + carry[:, 0:1]
        c_ref[...] = blk
        carry[...] = jnp.broadcast_to(blk[:, ROWS - 1:ROWS], carry.shape)

    return pl.pallas_call(
        body, name="forget_cumsum", grid=(S // ROWS,),
        in_specs=[pl.BlockSpec((ROWS, LANES), lambda i: (i, 0)), pl.BlockSpec((1, LANES), lambda i: (0, 0)),
                  pl.BlockSpec((ROWS, ROWS), lambda i: (0, 0))],
        out_specs=pl.BlockSpec((LANES, ROWS), lambda i: (0, i)),
        out_shape=jax.ShapeDtypeStruct((LANES, S), F32),
        scratch_shapes=[pltpu.VMEM((LANES, LANES), F32)],
        compiler_params=_params(1))(fpad, bpad, _tri(True))


def _forget_bwd(dct, fpad, bpad):
    nb = S // ROWS

    def body(dc_ref, f_ref, b_ref, l_ref, df_ref, db_ref, carry):
        @pl.when(pl.program_id(0) == 0)
        def _():
            carry[...] = jnp.zeros_like(carry)
            db_ref[...] = jnp.zeros_like(db_ref)

        blk = _dot_split(dc_ref[...], l_ref[...], 3) + carry[:, 0:1]
        carry[...] = jnp.broadcast_to(blk[:, 0:1], carry.shape)
        df = blk.T * _sigmoid(-(f_ref[...] + b_ref[...]))
        df_ref[...] = df.astype(BF16)
        db_ref[...] += jnp.sum(df, axis=0, keepdims=True)

    return pl.pallas_call(
        body, name="forget_bwd", grid=(nb,),
        in_specs=[pl.BlockSpec((LANES, ROWS), lambda i: (0, nb - 1 - i)),
                  pl.BlockSpec((ROWS, LANES), lambda i: (nb - 1 - i, 0)),
                  pl.BlockSpec((1, LANES), lambda i: (0, 0)), pl.BlockSpec((ROWS, ROWS), lambda i: (0, 0))],
        out_specs=[pl.BlockSpec((ROWS, LANES), lambda i: (nb - 1 - i, 0)), pl.BlockSpec((1, LANES), lambda i: (0, 0))],
        out_shape=[jax.ShapeDtypeStruct((S, LANES), BF16), jax.ShapeDtypeStruct((1, LANES), F32)],
        scratch_shapes=[pltpu.VMEM((LANES, LANES), F32)],
        compiler_params=_params(1))(dct, fpad, bpad, _tri(False))


def _headnorm_bwd(x, col, gain, dy, rope, name):
    e = _seg_mat(D)
    tabs = list(rope) if rope is not None else []

    def body(*refs):
        x_ref, g_ref, dy_ref, e_ref = refs[:4]
        dx_ref, dg_ref = refs[-2:]
        xv, dyv, ev = x_ref[...], dy_ref[...], e_ref[...]
        if rope is not None:
            c, a, b = (jnp.tile(t[...], (1, D // LANES)) for t in refs[4:7])
            dyv = _rope_t(dyv, c, a, b)
        r = _head_rstd(xv, ev)
        xh = xv * r
        part = jnp.sum(dyv * xh, axis=0, keepdims=True)

        @pl.when(pl.program_id(0) == 0)
        def _():
            dg_ref[...] = part

        @pl.when(pl.program_id(0) != 0)
        def _():
            dg_ref[...] += part

        gy = dyv * g_ref[...]
        seg = _spread(_dot_split(gy * xh, ev, 2) * (1.0 / HD), D)
        dx_ref[...] = (r * (gy - xh * seg)).astype(BF16)

    whole = lambda a: pl.BlockSpec(a.shape, lambda i: (0, 0))
    return pl.pallas_call(
        body, name=name, grid=(S // ROWS,),
        in_specs=[_col_spec(ROWS, D, col), whole(gain), _col_spec(ROWS, D, 0), whole(e)]
                 + [pl.BlockSpec((ROWS, LANES), lambda i: (i, 0))] * len(tabs),
        out_specs=[_col_spec(ROWS, D, 0), whole(gain)],
        out_shape=[jax.ShapeDtypeStruct((S, D), BF16), jax.ShapeDtypeStruct((1, D), F32)],
        compiler_params=_params(1))(x, gain, dy, e, *tabs)


def _dup_mat():
    r, c = np.arange(KVW)[:, None], np.arange(2 * KVW)[None, :]
    return (r // HD == c // LANES) & (r % HD == c % HD)


def _fold_mat():
    r, c = np.arange(D)[:, None], np.arange(KVW)[None, :]
    return (r // (2 * LANES) == c // HD) & (r % HD == c % HD)


def _b_post(pb, kv, qg, kg, rope):
    e, ek = _seg_mat(D), _seg_mat(KVW)
    dup = jnp.asarray(_dup_mat(), BF16)

    def body(q_ref, k_ref, v_ref, qg_ref, kg_ref, e_ref, ek_ref, dup_ref, c_ref, a_ref, b_ref, qo, ko, vo):
        c1, a1, b1 = c_ref[...], a_ref[...], b_ref[...]
        qv = q_ref[...]
        qn = qv * _head_rstd(qv, e_ref[...]) * qg_ref[...]
        t = lambda z, n: jnp.tile(z, (1, n))
        qo[...] = (_rope(qn, t(c1, D // LANES), t(a1, D // LANES), t(b1, D // LANES)) * SCALE).astype(BF16)
        kvv = k_ref[...]
        kn = kvv * _head_rstd(kvv, ek_ref[...]) * kg_ref[...]
        kr = _rope(kn, t(c1, KVW // LANES), t(a1, KVW // LANES), t(b1, KVW // LANES)).astype(BF16)
        ko[...] = _dot(kr, dup_ref[...]).astype(BF16)
        vo[...] = _dot(v_ref[...].astype(BF16), dup_ref[...]).astype(BF16)

    whole = lambda a: pl.BlockSpec(a.shape, lambda i: (0, 0))
    tab = pl.BlockSpec((ROWS, LANES), lambda i: (i, 0))
    return pl.pallas_call(
        body, name="b_post", grid=(S // ROWS,),
        in_specs=[_col_spec(ROWS, D, 0), _col_spec(ROWS, KVW, 0), _col_spec(ROWS, KVW, 1),
                  whole(qg), whole(kg), whole(e), whole(ek), whole(dup), tab, tab, tab],
        out_specs=[_col_spec(ROWS, D, 0), _col_spec(ROWS, 2 * KVW, 0), _col_spec(ROWS, 2 * KVW, 0)],
        out_shape=[jax.ShapeDtypeStruct((S, D), BF16), jax.ShapeDtypeStruct((S, 2 * KVW), BF16),
                   jax.ShapeDtypeStruct((S, 2 * KVW), BF16)],
        compiler_params=_params(1))(pb, kv, kv, qg, kg, e, ek, dup, *rope)


def _kv_bwd(dkdup, dvdup, kv, kg, rope):
    ek = _seg_mat(KVW)
    fold = jnp.asarray(_fold_mat(), BF16)

    def body(dk_ref, dv_ref, k_ref, kg_ref, ek_ref, fold_ref, c_ref, a_ref, b_ref, dkv_ref, dg_ref):
        ev, fv = ek_ref[...], fold_ref[...]
        t = lambda z: jnp.tile(z[...], (1, KVW // LANES))
        dk = _rope_t(_dot_split(dk_ref[...], fv, 2), t(c_ref), t(a_ref), t(b_ref))
        dv = _dot_split(dv_ref[...], fv, 2)
        xv = k_ref[...]
        r = _head_rstd(xv, ev)
        xh = xv * r
        part = jnp.sum(dk * xh, axis=0, keepdims=True)

        @pl.when(pl.program_id(0) == 0)
        def _():
            dg_ref[...] = part

        @pl.when(pl.program_id(0) != 0)
        def _():
            dg_ref[...] += part

        gy = dk * kg_ref[...]
        seg = _spread(_dot_split(gy * xh, ev, 2) * (1.0 / HD), KVW)
        dkv_ref[:, 0:KVW] = (r * (gy - xh * seg)).astype(BF16)
        dkv_ref[:, KVW:2 * KVW] = dv.astype(BF16)

    whole = lambda a: pl.BlockSpec(a.shape, lambda i: (0, 0))
    tab = pl.BlockSpec((ROWS, LANES), lambda i: (i, 0))
    return pl.pallas_call(
        body, name="kv_bwd", grid=(S // ROWS,),
        in_specs=[_col_spec(ROWS, D, 0), _col_spec(ROWS, D, 0), _col_spec(ROWS, KVW, 0),
                  whole(kg), whole(ek), whole(fold), tab, tab, tab],
        out_specs=[_col_spec(ROWS, 2 * KVW, 0), whole(kg)],
        out_shape=[jax.ShapeDtypeStruct((S, 2 * KVW), BF16), jax.ShapeDtypeStruct((1, KVW), F32)],
        compiler_params=_params(1))(dkdup, dvdup, kv, kg, ek, fold, *rope)


def _out_norms(y, w_out, residual, gains):
    n = len(gains)

    def body(y_ref, w_ref, r_ref, *refs):
        h = _dot(y_ref[...], w_ref[...]) + r_ref[...]
        refs[n][...] = h
        hn = h * lax.rsqrt(jnp.mean(h * h, axis=-1, keepdims=True) + EPS)
        for k in range(n):
            refs[n + 1 + k][...] = (hn * refs[k][...]).astype(BF16)

    rows = pl.BlockSpec((TM_TOKENS, D), lambda i: (i, 0))
    whole = pl.BlockSpec((D, D), lambda i: (0, 0))
    gain = pl.BlockSpec((1, D), lambda i: (0, 0))
    return pl.pallas_call(
        body, name="out_a_norms", grid=(S // TM_TOKENS,), in_specs=[rows, whole, rows] + [gain] * n,
        out_specs=[rows] * (n + 1),
        out_shape=[jax.ShapeDtypeStruct((S, D), F32)] + [jax.ShapeDtypeStruct((S, D), BF16)] * n,
        compiler_params=_params(1))(y, w_out, residual, *gains)


def _out_loss(y, w_out, residual, target):
    def body(y_ref, w_ref, r_ref, t_ref, d_ref, db_ref, l_ref):
        diff = _dot(y_ref[...], w_ref[...]) + r_ref[...] - t_ref[...]
        d = diff * (1.0 / D)
        d_ref[...] = d
        db_ref[...] = d.astype(BF16)

        @pl.when(pl.program_id(0) == 0)
        def _():
            l_ref[...] = jnp.zeros_like(l_ref)

        l_ref[...] += jnp.sum(diff * diff, axis=0, keepdims=True)

    rows = pl.BlockSpec((TM_TOKENS, D), lambda i: (i, 0))
    whole = pl.BlockSpec((D, D), lambda i: (0, 0))
    return pl.pallas_call(
        body, name="out_b_loss", grid=(S // TM_TOKENS,), in_specs=[rows, whole, rows, rows],
        out_specs=[rows, rows, pl.BlockSpec((1, D), lambda i: (0, 0))],
        out_shape=[jax.ShapeDtypeStruct((S, D), F32), jax.ShapeDtypeStruct((S, D), BF16),
                   jax.ShapeDtypeStruct((1, D), F32)],
        compiler_params=_params(1))(y, w_out, residual, target)


def _lane():
    return lax.broadcasted_iota(jnp.int32, (1, LANES), 1)


def _head_mask(hh):
    return (_lane() < HD) if hh == 0 else (_lane() >= HD)


def _fox_fwd(q, k, v, ct, gate, riding):
    nq, npair = S // ATT, NH // 2
    ni, no = len(riding.ins), len(riding.outs)

    def body(q_ref, k_ref, v_ref, c_ref, gate_ref, *rest):
        o_ref, lse_ref, y_ref = rest[ni:ni + 3]
        pair, i = pl.program_id(0), pl.program_id(1)
        at_end = riding.hooks(rest[:ni], rest[ni + 3:ni + 3 + no], *rest[ni + 3 + no:],
                              first=(pair == 0) & (i == 0), middle=(pair == npair // 2) & (i == 0),
                              last=(pair == npair - 1) & (i == nq - 1))
        q2 = q_ref[...]
        qms = [jnp.where(_head_mask(hh), q2, jnp.zeros_like(q2)) for hh in (0, 1)]

        def probs(off, width, m, hh, diag):
            s = _dot(qms[hh], k_ref[pl.ds(off, width), :], NT) - c_ref[hh:hh + 1, pl.ds(off, width)]
            if diag:
                row = i * ATT + lax.broadcasted_iota(jnp.int32, (ATT, width), 0)
                col = off + lax.broadcasted_iota(jnp.int32, (ATT, width), 1)
                s = jnp.where(col <= row, s, NEG)
            m_new = jnp.maximum(m, jnp.max(s, axis=1, keepdims=True))
            p = jnp.exp(s - m_new)
            p_hi = p.astype(BF16)
            return m_new, jnp.exp(m - m_new), p_hi, (p - p_hi.astype(F32)).astype(BF16)

        def weighted(off, width, p_hi, p_lo, hh):
            vj = v_ref[pl.ds(off, width), :]
            v1 = jnp.where(_head_mask(hh), vj, jnp.ones_like(vj))
            return _dot(p_hi, v1) + _dot(p_lo, v1)

        def step(off, width, carry, diag):
            off = pl.multiple_of(off, ATT)
            out = []
            for hh in (0, 1):
                m, acc = carry[hh]
                m, alpha, p_hi, p_lo = probs(off, width, m, hh, diag)
                out.append((m, alpha * acc + weighted(off, width, p_hi, p_lo, hh)))
            return tuple(out)

        one = (jnp.full((ATT, 1), NEG, F32), jnp.zeros((ATT, LANES), F32))
        carry = lax.fori_loop(0, i // 2, lambda j, cr: step(j * (2 * ATT), 2 * ATT, cr, False), (one, one))
        carry = lax.cond(i % 2 == 1, lambda cr: step((i - 1) * ATT, 2 * ATT, cr, True),
                         lambda cr: step(i * ATT, ATT, cr, True), carry)
        res = []
        for hh in (0, 1):
            m, acc = carry[hh]
            l = jnp.max(jnp.where(_head_mask(1 - hh), acc, 0.0), axis=1, keepdims=True)
            res.append((acc / l, m + jnp.log(l)))
        first = _head_mask(0)
        o = jnp.where(first, res[0][0], res[1][0])
        o_ref[...] = o
        lse_ref[...] = jnp.where(first, res[0][1], res[1][1])
        g = gate_ref[...]
        y_ref[...] = (o * (g * _sigmoid(g))).astype(BF16)
        at_end()

    blk = pl.BlockSpec((ATT, LANES), lambda p, i: (i, p))
    full = pl.BlockSpec((S, LANES), lambda p, i: (0, p))
    res = pl.pallas_call(
        body, name="fox_fwd", grid=(npair, nq),
        in_specs=[blk, full, full, pl.BlockSpec((None, 2, S), lambda p, i: (p, 0, 0)), blk] + riding.in_specs,
        out_specs=[blk, blk, blk] + riding.out_specs,
        out_shape=[jax.ShapeDtypeStruct((S, D), F32)] * 2 + [jax.ShapeDtypeStruct((S, D), BF16)] + riding.out_shape,
        scratch_shapes=riding.scratch,
        compiler_params=_params(2))(q, k, v, ct, gate, *riding.ins)
    return res[0], res[1], res[2], res[3:]


def _gate_grads(dy, o, g):
    sg = _sigmoid(g)
    return dy * (g * sg), dy * o * (sg * (1.0 + g * (1.0 - sg)))


def _fox_bwd(q, k, v, ct, o, lse, dy, gate, riding):
    nq, npair = S // ATT, NH // 2
    ni, no = len(riding.ins), len(riding.outs)

    def body(q_ref, k_ref, v_ref, c_ref, o_ref, lse_ref, dy_ref, gate_ref, *rest):
        dq_ref, dk_ref, dvb_ref, dc_ref, dgate_ref = rest[ni:ni + 5]
        dv_ref = rest[ni + 5 + no]
        pair, i = pl.program_id(0), pl.program_id(1)
        at_end = riding.hooks(rest[:ni], rest[ni + 5:ni + 5 + no], *rest[ni + 6 + no:],
                              first=(pair == 0) & (i == 0), middle=(pair == npair // 2) & (i == 0),
                              last=(pair == npair - 1) & (i == nq - 1))

        @pl.when(i == 0)
        def _():
            dk_ref[...] = jnp.zeros_like(dk_ref)
            dv_ref[...] = jnp.zeros_like(dv_ref)
            dc_ref[...] = jnp.zeros_like(dc_ref)

        q2, lse2 = q_ref[...], lse_ref[...]
        do2, dgate = _gate_grads(dy_ref[...], o_ref[...], gate_ref[...])
        dgate_ref[...] = dgate.astype(BF16)
        do2b = do2.astype(BF16)
        prod = do2b.astype(F32) * o_ref[...]
        heads = []
        for hh in (0, 1):
            hm = _head_mask(hh)
            heads.append((jnp.where(hm, q2, jnp.zeros_like(q2)), jnp.where(hm, do2b, jnp.zeros_like(do2b)),
                          jnp.sum(jnp.where(hm, prod, 0.0), axis=1, keepdims=True),
                          jnp.max(jnp.where(hm, lse2, NEG), axis=1, keepdims=True)))

        def step(off, width, dqs, diag):
            off = pl.multiple_of(off, ATT)
            kj, vj = k_ref[pl.ds(off, width), :], v_ref[pl.ds(off, width), :]
            dk, dv, out = None, None, []
            for hh in (0, 1):
                qm, dom, delta, lse_h = heads[hh]
                s = _dot(qm, kj, NT) - c_ref[hh:hh + 1, pl.ds(off, width)]
                p = jnp.exp(s - lse_h)
                if diag:
                    row = i * ATT + lax.broadcasted_iota(jnp.int32, (ATT, width), 0)
                    col = off + lax.broadcasted_iota(jnp.int32, (ATT, width), 1)
                    p = jnp.where(col <= row, p, 0.0)
                ds = p * (_dot(dom, vj, NT) - delta)
                dc_ref[hh:hh + 1, pl.ds(off, width)] += -jnp.sum(ds, axis=0, keepdims=True)
                dsb = ds.astype(BF16)
                dk_h, dv_h = _dot(dsb, qm, TN), _dot(p.astype(BF16), dom, TN)
                dk, dv = (dk_h, dv_h) if dk is None else (dk + dk_h, dv + dv_h)
                out.append(dqs[hh] + _dot(dsb, kj))
            dk_ref[pl.ds(off, width), :] += dk
            dv_ref[pl.ds(off, width), :] += dv
            return tuple(out)

        zero = jnp.zeros((ATT, LANES), F32)
        dqs = lax.fori_loop(0, i // 2, lambda j, acc: step(j * (2 * ATT), 2 * ATT, acc, False), (zero, zero))
        dqs = lax.cond(i % 2 == 1, lambda acc: step((i - 1) * ATT, 2 * ATT, acc, True),
                       lambda acc: step(i * ATT, ATT, acc, True), dqs)
        dq_ref[...] = jnp.where(_head_mask(0), dqs[0], dqs[1]) * SCALE

        @pl.when(i == nq - 1)
        def _():
            dvb_ref[...] = dv_ref[...].astype(BF16)

        at_end()

    blk = pl.BlockSpec((ATT, LANES), lambda p, i: (i, p))
    full = pl.BlockSpec((S, LANES), lambda p, i: (0, p))
    cspec = pl.BlockSpec((None, 2, S), lambda p, i: (p, 0, 0))
    res = pl.pallas_call(
        body, name="fox_bwd", grid=(npair, nq),
        in_specs=[blk, full, full, cspec, blk, blk, blk, blk] + riding.in_specs,
        out_specs=[blk, full, full, cspec, blk] + riding.out_specs,
        out_shape=[jax.ShapeDtypeStruct((S, D), F32)] * 2 + [jax.ShapeDtypeStruct((S, D), BF16),
                                                              jax.ShapeDtypeStruct((npair, 2, S), F32),
                                                              jax.ShapeDtypeStruct((S, D), BF16)]
                  + riding.out_shape,
        scratch_shapes=[pltpu.VMEM((S, LANES), F32)] + riding.scratch,
        compiler_params=_params(2))(q, k, v, ct, o, lse, dy, gate, *riding.ins)
    return res[0], res[1], res[2], res[3], res[4], res[5:]


def _both_heads(x):
    return jnp.concatenate([jnp.where(_head_mask(hh), x, jnp.zeros_like(x)) for hh in (0, 1)], axis=0)


def _per_head(col0, col1):
    return jnp.concatenate([jnp.broadcast_to(col0, (WINDOW, 1)), jnp.broadcast_to(col1, (WINDOW, 1))], axis=0)


def _unstack(x2):
    return jnp.where(_head_mask(0), x2[:WINDOW], x2[WINDOW:])


def _swa_valid(i, start):
    r = lax.broadcasted_iota(jnp.int32, (2 * WINDOW, 2 * WINDOW), 0)
    qabs = i * WINDOW + jnp.where(r >= WINDOW, r - WINDOW, r)
    kabs = start + lax.broadcasted_iota(jnp.int32, (2 * WINDOW, 2 * WINDOW), 1)
    return (kabs <= qabs) & (qabs - kabs < WINDOW)


def _swa_fwd(q, kdup, vdup, sinks_t, proj, gate_col):
    def body(q_ref, k_ref, v_ref, sk_ref, gate_ref, o_ref, lse_ref, y_ref):
        skv = sk_ref[...]
        first = _head_mask(0)
        for sb in range(SWQ):
            i = pl.program_id(1) * SWQ + sb
            rows = slice(sb * WINDOW, (sb + 1) * WINDOW)
            start = pl.multiple_of(jnp.maximum(i - 1, 0) * WINDOW, WINDOW)
            kk, vv = k_ref[pl.ds(start, 2 * WINDOW), :], v_ref[pl.ds(start, 2 * WINDOW), :]
            q2 = q_ref[rows, :]
            valid = _swa_valid(i, start)[:WINDOW]
            res = []
            for hh in (0, 1):
                hm = _head_mask(hh)
                sink = jnp.max(jnp.where(hm, skv, NEG), axis=1, keepdims=True)
                s = jnp.where(valid, _dot(jnp.where(hm, q2, jnp.zeros_like(q2)), kk, NT), NEG)
                m = jnp.maximum(jnp.max(s, axis=1, keepdims=True), sink)
                p = jnp.exp(s - m)
                l = jnp.sum(p, axis=1, keepdims=True) + jnp.exp(sink - m)
                res.append((_dot(p.astype(BF16), vv) / l, m + jnp.log(l)))
            o = jnp.where(first, res[0][0], res[1][0])
            o_ref[rows, :] = o
            lse_ref[rows, :] = jnp.where(first, res[0][1], res[1][1])
            g = gate_ref[rows, :]
            y_ref[rows, :] = (o * (g * _sigmoid(g))).astype(BF16)

    blk = pl.BlockSpec((SWQ * WINDOW, LANES), lambda p, i: (i, p))
    gate = pl.BlockSpec((SWQ * WINDOW, LANES), lambda p, i: (i, gate_col + p))
    full = pl.BlockSpec((S, LANES), lambda p, i: (0, p // 2))
    return pl.pallas_call(
        body, name="swa_fwd", grid=(NH // 2, S // (SWQ * WINDOW)),
        in_specs=[blk, full, full, pl.BlockSpec((1, LANES), lambda p, i: (0, p)), gate],
        out_specs=[blk, blk, blk],
        out_shape=[jax.ShapeDtypeStruct((S, D), F32)] * 2 + [jax.ShapeDtypeStruct((S, D), BF16)],
        compiler_params=_params(2))(q, kdup, vdup, sinks_t, proj)


def _swa_bwd(q, kdup, vdup, sinks_t, o, lse, dy, proj, gate_col):
    def body(q_ref, k_ref, v_ref, sk_ref, o_ref, lse_ref, dy_ref, gate_ref, dq_ref, dk_ref, dv_ref, dsk_ref,
             dgate_ref):
        @pl.when(pl.program_id(1) == 0)
        def _():
            dk_ref[...] = jnp.zeros_like(dk_ref)
            dv_ref[...] = jnp.zeros_like(dv_ref)
            dsk_ref[...] = jnp.zeros_like(dsk_ref)

        skv = sk_ref[...]
        first = _head_mask(0)
        sink = _per_head(*[jnp.max(jnp.where(_head_mask(hh), skv, NEG), axis=1, keepdims=True) for hh in (0, 1)])
        for sb in range(SWQ):
            i = pl.program_id(1) * SWQ + sb
            rows = slice(sb * WINDOW, (sb + 1) * WINDOW)
            start = pl.multiple_of(jnp.maximum(i - 1, 0) * WINDOW, WINDOW)
            kk, vv = k_ref[pl.ds(start, 2 * WINDOW), :], v_ref[pl.ds(start, 2 * WINDOW), :]
            do2, dgate = _gate_grads(dy_ref[rows, :], o_ref[rows, :], gate_ref[rows, :])
            dgate_ref[rows, :] = dgate.astype(BF16)
            do2b = do2.astype(BF16)
            prod, lse2 = do2b.astype(F32) * o_ref[rows, :], lse_ref[rows, :]
            qs, dos = _both_heads(q_ref[rows, :]), _both_heads(do2b)
            delta = jnp.concatenate([jnp.sum(jnp.where(_head_mask(hh), prod, 0.0), axis=1, keepdims=True)
                                     for hh in (0, 1)], axis=0)
            lse_h = jnp.concatenate([jnp.max(jnp.where(_head_mask(hh), lse2, NEG), axis=1, keepdims=True)
                                     for hh in (0, 1)], axis=0)
            p = jnp.where(_swa_valid(i, start), jnp.exp(_dot(qs, kk, NT) - lse_h), 0.0)
            dsb = (p * (_dot(dos, vv, NT) - delta)).astype(BF16)
            dk_ref[pl.ds(start, 2 * WINDOW), :] += _dot(dsb, qs, TN)
            dv_ref[pl.ds(start, 2 * WINDOW), :] += _dot(p.astype(BF16), dos, TN)
            dq_ref[rows, :] = _unstack(_dot(dsb, kk)) * SCALE
            t = jnp.exp(sink - lse_h) * delta
            dsk_ref[...] += -jnp.where(first, jnp.sum(t[:WINDOW], axis=0, keepdims=True),
                                       jnp.sum(t[WINDOW:], axis=0, keepdims=True))

    blk = pl.BlockSpec((SWQ * WINDOW, LANES), lambda p, i: (i, p))
    full = pl.BlockSpec((S, LANES), lambda p, i: (0, p // 2))
    acc = pl.BlockSpec((S, LANES), lambda p, i: (0, p))
    sk = pl.BlockSpec((1, LANES), lambda p, i: (0, p))
    gate = pl.BlockSpec((SWQ * WINDOW, LANES), lambda p, i: (i, gate_col + p))
    return pl.pallas_call(
        body, name="swa_bwd", grid=(NH // 2, S // (SWQ * WINDOW)),
        in_specs=[blk, full, full, sk, blk, blk, blk, gate],
        out_specs=[blk, acc, acc, sk, blk],
        out_shape=[jax.ShapeDtypeStruct((S, D), F32)] * 3 + [jax.ShapeDtypeStruct((1, D), F32),
                                                              jax.ShapeDtypeStruct((S, D), BF16)],
        compiler_params=_params(2))(q, kdup, vdup, sinks_t, o, lse, dy, proj)


def _adamw_math(w, g, m, v):
    m = ADAM_B1 * m + (1.0 - ADAM_B1) * g
    v = ADAM_B2 * v + (1.0 - ADAM_B2) * jnp.square(g)
    m_hat = m / (1.0 - ADAM_B1 ** ADAM_STEP)
    v_hat = v / (1.0 - ADAM_B2 ** ADAM_STEP)
    delta = -ADAM_LR * (m_hat / (jnp.sqrt(v_hat) + ADAM_EPS) + ADAM_WD * w)
    return delta, m, v


def _adamw_small(ws, gs, ms, vs):
    k = len(ws)

    def body(*refs):
        for p in range(k):
            w_ref, g_ref, m_ref, v_ref = (refs[q * k + p] for q in range(4))
            d, mo, vo = _adamw_math(w_ref[...], g_ref[...], m_ref[...], v_ref[...])
            refs[4 * k + p][...], refs[5 * k + p][...], refs[6 * k + p][...] = d, mo, vo

    res = pl.pallas_call(
        body, name="adamw_small",
        out_shape=[jax.ShapeDtypeStruct(t.shape, F32) for t in ws] * 3)(*ws, *gs, *ms, *vs)
    return res[:k], res[k:2 * k], res[2 * k:]


SUM_TILES = (512, 256, 128)


FLAT_BLOCK = 257 * 1024


def _tiles(shape, axis, lead=0, halves=False):
    if len(shape) == 1:
        count = shape[0] // FLAT_BLOCK
        return (FLAT_BLOCK,), count, lambda pos, *lead_idx: (sum(k * count for k in lead_idx) + pos,)
    r, c = shape
    tile = next(t for t in SUM_TILES if (shape[axis] // (2 if halves else 1)) % t == 0)
    blk = (tile, c) if axis == 0 else (r, tile)
    count = shape[axis] // tile

    def index(pos, *lead_idx):
        return tuple(lead_idx) + ((pos, 0) if axis == 0 else (0, pos))

    return (None,) * lead + blk, count, index


def _adamw_halves(w, g_mine, g_theirs, m, v, axis, name):
    blk, count, index = _tiles(w.shape, axis, halves=True)
    per_half = count // 2

    def body(w_ref, a_ref, b_ref, m_ref, v_ref, g_ref, d_ref, mo_ref, vo_ref):
        is_mine = pl.program_id(0) // per_half == lax.axis_index("c")
        g = jnp.where(is_mine, a_ref[...], b_ref[...])
        g_ref[...] = g
        d_ref[...], mo_ref[...], vo_ref[...] = _adamw_math(w_ref[...], g, m_ref[...], v_ref[...])

    spec = pl.BlockSpec(blk, lambda i: index(i))
    half = pl.BlockSpec(blk, lambda i: index(i % per_half))
    return pl.pallas_call(
        body, name=name, grid=(count,), in_specs=[spec, half, half, spec, spec], out_specs=[spec] * 4,
        out_shape=[jax.ShapeDtypeStruct(w.shape, F32)] * 4, compiler_params=_params(1))(w, g_mine, g_theirs, m, v)


def _chip_sum(blocks, from_sibling, axis, name):
    flat = blocks.ndim == 1
    blk, count, index = _tiles((from_sibling.shape[0] // NCHIP,) if flat else from_sibling.shape[1:], axis, lead=1)

    def body(lo_ref, hi_ref, p_ref, o32, o16):
        mine = jnp.where(lax.axis_index("c") == 0, lo_ref[...], hi_ref[...])
        acc = mine + p_ref[...]
        o32[...] = acc
        o16[...] = acc.astype(BF16)

    half = pl.BlockSpec(blk, lambda k, i: index(i, k))
    if flat:
        lo = pl.BlockSpec(blk, lambda k, i: (2 * count * k + i,))
        hi = pl.BlockSpec(blk, lambda k, i: (2 * count * k + count + i,))
    else:
        lo, hi = half, pl.BlockSpec(blk, lambda k, i: index(i + count, k))
    return pl.pallas_call(
        body, name=name, grid=(NCHIP, count), in_specs=[lo, hi, half], out_specs=[half, half],
        out_shape=[jax.ShapeDtypeStruct(from_sibling.shape, F32), jax.ShapeDtypeStruct(from_sibling.shape, BF16)],
        compiler_params=_params(2))(blocks, blocks, from_sibling)


def _mesh_sum(chip_sums, parts, axis, name):
    flat = chip_sums.ndim == 1
    one = (chip_sums.shape[0] // NCHIP,) if flat else chip_sums.shape[1:]
    blk, count, index = _tiles(one, axis)
    n = NCHIP - 1

    def body(chip_ref, a_ref, *refs):
        acc = a_ref[...]
        for k in range(n):
            acc = acc + refs[k][...].astype(F32)
        refs[n][...] = acc

    spec = pl.BlockSpec(blk, lambda i, chip: index(i))
    if flat:
        mine = pl.BlockSpec(blk, lambda i, chip: (chip[0] * count + i,))
        part = [pl.BlockSpec(blk, lambda i, chip, k=k: (k * count + i,)) for k in range(n)]
    else:
        mine = pl.BlockSpec((None,) + blk, lambda i, chip: (chip[0],) + index(i))
        part = [pl.BlockSpec((None,) + blk, lambda i, chip, k=k: (k,) + index(i)) for k in range(n)]
    return pl.pallas_call(
        body, name=name,
        grid_spec=pltpu.PrefetchScalarGridSpec(num_scalar_prefetch=1, grid=(count,), in_specs=[mine] + part,
                                               out_specs=spec),
        out_shape=jax.ShapeDtypeStruct(one, F32),
        compiler_params=_params(1))(_chip(_coords()).astype(jnp.int32).reshape(1), chip_sums, *([parts] * n))


def _sum_stack(parts, name):
    n = parts.shape[0]

    def body(p_ref, o_ref):
        acc = p_ref[0]
        for k in range(1, n):
            acc = acc + p_ref[k]
        o_ref[...] = acc

    return pl.pallas_call(body, name=name, out_shape=jax.ShapeDtypeStruct(parts.shape[1:], F32))(parts)


def _coords():
    return lax.axis_index("x"), lax.axis_index("y"), lax.axis_index("c")


def _chip(who):
    return 2 * who[0] + who[1]


def _flip(who, mask):
    return tuple((1 - v) if b else v for v, b in zip(who, mask))


def _transfer(transfers, t, I, O, ssem, rsem, receiving):
    tr, me = transfers[t], _coords()
    peer = _flip(me, tr["mask"])
    return pltpu.make_async_remote_copy(
        src_ref=tr["src"](I, O, me), dst_ref=tr["dst"](I, O, peer if receiving else me),
        send_sem=ssem.at[t], recv_sem=rsem.at[t], device_id=peer, device_id_type=MESH)


def _start_transfers(transfers, I, O, ssem, rsem, onward):
    arrived = set()
    for t, tr in enumerate(transfers):
        after = tr.get("after")
        if (after is not None) != onward:
            continue
        if after is not None and after not in arrived:
            _transfer(transfers, after, I, O, ssem, rsem, True).wait_recv()
            arrived.add(after)
        _transfer(transfers, t, I, O, ssem, rsem, False).start()


def _finish_transfers(transfers, I, O, ssem, rsem):
    passed_on = {tr["after"] for tr in transfers if tr.get("after") is not None}
    for t in range(len(transfers)):
        if t not in passed_on:
            _transfer(transfers, t, I, O, ssem, rsem, True).wait_recv()
    for t in range(len(transfers)):
        _transfer(transfers, t, I, O, ssem, rsem, False).wait_send()


def _own_copies(own, I, O, stage, lsem, leg):
    for n, (src, dst) in enumerate(own):
        me = _coords()
        bring =pltpu.make_async_copy(src(I, O, me), stage[n], lsem.at[2 * n])
        put = pltpu.make_async_copy(stage[n], dst(I, O, me), lsem.at[2 * n + 1])
        if leg == 0:
            bring.start()
        elif leg == 1:
            bring.wait()
            put.start()
        else:
            put.wait()


def _own_scratch(own, ins):
    return [pltpu.VMEM(ins[n].shape, ins[n].dtype) for n in range(len(own))], pltpu.SemaphoreType.DMA((max(2 * len(own), 1),))


def _exchange(name, ins, outs, transfers, own=()):
    ni, no = len(ins), len(outs)
    nt = len(transfers)
    stages, stage_sems = _own_scratch(own, ins)

    def body(*refs):
        I, O = refs[:ni], refs[ni:ni + no]
        ssem, rsem, lsem = refs[ni + no:ni + no + 3]
        stage = refs[ni + no + 3:]
        _own_copies(own, I, O, stage, lsem, 0)
        _start_transfers(transfers, I, O, ssem, rsem, False)
        _own_copies(own, I, O, stage, lsem, 1)
        _start_transfers(transfers, I, O, ssem, rsem, True)
        _finish_transfers(transfers, I, O, ssem, rsem)
        _own_copies(own, I, O, stage, lsem, 2)

    hbm = pl.BlockSpec(memory_space=pltpu.HBM)
    return pl.pallas_call(
        body, name=name, in_specs=[hbm] * ni, out_specs=[hbm] * no,
        out_shape=[jax.ShapeDtypeStruct(s, d) for s, d in outs],
        scratch_shapes=[pltpu.SemaphoreType.DMA((nt,)), pltpu.SemaphoreType.DMA((nt,)), stage_sems] + stages,
        compiler_params=pltpu.CompilerParams(has_side_effects=True, vmem_limit_bytes=VMEM_LIMIT))(*ins)


CHIP_MASKS = [(0, 1, 0), (1, 0, 0), (1, 1, 0)]
SIBLING = (0, 0, 1)


def _half(shape2d, axis, which):
    n = shape2d[axis] // 2
    cut = pl.ds(pl.multiple_of(which * n, n), n)
    return (cut, slice(None)) if axis == 0 else (slice(None), cut)


class _Riding:
    def __init__(self, transfers, ins, outs, own=()):
        self.transfers, self.ins, self.outs, self.own = transfers, list(ins), list(outs), list(own)
        hbm = pl.BlockSpec(memory_space=pltpu.HBM)
        self.in_specs, self.out_specs = [hbm] * len(self.ins), [hbm] * len(self.outs)
        self.out_shape = [jax.ShapeDtypeStruct(s, d) for s, d in self.outs]
        stages, stage_sems = _own_scratch(self.own, self.ins)
        self.scratch = [pltpu.SemaphoreType.DMA((max(len(transfers), 1),))] * 2 + [stage_sems] + stages

    def alone(self, name):
        return _exchange(name, self.ins, self.outs, self.transfers, self.own)

    def hooks(self, I, O, ssem, rsem, lsem, *stage, first, middle, last):
        tr, own = self.transfers, self.own

        @pl.when(first)
        def _():
            _own_copies(own, I, O, stage, lsem, 0)
            _start_transfers(tr, I, O, ssem, rsem, False)

        if own or any(t.get("after") is not None for t in tr):
            @pl.when(middle)
            def _():
                _own_copies(own, I, O, stage, lsem, 1)
                _start_transfers(tr, I, O, ssem, rsem, True)

        def at_end():
            @pl.when(last)
            def _():
                _finish_transfers(tr, I, O, ssem, rsem)
                _own_copies(own, I, O, stage, lsem, 2)

        return at_end


def _stretch(n, pos):
    return (pl.ds(pos * n if isinstance(pos, int) else pl.multiple_of(pos * n, n), n),)


def _gather_plan(shards, axes):
    def half(a, who):
        if shards[a].ndim == 1:
            return _stretch(shards[a].shape[0] // 2, who[2])
        return _half(shards[a].shape, axes[a], who[2])

    def landed(a, chip, who):
        if shards[a].ndim == 1:
            return _stretch(shards[a].shape[0] // 2, 2 * chip + who[2])
        return (chip,) + half(a, who)

    over_ici, onward = [], []
    for a in range(len(shards)):
        for mask in CHIP_MASKS:
            over_ici.append(dict(
                mask=mask,
                src=lambda I, O, me, a=a: I[a].at[half(a, me)],
                dst=lambda I, O, who, a=a: O[a].at[landed(a, _chip(who), who)]))
            onward.append(dict(
                mask=SIBLING, after=len(over_ici) - 1,
                src=lambda I, O, me, a=a, mask=mask: O[a].at[landed(a, _chip(_flip(me, mask)), me)],
                dst=lambda I, O, who, a=a, mask=mask: O[a].at[landed(a, _chip(_flip(who, mask)), who)]))
    outs = [((NCHIP * s.shape[0],) if s.ndim == 1 else (NCHIP,) + s.shape, s.dtype) for s in shards]

    def whole(a, chip):
        return _stretch(shards[a].shape[0], chip) if shards[a].ndim == 1 else (chip,)

    own = [(lambda I, O, me, a=a: I[a], lambda I, O, me, a=a: O[a].at[whole(a, _chip(me))])
           for a in range(len(shards))]
    return over_ici + onward, outs, own


def _gather_shards(shards, axes):
    transfers, outs, own = _gather_plan(shards, axes)
    return _exchange("gather_weights", shards, outs, transfers, own)


def _halves_plan(blocks, axes):
    def cut(a, which):
        return (slice(None),) + _half(blocks[a].shape[1:], axes[a], which)

    transfers, outs = [], []
    for a, (b, ax) in enumerate(zip(blocks, axes)):
        if b.ndim == 1:
            h = b.shape[0] // NCHIP // 2
            for k in range(NCHIP):
                transfers.append(dict(mask=SIBLING,
                                      src=lambda I, O, me, a=a, k=k, h=h: I[a].at[_stretch(h, 2 * k + 1 - me[2])],
                                      dst=lambda I, O, who, a=a, k=k, h=h: O[a].at[_stretch(h, k)]))
            outs.append(((NCHIP * h,), b.dtype))
        else:
            transfers.append(dict(mask=SIBLING, src=lambda I, O, me, a=a: I[a].at[cut(a, 1 - me[2])],
                                  dst=lambda I, O, who, a=a: O[a]))
            shape = list(b.shape)
            shape[ax + 1] //= 2
            outs.append((tuple(shape), b.dtype))
    return transfers, outs


def _scatter_plan(tb):
    def slot(a, k):
        return (k,) if tb[a].ndim == 3 else _stretch(tb[a].shape[0] // NCHIP, k)

    transfers = []
    for a in range(len(tb)):
        for n, mask in enumerate(CHIP_MASKS):
            transfers.append(dict(
                mask=mask,
                src=lambda I, O, me, a=a, mask=mask: I[a].at[slot(a, _chip(_flip(me, mask)))],
                dst=lambda I, O, who, a=a, n=n: O[a].at[slot(a, n)]))
    outs = [((3,) + t.shape[1:] if t.ndim == 3 else (3 * (t.shape[0] // NCHIP),), t.dtype) for t in tb]
    return transfers, outs


def _last_exchange(vec, halves):
    def slot(who):
        return 4 * who[0] + 2 * who[1] + who[2]

    masks = [(m >> 2 & 1, m >> 1 & 1, m & 1) for m in range(1, 8)]
    transfers = [dict(mask=mask, src=lambda I, O, me: I[0], dst=lambda I, O, who: O[0].at[slot(who)])
                 for mask in masks]
    transfers += [dict(mask=SIBLING, src=lambda I, O, me, a=a: I[a], dst=lambda I, O, who, a=a: O[a])
                  for a in range(1, 1 + len(halves))]
    own = [(lambda I, O, me: I[0], lambda I, O, me: O[0].at[slot(me)])]
    outs = [((8,) + vec.shape, vec.dtype)] + [(t.shape, t.dtype) for t in halves]
    res = _exchange("last_exchange", [vec] + list(halves), outs, transfers, own)
    return res[0], res[1:]


def _rope_tables(positions):
    half = ROT // 2
    inv_freq = jnp.power(jnp.float32(THETA), -jnp.arange(0, ROT, 2, dtype=F32) / ROT)
    ang = positions.astype(F32)[:, None] * inv_freq[None, :]
    cos, sin = jnp.cos(ang), jnp.sin(ang)
    one, zero, z8 = jnp.ones((S, HD - ROT), F32), jnp.zeros((S, HD - ROT), F32), jnp.zeros((S, half), F32)
    c = jnp.concatenate([cos, cos, one], axis=1)
    a = jnp.concatenate([-sin, z8, zero], axis=1)
    b = jnp.concatenate([z8, sin, zero], axis=1)
    return tuple(jnp.tile(t, (1, 2)) for t in (c, a, b))


def _tile_heads(g, w):
    return jnp.tile(g.reshape(1, HD), (1, w // HD))


def _fold_heads(dg):
    return dg.reshape(-1, HD).sum(axis=0)


def _pad_lanes(a):
    return jnp.pad(a, ((0, 0), (0, LANES - a.shape[1])))


def _local_step(x, target, positions, wt, fetch, late_weights, begin_reduce):
    rope = _rope_tables(positions)
    w1t = wt["w_in_a_t"]
    f_row = 3 * D // LANES
    wg_t = w1t[3 * D + NH:]
    in_b_block = lambda c: pl.BlockSpec((None, TN_WIDE, TN_), lambda j, i: (c, j, 0))
    b_pad = _pad_lanes(wt["b_forget"].reshape(1, NH))
    qg_a, kg_a = _tile_heads(wt["qnorm_a_g"], D), _tile_heads(wt["knorm_a_g"], D)
    qg_b, kg_b = _tile_heads(wt["qnorm_b_g"], D), _tile_heads(wt["knorm_b_g"], KVW)
    norm_a, kv_g, norm_b = wt["norm_a_g"].reshape(1, D), wt["kv_norm_g"].reshape(1, D), wt["norm_b_g"].reshape(1, D)
    sinks_t = jnp.repeat(wt["sinks"].reshape(1, NH), HD, axis=1)

    (u_a,) = _rmsnorm_fwd(x, [norm_a], "norm_a")
    qkv = _mm("proj_a", S, 3 * D, [(u_a, _a_rows(D), w1t, _b_rows(D, tn=TN_WIDE), NT)], tn=TN_WIDE)
    fpad = _mm("proj_f", S, LANES, [(u_a, _a_rows(D), w1t, _b_rows(D, row0=f_row, tn=LANES), NT)], tn=LANES)
    gate_a = _mm("proj_gate_a", S, D, [(u_a, _a_rows(D), wg_t, _b_rows(D, tn=TN_WIDE), NT)], tn=TN_WIDE)
    q_a, k_a, v_a = _a_post(qkv, qg_a, kg_a)
    ct = _forget_cumsum(fpad, b_pad)
    ct2 = ct[:NH].reshape(NH // 2, 2, S)
    o_a, lse_a, y_a, fetched = _fox_fwd(q_a, k_a, v_a, ct2, gate_a, fetch)
    wt = {**wt, **late_weights(fetched)}
    w_in_b = wt["w_in_b"]
    h1, u_kv, u_b = _out_norms(y_a, wt["w_out_a"], x, [kv_g, norm_b])
    kv = _mm("proj_kv", S, 2 * KVW, [(u_kv, _a_rows(D), wt["w_kv"], _b_cols(D), None)])
    pb = _mm("proj_b", S, 2 * D,
             [(u_b, _a_rows(D), w_in_b, pl.BlockSpec((None, D, TN_), lambda j, i: (j, 0, 0)), None)])
    q_b, kdup, vdup = _b_post(pb, kv, qg_b, kg_b, rope)
    gate_b_col = D // LANES
    o_b, lse_b, y_b = _swa_fwd(q_b, kdup, vdup, sinks_t, pb, gate_b_col)
    d_out, d_out_b, sq = _out_loss(y_b, wt["w_out_b"], h1, target)

    g = {}
    g["w_out_b"] = _mm("dw_out_b", D, D, [(y_b, _a_cols(S), d_out_b, _b_cols(S, tn=TN_WIDE), TN)], tn=TN_WIDE)
    d_y_b = _mm("dy_b", S, D, [(d_out_b, _a_rows(D), wt["w_out_b"], _b_rows(D, tn=TN_WIDE), NT)], tn=TN_WIDE)
    dq_b, dkdup, dvdup, dsk, d_gate_b = _swa_bwd(q_b, kdup, vdup, sinks_t, o_b, lse_b, d_y_b, pb, gate_b_col)
    g["sinks"] = dsk[0, ::HD]
    d_qb_raw, dg = _headnorm_bwd(pb, 0, qg_b, dq_b, rope, "qnorm_b_bwd")
    g["qnorm_b_g"] = _fold_heads(dg)
    d_pb = [d_qb_raw, d_qb_raw, d_gate_b, d_gate_b]
    g["w_in_b"] = jnp.concatenate([
        _mm("dw_in_b_q", D, D, [(u_b, _a_cols(S), d_qb_raw, _b_cols(S), TN)], stacked=True),
        _mm("dw_in_b_gate", D, D, [(u_b, _a_cols(S), d_gate_b, _b_cols(S), TN)], stacked=True)], axis=0)
    d_u_b = _mm("du_b", S, D, [(d_pb[c], _a_rows(TN_, col=c % 2), w_in_b, in_b_block(c), NT) for c in range(NCHIP)],
                tn=TN_WIDE)
    d_kv, dg = _kv_bwd(dkdup, dvdup, kv, kg_b, rope)
    g["knorm_b_g"] = _fold_heads(dg)
    g["w_kv"] = _mm("dw_kv", D, 2 * KVW, [(u_kv, _a_cols(S), d_kv, _b_cols(S), TN)])
    d_u_kv = _mm("du_kv", S, D, [(d_kv, _a_rows(2 * KVW), wt["w_kv"], _b_rows(2 * KVW, tn=TN_WIDE), NT)], tn=TN_WIDE)
    d_h1, d_h1_b, g["kv_norm_g"], g["norm_b_g"] = _rmsnorm_bwd(h1, [kv_g, norm_b], [d_u_kv, d_u_b], d_out, "norm_b_bwd")
    g["w_out_a"] = _mm("dw_out_a", D, D, [(y_a, _a_cols(S), d_h1_b, _b_cols(S, tn=TN_WIDE), TN)], tn=TN_WIDE)
    late = {n: g[n] for n in LATE}
    d_y_a, halves = _mm("dy_a", S, D, [(d_h1_b, _a_rows(D), wt["w_out_a"], _b_rows(D, tn=TN_WIDE), NT)],
                        tn=TN_WIDE, riding=begin_reduce(late))
    riding, so_far = begin_reduce(late, halves)
    dq_a, dk_a, dv_a, dct, d_gate_a, arrived = _fox_bwd(q_a, k_a, v_a, ct2, o_a, lse_a, d_y_a, gate_a, riding)
    dct_pad = jnp.pad(dct.reshape(NH, S), ((0, LANES - NH), (0, 0)))
    d_f, db = _forget_bwd(dct_pad, fpad, b_pad)
    g["b_forget"] = db[0, :NH]
    d_q_raw, dg = _headnorm_bwd(qkv, 0, qg_a, dq_a, None, "qnorm_a_bwd")
    g["qnorm_a_g"] = _fold_heads(dg)
    d_k_raw, dg = _headnorm_bwd(qkv, 1, kg_a, dk_a, None, "knorm_a_bwd")
    g["knorm_a_g"] = _fold_heads(dg)
    rows, gw = 4 * D + NH, None
    for n, t, row0 in (("q", d_q_raw, 0), ("k", d_k_raw, D), ("v", dv_a, 2 * D)):
        gw = _mm("dw_in_a_" + n, D, D, [(t, _a_cols(S), u_a, _b_cols(S, tn=TN_WIDE), TN)], tn=TN_WIDE,
                 rows_of=(gw, rows, row0))
    gw = _mm("dw_in_a_f", LANES, D, [(d_f, _a_cols(S, tm=LANES), u_a, _b_cols(S, tn=TN_WIDE), TN)], tm=LANES,
             tn=TN_WIDE, rows_of=(gw, rows, 3 * D))
    g["w_in_a"] = _mm("dw_in_a_gate", D, D, [(d_gate_a, _a_cols(S), u_a, _b_cols(S, tn=TN_WIDE), TN)], tn=TN_WIDE,
                      rows_of=(gw, rows, 3 * D + NH))
    first = {"w_in_a": g["w_in_a"]}
    riding, so_far_first = begin_reduce(first, begin_reduce(first).alone("sibling_halves_w_in_a"))
    d_u_a, arrived_first = _mm("du_a", S, D, [
        (d_q_raw, _a_rows(D), w1t, _b_cols(D, row=0, tn=TN_WIDE), None),
        (d_k_raw, _a_rows(D), w1t, _b_cols(D, row=1, tn=TN_WIDE), None),
        (dv_a, _a_rows(D), w1t, _b_cols(D, row=2, tn=TN_WIDE), None),
        (d_gate_a, _a_rows(D), wg_t, _b_cols(D, tn=TN_WIDE), None),
        (d_f, _a_rows(LANES), w1t, _b_cols(LANES, row=f_row, tn=TN_WIDE), None)], tn=TN_WIDE, riding=riding)
    d_x, _, g["norm_a_g"] = _rmsnorm_bwd(x, [norm_a], [d_u_a], d_h1, "norm_a_bwd")
    return sq, d_x, g, (list(so_far_first) + list(so_far), list(arrived_first) + list(arrived))


BIG = ["w_in_a", "w_out_a", "w_kv", "w_in_b", "w_out_b"]
LATE = BIG[1:]
SPLIT = {"w_in_a": None, "w_out_a": 0, "w_kv": 0, "w_in_b": 0, "w_out_b": 0}
SMALL = ["norm_a_g", "b_forget", "qnorm_a_g", "knorm_a_g", "kv_norm_g", "knorm_b_g", "norm_b_g", "qnorm_b_g", "sinks"]
NAMES = ["norm_a_g", "w_in_a", "b_forget", "qnorm_a_g", "knorm_a_g", "w_out_a", "kv_norm_g", "w_kv", "knorm_b_g",
         "norm_b_g", "w_in_b", "qnorm_b_g", "sinks", "w_out_b"]


def _pack(vals):
    flat = []
    for v in vals:
        v = v.reshape(-1)
        flat.append(jnp.pad(v, (0, -v.shape[0] % LANES)))
    flat = jnp.concatenate(flat)
    flat = jnp.pad(flat, (0, -flat.shape[0] % (8 * LANES)))
    return flat.reshape(-1, LANES)


def _unpack(packed, shapes):
    flat, out, off = packed.reshape(-1), [], 0
    for s in shapes:
        n = int(np.prod(s))
        out.append(flat[off:off + n].reshape(s))
        off += n + (-n % LANES)
    return out


def kernel(x, positions, norm_a_g, w_in_a, b_forget, qnorm_a_g, knorm_a_g, w_out_a, kv_norm_g, w_kv, knorm_b_g, norm_b_g, w_in_b, qnorm_b_g, sinks, w_out_b, loss_target, m_norm_a_g, m_w_in_a, m_b_forget, m_qnorm_a_g, m_knorm_a_g, m_w_out_a, m_kv_norm_g, m_w_kv, m_knorm_b_g, m_norm_b_g, m_w_in_b, m_qnorm_b_g, m_sinks, m_w_out_b, v_norm_a_g, v_w_in_a, v_b_forget, v_qnorm_a_g, v_knorm_a_g, v_w_out_a, v_kv_norm_g, v_w_kv, v_knorm_b_g, v_norm_b_g, v_w_in_b, v_qnorm_b_g, v_sinks, v_w_out_b):
    w = dict(norm_a_g=norm_a_g, w_in_a=w_in_a, b_forget=b_forget, qnorm_a_g=qnorm_a_g, knorm_a_g=knorm_a_g,
             w_out_a=w_out_a, kv_norm_g=kv_norm_g, w_kv=w_kv, knorm_b_g=knorm_b_g, norm_b_g=norm_b_g,
             w_in_b=w_in_b, qnorm_b_g=qnorm_b_g, sinks=sinks, w_out_b=w_out_b)
    m = dict(norm_a_g=m_norm_a_g, w_in_a=m_w_in_a, b_forget=m_b_forget, qnorm_a_g=m_qnorm_a_g, knorm_a_g=m_knorm_a_g,
             w_out_a=m_w_out_a, kv_norm_g=m_kv_norm_g, w_kv=m_w_kv, knorm_b_g=m_knorm_b_g, norm_b_g=m_norm_b_g,
             w_in_b=m_w_in_b, qnorm_b_g=m_qnorm_b_g, sinks=m_sinks, w_out_b=m_w_out_b)
    v = dict(norm_a_g=v_norm_a_g, w_in_a=v_w_in_a, b_forget=v_b_forget, qnorm_a_g=v_qnorm_a_g, knorm_a_g=v_knorm_a_g,
             w_out_a=v_w_out_a, kv_norm_g=v_kv_norm_g, w_kv=v_w_kv, knorm_b_g=v_knorm_b_g, norm_b_g=v_norm_b_g,
             w_in_b=v_w_in_b, qnorm_b_g=v_qnorm_b_g, sinks=v_sinks, w_out_b=v_w_out_b)
    my_chip = 2 * lax.axis_index("x") + lax.axis_index("y")

    def shard2d(t, n):
        if n == "w_in_a":
            return jnp.transpose(t, (2, 0, 1)).reshape(-1)
        return t.reshape(t.shape[-2:])

    def unflat(t, n):
        return jnp.transpose(t.reshape(-1, 1, D), (1, 2, 0)) if n == "w_in_a" else t.reshape(w[n].shape)

    w2d = {n: shard2d(w[n], n) for n in BIG}

    norm_a_rows = jnp.broadcast_to(norm_a_g.reshape(1, D // NCHIP), (2 * SUBLANES, D // NCHIP))
    w1t, norm_rows = _gather_shards([w2d["w_in_a"].astype(BF16), norm_a_rows], [SPLIT["w_in_a"], 0])
    wt = {"w_in_a_t": w1t.reshape(-1, D), "norm_a_g": norm_rows[:, 0, :].reshape(1, D)}
    for n in SMALL[1:]:
        wt[n] = w[n]
    late_shards = [w2d[n].astype(BF16) for n in LATE]
    late_axes = [SPLIT[n] for n in LATE]
    transfers, outs, own = _gather_plan(late_shards, late_axes)
    fetch = _Riding(transfers, late_shards, outs, own)

    def late_weights(fetched):
        return {n: t if n == "w_in_b" else t.reshape(-1, t.shape[2]) for n, t in zip(LATE, fetched)}

    def as_blocks(t):
        if t.ndim == 3:
            return t
        return t.reshape(-1) if t.shape[0] % (SUBLANES * NCHIP) else t.reshape(NCHIP, -1, t.shape[1])

    def begin_reduce(grads, halves=None):
        names = list(grads)
        axes = [SPLIT[n] for n in names]
        blocks = [as_blocks(grads[n]) for n in names]
        if halves is None:
            transfers, outs = _halves_plan(blocks, axes)
            return _Riding(transfers, blocks, outs)
        sums = [_chip_sum(blk, part, ax, "chip_sum_" + n) for n, ax, blk, part in zip(names, axes, blocks, halves)]
        bf16 = [s[1] for s in sums]
        transfers, outs = _scatter_plan(bf16)
        return _Riding(transfers, bf16, outs), [s[0] for s in sums]

    sq, d_x, g, (chip_f32, arrived) = _local_step(x[0], loss_target[0], positions, wt, fetch, late_weights,
                                                  begin_reduce)

    axes = [SPLIT[n] for n in BIG]
    halves = [_mesh_sum(t32, parts, ax, "mesh_sum_" + n) for n, ax, t32, parts in zip(BIG, axes, chip_f32, arrived)]

    small_shapes = [(D,), (NH,), (HD,), (HD,), (D,), (HD,), (D,), (HD,), (NH,), (D,)]
    gathered_small, sibling_done = _last_exchange(_pack([g[n] for n in SMALL] + [sq]), halves)
    total = _sum_stack(gathered_small, "sum_small")
    small_g = dict(zip(SMALL, _unpack(total, small_shapes)[:-1]))
    loss = 0.5 * jnp.sum(_unpack(total, small_shapes)[-1]) / D
    small_g["norm_a_g"] = lax.dynamic_slice(small_g["norm_a_g"], (my_chip * (D // NCHIP),), (D // NCHIP,))

    res = {}
    for n, ax, mine_half, their_half in zip(BIG, axes, halves, sibling_done):
        out4 = _adamw_halves(w2d[n], mine_half, their_half, shard2d(m[n], n), shard2d(v[n], n), ax, "adamw_" + n)
        res[n] = tuple(unflat(t, n) for t in out4)
    row = lambda t: t.reshape(1, -1)
    small_out = _adamw_small(*[[row(d[n]) for n in SMALL] for d in (w, small_g, m, v)])
    for i, n in enumerate(SMALL):
        res[n] = tuple(t.reshape(w[n].shape) for t in (small_g[n],) + tuple(out[i] for out in small_out))

    outs = [loss, d_x[None]]
    for k in range(4):
        outs += [res[n][k] for n in NAMES]
    return tuple(outs)
```

```python
import numpy as np
import jax
import jax.numpy as jnp
from jax import lax
from jax.experimental import pallas as pl
from jax.experimental.pallas import tpu as pltpu

F32, BF16 = jnp.float32, jnp.bfloat16
S, D, HD, NH, NKV = 2048, 1024, 64, 16, 4
KVW = NKV * HD
WINDOW = 128
ROT = HD // 4
THETA = 500000.0
EPS = 1e-6
SCALE = HD ** -0.5
LANES = 128
SUBLANES = 8
NEG = -1e30
VMEM_LIMIT = 48 * 2 ** 20
ROWS = 512
ATT = 512
SWQ = 16
NCHIP = 4
ADAM_LR, ADAM_B1, ADAM_B2, ADAM_EPS, ADAM_WD, ADAM_STEP = 0.001, 0.9, 0.999, 1e-08, 0.01, 10
NT = (((1,), (1,)), ((), ()))
TN = (((0,), (0,)), ((), ()))
MESH = pl.DeviceIdType.MESH


def _params(n):
    return pltpu.CompilerParams(dimension_semantics=("arbitrary",) * n, vmem_limit_bytes=VMEM_LIMIT)


def _dot(a, b, dims=None):
    if dims is None:
        return jnp.dot(a, b, preferred_element_type=F32)
    return lax.dot_general(a, b, dims, preferred_element_type=F32)


def _dot_split(a, b, n):
    out, rest = None, a
    for _ in range(n):
        hi = rest.astype(BF16)
        term = _dot(hi, b)
        out = term if out is None else out + term
        rest = rest - hi.astype(F32)
    return out


def _seg_mat(w):
    e = (np.arange(w)[:, None] // HD == np.arange(LANES)[None, :]).astype(np.float32)
    return jnp.asarray(e, BF16)


def _spread(r, w):
    head = lax.broadcasted_iota(jnp.int32, (2 * LANES, w), 1) >> (HD.bit_length() - 1)
    row = lax.broadcasted_iota(jnp.int32, (2 * LANES, w), 0)
    et2 = jnp.where(head == (row & (LANES - 1)), 1.0, 0.0).astype(BF16)
    hi = r.astype(BF16)
    lo = (r - hi.astype(F32)).astype(BF16)
    return _dot(jnp.concatenate([hi, lo], axis=1), et2)


def _head_rstd(x, e):
    ss = _dot_split(x * x, e, 2)
    return _spread(lax.rsqrt(ss * (1.0 / HD) + EPS), x.shape[1])


def _rope(x, c, a, b):
    w = x.shape[1]
    return x * c + pltpu.roll(x, w - ROT // 2, 1) * a + pltpu.roll(x, ROT // 2, 1) * b


def _rope_t(dy, c, a, b):
    w = dy.shape[1]
    return dy * c + pltpu.roll(dy * b, w - ROT // 2, 1) + pltpu.roll(dy * a, ROT // 2, 1)


def _sigmoid(x):
    return 1.0 / (1.0 + jnp.exp(-x))


def _row_spec(shape, ts):
    nd = len(shape)
    if shape[0] == S:
        return pl.BlockSpec((ts,) + tuple(shape[1:]), lambda i: (i,) + (0,) * (nd - 1))
    return pl.BlockSpec(tuple(shape), lambda i: (0,) * nd)


def _rows_call(body, name, ins, outs, ts=ROWS):
    return pl.pallas_call(
        body, name=name, grid=(S // ts,),
        in_specs=[_row_spec(a.shape, ts) for a in ins],
        out_specs=[_row_spec(s, ts) for s, _ in outs],
        out_shape=[jax.ShapeDtypeStruct(s, d) for s, d in outs],
        compiler_params=_params(1))(*ins)


def _col_spec(ts, w, col):
    return pl.BlockSpec((ts, w), lambda i: (i, col))


TM = TN_ = 512
TM_TOKENS = 1024
TN_WIDE = 1024


def _mm(name, m, n, terms, out_dtype=F32, add=None, tm=None, tn=TN_, stacked=False, riding=None, rows_of=None):
    nterm = len(terms)
    if tm is None:
        tm = TM_TOKENS if m == S else TM
    nj, ni_ = n // tn, m // tm
    n_in = 2 * nterm + (add is not None) + (rows_of is not None and rows_of[0] is not None)
    r_in, r_out = (len(riding.ins), len(riding.outs)) if riding is not None else (0, 0)

    def body(*refs):
        if riding is not None:
            j, i = pl.program_id(0), pl.program_id(1)
            at_end = riding.hooks(refs[n_in:n_in + r_in], refs[n_in + r_in + 1:n_in + r_in + 1 + r_out],
                                  *refs[n_in + r_in + 1 + r_out:], first=(j == 0) & (i == 0),
                                  middle=(j == nj // 2) & (i == 0), last=(j == nj - 1) & (i == ni_ - 1))
        acc = None
        for t in range(nterm):
            part = _dot(refs[2 * t][...], refs[2 * t + 1][...], terms[t][4])
            acc = part if acc is None else acc + part
        if add is not None:
            acc = acc + refs[2 * nterm][...]
        refs[n_in + r_in][...] = acc.astype(out_dtype)
        if riding is not None:
            at_end()

    tile = pl.BlockSpec((tm, tn), lambda j, i: (i, j))
    ins, specs = [], []
    for a, a_spec, b, b_spec, _ in terms:
        ins += [a, b]
        specs += [a_spec, b_spec]
    if add is not None:
        ins.append(add)
        specs.append(tile)
    out_spec = pl.BlockSpec((None, tm, tn), lambda j, i: (j, i, 0)) if stacked else tile
    out_shape = jax.ShapeDtypeStruct((nj, m, tn) if stacked else (m, n), out_dtype)
    if rows_of is not None:
        taller, rows, row0 = rows_of
        out_spec = pl.BlockSpec((pl.Element(tm), pl.Element(tn)), lambda j, i: (
            pl.multiple_of(row0 + i * tm, SUBLANES), pl.multiple_of(j * tn, LANES)))
        out_shape = jax.ShapeDtypeStruct((rows, n), out_dtype)
        alias = {}
        if taller is not None:
            ins.append(taller)
            specs.append(pl.BlockSpec(memory_space=pltpu.HBM))
            alias = {len(ins) - 1: 0}
        return pl.pallas_call(body, name=name, grid=(nj, ni_), in_specs=specs, out_specs=out_spec,
                              out_shape=out_shape, input_output_aliases=alias, compiler_params=_params(2))(*ins)
    if riding is None:
        return pl.pallas_call(body, name=name, grid=(nj, ni_), in_specs=specs, out_specs=out_spec,
                              out_shape=out_shape, compiler_params=_params(2))(*ins)
    res = pl.pallas_call(
        body, name=name, grid=(nj, ni_), in_specs=specs + riding.in_specs,
        out_specs=[out_spec] + riding.out_specs, out_shape=[out_shape] + riding.out_shape,
        scratch_shapes=riding.scratch, compiler_params=_params(2))(*ins, *riding.ins)
    return res[0], res[1:]


def _a_rows(k, col=0, tm=TM_TOKENS):
    return pl.BlockSpec((tm, k), lambda j, i: (i, col))


def _a_cols(k, tm=TM):
    return pl.BlockSpec((k, tm), lambda j, i: (0, i))


def _b_cols(k, row=0, col0=0, tn=TN_):
    return pl.BlockSpec((k, tn), lambda j, i: (row, col0 + j))


def _b_rows(k, row0=0, tn=TN_):
    return pl.BlockSpec((tn, k), lambda j, i: (row0 + j, 0))


def _rmsnorm_fwd(x, gains, name):
    def body(*refs):
        xv = refs[0][...]
        r = lax.rsqrt(jnp.mean(xv * xv, axis=-1, keepdims=True) + EPS)
        xh = xv * r
        for n in range(len(gains)):
            refs[1 + len(gains) + n][...] = (xh * refs[1 + n][...]).astype(BF16)

    return _rows_call(body, name, [x] + list(gains), [((S, D), BF16)] * len(gains))


def _rmsnorm_bwd(x, gains, dus, dres, name):
    n = len(gains)

    def body(*refs):
        x_ref, g_refs, du_refs, dres_ref = refs[0], refs[1:1 + n], refs[1 + n:1 + 2 * n], refs[1 + 2 * n]
        dx_ref, dxb_ref, dg_refs = refs[2 + 2 * n], refs[3 + 2 * n], refs[4 + 2 * n:]
        xv = x_ref[...]
        r = lax.rsqrt(jnp.mean(xv * xv, axis=-1, keepdims=True) + EPS)
        xh = xv * r
        gy = None
        for m in range(n):
            du = du_refs[m][...]
            part = jnp.sum(du * xh, axis=0, keepdims=True)

            @pl.when(pl.program_id(0) == 0)
            def _(m=m, part=part):
                dg_refs[m][...] = part

            @pl.when(pl.program_id(0) != 0)
            def _(m=m, part=part):
                dg_refs[m][...] += part

            t = du * g_refs[m][...]
            gy = t if gy is None else gy + t
        dx = dres_ref[...] + r * (gy - xh * jnp.mean(gy * xh, axis=-1, keepdims=True))
        dx_ref[...] = dx
        dxb_ref[...] = dx.astype(BF16)

    outs = [((S, D), F32), ((S, D), BF16)] + [((1, D), F32)] * n
    return _rows_call(body, name, [x] + list(gains) + list(dus) + [dres], outs)


def _proj_a(u, w1t, qg, kg):
    e = _seg_mat(D)
    gains = jnp.stack([qg * SCALE, kg])

    def body(u_ref, w_ref, g_ref, e_ref, raw_ref, out_ref):
        x = _dot(u_ref[...], w_ref[...], NT)
        raw_ref[...] = x

        @pl.when(pl.program_id(0) < 2)
        def _():
            out_ref[...] = (x * _head_rstd(x, e_ref[...]) * g_ref[...]).astype(BF16)

        @pl.when(pl.program_id(0) == 2)
        def _():
            out_ref[...] = x.astype(BF16)

    tm = TM_TOKENS
    return pl.pallas_call(
        body, name="proj_a", grid=(3, S // tm),
        in_specs=[pl.BlockSpec((tm, D), lambda j, i: (i, 0)), pl.BlockSpec((D, D), lambda j, i: (j, 0)),
                  pl.BlockSpec((None, 1, D), lambda j, i: (jnp.minimum(j, 1), 0, 0)),
                  pl.BlockSpec(e.shape, lambda j, i: (0, 0))],
        out_specs=[pl.BlockSpec((tm, D), lambda j, i: (i, j)), pl.BlockSpec((None, tm, D), lambda j, i: (j, i, 0))],
        out_shape=[jax.ShapeDtypeStruct((S, 3 * D), F32), jax.ShapeDtypeStruct((3, S, D), BF16)],
        compiler_params=_params(2))(u, w1t, gains, e)


def _tri(upper):
    r, c = np.arange(ROWS)[:, None], np.arange(ROWS)[None, :]
    return jnp.asarray((r <= c) if upper else (r >= c), BF16)


def _forget_cumsum(fpad, bpad):
    def body(f_ref, b_ref, u_ref, c_ref, carry):
        @pl.when(pl.program_id(0) == 0)
        def _():
            carry[...] = jnp.zeros_like(carry)

        lf = jax.nn.log_sigmoid(f_ref[...] + b_ref[...])
        blk = _dot_split(lf.T, u_ref[...], 3) + carry[:, 0:1]
        c_ref[...] = blk
        carry[...] = jnp.broadcast_to(blk[:, ROWS - 1:ROWS], carry.shape)

    return pl.pallas_call(
        body, name="forget_cumsum", grid=(S // ROWS,),
        in_specs=[pl.BlockSpec((ROWS, LANES), lambda i: (i, 0)), pl.BlockSpec((1, LANES), lambda i: (0, 0)),
                  pl.BlockSpec((ROWS, ROWS), lambda i: (0, 0))],
        out_specs=pl.BlockSpec((LANES, ROWS), lambda i: (0, i)),
        out_shape=jax.ShapeDtypeStruct((LANES, S), F32),
        scratch_shapes=[pltpu.VMEM((LANES, LANES), F32)],
        compiler_params=_params(1))(fpad, bpad, _tri(True))


def _forget_bwd(dct, fpad, bpad):
    nb = S // ROWS

    def body(dc_ref, f_ref, b_ref, l_ref, df_ref, db_ref, carry):
        @pl.when(pl.program_id(0) == 0)
        def _():
            carry[...] = jnp.zeros_like(carry)
            db_ref[...] = jnp.zeros_like(db_ref)

        blk = _dot_split(dc_ref[...], l_ref[...], 3) + carry[:, 0:1]
        carry[...] = jnp.broadcast_to(blk[:, 0:1], carry.shape)
        df = blk.T * _sigmoid(-(f_ref[...] + b_ref[...]))
        df_ref[...] = df.astype(BF16)
        db_ref[...] += jnp.sum(df, axis=0, keepdims=True)

    return pl.pallas_call(
        body, name="forget_bwd", grid=(nb,),
        in_specs=[pl.BlockSpec((LANES, ROWS), lambda i: (0, nb - 1 - i)),
                  pl.BlockSpec((ROWS, LANES), lambda i: (nb - 1 - i, 0)),
                  pl.BlockSpec((1, LANES), lambda i: (0, 0)), pl.BlockSpec((ROWS, ROWS), lambda i: (0, 0))],
        out_specs=[pl.BlockSpec((ROWS, LANES), lambda i: (nb - 1 - i, 0)), pl.BlockSpec((1, LANES), lambda i: (0, 0))],
        out_shape=[jax.ShapeDtypeStruct((S, LANES), BF16), jax.ShapeDtypeStruct((1, LANES), F32)],
        scratch_shapes=[pltpu.VMEM((LANES, LANES), F32)],
        compiler_params=_params(1))(dct, fpad, bpad, _tri(False))


def _headnorm_bwd(x, col, gain, dy, rope, name):
    e = _seg_mat(D)
    tabs = list(rope) if rope is not None else []

    def body(*refs):
        x_ref, g_ref, dy_ref, e_ref = refs[:4]
        dx_ref, dg_ref = refs[-2:]
        xv, dyv, ev = x_ref[...], dy_ref[...], e_ref[...]
        if rope is not None:
            c, a, b = (jnp.tile(t[...], (1, D // LANES)) for t in refs[4:7])
            dyv = _rope_t(dyv, c, a, b)
        r = _head_rstd(xv, ev)
        xh = xv * r
        part = jnp.sum(dyv * xh, axis=0, keepdims=True)

        @pl.when(pl.program_id(0) == 0)
        def _():
            dg_ref[...] = part

        @pl.when(pl.program_id(0) != 0)
        def _():
            dg_ref[...] += part

        gy = dyv * g_ref[...]
        seg = _spread(_dot_split(gy * xh, ev, 2) * (1.0 / HD), D)
        dx_ref[...] = (r * (gy - xh * seg)).astype(BF16)

    whole = lambda a: pl.BlockSpec(a.shape, lambda i: (0, 0))
    return pl.pallas_call(
        body, name=name, grid=(S // ROWS,),
        in_specs=[_col_spec(ROWS, D, col), whole(gain), _col_spec(ROWS, D, 0), whole(e)]
                 + [pl.BlockSpec((ROWS, LANES), lambda i: (i, 0))] * len(tabs),
        out_specs=[_col_spec(ROWS, D, 0), whole(gain)],
        out_shape=[jax.ShapeDtypeStruct((S, D), BF16), jax.ShapeDtypeStruct((1, D), F32)],
        compiler_params=_params(1))(x, gain, dy, e, *tabs)


def _dup_mat():
    r, c = np.arange(KVW)[:, None], np.arange(2 * KVW)[None, :]
    return (r // HD == c // LANES) & (r % HD == c % HD)


def _fold_mat():
    r, c = np.arange(D)[:, None], np.arange(KVW)[None, :]
    return (r // (2 * LANES) == c // HD) & (r % HD == c % HD)


def _b_post(pb, kv, qg, kg, rope):
    e, ek = _seg_mat(D), _seg_mat(KVW)
    dup = jnp.asarray(_dup_mat(), BF16)

    def body(q_ref, k_ref, v_ref, qg_ref, kg_ref, e_ref, ek_ref, dup_ref, c_ref, a_ref, b_ref, qo, ko, vo):
        c1, a1, b1 = c_ref[...], a_ref[...], b_ref[...]
        qv = q_ref[...]
        qn = qv * _head_rstd(qv, e_ref[...]) * qg_ref[...]
        t = lambda z, n: jnp.tile(z, (1, n))
        qo[...] = (_rope(qn, t(c1, D // LANES), t(a1, D // LANES), t(b1, D // LANES)) * SCALE).astype(BF16)
        kvv = k_ref[...]
        kn = kvv * _head_rstd(kvv, ek_ref[...]) * kg_ref[...]
        kr = _rope(kn, t(c1, KVW // LANES), t(a1, KVW // LANES), t(b1, KVW // LANES)).astype(BF16)
        ko[...] = _dot(kr, dup_ref[...]).astype(BF16)
        vo[...] = _dot(v_ref[...].astype(BF16), dup_ref[...]).astype(BF16)

    whole = lambda a: pl.BlockSpec(a.shape, lambda i: (0, 0))
    tab = pl.BlockSpec((ROWS, LANES), lambda i: (i, 0))
    return pl.pallas_call(
        body, name="b_post", grid=(S // ROWS,),
        in_specs=[_col_spec(ROWS, D, 0), _col_spec(ROWS, KVW, 0), _col_spec(ROWS, KVW, 1),
                  whole(qg), whole(kg), whole(e), whole(ek), whole(dup), tab, tab, tab],
        out_specs=[_col_spec(ROWS, D, 0), _col_spec(ROWS, 2 * KVW, 0), _col_spec(ROWS, 2 * KVW, 0)],
        out_shape=[jax.ShapeDtypeStruct((S, D), BF16), jax.ShapeDtypeStruct((S, 2 * KVW), BF16),
                   jax.ShapeDtypeStruct((S, 2 * KVW), BF16)],
        compiler_params=_params(1))(pb, kv, kv, qg, kg, e, ek, dup, *rope)


def _kv_bwd(dkdup, dvdup, kv, kg, rope):
    ek = _seg_mat(KVW)
    fold = jnp.asarray(_fold_mat(), BF16)

    def body(dk_ref, dv_ref, k_ref, kg_ref, ek_ref, fold_ref, c_ref, a_ref, b_ref, dkv_ref, dg_ref):
        ev, fv = ek_ref[...], fold_ref[...]
        t = lambda z: jnp.tile(z[...], (1, KVW // LANES))
        dk = _rope_t(_dot_split(dk_ref[...], fv, 2), t(c_ref), t(a_ref), t(b_ref))
        dv = _dot_split(dv_ref[...], fv, 2)
        xv = k_ref[...]
        r = _head_rstd(xv, ev)
        xh = xv * r
        part = jnp.sum(dk * xh, axis=0, keepdims=True)

        @pl.when(pl.program_id(0) == 0)
        def _():
            dg_ref[...] = part

        @pl.when(pl.program_id(0) != 0)
        def _():
            dg_ref[...] += part

        gy = dk * kg_ref[...]
        seg = _spread(_dot_split(gy * xh, ev, 2) * (1.0 / HD), KVW)
        dkv_ref[:, 0:KVW] = (r * (gy - xh * seg)).astype(BF16)
        dkv_ref[:, KVW:2 * KVW] = dv.astype(BF16)

    whole = lambda a: pl.BlockSpec(a.shape, lambda i: (0, 0))
    tab = pl.BlockSpec((ROWS, LANES), lambda i: (i, 0))
    return pl.pallas_call(
        body, name="kv_bwd", grid=(S // ROWS,),
        in_specs=[_col_spec(ROWS, D, 0), _col_spec(ROWS, D, 0), _col_spec(ROWS, KVW, 0),
                  whole(kg), whole(ek), whole(fold), tab, tab, tab],
        out_specs=[_col_spec(ROWS, 2 * KVW, 0), whole(kg)],
        out_shape=[jax.ShapeDtypeStruct((S, 2 * KVW), BF16), jax.ShapeDtypeStruct((1, KVW), F32)],
        compiler_params=_params(1))(dkdup, dvdup, kv, kg, ek, fold, *rope)


def _out_norms(y, w_out, residual, gains):
    n = len(gains)

    def body(y_ref, w_ref, r_ref, *refs):
        h = _dot(y_ref[...], w_ref[...]) + r_ref[...]
        refs[n][...] = h
        hn = h * lax.rsqrt(jnp.mean(h * h, axis=-1, keepdims=True) + EPS)
        for k in range(n):
            refs[n + 1 + k][...] = (hn * refs[k][...]).astype(BF16)

    rows = pl.BlockSpec((TM_TOKENS, D), lambda i: (i, 0))
    whole = pl.BlockSpec((D, D), lambda i: (0, 0))
    gain = pl.BlockSpec((1, D), lambda i: (0, 0))
    return pl.pallas_call(
        body, name="out_a_norms", grid=(S // TM_TOKENS,), in_specs=[rows, whole, rows] + [gain] * n,
        out_specs=[rows] * (n + 1),
        out_shape=[jax.ShapeDtypeStruct((S, D), F32)] + [jax.ShapeDtypeStruct((S, D), BF16)] * n,
        compiler_params=_params(1))(y, w_out, residual, *gains)


def _out_loss(y, w_out, residual, target):
    def body(y_ref, w_ref, r_ref, t_ref, d_ref, db_ref, l_ref):
        diff = _dot(y_ref[...], w_ref[...]) + r_ref[...] - t_ref[...]
        d = diff * (1.0 / D)
        d_ref[...] = d
        db_ref[...] = d.astype(BF16)

        @pl.when(pl.program_id(0) == 0)
        def _():
            l_ref[...] = jnp.zeros_like(l_ref)

        l_ref[...] += jnp.sum(diff * diff, axis=0, keepdims=True)

    rows = pl.BlockSpec((TM_TOKENS, D), lambda i: (i, 0))
    whole = pl.BlockSpec((D, D), lambda i: (0, 0))
    return pl.pallas_call(
        body, name="out_b_loss", grid=(S // TM_TOKENS,), in_specs=[rows, whole, rows, rows],
        out_specs=[rows, rows, pl.BlockSpec((1, D), lambda i: (0, 0))],
        out_shape=[jax.ShapeDtypeStruct((S, D), F32), jax.ShapeDtypeStruct((S, D), BF16),
                   jax.ShapeDtypeStruct((1, D), F32)],
        compiler_params=_params(1))(y, w_out, residual, target)


def _lane():
    return lax.broadcasted_iota(jnp.int32, (1, LANES), 1)


def _head_mask(hh):
    return (_lane() < HD) if hh == 0 else (_lane() >= HD)


def _qkv_specs():
    return (pl.BlockSpec((None, ATT, LANES), lambda p, i: (0, i, p)),
            pl.BlockSpec((None, S, LANES), lambda p, i: (1, 0, p)),
            pl.BlockSpec((None, S, LANES), lambda p, i: (2, 0, p)))


def _fox_fwd(qkv, ct, gate, riding):
    nq, npair = S // ATT, NH // 2
    ni, no = len(riding.ins), len(riding.outs)

    def body(q_ref, k_ref, v_ref, c_ref, gate_ref, *rest):
        o_ref, lse_ref, y_ref = rest[ni:ni + 3]
        pair, i = pl.program_id(0), pl.program_id(1)
        at_end = riding.hooks(rest[:ni], rest[ni + 3:ni + 3 + no], *rest[ni + 3 + no:],
                              first=(pair == 0) & (i == 0), middle=(pair == npair // 2) & (i == 0),
                              last=(pair == npair - 1) & (i == nq - 1))
        q2 = q_ref[...]
        qms = [jnp.where(_head_mask(hh), q2, jnp.zeros_like(q2)) for hh in (0, 1)]

        def probs(off, width, m, hh, diag):
            s = _dot(qms[hh], k_ref[pl.ds(off, width), :], NT) - c_ref[hh:hh + 1, pl.ds(off, width)]
            if diag:
                row = i * ATT + lax.broadcasted_iota(jnp.int32, (ATT, width), 0)
                col = off + lax.broadcasted_iota(jnp.int32, (ATT, width), 1)
                s = jnp.where(col <= row, s, NEG)
            m_new = jnp.maximum(m, jnp.max(s, axis=1, keepdims=True))
            p = jnp.exp(s - m_new)
            p_hi = p.astype(BF16)
            return m_new, jnp.exp(m - m_new), p_hi, (p - p_hi.astype(F32)).astype(BF16)

        def weighted(off, width, p_hi, p_lo, hh):
            vj = v_ref[pl.ds(off, width), :]
            v1 = jnp.where(_head_mask(hh), vj, jnp.ones_like(vj))
            return _dot(p_hi, v1) + _dot(p_lo, v1)

        def step(off, width, carry, diag):
            off = pl.multiple_of(off, ATT)
            out = []
            for hh in (0, 1):
                m, acc = carry[hh]
                m, alpha, p_hi, p_lo = probs(off, width, m, hh, diag)
                out.append((m, alpha * acc + weighted(off, width, p_hi, p_lo, hh)))
            return tuple(out)

        one = (jnp.full((ATT, 1), NEG, F32), jnp.zeros((ATT, LANES), F32))
        carry = lax.fori_loop(0, i // 2, lambda j, cr: step(j * (2 * ATT), 2 * ATT, cr, False), (one, one))
        carry = lax.cond(i % 2 == 1, lambda cr: step((i - 1) * ATT, 2 * ATT, cr, True),
                         lambda cr: step(i * ATT, ATT, cr, True), carry)
        res = []
        for hh in (0, 1):
            m, acc = carry[hh]
            l = jnp.max(jnp.where(_head_mask(1 - hh), acc, 0.0), axis=1, keepdims=True)
            res.append((acc / l, m + jnp.log(l)))
        first = _head_mask(0)
        o = jnp.where(first, res[0][0], res[1][0])
        o_ref[...] = o
        lse_ref[...] = jnp.where(first, res[0][1], res[1][1])
        g = gate_ref[...]
        y_ref[...] = (o * (g * _sigmoid(g))).astype(BF16)
        at_end()

    blk = pl.BlockSpec((ATT, LANES), lambda p, i: (i, p))
    res = pl.pallas_call(
        body, name="fox_fwd", grid=(npair, nq),
        in_specs=[*_qkv_specs(), pl.BlockSpec((None, 2, S), lambda p, i: (p, 0, 0)), blk] + riding.in_specs,
        out_specs=[blk, blk, blk] + riding.out_specs,
        out_shape=[jax.ShapeDtypeStruct((S, D), F32)] * 2 + [jax.ShapeDtypeStruct((S, D), BF16)] + riding.out_shape,
        scratch_shapes=riding.scratch,
        compiler_params=_params(2))(qkv, qkv, qkv, ct, gate, *riding.ins)
    return res[0], res[1], res[2], res[3:]


def _gate_grads(dy, o, g):
    sg = _sigmoid(g)
    return dy * (g * sg), dy * o * (sg * (1.0 + g * (1.0 - sg)))


def _fox_bwd(qkv, ct, o, lse, dy, gate, riding):
    nq, npair = S // ATT, NH // 2
    ni, no = len(riding.ins), len(riding.outs)

    def body(q_ref, k_ref, v_ref, c_ref, o_ref, lse_ref, dy_ref, gate_ref, *rest):
        dq_ref, dk_ref, dvb_ref, dc_ref, dgate_ref = rest[ni:ni + 5]
        dv_ref = rest[ni + 5 + no]
        pair, i = pl.program_id(0), pl.program_id(1)
        at_end = riding.hooks(rest[:ni], rest[ni + 5:ni + 5 + no], *rest[ni + 6 + no:],
                              first=(pair == 0) & (i == 0), middle=(pair == npair // 2) & (i == 0),
                              last=(pair == npair - 1) & (i == nq - 1))

        @pl.when(i == 0)
        def _():
            dk_ref[...] = jnp.zeros_like(dk_ref)
            dv_ref[...] = jnp.zeros_like(dv_ref)
            dc_ref[...] = jnp.zeros_like(dc_ref)

        q2, lse2 = q_ref[...], lse_ref[...]
        do2, dgate = _gate_grads(dy_ref[...], o_ref[...], gate_ref[...])
        dgate_ref[...] = dgate.astype(BF16)
        do2b = do2.astype(BF16)
        prod = do2b.astype(F32) * o_ref[...]
        heads = []
        for hh in (0, 1):
            hm = _head_mask(hh)
            heads.append((jnp.where(hm, q2, jnp.zeros_like(q2)), jnp.where(hm, do2b, jnp.zeros_like(do2b)),
                          jnp.sum(jnp.where(hm, prod, 0.0), axis=1, keepdims=True),
                          jnp.max(jnp.where(hm, lse2, NEG), axis=1, keepdims=True)))

        def step(off, width, dqs, diag):
            off = pl.multiple_of(off, ATT)
            kj, vj = k_ref[pl.ds(off, width), :], v_ref[pl.ds(off, width), :]
            dk, dv, out = None, None, []
            for hh in (0, 1):
                qm, dom, delta, lse_h = heads[hh]
                s = _dot(qm, kj, NT) - c_ref[hh:hh + 1, pl.ds(off, width)]
                p = jnp.exp(s - lse_h)
                if diag:
                    row = i * ATT + lax.broadcasted_iota(jnp.int32, (ATT, width), 0)
                    col = off + lax.broadcasted_iota(jnp.int32, (ATT, width), 1)
                    p = jnp.where(col <= row, p, 0.0)
                ds = p * (_dot(dom, vj, NT) - delta)
                dc_ref[hh:hh + 1, pl.ds(off, width)] += -jnp.sum(ds, axis=0, keepdims=True)
                dsb = ds.astype(BF16)
                dk_h, dv_h = _dot(dsb, qm, TN), _dot(p.astype(BF16), dom, TN)
                dk, dv = (dk_h, dv_h) if dk is None else (dk + dk_h, dv + dv_h)
                out.append(dqs[hh] + _dot(dsb, kj))
            dk_ref[pl.ds(off, width), :] += dk
            dv_ref[pl.ds(off, width), :] += dv
            return tuple(out)

        zero = jnp.zeros((ATT, LANES), F32)
        dqs = lax.fori_loop(0, i // 2, lambda j, acc: step(j * (2 * ATT), 2 * ATT, acc, False), (zero, zero))
        dqs = lax.cond(i % 2 == 1, lambda acc: step((i - 1) * ATT, 2 * ATT, acc, True),
                       lambda acc: step(i * ATT, ATT, acc, True), dqs)
        dq_ref[...] = jnp.where(_head_mask(0), dqs[0], dqs[1]) * SCALE

        @pl.when(i == nq - 1)
        def _():
            dvb_ref[...] = dv_ref[...].astype(BF16)

        at_end()

    blk = pl.BlockSpec((ATT, LANES), lambda p, i: (i, p))
    full = pl.BlockSpec((S, LANES), lambda p, i: (0, p))
    cspec = pl.BlockSpec((None, 2, S), lambda p, i: (p, 0, 0))
    res = pl.pallas_call(
        body, name="fox_bwd", grid=(npair, nq),
        in_specs=[*_qkv_specs(), cspec, blk, blk, blk, blk] + riding.in_specs,
        out_specs=[blk, full, full, cspec, blk] + riding.out_specs,
        out_shape=[jax.ShapeDtypeStruct((S, D), F32)] * 2 + [jax.ShapeDtypeStruct((S, D), BF16),
                                                              jax.ShapeDtypeStruct((npair, 2, S), F32),
                                                              jax.ShapeDtypeStruct((S, D), BF16)]
                  + riding.out_shape,
        scratch_shapes=[pltpu.VMEM((S, LANES), F32)] + riding.scratch,
        compiler_params=_params(2))(qkv, qkv, qkv, ct, o, lse, dy, gate, *riding.ins)
    return res[0], res[1], res[2], res[3], res[4], res[5:]


def _both_heads(x):
    return jnp.concatenate([jnp.where(_head_mask(hh), x, jnp.zeros_like(x)) for hh in (0, 1)], axis=0)


def _per_head(col0, col1):
    return jnp.concatenate([jnp.broadcast_to(col0, (WINDOW, 1)), jnp.broadcast_to(col1, (WINDOW, 1))], axis=0)


def _unstack(x2):
    return jnp.where(_head_mask(0), x2[:WINDOW], x2[WINDOW:])


def _swa_valid(i, start):
    r = lax.broadcasted_iota(jnp.int32, (2 * WINDOW, 2 * WINDOW), 0)
    qabs = i * WINDOW + jnp.where(r >= WINDOW, r - WINDOW, r)
    kabs = start + lax.broadcasted_iota(jnp.int32, (2 * WINDOW, 2 * WINDOW), 1)
    return (kabs <= qabs) & (qabs - kabs < WINDOW)


def _swa_fwd(q, kdup, vdup, sinks_t, proj, gate_col):
    def body(q_ref, k_ref, v_ref, sk_ref, gate_ref, o_ref, lse_ref, y_ref):
        skv = sk_ref[...]
        first = _head_mask(0)
        for sb in range(SWQ):
            i = pl.program_id(1) * SWQ + sb
            rows = slice(sb * WINDOW, (sb + 1) * WINDOW)
            start = pl.multiple_of(jnp.maximum(i - 1, 0) * WINDOW, WINDOW)
            kk, vv = k_ref[pl.ds(start, 2 * WINDOW), :], v_ref[pl.ds(start, 2 * WINDOW), :]
            q2 = q_ref[rows, :]
            valid = _swa_valid(i, start)[:WINDOW]
            res = []
            for hh in (0, 1):
                hm = _head_mask(hh)
                sink = jnp.max(jnp.where(hm, skv, NEG), axis=1, keepdims=True)
                s = jnp.where(valid, _dot(jnp.where(hm, q2, jnp.zeros_like(q2)), kk, NT), NEG)
                m = jnp.maximum(jnp.max(s, axis=1, keepdims=True), sink)
                p = jnp.exp(s - m)
                l = jnp.sum(p, axis=1, keepdims=True) + jnp.exp(sink - m)
                res.append((_dot(p.astype(BF16), vv) / l, m + jnp.log(l)))
            o = jnp.where(first, res[0][0], res[1][0])
            o_ref[rows, :] = o
            lse_ref[rows, :] = jnp.where(first, res[0][1], res[1][1])
            g = gate_ref[rows, :]
            y_ref[rows, :] = (o * (g * _sigmoid(g))).astype(BF16)

    blk = pl.BlockSpec((SWQ * WINDOW, LANES), lambda p, i: (i, p))
    gate = pl.BlockSpec((SWQ * WINDOW, LANES), lambda p, i: (i, gate_col + p))
    full = pl.BlockSpec((S, LANES), lambda p, i: (0, p // 2))
    return pl.pallas_call(
        body, name="swa_fwd", grid=(NH // 2, S // (SWQ * WINDOW)),
        in_specs=[blk, full, full, pl.BlockSpec((1, LANES), lambda p, i: (0, p)), gate],
        out_specs=[blk, blk, blk],
        out_shape=[jax.ShapeDtypeStruct((S, D), F32)] * 2 + [jax.ShapeDtypeStruct((S, D), BF16)],
        compiler_params=_params(2))(q, kdup, vdup, sinks_t, proj)


def _swa_bwd(q, kdup, vdup, sinks_t, o, lse, dy, proj, gate_col):
    def body(q_ref, k_ref, v_ref, sk_ref, o_ref, lse_ref, dy_ref, gate_ref, dq_ref, dk_ref, dv_ref, dsk_ref,
             dgate_ref):
        @pl.when(pl.program_id(1) == 0)
        def _():
            dk_ref[...] = jnp.zeros_like(dk_ref)
            dv_ref[...] = jnp.zeros_like(dv_ref)
            dsk_ref[...] = jnp.zeros_like(dsk_ref)

        skv = sk_ref[...]
        first = _head_mask(0)
        sink = _per_head(*[jnp.max(jnp.where(_head_mask(hh), skv, NEG), axis=1, keepdims=True) for hh in (0, 1)])
        for sb in range(SWQ):
            i = pl.program_id(1) * SWQ + sb
            rows = slice(sb * WINDOW, (sb + 1) * WINDOW)
            start = pl.multiple_of(jnp.maximum(i - 1, 0) * WINDOW, WINDOW)
            kk, vv = k_ref[pl.ds(start, 2 * WINDOW), :], v_ref[pl.ds(start, 2 * WINDOW), :]
            do2, dgate = _gate_grads(dy_ref[rows, :], o_ref[rows, :], gate_ref[rows, :])
            dgate_ref[rows, :] = dgate.astype(BF16)
            do2b = do2.astype(BF16)
            prod, lse2 = do2b.astype(F32) * o_ref[rows, :], lse_ref[rows, :]
            qs, dos = _both_heads(q_ref[rows, :]), _both_heads(do2b)
            delta = jnp.concatenate([jnp.sum(jnp.where(_head_mask(hh), prod, 0.0), axis=1, keepdims=True)
                                     for hh in (0, 1)], axis=0)
            lse_h = jnp.concatenate([jnp.max(jnp.where(_head_mask(hh), lse2, NEG), axis=1, keepdims=True)
                                     for hh in (0, 1)], axis=0)
            p = jnp.where(_swa_valid(i, start), jnp.exp(_dot(qs, kk, NT) - lse_h), 0.0)
            dsb = (p * (_dot(dos, vv, NT) - delta)).astype(BF16)
            dk_ref[pl.ds(start, 2 * WINDOW), :] += _dot(dsb, qs, TN)
            dv_ref[pl.ds(start, 2 * WINDOW), :] += _dot(p.astype(BF16), dos, TN)
            dq_ref[rows, :] = _unstack(_dot(dsb, kk)) * SCALE
            t = jnp.exp(sink - lse_h) * delta
            dsk_ref[...] += -jnp.where(first, jnp.sum(t[:WINDOW], axis=0, keepdims=True),
                                       jnp.sum(t[WINDOW:], axis=0, keepdims=True))

    blk = pl.BlockSpec((SWQ * WINDOW, LANES), lambda p, i: (i, p))
    full = pl.BlockSpec((S, LANES), lambda p, i: (0, p // 2))
    acc = pl.BlockSpec((S, LANES), lambda p, i: (0, p))
    sk = pl.BlockSpec((1, LANES), lambda p, i: (0, p))
    gate = pl.BlockSpec((SWQ * WINDOW, LANES), lambda p, i: (i, gate_col + p))
    return pl.pallas_call(
        body, name="swa_bwd", grid=(NH // 2, S // (SWQ * WINDOW)),
        in_specs=[blk, full, full, sk, blk, blk, blk, gate],
        out_specs=[blk, acc, acc, sk, blk],
        out_shape=[jax.ShapeDtypeStruct((S, D), F32)] * 3 + [jax.ShapeDtypeStruct((1, D), F32),
                                                              jax.ShapeDtypeStruct((S, D), BF16)],
        compiler_params=_params(2))(q, kdup, vdup, sinks_t, o, lse, dy, proj)


def _adamw_math(w, g, m, v):
    m = ADAM_B1 * m + (1.0 - ADAM_B1) * g
    v = ADAM_B2 * v + (1.0 - ADAM_B2) * jnp.square(g)
    m_hat = m / (1.0 - ADAM_B1 ** ADAM_STEP)
    v_hat = v / (1.0 - ADAM_B2 ** ADAM_STEP)
    delta = -ADAM_LR * (m_hat / (jnp.sqrt(v_hat) + ADAM_EPS) + ADAM_WD * w)
    return delta, m, v


def _adamw_small(ws, gs, ms, vs):
    k = len(ws)

    def body(*refs):
        for p in range(k):
            w_ref, g_ref, m_ref, v_ref = (refs[q * k + p] for q in range(4))
            d, mo, vo = _adamw_math(w_ref[...], g_ref[...], m_ref[...], v_ref[...])
            refs[4 * k + p][...], refs[5 * k + p][...], refs[6 * k + p][...] = d, mo, vo

    res = pl.pallas_call(
        body, name="adamw_small",
        out_shape=[jax.ShapeDtypeStruct(t.shape, F32) for t in ws] * 3)(*ws, *gs, *ms, *vs)
    return res[:k], res[k:2 * k], res[2 * k:]


SUM_TILES = (512, 256, 128)


FLAT_BLOCK = 257 * 1024


def _tiles(shape, axis, lead=0, halves=False):
    if len(shape) == 1:
        count = shape[0] // FLAT_BLOCK
        return (FLAT_BLOCK,), count, lambda pos, *lead_idx: (sum(k * count for k in lead_idx) + pos,)
    r, c = shape
    tile = next(t for t in SUM_TILES if (shape[axis] // (2 if halves else 1)) % t == 0)
    blk = (tile, c) if axis == 0 else (r, tile)
    count = shape[axis] // tile

    def index(pos, *lead_idx):
        return tuple(lead_idx) + ((pos, 0) if axis == 0 else (0, pos))

    return (None,) * lead + blk, count, index


def _adamw_halves(w, g_mine, g_theirs, m, v, axis, name):
    blk, count, index = _tiles(w.shape, axis, halves=True)
    per_half = count // 2

    def body(w_ref, a_ref, b_ref, m_ref, v_ref, g_ref, d_ref, mo_ref, vo_ref):
        is_mine = pl.program_id(0) // per_half == lax.axis_index("c")
        g = jnp.where(is_mine, a_ref[...], b_ref[...])
        g_ref[...] = g
        d_ref[...], mo_ref[...], vo_ref[...] = _adamw_math(w_ref[...], g, m_ref[...], v_ref[...])

    spec = pl.BlockSpec(blk, lambda i: index(i))
    half = pl.BlockSpec(blk, lambda i: index(i % per_half))
    return pl.pallas_call(
        body, name=name, grid=(count,), in_specs=[spec, half, half, spec, spec], out_specs=[spec] * 4,
        out_shape=[jax.ShapeDtypeStruct(w.shape, F32)] * 4, compiler_params=_params(1))(w, g_mine, g_theirs, m, v)


def _chip_sum(blocks, from_sibling, axis, name):
    flat = blocks.ndim == 1
    blk, count, index = _tiles((from_sibling.shape[0] // NCHIP,) if flat else from_sibling.shape[1:], axis, lead=1)

    def body(lo_ref, hi_ref, p_ref, o32, o16):
        mine = jnp.where(lax.axis_index("c") == 0, lo_ref[...], hi_ref[...])
        acc = mine + p_ref[...]
        o32[...] = acc
        o16[...] = acc.astype(BF16)

    half = pl.BlockSpec(blk, lambda k, i: index(i, k))
    if flat:
        lo = pl.BlockSpec(blk, lambda k, i: (2 * count * k + i,))
        hi = pl.BlockSpec(blk, lambda k, i: (2 * count * k + count + i,))
    else:
        lo, hi = half, pl.BlockSpec(blk, lambda k, i: index(i + count, k))
    return pl.pallas_call(
        body, name=name, grid=(NCHIP, count), in_specs=[lo, hi, half], out_specs=[half, half],
        out_shape=[jax.ShapeDtypeStruct(from_sibling.shape, F32), jax.ShapeDtypeStruct(from_sibling.shape, BF16)],
        compiler_params=_params(2))(blocks, blocks, from_sibling)


def _mesh_sum(chip_sums, parts, axis, name):
    flat = chip_sums.ndim == 1
    one = (chip_sums.shape[0] // NCHIP,) if flat else chip_sums.shape[1:]
    blk, count, index = _tiles(one, axis)
    n = NCHIP - 1

    def body(chip_ref, a_ref, *refs):
        acc = a_ref[...]
        for k in range(n):
            acc = acc + refs[k][...].astype(F32)
        refs[n][...] = acc

    spec = pl.BlockSpec(blk, lambda i, chip: index(i))
    if flat:
        mine = pl.BlockSpec(blk, lambda i, chip: (chip[0] * count + i,))
        part = [pl.BlockSpec(blk, lambda i, chip, k=k: (k * count + i,)) for k in range(n)]
    else:
        mine = pl.BlockSpec((None,) + blk, lambda i, chip: (chip[0],) + index(i))
        part = [pl.BlockSpec((None,) + blk, lambda i, chip, k=k: (k,) + index(i)) for k in range(n)]
    return pl.pallas_call(
        body, name=name,
        grid_spec=pltpu.PrefetchScalarGridSpec(num_scalar_prefetch=1, grid=(count,), in_specs=[mine] + part,
                                               out_specs=spec),
        out_shape=jax.ShapeDtypeStruct(one, F32),
        compiler_params=_params(1))(_chip(_coords()).astype(jnp.int32).reshape(1), chip_sums, *([parts] * n))


def _sum_stack(parts, name):
    n = parts.shape[0]

    def body(p_ref, o_ref):
        acc = p_ref[0]
        for k in range(1, n):
            acc = acc + p_ref[k]
        o_ref[...] = acc

    return pl.pallas_call(body, name=name, out_shape=jax.ShapeDtypeStruct(parts.shape[1:], F32))(parts)


def _coords():
    return lax.axis_index("x"), lax.axis_index("y"), lax.axis_index("c")


def _chip(who):
    return 2 * who[0] + who[1]


def _flip(who, mask):
    return tuple((1 - v) if b else v for v, b in zip(who, mask))


def _transfer(transfers, t, I, O, ssem, rsem, receiving):
    tr, me = transfers[t], _coords()
    peer = _flip(me, tr["mask"])
    return pltpu.make_async_remote_copy(
        src_ref=tr["src"](I, O, me), dst_ref=tr["dst"](I, O, peer if receiving else me),
        send_sem=ssem.at[t], recv_sem=rsem.at[t], device_id=peer, device_id_type=MESH)


def _start_transfers(transfers, I, O, ssem, rsem, onward):
    arrived = set()
    for t, tr in enumerate(transfers):
        after = tr.get("after")
        if (after is not None) != onward:
            continue
        if after is not None and after not in arrived:
            _transfer(transfers, after, I, O, ssem, rsem, True).wait_recv()
            arrived.add(after)
        _transfer(transfers, t, I, O, ssem, rsem, False).start()


def _finish_transfers(transfers, I, O, ssem, rsem):
    passed_on = {tr["after"] for tr in transfers if tr.get("after") is not None}
    for t in range(len(transfers)):
        if t not in passed_on:
            _transfer(transfers, t, I, O, ssem, rsem, True).wait_recv()
    for t in range(len(transfers)):
        _transfer(transfers, t, I, O, ssem, rsem, False).wait_send()


def _own_copies(own, I, O, stage, lsem, leg):
    for n, (src, dst) in enumerate(own):
        me = _coords()
        bring =pltpu.make_async_copy(src(I, O, me), stage[n], lsem.at[2 * n])
        put = pltpu.make_async_copy(stage[n], dst(I, O, me), lsem.at[2 * n + 1])
        if leg == 0:
            bring.start()
        elif leg == 1:
            bring.wait()
            put.start()
        else:
            put.wait()


def _own_scratch(own, ins):
    return [pltpu.VMEM(ins[n].shape, ins[n].dtype) for n in range(len(own))], pltpu.SemaphoreType.DMA((max(2 * len(own), 1),))


def _exchange(name, ins, outs, transfers, own=()):
    ni, no = len(ins), len(outs)
    nt = len(transfers)
    stages, stage_sems = _own_scratch(own, ins)

    def body(*refs):
        I, O = refs[:ni], refs[ni:ni + no]
        ssem, rsem, lsem = refs[ni + no:ni + no + 3]
        stage = refs[ni + no + 3:]
        _own_copies(own, I, O, stage, lsem, 0)
        _start_transfers(transfers, I, O, ssem, rsem, False)
        _own_copies(own, I, O, stage, lsem, 1)
        _start_transfers(transfers, I, O, ssem, rsem, True)
        _finish_transfers(transfers, I, O, ssem, rsem)
        _own_copies(own, I, O, stage, lsem, 2)

    hbm = pl.BlockSpec(memory_space=pltpu.HBM)
    return pl.pallas_call(
        body, name=name, in_specs=[hbm] * ni, out_specs=[hbm] * no,
        out_shape=[jax.ShapeDtypeStruct(s, d) for s, d in outs],
        scratch_shapes=[pltpu.SemaphoreType.DMA((nt,)), pltpu.SemaphoreType.DMA((nt,)), stage_sems] + stages,
        compiler_params=pltpu.CompilerParams(has_side_effects=True, vmem_limit_bytes=VMEM_LIMIT))(*ins)


CHIP_MASKS = [(0, 1, 0), (1, 0, 0), (1, 1, 0)]
SIBLING = (0, 0, 1)


def _half(shape2d, axis, which):
    n = shape2d[axis] // 2
    cut = pl.ds(pl.multiple_of(which * n, n), n)
    return (cut, slice(None)) if axis == 0 else (slice(None), cut)


class _Riding:
    def __init__(self, transfers, ins, outs, own=()):
        self.transfers, self.ins, self.outs, self.own = transfers, list(ins), list(outs), list(own)
        hbm = pl.BlockSpec(memory_space=pltpu.HBM)
        self.in_specs, self.out_specs = [hbm] * len(self.ins), [hbm] * len(self.outs)
        self.out_shape = [jax.ShapeDtypeStruct(s, d) for s, d in self.outs]
        stages, stage_sems = _own_scratch(self.own, self.ins)
        self.scratch = [pltpu.SemaphoreType.DMA((max(len(transfers), 1),))] * 2 + [stage_sems] + stages

    def alone(self, name):
        return _exchange(name, self.ins, self.outs, self.transfers, self.own)

    def hooks(self, I, O, ssem, rsem, lsem, *stage, first, middle, last):
        tr, own = self.transfers, self.own

        @pl.when(first)
        def _():
            _own_copies(own, I, O, stage, lsem, 0)
            _start_transfers(tr, I, O, ssem, rsem, False)

        if own or any(t.get("after") is not None for t in tr):
            @pl.when(middle)
            def _():
                _own_copies(own, I, O, stage, lsem, 1)
                _start_transfers(tr, I, O, ssem, rsem, True)

        def at_end():
            @pl.when(last)
            def _():
                _finish_transfers(tr, I, O, ssem, rsem)
                _own_copies(own, I, O, stage, lsem, 2)

        return at_end


def _stretch(n, pos):
    return (pl.ds(pos * n if isinstance(pos, int) else pl.multiple_of(pos * n, n), n),)


def _gather_plan(shards, axes):
    def half(a, who):
        if shards[a].ndim == 1:
            return _stretch(shards[a].shape[0] // 2, who[2])
        return _half(shards[a].shape, axes[a], who[2])

    def landed(a, chip, who):
        if shards[a].ndim == 1:
            return _stretch(shards[a].shape[0] // 2, 2 * chip + who[2])
        return (chip,) + half(a, who)

    over_ici, onward = [], []
    for a in range(len(shards)):
        for mask in CHIP_MASKS:
            over_ici.append(dict(
                mask=mask,
                src=lambda I, O, me, a=a: I[a].at[half(a, me)],
                dst=lambda I, O, who, a=a: O[a].at[landed(a, _chip(who), who)]))
            onward.append(dict(
                mask=SIBLING, after=len(over_ici) - 1,
                src=lambda I, O, me, a=a, mask=mask: O[a].at[landed(a, _chip(_flip(me, mask)), me)],
                dst=lambda I, O, who, a=a, mask=mask: O[a].at[landed(a, _chip(_flip(who, mask)), who)]))
    outs = [((NCHIP * s.shape[0],) if s.ndim == 1 else (NCHIP,) + s.shape, s.dtype) for s in shards]

    def whole(a, chip):
        return _stretch(shards[a].shape[0], chip) if shards[a].ndim == 1 else (chip,)

    own = [(lambda I, O, me, a=a: I[a], lambda I, O, me, a=a: O[a].at[whole(a, _chip(me))])
           for a in range(len(shards))]
    return over_ici + onward, outs, own


def _gather_shards(shards, axes):
    transfers, outs, own = _gather_plan(shards, axes)
    return _exchange("gather_weights", shards, outs, transfers, own)


def _halves_plan(blocks, axes):
    def cut(a, which):
        return (slice(None),) + _half(blocks[a].shape[1:], axes[a], which)

    transfers, outs = [], []
    for a, (b, ax) in enumerate(zip(blocks, axes)):
        if b.ndim == 1:
            h = b.shape[0] // NCHIP // 2
            for k in range(NCHIP):
                transfers.append(dict(mask=SIBLING,
                                      src=lambda I, O, me, a=a, k=k, h=h: I[a].at[_stretch(h, 2 * k + 1 - me[2])],
                                      dst=lambda I, O, who, a=a, k=k, h=h: O[a].at[_stretch(h, k)]))
            outs.append(((NCHIP * h,), b.dtype))
        else:
            transfers.append(dict(mask=SIBLING, src=lambda I, O, me, a=a: I[a].at[cut(a, 1 - me[2])],
                                  dst=lambda I, O, who, a=a: O[a]))
            shape = list(b.shape)
            shape[ax + 1] //= 2
            outs.append((tuple(shape), b.dtype))
    return transfers, outs


def _scatter_plan(tb):
    def slot(a, k):
        return (k,) if tb[a].ndim == 3 else _stretch(tb[a].shape[0] // NCHIP, k)

    transfers = []
    for a in range(len(tb)):
        for n, mask in enumerate(CHIP_MASKS):
            transfers.append(dict(
                mask=mask,
                src=lambda I, O, me, a=a, mask=mask: I[a].at[slot(a, _chip(_flip(me, mask)))],
                dst=lambda I, O, who, a=a, n=n: O[a].at[slot(a, n)]))
    outs = [((3,) + t.shape[1:] if t.ndim == 3 else (3 * (t.shape[0] // NCHIP),), t.dtype) for t in tb]
    return transfers, outs


def _last_exchange(vec, halves):
    def slot(who):
        return 4 * who[0] + 2 * who[1] + who[2]

    masks = [(m >> 2 & 1, m >> 1 & 1, m & 1) for m in range(1, 8)]
    transfers = [dict(mask=mask, src=lambda I, O, me: I[0], dst=lambda I, O, who: O[0].at[slot(who)])
                 for mask in masks]
    transfers += [dict(mask=SIBLING, src=lambda I, O, me, a=a: I[a], dst=lambda I, O, who, a=a: O[a])
                  for a in range(1, 1 + len(halves))]
    own = [(lambda I, O, me: I[0], lambda I, O, me: O[0].at[slot(me)])]
    outs = [((8,) + vec.shape, vec.dtype)] + [(t.shape, t.dtype) for t in halves]
    res = _exchange("last_exchange", [vec] + list(halves), outs, transfers, own)
    return res[0], res[1:]


def _rope_tables(positions):
    half = ROT // 2
    inv_freq = jnp.power(jnp.float32(THETA), -jnp.arange(0, ROT, 2, dtype=F32) / ROT)
    ang = positions.astype(F32)[:, None] * inv_freq[None, :]
    cos, sin = jnp.cos(ang), jnp.sin(ang)
    one, zero, z8 = jnp.ones((S, HD - ROT), F32), jnp.zeros((S, HD - ROT), F32), jnp.zeros((S, half), F32)
    c = jnp.concatenate([cos, cos, one], axis=1)
    a = jnp.concatenate([-sin, z8, zero], axis=1)
    b = jnp.concatenate([z8, sin, zero], axis=1)
    return tuple(jnp.tile(t, (1, 2)) for t in (c, a, b))


def _tile_heads(g, w):
    return jnp.tile(g.reshape(1, HD), (1, w // HD))


def _fold_heads(dg):
    return dg.reshape(-1, HD).sum(axis=0)


def _pad_lanes(a):
    return jnp.pad(a, ((0, 0), (0, LANES - a.shape[1])))


def _local_step(x, target, positions, wt, fetch, late_weights, begin_reduce):
    rope = _rope_tables(positions)
    w1t = wt["w_in_a_t"]
    f_row = 3 * D // LANES
    wg_t = w1t[3 * D + NH:]
    in_b_block = lambda c: pl.BlockSpec((None, TN_WIDE, TN_), lambda j, i: (c, j, 0))
    b_pad = _pad_lanes(wt["b_forget"].reshape(1, NH))
    qg_a, kg_a = _tile_heads(wt["qnorm_a_g"], D), _tile_heads(wt["knorm_a_g"], D)
    qg_b, kg_b = _tile_heads(wt["qnorm_b_g"], D), _tile_heads(wt["knorm_b_g"], KVW)
    norm_a, kv_g, norm_b = wt["norm_a_g"].reshape(1, D), wt["kv_norm_g"].reshape(1, D), wt["norm_b_g"].reshape(1, D)
    sinks_t = jnp.repeat(wt["sinks"].reshape(1, NH), HD, axis=1)

    (u_a,) = _rmsnorm_fwd(x, [norm_a], "norm_a")
    qkv, qkv_a = _proj_a(u_a, w1t, qg_a, kg_a)
    fpad = _mm("proj_f", S, LANES, [(u_a, _a_rows(D), w1t, _b_rows(D, row0=f_row, tn=LANES), NT)], tn=LANES)
    gate_a = _mm("proj_gate_a", S, D, [(u_a, _a_rows(D), wg_t, _b_rows(D, tn=TN_WIDE), NT)], tn=TN_WIDE)
    ct = _forget_cumsum(fpad, b_pad)
    ct2 = ct[:NH].reshape(NH // 2, 2, S)
    o_a, lse_a, y_a, fetched = _fox_fwd(qkv_a, ct2, gate_a, fetch)
    wt = {**wt, **late_weights(fetched)}
    w_in_b = wt["w_in_b"]
    h1, u_kv, u_b = _out_norms(y_a, wt["w_out_a"], x, [kv_g, norm_b])
    kv = _mm("proj_kv", S, 2 * KVW, [(u_kv, _a_rows(D), wt["w_kv"], _b_cols(D), None)])
    pb = _mm("proj_b", S, 2 * D,
             [(u_b, _a_rows(D), w_in_b, pl.BlockSpec((None, D, TN_), lambda j, i: (j, 0, 0)), None)])
    q_b, kdup, vdup = _b_post(pb, kv, qg_b, kg_b, rope)
    gate_b_col = D // LANES
    o_b, lse_b, y_b = _swa_fwd(q_b, kdup, vdup, sinks_t, pb, gate_b_col)
    d_out, d_out_b, sq = _out_loss(y_b, wt["w_out_b"], h1, target)

    g = {}
    g["w_out_b"] = _mm("dw_out_b", D, D, [(y_b, _a_cols(S), d_out_b, _b_cols(S, tn=TN_WIDE), TN)], tn=TN_WIDE)
    d_y_b = _mm("dy_b", S, D, [(d_out_b, _a_rows(D), wt["w_out_b"], _b_rows(D, tn=TN_WIDE), NT)], tn=TN_WIDE)
    dq_b, dkdup, dvdup, dsk, d_gate_b = _swa_bwd(q_b, kdup, vdup, sinks_t, o_b, lse_b, d_y_b, pb, gate_b_col)
    g["sinks"] = dsk[0, ::HD]
    d_qb_raw, dg = _headnorm_bwd(pb, 0, qg_b, dq_b, rope, "qnorm_b_bwd")
    g["qnorm_b_g"] = _fold_heads(dg)
    d_pb = [d_qb_raw, d_qb_raw, d_gate_b, d_gate_b]
    g["w_in_b"] = jnp.concatenate([
        _mm("dw_in_b_q", D, D, [(u_b, _a_cols(S), d_qb_raw, _b_cols(S), TN)], stacked=True),
        _mm("dw_in_b_gate", D, D, [(u_b, _a_cols(S), d_gate_b, _b_cols(S), TN)], stacked=True)], axis=0)
    d_u_b = _mm("du_b", S, D, [(d_pb[c], _a_rows(TN_, col=c % 2), w_in_b, in_b_block(c), NT) for c in range(NCHIP)],
                tn=TN_WIDE)
    d_kv, dg = _kv_bwd(dkdup, dvdup, kv, kg_b, rope)
    g["knorm_b_g"] = _fold_heads(dg)
    g["w_kv"] = _mm("dw_kv", D, 2 * KVW, [(u_kv, _a_cols(S), d_kv, _b_cols(S), TN)])
    d_u_kv = _mm("du_kv", S, D, [(d_kv, _a_rows(2 * KVW), wt["w_kv"], _b_rows(2 * KVW, tn=TN_WIDE), NT)], tn=TN_WIDE)
    d_h1, d_h1_b, g["kv_norm_g"], g["norm_b_g"] = _rmsnorm_bwd(h1, [kv_g, norm_b], [d_u_kv, d_u_b], d_out, "norm_b_bwd")
    g["w_out_a"] = _mm("dw_out_a", D, D, [(y_a, _a_cols(S), d_h1_b, _b_cols(S, tn=TN_WIDE), TN)], tn=TN_WIDE)
    late = {n: g[n] for n in LATE}
    d_y_a, halves = _mm("dy_a", S, D, [(d_h1_b, _a_rows(D), wt["w_out_a"], _b_rows(D, tn=TN_WIDE), NT)],
                        tn=TN_WIDE, riding=begin_reduce(late))
    riding, so_far = begin_reduce(late, halves)
    dq_a, dk_a, dv_a, dct, d_gate_a, arrived = _fox_bwd(qkv_a, ct2, o_a, lse_a, d_y_a, gate_a, riding)
    dct_pad = jnp.pad(dct.reshape(NH, S), ((0, LANES - NH), (0, 0)))
    d_f, db = _forget_bwd(dct_pad, fpad, b_pad)
    g["b_forget"] = db[0, :NH]
    d_q_raw, dg = _headnorm_bwd(qkv, 0, qg_a, dq_a, None, "qnorm_a_bwd")
    g["qnorm_a_g"] = _fold_heads(dg)
    d_k_raw, dg = _headnorm_bwd(qkv, 1, kg_a, dk_a, None, "knorm_a_bwd")
    g["knorm_a_g"] = _fold_heads(dg)
    rows, gw = 4 * D + NH, None
    for n, t, row0 in (("q", d_q_raw, 0), ("k", d_k_raw, D), ("v", dv_a, 2 * D)):
        gw = _mm("dw_in_a_" + n, D, D, [(t, _a_cols(S), u_a, _b_cols(S, tn=TN_WIDE), TN)], tn=TN_WIDE,
                 rows_of=(gw, rows, row0))
    gw = _mm("dw_in_a_f", LANES, D, [(d_f, _a_cols(S, tm=LANES), u_a, _b_cols(S, tn=TN_WIDE), TN)], tm=LANES,
             tn=TN_WIDE, rows_of=(gw, rows, 3 * D))
    g["w_in_a"] = _mm("dw_in_a_gate", D, D, [(d_gate_a, _a_cols(S), u_a, _b_cols(S, tn=TN_WIDE), TN)], tn=TN_WIDE,
                      rows_of=(gw, rows, 3 * D + NH))
    first = {"w_in_a": g["w_in_a"]}
    riding, so_far_first = begin_reduce(first, begin_reduce(first).alone("sibling_halves_w_in_a"))
    d_u_a, arrived_first = _mm("du_a", S, D, [
        (d_q_raw, _a_rows(D), w1t, _b_cols(D, row=0, tn=TN_WIDE), None),
        (d_k_raw, _a_rows(D), w1t, _b_cols(D, row=1, tn=TN_WIDE), None),
        (dv_a, _a_rows(D), w1t, _b_cols(D, row=2, tn=TN_WIDE), None),
        (d_gate_a, _a_rows(D), wg_t, _b_cols(D, tn=TN_WIDE), None),
        (d_f, _a_rows(LANES), w1t, _b_cols(LANES, row=f_row, tn=TN_WIDE), None)], tn=TN_WIDE, riding=riding)
    d_x, _, g["norm_a_g"] = _rmsnorm_bwd(x, [norm_a], [d_u_a], d_h1, "norm_a_bwd")
    return sq, d_x, g, (list(so_far_first) + list(so_far), list(arrived_first) + list(arrived))


BIG = ["w_in_a", "w_out_a", "w_kv", "w_in_b", "w_out_b"]
LATE = BIG[1:]
SPLIT = {"w_in_a": None, "w_out_a": 0, "w_kv": 0, "w_in_b": 0, "w_out_b": 0}
SMALL = ["norm_a_g", "b_forget", "qnorm_a_g", "knorm_a_g", "kv_norm_g", "knorm_b_g", "norm_b_g", "qnorm_b_g", "sinks"]
NAMES = ["norm_a_g", "w_in_a", "b_forget", "qnorm_a_g", "knorm_a_g", "w_out_a", "kv_norm_g", "w_kv", "knorm_b_g",
         "norm_b_g", "w_in_b", "qnorm_b_g", "sinks", "w_out_b"]


def _pack(vals):
    flat = []
    for v in vals:
        v = v.reshape(-1)
        flat.append(jnp.pad(v, (0, -v.shape[0] % LANES)))
    flat = jnp.concatenate(flat)
    flat = jnp.pad(flat, (0, -flat.shape[0] % (8 * LANES)))
    return flat.reshape(-1, LANES)


def _unpack(packed, shapes):
    flat, out, off = packed.reshape(-1), [], 0
    for s in shapes:
        n = int(np.prod(s))
        out.append(flat[off:off + n].reshape(s))
        off += n + (-n % LANES)
    return out


def kernel(x, positions, norm_a_g, w_in_a, b_forget, qnorm_a_g, knorm_a_g, w_out_a, kv_norm_g, w_kv, knorm_b_g, norm_b_g, w_in_b, qnorm_b_g, sinks, w_out_b, loss_target, m_norm_a_g, m_w_in_a, m_b_forget, m_qnorm_a_g, m_knorm_a_g, m_w_out_a, m_kv_norm_g, m_w_kv, m_knorm_b_g, m_norm_b_g, m_w_in_b, m_qnorm_b_g, m_sinks, m_w_out_b, v_norm_a_g, v_w_in_a, v_b_forget, v_qnorm_a_g, v_knorm_a_g, v_w_out_a, v_kv_norm_g, v_w_kv, v_knorm_b_g, v_norm_b_g, v_w_in_b, v_qnorm_b_g, v_sinks, v_w_out_b):
    w = dict(norm_a_g=norm_a_g, w_in_a=w_in_a, b_forget=b_forget, qnorm_a_g=qnorm_a_g, knorm_a_g=knorm_a_g,
             w_out_a=w_out_a, kv_norm_g=kv_norm_g, w_kv=w_kv, knorm_b_g=knorm_b_g, norm_b_g=norm_b_g,
             w_in_b=w_in_b, qnorm_b_g=qnorm_b_g, sinks=sinks, w_out_b=w_out_b)
    m = dict(norm_a_g=m_norm_a_g, w_in_a=m_w_in_a, b_forget=m_b_forget, qnorm_a_g=m_qnorm_a_g, knorm_a_g=m_knorm_a_g,
             w_out_a=m_w_out_a, kv_norm_g=m_kv_norm_g, w_kv=m_w_kv, knorm_b_g=m_knorm_b_g, norm_b_g=m_norm_b_g,
             w_in_b=m_w_in_b, qnorm_b_g=m_qnorm_b_g, sinks=m_sinks, w_out_b=m_w_out_b)
    v = dict(norm_a_g=v_norm_a_g, w_in_a=v_w_in_a, b_forget=v_b_forget, qnorm_a_g=v_qnorm_a_g, knorm_a_g=v_knorm_a_g,
             w_out_a=v_w_out_a, kv_norm_g=v_kv_norm_g, w_kv=v_w_kv, knorm_b_g=v_knorm_b_g, norm_b_g=v_norm_b_g,
             w_in_b=v_w_in_b, qnorm_b_g=v_qnorm_b_g, sinks=v_sinks, w_out_b=v_w_out_b)
    my_chip = 2 * lax.axis_index("x") + lax.axis_index("y")

    def shard2d(t, n):
        if n == "w_in_a":
            return jnp.transpose(t, (2, 0, 1)).reshape(-1)
        return t.reshape(t.shape[-2:])

    def unflat(t, n):
        return jnp.transpose(t.reshape(-1, 1, D), (1, 2, 0)) if n == "w_in_a" else t.reshape(w[n].shape)

    w2d = {n: shard2d(w[n], n) for n in BIG}

    norm_a_rows = jnp.broadcast_to(norm_a_g.reshape(1, D // NCHIP), (2 * SUBLANES, D // NCHIP))
    w1t, norm_rows = _gather_shards([w2d["w_in_a"].astype(BF16), norm_a_rows], [SPLIT["w_in_a"], 0])
    wt = {"w_in_a_t": w1t.reshape(-1, D), "norm_a_g": norm_rows[:, 0, :].reshape(1, D)}
    for n in SMALL[1:]:
        wt[n] = w[n]
    late_shards = [w2d[n].astype(BF16) for n in LATE]
    late_axes = [SPLIT[n] for n in LATE]
    transfers, outs, own = _gather_plan(late_shards, late_axes)
    fetch = _Riding(transfers, late_shards, outs, own)

    def late_weights(fetched):
        return {n: t if n == "w_in_b" else t.reshape(-1, t.shape[2]) for n, t in zip(LATE, fetched)}

    def as_blocks(t):
        if t.ndim == 3:
            return t
        return t.reshape(-1) if t.shape[0] % (SUBLANES * NCHIP) else t.reshape(NCHIP, -1, t.shape[1])

    def begin_reduce(grads, halves=None):
        names = list(grads)
        axes = [SPLIT[n] for n in names]
        blocks = [as_blocks(grads[n]) for n in names]
        if halves is None:
            transfers, outs = _halves_plan(blocks, axes)
            return _Riding(transfers, blocks, outs)
        sums = [_chip_sum(blk, part, ax, "chip_sum_" + n) for n, ax, blk, part in zip(names, axes, blocks, halves)]
        bf16 = [s[1] for s in sums]
        transfers, outs = _scatter_plan(bf16)
        return _Riding(transfers, bf16, outs), [s[0] for s in sums]

    sq, d_x, g, (chip_f32, arrived) = _local_step(x[0], loss_target[0], positions, wt, fetch, late_weights,
                                                  begin_reduce)

    axes = [SPLIT[n] for n in BIG]
    halves = [_mesh_sum(t32, parts, ax, "mesh_sum_" + n) for n, ax, t32, parts in zip(BIG, axes, chip_f32, arrived)]

    small_shapes = [(D,), (NH,), (HD,), (HD,), (D,), (HD,), (D,), (HD,), (NH,), (D,)]
    gathered_small, sibling_done = _last_exchange(_pack([g[n] for n in SMALL] + [sq]), halves)
    total = _sum_stack(gathered_small, "sum_small")
    small_g = dict(zip(SMALL, _unpack(total, small_shapes)[:-1]))
    loss = 0.5 * jnp.sum(_unpack(total, small_shapes)[-1]) / D
    small_g["norm_a_g"] = lax.dynamic_slice(small_g["norm_a_g"], (my_chip * (D // NCHIP),), (D // NCHIP,))

    res = {}
    for n, ax, mine_half, their_half in zip(BIG, axes, halves, sibling_done):
        out4 = _adamw_halves(w2d[n], mine_half, their_half, shard2d(m[n], n), shard2d(v[n], n), ax, "adamw_" + n)
        res[n] = tuple(unflat(t, n) for t in out4)
    row = lambda t: t.reshape(1, -1)
    small_out = _adamw_small(*[[row(d[n]) for n in SMALL] for d in (w, small_g, m, v)])
    for i, n in enumerate(SMALL):
        res[n] = tuple(t.reshape(w[n].shape) for t in (small_g[n],) + tuple(out[i] for out in small_out))

    outs = [loss, d_x[None]]
    for k in range(4):
        outs += [res[n][k] for n in NAMES]
    return tuple(outs)
```

```python
import numpy as np
import jax
import jax.numpy as jnp
from jax import lax
from jax.experimental import pallas as pl
from jax.experimental.pallas import tpu as pltpu

F32, BF16 = jnp.float32, jnp.bfloat16
S, D, HD, NH, NKV = 2048, 1024, 64, 16, 4
KVW = NKV * HD
WINDOW = 128
ROT = HD // 4
THETA = 500000.0
EPS = 1e-6
SCALE = HD ** -0.5
LANES = 128
SUBLANES = 8
NEG = -1e30
VMEM_LIMIT = 48 * 2 ** 20
ROWS = 512
ATT = 512
SWQ = 16
NCHIP = 4
ADAM_LR, ADAM_B1, ADAM_B2, ADAM_EPS, ADAM_WD, ADAM_STEP = 0.001, 0.9, 0.999, 1e-08, 0.01, 10
NT = (((1,), (1,)), ((), ()))
TN = (((0,), (0,)), ((), ()))
MESH = pl.DeviceIdType.MESH


def _params(n):
    return pltpu.CompilerParams(dimension_semantics=("arbitrary",) * n, vmem_limit_bytes=VMEM_LIMIT)


def _dot(a, b, dims=None):
    if dims is None:
        return jnp.dot(a, b, preferred_element_type=F32)
    return lax.dot_general(a, b, dims, preferred_element_type=F32)


def _dot_split(a, b, n):
    out, rest = None, a
    for _ in range(n):
        hi = rest.astype(BF16)
        term = _dot(hi, b)
        out = term if out is None else out + term
        rest = rest - hi.astype(F32)
    return out


def _seg_mat(w):
    e = (np.arange(w)[:, None] // HD == np.arange(LANES)[None, :]).astype(np.float32)
    return jnp.asarray(e, BF16)


def _spread(r, w):
    head = lax.broadcasted_iota(jnp.int32, (2 * LANES, w), 1) >> (HD.bit_length() - 1)
    row = lax.broadcasted_iota(jnp.int32, (2 * LANES, w), 0)
    et2 = jnp.where(head == (row & (LANES - 1)), 1.0, 0.0).astype(BF16)
    hi = r.astype(BF16)
    lo = (r - hi.astype(F32)).astype(BF16)
    return _dot(jnp.concatenate([hi, lo], axis=1), et2)


def _head_rstd(x, e):
    ss = _dot_split(x * x, e, 2)
    return _spread(lax.rsqrt(ss * (1.0 / HD) + EPS), x.shape[1])


def _rope(x, c, a, b):
    w = x.shape[1]
    return x * c + pltpu.roll(x, w - ROT // 2, 1) * a + pltpu.roll(x, ROT // 2, 1) * b


def _rope_t(dy, c, a, b):
    w = dy.shape[1]
    return dy * c + pltpu.roll(dy * b, w - ROT // 2, 1) + pltpu.roll(dy * a, ROT // 2, 1)


def _sigmoid(x):
    return 1.0 / (1.0 + jnp.exp(-x))


def _row_spec(shape, ts):
    nd = len(shape)
    if shape[0] == S:
        return pl.BlockSpec((ts,) + tuple(shape[1:]), lambda i: (i,) + (0,) * (nd - 1))
    return pl.BlockSpec(tuple(shape), lambda i: (0,) * nd)


def _rows_call(body, name, ins, outs, ts=ROWS):
    return pl.pallas_call(
        body, name=name, grid=(S // ts,),
        in_specs=[_row_spec(a.shape, ts) for a in ins],
        out_specs=[_row_spec(s, ts) for s, _ in outs],
        out_shape=[jax.ShapeDtypeStruct(s, d) for s, d in outs],
        compiler_params=_params(1))(*ins)


def _col_spec(ts, w, col):
    return pl.BlockSpec((ts, w), lambda i: (i, col))


TM = TN_ = 512
TM_TOKENS = 1024
TN_WIDE = 1024


def _mm(name, m, n, terms, out_dtype=F32, add=None, tm=None, tn=TN_, stacked=False, riding=None, rows_of=None):
    nterm = len(terms)
    if tm is None:
        tm = TM_TOKENS if m == S else TM
    nj, ni_ = n // tn, m // tm
    n_in = 2 * nterm + (add is not None) + (rows_of is not None and rows_of[0] is not None)
    r_in, r_out = (len(riding.ins), len(riding.outs)) if riding is not None else (0, 0)

    def body(*refs):
        if riding is not None:
            j, i = pl.program_id(0), pl.program_id(1)
            at_end = riding.hooks(refs[n_in:n_in + r_in], refs[n_in + r_in + 1:n_in + r_in + 1 + r_out],
                                  *refs[n_in + r_in + 1 + r_out:], first=(j == 0) & (i == 0),
                                  middle=(j == nj // 2) & (i == 0), last=(j == nj - 1) & (i == ni_ - 1))
        acc = None
        for t in range(nterm):
            part = _dot(refs[2 * t][...], refs[2 * t + 1][...], terms[t][4])
            acc = part if acc is None else acc + part
        if add is not None:
            acc = acc + refs[2 * nterm][...]
        refs[n_in + r_in][...] = acc.astype(out_dtype)
        if riding is not None:
            at_end()

    tile = pl.BlockSpec((tm, tn), lambda j, i: (i, j))
    ins, specs = [], []
    for a, a_spec, b, b_spec, _ in terms:
        ins += [a, b]
        specs += [a_spec, b_spec]
    if add is not None:
        ins.append(add)
        specs.append(tile)
    out_spec = pl.BlockSpec((None, tm, tn), lambda j, i: (j, i, 0)) if stacked else tile
    out_shape = jax.ShapeDtypeStruct((nj, m, tn) if stacked else (m, n), out_dtype)
    if rows_of is not None:
        taller, rows, row0 = rows_of
        out_spec = pl.BlockSpec((pl.Element(tm), pl.Element(tn)), lambda j, i: (
            pl.multiple_of(row0 + i * tm, SUBLANES), pl.multiple_of(j * tn, LANES)))
        out_shape = jax.ShapeDtypeStruct((rows, n), out_dtype)
        alias = {}
        if taller is not None:
            ins.append(taller)
            specs.append(pl.BlockSpec(memory_space=pltpu.HBM))
            alias = {len(ins) - 1: 0}
        return pl.pallas_call(body, name=name, grid=(nj, ni_), in_specs=specs, out_specs=out_spec,
                              out_shape=out_shape, input_output_aliases=alias, compiler_params=_params(2))(*ins)
    if riding is None:
        return pl.pallas_call(body, name=name, grid=(nj, ni_), in_specs=specs, out_specs=out_spec,
                              out_shape=out_shape, compiler_params=_params(2))(*ins)
    res = pl.pallas_call(
        body, name=name, grid=(nj, ni_), in_specs=specs + riding.in_specs,
        out_specs=[out_spec] + riding.out_specs, out_shape=[out_shape] + riding.out_shape,
        scratch_shapes=riding.scratch, compiler_params=_params(2))(*ins, *riding.ins)
    return res[0], res[1:]


def _a_rows(k, col=0, tm=TM_TOKENS):
    return pl.BlockSpec((tm, k), lambda j, i: (i, col))


def _a_cols(k, tm=TM):
    return pl.BlockSpec((k, tm), lambda j, i: (0, i))


def _b_cols(k, row=0, col0=0, tn=TN_):
    return pl.BlockSpec((k, tn), lambda j, i: (row, col0 + j))


def _b_rows(k, row0=0, tn=TN_):
    return pl.BlockSpec((tn, k), lambda j, i: (row0 + j, 0))


def _rmsnorm_fwd(x, gains, name):
    def body(*refs):
        xv = refs[0][...]
        r = lax.rsqrt(jnp.mean(xv * xv, axis=-1, keepdims=True) + EPS)
        xh = xv * r
        for n in range(len(gains)):
            refs[1 + len(gains) + n][...] = (xh * refs[1 + n][...]).astype(BF16)

    return _rows_call(body, name, [x] + list(gains), [((S, D), BF16)] * len(gains))


def _rmsnorm_bwd(x, gains, dus, dres, name):
    n = len(gains)

    def body(*refs):
        x_ref, g_refs, du_refs, dres_ref = refs[0], refs[1:1 + n], refs[1 + n:1 + 2 * n], refs[1 + 2 * n]
        dx_ref, dxb_ref, dg_refs = refs[2 + 2 * n], refs[3 + 2 * n], refs[4 + 2 * n:]
        xv = x_ref[...]
        r = lax.rsqrt(jnp.mean(xv * xv, axis=-1, keepdims=True) + EPS)
        xh = xv * r
        gy = None
        for m in range(n):
            du = du_refs[m][...]
            part = jnp.sum(du * xh, axis=0, keepdims=True)

            @pl.when(pl.program_id(0) == 0)
            def _(m=m, part=part):
                dg_refs[m][...] = part

            @pl.when(pl.program_id(0) != 0)
            def _(m=m, part=part):
                dg_refs[m][...] += part

            t = du * g_refs[m][...]
            gy = t if gy is None else gy + t
        dx = dres_ref[...] + r * (gy - xh * jnp.mean(gy * xh, axis=-1, keepdims=True))
        dx_ref[...] = dx
        dxb_ref[...] = dx.astype(BF16)

    outs = [((S, D), F32), ((S, D), BF16)] + [((1, D), F32)] * n
    return _rows_call(body, name, [x] + list(gains) + list(dus) + [dres], outs)


def _proj_a(u, w1t, qg, kg):
    e = _seg_mat(D)
    gains = jnp.stack([qg * SCALE, kg])

    def body(u_ref, w_ref, g_ref, e_ref, raw_ref, out_ref):
        x = _dot(u_ref[...], w_ref[...], NT)
        raw_ref[...] = x

        @pl.when(pl.program_id(0) < 2)
        def _():
            out_ref[...] = (x * _head_rstd(x, e_ref[...]) * g_ref[...]).astype(BF16)

        @pl.when(pl.program_id(0) == 2)
        def _():
            out_ref[...] = x.astype(BF16)

    tm = TM_TOKENS
    return pl.pallas_call(
        body, name="proj_a", grid=(3, S // tm),
        in_specs=[pl.BlockSpec((tm, D), lambda j, i: (i, 0)), pl.BlockSpec((D, D), lambda j, i: (j, 0)),
                  pl.BlockSpec((None, 1, D), lambda j, i: (jnp.minimum(j, 1), 0, 0)),
                  pl.BlockSpec(e.shape, lambda j, i: (0, 0))],
        out_specs=[pl.BlockSpec((tm, D), lambda j, i: (i, j)), pl.BlockSpec((None, tm, D), lambda j, i: (j, i, 0))],
        out_shape=[jax.ShapeDtypeStruct((S, 3 * D), F32), jax.ShapeDtypeStruct((3, S, D), BF16)],
        compiler_params=_params(2))(u, w1t, gains, e)


def _tri(upper):
    r, c = np.arange(ROWS)[:, None], np.arange(ROWS)[None, :]
    return jnp.asarray((r <= c) if upper else (r >= c), BF16)


def _forget_cumsum(fpad, bpad):
    def body(f_ref, b_ref, u_ref, c_ref, carry):
        @pl.when(pl.program_id(0) == 0)
        def _():
            carry[...] = jnp.zeros_like(carry)

        lf = jax.nn.log_sigmoid(f_ref[...] + b_ref[...])
        blk = _dot_split(lf.T, u_ref[...], 3) + carry[:, 0:1]
        c_ref[...] = blk
        carry[...] = jnp.broadcast_to(blk[:, ROWS - 1:ROWS], carry.shape)

    return pl.pallas_call(
        body, name="forget_cumsum", grid=(S // ROWS,),
        in_specs=[pl.BlockSpec((ROWS, LANES), lambda i: (i, 0)), pl.BlockSpec((1, LANES), lambda i: (0, 0)),
                  pl.BlockSpec((ROWS, ROWS), lambda i: (0, 0))],
        out_specs=pl.BlockSpec((LANES, ROWS), lambda i: (0, i)),
        out_shape=jax.ShapeDtypeStruct((LANES, S), F32),
        scratch_shapes=[pltpu.VMEM((LANES, LANES), F32)],
        compiler_params=_params(1))(fpad, bpad, _tri(True))


def _forget_bwd(dct, fpad, bpad):
    nb = S // ROWS

    def body(dc_ref, f_ref, b_ref, l_ref, df_ref, db_ref, carry):
        @pl.when(pl.program_id(0) == 0)
        def _():
            carry[...] = jnp.zeros_like(carry)
            db_ref[...] = jnp.zeros_like(db_ref)

        blk = _dot_split(dc_ref[...], l_ref[...], 3) + carry[:, 0:1]
        carry[...] = jnp.broadcast_to(blk[:, 0:1], carry.shape)
        df = blk.T * _sigmoid(-(f_ref[...] + b_ref[...]))
        df_ref[...] = df.astype(BF16)
        db_ref[...] += jnp.sum(df, axis=0, keepdims=True)

    return pl.pallas_call(
        body, name="forget_bwd", grid=(nb,),
        in_specs=[pl.BlockSpec((LANES, ROWS), lambda i: (0, nb - 1 - i)),
                  pl.BlockSpec((ROWS, LANES), lambda i: (nb - 1 - i, 0)),
                  pl.BlockSpec((1, LANES), lambda i: (0, 0)), pl.BlockSpec((ROWS, ROWS), lambda i: (0, 0))],
        out_specs=[pl.BlockSpec((ROWS, LANES), lambda i: (nb - 1 - i, 0)), pl.BlockSpec((1, LANES), lambda i: (0, 0))],
        out_shape=[jax.ShapeDtypeStruct((S, LANES), BF16), jax.ShapeDtypeStruct((1, LANES), F32)],
        scratch_shapes=[pltpu.VMEM((LANES, LANES), F32)],
        compiler_params=_params(1))(dct, fpad, bpad, _tri(False))


def _headnorm_bwd(x, col, gain, dy, rope, name):
    e = _seg_mat(D)
    tabs = list(rope) if rope is not None else []

    def body(*refs):
        x_ref, g_ref, dy_ref, e_ref = refs[:4]
        dx_ref, dg_ref = refs[-2:]
        xv, dyv, ev = x_ref[...], dy_ref[...], e_ref[...]
        if rope is not None:
            c, a, b = (jnp.tile(t[...], (1, D // LANES)) for t in refs[4:7])
            dyv = _rope_t(dyv, c, a, b)
        r = _head_rstd(xv, ev)
        xh = xv * r
        part = jnp.sum(dyv * xh, axis=0, keepdims=True)

        @pl.when(pl.program_id(0) == 0)
        def _():
            dg_ref[...] = part

        @pl.when(pl.program_id(0) != 0)
        def _():
            dg_ref[...] += part

        gy = dyv * g_ref[...]
        seg = _spread(_dot_split(gy * xh, ev, 2) * (1.0 / HD), D)
        dx_ref[...] = (r * (gy - xh * seg)).astype(BF16)

    whole = lambda a: pl.BlockSpec(a.shape, lambda i: (0, 0))
    return pl.pallas_call(
        body, name=name, grid=(S // ROWS,),
        in_specs=[_col_spec(ROWS, D, col), whole(gain), _col_spec(ROWS, D, 0), whole(e)]
                 + [pl.BlockSpec((ROWS, LANES), lambda i: (i, 0))] * len(tabs),
        out_specs=[_col_spec(ROWS, D, 0), whole(gain)],
        out_shape=[jax.ShapeDtypeStruct((S, D), BF16), jax.ShapeDtypeStruct((1, D), F32)],
        compiler_params=_params(1))(x, gain, dy, e, *tabs)


def _dup_mat():
    r, c = np.arange(KVW)[:, None], np.arange(2 * KVW)[None, :]
    return (r // HD == c // LANES) & (r % HD == c % HD)


def _fold_mat():
    r, c = np.arange(D)[:, None], np.arange(KVW)[None, :]
    return (r // (2 * LANES) == c // HD) & (r % HD == c % HD)


def _proj_b(u_b, u_kv, w_in_b, w_kv, qg, kg, rope):
    e, ek = _seg_mat(D), _seg_mat(KVW)
    dup = jnp.asarray(_dup_mat(), BF16)

    def body(ub_ref, ukv_ref, wb_ref, wkv_ref, qg_ref, kg_ref, e_ref, ek_ref, dup_ref, c_ref, a_ref, b_ref,
             pb_ref, kv_ref, qo, ko, vo):
        ub = ub_ref[...]
        pb = jnp.concatenate([_dot(ub, wb_ref[c]) for c in range(NCHIP)], axis=1)
        kv = _dot(ukv_ref[...], wkv_ref[...])
        pb_ref[...] = pb
        kv_ref[...] = kv
        c1, a1, b1 = c_ref[...], a_ref[...], b_ref[...]
        qv = pb[:, :D]
        qn = qv * _head_rstd(qv, e_ref[...]) * qg_ref[...]
        t = lambda z, n: jnp.tile(z, (1, n))
        qo[...] = (_rope(qn, t(c1, D // LANES), t(a1, D // LANES), t(b1, D // LANES)) * SCALE).astype(BF16)
        kvv = kv[:, :KVW]
        kn = kvv * _head_rstd(kvv, ek_ref[...]) * kg_ref[...]
        kr = _rope(kn, t(c1, KVW // LANES), t(a1, KVW // LANES), t(b1, KVW // LANES)).astype(BF16)
        ko[...] = _dot(kr, dup_ref[...]).astype(BF16)
        vo[...] = _dot(kv[:, KVW:].astype(BF16), dup_ref[...]).astype(BF16)

    whole = lambda a: pl.BlockSpec(a.shape, lambda i: (0,) * a.ndim)
    rows = lambda w: pl.BlockSpec((ROWS, w), lambda i: (i, 0))
    return pl.pallas_call(
        body, name="proj_b", grid=(S // ROWS,),
        in_specs=[rows(D), rows(D), whole(w_in_b), whole(w_kv), whole(qg), whole(kg), whole(e), whole(ek),
                  whole(dup), rows(LANES), rows(LANES), rows(LANES)],
        out_specs=[rows(2 * D), rows(2 * KVW), rows(D), rows(2 * KVW), rows(2 * KVW)],
        out_shape=[jax.ShapeDtypeStruct((S, 2 * D), F32), jax.ShapeDtypeStruct((S, 2 * KVW), F32),
                   jax.ShapeDtypeStruct((S, D), BF16), jax.ShapeDtypeStruct((S, 2 * KVW), BF16),
                   jax.ShapeDtypeStruct((S, 2 * KVW), BF16)],
        compiler_params=_params(1))(u_b, u_kv, w_in_b, w_kv, qg, kg, e, ek, dup, *rope)


def _kv_bwd(dkdup, dvdup, kv, kg, rope):
    ek = _seg_mat(KVW)
    fold = jnp.asarray(_fold_mat(), BF16)

    def body(dk_ref, dv_ref, k_ref, kg_ref, ek_ref, fold_ref, c_ref, a_ref, b_ref, dkv_ref, dg_ref):
        ev, fv = ek_ref[...], fold_ref[...]
        t = lambda z: jnp.tile(z[...], (1, KVW // LANES))
        dk = _rope_t(_dot_split(dk_ref[...], fv, 2), t(c_ref), t(a_ref), t(b_ref))
        dv = _dot_split(dv_ref[...], fv, 2)
        xv = k_ref[...]
        r = _head_rstd(xv, ev)
        xh = xv * r
        part = jnp.sum(dk * xh, axis=0, keepdims=True)

        @pl.when(pl.program_id(0) == 0)
        def _():
            dg_ref[...] = part

        @pl.when(pl.program_id(0) != 0)
        def _():
            dg_ref[...] += part

        gy = dk * kg_ref[...]
        seg = _spread(_dot_split(gy * xh, ev, 2) * (1.0 / HD), KVW)
        dkv_ref[:, 0:KVW] = (r * (gy - xh * seg)).astype(BF16)
        dkv_ref[:, KVW:2 * KVW] = dv.astype(BF16)

    whole = lambda a: pl.BlockSpec(a.shape, lambda i: (0, 0))
    tab = pl.BlockSpec((ROWS, LANES), lambda i: (i, 0))
    return pl.pallas_call(
        body, name="kv_bwd", grid=(S // ROWS,),
        in_specs=[_col_spec(ROWS, D, 0), _col_spec(ROWS, D, 0), _col_spec(ROWS, KVW, 0),
                  whole(kg), whole(ek), whole(fold), tab, tab, tab],
        out_specs=[_col_spec(ROWS, 2 * KVW, 0), whole(kg)],
        out_shape=[jax.ShapeDtypeStruct((S, 2 * KVW), BF16), jax.ShapeDtypeStruct((1, KVW), F32)],
        compiler_params=_params(1))(dkdup, dvdup, kv, kg, ek, fold, *rope)


def _out_norms(y, w_out, residual, gains):
    n = len(gains)

    def body(y_ref, w_ref, r_ref, *refs):
        h = _dot(y_ref[...], w_ref[...]) + r_ref[...]
        refs[n][...] = h
        hn = h * lax.rsqrt(jnp.mean(h * h, axis=-1, keepdims=True) + EPS)
        for k in range(n):
            refs[n + 1 + k][...] = (hn * refs[k][...]).astype(BF16)

    rows = pl.BlockSpec((TM_TOKENS, D), lambda i: (i, 0))
    whole = pl.BlockSpec((D, D), lambda i: (0, 0))
    gain = pl.BlockSpec((1, D), lambda i: (0, 0))
    return pl.pallas_call(
        body, name="out_a_norms", grid=(S // TM_TOKENS,), in_specs=[rows, whole, rows] + [gain] * n,
        out_specs=[rows] * (n + 1),
        out_shape=[jax.ShapeDtypeStruct((S, D), F32)] + [jax.ShapeDtypeStruct((S, D), BF16)] * n,
        compiler_params=_params(1))(y, w_out, residual, *gains)


def _out_loss(y, w_out, residual, target):
    def body(y_ref, w_ref, r_ref, t_ref, d_ref, db_ref, l_ref):
        diff = _dot(y_ref[...], w_ref[...]) + r_ref[...] - t_ref[...]
        d = diff * (1.0 / D)
        d_ref[...] = d
        db_ref[...] = d.astype(BF16)

        @pl.when(pl.program_id(0) == 0)
        def _():
            l_ref[...] = jnp.zeros_like(l_ref)

        l_ref[...] += jnp.sum(diff * diff, axis=0, keepdims=True)

    rows = pl.BlockSpec((TM_TOKENS, D), lambda i: (i, 0))
    whole = pl.BlockSpec((D, D), lambda i: (0, 0))
    return pl.pallas_call(
        body, name="out_b_loss", grid=(S // TM_TOKENS,), in_specs=[rows, whole, rows, rows],
        out_specs=[rows, rows, pl.BlockSpec((1, D), lambda i: (0, 0))],
        out_shape=[jax.ShapeDtypeStruct((S, D), F32), jax.ShapeDtypeStruct((S, D), BF16),
                   jax.ShapeDtypeStruct((1, D), F32)],
        compiler_params=_params(1))(y, w_out, residual, target)


def _lane():
    return lax.broadcasted_iota(jnp.int32, (1, LANES), 1)


def _head_mask(hh):
    return (_lane() < HD) if hh == 0 else (_lane() >= HD)


def _qkv_specs():
    return (pl.BlockSpec((None, ATT, LANES), lambda p, i: (0, i, p)),
            pl.BlockSpec((None, S, LANES), lambda p, i: (1, 0, p)),
            pl.BlockSpec((None, S, LANES), lambda p, i: (2, 0, p)))


def _fox_fwd(qkv, ct, gate, riding):
    nq, npair = S // ATT, NH // 2
    ni, no = len(riding.ins), len(riding.outs)

    def body(q_ref, k_ref, v_ref, c_ref, gate_ref, *rest):
        o_ref, lse_ref, y_ref = rest[ni:ni + 3]
        pair, i = pl.program_id(0), pl.program_id(1)
        at_end = riding.hooks(rest[:ni], rest[ni + 3:ni + 3 + no], *rest[ni + 3 + no:],
                              first=(pair == 0) & (i == 0), middle=(pair == npair // 2) & (i == 0),
                              last=(pair == npair - 1) & (i == nq - 1))
        q2 = q_ref[...]
        qms = [jnp.where(_head_mask(hh), q2, jnp.zeros_like(q2)) for hh in (0, 1)]

        def probs(off, width, m, hh, diag):
            s = _dot(qms[hh], k_ref[pl.ds(off, width), :], NT) - c_ref[hh:hh + 1, pl.ds(off, width)]
            if diag:
                row = i * ATT + lax.broadcasted_iota(jnp.int32, (ATT, width), 0)
                col = off + lax.broadcasted_iota(jnp.int32, (ATT, width), 1)
                s = jnp.where(col <= row, s, NEG)
            m_new = jnp.maximum(m, jnp.max(s, axis=1, keepdims=True))
            p = jnp.exp(s - m_new)
            p_hi = p.astype(BF16)
            return m_new, jnp.exp(m - m_new), p_hi, (p - p_hi.astype(F32)).astype(BF16)

        def weighted(off, width, p_hi, p_lo, hh):
            vj = v_ref[pl.ds(off, width), :]
            v1 = jnp.where(_head_mask(hh), vj, jnp.ones_like(vj))
            return _dot(p_hi, v1) + _dot(p_lo, v1)

        def step(off, width, carry, diag):
            off = pl.multiple_of(off, ATT)
            out = []
            for hh in (0, 1):
                m, acc = carry[hh]
                m, alpha, p_hi, p_lo = probs(off, width, m, hh, diag)
                out.append((m, alpha * acc + weighted(off, width, p_hi, p_lo, hh)))
            return tuple(out)

        one = (jnp.full((ATT, 1), NEG, F32), jnp.zeros((ATT, LANES), F32))
        carry = lax.fori_loop(0, i // 2, lambda j, cr: step(j * (2 * ATT), 2 * ATT, cr, False), (one, one))
        carry = lax.cond(i % 2 == 1, lambda cr: step((i - 1) * ATT, 2 * ATT, cr, True),
                         lambda cr: step(i * ATT, ATT, cr, True), carry)
        res = []
        for hh in (0, 1):
            m, acc = carry[hh]
            l = jnp.max(jnp.where(_head_mask(1 - hh), acc, 0.0), axis=1, keepdims=True)
            res.append((acc / l, m + jnp.log(l)))
        first = _head_mask(0)
        o = jnp.where(first, res[0][0], res[1][0])
        o_ref[...] = o
        lse_ref[...] = jnp.where(first, res[0][1], res[1][1])
        g = gate_ref[...]
        y_ref[...] = (o * (g * _sigmoid(g))).astype(BF16)
        at_end()

    blk = pl.BlockSpec((ATT, LANES), lambda p, i: (i, p))
    res = pl.pallas_call(
        body, name="fox_fwd", grid=(npair, nq),
        in_specs=[*_qkv_specs(), pl.BlockSpec((None, 2, S), lambda p, i: (p, 0, 0)), blk] + riding.in_specs,
        out_specs=[blk, blk, blk] + riding.out_specs,
        out_shape=[jax.ShapeDtypeStruct((S, D), F32)] * 2 + [jax.ShapeDtypeStruct((S, D), BF16)] + riding.out_shape,
        scratch_shapes=riding.scratch,
        compiler_params=_params(2))(qkv, qkv, qkv, ct, gate, *riding.ins)
    return res[0], res[1], res[2], res[3:]


def _gate_grads(dy, o, g):
    sg = _sigmoid(g)
    return dy * (g * sg), dy * o * (sg * (1.0 + g * (1.0 - sg)))


def _fox_bwd(qkv, ct, o, lse, dy, gate, riding):
    nq, npair = S // ATT, NH // 2
    ni, no = len(riding.ins), len(riding.outs)

    def body(q_ref, k_ref, v_ref, c_ref, o_ref, lse_ref, dy_ref, gate_ref, *rest):
        dq_ref, dk_ref, dvb_ref, dc_ref, dgate_ref = rest[ni:ni + 5]
        dv_ref = rest[ni + 5 + no]
        pair, i = pl.program_id(0), pl.program_id(1)
        at_end = riding.hooks(rest[:ni], rest[ni + 5:ni + 5 + no], *rest[ni + 6 + no:],
                              first=(pair == 0) & (i == 0), middle=(pair == npair // 2) & (i == 0),
                              last=(pair == npair - 1) & (i == nq - 1))

        @pl.when(i == 0)
        def _():
            dk_ref[...] = jnp.zeros_like(dk_ref)
            dv_ref[...] = jnp.zeros_like(dv_ref)
            dc_ref[...] = jnp.zeros_like(dc_ref)

        q2, lse2 = q_ref[...], lse_ref[...]
        do2, dgate = _gate_grads(dy_ref[...], o_ref[...], gate_ref[...])
        dgate_ref[...] = dgate.astype(BF16)
        do2b = do2.astype(BF16)
        prod = do2b.astype(F32) * o_ref[...]
        heads = []
        for hh in (0, 1):
            hm = _head_mask(hh)
            heads.append((jnp.where(hm, q2, jnp.zeros_like(q2)), jnp.where(hm, do2b, jnp.zeros_like(do2b)),
                          jnp.sum(jnp.where(hm, prod, 0.0), axis=1, keepdims=True),
                          jnp.max(jnp.where(hm, lse2, NEG), axis=1, keepdims=True)))

        def step(off, width, dqs, diag):
            off = pl.multiple_of(off, ATT)
            kj, vj = k_ref[pl.ds(off, width), :], v_ref[pl.ds(off, width), :]
            dk, dv, out = None, None, []
            for hh in (0, 1):
                qm, dom, delta, lse_h = heads[hh]
                s = _dot(qm, kj, NT) - c_ref[hh:hh + 1, pl.ds(off, width)]
                p = jnp.exp(s - lse_h)
                if diag:
                    row = i * ATT + lax.broadcasted_iota(jnp.int32, (ATT, width), 0)
                    col = off + lax.broadcasted_iota(jnp.int32, (ATT, width), 1)
                    p = jnp.where(col <= row, p, 0.0)
                ds = p * (_dot(dom, vj, NT) - delta)
                dc_ref[hh:hh + 1, pl.ds(off, width)] += -jnp.sum(ds, axis=0, keepdims=True)
                dsb = ds.astype(BF16)
                dk_h, dv_h = _dot(dsb, qm, TN), _dot(p.astype(BF16), dom, TN)
                dk, dv = (dk_h, dv_h) if dk is None else (dk + dk_h, dv + dv_h)
                out.append(dqs[hh] + _dot(dsb, kj))
            dk_ref[pl.ds(off, width), :] += dk
            dv_ref[pl.ds(off, width), :] += dv
            return tuple(out)

        zero = jnp.zeros((ATT, LANES), F32)
        dqs = lax.fori_loop(0, i // 2, lambda j, acc: step(j * (2 * ATT), 2 * ATT, acc, False), (zero, zero))
        dqs = lax.cond(i % 2 == 1, lambda acc: step((i - 1) * ATT, 2 * ATT, acc, True),
                       lambda acc: step(i * ATT, ATT, acc, True), dqs)
        dq_ref[...] = jnp.where(_head_mask(0), dqs[0], dqs[1]) * SCALE

        @pl.when(i == nq - 1)
        def _():
            dvb_ref[...] = dv_ref[...].astype(BF16)

        at_end()

    blk = pl.BlockSpec((ATT, LANES), lambda p, i: (i, p))
    full = pl.BlockSpec((S, LANES), lambda p, i: (0, p))
    cspec = pl.BlockSpec((None, 2, S), lambda p, i: (p, 0, 0))
    res = pl.pallas_call(
        body, name="fox_bwd", grid=(npair, nq),
        in_specs=[*_qkv_specs(), cspec, blk, blk, blk, blk] + riding.in_specs,
        out_specs=[blk, full, full, cspec, blk] + riding.out_specs,
        out_shape=[jax.ShapeDtypeStruct((S, D), F32)] * 2 + [jax.ShapeDtypeStruct((S, D), BF16),
                                                              jax.ShapeDtypeStruct((npair, 2, S), F32),
                                                              jax.ShapeDtypeStruct((S, D), BF16)]
                  + riding.out_shape,
        scratch_shapes=[pltpu.VMEM((S, LANES), F32)] + riding.scratch,
        compiler_params=_params(2))(qkv, qkv, qkv, ct, o, lse, dy, gate, *riding.ins)
    return res[0], res[1], res[2], res[3], res[4], res[5:]


def _both_heads(x):
    return jnp.concatenate([jnp.where(_head_mask(hh), x, jnp.zeros_like(x)) for hh in (0, 1)], axis=0)


def _per_head(col0, col1):
    return jnp.concatenate([jnp.broadcast_to(col0, (WINDOW, 1)), jnp.broadcast_to(col1, (WINDOW, 1))], axis=0)


def _unstack(x2):
    return jnp.where(_head_mask(0), x2[:WINDOW], x2[WINDOW:])


def _swa_valid(i, start):
    r = lax.broadcasted_iota(jnp.int32, (2 * WINDOW, 2 * WINDOW), 0)
    qabs = i * WINDOW + jnp.where(r >= WINDOW, r - WINDOW, r)
    kabs = start + lax.broadcasted_iota(jnp.int32, (2 * WINDOW, 2 * WINDOW), 1)
    return (kabs <= qabs) & (qabs - kabs < WINDOW)


def _swa_fwd(q, kdup, vdup, sinks_t, proj, gate_col):
    def body(q_ref, k_ref, v_ref, sk_ref, gate_ref, o_ref, lse_ref, y_ref):
        skv = sk_ref[...]
        first = _head_mask(0)
        for sb in range(SWQ):
            i = pl.program_id(1) * SWQ + sb
            rows = slice(sb * WINDOW, (sb + 1) * WINDOW)
            start = pl.multiple_of(jnp.maximum(i - 1, 0) * WINDOW, WINDOW)
            kk, vv = k_ref[pl.ds(start, 2 * WINDOW), :], v_ref[pl.ds(start, 2 * WINDOW), :]
            q2 = q_ref[rows, :]
            valid = _swa_valid(i, start)[:WINDOW]
            res = []
            for hh in (0, 1):
                hm = _head_mask(hh)
                sink = jnp.max(jnp.where(hm, skv, NEG), axis=1, keepdims=True)
                s = jnp.where(valid, _dot(jnp.where(hm, q2, jnp.zeros_like(q2)), kk, NT), NEG)
                m = jnp.maximum(jnp.max(s, axis=1, keepdims=True), sink)
                p = jnp.exp(s - m)
                l = jnp.sum(p, axis=1, keepdims=True) + jnp.exp(sink - m)
                res.append((_dot(p.astype(BF16), vv) / l, m + jnp.log(l)))
            o = jnp.where(first, res[0][0], res[1][0])
            o_ref[rows, :] = o
            lse_ref[rows, :] = jnp.where(first, res[0][1], res[1][1])
            g = gate_ref[rows, :]
            y_ref[rows, :] = (o * (g * _sigmoid(g))).astype(BF16)

    blk = pl.BlockSpec((SWQ * WINDOW, LANES), lambda p, i: (i, p))
    gate = pl.BlockSpec((SWQ * WINDOW, LANES), lambda p, i: (i, gate_col + p))
    full = pl.BlockSpec((S, LANES), lambda p, i: (0, p // 2))
    return pl.pallas_call(
        body, name="swa_fwd", grid=(NH // 2, S // (SWQ * WINDOW)),
        in_specs=[blk, full, full, pl.BlockSpec((1, LANES), lambda p, i: (0, p)), gate],
        out_specs=[blk, blk, blk],
        out_shape=[jax.ShapeDtypeStruct((S, D), F32)] * 2 + [jax.ShapeDtypeStruct((S, D), BF16)],
        compiler_params=_params(2))(q, kdup, vdup, sinks_t, proj)


def _swa_bwd(q, kdup, vdup, sinks_t, o, lse, dy, proj, gate_col):
    def body(q_ref, k_ref, v_ref, sk_ref, o_ref, lse_ref, dy_ref, gate_ref, dq_ref, dk_ref, dv_ref, dsk_ref,
             dgate_ref):
        @pl.when(pl.program_id(1) == 0)
        def _():
            dk_ref[...] = jnp.zeros_like(dk_ref)
            dv_ref[...] = jnp.zeros_like(dv_ref)
            dsk_ref[...] = jnp.zeros_like(dsk_ref)

        skv = sk_ref[...]
        first = _head_mask(0)
        sink = _per_head(*[jnp.max(jnp.where(_head_mask(hh), skv, NEG), axis=1, keepdims=True) for hh in (0, 1)])
        for sb in range(SWQ):
            i = pl.program_id(1) * SWQ + sb
            rows = slice(sb * WINDOW, (sb + 1) * WINDOW)
            start = pl.multiple_of(jnp.maximum(i - 1, 0) * WINDOW, WINDOW)
            kk, vv = k_ref[pl.ds(start, 2 * WINDOW), :], v_ref[pl.ds(start, 2 * WINDOW), :]
            do2, dgate = _gate_grads(dy_ref[rows, :], o_ref[rows, :], gate_ref[rows, :])
            dgate_ref[rows, :] = dgate.astype(BF16)
            do2b = do2.astype(BF16)
            prod, lse2 = do2b.astype(F32) * o_ref[rows, :], lse_ref[rows, :]
            qs, dos = _both_heads(q_ref[rows, :]), _both_heads(do2b)
            delta = jnp.concatenate([jnp.sum(jnp.where(_head_mask(hh), prod, 0.0), axis=1, keepdims=True)
                                     for hh in (0, 1)], axis=0)
            lse_h = jnp.concatenate([jnp.max(jnp.where(_head_mask(hh), lse2, NEG), axis=1, keepdims=True)
                                     for hh in (0, 1)], axis=0)
            p = jnp.where(_swa_valid(i, start), jnp.exp(_dot(qs, kk, NT) - lse_h), 0.0)
            dsb = (p * (_dot(dos, vv, NT) - delta)).astype(BF16)
            dk_ref[pl.ds(start, 2 * WINDOW), :] += _dot(dsb, qs, TN)
            dv_ref[pl.ds(start, 2 * WINDOW), :] += _dot(p.astype(BF16), dos, TN)
            dq_ref[rows, :] = _unstack(_dot(dsb, kk)) * SCALE
            t = jnp.exp(sink - lse_h) * delta
            dsk_ref[...] += -jnp.where(first, jnp.sum(t[:WINDOW], axis=0, keepdims=True),
                                       jnp.sum(t[WINDOW:], axis=0, keepdims=True))

    blk = pl.BlockSpec((SWQ * WINDOW, LANES), lambda p, i: (i, p))
    full = pl.BlockSpec((S, LANES), lambda p, i: (0, p // 2))
    acc = pl.BlockSpec((S, LANES), lambda p, i: (0, p))
    sk = pl.BlockSpec((1, LANES), lambda p, i: (0, p))
    gate = pl.BlockSpec((SWQ * WINDOW, LANES), lambda p, i: (i, gate_col + p))
    return pl.pallas_call(
        body, name="swa_bwd", grid=(NH // 2, S // (SWQ * WINDOW)),
        in_specs=[blk, full, full, sk, blk, blk, blk, gate],
        out_specs=[blk, acc, acc, sk, blk],
        out_shape=[jax.ShapeDtypeStruct((S, D), F32)] * 3 + [jax.ShapeDtypeStruct((1, D), F32),
                                                              jax.ShapeDtypeStruct((S, D), BF16)],
        compiler_params=_params(2))(q, kdup, vdup, sinks_t, o, lse, dy, proj)


def _adamw_math(w, g, m, v):
    m = ADAM_B1 * m + (1.0 - ADAM_B1) * g
    v = ADAM_B2 * v + (1.0 - ADAM_B2) * jnp.square(g)
    m_hat = m / (1.0 - ADAM_B1 ** ADAM_STEP)
    v_hat = v / (1.0 - ADAM_B2 ** ADAM_STEP)
    delta = -ADAM_LR * (m_hat / (jnp.sqrt(v_hat) + ADAM_EPS) + ADAM_WD * w)
    return delta, m, v


def _adamw_small(ws, gs, ms, vs):
    k = len(ws)

    def body(*refs):
        for p in range(k):
            w_ref, g_ref, m_ref, v_ref = (refs[q * k + p] for q in range(4))
            d, mo, vo = _adamw_math(w_ref[...], g_ref[...], m_ref[...], v_ref[...])
            refs[4 * k + p][...], refs[5 * k + p][...], refs[6 * k + p][...] = d, mo, vo

    res = pl.pallas_call(
        body, name="adamw_small",
        out_shape=[jax.ShapeDtypeStruct(t.shape, F32) for t in ws] * 3)(*ws, *gs, *ms, *vs)
    return res[:k], res[k:2 * k], res[2 * k:]


SUM_TILES = (512, 256, 128)


FLAT_BLOCK = 257 * 1024


def _tiles(shape, axis, lead=0, halves=False):
    if len(shape) == 1:
        count = shape[0] // FLAT_BLOCK
        return (FLAT_BLOCK,), count, lambda pos, *lead_idx: (sum(k * count for k in lead_idx) + pos,)
    r, c = shape
    tile = next(t for t in SUM_TILES if (shape[axis] // (2 if halves else 1)) % t == 0)
    blk = (tile, c) if axis == 0 else (r, tile)
    count = shape[axis] // tile

    def index(pos, *lead_idx):
        return tuple(lead_idx) + ((pos, 0) if axis == 0 else (0, pos))

    return (None,) * lead + blk, count, index


def _adamw_halves(w, g_mine, g_theirs, m, v, axis, name):
    blk, count, index = _tiles(w.shape, axis, halves=True)
    per_half = count // 2

    def body(w_ref, a_ref, b_ref, m_ref, v_ref, g_ref, d_ref, mo_ref, vo_ref):
        is_mine = pl.program_id(0) // per_half == lax.axis_index("c")
        g = jnp.where(is_mine, a_ref[...], b_ref[...])
        g_ref[...] = g
        d_ref[...], mo_ref[...], vo_ref[...] = _adamw_math(w_ref[...], g, m_ref[...], v_ref[...])

    spec = pl.BlockSpec(blk, lambda i: index(i))
    half = pl.BlockSpec(blk, lambda i: index(i % per_half))
    return pl.pallas_call(
        body, name=name, grid=(count,), in_specs=[spec, half, half, spec, spec], out_specs=[spec] * 4,
        out_shape=[jax.ShapeDtypeStruct(w.shape, F32)] * 4, compiler_params=_params(1))(w, g_mine, g_theirs, m, v)


def _chip_sum(blocks, from_sibling, axis, name):
    flat = blocks.ndim == 1
    blk, count, index = _tiles((from_sibling.shape[0] // NCHIP,) if flat else from_sibling.shape[1:], axis, lead=1)

    def body(lo_ref, hi_ref, p_ref, o32, o16):
        mine = jnp.where(lax.axis_index("c") == 0, lo_ref[...], hi_ref[...])
        acc = mine + p_ref[...]
        o32[...] = acc
        o16[...] = acc.astype(BF16)

    half = pl.BlockSpec(blk, lambda k, i: index(i, k))
    if flat:
        lo = pl.BlockSpec(blk, lambda k, i: (2 * count * k + i,))
        hi = pl.BlockSpec(blk, lambda k, i: (2 * count * k + count + i,))
    else:
        lo, hi = half, pl.BlockSpec(blk, lambda k, i: index(i + count, k))
    return pl.pallas_call(
        body, name=name, grid=(NCHIP, count), in_specs=[lo, hi, half], out_specs=[half, half],
        out_shape=[jax.ShapeDtypeStruct(from_sibling.shape, F32), jax.ShapeDtypeStruct(from_sibling.shape, BF16)],
        compiler_params=_params(2))(blocks, blocks, from_sibling)


def _mesh_sum(chip_sums, parts, axis, name):
    flat = chip_sums.ndim == 1
    one = (chip_sums.shape[0] // NCHIP,) if flat else chip_sums.shape[1:]
    blk, count, index = _tiles(one, axis)
    n = NCHIP - 1

    def body(chip_ref, a_ref, *refs):
        acc = a_ref[...]
        for k in range(n):
            acc = acc + refs[k][...].astype(F32)
        refs[n][...] = acc

    spec = pl.BlockSpec(blk, lambda i, chip: index(i))
    if flat:
        mine = pl.BlockSpec(blk, lambda i, chip: (chip[0] * count + i,))
        part = [pl.BlockSpec(blk, lambda i, chip, k=k: (k * count + i,)) for k in range(n)]
    else:
        mine = pl.BlockSpec((None,) + blk, lambda i, chip: (chip[0],) + index(i))
        part = [pl.BlockSpec((None,) + blk, lambda i, chip, k=k: (k,) + index(i)) for k in range(n)]
    return pl.pallas_call(
        body, name=name,
        grid_spec=pltpu.PrefetchScalarGridSpec(num_scalar_prefetch=1, grid=(count,), in_specs=[mine] + part,
                                               out_specs=spec),
        out_shape=jax.ShapeDtypeStruct(one, F32),
        compiler_params=_params(1))(_chip(_coords()).astype(jnp.int32).reshape(1), chip_sums, *([parts] * n))


def _sum_stack(parts, name):
    n = parts.shape[0]

    def body(p_ref, o_ref):
        acc = p_ref[0]
        for k in range(1, n):
            acc = acc + p_ref[k]
        o_ref[...] = acc

    return pl.pallas_call(body, name=name, out_shape=jax.ShapeDtypeStruct(parts.shape[1:], F32))(parts)


def _coords():
    return lax.axis_index("x"), lax.axis_index("y"), lax.axis_index("c")


def _chip(who):
    return 2 * who[0] + who[1]


def _flip(who, mask):
    return tuple((1 - v) if b else v for v, b in zip(who, mask))


def _transfer(transfers, t, I, O, ssem, rsem, receiving):
    tr, me = transfers[t], _coords()
    peer = _flip(me, tr["mask"])
    return pltpu.make_async_remote_copy(
        src_ref=tr["src"](I, O, me), dst_ref=tr["dst"](I, O, peer if receiving else me),
        send_sem=ssem.at[t], recv_sem=rsem.at[t], device_id=peer, device_id_type=MESH)


def _start_transfers(transfers, I, O, ssem, rsem, onward):
    arrived = set()
    for t, tr in enumerate(transfers):
        after = tr.get("after")
        if (after is not None) != onward:
            continue
        if after is not None and after not in arrived:
            _transfer(transfers, after, I, O, ssem, rsem, True).wait_recv()
            arrived.add(after)
        _transfer(transfers, t, I, O, ssem, rsem, False).start()


def _finish_transfers(transfers, I, O, ssem, rsem):
    passed_on = {tr["after"] for tr in transfers if tr.get("after") is not None}
    for t in range(len(transfers)):
        if t not in passed_on:
            _transfer(transfers, t, I, O, ssem, rsem, True).wait_recv()
    for t in range(len(transfers)):
        _transfer(transfers, t, I, O, ssem, rsem, False).wait_send()


def _own_copies(own, I, O, stage, lsem, leg):
    for n, (src, dst) in enumerate(own):
        me = _coords()
        bring =pltpu.make_async_copy(src(I, O, me), stage[n], lsem.at[2 * n])
        put = pltpu.make_async_copy(stage[n], dst(I, O, me), lsem.at[2 * n + 1])
        if leg == 0:
            bring.start()
        elif leg == 1:
            bring.wait()
            put.start()
        else:
            put.wait()


def _own_scratch(own, ins):
    return [pltpu.VMEM(ins[n].shape, ins[n].dtype) for n in range(len(own))], pltpu.SemaphoreType.DMA((max(2 * len(own), 1),))


def _exchange(name, ins, outs, transfers, own=()):
    ni, no = len(ins), len(outs)
    nt = len(transfers)
    stages, stage_sems = _own_scratch(own, ins)

    def body(*refs):
        I, O = refs[:ni], refs[ni:ni + no]
        ssem, rsem, lsem = refs[ni + no:ni + no + 3]
        stage = refs[ni + no + 3:]
        _own_copies(own, I, O, stage, lsem, 0)
        _start_transfers(transfers, I, O, ssem, rsem, False)
        _own_copies(own, I, O, stage, lsem, 1)
        _start_transfers(transfers, I, O, ssem, rsem, True)
        _finish_transfers(transfers, I, O, ssem, rsem)
        _own_copies(own, I, O, stage, lsem, 2)

    hbm = pl.BlockSpec(memory_space=pltpu.HBM)
    return pl.pallas_call(
        body, name=name, in_specs=[hbm] * ni, out_specs=[hbm] * no,
        out_shape=[jax.ShapeDtypeStruct(s, d) for s, d in outs],
        scratch_shapes=[pltpu.SemaphoreType.DMA((nt,)), pltpu.SemaphoreType.DMA((nt,)), stage_sems] + stages,
        compiler_params=pltpu.CompilerParams(has_side_effects=True, vmem_limit_bytes=VMEM_LIMIT))(*ins)


CHIP_MASKS = [(0, 1, 0), (1, 0, 0), (1, 1, 0)]
SIBLING = (0, 0, 1)


def _half(shape2d, axis, which):
    n = shape2d[axis] // 2
    cut = pl.ds(pl.multiple_of(which * n, n), n)
    return (cut, slice(None)) if axis == 0 else (slice(None), cut)


class _Riding:
    def __init__(self, transfers, ins, outs, own=()):
        self.transfers, self.ins, self.outs, self.own = transfers, list(ins), list(outs), list(own)
        hbm = pl.BlockSpec(memory_space=pltpu.HBM)
        self.in_specs, self.out_specs = [hbm] * len(self.ins), [hbm] * len(self.outs)
        self.out_shape = [jax.ShapeDtypeStruct(s, d) for s, d in self.outs]
        stages, stage_sems = _own_scratch(self.own, self.ins)
        self.scratch = [pltpu.SemaphoreType.DMA((max(len(transfers), 1),))] * 2 + [stage_sems] + stages

    def alone(self, name):
        return _exchange(name, self.ins, self.outs, self.transfers, self.own)

    def hooks(self, I, O, ssem, rsem, lsem, *stage, first, middle, last):
        tr, own = self.transfers, self.own

        @pl.when(first)
        def _():
            _own_copies(own, I, O, stage, lsem, 0)
            _start_transfers(tr, I, O, ssem, rsem, False)

        if own or any(t.get("after") is not None for t in tr):
            @pl.when(middle)
            def _():
                _own_copies(own, I, O, stage, lsem, 1)
                _start_transfers(tr, I, O, ssem, rsem, True)

        def at_end():
            @pl.when(last)
            def _():
                _finish_transfers(tr, I, O, ssem, rsem)
                _own_copies(own, I, O, stage, lsem, 2)

        return at_end


def _stretch(n, pos):
    return (pl.ds(pos * n if isinstance(pos, int) else pl.multiple_of(pos * n, n), n),)


def _gather_plan(shards, axes):
    def half(a, who):
        if shards[a].ndim == 1:
            return _stretch(shards[a].shape[0] // 2, who[2])
        return _half(shards[a].shape, axes[a], who[2])

    def landed(a, chip, who):
        if shards[a].ndim == 1:
            return _stretch(shards[a].shape[0] // 2, 2 * chip + who[2])
        return (chip,) + half(a, who)

    over_ici, onward = [], []
    for a in range(len(shards)):
        for mask in CHIP_MASKS:
            over_ici.append(dict(
                mask=mask,
                src=lambda I, O, me, a=a: I[a].at[half(a, me)],
                dst=lambda I, O, who, a=a: O[a].at[landed(a, _chip(who), who)]))
            onward.append(dict(
                mask=SIBLING, after=len(over_ici) - 1,
                src=lambda I, O, me, a=a, mask=mask: O[a].at[landed(a, _chip(_flip(me, mask)), me)],
                dst=lambda I, O, who, a=a, mask=mask: O[a].at[landed(a, _chip(_flip(who, mask)), who)]))
    outs = [((NCHIP * s.shape[0],) if s.ndim == 1 else (NCHIP,) + s.shape, s.dtype) for s in shards]

    def whole(a, chip):
        return _stretch(shards[a].shape[0], chip) if shards[a].ndim == 1 else (chip,)

    own = [(lambda I, O, me, a=a: I[a], lambda I, O, me, a=a: O[a].at[whole(a, _chip(me))])
           for a in range(len(shards))]
    return over_ici + onward, outs, own


def _gather_shards(shards, axes):
    transfers, outs, own = _gather_plan(shards, axes)
    return _exchange("gather_weights", shards, outs, transfers, own)


def _halves_plan(blocks, axes):
    def cut(a, which):
        return (slice(None),) + _half(blocks[a].shape[1:], axes[a], which)

    transfers, outs = [], []
    for a, (b, ax) in enumerate(zip(blocks, axes)):
        if b.ndim == 1:
            h = b.shape[0] // NCHIP // 2
            for k in range(NCHIP):
                transfers.append(dict(mask=SIBLING,
                                      src=lambda I, O, me, a=a, k=k, h=h: I[a].at[_stretch(h, 2 * k + 1 - me[2])],
                                      dst=lambda I, O, who, a=a, k=k, h=h: O[a].at[_stretch(h, k)]))
            outs.append(((NCHIP * h,), b.dtype))
        else:
            transfers.append(dict(mask=SIBLING, src=lambda I, O, me, a=a: I[a].at[cut(a, 1 - me[2])],
                                  dst=lambda I, O, who, a=a: O[a]))
            shape = list(b.shape)
            shape[ax + 1] //= 2
            outs.append((tuple(shape), b.dtype))
    return transfers, outs


def _scatter_plan(tb):
    def slot(a, k):
        return (k,) if tb[a].ndim == 3 else _stretch(tb[a].shape[0] // NCHIP, k)

    transfers = []
    for a in range(len(tb)):
        for n, mask in enumerate(CHIP_MASKS):
            transfers.append(dict(
                mask=mask,
                src=lambda I, O, me, a=a, mask=mask: I[a].at[slot(a, _chip(_flip(me, mask)))],
                dst=lambda I, O, who, a=a, n=n: O[a].at[slot(a, n)]))
    outs = [((3,) + t.shape[1:] if t.ndim == 3 else (3 * (t.shape[0] // NCHIP),), t.dtype) for t in tb]
    return transfers, outs


def _last_exchange(vec, halves):
    def slot(who):
        return 4 * who[0] + 2 * who[1] + who[2]

    masks = [(m >> 2 & 1, m >> 1 & 1, m & 1) for m in range(1, 8)]
    transfers = [dict(mask=mask, src=lambda I, O, me: I[0], dst=lambda I, O, who: O[0].at[slot(who)])
                 for mask in masks]
    transfers += [dict(mask=SIBLING, src=lambda I, O, me, a=a: I[a], dst=lambda I, O, who, a=a: O[a])
                  for a in range(1, 1 + len(halves))]
    own = [(lambda I, O, me: I[0], lambda I, O, me: O[0].at[slot(me)])]
    outs = [((8,) + vec.shape, vec.dtype)] + [(t.shape, t.dtype) for t in halves]
    res = _exchange("last_exchange", [vec] + list(halves), outs, transfers, own)
    return res[0], res[1:]


def _rope_tables(positions):
    half = ROT // 2
    inv_freq = jnp.power(jnp.float32(THETA), -jnp.arange(0, ROT, 2, dtype=F32) / ROT)
    ang = positions.astype(F32)[:, None] * inv_freq[None, :]
    cos, sin = jnp.cos(ang), jnp.sin(ang)
    one, zero, z8 = jnp.ones((S, HD - ROT), F32), jnp.zeros((S, HD - ROT), F32), jnp.zeros((S, half), F32)
    c = jnp.concatenate([cos, cos, one], axis=1)
    a = jnp.concatenate([-sin, z8, zero], axis=1)
    b = jnp.concatenate([z8, sin, zero], axis=1)
    return tuple(jnp.tile(t, (1, 2)) for t in (c, a, b))


def _tile_heads(g, w):
    return jnp.tile(g.reshape(1, HD), (1, w // HD))


def _fold_heads(dg):
    return dg.reshape(-1, HD).sum(axis=0)


def _pad_lanes(a):
    return jnp.pad(a, ((0, 0), (0, LANES - a.shape[1])))


def _local_step(x, target, positions, wt, fetch, late_weights, begin_reduce):
    rope = _rope_tables(positions)
    w1t = wt["w_in_a_t"]
    f_row = 3 * D // LANES
    wg_t = w1t[3 * D + NH:]
    in_b_block = lambda c: pl.BlockSpec((None, TN_WIDE, TN_), lambda j, i: (c, j, 0))
    b_pad = _pad_lanes(wt["b_forget"].reshape(1, NH))
    qg_a, kg_a = _tile_heads(wt["qnorm_a_g"], D), _tile_heads(wt["knorm_a_g"], D)
    qg_b, kg_b = _tile_heads(wt["qnorm_b_g"], D), _tile_heads(wt["knorm_b_g"], KVW)
    norm_a, kv_g, norm_b = wt["norm_a_g"].reshape(1, D), wt["kv_norm_g"].reshape(1, D), wt["norm_b_g"].reshape(1, D)
    sinks_t = jnp.repeat(wt["sinks"].reshape(1, NH), HD, axis=1)

    (u_a,) = _rmsnorm_fwd(x, [norm_a], "norm_a")
    qkv, qkv_a = _proj_a(u_a, w1t, qg_a, kg_a)
    fpad = _mm("proj_f", S, LANES, [(u_a, _a_rows(D), w1t, _b_rows(D, row0=f_row, tn=LANES), NT)], tn=LANES)
    gate_a = _mm("proj_gate_a", S, D, [(u_a, _a_rows(D), wg_t, _b_rows(D, tn=TN_WIDE), NT)], tn=TN_WIDE)
    ct = _forget_cumsum(fpad, b_pad)
    ct2 = ct[:NH].reshape(NH // 2, 2, S)
    o_a, lse_a, y_a, fetched = _fox_fwd(qkv_a, ct2, gate_a, fetch)
    wt = {**wt, **late_weights(fetched)}
    w_in_b = wt["w_in_b"]
    h1, u_kv, u_b = _out_norms(y_a, wt["w_out_a"], x, [kv_g, norm_b])
    pb, kv, q_b, kdup, vdup = _proj_b(u_b, u_kv, w_in_b, wt["w_kv"], qg_b, kg_b, rope)
    gate_b_col = D // LANES
    o_b, lse_b, y_b = _swa_fwd(q_b, kdup, vdup, sinks_t, pb, gate_b_col)
    d_out, d_out_b, sq = _out_loss(y_b, wt["w_out_b"], h1, target)

    g = {}
    g["w_out_b"] = _mm("dw_out_b", D, D, [(y_b, _a_cols(S), d_out_b, _b_cols(S, tn=TN_WIDE), TN)], tn=TN_WIDE)
    d_y_b = _mm("dy_b", S, D, [(d_out_b, _a_rows(D), wt["w_out_b"], _b_rows(D, tn=TN_WIDE), NT)], tn=TN_WIDE)
    dq_b, dkdup, dvdup, dsk, d_gate_b = _swa_bwd(q_b, kdup, vdup, sinks_t, o_b, lse_b, d_y_b, pb, gate_b_col)
    g["sinks"] = dsk[0, ::HD]
    d_qb_raw, dg = _headnorm_bwd(pb, 0, qg_b, dq_b, rope, "qnorm_b_bwd")
    g["qnorm_b_g"] = _fold_heads(dg)
    d_pb = [d_qb_raw, d_qb_raw, d_gate_b, d_gate_b]
    g["w_in_b"] = jnp.concatenate([
        _mm("dw_in_b_q", D, D, [(u_b, _a_cols(S), d_qb_raw, _b_cols(S), TN)], stacked=True),
        _mm("dw_in_b_gate", D, D, [(u_b, _a_cols(S), d_gate_b, _b_cols(S), TN)], stacked=True)], axis=0)
    d_u_b = _mm("du_b", S, D, [(d_pb[c], _a_rows(TN_, col=c % 2), w_in_b, in_b_block(c), NT) for c in range(NCHIP)],
                tn=TN_WIDE)
    d_kv, dg = _kv_bwd(dkdup, dvdup, kv, kg_b, rope)
    g["knorm_b_g"] = _fold_heads(dg)
    g["w_kv"] = _mm("dw_kv", D, 2 * KVW, [(u_kv, _a_cols(S), d_kv, _b_cols(S), TN)])
    d_u_kv = _mm("du_kv", S, D, [(d_kv, _a_rows(2 * KVW), wt["w_kv"], _b_rows(2 * KVW, tn=TN_WIDE), NT)], tn=TN_WIDE)
    d_h1, d_h1_b, g["kv_norm_g"], g["norm_b_g"] = _rmsnorm_bwd(h1, [kv_g, norm_b], [d_u_kv, d_u_b], d_out, "norm_b_bwd")
    g["w_out_a"] = _mm("dw_out_a", D, D, [(y_a, _a_cols(S), d_h1_b, _b_cols(S, tn=TN_WIDE), TN)], tn=TN_WIDE)
    late = {n: g[n] for n in LATE}
    d_y_a, halves = _mm("dy_a", S, D, [(d_h1_b, _a_rows(D), wt["w_out_a"], _b_rows(D, tn=TN_WIDE), NT)],
                        tn=TN_WIDE, riding=begin_reduce(late))
    riding, so_far = begin_reduce(late, halves)
    dq_a, dk_a, dv_a, dct, d_gate_a, arrived = _fox_bwd(qkv_a, ct2, o_a, lse_a, d_y_a, gate_a, riding)
    dct_pad = jnp.pad(dct.reshape(NH, S), ((0, LANES - NH), (0, 0)))
    d_f, db = _forget_bwd(dct_pad, fpad, b_pad)
    g["b_forget"] = db[0, :NH]
    d_q_raw, dg = _headnorm_bwd(qkv, 0, qg_a, dq_a, None, "qnorm_a_bwd")
    g["qnorm_a_g"] = _fold_heads(dg)
    d_k_raw, dg = _headnorm_bwd(qkv, 1, kg_a, dk_a, None, "knorm_a_bwd")
    g["knorm_a_g"] = _fold_heads(dg)
    rows, gw = 4 * D + NH, None
    for n, t, row0 in (("q", d_q_raw, 0), ("k", d_k_raw, D), ("v", dv_a, 2 * D)):
        gw = _mm("dw_in_a_" + n, D, D, [(t, _a_cols(S), u_a, _b_cols(S, tn=TN_WIDE), TN)], tn=TN_WIDE,
                 rows_of=(gw, rows, row0))
    gw = _mm("dw_in_a_f", LANES, D, [(d_f, _a_cols(S, tm=LANES), u_a, _b_cols(S, tn=TN_WIDE), TN)], tm=LANES,
             tn=TN_WIDE, rows_of=(gw, rows, 3 * D))
    g["w_in_a"] = _mm("dw_in_a_gate", D, D, [(d_gate_a, _a_cols(S), u_a, _b_cols(S, tn=TN_WIDE), TN)], tn=TN_WIDE,
                      rows_of=(gw, rows, 3 * D + NH))
    first = {"w_in_a": g["w_in_a"]}
    riding, so_far_first = begin_reduce(first, begin_reduce(first).alone("sibling_halves_w_in_a"))
    d_u_a, arrived_first = _mm("du_a", S, D, [
        (d_q_raw, _a_rows(D), w1t, _b_cols(D, row=0, tn=TN_WIDE), None),
        (d_k_raw, _a_rows(D), w1t, _b_cols(D, row=1, tn=TN_WIDE), None),
        (dv_a, _a_rows(D), w1t, _b_cols(D, row=2, tn=TN_WIDE), None),
        (d_gate_a, _a_rows(D), wg_t, _b_cols(D, tn=TN_WIDE), None),
        (d_f, _a_rows(LANES), w1t, _b_cols(LANES, row=f_row, tn=TN_WIDE), None)], tn=TN_WIDE, riding=riding)
    d_x, _, g["norm_a_g"] = _rmsnorm_bwd(x, [norm_a], [d_u_a], d_h1, "norm_a_bwd")
    return sq, d_x, g, (list(so_far_first) + list(so_far), list(arrived_first) + list(arrived))


BIG = ["w_in_a", "w_out_a", "w_kv", "w_in_b", "w_out_b"]
LATE = BIG[1:]
SPLIT = {"w_in_a": None, "w_out_a": 0, "w_kv": 0, "w_in_b": 0, "w_out_b": 0}
SMALL = ["norm_a_g", "b_forget", "qnorm_a_g", "knorm_a_g", "kv_norm_g", "knorm_b_g", "norm_b_g", "qnorm_b_g", "sinks"]
NAMES = ["norm_a_g", "w_in_a", "b_forget", "qnorm_a_g", "knorm_a_g", "w_out_a", "kv_norm_g", "w_kv", "knorm_b_g",
         "norm_b_g", "w_in_b", "qnorm_b_g", "sinks", "w_out_b"]


def _pack(vals):
    flat = []
    for v in vals:
        v = v.reshape(-1)
        flat.append(jnp.pad(v, (0, -v.shape[0] % LANES)))
    flat = jnp.concatenate(flat)
    flat = jnp.pad(flat, (0, -flat.shape[0] % (8 * LANES)))
    return flat.reshape(-1, LANES)


def _unpack(packed, shapes):
    flat, out, off = packed.reshape(-1), [], 0
    for s in shapes:
        n = int(np.prod(s))
        out.append(flat[off:off + n].reshape(s))
        off += n + (-n % LANES)
    return out


def kernel(x, positions, norm_a_g, w_in_a, b_forget, qnorm_a_g, knorm_a_g, w_out_a, kv_norm_g, w_kv, knorm_b_g, norm_b_g, w_in_b, qnorm_b_g, sinks, w_out_b, loss_target, m_norm_a_g, m_w_in_a, m_b_forget, m_qnorm_a_g, m_knorm_a_g, m_w_out_a, m_kv_norm_g, m_w_kv, m_knorm_b_g, m_norm_b_g, m_w_in_b, m_qnorm_b_g, m_sinks, m_w_out_b, v_norm_a_g, v_w_in_a, v_b_forget, v_qnorm_a_g, v_knorm_a_g, v_w_out_a, v_kv_norm_g, v_w_kv, v_knorm_b_g, v_norm_b_g, v_w_in_b, v_qnorm_b_g, v_sinks, v_w_out_b):
    w = dict(norm_a_g=norm_a_g, w_in_a=w_in_a, b_forget=b_forget, qnorm_a_g=qnorm_a_g, knorm_a_g=knorm_a_g,
             w_out_a=w_out_a, kv_norm_g=kv_norm_g, w_kv=w_kv, knorm_b_g=knorm_b_g, norm_b_g=norm_b_g,
             w_in_b=w_in_b, qnorm_b_g=qnorm_b_g, sinks=sinks, w_out_b=w_out_b)
    m = dict(norm_a_g=m_norm_a_g, w_in_a=m_w_in_a, b_forget=m_b_forget, qnorm_a_g=m_qnorm_a_g, knorm_a_g=m_knorm_a_g,
             w_out_a=m_w_out_a, kv_norm_g=m_kv_norm_g, w_kv=m_w_kv, knorm_b_g=m_knorm_b_g, norm_b_g=m_norm_b_g,
             w_in_b=m_w_in_b, qnorm_b_g=m_qnorm_b_g, sinks=m_sinks, w_out_b=m_w_out_b)
    v = dict(norm_a_g=v_norm_a_g, w_in_a=v_w_in_a, b_forget=v_b_forget, qnorm_a_g=v_qnorm_a_g, knorm_a_g=v_knorm_a_g,
             w_out_a=v_w_out_a, kv_norm_g=v_kv_norm_g, w_kv=v_w_kv, knorm_b_g=v_knorm_b_g, norm_b_g=v_norm_b_g,
             w_in_b=v_w_in_b, qnorm_b_g=v_qnorm_b_g, sinks=v_sinks, w_out_b=v_w_out_b)
    my_chip = 2 * lax.axis_index("x") + lax.axis_index("y")

    def shard2d(t, n):
        if n == "w_in_a":
            return jnp.transpose(t, (2, 0, 1)).reshape(-1)
        return t.reshape(t.shape[-2:])

    def unflat(t, n):
        return jnp.transpose(t.reshape(-1, 1, D), (1, 2, 0)) if n == "w_in_a" else t.reshape(w[n].shape)

    w2d = {n: shard2d(w[n], n) for n in BIG}

    norm_a_rows = jnp.broadcast_to(norm_a_g.reshape(1, D // NCHIP), (2 * SUBLANES, D // NCHIP))
    w1t, norm_rows = _gather_shards([w2d["w_in_a"].astype(BF16), norm_a_rows], [SPLIT["w_in_a"], 0])
    wt = {"w_in_a_t": w1t.reshape(-1, D), "norm_a_g": norm_rows[:, 0, :].reshape(1, D)}
    for n in SMALL[1:]:
        wt[n] = w[n]
    late_shards = [w2d[n].astype(BF16) for n in LATE]
    late_axes = [SPLIT[n] for n in LATE]
    transfers, outs, own = _gather_plan(late_shards, late_axes)
    fetch = _Riding(transfers, late_shards, outs, own)

    def late_weights(fetched):
        return {n: t if n == "w_in_b" else t.reshape(-1, t.shape[2]) for n, t in zip(LATE, fetched)}

    def as_blocks(t):
        if t.ndim == 3:
            return t
        return t.reshape(-1) if t.shape[0] % (SUBLANES * NCHIP) else t.reshape(NCHIP, -1, t.shape[1])

    def begin_reduce(grads, halves=None):
        names = list(grads)
        axes = [SPLIT[n] for n in names]
        blocks = [as_blocks(grads[n]) for n in names]
        if halves is None:
            transfers, outs = _halves_plan(blocks, axes)
            return _Riding(transfers, blocks, outs)
        sums = [_chip_sum(blk, part, ax, "chip_sum_" + n) for n, ax, blk, part in zip(names, axes, blocks, halves)]
        bf16 = [s[1] for s in sums]
        transfers, outs = _scatter_plan(bf16)
        return _Riding(transfers, bf16, outs), [s[0] for s in sums]

    sq, d_x, g, (chip_f32, arrived) = _local_step(x[0], loss_target[0], positions, wt, fetch, late_weights,
                                                  begin_reduce)

    axes = [SPLIT[n] for n in BIG]
    halves = [_mesh_sum(t32, parts, ax, "mesh_sum_" + n) for n, ax, t32, parts in zip(BIG, axes, chip_f32, arrived)]

    small_shapes = [(D,), (NH,), (HD,), (HD,), (D,), (HD,), (D,), (HD,), (NH,), (D,)]
    gathered_small, sibling_done = _last_exchange(_pack([g[n] for n in SMALL] + [sq]), halves)
    total = _sum_stack(gathered_small, "sum_small")
    small_g = dict(zip(SMALL, _unpack(total, small_shapes)[:-1]))
    loss = 0.5 * jnp.sum(_unpack(total, small_shapes)[-1]) / D
    small_g["norm_a_g"] = lax.dynamic_slice(small_g["norm_a_g"], (my_chip * (D // NCHIP),), (D // NCHIP,))

    res = {}
    for n, ax, mine_half, their_half in zip(BIG, axes, halves, sibling_done):
        out4 = _adamw_halves(w2d[n], mine_half, their_half, shard2d(m[n], n), shard2d(v[n], n), ax, "adamw_" + n)
        res[n] = tuple(unflat(t, n) for t in out4)
    row = lambda t: t.reshape(1, -1)
    small_out = _adamw_small(*[[row(d[n]) for n in SMALL] for d in (w, small_g, m, v)])
    for i, n in enumerate(SMALL):
        res[n] = tuple(t.reshape(w[n].shape) for t in (small_g[n],) + tuple(out[i] for out in small_out))

    outs = [loss, d_x[None]]
    for k in range(4):
        outs += [res[n][k] for n in NAMES]
    return tuple(outs)
```

```python
import numpy as np
import jax
import jax.numpy as jnp
from jax import lax
from jax.experimental import pallas as pl
from jax.experimental.pallas import tpu as pltpu

F32, BF16 = jnp.float32, jnp.bfloat16
S, D, HD, NH, NKV = 2048, 1024, 64, 16, 4
KVW = NKV * HD
WINDOW = 128
ROT = HD // 4
THETA = 500000.0
EPS = 1e-6
SCALE = HD ** -0.5
LANES = 128
SUBLANES = 8
NEG = -1e30
VMEM_LIMIT = 48 * 2 ** 20
ROWS = 512
ATT = 512
SWQ = 16
NCHIP = 4
ADAM_LR, ADAM_B1, ADAM_B2, ADAM_EPS, ADAM_WD, ADAM_STEP = 0.001, 0.9, 0.999, 1e-08, 0.01, 10
NT = (((1,), (1,)), ((), ()))
TN = (((0,), (0,)), ((), ()))
MESH = pl.DeviceIdType.MESH


def _params(n):
    return pltpu.CompilerParams(dimension_semantics=("arbitrary",) * n, vmem_limit_bytes=VMEM_LIMIT)


def _dot(a, b, dims=None):
    if dims is None:
        return jnp.dot(a, b, preferred_element_type=F32)
    return lax.dot_general(a, b, dims, preferred_element_type=F32)


def _dot_split(a, b, n):
    out, rest = None, a
    for _ in range(n):
        hi = rest.astype(BF16)
        term = _dot(hi, b)
        out = term if out is None else out + term
        rest = rest - hi.astype(F32)
    return out


def _seg_mat(w):
    e = (np.arange(w)[:, None] // HD == np.arange(LANES)[None, :]).astype(np.float32)
    return jnp.asarray(e, BF16)


def _spread(r, w):
    head = lax.broadcasted_iota(jnp.int32, (2 * LANES, w), 1) >> (HD.bit_length() - 1)
    row = lax.broadcasted_iota(jnp.int32, (2 * LANES, w), 0)
    et2 = jnp.where(head == (row & (LANES - 1)), 1.0, 0.0).astype(BF16)
    hi = r.astype(BF16)
    lo = (r - hi.astype(F32)).astype(BF16)
    return _dot(jnp.concatenate([hi, lo], axis=1), et2)


def _head_rstd(x, e):
    ss = _dot_split(x * x, e, 2)
    return _spread(lax.rsqrt(ss * (1.0 / HD) + EPS), x.shape[1])


def _rope(x, c, a, b):
    w = x.shape[1]
    return x * c + pltpu.roll(x, w - ROT // 2, 1) * a + pltpu.roll(x, ROT // 2, 1) * b


def _rope_t(dy, c, a, b):
    w = dy.shape[1]
    return dy * c + pltpu.roll(dy * b, w - ROT // 2, 1) + pltpu.roll(dy * a, ROT // 2, 1)


def _sigmoid(x):
    return 1.0 / (1.0 + jnp.exp(-x))


def _row_spec(shape, ts):
    nd = len(shape)
    if shape[0] == S:
        return pl.BlockSpec((ts,) + tuple(shape[1:]), lambda i: (i,) + (0,) * (nd - 1))
    return pl.BlockSpec(tuple(shape), lambda i: (0,) * nd)


def _rows_call(body, name, ins, outs, ts=ROWS):
    return pl.pallas_call(
        body, name=name, grid=(S // ts,),
        in_specs=[_row_spec(a.shape, ts) for a in ins],
        out_specs=[_row_spec(s, ts) for s, _ in outs],
        out_shape=[jax.ShapeDtypeStruct(s, d) for s, d in outs],
        compiler_params=_params(1))(*ins)


def _col_spec(ts, w, col):
    return pl.BlockSpec((ts, w), lambda i: (i, col))


TM = TN_ = 512
TM_TOKENS = 1024
TN_WIDE = 1024


def _mm(name, m, n, terms, out_dtype=F32, add=None, tm=None, tn=TN_, stacked=False, riding=None, rows_of=None):
    nterm = len(terms)
    if tm is None:
        tm = TM_TOKENS if m == S else TM
    nj, ni_ = n // tn, m // tm
    n_in = 2 * nterm + (add is not None) + (rows_of is not None and rows_of[0] is not None)
    r_in, r_out = (len(riding.ins), len(riding.outs)) if riding is not None else (0, 0)

    def body(*refs):
        if riding is not None:
            j, i = pl.program_id(0), pl.program_id(1)
            at_end = riding.hooks(refs[n_in:n_in + r_in], refs[n_in + r_in + 1:n_in + r_in + 1 + r_out],
                                  *refs[n_in + r_in + 1 + r_out:], first=(j == 0) & (i == 0),
                                  middle=(j == nj // 2) & (i == 0), last=(j == nj - 1) & (i == ni_ - 1))
        acc = None
        for t in range(nterm):
            part = _dot(refs[2 * t][...], refs[2 * t + 1][...], terms[t][4])
            acc = part if acc is None else acc + part
        if add is not None:
            acc = acc + refs[2 * nterm][...]
        refs[n_in + r_in][...] = acc.astype(out_dtype)
        if riding is not None:
            at_end()

    tile = pl.BlockSpec((tm, tn), lambda j, i: (i, j))
    ins, specs = [], []
    for a, a_spec, b, b_spec, _ in terms:
        ins += [a, b]
        specs += [a_spec, b_spec]
    if add is not None:
        ins.append(add)
        specs.append(tile)
    out_spec = pl.BlockSpec((None, tm, tn), lambda j, i: (j, i, 0)) if stacked else tile
    out_shape = jax.ShapeDtypeStruct((nj, m, tn) if stacked else (m, n), out_dtype)
    if rows_of is not None:
        taller, rows, row0 = rows_of
        out_spec = pl.BlockSpec((pl.Element(tm), pl.Element(tn)), lambda j, i: (
            pl.multiple_of(row0 + i * tm, SUBLANES), pl.multiple_of(j * tn, LANES)))
        out_shape = jax.ShapeDtypeStruct((rows, n), out_dtype)
        alias = {}
        if taller is not None:
            ins.append(taller)
            specs.append(pl.BlockSpec(memory_space=pltpu.HBM))
            alias = {len(ins) - 1: 0}
        return pl.pallas_call(body, name=name, grid=(nj, ni_), in_specs=specs, out_specs=out_spec,
                              out_shape=out_shape, input_output_aliases=alias, compiler_params=_params(2))(*ins)
    if riding is None:
        return pl.pallas_call(body, name=name, grid=(nj, ni_), in_specs=specs, out_specs=out_spec,
                              out_shape=out_shape, compiler_params=_params(2))(*ins)
    res = pl.pallas_call(
        body, name=name, grid=(nj, ni_), in_specs=specs + riding.in_specs,
        out_specs=[out_spec] + riding.out_specs, out_shape=[out_shape] + riding.out_shape,
        scratch_shapes=riding.scratch, compiler_params=_params(2))(*ins, *riding.ins)
    return res[0], res[1:]


def _a_rows(k, col=0, tm=TM_TOKENS):
    return pl.BlockSpec((tm, k), lambda j, i: (i, col))


def _a_cols(k, tm=TM):
    return pl.BlockSpec((k, tm), lambda j, i: (0, i))


def _b_cols(k, row=0, col0=0, tn=TN_):
    return pl.BlockSpec((k, tn), lambda j, i: (row, col0 + j))


def _b_rows(k, row0=0, tn=TN_):
    return pl.BlockSpec((tn, k), lambda j, i: (row0 + j, 0))


def _rmsnorm_fwd(x, gains, name):
    def body(*refs):
        xv = refs[0][...]
        r = lax.rsqrt(jnp.mean(xv * xv, axis=-1, keepdims=True) + EPS)
        xh = xv * r
        for n in range(len(gains)):
            refs[1 + len(gains) + n][...] = (xh * refs[1 + n][...]).astype(BF16)

    return _rows_call(body, name, [x] + list(gains), [((S, D), BF16)] * len(gains))


def _norm_bwd_tile(xv, gains, dus, dres, dg_refs):
    r = lax.rsqrt(jnp.mean(xv * xv, axis=-1, keepdims=True) + EPS)
    xh = xv * r
    gy = None
    for m, (gain, du) in enumerate(zip(gains, dus)):
        part = jnp.sum(du * xh, axis=0, keepdims=True)

        @pl.when(pl.program_id(0) == 0)
        def _(m=m, part=part):
            dg_refs[m][...] = part

        @pl.when(pl.program_id(0) != 0)
        def _(m=m, part=part):
            dg_refs[m][...] += part

        t = du * gain
        gy = t if gy is None else gy + t
    return dres + r * (gy - xh * jnp.mean(gy * xh, axis=-1, keepdims=True))


def _rmsnorm_bwd(x, gains, dus, dres, name):
    n = len(gains)

    def body(*refs):
        x_ref, g_refs, du_refs, dres_ref = refs[0], refs[1:1 + n], refs[1 + n:1 + 2 * n], refs[1 + 2 * n]
        dx_ref, dxb_ref, dg_refs = refs[2 + 2 * n], refs[3 + 2 * n], refs[4 + 2 * n:]
        dx = _norm_bwd_tile(x_ref[...], [g[...] for g in g_refs], [du[...] for du in du_refs], dres_ref[...], dg_refs)
        dx_ref[...] = dx
        dxb_ref[...] = dx.astype(BF16)

    outs = [((S, D), F32), ((S, D), BF16)] + [((1, D), F32)] * n
    return _rows_call(body, name, [x] + list(gains) + list(dus) + [dres], outs)


def _du_b_norms(d_q, d_gate, d_kv, w_in_b, w_kv, h, kv_g, norm_b, dres):
    half = D // 2

    def body(dq_ref, dgate_ref, dkv_ref, wb_ref, wkv_ref, h_ref, gk_ref, gb_ref, dres_ref,
             dx_ref, dxb_ref, dgk_ref, dgb_ref):
        pieces = (dq_ref[:, :half], dq_ref[:, half:], dgate_ref[:, :half], dgate_ref[:, half:])
        du_b = None
        for c in range(NCHIP):
            part = _dot(pieces[c], wb_ref[c], NT)
            du_b = part if du_b is None else du_b + part
        du_kv = _dot(dkv_ref[...], wkv_ref[...], NT)
        dx = _norm_bwd_tile(h_ref[...], [gk_ref[...], gb_ref[...]], [du_kv, du_b], dres_ref[...], [dgk_ref, dgb_ref])
        dx_ref[...] = dx
        dxb_ref[...] = dx.astype(BF16)

    whole = lambda a: pl.BlockSpec(a.shape, lambda i: (0,) * a.ndim)
    rows = lambda w: pl.BlockSpec((ROWS, w), lambda i: (i, 0))
    return pl.pallas_call(
        body, name="du_b_norms", grid=(S // ROWS,),
        in_specs=[rows(D), rows(D), rows(2 * KVW), whole(w_in_b), whole(w_kv), rows(D), whole(kv_g), whole(norm_b),
                  rows(D)],
        out_specs=[rows(D), rows(D), whole(kv_g), whole(norm_b)],
        out_shape=[jax.ShapeDtypeStruct((S, D), F32), jax.ShapeDtypeStruct((S, D), BF16),
                   jax.ShapeDtypeStruct((1, D), F32), jax.ShapeDtypeStruct((1, D), F32)],
        compiler_params=_params(1))(d_q, d_gate, d_kv, w_in_b, w_kv, h, kv_g, norm_b, dres)


def _proj_a(u, w1t, qg, kg):
    e = _seg_mat(D)
    gains = jnp.stack([qg * SCALE, kg])

    def body(u_ref, w_ref, g_ref, e_ref, raw_ref, out_ref):
        x = _dot(u_ref[...], w_ref[...], NT)
        raw_ref[...] = x

        @pl.when(pl.program_id(0) < 2)
        def _():
            out_ref[...] = (x * _head_rstd(x, e_ref[...]) * g_ref[...]).astype(BF16)

        @pl.when(pl.program_id(0) == 2)
        def _():
            out_ref[...] = x.astype(BF16)

    tm = TM_TOKENS
    return pl.pallas_call(
        body, name="proj_a", grid=(3, S // tm),
        in_specs=[pl.BlockSpec((tm, D), lambda j, i: (i, 0)), pl.BlockSpec((D, D), lambda j, i: (j, 0)),
                  pl.BlockSpec((None, 1, D), lambda j, i: (jnp.minimum(j, 1), 0, 0)),
                  pl.BlockSpec(e.shape, lambda j, i: (0, 0))],
        out_specs=[pl.BlockSpec((tm, D), lambda j, i: (i, j)), pl.BlockSpec((None, tm, D), lambda j, i: (j, i, 0))],
        out_shape=[jax.ShapeDtypeStruct((S, 3 * D), F32), jax.ShapeDtypeStruct((3, S, D), BF16)],
        compiler_params=_params(2))(u, w1t, gains, e)


def _tri(upper):
    r, c = np.arange(ROWS)[:, None], np.arange(ROWS)[None, :]
    return jnp.asarray((r <= c) if upper else (r >= c), BF16)


def _forget_cumsum(fpad, bpad):
    def body(f_ref, b_ref, u_ref, c_ref, carry):
        @pl.when(pl.program_id(0) == 0)
        def _():
            carry[...] = jnp.zeros_like(carry)

        lf = jax.nn.log_sigmoid(f_ref[...] + b_ref[...])
        blk = _dot_split(lf.T, u_ref[...], 3) + carry[:, 0:1]
        c_ref[...] = blk
        carry[...] = jnp.broadcast_to(blk[:, ROWS - 1:ROWS], carry.shape)

    return pl.pallas_call(
        body, name="forget_cumsum", grid=(S // ROWS,),
        in_specs=[pl.BlockSpec((ROWS, LANES), lambda i: (i, 0)), pl.BlockSpec((1, LANES), lambda i: (0, 0)),
                  pl.BlockSpec((ROWS, ROWS), lambda i: (0, 0))],
        out_specs=pl.BlockSpec((LANES, ROWS), lambda i: (0, i)),
        out_shape=jax.ShapeDtypeStruct((LANES, S), F32),
        scratch_shapes=[pltpu.VMEM((LANES, LANES), F32)],
        compiler_params=_params(1))(fpad, bpad, _tri(True))


def _forget_bwd(dct, fpad, bpad):
    nb = S // ROWS

    def body(dc_ref, f_ref, b_ref, l_ref, df_ref, db_ref, carry):
        @pl.when(pl.program_id(0) == 0)
        def _():
            carry[...] = jnp.zeros_like(carry)
            db_ref[...] = jnp.zeros_like(db_ref)

        blk = _dot_split(dc_ref[...], l_ref[...], 3) + carry[:, 0:1]
        carry[...] = jnp.broadcast_to(blk[:, 0:1], carry.shape)
        df = blk.T * _sigmoid(-(f_ref[...] + b_ref[...]))
        df_ref[...] = df.astype(BF16)
        db_ref[...] += jnp.sum(df, axis=0, keepdims=True)

    return pl.pallas_call(
        body, name="forget_bwd", grid=(nb,),
        in_specs=[pl.BlockSpec((LANES, ROWS), lambda i: (0, nb - 1 - i)),
                  pl.BlockSpec((ROWS, LANES), lambda i: (nb - 1 - i, 0)),
                  pl.BlockSpec((1, LANES), lambda i: (0, 0)), pl.BlockSpec((ROWS, ROWS), lambda i: (0, 0))],
        out_specs=[pl.BlockSpec((ROWS, LANES), lambda i: (nb - 1 - i, 0)), pl.BlockSpec((1, LANES), lambda i: (0, 0))],
        out_shape=[jax.ShapeDtypeStruct((S, LANES), BF16), jax.ShapeDtypeStruct((1, LANES), F32)],
        scratch_shapes=[pltpu.VMEM((LANES, LANES), F32)],
        compiler_params=_params(1))(dct, fpad, bpad, _tri(False))


def _headnorm_bwd(x, col, gain, dy, rope, name):
    e = _seg_mat(D)
    tabs = list(rope) if rope is not None else []

    def body(*refs):
        x_ref, g_ref, dy_ref, e_ref = refs[:4]
        dx_ref, dg_ref = refs[-2:]
        xv, dyv, ev = x_ref[...], dy_ref[...], e_ref[...]
        if rope is not None:
            c, a, b = (jnp.tile(t[...], (1, D // LANES)) for t in refs[4:7])
            dyv = _rope_t(dyv, c, a, b)
        r = _head_rstd(xv, ev)
        xh = xv * r
        part = jnp.sum(dyv * xh, axis=0, keepdims=True)

        @pl.when(pl.program_id(0) == 0)
        def _():
            dg_ref[...] = part

        @pl.when(pl.program_id(0) != 0)
        def _():
            dg_ref[...] += part

        gy = dyv * g_ref[...]
        seg = _spread(_dot_split(gy * xh, ev, 2) * (1.0 / HD), D)
        dx_ref[...] = (r * (gy - xh * seg)).astype(BF16)

    whole = lambda a: pl.BlockSpec(a.shape, lambda i: (0, 0))
    return pl.pallas_call(
        body, name=name, grid=(S // ROWS,),
        in_specs=[_col_spec(ROWS, D, col), whole(gain), _col_spec(ROWS, D, 0), whole(e)]
                 + [pl.BlockSpec((ROWS, LANES), lambda i: (i, 0))] * len(tabs),
        out_specs=[_col_spec(ROWS, D, 0), whole(gain)],
        out_shape=[jax.ShapeDtypeStruct((S, D), BF16), jax.ShapeDtypeStruct((1, D), F32)],
        compiler_params=_params(1))(x, gain, dy, e, *tabs)


def _dup_mat():
    r, c = np.arange(KVW)[:, None], np.arange(2 * KVW)[None, :]
    return (r // HD == c // LANES) & (r % HD == c % HD)


def _fold_mat():
    r, c = np.arange(D)[:, None], np.arange(KVW)[None, :]
    return (r // (2 * LANES) == c // HD) & (r % HD == c % HD)


def _proj_b(u_b, u_kv, w_in_b, w_kv, qg, kg, rope):
    e, ek = _seg_mat(D), _seg_mat(KVW)
    dup = jnp.asarray(_dup_mat(), BF16)

    def body(ub_ref, ukv_ref, wb_ref, wkv_ref, qg_ref, kg_ref, e_ref, ek_ref, dup_ref, c_ref, a_ref, b_ref,
             pb_ref, kv_ref, qo, ko, vo):
        ub = ub_ref[...]
        pb = jnp.concatenate([_dot(ub, wb_ref[c]) for c in range(NCHIP)], axis=1)
        kv = _dot(ukv_ref[...], wkv_ref[...])
        pb_ref[...] = pb
        kv_ref[...] = kv
        c1, a1, b1 = c_ref[...], a_ref[...], b_ref[...]
        qv = pb[:, :D]
        qn = qv * _head_rstd(qv, e_ref[...]) * qg_ref[...]
        t = lambda z, n: jnp.tile(z, (1, n))
        qo[...] = (_rope(qn, t(c1, D // LANES), t(a1, D // LANES), t(b1, D // LANES)) * SCALE).astype(BF16)
        kvv = kv[:, :KVW]
        kn = kvv * _head_rstd(kvv, ek_ref[...]) * kg_ref[...]
        kr = _rope(kn, t(c1, KVW // LANES), t(a1, KVW // LANES), t(b1, KVW // LANES)).astype(BF16)
        ko[...] = _dot(kr, dup_ref[...]).astype(BF16)
        vo[...] = _dot(kv[:, KVW:].astype(BF16), dup_ref[...]).astype(BF16)

    whole = lambda a: pl.BlockSpec(a.shape, lambda i: (0,) * a.ndim)
    rows = lambda w: pl.BlockSpec((ROWS, w), lambda i: (i, 0))
    return pl.pallas_call(
        body, name="proj_b", grid=(S // ROWS,),
        in_specs=[rows(D), rows(D), whole(w_in_b), whole(w_kv), whole(qg), whole(kg), whole(e), whole(ek),
                  whole(dup), rows(LANES), rows(LANES), rows(LANES)],
        out_specs=[rows(2 * D), rows(2 * KVW), rows(D), rows(2 * KVW), rows(2 * KVW)],
        out_shape=[jax.ShapeDtypeStruct((S, 2 * D), F32), jax.ShapeDtypeStruct((S, 2 * KVW), F32),
                   jax.ShapeDtypeStruct((S, D), BF16), jax.ShapeDtypeStruct((S, 2 * KVW), BF16),
                   jax.ShapeDtypeStruct((S, 2 * KVW), BF16)],
        compiler_params=_params(1))(u_b, u_kv, w_in_b, w_kv, qg, kg, e, ek, dup, *rope)


def _kv_bwd(dkdup, dvdup, kv, kg, rope):
    ek = _seg_mat(KVW)
    fold = jnp.asarray(_fold_mat(), BF16)

    def body(dk_ref, dv_ref, k_ref, kg_ref, ek_ref, fold_ref, c_ref, a_ref, b_ref, dkv_ref, dg_ref):
        ev, fv = ek_ref[...], fold_ref[...]
        t = lambda z: jnp.tile(z[...], (1, KVW // LANES))
        dk = _rope_t(_dot_split(dk_ref[...], fv, 2), t(c_ref), t(a_ref), t(b_ref))
        dv = _dot_split(dv_ref[...], fv, 2)
        xv = k_ref[...]
        r = _head_rstd(xv, ev)
        xh = xv * r
        part = jnp.sum(dk * xh, axis=0, keepdims=True)

        @pl.when(pl.program_id(0) == 0)
        def _():
            dg_ref[...] = part

        @pl.when(pl.program_id(0) != 0)
        def _():
            dg_ref[...] += part

        gy = dk * kg_ref[...]
        seg = _spread(_dot_split(gy * xh, ev, 2) * (1.0 / HD), KVW)
        dkv_ref[:, 0:KVW] = (r * (gy - xh * seg)).astype(BF16)
        dkv_ref[:, KVW:2 * KVW] = dv.astype(BF16)

    whole = lambda a: pl.BlockSpec(a.shape, lambda i: (0, 0))
    tab = pl.BlockSpec((ROWS, LANES), lambda i: (i, 0))
    return pl.pallas_call(
        body, name="kv_bwd", grid=(S // ROWS,),
        in_specs=[_col_spec(ROWS, D, 0), _col_spec(ROWS, D, 0), _col_spec(ROWS, KVW, 0),
                  whole(kg), whole(ek), whole(fold), tab, tab, tab],
        out_specs=[_col_spec(ROWS, 2 * KVW, 0), whole(kg)],
        out_shape=[jax.ShapeDtypeStruct((S, 2 * KVW), BF16), jax.ShapeDtypeStruct((1, KVW), F32)],
        compiler_params=_params(1))(dkdup, dvdup, kv, kg, ek, fold, *rope)


def _out_norms(y, w_out, residual, gains):
    n = len(gains)

    def body(y_ref, w_ref, r_ref, *refs):
        h = _dot(y_ref[...], w_ref[...]) + r_ref[...]
        refs[n][...] = h
        hn = h * lax.rsqrt(jnp.mean(h * h, axis=-1, keepdims=True) + EPS)
        for k in range(n):
            refs[n + 1 + k][...] = (hn * refs[k][...]).astype(BF16)

    rows = pl.BlockSpec((TM_TOKENS, D), lambda i: (i, 0))
    whole = pl.BlockSpec((D, D), lambda i: (0, 0))
    gain = pl.BlockSpec((1, D), lambda i: (0, 0))
    return pl.pallas_call(
        body, name="out_a_norms", grid=(S // TM_TOKENS,), in_specs=[rows, whole, rows] + [gain] * n,
        out_specs=[rows] * (n + 1),
        out_shape=[jax.ShapeDtypeStruct((S, D), F32)] + [jax.ShapeDtypeStruct((S, D), BF16)] * n,
        compiler_params=_params(1))(y, w_out, residual, *gains)


def _out_loss(y, w_out, residual, target):
    def body(y_ref, w_ref, r_ref, t_ref, d_ref, db_ref, l_ref):
        diff = _dot(y_ref[...], w_ref[...]) + r_ref[...] - t_ref[...]
        d = diff * (1.0 / D)
        d_ref[...] = d
        db_ref[...] = d.astype(BF16)

        @pl.when(pl.program_id(0) == 0)
        def _():
            l_ref[...] = jnp.zeros_like(l_ref)

        l_ref[...] += jnp.sum(diff * diff, axis=0, keepdims=True)

    rows = pl.BlockSpec((TM_TOKENS, D), lambda i: (i, 0))
    whole = pl.BlockSpec((D, D), lambda i: (0, 0))
    return pl.pallas_call(
        body, name="out_b_loss", grid=(S // TM_TOKENS,), in_specs=[rows, whole, rows, rows],
        out_specs=[rows, rows, pl.BlockSpec((1, D), lambda i: (0, 0))],
        out_shape=[jax.ShapeDtypeStruct((S, D), F32), jax.ShapeDtypeStruct((S, D), BF16),
                   jax.ShapeDtypeStruct((1, D), F32)],
        compiler_params=_params(1))(y, w_out, residual, target)


def _lane():
    return lax.broadcasted_iota(jnp.int32, (1, LANES), 1)


def _head_mask(hh):
    return (_lane() < HD) if hh == 0 else (_lane() >= HD)


def _qkv_specs():
    return (pl.BlockSpec((None, ATT, LANES), lambda p, i: (0, i, p)),
            pl.BlockSpec((None, S, LANES), lambda p, i: (1, 0, p)),
            pl.BlockSpec((None, S, LANES), lambda p, i: (2, 0, p)))


def _fox_fwd(qkv, ct, gate, riding):
    nq, npair = S // ATT, NH // 2
    ni, no = len(riding.ins), len(riding.outs)

    def body(q_ref, k_ref, v_ref, c_ref, gate_ref, *rest):
        o_ref, lse_ref, y_ref = rest[ni:ni + 3]
        pair, i = pl.program_id(0), pl.program_id(1)
        at_end = riding.hooks(rest[:ni], rest[ni + 3:ni + 3 + no], *rest[ni + 3 + no:],
                              first=(pair == 0) & (i == 0), middle=(pair == npair // 2) & (i == 0),
                              last=(pair == npair - 1) & (i == nq - 1))
        q2 = q_ref[...]
        qms = [jnp.where(_head_mask(hh), q2, jnp.zeros_like(q2)) for hh in (0, 1)]

        def probs(off, width, m, hh, diag):
            s = _dot(qms[hh], k_ref[pl.ds(off, width), :], NT) - c_ref[hh:hh + 1, pl.ds(off, width)]
            if diag:
                row = i * ATT + lax.broadcasted_iota(jnp.int32, (ATT, width), 0)
                col = off + lax.broadcasted_iota(jnp.int32, (ATT, width), 1)
                s = jnp.where(col <= row, s, NEG)
            m_new = jnp.maximum(m, jnp.max(s, axis=1, keepdims=True))
            p = jnp.exp(s - m_new)
            p_hi = p.astype(BF16)
            return m_new, jnp.exp(m - m_new), p_hi, (p - p_hi.astype(F32)).astype(BF16)

        def weighted(off, width, p_hi, p_lo, hh):
            vj = v_ref[pl.ds(off, width), :]
            v1 = jnp.where(_head_mask(hh), vj, jnp.ones_like(vj))
            return _dot(p_hi, v1) + _dot(p_lo, v1)

        def step(off, width, carry, diag):
            off = pl.multiple_of(off, ATT)
            out = []
            for hh in (0, 1):
                m, acc = carry[hh]
                m, alpha, p_hi, p_lo = probs(off, width, m, hh, diag)
                out.append((m, alpha * acc + weighted(off, width, p_hi, p_lo, hh)))
            return tuple(out)

        one = (jnp.full((ATT, 1), NEG, F32), jnp.zeros((ATT, LANES), F32))
        carry = lax.fori_loop(0, i // 2, lambda j, cr: step(j * (2 * ATT), 2 * ATT, cr, False), (one, one))
        carry = lax.cond(i % 2 == 1, lambda cr: step((i - 1) * ATT, 2 * ATT, cr, True),
                         lambda cr: step(i * ATT, ATT, cr, True), carry)
        res = []
        for hh in (0, 1):
            m, acc = carry[hh]
            l = jnp.max(jnp.where(_head_mask(1 - hh), acc, 0.0), axis=1, keepdims=True)
            res.append((acc / l, m + jnp.log(l)))
        first = _head_mask(0)
        o = jnp.where(first, res[0][0], res[1][0])
        o_ref[...] = o
        lse_ref[...] = jnp.where(first, res[0][1], res[1][1])
        g = gate_ref[...]
        y_ref[...] = (o * (g * _sigmoid(g))).astype(BF16)
        at_end()

    blk = pl.BlockSpec((ATT, LANES), lambda p, i: (i, p))
    res = pl.pallas_call(
        body, name="fox_fwd", grid=(npair, nq),
        in_specs=[*_qkv_specs(), pl.BlockSpec((None, 2, S), lambda p, i: (p, 0, 0)), blk] + riding.in_specs,
        out_specs=[blk, blk, blk] + riding.out_specs,
        out_shape=[jax.ShapeDtypeStruct((S, D), F32)] * 2 + [jax.ShapeDtypeStruct((S, D), BF16)] + riding.out_shape,
        scratch_shapes=riding.scratch,
        compiler_params=_params(2))(qkv, qkv, qkv, ct, gate, *riding.ins)
    return res[0], res[1], res[2], res[3:]


def _gate_grads(dy, o, g):
    sg = _sigmoid(g)
    return dy * (g * sg), dy * o * (sg * (1.0 + g * (1.0 - sg)))


def _fox_bwd(qkv, ct, o, lse, dy, gate, riding):
    nq, npair = S // ATT, NH // 2
    ni, no = len(riding.ins), len(riding.outs)

    def body(q_ref, k_ref, v_ref, c_ref, o_ref, lse_ref, dy_ref, gate_ref, *rest):
        dq_ref, dk_ref, dvb_ref, dc_ref, dgate_ref = rest[ni:ni + 5]
        dv_ref = rest[ni + 5 + no]
        pair, i = pl.program_id(0), pl.program_id(1)
        at_end = riding.hooks(rest[:ni], rest[ni + 5:ni + 5 + no], *rest[ni + 6 + no:],
                              first=(pair == 0) & (i == 0), middle=(pair == npair // 2) & (i == 0),
                              last=(pair == npair - 1) & (i == nq - 1))

        @pl.when(i == 0)
        def _():
            dk_ref[...] = jnp.zeros_like(dk_ref)
            dv_ref[...] = jnp.zeros_like(dv_ref)
            dc_ref[...] = jnp.zeros_like(dc_ref)

        q2, lse2 = q_ref[...], lse_ref[...]
        do2, dgate = _gate_grads(dy_ref[...], o_ref[...], gate_ref[...])
        dgate_ref[...] = dgate.astype(BF16)
        do2b = do2.astype(BF16)
        prod = do2b.astype(F32) * o_ref[...]
        heads = []
        for hh in (0, 1):
            hm = _head_mask(hh)
            heads.append((jnp.where(hm, q2, jnp.zeros_like(q2)), jnp.where(hm, do2b, jnp.zeros_like(do2b)),
                          jnp.sum(jnp.where(hm, prod, 0.0), axis=1, keepdims=True),
                          jnp.max(jnp.where(hm, lse2, NEG), axis=1, keepdims=True)))

        def step(off, width, dqs, diag):
            off = pl.multiple_of(off, ATT)
            kj, vj = k_ref[pl.ds(off, width), :], v_ref[pl.ds(off, width), :]
            dk, dv, out = None, None, []
            for hh in (0, 1):
                qm, dom, delta, lse_h = heads[hh]
                s = _dot(qm, kj, NT) - c_ref[hh:hh + 1, pl.ds(off, width)]
                p = jnp.exp(s - lse_h)
                if diag:
                    row = i * ATT + lax.broadcasted_iota(jnp.int32, (ATT, width), 0)
                    col = off + lax.broadcasted_iota(jnp.int32, (ATT, width), 1)
                    p = jnp.where(col <= row, p, 0.0)
                ds = p * (_dot(dom, vj, NT) - delta)
                dc_ref[hh:hh + 1, pl.ds(off, width)] += -jnp.sum(ds, axis=0, keepdims=True)
                dsb = ds.astype(BF16)
                dk_h, dv_h = _dot(dsb, qm, TN), _dot(p.astype(BF16), dom, TN)
                dk, dv = (dk_h, dv_h) if dk is None else (dk + dk_h, dv + dv_h)
                out.append(dqs[hh] + _dot(dsb, kj))
            dk_ref[pl.ds(off, width), :] += dk
            dv_ref[pl.ds(off, width), :] += dv
            return tuple(out)

        zero = jnp.zeros((ATT, LANES), F32)
        dqs = lax.fori_loop(0, i // 2, lambda j, acc: step(j * (2 * ATT), 2 * ATT, acc, False), (zero, zero))
        dqs = lax.cond(i % 2 == 1, lambda acc: step((i - 1) * ATT, 2 * ATT, acc, True),
                       lambda acc: step(i * ATT, ATT, acc, True), dqs)
        dq_ref[...] = jnp.where(_head_mask(0), dqs[0], dqs[1]) * SCALE

        @pl.when(i == nq - 1)
        def _():
            dvb_ref[...] = dv_ref[...].astype(BF16)

        at_end()

    blk = pl.BlockSpec((ATT, LANES), lambda p, i: (i, p))
    full = pl.BlockSpec((S, LANES), lambda p, i: (0, p))
    cspec = pl.BlockSpec((None, 2, S), lambda p, i: (p, 0, 0))
    res = pl.pallas_call(
        body, name="fox_bwd", grid=(npair, nq),
        in_specs=[*_qkv_specs(), cspec, blk, blk, blk, blk] + riding.in_specs,
        out_specs=[blk, full, full, cspec, blk] + riding.out_specs,
        out_shape=[jax.ShapeDtypeStruct((S, D), F32)] * 2 + [jax.ShapeDtypeStruct((S, D), BF16),
                                                              jax.ShapeDtypeStruct((npair, 2, S), F32),
                                                              jax.ShapeDtypeStruct((S, D), BF16)]
                  + riding.out_shape,
        scratch_shapes=[pltpu.VMEM((S, LANES), F32)] + riding.scratch,
        compiler_params=_params(2))(qkv, qkv, qkv, ct, o, lse, dy, gate, *riding.ins)
    return res[0], res[1], res[2], res[3], res[4], res[5:]


def _both_heads(x):
    return jnp.concatenate([jnp.where(_head_mask(hh), x, jnp.zeros_like(x)) for hh in (0, 1)], axis=0)


def _per_head(col0, col1):
    return jnp.concatenate([jnp.broadcast_to(col0, (WINDOW, 1)), jnp.broadcast_to(col1, (WINDOW, 1))], axis=0)


def _unstack(x2):
    return jnp.where(_head_mask(0), x2[:WINDOW], x2[WINDOW:])


def _swa_valid(i, start):
    r = lax.broadcasted_iota(jnp.int32, (2 * WINDOW, 2 * WINDOW), 0)
    qabs = i * WINDOW + jnp.where(r >= WINDOW, r - WINDOW, r)
    kabs = start + lax.broadcasted_iota(jnp.int32, (2 * WINDOW, 2 * WINDOW), 1)
    return (kabs <= qabs) & (qabs - kabs < WINDOW)


def _swa_fwd(q, kdup, vdup, sinks_t, proj, gate_col):
    def body(q_ref, k_ref, v_ref, sk_ref, gate_ref, o_ref, lse_ref, y_ref):
        skv = sk_ref[...]
        first = _head_mask(0)
        for sb in range(SWQ):
            i = pl.program_id(1) * SWQ + sb
            rows = slice(sb * WINDOW, (sb + 1) * WINDOW)
            start = pl.multiple_of(jnp.maximum(i - 1, 0) * WINDOW, WINDOW)
            kk, vv = k_ref[pl.ds(start, 2 * WINDOW), :], v_ref[pl.ds(start, 2 * WINDOW), :]
            q2 = q_ref[rows, :]
            valid = _swa_valid(i, start)[:WINDOW]
            res = []
            for hh in (0, 1):
                hm = _head_mask(hh)
                sink = jnp.max(jnp.where(hm, skv, NEG), axis=1, keepdims=True)
                s = jnp.where(valid, _dot(jnp.where(hm, q2, jnp.zeros_like(q2)), kk, NT), NEG)
                m = jnp.maximum(jnp.max(s, axis=1, keepdims=True), sink)
                p = jnp.exp(s - m)
                l = jnp.sum(p, axis=1, keepdims=True) + jnp.exp(sink - m)
                res.append((_dot(p.astype(BF16), vv) / l, m + jnp.log(l)))
            o = jnp.where(first, res[0][0], res[1][0])
            o_ref[rows, :] = o
            lse_ref[rows, :] = jnp.where(first, res[0][1], res[1][1])
            g = gate_ref[rows, :]
            y_ref[rows, :] = (o * (g * _sigmoid(g))).astype(BF16)

    blk = pl.BlockSpec((SWQ * WINDOW, LANES), lambda p, i: (i, p))
    gate = pl.BlockSpec((SWQ * WINDOW, LANES), lambda p, i: (i, gate_col + p))
    full = pl.BlockSpec((S, LANES), lambda p, i: (0, p // 2))
    return pl.pallas_call(
        body, name="swa_fwd", grid=(NH // 2, S // (SWQ * WINDOW)),
        in_specs=[blk, full, full, pl.BlockSpec((1, LANES), lambda p, i: (0, p)), gate],
        out_specs=[blk, blk, blk],
        out_shape=[jax.ShapeDtypeStruct((S, D), F32)] * 2 + [jax.ShapeDtypeStruct((S, D), BF16)],
        compiler_params=_params(2))(q, kdup, vdup, sinks_t, proj)


def _swa_bwd(q, kdup, vdup, sinks_t, o, lse, dy, proj, gate_col):
    def body(q_ref, k_ref, v_ref, sk_ref, o_ref, lse_ref, dy_ref, gate_ref, dq_ref, dk_ref, dv_ref, dsk_ref,
             dgate_ref):
        @pl.when(pl.program_id(1) == 0)
        def _():
            dk_ref[...] = jnp.zeros_like(dk_ref)
            dv_ref[...] = jnp.zeros_like(dv_ref)
            dsk_ref[...] = jnp.zeros_like(dsk_ref)

        skv = sk_ref[...]
        first = _head_mask(0)
        sink = _per_head(*[jnp.max(jnp.where(_head_mask(hh), skv, NEG), axis=1, keepdims=True) for hh in (0, 1)])
        for sb in range(SWQ):
            i = pl.program_id(1) * SWQ + sb
            rows = slice(sb * WINDOW, (sb + 1) * WINDOW)
            start = pl.multiple_of(jnp.maximum(i - 1, 0) * WINDOW, WINDOW)
            kk, vv = k_ref[pl.ds(start, 2 * WINDOW), :], v_ref[pl.ds(start, 2 * WINDOW), :]
            do2, dgate = _gate_grads(dy_ref[rows, :], o_ref[rows, :], gate_ref[rows, :])
            dgate_ref[rows, :] = dgate.astype(BF16)
            do2b = do2.astype(BF16)
            prod, lse2 = do2b.astype(F32) * o_ref[rows, :], lse_ref[rows, :]
            qs, dos = _both_heads(q_ref[rows, :]), _both_heads(do2b)
            delta = jnp.concatenate([jnp.sum(jnp.where(_head_mask(hh), prod, 0.0), axis=1, keepdims=True)
                                     for hh in (0, 1)], axis=0)
            lse_h = jnp.concatenate([jnp.max(jnp.where(_head_mask(hh), lse2, NEG), axis=1, keepdims=True)
                                     for hh in (0, 1)], axis=0)
            p = jnp.where(_swa_valid(i, start), jnp.exp(_dot(qs, kk, NT) - lse_h), 0.0)
            dsb = (p * (_dot(dos, vv, NT) - delta)).astype(BF16)
            dk_ref[pl.ds(start, 2 * WINDOW), :] += _dot(dsb, qs, TN)
            dv_ref[pl.ds(start, 2 * WINDOW), :] += _dot(p.astype(BF16), dos, TN)
            dq_ref[rows, :] = _unstack(_dot(dsb, kk)) * SCALE
            t = jnp.exp(sink - lse_h) * delta
            dsk_ref[...] += -jnp.where(first, jnp.sum(t[:WINDOW], axis=0, keepdims=True),
                                       jnp.sum(t[WINDOW:], axis=0, keepdims=True))

    blk = pl.BlockSpec((SWQ * WINDOW, LANES), lambda p, i: (i, p))
    full = pl.BlockSpec((S, LANES), lambda p, i: (0, p // 2))
    acc = pl.BlockSpec((S, LANES), lambda p, i: (0, p))
    sk = pl.BlockSpec((1, LANES), lambda p, i: (0, p))
    gate = pl.BlockSpec((SWQ * WINDOW, LANES), lambda p, i: (i, gate_col + p))
    return pl.pallas_call(
        body, name="swa_bwd", grid=(NH // 2, S // (SWQ * WINDOW)),
        in_specs=[blk, full, full, sk, blk, blk, blk, gate],
        out_specs=[blk, acc, acc, sk, blk],
        out_shape=[jax.ShapeDtypeStruct((S, D), F32)] * 3 + [jax.ShapeDtypeStruct((1, D), F32),
                                                              jax.ShapeDtypeStruct((S, D), BF16)],
        compiler_params=_params(2))(q, kdup, vdup, sinks_t, o, lse, dy, proj)


def _adamw_math(w, g, m, v):
    m = ADAM_B1 * m + (1.0 - ADAM_B1) * g
    v = ADAM_B2 * v + (1.0 - ADAM_B2) * jnp.square(g)
    m_hat = m / (1.0 - ADAM_B1 ** ADAM_STEP)
    v_hat = v / (1.0 - ADAM_B2 ** ADAM_STEP)
    delta = -ADAM_LR * (m_hat / (jnp.sqrt(v_hat) + ADAM_EPS) + ADAM_WD * w)
    return delta, m, v


def _adamw_small(ws, gs, ms, vs):
    k = len(ws)

    def body(*refs):
        for p in range(k):
            w_ref, g_ref, m_ref, v_ref = (refs[q * k + p] for q in range(4))
            d, mo, vo = _adamw_math(w_ref[...], g_ref[...], m_ref[...], v_ref[...])
            refs[4 * k + p][...], refs[5 * k + p][...], refs[6 * k + p][...] = d, mo, vo

    res = pl.pallas_call(
        body, name="adamw_small",
        out_shape=[jax.ShapeDtypeStruct(t.shape, F32) for t in ws] * 3)(*ws, *gs, *ms, *vs)
    return res[:k], res[k:2 * k], res[2 * k:]


SUM_TILES = (512, 256, 128)


FLAT_BLOCK = 257 * 1024


def _tiles(shape, axis, lead=0, halves=False):
    if len(shape) == 1:
        count = shape[0] // FLAT_BLOCK
        return (FLAT_BLOCK,), count, lambda pos, *lead_idx: (sum(k * count for k in lead_idx) + pos,)
    r, c = shape
    tile = next(t for t in SUM_TILES if (shape[axis] // (2 if halves else 1)) % t == 0)
    blk = (tile, c) if axis == 0 else (r, tile)
    count = shape[axis] // tile

    def index(pos, *lead_idx):
        return tuple(lead_idx) + ((pos, 0) if axis == 0 else (0, pos))

    return (None,) * lead + blk, count, index


def _adamw_halves(w, g_mine, g_theirs, m, v, axis, name):
    blk, count, index = _tiles(w.shape, axis, halves=True)
    per_half = count // 2

    def body(w_ref, a_ref, b_ref, m_ref, v_ref, g_ref, d_ref, mo_ref, vo_ref):
        is_mine = pl.program_id(0) // per_half == lax.axis_index("c")
        g = jnp.where(is_mine, a_ref[...], b_ref[...])
        g_ref[...] = g
        d_ref[...], mo_ref[...], vo_ref[...] = _adamw_math(w_ref[...], g, m_ref[...], v_ref[...])

    spec = pl.BlockSpec(blk, lambda i: index(i))
    half = pl.BlockSpec(blk, lambda i: index(i % per_half))
    return pl.pallas_call(
        body, name=name, grid=(count,), in_specs=[spec, half, half, spec, spec], out_specs=[spec] * 4,
        out_shape=[jax.ShapeDtypeStruct(w.shape, F32)] * 4, compiler_params=_params(1))(w, g_mine, g_theirs, m, v)


def _chip_sum(blocks, from_sibling, axis, name):
    flat = blocks.ndim == 1
    blk, count, index = _tiles((from_sibling.shape[0] // NCHIP,) if flat else from_sibling.shape[1:], axis, lead=1)

    def body(lo_ref, hi_ref, p_ref, o32, o16):
        mine = jnp.where(lax.axis_index("c") == 0, lo_ref[...], hi_ref[...])
        acc = mine + p_ref[...]
        o32[...] = acc
        o16[...] = acc.astype(BF16)

    half = pl.BlockSpec(blk, lambda k, i: index(i, k))
    if flat:
        lo = pl.BlockSpec(blk, lambda k, i: (2 * count * k + i,))
        hi = pl.BlockSpec(blk, lambda k, i: (2 * count * k + count + i,))
    else:
        lo, hi = half, pl.BlockSpec(blk, lambda k, i: index(i + count, k))
    return pl.pallas_call(
        body, name=name, grid=(NCHIP, count), in_specs=[lo, hi, half], out_specs=[half, half],
        out_shape=[jax.ShapeDtypeStruct(from_sibling.shape, F32), jax.ShapeDtypeStruct(from_sibling.shape, BF16)],
        compiler_params=_params(2))(blocks, blocks, from_sibling)


def _mesh_sum(chip_sums, parts, axis, name):
    flat = chip_sums.ndim == 1
    one = (chip_sums.shape[0] // NCHIP,) if flat else chip_sums.shape[1:]
    blk, count, index = _tiles(one, axis)
    n = NCHIP - 1

    def body(chip_ref, a_ref, *refs):
        acc = a_ref[...]
        for k in range(n):
            acc = acc + refs[k][...].astype(F32)
        refs[n][...] = acc

    spec = pl.BlockSpec(blk, lambda i, chip: index(i))
    if flat:
        mine = pl.BlockSpec(blk, lambda i, chip: (chip[0] * count + i,))
        part = [pl.BlockSpec(blk, lambda i, chip, k=k: (k * count + i,)) for k in range(n)]
    else:
        mine = pl.BlockSpec((None,) + blk, lambda i, chip: (chip[0],) + index(i))
        part = [pl.BlockSpec((None,) + blk, lambda i, chip, k=k: (k,) + index(i)) for k in range(n)]
    return pl.pallas_call(
        body, name=name,
        grid_spec=pltpu.PrefetchScalarGridSpec(num_scalar_prefetch=1, grid=(count,), in_specs=[mine] + part,
                                               out_specs=spec),
        out_shape=jax.ShapeDtypeStruct(one, F32),
        compiler_params=_params(1))(_chip(_coords()).astype(jnp.int32).reshape(1), chip_sums, *([parts] * n))


def _sum_stack(parts, name):
    n = parts.shape[0]

    def body(p_ref, o_ref):
        acc = p_ref[0]
        for k in range(1, n):
            acc = acc + p_ref[k]
        o_ref[...] = acc

    return pl.pallas_call(body, name=name, out_shape=jax.ShapeDtypeStruct(parts.shape[1:], F32))(parts)


def _coords():
    return lax.axis_index("x"), lax.axis_index("y"), lax.axis_index("c")


def _chip(who):
    return 2 * who[0] + who[1]


def _flip(who, mask):
    return tuple((1 - v) if b else v for v, b in zip(who, mask))


def _transfer(transfers, t, I, O, ssem, rsem, receiving):
    tr, me = transfers[t], _coords()
    peer = _flip(me, tr["mask"])
    return pltpu.make_async_remote_copy(
        src_ref=tr["src"](I, O, me), dst_ref=tr["dst"](I, O, peer if receiving else me),
        send_sem=ssem.at[t], recv_sem=rsem.at[t], device_id=peer, device_id_type=MESH)


def _start_transfers(transfers, I, O, ssem, rsem, onward):
    arrived = set()
    for t, tr in enumerate(transfers):
        after = tr.get("after")
        if (after is not None) != onward:
            continue
        if after is not None and after not in arrived:
            _transfer(transfers, after, I, O, ssem, rsem, True).wait_recv()
            arrived.add(after)
        _transfer(transfers, t, I, O, ssem, rsem, False).start()


def _finish_transfers(transfers, I, O, ssem, rsem):
    passed_on = {tr["after"] for tr in transfers if tr.get("after") is not None}
    for t in range(len(transfers)):
        if t not in passed_on:
            _transfer(transfers, t, I, O, ssem, rsem, True).wait_recv()
    for t in range(len(transfers)):
        _transfer(transfers, t, I, O, ssem, rsem, False).wait_send()


def _own_copies(own, I, O, stage, lsem, leg):
    for n, (src, dst) in enumerate(own):
        me = _coords()
        bring =pltpu.make_async_copy(src(I, O, me), stage[n], lsem.at[2 * n])
        put = pltpu.make_async_copy(stage[n], dst(I, O, me), lsem.at[2 * n + 1])
        if leg == 0:
            bring.start()
        elif leg == 1:
            bring.wait()
            put.start()
        else:
            put.wait()


def _own_scratch(own, ins):
    return [pltpu.VMEM(ins[n].shape, ins[n].dtype) for n in range(len(own))], pltpu.SemaphoreType.DMA((max(2 * len(own), 1),))


def _exchange(name, ins, outs, transfers, own=()):
    ni, no = len(ins), len(outs)
    nt = len(transfers)
    stages, stage_sems = _own_scratch(own, ins)

    def body(*refs):
        I, O = refs[:ni], refs[ni:ni + no]
        ssem, rsem, lsem = refs[ni + no:ni + no + 3]
        stage = refs[ni + no + 3:]
        _own_copies(own, I, O, stage, lsem, 0)
        _start_transfers(transfers, I, O, ssem, rsem, False)
        _own_copies(own, I, O, stage, lsem, 1)
        _start_transfers(transfers, I, O, ssem, rsem, True)
        _finish_transfers(transfers, I, O, ssem, rsem)
        _own_copies(own, I, O, stage, lsem, 2)

    hbm = pl.BlockSpec(memory_space=pltpu.HBM)
    return pl.pallas_call(
        body, name=name, in_specs=[hbm] * ni, out_specs=[hbm] * no,
        out_shape=[jax.ShapeDtypeStruct(s, d) for s, d in outs],
        scratch_shapes=[pltpu.SemaphoreType.DMA((nt,)), pltpu.SemaphoreType.DMA((nt,)), stage_sems] + stages,
        compiler_params=pltpu.CompilerParams(has_side_effects=True, vmem_limit_bytes=VMEM_LIMIT))(*ins)


CHIP_MASKS = [(0, 1, 0), (1, 0, 0), (1, 1, 0)]
SIBLING = (0, 0, 1)


def _half(shape2d, axis, which):
    n = shape2d[axis] // 2
    cut = pl.ds(pl.multiple_of(which * n, n), n)
    return (cut, slice(None)) if axis == 0 else (slice(None), cut)


class _Riding:
    def __init__(self, transfers, ins, outs, own=()):
        self.transfers, self.ins, self.outs, self.own = transfers, list(ins), list(outs), list(own)
        hbm = pl.BlockSpec(memory_space=pltpu.HBM)
        self.in_specs, self.out_specs = [hbm] * len(self.ins), [hbm] * len(self.outs)
        self.out_shape = [jax.ShapeDtypeStruct(s, d) for s, d in self.outs]
        stages, stage_sems = _own_scratch(self.own, self.ins)
        self.scratch = [pltpu.SemaphoreType.DMA((max(len(transfers), 1),))] * 2 + [stage_sems] + stages

    def alone(self, name):
        return _exchange(name, self.ins, self.outs, self.transfers, self.own)

    def hooks(self, I, O, ssem, rsem, lsem, *stage, first, middle, last):
        tr, own = self.transfers, self.own

        @pl.when(first)
        def _():
            _own_copies(own, I, O, stage, lsem, 0)
            _start_transfers(tr, I, O, ssem, rsem, False)

        if own or any(t.get("after") is not None for t in tr):
            @pl.when(middle)
            def _():
                _own_copies(own, I, O, stage, lsem, 1)
                _start_transfers(tr, I, O, ssem, rsem, True)

        def at_end():
            @pl.when(last)
            def _():
                _finish_transfers(tr, I, O, ssem, rsem)
                _own_copies(own, I, O, stage, lsem, 2)

        return at_end


def _stretch(n, pos):
    return (pl.ds(pos * n if isinstance(pos, int) else pl.multiple_of(pos * n, n), n),)


def _gather_plan(shards, axes):
    def half(a, who):
        if shards[a].ndim == 1:
            return _stretch(shards[a].shape[0] // 2, who[2])
        return _half(shards[a].shape, axes[a], who[2])

    def landed(a, chip, who):
        if shards[a].ndim == 1:
            return _stretch(shards[a].shape[0] // 2, 2 * chip + who[2])
        return (chip,) + half(a, who)

    over_ici, onward = [], []
    for a in range(len(shards)):
        for mask in CHIP_MASKS:
            over_ici.append(dict(
                mask=mask,
                src=lambda I, O, me, a=a: I[a].at[half(a, me)],
                dst=lambda I, O, who, a=a: O[a].at[landed(a, _chip(who), who)]))
            onward.append(dict(
                mask=SIBLING, after=len(over_ici) - 1,
                src=lambda I, O, me, a=a, mask=mask: O[a].at[landed(a, _chip(_flip(me, mask)), me)],
                dst=lambda I, O, who, a=a, mask=mask: O[a].at[landed(a, _chip(_flip(who, mask)), who)]))
    outs = [((NCHIP * s.shape[0],) if s.ndim == 1 else (NCHIP,) + s.shape, s.dtype) for s in shards]

    def whole(a, chip):
        return _stretch(shards[a].shape[0], chip) if shards[a].ndim == 1 else (chip,)

    own = [(lambda I, O, me, a=a: I[a], lambda I, O, me, a=a: O[a].at[whole(a, _chip(me))])
           for a in range(len(shards))]
    return over_ici + onward, outs, own


def _gather_shards(shards, axes):
    transfers, outs, own = _gather_plan(shards, axes)
    return _exchange("gather_weights", shards, outs, transfers, own)


def _halves_plan(blocks, axes):
    def cut(a, which):
        return (slice(None),) + _half(blocks[a].shape[1:], axes[a], which)

    transfers, outs = [], []
    for a, (b, ax) in enumerate(zip(blocks, axes)):
        if b.ndim == 1:
            h = b.shape[0] // NCHIP // 2
            for k in range(NCHIP):
                transfers.append(dict(mask=SIBLING,
                                      src=lambda I, O, me, a=a, k=k, h=h: I[a].at[_stretch(h, 2 * k + 1 - me[2])],
                                      dst=lambda I, O, who, a=a, k=k, h=h: O[a].at[_stretch(h, k)]))
            outs.append(((NCHIP * h,), b.dtype))
        else:
            transfers.append(dict(mask=SIBLING, src=lambda I, O, me, a=a: I[a].at[cut(a, 1 - me[2])],
                                  dst=lambda I, O, who, a=a: O[a]))
            shape = list(b.shape)
            shape[ax + 1] //= 2
            outs.append((tuple(shape), b.dtype))
    return transfers, outs


def _scatter_plan(tb):
    def slot(a, k):
        return (k,) if tb[a].ndim == 3 else _stretch(tb[a].shape[0] // NCHIP, k)

    transfers = []
    for a in range(len(tb)):
        for n, mask in enumerate(CHIP_MASKS):
            transfers.append(dict(
                mask=mask,
                src=lambda I, O, me, a=a, mask=mask: I[a].at[slot(a, _chip(_flip(me, mask)))],
                dst=lambda I, O, who, a=a, n=n: O[a].at[slot(a, n)]))
    outs = [((3,) + t.shape[1:] if t.ndim == 3 else (3 * (t.shape[0] // NCHIP),), t.dtype) for t in tb]
    return transfers, outs


def _last_exchange(vec, halves):
    def slot(who):
        return 4 * who[0] + 2 * who[1] + who[2]

    masks = [(m >> 2 & 1, m >> 1 & 1, m & 1) for m in range(1, 8)]
    transfers = [dict(mask=mask, src=lambda I, O, me: I[0], dst=lambda I, O, who: O[0].at[slot(who)])
                 for mask in masks]
    transfers += [dict(mask=SIBLING, src=lambda I, O, me, a=a: I[a], dst=lambda I, O, who, a=a: O[a])
                  for a in range(1, 1 + len(halves))]
    own = [(lambda I, O, me: I[0], lambda I, O, me: O[0].at[slot(me)])]
    outs = [((8,) + vec.shape, vec.dtype)] + [(t.shape, t.dtype) for t in halves]
    res = _exchange("last_exchange", [vec] + list(halves), outs, transfers, own)
    return res[0], res[1:]


def _rope_tables(positions):
    half = ROT // 2
    inv_freq = jnp.power(jnp.float32(THETA), -jnp.arange(0, ROT, 2, dtype=F32) / ROT)
    ang = positions.astype(F32)[:, None] * inv_freq[None, :]
    cos, sin = jnp.cos(ang), jnp.sin(ang)
    one, zero, z8 = jnp.ones((S, HD - ROT), F32), jnp.zeros((S, HD - ROT), F32), jnp.zeros((S, half), F32)
    c = jnp.concatenate([cos, cos, one], axis=1)
    a = jnp.concatenate([-sin, z8, zero], axis=1)
    b = jnp.concatenate([z8, sin, zero], axis=1)
    return tuple(jnp.tile(t, (1, 2)) for t in (c, a, b))


def _tile_heads(g, w):
    return jnp.tile(g.reshape(1, HD), (1, w // HD))


def _fold_heads(dg):
    return dg.reshape(-1, HD).sum(axis=0)


def _pad_lanes(a):
    return jnp.pad(a, ((0, 0), (0, LANES - a.shape[1])))


def _local_step(x, target, positions, wt, fetch, late_weights, begin_reduce):
    rope = _rope_tables(positions)
    w1t = wt["w_in_a_t"]
    f_row = 3 * D // LANES
    wg_t = w1t[3 * D + NH:]
    b_pad = _pad_lanes(wt["b_forget"].reshape(1, NH))
    qg_a, kg_a = _tile_heads(wt["qnorm_a_g"], D), _tile_heads(wt["knorm_a_g"], D)
    qg_b, kg_b = _tile_heads(wt["qnorm_b_g"], D), _tile_heads(wt["knorm_b_g"], KVW)
    norm_a, kv_g, norm_b = wt["norm_a_g"].reshape(1, D), wt["kv_norm_g"].reshape(1, D), wt["norm_b_g"].reshape(1, D)
    sinks_t = jnp.repeat(wt["sinks"].reshape(1, NH), HD, axis=1)

    (u_a,) = _rmsnorm_fwd(x, [norm_a], "norm_a")
    qkv, qkv_a = _proj_a(u_a, w1t, qg_a, kg_a)
    fpad = _mm("proj_f", S, LANES, [(u_a, _a_rows(D), w1t, _b_rows(D, row0=f_row, tn=LANES), NT)], tn=LANES)
    gate_a = _mm("proj_gate_a", S, D, [(u_a, _a_rows(D), wg_t, _b_rows(D, tn=TN_WIDE), NT)], tn=TN_WIDE)
    ct = _forget_cumsum(fpad, b_pad)
    ct2 = ct[:NH].reshape(NH // 2, 2, S)
    o_a, lse_a, y_a, fetched = _fox_fwd(qkv_a, ct2, gate_a, fetch)
    wt = {**wt, **late_weights(fetched)}
    w_in_b = wt["w_in_b"]
    h1, u_kv, u_b = _out_norms(y_a, wt["w_out_a"], x, [kv_g, norm_b])
    pb, kv, q_b, kdup, vdup = _proj_b(u_b, u_kv, w_in_b, wt["w_kv"], qg_b, kg_b, rope)
    gate_b_col = D // LANES
    o_b, lse_b, y_b = _swa_fwd(q_b, kdup, vdup, sinks_t, pb, gate_b_col)
    d_out, d_out_b, sq = _out_loss(y_b, wt["w_out_b"], h1, target)

    g = {}
    g["w_out_b"] = _mm("dw_out_b", D, D, [(y_b, _a_cols(S), d_out_b, _b_cols(S, tn=TN_WIDE), TN)], tn=TN_WIDE)
    d_y_b = _mm("dy_b", S, D, [(d_out_b, _a_rows(D), wt["w_out_b"], _b_rows(D, tn=TN_WIDE), NT)], tn=TN_WIDE)
    dq_b, dkdup, dvdup, dsk, d_gate_b = _swa_bwd(q_b, kdup, vdup, sinks_t, o_b, lse_b, d_y_b, pb, gate_b_col)
    g["sinks"] = dsk[0, ::HD]
    d_qb_raw, dg = _headnorm_bwd(pb, 0, qg_b, dq_b, rope, "qnorm_b_bwd")
    g["qnorm_b_g"] = _fold_heads(dg)
    g["w_in_b"] = jnp.concatenate([
        _mm("dw_in_b_q", D, D, [(u_b, _a_cols(S), d_qb_raw, _b_cols(S), TN)], stacked=True),
        _mm("dw_in_b_gate", D, D, [(u_b, _a_cols(S), d_gate_b, _b_cols(S), TN)], stacked=True)], axis=0)
    d_kv, dg = _kv_bwd(dkdup, dvdup, kv, kg_b, rope)
    g["knorm_b_g"] = _fold_heads(dg)
    g["w_kv"] = _mm("dw_kv", D, 2 * KVW, [(u_kv, _a_cols(S), d_kv, _b_cols(S), TN)])
    d_h1, d_h1_b, g["kv_norm_g"], g["norm_b_g"] = _du_b_norms(d_qb_raw, d_gate_b, d_kv, w_in_b, wt["w_kv"], h1, kv_g,
                                                              norm_b, d_out)
    g["w_out_a"] = _mm("dw_out_a", D, D, [(y_a, _a_cols(S), d_h1_b, _b_cols(S, tn=TN_WIDE), TN)], tn=TN_WIDE)
    late = {n: g[n] for n in LATE}
    d_y_a, halves = _mm("dy_a", S, D, [(d_h1_b, _a_rows(D), wt["w_out_a"], _b_rows(D, tn=TN_WIDE), NT)],
                        tn=TN_WIDE, riding=begin_reduce(late))
    riding, so_far = begin_reduce(late, halves)
    dq_a, dk_a, dv_a, dct, d_gate_a, arrived = _fox_bwd(qkv_a, ct2, o_a, lse_a, d_y_a, gate_a, riding)
    dct_pad = jnp.pad(dct.reshape(NH, S), ((0, LANES - NH), (0, 0)))
    d_f, db = _forget_bwd(dct_pad, fpad, b_pad)
    g["b_forget"] = db[0, :NH]
    d_q_raw, dg = _headnorm_bwd(qkv, 0, qg_a, dq_a, None, "qnorm_a_bwd")
    g["qnorm_a_g"] = _fold_heads(dg)
    d_k_raw, dg = _headnorm_bwd(qkv, 1, kg_a, dk_a, None, "knorm_a_bwd")
    g["knorm_a_g"] = _fold_heads(dg)
    rows, gw = 4 * D + NH, None
    for n, t, row0 in (("q", d_q_raw, 0), ("k", d_k_raw, D), ("v", dv_a, 2 * D)):
        gw = _mm("dw_in_a_" + n, D, D, [(t, _a_cols(S), u_a, _b_cols(S, tn=TN_WIDE), TN)], tn=TN_WIDE,
                 rows_of=(gw, rows, row0))
    gw = _mm("dw_in_a_f", LANES, D, [(d_f, _a_cols(S, tm=LANES), u_a, _b_cols(S, tn=TN_WIDE), TN)], tm=LANES,
             tn=TN_WIDE, rows_of=(gw, rows, 3 * D))
    g["w_in_a"] = _mm("dw_in_a_gate", D, D, [(d_gate_a, _a_cols(S), u_a, _b_cols(S, tn=TN_WIDE), TN)], tn=TN_WIDE,
                      rows_of=(gw, rows, 3 * D + NH))
    first = {"w_in_a": g["w_in_a"]}
    riding, so_far_first = begin_reduce(first, begin_reduce(first).alone("sibling_halves_w_in_a"))
    d_u_a, arrived_first = _mm("du_a", S, D, [
        (d_q_raw, _a_rows(D), w1t, _b_cols(D, row=0, tn=TN_WIDE), None),
        (d_k_raw, _a_rows(D), w1t, _b_cols(D, row=1, tn=TN_WIDE), None),
        (dv_a, _a_rows(D), w1t, _b_cols(D, row=2, tn=TN_WIDE), None),
        (d_gate_a, _a_rows(D), wg_t, _b_cols(D, tn=TN_WIDE), None),
        (d_f, _a_rows(LANES), w1t, _b_cols(LANES, row=f_row, tn=TN_WIDE), None)], tn=TN_WIDE, riding=riding)
    d_x, _, g["norm_a_g"] = _rmsnorm_bwd(x, [norm_a], [d_u_a], d_h1, "norm_a_bwd")
    return sq, d_x, g, (list(so_far_first) + list(so_far), list(arrived_first) + list(arrived))


BIG = ["w_in_a", "w_out_a", "w_kv", "w_in_b", "w_out_b"]
LATE = BIG[1:]
SPLIT = {"w_in_a": None, "w_out_a": 0, "w_kv": 0, "w_in_b": 0, "w_out_b": 0}
SMALL = ["norm_a_g", "b_forget", "qnorm_a_g", "knorm_a_g", "kv_norm_g", "knorm_b_g", "norm_b_g", "qnorm_b_g", "sinks"]
NAMES = ["norm_a_g", "w_in_a", "b_forget", "qnorm_a_g", "knorm_a_g", "w_out_a", "kv_norm_g", "w_kv", "knorm_b_g",
         "norm_b_g", "w_in_b", "qnorm_b_g", "sinks", "w_out_b"]


def _pack(vals):
    flat = []
    for v in vals:
        v = v.reshape(-1)
        flat.append(jnp.pad(v, (0, -v.shape[0] % LANES)))
    flat = jnp.concatenate(flat)
    flat = jnp.pad(flat, (0, -flat.shape[0] % (8 * LANES)))
    return flat.reshape(-1, LANES)


def _unpack(packed, shapes):
    flat, out, off = packed.reshape(-1), [], 0
    for s in shapes:
        n = int(np.prod(s))
        out.append(flat[off:off + n].reshape(s))
        off += n + (-n % LANES)
    return out


def kernel(x, positions, norm_a_g, w_in_a, b_forget, qnorm_a_g, knorm_a_g, w_out_a, kv_norm_g, w_kv, knorm_b_g, norm_b_g, w_in_b, qnorm_b_g, sinks, w_out_b, loss_target, m_norm_a_g, m_w_in_a, m_b_forget, m_qnorm_a_g, m_knorm_a_g, m_w_out_a, m_kv_norm_g, m_w_kv, m_knorm_b_g, m_norm_b_g, m_w_in_b, m_qnorm_b_g, m_sinks, m_w_out_b, v_norm_a_g, v_w_in_a, v_b_forget, v_qnorm_a_g, v_knorm_a_g, v_w_out_a, v_kv_norm_g, v_w_kv, v_knorm_b_g, v_norm_b_g, v_w_in_b, v_qnorm_b_g, v_sinks, v_w_out_b):
    w = dict(norm_a_g=norm_a_g, w_in_a=w_in_a, b_forget=b_forget, qnorm_a_g=qnorm_a_g, knorm_a_g=knorm_a_g,
             w_out_a=w_out_a, kv_norm_g=kv_norm_g, w_kv=w_kv, knorm_b_g=knorm_b_g, norm_b_g=norm_b_g,
             w_in_b=w_in_b, qnorm_b_g=qnorm_b_g, sinks=sinks, w_out_b=w_out_b)
    m = dict(norm_a_g=m_norm_a_g, w_in_a=m_w_in_a, b_forget=m_b_forget, qnorm_a_g=m_qnorm_a_g, knorm_a_g=m_knorm_a_g,
             w_out_a=m_w_out_a, kv_norm_g=m_kv_norm_g, w_kv=m_w_kv, knorm_b_g=m_knorm_b_g, norm_b_g=m_norm_b_g,
             w_in_b=m_w_in_b, qnorm_b_g=m_qnorm_b_g, sinks=m_sinks, w_out_b=m_w_out_b)
    v = dict(norm_a_g=v_norm_a_g, w_in_a=v_w_in_a, b_forget=v_b_forget, qnorm_a_g=v_qnorm_a_g, knorm_a_g=v_knorm_a_g,
             w_out_a=v_w_out_a, kv_norm_g=v_kv_norm_g, w_kv=v_w_kv, knorm_b_g=v_knorm_b_g, norm_b_g=v_norm_b_g,
             w_in_b=v_w_in_b, qnorm_b_g=v_qnorm_b_g, sinks=v_sinks, w_out_b=v_w_out_b)
    my_chip = 2 * lax.axis_index("x") + lax.axis_index("y")

    def shard2d(t, n):
        if n == "w_in_a":
            return jnp.transpose(t, (2, 0, 1)).reshape(-1)
        return t.reshape(t.shape[-2:])

    def unflat(t, n):
        return jnp.transpose(t.reshape(-1, 1, D), (1, 2, 0)) if n == "w_in_a" else t.reshape(w[n].shape)

    w2d = {n: shard2d(w[n], n) for n in BIG}

    norm_a_rows = jnp.broadcast_to(norm_a_g.reshape(1, D // NCHIP), (2 * SUBLANES, D // NCHIP))
    w1t, norm_rows = _gather_shards([w2d["w_in_a"].astype(BF16), norm_a_rows], [SPLIT["w_in_a"], 0])
    wt = {"w_in_a_t": w1t.reshape(-1, D), "norm_a_g": norm_rows[:, 0, :].reshape(1, D)}
    for n in SMALL[1:]:
        wt[n] = w[n]
    late_shards = [w2d[n].astype(BF16) for n in LATE]
    late_axes = [SPLIT[n] for n in LATE]
    transfers, outs, own = _gather_plan(late_shards, late_axes)
    fetch = _Riding(transfers, late_shards, outs, own)

    def late_weights(fetched):
        return {n: t if n == "w_in_b" else t.reshape(-1, t.shape[2]) for n, t in zip(LATE, fetched)}

    def as_blocks(t):
        if t.ndim == 3:
            return t
        return t.reshape(-1) if t.shape[0] % (SUBLANES * NCHIP) else t.reshape(NCHIP, -1, t.shape[1])

    def begin_reduce(grads, halves=None):
        names = list(grads)
        axes = [SPLIT[n] for n in names]
        blocks = [as_blocks(grads[n]) for n in names]
        if halves is None:
            transfers, outs = _halves_plan(blocks, axes)
            return _Riding(transfers, blocks, outs)
        sums = [_chip_sum(blk, part, ax, "chip_sum_" + n) for n, ax, blk, part in zip(names, axes, blocks, halves)]
        bf16 = [s[1] for s in sums]
        transfers, outs = _scatter_plan(bf16)
        return _Riding(transfers, bf16, outs), [s[0] for s in sums]

    sq, d_x, g, (chip_f32, arrived) = _local_step(x[0], loss_target[0], positions, wt, fetch, late_weights,
                                                  begin_reduce)

    axes = [SPLIT[n] for n in BIG]
    halves = [_mesh_sum(t32, parts, ax, "mesh_sum_" + n) for n, ax, t32, parts in zip(BIG, axes, chip_f32, arrived)]

    small_shapes = [(D,), (NH,), (HD,), (HD,), (D,), (HD,), (D,), (HD,), (NH,), (D,)]
    gathered_small, sibling_done = _last_exchange(_pack([g[n] for n in SMALL] + [sq]), halves)
    total = _sum_stack(gathered_small, "sum_small")
    small_g = dict(zip(SMALL, _unpack(total, small_shapes)[:-1]))
    loss = 0.5 * jnp.sum(_unpack(total, small_shapes)[-1]) / D
    small_g["norm_a_g"] = lax.dynamic_slice(small_g["norm_a_g"], (my_chip * (D // NCHIP),), (D // NCHIP,))

    res = {}
    for n, ax, mine_half, their_half in zip(BIG, axes, halves, sibling_done):
        out4 = _adamw_halves(w2d[n], mine_half, their_half, shard2d(m[n], n), shard2d(v[n], n), ax, "adamw_" + n)
        res[n] = tuple(unflat(t, n) for t in out4)
    row = lambda t: t.reshape(1, -1)
    small_out = _adamw_small(*[[row(d[n]) for n in SMALL] for d in (w, small_g, m, v)])
    for i, n in enumerate(SMALL):
        res[n] = tuple(t.reshape(w[n].shape) for t in (small_g[n],) + tuple(out[i] for out in small_out))

    outs = [loss, d_x[None]]
    for k in range(4):
        outs += [res[n][k] for n in NAMES]
    return tuple(outs)
```

```python
import numpy as np
import jax
import jax.numpy as jnp
from jax import lax
from jax.experimental import pallas as pl
from jax.experimental.pallas import tpu as pltpu

F32, BF16 = jnp.float32, jnp.bfloat16
S, D, HD, NH, NKV = 2048, 1024, 64, 16, 4
KVW = NKV * HD
WINDOW = 128
ROT = HD // 4
THETA = 500000.0
EPS = 1e-6
SCALE = HD ** -0.5
LANES = 128
SUBLANES = 8
NEG = -1e30
VMEM_LIMIT = 48 * 2 ** 20
ROWS = 512
ATT = 512
SWQ = 16
NCHIP = 4
ADAM_LR, ADAM_B1, ADAM_B2, ADAM_EPS, ADAM_WD, ADAM_STEP = 0.001, 0.9, 0.999, 1e-08, 0.01, 10
NT = (((1,), (1,)), ((), ()))
TN = (((0,), (0,)), ((), ()))
MESH = pl.DeviceIdType.MESH


def _params(n):
    return pltpu.CompilerParams(dimension_semantics=("arbitrary",) * n, vmem_limit_bytes=VMEM_LIMIT)


def _dot(a, b, dims=None):
    if dims is None:
        return jnp.dot(a, b, preferred_element_type=F32)
    return lax.dot_general(a, b, dims, preferred_element_type=F32)


def _dot_split(a, b, n):
    out, rest = None, a
    for _ in range(n):
        hi = rest.astype(BF16)
        term = _dot(hi, b)
        out = term if out is None else out + term
        rest = rest - hi.astype(F32)
    return out


def _seg_mat(w):
    e = (np.arange(w)[:, None] // HD == np.arange(LANES)[None, :]).astype(np.float32)
    return jnp.asarray(e, BF16)


def _spread(r, w):
    head = lax.broadcasted_iota(jnp.int32, (2 * LANES, w), 1) >> (HD.bit_length() - 1)
    row = lax.broadcasted_iota(jnp.int32, (2 * LANES, w), 0)
    et2 = jnp.where(head == (row & (LANES - 1)), 1.0, 0.0).astype(BF16)
    hi = r.astype(BF16)
    lo = (r - hi.astype(F32)).astype(BF16)
    return _dot(jnp.concatenate([hi, lo], axis=1), et2)


def _head_rstd(x, e):
    ss = _dot_split(x * x, e, 2)
    return _spread(lax.rsqrt(ss * (1.0 / HD) + EPS), x.shape[1])


def _rope(x, c, a, b):
    w = x.shape[1]
    return x * c + pltpu.roll(x, w - ROT // 2, 1) * a + pltpu.roll(x, ROT // 2, 1) * b


def _rope_t(dy, c, a, b):
    w = dy.shape[1]
    return dy * c + pltpu.roll(dy * b, w - ROT // 2, 1) + pltpu.roll(dy * a, ROT // 2, 1)


def _sigmoid(x):
    return 1.0 / (1.0 + jnp.exp(-x))


def _row_spec(shape, ts):
    nd = len(shape)
    if shape[0] == S:
        return pl.BlockSpec((ts,) + tuple(shape[1:]), lambda i: (i,) + (0,) * (nd - 1))
    return pl.BlockSpec(tuple(shape), lambda i: (0,) * nd)


def _rows_call(body, name, ins, outs, ts=ROWS):
    return pl.pallas_call(
        body, name=name, grid=(S // ts,),
        in_specs=[_row_spec(a.shape, ts) for a in ins],
        out_specs=[_row_spec(s, ts) for s, _ in outs],
        out_shape=[jax.ShapeDtypeStruct(s, d) for s, d in outs],
        compiler_params=_params(1))(*ins)


def _col_spec(ts, w, col):
    return pl.BlockSpec((ts, w), lambda i: (i, col))


TM = TN_ = 512
TM_TOKENS = 1024
TN_WIDE = 1024


def _mm(name, m, n, terms, out_dtype=F32, add=None, tm=None, tn=TN_, stacked=False, riding=None, rows_of=None):
    nterm = len(terms)
    if tm is None:
        tm = TM_TOKENS if m == S else TM
    nj, ni_ = n // tn, m // tm
    n_in = 2 * nterm + (add is not None) + (rows_of is not None and rows_of[0] is not None)
    r_in, r_out = (len(riding.ins), len(riding.outs)) if riding is not None else (0, 0)

    def body(*refs):
        if riding is not None:
            j, i = pl.program_id(0), pl.program_id(1)
            at_end = riding.hooks(refs[n_in:n_in + r_in], refs[n_in + r_in + 1:n_in + r_in + 1 + r_out],
                                  *refs[n_in + r_in + 1 + r_out:], first=(j == 0) & (i == 0),
                                  middle=(j == nj // 2) & (i == 0), last=(j == nj - 1) & (i == ni_ - 1))
        acc = None
        for t in range(nterm):
            part = _dot(refs[2 * t][...], refs[2 * t + 1][...], terms[t][4])
            acc = part if acc is None else acc + part
        if add is not None:
            acc = acc + refs[2 * nterm][...]
        refs[n_in + r_in][...] = acc.astype(out_dtype)
        if riding is not None:
            at_end()

    tile = pl.BlockSpec((tm, tn), lambda j, i: (i, j))
    ins, specs = [], []
    for a, a_spec, b, b_spec, _ in terms:
        ins += [a, b]
        specs += [a_spec, b_spec]
    if add is not None:
        ins.append(add)
        specs.append(tile)
    out_spec = pl.BlockSpec((None, tm, tn), lambda j, i: (j, i, 0)) if stacked else tile
    out_shape = jax.ShapeDtypeStruct((nj, m, tn) if stacked else (m, n), out_dtype)
    if rows_of is not None:
        taller, rows, row0 = rows_of
        out_spec = pl.BlockSpec((pl.Element(tm), pl.Element(tn)), lambda j, i: (
            pl.multiple_of(row0 + i * tm, SUBLANES), pl.multiple_of(j * tn, LANES)))
        out_shape = jax.ShapeDtypeStruct((rows, n), out_dtype)
        alias = {}
        if taller is not None:
            ins.append(taller)
            specs.append(pl.BlockSpec(memory_space=pltpu.HBM))
            alias = {len(ins) - 1: 0}
        return pl.pallas_call(body, name=name, grid=(nj, ni_), in_specs=specs, out_specs=out_spec,
                              out_shape=out_shape, input_output_aliases=alias, compiler_params=_params(2))(*ins)
    if riding is None:
        return pl.pallas_call(body, name=name, grid=(nj, ni_), in_specs=specs, out_specs=out_spec,
                              out_shape=out_shape, compiler_params=_params(2))(*ins)
    res = pl.pallas_call(
        body, name=name, grid=(nj, ni_), in_specs=specs + riding.in_specs,
        out_specs=[out_spec] + riding.out_specs, out_shape=[out_shape] + riding.out_shape,
        scratch_shapes=riding.scratch, compiler_params=_params(2))(*ins, *riding.ins)
    return res[0], res[1:]


def _a_rows(k, col=0, tm=TM_TOKENS):
    return pl.BlockSpec((tm, k), lambda j, i: (i, col))


def _a_cols(k, tm=TM):
    return pl.BlockSpec((k, tm), lambda j, i: (0, i))


def _b_cols(k, row=0, col0=0, tn=TN_):
    return pl.BlockSpec((k, tn), lambda j, i: (row, col0 + j))


def _b_rows(k, row0=0, tn=TN_):
    return pl.BlockSpec((tn, k), lambda j, i: (row0 + j, 0))


def _rmsnorm_fwd(x, gains, name):
    def body(*refs):
        xv = refs[0][...]
        r = lax.rsqrt(jnp.mean(xv * xv, axis=-1, keepdims=True) + EPS)
        xh = xv * r
        for n in range(len(gains)):
            refs[1 + len(gains) + n][...] = (xh * refs[1 + n][...]).astype(BF16)

    return _rows_call(body, name, [x] + list(gains), [((S, D), BF16)] * len(gains))


def _norm_bwd_tile(xv, gains, dus, dres, dg_refs):
    r = lax.rsqrt(jnp.mean(xv * xv, axis=-1, keepdims=True) + EPS)
    xh = xv * r
    gy = None
    for m, (gain, du) in enumerate(zip(gains, dus)):
        part = jnp.sum(du * xh, axis=0, keepdims=True)

        @pl.when(pl.program_id(0) == 0)
        def _(m=m, part=part):
            dg_refs[m][...] = part

        @pl.when(pl.program_id(0) != 0)
        def _(m=m, part=part):
            dg_refs[m][...] += part

        t = du * gain
        gy = t if gy is None else gy + t
    return dres + r * (gy - xh * jnp.mean(gy * xh, axis=-1, keepdims=True))


def _rmsnorm_bwd(x, gains, dus, dres, name):
    n = len(gains)

    def body(*refs):
        x_ref, g_refs, du_refs, dres_ref = refs[0], refs[1:1 + n], refs[1 + n:1 + 2 * n], refs[1 + 2 * n]
        dx_ref, dxb_ref, dg_refs = refs[2 + 2 * n], refs[3 + 2 * n], refs[4 + 2 * n:]
        dx = _norm_bwd_tile(x_ref[...], [g[...] for g in g_refs], [du[...] for du in du_refs], dres_ref[...], dg_refs)
        dx_ref[...] = dx
        dxb_ref[...] = dx.astype(BF16)

    outs = [((S, D), F32), ((S, D), BF16)] + [((1, D), F32)] * n
    return _rows_call(body, name, [x] + list(gains) + list(dus) + [dres], outs)


def _du_b_norms(d_q, d_gate, d_kv, w_in_b, w_kv, h, kv_g, norm_b, dres):
    half = D // 2

    def body(dq_ref, dgate_ref, dkv_ref, wb_ref, wkv_ref, h_ref, gk_ref, gb_ref, dres_ref,
             dx_ref, dxb_ref, dgk_ref, dgb_ref):
        pieces = (dq_ref[:, :half], dq_ref[:, half:], dgate_ref[:, :half], dgate_ref[:, half:])
        du_b = None
        for c in range(NCHIP):
            part = _dot(pieces[c], wb_ref[c], NT)
            du_b = part if du_b is None else du_b + part
        du_kv = _dot(dkv_ref[...], wkv_ref[...], NT)
        dx = _norm_bwd_tile(h_ref[...], [gk_ref[...], gb_ref[...]], [du_kv, du_b], dres_ref[...], [dgk_ref, dgb_ref])
        dx_ref[...] = dx
        dxb_ref[...] = dx.astype(BF16)

    whole = lambda a: pl.BlockSpec(a.shape, lambda i: (0,) * a.ndim)
    rows = lambda w: pl.BlockSpec((ROWS, w), lambda i: (i, 0))
    return pl.pallas_call(
        body, name="du_b_norms", grid=(S // ROWS,),
        in_specs=[rows(D), rows(D), rows(2 * KVW), whole(w_in_b), whole(w_kv), rows(D), whole(kv_g), whole(norm_b),
                  rows(D)],
        out_specs=[rows(D), rows(D), whole(kv_g), whole(norm_b)],
        out_shape=[jax.ShapeDtypeStruct((S, D), F32), jax.ShapeDtypeStruct((S, D), BF16),
                   jax.ShapeDtypeStruct((1, D), F32), jax.ShapeDtypeStruct((1, D), F32)],
        compiler_params=_params(1))(d_q, d_gate, d_kv, w_in_b, w_kv, h, kv_g, norm_b, dres)


def _proj_a(u, w1t, qg, kg):
    e = _seg_mat(D)
    gains = jnp.stack([qg * SCALE, kg])

    def body(u_ref, w_ref, g_ref, e_ref, raw_ref, out_ref):
        x = _dot(u_ref[...], w_ref[...], NT)
        raw_ref[...] = x

        @pl.when(pl.program_id(0) < 2)
        def _():
            out_ref[...] = (x * _head_rstd(x, e_ref[...]) * g_ref[...]).astype(BF16)

        @pl.when(pl.program_id(0) == 2)
        def _():
            out_ref[...] = x.astype(BF16)

    tm = TM_TOKENS
    return pl.pallas_call(
        body, name="proj_a", grid=(3, S // tm),
        in_specs=[pl.BlockSpec((tm, D), lambda j, i: (i, 0)), pl.BlockSpec((D, D), lambda j, i: (j, 0)),
                  pl.BlockSpec((None, 1, D), lambda j, i: (jnp.minimum(j, 1), 0, 0)),
                  pl.BlockSpec(e.shape, lambda j, i: (0, 0))],
        out_specs=[pl.BlockSpec((tm, D), lambda j, i: (i, j)), pl.BlockSpec((None, tm, D), lambda j, i: (j, i, 0))],
        out_shape=[jax.ShapeDtypeStruct((S, 3 * D), F32), jax.ShapeDtypeStruct((3, S, D), BF16)],
        compiler_params=_params(2))(u, w1t, gains, e)


def _tri(upper):
    r, c = np.arange(ROWS)[:, None], np.arange(ROWS)[None, :]
    return jnp.asarray((r <= c) if upper else (r >= c), BF16)


def _forget_cumsum(fpad, bpad):
    def body(f_ref, b_ref, u_ref, c_ref, carry):
        @pl.when(pl.program_id(0) == 0)
        def _():
            carry[...] = jnp.zeros_like(carry)

        lf = jax.nn.log_sigmoid(f_ref[...] + b_ref[...])
        blk = _dot_split(lf.T, u_ref[...], 3) + carry[:, 0:1]
        c_ref[...] = blk
        carry[...] = jnp.broadcast_to(blk[:, ROWS - 1:ROWS], carry.shape)

    return pl.pallas_call(
        body, name="forget_cumsum", grid=(S // ROWS,),
        in_specs=[pl.BlockSpec((ROWS, LANES), lambda i: (i, 0)), pl.BlockSpec((1, LANES), lambda i: (0, 0)),
                  pl.BlockSpec((ROWS, ROWS), lambda i: (0, 0))],
        out_specs=pl.BlockSpec((LANES, ROWS), lambda i: (0, i)),
        out_shape=jax.ShapeDtypeStruct((LANES, S), F32),
        scratch_shapes=[pltpu.VMEM((LANES, LANES), F32)],
        compiler_params=_params(1))(fpad, bpad, _tri(True))


def _forget_bwd(dct, fpad, bpad):
    nb = S // ROWS

    def body(dc_ref, f_ref, b_ref, l_ref, df_ref, db_ref, carry):
        @pl.when(pl.program_id(0) == 0)
        def _():
            carry[...] = jnp.zeros_like(carry)
            db_ref[...] = jnp.zeros_like(db_ref)

        blk = _dot_split(dc_ref[...], l_ref[...], 3) + carry[:, 0:1]
        carry[...] = jnp.broadcast_to(blk[:, 0:1], carry.shape)
        df = blk.T * _sigmoid(-(f_ref[...] + b_ref[...]))
        df_ref[...] = df.astype(BF16)
        db_ref[...] += jnp.sum(df, axis=0, keepdims=True)

    return pl.pallas_call(
        body, name="forget_bwd", grid=(nb,),
        in_specs=[pl.BlockSpec((LANES, ROWS), lambda i: (0, nb - 1 - i)),
                  pl.BlockSpec((ROWS, LANES), lambda i: (nb - 1 - i, 0)),
                  pl.BlockSpec((1, LANES), lambda i: (0, 0)), pl.BlockSpec((ROWS, ROWS), lambda i: (0, 0))],
        out_specs=[pl.BlockSpec((ROWS, LANES), lambda i: (nb - 1 - i, 0)), pl.BlockSpec((1, LANES), lambda i: (0, 0))],
        out_shape=[jax.ShapeDtypeStruct((S, LANES), BF16), jax.ShapeDtypeStruct((1, LANES), F32)],
        scratch_shapes=[pltpu.VMEM((LANES, LANES), F32)],
        compiler_params=_params(1))(dct, fpad, bpad, _tri(False))


def _headnorm_bwd(x, col, gain, dy, rope, name):
    e = _seg_mat(D)
    tabs = list(rope) if rope is not None else []

    def body(*refs):
        x_ref, g_ref, dy_ref, e_ref = refs[:4]
        dx_ref, dg_ref = refs[-2:]
        xv, dyv, ev = x_ref[...], dy_ref[...], e_ref[...]
        if rope is not None:
            c, a, b = (jnp.tile(t[...], (1, D // LANES)) for t in refs[4:7])
            dyv = _rope_t(dyv, c, a, b)
        r = _head_rstd(xv, ev)
        xh = xv * r
        part = jnp.sum(dyv * xh, axis=0, keepdims=True)

        @pl.when(pl.program_id(0) == 0)
        def _():
            dg_ref[...] = part

        @pl.when(pl.program_id(0) != 0)
        def _():
            dg_ref[...] += part

        gy = dyv * g_ref[...]
        seg = _spread(_dot_split(gy * xh, ev, 2) * (1.0 / HD), D)
        dx_ref[...] = (r * (gy - xh * seg)).astype(BF16)

    whole = lambda a: pl.BlockSpec(a.shape, lambda i: (0, 0))
    return pl.pallas_call(
        body, name=name, grid=(S // ROWS,),
        in_specs=[_col_spec(ROWS, D, col), whole(gain), _col_spec(ROWS, D, 0), whole(e)]
                 + [pl.BlockSpec((ROWS, LANES), lambda i: (i, 0))] * len(tabs),
        out_specs=[_col_spec(ROWS, D, 0), whole(gain)],
        out_shape=[jax.ShapeDtypeStruct((S, D), BF16), jax.ShapeDtypeStruct((1, D), F32)],
        compiler_params=_params(1))(x, gain, dy, e, *tabs)


def _dup_mat():
    r, c = np.arange(KVW)[:, None], np.arange(2 * KVW)[None, :]
    return (r // HD == c // LANES) & (r % HD == c % HD)


def _fold_mat():
    r, c = np.arange(D)[:, None], np.arange(KVW)[None, :]
    return (r // (2 * LANES) == c // HD) & (r % HD == c % HD)


def _proj_b(u_b, u_kv, w_in_b, w_kv, qg, kg, rope):
    e, ek = _seg_mat(D), _seg_mat(KVW)
    dup = jnp.asarray(_dup_mat(), BF16)

    def body(ub_ref, ukv_ref, wb_ref, wkv_ref, qg_ref, kg_ref, e_ref, ek_ref, dup_ref, c_ref, a_ref, b_ref,
             pb_ref, kv_ref, qo, ko, vo):
        ub = ub_ref[...]
        pb = jnp.concatenate([_dot(ub, wb_ref[c]) for c in range(NCHIP)], axis=1)
        kv = _dot(ukv_ref[...], wkv_ref[...])
        pb_ref[...] = pb
        kv_ref[...] = kv
        c1, a1, b1 = c_ref[...], a_ref[...], b_ref[...]
        qv = pb[:, :D]
        qn = qv * _head_rstd(qv, e_ref[...]) * qg_ref[...]
        t = lambda z, n: jnp.tile(z, (1, n))
        qo[...] = (_rope(qn, t(c1, D // LANES), t(a1, D // LANES), t(b1, D // LANES)) * SCALE).astype(BF16)
        kvv = kv[:, :KVW]
        kn = kvv * _head_rstd(kvv, ek_ref[...]) * kg_ref[...]
        kr = _rope(kn, t(c1, KVW // LANES), t(a1, KVW // LANES), t(b1, KVW // LANES)).astype(BF16)
        ko[...] = _dot(kr, dup_ref[...]).astype(BF16)
        vo[...] = _dot(kv[:, KVW:].astype(BF16), dup_ref[...]).astype(BF16)

    whole = lambda a: pl.BlockSpec(a.shape, lambda i: (0,) * a.ndim)
    rows = lambda w: pl.BlockSpec((ROWS, w), lambda i: (i, 0))
    return pl.pallas_call(
        body, name="proj_b", grid=(S // ROWS,),
        in_specs=[rows(D), rows(D), whole(w_in_b), whole(w_kv), whole(qg), whole(kg), whole(e), whole(ek),
                  whole(dup), rows(LANES), rows(LANES), rows(LANES)],
        out_specs=[rows(2 * D), rows(2 * KVW), rows(D), rows(2 * KVW), rows(2 * KVW)],
        out_shape=[jax.ShapeDtypeStruct((S, 2 * D), F32), jax.ShapeDtypeStruct((S, 2 * KVW), F32),
                   jax.ShapeDtypeStruct((S, D), BF16), jax.ShapeDtypeStruct((S, 2 * KVW), BF16),
                   jax.ShapeDtypeStruct((S, 2 * KVW), BF16)],
        compiler_params=_params(1))(u_b, u_kv, w_in_b, w_kv, qg, kg, e, ek, dup, *rope)


def _kv_bwd(dkdup, dvdup, kv, kg, rope):
    ek = _seg_mat(KVW)
    fold = jnp.asarray(_fold_mat(), BF16)

    def body(dk_ref, dv_ref, k_ref, kg_ref, ek_ref, fold_ref, c_ref, a_ref, b_ref, dkv_ref, dg_ref):
        ev, fv = ek_ref[...], fold_ref[...]
        t = lambda z: jnp.tile(z[...], (1, KVW // LANES))
        dk = _rope_t(_dot_split(dk_ref[...], fv, 2), t(c_ref), t(a_ref), t(b_ref))
        dv = _dot_split(dv_ref[...], fv, 2)
        xv = k_ref[...]
        r = _head_rstd(xv, ev)
        xh = xv * r
        part = jnp.sum(dk * xh, axis=0, keepdims=True)

        @pl.when(pl.program_id(0) == 0)
        def _():
            dg_ref[...] = part

        @pl.when(pl.program_id(0) != 0)
        def _():
            dg_ref[...] += part

        gy = dk * kg_ref[...]
        seg = _spread(_dot_split(gy * xh, ev, 2) * (1.0 / HD), KVW)
        dkv_ref[:, 0:KVW] = (r * (gy - xh * seg)).astype(BF16)
        dkv_ref[:, KVW:2 * KVW] = dv.astype(BF16)

    whole = lambda a: pl.BlockSpec(a.shape, lambda i: (0, 0))
    tab = pl.BlockSpec((ROWS, LANES), lambda i: (i, 0))
    return pl.pallas_call(
        body, name="kv_bwd", grid=(S // ROWS,),
        in_specs=[_col_spec(ROWS, D, 0), _col_spec(ROWS, D, 0), _col_spec(ROWS, KVW, 0),
                  whole(kg), whole(ek), whole(fold), tab, tab, tab],
        out_specs=[_col_spec(ROWS, 2 * KVW, 0), whole(kg)],
        out_shape=[jax.ShapeDtypeStruct((S, 2 * KVW), BF16), jax.ShapeDtypeStruct((1, KVW), F32)],
        compiler_params=_params(1))(dkdup, dvdup, kv, kg, ek, fold, *rope)


def _out_norms(y, w_out, residual, gains):
    n = len(gains)

    def body(y_ref, w_ref, r_ref, *refs):
        h = _dot(y_ref[...], w_ref[...]) + r_ref[...]
        refs[n][...] = h
        hn = h * lax.rsqrt(jnp.mean(h * h, axis=-1, keepdims=True) + EPS)
        for k in range(n):
            refs[n + 1 + k][...] = (hn * refs[k][...]).astype(BF16)

    rows = pl.BlockSpec((TM_TOKENS, D), lambda i: (i, 0))
    whole = pl.BlockSpec((D, D), lambda i: (0, 0))
    gain = pl.BlockSpec((1, D), lambda i: (0, 0))
    return pl.pallas_call(
        body, name="out_a_norms", grid=(S // TM_TOKENS,), in_specs=[rows, whole, rows] + [gain] * n,
        out_specs=[rows] * (n + 1),
        out_shape=[jax.ShapeDtypeStruct((S, D), F32)] + [jax.ShapeDtypeStruct((S, D), BF16)] * n,
        compiler_params=_params(1))(y, w_out, residual, *gains)


def _out_loss(y, w_out, residual, target):
    def body(y_ref, w_ref, r_ref, t_ref, d_ref, db_ref, l_ref):
        diff = _dot(y_ref[...], w_ref[...]) + r_ref[...] - t_ref[...]
        d = diff * (1.0 / D)
        d_ref[...] = d
        db_ref[...] = d.astype(BF16)

        @pl.when(pl.program_id(0) == 0)
        def _():
            l_ref[...] = jnp.zeros_like(l_ref)

        l_ref[...] += jnp.sum(diff * diff, axis=0, keepdims=True)

    rows = pl.BlockSpec((TM_TOKENS, D), lambda i: (i, 0))
    whole = pl.BlockSpec((D, D), lambda i: (0, 0))
    return pl.pallas_call(
        body, name="out_b_loss", grid=(S // TM_TOKENS,), in_specs=[rows, whole, rows, rows],
        out_specs=[rows, rows, pl.BlockSpec((1, D), lambda i: (0, 0))],
        out_shape=[jax.ShapeDtypeStruct((S, D), F32), jax.ShapeDtypeStruct((S, D), BF16),
                   jax.ShapeDtypeStruct((1, D), F32)],
        compiler_params=_params(1))(y, w_out, residual, target)


def _lane():
    return lax.broadcasted_iota(jnp.int32, (1, LANES), 1)


def _head_mask(hh):
    return (_lane() < HD) if hh == 0 else (_lane() >= HD)


def _qkv_specs():
    return (pl.BlockSpec((None, ATT, LANES), lambda p, i: (0, i, p)),
            pl.BlockSpec((None, S, LANES), lambda p, i: (1, 0, p)),
            pl.BlockSpec((None, S, LANES), lambda p, i: (2, 0, p)))


def _fox_fwd(qkv, ct, gate, riding):
    nq, npair = S // ATT, NH // 2
    ni, no = len(riding.ins), len(riding.outs)

    def body(q_ref, k_ref, v_ref, c_ref, gate_ref, *rest):
        o_ref, lse_ref, y_ref = rest[ni:ni + 3]
        pair, i = pl.program_id(0), pl.program_id(1)
        at_end = riding.hooks(rest[:ni], rest[ni + 3:ni + 3 + no], *rest[ni + 3 + no:],
                              first=(pair == 0) & (i == 0), middle=(pair == npair // 2) & (i == 0),
                              last=(pair == npair - 1) & (i == nq - 1))
        q2 = q_ref[...]
        qms = [jnp.where(_head_mask(hh), q2, jnp.zeros_like(q2)) for hh in (0, 1)]

        def probs(off, width, m, hh, diag):
            s = _dot(qms[hh], k_ref[pl.ds(off, width), :], NT) - c_ref[hh:hh + 1, pl.ds(off, width)]
            if diag:
                row = i * ATT + lax.broadcasted_iota(jnp.int32, (ATT, width), 0)
                col = off + lax.broadcasted_iota(jnp.int32, (ATT, width), 1)
                s = jnp.where(col <= row, s, NEG)
            m_new = jnp.maximum(m, jnp.max(s, axis=1, keepdims=True))
            p = jnp.exp(s - m_new)
            p_hi = p.astype(BF16)
            return m_new, jnp.exp(m - m_new), p_hi, (p - p_hi.astype(F32)).astype(BF16)

        def weighted(off, width, p_hi, p_lo, hh):
            vj = v_ref[pl.ds(off, width), :]
            v1 = jnp.where(_head_mask(hh), vj, jnp.ones_like(vj))
            return _dot(p_hi, v1) + _dot(p_lo, v1)

        def step(off, width, carry, diag):
            off = pl.multiple_of(off, ATT)
            out = []
            for hh in (0, 1):
                m, acc = carry[hh]
                m, alpha, p_hi, p_lo = probs(off, width, m, hh, diag)
                out.append((m, alpha * acc + weighted(off, width, p_hi, p_lo, hh)))
            return tuple(out)

        one = (jnp.full((ATT, 1), NEG, F32), jnp.zeros((ATT, LANES), F32))
        carry = lax.fori_loop(0, i // 2, lambda j, cr: step(j * (2 * ATT), 2 * ATT, cr, False), (one, one))
        carry = lax.cond(i % 2 == 1, lambda cr: step((i - 1) * ATT, 2 * ATT, cr, True),
                         lambda cr: step(i * ATT, ATT, cr, True), carry)
        res = []
        for hh in (0, 1):
            m, acc = carry[hh]
            l = jnp.max(jnp.where(_head_mask(1 - hh), acc, 0.0), axis=1, keepdims=True)
            res.append((acc / l, m + jnp.log(l)))
        first = _head_mask(0)
        o = jnp.where(first, res[0][0], res[1][0])
        o_ref[...] = o
        lse_ref[...] = jnp.where(first, res[0][1], res[1][1])
        g = gate_ref[...]
        y_ref[...] = (o * (g * _sigmoid(g))).astype(BF16)
        at_end()

    blk = pl.BlockSpec((ATT, LANES), lambda p, i: (i, p))
    res = pl.pallas_call(
        body, name="fox_fwd", grid=(npair, nq),
        in_specs=[*_qkv_specs(), pl.BlockSpec((None, 2, S), lambda p, i: (p, 0, 0)), blk] + riding.in_specs,
        out_specs=[blk, blk, blk] + riding.out_specs,
        out_shape=[jax.ShapeDtypeStruct((S, D), F32)] * 2 + [jax.ShapeDtypeStruct((S, D), BF16)] + riding.out_shape,
        scratch_shapes=riding.scratch,
        compiler_params=_params(2))(qkv, qkv, qkv, ct, gate, *riding.ins)
    return res[0], res[1], res[2], res[3:]


def _gate_grads(dy, o, g):
    sg = _sigmoid(g)
    return dy * (g * sg), dy * o * (sg * (1.0 + g * (1.0 - sg)))


def _fox_bwd(qkv, ct, o, lse, dy, gate, riding):
    nq, npair = S // ATT, NH // 2
    ni, no = len(riding.ins), len(riding.outs)

    def body(q_ref, k_ref, v_ref, c_ref, o_ref, lse_ref, dy_ref, gate_ref, *rest):
        dq_ref, dk_ref, dvb_ref, dc_ref, dgate_ref = rest[ni:ni + 5]
        dv_ref = rest[ni + 5 + no]
        pair, i = pl.program_id(0), pl.program_id(1)
        at_end = riding.hooks(rest[:ni], rest[ni + 5:ni + 5 + no], *rest[ni + 6 + no:],
                              first=(pair == 0) & (i == 0), middle=(pair == npair // 2) & (i == 0),
                              last=(pair == npair - 1) & (i == nq - 1))

        @pl.when(i == 0)
        def _():
            dk_ref[...] = jnp.zeros_like(dk_ref)
            dv_ref[...] = jnp.zeros_like(dv_ref)
            dc_ref[...] = jnp.zeros_like(dc_ref)

        q2, lse2 = q_ref[...], lse_ref[...]
        do2, dgate = _gate_grads(dy_ref[...], o_ref[...], gate_ref[...])
        dgate_ref[...] = dgate.astype(BF16)
        do2b = do2.astype(BF16)
        prod = do2b.astype(F32) * o_ref[...]
        heads = []
        for hh in (0, 1):
            hm = _head_mask(hh)
            heads.append((jnp.where(hm, q2, jnp.zeros_like(q2)), jnp.where(hm, do2b, jnp.zeros_like(do2b)),
                          jnp.sum(jnp.where(hm, prod, 0.0), axis=1, keepdims=True),
                          jnp.max(jnp.where(hm, lse2, NEG), axis=1, keepdims=True)))

        def step(off, width, dqs, diag):
            off = pl.multiple_of(off, ATT)
            kj, vj = k_ref[pl.ds(off, width), :], v_ref[pl.ds(off, width), :]
            dk, dv, out = None, None, []
            for hh in (0, 1):
                qm, dom, delta, lse_h = heads[hh]
                s = _dot(qm, kj, NT) - c_ref[hh:hh + 1, pl.ds(off, width)]
                p = jnp.exp(s - lse_h)
                if diag:
                    row = i * ATT + lax.broadcasted_iota(jnp.int32, (ATT, width), 0)
                    col = off + lax.broadcasted_iota(jnp.int32, (ATT, width), 1)
                    p = jnp.where(col <= row, p, 0.0)
                ds = p * (_dot(dom, vj, NT) - delta)
                dc_ref[hh:hh + 1, pl.ds(off, width)] += -jnp.sum(ds, axis=0, keepdims=True)
                dsb = ds.astype(BF16)
                dk_h, dv_h = _dot(dsb, qm, TN), _dot(p.astype(BF16), dom, TN)
                dk, dv = (dk_h, dv_h) if dk is None else (dk + dk_h, dv + dv_h)
                out.append(dqs[hh] + _dot(dsb, kj))
            dk_ref[pl.ds(off, width), :] += dk
            dv_ref[pl.ds(off, width), :] += dv
            return tuple(out)

        zero = jnp.zeros((ATT, LANES), F32)
        dqs = lax.fori_loop(0, i // 2, lambda j, acc: step(j * (2 * ATT), 2 * ATT, acc, False), (zero, zero))
        dqs = lax.cond(i % 2 == 1, lambda acc: step((i - 1) * ATT, 2 * ATT, acc, True),
                       lambda acc: step(i * ATT, ATT, acc, True), dqs)
        dq_ref[...] = jnp.where(_head_mask(0), dqs[0], dqs[1]) * SCALE

        @pl.when(i == nq - 1)
        def _():
            dvb_ref[...] = dv_ref[...].astype(BF16)

        at_end()

    blk = pl.BlockSpec((ATT, LANES), lambda p, i: (i, p))
    full = pl.BlockSpec((S, LANES), lambda p, i: (0, p))
    cspec = pl.BlockSpec((None, 2, S), lambda p, i: (p, 0, 0))
    res = pl.pallas_call(
        body, name="fox_bwd", grid=(npair, nq),
        in_specs=[*_qkv_specs(), cspec, blk, blk, blk, blk] + riding.in_specs,
        out_specs=[blk, full, full, cspec, blk] + riding.out_specs,
        out_shape=[jax.ShapeDtypeStruct((S, D), F32)] * 2 + [jax.ShapeDtypeStruct((S, D), BF16),
                                                              jax.ShapeDtypeStruct((npair, 2, S), F32),
                                                              jax.ShapeDtypeStruct((S, D), BF16)]
                  + riding.out_shape,
        scratch_shapes=[pltpu.VMEM((S, LANES), F32)] + riding.scratch,
        compiler_params=_params(2))(qkv, qkv, qkv, ct, o, lse, dy, gate, *riding.ins)
    return res[0], res[1], res[2], res[3], res[4], res[5:]


def _both_heads(x):
    return jnp.concatenate([jnp.where(_head_mask(hh), x, jnp.zeros_like(x)) for hh in (0, 1)], axis=0)


def _per_head(col0, col1):
    return jnp.concatenate([jnp.broadcast_to(col0, (WINDOW, 1)), jnp.broadcast_to(col1, (WINDOW, 1))], axis=0)


def _unstack(x2):
    return jnp.where(_head_mask(0), x2[:WINDOW], x2[WINDOW:])


def _swa_valid(i, start):
    r = lax.broadcasted_iota(jnp.int32, (2 * WINDOW, 2 * WINDOW), 0)
    qabs = i * WINDOW + jnp.where(r >= WINDOW, r - WINDOW, r)
    kabs = start + lax.broadcasted_iota(jnp.int32, (2 * WINDOW, 2 * WINDOW), 1)
    return (kabs <= qabs) & (qabs - kabs < WINDOW)


def _swa_fwd(q, kdup, vdup, sinks_t, proj, gate_col):
    def body(q_ref, k_ref, v_ref, sk_ref, gate_ref, o_ref, lse_ref, y_ref):
        skv = sk_ref[...]
        first = _head_mask(0)
        for sb in range(SWQ):
            i = pl.program_id(1) * SWQ + sb
            rows = slice(sb * WINDOW, (sb + 1) * WINDOW)
            start = pl.multiple_of(jnp.maximum(i - 1, 0) * WINDOW, WINDOW)
            kk, vv = k_ref[pl.ds(start, 2 * WINDOW), :], v_ref[pl.ds(start, 2 * WINDOW), :]
            q2 = q_ref[rows, :]
            valid = _swa_valid(i, start)[:WINDOW]
            res = []
            for hh in (0, 1):
                hm = _head_mask(hh)
                sink = jnp.max(jnp.where(hm, skv, NEG), axis=1, keepdims=True)
                s = jnp.where(valid, _dot(jnp.where(hm, q2, jnp.zeros_like(q2)), kk, NT), NEG)
                m = jnp.maximum(jnp.max(s, axis=1, keepdims=True), sink)
                p = jnp.exp(s - m)
                l = jnp.sum(p, axis=1, keepdims=True) + jnp.exp(sink - m)
                res.append((_dot(p.astype(BF16), vv) / l, m + jnp.log(l)))
            o = jnp.where(first, res[0][0], res[1][0])
            o_ref[rows, :] = o
            lse_ref[rows, :] = jnp.where(first, res[0][1], res[1][1])
            g = gate_ref[rows, :]
            y_ref[rows, :] = (o * (g * _sigmoid(g))).astype(BF16)

    blk = pl.BlockSpec((SWQ * WINDOW, LANES), lambda p, i: (i, p))
    gate = pl.BlockSpec((SWQ * WINDOW, LANES), lambda p, i: (i, gate_col + p))
    full = pl.BlockSpec((S, LANES), lambda p, i: (0, p // 2))
    return pl.pallas_call(
        body, name="swa_fwd", grid=(NH // 2, S // (SWQ * WINDOW)),
        in_specs=[blk, full, full, pl.BlockSpec((1, LANES), lambda p, i: (0, p)), gate],
        out_specs=[blk, blk, blk],
        out_shape=[jax.ShapeDtypeStruct((S, D), F32)] * 2 + [jax.ShapeDtypeStruct((S, D), BF16)],
        compiler_params=_params(2))(q, kdup, vdup, sinks_t, proj)


def _swa_bwd(q, kdup, vdup, sinks_t, o, lse, dy, proj, gate_col, qg, rope):
    e = _seg_mat(LANES)

    def body(q_ref, k_ref, v_ref, sk_ref, o_ref, lse_ref, dy_ref, gate_ref, x_ref, qg_ref, c_ref, a_ref, b_ref, e_ref,
             dx_ref, dk_ref, dv_ref, dsk_ref, dgate_ref, dg_ref, dq_ref):
        @pl.when(pl.program_id(1) == 0)
        def _():
            dk_ref[...] = jnp.zeros_like(dk_ref)
            dv_ref[...] = jnp.zeros_like(dv_ref)
            dsk_ref[...] = jnp.zeros_like(dsk_ref)
            dg_ref[...] = jnp.zeros_like(dg_ref)

        skv = sk_ref[...]
        first = _head_mask(0)
        sink = _per_head(*[jnp.max(jnp.where(_head_mask(hh), skv, NEG), axis=1, keepdims=True) for hh in (0, 1)])
        for sb in range(SWQ):
            i = pl.program_id(1) * SWQ + sb
            rows = slice(sb * WINDOW, (sb + 1) * WINDOW)
            start = pl.multiple_of(jnp.maximum(i - 1, 0) * WINDOW, WINDOW)
            kk, vv = k_ref[pl.ds(start, 2 * WINDOW), :], v_ref[pl.ds(start, 2 * WINDOW), :]
            do2, dgate = _gate_grads(dy_ref[rows, :], o_ref[rows, :], gate_ref[rows, :])
            dgate_ref[rows, :] = dgate.astype(BF16)
            do2b = do2.astype(BF16)
            prod, lse2 = do2b.astype(F32) * o_ref[rows, :], lse_ref[rows, :]
            qs, dos = _both_heads(q_ref[rows, :]), _both_heads(do2b)
            delta = jnp.concatenate([jnp.sum(jnp.where(_head_mask(hh), prod, 0.0), axis=1, keepdims=True)
                                     for hh in (0, 1)], axis=0)
            lse_h = jnp.concatenate([jnp.max(jnp.where(_head_mask(hh), lse2, NEG), axis=1, keepdims=True)
                                     for hh in (0, 1)], axis=0)
            p = jnp.where(_swa_valid(i, start), jnp.exp(_dot(qs, kk, NT) - lse_h), 0.0)
            dsb = (p * (_dot(dos, vv, NT) - delta)).astype(BF16)
            dk_ref[pl.ds(start, 2 * WINDOW), :] += _dot(dsb, qs, TN)
            dv_ref[pl.ds(start, 2 * WINDOW), :] += _dot(p.astype(BF16), dos, TN)
            dq_ref[rows, :] = _unstack(_dot(dsb, kk)) * SCALE
            t = jnp.exp(sink - lse_h) * delta
            dsk_ref[...] += -jnp.where(first, jnp.sum(t[:WINDOW], axis=0, keepdims=True),
                                       jnp.sum(t[WINDOW:], axis=0, keepdims=True))
        dyv = _rope_t(dq_ref[...], c_ref[...], a_ref[...], b_ref[...])
        xv, ev = x_ref[...], e_ref[...]
        r = _head_rstd(xv, ev)
        xh = xv * r
        dg_ref[...] += jnp.sum(dyv * xh, axis=0, keepdims=True)
        gy = dyv * qg_ref[...]
        seg = _spread(_dot_split(gy * xh, ev, 2) * (1.0 / HD), LANES)
        dx_ref[...] = (r * (gy - xh * seg)).astype(BF16)

    rows = SWQ * WINDOW
    blk = pl.BlockSpec((rows, LANES), lambda p, i: (i, p))
    full = pl.BlockSpec((S, LANES), lambda p, i: (0, p // 2))
    acc = pl.BlockSpec((S, LANES), lambda p, i: (0, p))
    sk = pl.BlockSpec((1, LANES), lambda p, i: (0, p))
    gate = pl.BlockSpec((rows, LANES), lambda p, i: (i, gate_col + p))
    tab = pl.BlockSpec((rows, LANES), lambda p, i: (i, 0))
    return pl.pallas_call(
        body, name="swa_bwd", grid=(NH // 2, S // rows),
        in_specs=[blk, full, full, sk, blk, blk, blk, gate, blk, sk, tab, tab, tab,
                  pl.BlockSpec(e.shape, lambda p, i: (0, 0))],
        out_specs=[blk, acc, acc, sk, blk, sk],
        out_shape=[jax.ShapeDtypeStruct((S, D), BF16), jax.ShapeDtypeStruct((S, D), F32),
                   jax.ShapeDtypeStruct((S, D), F32), jax.ShapeDtypeStruct((1, D), F32),
                   jax.ShapeDtypeStruct((S, D), BF16), jax.ShapeDtypeStruct((1, D), F32)],
        scratch_shapes=[pltpu.VMEM((rows, LANES), F32)],
        compiler_params=_params(2))(q, kdup, vdup, sinks_t, o, lse, dy, proj, proj, qg, *rope, e)


def _adamw_math(w, g, m, v):
    m = ADAM_B1 * m + (1.0 - ADAM_B1) * g
    v = ADAM_B2 * v + (1.0 - ADAM_B2) * jnp.square(g)
    m_hat = m / (1.0 - ADAM_B1 ** ADAM_STEP)
    v_hat = v / (1.0 - ADAM_B2 ** ADAM_STEP)
    delta = -ADAM_LR * (m_hat / (jnp.sqrt(v_hat) + ADAM_EPS) + ADAM_WD * w)
    return delta, m, v


def _adamw_small(ws, gs, ms, vs):
    k = len(ws)

    def body(*refs):
        for p in range(k):
            w_ref, g_ref, m_ref, v_ref = (refs[q * k + p] for q in range(4))
            d, mo, vo = _adamw_math(w_ref[...], g_ref[...], m_ref[...], v_ref[...])
            refs[4 * k + p][...], refs[5 * k + p][...], refs[6 * k + p][...] = d, mo, vo

    res = pl.pallas_call(
        body, name="adamw_small",
        out_shape=[jax.ShapeDtypeStruct(t.shape, F32) for t in ws] * 3)(*ws, *gs, *ms, *vs)
    return res[:k], res[k:2 * k], res[2 * k:]


SUM_TILES = (512, 256, 128)


FLAT_BLOCK = 257 * 1024


def _tiles(shape, axis, lead=0, halves=False):
    if len(shape) == 1:
        count = shape[0] // FLAT_BLOCK
        return (FLAT_BLOCK,), count, lambda pos, *lead_idx: (sum(k * count for k in lead_idx) + pos,)
    r, c = shape
    tile = next(t for t in SUM_TILES if (shape[axis] // (2 if halves else 1)) % t == 0)
    blk = (tile, c) if axis == 0 else (r, tile)
    count = shape[axis] // tile

    def index(pos, *lead_idx):
        return tuple(lead_idx) + ((pos, 0) if axis == 0 else (0, pos))

    return (None,) * lead + blk, count, index


def _adamw_halves(w, g_mine, g_theirs, m, v, axis, name):
    blk, count, index = _tiles(w.shape, axis, halves=True)
    per_half = count // 2

    def body(w_ref, a_ref, b_ref, m_ref, v_ref, g_ref, d_ref, mo_ref, vo_ref):
        is_mine = pl.program_id(0) // per_half == lax.axis_index("c")
        g = jnp.where(is_mine, a_ref[...], b_ref[...])
        g_ref[...] = g
        d_ref[...], mo_ref[...], vo_ref[...] = _adamw_math(w_ref[...], g, m_ref[...], v_ref[...])

    spec = pl.BlockSpec(blk, lambda i: index(i))
    half = pl.BlockSpec(blk, lambda i: index(i % per_half))
    return pl.pallas_call(
        body, name=name, grid=(count,), in_specs=[spec, half, half, spec, spec], out_specs=[spec] * 4,
        out_shape=[jax.ShapeDtypeStruct(w.shape, F32)] * 4, compiler_params=_params(1))(w, g_mine, g_theirs, m, v)


def _chip_sum(blocks, from_sibling, axis, name):
    flat = blocks.ndim == 1
    blk, count, index = _tiles((from_sibling.shape[0] // NCHIP,) if flat else from_sibling.shape[1:], axis, lead=1)

    def body(lo_ref, hi_ref, p_ref, o32, o16):
        mine = jnp.where(lax.axis_index("c") == 0, lo_ref[...], hi_ref[...])
        acc = mine + p_ref[...]
        o32[...] = acc
        o16[...] = acc.astype(BF16)

    half = pl.BlockSpec(blk, lambda k, i: index(i, k))
    if flat:
        lo = pl.BlockSpec(blk, lambda k, i: (2 * count * k + i,))
        hi = pl.BlockSpec(blk, lambda k, i: (2 * count * k + count + i,))
    else:
        lo, hi = half, pl.BlockSpec(blk, lambda k, i: index(i + count, k))
    return pl.pallas_call(
        body, name=name, grid=(NCHIP, count), in_specs=[lo, hi, half], out_specs=[half, half],
        out_shape=[jax.ShapeDtypeStruct(from_sibling.shape, F32), jax.ShapeDtypeStruct(from_sibling.shape, BF16)],
        compiler_params=_params(2))(blocks, blocks, from_sibling)


def _mesh_sum(chip_sums, parts, axis, name):
    flat = chip_sums.ndim == 1
    one = (chip_sums.shape[0] // NCHIP,) if flat else chip_sums.shape[1:]
    blk, count, index = _tiles(one, axis)
    n = NCHIP - 1

    def body(chip_ref, a_ref, *refs):
        acc = a_ref[...]
        for k in range(n):
            acc = acc + refs[k][...].astype(F32)
        refs[n][...] = acc

    spec = pl.BlockSpec(blk, lambda i, chip: index(i))
    if flat:
        mine = pl.BlockSpec(blk, lambda i, chip: (chip[0] * count + i,))
        part = [pl.BlockSpec(blk, lambda i, chip, k=k: (k * count + i,)) for k in range(n)]
    else:
        mine = pl.BlockSpec((None,) + blk, lambda i, chip: (chip[0],) + index(i))
        part = [pl.BlockSpec((None,) + blk, lambda i, chip, k=k: (k,) + index(i)) for k in range(n)]
    return pl.pallas_call(
        body, name=name,
        grid_spec=pltpu.PrefetchScalarGridSpec(num_scalar_prefetch=1, grid=(count,), in_specs=[mine] + part,
                                               out_specs=spec),
        out_shape=jax.ShapeDtypeStruct(one, F32),
        compiler_params=_params(1))(_chip(_coords()).astype(jnp.int32).reshape(1), chip_sums, *([parts] * n))


def _sum_stack(parts, name):
    n = parts.shape[0]

    def body(p_ref, o_ref):
        acc = p_ref[0]
        for k in range(1, n):
            acc = acc + p_ref[k]
        o_ref[...] = acc

    return pl.pallas_call(body, name=name, out_shape=jax.ShapeDtypeStruct(parts.shape[1:], F32))(parts)


def _coords():
    return lax.axis_index("x"), lax.axis_index("y"), lax.axis_index("c")


def _chip(who):
    return 2 * who[0] + who[1]


def _flip(who, mask):
    return tuple((1 - v) if b else v for v, b in zip(who, mask))


def _transfer(transfers, t, I, O, ssem, rsem, receiving):
    tr, me = transfers[t], _coords()
    peer = _flip(me, tr["mask"])
    return pltpu.make_async_remote_copy(
        src_ref=tr["src"](I, O, me), dst_ref=tr["dst"](I, O, peer if receiving else me),
        send_sem=ssem.at[t], recv_sem=rsem.at[t], device_id=peer, device_id_type=MESH)


def _start_transfers(transfers, I, O, ssem, rsem, onward):
    arrived = set()
    for t, tr in enumerate(transfers):
        after = tr.get("after")
        if (after is not None) != onward:
            continue
        if after is not None and after not in arrived:
            _transfer(transfers, after, I, O, ssem, rsem, True).wait_recv()
            arrived.add(after)
        _transfer(transfers, t, I, O, ssem, rsem, False).start()


def _finish_transfers(transfers, I, O, ssem, rsem):
    passed_on = {tr["after"] for tr in transfers if tr.get("after") is not None}
    for t in range(len(transfers)):
        if t not in passed_on:
            _transfer(transfers, t, I, O, ssem, rsem, True).wait_recv()
    for t in range(len(transfers)):
        _transfer(transfers, t, I, O, ssem, rsem, False).wait_send()


def _own_copies(own, I, O, stage, lsem, leg):
    for n, (src, dst) in enumerate(own):
        me = _coords()
        bring =pltpu.make_async_copy(src(I, O, me), stage[n], lsem.at[2 * n])
        put = pltpu.make_async_copy(stage[n], dst(I, O, me), lsem.at[2 * n + 1])
        if leg == 0:
            bring.start()
        elif leg == 1:
            bring.wait()
            put.start()
        else:
            put.wait()


def _own_scratch(own, ins):
    return [pltpu.VMEM(ins[n].shape, ins[n].dtype) for n in range(len(own))], pltpu.SemaphoreType.DMA((max(2 * len(own), 1),))


def _exchange(name, ins, outs, transfers, own=()):
    ni, no = len(ins), len(outs)
    nt = len(transfers)
    stages, stage_sems = _own_scratch(own, ins)

    def body(*refs):
        I, O = refs[:ni], refs[ni:ni + no]
        ssem, rsem, lsem = refs[ni + no:ni + no + 3]
        stage = refs[ni + no + 3:]
        _own_copies(own, I, O, stage, lsem, 0)
        _start_transfers(transfers, I, O, ssem, rsem, False)
        _own_copies(own, I, O, stage, lsem, 1)
        _start_transfers(transfers, I, O, ssem, rsem, True)
        _finish_transfers(transfers, I, O, ssem, rsem)
        _own_copies(own, I, O, stage, lsem, 2)

    hbm = pl.BlockSpec(memory_space=pltpu.HBM)
    return pl.pallas_call(
        body, name=name, in_specs=[hbm] * ni, out_specs=[hbm] * no,
        out_shape=[jax.ShapeDtypeStruct(s, d) for s, d in outs],
        scratch_shapes=[pltpu.SemaphoreType.DMA((nt,)), pltpu.SemaphoreType.DMA((nt,)), stage_sems] + stages,
        compiler_params=pltpu.CompilerParams(has_side_effects=True, vmem_limit_bytes=VMEM_LIMIT))(*ins)


CHIP_MASKS = [(0, 1, 0), (1, 0, 0), (1, 1, 0)]
SIBLING = (0, 0, 1)


def _half(shape2d, axis, which):
    n = shape2d[axis] // 2
    cut = pl.ds(pl.multiple_of(which * n, n), n)
    return (cut, slice(None)) if axis == 0 else (slice(None), cut)


class _Riding:
    def __init__(self, transfers, ins, outs, own=()):
        self.transfers, self.ins, self.outs, self.own = transfers, list(ins), list(outs), list(own)
        hbm = pl.BlockSpec(memory_space=pltpu.HBM)
        self.in_specs, self.out_specs = [hbm] * len(self.ins), [hbm] * len(self.outs)
        self.out_shape = [jax.ShapeDtypeStruct(s, d) for s, d in self.outs]
        stages, stage_sems = _own_scratch(self.own, self.ins)
        self.scratch = [pltpu.SemaphoreType.DMA((max(len(transfers), 1),))] * 2 + [stage_sems] + stages

    def alone(self, name):
        return _exchange(name, self.ins, self.outs, self.transfers, self.own)

    def hooks(self, I, O, ssem, rsem, lsem, *stage, first, middle, last):
        tr, own = self.transfers, self.own

        @pl.when(first)
        def _():
            _own_copies(own, I, O, stage, lsem, 0)
            _start_transfers(tr, I, O, ssem, rsem, False)

        if own or any(t.get("after") is not None for t in tr):
            @pl.when(middle)
            def _():
                _own_copies(own, I, O, stage, lsem, 1)
                _start_transfers(tr, I, O, ssem, rsem, True)

        def at_end():
            @pl.when(last)
            def _():
                _finish_transfers(tr, I, O, ssem, rsem)
                _own_copies(own, I, O, stage, lsem, 2)

        return at_end


def _stretch(n, pos):
    return (pl.ds(pos * n if isinstance(pos, int) else pl.multiple_of(pos * n, n), n),)


def _gather_plan(shards, axes):
    def half(a, who):
        if shards[a].ndim == 1:
            return _stretch(shards[a].shape[0] // 2, who[2])
        return _half(shards[a].shape, axes[a], who[2])

    def landed(a, chip, who):
        if shards[a].ndim == 1:
            return _stretch(shards[a].shape[0] // 2, 2 * chip + who[2])
        return (chip,) + half(a, who)

    over_ici, onward = [], []
    for a in range(len(shards)):
        for mask in CHIP_MASKS:
            over_ici.append(dict(
                mask=mask,
                src=lambda I, O, me, a=a: I[a].at[half(a, me)],
                dst=lambda I, O, who, a=a: O[a].at[landed(a, _chip(who), who)]))
            onward.append(dict(
                mask=SIBLING, after=len(over_ici) - 1,
                src=lambda I, O, me, a=a, mask=mask: O[a].at[landed(a, _chip(_flip(me, mask)), me)],
                dst=lambda I, O, who, a=a, mask=mask: O[a].at[landed(a, _chip(_flip(who, mask)), who)]))
    outs = [((NCHIP * s.shape[0],) if s.ndim == 1 else (NCHIP,) + s.shape, s.dtype) for s in shards]

    def whole(a, chip):
        return _stretch(shards[a].shape[0], chip) if shards[a].ndim == 1 else (chip,)

    own = [(lambda I, O, me, a=a: I[a], lambda I, O, me, a=a: O[a].at[whole(a, _chip(me))])
           for a in range(len(shards))]
    return over_ici + onward, outs, own


def _gather_shards(shards, axes):
    transfers, outs, own = _gather_plan(shards, axes)
    return _exchange("gather_weights", shards, outs, transfers, own)


def _halves_plan(blocks, axes):
    def cut(a, which):
        return (slice(None),) + _half(blocks[a].shape[1:], axes[a], which)

    transfers, outs = [], []
    for a, (b, ax) in enumerate(zip(blocks, axes)):
        if b.ndim == 1:
            h = b.shape[0] // NCHIP // 2
            for k in range(NCHIP):
                transfers.append(dict(mask=SIBLING,
                                      src=lambda I, O, me, a=a, k=k, h=h: I[a].at[_stretch(h, 2 * k + 1 - me[2])],
                                      dst=lambda I, O, who, a=a, k=k, h=h: O[a].at[_stretch(h, k)]))
            outs.append(((NCHIP * h,), b.dtype))
        else:
            transfers.append(dict(mask=SIBLING, src=lambda I, O, me, a=a: I[a].at[cut(a, 1 - me[2])],
                                  dst=lambda I, O, who, a=a: O[a]))
            shape = list(b.shape)
            shape[ax + 1] //= 2
            outs.append((tuple(shape), b.dtype))
    return transfers, outs


def _scatter_plan(tb):
    def slot(a, k):
        return (k,) if tb[a].ndim == 3 else _stretch(tb[a].shape[0] // NCHIP, k)

    transfers = []
    for a in range(len(tb)):
        for n, mask in enumerate(CHIP_MASKS):
            transfers.append(dict(
                mask=mask,
                src=lambda I, O, me, a=a, mask=mask: I[a].at[slot(a, _chip(_flip(me, mask)))],
                dst=lambda I, O, who, a=a, n=n: O[a].at[slot(a, n)]))
    outs = [((3,) + t.shape[1:] if t.ndim == 3 else (3 * (t.shape[0] // NCHIP),), t.dtype) for t in tb]
    return transfers, outs


def _last_exchange(vec, halves):
    def slot(who):
        return 4 * who[0] + 2 * who[1] + who[2]

    masks = [(m >> 2 & 1, m >> 1 & 1, m & 1) for m in range(1, 8)]
    transfers = [dict(mask=mask, src=lambda I, O, me: I[0], dst=lambda I, O, who: O[0].at[slot(who)])
                 for mask in masks]
    transfers += [dict(mask=SIBLING, src=lambda I, O, me, a=a: I[a], dst=lambda I, O, who, a=a: O[a])
                  for a in range(1, 1 + len(halves))]
    own = [(lambda I, O, me: I[0], lambda I, O, me: O[0].at[slot(me)])]
    outs = [((8,) + vec.shape, vec.dtype)] + [(t.shape, t.dtype) for t in halves]
    res = _exchange("last_exchange", [vec] + list(halves), outs, transfers, own)
    return res[0], res[1:]


def _rope_tables(positions):
    half = ROT // 2
    inv_freq = jnp.power(jnp.float32(THETA), -jnp.arange(0, ROT, 2, dtype=F32) / ROT)
    ang = positions.astype(F32)[:, None] * inv_freq[None, :]
    cos, sin = jnp.cos(ang), jnp.sin(ang)
    one, zero, z8 = jnp.ones((S, HD - ROT), F32), jnp.zeros((S, HD - ROT), F32), jnp.zeros((S, half), F32)
    c = jnp.concatenate([cos, cos, one], axis=1)
    a = jnp.concatenate([-sin, z8, zero], axis=1)
    b = jnp.concatenate([z8, sin, zero], axis=1)
    return tuple(jnp.tile(t, (1, 2)) for t in (c, a, b))


def _tile_heads(g, w):
    return jnp.tile(g.reshape(1, HD), (1, w // HD))


def _fold_heads(dg):
    return dg.reshape(-1, HD).sum(axis=0)


def _pad_lanes(a):
    return jnp.pad(a, ((0, 0), (0, LANES - a.shape[1])))


def _local_step(x, target, positions, wt, fetch, late_weights, begin_reduce):
    rope = _rope_tables(positions)
    w1t = wt["w_in_a_t"]
    f_row = 3 * D // LANES
    wg_t = w1t[3 * D + NH:]
    b_pad = _pad_lanes(wt["b_forget"].reshape(1, NH))
    qg_a, kg_a = _tile_heads(wt["qnorm_a_g"], D), _tile_heads(wt["knorm_a_g"], D)
    qg_b, kg_b = _tile_heads(wt["qnorm_b_g"], D), _tile_heads(wt["knorm_b_g"], KVW)
    norm_a, kv_g, norm_b = wt["norm_a_g"].reshape(1, D), wt["kv_norm_g"].reshape(1, D), wt["norm_b_g"].reshape(1, D)
    sinks_t = jnp.repeat(wt["sinks"].reshape(1, NH), HD, axis=1)

    (u_a,) = _rmsnorm_fwd(x, [norm_a], "norm_a")
    qkv, qkv_a = _proj_a(u_a, w1t, qg_a, kg_a)
    fpad = _mm("proj_f", S, LANES, [(u_a, _a_rows(D), w1t, _b_rows(D, row0=f_row, tn=LANES), NT)], tn=LANES)
    gate_a = _mm("proj_gate_a", S, D, [(u_a, _a_rows(D), wg_t, _b_rows(D, tn=TN_WIDE), NT)], tn=TN_WIDE)
    ct = _forget_cumsum(fpad, b_pad)
    ct2 = ct[:NH].reshape(NH // 2, 2, S)
    o_a, lse_a, y_a, fetched = _fox_fwd(qkv_a, ct2, gate_a, fetch)
    wt = {**wt, **late_weights(fetched)}
    w_in_b = wt["w_in_b"]
    h1, u_kv, u_b = _out_norms(y_a, wt["w_out_a"], x, [kv_g, norm_b])
    pb, kv, q_b, kdup, vdup = _proj_b(u_b, u_kv, w_in_b, wt["w_kv"], qg_b, kg_b, rope)
    gate_b_col = D // LANES
    o_b, lse_b, y_b = _swa_fwd(q_b, kdup, vdup, sinks_t, pb, gate_b_col)
    d_out, d_out_b, sq = _out_loss(y_b, wt["w_out_b"], h1, target)

    g = {}
    g["w_out_b"] = _mm("dw_out_b", D, D, [(y_b, _a_cols(S), d_out_b, _b_cols(S, tn=TN_WIDE), TN)], tn=TN_WIDE)
    d_y_b = _mm("dy_b", S, D, [(d_out_b, _a_rows(D), wt["w_out_b"], _b_rows(D, tn=TN_WIDE), NT)], tn=TN_WIDE)
    d_qb_raw, dkdup, dvdup, dsk, d_gate_b, dg = _swa_bwd(q_b, kdup, vdup, sinks_t, o_b, lse_b, d_y_b, pb, gate_b_col,
                                                         qg_b, rope)
    g["sinks"] = dsk[0, ::HD]
    g["qnorm_b_g"] = _fold_heads(dg)
    g["w_in_b"] = jnp.concatenate([
        _mm("dw_in_b_q", D, D, [(u_b, _a_cols(S), d_qb_raw, _b_cols(S), TN)], stacked=True),
        _mm("dw_in_b_gate", D, D, [(u_b, _a_cols(S), d_gate_b, _b_cols(S), TN)], stacked=True)], axis=0)
    d_kv, dg = _kv_bwd(dkdup, dvdup, kv, kg_b, rope)
    g["knorm_b_g"] = _fold_heads(dg)
    g["w_kv"] = _mm("dw_kv", D, 2 * KVW, [(u_kv, _a_cols(S), d_kv, _b_cols(S), TN)])
    d_h1, d_h1_b, g["kv_norm_g"], g["norm_b_g"] = _du_b_norms(d_qb_raw, d_gate_b, d_kv, w_in_b, wt["w_kv"], h1, kv_g,
                                                              norm_b, d_out)
    g["w_out_a"] = _mm("dw_out_a", D, D, [(y_a, _a_cols(S), d_h1_b, _b_cols(S, tn=TN_WIDE), TN)], tn=TN_WIDE)
    late = {n: g[n] for n in LATE}
    d_y_a, halves = _mm("dy_a", S, D, [(d_h1_b, _a_rows(D), wt["w_out_a"], _b_rows(D, tn=TN_WIDE), NT)],
                        tn=TN_WIDE, riding=begin_reduce(late))
    riding, so_far = begin_reduce(late, halves)
    dq_a, dk_a, dv_a, dct, d_gate_a, arrived = _fox_bwd(qkv_a, ct2, o_a, lse_a, d_y_a, gate_a, riding)
    dct_pad = jnp.pad(dct.reshape(NH, S), ((0, LANES - NH), (0, 0)))
    d_f, db = _forget_bwd(dct_pad, fpad, b_pad)
    g["b_forget"] = db[0, :NH]
    d_q_raw, dg = _headnorm_bwd(qkv, 0, qg_a, dq_a, None, "qnorm_a_bwd")
    g["qnorm_a_g"] = _fold_heads(dg)
    d_k_raw, dg = _headnorm_bwd(qkv, 1, kg_a, dk_a, None, "knorm_a_bwd")
    g["knorm_a_g"] = _fold_heads(dg)
    rows, gw = 4 * D + NH, None
    for n, t, row0 in (("q", d_q_raw, 0), ("k", d_k_raw, D), ("v", dv_a, 2 * D)):
        gw = _mm("dw_in_a_" + n, D, D, [(t, _a_cols(S), u_a, _b_cols(S, tn=TN_WIDE), TN)], tn=TN_WIDE,
                 rows_of=(gw, rows, row0))
    gw = _mm("dw_in_a_f", LANES, D, [(d_f, _a_cols(S, tm=LANES), u_a, _b_cols(S, tn=TN_WIDE), TN)], tm=LANES,
             tn=TN_WIDE, rows_of=(gw, rows, 3 * D))
    g["w_in_a"] = _mm("dw_in_a_gate", D, D, [(d_gate_a, _a_cols(S), u_a, _b_cols(S, tn=TN_WIDE), TN)], tn=TN_WIDE,
                      rows_of=(gw, rows, 3 * D + NH))
    first = {"w_in_a": g["w_in_a"]}
    riding, so_far_first = begin_reduce(first, begin_reduce(first).alone("sibling_halves_w_in_a"))
    d_u_a, arrived_first = _mm("du_a", S, D, [
        (d_q_raw, _a_rows(D), w1t, _b_cols(D, row=0, tn=TN_WIDE), None),
        (d_k_raw, _a_rows(D), w1t, _b_cols(D, row=1, tn=TN_WIDE), None),
        (dv_a, _a_rows(D), w1t, _b_cols(D, row=2, tn=TN_WIDE), None),
        (d_gate_a, _a_rows(D), wg_t, _b_cols(D, tn=TN_WIDE), None),
        (d_f, _a_rows(LANES), w1t, _b_cols(LANES, row=f_row, tn=TN_WIDE), None)], tn=TN_WIDE, riding=riding)
    d_x, _, g["norm_a_g"] = _rmsnorm_bwd(x, [norm_a], [d_u_a], d_h1, "norm_a_bwd")
    return sq, d_x, g, (list(so_far_first) + list(so_far), list(arrived_first) + list(arrived))


BIG = ["w_in_a", "w_out_a", "w_kv", "w_in_b", "w_out_b"]
LATE = BIG[1:]
SPLIT = {"w_in_a": None, "w_out_a": 0, "w_kv": 0, "w_in_b": 0, "w_out_b": 0}
SMALL = ["norm_a_g", "b_forget", "qnorm_a_g", "knorm_a_g", "kv_norm_g", "knorm_b_g", "norm_b_g", "qnorm_b_g", "sinks"]
NAMES = ["norm_a_g", "w_in_a", "b_forget", "qnorm_a_g", "knorm_a_g", "w_out_a", "kv_norm_g", "w_kv", "knorm_b_g",
         "norm_b_g", "w_in_b", "qnorm_b_g", "sinks", "w_out_b"]


def _pack(vals):
    flat = []
    for v in vals:
        v = v.reshape(-1)
        flat.append(jnp.pad(v, (0, -v.shape[0] % LANES)))
    flat = jnp.concatenate(flat)
    flat = jnp.pad(flat, (0, -flat.shape[0] % (8 * LANES)))
    return flat.reshape(-1, LANES)


def _unpack(packed, shapes):
    flat, out, off = packed.reshape(-1), [], 0
    for s in shapes:
        n = int(np.prod(s))
        out.append(flat[off:off + n].reshape(s))
        off += n + (-n % LANES)
    return out


def kernel(x, positions, norm_a_g, w_in_a, b_forget, qnorm_a_g, knorm_a_g, w_out_a, kv_norm_g, w_kv, knorm_b_g, norm_b_g, w_in_b, qnorm_b_g, sinks, w_out_b, loss_target, m_norm_a_g, m_w_in_a, m_b_forget, m_qnorm_a_g, m_knorm_a_g, m_w_out_a, m_kv_norm_g, m_w_kv, m_knorm_b_g, m_norm_b_g, m_w_in_b, m_qnorm_b_g, m_sinks, m_w_out_b, v_norm_a_g, v_w_in_a, v_b_forget, v_qnorm_a_g, v_knorm_a_g, v_w_out_a, v_kv_norm_g, v_w_kv, v_knorm_b_g, v_norm_b_g, v_w_in_b, v_qnorm_b_g, v_sinks, v_w_out_b):
    w = dict(norm_a_g=norm_a_g, w_in_a=w_in_a, b_forget=b_forget, qnorm_a_g=qnorm_a_g, knorm_a_g=knorm_a_g,
             w_out_a=w_out_a, kv_norm_g=kv_norm_g, w_kv=w_kv, knorm_b_g=knorm_b_g, norm_b_g=norm_b_g,
             w_in_b=w_in_b, qnorm_b_g=qnorm_b_g, sinks=sinks, w_out_b=w_out_b)
    m = dict(norm_a_g=m_norm_a_g, w_in_a=m_w_in_a, b_forget=m_b_forget, qnorm_a_g=m_qnorm_a_g, knorm_a_g=m_knorm_a_g,
             w_out_a=m_w_out_a, kv_norm_g=m_kv_norm_g, w_kv=m_w_kv, knorm_b_g=m_knorm_b_g, norm_b_g=m_norm_b_g,
             w_in_b=m_w_in_b, qnorm_b_g=m_qnorm_b_g, sinks=m_sinks, w_out_b=m_w_out_b)
    v = dict(norm_a_g=v_norm_a_g, w_in_a=v_w_in_a, b_forget=v_b_forget, qnorm_a_g=v_qnorm_a_g, knorm_a_g=v_knorm_a_g,
             w_out_a=v_w_out_a, kv_norm_g=v_kv_norm_g, w_kv=v_w_kv, knorm_b_g=v_knorm_b_g, norm_b_g=v_norm_b_g,
             w_in_b=v_w_in_b, qnorm_b_g=v_qnorm_b_g, sinks=v_sinks, w_out_b=v_w_out_b)
    my_chip = 2 * lax.axis_index("x") + lax.axis_index("y")

    def shard2d(t, n):
        if n == "w_in_a":
            return jnp.transpose(t, (2, 0, 1)).reshape(-1)
        return t.reshape(t.shape[-2:])

    def unflat(t, n):
        return jnp.transpose(t.reshape(-1, 1, D), (1, 2, 0)) if n == "w_in_a" else t.reshape(w[n].shape)

    w2d = {n: shard2d(w[n], n) for n in BIG}

    norm_a_rows = jnp.broadcast_to(norm_a_g.reshape(1, D // NCHIP), (2 * SUBLANES, D // NCHIP))
    w1t, norm_rows = _gather_shards([w2d["w_in_a"].astype(BF16), norm_a_rows], [SPLIT["w_in_a"], 0])
    wt = {"w_in_a_t": w1t.reshape(-1, D), "norm_a_g": norm_rows[:, 0, :].reshape(1, D)}
    for n in SMALL[1:]:
        wt[n] = w[n]
    late_shards = [w2d[n].astype(BF16) for n in LATE]
    late_axes = [SPLIT[n] for n in LATE]
    transfers, outs, own = _gather_plan(late_shards, late_axes)
    fetch = _Riding(transfers, late_shards, outs, own)

    def late_weights(fetched):
        return {n: t if n == "w_in_b" else t.reshape(-1, t.shape[2]) for n, t in zip(LATE, fetched)}

    def as_blocks(t):
        if t.ndim == 3:
            return t
        return t.reshape(-1) if t.shape[0] % (SUBLANES * NCHIP) else t.reshape(NCHIP, -1, t.shape[1])

    def begin_reduce(grads, halves=None):
        names = list(grads)
        axes = [SPLIT[n] for n in names]
        blocks = [as_blocks(grads[n]) for n in names]
        if halves is None:
            transfers, outs = _halves_plan(blocks, axes)
            return _Riding(transfers, blocks, outs)
        sums = [_chip_sum(blk, part, ax, "chip_sum_" + n) for n, ax, blk, part in zip(names, axes, blocks, halves)]
        bf16 = [s[1] for s in sums]
        transfers, outs = _scatter_plan(bf16)
        return _Riding(transfers, bf16, outs), [s[0] for s in sums]

    sq, d_x, g, (chip_f32, arrived) = _local_step(x[0], loss_target[0], positions, wt, fetch, late_weights,
                                                  begin_reduce)

    axes = [SPLIT[n] for n in BIG]
    halves = [_mesh_sum(t32, parts, ax, "mesh_sum_" + n) for n, ax, t32, parts in zip(BIG, axes, chip_f32, arrived)]

    small_shapes = [(D,), (NH,), (HD,), (HD,), (D,), (HD,), (D,), (HD,), (NH,), (D,)]
    gathered_small, sibling_done = _last_exchange(_pack([g[n] for n in SMALL] + [sq]), halves)
    total = _sum_stack(gathered_small, "sum_small")
    small_g = dict(zip(SMALL, _unpack(total, small_shapes)[:-1]))
    loss = 0.5 * jnp.sum(_unpack(total, small_shapes)[-1]) / D
    small_g["norm_a_g"] = lax.dynamic_slice(small_g["norm_a_g"], (my_chip * (D // NCHIP),), (D // NCHIP,))

    res = {}
    for n, ax, mine_half, their_half in zip(BIG, axes, halves, sibling_done):
        out4 = _adamw_halves(w2d[n], mine_half, their_half, shard2d(m[n], n), shard2d(v[n], n), ax, "adamw_" + n)
        res[n] = tuple(unflat(t, n) for t in out4)
    row = lambda t: t.reshape(1, -1)
    small_out = _adamw_small(*[[row(d[n]) for n in SMALL] for d in (w, small_g, m, v)])
    for i, n in enumerate(SMALL):
        res[n] = tuple(t.reshape(w[n].shape) for t in (small_g[n],) + tuple(out[i] for out in small_out))

    outs = [loss, d_x[None]]
    for k in range(4):
        outs += [res[n][k] for n in NAMES]
    return tuple(outs)
```

```python
import numpy as np
import jax
import jax.numpy as jnp
from jax import lax
from jax.experimental import pallas as pl
from jax.experimental.pallas import tpu as pltpu

F32, BF16 = jnp.float32, jnp.bfloat16
S, D, HD, NH, NKV = 2048, 1024, 64, 16, 4
KVW = NKV * HD
WINDOW = 128
ROT = HD // 4
THETA = 500000.0
EPS = 1e-6
SCALE = HD ** -0.5
LANES = 128
SUBLANES = 8
NEG = -1e30
VMEM_LIMIT = 48 * 2 ** 20
ROWS = 512
ATT = 512
SWQ = 16
NCHIP = 4
ADAM_LR, ADAM_B1, ADAM_B2, ADAM_EPS, ADAM_WD, ADAM_STEP = 0.001, 0.9, 0.999, 1e-08, 0.01, 10
NT = (((1,), (1,)), ((), ()))
TN = (((0,), (0,)), ((), ()))
MESH = pl.DeviceIdType.MESH


def _params(n):
    return pltpu.CompilerParams(dimension_semantics=("arbitrary",) * n, vmem_limit_bytes=VMEM_LIMIT)


def _dot(a, b, dims=None):
    if dims is None:
        return jnp.dot(a, b, preferred_element_type=F32)
    return lax.dot_general(a, b, dims, preferred_element_type=F32)


def _dot_split(a, b, n):
    out, rest = None, a
    for _ in range(n):
        hi = rest.astype(BF16)
        term = _dot(hi, b)
        out = term if out is None else out + term
        rest = rest - hi.astype(F32)
    return out


def _seg_mat(w):
    e = (np.arange(w)[:, None] // HD == np.arange(LANES)[None, :]).astype(np.float32)
    return jnp.asarray(e, BF16)


def _spread(r, w):
    head = lax.broadcasted_iota(jnp.int32, (2 * LANES, w), 1) >> (HD.bit_length() - 1)
    row = lax.broadcasted_iota(jnp.int32, (2 * LANES, w), 0)
    et2 = jnp.where(head == (row & (LANES - 1)), 1.0, 0.0).astype(BF16)
    hi = r.astype(BF16)
    lo = (r - hi.astype(F32)).astype(BF16)
    return _dot(jnp.concatenate([hi, lo], axis=1), et2)


def _head_rstd(x, e):
    ss = _dot_split(x * x, e, 2)
    return _spread(lax.rsqrt(ss * (1.0 / HD) + EPS), x.shape[1])


def _rope(x, c, a, b):
    w = x.shape[1]
    return x * c + pltpu.roll(x, w - ROT // 2, 1) * a + pltpu.roll(x, ROT // 2, 1) * b


def _rope_t(dy, c, a, b):
    w = dy.shape[1]
    return dy * c + pltpu.roll(dy * b, w - ROT // 2, 1) + pltpu.roll(dy * a, ROT // 2, 1)


def _sigmoid(x):
    return 1.0 / (1.0 + jnp.exp(-x))


def _row_spec(shape, ts):
    nd = len(shape)
    if shape[0] == S:
        return pl.BlockSpec((ts,) + tuple(shape[1:]), lambda i: (i,) + (0,) * (nd - 1))
    return pl.BlockSpec(tuple(shape), lambda i: (0,) * nd)


def _rows_call(body, name, ins, outs, ts=ROWS):
    return pl.pallas_call(
        body, name=name, grid=(S // ts,),
        in_specs=[_row_spec(a.shape, ts) for a in ins],
        out_specs=[_row_spec(s, ts) for s, _ in outs],
        out_shape=[jax.ShapeDtypeStruct(s, d) for s, d in outs],
        compiler_params=_params(1))(*ins)


def _col_spec(ts, w, col):
    return pl.BlockSpec((ts, w), lambda i: (i, col))


TM = TN_ = 512
TM_TOKENS = 1024
TN_WIDE = 1024


def _mm(name, m, n, terms, out_dtype=F32, add=None, tm=None, tn=TN_, stacked=False, riding=None, rows_of=None):
    nterm = len(terms)
    if tm is None:
        tm = TM_TOKENS if m == S else TM
    nj, ni_ = n // tn, m // tm
    n_in = 2 * nterm + (add is not None) + (rows_of is not None and rows_of[0] is not None)
    r_in, r_out = (len(riding.ins), len(riding.outs)) if riding is not None else (0, 0)

    def body(*refs):
        if riding is not None:
            j, i = pl.program_id(0), pl.program_id(1)
            at_end = riding.hooks(refs[n_in:n_in + r_in], refs[n_in + r_in + 1:n_in + r_in + 1 + r_out],
                                  *refs[n_in + r_in + 1 + r_out:], first=(j == 0) & (i == 0),
                                  middle=(j == nj // 2) & (i == 0), last=(j == nj - 1) & (i == ni_ - 1))
        acc = None
        for t in range(nterm):
            part = _dot(refs[2 * t][...], refs[2 * t + 1][...], terms[t][4])
            acc = part if acc is None else acc + part
        if add is not None:
            acc = acc + refs[2 * nterm][...]
        refs[n_in + r_in][...] = acc.astype(out_dtype)
        if riding is not None:
            at_end()

    tile = pl.BlockSpec((tm, tn), lambda j, i: (i, j))
    ins, specs = [], []
    for a, a_spec, b, b_spec, _ in terms:
        ins += [a, b]
        specs += [a_spec, b_spec]
    if add is not None:
        ins.append(add)
        specs.append(tile)
    out_spec = pl.BlockSpec((None, tm, tn), lambda j, i: (j, i, 0)) if stacked else tile
    out_shape = jax.ShapeDtypeStruct((nj, m, tn) if stacked else (m, n), out_dtype)
    if rows_of is not None:
        taller, rows, row0 = rows_of
        out_spec = pl.BlockSpec((pl.Element(tm), pl.Element(tn)), lambda j, i: (
            pl.multiple_of(row0 + i * tm, SUBLANES), pl.multiple_of(j * tn, LANES)))
        out_shape = jax.ShapeDtypeStruct((rows, n), out_dtype)
        alias = {}
        if taller is not None:
            ins.append(taller)
            specs.append(pl.BlockSpec(memory_space=pltpu.HBM))
            alias = {len(ins) - 1: 0}
        return pl.pallas_call(body, name=name, grid=(nj, ni_), in_specs=specs, out_specs=out_spec,
                              out_shape=out_shape, input_output_aliases=alias, compiler_params=_params(2))(*ins)
    if riding is None:
        return pl.pallas_call(body, name=name, grid=(nj, ni_), in_specs=specs, out_specs=out_spec,
                              out_shape=out_shape, compiler_params=_params(2))(*ins)
    res = pl.pallas_call(
        body, name=name, grid=(nj, ni_), in_specs=specs + riding.in_specs,
        out_specs=[out_spec] + riding.out_specs, out_shape=[out_shape] + riding.out_shape,
        scratch_shapes=riding.scratch, compiler_params=_params(2))(*ins, *riding.ins)
    return res[0], res[1:]


def _a_rows(k, col=0, tm=TM_TOKENS):
    return pl.BlockSpec((tm, k), lambda j, i: (i, col))


def _a_cols(k, tm=TM):
    return pl.BlockSpec((k, tm), lambda j, i: (0, i))


def _b_cols(k, row=0, col0=0, tn=TN_):
    return pl.BlockSpec((k, tn), lambda j, i: (row, col0 + j))


def _b_rows(k, row0=0, tn=TN_):
    return pl.BlockSpec((tn, k), lambda j, i: (row0 + j, 0))


def _rmsnorm_fwd(x, gains, name):
    def body(*refs):
        xv = refs[0][...]
        r = lax.rsqrt(jnp.mean(xv * xv, axis=-1, keepdims=True) + EPS)
        xh = xv * r
        for n in range(len(gains)):
            refs[1 + len(gains) + n][...] = (xh * refs[1 + n][...]).astype(BF16)

    return _rows_call(body, name, [x] + list(gains), [((S, D), BF16)] * len(gains))


def _norm_bwd_tile(xv, gains, dus, dres, dg_refs):
    r = lax.rsqrt(jnp.mean(xv * xv, axis=-1, keepdims=True) + EPS)
    xh = xv * r
    gy = None
    for m, (gain, du) in enumerate(zip(gains, dus)):
        part = jnp.sum(du * xh, axis=0, keepdims=True)

        @pl.when(pl.program_id(0) == 0)
        def _(m=m, part=part):
            dg_refs[m][...] = part

        @pl.when(pl.program_id(0) != 0)
        def _(m=m, part=part):
            dg_refs[m][...] += part

        t = du * gain
        gy = t if gy is None else gy + t
    return dres + r * (gy - xh * jnp.mean(gy * xh, axis=-1, keepdims=True))


def _rmsnorm_bwd(x, gains, dus, dres, name):
    n = len(gains)

    def body(*refs):
        x_ref, g_refs, du_refs, dres_ref = refs[0], refs[1:1 + n], refs[1 + n:1 + 2 * n], refs[1 + 2 * n]
        dx_ref, dxb_ref, dg_refs = refs[2 + 2 * n], refs[3 + 2 * n], refs[4 + 2 * n:]
        dx = _norm_bwd_tile(x_ref[...], [g[...] for g in g_refs], [du[...] for du in du_refs], dres_ref[...], dg_refs)
        dx_ref[...] = dx
        dxb_ref[...] = dx.astype(BF16)

    outs = [((S, D), F32), ((S, D), BF16)] + [((1, D), F32)] * n
    return _rows_call(body, name, [x] + list(gains) + list(dus) + [dres], outs)


def _du_b_norms(d_q, d_gate, d_kv, w_in_b, w_kv, h, kv_g, norm_b, dres):
    half = D // 2

    def body(dq_ref, dgate_ref, dkv_ref, wb_ref, wkv_ref, h_ref, gk_ref, gb_ref, dres_ref,
             dx_ref, dxb_ref, dgk_ref, dgb_ref):
        pieces = (dq_ref[:, :half], dq_ref[:, half:], dgate_ref[:, :half], dgate_ref[:, half:])
        du_b = None
        for c in range(NCHIP):
            part = _dot(pieces[c], wb_ref[c], NT)
            du_b = part if du_b is None else du_b + part
        du_kv = _dot(dkv_ref[...], wkv_ref[...], NT)
        dx = _norm_bwd_tile(h_ref[...], [gk_ref[...], gb_ref[...]], [du_kv, du_b], dres_ref[...], [dgk_ref, dgb_ref])
        dx_ref[...] = dx
        dxb_ref[...] = dx.astype(BF16)

    whole = lambda a: pl.BlockSpec(a.shape, lambda i: (0,) * a.ndim)
    rows = lambda w: pl.BlockSpec((ROWS, w), lambda i: (i, 0))
    return pl.pallas_call(
        body, name="du_b_norms", grid=(S // ROWS,),
        in_specs=[rows(D), rows(D), rows(2 * KVW), whole(w_in_b), whole(w_kv), rows(D), whole(kv_g), whole(norm_b),
                  rows(D)],
        out_specs=[rows(D), rows(D), whole(kv_g), whole(norm_b)],
        out_shape=[jax.ShapeDtypeStruct((S, D), F32), jax.ShapeDtypeStruct((S, D), BF16),
                   jax.ShapeDtypeStruct((1, D), F32), jax.ShapeDtypeStruct((1, D), F32)],
        compiler_params=_params(1))(d_q, d_gate, d_kv, w_in_b, w_kv, h, kv_g, norm_b, dres)


def _proj_a(u, w1t, qg, kg):
    e = _seg_mat(D)
    gains = jnp.stack([qg * SCALE, kg])

    def body(u_ref, w_ref, g_ref, e_ref, raw_ref, out_ref):
        x = _dot(u_ref[...], w_ref[...], NT)
        raw_ref[...] = x

        @pl.when(pl.program_id(0) < 2)
        def _():
            out_ref[...] = (x * _head_rstd(x, e_ref[...]) * g_ref[...]).astype(BF16)

        @pl.when(pl.program_id(0) == 2)
        def _():
            out_ref[...] = x.astype(BF16)

    tm = TM_TOKENS
    return pl.pallas_call(
        body, name="proj_a", grid=(3, S // tm),
        in_specs=[pl.BlockSpec((tm, D), lambda j, i: (i, 0)), pl.BlockSpec((D, D), lambda j, i: (j, 0)),
                  pl.BlockSpec((None, 1, D), lambda j, i: (jnp.minimum(j, 1), 0, 0)),
                  pl.BlockSpec(e.shape, lambda j, i: (0, 0))],
        out_specs=[pl.BlockSpec((tm, D), lambda j, i: (i, j)), pl.BlockSpec((None, tm, D), lambda j, i: (j, i, 0))],
        out_shape=[jax.ShapeDtypeStruct((S, 3 * D), F32), jax.ShapeDtypeStruct((3, S, D), BF16)],
        compiler_params=_params(2))(u, w1t, gains, e)


def _tri(upper):
    r, c = np.arange(ROWS)[:, None], np.arange(ROWS)[None, :]
    return jnp.asarray((r <= c) if upper else (r >= c), BF16)


def _forget_cumsum(fpad, bpad):
    def body(f_ref, b_ref, u_ref, c_ref, carry):
        @pl.when(pl.program_id(0) == 0)
        def _():
            carry[...] = jnp.zeros_like(carry)

        lf = jax.nn.log_sigmoid(f_ref[...] + b_ref[...])
        blk = _dot_split(lf.T, u_ref[...], 3) + carry[:, 0:1]
        c_ref[...] = blk
        carry[...] = jnp.broadcast_to(blk[:, ROWS - 1:ROWS], carry.shape)

    return pl.pallas_call(
        body, name="forget_cumsum", grid=(S // ROWS,),
        in_specs=[pl.BlockSpec((ROWS, LANES), lambda i: (i, 0)), pl.BlockSpec((1, LANES), lambda i: (0, 0)),
                  pl.BlockSpec((ROWS, ROWS), lambda i: (0, 0))],
        out_specs=pl.BlockSpec((LANES, ROWS), lambda i: (0, i)),
        out_shape=jax.ShapeDtypeStruct((LANES, S), F32),
        scratch_shapes=[pltpu.VMEM((LANES, LANES), F32)],
        compiler_params=_params(1))(fpad, bpad, _tri(True))


def _forget_bwd(dct, fpad, bpad):
    nb = S // ROWS

    def body(dc_ref, f_ref, b_ref, l_ref, df_ref, db_ref, carry):
        @pl.when(pl.program_id(0) == 0)
        def _():
            carry[...] = jnp.zeros_like(carry)
            db_ref[...] = jnp.zeros_like(db_ref)

        blk = _dot_split(dc_ref[...], l_ref[...], 3) + carry[:, 0:1]
        carry[...] = jnp.broadcast_to(blk[:, 0:1], carry.shape)
        df = blk.T * _sigmoid(-(f_ref[...] + b_ref[...]))
        df_ref[...] = df.astype(BF16)
        db_ref[...] += jnp.sum(df, axis=0, keepdims=True)

    return pl.pallas_call(
        body, name="forget_bwd", grid=(nb,),
        in_specs=[pl.BlockSpec((LANES, ROWS), lambda i: (0, nb - 1 - i)),
                  pl.BlockSpec((ROWS, LANES), lambda i: (nb - 1 - i, 0)),
                  pl.BlockSpec((1, LANES), lambda i: (0, 0)), pl.BlockSpec((ROWS, ROWS), lambda i: (0, 0))],
        out_specs=[pl.BlockSpec((ROWS, LANES), lambda i: (nb - 1 - i, 0)), pl.BlockSpec((1, LANES), lambda i: (0, 0))],
        out_shape=[jax.ShapeDtypeStruct((S, LANES), BF16), jax.ShapeDtypeStruct((1, LANES), F32)],
        scratch_shapes=[pltpu.VMEM((LANES, LANES), F32)],
        compiler_params=_params(1))(dct, fpad, bpad, _tri(False))


def _headnorm_bwd(x, col, gain, dy, rope, name):
    e = _seg_mat(D)
    tabs = list(rope) if rope is not None else []

    def body(*refs):
        x_ref, g_ref, dy_ref, e_ref = refs[:4]
        dx_ref, dg_ref = refs[-2:]
        xv, dyv, ev = x_ref[...], dy_ref[...], e_ref[...]
        if rope is not None:
            c, a, b = (jnp.tile(t[...], (1, D // LANES)) for t in refs[4:7])
            dyv = _rope_t(dyv, c, a, b)
        r = _head_rstd(xv, ev)
        xh = xv * r
        part = jnp.sum(dyv * xh, axis=0, keepdims=True)

        @pl.when(pl.program_id(0) == 0)
        def _():
            dg_ref[...] = part

        @pl.when(pl.program_id(0) != 0)
        def _():
            dg_ref[...] += part

        gy = dyv * g_ref[...]
        seg = _spread(_dot_split(gy * xh, ev, 2) * (1.0 / HD), D)
        dx_ref[...] = (r * (gy - xh * seg)).astype(BF16)

    whole = lambda a: pl.BlockSpec(a.shape, lambda i: (0, 0))
    return pl.pallas_call(
        body, name=name, grid=(S // ROWS,),
        in_specs=[_col_spec(ROWS, D, col), whole(gain), _col_spec(ROWS, D, 0), whole(e)]
                 + [pl.BlockSpec((ROWS, LANES), lambda i: (i, 0))] * len(tabs),
        out_specs=[_col_spec(ROWS, D, 0), whole(gain)],
        out_shape=[jax.ShapeDtypeStruct((S, D), BF16), jax.ShapeDtypeStruct((1, D), F32)],
        compiler_params=_params(1))(x, gain, dy, e, *tabs)


def _dup_mat():
    r, c = np.arange(KVW)[:, None], np.arange(2 * KVW)[None, :]
    return (r // HD == c // LANES) & (r % HD == c % HD)


def _fold_mat():
    r, c = np.arange(D)[:, None], np.arange(KVW)[None, :]
    return (r // (2 * LANES) == c // HD) & (r % HD == c % HD)


def _proj_b(u_b, u_kv, w_in_b, w_kv, qg, kg, rope):
    e, ek = _seg_mat(D), _seg_mat(KVW)
    dup = jnp.asarray(_dup_mat(), BF16)

    def body(ub_ref, ukv_ref, wb_ref, wkv_ref, qg_ref, kg_ref, e_ref, ek_ref, dup_ref, c_ref, a_ref, b_ref,
             pb_ref, kv_ref, qo, ko, vo):
        ub = ub_ref[...]
        pb = jnp.concatenate([_dot(ub, wb_ref[c]) for c in range(NCHIP)], axis=1)
        kv = _dot(ukv_ref[...], wkv_ref[...])
        pb_ref[...] = pb
        kv_ref[...] = kv
        c1, a1, b1 = c_ref[...], a_ref[...], b_ref[...]
        qv = pb[:, :D]
        qn = qv * _head_rstd(qv, e_ref[...]) * qg_ref[...]
        t = lambda z, n: jnp.tile(z, (1, n))
        qo[...] = (_rope(qn, t(c1, D // LANES), t(a1, D // LANES), t(b1, D // LANES)) * SCALE).astype(BF16)
        kvv = kv[:, :KVW]
        kn = kvv * _head_rstd(kvv, ek_ref[...]) * kg_ref[...]
        kr = _rope(kn, t(c1, KVW // LANES), t(a1, KVW // LANES), t(b1, KVW // LANES)).astype(BF16)
        ko[...] = _dot(kr, dup_ref[...]).astype(BF16)
        vo[...] = _dot(kv[:, KVW:].astype(BF16), dup_ref[...]).astype(BF16)

    whole = lambda a: pl.BlockSpec(a.shape, lambda i: (0,) * a.ndim)
    rows = lambda w: pl.BlockSpec((ROWS, w), lambda i: (i, 0))
    return pl.pallas_call(
        body, name="proj_b", grid=(S // ROWS,),
        in_specs=[rows(D), rows(D), whole(w_in_b), whole(w_kv), whole(qg), whole(kg), whole(e), whole(ek),
                  whole(dup), rows(LANES), rows(LANES), rows(LANES)],
        out_specs=[rows(2 * D), rows(2 * KVW), rows(D), rows(2 * KVW), rows(2 * KVW)],
        out_shape=[jax.ShapeDtypeStruct((S, 2 * D), F32), jax.ShapeDtypeStruct((S, 2 * KVW), F32),
                   jax.ShapeDtypeStruct((S, D), BF16), jax.ShapeDtypeStruct((S, 2 * KVW), BF16),
                   jax.ShapeDtypeStruct((S, 2 * KVW), BF16)],
        compiler_params=_params(1))(u_b, u_kv, w_in_b, w_kv, qg, kg, e, ek, dup, *rope)


def _kv_bwd(dkdup, dvdup, kv, kg, rope):
    ek = _seg_mat(KVW)
    fold = jnp.asarray(_fold_mat(), BF16)

    def body(dk_ref, dv_ref, k_ref, kg_ref, ek_ref, fold_ref, c_ref, a_ref, b_ref, dkv_ref, dg_ref):
        ev, fv = ek_ref[...], fold_ref[...]
        t = lambda z: jnp.tile(z[...], (1, KVW // LANES))
        dk = _rope_t(_dot_split(dk_ref[...], fv, 2), t(c_ref), t(a_ref), t(b_ref))
        dv = _dot_split(dv_ref[...], fv, 2)
        xv = k_ref[...]
        r = _head_rstd(xv, ev)
        xh = xv * r
        part = jnp.sum(dk * xh, axis=0, keepdims=True)

        @pl.when(pl.program_id(0) == 0)
        def _():
            dg_ref[...] = part

        @pl.when(pl.program_id(0) != 0)
        def _():
            dg_ref[...] += part

        gy = dk * kg_ref[...]
        seg = _spread(_dot_split(gy * xh, ev, 2) * (1.0 / HD), KVW)
        dkv_ref[:, 0:KVW] = (r * (gy - xh * seg)).astype(BF16)
        dkv_ref[:, KVW:2 * KVW] = dv.astype(BF16)

    whole = lambda a: pl.BlockSpec(a.shape, lambda i: (0, 0))
    tab = pl.BlockSpec((ROWS, LANES), lambda i: (i, 0))
    return pl.pallas_call(
        body, name="kv_bwd", grid=(S // ROWS,),
        in_specs=[_col_spec(ROWS, D, 0), _col_spec(ROWS, D, 0), _col_spec(ROWS, KVW, 0),
                  whole(kg), whole(ek), whole(fold), tab, tab, tab],
        out_specs=[_col_spec(ROWS, 2 * KVW, 0), whole(kg)],
        out_shape=[jax.ShapeDtypeStruct((S, 2 * KVW), BF16), jax.ShapeDtypeStruct((1, KVW), F32)],
        compiler_params=_params(1))(dkdup, dvdup, kv, kg, ek, fold, *rope)


def _out_norms(y, w_out, residual, gains):
    n = len(gains)

    def body(y_ref, w_ref, r_ref, *refs):
        h = _dot(y_ref[...], w_ref[...]) + r_ref[...]
        refs[n][...] = h
        hn = h * lax.rsqrt(jnp.mean(h * h, axis=-1, keepdims=True) + EPS)
        for k in range(n):
            refs[n + 1 + k][...] = (hn * refs[k][...]).astype(BF16)

    rows = pl.BlockSpec((TM_TOKENS, D), lambda i: (i, 0))
    whole = pl.BlockSpec((D, D), lambda i: (0, 0))
    gain = pl.BlockSpec((1, D), lambda i: (0, 0))
    return pl.pallas_call(
        body, name="out_a_norms", grid=(S // TM_TOKENS,), in_specs=[rows, whole, rows] + [gain] * n,
        out_specs=[rows] * (n + 1),
        out_shape=[jax.ShapeDtypeStruct((S, D), F32)] + [jax.ShapeDtypeStruct((S, D), BF16)] * n,
        compiler_params=_params(1))(y, w_out, residual, *gains)


def _out_loss(y, w_out, residual, target):
    def body(y_ref, w_ref, r_ref, t_ref, d_ref, db_ref, l_ref):
        diff = _dot(y_ref[...], w_ref[...]) + r_ref[...] - t_ref[...]
        d = diff * (1.0 / D)
        d_ref[...] = d
        db_ref[...] = d.astype(BF16)

        @pl.when(pl.program_id(0) == 0)
        def _():
            l_ref[...] = jnp.zeros_like(l_ref)

        l_ref[...] += jnp.sum(diff * diff, axis=0, keepdims=True)

    rows = pl.BlockSpec((TM_TOKENS, D), lambda i: (i, 0))
    whole = pl.BlockSpec((D, D), lambda i: (0, 0))
    return pl.pallas_call(
        body, name="out_b_loss", grid=(S // TM_TOKENS,), in_specs=[rows, whole, rows, rows],
        out_specs=[rows, rows, pl.BlockSpec((1, D), lambda i: (0, 0))],
        out_shape=[jax.ShapeDtypeStruct((S, D), F32), jax.ShapeDtypeStruct((S, D), BF16),
                   jax.ShapeDtypeStruct((1, D), F32)],
        compiler_params=_params(1))(y, w_out, residual, target)


def _lane():
    return lax.broadcasted_iota(jnp.int32, (1, LANES), 1)


def _head_mask(hh):
    return (_lane() < HD) if hh == 0 else (_lane() >= HD)


def _qkv_specs():
    return (pl.BlockSpec((None, ATT, LANES), lambda p, i: (0, i, p)),
            pl.BlockSpec((None, S, LANES), lambda p, i: (1, 0, p)),
            pl.BlockSpec((None, S, LANES), lambda p, i: (2, 0, p)))


def _fox_fwd(qkv, ct, gate, riding):
    nq, npair = S // ATT, NH // 2
    ni, no = len(riding.ins), len(riding.outs)

    def body(q_ref, k_ref, v_ref, c_ref, gate_ref, *rest):
        o_ref, lse_ref, y_ref = rest[ni:ni + 3]
        pair, i = pl.program_id(0), pl.program_id(1)
        at_end = riding.hooks(rest[:ni], rest[ni + 3:ni + 3 + no], *rest[ni + 3 + no:],
                              first=(pair == 0) & (i == 0), middle=(pair == npair // 2) & (i == 0),
                              last=(pair == npair - 1) & (i == nq - 1))
        q2 = q_ref[...]
        qms = [jnp.where(_head_mask(hh), q2, jnp.zeros_like(q2)) for hh in (0, 1)]

        def probs(off, width, m, hh, diag):
            s = _dot(qms[hh], k_ref[pl.ds(off, width), :], NT) - c_ref[hh:hh + 1, pl.ds(off, width)]
            if diag:
                row = i * ATT + lax.broadcasted_iota(jnp.int32, (ATT, width), 0)
                col = off + lax.broadcasted_iota(jnp.int32, (ATT, width), 1)
                s = jnp.where(col <= row, s, NEG)
            m_new = jnp.maximum(m, jnp.max(s, axis=1, keepdims=True))
            p = jnp.exp(s - m_new)
            p_hi = p.astype(BF16)
            return m_new, jnp.exp(m - m_new), p_hi, (p - p_hi.astype(F32)).astype(BF16)

        def weighted(off, width, p_hi, p_lo, hh):
            vj = v_ref[pl.ds(off, width), :]
            v1 = jnp.where(_head_mask(hh), vj, jnp.ones_like(vj))
            return _dot(p_hi, v1) + _dot(p_lo, v1)

        def step(off, width, carry, diag):
            off = pl.multiple_of(off, ATT)
            out = []
            for hh in (0, 1):
                m, acc = carry[hh]
                m, alpha, p_hi, p_lo = probs(off, width, m, hh, diag)
                out.append((m, alpha * acc + weighted(off, width, p_hi, p_lo, hh)))
            return tuple(out)

        one = (jnp.full((ATT, 1), NEG, F32), jnp.zeros((ATT, LANES), F32))
        carry = lax.fori_loop(0, i // 2, lambda j, cr: step(j * (2 * ATT), 2 * ATT, cr, False), (one, one))
        carry = lax.cond(i % 2 == 1, lambda cr: step((i - 1) * ATT, 2 * ATT, cr, True),
                         lambda cr: step(i * ATT, ATT, cr, True), carry)
        res = []
        for hh in (0, 1):
            m, acc = carry[hh]
            l = jnp.max(jnp.where(_head_mask(1 - hh), acc, 0.0), axis=1, keepdims=True)
            res.append((acc / l, m + jnp.log(l)))
        first = _head_mask(0)
        o = jnp.where(first, res[0][0], res[1][0])
        o_ref[...] = o
        lse_ref[...] = jnp.where(first, res[0][1], res[1][1])
        g = gate_ref[...]
        y_ref[...] = (o * (g * _sigmoid(g))).astype(BF16)
        at_end()

    blk = pl.BlockSpec((ATT, LANES), lambda p, i: (i, p))
    res = pl.pallas_call(
        body, name="fox_fwd", grid=(npair, nq),
        in_specs=[*_qkv_specs(), pl.BlockSpec((None, 2, S), lambda p, i: (p, 0, 0)), blk] + riding.in_specs,
        out_specs=[blk, blk, blk] + riding.out_specs,
        out_shape=[jax.ShapeDtypeStruct((S, D), F32)] * 2 + [jax.ShapeDtypeStruct((S, D), BF16)] + riding.out_shape,
        scratch_shapes=riding.scratch,
        compiler_params=_params(2))(qkv, qkv, qkv, ct, gate, *riding.ins)
    return res[0], res[1], res[2], res[3:]


def _gate_grads(dy, o, g):
    sg = _sigmoid(g)
    return dy * (g * sg), dy * o * (sg * (1.0 + g * (1.0 - sg)))


def _fox_bwd(qkv, ct, o, lse, dy, gate, raw, qg, riding):
    nq, npair = S // ATT, NH // 2
    ni, no = len(riding.ins), len(riding.outs)
    e = _seg_mat(LANES)

    def body(q_ref, k_ref, v_ref, c_ref, o_ref, lse_ref, dy_ref, gate_ref, x_ref, qg_ref, e_ref, *rest):
        dx_ref, dk_ref, dvb_ref, dc_ref, dgate_ref, dg_ref = rest[ni:ni + 6]
        dv_ref = rest[ni + 6 + no]
        pair, i = pl.program_id(0), pl.program_id(1)
        at_end = riding.hooks(rest[:ni], rest[ni + 6:ni + 6 + no], *rest[ni + 7 + no:],
                              first=(pair == 0) & (i == 0), middle=(pair == npair // 2) & (i == 0),
                              last=(pair == npair - 1) & (i == nq - 1))

        @pl.when(i == 0)
        def _():
            dk_ref[...] = jnp.zeros_like(dk_ref)
            dv_ref[...] = jnp.zeros_like(dv_ref)
            dc_ref[...] = jnp.zeros_like(dc_ref)
            dg_ref[...] = jnp.zeros_like(dg_ref)

        q2, lse2 = q_ref[...], lse_ref[...]
        do2, dgate = _gate_grads(dy_ref[...], o_ref[...], gate_ref[...])
        dgate_ref[...] = dgate.astype(BF16)
        do2b = do2.astype(BF16)
        prod = do2b.astype(F32) * o_ref[...]
        heads = []
        for hh in (0, 1):
            hm = _head_mask(hh)
            heads.append((jnp.where(hm, q2, jnp.zeros_like(q2)), jnp.where(hm, do2b, jnp.zeros_like(do2b)),
                          jnp.sum(jnp.where(hm, prod, 0.0), axis=1, keepdims=True),
                          jnp.max(jnp.where(hm, lse2, NEG), axis=1, keepdims=True)))

        def step(off, width, dqs, diag):
            off = pl.multiple_of(off, ATT)
            kj, vj = k_ref[pl.ds(off, width), :], v_ref[pl.ds(off, width), :]
            dk, dv, out = None, None, []
            for hh in (0, 1):
                qm, dom, delta, lse_h = heads[hh]
                s = _dot(qm, kj, NT) - c_ref[hh:hh + 1, pl.ds(off, width)]
                p = jnp.exp(s - lse_h)
                if diag:
                    row = i * ATT + lax.broadcasted_iota(jnp.int32, (ATT, width), 0)
                    col = off + lax.broadcasted_iota(jnp.int32, (ATT, width), 1)
                    p = jnp.where(col <= row, p, 0.0)
                ds = p * (_dot(dom, vj, NT) - delta)
                dc_ref[hh:hh + 1, pl.ds(off, width)] += -jnp.sum(ds, axis=0, keepdims=True)
                dsb = ds.astype(BF16)
                dk_h, dv_h = _dot(dsb, qm, TN), _dot(p.astype(BF16), dom, TN)
                dk, dv = (dk_h, dv_h) if dk is None else (dk + dk_h, dv + dv_h)
                out.append(dqs[hh] + _dot(dsb, kj))
            dk_ref[pl.ds(off, width), :] += dk
            dv_ref[pl.ds(off, width), :] += dv
            return tuple(out)

        zero = jnp.zeros((ATT, LANES), F32)
        dqs = lax.fori_loop(0, i // 2, lambda j, acc: step(j * (2 * ATT), 2 * ATT, acc, False), (zero, zero))
        dqs = lax.cond(i % 2 == 1, lambda acc: step((i - 1) * ATT, 2 * ATT, acc, True),
                       lambda acc: step(i * ATT, ATT, acc, True), dqs)
        dqv = jnp.where(_head_mask(0), dqs[0], dqs[1]) * SCALE
        xv, ev = x_ref[...], e_ref[...]
        r = _head_rstd(xv, ev)
        xh = xv * r
        dg_ref[...] += jnp.sum(dqv * xh, axis=0, keepdims=True)
        gy = dqv * qg_ref[...]
        seg = _spread(_dot_split(gy * xh, ev, 2) * (1.0 / HD), LANES)
        dx_ref[...] = (r * (gy - xh * seg)).astype(BF16)

        @pl.when(i == nq - 1)
        def _():
            dvb_ref[...] = dv_ref[...].astype(BF16)

        at_end()

    blk = pl.BlockSpec((ATT, LANES), lambda p, i: (i, p))
    full = pl.BlockSpec((S, LANES), lambda p, i: (0, p))
    lanes = pl.BlockSpec((1, LANES), lambda p, i: (0, p))
    cspec = pl.BlockSpec((None, 2, S), lambda p, i: (p, 0, 0))
    res = pl.pallas_call(
        body, name="fox_bwd", grid=(npair, nq),
        in_specs=[*_qkv_specs(), cspec, blk, blk, blk, blk, blk, lanes, pl.BlockSpec(e.shape, lambda p, i: (0, 0))]
                 + riding.in_specs,
        out_specs=[blk, full, full, cspec, blk, lanes] + riding.out_specs,
        out_shape=[jax.ShapeDtypeStruct((S, D), BF16), jax.ShapeDtypeStruct((S, D), F32),
                   jax.ShapeDtypeStruct((S, D), BF16), jax.ShapeDtypeStruct((npair, 2, S), F32),
                   jax.ShapeDtypeStruct((S, D), BF16), jax.ShapeDtypeStruct((1, D), F32)]
                  + riding.out_shape,
        scratch_shapes=[pltpu.VMEM((S, LANES), F32)] + riding.scratch,
        compiler_params=_params(2))(qkv, qkv, qkv, ct, o, lse, dy, gate, raw, qg, e, *riding.ins)
    return res[0], res[1], res[2], res[3], res[4], res[5], res[6:]


def _both_heads(x):
    return jnp.concatenate([jnp.where(_head_mask(hh), x, jnp.zeros_like(x)) for hh in (0, 1)], axis=0)


def _per_head(col0, col1):
    return jnp.concatenate([jnp.broadcast_to(col0, (WINDOW, 1)), jnp.broadcast_to(col1, (WINDOW, 1))], axis=0)


def _unstack(x2):
    return jnp.where(_head_mask(0), x2[:WINDOW], x2[WINDOW:])


def _swa_valid(i, start):
    r = lax.broadcasted_iota(jnp.int32, (2 * WINDOW, 2 * WINDOW), 0)
    qabs = i * WINDOW + jnp.where(r >= WINDOW, r - WINDOW, r)
    kabs = start + lax.broadcasted_iota(jnp.int32, (2 * WINDOW, 2 * WINDOW), 1)
    return (kabs <= qabs) & (qabs - kabs < WINDOW)


def _swa_fwd(q, kdup, vdup, sinks_t, proj, gate_col):
    def body(q_ref, k_ref, v_ref, sk_ref, gate_ref, o_ref, lse_ref, y_ref):
        skv = sk_ref[...]
        first = _head_mask(0)
        for sb in range(SWQ):
            i = pl.program_id(1) * SWQ + sb
            rows = slice(sb * WINDOW, (sb + 1) * WINDOW)
            start = pl.multiple_of(jnp.maximum(i - 1, 0) * WINDOW, WINDOW)
            kk, vv = k_ref[pl.ds(start, 2 * WINDOW), :], v_ref[pl.ds(start, 2 * WINDOW), :]
            q2 = q_ref[rows, :]
            valid = _swa_valid(i, start)[:WINDOW]
            res = []
            for hh in (0, 1):
                hm = _head_mask(hh)
                sink = jnp.max(jnp.where(hm, skv, NEG), axis=1, keepdims=True)
                s = jnp.where(valid, _dot(jnp.where(hm, q2, jnp.zeros_like(q2)), kk, NT), NEG)
                m = jnp.maximum(jnp.max(s, axis=1, keepdims=True), sink)
                p = jnp.exp(s - m)
                l = jnp.sum(p, axis=1, keepdims=True) + jnp.exp(sink - m)
                res.append((_dot(p.astype(BF16), vv) / l, m + jnp.log(l)))
            o = jnp.where(first, res[0][0], res[1][0])
            o_ref[rows, :] = o
            lse_ref[rows, :] = jnp.where(first, res[0][1], res[1][1])
            g = gate_ref[rows, :]
            y_ref[rows, :] = (o * (g * _sigmoid(g))).astype(BF16)

    blk = pl.BlockSpec((SWQ * WINDOW, LANES), lambda p, i: (i, p))
    gate = pl.BlockSpec((SWQ * WINDOW, LANES), lambda p, i: (i, gate_col + p))
    full = pl.BlockSpec((S, LANES), lambda p, i: (0, p // 2))
    return pl.pallas_call(
        body, name="swa_fwd", grid=(NH // 2, S // (SWQ * WINDOW)),
        in_specs=[blk, full, full, pl.BlockSpec((1, LANES), lambda p, i: (0, p)), gate],
        out_specs=[blk, blk, blk],
        out_shape=[jax.ShapeDtypeStruct((S, D), F32)] * 2 + [jax.ShapeDtypeStruct((S, D), BF16)],
        compiler_params=_params(2))(q, kdup, vdup, sinks_t, proj)


def _swa_bwd(q, kdup, vdup, sinks_t, o, lse, dy, proj, gate_col, qg, rope):
    e = _seg_mat(LANES)

    def body(q_ref, k_ref, v_ref, sk_ref, o_ref, lse_ref, dy_ref, gate_ref, x_ref, qg_ref, c_ref, a_ref, b_ref, e_ref,
             dx_ref, dk_ref, dv_ref, dsk_ref, dgate_ref, dg_ref, dq_ref):
        @pl.when(pl.program_id(1) == 0)
        def _():
            dk_ref[...] = jnp.zeros_like(dk_ref)
            dv_ref[...] = jnp.zeros_like(dv_ref)
            dsk_ref[...] = jnp.zeros_like(dsk_ref)
            dg_ref[...] = jnp.zeros_like(dg_ref)

        skv = sk_ref[...]
        first = _head_mask(0)
        sink = _per_head(*[jnp.max(jnp.where(_head_mask(hh), skv, NEG), axis=1, keepdims=True) for hh in (0, 1)])
        for sb in range(SWQ):
            i = pl.program_id(1) * SWQ + sb
            rows = slice(sb * WINDOW, (sb + 1) * WINDOW)
            start = pl.multiple_of(jnp.maximum(i - 1, 0) * WINDOW, WINDOW)
            kk, vv = k_ref[pl.ds(start, 2 * WINDOW), :], v_ref[pl.ds(start, 2 * WINDOW), :]
            do2, dgate = _gate_grads(dy_ref[rows, :], o_ref[rows, :], gate_ref[rows, :])
            dgate_ref[rows, :] = dgate.astype(BF16)
            do2b = do2.astype(BF16)
            prod, lse2 = do2b.astype(F32) * o_ref[rows, :], lse_ref[rows, :]
            qs, dos = _both_heads(q_ref[rows, :]), _both_heads(do2b)
            delta = jnp.concatenate([jnp.sum(jnp.where(_head_mask(hh), prod, 0.0), axis=1, keepdims=True)
                                     for hh in (0, 1)], axis=0)
            lse_h = jnp.concatenate([jnp.max(jnp.where(_head_mask(hh), lse2, NEG), axis=1, keepdims=True)
                                     for hh in (0, 1)], axis=0)
            p = jnp.where(_swa_valid(i, start), jnp.exp(_dot(qs, kk, NT) - lse_h), 0.0)
            dsb = (p * (_dot(dos, vv, NT) - delta)).astype(BF16)
            dk_ref[pl.ds(start, 2 * WINDOW), :] += _dot(dsb, qs, TN)
            dv_ref[pl.ds(start, 2 * WINDOW), :] += _dot(p.astype(BF16), dos, TN)
            dq_ref[rows, :] = _unstack(_dot(dsb, kk)) * SCALE
            t = jnp.exp(sink - lse_h) * delta
            dsk_ref[...] += -jnp.where(first, jnp.sum(t[:WINDOW], axis=0, keepdims=True),
                                       jnp.sum(t[WINDOW:], axis=0, keepdims=True))
        dyv = _rope_t(dq_ref[...], c_ref[...], a_ref[...], b_ref[...])
        xv, ev = x_ref[...], e_ref[...]
        r = _head_rstd(xv, ev)
        xh = xv * r
        dg_ref[...] += jnp.sum(dyv * xh, axis=0, keepdims=True)
        gy = dyv * qg_ref[...]
        seg = _spread(_dot_split(gy * xh, ev, 2) * (1.0 / HD), LANES)
        dx_ref[...] = (r * (gy - xh * seg)).astype(BF16)

    rows = SWQ * WINDOW
    blk = pl.BlockSpec((rows, LANES), lambda p, i: (i, p))
    full = pl.BlockSpec((S, LANES), lambda p, i: (0, p // 2))
    acc = pl.BlockSpec((S, LANES), lambda p, i: (0, p))
    sk = pl.BlockSpec((1, LANES), lambda p, i: (0, p))
    gate = pl.BlockSpec((rows, LANES), lambda p, i: (i, gate_col + p))
    tab = pl.BlockSpec((rows, LANES), lambda p, i: (i, 0))
    return pl.pallas_call(
        body, name="swa_bwd", grid=(NH // 2, S // rows),
        in_specs=[blk, full, full, sk, blk, blk, blk, gate, blk, sk, tab, tab, tab,
                  pl.BlockSpec(e.shape, lambda p, i: (0, 0))],
        out_specs=[blk, acc, acc, sk, blk, sk],
        out_shape=[jax.ShapeDtypeStruct((S, D), BF16), jax.ShapeDtypeStruct((S, D), F32),
                   jax.ShapeDtypeStruct((S, D), F32), jax.ShapeDtypeStruct((1, D), F32),
                   jax.ShapeDtypeStruct((S, D), BF16), jax.ShapeDtypeStruct((1, D), F32)],
        scratch_shapes=[pltpu.VMEM((rows, LANES), F32)],
        compiler_params=_params(2))(q, kdup, vdup, sinks_t, o, lse, dy, proj, proj, qg, *rope, e)


def _adamw_math(w, g, m, v):
    m = ADAM_B1 * m + (1.0 - ADAM_B1) * g
    v = ADAM_B2 * v + (1.0 - ADAM_B2) * jnp.square(g)
    m_hat = m / (1.0 - ADAM_B1 ** ADAM_STEP)
    v_hat = v / (1.0 - ADAM_B2 ** ADAM_STEP)
    delta = -ADAM_LR * (m_hat / (jnp.sqrt(v_hat) + ADAM_EPS) + ADAM_WD * w)
    return delta, m, v


def _adamw_small(ws, gs, ms, vs):
    k = len(ws)

    def body(*refs):
        for p in range(k):
            w_ref, g_ref, m_ref, v_ref = (refs[q * k + p] for q in range(4))
            d, mo, vo = _adamw_math(w_ref[...], g_ref[...], m_ref[...], v_ref[...])
            refs[4 * k + p][...], refs[5 * k + p][...], refs[6 * k + p][...] = d, mo, vo

    res = pl.pallas_call(
        body, name="adamw_small",
        out_shape=[jax.ShapeDtypeStruct(t.shape, F32) for t in ws] * 3)(*ws, *gs, *ms, *vs)
    return res[:k], res[k:2 * k], res[2 * k:]


SUM_TILES = (512, 256, 128)


FLAT_BLOCK = 257 * 1024


def _tiles(shape, axis, lead=0, halves=False):
    if len(shape) == 1:
        count = shape[0] // FLAT_BLOCK
        return (FLAT_BLOCK,), count, lambda pos, *lead_idx: (sum(k * count for k in lead_idx) + pos,)
    r, c = shape
    tile = next(t for t in SUM_TILES if (shape[axis] // (2 if halves else 1)) % t == 0)
    blk = (tile, c) if axis == 0 else (r, tile)
    count = shape[axis] // tile

    def index(pos, *lead_idx):
        return tuple(lead_idx) + ((pos, 0) if axis == 0 else (0, pos))

    return (None,) * lead + blk, count, index


def _adamw_halves(w, g_mine, g_theirs, m, v, axis, name):
    blk, count, index = _tiles(w.shape, axis, halves=True)
    per_half = count // 2

    def body(w_ref, a_ref, b_ref, m_ref, v_ref, g_ref, d_ref, mo_ref, vo_ref):
        is_mine = pl.program_id(0) // per_half == lax.axis_index("c")
        g = jnp.where(is_mine, a_ref[...], b_ref[...])
        g_ref[...] = g
        d_ref[...], mo_ref[...], vo_ref[...] = _adamw_math(w_ref[...], g, m_ref[...], v_ref[...])

    spec = pl.BlockSpec(blk, lambda i: index(i))
    half = pl.BlockSpec(blk, lambda i: index(i % per_half))
    return pl.pallas_call(
        body, name=name, grid=(count,), in_specs=[spec, half, half, spec, spec], out_specs=[spec] * 4,
        out_shape=[jax.ShapeDtypeStruct(w.shape, F32)] * 4, compiler_params=_params(1))(w, g_mine, g_theirs, m, v)


def _chip_sum(blocks, from_sibling, axis, name):
    flat = blocks.ndim == 1
    blk, count, index = _tiles((from_sibling.shape[0] // NCHIP,) if flat else from_sibling.shape[1:], axis, lead=1)

    def body(lo_ref, hi_ref, p_ref, o32, o16):
        mine = jnp.where(lax.axis_index("c") == 0, lo_ref[...], hi_ref[...])
        acc = mine + p_ref[...]
        o32[...] = acc
        o16[...] = acc.astype(BF16)

    half = pl.BlockSpec(blk, lambda k, i: index(i, k))
    if flat:
        lo = pl.BlockSpec(blk, lambda k, i: (2 * count * k + i,))
        hi = pl.BlockSpec(blk, lambda k, i: (2 * count * k + count + i,))
    else:
        lo, hi = half, pl.BlockSpec(blk, lambda k, i: index(i + count, k))
    return pl.pallas_call(
        body, name=name, grid=(NCHIP, count), in_specs=[lo, hi, half], out_specs=[half, half],
        out_shape=[jax.ShapeDtypeStruct(from_sibling.shape, F32), jax.ShapeDtypeStruct(from_sibling.shape, BF16)],
        compiler_params=_params(2))(blocks, blocks, from_sibling)


def _mesh_sum(chip_sums, parts, axis, name):
    flat = chip_sums.ndim == 1
    one = (chip_sums.shape[0] // NCHIP,) if flat else chip_sums.shape[1:]
    blk, count, index = _tiles(one, axis)
    n = NCHIP - 1

    def body(chip_ref, a_ref, *refs):
        acc = a_ref[...]
        for k in range(n):
            acc = acc + refs[k][...].astype(F32)
        refs[n][...] = acc

    spec = pl.BlockSpec(blk, lambda i, chip: index(i))
    if flat:
        mine = pl.BlockSpec(blk, lambda i, chip: (chip[0] * count + i,))
        part = [pl.BlockSpec(blk, lambda i, chip, k=k: (k * count + i,)) for k in range(n)]
    else:
        mine = pl.BlockSpec((None,) + blk, lambda i, chip: (chip[0],) + index(i))
        part = [pl.BlockSpec((None,) + blk, lambda i, chip, k=k: (k,) + index(i)) for k in range(n)]
    return pl.pallas_call(
        body, name=name,
        grid_spec=pltpu.PrefetchScalarGridSpec(num_scalar_prefetch=1, grid=(count,), in_specs=[mine] + part,
                                               out_specs=spec),
        out_shape=jax.ShapeDtypeStruct(one, F32),
        compiler_params=_params(1))(_chip(_coords()).astype(jnp.int32).reshape(1), chip_sums, *([parts] * n))


def _sum_stack(parts, name):
    n = parts.shape[0]

    def body(p_ref, o_ref):
        acc = p_ref[0]
        for k in range(1, n):
            acc = acc + p_ref[k]
        o_ref[...] = acc

    return pl.pallas_call(body, name=name, out_shape=jax.ShapeDtypeStruct(parts.shape[1:], F32))(parts)


def _coords():
    return lax.axis_index("x"), lax.axis_index("y"), lax.axis_index("c")


def _chip(who):
    return 2 * who[0] + who[1]


def _flip(who, mask):
    return tuple((1 - v) if b else v for v, b in zip(who, mask))


def _transfer(transfers, t, I, O, ssem, rsem, receiving):
    tr, me = transfers[t], _coords()
    peer = _flip(me, tr["mask"])
    return pltpu.make_async_remote_copy(
        src_ref=tr["src"](I, O, me), dst_ref=tr["dst"](I, O, peer if receiving else me),
        send_sem=ssem.at[t], recv_sem=rsem.at[t], device_id=peer, device_id_type=MESH)


def _start_transfers(transfers, I, O, ssem, rsem, onward):
    arrived = set()
    for t, tr in enumerate(transfers):
        after = tr.get("after")
        if (after is not None) != onward:
            continue
        if after is not None and after not in arrived:
            _transfer(transfers, after, I, O, ssem, rsem, True).wait_recv()
            arrived.add(after)
        _transfer(transfers, t, I, O, ssem, rsem, False).start()


def _finish_transfers(transfers, I, O, ssem, rsem):
    passed_on = {tr["after"] for tr in transfers if tr.get("after") is not None}
    for t in range(len(transfers)):
        if t not in passed_on:
            _transfer(transfers, t, I, O, ssem, rsem, True).wait_recv()
    for t in range(len(transfers)):
        _transfer(transfers, t, I, O, ssem, rsem, False).wait_send()


def _own_copies(own, I, O, stage, lsem, leg):
    for n, (src, dst) in enumerate(own):
        me = _coords()
        bring =pltpu.make_async_copy(src(I, O, me), stage[n], lsem.at[2 * n])
        put = pltpu.make_async_copy(stage[n], dst(I, O, me), lsem.at[2 * n + 1])
        if leg == 0:
            bring.start()
        elif leg == 1:
            bring.wait()
            put.start()
        else:
            put.wait()


def _own_scratch(own, ins):
    return [pltpu.VMEM(ins[n].shape, ins[n].dtype) for n in range(len(own))], pltpu.SemaphoreType.DMA((max(2 * len(own), 1),))


def _exchange(name, ins, outs, transfers, own=()):
    ni, no = len(ins), len(outs)
    nt = len(transfers)
    stages, stage_sems = _own_scratch(own, ins)

    def body(*refs):
        I, O = refs[:ni], refs[ni:ni + no]
        ssem, rsem, lsem = refs[ni + no:ni + no + 3]
        stage = refs[ni + no + 3:]
        _own_copies(own, I, O, stage, lsem, 0)
        _start_transfers(transfers, I, O, ssem, rsem, False)
        _own_copies(own, I, O, stage, lsem, 1)
        _start_transfers(transfers, I, O, ssem, rsem, True)
        _finish_transfers(transfers, I, O, ssem, rsem)
        _own_copies(own, I, O, stage, lsem, 2)

    hbm = pl.BlockSpec(memory_space=pltpu.HBM)
    return pl.pallas_call(
        body, name=name, in_specs=[hbm] * ni, out_specs=[hbm] * no,
        out_shape=[jax.ShapeDtypeStruct(s, d) for s, d in outs],
        scratch_shapes=[pltpu.SemaphoreType.DMA((nt,)), pltpu.SemaphoreType.DMA((nt,)), stage_sems] + stages,
        compiler_params=pltpu.CompilerParams(has_side_effects=True, vmem_limit_bytes=VMEM_LIMIT))(*ins)


CHIP_MASKS = [(0, 1, 0), (1, 0, 0), (1, 1, 0)]
SIBLING = (0, 0, 1)


def _half(shape2d, axis, which):
    n = shape2d[axis] // 2
    cut = pl.ds(pl.multiple_of(which * n, n), n)
    return (cut, slice(None)) if axis == 0 else (slice(None), cut)


class _Riding:
    def __init__(self, transfers, ins, outs, own=()):
        self.transfers, self.ins, self.outs, self.own = transfers, list(ins), list(outs), list(own)
        hbm = pl.BlockSpec(memory_space=pltpu.HBM)
        self.in_specs, self.out_specs = [hbm] * len(self.ins), [hbm] * len(self.outs)
        self.out_shape = [jax.ShapeDtypeStruct(s, d) for s, d in self.outs]
        stages, stage_sems = _own_scratch(self.own, self.ins)
        self.scratch = [pltpu.SemaphoreType.DMA((max(len(transfers), 1),))] * 2 + [stage_sems] + stages

    def alone(self, name):
        return _exchange(name, self.ins, self.outs, self.transfers, self.own)

    def hooks(self, I, O, ssem, rsem, lsem, *stage, first, middle, last):
        tr, own = self.transfers, self.own

        @pl.when(first)
        def _():
            _own_copies(own, I, O, stage, lsem, 0)
            _start_transfers(tr, I, O, ssem, rsem, False)

        if own or any(t.get("after") is not None for t in tr):
            @pl.when(middle)
            def _():
                _own_copies(own, I, O, stage, lsem, 1)
                _start_transfers(tr, I, O, ssem, rsem, True)

        def at_end():
            @pl.when(last)
            def _():
                _finish_transfers(tr, I, O, ssem, rsem)
                _own_copies(own, I, O, stage, lsem, 2)

        return at_end


def _stretch(n, pos):
    return (pl.ds(pos * n if isinstance(pos, int) else pl.multiple_of(pos * n, n), n),)


def _gather_plan(shards, axes):
    def half(a, who):
        if shards[a].ndim == 1:
            return _stretch(shards[a].shape[0] // 2, who[2])
        return _half(shards[a].shape, axes[a], who[2])

    def landed(a, chip, who):
        if shards[a].ndim == 1:
            return _stretch(shards[a].shape[0] // 2, 2 * chip + who[2])
        return (chip,) + half(a, who)

    over_ici, onward = [], []
    for a in range(len(shards)):
        for mask in CHIP_MASKS:
            over_ici.append(dict(
                mask=mask,
                src=lambda I, O, me, a=a: I[a].at[half(a, me)],
                dst=lambda I, O, who, a=a: O[a].at[landed(a, _chip(who), who)]))
            onward.append(dict(
                mask=SIBLING, after=len(over_ici) - 1,
                src=lambda I, O, me, a=a, mask=mask: O[a].at[landed(a, _chip(_flip(me, mask)), me)],
                dst=lambda I, O, who, a=a, mask=mask: O[a].at[landed(a, _chip(_flip(who, mask)), who)]))
    outs = [((NCHIP * s.shape[0],) if s.ndim == 1 else (NCHIP,) + s.shape, s.dtype) for s in shards]

    def whole(a, chip):
        return _stretch(shards[a].shape[0], chip) if shards[a].ndim == 1 else (chip,)

    own = [(lambda I, O, me, a=a: I[a], lambda I, O, me, a=a: O[a].at[whole(a, _chip(me))])
           for a in range(len(shards))]
    return over_ici + onward, outs, own


def _gather_shards(shards, axes):
    transfers, outs, own = _gather_plan(shards, axes)
    return _exchange("gather_weights", shards, outs, transfers, own)


def _halves_plan(blocks, axes):
    def cut(a, which):
        return (slice(None),) + _half(blocks[a].shape[1:], axes[a], which)

    transfers, outs = [], []
    for a, (b, ax) in enumerate(zip(blocks, axes)):
        if b.ndim == 1:
            h = b.shape[0] // NCHIP // 2
            for k in range(NCHIP):
                transfers.append(dict(mask=SIBLING,
                                      src=lambda I, O, me, a=a, k=k, h=h: I[a].at[_stretch(h, 2 * k + 1 - me[2])],
                                      dst=lambda I, O, who, a=a, k=k, h=h: O[a].at[_stretch(h, k)]))
            outs.append(((NCHIP * h,), b.dtype))
        else:
            transfers.append(dict(mask=SIBLING, src=lambda I, O, me, a=a: I[a].at[cut(a, 1 - me[2])],
                                  dst=lambda I, O, who, a=a: O[a]))
            shape = list(b.shape)
            shape[ax + 1] //= 2
            outs.append((tuple(shape), b.dtype))
    return transfers, outs


def _scatter_plan(tb):
    def slot(a, k):
        return (k,) if tb[a].ndim == 3 else _stretch(tb[a].shape[0] // NCHIP, k)

    transfers = []
    for a in range(len(tb)):
        for n, mask in enumerate(CHIP_MASKS):
            transfers.append(dict(
                mask=mask,
                src=lambda I, O, me, a=a, mask=mask: I[a].at[slot(a, _chip(_flip(me, mask)))],
                dst=lambda I, O, who, a=a, n=n: O[a].at[slot(a, n)]))
    outs = [((3,) + t.shape[1:] if t.ndim == 3 else (3 * (t.shape[0] // NCHIP),), t.dtype) for t in tb]
    return transfers, outs


def _last_exchange(vec, halves):
    def slot(who):
        return 4 * who[0] + 2 * who[1] + who[2]

    masks = [(m >> 2 & 1, m >> 1 & 1, m & 1) for m in range(1, 8)]
    transfers = [dict(mask=mask, src=lambda I, O, me: I[0], dst=lambda I, O, who: O[0].at[slot(who)])
                 for mask in masks]
    transfers += [dict(mask=SIBLING, src=lambda I, O, me, a=a: I[a], dst=lambda I, O, who, a=a: O[a])
                  for a in range(1, 1 + len(halves))]
    own = [(lambda I, O, me: I[0], lambda I, O, me: O[0].at[slot(me)])]
    outs = [((8,) + vec.shape, vec.dtype)] + [(t.shape, t.dtype) for t in halves]
    res = _exchange("last_exchange", [vec] + list(halves), outs, transfers, own)
    return res[0], res[1:]


def _rope_tables(positions):
    half = ROT // 2
    inv_freq = jnp.power(jnp.float32(THETA), -jnp.arange(0, ROT, 2, dtype=F32) / ROT)
    ang = positions.astype(F32)[:, None] * inv_freq[None, :]
    cos, sin = jnp.cos(ang), jnp.sin(ang)
    one, zero, z8 = jnp.ones((S, HD - ROT), F32), jnp.zeros((S, HD - ROT), F32), jnp.zeros((S, half), F32)
    c = jnp.concatenate([cos, cos, one], axis=1)
    a = jnp.concatenate([-sin, z8, zero], axis=1)
    b = jnp.concatenate([z8, sin, zero], axis=1)
    return tuple(jnp.tile(t, (1, 2)) for t in (c, a, b))


def _tile_heads(g, w):
    return jnp.tile(g.reshape(1, HD), (1, w // HD))


def _fold_heads(dg):
    return dg.reshape(-1, HD).sum(axis=0)


def _pad_lanes(a):
    return jnp.pad(a, ((0, 0), (0, LANES - a.shape[1])))


def _local_step(x, target, positions, wt, fetch, late_weights, begin_reduce):
    rope = _rope_tables(positions)
    w1t = wt["w_in_a_t"]
    f_row = 3 * D // LANES
    wg_t = w1t[3 * D + NH:]
    b_pad = _pad_lanes(wt["b_forget"].reshape(1, NH))
    qg_a, kg_a = _tile_heads(wt["qnorm_a_g"], D), _tile_heads(wt["knorm_a_g"], D)
    qg_b, kg_b = _tile_heads(wt["qnorm_b_g"], D), _tile_heads(wt["knorm_b_g"], KVW)
    norm_a, kv_g, norm_b = wt["norm_a_g"].reshape(1, D), wt["kv_norm_g"].reshape(1, D), wt["norm_b_g"].reshape(1, D)
    sinks_t = jnp.repeat(wt["sinks"].reshape(1, NH), HD, axis=1)

    (u_a,) = _rmsnorm_fwd(x, [norm_a], "norm_a")
    qkv, qkv_a = _proj_a(u_a, w1t, qg_a, kg_a)
    fpad = _mm("proj_f", S, LANES, [(u_a, _a_rows(D), w1t, _b_rows(D, row0=f_row, tn=LANES), NT)], tn=LANES)
    gate_a = _mm("proj_gate_a", S, D, [(u_a, _a_rows(D), wg_t, _b_rows(D, tn=TN_WIDE), NT)], tn=TN_WIDE)
    ct = _forget_cumsum(fpad, b_pad)
    ct2 = ct[:NH].reshape(NH // 2, 2, S)
    o_a, lse_a, y_a, fetched = _fox_fwd(qkv_a, ct2, gate_a, fetch)
    wt = {**wt, **late_weights(fetched)}
    w_in_b = wt["w_in_b"]
    h1, u_kv, u_b = _out_norms(y_a, wt["w_out_a"], x, [kv_g, norm_b])
    pb, kv, q_b, kdup, vdup = _proj_b(u_b, u_kv, w_in_b, wt["w_kv"], qg_b, kg_b, rope)
    gate_b_col = D // LANES
    o_b, lse_b, y_b = _swa_fwd(q_b, kdup, vdup, sinks_t, pb, gate_b_col)
    d_out, d_out_b, sq = _out_loss(y_b, wt["w_out_b"], h1, target)

    g = {}
    g["w_out_b"] = _mm("dw_out_b", D, D, [(y_b, _a_cols(S), d_out_b, _b_cols(S, tn=TN_WIDE), TN)], tn=TN_WIDE)
    d_y_b = _mm("dy_b", S, D, [(d_out_b, _a_rows(D), wt["w_out_b"], _b_rows(D, tn=TN_WIDE), NT)], tn=TN_WIDE)
    d_qb_raw, dkdup, dvdup, dsk, d_gate_b, dg = _swa_bwd(q_b, kdup, vdup, sinks_t, o_b, lse_b, d_y_b, pb, gate_b_col,
                                                         qg_b, rope)
    g["sinks"] = dsk[0, ::HD]
    g["qnorm_b_g"] = _fold_heads(dg)
    g["w_in_b"] = jnp.concatenate([
        _mm("dw_in_b_q", D, D, [(u_b, _a_cols(S), d_qb_raw, _b_cols(S), TN)], stacked=True),
        _mm("dw_in_b_gate", D, D, [(u_b, _a_cols(S), d_gate_b, _b_cols(S), TN)], stacked=True)], axis=0)
    d_kv, dg = _kv_bwd(dkdup, dvdup, kv, kg_b, rope)
    g["knorm_b_g"] = _fold_heads(dg)
    g["w_kv"] = _mm("dw_kv", D, 2 * KVW, [(u_kv, _a_cols(S), d_kv, _b_cols(S), TN)])
    d_h1, d_h1_b, g["kv_norm_g"], g["norm_b_g"] = _du_b_norms(d_qb_raw, d_gate_b, d_kv, w_in_b, wt["w_kv"], h1, kv_g,
                                                              norm_b, d_out)
    g["w_out_a"] = _mm("dw_out_a", D, D, [(y_a, _a_cols(S), d_h1_b, _b_cols(S, tn=TN_WIDE), TN)], tn=TN_WIDE)
    late = {n: g[n] for n in LATE}
    d_y_a, halves = _mm("dy_a", S, D, [(d_h1_b, _a_rows(D), wt["w_out_a"], _b_rows(D, tn=TN_WIDE), NT)],
                        tn=TN_WIDE, riding=begin_reduce(late))
    riding, so_far = begin_reduce(late, halves)
    d_q_raw, dk_a, dv_a, dct, d_gate_a, dg, arrived = _fox_bwd(qkv_a, ct2, o_a, lse_a, d_y_a, gate_a, qkv, qg_a, riding)
    g["qnorm_a_g"] = _fold_heads(dg)
    dct_pad = jnp.pad(dct.reshape(NH, S), ((0, LANES - NH), (0, 0)))
    d_f, db = _forget_bwd(dct_pad, fpad, b_pad)
    g["b_forget"] = db[0, :NH]
    d_k_raw, dg = _headnorm_bwd(qkv, 1, kg_a, dk_a, None, "knorm_a_bwd")
    g["knorm_a_g"] = _fold_heads(dg)
    rows, gw = 4 * D + NH, None
    for n, t, row0 in (("q", d_q_raw, 0), ("k", d_k_raw, D), ("v", dv_a, 2 * D)):
        gw = _mm("dw_in_a_" + n, D, D, [(t, _a_cols(S), u_a, _b_cols(S, tn=TN_WIDE), TN)], tn=TN_WIDE,
                 rows_of=(gw, rows, row0))
    gw = _mm("dw_in_a_f", LANES, D, [(d_f, _a_cols(S, tm=LANES), u_a, _b_cols(S, tn=TN_WIDE), TN)], tm=LANES,
             tn=TN_WIDE, rows_of=(gw, rows, 3 * D))
    g["w_in_a"] = _mm("dw_in_a_gate", D, D, [(d_gate_a, _a_cols(S), u_a, _b_cols(S, tn=TN_WIDE), TN)], tn=TN_WIDE,
                      rows_of=(gw, rows, 3 * D + NH))
    first = {"w_in_a": g["w_in_a"]}
    riding, so_far_first = begin_reduce(first, begin_reduce(first).alone("sibling_halves_w_in_a"))
    d_u_a, arrived_first = _mm("du_a", S, D, [
        (d_q_raw, _a_rows(D), w1t, _b_cols(D, row=0, tn=TN_WIDE), None),
        (d_k_raw, _a_rows(D), w1t, _b_cols(D, row=1, tn=TN_WIDE), None),
        (dv_a, _a_rows(D), w1t, _b_cols(D, row=2, tn=TN_WIDE), None),
        (d_gate_a, _a_rows(D), wg_t, _b_cols(D, tn=TN_WIDE), None),
        (d_f, _a_rows(LANES), w1t, _b_cols(LANES, row=f_row, tn=TN_WIDE), None)], tn=TN_WIDE, riding=riding)
    d_x, _, g["norm_a_g"] = _rmsnorm_bwd(x, [norm_a], [d_u_a], d_h1, "norm_a_bwd")
    return sq, d_x, g, (list(so_far_first) + list(so_far), list(arrived_first) + list(arrived))


BIG = ["w_in_a", "w_out_a", "w_kv", "w_in_b", "w_out_b"]
LATE = BIG[1:]
SPLIT = {"w_in_a": None, "w_out_a": 0, "w_kv": 0, "w_in_b": 0, "w_out_b": 0}
SMALL = ["norm_a_g", "b_forget", "qnorm_a_g", "knorm_a_g", "kv_norm_g", "knorm_b_g", "norm_b_g", "qnorm_b_g", "sinks"]
NAMES = ["norm_a_g", "w_in_a", "b_forget", "qnorm_a_g", "knorm_a_g", "w_out_a", "kv_norm_g", "w_kv", "knorm_b_g",
         "norm_b_g", "w_in_b", "qnorm_b_g", "sinks", "w_out_b"]


def _pack(vals):
    flat = []
    for v in vals:
        v = v.reshape(-1)
        flat.append(jnp.pad(v, (0, -v.shape[0] % LANES)))
    flat = jnp.concatenate(flat)
    flat = jnp.pad(flat, (0, -flat.shape[0] % (8 * LANES)))
    return flat.reshape(-1, LANES)


def _unpack(packed, shapes):
    flat, out, off = packed.reshape(-1), [], 0
    for s in shapes:
        n = int(np.prod(s))
        out.append(flat[off:off + n].reshape(s))
        off += n + (-n % LANES)
    return out


def kernel(x, positions, norm_a_g, w_in_a, b_forget, qnorm_a_g, knorm_a_g, w_out_a, kv_norm_g, w_kv, knorm_b_g, norm_b_g, w_in_b, qnorm_b_g, sinks, w_out_b, loss_target, m_norm_a_g, m_w_in_a, m_b_forget, m_qnorm_a_g, m_knorm_a_g, m_w_out_a, m_kv_norm_g, m_w_kv, m_knorm_b_g, m_norm_b_g, m_w_in_b, m_qnorm_b_g, m_sinks, m_w_out_b, v_norm_a_g, v_w_in_a, v_b_forget, v_qnorm_a_g, v_knorm_a_g, v_w_out_a, v_kv_norm_g, v_w_kv, v_knorm_b_g, v_norm_b_g, v_w_in_b, v_qnorm_b_g, v_sinks, v_w_out_b):
    w = dict(norm_a_g=norm_a_g, w_in_a=w_in_a, b_forget=b_forget, qnorm_a_g=qnorm_a_g, knorm_a_g=knorm_a_g,
             w_out_a=w_out_a, kv_norm_g=kv_norm_g, w_kv=w_kv, knorm_b_g=knorm_b_g, norm_b_g=norm_b_g,
             w_in_b=w_in_b, qnorm_b_g=qnorm_b_g, sinks=sinks, w_out_b=w_out_b)
    m = dict(norm_a_g=m_norm_a_g, w_in_a=m_w_in_a, b_forget=m_b_forget, qnorm_a_g=m_qnorm_a_g, knorm_a_g=m_knorm_a_g,
             w_out_a=m_w_out_a, kv_norm_g=m_kv_norm_g, w_kv=m_w_kv, knorm_b_g=m_knorm_b_g, norm_b_g=m_norm_b_g,
             w_in_b=m_w_in_b, qnorm_b_g=m_qnorm_b_g, sinks=m_sinks, w_out_b=m_w_out_b)
    v = dict(norm_a_g=v_norm_a_g, w_in_a=v_w_in_a, b_forget=v_b_forget, qnorm_a_g=v_qnorm_a_g, knorm_a_g=v_knorm_a_g,
             w_out_a=v_w_out_a, kv_norm_g=v_kv_norm_g, w_kv=v_w_kv, knorm_b_g=v_knorm_b_g, norm_b_g=v_norm_b_g,
             w_in_b=v_w_in_b, qnorm_b_g=v_qnorm_b_g, sinks=v_sinks, w_out_b=v_w_out_b)
    my_chip = 2 * lax.axis_index("x") + lax.axis_index("y")

    def shard2d(t, n):
        if n == "w_in_a":
            return jnp.transpose(t, (2, 0, 1)).reshape(-1)
        return t.reshape(t.shape[-2:])

    def unflat(t, n):
        return jnp.transpose(t.reshape(-1, 1, D), (1, 2, 0)) if n == "w_in_a" else t.reshape(w[n].shape)

    w2d = {n: shard2d(w[n], n) for n in BIG}

    norm_a_rows = jnp.broadcast_to(norm_a_g.reshape(1, D // NCHIP), (2 * SUBLANES, D // NCHIP))
    w1t, norm_rows = _gather_shards([w2d["w_in_a"].astype(BF16), norm_a_rows], [SPLIT["w_in_a"], 0])
    wt = {"w_in_a_t": w1t.reshape(-1, D), "norm_a_g": norm_rows[:, 0, :].reshape(1, D)}
    for n in SMALL[1:]:
        wt[n] = w[n]
    late_shards = [w2d[n].astype(BF16) for n in LATE]
    late_axes = [SPLIT[n] for n in LATE]
    transfers, outs, own = _gather_plan(late_shards, late_axes)
    fetch = _Riding(transfers, late_shards, outs, own)

    def late_weights(fetched):
        return {n: t if n == "w_in_b" else t.reshape(-1, t.shape[2]) for n, t in zip(LATE, fetched)}

    def as_blocks(t):
        if t.ndim == 3:
            return t
        return t.reshape(-1) if t.shape[0] % (SUBLANES * NCHIP) else t.reshape(NCHIP, -1, t.shape[1])

    def begin_reduce(grads, halves=None):
        names = list(grads)
        axes = [SPLIT[n] for n in names]
        blocks = [as_blocks(grads[n]) for n in names]
        if halves is None:
            transfers, outs = _halves_plan(blocks, axes)
            return _Riding(transfers, blocks, outs)
        sums = [_chip_sum(blk, part, ax, "chip_sum_" + n) for n, ax, blk, part in zip(names, axes, blocks, halves)]
        bf16 = [s[1] for s in sums]
        transfers, outs = _scatter_plan(bf16)
        return _Riding(transfers, bf16, outs), [s[0] for s in sums]

    sq, d_x, g, (chip_f32, arrived) = _local_step(x[0], loss_target[0], positions, wt, fetch, late_weights,
                                                  begin_reduce)

    axes = [SPLIT[n] for n in BIG]
    halves = [_mesh_sum(t32, parts, ax, "mesh_sum_" + n) for n, ax, t32, parts in zip(BIG, axes, chip_f32, arrived)]

    small_shapes = [(D,), (NH,), (HD,), (HD,), (D,), (HD,), (D,), (HD,), (NH,), (D,)]
    gathered_small, sibling_done = _last_exchange(_pack([g[n] for n in SMALL] + [sq]), halves)
    total = _sum_stack(gathered_small, "sum_small")
    small_g = dict(zip(SMALL, _unpack(total, small_shapes)[:-1]))
    loss = 0.5 * jnp.sum(_unpack(total, small_shapes)[-1]) / D
    small_g["norm_a_g"] = lax.dynamic_slice(small_g["norm_a_g"], (my_chip * (D // NCHIP),), (D // NCHIP,))

    res = {}
    for n, ax, mine_half, their_half in zip(BIG, axes, halves, sibling_done):
        out4 = _adamw_halves(w2d[n], mine_half, their_half, shard2d(m[n], n), shard2d(v[n], n), ax, "adamw_" + n)
        res[n] = tuple(unflat(t, n) for t in out4)
    row = lambda t: t.reshape(1, -1)
    small_out = _adamw_small(*[[row(d[n]) for n in SMALL] for d in (w, small_g, m, v)])
    for i, n in enumerate(SMALL):
        res[n] = tuple(t.reshape(w[n].shape) for t in (small_g[n],) + tuple(out[i] for out in small_out))

    outs = [loss, d_x[None]]
    for k in range(4):
        outs += [res[n][k] for n in NAMES]
    return tuple(outs)
```
